```python
import math
import jax, jax.numpy as jnp
from jax import lax
import numpy as np

D_MODEL = 1024
BATCH = 8
SEQ = 8192
DEPTH = 1

MIX_WIDTH = D_MODEL
GLA_WIDTH = MIX_WIDTH // 2
GMLP_WIDTH = MIX_WIDTH - GLA_WIDTH
GLA_HEADS = 4
GLA_DV = GLA_WIDTH // GLA_HEADS
GLA_DK = GLA_DV // 2
GLA_KEY_WIDTH = GLA_HEADS * GLA_DK
GLA_LOWRANK = 16
GLA_TAU = 16.0
GLA_CHUNK = 64
GMLP_GROUPS = 4
GMLP_GROUP_DIM = GMLP_WIDTH // GMLP_GROUPS
GMLP_CHUNK = 128
D_FF = int(math.ceil(8 * D_MODEL / 3 / 256) * 256)
EPS = 1e-6

PROJ_SIZES = [GLA_KEY_WIDTH, GLA_KEY_WIDTH, GLA_WIDTH, GLA_WIDTH,
              GLA_LOWRANK, GLA_LOWRANK, 2 * GMLP_WIDTH]
PROJ_WIDTH = sum(PROJ_SIZES)
PROJ_SPLITS = [int(v) for v in np.cumsum(PROJ_SIZES)[:-1]]

kernel_name = "hybrid_gla_gmlp_encoder_block"


def rmsnorm(x, g):
    xf = x.astype(jnp.float32)
    y = xf * lax.rsqrt(jnp.mean(xf * xf, axis=-1, keepdims=True) + EPS)
    return (y * g.astype(jnp.float32)).astype(x.dtype)


def layernorm(x, g, b):
    xf = x.astype(jnp.float32)
    mu = jnp.mean(xf, axis=-1, keepdims=True)
    xc = xf - mu
    y = xc * lax.rsqrt(jnp.mean(xc * xc, axis=-1, keepdims=True) + EPS)
    return (y * g.astype(jnp.float32) + b.astype(jnp.float32)).astype(x.dtype)


def gla_one_direction(q, k, v, log_a):
    B, S, H, DK = q.shape
    DV = v.shape[-1]
    C = GLA_CHUNK
    N = S // C
    f32 = jnp.float32
    q = q.astype(f32).reshape(B, N, C, H, DK)
    k = k.astype(f32).reshape(B, N, C, H, DK)
    v = v.astype(f32).reshape(B, N, C, H, DV)
    b = jnp.cumsum(log_a.astype(f32).reshape(B, N, C, H, DK), axis=2)
    b_last = b[:, :, -1]
    q_dec = q * jnp.exp(b)
    k_dec = k * jnp.exp(-b)
    k_to_end = k * jnp.exp(b_last[:, :, None] - b)
    scores = jnp.einsum('bnthd,bnshd->bnhts', q_dec, k_dec)
    tril = jnp.tril(jnp.ones((C, C), dtype=bool))
    scores = jnp.where(tril, scores, 0.0)
    o_intra = jnp.einsum('bnhts,bnshv->bnthv', scores, v)
    d_state = jnp.einsum('bnshd,bnshv->bnhdv', k_to_end, v)
    chunk_decay = jnp.exp(b_last)

    def step(state, inp):
        ds, dec = inp
        return dec[..., None] * state + ds, state

    state0 = jnp.zeros((B, H, DK, DV), f32)
    _, states_before = lax.scan(step, state0,
                                (jnp.moveaxis(d_state, 1, 0), jnp.moveaxis(chunk_decay, 1, 0)))
    states_before = jnp.moveaxis(states_before, 0, 1)
    o_inter = jnp.einsum('bnthd,bnhdv->bnthv', q_dec, states_before)
    return (o_intra + o_inter).reshape(B, S, H, DV)


def gla_mixer(h_q, h_k, h_v, h_g, lr_f, lr_b, w_decay_f, b_decay_f, w_decay_b, b_decay_b, gla_norm_g):
    B, S, _ = h_q.shape
    f32 = jnp.float32
    q = h_q.reshape(B, S, GLA_HEADS, GLA_DK) * (GLA_DK ** -0.5)
    k = h_k.reshape(B, S, GLA_HEADS, GLA_DK)
    v = h_v.reshape(B, S, GLA_HEADS, GLA_DV)
    la_f = (jax.nn.log_sigmoid((lr_f @ w_decay_f + b_decay_f).astype(f32)) / GLA_TAU
            ).reshape(B, S, GLA_HEADS, GLA_DK)
    la_b = (jax.nn.log_sigmoid((lr_b @ w_decay_b + b_decay_b).astype(f32)) / GLA_TAU
            ).reshape(B, S, GLA_HEADS, GLA_DK)
    o_fwd = gla_one_direction(q, k, v, la_f)
    o_bwd = jnp.flip(gla_one_direction(jnp.flip(q, 1), jnp.flip(k, 1), jnp.flip(v, 1),
                                       jnp.flip(la_b, 1)), 1)
    o = o_fwd + o_bwd
    o = o * lax.rsqrt(jnp.mean(o * o, axis=-1, keepdims=True) + EPS)
    o = o.reshape(B, S, GLA_WIDTH) * gla_norm_g.astype(f32)
    return (o * jax.nn.silu(h_g.astype(f32))).astype(h_q.dtype)


def gmlp_mixer(h_uv, ln_g, ln_b, w_spatial, b_spatial):
    B, S, _ = h_uv.shape
    z = jax.nn.gelu(h_uv, approximate=False)
    u, v = jnp.split(z, 2, axis=-1)
    v = layernorm(v, ln_g, ln_b)
    v = v.reshape(B, S // GMLP_CHUNK, GMLP_CHUNK, GMLP_GROUPS, GMLP_GROUP_DIM)
    s = jnp.einsum('gij,bnjgc->bnigc', w_spatial, v) + b_spatial.T[None, None, :, :, None]
    return u * s.reshape(B, S, GMLP_WIDTH)


def _fwd_setup_inputs(seed: int = 0) -> dict:
    key = jax.random.key(seed)
    ks = jax.random.split(key, 20)
    L = DEPTH
    nrm = lambda k, shape, fan_in: jax.random.normal(k, shape, jnp.float32) * (fan_in ** -0.5)
    gain = lambda k, shape: 1.0 + 0.02 * jax.random.normal(k, shape, jnp.float32)
    small = lambda k, shape: 0.01 * jax.random.normal(k, shape, jnp.float32)
    return {
        "x": jax.random.normal(ks[0], (BATCH, SEQ, D_MODEL), jnp.float32),
        "norm1_g": gain(ks[1], (L, D_MODEL)),
        "w_in": nrm(ks[2], (L, D_MODEL, PROJ_WIDTH), D_MODEL),
        "w_decay_f": nrm(ks[3], (L, GLA_LOWRANK, GLA_KEY_WIDTH), GLA_LOWRANK),
        "b_decay_f": small(ks[4], (L, GLA_KEY_WIDTH)),
        "w_decay_b": nrm(ks[5], (L, GLA_LOWRANK, GLA_KEY_WIDTH), GLA_LOWRANK),
        "b_decay_b": small(ks[6], (L, GLA_KEY_WIDTH)),
        "gla_norm_g": gain(ks[7], (L, GLA_WIDTH)),
        "gmlp_ln_g": gain(ks[8], (L, GMLP_WIDTH)),
        "gmlp_ln_b": small(ks[9], (L, GMLP_WIDTH)),
        "w_spatial": nrm(ks[10], (L, GMLP_GROUPS, GMLP_CHUNK, GMLP_CHUNK), GMLP_CHUNK),
        "b_spatial": gain(ks[11], (L, GMLP_GROUPS, GMLP_CHUNK)),
        "w_out": nrm(ks[12], (L, MIX_WIDTH, D_MODEL), MIX_WIDTH),
        "norm2_g": gain(ks[13], (L, D_MODEL)),
        "w_gate": nrm(ks[14], (L, D_MODEL, D_FF), D_MODEL),
        "w_up": nrm(ks[15], (L, D_MODEL, D_FF), D_MODEL),
        "w_down": nrm(ks[16], (L, D_FF, D_MODEL), D_FF),
        "final_norm_g": gain(ks[17], (D_MODEL,)),
    }


def _fwd_reference(x, norm1_g, w_in, w_decay_f, b_decay_f, w_decay_b, b_decay_b, gla_norm_g,
              gmlp_ln_g, gmlp_ln_b, w_spatial, b_spatial, w_out, norm2_g, w_gate, w_up,
              w_down, final_norm_g):
    for l in range(DEPTH):
        h = rmsnorm(x, norm1_g[l])
        p = h @ w_in[l]
        h_q, h_k, h_v, h_g, lr_f, lr_b, h_uv = jnp.split(p, PROJ_SPLITS, axis=-1)
        y_a = gla_mixer(h_q, h_k, h_v, h_g, lr_f, lr_b, w_decay_f[l], b_decay_f[l],
                        w_decay_b[l], b_decay_b[l], gla_norm_g[l])
        y_b = gmlp_mixer(h_uv, gmlp_ln_g[l], gmlp_ln_b[l], w_spatial[l], b_spatial[l])
        x = x + jnp.concatenate([y_a, y_b.astype(y_a.dtype)], axis=-1) @ w_out[l]
        h2 = rmsnorm(x, norm2_g[l])
        x = x + (jax.nn.silu(h2 @ w_gate[l]) * (h2 @ w_up[l])) @ w_down[l]
    return rmsnorm(x, final_norm_g)


import jax as _jax
import jax.numpy as _jnp

TWIN_FORMAT = 'train_step'
FWD_PARAMS = ['x', 'norm1_g', 'w_in', 'w_decay_f', 'b_decay_f', 'w_decay_b', 'b_decay_b', 'gla_norm_g', 'gmlp_ln_g', 'gmlp_ln_b', 'w_spatial', 'b_spatial', 'w_out', 'norm2_g', 'w_gate', 'w_up', 'w_down', 'final_norm_g']
TWIN_WEIGHTS = ['norm1_g', 'w_in', 'w_decay_f', 'b_decay_f', 'w_decay_b', 'b_decay_b', 'gla_norm_g', 'gmlp_ln_g', 'gmlp_ln_b', 'w_spatial', 'b_spatial', 'w_out', 'norm2_g', 'w_gate', 'w_up', 'w_down', 'final_norm_g']
TWIN_DIFF_INPUT = 'x'
TWIN_INPUTS = ['x', 'norm1_g', 'w_in', 'w_decay_f', 'b_decay_f', 'w_decay_b', 'b_decay_b', 'gla_norm_g', 'gmlp_ln_g', 'gmlp_ln_b', 'w_spatial', 'b_spatial', 'w_out', 'norm2_g', 'w_gate', 'w_up', 'w_down', 'final_norm_g', 'loss_target', 'm_norm1_g', 'm_w_in', 'm_w_decay_f', 'm_b_decay_f', 'm_w_decay_b', 'm_b_decay_b', 'm_gla_norm_g', 'm_gmlp_ln_g', 'm_gmlp_ln_b', 'm_w_spatial', 'm_b_spatial', 'm_w_out', 'm_norm2_g', 'm_w_gate', 'm_w_up', 'm_w_down', 'm_final_norm_g', 'v_norm1_g', 'v_w_in', 'v_w_decay_f', 'v_b_decay_f', 'v_w_decay_b', 'v_b_decay_b', 'v_gla_norm_g', 'v_gmlp_ln_g', 'v_gmlp_ln_b', 'v_w_spatial', 'v_b_spatial', 'v_w_out', 'v_norm2_g', 'v_w_gate', 'v_w_up', 'v_w_down', 'v_final_norm_g']
TWIN_OUTPUTS = ['loss', 'grad_x', 'grad_norm1_g', 'grad_w_in', 'grad_w_decay_f', 'grad_b_decay_f', 'grad_w_decay_b', 'grad_b_decay_b', 'grad_gla_norm_g', 'grad_gmlp_ln_g', 'grad_gmlp_ln_b', 'grad_w_spatial', 'grad_b_spatial', 'grad_w_out', 'grad_norm2_g', 'grad_w_gate', 'grad_w_up', 'grad_w_down', 'grad_final_norm_g', 'delta_norm1_g', 'delta_w_in', 'delta_w_decay_f', 'delta_b_decay_f', 'delta_w_decay_b', 'delta_b_decay_b', 'delta_gla_norm_g', 'delta_gmlp_ln_g', 'delta_gmlp_ln_b', 'delta_w_spatial', 'delta_b_spatial', 'delta_w_out', 'delta_norm2_g', 'delta_w_gate', 'delta_w_up', 'delta_w_down', 'delta_final_norm_g', 'new_m_norm1_g', 'new_m_w_in', 'new_m_w_decay_f', 'new_m_b_decay_f', 'new_m_w_decay_b', 'new_m_b_decay_b', 'new_m_gla_norm_g', 'new_m_gmlp_ln_g', 'new_m_gmlp_ln_b', 'new_m_w_spatial', 'new_m_b_spatial', 'new_m_w_out', 'new_m_norm2_g', 'new_m_w_gate', 'new_m_w_up', 'new_m_w_down', 'new_m_final_norm_g', 'new_v_norm1_g', 'new_v_w_in', 'new_v_w_decay_f', 'new_v_b_decay_f', 'new_v_w_decay_b', 'new_v_b_decay_b', 'new_v_gla_norm_g', 'new_v_gmlp_ln_g', 'new_v_gmlp_ln_b', 'new_v_w_spatial', 'new_v_b_spatial', 'new_v_w_out', 'new_v_norm2_g', 'new_v_w_gate', 'new_v_w_up', 'new_v_w_down', 'new_v_final_norm_g']
TWIN_LEAF_KINDS = {'loss': 'loss', 'grad_x': 'grad_x', 'grad_norm1_g': 'grad_w', 'grad_w_in': 'grad_w', 'grad_w_decay_f': 'grad_w', 'grad_b_decay_f': 'grad_w', 'grad_w_decay_b': 'grad_w', 'grad_b_decay_b': 'grad_w', 'grad_gla_norm_g': 'grad_w', 'grad_gmlp_ln_g': 'grad_w', 'grad_gmlp_ln_b': 'grad_w', 'grad_w_spatial': 'grad_w', 'grad_b_spatial': 'grad_w', 'grad_w_out': 'grad_w', 'grad_norm2_g': 'grad_w', 'grad_w_gate': 'grad_w', 'grad_w_up': 'grad_w', 'grad_w_down': 'grad_w', 'grad_final_norm_g': 'grad_w', 'delta_norm1_g': 'delta_w', 'delta_w_in': 'delta_w', 'delta_w_decay_f': 'delta_w', 'delta_b_decay_f': 'delta_w', 'delta_w_decay_b': 'delta_w', 'delta_b_decay_b': 'delta_w', 'delta_gla_norm_g': 'delta_w', 'delta_gmlp_ln_g': 'delta_w', 'delta_gmlp_ln_b': 'delta_w', 'delta_w_spatial': 'delta_w', 'delta_b_spatial': 'delta_w', 'delta_w_out': 'delta_w', 'delta_norm2_g': 'delta_w', 'delta_w_gate': 'delta_w', 'delta_w_up': 'delta_w', 'delta_w_down': 'delta_w', 'delta_final_norm_g': 'delta_w', 'new_m_norm1_g': 'new_m', 'new_m_w_in': 'new_m', 'new_m_w_decay_f': 'new_m', 'new_m_b_decay_f': 'new_m', 'new_m_w_decay_b': 'new_m', 'new_m_b_decay_b': 'new_m', 'new_m_gla_norm_g': 'new_m', 'new_m_gmlp_ln_g': 'new_m', 'new_m_gmlp_ln_b': 'new_m', 'new_m_w_spatial': 'new_m', 'new_m_b_spatial': 'new_m', 'new_m_w_out': 'new_m', 'new_m_norm2_g': 'new_m', 'new_m_w_gate': 'new_m', 'new_m_w_up': 'new_m', 'new_m_w_down': 'new_m', 'new_m_final_norm_g': 'new_m', 'new_v_norm1_g': 'new_v', 'new_v_w_in': 'new_v', 'new_v_w_decay_f': 'new_v', 'new_v_b_decay_f': 'new_v', 'new_v_w_decay_b': 'new_v', 'new_v_b_decay_b': 'new_v', 'new_v_gla_norm_g': 'new_v', 'new_v_gmlp_ln_g': 'new_v', 'new_v_gmlp_ln_b': 'new_v', 'new_v_w_spatial': 'new_v', 'new_v_b_spatial': 'new_v', 'new_v_w_out': 'new_v', 'new_v_norm2_g': 'new_v', 'new_v_w_gate': 'new_v', 'new_v_w_up': 'new_v', 'new_v_w_down': 'new_v', 'new_v_final_norm_g': 'new_v'}


def _forward(args):
    return _fwd_reference(*[args[k] for k in FWD_PARAMS])


def _output_shape():
    def fwd():
        inp = _fwd_setup_inputs(0)
        return _fwd_reference(*[inp[k] for k in FWD_PARAMS])
    out = _jax.eval_shape(fwd)
    return out.shape, out.dtype

N_MICROBATCH = 1
ADAM_LR = 0.001
ADAM_B1 = 0.9
ADAM_B2 = 0.999
ADAM_EPS = 1e-08
ADAM_WD = 0.01
ADAM_STEP = 10
PER_EXAMPLE_BATCH_AXIS = {'x': 0, 'loss_target': 0}
SHARED_INPUTS = []
_WEIGHT_DTYPES = {'norm1_g': _jnp.float32, 'w_in': _jnp.float32, 'w_decay_f': _jnp.float32, 'b_decay_f': _jnp.float32, 'w_decay_b': _jnp.float32, 'b_decay_b': _jnp.float32, 'gla_norm_g': _jnp.float32, 'gmlp_ln_g': _jnp.float32, 'gmlp_ln_b': _jnp.float32, 'w_spatial': _jnp.float32, 'b_spatial': _jnp.float32, 'w_out': _jnp.float32, 'norm2_g': _jnp.float32, 'w_gate': _jnp.float32, 'w_up': _jnp.float32, 'w_down': _jnp.float32, 'final_norm_g': _jnp.float32}
MOMENT_SCALE = {'norm1_g': 2.654003e-01, 'w_in': 1.653209e-01, 'w_decay_f': 1.458459e-02, 'b_decay_f': 6.578748e-02, 'w_decay_b': 1.437488e-02, 'b_decay_b': 5.833012e-02, 'gla_norm_g': 1.293359e-01, 'gmlp_ln_g': 1.418123e-01, 'gmlp_ln_b': 1.471043e-01, 'w_spatial': 1.408868e-01, 'b_spatial': 1.456169e-01, 'w_out': 1.702538e-01, 'norm2_g': 1.563555e-01, 'w_gate': 6.660979e-02, 'w_up': 6.441512e-02, 'w_down': 1.072114e-01, 'final_norm_g': 6.411304e+01}


def _to_microbatches(a, axis):
    t = _jnp.moveaxis(a, axis, 0)
    t = t.reshape((N_MICROBATCH, t.shape[0] // N_MICROBATCH) + t.shape[1:])
    return _jnp.moveaxis(t, 1, axis + 1)


def setup_inputs(seed: int = 0) -> dict:
    inp = _fwd_setup_inputs(seed)
    key = _jax.random.fold_in(_jax.random.key(seed), 7919)
    shape, _ = _output_shape()
    out = dict(inp)
    out["loss_target"] = _jax.random.normal(_jax.random.fold_in(key, 0), shape, _jnp.float32)
    for i, name in enumerate(TWIN_WEIGHTS):
        w = inp[name].astype(_jnp.float32)
        if MOMENT_SCALE is None:
            s = _jnp.sqrt(_jnp.mean(_jnp.square(w)) + 1e-30)
        else:
            s = MOMENT_SCALE[name]
        km, kv = _jax.random.split(_jax.random.fold_in(key, i + 1))
        out[name] = w
        out["m_" + name] = s * _jax.random.normal(km, w.shape, _jnp.float32)
        out["v_" + name] = (s * s) * _jax.random.uniform(kv, w.shape, _jnp.float32, 0.5, 1.5)
    if N_MICROBATCH > 1:
        for name, axis in PER_EXAMPLE_BATCH_AXIS.items():
            out[name] = _to_microbatches(out[name], axis)
    return {'x': out['x'], 'norm1_g': out['norm1_g'], 'w_in': out['w_in'], 'w_decay_f': out['w_decay_f'], 'b_decay_f': out['b_decay_f'], 'w_decay_b': out['w_decay_b'], 'b_decay_b': out['b_decay_b'], 'gla_norm_g': out['gla_norm_g'], 'gmlp_ln_g': out['gmlp_ln_g'], 'gmlp_ln_b': out['gmlp_ln_b'], 'w_spatial': out['w_spatial'], 'b_spatial': out['b_spatial'], 'w_out': out['w_out'], 'norm2_g': out['norm2_g'], 'w_gate': out['w_gate'], 'w_up': out['w_up'], 'w_down': out['w_down'], 'final_norm_g': out['final_norm_g'], 'loss_target': out['loss_target'], 'm_norm1_g': out['m_norm1_g'], 'm_w_in': out['m_w_in'], 'm_w_decay_f': out['m_w_decay_f'], 'm_b_decay_f': out['m_b_decay_f'], 'm_w_decay_b': out['m_w_decay_b'], 'm_b_decay_b': out['m_b_decay_b'], 'm_gla_norm_g': out['m_gla_norm_g'], 'm_gmlp_ln_g': out['m_gmlp_ln_g'], 'm_gmlp_ln_b': out['m_gmlp_ln_b'], 'm_w_spatial': out['m_w_spatial'], 'm_b_spatial': out['m_b_spatial'], 'm_w_out': out['m_w_out'], 'm_norm2_g': out['m_norm2_g'], 'm_w_gate': out['m_w_gate'], 'm_w_up': out['m_w_up'], 'm_w_down': out['m_w_down'], 'm_final_norm_g': out['m_final_norm_g'], 'v_norm1_g': out['v_norm1_g'], 'v_w_in': out['v_w_in'], 'v_w_decay_f': out['v_w_decay_f'], 'v_b_decay_f': out['v_b_decay_f'], 'v_w_decay_b': out['v_w_decay_b'], 'v_b_decay_b': out['v_b_decay_b'], 'v_gla_norm_g': out['v_gla_norm_g'], 'v_gmlp_ln_g': out['v_gmlp_ln_g'], 'v_gmlp_ln_b': out['v_gmlp_ln_b'], 'v_w_spatial': out['v_w_spatial'], 'v_b_spatial': out['v_b_spatial'], 'v_w_out': out['v_w_out'], 'v_norm2_g': out['v_norm2_g'], 'v_w_gate': out['v_w_gate'], 'v_w_up': out['v_w_up'], 'v_w_down': out['v_w_down'], 'v_final_norm_g': out['v_final_norm_g']}


def _loss(weights, diff, rest, loss_target):
    with _jax.named_scope("forward"):
        args = {**rest, TWIN_DIFF_INPUT: diff, **{k: w.astype(_WEIGHT_DTYPES[k]) for k, w in weights.items()}}
        y = _forward(args)
    with _jax.named_scope("loss_head"):
        err = _jnp.square(y.astype(_jnp.float32) - loss_target)
        return 0.5 * _jnp.sum(_jnp.mean(err, axis=-1)) if err.ndim else 0.5 * err


def _adamw(w, g, m, v):
    m = ADAM_B1 * m + (1.0 - ADAM_B1) * g
    v = ADAM_B2 * v + (1.0 - ADAM_B2) * _jnp.square(g)
    m_hat = m / (1.0 - ADAM_B1 ** ADAM_STEP)
    v_hat = v / (1.0 - ADAM_B2 ** ADAM_STEP)
    delta = -ADAM_LR * (m_hat / (_jnp.sqrt(v_hat) + ADAM_EPS) + ADAM_WD * w)
    return delta, m, v


def reference(x, norm1_g, w_in, w_decay_f, b_decay_f, w_decay_b, b_decay_b, gla_norm_g, gmlp_ln_g, gmlp_ln_b, w_spatial, b_spatial, w_out, norm2_g, w_gate, w_up, w_down, final_norm_g, loss_target, m_norm1_g, m_w_in, m_w_decay_f, m_b_decay_f, m_w_decay_b, m_b_decay_b, m_gla_norm_g, m_gmlp_ln_g, m_gmlp_ln_b, m_w_spatial, m_b_spatial, m_w_out, m_norm2_g, m_w_gate, m_w_up, m_w_down, m_final_norm_g, v_norm1_g, v_w_in, v_w_decay_f, v_b_decay_f, v_w_decay_b, v_b_decay_b, v_gla_norm_g, v_gmlp_ln_g, v_gmlp_ln_b, v_w_spatial, v_b_spatial, v_w_out, v_norm2_g, v_w_gate, v_w_up, v_w_down, v_final_norm_g):
    given = dict(x=x, norm1_g=norm1_g, w_in=w_in, w_decay_f=w_decay_f, b_decay_f=b_decay_f, w_decay_b=w_decay_b, b_decay_b=b_decay_b, gla_norm_g=gla_norm_g, gmlp_ln_g=gmlp_ln_g, gmlp_ln_b=gmlp_ln_b, w_spatial=w_spatial, b_spatial=b_spatial, w_out=w_out, norm2_g=norm2_g, w_gate=w_gate, w_up=w_up, w_down=w_down, final_norm_g=final_norm_g, loss_target=loss_target, m_norm1_g=m_norm1_g, m_w_in=m_w_in, m_w_decay_f=m_w_decay_f, m_b_decay_f=m_b_decay_f, m_w_decay_b=m_w_decay_b, m_b_decay_b=m_b_decay_b, m_gla_norm_g=m_gla_norm_g, m_gmlp_ln_g=m_gmlp_ln_g, m_gmlp_ln_b=m_gmlp_ln_b, m_w_spatial=m_w_spatial, m_b_spatial=m_b_spatial, m_w_out=m_w_out, m_norm2_g=m_norm2_g, m_w_gate=m_w_gate, m_w_up=m_w_up, m_w_down=m_w_down, m_final_norm_g=m_final_norm_g, v_norm1_g=v_norm1_g, v_w_in=v_w_in, v_w_decay_f=v_w_decay_f, v_b_decay_f=v_b_decay_f, v_w_decay_b=v_w_decay_b, v_b_decay_b=v_b_decay_b, v_gla_norm_g=v_gla_norm_g, v_gmlp_ln_g=v_gmlp_ln_g, v_gmlp_ln_b=v_gmlp_ln_b, v_w_spatial=v_w_spatial, v_b_spatial=v_b_spatial, v_w_out=v_w_out, v_norm2_g=v_norm2_g, v_w_gate=v_w_gate, v_w_up=v_w_up, v_w_down=v_w_down, v_final_norm_g=v_final_norm_g)
    weights = {n: given[n] for n in TWIN_WEIGHTS}
    shared = {n: given[n] for n in SHARED_INPUTS}
    per_example = {n: given[n] for n in ['x']}
    grad_fn = _jax.value_and_grad(_loss, argnums=(0, 1))

    def one_microbatch(ex, loss_target):
        ex = dict(ex)
        diff = ex.pop(TWIN_DIFF_INPUT)
        return grad_fn(weights, diff, {**shared, **ex}, loss_target)

    if N_MICROBATCH == 1:
        loss, (grad_w, grad_x) = one_microbatch(per_example, given["loss_target"])
    else:
        def body(carry, xs):
            loss_sum, grad_sum = carry
            l_k, (gw_k, gx_k) = one_microbatch(xs[0], xs[1])
            with _jax.named_scope("update"):
                return (loss_sum + l_k, _jax.tree.map(_jnp.add, grad_sum, gw_k)), gx_k

        init = (_jnp.zeros((), _jnp.float32), _jax.tree.map(_jnp.zeros_like, weights))
        (loss, grad_w), grad_x = _jax.lax.scan(body, init, (per_example, given["loss_target"]))
    with _jax.named_scope("update"):
        delta_w, new_m, new_v = {}, {}, {}
        for n in TWIN_WEIGHTS:
            delta_w[n], new_m[n], new_v[n] = _adamw(weights[n], grad_w[n], given["m_" + n], given["v_" + n])
    return (loss, grad_x, *[grad_w[n] for n in TWIN_WEIGHTS], *[delta_w[n] for n in TWIN_WEIGHTS],
            *[new_m[n] for n in TWIN_WEIGHTS], *[new_v[n] for n in TWIN_WEIGHTS])
```

```python
import functools
import math

import jax
import jax.numpy as jnp
from jax import lax
from jax.experimental import pallas as pl
from jax.experimental.pallas import tpu as pltpu

F32 = jnp.float32
BF16 = jnp.bfloat16

D_MODEL = 1024
GLA_HEADS = 4
GLA_DK = 64
GLA_DV = 128
KEY_W = GLA_HEADS * GLA_DK
VAL_W = GLA_HEADS * GLA_DV
LOWRANK = 16
GLA_TAU = 16.0
GLA_CHUNK = 64
GMLP_W = 512
GMLP_GROUPS = 4
GMLP_CHUNK = 128
D_FF = 2816
EPS = 1e-6
Q_SCALE = GLA_DK ** -0.5
PROJ_PAD = 2688
LR_COL = 2560
LANE = 128
N_DEV = 8

ADAM_LR = 0.001
ADAM_B1 = 0.9
ADAM_B2 = 0.999
ADAM_EPS = 1e-08
ADAM_WD = 0.01
ADAM_STEP = 10

VMEM_LIMIT = 56 * 1024 * 1024
MESH_ID = pl.DeviceIdType.MESH
INV_SQRT2 = 0.7071067811865476
INV_SQRT_2PI = 0.3989422804014327


def _params(n_axes=1):
    return pltpu.CompilerParams(dimension_semantics=("arbitrary",) * n_axes, vmem_limit_bytes=VMEM_LIMIT)


def _mm(a, b):
    return jnp.dot(a.astype(BF16), b.astype(BF16), preferred_element_type=F32)


def _mm_nt(a, b):
    return lax.dot_general(a.astype(BF16), b.astype(BF16), (((1,), (1,)), ((), ())), preferred_element_type=F32)


def _mm_tn(a, b):
    return lax.dot_general(a.astype(BF16), b.astype(BF16), (((0,), (0,)), ((), ())), preferred_element_type=F32)


def _const_spec(shape):
    nd = len(shape)
    return pl.BlockSpec(shape, lambda *_: (0,) * nd, pipeline_mode=pl.Buffered(1))


def _acc_spec(shape):
    nd = len(shape)
    return pl.BlockSpec(shape, lambda *_: (0,) * nd)


def _gelu(x):
    return 0.5 * x * (1.0 + lax.erf(x * INV_SQRT2))


def _gelu_grad(x):
    return 0.5 * (1.0 + lax.erf(x * INV_SQRT2)) + x * jnp.exp(-0.5 * x * x) * INV_SQRT_2PI


def _silu_and_grad(x):
    s = jax.nn.sigmoid(x)
    return x * s, s * (1.0 + x * (1.0 - s))


def _in_proj(x, g1, w_main, tm):
    t = x.shape[0]

    def body(x_ref, g_ref, w_ref, p_ref, h_ref):
        xv = x_ref[...]
        r = lax.rsqrt(jnp.mean(xv * xv, axis=-1, keepdims=True) + EPS)
        h = (xv * r * g_ref[...]).astype(BF16)
        h_ref[...] = h
        p_ref[...] = jnp.dot(h, w_ref[...], preferred_element_type=F32)

    return pl.pallas_call(
        body, name="in_proj", grid=(t // tm,),
        in_specs=[pl.BlockSpec((tm, D_MODEL), lambda i: (i, 0)), _const_spec((1, D_MODEL)),
                  _const_spec((D_MODEL, PROJ_PAD))],
        out_specs=(pl.BlockSpec((tm, PROJ_PAD), lambda i: (i, 0)), pl.BlockSpec((tm, D_MODEL), lambda i: (i, 0))),
        out_shape=(jax.ShapeDtypeStruct((t, PROJ_PAD), F32), jax.ShapeDtypeStruct((t, D_MODEL), BF16)),
        compiler_params=_params(),
    )(x, g1, w_main)


def _tri(upper):
    r = lax.broadcasted_iota(jnp.int32, (GLA_CHUNK, GLA_CHUNK), 0)
    c = lax.broadcasted_iota(jnp.int32, (GLA_CHUNK, GLA_CHUNK), 1)
    return jnp.where((c >= r) if upper else (c <= r), 1.0, 0.0).astype(BF16)


def _tri_matmul(tri, a):
    a1 = a.astype(BF16)
    r1 = a - a1.astype(F32)
    a2 = r1.astype(BF16)
    a3 = (r1 - a2.astype(F32)).astype(BF16)
    dot = functools.partial(jnp.dot, preferred_element_type=F32)
    return dot(tri, a1) + dot(tri, a2) + dot(tri, a3)


def _gla_masks(rev):
    dk_bits, dv_bits = GLA_DK.bit_length() - 1, GLA_DV.bit_length() - 1
    key_head = lax.broadcasted_iota(jnp.int32, (GLA_CHUNK, KEY_W), 1) >> dk_bits
    val_head = lax.broadcasted_iota(jnp.int32, (GLA_CHUNK, VAL_W), 1) >> dv_bits
    t = lax.broadcasted_iota(jnp.int32, (GLA_HEADS * GLA_CHUNK, GLA_CHUNK), 0) & (GLA_CHUNK - 1)
    s = lax.broadcasted_iota(jnp.int32, (GLA_HEADS * GLA_CHUNK, GLA_CHUNK), 1)
    causal = (s >= t) if rev else (s <= t)
    blockdiag = (lax.broadcasted_iota(jnp.int32, (VAL_W, KEY_W), 0) >> dv_bits
                 == lax.broadcasted_iota(jnp.int32, (VAL_W, KEY_W), 1) >> dk_bits)
    return key_head, val_head, causal, blockdiag


def _stack_heads(a, head_of_lane):
    return jnp.concatenate([jnp.where(head_of_lane == h, a, 0.0) for h in range(GLA_HEADS)], axis=0)


def _chunk_terms(la_c, q_c, k_c, tri, rev):
    b = _tri_matmul(tri, la_c)
    bl = b[0:1] if rev else b[GLA_CHUNK - 1:GLA_CHUNK]
    eb = jnp.exp(b)
    enb = jnp.exp(-b)
    ee = jnp.exp(bl - b)
    return bl, eb, enb, ee, q_c * Q_SCALE * eb, k_c * enb, k_c * ee


def _log_decay(lr_ref, wd_ref, bd_ref):
    z = _mm(lr_ref[...], wd_ref[...]) + bd_ref[...]
    return z, jax.nn.log_sigmoid(z) * (1.0 / GLA_TAU)


def _gla_fwd(p, wd_pad, bd, rev, tg):
    t = p.shape[0]
    nt = t // tg
    nc = tg // GLA_CHUNK
    tile = (lambda i: nt - 1 - i) if rev else (lambda i: i)

    def body(q_ref, k_ref, v_ref, lr_ref, wd_ref, bd_ref, o_ref, st_ref, state):
        @pl.when(pl.program_id(0) == 0)
        def _():
            state[...] = jnp.zeros_like(state)

        key_head, _, causal, blockdiag = _gla_masks(rev)
        tri = _tri(rev)
        _, la = _log_decay(lr_ref, wd_ref, bd_ref)
        for cc in range(nc):
            c = nc - 1 - cc if rev else cc
            rows = slice(c * GLA_CHUNK, (c + 1) * GLA_CHUNK)
            v_c = v_ref[rows, :].astype(BF16)
            bl, _, _, _, qd, kd, ke = _chunk_terms(la[rows], q_ref[rows, :], k_ref[rows, :], tri, rev)
            a_all = jnp.where(causal, _mm_nt(_stack_heads(qd, key_head), kd), 0.0)
            r = _mm(a_all, v_c)
            o_intra = jnp.concatenate(
                [r[h * GLA_CHUNK:(h + 1) * GLA_CHUNK, h * GLA_DV:(h + 1) * GLA_DV] for h in range(GLA_HEADS)], axis=1)
            st = state[...]
            o_ref[rows, :] = o_intra + _mm_nt(qd, st)
            st_ref[c] = st.astype(BF16)
            state[...] = st * jnp.exp(bl) + jnp.where(blockdiag, _mm_tn(v_c, ke), 0.0)

    return pl.pallas_call(
        body, name="gla_fwd_rev" if rev else "gla_fwd", grid=(nt,),
        in_specs=[pl.BlockSpec((tg, KEY_W), lambda i: (tile(i), 0)),
                  pl.BlockSpec((tg, KEY_W), lambda i: (tile(i), 1)),
                  pl.BlockSpec((tg, VAL_W), lambda i: (tile(i), 1)),
                  pl.BlockSpec((tg, LANE), lambda i: (tile(i), LR_COL // LANE)),
                  _const_spec((LANE, KEY_W)), _const_spec((1, KEY_W))],
        out_specs=(pl.BlockSpec((tg, VAL_W), lambda i: (tile(i), 0)),
                   pl.BlockSpec((nc, VAL_W, KEY_W), lambda i: (tile(i), 0, 0))),
        out_shape=(jax.ShapeDtypeStruct((t, VAL_W), F32),
                   jax.ShapeDtypeStruct((t // GLA_CHUNK, VAL_W, KEY_W), BF16)),
        scratch_shapes=[pltpu.VMEM((VAL_W, KEY_W), F32)],
        compiler_params=_params(),
    )(p, p, p, p, wd_pad, bd)


def _gla_bwd(p, wd_pad, bd, states, d_o, rev, tg):
    t = p.shape[0]
    nt = t // tg
    nc = tg // GLA_CHUNK
    tile = (lambda i: i) if rev else (lambda i: nt - 1 - i)

    def body(q_ref, k_ref, v_ref, lr_ref, wd_ref, bd_ref, st_ref, do_ref,
             dq_ref, dk_ref, dv_ref, dlr_ref, dwd_ref, dbd_ref, dstate, dz_scr):
        @pl.when(pl.program_id(0) == 0)
        def _():
            dstate[...] = jnp.zeros_like(dstate)
            dwd_ref[...] = jnp.zeros_like(dwd_ref)
            dbd_ref[...] = jnp.zeros_like(dbd_ref)

        key_head, val_head, causal, blockdiag = _gla_masks(rev)
        tri = _tri(rev)
        tri_t = _tri(not rev)
        z, la = _log_decay(lr_ref, wd_ref, bd_ref)
        dlog = jax.nn.sigmoid(-z) * (1.0 / GLA_TAU)
        for cc in range(nc):
            c = cc if rev else nc - 1 - cc
            rows = slice(c * GLA_CHUNK, (c + 1) * GLA_CHUNK)
            v_c = v_ref[rows, :].astype(BF16)
            do_c = do_ref[rows, :]
            bl, eb, enb, ee, qd, kd, ke = _chunk_terms(la[rows], q_ref[rows, :], k_ref[rows, :], tri, rev)
            qd_stack = _stack_heads(qd, key_head)
            do_stack = _stack_heads(do_c, val_head)
            a_all = jnp.where(causal, _mm_nt(qd_stack, kd), 0.0)
            da_all = jnp.where(causal, _mm_nt(do_stack, v_c), 0.0)
            dst = dstate[...]
            st_prev = st_ref[c]
            dv_ref[rows, :] = _mm_tn(a_all, do_stack) + _mm_nt(ke, dst)
            r2 = _mm(da_all, kd)
            dqd = _mm(do_c, st_prev)
            for h in range(GLA_HEADS):
                dqd = dqd + jnp.where(key_head == h, r2[h * GLA_CHUNK:(h + 1) * GLA_CHUNK, :], 0.0)
            dkd = _mm_tn(da_all, qd_stack)
            dke = _mm(v_c, dst)
            ebl = jnp.exp(bl)
            dbl = (jnp.sum(dst * st_prev.astype(F32), axis=0, keepdims=True) * ebl
                   + jnp.sum(dke * ke, axis=0, keepdims=True))
            dstate[...] = dst * ebl + jnp.where(blockdiag, _mm_tn(do_c, qd), 0.0)
            dq_ref[rows, :] = dqd * eb * Q_SCALE
            dk_ref[rows, :] = dkd * enb + dke * ee
            db = dqd * qd - dkd * kd - dke * ke
            dz_scr[rows, :] = (_tri_matmul(tri_t, db) + dbl) * dlog[rows]
        dz = dz_scr[...]
        dlr_ref[...] = _mm_nt(dz, wd_ref[...])
        dwd_ref[...] += _mm_tn(lr_ref[...], dz)
        dbd_ref[...] += jnp.sum(dz, axis=0, keepdims=True)

    return pl.pallas_call(
        body, name="gla_bwd_rev" if rev else "gla_bwd", grid=(nt,),
        in_specs=[pl.BlockSpec((tg, KEY_W), lambda i: (tile(i), 0)),
                  pl.BlockSpec((tg, KEY_W), lambda i: (tile(i), 1)),
                  pl.BlockSpec((tg, VAL_W), lambda i: (tile(i), 1)),
                  pl.BlockSpec((tg, LANE), lambda i: (tile(i), LR_COL // LANE)),
                  _const_spec((LANE, KEY_W)), _const_spec((1, KEY_W)),
                  pl.BlockSpec((nc, VAL_W, KEY_W), lambda i: (tile(i), 0, 0)),
                  pl.BlockSpec((tg, VAL_W), lambda i: (tile(i), 0))],
        out_specs=(pl.BlockSpec((tg, KEY_W), lambda i: (tile(i), 0)),
                   pl.BlockSpec((tg, KEY_W), lambda i: (tile(i), 0)),
                   pl.BlockSpec((tg, VAL_W), lambda i: (tile(i), 0)),
                   pl.BlockSpec((tg, LANE), lambda i: (tile(i), 0)),
                   _acc_spec((LANE, KEY_W)), _acc_spec((1, KEY_W))),
        out_shape=(jax.ShapeDtypeStruct((t, KEY_W), F32), jax.ShapeDtypeStruct((t, KEY_W), F32),
                   jax.ShapeDtypeStruct((t, VAL_W), F32), jax.ShapeDtypeStruct((t, LANE), F32),
                   jax.ShapeDtypeStruct((LANE, KEY_W), F32), jax.ShapeDtypeStruct((1, KEY_W), F32)),
        scratch_shapes=[pltpu.VMEM((VAL_W, KEY_W), F32), pltpu.VMEM((tg, KEY_W), F32)],
        compiler_params=_params(),
    )(p, p, p, p, wd_pad, bd, states, d_o)


def _head_rms(o):
    parts, scales = [], []
    for h in range(GLA_HEADS):
        oh = o[:, h * GLA_DV:(h + 1) * GLA_DV]
        r = lax.rsqrt(jnp.mean(oh * oh, axis=-1, keepdims=True) + EPS)
        parts.append(oh * r)
        scales.append(jnp.broadcast_to(r, oh.shape))
    return jnp.concatenate(parts, axis=1), jnp.concatenate(scales, axis=1)


def _layernorm_stats(zv):
    mu = jnp.mean(zv, axis=-1, keepdims=True)
    xc = zv - mu
    rs = lax.rsqrt(jnp.mean(xc * xc, axis=-1, keepdims=True) + EPS)
    return xc * rs, rs


def _mix_fwd(x, o_f, o_b, p, gla_g, ln_g, ln_b, w_sp, b_sp, w_out, tm):
    t = x.shape[0]
    nch = tm // GMLP_CHUNK

    def body(x_ref, of_ref, ob_ref, pg_ref, pu_ref, pv_ref, gg_ref, lg_ref, lb_ref, ws_ref, bs_ref, wo_ref,
             x1_ref, y_ref, s_scr):
        on, _ = _head_rms(of_ref[...] + ob_ref[...])
        pg = pg_ref[...]
        y_a = on * gg_ref[...] * (pg * jax.nn.sigmoid(pg))
        zu = _gelu(pu_ref[...])
        vhat, _ = _layernorm_stats(_gelu(pv_ref[...]))
        vln = (vhat * lg_ref[...] + lb_ref[...]).astype(BF16)
        for g in range(GMLP_GROUPS):
            w_g = ws_ref[g].astype(BF16)
            b_g = bs_ref[g]
            cols = slice(g * LANE, (g + 1) * LANE)
            for n in range(nch):
                rows = slice(n * GMLP_CHUNK, (n + 1) * GMLP_CHUNK)
                s_scr[rows, cols] = jnp.dot(w_g, vln[rows, cols], preferred_element_type=F32) + b_g
        ycat = jnp.concatenate([y_a, zu * s_scr[...]], axis=1).astype(BF16)
        y_ref[...] = ycat
        x1_ref[...] = x_ref[...] + jnp.dot(ycat, wo_ref[...], preferred_element_type=F32)

    half = lambda j: pl.BlockSpec((tm, VAL_W), lambda i: (i, j))
    return pl.pallas_call(
        body, name="mix_fwd", grid=(t // tm,),
        in_specs=[pl.BlockSpec((tm, D_MODEL), lambda i: (i, 0)), half(0), half(0), half(2), half(3), half(4),
                  _const_spec((1, VAL_W)), _const_spec((1, GMLP_W)), _const_spec((1, GMLP_W)),
                  _const_spec((GMLP_GROUPS, GMLP_CHUNK, GMLP_CHUNK)), _const_spec((GMLP_GROUPS, GMLP_CHUNK, 1)),
                  _const_spec((D_MODEL, D_MODEL))],
        out_specs=(pl.BlockSpec((tm, D_MODEL), lambda i: (i, 0)), pl.BlockSpec((tm, D_MODEL), lambda i: (i, 0))),
        out_shape=(jax.ShapeDtypeStruct((t, D_MODEL), F32), jax.ShapeDtypeStruct((t, D_MODEL), BF16)),
        scratch_shapes=[pltpu.VMEM((tm, GMLP_W), F32)],
        compiler_params=_params(),
    )(x, o_f, o_b, p, p, p, gla_g, ln_g, ln_b, w_sp, b_sp, w_out)


def _mix_bwd(dx1, ycat, o_f, o_b, p, gla_g, ln_g, ln_b, w_sp, b_sp, w_out, tm):
    t = dx1.shape[0]
    nch = tm // GMLP_CHUNK

    def body(dx1_ref, y_ref, of_ref, ob_ref, pg_ref, pu_ref, pv_ref, gg_ref, lg_ref, lb_ref, ws_ref, bs_ref, wo_ref,
             do_ref, dpg_ref, dpu_ref, dpv_ref, dwo_ref, dgg_ref, dlg_ref, dlb_ref, dws_ref, dbs_ref,
             s_scr, dvln_scr):
        @pl.when(pl.program_id(0) == 0)
        def _():
            for ref in (dwo_ref, dgg_ref, dlg_ref, dlb_ref, dws_ref, dbs_ref):
                ref[...] = jnp.zeros_like(ref)

        dx1 = dx1_ref[...].astype(BF16)
        dycat = _mm_nt(dx1, wo_ref[...])
        dwo_ref[...] += _mm_tn(y_ref[...], dx1)
        dy_a = dycat[:, :VAL_W]
        dy_b = dycat[:, VAL_W:]
        on, r = _head_rms(of_ref[...] + ob_ref[...])
        pg = pg_ref[...]
        sil, dsil = _silu_and_grad(pg)
        gg = gg_ref[...]
        dgg_ref[...] += jnp.sum(dy_a * sil * on, axis=0, keepdims=True)
        don = dy_a * sil * gg
        prod = don * on
        means = jnp.concatenate(
            [jnp.broadcast_to(jnp.mean(prod[:, h * GLA_DV:(h + 1) * GLA_DV], axis=-1, keepdims=True),
                              (tm, GLA_DV)) for h in range(GLA_HEADS)], axis=1)
        do_ref[...] = r * (don - on * means)
        dpg_ref[...] = dy_a * on * gg * dsil
        pu = pu_ref[...]
        pv = pv_ref[...]
        zu = _gelu(pu)
        vhat, rs = _layernorm_stats(_gelu(pv))
        lg = lg_ref[...]
        vln = (vhat * lg + lb_ref[...]).astype(BF16)
        ds32 = dy_b * zu
        ds = ds32.astype(BF16)
        for g in range(GMLP_GROUPS):
            w_g = ws_ref[g].astype(BF16)
            b_g = bs_ref[g]
            cols = slice(g * LANE, (g + 1) * LANE)
            dw_g = jnp.zeros((GMLP_CHUNK, GMLP_CHUNK), F32)
            db_g = jnp.zeros((GMLP_CHUNK, 1), F32)
            for n in range(nch):
                rows = slice(n * GMLP_CHUNK, (n + 1) * GMLP_CHUNK)
                v_blk = vln[rows, cols]
                ds_blk = ds[rows, cols]
                s_scr[rows, cols] = jnp.dot(w_g, v_blk, preferred_element_type=F32) + b_g
                dw_g = dw_g + _mm_nt(ds_blk, v_blk)
                db_g = db_g + jnp.sum(ds32[rows, cols], axis=-1, keepdims=True)
                dvln_scr[rows, cols] = _mm_tn(w_g, ds_blk)
            dws_ref[g] += dw_g
            dbs_ref[g] += db_g
        dpu_ref[...] = dy_b * s_scr[...] * _gelu_grad(pu)
        dvln = dvln_scr[...]
        dlg_ref[...] += jnp.sum(dvln * vhat, axis=0, keepdims=True)
        dlb_ref[...] += jnp.sum(dvln, axis=0, keepdims=True)
        dvhat = dvln * lg
        dzv = rs * (dvhat - jnp.mean(dvhat, axis=-1, keepdims=True)
                    - vhat * jnp.mean(dvhat * vhat, axis=-1, keepdims=True))
        dpv_ref[...] = dzv * _gelu_grad(pv)

    half = lambda j: pl.BlockSpec((tm, VAL_W), lambda i: (i, j))
    full = pl.BlockSpec((tm, D_MODEL), lambda i: (i, 0))
    sp_shape = (GMLP_GROUPS, GMLP_CHUNK, GMLP_CHUNK)
    bs_shape = (GMLP_GROUPS, GMLP_CHUNK, 1)
    return pl.pallas_call(
        body, name="mix_bwd", grid=(t // tm,),
        in_specs=[full, full, half(0), half(0), half(2), half(3), half(4),
                  _const_spec((1, VAL_W)), _const_spec((1, GMLP_W)), _const_spec((1, GMLP_W)),
                  _const_spec(sp_shape), _const_spec(bs_shape), _const_spec((D_MODEL, D_MODEL))],
        out_specs=(half(0), half(0), half(0), half(0), _acc_spec((D_MODEL, D_MODEL)), _acc_spec((1, VAL_W)),
                   _acc_spec((1, GMLP_W)), _acc_spec((1, GMLP_W)), _acc_spec(sp_shape), _acc_spec(bs_shape)),
        out_shape=(jax.ShapeDtypeStruct((t, VAL_W), F32),) * 4 + (
            jax.ShapeDtypeStruct((D_MODEL, D_MODEL), F32), jax.ShapeDtypeStruct((1, VAL_W), F32),
            jax.ShapeDtypeStruct((1, GMLP_W), F32), jax.ShapeDtypeStruct((1, GMLP_W), F32),
            jax.ShapeDtypeStruct(sp_shape, F32), jax.ShapeDtypeStruct(bs_shape, F32)),
        scratch_shapes=[pltpu.VMEM((tm, GMLP_W), F32), pltpu.VMEM((tm, GMLP_W), F32)],
        compiler_params=_params(),
    )(dx1, ycat, o_f, o_b, p, p, p, gla_g, ln_g, ln_b, w_sp, b_sp, w_out)


def _rms_bwd(dy_scaled, xn, r):
    return r * (dy_scaled - xn * jnp.mean(dy_scaled * xn, axis=-1, keepdims=True))


def _ffn(x1, target, g2, gf, w_gate, w_up, w_down, tm):
    t = x1.shape[0]

    def body(x1_ref, tg_ref, g2_ref, gf_ref, wg_ref, wu_ref, wd_ref,
             dx1_ref, h2_ref, dgate_ref, dup_ref, act_ref, dx2_ref, loss_ref, dgf_ref, dg2_ref):
        @pl.when(pl.program_id(0) == 0)
        def _():
            for ref in (loss_ref, dgf_ref, dg2_ref):
                ref[...] = jnp.zeros_like(ref)

        x1v = x1_ref[...]
        g2v = g2_ref[...]
        gfv = gf_ref[...]
        r2 = lax.rsqrt(jnp.mean(x1v * x1v, axis=-1, keepdims=True) + EPS)
        xn1 = x1v * r2
        h2 = (xn1 * g2v).astype(BF16)
        h2_ref[...] = h2
        gate = jnp.dot(h2, wg_ref[...], preferred_element_type=F32)
        up = jnp.dot(h2, wu_ref[...], preferred_element_type=F32)
        sil, dsil = _silu_and_grad(gate)
        act = (sil * up).astype(BF16)
        act_ref[...] = act
        x2 = x1v + jnp.dot(act, wd_ref[...], preferred_element_type=F32)
        rf = lax.rsqrt(jnp.mean(x2 * x2, axis=-1, keepdims=True) + EPS)
        xn2 = x2 * rf
        err = xn2 * gfv - tg_ref[...]
        loss_ref[...] += 0.5 * jnp.sum(jnp.mean(err * err, axis=-1, keepdims=True))
        dy = err * (1.0 / D_MODEL)
        dgf_ref[...] += jnp.sum(dy * xn2, axis=0, keepdims=True)
        dx2 = _rms_bwd(dy * gfv, xn2, rf)
        dx2b = dx2.astype(BF16)
        dx2_ref[...] = dx2b
        dact = _mm_nt(dx2b, wd_ref[...])
        dgate = (dact * up * dsil).astype(BF16)
        dup = (dact * sil).astype(BF16)
        dgate_ref[...] = dgate
        dup_ref[...] = dup
        dh2 = _mm_nt(dgate, wg_ref[...]) + _mm_nt(dup, wu_ref[...])
        dg2_ref[...] += jnp.sum(dh2 * xn1, axis=0, keepdims=True)
        dx1_ref[...] = dx2 + _rms_bwd(dh2 * g2v, xn1, r2)

    row = lambda w: pl.BlockSpec((tm, w), lambda i: (i, 0))
    return pl.pallas_call(
        body, name="ffn_fwd_bwd", grid=(t // tm,),
        in_specs=[row(D_MODEL), row(D_MODEL), _const_spec((1, D_MODEL)), _const_spec((1, D_MODEL)),
                  _const_spec((D_MODEL, D_FF)), _const_spec((D_MODEL, D_FF)), _const_spec((D_FF, D_MODEL))],
        out_specs=(row(D_MODEL), row(D_MODEL), row(D_FF), row(D_FF), row(D_FF), row(D_MODEL),
                   _acc_spec((8, LANE)), _acc_spec((1, D_MODEL)), _acc_spec((1, D_MODEL))),
        out_shape=(jax.ShapeDtypeStruct((t, D_MODEL), F32), jax.ShapeDtypeStruct((t, D_MODEL), BF16),
                   jax.ShapeDtypeStruct((t, D_FF), BF16), jax.ShapeDtypeStruct((t, D_FF), BF16),
                   jax.ShapeDtypeStruct((t, D_FF), BF16), jax.ShapeDtypeStruct((t, D_MODEL), BF16),
                   jax.ShapeDtypeStruct((8, LANE), F32), jax.ShapeDtypeStruct((1, D_MODEL), F32),
                   jax.ShapeDtypeStruct((1, D_MODEL), F32)),
        compiler_params=_params(),
    )(x1, target, g2, gf, w_gate, w_up, w_down)


def _matmul_tn(a, bs, tn, tk, name):
    t, m = a.shape
    n = bs[0].shape[1]
    nb = len(bs)

    def body(*refs):
        a_ref, b_refs, o_refs = refs[0], refs[1:1 + nb], refs[1 + nb:]

        @pl.when(pl.program_id(1) == 0)
        def _():
            for o_ref in o_refs:
                o_ref[...] = jnp.zeros_like(o_ref)

        a_t = a_ref[...]
        for b_ref, o_ref in zip(b_refs, o_refs):
            o_ref[...] += _mm_tn(a_t, b_ref[...])

    return pl.pallas_call(
        body, name=name, grid=(n // tn, t // tk),
        in_specs=[pl.BlockSpec((tk, m), lambda j, k: (k, 0))] + [pl.BlockSpec((tk, tn), lambda j, k: (k, j))] * nb,
        out_specs=tuple(pl.BlockSpec((m, tn), lambda j, k: (0, j)) for _ in range(nb)),
        out_shape=tuple(jax.ShapeDtypeStruct((m, n), F32) for _ in range(nb)),
        compiler_params=_params(2),
    )(a, *bs)


def _in_proj_bwd(x, g1, dx1, dq_f, dq_b, dk_f, dk_b, dv_f, dv_b, dpg, dpu, dpv, dlr_f, dlr_b, w_main, tm):
    t = x.shape[0]

    def body(x_ref, g_ref, dx1_ref, dqf, dqb, dkf, dkb, dvf, dvb, dg, du, dv, dlf, dlb, w_ref,
             dx_ref, dp_ref, dg1_ref):
        @pl.when(pl.program_id(0) == 0)
        def _():
            dg1_ref[...] = jnp.zeros_like(dg1_ref)

        dp = jnp.concatenate([dqf[...] + dqb[...], dkf[...] + dkb[...], dvf[...] + dvb[...],
                              dg[...], du[...], dv[...], dlf[...] + dlb[...]], axis=1).astype(BF16)
        dp_ref[...] = dp
        dh = _mm_nt(dp, w_ref[...])
        xv = x_ref[...]
        r = lax.rsqrt(jnp.mean(xv * xv, axis=-1, keepdims=True) + EPS)
        xn = xv * r
        dg1_ref[...] += jnp.sum(dh * xn, axis=0, keepdims=True)
        dx_ref[...] = dx1_ref[...] + _rms_bwd(dh * g_ref[...], xn, r)

    row = lambda w: pl.BlockSpec((tm, w), lambda i: (i, 0))
    return pl.pallas_call(
        body, name="in_proj_bwd", grid=(t // tm,),
        in_specs=[row(D_MODEL), _const_spec((1, D_MODEL)), row(D_MODEL), row(KEY_W), row(KEY_W), row(KEY_W),
                  row(KEY_W), row(VAL_W), row(VAL_W), row(VAL_W), row(VAL_W), row(VAL_W), row(LANE), row(LANE),
                  _const_spec((D_MODEL, PROJ_PAD))],
        out_specs=(row(D_MODEL), row(PROJ_PAD), _acc_spec((1, D_MODEL))),
        out_shape=(jax.ShapeDtypeStruct((t, D_MODEL), F32), jax.ShapeDtypeStruct((t, PROJ_PAD), BF16),
                   jax.ShapeDtypeStruct((1, D_MODEL), F32)),
        compiler_params=_params(),
    )(x, g1, dx1, dq_f, dq_b, dk_f, dk_b, dv_f, dv_b, dpg, dpu, dpv, dlr_f, dlr_b, w_main)


def _adamw(w, g, m, v):
    m_new = ADAM_B1 * m + (1.0 - ADAM_B1) * g
    v_new = ADAM_B2 * v + (1.0 - ADAM_B2) * (g * g)
    m_hat = m_new / (1.0 - ADAM_B1 ** ADAM_STEP)
    v_hat = v_new / (1.0 - ADAM_B2 ** ADAM_STEP)
    delta = -ADAM_LR * (m_hat / (jnp.sqrt(v_hat) + ADAM_EPS) + ADAM_WD * w)
    return delta, m_new, v_new


def _adamw_shard(own, recv, w, m, v, tr, name):
    r, c = w.shape

    def body(own_ref, recv_ref, w_ref, m_ref, v_ref, g_ref, d_ref, nm_ref, nv_ref):
        g = own_ref[...]
        for k in range(3):
            g = g + recv_ref[k].astype(F32)
        g_ref[...] = g
        d_ref[...], nm_ref[...], nv_ref[...] = _adamw(w_ref[...], g, m_ref[...], v_ref[...])

    row = pl.BlockSpec((tr, c), lambda i: (i, 0))
    return pl.pallas_call(
        body, name=name, grid=(r // tr,),
        in_specs=[row, pl.BlockSpec((3, tr, c), lambda i: (0, i, 0)), row, row, row],
        out_specs=(row,) * 4, out_shape=(jax.ShapeDtypeStruct((r, c), F32),) * 4,
        compiler_params=_params(),
    )(own, recv, w, m, v)


def _adamw_small(g, w, m, v):
    def body(g_ref, w_ref, m_ref, v_ref, d_ref, nm_ref, nv_ref):
        d_ref[...], nm_ref[...], nv_ref[...] = _adamw(w_ref[...], g_ref[...], m_ref[...], v_ref[...])

    return pl.pallas_call(body, name="adamw_small", out_shape=(jax.ShapeDtypeStruct(g.shape, F32),) * 3,
                          compiler_params=pltpu.CompilerParams(vmem_limit_bytes=VMEM_LIMIT))(g, w, m, v)


def _mesh_pos():
    return lax.axis_index("x"), lax.axis_index("y"), lax.axis_index("c")


def _other_chips(x, y):
    return [(x, 1 - y), (1 - x, y), (1 - x, 1 - y)]


def _all_gather_weights(shards_f32, shard_small):
    arrays = list(shards_f32) + [shard_small]
    na = len(arrays)
    nbig = len(shards_f32)

    def body(*refs):
        in_refs, out_refs = refs[:na], refs[na:2 * na]
        stage = refs[2 * na:2 * na + nbig]
        send_sems, recv_sems, local_sems = refs[2 * na + nbig:]
        x, y, c = _mesh_pos()
        me, sibling = (x, y, c), (x, y, 1 - c)
        chips = _other_chips(x, y)
        for a in range(nbig):
            stage[a][...] = in_refs[a][...].astype(BF16)
        srcs = list(stage) + [in_refs[nbig]]

        def rows(a, pos):
            px, py, pc = pos
            return out_refs[a].at[4 * px + 2 * py + pc]

        def copy(a, k, block, to, src=None):
            return pltpu.make_async_remote_copy(
                src_ref=rows(a, block) if src is None else src, dst_ref=rows(a, block),
                send_sem=send_sems.at[a, k], recv_sem=recv_sems.at[a, k], device_id=to, device_id_type=MESH_ID)

        mine = [pltpu.make_async_copy(srcs[a], rows(a, me), local_sems.at[a]) for a in range(na)]
        for cp in mine:
            cp.start()
        first = []
        for a in range(na):
            first.append(copy(a, 0, me, sibling, src=srcs[a]))
            first += [copy(a, 1 + j, me, (*chip, c), src=srcs[a]) for j, chip in enumerate(chips)]
        for cp in first:
            cp.start()
        passed = []
        for j, chip in enumerate(chips):
            for a in range(na):
                copy(a, 1 + j, (*chip, c), me).wait_recv()
                cp = copy(a, 4 + j, (*chip, c), sibling)
                cp.start()
                passed.append(cp)
        for a in range(na):
            copy(a, 0, sibling, me).wait_recv()
            for j, chip in enumerate(chips):
                copy(a, 4 + j, (*chip, 1 - c), me).wait_recv()
        for cp in first + passed:
            cp.wait_send()
        for cp in mine:
            cp.wait()

    vmem = pl.BlockSpec(memory_space=pltpu.VMEM)
    hbm = pl.BlockSpec(memory_space=pl.ANY)
    out_shape = tuple(jax.ShapeDtypeStruct((N_DEV,) + s.shape, BF16) for s in shards_f32) + (
        jax.ShapeDtypeStruct((N_DEV,) + shard_small.shape, F32),)
    return pl.pallas_call(
        body, name="all_gather_weights", in_specs=[vmem] * na, out_specs=(hbm,) * na, out_shape=out_shape,
        scratch_shapes=[pltpu.VMEM(s.shape, BF16) for s in shards_f32] + [
            pltpu.SemaphoreType.DMA((na, 7)), pltpu.SemaphoreType.DMA((na, 7)), pltpu.SemaphoreType.DMA((na,))],
        compiler_params=pltpu.CompilerParams(vmem_limit_bytes=VMEM_LIMIT),
    )(*arrays)


def _exchange_with_sibling(grads):
    na = len(grads)

    def body(*refs):
        in_refs, out_refs = refs[:na], refs[na:2 * na]
        send_sems, recv_sems = refs[2 * na:]
        x, y, c = _mesh_pos()
        copies = [pltpu.make_async_remote_copy(
            src_ref=in_refs[a].at[:, pl.ds(1 - c, 1)], dst_ref=out_refs[a], send_sem=send_sems.at[a],
            recv_sem=recv_sems.at[a], device_id=(x, y, 1 - c), device_id_type=MESH_ID) for a in range(na)]
        for cp in copies:
            cp.start()
        for cp in copies:
            cp.wait()

    hbm = pl.BlockSpec(memory_space=pl.ANY)
    return pl.pallas_call(
        body, name="grad_exchange_sibling", in_specs=[hbm] * na, out_specs=(hbm,) * na,
        out_shape=tuple(jax.ShapeDtypeStruct((4, 1) + g.shape[2:], F32) for g in grads),
        scratch_shapes=[pltpu.SemaphoreType.DMA((na,)), pltpu.SemaphoreType.DMA((na,))],
    )(*grads)


def _chip_sum(my_pos, mine, from_sibling, tr, name):
    _, _, r, c = mine.shape

    def body(pos_ref, a_ref, b_ref, own_ref, out_ref):
        s = a_ref[0, 0] + b_ref[0, 0]

        @pl.when(pl.program_id(1) == 0)
        def _():
            own_ref[...] = s

        @pl.when(pl.program_id(1) > 0)
        def _():
            out_ref[0] = s.astype(BF16)

    grid_spec = pltpu.PrefetchScalarGridSpec(
        num_scalar_prefetch=1, grid=(r // tr, 4),
        in_specs=[pl.BlockSpec((1, 1, tr, c), lambda i, k, pos: (pos[0] ^ k, pos[1], i, 0)),
                  pl.BlockSpec((1, 1, tr, c), lambda i, k, pos: (pos[0] ^ k, 0, i, 0))],
        out_specs=(pl.BlockSpec((tr, c), lambda i, k, pos: (i, 0)),
                   pl.BlockSpec((1, tr, c), lambda i, k, pos: (jnp.maximum(k - 1, 0), i, 0))))
    return pl.pallas_call(
        body, name=name, grid_spec=grid_spec,
        out_shape=(jax.ShapeDtypeStruct((r, c), F32), jax.ShapeDtypeStruct((3, r, c), BF16)),
        compiler_params=_params(2),
    )(my_pos, mine, from_sibling)


def _exchange_with_chips(partials):
    na = len(partials)

    def body(*refs):
        in_refs, out_refs = refs[:na], refs[na:2 * na]
        send_sems, recv_sems = refs[2 * na:]
        x, y, c = _mesh_pos()
        copies = []
        for a in range(na):
            for j, chip in enumerate(_other_chips(x, y)):
                copies.append(pltpu.make_async_remote_copy(
                    src_ref=in_refs[a].at[j], dst_ref=out_refs[a].at[j], send_sem=send_sems.at[a, j],
                    recv_sem=recv_sems.at[a, j], device_id=(*chip, c), device_id_type=MESH_ID))
        for cp in copies:
            cp.start()
        for cp in copies:
            cp.wait()

    hbm = pl.BlockSpec(memory_space=pl.ANY)
    return pl.pallas_call(
        body, name="grad_exchange_chips", in_specs=[hbm] * na, out_specs=(hbm,) * na,
        out_shape=tuple(jax.ShapeDtypeStruct(g.shape, BF16) for g in partials),
        scratch_shapes=[pltpu.SemaphoreType.DMA((na, 3)), pltpu.SemaphoreType.DMA((na, 3))],
    )(*partials)


def _all_reduce_small(part):
    def body(in_ref, out_ref, gathered, send_sems, recv_sems):
        x, y, c = _mesh_pos()
        my_id = 4 * x + 2 * y + c
        copies = []
        for k in range(1, N_DEV):
            peer = (x ^ (k >> 2), y ^ ((k >> 1) & 1), c ^ (k & 1))
            copies.append(pltpu.make_async_remote_copy(
                src_ref=in_ref, dst_ref=gathered.at[my_id], send_sem=send_sems.at[k - 1],
                recv_sem=recv_sems.at[k - 1], device_id=peer, device_id_type=MESH_ID))
        for cp in copies:
            cp.start()
        gathered[my_id] = in_ref[...]
        for cp in copies:
            cp.wait()
        acc = gathered[0]
        for d in range(1, N_DEV):
            acc = acc + gathered[d]
        out_ref[...] = acc

    return pl.pallas_call(
        body, name="all_reduce_small", out_shape=jax.ShapeDtypeStruct(part.shape, F32),
        in_specs=[pl.BlockSpec(memory_space=pltpu.VMEM)], out_specs=pl.BlockSpec(memory_space=pltpu.VMEM),
        scratch_shapes=[pltpu.VMEM((N_DEV,) + part.shape, F32), pltpu.SemaphoreType.DMA((N_DEV - 1,)),
                        pltpu.SemaphoreType.DMA((N_DEV - 1,))],
        compiler_params=pltpu.CompilerParams(vmem_limit_bytes=VMEM_LIMIT),
    )(part)


def _unshard_cols(g):
    return jnp.transpose(g, (1, 0, 2)).reshape(g.shape[1], N_DEV * g.shape[2])


def _shard_cols(w):
    r, n = w.shape
    return jnp.transpose(w.reshape(r, N_DEV, n // N_DEV), (1, 0, 2)).reshape(4, 2, r, n // N_DEV)


def _pack_small(parts):
    flat = jnp.concatenate([a.reshape(-1) for a in parts])
    pad = (-flat.shape[0]) % (8 * LANE)
    return jnp.pad(flat, (0, pad)).reshape(-1, LANE)


def _unpack_small(packed, shapes):
    flat = packed.reshape(-1)
    out, off = [], 0
    for s in shapes:
        n = math.prod(s)
        out.append(flat[off:off + n].reshape(s))
        off += n
    return out


def _forward_backward(xt, target, norm1_g, w_main, wd_pad_f, b_decay_f, wd_pad_b, b_decay_b, gla_norm_g, ln_g, ln_b,
                      w_sp, b_sp_col, w_out_full, norm2_g, w_gate_full, w_up_full, w_down_full, gf):
    t = xt.shape[0]
    tile = lambda n: min(n, t)
    p, hb = _in_proj(xt, norm1_g, w_main, tile(512))
    o_f, st_f = _gla_fwd(p, wd_pad_f, b_decay_f, False, tile(512))
    o_b, st_b = _gla_fwd(p, wd_pad_b, b_decay_b, True, tile(512))
    x1, ycat = _mix_fwd(xt, o_f, o_b, p, gla_norm_g, ln_g, ln_b, w_sp, b_sp_col, w_out_full, tile(512))
    dx1, h2b, dgate, dup, act, dx2, loss_acc, d_gf, d_g2 = _ffn(
        x1, target, norm2_g, gf, w_gate_full, w_up_full, w_down_full, tile(256))
    dw_gate, dw_up = _matmul_tn(h2b, [dgate, dup], D_FF // 2, tile(512), "grad_w_gate_up")
    (dw_down,) = _matmul_tn(act, [dx2], D_MODEL, tile(512), "grad_w_down")
    d_o, dpg, dpu, dpv, dw_out, d_gg, d_lg, d_lb, dw_sp, db_sp = _mix_bwd(
        dx1, ycat, o_f, o_b, p, gla_norm_g, ln_g, ln_b, w_sp, b_sp_col, w_out_full, tile(256))
    dq_f, dk_f, dv_f, dlr_f, dwd_f, dbd_f = _gla_bwd(p, wd_pad_f, b_decay_f, st_f, d_o, False, tile(512))
    dq_b, dk_b, dv_b, dlr_b, dwd_b, dbd_b = _gla_bwd(p, wd_pad_b, b_decay_b, st_b, d_o, True, tile(512))
    grad_x, dp, d_g1 = _in_proj_bwd(xt, norm1_g, dx1, dq_f, dq_b, dk_f, dk_b, dv_f, dv_b, dpg, dpu, dpv,
                                    dlr_f, dlr_b, w_main, tile(256))
    (dw_main,) = _matmul_tn(hb, [dp], PROJ_PAD // 3, tile(512), "grad_w_in")
    dw_in = jnp.concatenate([dw_main[:, :1536], dw_main[:, LR_COL:LR_COL + 2 * LOWRANK], dw_main[:, 1536:LR_COL]],
                            axis=1)
    small_parts = [d_g1, dwd_f[:LOWRANK], dbd_f, dwd_b[LOWRANK:2 * LOWRANK], dbd_b, d_gg, d_lg, d_lb, dw_sp,
                   db_sp, d_g2, d_gf, loss_acc[0]]
    return grad_x, dw_in, dw_out, dw_gate, dw_up, dw_down, small_parts


def kernel(x, norm1_g, w_in,w_decay_f, b_decay_f, w_decay_b, b_decay_b, gla_norm_g, gmlp_ln_g, gmlp_ln_b, w_spatial, b_spatial, w_out, norm2_g, w_gate, w_up, w_down, final_norm_g, loss_target, m_norm1_g, m_w_in, m_w_decay_f, m_b_decay_f, m_w_decay_b, m_b_decay_b, m_gla_norm_g, m_gmlp_ln_g, m_gmlp_ln_b, m_w_spatial, m_b_spatial, m_w_out, m_norm2_g, m_w_gate, m_w_up, m_w_down, m_final_norm_g, v_norm1_g, v_w_in, v_w_decay_f, v_b_decay_f, v_w_decay_b, v_b_decay_b, v_gla_norm_g, v_gmlp_ln_g, v_gmlp_ln_b, v_w_spatial, v_b_spatial, v_w_out, v_norm2_g, v_w_gate, v_w_up, v_w_down, v_final_norm_g):
    t = x.shape[1]
    xt = x[0]
    target = loss_target[0]
    pos_x, pos_y, pos_c = _mesh_pos()
    my_pos = jnp.stack([2 * pos_x + pos_y, pos_c]).astype(jnp.int32)
    my_id = 4 * pos_x + 2 * pos_y + pos_c

    decay_shard = jnp.stack([w_decay_f[0], w_decay_b[0]])
    g_in, g_out, g_gate, g_up, g_down, g_decay = _all_gather_weights(
        [w_in[0], w_out[0], w_gate[0], w_up[0], w_down[0]], decay_shard)
    w_in_full = _unshard_cols(g_in)
    w_main = jnp.concatenate([w_in_full[:, :1536], w_in_full[:, 1568:], w_in_full[:, 1536:1568],
                              jnp.zeros((D_MODEL, PROJ_PAD - 2592), BF16)], axis=1)
    w_out_full = g_out.reshape(D_MODEL, D_MODEL)
    w_gate_full = _unshard_cols(g_gate)
    w_up_full = _unshard_cols(g_up)
    w_down_full = g_down.reshape(D_FF, D_MODEL)
    wd_f = _unshard_cols(g_decay[:, 0])
    wd_b = _unshard_cols(g_decay[:, 1])
    zeros = lambda n: jnp.zeros((n, KEY_W), F32)
    wd_pad_f = jnp.concatenate([wd_f, zeros(LANE - LOWRANK)], axis=0)
    wd_pad_b = jnp.concatenate([zeros(LOWRANK), wd_b, zeros(LANE - 2 * LOWRANK)], axis=0)
    b_sp_col = b_spatial[0][:, :, None]

    grad_x, dw_in, dw_out, dw_gate, dw_up, dw_down, small_parts = _forward_backward(
        xt, target, norm1_g, w_main, wd_pad_f, b_decay_f, wd_pad_b, b_decay_b, gla_norm_g, gmlp_ln_g, gmlp_ln_b,
        w_spatial[0], b_sp_col, w_out_full, norm2_g, w_gate_full, w_up_full, w_down_full, final_norm_g[None, :])

    big = [_shard_cols(dw_in), dw_out.reshape(4, 2, D_MODEL // N_DEV, D_MODEL), _shard_cols(dw_gate),
           _shard_cols(dw_up), dw_down.reshape(4, 2, D_FF // N_DEV, D_MODEL)]
    from_sibling = _exchange_with_sibling(big)
    names = ["w_in", "w_out", "w_gate", "w_up", "w_down"]
    rows = [256, 128, 256, 256, 352]
    sums = [_chip_sum(my_pos, g, s, tr, "chip_sum_" + n) for g, s, tr, n in zip(big, from_sibling, rows, names)]
    received = _exchange_with_chips([s[1] for s in sums])
    big_w = [w_in[0], w_out[0], w_gate[0], w_up[0], w_down[0]]
    big_m = [m_w_in[0], m_w_out[0], m_w_gate[0], m_w_up[0], m_w_down[0]]
    big_v = [v_w_in[0], v_w_out[0], v_w_gate[0], v_w_up[0], v_w_down[0]]
    big_out = {n: _adamw_shard(s[0], rc, w, m, v, tr, "adamw_" + n)
               for n, s, rc, w, m, v, tr in zip(names, sums, received, big_w, big_m, big_v, rows)}

    small_shapes = [(1, D_MODEL), (LOWRANK, KEY_W), (1, KEY_W), (LOWRANK, KEY_W), (1, KEY_W), (1, VAL_W),
                    (1, GMLP_W), (1, GMLP_W), (1, GMLP_GROUPS, GMLP_CHUNK, GMLP_CHUNK), (1, GMLP_GROUPS, GMLP_CHUNK),
                    (1, D_MODEL), (D_MODEL,), (LANE,)]
    reduced =_unpack_small(_all_reduce_small(_pack_small(small_parts)), small_shapes)
    loss = reduced[-1][0]
    col0 = my_id * (KEY_W // N_DEV)
    g_small = list(reduced[:-1])
    g_small[1] = lax.dynamic_slice_in_dim(g_small[1], col0, KEY_W // N_DEV, axis=1)[None]
    g_small[3] = lax.dynamic_slice_in_dim(g_small[3], col0, KEY_W // N_DEV, axis=1)[None]
    small_w = [norm1_g, w_decay_f, b_decay_f, w_decay_b, b_decay_b, gla_norm_g, gmlp_ln_g, gmlp_ln_b, w_spatial,
               b_spatial, norm2_g, final_norm_g]
    small_m = [m_norm1_g, m_w_decay_f, m_b_decay_f, m_w_decay_b, m_b_decay_b, m_gla_norm_g, m_gmlp_ln_g,
               m_gmlp_ln_b, m_w_spatial, m_b_spatial, m_norm2_g, m_final_norm_g]
    small_v = [v_norm1_g, v_w_decay_f, v_b_decay_f, v_w_decay_b, v_b_decay_b, v_gla_norm_g, v_gmlp_ln_g,
               v_gmlp_ln_b, v_w_spatial, v_b_spatial, v_norm2_g, v_final_norm_g]
    shapes = [w.shape for w in small_w]
    packed = _adamw_small(_pack_small(g_small), _pack_small(small_w), _pack_small(small_m), _pack_small(small_v))
    s_delta, s_m, s_v = (_unpack_small(a, shapes) for a in packed)
    s_grad = [g.reshape(s) for g, s in zip(g_small, shapes)]

    order = ["norm1_g", "w_in", "w_decay_f", "b_decay_f", "w_decay_b", "b_decay_b", "gla_norm_g", "gmlp_ln_g",
             "gmlp_ln_b", "w_spatial", "b_spatial", "w_out", "norm2_g", "w_gate", "w_up", "w_down", "final_norm_g"]
    small_names = [n for n in order if n not in big_out]
    small_out = {n: (s_grad[i], s_delta[i], s_m[i], s_v[i]) for i, n in enumerate(small_names)}
    outs = []
    for kind in range(4):
        for n in order:
            outs.append(big_out[n][kind][None] if n in big_out else small_out[n][kind])
    return (loss, grad_x[None], *outs)
```

```python
import functools
import math

import jax
import jax.numpy as jnp
from jax import lax
from jax.experimental import pallas as pl
from jax.experimental.pallas import tpu as pltpu

F32 = jnp.float32
BF16 = jnp.bfloat16

D_MODEL = 1024
GLA_HEADS = 4
GLA_DK = 64
GLA_DV = 128
KEY_W = GLA_HEADS * GLA_DK
VAL_W = GLA_HEADS * GLA_DV
LOWRANK = 16
GLA_TAU = 16.0
GLA_CHUNK = 64
GMLP_W = 512
GMLP_GROUPS = 4
GMLP_CHUNK = 128
D_FF = 2816
EPS = 1e-6
Q_SCALE = GLA_DK ** -0.5
PROJ_PAD = 2688
LR_COL = 2560
LANE = 128
N_DEV = 8

ADAM_LR = 0.001
ADAM_B1 = 0.9
ADAM_B2 = 0.999
ADAM_EPS = 1e-08
ADAM_WD = 0.01
ADAM_STEP = 10

VMEM_LIMIT = 56 * 1024 * 1024
MESH_ID = pl.DeviceIdType.MESH
INV_SQRT2 = 0.7071067811865476
INV_SQRT_2PI = 0.3989422804014327


def _params(n_axes=1):
    return pltpu.CompilerParams(dimension_semantics=("arbitrary",) * n_axes, vmem_limit_bytes=VMEM_LIMIT)


def _mm(a, b):
    return jnp.dot(a.astype(BF16), b.astype(BF16), preferred_element_type=F32)


def _mm_nt(a, b):
    return lax.dot_general(a.astype(BF16), b.astype(BF16), (((1,), (1,)), ((), ())), preferred_element_type=F32)


def _mm_tn(a, b):
    return lax.dot_general(a.astype(BF16), b.astype(BF16), (((0,), (0,)), ((), ())), preferred_element_type=F32)


def _const_spec(shape):
    nd = len(shape)
    return pl.BlockSpec(shape, lambda *_: (0,) * nd, pipeline_mode=pl.Buffered(1))


def _acc_spec(shape):
    nd = len(shape)
    return pl.BlockSpec(shape, lambda *_: (0,) * nd)


class _Comm:
    def __init__(self, inputs, in_specs, out_shape, out_specs, scratch_shapes, before, after):
        self.inputs, self.in_specs, self.out_shape, self.out_specs = inputs, in_specs, out_shape, out_specs
        self.scratch_shapes, self.before, self.after = scratch_shapes, before, after


def _fused_call(body, comms, *, name, grid, inputs, in_specs, out_specs, out_shape, scratch_shapes=()):
    n_in, n_out, n_scr = len(in_specs), len(out_specs), len(scratch_shapes)
    nsteps = math.prod(grid)
    sizes = [(len(c.inputs), len(c.out_shape), len(c.scratch_shapes)) for c in comms]

    def full_body(*refs):
        step = pl.program_id(0)
        for axis in range(1, len(grid)):
            step = step * grid[axis] + pl.program_id(axis)
        ins, rest = refs[:n_in], refs[n_in:]
        c_ins = []
        for ci, _, _ in sizes:
            c_ins.append(rest[:ci])
            rest = rest[ci:]
        outs, rest = rest[:n_out], rest[n_out:]
        c_outs = []
        for _, co, _ in sizes:
            c_outs.append(rest[:co])
            rest = rest[co:]
        scr, rest = rest[:n_scr], rest[n_scr:]
        c_scr = []
        for _, _, cs in sizes:
            c_scr.append(rest[:cs])
            rest = rest[cs:]
        for c, a, b, s in zip(comms, c_ins, c_outs, c_scr):
            c.before(step, nsteps, a, b, s)
        body(*ins, *outs, *scr)
        for c, a, b, s in zip(comms, c_ins, c_outs, c_scr):
            c.after(step, nsteps, a, b, s)

    results = pl.pallas_call(
        full_body, name=name, grid=grid,
        in_specs=list(in_specs) + [s for c in comms for s in c.in_specs],
        out_specs=tuple(out_specs) + tuple(s for c in comms for s in c.out_specs),
        out_shape=tuple(out_shape) + tuple(s for c in comms for s in c.out_shape),
        scratch_shapes=list(scratch_shapes) + [s for c in comms for s in c.scratch_shapes],
        compiler_params=_params(len(grid)),
    )(*inputs, *[a for c in comms for a in c.inputs])
    own, rest = results[:n_out], results[n_out:]
    comm_results = []
    for _, co, _ in sizes:
        comm_results.append(rest[:co])
        rest = rest[co:]
    return own, comm_results


def _gelu(x):
    return 0.5 * x * (1.0 + lax.erf(x * INV_SQRT2))


def _gelu_grad(x):
    return 0.5 * (1.0 + lax.erf(x * INV_SQRT2)) + x * jnp.exp(-0.5 * x * x) * INV_SQRT_2PI


def _silu_and_grad(x):
    s = jax.nn.sigmoid(x)
    return x * s, s * (1.0 + x * (1.0 - s))


def _in_proj(x, g1, w_main, tm, comms=()):
    t = x.shape[0]

    def body(x_ref, g_ref, w_ref, p_ref, h_ref):
        xv = x_ref[...]
        r = lax.rsqrt(jnp.mean(xv * xv, axis=-1, keepdims=True) + EPS)
        h = (xv * r * g_ref[...]).astype(BF16)
        h_ref[...] = h
        p_ref[...] = jnp.dot(h, w_ref[...], preferred_element_type=F32)

    return _fused_call(
        body, comms, name="in_proj", grid=(t // tm,), inputs=(x, g1, w_main),
        in_specs=[pl.BlockSpec((tm, D_MODEL), lambda i: (i, 0)), _const_spec((1, D_MODEL)),
                  _const_spec((D_MODEL, PROJ_PAD))],
        out_specs=(pl.BlockSpec((tm, PROJ_PAD), lambda i: (i, 0)), pl.BlockSpec((tm, D_MODEL), lambda i: (i, 0))),
        out_shape=(jax.ShapeDtypeStruct((t, PROJ_PAD), F32), jax.ShapeDtypeStruct((t, D_MODEL), BF16)))


def _tri(upper):
    r = lax.broadcasted_iota(jnp.int32, (GLA_CHUNK, GLA_CHUNK), 0)
    c = lax.broadcasted_iota(jnp.int32, (GLA_CHUNK, GLA_CHUNK), 1)
    return jnp.where((c >= r) if upper else (c <= r), 1.0, 0.0).astype(BF16)


def _tri_matmul(tri, a):
    a1 = a.astype(BF16)
    r1 = a - a1.astype(F32)
    a2 = r1.astype(BF16)
    a3 = (r1 - a2.astype(F32)).astype(BF16)
    dot = functools.partial(jnp.dot, preferred_element_type=F32)
    return dot(tri, a1) + dot(tri, a2) + dot(tri, a3)


def _gla_masks(rev):
    dk_bits, dv_bits = GLA_DK.bit_length() - 1, GLA_DV.bit_length() - 1
    key_head = lax.broadcasted_iota(jnp.int32, (GLA_CHUNK, KEY_W), 1) >> dk_bits
    val_head = lax.broadcasted_iota(jnp.int32, (GLA_CHUNK, VAL_W), 1) >> dv_bits
    t = lax.broadcasted_iota(jnp.int32, (GLA_HEADS * GLA_CHUNK, GLA_CHUNK), 0) & (GLA_CHUNK - 1)
    s = lax.broadcasted_iota(jnp.int32, (GLA_HEADS * GLA_CHUNK, GLA_CHUNK), 1)
    causal = (s >= t) if rev else (s <= t)
    blockdiag = (lax.broadcasted_iota(jnp.int32, (VAL_W, KEY_W), 0) >> dv_bits
                 == lax.broadcasted_iota(jnp.int32, (VAL_W, KEY_W), 1) >> dk_bits)
    return key_head, val_head, causal, blockdiag


def _stack_heads(a, head_of_lane):
    return jnp.concatenate([jnp.where(head_of_lane == h, a, 0.0) for h in range(GLA_HEADS)], axis=0)


def _chunk_terms(la_c, q_c, k_c, tri, rev):
    b = _tri_matmul(tri, la_c)
    bl = b[0:1] if rev else b[GLA_CHUNK - 1:GLA_CHUNK]
    eb = jnp.exp(b)
    enb = jnp.exp(-b)
    ee = jnp.exp(bl - b)
    return bl, eb, enb, ee, q_c * Q_SCALE * eb, k_c * enb, k_c * ee


def _log_decay(lr_ref, wd_ref, bd_ref):
    z = _mm(lr_ref[...], wd_ref[...]) + bd_ref[...]
    return z, jax.nn.log_sigmoid(z) * (1.0 / GLA_TAU)


def _gla_fwd(p, wd_pad, bd, rev, tg, comms=()):
    t = p.shape[0]
    nt = t // tg
    nc = tg // GLA_CHUNK
    tile = (lambda i: nt - 1 - i) if rev else (lambda i: i)

    def body(q_ref, k_ref, v_ref, lr_ref, wd_ref, bd_ref, o_ref, st_ref, state):
        @pl.when(pl.program_id(0) == 0)
        def _():
            state[...] = jnp.zeros_like(state)

        key_head, _, causal, blockdiag = _gla_masks(rev)
        tri = _tri(rev)
        _, la = _log_decay(lr_ref, wd_ref, bd_ref)
        for cc in range(nc):
            c = nc - 1 - cc if rev else cc
            rows = slice(c * GLA_CHUNK, (c + 1) * GLA_CHUNK)
            v_c = v_ref[rows, :].astype(BF16)
            bl, _, _, _, qd, kd, ke = _chunk_terms(la[rows], q_ref[rows, :], k_ref[rows, :], tri, rev)
            a_all = jnp.where(causal, _mm_nt(_stack_heads(qd, key_head), kd), 0.0)
            r = _mm(a_all, v_c)
            o_intra = jnp.concatenate(
                [r[h * GLA_CHUNK:(h + 1) * GLA_CHUNK, h * GLA_DV:(h + 1) * GLA_DV] for h in range(GLA_HEADS)], axis=1)
            st = state[...]
            o_ref[rows, :] = o_intra + _mm_nt(qd, st)
            st_ref[c] = st.astype(BF16)
            state[...] = st * jnp.exp(bl) + jnp.where(blockdiag, _mm_tn(v_c, ke), 0.0)

    return _fused_call(
        body, comms, name="gla_fwd_rev" if rev else "gla_fwd", grid=(nt,), inputs=(p, p, p, p, wd_pad, bd),
        in_specs=[pl.BlockSpec((tg, KEY_W), lambda i: (tile(i), 0)),
                  pl.BlockSpec((tg, KEY_W), lambda i: (tile(i), 1)),
                  pl.BlockSpec((tg, VAL_W), lambda i: (tile(i), 1)),
                  pl.BlockSpec((tg, LANE), lambda i: (tile(i), LR_COL // LANE)),
                  _const_spec((LANE, KEY_W)), _const_spec((1, KEY_W))],
        out_specs=(pl.BlockSpec((tg, VAL_W), lambda i: (tile(i), 0)),
                   pl.BlockSpec((nc, VAL_W, KEY_W), lambda i: (tile(i), 0, 0))),
        out_shape=(jax.ShapeDtypeStruct((t, VAL_W), F32),
                   jax.ShapeDtypeStruct((t // GLA_CHUNK, VAL_W, KEY_W), BF16)),
        scratch_shapes=[pltpu.VMEM((VAL_W, KEY_W), F32)])


def _gla_bwd(p, wd_pad, bd, states, d_o, rev, tg, comms=()):
    t = p.shape[0]
    nt = t // tg
    nc = tg // GLA_CHUNK
    tile = (lambda i: i) if rev else (lambda i: nt - 1 - i)

    def body(q_ref, k_ref, v_ref, lr_ref, wd_ref, bd_ref, st_ref, do_ref,
             dq_ref, dk_ref, dv_ref, dlr_ref, dwd_ref, dbd_ref, dstate, dz_scr):
        @pl.when(pl.program_id(0) == 0)
        def _():
            dstate[...] = jnp.zeros_like(dstate)
            dwd_ref[...] = jnp.zeros_like(dwd_ref)
            dbd_ref[...] = jnp.zeros_like(dbd_ref)

        key_head, val_head, causal, blockdiag = _gla_masks(rev)
        tri = _tri(rev)
        tri_t = _tri(not rev)
        z, la = _log_decay(lr_ref, wd_ref, bd_ref)
        dlog = jax.nn.sigmoid(-z) * (1.0 / GLA_TAU)
        for cc in range(nc):
            c = cc if rev else nc - 1 - cc
            rows = slice(c * GLA_CHUNK, (c + 1) * GLA_CHUNK)
            v_c = v_ref[rows, :].astype(BF16)
            do_c = do_ref[rows, :]
            bl, eb, enb, ee, qd, kd, ke = _chunk_terms(la[rows], q_ref[rows, :], k_ref[rows, :], tri, rev)
            qd_stack = _stack_heads(qd, key_head)
            do_stack = _stack_heads(do_c, val_head)
            a_all = jnp.where(causal, _mm_nt(qd_stack, kd), 0.0)
            da_all = jnp.where(causal, _mm_nt(do_stack, v_c), 0.0)
            dst = dstate[...]
            st_prev = st_ref[c]
            dv_ref[rows, :] = _mm_tn(a_all, do_stack) + _mm_nt(ke, dst)
            r2 = _mm(da_all, kd)
            dqd = _mm(do_c, st_prev)
            for h in range(GLA_HEADS):
                dqd = dqd + jnp.where(key_head == h, r2[h * GLA_CHUNK:(h + 1) * GLA_CHUNK, :], 0.0)
            dkd = _mm_tn(da_all, qd_stack)
            dke = _mm(v_c, dst)
            ebl = jnp.exp(bl)
            dbl = (jnp.sum(dst * st_prev.astype(F32), axis=0, keepdims=True) * ebl
                   + jnp.sum(dke * ke, axis=0, keepdims=True))
            dstate[...] = dst * ebl + jnp.where(blockdiag, _mm_tn(do_c, qd), 0.0)
            dq_ref[rows, :] = dqd * eb * Q_SCALE
            dk_ref[rows, :] = dkd * enb + dke * ee
            db = dqd * qd - dkd * kd - dke * ke
            dz_scr[rows, :] = (_tri_matmul(tri_t, db) + dbl) * dlog[rows]
        dz = dz_scr[...]
        dlr_ref[...] = _mm_nt(dz, wd_ref[...])
        dwd_ref[...] += _mm_tn(lr_ref[...], dz)
        dbd_ref[...] += jnp.sum(dz, axis=0, keepdims=True)

    return _fused_call(
        body, comms, name="gla_bwd_rev" if rev else "gla_bwd", grid=(nt,),
        inputs=(p, p, p, p, wd_pad, bd, states, d_o),
        in_specs=[pl.BlockSpec((tg, KEY_W), lambda i: (tile(i), 0)),
                  pl.BlockSpec((tg, KEY_W), lambda i: (tile(i), 1)),
                  pl.BlockSpec((tg, VAL_W), lambda i: (tile(i), 1)),
                  pl.BlockSpec((tg, LANE), lambda i: (tile(i), LR_COL // LANE)),
                  _const_spec((LANE, KEY_W)), _const_spec((1, KEY_W)),
                  pl.BlockSpec((nc, VAL_W, KEY_W), lambda i: (tile(i), 0, 0)),
                  pl.BlockSpec((tg, VAL_W), lambda i: (tile(i), 0))],
        out_specs=(pl.BlockSpec((tg, KEY_W), lambda i: (tile(i), 0)),
                   pl.BlockSpec((tg, KEY_W), lambda i: (tile(i), 0)),
                   pl.BlockSpec((tg, VAL_W), lambda i: (tile(i), 0)),
                   pl.BlockSpec((tg, LANE), lambda i: (tile(i), 0)),
                   _acc_spec((LANE, KEY_W)), _acc_spec((1, KEY_W))),
        out_shape=(jax.ShapeDtypeStruct((t, KEY_W), F32), jax.ShapeDtypeStruct((t, KEY_W), F32),
                   jax.ShapeDtypeStruct((t, VAL_W), F32), jax.ShapeDtypeStruct((t, LANE), F32),
                   jax.ShapeDtypeStruct((LANE, KEY_W), F32), jax.ShapeDtypeStruct((1, KEY_W), F32)),
        scratch_shapes=[pltpu.VMEM((VAL_W, KEY_W), F32), pltpu.VMEM((tg, KEY_W), F32)])


def _head_rms(o):
    parts, scales = [], []
    for h in range(GLA_HEADS):
        oh = o[:, h * GLA_DV:(h + 1) * GLA_DV]
        r = lax.rsqrt(jnp.mean(oh * oh, axis=-1, keepdims=True) + EPS)
        parts.append(oh * r)
        scales.append(jnp.broadcast_to(r, oh.shape))
    return jnp.concatenate(parts, axis=1), jnp.concatenate(scales, axis=1)


def _layernorm_stats(zv):
    mu = jnp.mean(zv, axis=-1, keepdims=True)
    xc = zv - mu
    rs = lax.rsqrt(jnp.mean(xc * xc, axis=-1, keepdims=True) + EPS)
    return xc * rs, rs


def _mix_fwd(x, o_f, o_b, p, gla_g, ln_g, ln_b, w_sp, b_sp, w_out, tm):
    t = x.shape[0]
    nch = tm // GMLP_CHUNK

    def body(x_ref, of_ref, ob_ref, pg_ref, pu_ref, pv_ref, gg_ref, lg_ref, lb_ref, ws_ref, bs_ref, wo_ref,
             x1_ref, y_ref, s_scr):
        on, _ = _head_rms(of_ref[...] + ob_ref[...])
        pg = pg_ref[...]
        y_a = on * gg_ref[...] * (pg * jax.nn.sigmoid(pg))
        zu = _gelu(pu_ref[...])
        vhat, _ = _layernorm_stats(_gelu(pv_ref[...]))
        vln = (vhat * lg_ref[...] + lb_ref[...]).astype(BF16)
        for g in range(GMLP_GROUPS):
            w_g = ws_ref[g].astype(BF16)
            b_g = bs_ref[g]
            cols = slice(g * LANE, (g + 1) * LANE)
            for n in range(nch):
                rows = slice(n * GMLP_CHUNK, (n + 1) * GMLP_CHUNK)
                s_scr[rows, cols] = jnp.dot(w_g, vln[rows, cols], preferred_element_type=F32) + b_g
        ycat = jnp.concatenate([y_a, zu * s_scr[...]], axis=1).astype(BF16)
        y_ref[...] = ycat
        x1_ref[...] = x_ref[...] + jnp.dot(ycat, wo_ref[...], preferred_element_type=F32)

    half = lambda j: pl.BlockSpec((tm, VAL_W), lambda i: (i, j))
    return pl.pallas_call(
        body, name="mix_fwd", grid=(t // tm,),
        in_specs=[pl.BlockSpec((tm, D_MODEL), lambda i: (i, 0)), half(0), half(0), half(2), half(3), half(4),
                  _const_spec((1, VAL_W)), _const_spec((1, GMLP_W)), _const_spec((1, GMLP_W)),
                  _const_spec((GMLP_GROUPS, GMLP_CHUNK, GMLP_CHUNK)), _const_spec((GMLP_GROUPS, GMLP_CHUNK, 1)),
                  _const_spec((D_MODEL, D_MODEL))],
        out_specs=(pl.BlockSpec((tm, D_MODEL), lambda i: (i, 0)), pl.BlockSpec((tm, D_MODEL), lambda i: (i, 0))),
        out_shape=(jax.ShapeDtypeStruct((t, D_MODEL), F32), jax.ShapeDtypeStruct((t, D_MODEL), BF16)),
        scratch_shapes=[pltpu.VMEM((tm, GMLP_W), F32)],
        compiler_params=_params(),
    )(x, o_f, o_b, p, p, p, gla_g, ln_g, ln_b, w_sp, b_sp, w_out)


def _mix_bwd(dx1, ycat, o_f, o_b, p, gla_g, ln_g, ln_b, w_sp, b_sp, w_out, tm, comms=()):
    t = dx1.shape[0]
    nch = tm // GMLP_CHUNK

    def body(dx1_ref, y_ref, of_ref, ob_ref, pg_ref, pu_ref, pv_ref, gg_ref, lg_ref, lb_ref, ws_ref, bs_ref, wo_ref,
             do_ref, dpg_ref, dpu_ref, dpv_ref, dwo_ref, dgg_ref, dlg_ref, dlb_ref, dws_ref, dbs_ref,
             s_scr, dvln_scr):
        @pl.when(pl.program_id(0) == 0)
        def _():
            for ref in (dwo_ref, dgg_ref, dlg_ref, dlb_ref, dws_ref, dbs_ref):
                ref[...] = jnp.zeros_like(ref)

        dx1 = dx1_ref[...].astype(BF16)
        dycat = _mm_nt(dx1, wo_ref[...])
        dwo_ref[...] += _mm_tn(y_ref[...], dx1)
        dy_a = dycat[:, :VAL_W]
        dy_b = dycat[:, VAL_W:]
        on, r = _head_rms(of_ref[...] + ob_ref[...])
        pg = pg_ref[...]
        sil, dsil = _silu_and_grad(pg)
        gg = gg_ref[...]
        dgg_ref[...] += jnp.sum(dy_a * sil * on, axis=0, keepdims=True)
        don = dy_a * sil * gg
        prod = don * on
        means = jnp.concatenate(
            [jnp.broadcast_to(jnp.mean(prod[:, h * GLA_DV:(h + 1) * GLA_DV], axis=-1, keepdims=True),
                              (tm, GLA_DV)) for h in range(GLA_HEADS)], axis=1)
        do_ref[...] = r * (don - on * means)
        dpg_ref[...] = dy_a * on * gg * dsil
        pu = pu_ref[...]
        pv = pv_ref[...]
        zu = _gelu(pu)
        vhat, rs = _layernorm_stats(_gelu(pv))
        lg = lg_ref[...]
        vln = (vhat * lg + lb_ref[...]).astype(BF16)
        ds32 = dy_b * zu
        ds = ds32.astype(BF16)
        for g in range(GMLP_GROUPS):
            w_g = ws_ref[g].astype(BF16)
            b_g = bs_ref[g]
            cols = slice(g * LANE, (g + 1) * LANE)
            dw_g = jnp.zeros((GMLP_CHUNK, GMLP_CHUNK), F32)
            db_g = jnp.zeros((GMLP_CHUNK, 1), F32)
            for n in range(nch):
                rows = slice(n * GMLP_CHUNK, (n + 1) * GMLP_CHUNK)
                v_blk = vln[rows, cols]
                ds_blk = ds[rows, cols]
                s_scr[rows, cols] = jnp.dot(w_g, v_blk, preferred_element_type=F32) + b_g
                dw_g = dw_g + _mm_nt(ds_blk, v_blk)
                db_g = db_g + jnp.sum(ds32[rows, cols], axis=-1, keepdims=True)
                dvln_scr[rows, cols] = _mm_tn(w_g, ds_blk)
            dws_ref[g] += dw_g
            dbs_ref[g] += db_g
        dpu_ref[...] = dy_b * s_scr[...] * _gelu_grad(pu)
        dvln = dvln_scr[...]
        dlg_ref[...] += jnp.sum(dvln * vhat, axis=0, keepdims=True)
        dlb_ref[...] += jnp.sum(dvln, axis=0, keepdims=True)
        dvhat = dvln * lg
        dzv = rs * (dvhat - jnp.mean(dvhat, axis=-1, keepdims=True)
                    - vhat * jnp.mean(dvhat * vhat, axis=-1, keepdims=True))
        dpv_ref[...] = dzv * _gelu_grad(pv)

    half = lambda j: pl.BlockSpec((tm, VAL_W), lambda i: (i, j))
    full = pl.BlockSpec((tm, D_MODEL), lambda i: (i, 0))
    sp_shape = (GMLP_GROUPS, GMLP_CHUNK, GMLP_CHUNK)
    bs_shape = (GMLP_GROUPS, GMLP_CHUNK, 1)
    return _fused_call(
        body, comms, name="mix_bwd", grid=(t // tm,),
        inputs=(dx1, ycat, o_f, o_b, p, p, p, gla_g, ln_g, ln_b, w_sp, b_sp, w_out),
        in_specs=[full, full, half(0), half(0), half(2), half(3), half(4),
                  _const_spec((1, VAL_W)), _const_spec((1, GMLP_W)), _const_spec((1, GMLP_W)),
                  _const_spec(sp_shape), _const_spec(bs_shape), _const_spec((D_MODEL, D_MODEL))],
        out_specs=(half(0), half(0), half(0), half(0), _acc_spec((D_MODEL, D_MODEL)), _acc_spec((1, VAL_W)),
                   _acc_spec((1, GMLP_W)), _acc_spec((1, GMLP_W)), _acc_spec(sp_shape), _acc_spec(bs_shape)),
        out_shape=(jax.ShapeDtypeStruct((t, VAL_W), F32),) * 4 + (
            jax.ShapeDtypeStruct((D_MODEL, D_MODEL), F32), jax.ShapeDtypeStruct((1, VAL_W), F32),
            jax.ShapeDtypeStruct((1, GMLP_W), F32), jax.ShapeDtypeStruct((1, GMLP_W), F32),
            jax.ShapeDtypeStruct(sp_shape, F32), jax.ShapeDtypeStruct(bs_shape, F32)),
        scratch_shapes=[pltpu.VMEM((tm, GMLP_W), F32), pltpu.VMEM((tm, GMLP_W), F32)])


def _rms_bwd(dy_scaled, xn, r):
    return r * (dy_scaled - xn * jnp.mean(dy_scaled * xn, axis=-1, keepdims=True))


def _ffn(x1, target, g2, gf, w_gate, w_up, w_down, tm):
    t = x1.shape[0]

    def body(x1_ref, tg_ref, g2_ref, gf_ref, wg_ref, wu_ref, wd_ref,
             dx1_ref, h2_ref, dgate_ref, dup_ref, act_ref, dx2_ref, loss_ref, dgf_ref, dg2_ref):
        @pl.when(pl.program_id(0) == 0)
        def _():
            for ref in (loss_ref, dgf_ref, dg2_ref):
                ref[...] = jnp.zeros_like(ref)

        x1v = x1_ref[...]
        g2v = g2_ref[...]
        gfv = gf_ref[...]
        r2 = lax.rsqrt(jnp.mean(x1v * x1v, axis=-1, keepdims=True) + EPS)
        xn1 = x1v * r2
        h2 = (xn1 * g2v).astype(BF16)
        h2_ref[...] = h2
        gate = jnp.dot(h2, wg_ref[...], preferred_element_type=F32)
        up = jnp.dot(h2, wu_ref[...], preferred_element_type=F32)
        sil, dsil = _silu_and_grad(gate)
        act = (sil * up).astype(BF16)
        act_ref[...] = act
        x2 = x1v + jnp.dot(act, wd_ref[...], preferred_element_type=F32)
        rf = lax.rsqrt(jnp.mean(x2 * x2, axis=-1, keepdims=True) + EPS)
        xn2 = x2 * rf
        err = xn2 * gfv - tg_ref[...]
        loss_ref[...] += 0.5 * jnp.sum(jnp.mean(err * err, axis=-1, keepdims=True))
        dy = err * (1.0 / D_MODEL)
        dgf_ref[...] += jnp.sum(dy * xn2, axis=0, keepdims=True)
        dx2 = _rms_bwd(dy * gfv, xn2, rf)
        dx2b = dx2.astype(BF16)
        dx2_ref[...] = dx2b
        dact = _mm_nt(dx2b, wd_ref[...])
        dgate = (dact * up * dsil).astype(BF16)
        dup = (dact * sil).astype(BF16)
        dgate_ref[...] = dgate
        dup_ref[...] = dup
        dh2 = _mm_nt(dgate, wg_ref[...]) + _mm_nt(dup, wu_ref[...])
        dg2_ref[...] += jnp.sum(dh2 * xn1, axis=0, keepdims=True)
        dx1_ref[...] = dx2 + _rms_bwd(dh2 * g2v, xn1, r2)

    row = lambda w: pl.BlockSpec((tm, w), lambda i: (i, 0))
    return pl.pallas_call(
        body, name="ffn_fwd_bwd", grid=(t // tm,),
        in_specs=[row(D_MODEL), row(D_MODEL), _const_spec((1, D_MODEL)), _const_spec((1, D_MODEL)),
                  _const_spec((D_MODEL, D_FF)), _const_spec((D_MODEL, D_FF)), _const_spec((D_FF, D_MODEL))],
        out_specs=(row(D_MODEL), row(D_MODEL), row(D_FF), row(D_FF), row(D_FF), row(D_MODEL),
                   _acc_spec((8, LANE)), _acc_spec((1, D_MODEL)), _acc_spec((1, D_MODEL))),
        out_shape=(jax.ShapeDtypeStruct((t, D_MODEL), F32), jax.ShapeDtypeStruct((t, D_MODEL), BF16),
                   jax.ShapeDtypeStruct((t, D_FF), BF16), jax.ShapeDtypeStruct((t, D_FF), BF16),
                   jax.ShapeDtypeStruct((t, D_FF), BF16), jax.ShapeDtypeStruct((t, D_MODEL), BF16),
                   jax.ShapeDtypeStruct((8, LANE), F32), jax.ShapeDtypeStruct((1, D_MODEL), F32),
                   jax.ShapeDtypeStruct((1, D_MODEL), F32)),
        compiler_params=_params(),
    )(x1, target, g2, gf, w_gate, w_up, w_down)


def _matmul_tn(a, bs, tn, tk, name, comms=()):
    t, m = a.shape
    n = bs[0].shape[1]
    nb = len(bs)

    def body(*refs):
        a_ref, b_refs, o_refs = refs[0], refs[1:1 + nb], refs[1 + nb:]

        @pl.when(pl.program_id(1) == 0)
        def _():
            for o_ref in o_refs:
                o_ref[...] = jnp.zeros_like(o_ref)

        a_t = a_ref[...]
        for b_ref, o_ref in zip(b_refs, o_refs):
            o_ref[...] += _mm_tn(a_t, b_ref[...])

    return _fused_call(
        body, comms, name=name, grid=(n // tn, t // tk), inputs=(a, *bs),
        in_specs=[pl.BlockSpec((tk, m), lambda j, k: (k, 0))] + [pl.BlockSpec((tk, tn), lambda j, k: (k, j))] * nb,
        out_specs=tuple(pl.BlockSpec((m, tn), lambda j, k: (0, j)) for _ in range(nb)),
        out_shape=tuple(jax.ShapeDtypeStruct((m, n), F32) for _ in range(nb)))


def _in_proj_bwd(x, g1, dx1, dq_f, dq_b, dk_f, dk_b, dv_f, dv_b, dpg, dpu, dpv, dlr_f, dlr_b, w_main, tm):
    t = x.shape[0]

    def body(x_ref, g_ref, dx1_ref, dqf, dqb, dkf, dkb, dvf, dvb, dg, du, dv, dlf, dlb, w_ref,
             dx_ref, dp_ref, dg1_ref):
        @pl.when(pl.program_id(0) == 0)
        def _():
            dg1_ref[...] = jnp.zeros_like(dg1_ref)

        dp = jnp.concatenate([dqf[...] + dqb[...], dkf[...] + dkb[...], dvf[...] + dvb[...],
                              dg[...], du[...], dv[...], dlf[...] + dlb[...]], axis=1).astype(BF16)
        dp_ref[...] = dp
        dh = _mm_nt(dp, w_ref[...])
        xv = x_ref[...]
        r = lax.rsqrt(jnp.mean(xv * xv, axis=-1, keepdims=True) + EPS)
        xn = xv * r
        dg1_ref[...] += jnp.sum(dh * xn, axis=0, keepdims=True)
        dx_ref[...] = dx1_ref[...] + _rms_bwd(dh * g_ref[...], xn, r)

    row = lambda w: pl.BlockSpec((tm, w), lambda i: (i, 0))
    return pl.pallas_call(
        body, name="in_proj_bwd", grid=(t // tm,),
        in_specs=[row(D_MODEL), _const_spec((1, D_MODEL)), row(D_MODEL), row(KEY_W), row(KEY_W), row(KEY_W),
                  row(KEY_W), row(VAL_W), row(VAL_W), row(VAL_W), row(VAL_W), row(VAL_W), row(LANE), row(LANE),
                  _const_spec((D_MODEL, PROJ_PAD))],
        out_specs=(row(D_MODEL), row(PROJ_PAD), _acc_spec((1, D_MODEL))),
        out_shape=(jax.ShapeDtypeStruct((t, D_MODEL), F32), jax.ShapeDtypeStruct((t, PROJ_PAD), BF16),
                   jax.ShapeDtypeStruct((1, D_MODEL), F32)),
        compiler_params=_params(),
    )(x, g1, dx1, dq_f, dq_b, dk_f, dk_b, dv_f, dv_b, dpg, dpu, dpv, dlr_f, dlr_b, w_main)


def _adamw(w, g, m, v):
    m_new = ADAM_B1 * m + (1.0 - ADAM_B1) * g
    v_new = ADAM_B2 * v + (1.0 - ADAM_B2) * (g * g)
    m_hat = m_new / (1.0 - ADAM_B1 ** ADAM_STEP)
    v_hat = v_new / (1.0 - ADAM_B2 ** ADAM_STEP)
    delta = -ADAM_LR * (m_hat / (jnp.sqrt(v_hat) + ADAM_EPS) + ADAM_WD * w)
    return delta, m_new, v_new


def _adamw_shard(own, recv, w, m, v, tr, name):
    r, c = w.shape

    def body(own_ref, recv_ref, w_ref, m_ref, v_ref, g_ref, d_ref, nm_ref, nv_ref):
        g = own_ref[...]
        for k in range(3):
            g = g + recv_ref[k].astype(F32)
        g_ref[...] = g
        d_ref[...], nm_ref[...], nv_ref[...] = _adamw(w_ref[...], g, m_ref[...], v_ref[...])

    row = pl.BlockSpec((tr, c), lambda i: (i, 0))
    return pl.pallas_call(
        body, name=name, grid=(r // tr,),
        in_specs=[row, pl.BlockSpec((3, tr, c), lambda i: (0, i, 0)), row, row, row],
        out_specs=(row,) * 4, out_shape=(jax.ShapeDtypeStruct((r, c), F32),) * 4,
        compiler_params=_params(),
    )(own, recv, w, m, v)


def _adamw_small(g, w, m, v):
    def body(g_ref, w_ref, m_ref, v_ref, d_ref, nm_ref, nv_ref):
        d_ref[...], nm_ref[...], nv_ref[...] = _adamw(w_ref[...], g_ref[...], m_ref[...], v_ref[...])

    return pl.pallas_call(body, name="adamw_small", out_shape=(jax.ShapeDtypeStruct(g.shape, F32),) * 3,
                          compiler_params=pltpu.CompilerParams(vmem_limit_bytes=VMEM_LIMIT))(g, w, m, v)


def _mesh_pos():
    return lax.axis_index("x"), lax.axis_index("y"), lax.axis_index("c")


def _other_chips(x, y):
    return [(x, 1 - y), (1 - x, y), (1 - x, 1 - y)]


_VMEM_WHOLE = pl.BlockSpec(memory_space=pltpu.VMEM)
_HBM_WHOLE = pl.BlockSpec(memory_space=pl.ANY)


def _gather_comm(shards, cast, mid=(1, 2)):
    na = len(shards)
    staged = [a for a in range(na) if cast[a]]

    def phases(in_refs, out_refs, scr):
        stage = dict(zip(staged, scr[:len(staged)]))
        send_sems, recv_sems, local_sems = scr[len(staged):]
        x, y, c = _mesh_pos()
        me, sibling = (x, y, c), (x, y, 1 - c)
        chips = _other_chips(x, y)
        srcs = [stage[a] if cast[a] else in_refs[a] for a in range(na)]

        def rows(a, pos):
            px, py, pc = pos
            return out_refs[a].at[4 * px + 2 * py + pc]

        def copy(a, k, block, to, src=None):
            return pltpu.make_async_remote_copy(
                src_ref=rows(a, block) if src is None else src, dst_ref=rows(a, block),
                send_sem=send_sems.at[a, k], recv_sem=recv_sems.at[a, k], device_id=to, device_id_type=MESH_ID)

        mine = [pltpu.make_async_copy(srcs[a], rows(a, me), local_sems.at[a]) for a in range(na)]
        first = []
        for a in range(na):
            first.append(copy(a, 0, me, sibling, src=srcs[a]))
            first += [copy(a, 1 + j, me, (*chip, c), src=srcs[a]) for j, chip in enumerate(chips)]
        passed = [copy(a, 4 + j, (*chip, c), sibling) for j, chip in enumerate(chips) for a in range(na)]

        def start():
            for a in staged:
                stage[a][...] = in_refs[a][...].astype(BF16)
            for cp in mine + first:
                cp.start()

        def forward():
            i = 0
            for j, chip in enumerate(chips):
                for a in range(na):
                    copy(a, 1 + j, (*chip, c), me).wait_recv()
                    passed[i].start()
                    i += 1

        def finish():
            for a in range(na):
                copy(a, 0, sibling, me).wait_recv()
                for j, chip in enumerate(chips):
                    copy(a, 4 + j, (*chip, 1 - c), me).wait_recv()
            for cp in first + passed:
                cp.wait_send()
            for cp in mine:
                cp.wait()

        return start, forward, finish

    def before(step, nsteps, in_refs, out_refs, scr):
        start, forward, _ = phases(in_refs, out_refs, scr)
        pl.when(step == 0)(start)
        pl.when(step == nsteps * mid[0] // mid[1])(forward)

    def after(step, nsteps, in_refs, out_refs, scr):
        pl.when(step == nsteps - 1)(phases(in_refs, out_refs, scr)[2])

    return _Comm(
        inputs=list(shards), in_specs=[_VMEM_WHOLE] * na,
        out_shape=[jax.ShapeDtypeStruct((N_DEV,) + s.shape, BF16 if cast[a] else s.dtype)
                   for a, s in enumerate(shards)],
        out_specs=[_HBM_WHOLE] * na,
        scratch_shapes=[pltpu.VMEM(shards[a].shape, BF16) for a in staged] + [
            pltpu.SemaphoreType.DMA((na, 7)), pltpu.SemaphoreType.DMA((na, 7)), pltpu.SemaphoreType.DMA((na,))],
        before=before, after=after)


def _exchange_comm(arrays, out_shape, make_copies):
    na = len(arrays)

    def copies(in_refs, out_refs, scr):
        return make_copies(in_refs, out_refs, *scr)

    def before(step, nsteps, in_refs, out_refs, scr):
        @pl.when(step == 0)
        def _():
            for cp in copies(in_refs, out_refs, scr):
                cp.start()

    def after(step, nsteps, in_refs, out_refs, scr):
        @pl.when(step == nsteps - 1)
        def _():
            for cp in copies(in_refs, out_refs, scr):
                cp.wait()

    return _Comm(inputs=list(arrays), in_specs=[_HBM_WHOLE] * na, out_shape=list(out_shape),
                 out_specs=[_HBM_WHOLE] * na,
                 scratch_shapes=[pltpu.SemaphoreType.DMA((na, 3)), pltpu.SemaphoreType.DMA((na, 3))],
                 before=before, after=after)


def _sibling_exchange_comm(grads):
    def make_copies(in_refs, out_refs, send_sems, recv_sems):
        x, y, c = _mesh_pos()
        return [pltpu.make_async_remote_copy(
            src_ref=in_refs[a].at[:, pl.ds(1 - c, 1)], dst_ref=out_refs[a], send_sem=send_sems.at[a, 0],
            recv_sem=recv_sems.at[a, 0], device_id=(x, y, 1 - c), device_id_type=MESH_ID)
            for a in range(len(grads))]

    return _exchange_comm(grads, [jax.ShapeDtypeStruct((4, 1) + g.shape[2:], F32) for g in grads], make_copies)


def _chips_exchange_comm(partials):
    def make_copies(in_refs, out_refs, send_sems, recv_sems):
        x, y, c = _mesh_pos()
        return [pltpu.make_async_remote_copy(
            src_ref=in_refs[a].at[j], dst_ref=out_refs[a].at[j], send_sem=send_sems.at[a, j],
            recv_sem=recv_sems.at[a, j], device_id=(*chip, c), device_id_type=MESH_ID)
            for a in range(len(partials)) for j, chip in enumerate(_other_chips(x, y))]

    return _exchange_comm(partials, [jax.ShapeDtypeStruct(g.shape, BF16) for g in partials], make_copies)


def _comm_only(comms, name):
    return _fused_call(lambda: None, comms, name=name, grid=(1,), inputs=(), in_specs=[], out_specs=(),
                       out_shape=())[1]


def _chip_sum(my_pos, mine, from_sibling, tr, name):
    _, _, r, c = mine.shape

    def body(pos_ref, a_ref, b_ref, own_ref, out_ref):
        s = a_ref[0, 0] + b_ref[0, 0]

        @pl.when(pl.program_id(1) == 0)
        def _():
            own_ref[...] = s

        @pl.when(pl.program_id(1) > 0)
        def _():
            out_ref[0] = s.astype(BF16)

    grid_spec = pltpu.PrefetchScalarGridSpec(
        num_scalar_prefetch=1, grid=(r // tr, 4),
        in_specs=[pl.BlockSpec((1, 1, tr, c), lambda i, k, pos: (pos[0] ^ k, pos[1], i, 0)),
                  pl.BlockSpec((1, 1, tr, c), lambda i, k, pos: (pos[0] ^ k, 0, i, 0))],
        out_specs=(pl.BlockSpec((tr, c), lambda i, k, pos: (i, 0)),
                   pl.BlockSpec((1, tr, c), lambda i, k, pos: (jnp.maximum(k - 1, 0), i, 0))))
    return pl.pallas_call(
        body, name=name, grid_spec=grid_spec,
        out_shape=(jax.ShapeDtypeStruct((r, c), F32), jax.ShapeDtypeStruct((3, r, c), BF16)),
        compiler_params=_params(2),
    )(my_pos, mine, from_sibling)


def _all_reduce_small_comm(part):
    def copies(in_ref, gathered, send_sems, recv_sems):
        x, y, c = _mesh_pos()
        my_id = 4 * x + 2 * y + c
        return my_id, [pltpu.make_async_remote_copy(
            src_ref=in_ref, dst_ref=gathered.at[my_id], send_sem=send_sems.at[k - 1],
            recv_sem=recv_sems.at[k - 1], device_id=(x ^ (k >> 2), y ^ ((k >> 1) & 1), c ^ (k & 1)),
            device_id_type=MESH_ID) for k in range(1, N_DEV)]

    def before(step, nsteps, in_refs, out_refs, scr):
        @pl.when(step == 0)
        def _():
            for cp in copies(in_refs[0], *scr)[1]:
                cp.start()

    def after(step, nsteps, in_refs, out_refs, scr):
        @pl.when(step == nsteps - 1)
        def _():
            gathered = scr[0]
            my_id, cps = copies(in_refs[0], *scr)
            gathered[my_id] = in_refs[0][...]
            for cp in cps:
                cp.wait()
            acc = gathered[0]
            for d in range(1, N_DEV):
                acc = acc + gathered[d]
            out_refs[0][...] = acc

    return _Comm(inputs=[part], in_specs=[_VMEM_WHOLE], out_shape=[jax.ShapeDtypeStruct(part.shape, F32)],
                 out_specs=[_VMEM_WHOLE],
                 scratch_shapes=[pltpu.VMEM((N_DEV,) + part.shape, F32), pltpu.SemaphoreType.DMA((N_DEV - 1,)),
                                 pltpu.SemaphoreType.DMA((N_DEV - 1,))],
                 before=before, after=after)


def _unshard_cols(g):
    return jnp.transpose(g, (1, 0, 2)).reshape(g.shape[1], N_DEV * g.shape[2])


def _shard_cols(w):
    r, n = w.shape
    return jnp.transpose(w.reshape(r, N_DEV, n // N_DEV), (1, 0, 2)).reshape(4, 2, r, n // N_DEV)


def _pack_small(parts):
    flat = jnp.concatenate([a.reshape(-1) for a in parts])
    pad = (-flat.shape[0]) % (8 * LANE)
    return jnp.pad(flat, (0, pad)).reshape(-1, LANE)


def _unpack_small(packed, shapes):
    flat = packed.reshape(-1)
    out, off = [], 0
    for s in shapes:
        n = math.prod(s)
        out.append(flat[off:off + n].reshape(s))
        off += n
    return out


def _main_proj_weight(w_in_full):
    return jnp.concatenate([w_in_full[:, :1536], w_in_full[:, 1568:], w_in_full[:, 1536:1568],
                            jnp.zeros((D_MODEL, PROJ_PAD - 2592), w_in_full.dtype)], axis=1)


def _padded_decay_weights(wd_f, wd_b):
    zeros = lambda n: jnp.zeros((n, KEY_W), F32)
    return (jnp.concatenate([wd_f, zeros(LANE - LOWRANK)], axis=0),
            jnp.concatenate([zeros(LOWRANK), wd_b, zeros(LANE - 2 * LOWRANK)], axis=0))


def kernel(x, norm1_g, w_in,w_decay_f, b_decay_f, w_decay_b, b_decay_b, gla_norm_g, gmlp_ln_g, gmlp_ln_b, w_spatial, b_spatial, w_out, norm2_g, w_gate, w_up, w_down, final_norm_g, loss_target, m_norm1_g, m_w_in, m_w_decay_f, m_b_decay_f, m_w_decay_b, m_b_decay_b, m_gla_norm_g, m_gmlp_ln_g, m_gmlp_ln_b, m_w_spatial, m_b_spatial, m_w_out, m_norm2_g, m_w_gate, m_w_up, m_w_down, m_final_norm_g, v_norm1_g, v_w_in, v_w_decay_f, v_b_decay_f, v_w_decay_b, v_b_decay_b, v_gla_norm_g, v_gmlp_ln_g, v_gmlp_ln_b, v_w_spatial, v_b_spatial, v_w_out, v_norm2_g, v_w_gate, v_w_up, v_w_down, v_final_norm_g):
    t = x.shape[1]
    xt = x[0]
    target = loss_target[0]
    pos_x, pos_y, pos_c = _mesh_pos()
    my_pos = jnp.stack([2 * pos_x + pos_y, pos_c]).astype(jnp.int32)
    my_id = 4 * pos_x + 2 * pos_y + pos_c

    tile = lambda n: min(n, t)
    ln_g, ln_b, w_sp = gmlp_ln_g, gmlp_ln_b, w_spatial[0]
    b_sp_col = b_spatial[0][:, :, None]
    chip_rows = {"w_in": 256, "w_out": 128, "w_gate": 256, "w_up": 256, "w_down": 352}
    chip_sum = lambda n, g, s: _chip_sum(my_pos, g, s[0], chip_rows[n], "chip_sum_" + n)

    decay_shard = jnp.stack([w_decay_f[0], w_decay_b[0]])
    ((g_in, g_decay),) = _comm_only([_gather_comm([w_in[0], decay_shard], [True, False])], "all_gather_w_in")
    w_main = _main_proj_weight(_unshard_cols(g_in))
    wd_pad_f, wd_pad_b = _padded_decay_weights(_unshard_cols(g_decay[:, 0]), _unshard_cols(g_decay[:, 1]))
    (p, hb), ((g_gate,),) = _in_proj(xt, norm1_g, w_main, tile(512), [_gather_comm([w_gate[0]], [True])])
    (o_f, st_f), ((g_up, g_out),) = _gla_fwd(p, wd_pad_f, b_decay_f, False, tile(512),
                                             [_gather_comm([w_up[0], w_out[0]], [True, True])])
    (o_b, st_b), ((g_down,),) = _gla_fwd(p, wd_pad_b, b_decay_b, True, tile(512),
                                         [_gather_comm([w_down[0]], [True])])
    w_out_full = g_out.reshape(D_MODEL, D_MODEL)
    x1, ycat = _mix_fwd(xt, o_f, o_b, p, gla_norm_g, ln_g, ln_b, w_sp, b_sp_col, w_out_full, tile(512))

    dx1, h2b, dgate, dup, act, dx2, loss_acc, d_gf, d_g2 = _ffn(
        x1, target, norm2_g, final_norm_g[None, :], _unshard_cols(g_gate), _unshard_cols(g_up),
        g_down.reshape(D_FF, D_MODEL), tile(256))
    (dw_gate, dw_up), _ = _matmul_tn(h2b, [dgate, dup], D_FF // 2, tile(512), "grad_w_gate_up")
    ((dw_down,), _) = _matmul_tn(act, [dx2], D_MODEL, tile(512), "grad_w_down")

    ffn_grads = [_shard_cols(dw_gate), _shard_cols(dw_up), dw_down.reshape(4, 2, D_FF // N_DEV, D_MODEL)]
    (d_o, dpg, dpu, dpv, dw_out, d_gg, d_lg, d_lb, dw_sp, db_sp), (ffn_sib,) = _mix_bwd(
        dx1, ycat, o_f, o_b, p, gla_norm_g, ln_g, ln_b, w_sp, b_sp_col, w_out_full, tile(256),
        [_sibling_exchange_comm(ffn_grads)])
    ffn_names = ["w_gate", "w_up", "w_down"]
    ffn_sums = [chip_sum(n, g, [s]) for n, g, s in zip(ffn_names, ffn_grads, ffn_sib)]
    out_grad = dw_out.reshape(4, 2, D_MODEL // N_DEV, D_MODEL)
    (dq_f, dk_f, dv_f, dlr_f, dwd_f, dbd_f), (ffn_recv, out_sib) = _gla_bwd(
        p, wd_pad_f, b_decay_f, st_f, d_o, False, tile(512),
        [_chips_exchange_comm([s[1] for s in ffn_sums]), _sibling_exchange_comm([out_grad])])
    out_sum = chip_sum("w_out", out_grad, out_sib)
    (dq_b, dk_b, dv_b, dlr_b, dwd_b, dbd_b), (out_recv,) = _gla_bwd(
        p, wd_pad_b, b_decay_b, st_b, d_o, True, tile(512), [_chips_exchange_comm([out_sum[1]])])
    grad_x, dp, d_g1 = _in_proj_bwd(xt, norm1_g, dx1, dq_f, dq_b, dk_f, dk_b, dv_f, dv_b, dpg, dpu, dpv,
                                    dlr_f, dlr_b, w_main, tile(256))

    small_shapes = [(1, D_MODEL), (LOWRANK, KEY_W), (1, KEY_W), (LOWRANK, KEY_W), (1, KEY_W), (1, VAL_W),
                    (1, GMLP_W), (1, GMLP_W), (1, GMLP_GROUPS, GMLP_CHUNK, GMLP_CHUNK), (1, GMLP_GROUPS, GMLP_CHUNK),
                    (1, D_MODEL), (D_MODEL,), (LANE,)]
    small_parts = [d_g1, dwd_f[:LOWRANK], dbd_f, dwd_b[LOWRANK:2 * LOWRANK], dbd_b, d_gg, d_lg, d_lb, dw_sp,
                   db_sp, d_g2, d_gf, loss_acc[0]]
    (dw_main,), ((small_sum,),) = _matmul_tn(hb, [dp], PROJ_PAD // 3, tile(512), "grad_w_in",
                                              [_all_reduce_small_comm(_pack_small(small_parts))])
    dw_in = jnp.concatenate([dw_main[:, :1536], dw_main[:, LR_COL:LR_COL + 2 * LOWRANK], dw_main[:, 1536:LR_COL]],
                            axis=1)
    in_grad = _shard_cols(dw_in)
    (in_sib,) = _comm_only([_sibling_exchange_comm([in_grad])], "grad_w_in_exchange_sibling")
    in_sum = chip_sum("w_in", in_grad, in_sib)
    (in_recv,) = _comm_only([_chips_exchange_comm([in_sum[1]])], "grad_w_in_exchange_chips")

    names = ["w_in", "w_out", "w_gate", "w_up", "w_down"]
    sums = [in_sum, out_sum] + ffn_sums
    received = [in_recv[0], out_recv[0]] + list(ffn_recv)
    big_w = [w_in[0], w_out[0], w_gate[0], w_up[0], w_down[0]]
    big_m = [m_w_in[0], m_w_out[0], m_w_gate[0], m_w_up[0], m_w_down[0]]
    big_v = [v_w_in[0], v_w_out[0], v_w_gate[0], v_w_up[0], v_w_down[0]]
    big_out = {n: _adamw_shard(s[0], rc, w, m, v, chip_rows[n], "adamw_" + n)
               for n, s, rc, w, m, v in zip(names, sums, received, big_w, big_m, big_v)}

    reduced = _unpack_small(small_sum, small_shapes)
    loss = reduced[-1][0]
    col0 = my_id * (KEY_W // N_DEV)
    g_small = list(reduced[:-1])
    g_small[1] = lax.dynamic_slice_in_dim(g_small[1], col0, KEY_W // N_DEV, axis=1)[None]
    g_small[3] = lax.dynamic_slice_in_dim(g_small[3], col0, KEY_W // N_DEV, axis=1)[None]
    small_w = [norm1_g, w_decay_f, b_decay_f, w_decay_b, b_decay_b, gla_norm_g, gmlp_ln_g, gmlp_ln_b, w_spatial,
               b_spatial, norm2_g, final_norm_g]
    small_m = [m_norm1_g, m_w_decay_f, m_b_decay_f, m_w_decay_b, m_b_decay_b, m_gla_norm_g, m_gmlp_ln_g,
               m_gmlp_ln_b, m_w_spatial, m_b_spatial, m_norm2_g, m_final_norm_g]
    small_v = [v_norm1_g, v_w_decay_f, v_b_decay_f, v_w_decay_b, v_b_decay_b, v_gla_norm_g, v_gmlp_ln_g,
               v_gmlp_ln_b, v_w_spatial, v_b_spatial, v_norm2_g, v_final_norm_g]
    shapes = [w.shape for w in small_w]
    packed = _adamw_small(_pack_small(g_small), _pack_small(small_w), _pack_small(small_m), _pack_small(small_v))
    s_delta, s_m, s_v = (_unpack_small(a, shapes) for a in packed)
    s_grad = [g.reshape(s) for g, s in zip(g_small, shapes)]

    order = ["norm1_g", "w_in", "w_decay_f", "b_decay_f", "w_decay_b", "b_decay_b", "gla_norm_g", "gmlp_ln_g",
             "gmlp_ln_b", "w_spatial", "b_spatial", "w_out", "norm2_g", "w_gate", "w_up", "w_down", "final_norm_g"]
    small_names = [n for n in order if n not in big_out]
    small_out = {n: (s_grad[i], s_delta[i], s_m[i], s_v[i]) for i, n in enumerate(small_names)}
    outs = []
    for kind in range(4):
        for n in order:
            outs.append(big_out[n][kind][None] if n in big_out else small_out[n][kind])
    return (loss, grad_x[None], *outs)
```

```python
import functools
import math

import jax
import jax.numpy as jnp
from jax import lax
from jax.experimental import pallas as pl
from jax.experimental.pallas import tpu as pltpu

F32 = jnp.float32
BF16 = jnp.bfloat16

D_MODEL = 1024
GLA_HEADS = 4
GLA_DK = 64
GLA_DV = 128
KEY_W = GLA_HEADS * GLA_DK
VAL_W = GLA_HEADS * GLA_DV
LOWRANK = 16
GLA_TAU = 16.0
GLA_CHUNK = 64
GMLP_W = 512
GMLP_GROUPS = 4
GMLP_CHUNK = 128
D_FF = 2816
EPS = 1e-6
Q_SCALE = GLA_DK ** -0.5
PROJ_PAD = 2688
LR_COL = 2560
LANE = 128
N_DEV = 8

ADAM_LR = 0.001
ADAM_B1 = 0.9
ADAM_B2 = 0.999
ADAM_EPS = 1e-08
ADAM_WD = 0.01
ADAM_STEP = 10

VMEM_LIMIT = 56 * 1024 * 1024
MESH_ID = pl.DeviceIdType.MESH
INV_SQRT2 = 0.7071067811865476
INV_SQRT_2PI = 0.3989422804014327


def _params(n_axes=1):
    return pltpu.CompilerParams(dimension_semantics=("arbitrary",) * n_axes, vmem_limit_bytes=VMEM_LIMIT)


def _mm(a, b):
    return jnp.dot(a.astype(BF16), b.astype(BF16), preferred_element_type=F32)


def _mm_nt(a, b):
    return lax.dot_general(a.astype(BF16), b.astype(BF16), (((1,), (1,)), ((), ())), preferred_element_type=F32)


def _mm_tn(a, b):
    return lax.dot_general(a.astype(BF16), b.astype(BF16), (((0,), (0,)), ((), ())), preferred_element_type=F32)


def _const_spec(shape):
    nd = len(shape)
    return pl.BlockSpec(shape, lambda *_: (0,) * nd, pipeline_mode=pl.Buffered(1))


def _acc_spec(shape):
    nd = len(shape)
    return pl.BlockSpec(shape, lambda *_: (0,) * nd)


class _Comm:
    def __init__(self, inputs, in_specs, out_shape, out_specs, scratch_shapes, before, after):
        self.inputs, self.in_specs, self.out_shape, self.out_specs = inputs, in_specs, out_shape, out_specs
        self.scratch_shapes, self.before, self.after = scratch_shapes, before, after


def _fused_call(body, comms, *, name, grid, inputs, in_specs, out_specs, out_shape, scratch_shapes=()):
    n_in, n_out, n_scr = len(in_specs), len(out_specs), len(scratch_shapes)
    nsteps = math.prod(grid)
    sizes = [(len(c.inputs), len(c.out_shape), len(c.scratch_shapes)) for c in comms]

    def full_body(*refs):
        step = pl.program_id(0)
        for axis in range(1, len(grid)):
            step = step * grid[axis] + pl.program_id(axis)
        ins, rest = refs[:n_in], refs[n_in:]
        c_ins = []
        for ci, _, _ in sizes:
            c_ins.append(rest[:ci])
            rest = rest[ci:]
        outs, rest = rest[:n_out], rest[n_out:]
        c_outs = []
        for _, co, _ in sizes:
            c_outs.append(rest[:co])
            rest = rest[co:]
        scr, rest = rest[:n_scr], rest[n_scr:]
        c_scr = []
        for _, _, cs in sizes:
            c_scr.append(rest[:cs])
            rest = rest[cs:]
        for c, a, b, s in zip(comms, c_ins, c_outs, c_scr):
            c.before(step, nsteps, a, b, s)
        body(*ins, *outs, *scr)
        for c, a, b, s in zip(comms, c_ins, c_outs, c_scr):
            c.after(step, nsteps, a, b, s)

    results = pl.pallas_call(
        full_body, name=name, grid=grid,
        in_specs=list(in_specs) + [s for c in comms for s in c.in_specs],
        out_specs=tuple(out_specs) + tuple(s for c in comms for s in c.out_specs),
        out_shape=tuple(out_shape) + tuple(s for c in comms for s in c.out_shape),
        scratch_shapes=list(scratch_shapes) + [s for c in comms for s in c.scratch_shapes],
        compiler_params=_params(len(grid)),
    )(*inputs, *[a for c in comms for a in c.inputs])
    own, rest = results[:n_out], results[n_out:]
    comm_results = []
    for _, co, _ in sizes:
        comm_results.append(rest[:co])
        rest = rest[co:]
    return own, comm_results


def _gelu(x):
    return 0.5 * x * (1.0 + lax.erf(x * INV_SQRT2))


def _gelu_grad(x):
    return 0.5 * (1.0 + lax.erf(x * INV_SQRT2)) + x * jnp.exp(-0.5 * x * x) * INV_SQRT_2PI


def _silu_and_grad(x):
    s = jax.nn.sigmoid(x)
    return x * s, s * (1.0 + x * (1.0 - s))


def _in_proj(x, g1, w_main, tm, comms=()):
    t = x.shape[0]

    def body(x_ref, g_ref, w_ref, p_ref, h_ref):
        xv = x_ref[...]
        r = lax.rsqrt(jnp.mean(xv * xv, axis=-1, keepdims=True) + EPS)
        h = (xv * r * g_ref[...]).astype(BF16)
        h_ref[...] = h
        p_ref[...] = _mm_nt(h, w_ref[...])

    return _fused_call(
        body, comms, name="in_proj", grid=(t // tm,), inputs=(x, g1, w_main),
        in_specs=[pl.BlockSpec((tm, D_MODEL), lambda i: (i, 0)), _const_spec((1, D_MODEL)),
                  _const_spec((PROJ_PAD, D_MODEL))],
        out_specs=(pl.BlockSpec((tm, PROJ_PAD), lambda i: (i, 0)), pl.BlockSpec((tm, D_MODEL), lambda i: (i, 0))),
        out_shape=(jax.ShapeDtypeStruct((t, PROJ_PAD), F32), jax.ShapeDtypeStruct((t, D_MODEL), BF16)))


def _tri(upper):
    r = lax.broadcasted_iota(jnp.int32, (GLA_CHUNK, GLA_CHUNK), 0)
    c = lax.broadcasted_iota(jnp.int32, (GLA_CHUNK, GLA_CHUNK), 1)
    return jnp.where((c >= r) if upper else (c <= r), 1.0, 0.0).astype(BF16)


def _tri_matmul(tri, a):
    a1 = a.astype(BF16)
    r1 = a - a1.astype(F32)
    a2 = r1.astype(BF16)
    a3 = (r1 - a2.astype(F32)).astype(BF16)
    dot = functools.partial(jnp.dot, preferred_element_type=F32)
    return dot(tri, a1) + dot(tri, a2) + dot(tri, a3)


def _gla_masks(rev):
    dk_bits, dv_bits = GLA_DK.bit_length() - 1, GLA_DV.bit_length() - 1
    key_head = lax.broadcasted_iota(jnp.int32, (GLA_CHUNK, KEY_W), 1) >> dk_bits
    val_head = lax.broadcasted_iota(jnp.int32, (GLA_CHUNK, VAL_W), 1) >> dv_bits
    t = lax.broadcasted_iota(jnp.int32, (GLA_HEADS * GLA_CHUNK, GLA_CHUNK), 0) & (GLA_CHUNK - 1)
    s = lax.broadcasted_iota(jnp.int32, (GLA_HEADS * GLA_CHUNK, GLA_CHUNK), 1)
    causal = (s >= t) if rev else (s <= t)
    blockdiag = (lax.broadcasted_iota(jnp.int32, (VAL_W, KEY_W), 0) >> dv_bits
                 == lax.broadcasted_iota(jnp.int32, (VAL_W, KEY_W), 1) >> dk_bits)
    return key_head, val_head, causal, blockdiag


def _stack_heads(a, head_of_lane):
    return jnp.concatenate([jnp.where(head_of_lane == h, a, 0.0) for h in range(GLA_HEADS)], axis=0)


def _chunk_terms(la_c, q_c, k_c, tri, rev):
    b = _tri_matmul(tri, la_c)
    bl = b[0:1] if rev else b[GLA_CHUNK - 1:GLA_CHUNK]
    eb = jnp.exp(b)
    enb = jnp.exp(-b)
    ee = jnp.exp(bl - b)
    return bl, eb, enb, ee, q_c * Q_SCALE * eb, k_c * enb, k_c * ee


def _log_decay(lr_ref, wd_ref, bd_ref):
    z = _mm(lr_ref[...], wd_ref[...]) + bd_ref[...]
    return z, jax.nn.log_sigmoid(z) * (1.0 / GLA_TAU)


def _gla_fwd(p, wd_pad, bd, rev, tg, comms=()):
    t = p.shape[0]
    nt = t // tg
    nc = tg // GLA_CHUNK
    tile = (lambda i: nt - 1 - i) if rev else (lambda i: i)

    def body(q_ref, k_ref, v_ref, lr_ref, wd_ref, bd_ref, o_ref, st_ref, state):
        @pl.when(pl.program_id(0) == 0)
        def _():
            state[...] = jnp.zeros_like(state)

        key_head, _, causal, blockdiag = _gla_masks(rev)
        tri = _tri(rev)
        _, la = _log_decay(lr_ref, wd_ref, bd_ref)
        for cc in range(nc):
            c = nc - 1 - cc if rev else cc
            rows = slice(c * GLA_CHUNK, (c + 1) * GLA_CHUNK)
            v_c = v_ref[rows, :].astype(BF16)
            bl, _, _, _, qd, kd, ke = _chunk_terms(la[rows], q_ref[rows, :], k_ref[rows, :], tri, rev)
            a_all = jnp.where(causal, _mm_nt(_stack_heads(qd, key_head), kd), 0.0)
            r = _mm(a_all, v_c)
            o_intra = jnp.concatenate(
                [r[h * GLA_CHUNK:(h + 1) * GLA_CHUNK, h * GLA_DV:(h + 1) * GLA_DV] for h in range(GLA_HEADS)], axis=1)
            st = state[...]
            o_ref[rows, :] = o_intra + _mm_nt(qd, st)
            st_ref[c] = st.astype(BF16)
            state[...] = st * jnp.exp(bl) + jnp.where(blockdiag, _mm_tn(v_c, ke), 0.0)

    return _fused_call(
        body, comms, name="gla_fwd_rev" if rev else "gla_fwd", grid=(nt,), inputs=(p, p, p, p, wd_pad, bd),
        in_specs=[pl.BlockSpec((tg, KEY_W), lambda i: (tile(i), 0)),
                  pl.BlockSpec((tg, KEY_W), lambda i: (tile(i), 1)),
                  pl.BlockSpec((tg, VAL_W), lambda i: (tile(i), 1)),
                  pl.BlockSpec((tg, LANE), lambda i: (tile(i), LR_COL // LANE)),
                  _const_spec((LANE, KEY_W)), _const_spec((1, KEY_W))],
        out_specs=(pl.BlockSpec((tg, VAL_W), lambda i: (tile(i), 0)),
                   pl.BlockSpec((nc, VAL_W, KEY_W), lambda i: (tile(i), 0, 0))),
        out_shape=(jax.ShapeDtypeStruct((t, VAL_W), F32),
                   jax.ShapeDtypeStruct((t // GLA_CHUNK, VAL_W, KEY_W), BF16)),
        scratch_shapes=[pltpu.VMEM((VAL_W, KEY_W), F32)])


def _gla_bwd(p, wd_pad, bd, states, d_o, rev, tg, comms=()):
    t = p.shape[0]
    nt = t // tg
    nc = tg // GLA_CHUNK
    tile = (lambda i: i) if rev else (lambda i: nt - 1 - i)

    def body(q_ref, k_ref, v_ref, lr_ref, wd_ref, bd_ref, st_ref, do_ref,
             dq_ref, dk_ref, dv_ref, dlr_ref, dwd_ref, dbd_ref, dstate, dz_scr):
        @pl.when(pl.program_id(0) == 0)
        def _():
            dstate[...] = jnp.zeros_like(dstate)
            dwd_ref[...] = jnp.zeros_like(dwd_ref)
            dbd_ref[...] = jnp.zeros_like(dbd_ref)

        key_head, val_head, causal, blockdiag = _gla_masks(rev)
        tri = _tri(rev)
        tri_t = _tri(not rev)
        z, la = _log_decay(lr_ref, wd_ref, bd_ref)
        dlog = jax.nn.sigmoid(-z) * (1.0 / GLA_TAU)
        for cc in range(nc):
            c = cc if rev else nc - 1 - cc
            rows = slice(c * GLA_CHUNK, (c + 1) * GLA_CHUNK)
            v_c = v_ref[rows, :].astype(BF16)
            do_c = do_ref[rows, :]
            bl, eb, enb, ee, qd, kd, ke = _chunk_terms(la[rows], q_ref[rows, :], k_ref[rows, :], tri, rev)
            qd_stack = _stack_heads(qd, key_head)
            do_stack = _stack_heads(do_c, val_head)
            a_all = jnp.where(causal, _mm_nt(qd_stack, kd), 0.0)
            da_all = jnp.where(causal, _mm_nt(do_stack, v_c), 0.0)
            dst = dstate[...]
            st_prev = st_ref[c]
            dv_ref[rows, :] = _mm_tn(a_all, do_stack) + _mm_nt(ke, dst)
            r2 = _mm(da_all, kd)
            dqd = _mm(do_c, st_prev)
            for h in range(GLA_HEADS):
                dqd = dqd + jnp.where(key_head == h, r2[h * GLA_CHUNK:(h + 1) * GLA_CHUNK, :], 0.0)
            dkd = _mm_tn(da_all, qd_stack)
            dke = _mm(v_c, dst)
            ebl = jnp.exp(bl)
            dbl = (jnp.sum(dst * st_prev.astype(F32), axis=0, keepdims=True) * ebl
                   + jnp.sum(dke * ke, axis=0, keepdims=True))
            dstate[...] = dst * ebl + jnp.where(blockdiag, _mm_tn(do_c, qd), 0.0)
            dq_ref[rows, :] = dqd * eb * Q_SCALE
            dk_ref[rows, :] = dkd * enb + dke * ee
            db = dqd * qd - dkd * kd - dke * ke
            dz_scr[rows, :] = (_tri_matmul(tri_t, db) + dbl) * dlog[rows]
        dz = dz_scr[...]
        dlr_ref[...] = _mm_nt(dz, wd_ref[...])
        dwd_ref[...] += _mm_tn(lr_ref[...], dz)
        dbd_ref[...] += jnp.sum(dz, axis=0, keepdims=True)

    return _fused_call(
        body, comms, name="gla_bwd_rev" if rev else "gla_bwd", grid=(nt,),
        inputs=(p, p, p, p, wd_pad, bd, states, d_o),
        in_specs=[pl.BlockSpec((tg, KEY_W), lambda i: (tile(i), 0)),
                  pl.BlockSpec((tg, KEY_W), lambda i: (tile(i), 1)),
                  pl.BlockSpec((tg, VAL_W), lambda i: (tile(i), 1)),
                  pl.BlockSpec((tg, LANE), lambda i: (tile(i), LR_COL // LANE)),
                  _const_spec((LANE, KEY_W)), _const_spec((1, KEY_W)),
                  pl.BlockSpec((nc, VAL_W, KEY_W), lambda i: (tile(i), 0, 0)),
                  pl.BlockSpec((tg, VAL_W), lambda i: (tile(i), 0))],
        out_specs=(pl.BlockSpec((tg, KEY_W), lambda i: (tile(i), 0)),
                   pl.BlockSpec((tg, KEY_W), lambda i: (tile(i), 0)),
                   pl.BlockSpec((tg, VAL_W), lambda i: (tile(i), 0)),
                   pl.BlockSpec((tg, LANE), lambda i: (tile(i), 0)),
                   _acc_spec((LANE, KEY_W)), _acc_spec((1, KEY_W))),
        out_shape=(jax.ShapeDtypeStruct((t, KEY_W), F32), jax.ShapeDtypeStruct((t, KEY_W), F32),
                   jax.ShapeDtypeStruct((t, VAL_W), F32), jax.ShapeDtypeStruct((t, LANE), F32),
                   jax.ShapeDtypeStruct((LANE, KEY_W), F32), jax.ShapeDtypeStruct((1, KEY_W), F32)),
        scratch_shapes=[pltpu.VMEM((VAL_W, KEY_W), F32), pltpu.VMEM((tg, KEY_W), F32)])


def _head_rms(o):
    parts, scales = [], []
    for h in range(GLA_HEADS):
        oh = o[:, h * GLA_DV:(h + 1) * GLA_DV]
        r = lax.rsqrt(jnp.mean(oh * oh, axis=-1, keepdims=True) + EPS)
        parts.append(oh * r)
        scales.append(jnp.broadcast_to(r, oh.shape))
    return jnp.concatenate(parts, axis=1), jnp.concatenate(scales, axis=1)


def _layernorm_stats(zv):
    mu = jnp.mean(zv, axis=-1, keepdims=True)
    xc = zv - mu
    rs = lax.rsqrt(jnp.mean(xc * xc, axis=-1, keepdims=True) + EPS)
    return xc * rs, rs


def _mix_fwd(x, o_f, o_b, p, gla_g, ln_g, ln_b, w_sp, b_sp, w_out, tm):
    t = x.shape[0]
    nch = tm // GMLP_CHUNK

    def body(x_ref, of_ref, ob_ref, pg_ref, pu_ref, pv_ref, gg_ref, lg_ref, lb_ref, ws_ref, bs_ref, wo_ref,
             x1_ref, y_ref, s_scr):
        on, _ = _head_rms(of_ref[...] + ob_ref[...])
        pg = pg_ref[...]
        y_a = on * gg_ref[...] * (pg * jax.nn.sigmoid(pg))
        zu = _gelu(pu_ref[...])
        vhat, _ = _layernorm_stats(_gelu(pv_ref[...]))
        vln = (vhat * lg_ref[...] + lb_ref[...]).astype(BF16)
        for g in range(GMLP_GROUPS):
            w_g = ws_ref[g].astype(BF16)
            b_g = bs_ref[g]
            cols = slice(g * LANE, (g + 1) * LANE)
            for n in range(nch):
                rows = slice(n * GMLP_CHUNK, (n + 1) * GMLP_CHUNK)
                s_scr[rows, cols] = jnp.dot(w_g, vln[rows, cols], preferred_element_type=F32) + b_g
        ycat = jnp.concatenate([y_a, zu * s_scr[...]], axis=1).astype(BF16)
        y_ref[...] = ycat
        x1_ref[...] = x_ref[...] + jnp.dot(ycat, wo_ref[...], preferred_element_type=F32)

    half = lambda j: pl.BlockSpec((tm, VAL_W), lambda i: (i, j))
    return pl.pallas_call(
        body, name="mix_fwd", grid=(t // tm,),
        in_specs=[pl.BlockSpec((tm, D_MODEL), lambda i: (i, 0)), half(0), half(0), half(2), half(3), half(4),
                  _const_spec((1, VAL_W)), _const_spec((1, GMLP_W)), _const_spec((1, GMLP_W)),
                  _const_spec((GMLP_GROUPS, GMLP_CHUNK, GMLP_CHUNK)), _const_spec((GMLP_GROUPS, GMLP_CHUNK, 1)),
                  _const_spec((D_MODEL, D_MODEL))],
        out_specs=(pl.BlockSpec((tm, D_MODEL), lambda i: (i, 0)), pl.BlockSpec((tm, D_MODEL), lambda i: (i, 0))),
        out_shape=(jax.ShapeDtypeStruct((t, D_MODEL), F32), jax.ShapeDtypeStruct((t, D_MODEL), BF16)),
        scratch_shapes=[pltpu.VMEM((tm, GMLP_W), F32)],
        compiler_params=_params(),
    )(x, o_f, o_b, p, p, p, gla_g, ln_g, ln_b, w_sp, b_sp, w_out)


def _mix_bwd(dx1, ycat, o_f, o_b, p, gla_g, ln_g, ln_b, w_sp, b_sp, w_out, tm, comms=()):
    t = dx1.shape[0]
    nch = tm // GMLP_CHUNK

    def body(dx1_ref, y_ref, of_ref, ob_ref, pg_ref, pu_ref, pv_ref, gg_ref, lg_ref, lb_ref, ws_ref, bs_ref, wo_ref,
             do_ref, dpg_ref, dpu_ref, dpv_ref, dwo_ref, dgg_ref, dlg_ref, dlb_ref, dws_ref, dbs_ref,
             s_scr, dvln_scr):
        @pl.when(pl.program_id(0) == 0)
        def _():
            for ref in (dwo_ref, dgg_ref, dlg_ref, dlb_ref, dws_ref, dbs_ref):
                ref[...] = jnp.zeros_like(ref)

        dx1 = dx1_ref[...].astype(BF16)
        dycat = _mm_nt(dx1, wo_ref[...])
        dwo_ref[...] += _mm_tn(y_ref[...], dx1)
        dy_a = dycat[:, :VAL_W]
        dy_b = dycat[:, VAL_W:]
        on, r = _head_rms(of_ref[...] + ob_ref[...])
        pg = pg_ref[...]
        sil, dsil = _silu_and_grad(pg)
        gg = gg_ref[...]
        dgg_ref[...] += jnp.sum(dy_a * sil * on, axis=0, keepdims=True)
        don = dy_a * sil * gg
        prod = don * on
        means = jnp.concatenate(
            [jnp.broadcast_to(jnp.mean(prod[:, h * GLA_DV:(h + 1) * GLA_DV], axis=-1, keepdims=True),
                              (tm, GLA_DV)) for h in range(GLA_HEADS)], axis=1)
        do_ref[...] = r * (don - on * means)
        dpg_ref[...] = dy_a * on * gg * dsil
        pu = pu_ref[...]
        pv = pv_ref[...]
        zu = _gelu(pu)
        vhat, rs = _layernorm_stats(_gelu(pv))
        lg = lg_ref[...]
        vln = (vhat * lg + lb_ref[...]).astype(BF16)
        ds32 = dy_b * zu
        ds = ds32.astype(BF16)
        for g in range(GMLP_GROUPS):
            w_g = ws_ref[g].astype(BF16)
            b_g = bs_ref[g]
            cols = slice(g * LANE, (g + 1) * LANE)
            dw_g = jnp.zeros((GMLP_CHUNK, GMLP_CHUNK), F32)
            db_g = jnp.zeros((GMLP_CHUNK, 1), F32)
            for n in range(nch):
                rows = slice(n * GMLP_CHUNK, (n + 1) * GMLP_CHUNK)
                v_blk = vln[rows, cols]
                ds_blk = ds[rows, cols]
                s_scr[rows, cols] = jnp.dot(w_g, v_blk, preferred_element_type=F32) + b_g
                dw_g = dw_g + _mm_nt(ds_blk, v_blk)
                db_g = db_g + jnp.sum(ds32[rows, cols], axis=-1, keepdims=True)
                dvln_scr[rows, cols] = _mm_tn(w_g, ds_blk)
            dws_ref[g] += dw_g
            dbs_ref[g] += db_g
        dpu_ref[...] = dy_b * s_scr[...] * _gelu_grad(pu)
        dvln = dvln_scr[...]
        dlg_ref[...] += jnp.sum(dvln * vhat, axis=0, keepdims=True)
        dlb_ref[...] += jnp.sum(dvln, axis=0, keepdims=True)
        dvhat = dvln * lg
        dzv = rs * (dvhat - jnp.mean(dvhat, axis=-1, keepdims=True)
                    - vhat * jnp.mean(dvhat * vhat, axis=-1, keepdims=True))
        dpv_ref[...] = dzv * _gelu_grad(pv)

    half = lambda j: pl.BlockSpec((tm, VAL_W), lambda i: (i, j))
    full = pl.BlockSpec((tm, D_MODEL), lambda i: (i, 0))
    sp_shape = (GMLP_GROUPS, GMLP_CHUNK, GMLP_CHUNK)
    bs_shape = (GMLP_GROUPS, GMLP_CHUNK, 1)
    return _fused_call(
        body, comms, name="mix_bwd", grid=(t // tm,),
        inputs=(dx1, ycat, o_f, o_b, p, p, p, gla_g, ln_g, ln_b, w_sp, b_sp, w_out),
        in_specs=[full, full, half(0), half(0), half(2), half(3), half(4),
                  _const_spec((1, VAL_W)), _const_spec((1, GMLP_W)), _const_spec((1, GMLP_W)),
                  _const_spec(sp_shape), _const_spec(bs_shape), _const_spec((D_MODEL, D_MODEL))],
        out_specs=(half(0), half(0), half(0), half(0), _acc_spec((D_MODEL, D_MODEL)), _acc_spec((1, VAL_W)),
                   _acc_spec((1, GMLP_W)), _acc_spec((1, GMLP_W)), _acc_spec(sp_shape), _acc_spec(bs_shape)),
        out_shape=(jax.ShapeDtypeStruct((t, VAL_W), F32),) * 4 + (
            jax.ShapeDtypeStruct((D_MODEL, D_MODEL), F32), jax.ShapeDtypeStruct((1, VAL_W), F32),
            jax.ShapeDtypeStruct((1, GMLP_W), F32), jax.ShapeDtypeStruct((1, GMLP_W), F32),
            jax.ShapeDtypeStruct(sp_shape, F32), jax.ShapeDtypeStruct(bs_shape, F32)),
        scratch_shapes=[pltpu.VMEM((tm, GMLP_W), F32), pltpu.VMEM((tm, GMLP_W), F32)])


def _rms_bwd(dy_scaled, xn, r):
    return r * (dy_scaled - xn * jnp.mean(dy_scaled * xn, axis=-1, keepdims=True))


def _ffn(x1, target, g2, gf, w_gate, w_up, w_down, tm):
    t = x1.shape[0]

    def body(x1_ref, tg_ref, g2_ref, gf_ref, wg_ref, wu_ref, wd_ref,
             dx1_ref, h2_ref, dgate_ref, dup_ref, act_ref, dx2_ref, loss_ref, dgf_ref, dg2_ref):
        @pl.when(pl.program_id(0) == 0)
        def _():
            for ref in (loss_ref, dgf_ref, dg2_ref):
                ref[...] = jnp.zeros_like(ref)

        x1v = x1_ref[...]
        g2v = g2_ref[...]
        gfv = gf_ref[...]
        r2 = lax.rsqrt(jnp.mean(x1v * x1v, axis=-1, keepdims=True) + EPS)
        xn1 = x1v * r2
        h2 = (xn1 * g2v).astype(BF16)
        h2_ref[...] = h2
        gate = _mm_nt(h2, wg_ref[...])
        up = _mm_nt(h2, wu_ref[...])
        sil, dsil = _silu_and_grad(gate)
        act = (sil * up).astype(BF16)
        act_ref[...] = act
        x2 = x1v + jnp.dot(act, wd_ref[...], preferred_element_type=F32)
        rf = lax.rsqrt(jnp.mean(x2 * x2, axis=-1, keepdims=True) + EPS)
        xn2 = x2 * rf
        err = xn2 * gfv - tg_ref[...]
        loss_ref[...] += 0.5 * jnp.sum(jnp.mean(err * err, axis=-1, keepdims=True))
        dy = err * (1.0 / D_MODEL)
        dgf_ref[...] += jnp.sum(dy * xn2, axis=0, keepdims=True)
        dx2 = _rms_bwd(dy * gfv, xn2, rf)
        dx2b = dx2.astype(BF16)
        dx2_ref[...] = dx2b
        dact = _mm_nt(dx2b, wd_ref[...])
        dgate = (dact * up * dsil).astype(BF16)
        dup = (dact * sil).astype(BF16)
        dgate_ref[...] = dgate
        dup_ref[...] = dup
        dh2 = _mm(dgate, wg_ref[...]) + _mm(dup, wu_ref[...])
        dg2_ref[...] += jnp.sum(dh2 * xn1, axis=0, keepdims=True)
        dx1_ref[...] = dx2 + _rms_bwd(dh2 * g2v, xn1, r2)

    row = lambda w: pl.BlockSpec((tm, w), lambda i: (i, 0))
    return pl.pallas_call(
        body, name="ffn_fwd_bwd", grid=(t // tm,),
        in_specs=[row(D_MODEL), row(D_MODEL), _const_spec((1, D_MODEL)), _const_spec((1, D_MODEL)),
                  _const_spec((D_FF, D_MODEL)), _const_spec((D_FF, D_MODEL)), _const_spec((D_FF, D_MODEL))],
        out_specs=(row(D_MODEL), row(D_MODEL), row(D_FF), row(D_FF), row(D_FF), row(D_MODEL),
                   _acc_spec((8, LANE)), _acc_spec((1, D_MODEL)), _acc_spec((1, D_MODEL))),
        out_shape=(jax.ShapeDtypeStruct((t, D_MODEL), F32), jax.ShapeDtypeStruct((t, D_MODEL), BF16),
                   jax.ShapeDtypeStruct((t, D_FF), BF16), jax.ShapeDtypeStruct((t, D_FF), BF16),
                   jax.ShapeDtypeStruct((t, D_FF), BF16), jax.ShapeDtypeStruct((t, D_MODEL), BF16),
                   jax.ShapeDtypeStruct((8, LANE), F32), jax.ShapeDtypeStruct((1, D_MODEL), F32),
                   jax.ShapeDtypeStruct((1, D_MODEL), F32)),
        compiler_params=_params(),
    )(x1, target, g2, gf, w_gate, w_up, w_down)


def _matmul_tn(a, b, tm, tk, name, comms=()):
    t, m = a.shape
    n = b.shape[1]

    def body(a_ref, b_ref, o_ref):
        @pl.when(pl.program_id(1) == 0)
        def _():
            o_ref[...] = jnp.zeros_like(o_ref)

        o_ref[...] += _mm_tn(a_ref[...], b_ref[...])

    (out,), comm_results = _fused_call(
        body, comms, name=name, grid=(m // tm, t // tk), inputs=(a, b),
        in_specs=[pl.BlockSpec((tk, tm), lambda j, k: (k, j)), pl.BlockSpec((tk, n), lambda j, k: (k, 0))],
        out_specs=(pl.BlockSpec((tm, n), lambda j, k: (j, 0)),),
        out_shape=(jax.ShapeDtypeStruct((m, n), F32),))
    return out, comm_results


def _in_proj_bwd(x, g1, dx1, dq_f, dq_b, dk_f, dk_b, dv_f, dv_b, dpg, dpu, dpv, dlr_f, dlr_b, w_main, tm):
    t = x.shape[0]

    def body(x_ref, g_ref, dx1_ref, dqf, dqb, dkf, dkb, dvf, dvb, dg, du, dv, dlf, dlb, w_ref,
             dx_ref, dp_ref, dg1_ref):
        @pl.when(pl.program_id(0) == 0)
        def _():
            dg1_ref[...] = jnp.zeros_like(dg1_ref)

        dp = jnp.concatenate([dqf[...] + dqb[...], dkf[...] + dkb[...], dvf[...] + dvb[...],
                              dg[...], du[...], dv[...], dlf[...] + dlb[...]], axis=1).astype(BF16)
        dp_ref[...] = dp
        dh = _mm(dp, w_ref[...])
        xv = x_ref[...]
        r = lax.rsqrt(jnp.mean(xv * xv, axis=-1, keepdims=True) + EPS)
        xn = xv * r
        dg1_ref[...] += jnp.sum(dh * xn, axis=0, keepdims=True)
        dx_ref[...] = dx1_ref[...] + _rms_bwd(dh * g_ref[...], xn, r)

    row = lambda w: pl.BlockSpec((tm, w), lambda i: (i, 0))
    return pl.pallas_call(
        body, name="in_proj_bwd", grid=(t // tm,),
        in_specs=[row(D_MODEL), _const_spec((1, D_MODEL)), row(D_MODEL), row(KEY_W), row(KEY_W), row(KEY_W),
                  row(KEY_W), row(VAL_W), row(VAL_W), row(VAL_W), row(VAL_W), row(VAL_W), row(LANE), row(LANE),
                  _const_spec((PROJ_PAD, D_MODEL))],
        out_specs=(row(D_MODEL), row(PROJ_PAD), _acc_spec((1, D_MODEL))),
        out_shape=(jax.ShapeDtypeStruct((t, D_MODEL), F32), jax.ShapeDtypeStruct((t, PROJ_PAD), BF16),
                   jax.ShapeDtypeStruct((1, D_MODEL), F32)),
        compiler_params=_params(),
    )(x, g1, dx1, dq_f, dq_b, dk_f, dk_b, dv_f, dv_b, dpg, dpu, dpv, dlr_f, dlr_b, w_main)


def _adamw(w, g, m, v):
    m_new = ADAM_B1 * m + (1.0 - ADAM_B1) * g
    v_new = ADAM_B2 * v + (1.0 - ADAM_B2) * (g * g)
    m_hat = m_new / (1.0 - ADAM_B1 ** ADAM_STEP)
    v_hat = v_new / (1.0 - ADAM_B2 ** ADAM_STEP)
    delta = -ADAM_LR * (m_hat / (jnp.sqrt(v_hat) + ADAM_EPS) + ADAM_WD * w)
    return delta, m_new, v_new


def _adamw_shard(own, recv, w, m, v, tr, name):
    r, c = w.shape

    def body(own_ref, recv_ref, w_ref, m_ref, v_ref, g_ref, d_ref, nm_ref, nv_ref):
        g = own_ref[...]
        for k in range(3):
            g = g + recv_ref[k].astype(F32)
        g_ref[...] = g
        d_ref[...], nm_ref[...], nv_ref[...] = _adamw(w_ref[...], g, m_ref[...], v_ref[...])

    row = pl.BlockSpec((tr, c), lambda i: (i, 0))
    return pl.pallas_call(
        body, name=name, grid=(r // tr,),
        in_specs=[row, pl.BlockSpec((3, tr, c), lambda i: (0, i, 0)), row, row, row],
        out_specs=(row,) * 4, out_shape=(jax.ShapeDtypeStruct((r, c), F32),) * 4,
        compiler_params=_params(),
    )(own, recv, w, m, v)


def _adamw_small(g, w, m, v):
    def body(g_ref, w_ref, m_ref, v_ref, d_ref, nm_ref, nv_ref):
        d_ref[...], nm_ref[...], nv_ref[...] = _adamw(w_ref[...], g_ref[...], m_ref[...], v_ref[...])

    return pl.pallas_call(body, name="adamw_small", out_shape=(jax.ShapeDtypeStruct(g.shape, F32),) * 3,
                          compiler_params=pltpu.CompilerParams(vmem_limit_bytes=VMEM_LIMIT))(g, w, m, v)


def _mesh_pos():
    return lax.axis_index("x"), lax.axis_index("y"), lax.axis_index("c")


def _other_chips(x, y):
    return [(x, 1 - y), (1 - x, y), (1 - x, 1 - y)]


_VMEM_WHOLE = pl.BlockSpec(memory_space=pltpu.VMEM)
_HBM_WHOLE = pl.BlockSpec(memory_space=pl.ANY)


def _gather_comm(shards, cast, mid=(1, 2)):
    na = len(shards)
    staged = [a for a in range(na) if cast[a]]

    def phases(in_refs, out_refs, scr):
        stage = dict(zip(staged, scr[:len(staged)]))
        send_sems, recv_sems, local_sems = scr[len(staged):]
        x, y, c = _mesh_pos()
        me, sibling = (x, y, c), (x, y, 1 - c)
        chips = _other_chips(x, y)
        srcs = [stage[a] if cast[a] else in_refs[a] for a in range(na)]

        def rows(a, pos):
            px, py, pc = pos
            return out_refs[a].at[4 * px + 2 * py + pc]

        def copy(a, k, block, to, src=None):
            return pltpu.make_async_remote_copy(
                src_ref=rows(a, block) if src is None else src, dst_ref=rows(a, block),
                send_sem=send_sems.at[a, k], recv_sem=recv_sems.at[a, k], device_id=to, device_id_type=MESH_ID)

        mine = [pltpu.make_async_copy(srcs[a], rows(a, me), local_sems.at[a]) for a in range(na)]
        first = []
        for a in range(na):
            first.append(copy(a, 0, me, sibling, src=srcs[a]))
            first += [copy(a, 1 + j, me, (*chip, c), src=srcs[a]) for j, chip in enumerate(chips)]
        passed = [copy(a, 4 + j, (*chip, c), sibling) for j, chip in enumerate(chips) for a in range(na)]

        def start():
            for a in staged:
                stage[a][...] = in_refs[a][...].astype(BF16)
            for cp in mine + first:
                cp.start()

        def forward():
            i = 0
            for j, chip in enumerate(chips):
                for a in range(na):
                    copy(a, 1 + j, (*chip, c), me).wait_recv()
                    passed[i].start()
                    i += 1

        def finish():
            for a in range(na):
                copy(a, 0, sibling, me).wait_recv()
                for j, chip in enumerate(chips):
                    copy(a, 4 + j, (*chip, 1 - c), me).wait_recv()
            for cp in first + passed:
                cp.wait_send()
            for cp in mine:
                cp.wait()

        return start, forward, finish

    def before(step, nsteps, in_refs, out_refs, scr):
        start, forward, _ = phases(in_refs, out_refs, scr)
        pl.when(step == 0)(start)
        pl.when(step == nsteps * mid[0] // mid[1])(forward)

    def after(step, nsteps, in_refs, out_refs, scr):
        pl.when(step == nsteps - 1)(phases(in_refs, out_refs, scr)[2])

    return _Comm(
        inputs=list(shards), in_specs=[_VMEM_WHOLE] * na,
        out_shape=[jax.ShapeDtypeStruct((N_DEV,) + s.shape, BF16 if cast[a] else s.dtype)
                   for a, s in enumerate(shards)],
        out_specs=[_HBM_WHOLE] * na,
        scratch_shapes=[pltpu.VMEM(shards[a].shape, BF16) for a in staged] + [
            pltpu.SemaphoreType.DMA((na, 7)), pltpu.SemaphoreType.DMA((na, 7)), pltpu.SemaphoreType.DMA((na,))],
        before=before, after=after)


def _exchange_comm(arrays, out_shape, make_copies):
    na = len(arrays)

    def copies(in_refs, out_refs, scr):
        return make_copies(in_refs, out_refs, *scr)

    def before(step, nsteps, in_refs, out_refs, scr):
        @pl.when(step == 0)
        def _():
            for cp in copies(in_refs, out_refs, scr):
                cp.start()

    def after(step, nsteps, in_refs, out_refs, scr):
        @pl.when(step == nsteps - 1)
        def _():
            for cp in copies(in_refs, out_refs, scr):
                cp.wait()

    return _Comm(inputs=list(arrays), in_specs=[_HBM_WHOLE] * na, out_shape=list(out_shape),
                 out_specs=[_HBM_WHOLE] * na,
                 scratch_shapes=[pltpu.SemaphoreType.DMA((na, 3)), pltpu.SemaphoreType.DMA((na, 3))],
                 before=before, after=after)


def _sibling_exchange_comm(grads):
    def make_copies(in_refs, out_refs, send_sems, recv_sems):
        x, y, c = _mesh_pos()
        return [pltpu.make_async_remote_copy(
            src_ref=in_refs[a].at[:, pl.ds(1 - c, 1)], dst_ref=out_refs[a], send_sem=send_sems.at[a, 0],
            recv_sem=recv_sems.at[a, 0], device_id=(x, y, 1 - c), device_id_type=MESH_ID)
            for a in range(len(grads))]

    return _exchange_comm(grads, [jax.ShapeDtypeStruct((4, 1) + g.shape[2:], F32) for g in grads], make_copies)


def _chips_exchange_comm(partials):
    def make_copies(in_refs, out_refs, send_sems, recv_sems):
        x, y, c = _mesh_pos()
        return [pltpu.make_async_remote_copy(
            src_ref=in_refs[a].at[j], dst_ref=out_refs[a].at[j], send_sem=send_sems.at[a, j],
            recv_sem=recv_sems.at[a, j], device_id=(*chip, c), device_id_type=MESH_ID)
            for a in range(len(partials)) for j, chip in enumerate(_other_chips(x, y))]

    return _exchange_comm(partials, [jax.ShapeDtypeStruct(g.shape, BF16) for g in partials], make_copies)


def _comm_only(comms, name):
    return _fused_call(lambda: None, comms, name=name, grid=(1,), inputs=(), in_specs=[], out_specs=(),
                       out_shape=())[1]


def _chip_sum(my_pos, mine, from_sibling, tr, name):
    _, _, r, c = mine.shape

    def body(pos_ref, a_ref, b_ref, own_ref, out_ref):
        s = a_ref[0, 0] + b_ref[0, 0]

        @pl.when(pl.program_id(1) == 0)
        def _():
            own_ref[...] = s

        @pl.when(pl.program_id(1) > 0)
        def _():
            out_ref[0] = s.astype(BF16)

    grid_spec = pltpu.PrefetchScalarGridSpec(
        num_scalar_prefetch=1, grid=(r // tr, 4),
        in_specs=[pl.BlockSpec((1, 1, tr, c), lambda i, k, pos: (pos[0] ^ k, pos[1], i, 0)),
                  pl.BlockSpec((1, 1, tr, c), lambda i, k, pos: (pos[0] ^ k, 0, i, 0))],
        out_specs=(pl.BlockSpec((tr, c), lambda i, k, pos: (i, 0)),
                   pl.BlockSpec((1, tr, c), lambda i, k, pos: (jnp.maximum(k - 1, 0), i, 0))))
    return pl.pallas_call(
        body, name=name, grid_spec=grid_spec,
        out_shape=(jax.ShapeDtypeStruct((r, c), F32), jax.ShapeDtypeStruct((3, r, c), BF16)),
        compiler_params=_params(2),
    )(my_pos, mine, from_sibling)


def _all_reduce_small_comm(part):
    def copies(in_ref, gathered, send_sems, recv_sems):
        x, y, c = _mesh_pos()
        my_id = 4 * x + 2 * y + c
        return my_id, [pltpu.make_async_remote_copy(
            src_ref=in_ref, dst_ref=gathered.at[my_id], send_sem=send_sems.at[k - 1],
            recv_sem=recv_sems.at[k - 1], device_id=(x ^ (k >> 2), y ^ ((k >> 1) & 1), c ^ (k & 1)),
            device_id_type=MESH_ID) for k in range(1, N_DEV)]

    def before(step, nsteps, in_refs, out_refs, scr):
        @pl.when(step == 0)
        def _():
            for cp in copies(in_refs[0], *scr)[1]:
                cp.start()

    def after(step, nsteps, in_refs, out_refs, scr):
        @pl.when(step == nsteps - 1)
        def _():
            gathered = scr[0]
            my_id, cps = copies(in_refs[0], *scr)
            gathered[my_id] = in_refs[0][...]
            for cp in cps:
                cp.wait()
            acc = gathered[0]
            for d in range(1, N_DEV):
                acc = acc + gathered[d]
            out_refs[0][...] = acc

    return _Comm(inputs=[part], in_specs=[_VMEM_WHOLE], out_shape=[jax.ShapeDtypeStruct(part.shape, F32)],
                 out_specs=[_VMEM_WHOLE],
                 scratch_shapes=[pltpu.VMEM((N_DEV,) + part.shape, F32), pltpu.SemaphoreType.DMA((N_DEV - 1,)),
                                 pltpu.SemaphoreType.DMA((N_DEV - 1,))],
                 before=before, after=after)


def _unshard_cols(g):
    return jnp.transpose(g, (1, 0, 2)).reshape(g.shape[1], N_DEV * g.shape[2])


def _row_blocks(w):
    return w.reshape(4, 2, w.shape[0] // N_DEV, w.shape[1])


def _pack_small(parts):
    flat = jnp.concatenate([a.reshape(-1) for a in parts])
    pad = (-flat.shape[0]) % (8 * LANE)
    return jnp.pad(flat, (0, pad)).reshape(-1, LANE)


def _unpack_small(packed, shapes):
    flat = packed.reshape(-1)
    out, off = [], 0
    for s in shapes:
        n = math.prod(s)
        out.append(flat[off:off + n].reshape(s))
        off += n
    return out


PROJ_W = 2592
LR_REF = 1536


def _main_proj_weight(w_in_t):
    return jnp.concatenate([w_in_t[:LR_REF], w_in_t[LR_REF + 2 * LOWRANK:], w_in_t[LR_REF:LR_REF + 2 * LOWRANK],
                            jnp.zeros((PROJ_PAD - PROJ_W, D_MODEL), w_in_t.dtype)], axis=0)


def _main_proj_grad(dw_main_t):
    return jnp.concatenate([dw_main_t[:LR_REF], dw_main_t[LR_COL:LR_COL + 2 * LOWRANK], dw_main_t[LR_REF:LR_COL]],
                           axis=0)


def _padded_decay_weights(wd_f, wd_b):
    zeros = lambda n: jnp.zeros((n, KEY_W), F32)
    return (jnp.concatenate([wd_f, zeros(LANE - LOWRANK)], axis=0),
            jnp.concatenate([zeros(LOWRANK), wd_b, zeros(LANE - 2 * LOWRANK)], axis=0))


def kernel(x, norm1_g, w_in,w_decay_f, b_decay_f, w_decay_b, b_decay_b, gla_norm_g, gmlp_ln_g, gmlp_ln_b, w_spatial, b_spatial, w_out, norm2_g, w_gate, w_up, w_down, final_norm_g, loss_target, m_norm1_g, m_w_in, m_w_decay_f, m_b_decay_f, m_w_decay_b, m_b_decay_b, m_gla_norm_g, m_gmlp_ln_g, m_gmlp_ln_b, m_w_spatial, m_b_spatial, m_w_out, m_norm2_g, m_w_gate, m_w_up, m_w_down, m_final_norm_g, v_norm1_g, v_w_in, v_w_decay_f, v_b_decay_f, v_w_decay_b, v_b_decay_b, v_gla_norm_g, v_gmlp_ln_g, v_gmlp_ln_b, v_w_spatial, v_b_spatial, v_w_out, v_norm2_g, v_w_gate, v_w_up, v_w_down, v_final_norm_g):
    t = x.shape[1]
    xt = x[0]
    target = loss_target[0]
    pos_x, pos_y, pos_c = _mesh_pos()
    my_pos = jnp.stack([2 * pos_x + pos_y, pos_c]).astype(jnp.int32)
    my_id = 4 * pos_x + 2 * pos_y + pos_c

    tile = lambda n: min(n, t)
    ln_g, ln_b, w_sp = gmlp_ln_g, gmlp_ln_b, w_spatial[0]
    b_sp_col = b_spatial[0][:, :, None]
    shard = {"w_in": w_in[0].T, "w_out": w_out[0], "w_gate": w_gate[0].T, "w_up": w_up[0].T, "w_down": w_down[0]}
    shard_m = {"w_in": m_w_in[0].T, "w_out": m_w_out[0], "w_gate": m_w_gate[0].T, "w_up": m_w_up[0].T,
               "w_down": m_w_down[0]}
    shard_v = {"w_in": v_w_in[0].T, "w_out": v_w_out[0], "w_gate": v_w_gate[0].T, "w_up": v_w_up[0].T,
               "w_down": v_w_down[0]}
    transposed = ("w_in", "w_gate", "w_up")
    chip_sum = lambda n, g, s: _chip_sum(my_pos, g, s[0], g.shape[2], "chip_sum_" + n)

    decay_shard = jnp.stack([w_decay_f[0], w_decay_b[0]])
    ((g_in, g_decay),) = _comm_only([_gather_comm([shard["w_in"], decay_shard], [True, False])], "all_gather_w_in")
    w_main = _main_proj_weight(g_in.reshape(PROJ_W, D_MODEL))
    wd_pad_f, wd_pad_b = _padded_decay_weights(_unshard_cols(g_decay[:, 0]), _unshard_cols(g_decay[:, 1]))
    (p, hb), ((g_gate,),) = _in_proj(xt, norm1_g, w_main, tile(512), [_gather_comm([shard["w_gate"]], [True])])
    (o_f, st_f), ((g_up, g_out),) = _gla_fwd(p, wd_pad_f, b_decay_f, False, tile(512),
                                             [_gather_comm([shard["w_up"], shard["w_out"]], [True, True])])
    (o_b, st_b), ((g_down,),) = _gla_fwd(p, wd_pad_b, b_decay_b, True, tile(512),
                                         [_gather_comm([shard["w_down"]], [True])])
    w_out_full = g_out.reshape(D_MODEL, D_MODEL)
    x1, ycat = _mix_fwd(xt, o_f, o_b, p, gla_norm_g, ln_g, ln_b, w_sp, b_sp_col, w_out_full, tile(512))

    dx1, h2b, dgate, dup, act, dx2, loss_acc, d_gf, d_g2 = _ffn(
        x1, target, norm2_g, final_norm_g[None, :], g_gate.reshape(D_FF, D_MODEL), g_up.reshape(D_FF, D_MODEL),
        g_down.reshape(D_FF, D_MODEL), tile(256))
    dw_gate, _ = _matmul_tn(dgate, h2b, D_FF // 2, tile(512), "grad_w_gate")
    dw_up, _ = _matmul_tn(dup, h2b, D_FF // 2, tile(512), "grad_w_up")
    dw_down, _ = _matmul_tn(act, dx2, D_FF // 2, tile(512), "grad_w_down")

    ffn_grads = [_row_blocks(dw_gate), _row_blocks(dw_up), _row_blocks(dw_down)]
    (d_o, dpg, dpu, dpv, dw_out, d_gg, d_lg, d_lb, dw_sp, db_sp), (ffn_sib,) = _mix_bwd(
        dx1, ycat, o_f, o_b, p, gla_norm_g, ln_g, ln_b, w_sp, b_sp_col, w_out_full, tile(256),
        [_sibling_exchange_comm(ffn_grads)])
    ffn_names = ["w_gate", "w_up", "w_down"]
    ffn_sums = [chip_sum(n, g, [s]) for n, g, s in zip(ffn_names, ffn_grads, ffn_sib)]
    out_grad = _row_blocks(dw_out)
    (dq_f, dk_f, dv_f, dlr_f, dwd_f, dbd_f), (ffn_recv, out_sib) = _gla_bwd(
        p, wd_pad_f, b_decay_f, st_f, d_o, False, tile(512),
        [_chips_exchange_comm([s[1] for s in ffn_sums]), _sibling_exchange_comm([out_grad])])
    out_sum = chip_sum("w_out", out_grad, out_sib)
    (dq_b, dk_b, dv_b, dlr_b, dwd_b, dbd_b), (out_recv,) = _gla_bwd(
        p, wd_pad_b, b_decay_b, st_b, d_o, True, tile(512), [_chips_exchange_comm([out_sum[1]])])
    grad_x, dp, d_g1 = _in_proj_bwd(xt, norm1_g, dx1, dq_f, dq_b, dk_f, dk_b, dv_f, dv_b, dpg, dpu, dpv,
                                    dlr_f, dlr_b, w_main, tile(256))

    small_shapes = [(1, D_MODEL), (LOWRANK, KEY_W), (1, KEY_W), (LOWRANK, KEY_W), (1, KEY_W), (1, VAL_W),
                    (1, GMLP_W), (1, GMLP_W), (1, GMLP_GROUPS, GMLP_CHUNK, GMLP_CHUNK), (1, GMLP_GROUPS, GMLP_CHUNK),
                    (1, D_MODEL), (D_MODEL,), (LANE,)]
    small_parts = [d_g1, dwd_f[:LOWRANK], dbd_f, dwd_b[LOWRANK:2 * LOWRANK], dbd_b, d_gg, d_lg, d_lb, dw_sp,
                   db_sp, d_g2, d_gf, loss_acc[0]]
    dw_main, ((small_sum,),) = _matmul_tn(dp, hb, PROJ_PAD // 3, tile(512), "grad_w_in",
                                           [_all_reduce_small_comm(_pack_small(small_parts))])
    in_grad = _row_blocks(_main_proj_grad(dw_main))
    (in_sib,) = _comm_only([_sibling_exchange_comm([in_grad])], "grad_w_in_exchange_sibling")
    in_sum = chip_sum("w_in", in_grad, in_sib)
    (in_recv,) = _comm_only([_chips_exchange_comm([in_sum[1]])], "grad_w_in_exchange_chips")

    names = ["w_in", "w_out", "w_gate", "w_up", "w_down"]
    sums = [in_sum, out_sum] + ffn_sums
    received = [in_recv[0], out_recv[0]] + list(ffn_recv)
    big_out = {}
    for n, s, rc in zip(names, sums, received):
        res = _adamw_shard(s[0], rc, shard[n], shard_m[n], shard_v[n], shard[n].shape[0], "adamw_" + n)
        big_out[n] = [r.T if n in transposed else r for r in res]

    reduced = _unpack_small(small_sum, small_shapes)
    loss = reduced[-1][0]
    col0 = my_id * (KEY_W // N_DEV)
    g_small = list(reduced[:-1])
    g_small[1] = lax.dynamic_slice_in_dim(g_small[1], col0, KEY_W // N_DEV, axis=1)[None]
    g_small[3] = lax.dynamic_slice_in_dim(g_small[3], col0, KEY_W // N_DEV, axis=1)[None]
    small_w = [norm1_g, w_decay_f, b_decay_f, w_decay_b, b_decay_b, gla_norm_g, gmlp_ln_g, gmlp_ln_b, w_spatial,
               b_spatial, norm2_g, final_norm_g]
    small_m = [m_norm1_g, m_w_decay_f, m_b_decay_f, m_w_decay_b, m_b_decay_b, m_gla_norm_g, m_gmlp_ln_g,
               m_gmlp_ln_b, m_w_spatial, m_b_spatial, m_norm2_g, m_final_norm_g]
    small_v = [v_norm1_g, v_w_decay_f, v_b_decay_f, v_w_decay_b, v_b_decay_b, v_gla_norm_g, v_gmlp_ln_g,
               v_gmlp_ln_b, v_w_spatial, v_b_spatial, v_norm2_g, v_final_norm_g]
    shapes = [w.shape for w in small_w]
    packed = _adamw_small(_pack_small(g_small), _pack_small(small_w), _pack_small(small_m), _pack_small(small_v))
    s_delta, s_m, s_v = (_unpack_small(a, shapes) for a in packed)
    s_grad = [g.reshape(s) for g, s in zip(g_small, shapes)]

    order = ["norm1_g", "w_in", "w_decay_f", "b_decay_f", "w_decay_b", "b_decay_b", "gla_norm_g", "gmlp_ln_g",
             "gmlp_ln_b", "w_spatial", "b_spatial", "w_out", "norm2_g", "w_gate", "w_up", "w_down", "final_norm_g"]
    small_names = [n for n in order if n not in big_out]
    small_out = {n: (s_grad[i], s_delta[i], s_m[i], s_v[i]) for i, n in enumerate(small_names)}
    outs = []
    for kind in range(4):
        for n in order:
            outs.append(big_out[n][kind][None] if n in big_out else small_out[n][kind])
    return (loss, grad_x[None], *outs)
```

```python
import functools
import math

import jax
import jax.numpy as jnp
from jax import lax
from jax.experimental import pallas as pl
from jax.experimental.pallas import tpu as pltpu

F32 = jnp.float32
BF16 = jnp.bfloat16

D_MODEL = 1024
GLA_HEADS = 4
GLA_DK = 64
GLA_DV = 128
KEY_W = GLA_HEADS * GLA_DK
VAL_W = GLA_HEADS * GLA_DV
LOWRANK = 16
GLA_TAU = 16.0
GLA_CHUNK = 64
GMLP_W = 512
GMLP_GROUPS = 4
GMLP_CHUNK = 128
D_FF = 2816
EPS = 1e-6
Q_SCALE = GLA_DK ** -0.5
PROJ_PAD = 2688
LR_COL = 2560
LANE = 128
N_DEV = 8

ADAM_LR = 0.001
ADAM_B1 = 0.9
ADAM_B2 = 0.999
ADAM_EPS = 1e-08
ADAM_WD = 0.01
ADAM_STEP = 10

VMEM_LIMIT = 56 * 1024 * 1024
MESH_ID = pl.DeviceIdType.MESH
INV_SQRT2 = 0.7071067811865476
INV_SQRT_2PI = 0.3989422804014327


def _params(n_axes=1):
    return pltpu.CompilerParams(dimension_semantics=("arbitrary",) * n_axes, vmem_limit_bytes=VMEM_LIMIT)


def _mm(a, b):
    return jnp.dot(a.astype(BF16), b.astype(BF16), preferred_element_type=F32)


def _mm_nt(a, b):
    return lax.dot_general(a.astype(BF16), b.astype(BF16), (((1,), (1,)), ((), ())), preferred_element_type=F32)


def _mm_tn(a, b):
    return lax.dot_general(a.astype(BF16), b.astype(BF16), (((0,), (0,)), ((), ())), preferred_element_type=F32)


def _const_spec(shape):
    nd = len(shape)
    return pl.BlockSpec(shape, lambda *_: (0,) * nd, pipeline_mode=pl.Buffered(1))


def _acc_spec(shape):
    nd = len(shape)
    return pl.BlockSpec(shape, lambda *_: (0,) * nd)


class _Comm:
    def __init__(self, inputs, in_specs, out_shape, out_specs, scratch_shapes, before, after):
        self.inputs, self.in_specs, self.out_shape, self.out_specs = inputs, in_specs, out_shape, out_specs
        self.scratch_shapes, self.before, self.after = scratch_shapes, before, after


def _fused_call(body, comms, *, name, grid, inputs, in_specs, out_specs, out_shape, scratch_shapes=()):
    n_in, n_out, n_scr = len(in_specs), len(out_specs), len(scratch_shapes)
    nsteps = math.prod(grid)
    sizes = [(len(c.inputs), len(c.out_shape), len(c.scratch_shapes)) for c in comms]

    def full_body(*refs):
        step = pl.program_id(0)
        for axis in range(1, len(grid)):
            step = step * grid[axis] + pl.program_id(axis)
        ins, rest = refs[:n_in], refs[n_in:]
        c_ins = []
        for ci, _, _ in sizes:
            c_ins.append(rest[:ci])
            rest = rest[ci:]
        outs, rest = rest[:n_out], rest[n_out:]
        c_outs = []
        for _, co, _ in sizes:
            c_outs.append(rest[:co])
            rest = rest[co:]
        scr, rest = rest[:n_scr], rest[n_scr:]
        c_scr = []
        for _, _, cs in sizes:
            c_scr.append(rest[:cs])
            rest = rest[cs:]
        for c, a, b, s in zip(comms, c_ins, c_outs, c_scr):
            c.before(step, nsteps, a, b, s)
        body(*ins, *outs, *scr)
        for c, a, b, s in zip(comms, c_ins, c_outs, c_scr):
            c.after(step, nsteps, a, b, s)

    results = pl.pallas_call(
        full_body, name=name, grid=grid,
        in_specs=list(in_specs) + [s for c in comms for s in c.in_specs],
        out_specs=tuple(out_specs) + tuple(s for c in comms for s in c.out_specs),
        out_shape=tuple(out_shape) + tuple(s for c in comms for s in c.out_shape),
        scratch_shapes=list(scratch_shapes) + [s for c in comms for s in c.scratch_shapes],
        compiler_params=_params(len(grid)),
    )(*inputs, *[a for c in comms for a in c.inputs])
    own, rest = results[:n_out], results[n_out:]
    comm_results = []
    for _, co, _ in sizes:
        comm_results.append(rest[:co])
        rest = rest[co:]
    return own, comm_results


def _gelu(x):
    return 0.5 * x * (1.0 + lax.erf(x * INV_SQRT2))


def _gelu_grad(x):
    return 0.5 * (1.0 + lax.erf(x * INV_SQRT2)) + x * jnp.exp(-0.5 * x * x) * INV_SQRT_2PI


def _silu_and_grad(x):
    s = jax.nn.sigmoid(x)
    return x * s, s * (1.0 + x * (1.0 - s))


def _in_proj(x, g1, w_main, tm, comms=()):
    t = x.shape[0]

    def body(x_ref, g_ref, w_ref, p_ref, h_ref):
        xv = x_ref[...]
        r = lax.rsqrt(jnp.mean(xv * xv, axis=-1, keepdims=True) + EPS)
        h = (xv * r * g_ref[...]).astype(BF16)
        h_ref[...] = h
        p_ref[...] = _mm_nt(h, w_ref[...]).astype(BF16)

    return _fused_call(
        body, comms, name="in_proj", grid=(t // tm,), inputs=(x, g1, w_main),
        in_specs=[pl.BlockSpec((tm, D_MODEL), lambda i: (i, 0)), _const_spec((1, D_MODEL)),
                  _const_spec((PROJ_PAD, D_MODEL))],
        out_specs=(pl.BlockSpec((tm, PROJ_PAD), lambda i: (i, 0)), pl.BlockSpec((tm, D_MODEL), lambda i: (i, 0))),
        out_shape=(jax.ShapeDtypeStruct((t, PROJ_PAD), BF16), jax.ShapeDtypeStruct((t, D_MODEL), BF16)))


def _tri(upper):
    r = lax.broadcasted_iota(jnp.int32, (GLA_CHUNK, GLA_CHUNK), 0)
    c = lax.broadcasted_iota(jnp.int32, (GLA_CHUNK, GLA_CHUNK), 1)
    return jnp.where((c >= r) if upper else (c <= r), 1.0, 0.0).astype(BF16)


def _tri_matmul(tri, a):
    a1 = a.astype(BF16)
    r1 = a - a1.astype(F32)
    a2 = r1.astype(BF16)
    a3 = (r1 - a2.astype(F32)).astype(BF16)
    dot = functools.partial(jnp.dot, preferred_element_type=F32)
    return dot(tri, a1) + dot(tri, a2) + dot(tri, a3)


def _gla_masks(rev):
    dk_bits, dv_bits = GLA_DK.bit_length() - 1, GLA_DV.bit_length() - 1
    key_head = lax.broadcasted_iota(jnp.int32, (GLA_CHUNK, KEY_W), 1) >> dk_bits
    val_head = lax.broadcasted_iota(jnp.int32, (GLA_CHUNK, VAL_W), 1) >> dv_bits
    t = lax.broadcasted_iota(jnp.int32, (GLA_HEADS * GLA_CHUNK, GLA_CHUNK), 0) & (GLA_CHUNK - 1)
    s = lax.broadcasted_iota(jnp.int32, (GLA_HEADS * GLA_CHUNK, GLA_CHUNK), 1)
    causal = (s >= t) if rev else (s <= t)
    blockdiag = (lax.broadcasted_iota(jnp.int32, (VAL_W, KEY_W), 0) >> dv_bits
                 == lax.broadcasted_iota(jnp.int32, (VAL_W, KEY_W), 1) >> dk_bits)
    return key_head, val_head, causal, blockdiag


def _stack_heads(a, head_of_lane):
    return jnp.concatenate([jnp.where(head_of_lane == h, a, 0.0) for h in range(GLA_HEADS)], axis=0)


def _chunk_terms(la_c, q_c, k_c, tri, rev):
    q_c, k_c = q_c.astype(F32), k_c.astype(F32)
    b = _tri_matmul(tri, la_c)
    bl = b[0:1] if rev else b[GLA_CHUNK - 1:GLA_CHUNK]
    eb = jnp.exp(b)
    enb = jnp.exp(-b)
    ee = jnp.exp(bl - b)
    return bl, eb, enb, ee, q_c * Q_SCALE * eb, k_c * enb, k_c * ee


def _log_decay(lr_ref, wd_ref, bd_ref):
    z = _mm(lr_ref[...], wd_ref[...]) + bd_ref[...]
    return z, jax.nn.log_sigmoid(z) * (1.0 / GLA_TAU)


def _gla_fwd(p, wd_pad, bd, rev, tg, comms=()):
    t = p.shape[0]
    nt = t // tg
    nc = tg // GLA_CHUNK
    tile = (lambda i: nt - 1 - i) if rev else (lambda i: i)

    def body(q_ref, k_ref, v_ref, lr_ref, wd_ref, bd_ref, o_ref, st_ref, state):
        @pl.when(pl.program_id(0) == 0)
        def _():
            state[...] = jnp.zeros_like(state)

        key_head, _, causal, blockdiag = _gla_masks(rev)
        tri = _tri(rev)
        _, la = _log_decay(lr_ref, wd_ref, bd_ref)
        for cc in range(nc):
            c = nc - 1 - cc if rev else cc
            rows = slice(c * GLA_CHUNK, (c + 1) * GLA_CHUNK)
            v_c = v_ref[rows, :].astype(BF16)
            bl, _, _, _, qd, kd, ke = _chunk_terms(la[rows], q_ref[rows, :], k_ref[rows, :], tri, rev)
            a_all = jnp.where(causal, _mm_nt(_stack_heads(qd, key_head), kd), 0.0)
            r = _mm(a_all, v_c)
            o_intra = jnp.concatenate(
                [r[h * GLA_CHUNK:(h + 1) * GLA_CHUNK, h * GLA_DV:(h + 1) * GLA_DV] for h in range(GLA_HEADS)], axis=1)
            st = state[...]
            o_ref[rows, :] = o_intra + _mm_nt(qd, st)
            st_ref[c] = st.astype(BF16)
            state[...] = st * jnp.exp(bl) + jnp.where(blockdiag, _mm_tn(v_c, ke), 0.0)

    return _fused_call(
        body, comms, name="gla_fwd_rev" if rev else "gla_fwd", grid=(nt,), inputs=(p, p, p, p, wd_pad, bd),
        in_specs=[pl.BlockSpec((tg, KEY_W), lambda i: (tile(i), 0)),
                  pl.BlockSpec((tg, KEY_W), lambda i: (tile(i), 1)),
                  pl.BlockSpec((tg, VAL_W), lambda i: (tile(i), 1)),
                  pl.BlockSpec((tg, LANE), lambda i: (tile(i), LR_COL // LANE)),
                  _const_spec((LANE, KEY_W)), _const_spec((1, KEY_W))],
        out_specs=(pl.BlockSpec((tg, VAL_W), lambda i: (tile(i), 0)),
                   pl.BlockSpec((nc, VAL_W, KEY_W), lambda i: (tile(i), 0, 0))),
        out_shape=(jax.ShapeDtypeStruct((t, VAL_W), F32),
                   jax.ShapeDtypeStruct((t // GLA_CHUNK, VAL_W, KEY_W), BF16)),
        scratch_shapes=[pltpu.VMEM((VAL_W, KEY_W), F32)])


def _gla_bwd(p, wd_pad, bd, states, d_o, rev, tg, comms=()):
    t = p.shape[0]
    nt = t // tg
    nc = tg // GLA_CHUNK
    tile = (lambda i: i) if rev else (lambda i: nt - 1 - i)

    def body(q_ref, k_ref, v_ref, lr_ref, wd_ref, bd_ref, st_ref, do_ref,
             dq_ref, dk_ref, dv_ref, dlr_ref, dwd_ref, dbd_ref, dstate, dz_scr):
        @pl.when(pl.program_id(0) == 0)
        def _():
            dstate[...] = jnp.zeros_like(dstate)
            dwd_ref[...] = jnp.zeros_like(dwd_ref)
            dbd_ref[...] = jnp.zeros_like(dbd_ref)

        key_head, val_head, causal, blockdiag = _gla_masks(rev)
        tri = _tri(rev)
        tri_t = _tri(not rev)
        z, la = _log_decay(lr_ref, wd_ref, bd_ref)
        dlog = jax.nn.sigmoid(-z) * (1.0 / GLA_TAU)
        for cc in range(nc):
            c = cc if rev else nc - 1 - cc
            rows = slice(c * GLA_CHUNK, (c + 1) * GLA_CHUNK)
            v_c = v_ref[rows, :].astype(BF16)
            do_c = do_ref[rows, :]
            bl, eb, enb, ee, qd, kd, ke = _chunk_terms(la[rows], q_ref[rows, :], k_ref[rows, :], tri, rev)
            qd_stack = _stack_heads(qd, key_head)
            do_stack = _stack_heads(do_c, val_head)
            a_all = jnp.where(causal, _mm_nt(qd_stack, kd), 0.0)
            da_all = jnp.where(causal, _mm_nt(do_stack, v_c), 0.0)
            dst = dstate[...]
            st_prev = st_ref[c]
            dv_ref[rows, :] = (_mm_tn(a_all, do_stack) + _mm_nt(ke, dst)).astype(BF16)
            r2 = _mm(da_all, kd)
            dqd = _mm(do_c, st_prev)
            for h in range(GLA_HEADS):
                dqd = dqd + jnp.where(key_head == h, r2[h * GLA_CHUNK:(h + 1) * GLA_CHUNK, :], 0.0)
            dkd = _mm_tn(da_all, qd_stack)
            dke = _mm(v_c, dst)
            ebl = jnp.exp(bl)
            dbl = (jnp.sum(dst * st_prev.astype(F32), axis=0, keepdims=True) * ebl
                   + jnp.sum(dke * ke, axis=0, keepdims=True))
            dstate[...] = dst * ebl + jnp.where(blockdiag, _mm_tn(do_c, qd), 0.0)
            dq_ref[rows, :] = (dqd * eb * Q_SCALE).astype(BF16)
            dk_ref[rows, :] = (dkd * enb + dke * ee).astype(BF16)
            db = dqd * qd - dkd * kd - dke * ke
            dz_scr[rows, :] = (_tri_matmul(tri_t, db) + dbl) * dlog[rows]
        dz = dz_scr[...]
        dlr_ref[...] = _mm_nt(dz, wd_ref[...]).astype(BF16)
        dwd_ref[...] += _mm_tn(lr_ref[...], dz)
        dbd_ref[...] += jnp.sum(dz, axis=0, keepdims=True)

    return _fused_call(
        body, comms, name="gla_bwd_rev" if rev else "gla_bwd", grid=(nt,),
        inputs=(p, p, p, p, wd_pad, bd, states, d_o),
        in_specs=[pl.BlockSpec((tg, KEY_W), lambda i: (tile(i), 0)),
                  pl.BlockSpec((tg, KEY_W), lambda i: (tile(i), 1)),
                  pl.BlockSpec((tg, VAL_W), lambda i: (tile(i), 1)),
                  pl.BlockSpec((tg, LANE), lambda i: (tile(i), LR_COL // LANE)),
                  _const_spec((LANE, KEY_W)), _const_spec((1, KEY_W)),
                  pl.BlockSpec((nc, VAL_W, KEY_W), lambda i: (tile(i), 0, 0)),
                  pl.BlockSpec((tg, VAL_W), lambda i: (tile(i), 0))],
        out_specs=(pl.BlockSpec((tg, KEY_W), lambda i: (tile(i), 0)),
                   pl.BlockSpec((tg, KEY_W), lambda i: (tile(i), 0)),
                   pl.BlockSpec((tg, VAL_W), lambda i: (tile(i), 0)),
                   pl.BlockSpec((tg, LANE), lambda i: (tile(i), 0)),
                   _acc_spec((LANE, KEY_W)), _acc_spec((1, KEY_W))),
        out_shape=(jax.ShapeDtypeStruct((t, KEY_W), BF16), jax.ShapeDtypeStruct((t, KEY_W), BF16),
                   jax.ShapeDtypeStruct((t, VAL_W), BF16), jax.ShapeDtypeStruct((t, LANE), BF16),
                   jax.ShapeDtypeStruct((LANE, KEY_W), F32), jax.ShapeDtypeStruct((1, KEY_W), F32)),
        scratch_shapes=[pltpu.VMEM((VAL_W, KEY_W), F32), pltpu.VMEM((tg, KEY_W), F32)])


def _head_rms(o):
    parts, scales = [], []
    for h in range(GLA_HEADS):
        oh = o[:, h * GLA_DV:(h + 1) * GLA_DV]
        r = lax.rsqrt(jnp.mean(oh * oh, axis=-1, keepdims=True) + EPS)
        parts.append(oh * r)
        scales.append(jnp.broadcast_to(r, oh.shape))
    return jnp.concatenate(parts, axis=1), jnp.concatenate(scales, axis=1)


def _layernorm_stats(zv):
    mu = jnp.mean(zv, axis=-1, keepdims=True)
    xc = zv - mu
    rs = lax.rsqrt(jnp.mean(xc * xc, axis=-1, keepdims=True) + EPS)
    return xc * rs, rs


def _mix_fwd(x, o_f, o_b, p, gla_g, ln_g, ln_b, w_sp, b_sp, w_out, tm):
    t = x.shape[0]
    nch = tm // GMLP_CHUNK

    def body(x_ref, of_ref, ob_ref, pg_ref, pu_ref, pv_ref, gg_ref, lg_ref, lb_ref, ws_ref, bs_ref, wo_ref,
             x1_ref, y_ref, s_scr):
        on, _ = _head_rms(of_ref[...] + ob_ref[...])
        pg = pg_ref[...].astype(F32)
        y_a = on * gg_ref[...] * (pg * jax.nn.sigmoid(pg))
        zu = _gelu(pu_ref[...].astype(F32))
        vhat, _ = _layernorm_stats(_gelu(pv_ref[...].astype(F32)))
        vln = (vhat * lg_ref[...] + lb_ref[...]).astype(BF16)
        for g in range(GMLP_GROUPS):
            w_g = ws_ref[g].astype(BF16)
            b_g = bs_ref[g]
            cols = slice(g * LANE, (g + 1) * LANE)
            for n in range(nch):
                rows = slice(n * GMLP_CHUNK, (n + 1) * GMLP_CHUNK)
                s_scr[rows, cols] = jnp.dot(w_g, vln[rows, cols], preferred_element_type=F32) + b_g
        ycat = jnp.concatenate([y_a, zu * s_scr[...]], axis=1).astype(BF16)
        y_ref[...] = ycat
        x1_ref[...] = x_ref[...] + jnp.dot(ycat, wo_ref[...], preferred_element_type=F32)

    half = lambda j: pl.BlockSpec((tm, VAL_W), lambda i: (i, j))
    return pl.pallas_call(
        body, name="mix_fwd", grid=(t // tm,),
        in_specs=[pl.BlockSpec((tm, D_MODEL), lambda i: (i, 0)), half(0), half(0), half(2), half(3), half(4),
                  _const_spec((1, VAL_W)), _const_spec((1, GMLP_W)), _const_spec((1, GMLP_W)),
                  _const_spec((GMLP_GROUPS, GMLP_CHUNK, GMLP_CHUNK)), _const_spec((GMLP_GROUPS, GMLP_CHUNK, 1)),
                  _const_spec((D_MODEL, D_MODEL))],
        out_specs=(pl.BlockSpec((tm, D_MODEL), lambda i: (i, 0)), pl.BlockSpec((tm, D_MODEL), lambda i: (i, 0))),
        out_shape=(jax.ShapeDtypeStruct((t, D_MODEL), F32), jax.ShapeDtypeStruct((t, D_MODEL), BF16)),
        scratch_shapes=[pltpu.VMEM((tm, GMLP_W), F32)],
        compiler_params=_params(),
    )(x, o_f, o_b, p, p, p, gla_g, ln_g, ln_b, w_sp, b_sp, w_out)


def _mix_bwd(dx1, ycat, o_f, o_b, p, gla_g, ln_g, ln_b, w_sp, b_sp, w_out, tm, comms=()):
    t = dx1.shape[0]
    nch = tm // GMLP_CHUNK

    def body(dx1_ref, y_ref, of_ref, ob_ref, pg_ref, pu_ref, pv_ref, gg_ref, lg_ref, lb_ref, ws_ref, bs_ref, wo_ref,
             do_ref, dpg_ref, dpu_ref, dpv_ref, dwo_ref, dgg_ref, dlg_ref, dlb_ref, dws_ref, dbs_ref,
             s_scr, dvln_scr):
        @pl.when(pl.program_id(0) == 0)
        def _():
            for ref in (dwo_ref, dgg_ref, dlg_ref, dlb_ref, dws_ref, dbs_ref):
                ref[...] = jnp.zeros_like(ref)

        dx1 = dx1_ref[...].astype(BF16)
        dycat = _mm_nt(dx1, wo_ref[...])
        dwo_ref[...] += _mm_tn(y_ref[...], dx1)
        dy_a = dycat[:, :VAL_W]
        dy_b = dycat[:, VAL_W:]
        on, r = _head_rms(of_ref[...] + ob_ref[...])
        pg = pg_ref[...].astype(F32)
        sil, dsil = _silu_and_grad(pg)
        gg = gg_ref[...]
        dgg_ref[...] += jnp.sum(dy_a * sil * on, axis=0, keepdims=True)
        don = dy_a * sil * gg
        prod = don * on
        means = jnp.concatenate(
            [jnp.broadcast_to(jnp.mean(prod[:, h * GLA_DV:(h + 1) * GLA_DV], axis=-1, keepdims=True),
                              (tm, GLA_DV)) for h in range(GLA_HEADS)], axis=1)
        do_ref[...] = (r * (don - on * means)).astype(BF16)
        dpg_ref[...] = (dy_a * on * gg * dsil).astype(BF16)
        pu = pu_ref[...].astype(F32)
        pv = pv_ref[...].astype(F32)
        zu = _gelu(pu)
        vhat, rs = _layernorm_stats(_gelu(pv))
        lg = lg_ref[...]
        vln = (vhat * lg + lb_ref[...]).astype(BF16)
        ds32 = dy_b * zu
        ds = ds32.astype(BF16)
        for g in range(GMLP_GROUPS):
            w_g = ws_ref[g].astype(BF16)
            b_g = bs_ref[g]
            cols = slice(g * LANE, (g + 1) * LANE)
            dw_g = jnp.zeros((GMLP_CHUNK, GMLP_CHUNK), F32)
            db_g = jnp.zeros((GMLP_CHUNK, 1), F32)
            for n in range(nch):
                rows = slice(n * GMLP_CHUNK, (n + 1) * GMLP_CHUNK)
                v_blk = vln[rows, cols]
                ds_blk = ds[rows, cols]
                s_scr[rows, cols] = jnp.dot(w_g, v_blk, preferred_element_type=F32) + b_g
                dw_g = dw_g + _mm_nt(ds_blk, v_blk)
                db_g = db_g + jnp.sum(ds32[rows, cols], axis=-1, keepdims=True)
                dvln_scr[rows, cols] = _mm_tn(w_g, ds_blk)
            dws_ref[g] += dw_g
            dbs_ref[g] += db_g
        dpu_ref[...] = (dy_b * s_scr[...] * _gelu_grad(pu)).astype(BF16)
        dvln = dvln_scr[...]
        dlg_ref[...] += jnp.sum(dvln * vhat, axis=0, keepdims=True)
        dlb_ref[...] += jnp.sum(dvln, axis=0, keepdims=True)
        dvhat = dvln * lg
        dzv = rs * (dvhat - jnp.mean(dvhat, axis=-1, keepdims=True)
                    - vhat * jnp.mean(dvhat * vhat, axis=-1, keepdims=True))
        dpv_ref[...] = (dzv * _gelu_grad(pv)).astype(BF16)

    half = lambda j: pl.BlockSpec((tm, VAL_W), lambda i: (i, j))
    full = pl.BlockSpec((tm, D_MODEL), lambda i: (i, 0))
    sp_shape = (GMLP_GROUPS, GMLP_CHUNK, GMLP_CHUNK)
    bs_shape = (GMLP_GROUPS, GMLP_CHUNK, 1)
    return _fused_call(
        body, comms, name="mix_bwd", grid=(t // tm,),
        inputs=(dx1, ycat, o_f, o_b, p, p, p, gla_g, ln_g, ln_b, w_sp, b_sp, w_out),
        in_specs=[full, full, half(0), half(0), half(2), half(3), half(4),
                  _const_spec((1, VAL_W)), _const_spec((1, GMLP_W)), _const_spec((1, GMLP_W)),
                  _const_spec(sp_shape), _const_spec(bs_shape), _const_spec((D_MODEL, D_MODEL))],
        out_specs=(half(0), half(0), half(0), half(0), _acc_spec((D_MODEL, D_MODEL)), _acc_spec((1, VAL_W)),
                   _acc_spec((1, GMLP_W)), _acc_spec((1, GMLP_W)), _acc_spec(sp_shape), _acc_spec(bs_shape)),
        out_shape=(jax.ShapeDtypeStruct((t, VAL_W), BF16),) * 4 + (
            jax.ShapeDtypeStruct((D_MODEL, D_MODEL), F32), jax.ShapeDtypeStruct((1, VAL_W), F32),
            jax.ShapeDtypeStruct((1, GMLP_W), F32), jax.ShapeDtypeStruct((1, GMLP_W), F32),
            jax.ShapeDtypeStruct(sp_shape, F32), jax.ShapeDtypeStruct(bs_shape, F32)),
        scratch_shapes=[pltpu.VMEM((tm, GMLP_W), F32), pltpu.VMEM((tm, GMLP_W), F32)])


def _rms_bwd(dy_scaled, xn, r):
    return r * (dy_scaled - xn * jnp.mean(dy_scaled * xn, axis=-1, keepdims=True))


def _ffn(x1, target, g2, gf, w_gate, w_up, w_down, tm):
    t = x1.shape[0]

    def body(x1_ref, tg_ref, g2_ref, gf_ref, wg_ref, wu_ref, wd_ref,
             dx1_ref, h2_ref, dgate_ref, dup_ref, act_ref, dx2_ref, loss_ref, dgf_ref, dg2_ref):
        @pl.when(pl.program_id(0) == 0)
        def _():
            for ref in (loss_ref, dgf_ref, dg2_ref):
                ref[...] = jnp.zeros_like(ref)

        x1v = x1_ref[...]
        g2v = g2_ref[...]
        gfv = gf_ref[...]
        r2 = lax.rsqrt(jnp.mean(x1v * x1v, axis=-1, keepdims=True) + EPS)
        xn1 = x1v * r2
        h2 = (xn1 * g2v).astype(BF16)
        h2_ref[...] = h2
        gate = _mm_nt(h2, wg_ref[...])
        up = _mm_nt(h2, wu_ref[...])
        sil, dsil = _silu_and_grad(gate)
        act = (sil * up).astype(BF16)
        act_ref[...] = act
        x2 = x1v + jnp.dot(act, wd_ref[...], preferred_element_type=F32)
        rf = lax.rsqrt(jnp.mean(x2 * x2, axis=-1, keepdims=True) + EPS)
        xn2 = x2 * rf
        err = xn2 * gfv - tg_ref[...]
        loss_ref[...] += 0.5 * jnp.sum(jnp.mean(err * err, axis=-1, keepdims=True))
        dy = err * (1.0 / D_MODEL)
        dgf_ref[...] += jnp.sum(dy * xn2, axis=0, keepdims=True)
        dx2 = _rms_bwd(dy * gfv, xn2, rf)
        dx2b = dx2.astype(BF16)
        dx2_ref[...] = dx2b
        dact = _mm_nt(dx2b, wd_ref[...])
        dgate = (dact * up * dsil).astype(BF16)
        dup = (dact * sil).astype(BF16)
        dgate_ref[...] = dgate
        dup_ref[...] = dup
        dh2 = _mm(dgate, wg_ref[...]) + _mm(dup, wu_ref[...])
        dg2_ref[...] += jnp.sum(dh2 * xn1, axis=0, keepdims=True)
        dx1_ref[...] = dx2 + _rms_bwd(dh2 * g2v, xn1, r2)

    row = lambda w: pl.BlockSpec((tm, w), lambda i: (i, 0))
    return pl.pallas_call(
        body, name="ffn_fwd_bwd", grid=(t // tm,),
        in_specs=[row(D_MODEL), row(D_MODEL), _const_spec((1, D_MODEL)), _const_spec((1, D_MODEL)),
                  _const_spec((D_FF, D_MODEL)), _const_spec((D_FF, D_MODEL)), _const_spec((D_FF, D_MODEL))],
        out_specs=(row(D_MODEL), row(D_MODEL), row(D_FF), row(D_FF), row(D_FF), row(D_MODEL),
                   _acc_spec((8, LANE)), _acc_spec((1, D_MODEL)), _acc_spec((1, D_MODEL))),
        out_shape=(jax.ShapeDtypeStruct((t, D_MODEL), F32), jax.ShapeDtypeStruct((t, D_MODEL), BF16),
                   jax.ShapeDtypeStruct((t, D_FF), BF16), jax.ShapeDtypeStruct((t, D_FF), BF16),
                   jax.ShapeDtypeStruct((t, D_FF), BF16), jax.ShapeDtypeStruct((t, D_MODEL), BF16),
                   jax.ShapeDtypeStruct((8, LANE), F32), jax.ShapeDtypeStruct((1, D_MODEL), F32),
                   jax.ShapeDtypeStruct((1, D_MODEL), F32)),
        compiler_params=_params(),
    )(x1, target, g2, gf, w_gate, w_up, w_down)


def _matmul_tn(a, b, tm, tk, name, comms=()):
    t, m = a.shape
    n = b.shape[1]

    def body(a_ref, b_ref, o_ref):
        @pl.when(pl.program_id(1) == 0)
        def _():
            o_ref[...] = jnp.zeros_like(o_ref)

        o_ref[...] += _mm_tn(a_ref[...], b_ref[...])

    (out,), comm_results = _fused_call(
        body, comms, name=name, grid=(m // tm, t // tk), inputs=(a, b),
        in_specs=[pl.BlockSpec((tk, tm), lambda j, k: (k, j)), pl.BlockSpec((tk, n), lambda j, k: (k, 0))],
        out_specs=(pl.BlockSpec((tm, n), lambda j, k: (j, 0)),),
        out_shape=(jax.ShapeDtypeStruct((m, n), F32),))
    return out, comm_results


def _in_proj_bwd(x, g1, dx1, dq_f, dq_b, dk_f, dk_b, dv_f, dv_b, dpg, dpu, dpv, dlr_f, dlr_b, w_main, tm):
    t = x.shape[0]

    def body(x_ref, g_ref, dx1_ref, dqf, dqb, dkf, dkb, dvf, dvb, dg, du, dv, dlf, dlb, w_ref,
             dx_ref, dp_ref, dg1_ref):
        @pl.when(pl.program_id(0) == 0)
        def _():
            dg1_ref[...] = jnp.zeros_like(dg1_ref)

        both = lambda a, b: (a[...].astype(F32) + b[...].astype(F32)).astype(BF16)
        dp = jnp.concatenate([both(dqf, dqb), both(dkf, dkb), both(dvf, dvb), dg[...], du[...], dv[...],
                              both(dlf, dlb)], axis=1)
        dp_ref[...] = dp
        dh = _mm(dp, w_ref[...])
        xv = x_ref[...]
        r = lax.rsqrt(jnp.mean(xv * xv, axis=-1, keepdims=True) + EPS)
        xn = xv * r
        dg1_ref[...] += jnp.sum(dh * xn, axis=0, keepdims=True)
        dx_ref[...] = dx1_ref[...] + _rms_bwd(dh * g_ref[...], xn, r)

    row = lambda w: pl.BlockSpec((tm, w), lambda i: (i, 0))
    return pl.pallas_call(
        body, name="in_proj_bwd", grid=(t // tm,),
        in_specs=[row(D_MODEL), _const_spec((1, D_MODEL)), row(D_MODEL), row(KEY_W), row(KEY_W), row(KEY_W),
                  row(KEY_W), row(VAL_W), row(VAL_W), row(VAL_W), row(VAL_W), row(VAL_W), row(LANE), row(LANE),
                  _const_spec((PROJ_PAD, D_MODEL))],
        out_specs=(row(D_MODEL), row(PROJ_PAD), _acc_spec((1, D_MODEL))),
        out_shape=(jax.ShapeDtypeStruct((t, D_MODEL), F32), jax.ShapeDtypeStruct((t, PROJ_PAD), BF16),
                   jax.ShapeDtypeStruct((1, D_MODEL), F32)),
        compiler_params=_params(),
    )(x, g1, dx1, dq_f, dq_b, dk_f, dk_b, dv_f, dv_b, dpg, dpu, dpv, dlr_f, dlr_b, w_main)


def _adamw(w, g, m, v):
    m_new = ADAM_B1 * m + (1.0 - ADAM_B1) * g
    v_new = ADAM_B2 * v + (1.0 - ADAM_B2) * (g * g)
    m_hat = m_new / (1.0 - ADAM_B1 ** ADAM_STEP)
    v_hat = v_new / (1.0 - ADAM_B2 ** ADAM_STEP)
    delta = -ADAM_LR * (m_hat / (jnp.sqrt(v_hat) + ADAM_EPS) + ADAM_WD * w)
    return delta, m_new, v_new


def _adamw_shard(own, recv, w, m, v, tr, name):
    r, c = w.shape

    def body(own_ref, recv_ref, w_ref, m_ref, v_ref, g_ref, d_ref, nm_ref, nv_ref):
        g = own_ref[...]
        for k in range(3):
            g = g + recv_ref[k].astype(F32)
        g_ref[...] = g
        d_ref[...], nm_ref[...], nv_ref[...] = _adamw(w_ref[...], g, m_ref[...], v_ref[...])

    row = pl.BlockSpec((tr, c), lambda i: (i, 0))
    return pl.pallas_call(
        body, name=name, grid=(r // tr,),
        in_specs=[row, pl.BlockSpec((3, tr, c), lambda i: (0, i, 0)), row, row, row],
        out_specs=(row,) * 4, out_shape=(jax.ShapeDtypeStruct((r, c), F32),) * 4,
        compiler_params=_params(),
    )(own, recv, w, m, v)


def _adamw_small(g, w, m, v):
    def body(g_ref, w_ref, m_ref, v_ref, d_ref, nm_ref, nv_ref):
        d_ref[...], nm_ref[...], nv_ref[...] = _adamw(w_ref[...], g_ref[...], m_ref[...], v_ref[...])

    return pl.pallas_call(body, name="adamw_small", out_shape=(jax.ShapeDtypeStruct(g.shape, F32),) * 3,
                          compiler_params=pltpu.CompilerParams(vmem_limit_bytes=VMEM_LIMIT))(g, w, m, v)


def _mesh_pos():
    return lax.axis_index("x"), lax.axis_index("y"), lax.axis_index("c")


def _other_chips(x, y):
    return [(x, 1 - y), (1 - x, y), (1 - x, 1 - y)]


_VMEM_WHOLE = pl.BlockSpec(memory_space=pltpu.VMEM)
_HBM_WHOLE = pl.BlockSpec(memory_space=pl.ANY)


def _gather_comm(shards, cast, mid=(1, 2)):
    na = len(shards)
    staged = [a for a in range(na) if cast[a]]

    def phases(in_refs, out_refs, scr):
        stage = dict(zip(staged, scr[:len(staged)]))
        send_sems, recv_sems, local_sems = scr[len(staged):]
        x, y, c = _mesh_pos()
        me, sibling = (x, y, c), (x, y, 1 - c)
        chips = _other_chips(x, y)
        srcs = [stage[a] if cast[a] else in_refs[a] for a in range(na)]

        def rows(a, pos):
            px, py, pc = pos
            return out_refs[a].at[4 * px + 2 * py + pc]

        def copy(a, k, block, to, src=None):
            return pltpu.make_async_remote_copy(
                src_ref=rows(a, block) if src is None else src, dst_ref=rows(a, block),
                send_sem=send_sems.at[a, k], recv_sem=recv_sems.at[a, k], device_id=to, device_id_type=MESH_ID)

        mine = [pltpu.make_async_copy(srcs[a], rows(a, me), local_sems.at[a]) for a in range(na)]
        first = []
        for a in range(na):
            first.append(copy(a, 0, me, sibling, src=srcs[a]))
            first += [copy(a, 1 + j, me, (*chip, c), src=srcs[a]) for j, chip in enumerate(chips)]
        passed = [copy(a, 4 + j, (*chip, c), sibling) for j, chip in enumerate(chips) for a in range(na)]

        def start():
            for a in staged:
                stage[a][...] = in_refs[a][...].astype(BF16)
            for cp in mine + first:
                cp.start()

        def forward():
            i = 0
            for j, chip in enumerate(chips):
                for a in range(na):
                    copy(a, 1 + j, (*chip, c), me).wait_recv()
                    passed[i].start()
                    i += 1

        def finish():
            for a in range(na):
                copy(a, 0, sibling, me).wait_recv()
                for j, chip in enumerate(chips):
                    copy(a, 4 + j, (*chip, 1 - c), me).wait_recv()
            for cp in first + passed:
                cp.wait_send()
            for cp in mine:
                cp.wait()

        return start, forward, finish

    def before(step, nsteps, in_refs, out_refs, scr):
        start, forward, _ = phases(in_refs, out_refs, scr)
        pl.when(step == 0)(start)
        pl.when(step == nsteps * mid[0] // mid[1])(forward)

    def after(step, nsteps, in_refs, out_refs, scr):
        pl.when(step == nsteps - 1)(phases(in_refs, out_refs, scr)[2])

    return _Comm(
        inputs=list(shards), in_specs=[_VMEM_WHOLE] * na,
        out_shape=[jax.ShapeDtypeStruct((N_DEV,) + s.shape, BF16 if cast[a] else s.dtype)
                   for a, s in enumerate(shards)],
        out_specs=[_HBM_WHOLE] * na,
        scratch_shapes=[pltpu.VMEM(shards[a].shape, BF16) for a in staged] + [
            pltpu.SemaphoreType.DMA((na, 7)), pltpu.SemaphoreType.DMA((na, 7)), pltpu.SemaphoreType.DMA((na,))],
        before=before, after=after)


def _exchange_comm(arrays, out_shape, make_copies):
    na = len(arrays)

    def copies(in_refs, out_refs, scr):
        return make_copies(in_refs, out_refs, *scr)

    def before(step, nsteps, in_refs, out_refs, scr):
        @pl.when(step == 0)
        def _():
            for cp in copies(in_refs, out_refs, scr):
                cp.start()

    def after(step, nsteps, in_refs, out_refs, scr):
        @pl.when(step == nsteps - 1)
        def _():
            for cp in copies(in_refs, out_refs, scr):
                cp.wait()

    return _Comm(inputs=list(arrays), in_specs=[_HBM_WHOLE] * na, out_shape=list(out_shape),
                 out_specs=[_HBM_WHOLE] * na,
                 scratch_shapes=[pltpu.SemaphoreType.DMA((na, 3)), pltpu.SemaphoreType.DMA((na, 3))],
                 before=before, after=after)


def _sibling_exchange_comm(grads):
    def make_copies(in_refs, out_refs, send_sems, recv_sems):
        x, y, c = _mesh_pos()
        return [pltpu.make_async_remote_copy(
            src_ref=in_refs[a].at[:, pl.ds(1 - c, 1)], dst_ref=out_refs[a], send_sem=send_sems.at[a, 0],
            recv_sem=recv_sems.at[a, 0], device_id=(x, y, 1 - c), device_id_type=MESH_ID)
            for a in range(len(grads))]

    return _exchange_comm(grads, [jax.ShapeDtypeStruct((4, 1) + g.shape[2:], F32) for g in grads], make_copies)


def _chips_exchange_comm(partials):
    def make_copies(in_refs, out_refs, send_sems, recv_sems):
        x, y, c = _mesh_pos()
        return [pltpu.make_async_remote_copy(
            src_ref=in_refs[a].at[j], dst_ref=out_refs[a].at[j], send_sem=send_sems.at[a, j],
            recv_sem=recv_sems.at[a, j], device_id=(*chip, c), device_id_type=MESH_ID)
            for a in range(len(partials)) for j, chip in enumerate(_other_chips(x, y))]

    return _exchange_comm(partials, [jax.ShapeDtypeStruct(g.shape, BF16) for g in partials], make_copies)


def _comm_only(comms, name):
    return _fused_call(lambda: None, comms, name=name, grid=(1,), inputs=(), in_specs=[], out_specs=(),
                       out_shape=())[1]


def _chip_sum(my_pos, mine, from_sibling, tr, name):
    _, _, r, c = mine.shape

    def body(pos_ref, a_ref, b_ref, own_ref, out_ref):
        s = a_ref[0, 0] + b_ref[0, 0]

        @pl.when(pl.program_id(1) == 0)
        def _():
            own_ref[...] = s

        @pl.when(pl.program_id(1) > 0)
        def _():
            out_ref[0] = s.astype(BF16)

    grid_spec = pltpu.PrefetchScalarGridSpec(
        num_scalar_prefetch=1, grid=(r // tr, 4),
        in_specs=[pl.BlockSpec((1, 1, tr, c), lambda i, k, pos: (pos[0] ^ k, pos[1], i, 0)),
                  pl.BlockSpec((1, 1, tr, c), lambda i, k, pos: (pos[0] ^ k, 0, i, 0))],
        out_specs=(pl.BlockSpec((tr, c), lambda i, k, pos: (i, 0)),
                   pl.BlockSpec((1, tr, c), lambda i, k, pos: (jnp.maximum(k - 1, 0), i, 0))))
    return pl.pallas_call(
        body, name=name, grid_spec=grid_spec,
        out_shape=(jax.ShapeDtypeStruct((r, c), F32), jax.ShapeDtypeStruct((3, r, c), BF16)),
        compiler_params=_params(2),
    )(my_pos, mine, from_sibling)


def _all_reduce_small_comm(part):
    def copies(in_ref, gathered, send_sems, recv_sems):
        x, y, c = _mesh_pos()
        my_id = 4 * x + 2 * y + c
        return my_id, [pltpu.make_async_remote_copy(
            src_ref=in_ref, dst_ref=gathered.at[my_id], send_sem=send_sems.at[k - 1],
            recv_sem=recv_sems.at[k - 1], device_id=(x ^ (k >> 2), y ^ ((k >> 1) & 1), c ^ (k & 1)),
            device_id_type=MESH_ID) for k in range(1, N_DEV)]

    def before(step, nsteps, in_refs, out_refs, scr):
        @pl.when(step == 0)
        def _():
            for cp in copies(in_refs[0], *scr)[1]:
                cp.start()

    def after(step, nsteps, in_refs, out_refs, scr):
        @pl.when(step == nsteps - 1)
        def _():
            gathered = scr[0]
            my_id, cps = copies(in_refs[0], *scr)
            gathered[my_id] = in_refs[0][...]
            for cp in cps:
                cp.wait()
            acc = gathered[0]
            for d in range(1, N_DEV):
                acc = acc + gathered[d]
            out_refs[0][...] = acc

    return _Comm(inputs=[part], in_specs=[_VMEM_WHOLE], out_shape=[jax.ShapeDtypeStruct(part.shape, F32)],
                 out_specs=[_VMEM_WHOLE],
                 scratch_shapes=[pltpu.VMEM((N_DEV,) + part.shape, F32), pltpu.SemaphoreType.DMA((N_DEV - 1,)),
                                 pltpu.SemaphoreType.DMA((N_DEV - 1,))],
                 before=before, after=after)


def _unshard_cols(g):
    return jnp.transpose(g, (1, 0, 2)).reshape(g.shape[1], N_DEV * g.shape[2])


def _row_blocks(w):
    return w.reshape(4, 2, w.shape[0] // N_DEV, w.shape[1])


def _pack_small(parts):
    flat = jnp.concatenate([a.reshape(-1) for a in parts])
    pad = (-flat.shape[0]) % (8 * LANE)
    return jnp.pad(flat, (0, pad)).reshape(-1, LANE)


def _unpack_small(packed, shapes):
    flat = packed.reshape(-1)
    out, off = [], 0
    for s in shapes:
        n = math.prod(s)
        out.append(flat[off:off + n].reshape(s))
        off += n
    return out


PROJ_W = 2592
LR_REF = 1536


def _main_proj_weight(w_in_t):
    return jnp.concatenate([w_in_t[:LR_REF], w_in_t[LR_REF + 2 * LOWRANK:], w_in_t[LR_REF:LR_REF + 2 * LOWRANK],
                            jnp.zeros((PROJ_PAD - PROJ_W, D_MODEL), w_in_t.dtype)], axis=0)


def _main_proj_grad(dw_main_t):
    return jnp.concatenate([dw_main_t[:LR_REF], dw_main_t[LR_COL:LR_COL + 2 * LOWRANK], dw_main_t[LR_REF:LR_COL]],
                           axis=0)


def _padded_decay_weights(wd_f, wd_b):
    zeros = lambda n: jnp.zeros((n, KEY_W), F32)
    return (jnp.concatenate([wd_f, zeros(LANE - LOWRANK)], axis=0),
            jnp.concatenate([zeros(LOWRANK), wd_b, zeros(LANE - 2 * LOWRANK)], axis=0))


def kernel(x, norm1_g, w_in,w_decay_f, b_decay_f, w_decay_b, b_decay_b, gla_norm_g, gmlp_ln_g, gmlp_ln_b, w_spatial, b_spatial, w_out, norm2_g, w_gate, w_up, w_down, final_norm_g, loss_target, m_norm1_g, m_w_in, m_w_decay_f, m_b_decay_f, m_w_decay_b, m_b_decay_b, m_gla_norm_g, m_gmlp_ln_g, m_gmlp_ln_b, m_w_spatial, m_b_spatial, m_w_out, m_norm2_g, m_w_gate, m_w_up, m_w_down, m_final_norm_g, v_norm1_g, v_w_in, v_w_decay_f, v_b_decay_f, v_w_decay_b, v_b_decay_b, v_gla_norm_g, v_gmlp_ln_g, v_gmlp_ln_b, v_w_spatial, v_b_spatial, v_w_out, v_norm2_g, v_w_gate, v_w_up, v_w_down, v_final_norm_g):
    t = x.shape[1]
    xt = x[0]
    target = loss_target[0]
    pos_x, pos_y, pos_c = _mesh_pos()
    my_pos = jnp.stack([2 * pos_x + pos_y, pos_c]).astype(jnp.int32)
    my_id = 4 * pos_x + 2 * pos_y + pos_c

    tile = lambda n: min(n, t)
    ln_g, ln_b, w_sp = gmlp_ln_g, gmlp_ln_b, w_spatial[0]
    b_sp_col = b_spatial[0][:, :, None]
    shard = {"w_in": w_in[0].T, "w_out": w_out[0], "w_gate": w_gate[0].T, "w_up": w_up[0].T, "w_down": w_down[0]}
    shard_m = {"w_in": m_w_in[0].T, "w_out": m_w_out[0], "w_gate": m_w_gate[0].T, "w_up": m_w_up[0].T,
               "w_down": m_w_down[0]}
    shard_v = {"w_in": v_w_in[0].T, "w_out": v_w_out[0], "w_gate": v_w_gate[0].T, "w_up": v_w_up[0].T,
               "w_down": v_w_down[0]}
    transposed = ("w_in", "w_gate", "w_up")
    chip_sum = lambda n, g, s: _chip_sum(my_pos, g, s[0], g.shape[2], "chip_sum_" + n)

    decay_shard = jnp.stack([w_decay_f[0], w_decay_b[0]])
    ((g_in, g_decay),) = _comm_only([_gather_comm([shard["w_in"], decay_shard], [True, False])], "all_gather_w_in")
    w_main = _main_proj_weight(g_in.reshape(PROJ_W, D_MODEL))
    wd_pad_f, wd_pad_b = _padded_decay_weights(_unshard_cols(g_decay[:, 0]), _unshard_cols(g_decay[:, 1]))
    (p, hb), ((g_gate,),) = _in_proj(xt, norm1_g, w_main, tile(512), [_gather_comm([shard["w_gate"]], [True])])
    (o_f, st_f), ((g_up, g_out),) = _gla_fwd(p, wd_pad_f, b_decay_f, False, tile(512),
                                             [_gather_comm([shard["w_up"], shard["w_out"]], [True, True])])
    (o_b, st_b), ((g_down,),) = _gla_fwd(p, wd_pad_b, b_decay_b, True, tile(512),
                                         [_gather_comm([shard["w_down"]], [True])])
    w_out_full = g_out.reshape(D_MODEL, D_MODEL)
    x1, ycat = _mix_fwd(xt, o_f, o_b, p, gla_norm_g, ln_g, ln_b, w_sp, b_sp_col, w_out_full, tile(512))

    dx1, h2b, dgate, dup, act, dx2, loss_acc, d_gf, d_g2 = _ffn(
        x1, target, norm2_g, final_norm_g[None, :], g_gate.reshape(D_FF, D_MODEL), g_up.reshape(D_FF, D_MODEL),
        g_down.reshape(D_FF, D_MODEL), tile(256))
    dw_gate, _ = _matmul_tn(dgate, h2b, D_FF // 2, tile(2048), "grad_w_gate")
    dw_up, _ = _matmul_tn(dup, h2b, D_FF // 2, tile(2048), "grad_w_up")
    dw_down, _ = _matmul_tn(act, dx2, D_FF // 2, tile(2048), "grad_w_down")

    ffn_grads = [_row_blocks(dw_gate), _row_blocks(dw_up), _row_blocks(dw_down)]
    (d_o, dpg, dpu, dpv, dw_out, d_gg, d_lg, d_lb, dw_sp, db_sp), (ffn_sib,) = _mix_bwd(
        dx1, ycat, o_f, o_b, p, gla_norm_g, ln_g, ln_b, w_sp, b_sp_col, w_out_full, tile(256),
        [_sibling_exchange_comm(ffn_grads)])
    ffn_names = ["w_gate", "w_up", "w_down"]
    ffn_sums = [chip_sum(n, g, [s]) for n, g, s in zip(ffn_names, ffn_grads, ffn_sib)]
    out_grad = _row_blocks(dw_out)
    (dq_f, dk_f, dv_f, dlr_f, dwd_f, dbd_f), (ffn_recv, out_sib) = _gla_bwd(
        p, wd_pad_f, b_decay_f, st_f, d_o, False, tile(512),
        [_chips_exchange_comm([s[1] for s in ffn_sums]), _sibling_exchange_comm([out_grad])])
    out_sum = chip_sum("w_out", out_grad, out_sib)
    (dq_b, dk_b, dv_b, dlr_b, dwd_b, dbd_b), (out_recv,) = _gla_bwd(
        p, wd_pad_b, b_decay_b, st_b, d_o, True, tile(512), [_chips_exchange_comm([out_sum[1]])])
    grad_x, dp, d_g1 = _in_proj_bwd(xt, norm1_g, dx1, dq_f, dq_b, dk_f, dk_b, dv_f, dv_b, dpg, dpu, dpv,
                                    dlr_f, dlr_b, w_main, tile(256))

    small_shapes = [(1, D_MODEL), (LOWRANK, KEY_W), (1, KEY_W), (LOWRANK, KEY_W), (1, KEY_W), (1, VAL_W),
                    (1, GMLP_W), (1, GMLP_W), (1, GMLP_GROUPS, GMLP_CHUNK, GMLP_CHUNK), (1, GMLP_GROUPS, GMLP_CHUNK),
                    (1, D_MODEL), (D_MODEL,), (LANE,)]
    small_parts = [d_g1, dwd_f[:LOWRANK], dbd_f, dwd_b[LOWRANK:2 * LOWRANK], dbd_b, d_gg, d_lg, d_lb, dw_sp,
                   db_sp, d_g2, d_gf, loss_acc[0]]
    dw_main, ((small_sum,),) = _matmul_tn(dp, hb, PROJ_PAD // 3, tile(2048), "grad_w_in",
                                           [_all_reduce_small_comm(_pack_small(small_parts))])
    in_grad = _row_blocks(_main_proj_grad(dw_main))
    (in_sib,) = _comm_only([_sibling_exchange_comm([in_grad])], "grad_w_in_exchange_sibling")
    in_sum = chip_sum("w_in", in_grad, in_sib)
    (in_recv,) = _comm_only([_chips_exchange_comm([in_sum[1]])], "grad_w_in_exchange_chips")

    names = ["w_in", "w_out", "w_gate", "w_up", "w_down"]
    sums = [in_sum, out_sum] + ffn_sums
    received = [in_recv[0], out_recv[0]] + list(ffn_recv)
    big_out = {}
    for n, s, rc in zip(names, sums, received):
        res = _adamw_shard(s[0], rc, shard[n], shard_m[n], shard_v[n], shard[n].shape[0], "adamw_" + n)
        big_out[n] = [r.T if n in transposed else r for r in res]

    reduced = _unpack_small(small_sum, small_shapes)
    loss = reduced[-1][0]
    col0 = my_id * (KEY_W // N_DEV)
    g_small = list(reduced[:-1])
    g_small[1] = lax.dynamic_slice_in_dim(g_small[1], col0, KEY_W // N_DEV, axis=1)[None]
    g_small[3] = lax.dynamic_slice_in_dim(g_small[3], col0, KEY_W // N_DEV, axis=1)[None]
    small_w = [norm1_g, w_decay_f, b_decay_f, w_decay_b, b_decay_b, gla_norm_g, gmlp_ln_g, gmlp_ln_b, w_spatial,
               b_spatial, norm2_g, final_norm_g]
    small_m = [m_norm1_g, m_w_decay_f, m_b_decay_f, m_w_decay_b, m_b_decay_b, m_gla_norm_g, m_gmlp_ln_g,
               m_gmlp_ln_b, m_w_spatial, m_b_spatial, m_norm2_g, m_final_norm_g]
    small_v = [v_norm1_g, v_w_decay_f, v_b_decay_f, v_w_decay_b, v_b_decay_b, v_gla_norm_g, v_gmlp_ln_g,
               v_gmlp_ln_b, v_w_spatial, v_b_spatial, v_norm2_g, v_final_norm_g]
    shapes = [w.shape for w in small_w]
    packed = _adamw_small(_pack_small(g_small), _pack_small(small_w), _pack_small(small_m), _pack_small(small_v))
    s_delta, s_m, s_v = (_unpack_small(a, shapes) for a in packed)
    s_grad = [g.reshape(s) for g, s in zip(g_small, shapes)]

    order = ["norm1_g", "w_in", "w_decay_f", "b_decay_f", "w_decay_b", "b_decay_b", "gla_norm_g", "gmlp_ln_g",
             "gmlp_ln_b", "w_spatial", "b_spatial", "w_out", "norm2_g", "w_gate", "w_up", "w_down", "final_norm_g"]
    small_names = [n for n in order if n not in big_out]
    small_out = {n: (s_grad[i], s_delta[i], s_m[i], s_v[i]) for i, n in enumerate(small_names)}
    outs = []
    for kind in range(4):
        for n in order:
            outs.append(big_out[n][kind][None] if n in big_out else small_out[n][kind])
    return (loss, grad_x[None], *outs)
```

```python
import functools
import math

import jax
import jax.numpy as jnp
from jax import lax
from jax.experimental import pallas as pl
from jax.experimental.pallas import tpu as pltpu

F32 = jnp.float32
BF16 = jnp.bfloat16

D_MODEL = 1024
GLA_HEADS = 4
GLA_DK = 64
GLA_DV = 128
KEY_W = GLA_HEADS * GLA_DK
VAL_W = GLA_HEADS * GLA_DV
LOWRANK = 16
GLA_TAU = 16.0
GLA_CHUNK = 64
GMLP_W = 512
GMLP_GROUPS = 4
GMLP_CHUNK = 128
D_FF = 2816
EPS = 1e-6
Q_SCALE = GLA_DK ** -0.5
PROJ_PAD = 2688
LR_COL = 2560
LANE = 128
N_DEV = 8

ADAM_LR = 0.001
ADAM_B1 = 0.9
ADAM_B2 = 0.999
ADAM_EPS = 1e-08
ADAM_WD = 0.01
ADAM_STEP = 10

VMEM_LIMIT = 56 * 1024 * 1024
MESH_ID = pl.DeviceIdType.MESH
INV_SQRT2 = 0.7071067811865476
INV_SQRT_2PI = 0.3989422804014327


def _params(n_axes=1):
    return pltpu.CompilerParams(dimension_semantics=("arbitrary",) * n_axes, vmem_limit_bytes=VMEM_LIMIT)


def _mm(a, b):
    return jnp.dot(a.astype(BF16), b.astype(BF16), preferred_element_type=F32)


def _mm_nt(a, b):
    return lax.dot_general(a.astype(BF16), b.astype(BF16), (((1,), (1,)), ((), ())), preferred_element_type=F32)


def _mm_tn(a, b):
    return lax.dot_general(a.astype(BF16), b.astype(BF16), (((0,), (0,)), ((), ())), preferred_element_type=F32)


def _const_spec(shape):
    nd = len(shape)
    return pl.BlockSpec(shape, lambda *_: (0,) * nd, pipeline_mode=pl.Buffered(1))


def _acc_spec(shape):
    nd = len(shape)
    return pl.BlockSpec(shape, lambda *_: (0,) * nd)


class _Comm:
    def __init__(self, inputs, in_specs, out_shape, out_specs, scratch_shapes, before, after):
        self.inputs, self.in_specs, self.out_shape, self.out_specs = inputs, in_specs, out_shape, out_specs
        self.scratch_shapes, self.before, self.after = scratch_shapes, before, after


def _fused_call(body, comms, *, name, grid, inputs, in_specs, out_specs, out_shape, scratch_shapes=()):
    n_in, n_out, n_scr = len(in_specs), len(out_specs), len(scratch_shapes)
    nsteps = math.prod(grid)
    sizes = [(len(c.inputs), len(c.out_shape), len(c.scratch_shapes)) for c in comms]

    def full_body(*refs):
        step = pl.program_id(0)
        for axis in range(1, len(grid)):
            step = step * grid[axis] + pl.program_id(axis)
        ins, rest = refs[:n_in], refs[n_in:]
        c_ins = []
        for ci, _, _ in sizes:
            c_ins.append(rest[:ci])
            rest = rest[ci:]
        outs, rest = rest[:n_out], rest[n_out:]
        c_outs = []
        for _, co, _ in sizes:
            c_outs.append(rest[:co])
            rest = rest[co:]
        scr, rest = rest[:n_scr], rest[n_scr:]
        c_scr = []
        for _, _, cs in sizes:
            c_scr.append(rest[:cs])
            rest = rest[cs:]
        for c, a, b, s in zip(comms, c_ins, c_outs, c_scr):
            c.before(step, nsteps, a, b, s)
        body(*ins, *outs, *scr)
        for c, a, b, s in zip(comms, c_ins, c_outs, c_scr):
            c.after(step, nsteps, a, b, s)

    results = pl.pallas_call(
        full_body, name=name, grid=grid,
        in_specs=list(in_specs) + [s for c in comms for s in c.in_specs],
        out_specs=tuple(out_specs) + tuple(s for c in comms for s in c.out_specs),
        out_shape=tuple(out_shape) + tuple(s for c in comms for s in c.out_shape),
        scratch_shapes=list(scratch_shapes) + [s for c in comms for s in c.scratch_shapes],
        compiler_params=_params(len(grid)),
    )(*inputs, *[a for c in comms for a in c.inputs])
    own, rest = results[:n_out], results[n_out:]
    comm_results = []
    for _, co, _ in sizes:
        comm_results.append(rest[:co])
        rest = rest[co:]
    return own, comm_results


def _gelu(x):
    return 0.5 * x * (1.0 + lax.erf(x * INV_SQRT2))


def _gelu_grad(x):
    return 0.5 * (1.0 + lax.erf(x * INV_SQRT2)) + x * jnp.exp(-0.5 * x * x) * INV_SQRT_2PI


def _silu_and_grad(x):
    s = jax.nn.sigmoid(x)
    return x * s, s * (1.0 + x * (1.0 - s))


def _in_proj(x, g1, w_main, tm, comms=()):
    t = x.shape[0]

    def body(x_ref, g_ref, w_ref, p_ref, h_ref):
        xv = x_ref[...]
        r = lax.rsqrt(jnp.mean(xv * xv, axis=-1, keepdims=True) + EPS)
        h = (xv * r * g_ref[...]).astype(BF16)
        h_ref[...] = h
        p_ref[...] = _mm_nt(h, w_ref[...]).astype(BF16)

    return _fused_call(
        body, comms, name="in_proj", grid=(t // tm,), inputs=(x, g1, w_main),
        in_specs=[pl.BlockSpec((tm, D_MODEL), lambda i: (i, 0)), _const_spec((1, D_MODEL)),
                  _const_spec((PROJ_PAD, D_MODEL))],
        out_specs=(pl.BlockSpec((tm, PROJ_PAD), lambda i: (i, 0)), pl.BlockSpec((tm, D_MODEL), lambda i: (i, 0))),
        out_shape=(jax.ShapeDtypeStruct((t, PROJ_PAD), BF16), jax.ShapeDtypeStruct((t, D_MODEL), BF16)))


def _tri(upper):
    r = lax.broadcasted_iota(jnp.int32, (GLA_CHUNK, GLA_CHUNK), 0)
    c = lax.broadcasted_iota(jnp.int32, (GLA_CHUNK, GLA_CHUNK), 1)
    return jnp.where((c >= r) if upper else (c <= r), 1.0, 0.0).astype(BF16)


def _tri_matmul(tri, a):
    a1 = a.astype(BF16)
    r1 = a - a1.astype(F32)
    a2 = r1.astype(BF16)
    a3 = (r1 - a2.astype(F32)).astype(BF16)
    dot = functools.partial(jnp.dot, preferred_element_type=F32)
    return dot(tri, a1) + dot(tri, a2) + dot(tri, a3)


def _gla_masks(rev):
    dk_bits, dv_bits = GLA_DK.bit_length() - 1, GLA_DV.bit_length() - 1
    key_head = lax.broadcasted_iota(jnp.int32, (GLA_CHUNK, KEY_W), 1) >> dk_bits
    val_head = lax.broadcasted_iota(jnp.int32, (GLA_CHUNK, VAL_W), 1) >> dv_bits
    t = lax.broadcasted_iota(jnp.int32, (GLA_HEADS * GLA_CHUNK, GLA_CHUNK), 0) & (GLA_CHUNK - 1)
    s = lax.broadcasted_iota(jnp.int32, (GLA_HEADS * GLA_CHUNK, GLA_CHUNK), 1)
    causal = (s >= t) if rev else (s <= t)
    blockdiag = (lax.broadcasted_iota(jnp.int32, (VAL_W, KEY_W), 0) >> dv_bits
                 == lax.broadcasted_iota(jnp.int32, (VAL_W, KEY_W), 1) >> dk_bits)
    return key_head, val_head, causal, blockdiag


def _stack_heads(a, head_of_lane):
    return jnp.concatenate([jnp.where(head_of_lane == h, a, 0.0) for h in range(GLA_HEADS)], axis=0)


def _chunk_terms(la_c, q_c, k_c, tri, rev):
    q_c, k_c = q_c.astype(F32), k_c.astype(F32)
    b = _tri_matmul(tri, la_c)
    bl = b[0:1] if rev else b[GLA_CHUNK - 1:GLA_CHUNK]
    eb = jnp.exp(b)
    enb = jnp.exp(-b)
    ee = jnp.exp(bl - b)
    return bl, eb, enb, ee, q_c * Q_SCALE * eb, k_c * enb, k_c * ee


def _log_decay(lr_ref, wd_ref, bd_ref):
    z = _mm(lr_ref[...], wd_ref[...]) + bd_ref[...]
    return z, jax.nn.log_sigmoid(z) * (1.0 / GLA_TAU)


def _p_specs(tg, tile):
    return [pl.BlockSpec((tg, KEY_W), lambda i: (tile(i), 0)),
            pl.BlockSpec((tg, KEY_W), lambda i: (tile(i), 1)),
            pl.BlockSpec((tg, VAL_W), lambda i: (tile(i), 1)),
            pl.BlockSpec((tg, LANE), lambda i: (tile(i), LR_COL // LANE))]


def _gla_fwd_dir(rev, nc, q_ref, k_ref, v_ref, lr_ref, wd_ref, bd_ref, o_ref, st_ref, state):
    key_head, _, causal, blockdiag = _gla_masks(rev)
    tri = _tri(rev)
    _, la = _log_decay(lr_ref, wd_ref, bd_ref)

    def step(cc):
        c = nc - 1 - cc if rev else cc
        rows = slice(c * GLA_CHUNK, (c + 1) * GLA_CHUNK)
        v_c = v_ref[rows, :].astype(BF16)
        bl, _, _, _, qd, kd, ke = _chunk_terms(la[rows], q_ref[rows, :], k_ref[rows, :], tri, rev)
        a_all = jnp.where(causal, _mm_nt(_stack_heads(qd, key_head), kd), 0.0)
        r = _mm(a_all, v_c)
        o_intra = jnp.concatenate(
            [r[h * GLA_CHUNK:(h + 1) * GLA_CHUNK, h * GLA_DV:(h + 1) * GLA_DV] for h in range(GLA_HEADS)], axis=1)
        st = state[...]
        o_ref[rows, :] = o_intra + _mm_nt(qd, st)
        st_ref[c] = st.astype(BF16)
        state[...] = st * jnp.exp(bl) + jnp.where(blockdiag, _mm_tn(v_c, ke), 0.0)

    return step


def _gla_fwd(p, wd_pad_f, bd_f, wd_pad_b, bd_b, tg, comms=()):
    t = p.shape[0]
    nt = t // tg
    nc = tg // GLA_CHUNK
    up, down = (lambda i: i), (lambda i: nt - 1 - i)

    def body(qf, kf, vf, lrf, qb, kb, vb, lrb, wdf, bdf, wdb, bdb, of, stf, ob, stb, state_f, state_b):
        @pl.when(pl.program_id(0) == 0)
        def _():
            state_f[...] = jnp.zeros_like(state_f)
            state_b[...] = jnp.zeros_like(state_b)

        steps = [_gla_fwd_dir(False, nc, qf, kf, vf, lrf, wdf, bdf, of, stf, state_f),
                 _gla_fwd_dir(True, nc, qb, kb, vb, lrb, wdb, bdb, ob, stb, state_b)]
        for cc in range(nc):
            for step in steps:
                step(cc)

    wd_spec, bd_spec = _const_spec((LANE, KEY_W)), _const_spec((1, KEY_W))
    outs = lambda tile: (pl.BlockSpec((tg, VAL_W), lambda i: (tile(i), 0)),
                         pl.BlockSpec((nc, VAL_W, KEY_W), lambda i: (tile(i), 0, 0)))
    out_shape = (jax.ShapeDtypeStruct((t, VAL_W), F32), jax.ShapeDtypeStruct((t // GLA_CHUNK, VAL_W, KEY_W), BF16))
    return _fused_call(
        body, comms, name="gla_fwd", grid=(nt,), inputs=(p,) * 8 + (wd_pad_f, bd_f, wd_pad_b, bd_b),
        in_specs=_p_specs(tg, up) + _p_specs(tg, down) + [wd_spec, bd_spec, wd_spec, bd_spec],
        out_specs=outs(up) + outs(down), out_shape=out_shape * 2,
        scratch_shapes=[pltpu.VMEM((VAL_W, KEY_W), F32)] * 2)


def _gla_bwd_dir(rev, nc, q_ref, k_ref, v_ref, lr_ref, wd_ref, bd_ref, st_ref, do_ref,
                 dq_ref, dk_ref, dv_ref, dlr_ref, dwd_ref, dbd_ref, dstate, dz_scr):
    key_head, val_head, causal, blockdiag = _gla_masks(rev)
    tri = _tri(rev)
    tri_t = _tri(not rev)
    z, la = _log_decay(lr_ref, wd_ref, bd_ref)
    dlog = jax.nn.sigmoid(-z) * (1.0 / GLA_TAU)

    def step(cc):
        c = cc if rev else nc - 1 - cc
        rows = slice(c * GLA_CHUNK, (c + 1) * GLA_CHUNK)
        v_c = v_ref[rows, :].astype(BF16)
        do_c = do_ref[rows, :]
        bl, eb, enb, ee, qd, kd, ke = _chunk_terms(la[rows], q_ref[rows, :], k_ref[rows, :], tri, rev)
        qd_stack = _stack_heads(qd, key_head)
        do_stack = _stack_heads(do_c, val_head)
        a_all = jnp.where(causal, _mm_nt(qd_stack, kd), 0.0)
        da_all = jnp.where(causal, _mm_nt(do_stack, v_c), 0.0)
        dst = dstate[...]
        st_prev = st_ref[c]
        dv_ref[rows, :] = (_mm_tn(a_all, do_stack) + _mm_nt(ke, dst)).astype(BF16)
        r2 = _mm(da_all, kd)
        dqd = _mm(do_c, st_prev)
        for h in range(GLA_HEADS):
            dqd = dqd + jnp.where(key_head == h, r2[h * GLA_CHUNK:(h + 1) * GLA_CHUNK, :], 0.0)
        dkd = _mm_tn(da_all, qd_stack)
        dke = _mm(v_c, dst)
        ebl = jnp.exp(bl)
        dbl = (jnp.sum(dst * st_prev.astype(F32), axis=0, keepdims=True) * ebl
               + jnp.sum(dke * ke, axis=0, keepdims=True))
        dstate[...] = dst * ebl + jnp.where(blockdiag, _mm_tn(do_c, qd), 0.0)
        dq_ref[rows, :] = (dqd * eb * Q_SCALE).astype(BF16)
        dk_ref[rows, :] = (dkd * enb + dke * ee).astype(BF16)
        db = dqd * qd - dkd * kd - dke * ke
        dz_scr[rows, :] = (_tri_matmul(tri_t, db) + dbl) * dlog[rows]

    def finish():
        dz = dz_scr[...]
        dlr_ref[...] = _mm_nt(dz, wd_ref[...]).astype(BF16)
        dwd_ref[...] += _mm_tn(lr_ref[...], dz)
        dbd_ref[...] += jnp.sum(dz, axis=0, keepdims=True)

    return step, finish


def _gla_bwd(p, wd_pad_f, bd_f, wd_pad_b, bd_b, st_f, st_b, d_o, tg, comms=()):
    t = p.shape[0]
    nt = t // tg
    nc = tg // GLA_CHUNK
    up, down = (lambda i: i), (lambda i: nt - 1 - i)

    def body(qf, kf, vf, lrf, stf, dof, qb, kb, vb, lrb, stb, dob, wdf, bdf, wdb, bdb,
             dqf, dkf, dvf, dlrf, dwdf, dbdf, dqb, dkb, dvb, dlrb, dwdb, dbdb,
             dstate_f, dz_f, dstate_b, dz_b):
        @pl.when(pl.program_id(0) == 0)
        def _():
            for ref in (dstate_f, dstate_b, dwdf, dbdf, dwdb, dbdb):
                ref[...] = jnp.zeros_like(ref)

        dirs = [_gla_bwd_dir(False, nc, qf, kf, vf, lrf, wdf, bdf, stf, dof, dqf, dkf, dvf, dlrf, dwdf, dbdf,
                             dstate_f, dz_f),
                _gla_bwd_dir(True, nc, qb, kb, vb, lrb, wdb, bdb, stb, dob, dqb, dkb, dvb, dlrb, dwdb, dbdb,
                             dstate_b, dz_b)]
        for cc in range(nc):
            for step, _ in dirs:
                step(cc)
        for _, finish in dirs:
            finish()

    wd_spec, bd_spec = _const_spec((LANE, KEY_W)), _const_spec((1, KEY_W))
    ins = lambda tile: _p_specs(tg, tile) + [pl.BlockSpec((nc, VAL_W, KEY_W), lambda i: (tile(i), 0, 0)),
                                             pl.BlockSpec((tg, VAL_W), lambda i: (tile(i), 0))]
    outs = lambda tile: (pl.BlockSpec((tg, KEY_W), lambda i: (tile(i), 0)),
                         pl.BlockSpec((tg, KEY_W), lambda i: (tile(i), 0)),
                         pl.BlockSpec((tg, VAL_W), lambda i: (tile(i), 0)),
                         pl.BlockSpec((tg, LANE), lambda i: (tile(i), 0)),
                         _acc_spec((LANE, KEY_W)), _acc_spec((1, KEY_W)))
    out_shape = (jax.ShapeDtypeStruct((t, KEY_W), BF16), jax.ShapeDtypeStruct((t, KEY_W), BF16),
                 jax.ShapeDtypeStruct((t, VAL_W), BF16), jax.ShapeDtypeStruct((t, LANE), BF16),
                 jax.ShapeDtypeStruct((LANE, KEY_W), F32), jax.ShapeDtypeStruct((1, KEY_W), F32))
    scratch = [pltpu.VMEM((VAL_W, KEY_W), F32), pltpu.VMEM((tg, KEY_W), F32)]
    return _fused_call(
        body, comms, name="gla_bwd", grid=(nt,),
        inputs=(p, p, p, p, st_f, d_o, p, p, p, p, st_b, d_o, wd_pad_f, bd_f, wd_pad_b, bd_b),
        in_specs=ins(down) + ins(up) + [wd_spec, bd_spec, wd_spec, bd_spec],
        out_specs=outs(down) + outs(up), out_shape=out_shape * 2, scratch_shapes=scratch * 2)


def _head_rms(o):
    parts, scales = [], []
    for h in range(GLA_HEADS):
        oh = o[:, h * GLA_DV:(h + 1) * GLA_DV]
        r = lax.rsqrt(jnp.mean(oh * oh, axis=-1, keepdims=True) + EPS)
        parts.append(oh * r)
        scales.append(jnp.broadcast_to(r, oh.shape))
    return jnp.concatenate(parts, axis=1), jnp.concatenate(scales, axis=1)


def _layernorm_stats(zv):
    mu = jnp.mean(zv, axis=-1, keepdims=True)
    xc = zv - mu
    rs = lax.rsqrt(jnp.mean(xc * xc, axis=-1, keepdims=True) + EPS)
    return xc * rs, rs


def _mix_fwd(x, o_f, o_b, p, gla_g, ln_g, ln_b, w_sp, b_sp, w_out, tm):
    t = x.shape[0]
    nch = tm // GMLP_CHUNK

    def body(x_ref, of_ref, ob_ref, pg_ref, pu_ref, pv_ref, gg_ref, lg_ref, lb_ref, ws_ref, bs_ref, wo_ref,
             x1_ref, y_ref, s_scr):
        on, _ = _head_rms(of_ref[...] + ob_ref[...])
        pg = pg_ref[...].astype(F32)
        y_a = on * gg_ref[...] * (pg * jax.nn.sigmoid(pg))
        zu = _gelu(pu_ref[...].astype(F32))
        vhat, _ = _layernorm_stats(_gelu(pv_ref[...].astype(F32)))
        vln = (vhat * lg_ref[...] + lb_ref[...]).astype(BF16)
        for g in range(GMLP_GROUPS):
            w_g = ws_ref[g].astype(BF16)
            b_g = bs_ref[g]
            cols = slice(g * LANE, (g + 1) * LANE)
            for n in range(nch):
                rows = slice(n * GMLP_CHUNK, (n + 1) * GMLP_CHUNK)
                s_scr[rows, cols] = jnp.dot(w_g, vln[rows, cols], preferred_element_type=F32) + b_g
        ycat = jnp.concatenate([y_a, zu * s_scr[...]], axis=1).astype(BF16)
        y_ref[...] = ycat
        x1_ref[...] = x_ref[...] + jnp.dot(ycat, wo_ref[...], preferred_element_type=F32)

    half = lambda j: pl.BlockSpec((tm, VAL_W), lambda i: (i, j))
    return pl.pallas_call(
        body, name="mix_fwd", grid=(t // tm,),
        in_specs=[pl.BlockSpec((tm, D_MODEL), lambda i: (i, 0)), half(0), half(0), half(2), half(3), half(4),
                  _const_spec((1, VAL_W)), _const_spec((1, GMLP_W)), _const_spec((1, GMLP_W)),
                  _const_spec((GMLP_GROUPS, GMLP_CHUNK, GMLP_CHUNK)), _const_spec((GMLP_GROUPS, GMLP_CHUNK, 1)),
                  _const_spec((D_MODEL, D_MODEL))],
        out_specs=(pl.BlockSpec((tm, D_MODEL), lambda i: (i, 0)), pl.BlockSpec((tm, D_MODEL), lambda i: (i, 0))),
        out_shape=(jax.ShapeDtypeStruct((t, D_MODEL), F32), jax.ShapeDtypeStruct((t, D_MODEL), BF16)),
        scratch_shapes=[pltpu.VMEM((tm, GMLP_W), F32)],
        compiler_params=_params(),
    )(x, o_f, o_b, p, p, p, gla_g, ln_g, ln_b, w_sp, b_sp, w_out)


def _mix_bwd(dx1, ycat, o_f, o_b, p, gla_g, ln_g, ln_b, w_sp, b_sp, w_out, tm, comms=()):
    t = dx1.shape[0]
    nch = tm // GMLP_CHUNK

    def body(dx1_ref, y_ref, of_ref, ob_ref, pg_ref, pu_ref, pv_ref, gg_ref, lg_ref, lb_ref, ws_ref, bs_ref, wo_ref,
             do_ref, dpg_ref, dpu_ref, dpv_ref, dwo_ref, dgg_ref, dlg_ref, dlb_ref, dws_ref, dbs_ref,
             s_scr, dvln_scr):
        @pl.when(pl.program_id(0) == 0)
        def _():
            for ref in (dwo_ref, dgg_ref, dlg_ref, dlb_ref, dws_ref, dbs_ref):
                ref[...] = jnp.zeros_like(ref)

        dx1 = dx1_ref[...].astype(BF16)
        dycat = _mm_nt(dx1, wo_ref[...])
        dwo_ref[...] += _mm_tn(y_ref[...], dx1)
        dy_a = dycat[:, :VAL_W]
        dy_b = dycat[:, VAL_W:]
        on, r = _head_rms(of_ref[...] + ob_ref[...])
        pg = pg_ref[...].astype(F32)
        sil, dsil = _silu_and_grad(pg)
        gg = gg_ref[...]
        dgg_ref[...] += jnp.sum(dy_a * sil * on, axis=0, keepdims=True)
        don = dy_a * sil * gg
        prod = don * on
        means = jnp.concatenate(
            [jnp.broadcast_to(jnp.mean(prod[:, h * GLA_DV:(h + 1) * GLA_DV], axis=-1, keepdims=True),
                              (tm, GLA_DV)) for h in range(GLA_HEADS)], axis=1)
        do_ref[...] = (r * (don - on * means)).astype(BF16)
        dpg_ref[...] = (dy_a * on * gg * dsil).astype(BF16)
        pu = pu_ref[...].astype(F32)
        pv = pv_ref[...].astype(F32)
        zu = _gelu(pu)
        vhat, rs = _layernorm_stats(_gelu(pv))
        lg = lg_ref[...]
        vln = (vhat * lg + lb_ref[...]).astype(BF16)
        ds32 = dy_b * zu
        ds = ds32.astype(BF16)
        for g in range(GMLP_GROUPS):
            w_g = ws_ref[g].astype(BF16)
            b_g = bs_ref[g]
            cols = slice(g * LANE, (g + 1) * LANE)
            dw_g = jnp.zeros((GMLP_CHUNK, GMLP_CHUNK), F32)
            db_g = jnp.zeros((GMLP_CHUNK, 1), F32)
            for n in range(nch):
                rows = slice(n * GMLP_CHUNK, (n + 1) * GMLP_CHUNK)
                v_blk = vln[rows, cols]
                ds_blk = ds[rows, cols]
                s_scr[rows, cols] = jnp.dot(w_g, v_blk, preferred_element_type=F32) + b_g
                dw_g = dw_g + _mm_nt(ds_blk, v_blk)
                db_g = db_g + jnp.sum(ds32[rows, cols], axis=-1, keepdims=True)
                dvln_scr[rows, cols] = _mm_tn(w_g, ds_blk)
            dws_ref[g] += dw_g
            dbs_ref[g] += db_g
        dpu_ref[...] = (dy_b * s_scr[...] * _gelu_grad(pu)).astype(BF16)
        dvln = dvln_scr[...]
        dlg_ref[...] += jnp.sum(dvln * vhat, axis=0, keepdims=True)
        dlb_ref[...] += jnp.sum(dvln, axis=0, keepdims=True)
        dvhat = dvln * lg
        dzv = rs * (dvhat - jnp.mean(dvhat, axis=-1, keepdims=True)
                    - vhat * jnp.mean(dvhat * vhat, axis=-1, keepdims=True))
        dpv_ref[...] = (dzv * _gelu_grad(pv)).astype(BF16)

    half = lambda j: pl.BlockSpec((tm, VAL_W), lambda i: (i, j))
    full = pl.BlockSpec((tm, D_MODEL), lambda i: (i, 0))
    sp_shape = (GMLP_GROUPS, GMLP_CHUNK, GMLP_CHUNK)
    bs_shape = (GMLP_GROUPS, GMLP_CHUNK, 1)
    return _fused_call(
        body, comms, name="mix_bwd", grid=(t // tm,),
        inputs=(dx1, ycat, o_f, o_b, p, p, p, gla_g, ln_g, ln_b, w_sp, b_sp, w_out),
        in_specs=[full, full, half(0), half(0), half(2), half(3), half(4),
                  _const_spec((1, VAL_W)), _const_spec((1, GMLP_W)), _const_spec((1, GMLP_W)),
                  _const_spec(sp_shape), _const_spec(bs_shape), _const_spec((D_MODEL, D_MODEL))],
        out_specs=(half(0), half(0), half(0), half(0), _acc_spec((D_MODEL, D_MODEL)), _acc_spec((1, VAL_W)),
                   _acc_spec((1, GMLP_W)), _acc_spec((1, GMLP_W)), _acc_spec(sp_shape), _acc_spec(bs_shape)),
        out_shape=(jax.ShapeDtypeStruct((t, VAL_W), BF16),) * 4 + (
            jax.ShapeDtypeStruct((D_MODEL, D_MODEL), F32), jax.ShapeDtypeStruct((1, VAL_W), F32),
            jax.ShapeDtypeStruct((1, GMLP_W), F32), jax.ShapeDtypeStruct((1, GMLP_W), F32),
            jax.ShapeDtypeStruct(sp_shape, F32), jax.ShapeDtypeStruct(bs_shape, F32)),
        scratch_shapes=[pltpu.VMEM((tm, GMLP_W), F32), pltpu.VMEM((tm, GMLP_W), F32)])


def _rms_bwd(dy_scaled, xn, r):
    return r * (dy_scaled - xn * jnp.mean(dy_scaled * xn, axis=-1, keepdims=True))


def _ffn(x1, target, g2, gf, w_gate, w_up, w_down, tm):
    t = x1.shape[0]

    def body(x1_ref, tg_ref, g2_ref, gf_ref, wg_ref, wu_ref, wd_ref,
             dx1_ref, h2_ref, dgate_ref, dup_ref, act_ref, dx2_ref, loss_ref, dgf_ref, dg2_ref):
        @pl.when(pl.program_id(0) == 0)
        def _():
            for ref in (loss_ref, dgf_ref, dg2_ref):
                ref[...] = jnp.zeros_like(ref)

        x1v = x1_ref[...]
        g2v = g2_ref[...]
        gfv = gf_ref[...]
        r2 = lax.rsqrt(jnp.mean(x1v * x1v, axis=-1, keepdims=True) + EPS)
        xn1 = x1v * r2
        h2 = (xn1 * g2v).astype(BF16)
        h2_ref[...] = h2
        gate = _mm_nt(h2, wg_ref[...])
        up = _mm_nt(h2, wu_ref[...])
        sil, dsil = _silu_and_grad(gate)
        act = (sil * up).astype(BF16)
        act_ref[...] = act
        x2 = x1v + jnp.dot(act, wd_ref[...], preferred_element_type=F32)
        rf = lax.rsqrt(jnp.mean(x2 * x2, axis=-1, keepdims=True) + EPS)
        xn2 = x2 * rf
        err = xn2 * gfv - tg_ref[...]
        loss_ref[...] += 0.5 * jnp.sum(jnp.mean(err * err, axis=-1, keepdims=True))
        dy = err * (1.0 / D_MODEL)
        dgf_ref[...] += jnp.sum(dy * xn2, axis=0, keepdims=True)
        dx2 = _rms_bwd(dy * gfv, xn2, rf)
        dx2b = dx2.astype(BF16)
        dx2_ref[...] = dx2b
        dact = _mm_nt(dx2b, wd_ref[...])
        dgate = (dact * up * dsil).astype(BF16)
        dup = (dact * sil).astype(BF16)
        dgate_ref[...] = dgate
        dup_ref[...] = dup
        dh2 = _mm(dgate, wg_ref[...]) + _mm(dup, wu_ref[...])
        dg2_ref[...] += jnp.sum(dh2 * xn1, axis=0, keepdims=True)
        dx1_ref[...] = dx2 + _rms_bwd(dh2 * g2v, xn1, r2)

    row = lambda w: pl.BlockSpec((tm, w), lambda i: (i, 0))
    return pl.pallas_call(
        body, name="ffn_fwd_bwd", grid=(t // tm,),
        in_specs=[row(D_MODEL), row(D_MODEL), _const_spec((1, D_MODEL)), _const_spec((1, D_MODEL)),
                  _const_spec((D_FF, D_MODEL)), _const_spec((D_FF, D_MODEL)), _const_spec((D_FF, D_MODEL))],
        out_specs=(row(D_MODEL), row(D_MODEL), row(D_FF), row(D_FF), row(D_FF), row(D_MODEL),
                   _acc_spec((8, LANE)), _acc_spec((1, D_MODEL)), _acc_spec((1, D_MODEL))),
        out_shape=(jax.ShapeDtypeStruct((t, D_MODEL), F32), jax.ShapeDtypeStruct((t, D_MODEL), BF16),
                   jax.ShapeDtypeStruct((t, D_FF), BF16), jax.ShapeDtypeStruct((t, D_FF), BF16),
                   jax.ShapeDtypeStruct((t, D_FF), BF16), jax.ShapeDtypeStruct((t, D_MODEL), BF16),
                   jax.ShapeDtypeStruct((8, LANE), F32), jax.ShapeDtypeStruct((1, D_MODEL), F32),
                   jax.ShapeDtypeStruct((1, D_MODEL), F32)),
        compiler_params=_params(),
    )(x1, target, g2, gf, w_gate, w_up, w_down)


def _matmul_tn(a, b, tm, tk, name, comms=()):
    t, m = a.shape
    n = b.shape[1]

    def body(a_ref, b_ref, o_ref):
        @pl.when(pl.program_id(1) == 0)
        def _():
            o_ref[...] = jnp.zeros_like(o_ref)

        o_ref[...] += _mm_tn(a_ref[...], b_ref[...])

    (out,), comm_results = _fused_call(
        body, comms, name=name, grid=(m // tm, t // tk), inputs=(a, b),
        in_specs=[pl.BlockSpec((tk, tm), lambda j, k: (k, j)), pl.BlockSpec((tk, n), lambda j, k: (k, 0))],
        out_specs=(pl.BlockSpec((tm, n), lambda j, k: (j, 0)),),
        out_shape=(jax.ShapeDtypeStruct((m, n), F32),))
    return out, comm_results


def _in_proj_bwd(x, g1, dx1, dq_f, dq_b, dk_f, dk_b, dv_f, dv_b, dpg, dpu, dpv, dlr_f, dlr_b, w_main, tm, comms=()):
    t = x.shape[0]

    def body(x_ref, g_ref, dx1_ref, dqf, dqb, dkf, dkb, dvf, dvb, dg, du, dv, dlf, dlb, w_ref,
             dx_ref, dp_ref, dg1_ref):
        @pl.when(pl.program_id(0) == 0)
        def _():
            dg1_ref[...] = jnp.zeros_like(dg1_ref)

        both = lambda a, b: (a[...].astype(F32) + b[...].astype(F32)).astype(BF16)
        dp = jnp.concatenate([both(dqf, dqb), both(dkf, dkb), both(dvf, dvb), dg[...], du[...], dv[...],
                              both(dlf, dlb)], axis=1)
        dp_ref[...] = dp
        dh = _mm(dp, w_ref[...])
        xv = x_ref[...]
        r = lax.rsqrt(jnp.mean(xv * xv, axis=-1, keepdims=True) + EPS)
        xn = xv * r
        dg1_ref[...] += jnp.sum(dh * xn, axis=0, keepdims=True)
        dx_ref[...] = dx1_ref[...] + _rms_bwd(dh * g_ref[...], xn, r)

    row = lambda w: pl.BlockSpec((tm, w), lambda i: (i, 0))
    return _fused_call(
        body, comms, name="in_proj_bwd", grid=(t // tm,),
        inputs=(x, g1, dx1, dq_f, dq_b, dk_f, dk_b, dv_f, dv_b, dpg, dpu, dpv, dlr_f, dlr_b, w_main),
        in_specs=[row(D_MODEL), _const_spec((1, D_MODEL)), row(D_MODEL), row(KEY_W), row(KEY_W), row(KEY_W),
                  row(KEY_W), row(VAL_W), row(VAL_W), row(VAL_W), row(VAL_W), row(VAL_W), row(LANE), row(LANE),
                  _const_spec((PROJ_PAD, D_MODEL))],
        out_specs=(row(D_MODEL), row(PROJ_PAD), _acc_spec((1, D_MODEL))),
        out_shape=(jax.ShapeDtypeStruct((t, D_MODEL), F32), jax.ShapeDtypeStruct((t, PROJ_PAD), BF16),
                   jax.ShapeDtypeStruct((1, D_MODEL), F32)))


def _adamw(w, g, m, v):
    m_new = ADAM_B1 * m + (1.0 - ADAM_B1) * g
    v_new = ADAM_B2 * v + (1.0 - ADAM_B2) * (g * g)
    m_hat = m_new / (1.0 - ADAM_B1 ** ADAM_STEP)
    v_hat = v_new / (1.0 - ADAM_B2 ** ADAM_STEP)
    delta = -ADAM_LR * (m_hat / (jnp.sqrt(v_hat) + ADAM_EPS) + ADAM_WD * w)
    return delta, m_new, v_new


def _adamw_shard(own, recv, w, m, v, tr, name):
    r, c = w.shape

    def body(own_ref, recv_ref, w_ref, m_ref, v_ref, g_ref, d_ref, nm_ref, nv_ref):
        g = own_ref[...]
        for k in range(3):
            g = g + recv_ref[k].astype(F32)
        g_ref[...] = g
        d_ref[...], nm_ref[...], nv_ref[...] = _adamw(w_ref[...], g, m_ref[...], v_ref[...])

    row = pl.BlockSpec((tr, c), lambda i: (i, 0))
    return pl.pallas_call(
        body, name=name, grid=(r // tr,),
        in_specs=[row, pl.BlockSpec((3, tr, c), lambda i: (0, i, 0)), row, row, row],
        out_specs=(row,) * 4, out_shape=(jax.ShapeDtypeStruct((r, c), F32),) * 4,
        compiler_params=_params(),
    )(own, recv, w, m, v)


def _adamw_small(g, w, m, v):
    def body(g_ref, w_ref, m_ref, v_ref, d_ref, nm_ref, nv_ref):
        d_ref[...], nm_ref[...], nv_ref[...] = _adamw(w_ref[...], g_ref[...], m_ref[...], v_ref[...])

    return pl.pallas_call(body, name="adamw_small", out_shape=(jax.ShapeDtypeStruct(g.shape, F32),) * 3,
                          compiler_params=pltpu.CompilerParams(vmem_limit_bytes=VMEM_LIMIT))(g, w, m, v)


def _mesh_pos():
    return lax.axis_index("x"), lax.axis_index("y"), lax.axis_index("c")


def _other_chips(x, y):
    return [(x, 1 - y), (1 - x, y), (1 - x, 1 - y)]


_VMEM_WHOLE = pl.BlockSpec(memory_space=pltpu.VMEM)
_HBM_WHOLE = pl.BlockSpec(memory_space=pl.ANY)


def _gather_comm(shards, cast, mid=(1, 2)):
    na = len(shards)
    staged = [a for a in range(na) if cast[a]]

    def phases(in_refs, out_refs, scr):
        stage = dict(zip(staged, scr[:len(staged)]))
        send_sems, recv_sems, local_sems = scr[len(staged):]
        x, y, c = _mesh_pos()
        me, sibling = (x, y, c), (x, y, 1 - c)
        chips = _other_chips(x, y)
        srcs = [stage[a] if cast[a] else in_refs[a] for a in range(na)]

        def rows(a, pos):
            px, py, pc = pos
            return out_refs[a].at[4 * px + 2 * py + pc]

        def copy(a, k, block, to, src=None):
            return pltpu.make_async_remote_copy(
                src_ref=rows(a, block) if src is None else src, dst_ref=rows(a, block),
                send_sem=send_sems.at[a, k], recv_sem=recv_sems.at[a, k], device_id=to, device_id_type=MESH_ID)

        mine = [pltpu.make_async_copy(srcs[a], rows(a, me), local_sems.at[a]) for a in range(na)]
        first = []
        for a in range(na):
            first.append(copy(a, 0, me, sibling, src=srcs[a]))
            first += [copy(a, 1 + j, me, (*chip, c), src=srcs[a]) for j, chip in enumerate(chips)]
        passed = [copy(a, 4 + j, (*chip, c), sibling) for j, chip in enumerate(chips) for a in range(na)]

        def start():
            for a in staged:
                stage[a][...] = in_refs[a][...].astype(BF16)
            for cp in mine + first:
                cp.start()

        def forward():
            i = 0
            for j, chip in enumerate(chips):
                for a in range(na):
                    copy(a, 1 + j, (*chip, c), me).wait_recv()
                    passed[i].start()
                    i += 1

        def finish():
            for a in range(na):
                copy(a, 0, sibling, me).wait_recv()
                for j, chip in enumerate(chips):
                    copy(a, 4 + j, (*chip, 1 - c), me).wait_recv()
            for cp in first + passed:
                cp.wait_send()
            for cp in mine:
                cp.wait()

        return start, forward, finish

    def before(step, nsteps, in_refs, out_refs, scr):
        start, forward, _ = phases(in_refs, out_refs, scr)
        pl.when(step == 0)(start)
        pl.when(step == nsteps * mid[0] // mid[1])(forward)

    def after(step, nsteps, in_refs, out_refs, scr):
        pl.when(step == nsteps - 1)(phases(in_refs, out_refs, scr)[2])

    return _Comm(
        inputs=list(shards), in_specs=[_VMEM_WHOLE] * na,
        out_shape=[jax.ShapeDtypeStruct((N_DEV,) + s.shape, BF16 if cast[a] else s.dtype)
                   for a, s in enumerate(shards)],
        out_specs=[_HBM_WHOLE] * na,
        scratch_shapes=[pltpu.VMEM(shards[a].shape, BF16) for a in staged] + [
            pltpu.SemaphoreType.DMA((na, 7)), pltpu.SemaphoreType.DMA((na, 7)), pltpu.SemaphoreType.DMA((na,))],
        before=before, after=after)


def _exchange_comm(arrays, out_shape, make_copies):
    na = len(arrays)

    def copies(in_refs, out_refs, scr):
        return make_copies(in_refs, out_refs, *scr)

    def before(step, nsteps, in_refs, out_refs, scr):
        @pl.when(step == 0)
        def _():
            for cp in copies(in_refs, out_refs, scr):
                cp.start()

    def after(step, nsteps, in_refs, out_refs, scr):
        @pl.when(step == nsteps - 1)
        def _():
            for cp in copies(in_refs, out_refs, scr):
                cp.wait()

    return _Comm(inputs=list(arrays), in_specs=[_HBM_WHOLE] * na, out_shape=list(out_shape),
                 out_specs=[_HBM_WHOLE] * na,
                 scratch_shapes=[pltpu.SemaphoreType.DMA((na, 3)), pltpu.SemaphoreType.DMA((na, 3))],
                 before=before, after=after)


def _sibling_exchange_comm(grads):
    def make_copies(in_refs, out_refs, send_sems, recv_sems):
        x, y, c = _mesh_pos()
        return [pltpu.make_async_remote_copy(
            src_ref=in_refs[a].at[:, pl.ds(1 - c, 1)], dst_ref=out_refs[a], send_sem=send_sems.at[a, 0],
            recv_sem=recv_sems.at[a, 0], device_id=(x, y, 1 - c), device_id_type=MESH_ID)
            for a in range(len(grads))]

    return _exchange_comm(grads, [jax.ShapeDtypeStruct((4, 1) + g.shape[2:], F32) for g in grads], make_copies)


def _chips_exchange_comm(partials):
    def make_copies(in_refs, out_refs, send_sems, recv_sems):
        x, y, c = _mesh_pos()
        return [pltpu.make_async_remote_copy(
            src_ref=in_refs[a].at[j], dst_ref=out_refs[a].at[j], send_sem=send_sems.at[a, j],
            recv_sem=recv_sems.at[a, j], device_id=(*chip, c), device_id_type=MESH_ID)
            for a in range(len(partials)) for j, chip in enumerate(_other_chips(x, y))]

    return _exchange_comm(partials, [jax.ShapeDtypeStruct(g.shape, BF16) for g in partials], make_copies)


def _comm_only(comms, name):
    return _fused_call(lambda: None, comms, name=name, grid=(1,), inputs=(), in_specs=[], out_specs=(),
                       out_shape=())[1]


def _chip_sum(my_pos, mine, from_sibling, tr, name):
    _, _, r, c = mine.shape

    def body(pos_ref, a_ref, b_ref, own_ref, out_ref):
        s = a_ref[0, 0] + b_ref[0, 0]

        @pl.when(pl.program_id(1) == 0)
        def _():
            own_ref[...] = s

        @pl.when(pl.program_id(1) > 0)
        def _():
            out_ref[0] = s.astype(BF16)

    grid_spec = pltpu.PrefetchScalarGridSpec(
        num_scalar_prefetch=1, grid=(r // tr, 4),
        in_specs=[pl.BlockSpec((1, 1, tr, c), lambda i, k, pos: (pos[0] ^ k, pos[1], i, 0)),
                  pl.BlockSpec((1, 1, tr, c), lambda i, k, pos: (pos[0] ^ k, 0, i, 0))],
        out_specs=(pl.BlockSpec((tr, c), lambda i, k, pos: (i, 0)),
                   pl.BlockSpec((1, tr, c), lambda i, k, pos: (jnp.maximum(k - 1, 0), i, 0))))
    return pl.pallas_call(
        body, name=name, grid_spec=grid_spec,
        out_shape=(jax.ShapeDtypeStruct((r, c), F32), jax.ShapeDtypeStruct((3, r, c), BF16)),
        compiler_params=_params(2),
    )(my_pos, mine, from_sibling)


def _all_reduce_small_comm(part):
    def copies(in_ref, gathered, send_sems, recv_sems):
        x, y, c = _mesh_pos()
        my_id = 4 * x + 2 * y + c
        return my_id, [pltpu.make_async_remote_copy(
            src_ref=in_ref, dst_ref=gathered.at[my_id], send_sem=send_sems.at[k - 1],
            recv_sem=recv_sems.at[k - 1], device_id=(x ^ (k >> 2), y ^ ((k >> 1) & 1), c ^ (k & 1)),
            device_id_type=MESH_ID) for k in range(1, N_DEV)]

    def before(step, nsteps, in_refs, out_refs, scr):
        @pl.when(step == 0)
        def _():
            for cp in copies(in_refs[0], *scr)[1]:
                cp.start()

    def after(step, nsteps, in_refs, out_refs, scr):
        @pl.when(step == nsteps - 1)
        def _():
            gathered = scr[0]
            my_id, cps = copies(in_refs[0], *scr)
            gathered[my_id] = in_refs[0][...]
            for cp in cps:
                cp.wait()
            acc = gathered[0]
            for d in range(1, N_DEV):
                acc = acc + gathered[d]
            out_refs[0][...] = acc

    return _Comm(inputs=[part], in_specs=[_VMEM_WHOLE], out_shape=[jax.ShapeDtypeStruct(part.shape, F32)],
                 out_specs=[_VMEM_WHOLE],
                 scratch_shapes=[pltpu.VMEM((N_DEV,) + part.shape, F32), pltpu.SemaphoreType.DMA((N_DEV - 1,)),
                                 pltpu.SemaphoreType.DMA((N_DEV - 1,))],
                 before=before, after=after)


def _unshard_cols(g):
    return jnp.transpose(g, (1, 0, 2)).reshape(g.shape[1], N_DEV * g.shape[2])


def _row_blocks(w):
    return w.reshape(4, 2, w.shape[0] // N_DEV, w.shape[1])


def _pack_small(parts):
    flat = jnp.concatenate([a.reshape(-1) for a in parts])
    pad = (-flat.shape[0]) % (8 * LANE)
    return jnp.pad(flat, (0, pad)).reshape(-1, LANE)


def _unpack_small(packed, shapes):
    flat = packed.reshape(-1)
    out, off = [], 0
    for s in shapes:
        n = math.prod(s)
        out.append(flat[off:off + n].reshape(s))
        off += n
    return out


PROJ_W = 2592
LR_REF = 1536


def _main_proj_weight(w_in_t):
    return jnp.concatenate([w_in_t[:LR_REF], w_in_t[LR_REF + 2 * LOWRANK:], w_in_t[LR_REF:LR_REF + 2 * LOWRANK],
                            jnp.zeros((PROJ_PAD - PROJ_W, D_MODEL), w_in_t.dtype)], axis=0)


def _main_proj_grad(dw_main_t):
    return jnp.concatenate([dw_main_t[:LR_REF], dw_main_t[LR_COL:LR_COL + 2 * LOWRANK], dw_main_t[LR_REF:LR_COL]],
                           axis=0)


def _padded_decay_weights(wd_f, wd_b):
    zeros = lambda n: jnp.zeros((n, KEY_W), F32)
    return (jnp.concatenate([wd_f, zeros(LANE - LOWRANK)], axis=0),
            jnp.concatenate([zeros(LOWRANK), wd_b, zeros(LANE - 2 * LOWRANK)], axis=0))


def kernel(x, norm1_g, w_in,w_decay_f, b_decay_f, w_decay_b, b_decay_b, gla_norm_g, gmlp_ln_g, gmlp_ln_b, w_spatial, b_spatial, w_out, norm2_g, w_gate, w_up, w_down, final_norm_g, loss_target, m_norm1_g, m_w_in, m_w_decay_f, m_b_decay_f, m_w_decay_b, m_b_decay_b, m_gla_norm_g, m_gmlp_ln_g, m_gmlp_ln_b, m_w_spatial, m_b_spatial, m_w_out, m_norm2_g, m_w_gate, m_w_up, m_w_down, m_final_norm_g, v_norm1_g, v_w_in, v_w_decay_f, v_b_decay_f, v_w_decay_b, v_b_decay_b, v_gla_norm_g, v_gmlp_ln_g, v_gmlp_ln_b, v_w_spatial, v_b_spatial, v_w_out, v_norm2_g, v_w_gate, v_w_up, v_w_down, v_final_norm_g):
    t = x.shape[1]
    xt = x[0]
    target = loss_target[0]
    pos_x, pos_y, pos_c = _mesh_pos()
    my_pos = jnp.stack([2 * pos_x + pos_y, pos_c]).astype(jnp.int32)
    my_id = 4 * pos_x + 2 * pos_y + pos_c

    tile = lambda n: min(n, t)
    ln_g, ln_b, w_sp = gmlp_ln_g, gmlp_ln_b, w_spatial[0]
    b_sp_col = b_spatial[0][:, :, None]
    shard = {"w_in": w_in[0].T, "w_out": w_out[0], "w_gate": w_gate[0].T, "w_up": w_up[0].T, "w_down": w_down[0]}
    shard_m = {"w_in": m_w_in[0].T, "w_out": m_w_out[0], "w_gate": m_w_gate[0].T, "w_up": m_w_up[0].T,
               "w_down": m_w_down[0]}
    shard_v = {"w_in": v_w_in[0].T, "w_out": v_w_out[0], "w_gate": v_w_gate[0].T, "w_up": v_w_up[0].T,
               "w_down": v_w_down[0]}
    transposed = ("w_in", "w_gate", "w_up")
    chip_sum = lambda n, g, s: _chip_sum(my_pos, g, s[0], g.shape[2], "chip_sum_" + n)

    decay_shard = jnp.stack([w_decay_f[0], w_decay_b[0]])
    ((g_in, g_decay),) = _comm_only([_gather_comm([shard["w_in"], decay_shard], [True, False])], "all_gather_w_in")
    w_main = _main_proj_weight(g_in.reshape(PROJ_W, D_MODEL))
    wd_pad_f, wd_pad_b = _padded_decay_weights(_unshard_cols(g_decay[:, 0]), _unshard_cols(g_decay[:, 1]))
    (p, hb), ((g_gate,),) = _in_proj(xt, norm1_g, w_main, tile(1024), [_gather_comm([shard["w_gate"]], [True])])
    (o_f, st_f, o_b, st_b), ((g_up, g_out, g_down),) = _gla_fwd(
        p, wd_pad_f, b_decay_f, wd_pad_b, b_decay_b, tile(512),
        [_gather_comm([shard["w_up"], shard["w_out"], shard["w_down"]], [True, True, True])])
    w_out_full = g_out.reshape(D_MODEL, D_MODEL)
    x1, ycat = _mix_fwd(xt, o_f, o_b, p, gla_norm_g, ln_g, ln_b, w_sp, b_sp_col, w_out_full, tile(512))

    dx1, h2b, dgate, dup, act, dx2, loss_acc, d_gf, d_g2 = _ffn(
        x1, target, norm2_g, final_norm_g[None, :], g_gate.reshape(D_FF, D_MODEL), g_up.reshape(D_FF, D_MODEL),
        g_down.reshape(D_FF, D_MODEL), tile(256))
    dw_gate, _ = _matmul_tn(dgate, h2b, D_FF // 2, tile(2048), "grad_w_gate")
    dw_up, _ = _matmul_tn(dup, h2b, D_FF // 2, tile(2048), "grad_w_up")
    dw_down, _ = _matmul_tn(act, dx2, D_FF // 2, tile(2048), "grad_w_down")

    ffn_grads = [_row_blocks(dw_gate), _row_blocks(dw_up), _row_blocks(dw_down)]
    (d_o, dpg, dpu, dpv, dw_out, d_gg, d_lg, d_lb, dw_sp, db_sp), (ffn_sib,) = _mix_bwd(
        dx1, ycat, o_f, o_b, p, gla_norm_g, ln_g, ln_b, w_sp, b_sp_col, w_out_full, tile(256),
        [_sibling_exchange_comm(ffn_grads)])
    ffn_names = ["w_gate", "w_up", "w_down"]
    ffn_sums = [chip_sum(n, g, [s]) for n, g, s in zip(ffn_names, ffn_grads, ffn_sib)]
    out_grad = _row_blocks(dw_out)
    (dq_f, dk_f, dv_f, dlr_f, dwd_f, dbd_f, dq_b, dk_b, dv_b, dlr_b, dwd_b, dbd_b), (ffn_recv, out_sib) = _gla_bwd(
        p, wd_pad_f, b_decay_f, wd_pad_b, b_decay_b, st_f, st_b, d_o, tile(512),
        [_chips_exchange_comm([s[1] for s in ffn_sums]), _sibling_exchange_comm([out_grad])])
    out_sum = chip_sum("w_out", out_grad, out_sib)
    (grad_x, dp, d_g1), (out_recv,) = _in_proj_bwd(
        xt, norm1_g, dx1, dq_f, dq_b, dk_f, dk_b, dv_f, dv_b, dpg, dpu, dpv, dlr_f, dlr_b, w_main, tile(512),
        [_chips_exchange_comm([out_sum[1]])])

    small_shapes = [(1, D_MODEL), (LOWRANK, KEY_W), (1, KEY_W), (LOWRANK, KEY_W), (1, KEY_W), (1, VAL_W),
                    (1, GMLP_W), (1, GMLP_W), (1, GMLP_GROUPS, GMLP_CHUNK, GMLP_CHUNK), (1, GMLP_GROUPS, GMLP_CHUNK),
                    (1, D_MODEL), (D_MODEL,), (LANE,)]
    small_parts = [d_g1, dwd_f[:LOWRANK], dbd_f, dwd_b[LOWRANK:2 * LOWRANK], dbd_b, d_gg, d_lg, d_lb, dw_sp,
                   db_sp, d_g2, d_gf, loss_acc[0]]
    dw_main, ((small_sum,),) = _matmul_tn(dp, hb, PROJ_PAD // 3, tile(2048), "grad_w_in",
                                           [_all_reduce_small_comm(_pack_small(small_parts))])
    in_grad = _row_blocks(_main_proj_grad(dw_main))
    (in_sib,) = _comm_only([_sibling_exchange_comm([in_grad])], "grad_w_in_exchange_sibling")
    in_sum = chip_sum("w_in", in_grad, in_sib)
    (in_recv,) = _comm_only([_chips_exchange_comm([in_sum[1]])], "grad_w_in_exchange_chips")

    names = ["w_in", "w_out", "w_gate", "w_up", "w_down"]
    sums = [in_sum, out_sum] + ffn_sums
    received = [in_recv[0], out_recv[0]] + list(ffn_recv)
    big_out = {}
    for n, s, rc in zip(names, sums, received):
        res = _adamw_shard(s[0], rc, shard[n], shard_m[n], shard_v[n], shard[n].shape[0], "adamw_" + n)
        big_out[n] = [r.T if n in transposed else r for r in res]

    reduced = _unpack_small(small_sum, small_shapes)
    loss = reduced[-1][0]
    col0 = my_id * (KEY_W // N_DEV)
    g_small = list(reduced[:-1])
    g_small[1] = lax.dynamic_slice_in_dim(g_small[1], col0, KEY_W // N_DEV, axis=1)[None]
    g_small[3] = lax.dynamic_slice_in_dim(g_small[3], col0, KEY_W // N_DEV, axis=1)[None]
    small_w = [norm1_g, w_decay_f, b_decay_f, w_decay_b, b_decay_b, gla_norm_g, gmlp_ln_g, gmlp_ln_b, w_spatial,
               b_spatial, norm2_g, final_norm_g]
    small_m = [m_norm1_g, m_w_decay_f, m_b_decay_f, m_w_decay_b, m_b_decay_b, m_gla_norm_g, m_gmlp_ln_g,
               m_gmlp_ln_b, m_w_spatial, m_b_spatial, m_norm2_g, m_final_norm_g]
    small_v = [v_norm1_g, v_w_decay_f, v_b_decay_f, v_w_decay_b, v_b_decay_b, v_gla_norm_g, v_gmlp_ln_g,
               v_gmlp_ln_b, v_w_spatial, v_b_spatial, v_norm2_g, v_final_norm_g]
    shapes = [w.shape for w in small_w]
    packed = _adamw_small(_pack_small(g_small), _pack_small(small_w), _pack_small(small_m), _pack_small(small_v))
    s_delta, s_m, s_v = (_unpack_small(a, shapes) for a in packed)
    s_grad = [g.reshape(s) for g, s in zip(g_small, shapes)]

    order = ["norm1_g", "w_in", "w_decay_f", "b_decay_f", "w_decay_b", "b_decay_b", "gla_norm_g", "gmlp_ln_g",
             "gmlp_ln_b", "w_spatial", "b_spatial", "w_out", "norm2_g", "w_gate", "w_up", "w_down", "final_norm_g"]
    small_names = [n for n in order if n not in big_out]
    small_out = {n: (s_grad[i], s_delta[i], s_m[i], s_v[i]) for i, n in enumerate(small_names)}
    outs = []
    for kind in range(4):
        for n in order:
            outs.append(big_out[n][kind][None] if n in big_out else small_out[n][kind])
    return (loss, grad_x[None], *outs)
```

```python
import functools
import math

import jax
import jax.numpy as jnp
from jax import lax
from jax.experimental import pallas as pl
from jax.experimental.pallas import tpu as pltpu

F32 = jnp.float32
BF16 = jnp.bfloat16

D_MODEL = 1024
GLA_HEADS = 4
GLA_DK = 64
GLA_DV = 128
KEY_W = GLA_HEADS * GLA_DK
VAL_W = GLA_HEADS * GLA_DV
LOWRANK = 16
GLA_TAU = 16.0
GLA_CHUNK = 64
GMLP_W = 512
GMLP_GROUPS = 4
GMLP_CHUNK = 128
D_FF = 2816
EPS = 1e-6
Q_SCALE = GLA_DK ** -0.5
PROJ_PAD = 2688
LR_COL = 2560
LANE = 128
N_DEV = 8

ADAM_LR = 0.001
ADAM_B1 = 0.9
ADAM_B2 = 0.999
ADAM_EPS = 1e-08
ADAM_WD = 0.01
ADAM_STEP = 10

VMEM_LIMIT = 56 * 1024 * 1024
MESH_ID = pl.DeviceIdType.MESH
INV_SQRT2 = 0.7071067811865476
INV_SQRT_2PI = 0.3989422804014327


def _params(n_axes=1):
    return pltpu.CompilerParams(dimension_semantics=("arbitrary",) * n_axes, vmem_limit_bytes=VMEM_LIMIT)


def _mm(a, b):
    return jnp.dot(a.astype(BF16), b.astype(BF16), preferred_element_type=F32)


def _mm_nt(a, b):
    return lax.dot_general(a.astype(BF16), b.astype(BF16), (((1,), (1,)), ((), ())), preferred_element_type=F32)


def _mm_tn(a, b):
    return lax.dot_general(a.astype(BF16), b.astype(BF16), (((0,), (0,)), ((), ())), preferred_element_type=F32)


def _const_spec(shape):
    nd = len(shape)
    return pl.BlockSpec(shape, lambda *_: (0,) * nd, pipeline_mode=pl.Buffered(1))


def _acc_spec(shape):
    nd = len(shape)
    return pl.BlockSpec(shape, lambda *_: (0,) * nd)


class _Comm:
    def __init__(self, inputs, in_specs, out_shape, out_specs, scratch_shapes, before, after):
        self.inputs, self.in_specs, self.out_shape, self.out_specs = inputs, in_specs, out_shape, out_specs
        self.scratch_shapes, self.before, self.after = scratch_shapes, before, after


def _fused_call(body, comms, *, name, grid, inputs, in_specs, out_specs, out_shape, scratch_shapes=()):
    n_in, n_out, n_scr = len(in_specs), len(out_specs), len(scratch_shapes)
    nsteps = math.prod(grid)
    sizes = [(len(c.inputs), len(c.out_shape), len(c.scratch_shapes)) for c in comms]

    def full_body(*refs):
        step = pl.program_id(0)
        for axis in range(1, len(grid)):
            step = step * grid[axis] + pl.program_id(axis)
        ins, rest = refs[:n_in], refs[n_in:]
        c_ins = []
        for ci, _, _ in sizes:
            c_ins.append(rest[:ci])
            rest = rest[ci:]
        outs, rest = rest[:n_out], rest[n_out:]
        c_outs = []
        for _, co, _ in sizes:
            c_outs.append(rest[:co])
            rest = rest[co:]
        scr, rest = rest[:n_scr], rest[n_scr:]
        c_scr = []
        for _, _, cs in sizes:
            c_scr.append(rest[:cs])
            rest = rest[cs:]
        for c, a, b, s in zip(comms, c_ins, c_outs, c_scr):
            c.before(step, nsteps, a, b, s)
        body(*ins, *outs, *scr)
        for c, a, b, s in zip(comms, c_ins, c_outs, c_scr):
            c.after(step, nsteps, a, b, s)

    results = pl.pallas_call(
        full_body, name=name, grid=grid,
        in_specs=list(in_specs) + [s for c in comms for s in c.in_specs],
        out_specs=tuple(out_specs) + tuple(s for c in comms for s in c.out_specs),
        out_shape=tuple(out_shape) + tuple(s for c in comms for s in c.out_shape),
        scratch_shapes=list(scratch_shapes) + [s for c in comms for s in c.scratch_shapes],
        compiler_params=_params(len(grid)),
    )(*inputs, *[a for c in comms for a in c.inputs])
    own, rest = results[:n_out], results[n_out:]
    comm_results = []
    for _, co, _ in sizes:
        comm_results.append(rest[:co])
        rest = rest[co:]
    return own, comm_results


def _gelu(x):
    return 0.5 * x * (1.0 + lax.erf(x * INV_SQRT2))


def _gelu_and_grad(x):
    cdf = 0.5 * (1.0 + lax.erf(x * INV_SQRT2))
    return x * cdf, cdf + x * jnp.exp(-0.5 * x * x) * INV_SQRT_2PI


def _silu_and_grad(x):
    s = jax.nn.sigmoid(x)
    return x * s, s * (1.0 + x * (1.0 - s))


def _in_proj(x, g1, w_main, tm, comms=()):
    t = x.shape[0]

    def body(x_ref, g_ref, w_ref, p_ref, h_ref):
        xv = x_ref[...]
        r = lax.rsqrt(jnp.mean(xv * xv, axis=-1, keepdims=True) + EPS)
        h = (xv * r * g_ref[...]).astype(BF16)
        h_ref[...] = h
        p_ref[...] = _mm_nt(h, w_ref[...]).astype(BF16)

    return _fused_call(
        body, comms, name="in_proj", grid=(t // tm,), inputs=(x, g1, w_main),
        in_specs=[pl.BlockSpec((tm, D_MODEL), lambda i: (i, 0)), _const_spec((1, D_MODEL)),
                  _const_spec((PROJ_PAD, D_MODEL))],
        out_specs=(pl.BlockSpec((tm, PROJ_PAD), lambda i: (i, 0)), pl.BlockSpec((tm, D_MODEL), lambda i: (i, 0))),
        out_shape=(jax.ShapeDtypeStruct((t, PROJ_PAD), BF16), jax.ShapeDtypeStruct((t, D_MODEL), BF16)))


def _tri(upper):
    r = lax.broadcasted_iota(jnp.int32, (GLA_CHUNK, GLA_CHUNK), 0)
    c = lax.broadcasted_iota(jnp.int32, (GLA_CHUNK, GLA_CHUNK), 1)
    return jnp.where((c >= r) if upper else (c <= r), 1.0, 0.0).astype(BF16)


def _tri_matmul(tri, a):
    a1 = a.astype(BF16)
    r1 = a - a1.astype(F32)
    a2 = r1.astype(BF16)
    a3 = (r1 - a2.astype(F32)).astype(BF16)
    dot = functools.partial(jnp.dot, preferred_element_type=F32)
    return dot(tri, a1) + dot(tri, a2) + dot(tri, a3)


def _gla_masks(rev):
    dk_bits, dv_bits = GLA_DK.bit_length() - 1, GLA_DV.bit_length() - 1
    key_head = lax.broadcasted_iota(jnp.int32, (GLA_CHUNK, KEY_W), 1) >> dk_bits
    val_head = lax.broadcasted_iota(jnp.int32, (GLA_CHUNK, VAL_W), 1) >> dv_bits
    t = lax.broadcasted_iota(jnp.int32, (GLA_HEADS * GLA_CHUNK, GLA_CHUNK), 0) & (GLA_CHUNK - 1)
    s = lax.broadcasted_iota(jnp.int32, (GLA_HEADS * GLA_CHUNK, GLA_CHUNK), 1)
    causal = (s >= t) if rev else (s <= t)
    state_head = lax.broadcasted_iota(jnp.int32, (GLA_DV, KEY_W), 1) >> dk_bits
    return key_head, val_head, causal, state_head


def _stack_heads(a, head_of_lane):
    return jnp.concatenate([jnp.where(head_of_lane == h, a, 0.0) for h in range(GLA_HEADS)], axis=0)


def _rows_by_head(a):
    return jnp.concatenate([a[:, h * GLA_DV:(h + 1) * GLA_DV] for h in range(GLA_HEADS)], axis=0)


def _lanes_by_head(r):
    return jnp.concatenate([r[h * GLA_CHUNK:(h + 1) * GLA_CHUNK] for h in range(GLA_HEADS)], axis=1)


def _head_diagonal(r, head_of_lane):
    rows = r.shape[0] // GLA_HEADS
    out = jnp.where(head_of_lane == 0, r[:rows], 0.0)
    for h in range(1, GLA_HEADS):
        out = out + jnp.where(head_of_lane == h, r[h * rows:(h + 1) * rows], 0.0)
    return out


def _chunk_terms(la_c, q_c, k_c, tri, rev):
    q_c, k_c = q_c.astype(F32), k_c.astype(F32)
    b = _tri_matmul(tri, la_c)
    bl = b[0:1] if rev else b[GLA_CHUNK - 1:GLA_CHUNK]
    eb = jnp.exp(b)
    enb = jnp.exp(-b)
    ee = jnp.exp(bl - b)
    return bl, eb, enb, ee, q_c * Q_SCALE * eb, k_c * enb, k_c * ee


def _log_decay(lr_ref, wd_ref, bd_ref):
    z = _mm(lr_ref[...], wd_ref[...]) + bd_ref[...]
    return z, jax.nn.log_sigmoid(z) * (1.0 / GLA_TAU)


def _p_specs(tg, tile):
    return [pl.BlockSpec((tg, KEY_W), lambda i: (tile(i), 0)),
            pl.BlockSpec((tg, KEY_W), lambda i: (tile(i), 1)),
            pl.BlockSpec((tg, VAL_W), lambda i: (tile(i), 1)),
            pl.BlockSpec((tg, LANE), lambda i: (tile(i), LR_COL // LANE))]


def _gla_fwd_dir(rev, nc, q_ref, k_ref, v_ref, lr_ref, wd_ref, bd_ref, o_ref, st_ref, state):
    key_head, _, causal, state_head = _gla_masks(rev)
    tri = _tri(rev)
    _, la = _log_decay(lr_ref, wd_ref, bd_ref)

    def step(cc):
        c = nc - 1 - cc if rev else cc
        rows = slice(c * GLA_CHUNK, (c + 1) * GLA_CHUNK)
        v_c = v_ref[rows, :].astype(BF16)
        bl, _, _, _, qd, kd, ke = _chunk_terms(la[rows], q_ref[rows, :], k_ref[rows, :], tri, rev)
        qd_stack = _stack_heads(qd, key_head).astype(BF16)
        a_all = jnp.where(causal, _mm_nt(qd_stack, kd), 0.0)
        r = _mm(a_all, v_c)
        st = state[...]
        r_inter = _mm_nt(qd_stack, st)
        o_ref[rows, :] = jnp.concatenate(
            [r[h * GLA_CHUNK:(h + 1) * GLA_CHUNK, h * GLA_DV:(h + 1) * GLA_DV]
             + r_inter[h * GLA_CHUNK:(h + 1) * GLA_CHUNK] for h in range(GLA_HEADS)], axis=1)
        st_ref[c] = st.astype(BF16)
        state[...] = st * jnp.exp(bl) + _head_diagonal(_mm_tn(v_c, ke), state_head)

    return step


def _gla_fwd(p, wd_pad_f, bd_f, wd_pad_b, bd_b, tg, comms=()):
    t = p.shape[0]
    nt = t // tg
    nc = tg // GLA_CHUNK
    up, down = (lambda i: i), (lambda i: nt - 1 - i)

    def body(qf, kf, vf, lrf, qb, kb, vb, lrb, wdf, bdf, wdb, bdb, of, stf, ob, stb, state_f, state_b):
        @pl.when(pl.program_id(0) == 0)
        def _():
            state_f[...] = jnp.zeros_like(state_f)
            state_b[...] = jnp.zeros_like(state_b)

        steps = [_gla_fwd_dir(False, nc, qf, kf, vf, lrf, wdf, bdf, of, stf, state_f),
                 _gla_fwd_dir(True, nc, qb, kb, vb, lrb, wdb, bdb, ob, stb, state_b)]
        for cc in range(nc):
            for step in steps:
                step(cc)

    wd_spec, bd_spec = _const_spec((LANE, KEY_W)), _const_spec((1, KEY_W))
    outs = lambda tile: (pl.BlockSpec((tg, VAL_W), lambda i: (tile(i), 0)),
                         pl.BlockSpec((nc, GLA_DV, KEY_W), lambda i: (tile(i), 0, 0)))
    out_shape = (jax.ShapeDtypeStruct((t, VAL_W), F32), jax.ShapeDtypeStruct((t // GLA_CHUNK, GLA_DV, KEY_W), BF16))
    return _fused_call(
        body, comms, name="gla_fwd", grid=(nt,), inputs=(p,) * 8 + (wd_pad_f, bd_f, wd_pad_b, bd_b),
        in_specs=_p_specs(tg, up) + _p_specs(tg, down) + [wd_spec, bd_spec, wd_spec, bd_spec],
        out_specs=outs(up) + outs(down), out_shape=out_shape * 2,
        scratch_shapes=[pltpu.VMEM((GLA_DV, KEY_W), F32)] * 2)


def _gla_bwd_dir(rev, nc, q_ref, k_ref, v_ref, lr_ref, wd_ref, bd_ref, st_ref, do_ref,
                 dq_ref, dk_ref, dv_ref, dlr_ref, dwd_ref, dbd_ref, dstate, dz_scr):
    key_head, val_head, causal, state_head = _gla_masks(rev)
    tri = _tri(rev)
    tri_t = _tri(not rev)
    z, la = _log_decay(lr_ref, wd_ref, bd_ref)
    dlog = jax.nn.sigmoid(-z) * (1.0 / GLA_TAU)

    def step(cc):
        c = cc if rev else nc - 1 - cc
        rows = slice(c * GLA_CHUNK, (c + 1) * GLA_CHUNK)
        v_c = v_ref[rows, :].astype(BF16)
        do_c = do_ref[rows, :]
        bl, eb, enb, ee, qd, kd, ke = _chunk_terms(la[rows], q_ref[rows, :], k_ref[rows, :], tri, rev)
        qd_stack = _stack_heads(qd, key_head).astype(BF16)
        ke_stack = _stack_heads(ke, key_head).astype(BF16)
        do_stack = _stack_heads(do_c, val_head)
        a_all = jnp.where(causal, _mm_nt(qd_stack, kd), 0.0)
        da_all = jnp.where(causal, _mm_nt(do_stack, v_c), 0.0)
        dst = dstate[...]
        st_prev = st_ref[c]
        dv_ref[rows, :] = (_mm_tn(a_all, do_stack) + _lanes_by_head(_mm_nt(ke_stack, dst))).astype(BF16)
        dqd = _head_diagonal(_mm(da_all, kd) + _mm(_rows_by_head(do_c), st_prev), key_head)
        dkd = _mm_tn(da_all, qd_stack)
        dke = _head_diagonal(_mm(_rows_by_head(v_c), dst), key_head)
        ebl = jnp.exp(bl)
        dbl = (jnp.sum(dst * st_prev.astype(F32), axis=0, keepdims=True) * ebl
               + jnp.sum(dke * ke, axis=0, keepdims=True))
        dstate[...] = dst * ebl + _head_diagonal(_mm_tn(do_c, qd), state_head)
        dq_ref[rows, :] = (dqd * eb * Q_SCALE).astype(BF16)
        dk_ref[rows, :] = (dkd * enb + dke * ee).astype(BF16)
        db = dqd * qd - dkd * kd - dke * ke
        dz_scr[rows, :] = (_tri_matmul(tri_t, db) + dbl) * dlog[rows]

    def finish():
        dz = dz_scr[...]
        dlr_ref[...] = _mm_nt(dz, wd_ref[...]).astype(BF16)
        dwd_ref[...] += _mm_tn(lr_ref[...], dz)
        dbd_ref[...] += jnp.sum(dz, axis=0, keepdims=True)

    return step, finish


def _gla_bwd(p, wd_pad_f, bd_f, wd_pad_b, bd_b, st_f, st_b, d_o, tg, comms=()):
    t = p.shape[0]
    nt = t // tg
    nc = tg // GLA_CHUNK
    up, down = (lambda i: i), (lambda i: nt - 1 - i)

    def body(qf, kf, vf, lrf, stf, dof, qb, kb, vb, lrb, stb, dob, wdf, bdf, wdb, bdb,
             dqf, dkf, dvf, dlrf, dwdf, dbdf, dqb, dkb, dvb, dlrb, dwdb, dbdb,
             dstate_f, dz_f, dstate_b, dz_b):
        @pl.when(pl.program_id(0) == 0)
        def _():
            for ref in (dstate_f, dstate_b, dwdf, dbdf, dwdb, dbdb):
                ref[...] = jnp.zeros_like(ref)

        dirs = [_gla_bwd_dir(False, nc, qf, kf, vf, lrf, wdf, bdf, stf, dof, dqf, dkf, dvf, dlrf, dwdf, dbdf,
                             dstate_f, dz_f),
                _gla_bwd_dir(True, nc, qb, kb, vb, lrb, wdb, bdb, stb, dob, dqb, dkb, dvb, dlrb, dwdb, dbdb,
                             dstate_b, dz_b)]
        for cc in range(nc):
            for step, _ in dirs:
                step(cc)
        for _, finish in dirs:
            finish()

    wd_spec, bd_spec = _const_spec((LANE, KEY_W)), _const_spec((1, KEY_W))
    ins = lambda tile: _p_specs(tg, tile) + [pl.BlockSpec((nc, GLA_DV, KEY_W), lambda i: (tile(i), 0, 0)),
                                             pl.BlockSpec((tg, VAL_W), lambda i: (tile(i), 0))]
    outs = lambda tile: (pl.BlockSpec((tg, KEY_W), lambda i: (tile(i), 0)),
                         pl.BlockSpec((tg, KEY_W), lambda i: (tile(i), 0)),
                         pl.BlockSpec((tg, VAL_W), lambda i: (tile(i), 0)),
                         pl.BlockSpec((tg, LANE), lambda i: (tile(i), 0)),
                         _acc_spec((LANE, KEY_W)), _acc_spec((1, KEY_W)))
    out_shape = (jax.ShapeDtypeStruct((t, KEY_W), BF16), jax.ShapeDtypeStruct((t, KEY_W), BF16),
                 jax.ShapeDtypeStruct((t, VAL_W), BF16), jax.ShapeDtypeStruct((t, LANE), BF16),
                 jax.ShapeDtypeStruct((LANE, KEY_W), F32), jax.ShapeDtypeStruct((1, KEY_W), F32))
    scratch = [pltpu.VMEM((GLA_DV, KEY_W), F32), pltpu.VMEM((tg, KEY_W), F32)]
    return _fused_call(
        body, comms, name="gla_bwd", grid=(nt,),
        inputs=(p, p, p, p, st_f, d_o, p, p, p, p, st_b, d_o, wd_pad_f, bd_f, wd_pad_b, bd_b),
        in_specs=ins(down) + ins(up) + [wd_spec, bd_spec, wd_spec, bd_spec],
        out_specs=outs(down) + outs(up), out_shape=out_shape * 2, scratch_shapes=scratch * 2)


def _head_rms(o):
    parts, scales = [], []
    for h in range(GLA_HEADS):
        oh = o[:, h * GLA_DV:(h + 1) * GLA_DV]
        r = lax.rsqrt(jnp.mean(oh * oh, axis=-1, keepdims=True) + EPS)
        parts.append(oh * r)
        scales.append(jnp.broadcast_to(r, oh.shape))
    return jnp.concatenate(parts, axis=1), jnp.concatenate(scales, axis=1)


def _layernorm_stats(zv):
    mu = jnp.mean(zv, axis=-1, keepdims=True)
    xc = zv - mu
    rs = lax.rsqrt(jnp.mean(xc * xc, axis=-1, keepdims=True) + EPS)
    return xc * rs, rs


def _mix_fwd(x, o_f, o_b, p, gla_g, ln_g, ln_b, w_sp, b_sp, w_out, tm):
    t = x.shape[0]
    nch = tm // GMLP_CHUNK

    def body(x_ref, of_ref, ob_ref, pg_ref, pu_ref, pv_ref, gg_ref, lg_ref, lb_ref, ws_ref, bs_ref, wo_ref,
             x1_ref, y_ref, s_scr):
        on, _ = _head_rms(of_ref[...] + ob_ref[...])
        pg = pg_ref[...].astype(F32)
        y_a = on * gg_ref[...] * (pg * jax.nn.sigmoid(pg))
        zu = _gelu(pu_ref[...].astype(F32))
        vhat, _ = _layernorm_stats(_gelu(pv_ref[...].astype(F32)))
        vln = (vhat * lg_ref[...] + lb_ref[...]).astype(BF16)
        for g in range(GMLP_GROUPS):
            w_g = ws_ref[g].astype(BF16)
            b_g = bs_ref[g]
            cols = slice(g * LANE, (g + 1) * LANE)
            for n in range(nch):
                rows = slice(n * GMLP_CHUNK, (n + 1) * GMLP_CHUNK)
                s_scr[rows, cols] = jnp.dot(w_g, vln[rows, cols], preferred_element_type=F32) + b_g
        ycat = jnp.concatenate([y_a, zu * s_scr[...]], axis=1).astype(BF16)
        y_ref[...] = ycat
        x1_ref[...] = x_ref[...] + jnp.dot(ycat, wo_ref[...], preferred_element_type=F32)

    half = lambda j: pl.BlockSpec((tm, VAL_W), lambda i: (i, j))
    return pl.pallas_call(
        body, name="mix_fwd", grid=(t // tm,),
        in_specs=[pl.BlockSpec((tm, D_MODEL), lambda i: (i, 0)), half(0), half(0), half(2), half(3), half(4),
                  _const_spec((1, VAL_W)), _const_spec((1, GMLP_W)), _const_spec((1, GMLP_W)),
                  _const_spec((GMLP_GROUPS, GMLP_CHUNK, GMLP_CHUNK)), _const_spec((GMLP_GROUPS, GMLP_CHUNK, 1)),
                  _const_spec((D_MODEL, D_MODEL))],
        out_specs=(pl.BlockSpec((tm, D_MODEL), lambda i: (i, 0)), pl.BlockSpec((tm, D_MODEL), lambda i: (i, 0))),
        out_shape=(jax.ShapeDtypeStruct((t, D_MODEL), F32), jax.ShapeDtypeStruct((t, D_MODEL), BF16)),
        scratch_shapes=[pltpu.VMEM((tm, GMLP_W), F32)],
        compiler_params=_params(),
    )(x, o_f, o_b, p, p, p, gla_g, ln_g, ln_b, w_sp, b_sp, w_out)


def _mix_bwd(dx1, ycat, o_f, o_b, p, gla_g, ln_g, ln_b, w_sp, b_sp, w_out, tm, comms=()):
    t = dx1.shape[0]
    nch = tm // GMLP_CHUNK

    def body(dx1_ref, y_ref, of_ref, ob_ref, pg_ref, pu_ref, pv_ref, gg_ref, lg_ref, lb_ref, ws_ref, bs_ref, wo_ref,
             do_ref, dpg_ref, dpu_ref, dpv_ref, dwo_ref, dgg_ref, dlg_ref, dlb_ref, dws_ref, dbs_ref,
             s_scr, dvln_scr):
        @pl.when(pl.program_id(0) == 0)
        def _():
            for ref in (dwo_ref, dgg_ref, dlg_ref, dlb_ref, dws_ref, dbs_ref):
                ref[...] = jnp.zeros_like(ref)

        dx1 = dx1_ref[...].astype(BF16)
        dycat = _mm_nt(dx1, wo_ref[...])
        dwo_ref[...] += _mm_tn(y_ref[...], dx1)
        dy_a = dycat[:, :VAL_W]
        dy_b = dycat[:, VAL_W:]
        on, r = _head_rms(of_ref[...] + ob_ref[...])
        pg = pg_ref[...].astype(F32)
        sil, dsil = _silu_and_grad(pg)
        gg = gg_ref[...]
        dgg_ref[...] += jnp.sum(dy_a * sil * on, axis=0, keepdims=True)
        don = dy_a * sil * gg
        prod = don * on
        means = jnp.concatenate(
            [jnp.broadcast_to(jnp.mean(prod[:, h * GLA_DV:(h + 1) * GLA_DV], axis=-1, keepdims=True),
                              (tm, GLA_DV)) for h in range(GLA_HEADS)], axis=1)
        do_ref[...] = (r * (don - on * means)).astype(BF16)
        dpg_ref[...] = (dy_a * on * gg * dsil).astype(BF16)
        pu = pu_ref[...].astype(F32)
        pv = pv_ref[...].astype(F32)
        zu, dzu_dpu = _gelu_and_grad(pu)
        zv, dzv_dpv = _gelu_and_grad(pv)
        vhat, rs = _layernorm_stats(zv)
        lg = lg_ref[...]
        vln = (vhat * lg + lb_ref[...]).astype(BF16)
        ds32 = dy_b * zu
        ds = ds32.astype(BF16)
        for g in range(GMLP_GROUPS):
            w_g = ws_ref[g].astype(BF16)
            b_g = bs_ref[g]
            cols = slice(g * LANE, (g + 1) * LANE)
            dw_g = jnp.zeros((GMLP_CHUNK, GMLP_CHUNK), F32)
            db_g = jnp.zeros((GMLP_CHUNK, 1), F32)
            for n in range(nch):
                rows = slice(n * GMLP_CHUNK, (n + 1) * GMLP_CHUNK)
                v_blk = vln[rows, cols]
                ds_blk = ds[rows, cols]
                s_scr[rows, cols] = jnp.dot(w_g, v_blk, preferred_element_type=F32) + b_g
                dw_g = dw_g + _mm_nt(ds_blk, v_blk)
                db_g = db_g + jnp.sum(ds32[rows, cols], axis=-1, keepdims=True)
                dvln_scr[rows, cols] = _mm_tn(w_g, ds_blk)
            dws_ref[g] += dw_g
            dbs_ref[g] += db_g
        dpu_ref[...] = (dy_b * s_scr[...] * dzu_dpu).astype(BF16)
        dvln = dvln_scr[...]
        dlg_ref[...] += jnp.sum(dvln * vhat, axis=0, keepdims=True)
        dlb_ref[...] += jnp.sum(dvln, axis=0, keepdims=True)
        dvhat = dvln * lg
        dzv = rs * (dvhat - jnp.mean(dvhat, axis=-1, keepdims=True)
                    - vhat * jnp.mean(dvhat * vhat, axis=-1, keepdims=True))
        dpv_ref[...] = (dzv * dzv_dpv).astype(BF16)

    half = lambda j: pl.BlockSpec((tm, VAL_W), lambda i: (i, j))
    full = pl.BlockSpec((tm, D_MODEL), lambda i: (i, 0))
    sp_shape = (GMLP_GROUPS, GMLP_CHUNK, GMLP_CHUNK)
    bs_shape = (GMLP_GROUPS, GMLP_CHUNK, 1)
    return _fused_call(
        body, comms, name="mix_bwd", grid=(t // tm,),
        inputs=(dx1, ycat, o_f, o_b, p, p, p, gla_g, ln_g, ln_b, w_sp, b_sp, w_out),
        in_specs=[full, full, half(0), half(0), half(2), half(3), half(4),
                  _const_spec((1, VAL_W)), _const_spec((1, GMLP_W)), _const_spec((1, GMLP_W)),
                  _const_spec(sp_shape), _const_spec(bs_shape), _const_spec((D_MODEL, D_MODEL))],
        out_specs=(half(0), half(0), half(0), half(0), _acc_spec((D_MODEL, D_MODEL)), _acc_spec((1, VAL_W)),
                   _acc_spec((1, GMLP_W)), _acc_spec((1, GMLP_W)), _acc_spec(sp_shape), _acc_spec(bs_shape)),
        out_shape=(jax.ShapeDtypeStruct((t, VAL_W), BF16),) * 4 + (
            jax.ShapeDtypeStruct((D_MODEL, D_MODEL), F32), jax.ShapeDtypeStruct((1, VAL_W), F32),
            jax.ShapeDtypeStruct((1, GMLP_W), F32), jax.ShapeDtypeStruct((1, GMLP_W), F32),
            jax.ShapeDtypeStruct(sp_shape, F32), jax.ShapeDtypeStruct(bs_shape, F32)),
        scratch_shapes=[pltpu.VMEM((tm, GMLP_W), F32), pltpu.VMEM((tm, GMLP_W), F32)])


def _rms_bwd(dy_scaled, xn, r):
    return r * (dy_scaled - xn * jnp.mean(dy_scaled * xn, axis=-1, keepdims=True))


def _ffn(x1, target, g2, gf, w_gate, w_up, w_down, tm):
    t = x1.shape[0]

    def body(x1_ref, tg_ref, g2_ref, gf_ref, wg_ref, wu_ref, wd_ref,
             dx1_ref, h2_ref, dgate_ref, dup_ref, act_ref, dx2_ref, loss_ref, dgf_ref, dg2_ref):
        @pl.when(pl.program_id(0) == 0)
        def _():
            for ref in (loss_ref, dgf_ref, dg2_ref):
                ref[...] = jnp.zeros_like(ref)

        x1v = x1_ref[...]
        g2v = g2_ref[...]
        gfv = gf_ref[...]
        r2 = lax.rsqrt(jnp.mean(x1v * x1v, axis=-1, keepdims=True) + EPS)
        xn1 = x1v * r2
        h2 = (xn1 * g2v).astype(BF16)
        h2_ref[...] = h2
        gate = _mm_nt(h2, wg_ref[...])
        up = _mm_nt(h2, wu_ref[...])
        sil, dsil = _silu_and_grad(gate)
        act = (sil * up).astype(BF16)
        act_ref[...] = act
        x2 = x1v + jnp.dot(act, wd_ref[...], preferred_element_type=F32)
        rf = lax.rsqrt(jnp.mean(x2 * x2, axis=-1, keepdims=True) + EPS)
        xn2 = x2 * rf
        err = xn2 * gfv - tg_ref[...]
        loss_ref[...] += 0.5 * jnp.sum(jnp.mean(err * err, axis=-1, keepdims=True))
        dy = err * (1.0 / D_MODEL)
        dgf_ref[...] += jnp.sum(dy * xn2, axis=0, keepdims=True)
        dx2 = _rms_bwd(dy * gfv, xn2, rf)
        dx2b = dx2.astype(BF16)
        dx2_ref[...] = dx2b
        dact = _mm_nt(dx2b, wd_ref[...])
        dgate = (dact * up * dsil).astype(BF16)
        dup = (dact * sil).astype(BF16)
        dgate_ref[...] = dgate
        dup_ref[...] = dup
        dh2 = _mm(dgate, wg_ref[...]) + _mm(dup, wu_ref[...])
        dg2_ref[...] += jnp.sum(dh2 * xn1, axis=0, keepdims=True)
        dx1_ref[...] = dx2 + _rms_bwd(dh2 * g2v, xn1, r2)

    row = lambda w: pl.BlockSpec((tm, w), lambda i: (i, 0))
    return pl.pallas_call(
        body, name="ffn_fwd_bwd", grid=(t // tm,),
        in_specs=[row(D_MODEL), row(D_MODEL), _const_spec((1, D_MODEL)), _const_spec((1, D_MODEL)),
                  _const_spec((D_FF, D_MODEL)), _const_spec((D_FF, D_MODEL)), _const_spec((D_FF, D_MODEL))],
        out_specs=(row(D_MODEL), row(D_MODEL), row(D_FF), row(D_FF), row(D_FF), row(D_MODEL),
                   _acc_spec((8, LANE)), _acc_spec((1, D_MODEL)), _acc_spec((1, D_MODEL))),
        out_shape=(jax.ShapeDtypeStruct((t, D_MODEL), F32), jax.ShapeDtypeStruct((t, D_MODEL), BF16),
                   jax.ShapeDtypeStruct((t, D_FF), BF16), jax.ShapeDtypeStruct((t, D_FF), BF16),
                   jax.ShapeDtypeStruct((t, D_FF), BF16), jax.ShapeDtypeStruct((t, D_MODEL), BF16),
                   jax.ShapeDtypeStruct((8, LANE), F32), jax.ShapeDtypeStruct((1, D_MODEL), F32),
                   jax.ShapeDtypeStruct((1, D_MODEL), F32)),
        compiler_params=_params(),
    )(x1, target, g2, gf, w_gate, w_up, w_down)


def _matmul_tn(a, b, tm, tk, name, comms=()):
    t, m = a.shape
    n = b.shape[1]

    def body(a_ref, b_ref, o_ref):
        @pl.when(pl.program_id(1) == 0)
        def _():
            o_ref[...] = jnp.zeros_like(o_ref)

        o_ref[...] += _mm_tn(a_ref[...], b_ref[...])

    (out,), comm_results = _fused_call(
        body, comms, name=name, grid=(m // tm, t // tk), inputs=(a, b),
        in_specs=[pl.BlockSpec((tk, tm), lambda j, k: (k, j)), pl.BlockSpec((tk, n), lambda j, k: (k, 0))],
        out_specs=(pl.BlockSpec((tm, n), lambda j, k: (j, 0)),),
        out_shape=(jax.ShapeDtypeStruct((m, n), F32),))
    return out, comm_results


def _in_proj_bwd(x, g1, dx1, dq_f, dq_b, dk_f, dk_b, dv_f, dv_b, dpg, dpu, dpv, dlr_f, dlr_b, w_main, tm, comms=()):
    t = x.shape[0]

    def body(x_ref, g_ref, dx1_ref, dqf, dqb, dkf, dkb, dvf, dvb, dg, du, dv, dlf, dlb, w_ref,
             dx_ref, dp_ref, dg1_ref):
        @pl.when(pl.program_id(0) == 0)
        def _():
            dg1_ref[...] = jnp.zeros_like(dg1_ref)

        both = lambda a, b: (a[...].astype(F32) + b[...].astype(F32)).astype(BF16)
        dp = jnp.concatenate([both(dqf, dqb), both(dkf, dkb), both(dvf, dvb), dg[...], du[...], dv[...],
                              both(dlf, dlb)], axis=1)
        dp_ref[...] = dp
        dh = _mm(dp, w_ref[...])
        xv = x_ref[...]
        r = lax.rsqrt(jnp.mean(xv * xv, axis=-1, keepdims=True) + EPS)
        xn = xv * r
        dg1_ref[...] += jnp.sum(dh * xn, axis=0, keepdims=True)
        dx_ref[...] = dx1_ref[...] + _rms_bwd(dh * g_ref[...], xn, r)

    row = lambda w: pl.BlockSpec((tm, w), lambda i: (i, 0))
    return _fused_call(
        body, comms, name="in_proj_bwd", grid=(t // tm,),
        inputs=(x, g1, dx1, dq_f, dq_b, dk_f, dk_b, dv_f, dv_b, dpg, dpu, dpv, dlr_f, dlr_b, w_main),
        in_specs=[row(D_MODEL), _const_spec((1, D_MODEL)), row(D_MODEL), row(KEY_W), row(KEY_W), row(KEY_W),
                  row(KEY_W), row(VAL_W), row(VAL_W), row(VAL_W), row(VAL_W), row(VAL_W), row(LANE), row(LANE),
                  _const_spec((PROJ_PAD, D_MODEL))],
        out_specs=(row(D_MODEL), row(PROJ_PAD), _acc_spec((1, D_MODEL))),
        out_shape=(jax.ShapeDtypeStruct((t, D_MODEL), F32), jax.ShapeDtypeStruct((t, PROJ_PAD), BF16),
                   jax.ShapeDtypeStruct((1, D_MODEL), F32)))


def _adamw(w, g, m, v):
    m_new = ADAM_B1 * m + (1.0 - ADAM_B1) * g
    v_new = ADAM_B2 * v + (1.0 - ADAM_B2) * (g * g)
    m_hat = m_new / (1.0 - ADAM_B1 ** ADAM_STEP)
    v_hat = v_new / (1.0 - ADAM_B2 ** ADAM_STEP)
    delta = -ADAM_LR * (m_hat / (jnp.sqrt(v_hat) + ADAM_EPS) + ADAM_WD * w)
    return delta, m_new, v_new


def _adamw_shard(own, recv, w, m, v, tr, name):
    r, c = w.shape

    def body(own_ref, recv_ref, w_ref, m_ref, v_ref, g_ref, d_ref, nm_ref, nv_ref):
        g = own_ref[...]
        for k in range(3):
            g = g + recv_ref[k].astype(F32)
        g_ref[...] = g
        d_ref[...], nm_ref[...], nv_ref[...] = _adamw(w_ref[...], g, m_ref[...], v_ref[...])

    row = pl.BlockSpec((tr, c), lambda i: (i, 0))
    return pl.pallas_call(
        body, name=name, grid=(r // tr,),
        in_specs=[row, pl.BlockSpec((3, tr, c), lambda i: (0, i, 0)), row, row, row],
        out_specs=(row,) * 4, out_shape=(jax.ShapeDtypeStruct((r, c), F32),) * 4,
        compiler_params=_params(),
    )(own, recv, w, m, v)


def _adamw_small(g, w, m, v):
    def body(g_ref, w_ref, m_ref, v_ref, d_ref, nm_ref, nv_ref):
        d_ref[...], nm_ref[...], nv_ref[...] = _adamw(w_ref[...], g_ref[...], m_ref[...], v_ref[...])

    return pl.pallas_call(body, name="adamw_small", out_shape=(jax.ShapeDtypeStruct(g.shape, F32),) * 3,
                          compiler_params=pltpu.CompilerParams(vmem_limit_bytes=VMEM_LIMIT))(g, w, m, v)


def _mesh_pos():
    return lax.axis_index("x"), lax.axis_index("y"), lax.axis_index("c")


def _other_chips(x, y):
    return [(x, 1 - y), (1 - x, y), (1 - x, 1 - y)]


_VMEM_WHOLE = pl.BlockSpec(memory_space=pltpu.VMEM)
_HBM_WHOLE = pl.BlockSpec(memory_space=pl.ANY)


def _gather_comm(shards, cast, mid=(1, 2)):
    na = len(shards)
    staged = [a for a in range(na) if cast[a]]

    def phases(in_refs, out_refs, scr):
        stage = dict(zip(staged, scr[:len(staged)]))
        send_sems, recv_sems, local_sems = scr[len(staged):]
        x, y, c = _mesh_pos()
        me, sibling = (x, y, c), (x, y, 1 - c)
        chips = _other_chips(x, y)
        srcs = [stage[a] if cast[a] else in_refs[a] for a in range(na)]

        def rows(a, pos):
            px, py, pc = pos
            return out_refs[a].at[4 * px + 2 * py + pc]

        def copy(a, k, block, to, src=None):
            return pltpu.make_async_remote_copy(
                src_ref=rows(a, block) if src is None else src, dst_ref=rows(a, block),
                send_sem=send_sems.at[a, k], recv_sem=recv_sems.at[a, k], device_id=to, device_id_type=MESH_ID)

        mine = [pltpu.make_async_copy(srcs[a], rows(a, me), local_sems.at[a]) for a in range(na)]
        first = []
        for a in range(na):
            first.append(copy(a, 0, me, sibling, src=srcs[a]))
            first += [copy(a, 1 + j, me, (*chip, c), src=srcs[a]) for j, chip in enumerate(chips)]
        passed = [copy(a, 4 + j, (*chip, c), sibling) for j, chip in enumerate(chips) for a in range(na)]

        def start():
            for a in staged:
                stage[a][...] = in_refs[a][...].astype(BF16)
            for cp in mine + first:
                cp.start()

        def forward():
            i = 0
            for j, chip in enumerate(chips):
                for a in range(na):
                    copy(a, 1 + j, (*chip, c), me).wait_recv()
                    passed[i].start()
                    i += 1

        def finish():
            for a in range(na):
                copy(a, 0, sibling, me).wait_recv()
                for j, chip in enumerate(chips):
                    copy(a, 4 + j, (*chip, 1 - c), me).wait_recv()
            for cp in first + passed:
                cp.wait_send()
            for cp in mine:
                cp.wait()

        return start, forward, finish

    def before(step, nsteps, in_refs, out_refs, scr):
        start, forward, _ = phases(in_refs, out_refs, scr)
        pl.when(step == 0)(start)
        pl.when(step == nsteps * mid[0] // mid[1])(forward)

    def after(step, nsteps, in_refs, out_refs, scr):
        pl.when(step == nsteps - 1)(phases(in_refs, out_refs, scr)[2])

    return _Comm(
        inputs=list(shards), in_specs=[_VMEM_WHOLE] * na,
        out_shape=[jax.ShapeDtypeStruct((N_DEV,) + s.shape, BF16 if cast[a] else s.dtype)
                   for a, s in enumerate(shards)],
        out_specs=[_HBM_WHOLE] * na,
        scratch_shapes=[pltpu.VMEM(shards[a].shape, BF16) for a in staged] + [
            pltpu.SemaphoreType.DMA((na, 7)), pltpu.SemaphoreType.DMA((na, 7)), pltpu.SemaphoreType.DMA((na,))],
        before=before, after=after)


def _exchange_comm(arrays, out_shape, make_copies):
    na = len(arrays)

    def copies(in_refs, out_refs, scr):
        return make_copies(in_refs, out_refs, *scr)

    def before(step, nsteps, in_refs, out_refs, scr):
        @pl.when(step == 0)
        def _():
            for cp in copies(in_refs, out_refs, scr):
                cp.start()

    def after(step, nsteps, in_refs, out_refs, scr):
        @pl.when(step == nsteps - 1)
        def _():
            for cp in copies(in_refs, out_refs, scr):
                cp.wait()

    return _Comm(inputs=list(arrays), in_specs=[_HBM_WHOLE] * na, out_shape=list(out_shape),
                 out_specs=[_HBM_WHOLE] * na,
                 scratch_shapes=[pltpu.SemaphoreType.DMA((na, 3)), pltpu.SemaphoreType.DMA((na, 3))],
                 before=before, after=after)


def _sibling_exchange_comm(grads):
    def make_copies(in_refs, out_refs, send_sems, recv_sems):
        x, y, c = _mesh_pos()
        return [pltpu.make_async_remote_copy(
            src_ref=in_refs[a].at[:, pl.ds(1 - c, 1)], dst_ref=out_refs[a], send_sem=send_sems.at[a, 0],
            recv_sem=recv_sems.at[a, 0], device_id=(x, y, 1 - c), device_id_type=MESH_ID)
            for a in range(len(grads))]

    return _exchange_comm(grads, [jax.ShapeDtypeStruct((4, 1) + g.shape[2:], F32) for g in grads], make_copies)


def _chips_exchange_comm(partials):
    def make_copies(in_refs, out_refs, send_sems, recv_sems):
        x, y, c = _mesh_pos()
        return [pltpu.make_async_remote_copy(
            src_ref=in_refs[a].at[j], dst_ref=out_refs[a].at[j], send_sem=send_sems.at[a, j],
            recv_sem=recv_sems.at[a, j], device_id=(*chip, c), device_id_type=MESH_ID)
            for a in range(len(partials)) for j, chip in enumerate(_other_chips(x, y))]

    return _exchange_comm(partials, [jax.ShapeDtypeStruct(g.shape, BF16) for g in partials], make_copies)


def _comm_only(comms, name):
    return _fused_call(lambda: None, comms, name=name, grid=(1,), inputs=(), in_specs=[], out_specs=(),
                       out_shape=())[1]


def _chip_sum(my_pos, mine, from_sibling, tr, name):
    _, _, r, c = mine.shape

    def body(pos_ref, a_ref, b_ref, own_ref, out_ref):
        s = a_ref[0, 0] + b_ref[0, 0]

        @pl.when(pl.program_id(1) == 0)
        def _():
            own_ref[...] = s

        @pl.when(pl.program_id(1) > 0)
        def _():
            out_ref[0] = s.astype(BF16)

    grid_spec = pltpu.PrefetchScalarGridSpec(
        num_scalar_prefetch=1, grid=(r // tr, 4),
        in_specs=[pl.BlockSpec((1, 1, tr, c), lambda i, k, pos: (pos[0] ^ k, pos[1], i, 0)),
                  pl.BlockSpec((1, 1, tr, c), lambda i, k, pos: (pos[0] ^ k, 0, i, 0))],
        out_specs=(pl.BlockSpec((tr, c), lambda i, k, pos: (i, 0)),
                   pl.BlockSpec((1, tr, c), lambda i, k, pos: (jnp.maximum(k - 1, 0), i, 0))))
    return pl.pallas_call(
        body, name=name, grid_spec=grid_spec,
        out_shape=(jax.ShapeDtypeStruct((r, c), F32), jax.ShapeDtypeStruct((3, r, c), BF16)),
        compiler_params=_params(2),
    )(my_pos, mine, from_sibling)


def _all_reduce_small_comm(part):
    def copies(in_ref, gathered, send_sems, recv_sems):
        x, y, c = _mesh_pos()
        my_id = 4 * x + 2 * y + c
        return my_id, [pltpu.make_async_remote_copy(
            src_ref=in_ref, dst_ref=gathered.at[my_id], send_sem=send_sems.at[k - 1],
            recv_sem=recv_sems.at[k - 1], device_id=(x ^ (k >> 2), y ^ ((k >> 1) & 1), c ^ (k & 1)),
            device_id_type=MESH_ID) for k in range(1, N_DEV)]

    def before(step, nsteps, in_refs, out_refs, scr):
        @pl.when(step == 0)
        def _():
            for cp in copies(in_refs[0], *scr)[1]:
                cp.start()

    def after(step, nsteps, in_refs, out_refs, scr):
        @pl.when(step == nsteps - 1)
        def _():
            gathered = scr[0]
            my_id, cps = copies(in_refs[0], *scr)
            gathered[my_id] = in_refs[0][...]
            for cp in cps:
                cp.wait()
            acc = gathered[0]
            for d in range(1, N_DEV):
                acc = acc + gathered[d]
            out_refs[0][...] = acc

    return _Comm(inputs=[part], in_specs=[_VMEM_WHOLE], out_shape=[jax.ShapeDtypeStruct(part.shape, F32)],
                 out_specs=[_VMEM_WHOLE],
                 scratch_shapes=[pltpu.VMEM((N_DEV,) + part.shape, F32), pltpu.SemaphoreType.DMA((N_DEV - 1,)),
                                 pltpu.SemaphoreType.DMA((N_DEV - 1,))],
                 before=before, after=after)


def _unshard_cols(g):
    return jnp.transpose(g, (1, 0, 2)).reshape(g.shape[1], N_DEV * g.shape[2])


def _row_blocks(w):
    return w.reshape(4, 2, w.shape[0] // N_DEV, w.shape[1])


def _pack_small(parts):
    flat = jnp.concatenate([a.reshape(-1) for a in parts])
    pad = (-flat.shape[0]) % (8 * LANE)
    return jnp.pad(flat, (0, pad)).reshape(-1, LANE)


def _unpack_small(packed, shapes):
    flat = packed.reshape(-1)
    out, off = [], 0
    for s in shapes:
        n = math.prod(s)
        out.append(flat[off:off + n].reshape(s))
        off += n
    return out


PROJ_W = 2592
LR_REF = 1536


def _main_proj_weight(w_in_t):
    return jnp.concatenate([w_in_t[:LR_REF], w_in_t[LR_REF + 2 * LOWRANK:], w_in_t[LR_REF:LR_REF + 2 * LOWRANK],
                            jnp.zeros((PROJ_PAD - PROJ_W, D_MODEL), w_in_t.dtype)], axis=0)


def _main_proj_grad(dw_main_t):
    return jnp.concatenate([dw_main_t[:LR_REF], dw_main_t[LR_COL:LR_COL + 2 * LOWRANK], dw_main_t[LR_REF:LR_COL]],
                           axis=0)


def _padded_decay_weights(wd_f, wd_b):
    zeros = lambda n: jnp.zeros((n, KEY_W), F32)
    return (jnp.concatenate([wd_f, zeros(LANE - LOWRANK)], axis=0),
            jnp.concatenate([zeros(LOWRANK), wd_b, zeros(LANE - 2 * LOWRANK)], axis=0))


def kernel(x, norm1_g, w_in,w_decay_f, b_decay_f, w_decay_b, b_decay_b, gla_norm_g, gmlp_ln_g, gmlp_ln_b, w_spatial, b_spatial, w_out, norm2_g, w_gate, w_up, w_down, final_norm_g, loss_target, m_norm1_g, m_w_in, m_w_decay_f, m_b_decay_f, m_w_decay_b, m_b_decay_b, m_gla_norm_g, m_gmlp_ln_g, m_gmlp_ln_b, m_w_spatial, m_b_spatial, m_w_out, m_norm2_g, m_w_gate, m_w_up, m_w_down, m_final_norm_g, v_norm1_g, v_w_in, v_w_decay_f, v_b_decay_f, v_w_decay_b, v_b_decay_b, v_gla_norm_g, v_gmlp_ln_g, v_gmlp_ln_b, v_w_spatial, v_b_spatial, v_w_out, v_norm2_g, v_w_gate, v_w_up, v_w_down, v_final_norm_g):
    t = x.shape[1]
    xt = x[0]
    target = loss_target[0]
    pos_x, pos_y, pos_c = _mesh_pos()
    my_pos = jnp.stack([2 * pos_x + pos_y, pos_c]).astype(jnp.int32)
    my_id = 4 * pos_x + 2 * pos_y + pos_c

    tile = lambda n: min(n, t)
    ln_g, ln_b, w_sp = gmlp_ln_g, gmlp_ln_b, w_spatial[0]
    b_sp_col = b_spatial[0][:, :, None]
    shard = {"w_in": w_in[0].T, "w_out": w_out[0], "w_gate": w_gate[0].T, "w_up": w_up[0].T, "w_down": w_down[0]}
    shard_m = {"w_in": m_w_in[0].T, "w_out": m_w_out[0], "w_gate": m_w_gate[0].T, "w_up": m_w_up[0].T,
               "w_down": m_w_down[0]}
    shard_v = {"w_in": v_w_in[0].T, "w_out": v_w_out[0], "w_gate": v_w_gate[0].T, "w_up": v_w_up[0].T,
               "w_down": v_w_down[0]}
    transposed = ("w_in", "w_gate", "w_up")
    chip_sum = lambda n, g, s: _chip_sum(my_pos, g, s[0], g.shape[2], "chip_sum_" + n)

    decay_shard = jnp.stack([w_decay_f[0], w_decay_b[0]])
    ((g_in, g_decay),) = _comm_only([_gather_comm([shard["w_in"], decay_shard], [True, False])], "all_gather_w_in")
    w_main = _main_proj_weight(g_in.reshape(PROJ_W, D_MODEL))
    wd_pad_f, wd_pad_b = _padded_decay_weights(_unshard_cols(g_decay[:, 0]), _unshard_cols(g_decay[:, 1]))
    (p, hb), ((g_gate,),) = _in_proj(xt, norm1_g, w_main, tile(1024), [_gather_comm([shard["w_gate"]], [True])])
    (o_f, st_f, o_b, st_b), ((g_up, g_out, g_down),) = _gla_fwd(
        p, wd_pad_f, b_decay_f, wd_pad_b, b_decay_b, tile(512),
        [_gather_comm([shard["w_up"], shard["w_out"], shard["w_down"]], [True, True, True])])
    w_out_full = g_out.reshape(D_MODEL, D_MODEL)
    x1, ycat = _mix_fwd(xt, o_f, o_b, p, gla_norm_g, ln_g, ln_b, w_sp, b_sp_col, w_out_full, tile(512))

    dx1, h2b, dgate, dup, act, dx2, loss_acc, d_gf, d_g2 = _ffn(
        x1, target, norm2_g, final_norm_g[None, :], g_gate.reshape(D_FF, D_MODEL), g_up.reshape(D_FF, D_MODEL),
        g_down.reshape(D_FF, D_MODEL), tile(256))
    dw_gate, _ = _matmul_tn(dgate, h2b, D_FF // 2, tile(2048), "grad_w_gate")
    dw_up, _ = _matmul_tn(dup, h2b, D_FF // 2, tile(2048), "grad_w_up")
    dw_down, _ = _matmul_tn(act, dx2, D_FF // 2, tile(2048), "grad_w_down")

    ffn_grads = [_row_blocks(dw_gate), _row_blocks(dw_up), _row_blocks(dw_down)]
    (d_o, dpg, dpu, dpv, dw_out, d_gg, d_lg, d_lb, dw_sp, db_sp), (ffn_sib,) = _mix_bwd(
        dx1, ycat, o_f, o_b, p, gla_norm_g, ln_g, ln_b, w_sp, b_sp_col, w_out_full, tile(256),
        [_sibling_exchange_comm(ffn_grads)])
    ffn_names = ["w_gate", "w_up", "w_down"]
    ffn_sums = [chip_sum(n, g, [s]) for n, g, s in zip(ffn_names, ffn_grads, ffn_sib)]
    out_grad = _row_blocks(dw_out)
    (dq_f, dk_f, dv_f, dlr_f, dwd_f, dbd_f, dq_b, dk_b, dv_b, dlr_b, dwd_b, dbd_b), (ffn_recv, out_sib) = _gla_bwd(
        p, wd_pad_f, b_decay_f, wd_pad_b, b_decay_b, st_f, st_b, d_o, tile(512),
        [_chips_exchange_comm([s[1] for s in ffn_sums]), _sibling_exchange_comm([out_grad])])
    out_sum = chip_sum("w_out", out_grad, out_sib)
    (grad_x, dp, d_g1), _ = _in_proj_bwd(
        xt, norm1_g, dx1, dq_f, dq_b, dk_f, dk_b, dv_f, dv_b, dpg, dpu, dpv, dlr_f, dlr_b, w_main, tile(512))

    small_shapes = [(1, D_MODEL), (LOWRANK, KEY_W), (1, KEY_W), (LOWRANK, KEY_W), (1, KEY_W), (1, VAL_W),
                    (1, GMLP_W), (1, GMLP_W), (1, GMLP_GROUPS, GMLP_CHUNK, GMLP_CHUNK), (1, GMLP_GROUPS, GMLP_CHUNK),
                    (1, D_MODEL), (D_MODEL,), (LANE,)]
    small_parts = [d_g1, dwd_f[:LOWRANK], dbd_f, dwd_b[LOWRANK:2 * LOWRANK], dbd_b, d_gg, d_lg, d_lb, dw_sp,
                   db_sp, d_g2, d_gf, loss_acc[0]]
    dw_main, ((small_sum,), out_recv) = _matmul_tn(
        dp, hb, PROJ_PAD // 3, tile(2048), "grad_w_in",
        [_all_reduce_small_comm(_pack_small(small_parts)), _chips_exchange_comm([out_sum[1]])])
    in_grad = _row_blocks(_main_proj_grad(dw_main))
    (in_sib,) = _comm_only([_sibling_exchange_comm([in_grad])], "grad_w_in_exchange_sibling")
    in_sum = chip_sum("w_in", in_grad, in_sib)
    (in_recv,) = _comm_only([_chips_exchange_comm([in_sum[1]])], "grad_w_in_exchange_chips")

    names = ["w_in", "w_out", "w_gate", "w_up", "w_down"]
    sums = [in_sum, out_sum] + ffn_sums
    received = [in_recv[0], out_recv[0]] + list(ffn_recv)
    big_out = {}
    for n, s, rc in zip(names, sums, received):
        res = _adamw_shard(s[0], rc, shard[n], shard_m[n], shard_v[n], shard[n].shape[0], "adamw_" + n)
        big_out[n] = [r.T if n in transposed else r for r in res]

    reduced = _unpack_small(small_sum, small_shapes)
    loss = reduced[-1][0]
    col0 = my_id * (KEY_W // N_DEV)
    g_small = list(reduced[:-1])
    g_small[1] = lax.dynamic_slice_in_dim(g_small[1], col0, KEY_W // N_DEV, axis=1)[None]
    g_small[3] = lax.dynamic_slice_in_dim(g_small[3], col0, KEY_W // N_DEV, axis=1)[None]
    small_w = [norm1_g, w_decay_f, b_decay_f, w_decay_b, b_decay_b, gla_norm_g, gmlp_ln_g, gmlp_ln_b, w_spatial,
               b_spatial, norm2_g, final_norm_g]
    small_m = [m_norm1_g, m_w_decay_f, m_b_decay_f, m_w_decay_b, m_b_decay_b, m_gla_norm_g, m_gmlp_ln_g,
               m_gmlp_ln_b, m_w_spatial, m_b_spatial, m_norm2_g, m_final_norm_g]
    small_v = [v_norm1_g, v_w_decay_f, v_b_decay_f, v_w_decay_b, v_b_decay_b, v_gla_norm_g, v_gmlp_ln_g,
               v_gmlp_ln_b, v_w_spatial, v_b_spatial, v_norm2_g, v_final_norm_g]
    shapes = [w.shape for w in small_w]
    packed = _adamw_small(_pack_small(g_small), _pack_small(small_w), _pack_small(small_m), _pack_small(small_v))
    s_delta, s_m, s_v = (_unpack_small(a, shapes) for a in packed)
    s_grad = [g.reshape(s) for g, s in zip(g_small, shapes)]

    order = ["norm1_g", "w_in", "w_decay_f", "b_decay_f", "w_decay_b", "b_decay_b", "gla_norm_g", "gmlp_ln_g",
             "gmlp_ln_b", "w_spatial", "b_spatial", "w_out", "norm2_g", "w_gate", "w_up", "w_down", "final_norm_g"]
    small_names = [n for n in order if n not in big_out]
    small_out = {n: (s_grad[i], s_delta[i], s_m[i], s_v[i]) for i, n in enumerate(small_names)}
    outs = []
    for kind in range(4):
        for n in order:
            outs.append(big_out[n][kind][None] if n in big_out else small_out[n][kind])
    return (loss, grad_x[None], *outs)
```

```python
import functools
import math

import jax
import jax.numpy as jnp
from jax import lax
from jax.experimental import pallas as pl
from jax.experimental.pallas import tpu as pltpu

F32 = jnp.float32
BF16 = jnp.bfloat16

D_MODEL = 1024
GLA_HEADS = 4
GLA_DK = 64
GLA_DV = 128
KEY_W = GLA_HEADS * GLA_DK
VAL_W = GLA_HEADS * GLA_DV
LOWRANK = 16
GLA_TAU = 16.0
GLA_CHUNK = 64
GMLP_W = 512
GMLP_GROUPS = 4
GMLP_CHUNK = 128
D_FF = 2816
EPS = 1e-6
Q_SCALE = GLA_DK ** -0.5
PROJ_PAD = 2688
LR_COL = 2560
LANE = 128
N_DEV = 8

ADAM_LR = 0.001
ADAM_B1 = 0.9
ADAM_B2 = 0.999
ADAM_EPS = 1e-08
ADAM_WD = 0.01
ADAM_STEP = 10

VMEM_LIMIT = 56 * 1024 * 1024
MESH_ID = pl.DeviceIdType.MESH
INV_SQRT2 = 0.7071067811865476
INV_SQRT_2PI = 0.3989422804014327


def _params(n_axes=1):
    return pltpu.CompilerParams(dimension_semantics=("arbitrary",) * n_axes, vmem_limit_bytes=VMEM_LIMIT)


def _mm(a, b):
    return jnp.dot(a.astype(BF16), b.astype(BF16), preferred_element_type=F32)


def _mm_nt(a, b):
    return lax.dot_general(a.astype(BF16), b.astype(BF16), (((1,), (1,)), ((), ())), preferred_element_type=F32)


def _mm_tn(a, b):
    return lax.dot_general(a.astype(BF16), b.astype(BF16), (((0,), (0,)), ((), ())), preferred_element_type=F32)


def _const_spec(shape):
    nd = len(shape)
    return pl.BlockSpec(shape, lambda *_: (0,) * nd, pipeline_mode=pl.Buffered(1))


def _acc_spec(shape):
    nd = len(shape)
    return pl.BlockSpec(shape, lambda *_: (0,) * nd)


class _Comm:
    def __init__(self, inputs, in_specs, out_shape, out_specs, scratch_shapes, before, after):
        self.inputs, self.in_specs, self.out_shape, self.out_specs = inputs, in_specs, out_shape, out_specs
        self.scratch_shapes, self.before, self.after = scratch_shapes, before, after


def _fused_call(body, comms, *, name, grid, inputs, in_specs, out_specs, out_shape, scratch_shapes=()):
    n_in, n_out, n_scr = len(in_specs), len(out_specs), len(scratch_shapes)
    nsteps = math.prod(grid)
    sizes = [(len(c.inputs), len(c.out_shape), len(c.scratch_shapes)) for c in comms]

    def full_body(*refs):
        step = pl.program_id(0)
        for axis in range(1, len(grid)):
            step = step * grid[axis] + pl.program_id(axis)
        ins, rest = refs[:n_in], refs[n_in:]
        c_ins = []
        for ci, _, _ in sizes:
            c_ins.append(rest[:ci])
            rest = rest[ci:]
        outs, rest = rest[:n_out], rest[n_out:]
        c_outs = []
        for _, co, _ in sizes:
            c_outs.append(rest[:co])
            rest = rest[co:]
        scr, rest = rest[:n_scr], rest[n_scr:]
        c_scr = []
        for _, _, cs in sizes:
            c_scr.append(rest[:cs])
            rest = rest[cs:]
        for c, a, b, s in zip(comms, c_ins, c_outs, c_scr):
            c.before(step, nsteps, a, b, s)
        body(*ins, *outs, *scr)
        for c, a, b, s in zip(comms, c_ins, c_outs, c_scr):
            c.after(step, nsteps, a, b, s)

    results = pl.pallas_call(
        full_body, name=name, grid=grid,
        in_specs=list(in_specs) + [s for c in comms for s in c.in_specs],
        out_specs=tuple(out_specs) + tuple(s for c in comms for s in c.out_specs),
        out_shape=tuple(out_shape) + tuple(s for c in comms for s in c.out_shape),
        scratch_shapes=list(scratch_shapes) + [s for c in comms for s in c.scratch_shapes],
        compiler_params=_params(len(grid)),
    )(*inputs, *[a for c in comms for a in c.inputs])
    own, rest = results[:n_out], results[n_out:]
    comm_results = []
    for _, co, _ in sizes:
        comm_results.append(rest[:co])
        rest = rest[co:]
    return own, comm_results


def _gelu(x):
    return 0.5 * x * (1.0 + lax.erf(x * INV_SQRT2))


def _gelu_and_grad(x):
    cdf = 0.5 * (1.0 + lax.erf(x * INV_SQRT2))
    return x * cdf, cdf + x * jnp.exp(-0.5 * x * x) * INV_SQRT_2PI


def _silu_and_grad(x):
    s = jax.nn.sigmoid(x)
    return x * s, s * (1.0 + x * (1.0 - s))


def _norm1(x, g1, tm, comms=()):
    t = x.shape[0]

    def body(x_ref, g_ref, h_ref):
        xv = x_ref[...]
        r = lax.rsqrt(jnp.mean(xv * xv, axis=-1, keepdims=True) + EPS)
        h_ref[...] = (xv * r * g_ref[...]).astype(BF16)

    row = pl.BlockSpec((tm, D_MODEL), lambda i: (i, 0))
    return _fused_call(body, comms, name="norm1", grid=(t // tm,), inputs=(x, g1),
                       in_specs=[row, _const_spec((1, D_MODEL))], out_specs=(row,),
                       out_shape=(jax.ShapeDtypeStruct((t, D_MODEL), BF16),))


PROJ_W = 2592
LR_REF = 1536
PROJ_ROWS = ((0, LR_REF, 0), (LR_REF + 2 * LOWRANK, PROJ_W, LR_REF), (LR_REF, LR_REF + LANE, LR_COL))


def _in_proj(h, w_in_t, tm, comms=()):
    t = h.shape[0]

    def body(h_ref, w_ref, p_ref):
        hv = h_ref[...]
        for r0, r1, c0 in PROJ_ROWS:
            p_ref[:, c0:c0 + r1 - r0] = _mm_nt(hv, w_ref[r0:r1, :]).astype(BF16)

    return _fused_call(
        body, comms, name="in_proj", grid=(t // tm,), inputs=(h, w_in_t),
        in_specs=[pl.BlockSpec((tm, D_MODEL), lambda i: (i, 0)), _const_spec((PROJ_W, D_MODEL))],
        out_specs=(pl.BlockSpec((tm, PROJ_PAD), lambda i: (i, 0)),),
        out_shape=(jax.ShapeDtypeStruct((t, PROJ_PAD), BF16),))


def _tri(n, upper):
    r = lax.broadcasted_iota(jnp.int32, (n, n), 0)
    c = lax.broadcasted_iota(jnp.int32, (n, n), 1)
    bits = GLA_CHUNK.bit_length() - 1
    return jnp.where(((c >= r) if upper else (c <= r)) & ((r >> bits) == (c >> bits)), 1.0, 0.0).astype(BF16)


def _tri_matmul(tri, a):
    a1 = a.astype(BF16)
    r1 = a - a1.astype(F32)
    a2 = r1.astype(BF16)
    a3 = (r1 - a2.astype(F32)).astype(BF16)
    dot = functools.partial(jnp.dot, preferred_element_type=F32)
    return dot(tri, a1) + dot(tri, a2) + dot(tri, a3)


def _gla_masks(rev):
    dk_bits, dv_bits = GLA_DK.bit_length() - 1, GLA_DV.bit_length() - 1
    key_head = lax.broadcasted_iota(jnp.int32, (GLA_CHUNK, KEY_W), 1) >> dk_bits
    val_head = lax.broadcasted_iota(jnp.int32, (GLA_CHUNK, VAL_W), 1) >> dv_bits
    t = lax.broadcasted_iota(jnp.int32, (GLA_HEADS * GLA_CHUNK, GLA_CHUNK), 0) & (GLA_CHUNK - 1)
    s = lax.broadcasted_iota(jnp.int32, (GLA_HEADS * GLA_CHUNK, GLA_CHUNK), 1)
    causal = (s >= t) if rev else (s <= t)
    state_head = lax.broadcasted_iota(jnp.int32, (GLA_DV, KEY_W), 1) >> dk_bits
    return key_head, val_head, causal, state_head


def _stack_heads(a, head_of_lane):
    return jnp.concatenate([jnp.where(head_of_lane == h, a, 0.0) for h in range(GLA_HEADS)], axis=0)


def _rows_by_head(a):
    return jnp.concatenate([a[:, h * GLA_DV:(h + 1) * GLA_DV] for h in range(GLA_HEADS)], axis=0)


def _lanes_by_head(r):
    return jnp.concatenate([r[h * GLA_CHUNK:(h + 1) * GLA_CHUNK] for h in range(GLA_HEADS)], axis=1)


def _head_diagonal(r, head_of_lane):
    rows = r.shape[0] // GLA_HEADS
    out = jnp.where(head_of_lane == 0, r[:rows], 0.0)
    for h in range(1, GLA_HEADS):
        out = out + jnp.where(head_of_lane == h, r[h * rows:(h + 1) * rows], 0.0)
    return out


def _chunk_rows(c):
    return slice(c * GLA_CHUNK, (c + 1) * GLA_CHUNK)


def _tile_terms(la, q, k, tri, rev):
    nc = la.shape[0] // GLA_CHUNK
    q, k = q.astype(F32), k.astype(F32)
    b = _tri_matmul(tri, la)
    last = [b[c * GLA_CHUNK:c * GLA_CHUNK + 1] if rev else b[(c + 1) * GLA_CHUNK - 1:(c + 1) * GLA_CHUNK]
            for c in range(nc)]
    bl = jnp.concatenate([jnp.broadcast_to(row, (GLA_CHUNK, KEY_W)) for row in last], axis=0)
    eb = jnp.exp(b)
    enb = jnp.exp(-b)
    ee = jnp.exp(bl - b)
    return [jnp.exp(row) for row in last], eb, enb, ee, q * Q_SCALE * eb, k * enb, k * ee


def _log_decay(lr_ref, wd_ref, bd_ref):
    z = _mm(lr_ref[...], wd_ref[...]) + bd_ref[...]
    return z, jax.nn.log_sigmoid(z) * (1.0 / GLA_TAU)


def _p_specs(tg, tile):
    return [pl.BlockSpec((tg, KEY_W), lambda i: (tile(i), 0)),
            pl.BlockSpec((tg, KEY_W), lambda i: (tile(i), 1)),
            pl.BlockSpec((tg, VAL_W), lambda i: (tile(i), 1)),
            pl.BlockSpec((tg, LANE), lambda i: (tile(i), LR_COL // LANE))]


def _gla_fwd_dir(rev, nc, q_ref, k_ref, v_ref, lr_ref, wd_ref, bd_ref, o_ref, st_ref, state):
    key_head, _, causal, state_head = _gla_masks(rev)
    order = range(nc - 1, -1, -1) if rev else range(nc)

    def intra():
        _, la = _log_decay(lr_ref, wd_ref, bd_ref)
        ebl, _, _, _, qd, kd, ke = _tile_terms(la, q_ref[...], k_ref[...], _tri(nc * GLA_CHUNK, rev), rev)
        terms = {}
        for c in order:
            rows = _chunk_rows(c)
            v_c = v_ref[rows, :].astype(BF16)
            qd_stack = _stack_heads(qd[rows], key_head).astype(BF16)
            a_all = jnp.where(causal, _mm_nt(qd_stack, kd[rows]), 0.0)
            r = _mm(a_all, v_c)
            terms[c] = (ebl[c], qd_stack, r, _head_diagonal(_mm_tn(v_c, ke[rows]), state_head))
        return terms

    def scan(terms):
        st = state[...]
        states = {}
        for c in order:
            states[c] = st
            st_ref[c] = st.astype(BF16)
            st = st * terms[c][0] + terms[c][3]
        state[...] = st
        return states

    def inter(terms, states):
        for c in order:
            _, qd_stack, r, _ = terms[c]
            r_inter = _mm_nt(qd_stack, states[c])
            o_ref[_chunk_rows(c), :] = jnp.concatenate(
                [r[h * GLA_CHUNK:(h + 1) * GLA_CHUNK, h * GLA_DV:(h + 1) * GLA_DV]
                 + r_inter[h * GLA_CHUNK:(h + 1) * GLA_CHUNK] for h in range(GLA_HEADS)], axis=1)

    return intra, scan, inter


def _gla_fwd(p, wd_pad_f, bd_f, wd_pad_b, bd_b, tg, comms=()):
    t = p.shape[0]
    nt = t // tg
    nc = tg // GLA_CHUNK
    up, down = (lambda i: i), (lambda i: nt - 1 - i)

    def body(qf, kf, vf, lrf, qb, kb, vb, lrb, wdf, bdf, wdb, bdb, of, stf, ob, stb, state_f, state_b):
        @pl.when(pl.program_id(0) == 0)
        def _():
            state_f[...] = jnp.zeros_like(state_f)
            state_b[...] = jnp.zeros_like(state_b)

        dirs = [_gla_fwd_dir(False, nc, qf, kf, vf, lrf, wdf, bdf, of, stf, state_f),
                _gla_fwd_dir(True, nc, qb, kb, vb, lrb, wdb, bdb, ob, stb, state_b)]
        terms = [intra() for intra, _, _ in dirs]
        states = [scan(t) for (_, scan, _), t in zip(dirs, terms)]
        for (_, _, inter), t, s in zip(dirs, terms, states):
            inter(t, s)

    wd_spec, bd_spec = _const_spec((LANE, KEY_W)), _const_spec((1, KEY_W))
    outs = lambda tile: (pl.BlockSpec((tg, VAL_W), lambda i: (tile(i), 0)),
                         pl.BlockSpec((nc, GLA_DV, KEY_W), lambda i: (tile(i), 0, 0)))
    out_shape = (jax.ShapeDtypeStruct((t, VAL_W), F32), jax.ShapeDtypeStruct((t // GLA_CHUNK, GLA_DV, KEY_W), BF16))
    return _fused_call(
        body, comms, name="gla_fwd", grid=(nt,), inputs=(p,) * 8 + (wd_pad_f, bd_f, wd_pad_b, bd_b),
        in_specs=_p_specs(tg, up) + _p_specs(tg, down) + [wd_spec, bd_spec, wd_spec, bd_spec],
        out_specs=outs(up) + outs(down), out_shape=out_shape * 2,
        scratch_shapes=[pltpu.VMEM((GLA_DV, KEY_W), F32)] * 2)


def _gla_bwd_dir(rev, nc, q_ref, k_ref, v_ref, lr_ref, wd_ref, bd_ref, st_ref, do_ref,
                 dq_ref, dk_ref, dv_ref, dlr_ref, dwd_ref, dbd_ref, dstate):
    key_head, val_head, causal, state_head = _gla_masks(rev)
    order = range(nc) if rev else range(nc - 1, -1, -1)
    tg = nc * GLA_CHUNK

    def intra():
        z, la = _log_decay(lr_ref, wd_ref, bd_ref)
        tile = _tile_terms(la, q_ref[...], k_ref[...], _tri(tg, rev), rev)
        qd, kd = tile[4], tile[5]
        per = {}
        for c in order:
            rows = _chunk_rows(c)
            v_c = v_ref[rows, :].astype(BF16)
            do_c = do_ref[rows, :]
            kd_c = kd[rows].astype(BF16)
            qd_stack = _stack_heads(qd[rows], key_head).astype(BF16)
            do_stack = _stack_heads(do_c, val_head)
            a_all = jnp.where(causal, _mm_nt(qd_stack, kd_c), 0.0)
            da_all = jnp.where(causal, _mm_nt(do_stack, v_c), 0.0)
            per[c] = dict(
                dv=_mm_tn(a_all, do_stack),
                dqd=_head_diagonal(_mm(da_all, kd_c) + _mm(_rows_by_head(do_c), st_ref[c]), key_head),
                dkd=_mm_tn(da_all, qd_stack),
                upd=_head_diagonal(_mm_tn(do_c, qd[rows]), state_head))
        return z, tile, per

    def scan(tile, per):
        dst = dstate[...]
        dsts = {}
        for c in order:
            dsts[c] = dst
            dst = dst * tile[0][c] + per[c]["upd"]
        dstate[...] = dst
        return dsts

    def inter(z, tile, per, dsts):
        ebl, eb, enb, ee, qd, kd, ke = tile
        dke_c, dbl_c = {}, {}
        for c in order:
            rows = _chunk_rows(c)
            ke_stack = _stack_heads(ke[rows], key_head).astype(BF16)
            dv_ref[rows, :] = (per[c]["dv"] + _lanes_by_head(_mm_nt(ke_stack, dsts[c]))).astype(BF16)
            dke_c[c] = _head_diagonal(_mm(_rows_by_head(v_ref[rows, :].astype(BF16)), dsts[c]), key_head)
            dbl = (jnp.sum(dsts[c] * st_ref[c].astype(F32), axis=0, keepdims=True) * ebl[c]
                   + jnp.sum(dke_c[c] * ke[rows], axis=0, keepdims=True))
            dbl_c[c] = jnp.broadcast_to(dbl, (GLA_CHUNK, KEY_W))
        tile_of = lambda parts: jnp.concatenate([parts[c] for c in range(nc)], axis=0)
        dqd, dkd = tile_of({c: per[c]["dqd"] for c in order}), tile_of({c: per[c]["dkd"] for c in order})
        dke, dbl = tile_of(dke_c), tile_of(dbl_c)
        dq_ref[...] = (dqd * eb * Q_SCALE).astype(BF16)
        dk_ref[...] = (dkd * enb + dke * ee).astype(BF16)
        db = dqd * qd - dkd * kd - dke * ke
        dz = (_tri_matmul(_tri(tg, not rev), db) + dbl) * (jax.nn.sigmoid(-z) * (1.0 / GLA_TAU))
        dlr_ref[...] = _mm_nt(dz, wd_ref[...]).astype(BF16)
        dwd_ref[...] += _mm_tn(lr_ref[...], dz)
        dbd_ref[...] += jnp.sum(dz, axis=0, keepdims=True)

    return intra, scan, inter


def _gla_bwd(p, wd_pad_f, bd_f, wd_pad_b, bd_b, st_f, st_b, d_o, tg, comms=()):
    t = p.shape[0]
    nt = t // tg
    nc = tg // GLA_CHUNK
    up, down = (lambda i: i), (lambda i: nt - 1 - i)

    def body(qf, kf, vf, lrf, stf, dof, qb, kb, vb, lrb, stb, dob, wdf, bdf, wdb, bdb,
             dqf, dkf, dvf, dlrf, dwdf, dbdf, dqb, dkb, dvb, dlrb, dwdb, dbdb, dstate_f, dstate_b):
        @pl.when(pl.program_id(0) == 0)
        def _():
            for ref in (dstate_f, dstate_b, dwdf, dbdf, dwdb, dbdb):
                ref[...] = jnp.zeros_like(ref)

        dirs = [_gla_bwd_dir(False, nc, qf, kf, vf, lrf, wdf, bdf, stf, dof, dqf, dkf, dvf, dlrf, dwdf, dbdf,
                             dstate_f),
                _gla_bwd_dir(True, nc, qb, kb, vb, lrb, wdb, bdb, stb, dob, dqb, dkb, dvb, dlrb, dwdb, dbdb,
                             dstate_b)]
        first = [intra() for intra, _, _ in dirs]
        dsts = [scan(tile, per) for (_, scan, _), (_, tile, per) in zip(dirs, first)]
        for (_, _, inter), (z, tile, per), d in zip(dirs, first, dsts):
            inter(z, tile, per, d)

    wd_spec, bd_spec = _const_spec((LANE, KEY_W)), _const_spec((1, KEY_W))
    ins = lambda tile: _p_specs(tg, tile) + [pl.BlockSpec((nc, GLA_DV, KEY_W), lambda i: (tile(i), 0, 0)),
                                             pl.BlockSpec((tg, VAL_W), lambda i: (tile(i), 0))]
    outs = lambda tile: (pl.BlockSpec((tg, KEY_W), lambda i: (tile(i), 0)),
                         pl.BlockSpec((tg, KEY_W), lambda i: (tile(i), 0)),
                         pl.BlockSpec((tg, VAL_W), lambda i: (tile(i), 0)),
                         pl.BlockSpec((tg, LANE), lambda i: (tile(i), 0)),
                         _acc_spec((LANE, KEY_W)), _acc_spec((1, KEY_W)))
    out_shape = (jax.ShapeDtypeStruct((t, KEY_W), BF16), jax.ShapeDtypeStruct((t, KEY_W), BF16),
                 jax.ShapeDtypeStruct((t, VAL_W), BF16), jax.ShapeDtypeStruct((t, LANE), BF16),
                 jax.ShapeDtypeStruct((LANE, KEY_W), F32), jax.ShapeDtypeStruct((1, KEY_W), F32))
    scratch = [pltpu.VMEM((GLA_DV, KEY_W), F32)]
    return _fused_call(
        body, comms, name="gla_bwd", grid=(nt,),
        inputs=(p, p, p, p, st_f, d_o, p, p, p, p, st_b, d_o, wd_pad_f, bd_f, wd_pad_b, bd_b),
        in_specs=ins(down) + ins(up) + [wd_spec, bd_spec, wd_spec, bd_spec],
        out_specs=outs(down) + outs(up), out_shape=out_shape * 2, scratch_shapes=scratch * 2)


def _head_rms(o):
    parts, scales = [], []
    for h in range(GLA_HEADS):
        oh = o[:, h * GLA_DV:(h + 1) * GLA_DV]
        r = lax.rsqrt(jnp.mean(oh * oh, axis=-1, keepdims=True) + EPS)
        parts.append(oh * r)
        scales.append(jnp.broadcast_to(r, oh.shape))
    return jnp.concatenate(parts, axis=1), jnp.concatenate(scales, axis=1)


def _layernorm_stats(zv):
    mu = jnp.mean(zv, axis=-1, keepdims=True)
    xc = zv - mu
    rs = lax.rsqrt(jnp.mean(xc * xc, axis=-1, keepdims=True) + EPS)
    return xc * rs, rs


def _mix_fwd(x, o_f, o_b, p, gla_g, ln_g, ln_b, w_sp, b_sp, w_out, tm):
    t = x.shape[0]
    nch = tm // GMLP_CHUNK

    def body(x_ref, of_ref, ob_ref, pg_ref, pu_ref, pv_ref, gg_ref, lg_ref, lb_ref, ws_ref, bs_ref, wo_ref,
             x1_ref, y_ref, s_scr):
        on, _ = _head_rms(of_ref[...] + ob_ref[...])
        pg = pg_ref[...].astype(F32)
        y_a = on * gg_ref[...] * (pg * jax.nn.sigmoid(pg))
        zu = _gelu(pu_ref[...].astype(F32))
        vhat, _ = _layernorm_stats(_gelu(pv_ref[...].astype(F32)))
        vln = (vhat * lg_ref[...] + lb_ref[...]).astype(BF16)
        for g in range(GMLP_GROUPS):
            w_g = ws_ref[g].astype(BF16)
            b_g = bs_ref[g]
            cols = slice(g * LANE, (g + 1) * LANE)
            for n in range(nch):
                rows = slice(n * GMLP_CHUNK, (n + 1) * GMLP_CHUNK)
                s_scr[rows, cols] = jnp.dot(w_g, vln[rows, cols], preferred_element_type=F32) + b_g
        ycat = jnp.concatenate([y_a, zu * s_scr[...]], axis=1).astype(BF16)
        y_ref[...] = ycat
        x1_ref[...] = x_ref[...] + jnp.dot(ycat, wo_ref[...], preferred_element_type=F32)

    half = lambda j: pl.BlockSpec((tm, VAL_W), lambda i: (i, j))
    return pl.pallas_call(
        body, name="mix_fwd", grid=(t // tm,),
        in_specs=[pl.BlockSpec((tm, D_MODEL), lambda i: (i, 0)), half(0), half(0), half(2), half(3), half(4),
                  _const_spec((1, VAL_W)), _const_spec((1, GMLP_W)), _const_spec((1, GMLP_W)),
                  _const_spec((GMLP_GROUPS, GMLP_CHUNK, GMLP_CHUNK)), _const_spec((GMLP_GROUPS, GMLP_CHUNK, 1)),
                  _const_spec((D_MODEL, D_MODEL))],
        out_specs=(pl.BlockSpec((tm, D_MODEL), lambda i: (i, 0)), pl.BlockSpec((tm, D_MODEL), lambda i: (i, 0))),
        out_shape=(jax.ShapeDtypeStruct((t, D_MODEL), F32), jax.ShapeDtypeStruct((t, D_MODEL), BF16)),
        scratch_shapes=[pltpu.VMEM((tm, GMLP_W), F32)],
        compiler_params=_params(),
    )(x, o_f, o_b, p, p, p, gla_g, ln_g, ln_b, w_sp, b_sp, w_out)


def _mix_bwd(dx1, ycat, o_f, o_b, p, gla_g, ln_g, ln_b, w_sp, b_sp, w_out, tm, comms=()):
    t = dx1.shape[0]
    nch = tm // GMLP_CHUNK

    def body(dx1_ref, y_ref, of_ref, ob_ref, pg_ref, pu_ref, pv_ref, gg_ref, lg_ref, lb_ref, ws_ref, bs_ref, wo_ref,
             do_ref, dpg_ref, dpu_ref, dpv_ref, dwo_ref, dgg_ref, dlg_ref, dlb_ref, dws_ref, dbs_ref,
             s_scr, dvln_scr):
        @pl.when(pl.program_id(0) == 0)
        def _():
            for ref in (dwo_ref, dgg_ref, dlg_ref, dlb_ref, dws_ref, dbs_ref):
                ref[...] = jnp.zeros_like(ref)

        dx1 = dx1_ref[...].astype(BF16)
        dycat = _mm_nt(dx1, wo_ref[...])
        dwo_ref[...] += _mm_tn(y_ref[...], dx1)
        dy_a = dycat[:, :VAL_W]
        dy_b = dycat[:, VAL_W:]
        on, r = _head_rms(of_ref[...] + ob_ref[...])
        pg = pg_ref[...].astype(F32)
        sil, dsil = _silu_and_grad(pg)
        gg = gg_ref[...]
        dgg_ref[...] += jnp.sum(dy_a * sil * on, axis=0, keepdims=True)
        don = dy_a * sil * gg
        prod = don * on
        means = jnp.concatenate(
            [jnp.broadcast_to(jnp.mean(prod[:, h * GLA_DV:(h + 1) * GLA_DV], axis=-1, keepdims=True),
                              (tm, GLA_DV)) for h in range(GLA_HEADS)], axis=1)
        do_ref[...] = (r * (don - on * means)).astype(BF16)
        dpg_ref[...] = (dy_a * on * gg * dsil).astype(BF16)
        pu = pu_ref[...].astype(F32)
        pv = pv_ref[...].astype(F32)
        zu, dzu_dpu = _gelu_and_grad(pu)
        zv, dzv_dpv = _gelu_and_grad(pv)
        vhat, rs = _layernorm_stats(zv)
        lg = lg_ref[...]
        vln = (vhat * lg + lb_ref[...]).astype(BF16)
        ds32 = dy_b * zu
        ds = ds32.astype(BF16)
        for g in range(GMLP_GROUPS):
            w_g = ws_ref[g].astype(BF16)
            b_g = bs_ref[g]
            cols = slice(g * LANE, (g + 1) * LANE)
            dw_g = jnp.zeros((GMLP_CHUNK, GMLP_CHUNK), F32)
            db_g = jnp.zeros((GMLP_CHUNK, 1), F32)
            for n in range(nch):
                rows = slice(n * GMLP_CHUNK, (n + 1) * GMLP_CHUNK)
                v_blk = vln[rows, cols]
                ds_blk = ds[rows, cols]
                s_scr[rows, cols] = jnp.dot(w_g, v_blk, preferred_element_type=F32) + b_g
                dw_g = dw_g + _mm_nt(ds_blk, v_blk)
                db_g = db_g + jnp.sum(ds32[rows, cols], axis=-1, keepdims=True)
                dvln_scr[rows, cols] = _mm_tn(w_g, ds_blk)
            dws_ref[g] += dw_g
            dbs_ref[g] += db_g
        dpu_ref[...] = (dy_b * s_scr[...] * dzu_dpu).astype(BF16)
        dvln = dvln_scr[...]
        dlg_ref[...] += jnp.sum(dvln * vhat, axis=0, keepdims=True)
        dlb_ref[...] += jnp.sum(dvln, axis=0, keepdims=True)
        dvhat = dvln * lg
        dzv = rs * (dvhat - jnp.mean(dvhat, axis=-1, keepdims=True)
                    - vhat * jnp.mean(dvhat * vhat, axis=-1, keepdims=True))
        dpv_ref[...] = (dzv * dzv_dpv).astype(BF16)

    half = lambda j: pl.BlockSpec((tm, VAL_W), lambda i: (i, j))
    full = pl.BlockSpec((tm, D_MODEL), lambda i: (i, 0))
    sp_shape = (GMLP_GROUPS, GMLP_CHUNK, GMLP_CHUNK)
    bs_shape = (GMLP_GROUPS, GMLP_CHUNK, 1)
    return _fused_call(
        body, comms, name="mix_bwd", grid=(t // tm,),
        inputs=(dx1, ycat, o_f, o_b, p, p, p, gla_g, ln_g, ln_b, w_sp, b_sp, w_out),
        in_specs=[full, full, half(0), half(0), half(2), half(3), half(4),
                  _const_spec((1, VAL_W)), _const_spec((1, GMLP_W)), _const_spec((1, GMLP_W)),
                  _const_spec(sp_shape), _const_spec(bs_shape), _const_spec((D_MODEL, D_MODEL))],
        out_specs=(half(0), half(0), half(0), half(0), _acc_spec((D_MODEL, D_MODEL)), _acc_spec((1, VAL_W)),
                   _acc_spec((1, GMLP_W)), _acc_spec((1, GMLP_W)), _acc_spec(sp_shape), _acc_spec(bs_shape)),
        out_shape=(jax.ShapeDtypeStruct((t, VAL_W), BF16),) * 4 + (
            jax.ShapeDtypeStruct((D_MODEL, D_MODEL), F32), jax.ShapeDtypeStruct((1, VAL_W), F32),
            jax.ShapeDtypeStruct((1, GMLP_W), F32), jax.ShapeDtypeStruct((1, GMLP_W), F32),
            jax.ShapeDtypeStruct(sp_shape, F32), jax.ShapeDtypeStruct(bs_shape, F32)),
        scratch_shapes=[pltpu.VMEM((tm, GMLP_W), F32), pltpu.VMEM((tm, GMLP_W), F32)])


def _rms_bwd(dy_scaled, xn, r):
    return r * (dy_scaled - xn * jnp.mean(dy_scaled * xn, axis=-1, keepdims=True))


def _ffn(x1, target, g2, gf, w_gate, w_up, w_down, tm):
    t = x1.shape[0]

    def body(x1_ref, tg_ref, g2_ref, gf_ref, wg_ref, wu_ref, wd_ref,
             dx1_ref, h2_ref, dgate_ref, dup_ref, act_ref, dx2_ref, loss_ref, dgf_ref, dg2_ref):
        @pl.when(pl.program_id(0) == 0)
        def _():
            for ref in (loss_ref, dgf_ref, dg2_ref):
                ref[...] = jnp.zeros_like(ref)

        x1v = x1_ref[...]
        g2v = g2_ref[...]
        gfv = gf_ref[...]
        r2 = lax.rsqrt(jnp.mean(x1v * x1v, axis=-1, keepdims=True) + EPS)
        xn1 = x1v * r2
        h2 = (xn1 * g2v).astype(BF16)
        h2_ref[...] = h2
        gate = _mm_nt(h2, wg_ref[...])
        up = _mm_nt(h2, wu_ref[...])
        sil, dsil = _silu_and_grad(gate)
        act = (sil * up).astype(BF16)
        act_ref[...] = act
        x2 = x1v + jnp.dot(act, wd_ref[...], preferred_element_type=F32)
        rf = lax.rsqrt(jnp.mean(x2 * x2, axis=-1, keepdims=True) + EPS)
        xn2 = x2 * rf
        err = xn2 * gfv - tg_ref[...]
        loss_ref[...] += 0.5 * jnp.sum(jnp.mean(err * err, axis=-1, keepdims=True))
        dy = err * (1.0 / D_MODEL)
        dgf_ref[...] += jnp.sum(dy * xn2, axis=0, keepdims=True)
        dx2 = _rms_bwd(dy * gfv, xn2, rf)
        dx2b = dx2.astype(BF16)
        dx2_ref[...] = dx2b
        dact = _mm_nt(dx2b, wd_ref[...])
        dgate = (dact * up * dsil).astype(BF16)
        dup = (dact * sil).astype(BF16)
        dgate_ref[...] = dgate
        dup_ref[...] = dup
        dh2 = _mm(dgate, wg_ref[...]) + _mm(dup, wu_ref[...])
        dg2_ref[...] += jnp.sum(dh2 * xn1, axis=0, keepdims=True)
        dx1_ref[...] = dx2 + _rms_bwd(dh2 * g2v, xn1, r2)

    row = lambda w: pl.BlockSpec((tm, w), lambda i: (i, 0))
    return pl.pallas_call(
        body, name="ffn_fwd_bwd", grid=(t // tm,),
        in_specs=[row(D_MODEL), row(D_MODEL), _const_spec((1, D_MODEL)), _const_spec((1, D_MODEL)),
                  _const_spec((D_FF, D_MODEL)), _const_spec((D_FF, D_MODEL)), _const_spec((D_FF, D_MODEL))],
        out_specs=(row(D_MODEL), row(D_MODEL), row(D_FF), row(D_FF), row(D_FF), row(D_MODEL),
                   _acc_spec((8, LANE)), _acc_spec((1, D_MODEL)), _acc_spec((1, D_MODEL))),
        out_shape=(jax.ShapeDtypeStruct((t, D_MODEL), F32), jax.ShapeDtypeStruct((t, D_MODEL), BF16),
                   jax.ShapeDtypeStruct((t, D_FF), BF16), jax.ShapeDtypeStruct((t, D_FF), BF16),
                   jax.ShapeDtypeStruct((t, D_FF), BF16), jax.ShapeDtypeStruct((t, D_MODEL), BF16),
                   jax.ShapeDtypeStruct((8, LANE), F32), jax.ShapeDtypeStruct((1, D_MODEL), F32),
                   jax.ShapeDtypeStruct((1, D_MODEL), F32)),
        compiler_params=_params(),
    )(x1, target, g2, gf, w_gate, w_up, w_down)


def _matmul_tn(a, b, tm, tk, name, comms=()):
    t, m = a.shape
    n = b.shape[1]

    def body(a_ref, b_ref, o_ref):
        @pl.when(pl.program_id(1) == 0)
        def _():
            o_ref[...] = jnp.zeros_like(o_ref)

        o_ref[...] += _mm_tn(a_ref[...], b_ref[...])

    (out,), comm_results = _fused_call(
        body, comms, name=name, grid=(m // tm, t // tk), inputs=(a, b),
        in_specs=[pl.BlockSpec((tk, tm), lambda j, k: (k, j)), pl.BlockSpec((tk, n), lambda j, k: (k, 0))],
        out_specs=(pl.BlockSpec((tm, n), lambda j, k: (j, 0)),),
        out_shape=(jax.ShapeDtypeStruct((m, n), F32),))
    return out, comm_results


def _in_proj_bwd(x, g1, dx1, dq_f, dq_b, dk_f, dk_b, dv_f, dv_b, dpg, dpu, dpv, dlr_f, dlr_b, w_main, tm, comms=()):
    t = x.shape[0]

    def body(x_ref, g_ref, dx1_ref, dqf, dqb, dkf, dkb, dvf, dvb, dg, du, dv, dlf, dlb, w_ref,
             dx_ref, dp_ref, dg1_ref):
        @pl.when(pl.program_id(0) == 0)
        def _():
            dg1_ref[...] = jnp.zeros_like(dg1_ref)

        both = lambda a, b: (a[...].astype(F32) + b[...].astype(F32)).astype(BF16)
        dp = jnp.concatenate([both(dqf, dqb), both(dkf, dkb), both(dvf, dvb), dg[...], du[...], dv[...],
                              both(dlf, dlb)], axis=1)
        dp_ref[...] = dp
        dh = sum(_mm(dp[:, c0:c0 + r1 - r0], w_ref[r0:r1, :]) for r0, r1, c0 in PROJ_ROWS)
        xv = x_ref[...]
        r = lax.rsqrt(jnp.mean(xv * xv, axis=-1, keepdims=True) + EPS)
        xn = xv * r
        dg1_ref[...] += jnp.sum(dh * xn, axis=0, keepdims=True)
        dx_ref[...] = dx1_ref[...] + _rms_bwd(dh * g_ref[...], xn, r)

    row = lambda w: pl.BlockSpec((tm, w), lambda i: (i, 0))
    return _fused_call(
        body, comms, name="in_proj_bwd", grid=(t // tm,),
        inputs=(x, g1, dx1, dq_f, dq_b, dk_f, dk_b, dv_f, dv_b, dpg, dpu, dpv, dlr_f, dlr_b, w_main),
        in_specs=[row(D_MODEL), _const_spec((1, D_MODEL)), row(D_MODEL), row(KEY_W), row(KEY_W), row(KEY_W),
                  row(KEY_W), row(VAL_W), row(VAL_W), row(VAL_W), row(VAL_W), row(VAL_W), row(LANE), row(LANE),
                  _const_spec((PROJ_W, D_MODEL))],
        out_specs=(row(D_MODEL), row(PROJ_PAD), _acc_spec((1, D_MODEL))),
        out_shape=(jax.ShapeDtypeStruct((t, D_MODEL), F32), jax.ShapeDtypeStruct((t, PROJ_PAD), BF16),
                   jax.ShapeDtypeStruct((1, D_MODEL), F32)))


def _adamw(w, g, m, v):
    m_new = ADAM_B1 * m + (1.0 - ADAM_B1) * g
    v_new = ADAM_B2 * v + (1.0 - ADAM_B2) * (g * g)
    m_hat = m_new / (1.0 - ADAM_B1 ** ADAM_STEP)
    v_hat = v_new / (1.0 - ADAM_B2 ** ADAM_STEP)
    delta = -ADAM_LR * (m_hat / (jnp.sqrt(v_hat) + ADAM_EPS) + ADAM_WD * w)
    return delta, m_new, v_new


def _adamw_shard(own, recv, w, m, v, tr, name):
    r, c = w.shape

    def body(own_ref, recv_ref, w_ref, m_ref, v_ref, g_ref, d_ref, nm_ref, nv_ref):
        g = own_ref[...]
        for k in range(3):
            g = g + recv_ref[k].astype(F32)
        g_ref[...] = g
        d_ref[...], nm_ref[...], nv_ref[...] = _adamw(w_ref[...], g, m_ref[...], v_ref[...])

    row = pl.BlockSpec((tr, c), lambda i: (i, 0))
    return pl.pallas_call(
        body, name=name, grid=(r // tr,),
        in_specs=[row, pl.BlockSpec((3, tr, c), lambda i: (0, i, 0)), row, row, row],
        out_specs=(row,) * 4, out_shape=(jax.ShapeDtypeStruct((r, c), F32),) * 4,
        compiler_params=_params(),
    )(own, recv, w, m, v)


def _adamw_small(entries):
    stacks = []
    for (g, _, _), _, _, _ in entries:
        if not any(g is s for s in stacks):
            stacks.append(g)
    where = [next(i for i, s in enumerate(stacks) if s is g) for (g, _, _), _, _, _ in entries]
    ns, ne = len(stacks), len(entries)

    def body(*refs):
        s_refs, wmv, outs = refs[:ns], refs[ns:ns + 3 * ne], refs[ns + 3 * ne:]
        for e, ((_, r0, nr), _, _, _) in enumerate(entries):
            grad = s_refs[where[e]][r0:r0 + nr, :]
            w_ref, m_ref, v_ref = wmv[3 * e:3 * e + 3]
            g_ref, d_ref, nm_ref, nv_ref = outs[4 * e:4 * e + 4]
            g_ref[...] = grad
            d_ref[...], nm_ref[...], nv_ref[...] = _adamw(w_ref[...], grad, m_ref[...], v_ref[...])

    results = pl.pallas_call(
        body, name="adamw_small",
        out_shape=tuple(jax.ShapeDtypeStruct(w.shape, F32) for _, w, _, _ in entries for _ in range(4)),
        compiler_params=pltpu.CompilerParams(vmem_limit_bytes=VMEM_LIMIT),
    )(*stacks, *[a for _, w, m, v in entries for a in (w, m, v)])
    return [results[4 * e:4 * e + 4] for e in range(ne)]


def _mesh_pos():
    return lax.axis_index("x"), lax.axis_index("y"), lax.axis_index("c")


def _other_chips(x, y):
    return [(x, 1 - y), (1 - x, y), (1 - x, 1 - y)]


_VMEM_WHOLE = pl.BlockSpec(memory_space=pltpu.VMEM)
_HBM_WHOLE = pl.BlockSpec(memory_space=pl.ANY)


def _gather_comm(shards, cast, mid=(1, 2)):
    na = len(shards)
    staged = [a for a in range(na) if cast[a]]

    def phases(in_refs, out_refs, scr):
        stage = dict(zip(staged, scr[:len(staged)]))
        send_sems, recv_sems, local_sems = scr[len(staged):]
        x, y, c = _mesh_pos()
        me, sibling = (x, y, c), (x, y, 1 - c)
        chips = _other_chips(x, y)
        srcs = [stage[a] if cast[a] else in_refs[a] for a in range(na)]

        def rows(a, pos):
            px, py, pc = pos
            return out_refs[a].at[4 * px + 2 * py + pc]

        def copy(a, k, block, to, src=None):
            return pltpu.make_async_remote_copy(
                src_ref=rows(a, block) if src is None else src, dst_ref=rows(a, block),
                send_sem=send_sems.at[a, k], recv_sem=recv_sems.at[a, k], device_id=to, device_id_type=MESH_ID)

        mine = [pltpu.make_async_copy(srcs[a], rows(a, me), local_sems.at[a]) for a in range(na)]
        first = []
        for a in range(na):
            first.append(copy(a, 0, me, sibling, src=srcs[a]))
            first += [copy(a, 1 + j, me, (*chip, c), src=srcs[a]) for j, chip in enumerate(chips)]
        passed = [copy(a, 4 + j, (*chip, c), sibling) for j, chip in enumerate(chips) for a in range(na)]

        def start():
            for a in staged:
                stage[a][...] = in_refs[a][...].astype(BF16)
            for cp in mine + first:
                cp.start()

        def forward():
            i = 0
            for j, chip in enumerate(chips):
                for a in range(na):
                    copy(a, 1 + j, (*chip, c), me).wait_recv()
                    passed[i].start()
                    i += 1

        def finish():
            for a in range(na):
                copy(a, 0, sibling, me).wait_recv()
                for j, chip in enumerate(chips):
                    copy(a, 4 + j, (*chip, 1 - c), me).wait_recv()
            for cp in first + passed:
                cp.wait_send()
            for cp in mine:
                cp.wait()

        return start, forward, finish

    def before(step, nsteps, in_refs, out_refs, scr):
        start, forward, _ = phases(in_refs, out_refs, scr)
        pl.when(step == 0)(start)
        pl.when(step == nsteps * mid[0] // mid[1])(forward)

    def after(step, nsteps, in_refs, out_refs, scr):
        pl.when(step == nsteps - 1)(phases(in_refs, out_refs, scr)[2])

    return _Comm(
        inputs=list(shards), in_specs=[_VMEM_WHOLE] * na,
        out_shape=[jax.ShapeDtypeStruct((N_DEV,) + s.shape, BF16 if cast[a] else s.dtype)
                   for a, s in enumerate(shards)],
        out_specs=[_HBM_WHOLE] * na,
        scratch_shapes=[pltpu.VMEM(shards[a].shape, BF16) for a in staged] + [
            pltpu.SemaphoreType.DMA((na, 7)), pltpu.SemaphoreType.DMA((na, 7)), pltpu.SemaphoreType.DMA((na,))],
        before=before, after=after)


def _exchange_comm(arrays, out_shape, make_copies):
    na = len(arrays)

    def copies(in_refs, out_refs, scr):
        return make_copies(in_refs, out_refs, *scr)

    def before(step, nsteps, in_refs, out_refs, scr):
        @pl.when(step == 0)
        def _():
            for cp in copies(in_refs, out_refs, scr):
                cp.start()

    def after(step, nsteps, in_refs, out_refs, scr):
        @pl.when(step == nsteps - 1)
        def _():
            for cp in copies(in_refs, out_refs, scr):
                cp.wait()

    return _Comm(inputs=list(arrays), in_specs=[_HBM_WHOLE] * na, out_shape=list(out_shape),
                 out_specs=[_HBM_WHOLE] * na,
                 scratch_shapes=[pltpu.SemaphoreType.DMA((na, 3)), pltpu.SemaphoreType.DMA((na, 3))],
                 before=before, after=after)


def _sibling_exchange_comm(grads):
    def make_copies(in_refs, out_refs, send_sems, recv_sems):
        x, y, c = _mesh_pos()
        return [pltpu.make_async_remote_copy(
            src_ref=in_refs[a].at[:, pl.ds(1 - c, 1)], dst_ref=out_refs[a], send_sem=send_sems.at[a, 0],
            recv_sem=recv_sems.at[a, 0], device_id=(x, y, 1 - c), device_id_type=MESH_ID)
            for a in range(len(grads))]

    return _exchange_comm(grads, [jax.ShapeDtypeStruct((4, 1) + g.shape[2:], F32) for g in grads], make_copies)


def _chips_exchange_comm(partials):
    def make_copies(in_refs, out_refs, send_sems, recv_sems):
        x, y, c = _mesh_pos()
        return [pltpu.make_async_remote_copy(
            src_ref=in_refs[a].at[j], dst_ref=out_refs[a].at[j], send_sem=send_sems.at[a, j],
            recv_sem=recv_sems.at[a, j], device_id=(*chip, c), device_id_type=MESH_ID)
            for a in range(len(partials)) for j, chip in enumerate(_other_chips(x, y))]

    return _exchange_comm(partials, [jax.ShapeDtypeStruct(g.shape, BF16) for g in partials], make_copies)


def _comm_only(comms, name):
    return _fused_call(lambda: None, comms, name=name, grid=(1,), inputs=(), in_specs=[], out_specs=(),
                       out_shape=())[1]


def _chip_sum(my_pos, mine, from_sibling, tr, name):
    _, _, r, c = mine.shape

    def body(pos_ref, a_ref, b_ref, own_ref, out_ref):
        s = a_ref[0, 0] + b_ref[0, 0]

        @pl.when(pl.program_id(1) == 0)
        def _():
            own_ref[...] = s

        @pl.when(pl.program_id(1) > 0)
        def _():
            out_ref[0] = s.astype(BF16)

    grid_spec = pltpu.PrefetchScalarGridSpec(
        num_scalar_prefetch=1, grid=(r // tr, 4),
        in_specs=[pl.BlockSpec((1, 1, tr, c), lambda i, k, pos: (pos[0] ^ k, pos[1], i, 0)),
                  pl.BlockSpec((1, 1, tr, c), lambda i, k, pos: (pos[0] ^ k, 0, i, 0))],
        out_specs=(pl.BlockSpec((tr, c), lambda i, k, pos: (i, 0)),
                   pl.BlockSpec((1, tr, c), lambda i, k, pos: (jnp.maximum(k - 1, 0), i, 0))))
    return pl.pallas_call(
        body, name=name, grid_spec=grid_spec,
        out_shape=(jax.ShapeDtypeStruct((r, c), F32), jax.ShapeDtypeStruct((3, r, c), BF16)),
        compiler_params=_params(2),
    )(my_pos, mine, from_sibling)


def _all_reduce_small_comm(parts):
    na = len(parts)

    def copies(in_refs, scr):
        gathered, (send_sems, recv_sems) = scr[:na], scr[na:]
        x, y, c = _mesh_pos()
        my_id = 4 * x + 2 * y + c
        return my_id, [pltpu.make_async_remote_copy(
            src_ref=in_refs[a], dst_ref=gathered[a].at[my_id], send_sem=send_sems.at[a, k - 1],
            recv_sem=recv_sems.at[a, k - 1], device_id=(x ^ (k >> 2), y ^ ((k >> 1) & 1), c ^ (k & 1)),
            device_id_type=MESH_ID) for a in range(na) for k in range(1, N_DEV)]

    def before(step, nsteps, in_refs, out_refs, scr):
        @pl.when(step == 0)
        def _():
            for cp in copies(in_refs, scr)[1]:
                cp.start()

    def after(step, nsteps, in_refs, out_refs, scr):
        @pl.when(step == nsteps - 1)
        def _():
            my_id, cps = copies(in_refs, scr)
            for a in range(na):
                scr[a][my_id] = in_refs[a][...]
            for cp in cps:
                cp.wait()
            for a in range(na):
                acc = scr[a][0]
                for d in range(1, N_DEV):
                    acc = acc + scr[a][d]
                out_refs[a][...] = acc

    return _Comm(inputs=list(parts), in_specs=[_VMEM_WHOLE] * na,
                 out_shape=[jax.ShapeDtypeStruct(p.shape, F32) for p in parts], out_specs=[_VMEM_WHOLE] * na,
                 scratch_shapes=[pltpu.VMEM((N_DEV,) + p.shape, F32) for p in parts] + [
                     pltpu.SemaphoreType.DMA((na, N_DEV - 1)), pltpu.SemaphoreType.DMA((na, N_DEV - 1))],
                 before=before, after=after)


def _unshard_cols(g):
    return jnp.transpose(g, (1, 0, 2)).reshape(g.shape[1], N_DEV * g.shape[2])


def _row_blocks(w):
    return w.reshape(4, 2, w.shape[0] // N_DEV, w.shape[1])


def _stack_rows(parts):
    a = jnp.concatenate(parts, axis=0)
    return jnp.pad(a, ((0, (-a.shape[0]) % 8), (0, 0)))


def _main_proj_grad(dw_main_t):
    return jnp.concatenate([dw_main_t[:LR_REF], dw_main_t[LR_COL:LR_COL + 2 * LOWRANK], dw_main_t[LR_REF:LR_COL]],
                           axis=0)


def _padded_decay_weights(wd_f, wd_b):
    zeros = lambda n: jnp.zeros((n, KEY_W), F32)
    return (jnp.concatenate([wd_f, zeros(LANE - LOWRANK)], axis=0),
            jnp.concatenate([zeros(LOWRANK), wd_b, zeros(LANE - 2 * LOWRANK)], axis=0))


def kernel(x, norm1_g, w_in,w_decay_f, b_decay_f, w_decay_b, b_decay_b, gla_norm_g, gmlp_ln_g, gmlp_ln_b, w_spatial, b_spatial, w_out, norm2_g, w_gate, w_up, w_down, final_norm_g, loss_target, m_norm1_g, m_w_in, m_w_decay_f, m_b_decay_f, m_w_decay_b, m_b_decay_b, m_gla_norm_g, m_gmlp_ln_g, m_gmlp_ln_b, m_w_spatial, m_b_spatial, m_w_out, m_norm2_g, m_w_gate, m_w_up, m_w_down, m_final_norm_g, v_norm1_g, v_w_in, v_w_decay_f, v_b_decay_f, v_w_decay_b, v_b_decay_b, v_gla_norm_g, v_gmlp_ln_g, v_gmlp_ln_b, v_w_spatial, v_b_spatial, v_w_out, v_norm2_g, v_w_gate, v_w_up, v_w_down, v_final_norm_g):
    t = x.shape[1]
    xt = x[0]
    target = loss_target[0]
    pos_x, pos_y, pos_c = _mesh_pos()
    my_pos = jnp.stack([2 * pos_x + pos_y, pos_c]).astype(jnp.int32)
    my_id = 4 * pos_x + 2 * pos_y + pos_c

    tile = lambda n: min(n, t)
    ln_g, ln_b, w_sp = gmlp_ln_g, gmlp_ln_b, w_spatial[0]
    b_sp_col = b_spatial[0][:, :, None]
    shard = {"w_in": w_in[0].T, "w_out": w_out[0], "w_gate": w_gate[0].T, "w_up": w_up[0].T, "w_down": w_down[0]}
    shard_m = {"w_in": m_w_in[0].T, "w_out": m_w_out[0], "w_gate": m_w_gate[0].T, "w_up": m_w_up[0].T,
               "w_down": m_w_down[0]}
    shard_v = {"w_in": v_w_in[0].T, "w_out": v_w_out[0], "w_gate": v_w_gate[0].T, "w_up": v_w_up[0].T,
               "w_down": v_w_down[0]}
    transposed = ("w_in", "w_gate", "w_up")
    chip_sum = lambda n, g, s: _chip_sum(my_pos, g, s[0], g.shape[2], "chip_sum_" + n)

    decay_shard = jnp.stack([w_decay_f[0], w_decay_b[0]])
    (hb,), ((g_in, g_decay),) = _norm1(xt, norm1_g, tile(512),
                                       [_gather_comm([shard["w_in"], decay_shard], [True, False])])
    w_in_t = g_in.reshape(PROJ_W, D_MODEL)
    wd_pad_f, wd_pad_b = _padded_decay_weights(_unshard_cols(g_decay[:, 0]), _unshard_cols(g_decay[:, 1]))
    (p,), ((g_gate,),) = _in_proj(hb, w_in_t, tile(1024), [_gather_comm([shard["w_gate"]], [True])])
    (o_f, st_f, o_b, st_b), ((g_up, g_out, g_down),) = _gla_fwd(
        p, wd_pad_f, b_decay_f, wd_pad_b, b_decay_b, tile(512),
        [_gather_comm([shard["w_up"], shard["w_out"], shard["w_down"]], [True, True, True])])
    w_out_full = g_out.reshape(D_MODEL, D_MODEL)
    x1, ycat = _mix_fwd(xt, o_f, o_b, p, gla_norm_g, ln_g, ln_b, w_sp, b_sp_col, w_out_full, tile(512))

    dx1, h2b, dgate, dup, act, dx2, loss_acc, d_gf, d_g2 = _ffn(
        x1, target, norm2_g, final_norm_g[None, :], g_gate.reshape(D_FF, D_MODEL), g_up.reshape(D_FF, D_MODEL),
        g_down.reshape(D_FF, D_MODEL), tile(256))
    dw_gate, _ = _matmul_tn(dgate, h2b, D_FF // 2, tile(2048), "grad_w_gate")
    dw_up, _ = _matmul_tn(dup, h2b, D_FF // 2, tile(2048), "grad_w_up")
    dw_down, _ = _matmul_tn(act, dx2, D_FF // 2, tile(2048), "grad_w_down")

    ffn_grads = [_row_blocks(dw_gate), _row_blocks(dw_up), _row_blocks(dw_down)]
    (d_o, dpg, dpu, dpv, dw_out, d_gg, d_lg, d_lb, dw_sp, db_sp), (ffn_sib,) = _mix_bwd(
        dx1, ycat, o_f, o_b, p, gla_norm_g, ln_g, ln_b, w_sp, b_sp_col, w_out_full, tile(256),
        [_sibling_exchange_comm(ffn_grads)])
    ffn_names = ["w_gate", "w_up", "w_down"]
    ffn_sums = [chip_sum(n, g, [s]) for n, g, s in zip(ffn_names, ffn_grads, ffn_sib)]
    out_grad = _row_blocks(dw_out)
    (dq_f, dk_f, dv_f, dlr_f, dwd_f, dbd_f, dq_b, dk_b, dv_b, dlr_b, dwd_b, dbd_b), (ffn_recv, out_sib) = _gla_bwd(
        p, wd_pad_f, b_decay_f, wd_pad_b, b_decay_b, st_f, st_b, d_o, tile(512),
        [_chips_exchange_comm([s[1] for s in ffn_sums]), _sibling_exchange_comm([out_grad])])
    out_sum = chip_sum("w_out", out_grad, out_sib)
    (grad_x, dp, d_g1), _ = _in_proj_bwd(
        xt, norm1_g, dx1, dq_f, dq_b, dk_f, dk_b, dv_f, dv_b, dpg, dpu, dpv, dlr_f, dlr_b, w_in_t, tile(512))

    stacks = [_stack_rows([d_g1, d_g2, d_gf]), _stack_rows([d_gg, d_lg, d_lb]),
              _stack_rows([dbd_f, dbd_b, jnp.zeros((6, KEY_W), F32), dwd_f[:LOWRANK], dwd_b[LOWRANK:2 * LOWRANK]]),
              _stack_rows([dw_sp.reshape(GMLP_W, GMLP_CHUNK), db_sp[:, :, 0], loss_acc[:1]])]
    dw_main, (small_sums, out_recv) = _matmul_tn(
        dp, hb, PROJ_PAD // 3, tile(2048), "grad_w_in",
        [_all_reduce_small_comm(stacks), _chips_exchange_comm([out_sum[1]])])
    in_grad = _row_blocks(_main_proj_grad(dw_main))
    (in_sib,) = _comm_only([_sibling_exchange_comm([in_grad])], "grad_w_in_exchange_sibling")
    in_sum = chip_sum("w_in", in_grad, in_sib)
    (in_recv,) = _comm_only([_chips_exchange_comm([in_sum[1]])], "grad_w_in_exchange_chips")

    names = ["w_in", "w_out", "w_gate", "w_up", "w_down"]
    sums = [in_sum, out_sum] + ffn_sums
    received = [in_recv[0], out_recv[0]] + list(ffn_recv)
    big_out = {}
    for n, s, rc in zip(names, sums, received):
        res = _adamw_shard(s[0], rc, shard[n], shard_m[n], shard_v[n], shard[n].shape[0], "adamw_" + n)
        big_out[n] = [r.T if n in transposed else r for r in res]

    s1024, s512, s256, s128 = small_sums
    loss = s128[GMLP_W + GMLP_GROUPS, 0]
    col0 = my_id * (KEY_W // N_DEV)
    decay_cols = lambda row0: lax.dynamic_slice(s256, (row0, col0), (LOWRANK, KEY_W // N_DEV))
    flat = lambda a: a.reshape(-1, a.shape[-1])
    small = {
        "norm1_g": ((s1024, 0, 1), norm1_g, m_norm1_g, v_norm1_g),
        "w_decay_f": ((decay_cols(8), 0, LOWRANK), w_decay_f, m_w_decay_f, v_w_decay_f),
        "b_decay_f": ((s256, 0, 1), b_decay_f, m_b_decay_f, v_b_decay_f),
        "w_decay_b": ((decay_cols(8 + LOWRANK), 0, LOWRANK), w_decay_b, m_w_decay_b, v_w_decay_b),
        "b_decay_b": ((s256, 1, 1), b_decay_b, m_b_decay_b, v_b_decay_b),
        "gla_norm_g": ((s512, 0, 1), gla_norm_g, m_gla_norm_g, v_gla_norm_g),
        "gmlp_ln_g": ((s512, 1, 1), gmlp_ln_g, m_gmlp_ln_g, v_gmlp_ln_g),
        "gmlp_ln_b": ((s512, 2, 1), gmlp_ln_b, m_gmlp_ln_b, v_gmlp_ln_b),
        "w_spatial": ((s128, 0, GMLP_W), w_spatial, m_w_spatial, v_w_spatial),
        "b_spatial": ((s128, GMLP_W, GMLP_GROUPS), b_spatial, m_b_spatial, v_b_spatial),
        "norm2_g": ((s1024, 1, 1), norm2_g, m_norm2_g, v_norm2_g),
        "final_norm_g": ((s1024, 2, 1), final_norm_g, m_final_norm_g, v_final_norm_g),
    }
    small_res = _adamw_small([(g, flat(w), flat(m), flat(v)) for g, w, m, v in small.values()])
    small_out = {n: [r.reshape(small[n][1].shape) for r in res] for n, res in zip(small, small_res)}

    order = ["norm1_g", "w_in", "w_decay_f", "b_decay_f", "w_decay_b", "b_decay_b", "gla_norm_g", "gmlp_ln_g",
             "gmlp_ln_b", "w_spatial", "b_spatial", "w_out", "norm2_g", "w_gate", "w_up", "w_down", "final_norm_g"]
    outs = []
    for kind in range(4):
        for n in order:
            outs.append(big_out[n][kind][None] if n in big_out else small_out[n][kind])
    return (loss, grad_x[None], *outs)
```

```python
import functools
import math

import jax
import jax.numpy as jnp
from jax import lax
from jax.experimental import pallas as pl
from jax.experimental.pallas import tpu as pltpu

F32 = jnp.float32
BF16 = jnp.bfloat16

D_MODEL = 1024
GLA_HEADS = 4
GLA_DK = 64
GLA_DV = 128
KEY_W = GLA_HEADS * GLA_DK
VAL_W = GLA_HEADS * GLA_DV
LOWRANK = 16
GLA_TAU = 16.0
GLA_CHUNK = 64
GMLP_W = 512
GMLP_GROUPS = 4
GMLP_CHUNK = 128
D_FF = 2816
EPS = 1e-6
Q_SCALE = GLA_DK ** -0.5
PROJ_PAD = 2688
LR_COL = 2560
LANE = 128
N_DEV = 8

ADAM_LR = 0.001
ADAM_B1 = 0.9
ADAM_B2 = 0.999
ADAM_EPS = 1e-08
ADAM_WD = 0.01
ADAM_STEP = 10

VMEM_LIMIT = 56 * 1024 * 1024
MESH_ID = pl.DeviceIdType.MESH
INV_SQRT2 = 0.7071067811865476
INV_SQRT_2PI = 0.3989422804014327


def _params(n_axes=1):
    return pltpu.CompilerParams(dimension_semantics=("arbitrary",) * n_axes, vmem_limit_bytes=VMEM_LIMIT)


def _mm(a, b):
    return jnp.dot(a.astype(BF16), b.astype(BF16), preferred_element_type=F32)


def _mm_nt(a, b):
    return lax.dot_general(a.astype(BF16), b.astype(BF16), (((1,), (1,)), ((), ())), preferred_element_type=F32)


def _mm_tn(a, b):
    return lax.dot_general(a.astype(BF16), b.astype(BF16), (((0,), (0,)), ((), ())), preferred_element_type=F32)


def _const_spec(shape):
    nd = len(shape)
    return pl.BlockSpec(shape, lambda *_: (0,) * nd, pipeline_mode=pl.Buffered(1))


def _acc_spec(shape):
    nd = len(shape)
    return pl.BlockSpec(shape, lambda *_: (0,) * nd)


class _Comm:
    def __init__(self, inputs, in_specs, out_shape, out_specs, scratch_shapes, before, after):
        self.inputs, self.in_specs, self.out_shape, self.out_specs = inputs, in_specs, out_shape, out_specs
        self.scratch_shapes, self.before, self.after = scratch_shapes, before, after


def _fused_call(body, comms, *, name, grid, inputs, in_specs, out_specs, out_shape, scratch_shapes=()):
    n_in, n_out, n_scr = len(in_specs), len(out_specs), len(scratch_shapes)
    nsteps = math.prod(grid)
    sizes = [(len(c.inputs), len(c.out_shape), len(c.scratch_shapes)) for c in comms]

    def full_body(*refs):
        step = pl.program_id(0)
        for axis in range(1, len(grid)):
            step = step * grid[axis] + pl.program_id(axis)
        ins, rest = refs[:n_in], refs[n_in:]
        c_ins = []
        for ci, _, _ in sizes:
            c_ins.append(rest[:ci])
            rest = rest[ci:]
        outs, rest = rest[:n_out], rest[n_out:]
        c_outs = []
        for _, co, _ in sizes:
            c_outs.append(rest[:co])
            rest = rest[co:]
        scr, rest = rest[:n_scr], rest[n_scr:]
        c_scr = []
        for _, _, cs in sizes:
            c_scr.append(rest[:cs])
            rest = rest[cs:]
        for c, a, b, s in zip(comms, c_ins, c_outs, c_scr):
            c.before(step, nsteps, a, b, s)
        body(*ins, *outs, *scr)
        for c, a, b, s in zip(comms, c_ins, c_outs, c_scr):
            c.after(step, nsteps, a, b, s)

    results = pl.pallas_call(
        full_body, name=name, grid=grid,
        in_specs=list(in_specs) + [s for c in comms for s in c.in_specs],
        out_specs=tuple(out_specs) + tuple(s for c in comms for s in c.out_specs),
        out_shape=tuple(out_shape) + tuple(s for c in comms for s in c.out_shape),
        scratch_shapes=list(scratch_shapes) + [s for c in comms for s in c.scratch_shapes],
        compiler_params=_params(len(grid)),
    )(*inputs, *[a for c in comms for a in c.inputs])
    own, rest = results[:n_out], results[n_out:]
    comm_results = []
    for _, co, _ in sizes:
        comm_results.append(rest[:co])
        rest = rest[co:]
    return own, comm_results


def _gelu(x):
    return 0.5 * x * (1.0 + lax.erf(x * INV_SQRT2))


def _gelu_and_grad(x):
    cdf = 0.5 * (1.0 + lax.erf(x * INV_SQRT2))
    return x * cdf, cdf + x * jnp.exp(-0.5 * x * x) * INV_SQRT_2PI


def _silu_and_grad(x):
    s = jax.nn.sigmoid(x)
    return x * s, s * (1.0 + x * (1.0 - s))


def _norm1(x, g1, tm, comms=()):
    t = x.shape[0]

    def body(x_ref, g_ref, h_ref):
        xv = x_ref[...]
        r = lax.rsqrt(jnp.mean(xv * xv, axis=-1, keepdims=True) + EPS)
        h_ref[...] = (xv * r * g_ref[...]).astype(BF16)

    row = pl.BlockSpec((tm, D_MODEL), lambda i: (i, 0))
    return _fused_call(body, comms, name="norm1", grid=(t // tm,), inputs=(x, g1),
                       in_specs=[row, _const_spec((1, D_MODEL))], out_specs=(row,),
                       out_shape=(jax.ShapeDtypeStruct((t, D_MODEL), BF16),))


PROJ_W = 2592
LR_REF = 1536
PROJ_ROWS = ((0, LR_REF, 0), (LR_REF + 2 * LOWRANK, PROJ_W, LR_REF), (LR_REF, LR_REF + LANE, LR_COL))


def _in_proj(h, w_in_t, tm, comms=()):
    t = h.shape[0]

    def body(h_ref, w_ref, p_ref):
        hv = h_ref[...]
        for r0, r1, c0 in PROJ_ROWS:
            p_ref[:, c0:c0 + r1 - r0] = _mm_nt(hv, w_ref[r0:r1, :]).astype(BF16)

    return _fused_call(
        body, comms, name="in_proj", grid=(t // tm,), inputs=(h, w_in_t),
        in_specs=[pl.BlockSpec((tm, D_MODEL), lambda i: (i, 0)), _const_spec((PROJ_W, D_MODEL))],
        out_specs=(pl.BlockSpec((tm, PROJ_PAD), lambda i: (i, 0)),),
        out_shape=(jax.ShapeDtypeStruct((t, PROJ_PAD), BF16),))


def _tri(upper):
    r = lax.broadcasted_iota(jnp.int32, (GLA_CHUNK, GLA_CHUNK), 0)
    c = lax.broadcasted_iota(jnp.int32, (GLA_CHUNK, GLA_CHUNK), 1)
    return jnp.where((c >= r) if upper else (c <= r), 1.0, 0.0).astype(BF16)


def _chunk_cumsum(tri, a):
    hi = a.astype(BF16)
    lo = (a - hi.astype(F32)).astype(BF16)
    dot = functools.partial(jnp.dot, preferred_element_type=F32)
    return jnp.concatenate([dot(tri, hi[_chunk_rows(c)]) + dot(tri, lo[_chunk_rows(c)])
                            for c in range(a.shape[0] // GLA_CHUNK)], axis=0)


def _chunk_rows(c):
    return slice(c * GLA_CHUNK, (c + 1) * GLA_CHUNK)


def _gla_masks(rev):
    dk_bits, dv_bits = GLA_DK.bit_length() - 1, GLA_DV.bit_length() - 1
    key_head = lax.broadcasted_iota(jnp.int32, (GLA_CHUNK, KEY_W), 1) >> dk_bits
    val_head = lax.broadcasted_iota(jnp.int32, (GLA_CHUNK, VAL_W), 1) >> dv_bits
    t = lax.broadcasted_iota(jnp.int32, (GLA_HEADS * GLA_CHUNK, GLA_CHUNK), 0) & (GLA_CHUNK - 1)
    s = lax.broadcasted_iota(jnp.int32, (GLA_HEADS * GLA_CHUNK, GLA_CHUNK), 1)
    causal = (s >= t) if rev else (s <= t)
    state_head = lax.broadcasted_iota(jnp.int32, (GLA_DV, KEY_W), 1) >> dk_bits
    return key_head, val_head, causal, state_head


def _stack_heads(a, head_of_lane):
    a = a.astype(BF16)
    return jnp.concatenate([jnp.where(head_of_lane == h, a, jnp.zeros_like(a)) for h in range(GLA_HEADS)], axis=0)


def _rows_by_head(a):
    return jnp.concatenate([a[:, h * GLA_DV:(h + 1) * GLA_DV] for h in range(GLA_HEADS)], axis=0)


def _lanes_by_head(r):
    return jnp.concatenate([r[h * GLA_CHUNK:(h + 1) * GLA_CHUNK] for h in range(GLA_HEADS)], axis=1)


def _head_diagonal(r, head_of_lane):
    rows = r.shape[0] // GLA_HEADS
    out = jnp.where(head_of_lane == 0, r[:rows], 0.0)
    for h in range(1, GLA_HEADS):
        out = out + jnp.where(head_of_lane == h, r[h * rows:(h + 1) * rows], 0.0)
    return out


def _tile_terms(la, q, k, tri, rev):
    nc = la.shape[0] // GLA_CHUNK
    q, k = q.astype(F32), k.astype(F32)
    b = _chunk_cumsum(tri, la)
    ebl = [jnp.exp(b[c * GLA_CHUNK:c * GLA_CHUNK + 1] if rev else b[(c + 1) * GLA_CHUNK - 1:(c + 1) * GLA_CHUNK])
           for c in range(nc)]
    eb = jnp.exp(b)
    enb = jnp.exp(-b)
    ee = enb * jnp.concatenate([jnp.broadcast_to(row, (GLA_CHUNK, KEY_W)) for row in ebl], axis=0)
    return ebl, eb, enb, ee, q * Q_SCALE * eb, k * enb, k * ee


def _log_decay(lr_ref, wd_ref, bd_ref):
    z = _mm(lr_ref[...], wd_ref[...]) + bd_ref[...]
    return z, jax.nn.log_sigmoid(z) * (1.0 / GLA_TAU)


def _p_specs(tg, tile):
    return [pl.BlockSpec((tg, KEY_W), lambda i: (tile(i), 0)),
            pl.BlockSpec((tg, KEY_W), lambda i: (tile(i), 1)),
            pl.BlockSpec((tg, VAL_W), lambda i: (tile(i), 1)),
            pl.BlockSpec((tg, LANE), lambda i: (tile(i), LR_COL // LANE))]


def _gla_fwd_dir(rev, nc, q_ref, k_ref, v_ref, lr_ref, wd_ref, bd_ref, o_ref, st_ref, state):
    key_head, _, causal, state_head = _gla_masks(rev)
    order = range(nc - 1, -1, -1) if rev else range(nc)

    def intra():
        _, la = _log_decay(lr_ref, wd_ref, bd_ref)
        ebl, _, _, _, qd, kd, ke = _tile_terms(la, q_ref[...], k_ref[...], _tri(rev), rev)
        kd = kd.astype(BF16)
        v = {c: v_ref[_chunk_rows(c), :].astype(BF16) for c in order}
        qd_stack = {c: _stack_heads(qd[_chunk_rows(c)], key_head) for c in order}
        ke_stack = {c: _stack_heads(ke[_chunk_rows(c)], key_head) for c in order}
        a_all = {c: _mm_nt(qd_stack[c], kd[_chunk_rows(c)]) for c in order}
        a_all = {c: jnp.where(causal, a_all[c], 0.0).astype(BF16) for c in order}
        r = {c: _mm(a_all[c], v[c]) for c in order}
        upd = {c: _mm_tn(_rows_by_head(v[c]), ke_stack[c]) for c in order}
        return {c: (ebl[c], qd_stack[c], r[c], upd[c]) for c in order}

    def scan(terms):
        st = state[...]
        states = {}
        for c in order:
            states[c] = st
            st_ref[c] = st.astype(BF16)
            st = st * terms[c][0] + terms[c][3]
        state[...] = st
        return states

    def inter(terms, states):
        r_inter = {c: _mm_nt(terms[c][1], states[c]) for c in order}
        for c in order:
            r = terms[c][2]
            o_ref[_chunk_rows(c), :] = jnp.concatenate(
                [r[h * GLA_CHUNK:(h + 1) * GLA_CHUNK, h * GLA_DV:(h + 1) * GLA_DV]
                 + r_inter[c][h * GLA_CHUNK:(h + 1) * GLA_CHUNK] for h in range(GLA_HEADS)], axis=1)

    return intra, scan, inter


def _gla_fwd(p, wd_pad_f, bd_f, wd_pad_b, bd_b, tg, comms=()):
    t = p.shape[0]
    nt = t // tg
    nc = tg // GLA_CHUNK
    up, down = (lambda i: i), (lambda i: nt - 1 - i)

    def body(qf, kf, vf, lrf, qb, kb, vb, lrb, wdf, bdf, wdb, bdb, of, stf, ob, stb, state_f, state_b):
        @pl.when(pl.program_id(0) == 0)
        def _():
            state_f[...] = jnp.zeros_like(state_f)
            state_b[...] = jnp.zeros_like(state_b)

        dirs = [_gla_fwd_dir(False, nc, qf, kf, vf, lrf, wdf, bdf, of, stf, state_f),
                _gla_fwd_dir(True, nc, qb, kb, vb, lrb, wdb, bdb, ob, stb, state_b)]
        terms = [intra() for intra, _, _ in dirs]
        states = [scan(t) for (_, scan, _), t in zip(dirs, terms)]
        for (_, _, inter), t, s in zip(dirs, terms, states):
            inter(t, s)

    wd_spec, bd_spec = _const_spec((LANE, KEY_W)), _const_spec((1, KEY_W))
    outs = lambda tile: (pl.BlockSpec((tg, VAL_W), lambda i: (tile(i), 0)),
                         pl.BlockSpec((nc, GLA_DV, KEY_W), lambda i: (tile(i), 0, 0)))
    out_shape = (jax.ShapeDtypeStruct((t, VAL_W), F32), jax.ShapeDtypeStruct((t // GLA_CHUNK, GLA_DV, KEY_W), BF16))
    return _fused_call(
        body, comms, name="gla_fwd", grid=(nt,), inputs=(p,) * 8 + (wd_pad_f, bd_f, wd_pad_b, bd_b),
        in_specs=_p_specs(tg, up) + _p_specs(tg, down) + [wd_spec, bd_spec, wd_spec, bd_spec],
        out_specs=outs(up) + outs(down), out_shape=out_shape * 2,
        scratch_shapes=[pltpu.VMEM((GLA_DV, KEY_W), F32)] * 2)


def _gla_bwd_dir(rev, nc, q_ref, k_ref, v_ref, lr_ref, wd_ref, bd_ref, st_ref, do_ref,
                 dq_ref, dk_ref, dv_ref, dlr_ref, dwd_ref, dbd_ref, dstate):
    key_head, val_head, causal, state_head = _gla_masks(rev)
    order = range(nc) if rev else range(nc - 1, -1, -1)
    tg = nc * GLA_CHUNK

    def intra():
        z, la = _log_decay(lr_ref, wd_ref, bd_ref)
        tile = _tile_terms(la, q_ref[...], k_ref[...], _tri(rev), rev)
        qd, kd = tile[4], tile[5].astype(BF16)
        v = {c: v_ref[_chunk_rows(c), :].astype(BF16) for c in order}
        d_o = {c: do_ref[_chunk_rows(c), :] for c in order}
        kd_c = {c: kd[_chunk_rows(c)] for c in order}
        qd_stack = {c: _stack_heads(qd[_chunk_rows(c)], key_head) for c in order}
        do_stack = {c: _stack_heads(d_o[c], val_head) for c in order}
        do_rows = {c: _rows_by_head(d_o[c]) for c in order}
        a_all = {c: _mm_nt(qd_stack[c], kd_c[c]) for c in order}
        da_all = {c: _mm_nt(do_stack[c], v[c]) for c in order}
        a_all = {c: jnp.where(causal, a_all[c], 0.0).astype(BF16) for c in order}
        da_all = {c: jnp.where(causal, da_all[c], 0.0).astype(BF16) for c in order}
        dv = {c: _mm_tn(a_all[c], do_stack[c]) for c in order}
        dqd = {c: _mm(da_all[c], kd_c[c]) + _mm(do_rows[c], st_ref[c]) for c in order}
        dkd = {c: _mm_tn(da_all[c], qd_stack[c]) for c in order}
        upd = {c: _mm_tn(do_rows[c], qd_stack[c]) for c in order}
        dqd = {c: _head_diagonal(dqd[c], key_head) for c in order}
        return z, tile, {c: dict(dv=dv[c], dqd=dqd[c], dkd=dkd[c], upd=upd[c]) for c in order}

    def scan(tile, per):
        dst = dstate[...]
        dsts = {}
        for c in order:
            dsts[c] = dst
            dst = dst * tile[0][c] + per[c]["upd"]
        dstate[...] = dst
        return dsts

    def inter(z, tile, per, dsts):
        ebl, eb, enb, ee, qd, kd, ke = tile
        ke_stack = {c: _stack_heads(ke[_chunk_rows(c)], key_head) for c in order}
        v_rows = {c: _rows_by_head(v_ref[_chunk_rows(c), :].astype(BF16)) for c in order}
        dst_b = {c: dsts[c].astype(BF16) for c in order}
        dv_state = {c: _mm_nt(ke_stack[c], dst_b[c]) for c in order}
        dke_c = {c: _mm(v_rows[c], dst_b[c]) for c in order}
        dke_c = {c: _head_diagonal(dke_c[c], key_head) for c in order}
        dbl_c = {}
        for c in order:
            rows = _chunk_rows(c)
            dv_ref[rows, :] = (per[c]["dv"] + _lanes_by_head(dv_state[c])).astype(BF16)
            dbl = (jnp.sum(dsts[c] * st_ref[c].astype(F32), axis=0, keepdims=True) * ebl[c]
                   + jnp.sum(dke_c[c] * ke[rows], axis=0, keepdims=True))
            dbl_c[c] = jnp.broadcast_to(dbl, (GLA_CHUNK, KEY_W))
        tile_of = lambda parts: jnp.concatenate([parts[c] for c in range(nc)], axis=0)
        dqd, dkd = tile_of({c: per[c]["dqd"] for c in order}), tile_of({c: per[c]["dkd"] for c in order})
        dke, dbl = tile_of(dke_c), tile_of(dbl_c)
        dq_ref[...] = (dqd * eb * Q_SCALE).astype(BF16)
        dk_ref[...] = (dkd * enb + dke * ee).astype(BF16)
        db = dqd * qd - dkd * kd - dke * ke
        dz = (_chunk_cumsum(_tri(not rev), db) + dbl) * (jax.nn.sigmoid(-z) * (1.0 / GLA_TAU))
        dlr_ref[...] = _mm_nt(dz, wd_ref[...]).astype(BF16)
        dwd_ref[...] += _mm_tn(lr_ref[...], dz)
        dbd_ref[...] += jnp.sum(dz, axis=0, keepdims=True)

    return intra, scan, inter


def _gla_bwd(p, wd_pad_f, bd_f, wd_pad_b, bd_b, st_f, st_b, d_o, tg, comms=()):
    t = p.shape[0]
    nt = t // tg
    nc = tg // GLA_CHUNK
    up, down = (lambda i: i), (lambda i: nt - 1 - i)

    def body(qf, kf, vf, lrf, stf, dof, qb, kb, vb, lrb, stb, dob, wdf, bdf, wdb, bdb,
             dqf, dkf, dvf, dlrf, dwdf, dbdf, dqb, dkb, dvb, dlrb, dwdb, dbdb, dstate_f, dstate_b):
        @pl.when(pl.program_id(0) == 0)
        def _():
            for ref in (dstate_f, dstate_b, dwdf, dbdf, dwdb, dbdb):
                ref[...] = jnp.zeros_like(ref)

        dirs = [_gla_bwd_dir(False, nc, qf, kf, vf, lrf, wdf, bdf, stf, dof, dqf, dkf, dvf, dlrf, dwdf, dbdf,
                             dstate_f),
                _gla_bwd_dir(True, nc, qb, kb, vb, lrb, wdb, bdb, stb, dob, dqb, dkb, dvb, dlrb, dwdb, dbdb,
                             dstate_b)]
        first = [intra() for intra, _, _ in dirs]
        dsts = [scan(tile, per) for (_, scan, _), (_, tile, per) in zip(dirs, first)]
        for (_, _, inter), (z, tile, per), d in zip(dirs, first, dsts):
            inter(z, tile, per, d)

    wd_spec, bd_spec = _const_spec((LANE, KEY_W)), _const_spec((1, KEY_W))
    ins = lambda tile: _p_specs(tg, tile) + [pl.BlockSpec((nc, GLA_DV, KEY_W), lambda i: (tile(i), 0, 0)),
                                             pl.BlockSpec((tg, VAL_W), lambda i: (tile(i), 0))]
    outs = lambda tile: (pl.BlockSpec((tg, KEY_W), lambda i: (tile(i), 0)),
                         pl.BlockSpec((tg, KEY_W), lambda i: (tile(i), 0)),
                         pl.BlockSpec((tg, VAL_W), lambda i: (tile(i), 0)),
                         pl.BlockSpec((tg, LANE), lambda i: (tile(i), 0)),
                         _acc_spec((LANE, KEY_W)), _acc_spec((1, KEY_W)))
    out_shape = (jax.ShapeDtypeStruct((t, KEY_W), BF16), jax.ShapeDtypeStruct((t, KEY_W), BF16),
                 jax.ShapeDtypeStruct((t, VAL_W), BF16), jax.ShapeDtypeStruct((t, LANE), BF16),
                 jax.ShapeDtypeStruct((LANE, KEY_W), F32), jax.ShapeDtypeStruct((1, KEY_W), F32))
    scratch = [pltpu.VMEM((GLA_DV, KEY_W), F32)]
    return _fused_call(
        body, comms, name="gla_bwd", grid=(nt,),
        inputs=(p, p, p, p, st_f, d_o, p, p, p, p, st_b, d_o, wd_pad_f, bd_f, wd_pad_b, bd_b),
        in_specs=ins(down) + ins(up) + [wd_spec, bd_spec, wd_spec, bd_spec],
        out_specs=outs(down) + outs(up), out_shape=out_shape * 2, scratch_shapes=scratch * 2)


def _head_rms(o):
    parts, scales = [], []
    for h in range(GLA_HEADS):
        oh = o[:, h * GLA_DV:(h + 1) * GLA_DV]
        r = lax.rsqrt(jnp.mean(oh * oh, axis=-1, keepdims=True) + EPS)
        parts.append(oh * r)
        scales.append(jnp.broadcast_to(r, oh.shape))
    return jnp.concatenate(parts, axis=1), jnp.concatenate(scales, axis=1)


def _layernorm_stats(zv):
    mu = jnp.mean(zv, axis=-1, keepdims=True)
    xc = zv - mu
    rs = lax.rsqrt(jnp.mean(xc * xc, axis=-1, keepdims=True) + EPS)
    return xc * rs, rs


def _mix_fwd(x, o_f, o_b, p, gla_g, ln_g, ln_b, w_sp, b_sp, w_out, tm):
    t = x.shape[0]
    nch = tm // GMLP_CHUNK

    def body(x_ref, of_ref, ob_ref, pg_ref, pu_ref, pv_ref, gg_ref, lg_ref, lb_ref, ws_ref, bs_ref, wo_ref,
             x1_ref, y_ref, s_scr):
        on, _ = _head_rms(of_ref[...] + ob_ref[...])
        pg = pg_ref[...].astype(F32)
        y_a = on * gg_ref[...] * (pg * jax.nn.sigmoid(pg))
        zu = _gelu(pu_ref[...].astype(F32))
        vhat, _ = _layernorm_stats(_gelu(pv_ref[...].astype(F32)))
        vln = (vhat * lg_ref[...] + lb_ref[...]).astype(BF16)
        for g in range(GMLP_GROUPS):
            w_g = ws_ref[g].astype(BF16)
            b_g = bs_ref[g]
            cols = slice(g * LANE, (g + 1) * LANE)
            for n in range(nch):
                rows = slice(n * GMLP_CHUNK, (n + 1) * GMLP_CHUNK)
                s_scr[rows, cols] = jnp.dot(w_g, vln[rows, cols], preferred_element_type=F32) + b_g
        ycat = jnp.concatenate([y_a, zu * s_scr[...]], axis=1).astype(BF16)
        y_ref[...] = ycat
        x1_ref[...] = x_ref[...] + jnp.dot(ycat, wo_ref[...], preferred_element_type=F32)

    half = lambda j: pl.BlockSpec((tm, VAL_W), lambda i: (i, j))
    return pl.pallas_call(
        body, name="mix_fwd", grid=(t // tm,),
        in_specs=[pl.BlockSpec((tm, D_MODEL), lambda i: (i, 0)), half(0), half(0), half(2), half(3), half(4),
                  _const_spec((1, VAL_W)), _const_spec((1, GMLP_W)), _const_spec((1, GMLP_W)),
                  _const_spec((GMLP_GROUPS, GMLP_CHUNK, GMLP_CHUNK)), _const_spec((GMLP_GROUPS, GMLP_CHUNK, 1)),
                  _const_spec((D_MODEL, D_MODEL))],
        out_specs=(pl.BlockSpec((tm, D_MODEL), lambda i: (i, 0)), pl.BlockSpec((tm, D_MODEL), lambda i: (i, 0))),
        out_shape=(jax.ShapeDtypeStruct((t, D_MODEL), F32), jax.ShapeDtypeStruct((t, D_MODEL), BF16)),
        scratch_shapes=[pltpu.VMEM((tm, GMLP_W), F32)],
        compiler_params=_params(),
    )(x, o_f, o_b, p, p, p, gla_g, ln_g, ln_b, w_sp, b_sp, w_out)


def _mix_bwd(dx1, ycat, o_f, o_b, p, gla_g, ln_g, ln_b, w_sp, b_sp, w_out, tm, comms=()):
    t = dx1.shape[0]
    nch = tm // GMLP_CHUNK

    def body(dx1_ref, y_ref, of_ref, ob_ref, pg_ref, pu_ref, pv_ref, gg_ref, lg_ref, lb_ref, ws_ref, bs_ref, wo_ref,
             do_ref, dpg_ref, dpu_ref, dpv_ref, dwo_ref, dgg_ref, dlg_ref, dlb_ref, dws_ref, dbs_ref,
             s_scr, dvln_scr):
        @pl.when(pl.program_id(0) == 0)
        def _():
            for ref in (dwo_ref, dgg_ref, dlg_ref, dlb_ref, dws_ref, dbs_ref):
                ref[...] = jnp.zeros_like(ref)

        dx1 = dx1_ref[...].astype(BF16)
        dycat = _mm_nt(dx1, wo_ref[...])
        dwo_ref[...] += _mm_tn(y_ref[...], dx1)
        dy_a = dycat[:, :VAL_W]
        dy_b = dycat[:, VAL_W:]
        on, r = _head_rms(of_ref[...] + ob_ref[...])
        pg = pg_ref[...].astype(F32)
        sil, dsil = _silu_and_grad(pg)
        gg = gg_ref[...]
        dgg_ref[...] += jnp.sum(dy_a * sil * on, axis=0, keepdims=True)
        don = dy_a * sil * gg
        prod = don * on
        means = jnp.concatenate(
            [jnp.broadcast_to(jnp.mean(prod[:, h * GLA_DV:(h + 1) * GLA_DV], axis=-1, keepdims=True),
                              (tm, GLA_DV)) for h in range(GLA_HEADS)], axis=1)
        do_ref[...] = (r * (don - on * means)).astype(BF16)
        dpg_ref[...] = (dy_a * on * gg * dsil).astype(BF16)
        pu = pu_ref[...].astype(F32)
        pv = pv_ref[...].astype(F32)
        zu, dzu_dpu = _gelu_and_grad(pu)
        zv, dzv_dpv = _gelu_and_grad(pv)
        vhat, rs = _layernorm_stats(zv)
        lg = lg_ref[...]
        vln = (vhat * lg + lb_ref[...]).astype(BF16)
        ds32 = dy_b * zu
        ds = ds32.astype(BF16)
        for g in range(GMLP_GROUPS):
            w_g = ws_ref[g].astype(BF16)
            b_g = bs_ref[g]
            cols = slice(g * LANE, (g + 1) * LANE)
            dw_g = jnp.zeros((GMLP_CHUNK, GMLP_CHUNK), F32)
            db_g = jnp.zeros((GMLP_CHUNK, 1), F32)
            for n in range(nch):
                rows = slice(n * GMLP_CHUNK, (n + 1) * GMLP_CHUNK)
                v_blk = vln[rows, cols]
                ds_blk = ds[rows, cols]
                s_scr[rows, cols] = jnp.dot(w_g, v_blk, preferred_element_type=F32) + b_g
                dw_g = dw_g + _mm_nt(ds_blk, v_blk)
                db_g = db_g + jnp.sum(ds32[rows, cols], axis=-1, keepdims=True)
                dvln_scr[rows, cols] = _mm_tn(w_g, ds_blk)
            dws_ref[g] += dw_g
            dbs_ref[g] += db_g
        dpu_ref[...] = (dy_b * s_scr[...] * dzu_dpu).astype(BF16)
        dvln = dvln_scr[...]
        dlg_ref[...] += jnp.sum(dvln * vhat, axis=0, keepdims=True)
        dlb_ref[...] += jnp.sum(dvln, axis=0, keepdims=True)
        dvhat = dvln * lg
        dzv = rs * (dvhat - jnp.mean(dvhat, axis=-1, keepdims=True)
                    - vhat * jnp.mean(dvhat * vhat, axis=-1, keepdims=True))
        dpv_ref[...] = (dzv * dzv_dpv).astype(BF16)

    half = lambda j: pl.BlockSpec((tm, VAL_W), lambda i: (i, j))
    full = pl.BlockSpec((tm, D_MODEL), lambda i: (i, 0))
    sp_shape = (GMLP_GROUPS, GMLP_CHUNK, GMLP_CHUNK)
    bs_shape = (GMLP_GROUPS, GMLP_CHUNK, 1)
    return _fused_call(
        body, comms, name="mix_bwd", grid=(t // tm,),
        inputs=(dx1, ycat, o_f, o_b, p, p, p, gla_g, ln_g, ln_b, w_sp, b_sp, w_out),
        in_specs=[full, full, half(0), half(0), half(2), half(3), half(4),
                  _const_spec((1, VAL_W)), _const_spec((1, GMLP_W)), _const_spec((1, GMLP_W)),
                  _const_spec(sp_shape), _const_spec(bs_shape), _const_spec((D_MODEL, D_MODEL))],
        out_specs=(half(0), half(0), half(0), half(0), _acc_spec((D_MODEL, D_MODEL)), _acc_spec((1, VAL_W)),
                   _acc_spec((1, GMLP_W)), _acc_spec((1, GMLP_W)), _acc_spec(sp_shape), _acc_spec(bs_shape)),
        out_shape=(jax.ShapeDtypeStruct((t, VAL_W), BF16),) * 4 + (
            jax.ShapeDtypeStruct((D_MODEL, D_MODEL), F32), jax.ShapeDtypeStruct((1, VAL_W), F32),
            jax.ShapeDtypeStruct((1, GMLP_W), F32), jax.ShapeDtypeStruct((1, GMLP_W), F32),
            jax.ShapeDtypeStruct(sp_shape, F32), jax.ShapeDtypeStruct(bs_shape, F32)),
        scratch_shapes=[pltpu.VMEM((tm, GMLP_W), F32), pltpu.VMEM((tm, GMLP_W), F32)])


def _rms_bwd(dy_scaled, xn, r):
    return r * (dy_scaled - xn * jnp.mean(dy_scaled * xn, axis=-1, keepdims=True))


def _ffn(x1, target, g2, gf, w_gate, w_up, w_down, tm):
    t = x1.shape[0]

    def body(x1_ref, tg_ref, g2_ref, gf_ref, wg_ref, wu_ref, wd_ref,
             dx1_ref, h2_ref, dgate_ref, dup_ref, act_ref, dx2_ref, loss_ref, dgf_ref, dg2_ref):
        @pl.when(pl.program_id(0) == 0)
        def _():
            for ref in (loss_ref, dgf_ref, dg2_ref):
                ref[...] = jnp.zeros_like(ref)

        x1v = x1_ref[...]
        g2v = g2_ref[...]
        gfv = gf_ref[...]
        r2 = lax.rsqrt(jnp.mean(x1v * x1v, axis=-1, keepdims=True) + EPS)
        xn1 = x1v * r2
        h2 = (xn1 * g2v).astype(BF16)
        h2_ref[...] = h2
        gate = _mm_nt(h2, wg_ref[...])
        up = _mm_nt(h2, wu_ref[...])
        sil, dsil = _silu_and_grad(gate)
        act = (sil * up).astype(BF16)
        act_ref[...] = act
        x2 = x1v + jnp.dot(act, wd_ref[...], preferred_element_type=F32)
        rf = lax.rsqrt(jnp.mean(x2 * x2, axis=-1, keepdims=True) + EPS)
        xn2 = x2 * rf
        err = xn2 * gfv - tg_ref[...]
        loss_ref[...] += 0.5 * jnp.sum(jnp.mean(err * err, axis=-1, keepdims=True))
        dy = err * (1.0 / D_MODEL)
        dgf_ref[...] += jnp.sum(dy * xn2, axis=0, keepdims=True)
        dx2 = _rms_bwd(dy * gfv, xn2, rf)
        dx2b = dx2.astype(BF16)
        dx2_ref[...] = dx2b
        dact = _mm_nt(dx2b, wd_ref[...])
        dgate = (dact * up * dsil).astype(BF16)
        dup = (dact * sil).astype(BF16)
        dgate_ref[...] = dgate
        dup_ref[...] = dup
        dh2 = _mm(dgate, wg_ref[...]) + _mm(dup, wu_ref[...])
        dg2_ref[...] += jnp.sum(dh2 * xn1, axis=0, keepdims=True)
        dx1_ref[...] = dx2 + _rms_bwd(dh2 * g2v, xn1, r2)

    row = lambda w: pl.BlockSpec((tm, w), lambda i: (i, 0))
    return pl.pallas_call(
        body, name="ffn_fwd_bwd", grid=(t // tm,),
        in_specs=[row(D_MODEL), row(D_MODEL), _const_spec((1, D_MODEL)), _const_spec((1, D_MODEL)),
                  _const_spec((D_FF, D_MODEL)), _const_spec((D_FF, D_MODEL)), _const_spec((D_FF, D_MODEL))],
        out_specs=(row(D_MODEL), row(D_MODEL), row(D_FF), row(D_FF), row(D_FF), row(D_MODEL),
                   _acc_spec((8, LANE)), _acc_spec((1, D_MODEL)), _acc_spec((1, D_MODEL))),
        out_shape=(jax.ShapeDtypeStruct((t, D_MODEL), F32), jax.ShapeDtypeStruct((t, D_MODEL), BF16),
                   jax.ShapeDtypeStruct((t, D_FF), BF16), jax.ShapeDtypeStruct((t, D_FF), BF16),
                   jax.ShapeDtypeStruct((t, D_FF), BF16), jax.ShapeDtypeStruct((t, D_MODEL), BF16),
                   jax.ShapeDtypeStruct((8, LANE), F32), jax.ShapeDtypeStruct((1, D_MODEL), F32),
                   jax.ShapeDtypeStruct((1, D_MODEL), F32)),
        compiler_params=_params(),
    )(x1, target, g2, gf, w_gate, w_up, w_down)


def _matmul_tn(a, b, tm, tk, name, comms=()):
    t, m = a.shape
    n = b.shape[1]

    def body(a_ref, b_ref, o_ref):
        @pl.when(pl.program_id(1) == 0)
        def _():
            o_ref[...] = jnp.zeros_like(o_ref)

        o_ref[...] += _mm_tn(a_ref[...], b_ref[...])

    (out,), comm_results = _fused_call(
        body, comms, name=name, grid=(m // tm, t // tk), inputs=(a, b),
        in_specs=[pl.BlockSpec((tk, tm), lambda j, k: (k, j)), pl.BlockSpec((tk, n), lambda j, k: (k, 0))],
        out_specs=(pl.BlockSpec((tm, n), lambda j, k: (j, 0)),),
        out_shape=(jax.ShapeDtypeStruct((m, n), F32),))
    return out, comm_results


def _in_proj_bwd(x, g1, dx1, dq_f, dq_b, dk_f, dk_b, dv_f, dv_b, dpg, dpu, dpv, dlr_f, dlr_b, w_main, tm, comms=()):
    t = x.shape[0]

    def body(x_ref, g_ref, dx1_ref, dqf, dqb, dkf, dkb, dvf, dvb, dg, du, dv, dlf, dlb, w_ref,
             dx_ref, dp_ref, dg1_ref):
        @pl.when(pl.program_id(0) == 0)
        def _():
            dg1_ref[...] = jnp.zeros_like(dg1_ref)

        both = lambda a, b: (a[...].astype(F32) + b[...].astype(F32)).astype(BF16)
        dp = jnp.concatenate([both(dqf, dqb), both(dkf, dkb), both(dvf, dvb), dg[...], du[...], dv[...],
                              both(dlf, dlb)], axis=1)
        dp_ref[...] = dp
        dh = sum(_mm(dp[:, c0:c0 + r1 - r0], w_ref[r0:r1, :]) for r0, r1, c0 in PROJ_ROWS)
        xv = x_ref[...]
        r = lax.rsqrt(jnp.mean(xv * xv, axis=-1, keepdims=True) + EPS)
        xn = xv * r
        dg1_ref[...] += jnp.sum(dh * xn, axis=0, keepdims=True)
        dx_ref[...] = dx1_ref[...] + _rms_bwd(dh * g_ref[...], xn, r)

    row = lambda w: pl.BlockSpec((tm, w), lambda i: (i, 0))
    return _fused_call(
        body, comms, name="in_proj_bwd", grid=(t // tm,),
        inputs=(x, g1, dx1, dq_f, dq_b, dk_f, dk_b, dv_f, dv_b, dpg, dpu, dpv, dlr_f, dlr_b, w_main),
        in_specs=[row(D_MODEL), _const_spec((1, D_MODEL)), row(D_MODEL), row(KEY_W), row(KEY_W), row(KEY_W),
                  row(KEY_W), row(VAL_W), row(VAL_W), row(VAL_W), row(VAL_W), row(VAL_W), row(LANE), row(LANE),
                  _const_spec((PROJ_W, D_MODEL))],
        out_specs=(row(D_MODEL), row(PROJ_PAD), _acc_spec((1, D_MODEL))),
        out_shape=(jax.ShapeDtypeStruct((t, D_MODEL), F32), jax.ShapeDtypeStruct((t, PROJ_PAD), BF16),
                   jax.ShapeDtypeStruct((1, D_MODEL), F32)))


def _adamw(w, g, m, v):
    m_new = ADAM_B1 * m + (1.0 - ADAM_B1) * g
    v_new = ADAM_B2 * v + (1.0 - ADAM_B2) * (g * g)
    m_hat = m_new / (1.0 - ADAM_B1 ** ADAM_STEP)
    v_hat = v_new / (1.0 - ADAM_B2 ** ADAM_STEP)
    delta = -ADAM_LR * (m_hat / (jnp.sqrt(v_hat) + ADAM_EPS) + ADAM_WD * w)
    return delta, m_new, v_new


def _adamw_shard(own, recv, w, m, v, tr, name):
    r, c = w.shape

    def body(own_ref, recv_ref, w_ref, m_ref, v_ref, g_ref, d_ref, nm_ref, nv_ref):
        g = own_ref[...]
        for k in range(3):
            g = g + recv_ref[k].astype(F32)
        g_ref[...] = g
        d_ref[...], nm_ref[...], nv_ref[...] = _adamw(w_ref[...], g, m_ref[...], v_ref[...])

    row = pl.BlockSpec((tr, c), lambda i: (i, 0))
    return pl.pallas_call(
        body, name=name, grid=(r // tr,),
        in_specs=[row, pl.BlockSpec((3, tr, c), lambda i: (0, i, 0)), row, row, row],
        out_specs=(row,) * 4, out_shape=(jax.ShapeDtypeStruct((r, c), F32),) * 4,
        compiler_params=_params(),
    )(own, recv, w, m, v)


def _adamw_small(entries):
    stacks = []
    for (g, _, _), _, _, _ in entries:
        if not any(g is s for s in stacks):
            stacks.append(g)
    where = [next(i for i, s in enumerate(stacks) if s is g) for (g, _, _), _, _, _ in entries]
    ns, ne = len(stacks), len(entries)

    def body(*refs):
        s_refs, wmv, outs = refs[:ns], refs[ns:ns + 3 * ne], refs[ns + 3 * ne:]
        for e, ((_, r0, nr), _, _, _) in enumerate(entries):
            grad = s_refs[where[e]][r0:r0 + nr, :]
            w_ref, m_ref, v_ref = wmv[3 * e:3 * e + 3]
            g_ref, d_ref, nm_ref, nv_ref = outs[4 * e:4 * e + 4]
            g_ref[...] = grad
            d_ref[...], nm_ref[...], nv_ref[...] = _adamw(w_ref[...], grad, m_ref[...], v_ref[...])

    results = pl.pallas_call(
        body, name="adamw_small",
        out_shape=tuple(jax.ShapeDtypeStruct(w.shape, F32) for _, w, _, _ in entries for _ in range(4)),
        compiler_params=pltpu.CompilerParams(vmem_limit_bytes=VMEM_LIMIT),
    )(*stacks, *[a for _, w, m, v in entries for a in (w, m, v)])
    return [results[4 * e:4 * e + 4] for e in range(ne)]


def _mesh_pos():
    return lax.axis_index("x"), lax.axis_index("y"), lax.axis_index("c")


def _other_chips(x, y):
    return [(x, 1 - y), (1 - x, y), (1 - x, 1 - y)]


_VMEM_WHOLE = pl.BlockSpec(memory_space=pltpu.VMEM)
_HBM_WHOLE = pl.BlockSpec(memory_space=pl.ANY)


def _gather_comm(shards, cast, mid=(1, 2)):
    na = len(shards)
    staged = [a for a in range(na) if cast[a]]

    def phases(in_refs, out_refs, scr):
        stage = dict(zip(staged, scr[:len(staged)]))
        send_sems, recv_sems, local_sems = scr[len(staged):]
        x, y, c = _mesh_pos()
        me, sibling = (x, y, c), (x, y, 1 - c)
        chips = _other_chips(x, y)
        srcs = [stage[a] if cast[a] else in_refs[a] for a in range(na)]

        def rows(a, pos):
            px, py, pc = pos
            return out_refs[a].at[4 * px + 2 * py + pc]

        def copy(a, k, block, to, src=None):
            return pltpu.make_async_remote_copy(
                src_ref=rows(a, block) if src is None else src, dst_ref=rows(a, block),
                send_sem=send_sems.at[a, k], recv_sem=recv_sems.at[a, k], device_id=to, device_id_type=MESH_ID)

        mine = [pltpu.make_async_copy(srcs[a], rows(a, me), local_sems.at[a]) for a in range(na)]
        first = []
        for a in range(na):
            first.append(copy(a, 0, me, sibling, src=srcs[a]))
            first += [copy(a, 1 + j, me, (*chip, c), src=srcs[a]) for j, chip in enumerate(chips)]
        passed = [copy(a, 4 + j, (*chip, c), sibling) for j, chip in enumerate(chips) for a in range(na)]

        def start():
            for a in staged:
                stage[a][...] = in_refs[a][...].astype(BF16)
            for cp in mine + first:
                cp.start()

        def forward():
            i = 0
            for j, chip in enumerate(chips):
                for a in range(na):
                    copy(a, 1 + j, (*chip, c), me).wait_recv()
                    passed[i].start()
                    i += 1

        def finish():
            for a in range(na):
                copy(a, 0, sibling, me).wait_recv()
                for j, chip in enumerate(chips):
                    copy(a, 4 + j, (*chip, 1 - c), me).wait_recv()
            for cp in first + passed:
                cp.wait_send()
            for cp in mine:
                cp.wait()

        return start, forward, finish

    def before(step, nsteps, in_refs, out_refs, scr):
        start, forward, _ = phases(in_refs, out_refs, scr)
        pl.when(step == 0)(start)
        pl.when(step == nsteps * mid[0] // mid[1])(forward)

    def after(step, nsteps, in_refs, out_refs, scr):
        pl.when(step == nsteps - 1)(phases(in_refs, out_refs, scr)[2])

    return _Comm(
        inputs=list(shards), in_specs=[_VMEM_WHOLE] * na,
        out_shape=[jax.ShapeDtypeStruct((N_DEV,) + s.shape, BF16 if cast[a] else s.dtype)
                   for a, s in enumerate(shards)],
        out_specs=[_HBM_WHOLE] * na,
        scratch_shapes=[pltpu.VMEM(shards[a].shape, BF16) for a in staged] + [
            pltpu.SemaphoreType.DMA((na, 7)), pltpu.SemaphoreType.DMA((na, 7)), pltpu.SemaphoreType.DMA((na,))],
        before=before, after=after)


def _exchange_comm(arrays, out_shape, make_copies):
    na = len(arrays)

    def copies(in_refs, out_refs, scr):
        return make_copies(in_refs, out_refs, *scr)

    def before(step, nsteps, in_refs, out_refs, scr):
        @pl.when(step == 0)
        def _():
            for cp in copies(in_refs, out_refs, scr):
                cp.start()

    def after(step, nsteps, in_refs, out_refs, scr):
        @pl.when(step == nsteps - 1)
        def _():
            for cp in copies(in_refs, out_refs, scr):
                cp.wait()

    return _Comm(inputs=list(arrays), in_specs=[_HBM_WHOLE] * na, out_shape=list(out_shape),
                 out_specs=[_HBM_WHOLE] * na,
                 scratch_shapes=[pltpu.SemaphoreType.DMA((na, 3)), pltpu.SemaphoreType.DMA((na, 3))],
                 before=before, after=after)


def _sibling_exchange_comm(grads):
    def make_copies(in_refs, out_refs, send_sems, recv_sems):
        x, y, c = _mesh_pos()
        return [pltpu.make_async_remote_copy(
            src_ref=in_refs[a].at[:, pl.ds(1 - c, 1)], dst_ref=out_refs[a], send_sem=send_sems.at[a, 0],
            recv_sem=recv_sems.at[a, 0], device_id=(x, y, 1 - c), device_id_type=MESH_ID)
            for a in range(len(grads))]

    return _exchange_comm(grads, [jax.ShapeDtypeStruct((4, 1) + g.shape[2:], F32) for g in grads], make_copies)


def _chips_exchange_comm(partials):
    def make_copies(in_refs, out_refs, send_sems, recv_sems):
        x, y, c = _mesh_pos()
        return [pltpu.make_async_remote_copy(
            src_ref=in_refs[a].at[j], dst_ref=out_refs[a].at[j], send_sem=send_sems.at[a, j],
            recv_sem=recv_sems.at[a, j], device_id=(*chip, c), device_id_type=MESH_ID)
            for a in range(len(partials)) for j, chip in enumerate(_other_chips(x, y))]

    return _exchange_comm(partials, [jax.ShapeDtypeStruct(g.shape, BF16) for g in partials], make_copies)


def _comm_only(comms, name):
    return _fused_call(lambda: None, comms, name=name, grid=(1,), inputs=(), in_specs=[], out_specs=(),
                       out_shape=())[1]


def _chip_sum(my_pos, mine, from_sibling, tr, name):
    _, _, r, c = mine.shape

    def body(pos_ref, a_ref, b_ref, own_ref, out_ref):
        s = a_ref[0, 0] + b_ref[0, 0]

        @pl.when(pl.program_id(1) == 0)
        def _():
            own_ref[...] = s

        @pl.when(pl.program_id(1) > 0)
        def _():
            out_ref[0] = s.astype(BF16)

    grid_spec = pltpu.PrefetchScalarGridSpec(
        num_scalar_prefetch=1, grid=(r // tr, 4),
        in_specs=[pl.BlockSpec((1, 1, tr, c), lambda i, k, pos: (pos[0] ^ k, pos[1], i, 0)),
                  pl.BlockSpec((1, 1, tr, c), lambda i, k, pos: (pos[0] ^ k, 0, i, 0))],
        out_specs=(pl.BlockSpec((tr, c), lambda i, k, pos: (i, 0)),
                   pl.BlockSpec((1, tr, c), lambda i, k, pos: (jnp.maximum(k - 1, 0), i, 0))))
    return pl.pallas_call(
        body, name=name, grid_spec=grid_spec,
        out_shape=(jax.ShapeDtypeStruct((r, c), F32), jax.ShapeDtypeStruct((3, r, c), BF16)),
        compiler_params=_params(2),
    )(my_pos, mine, from_sibling)


def _all_reduce_small_comm(parts):
    na = len(parts)

    def copies(in_refs, scr):
        gathered, (send_sems, recv_sems) = scr[:na], scr[na:]
        x, y, c = _mesh_pos()
        my_id = 4 * x + 2 * y + c
        return my_id, [pltpu.make_async_remote_copy(
            src_ref=in_refs[a], dst_ref=gathered[a].at[my_id], send_sem=send_sems.at[a, k - 1],
            recv_sem=recv_sems.at[a, k - 1], device_id=(x ^ (k >> 2), y ^ ((k >> 1) & 1), c ^ (k & 1)),
            device_id_type=MESH_ID) for a in range(na) for k in range(1, N_DEV)]

    def before(step, nsteps, in_refs, out_refs, scr):
        @pl.when(step == 0)
        def _():
            for cp in copies(in_refs, scr)[1]:
                cp.start()

    def after(step, nsteps, in_refs, out_refs, scr):
        @pl.when(step == nsteps - 1)
        def _():
            my_id, cps = copies(in_refs, scr)
            for a in range(na):
                scr[a][my_id] = in_refs[a][...]
            for cp in cps:
                cp.wait()
            for a in range(na):
                acc = scr[a][0]
                for d in range(1, N_DEV):
                    acc = acc + scr[a][d]
                out_refs[a][...] = acc

    return _Comm(inputs=list(parts), in_specs=[_VMEM_WHOLE] * na,
                 out_shape=[jax.ShapeDtypeStruct(p.shape, F32) for p in parts], out_specs=[_VMEM_WHOLE] * na,
                 scratch_shapes=[pltpu.VMEM((N_DEV,) + p.shape, F32) for p in parts] + [
                     pltpu.SemaphoreType.DMA((na, N_DEV - 1)), pltpu.SemaphoreType.DMA((na, N_DEV - 1))],
                 before=before, after=after)


def _unshard_cols(g):
    return jnp.transpose(g, (1, 0, 2)).reshape(g.shape[1], N_DEV * g.shape[2])


def _row_blocks(w):
    return w.reshape(4, 2, w.shape[0] // N_DEV, w.shape[1])


def _stack_rows(parts):
    a = jnp.concatenate(parts, axis=0)
    return jnp.pad(a, ((0, (-a.shape[0]) % 8), (0, 0)))


def _main_proj_grad(dw_main_t):
    return jnp.concatenate([dw_main_t[:LR_REF], dw_main_t[LR_COL:LR_COL + 2 * LOWRANK], dw_main_t[LR_REF:LR_COL]],
                           axis=0)


def _padded_decay_weights(wd_f, wd_b):
    zeros = lambda n: jnp.zeros((n, KEY_W), F32)
    return (jnp.concatenate([wd_f, zeros(LANE - LOWRANK)], axis=0),
            jnp.concatenate([zeros(LOWRANK), wd_b, zeros(LANE - 2 * LOWRANK)], axis=0))


def kernel(x, norm1_g, w_in,w_decay_f, b_decay_f, w_decay_b, b_decay_b, gla_norm_g, gmlp_ln_g, gmlp_ln_b, w_spatial, b_spatial, w_out, norm2_g, w_gate, w_up, w_down, final_norm_g, loss_target, m_norm1_g, m_w_in, m_w_decay_f, m_b_decay_f, m_w_decay_b, m_b_decay_b, m_gla_norm_g, m_gmlp_ln_g, m_gmlp_ln_b, m_w_spatial, m_b_spatial, m_w_out, m_norm2_g, m_w_gate, m_w_up, m_w_down, m_final_norm_g, v_norm1_g, v_w_in, v_w_decay_f, v_b_decay_f, v_w_decay_b, v_b_decay_b, v_gla_norm_g, v_gmlp_ln_g, v_gmlp_ln_b, v_w_spatial, v_b_spatial, v_w_out, v_norm2_g, v_w_gate, v_w_up, v_w_down, v_final_norm_g):
    t = x.shape[1]
    xt = x[0]
    target = loss_target[0]
    pos_x, pos_y, pos_c = _mesh_pos()
    my_pos = jnp.stack([2 * pos_x + pos_y, pos_c]).astype(jnp.int32)
    my_id = 4 * pos_x + 2 * pos_y + pos_c

    tile = lambda n: min(n, t)
    ln_g, ln_b, w_sp = gmlp_ln_g, gmlp_ln_b, w_spatial[0]
    b_sp_col = b_spatial[0][:, :, None]
    shard = {"w_in": w_in[0].T, "w_out": w_out[0], "w_gate": w_gate[0].T, "w_up": w_up[0].T, "w_down": w_down[0]}
    shard_m = {"w_in": m_w_in[0].T, "w_out": m_w_out[0], "w_gate": m_w_gate[0].T, "w_up": m_w_up[0].T,
               "w_down": m_w_down[0]}
    shard_v = {"w_in": v_w_in[0].T, "w_out": v_w_out[0], "w_gate": v_w_gate[0].T, "w_up": v_w_up[0].T,
               "w_down": v_w_down[0]}
    transposed = ("w_in", "w_gate", "w_up")
    chip_sum = lambda n, g, s: _chip_sum(my_pos, g, s[0], g.shape[2], "chip_sum_" + n)

    decay_shard = jnp.stack([w_decay_f[0], w_decay_b[0]])
    (hb,), ((g_in, g_decay),) = _norm1(xt, norm1_g, tile(512),
                                       [_gather_comm([shard["w_in"], decay_shard], [True, False])])
    w_in_t = g_in.reshape(PROJ_W, D_MODEL)
    wd_pad_f, wd_pad_b = _padded_decay_weights(_unshard_cols(g_decay[:, 0]), _unshard_cols(g_decay[:, 1]))
    (p,), ((g_gate,),) = _in_proj(hb, w_in_t, tile(1024), [_gather_comm([shard["w_gate"]], [True])])
    (o_f, st_f, o_b, st_b), ((g_up, g_out, g_down),) = _gla_fwd(
        p, wd_pad_f, b_decay_f, wd_pad_b, b_decay_b, tile(512),
        [_gather_comm([shard["w_up"], shard["w_out"], shard["w_down"]], [True, True, True])])
    w_out_full = g_out.reshape(D_MODEL, D_MODEL)
    x1, ycat = _mix_fwd(xt, o_f, o_b, p, gla_norm_g, ln_g, ln_b, w_sp, b_sp_col, w_out_full, tile(512))

    dx1, h2b, dgate, dup, act, dx2, loss_acc, d_gf, d_g2 = _ffn(
        x1, target, norm2_g, final_norm_g[None, :], g_gate.reshape(D_FF, D_MODEL), g_up.reshape(D_FF, D_MODEL),
        g_down.reshape(D_FF, D_MODEL), tile(256))
    dw_gate, _ = _matmul_tn(dgate, h2b, D_FF // 2, tile(2048), "grad_w_gate")
    dw_up, _ = _matmul_tn(dup, h2b, D_FF // 2, tile(2048), "grad_w_up")
    dw_down, _ = _matmul_tn(act, dx2, D_FF // 2, tile(2048), "grad_w_down")

    ffn_grads = [_row_blocks(dw_gate), _row_blocks(dw_up), _row_blocks(dw_down)]
    (d_o, dpg, dpu, dpv, dw_out, d_gg, d_lg, d_lb, dw_sp, db_sp), (ffn_sib,) = _mix_bwd(
        dx1, ycat, o_f, o_b, p, gla_norm_g, ln_g, ln_b, w_sp, b_sp_col, w_out_full, tile(256),
        [_sibling_exchange_comm(ffn_grads)])
    ffn_names = ["w_gate", "w_up", "w_down"]
    ffn_sums = [chip_sum(n, g, [s]) for n, g, s in zip(ffn_names, ffn_grads, ffn_sib)]
    out_grad = _row_blocks(dw_out)
    (dq_f, dk_f, dv_f, dlr_f, dwd_f, dbd_f, dq_b, dk_b, dv_b, dlr_b, dwd_b, dbd_b), (ffn_recv, out_sib) = _gla_bwd(
        p, wd_pad_f, b_decay_f, wd_pad_b, b_decay_b, st_f, st_b, d_o, tile(512),
        [_chips_exchange_comm([s[1] for s in ffn_sums]), _sibling_exchange_comm([out_grad])])
    out_sum = chip_sum("w_out", out_grad, out_sib)
    (grad_x, dp, d_g1), _ = _in_proj_bwd(
        xt, norm1_g, dx1, dq_f, dq_b, dk_f, dk_b, dv_f, dv_b, dpg, dpu, dpv, dlr_f, dlr_b, w_in_t, tile(512))

    stacks = [_stack_rows([d_g1, d_g2, d_gf]), _stack_rows([d_gg, d_lg, d_lb]),
              _stack_rows([dbd_f, dbd_b, jnp.zeros((6, KEY_W), F32), dwd_f[:LOWRANK], dwd_b[LOWRANK:2 * LOWRANK]]),
              _stack_rows([dw_sp.reshape(GMLP_W, GMLP_CHUNK), db_sp[:, :, 0], loss_acc[:1]])]
    dw_main, (small_sums, out_recv) = _matmul_tn(
        dp, hb, PROJ_PAD // 3, tile(2048), "grad_w_in",
        [_all_reduce_small_comm(stacks), _chips_exchange_comm([out_sum[1]])])
    in_grad = _row_blocks(_main_proj_grad(dw_main))
    (in_sib,) = _comm_only([_sibling_exchange_comm([in_grad])], "grad_w_in_exchange_sibling")
    in_sum = chip_sum("w_in", in_grad, in_sib)
    (in_recv,) = _comm_only([_chips_exchange_comm([in_sum[1]])], "grad_w_in_exchange_chips")

    names = ["w_in", "w_out", "w_gate", "w_up", "w_down"]
    sums = [in_sum, out_sum] + ffn_sums
    received = [in_recv[0], out_recv[0]] + list(ffn_recv)
    big_out = {}
    for n, s, rc in zip(names, sums, received):
        res = _adamw_shard(s[0], rc, shard[n], shard_m[n], shard_v[n], shard[n].shape[0], "adamw_" + n)
        big_out[n] = [r.T if n in transposed else r for r in res]

    s1024, s512, s256, s128 = small_sums
    loss = s128[GMLP_W + GMLP_GROUPS, 0]
    col0 = my_id * (KEY_W // N_DEV)
    decay_cols = lambda row0: lax.dynamic_slice(s256, (row0, col0), (LOWRANK, KEY_W // N_DEV))
    flat = lambda a: a.reshape(-1, a.shape[-1])
    small = {
        "norm1_g": ((s1024, 0, 1), norm1_g, m_norm1_g, v_norm1_g),
        "w_decay_f": ((decay_cols(8), 0, LOWRANK), w_decay_f, m_w_decay_f, v_w_decay_f),
        "b_decay_f": ((s256, 0, 1), b_decay_f, m_b_decay_f, v_b_decay_f),
        "w_decay_b": ((decay_cols(8 + LOWRANK), 0, LOWRANK), w_decay_b, m_w_decay_b, v_w_decay_b),
        "b_decay_b": ((s256, 1, 1), b_decay_b, m_b_decay_b, v_b_decay_b),
        "gla_norm_g": ((s512, 0, 1), gla_norm_g, m_gla_norm_g, v_gla_norm_g),
        "gmlp_ln_g": ((s512, 1, 1), gmlp_ln_g, m_gmlp_ln_g, v_gmlp_ln_g),
        "gmlp_ln_b": ((s512, 2, 1), gmlp_ln_b, m_gmlp_ln_b, v_gmlp_ln_b),
        "w_spatial": ((s128, 0, GMLP_W), w_spatial, m_w_spatial, v_w_spatial),
        "b_spatial": ((s128, GMLP_W, GMLP_GROUPS), b_spatial, m_b_spatial, v_b_spatial),
        "norm2_g": ((s1024, 1, 1), norm2_g, m_norm2_g, v_norm2_g),
        "final_norm_g": ((s1024, 2, 1), final_norm_g, m_final_norm_g, v_final_norm_g),
    }
    small_res = _adamw_small([(g, flat(w), flat(m), flat(v)) for g, w, m, v in small.values()])
    small_out = {n: [r.reshape(small[n][1].shape) for r in res] for n, res in zip(small, small_res)}

    order = ["norm1_g", "w_in", "w_decay_f", "b_decay_f", "w_decay_b", "b_decay_b", "gla_norm_g", "gmlp_ln_g",
             "gmlp_ln_b", "w_spatial", "b_spatial", "w_out", "norm2_g", "w_gate", "w_up", "w_down", "final_norm_g"]
    outs = []
    for kind in range(4):
        for n in order:
            outs.append(big_out[n][kind][None] if n in big_out else small_out[n][kind])
    return (loss, grad_x[None], *outs)
```

```python
import functools
import math

import jax
import jax.numpy as jnp
from jax import lax
from jax.experimental import pallas as pl
from jax.experimental.pallas import tpu as pltpu

F32 = jnp.float32
BF16 = jnp.bfloat16

D_MODEL = 1024
GLA_HEADS = 4
GLA_DK = 64
GLA_DV = 128
KEY_W = GLA_HEADS * GLA_DK
VAL_W = GLA_HEADS * GLA_DV
LOWRANK = 16
GLA_TAU = 16.0
GLA_CHUNK = 64
GMLP_W = 512
GMLP_GROUPS = 4
GMLP_CHUNK = 128
D_FF = 2816
EPS = 1e-6
Q_SCALE = GLA_DK ** -0.5
PROJ_PAD = 2688
LR_COL = 2560
LANE = 128
N_DEV = 8

ADAM_LR = 0.001
ADAM_B1 = 0.9
ADAM_B2 = 0.999
ADAM_EPS = 1e-08
ADAM_WD = 0.01
ADAM_STEP = 10

VMEM_LIMIT = 56 * 1024 * 1024
MESH_ID = pl.DeviceIdType.MESH
INV_SQRT2 = 0.7071067811865476
INV_SQRT_2PI = 0.3989422804014327


def _params(n_axes=1):
    return pltpu.CompilerParams(dimension_semantics=("arbitrary",) * n_axes, vmem_limit_bytes=VMEM_LIMIT)


def _mm(a, b):
    return jnp.dot(a.astype(BF16), b.astype(BF16), preferred_element_type=F32)


def _mm_nt(a, b):
    return lax.dot_general(a.astype(BF16), b.astype(BF16), (((1,), (1,)), ((), ())), preferred_element_type=F32)


def _mm_tn(a, b):
    return lax.dot_general(a.astype(BF16), b.astype(BF16), (((0,), (0,)), ((), ())), preferred_element_type=F32)


def _const_spec(shape):
    nd = len(shape)
    return pl.BlockSpec(shape, lambda *_: (0,) * nd, pipeline_mode=pl.Buffered(1))


def _acc_spec(shape):
    nd = len(shape)
    return pl.BlockSpec(shape, lambda *_: (0,) * nd)


class _Comm:
    def __init__(self, inputs, in_specs, out_shape, out_specs, scratch_shapes, before, after):
        self.inputs, self.in_specs, self.out_shape, self.out_specs = inputs, in_specs, out_shape, out_specs
        self.scratch_shapes, self.before, self.after = scratch_shapes, before, after


def _fused_call(body, comms, *, name, grid, inputs, in_specs, out_specs, out_shape, scratch_shapes=()):
    n_in, n_out, n_scr = len(in_specs), len(out_specs), len(scratch_shapes)
    nsteps = math.prod(grid)
    sizes = [(len(c.inputs), len(c.out_shape), len(c.scratch_shapes)) for c in comms]

    def full_body(*refs):
        step = pl.program_id(0)
        for axis in range(1, len(grid)):
            step = step * grid[axis] + pl.program_id(axis)
        ins, rest = refs[:n_in], refs[n_in:]
        c_ins = []
        for ci, _, _ in sizes:
            c_ins.append(rest[:ci])
            rest = rest[ci:]
        outs, rest = rest[:n_out], rest[n_out:]
        c_outs = []
        for _, co, _ in sizes:
            c_outs.append(rest[:co])
            rest = rest[co:]
        scr, rest = rest[:n_scr], rest[n_scr:]
        c_scr = []
        for _, _, cs in sizes:
            c_scr.append(rest[:cs])
            rest = rest[cs:]
        for c, a, b, s in zip(comms, c_ins, c_outs, c_scr):
            c.before(step, nsteps, a, b, s)
        body(*ins, *outs, *scr)
        for c, a, b, s in zip(comms, c_ins, c_outs, c_scr):
            c.after(step, nsteps, a, b, s)

    results = pl.pallas_call(
        full_body, name=name, grid=grid,
        in_specs=list(in_specs) + [s for c in comms for s in c.in_specs],
        out_specs=tuple(out_specs) + tuple(s for c in comms for s in c.out_specs),
        out_shape=tuple(out_shape) + tuple(s for c in comms for s in c.out_shape),
        scratch_shapes=list(scratch_shapes) + [s for c in comms for s in c.scratch_shapes],
        compiler_params=_params(len(grid)),
    )(*inputs, *[a for c in comms for a in c.inputs])
    own, rest = results[:n_out], results[n_out:]
    comm_results = []
    for _, co, _ in sizes:
        comm_results.append(rest[:co])
        rest = rest[co:]
    return own, comm_results


def _gelu(x):
    return 0.5 * x * (1.0 + lax.erf(x * INV_SQRT2))


def _gelu_and_grad(x):
    cdf = 0.5 * (1.0 + lax.erf(x * INV_SQRT2))
    return x * cdf, cdf + x * jnp.exp(-0.5 * x * x) * INV_SQRT_2PI


def _silu_and_grad(x):
    s = jax.nn.sigmoid(x)
    return x * s, s * (1.0 + x * (1.0 - s))


def _norm1(x, g1, tm, comms=()):
    t = x.shape[0]

    def body(x_ref, g_ref, h_ref):
        xv = x_ref[...]
        r = lax.rsqrt(jnp.mean(xv * xv, axis=-1, keepdims=True) + EPS)
        h_ref[...] = (xv * r * g_ref[...]).astype(BF16)

    row = pl.BlockSpec((tm, D_MODEL), lambda i: (i, 0))
    return _fused_call(body, comms, name="norm1", grid=(t // tm,), inputs=(x, g1),
                       in_specs=[row, _const_spec((1, D_MODEL))], out_specs=(row,),
                       out_shape=(jax.ShapeDtypeStruct((t, D_MODEL), BF16),))


PROJ_W = 2592
LR_REF = 1536
PROJ_ROWS = ((0, LR_REF, 0), (LR_REF + 2 * LOWRANK, PROJ_W, LR_REF), (LR_REF, LR_REF + LANE, LR_COL))


def _in_proj(h, w_in_t, tm, comms=()):
    t = h.shape[0]

    def body(h_ref, w_ref, p_ref):
        hv = h_ref[...]
        for r0, r1, c0 in PROJ_ROWS:
            p_ref[:, c0:c0 + r1 - r0] = _mm_nt(hv, w_ref[r0:r1, :]).astype(BF16)

    return _fused_call(
        body, comms, name="in_proj", grid=(t // tm,), inputs=(h, w_in_t),
        in_specs=[pl.BlockSpec((tm, D_MODEL), lambda i: (i, 0)), _const_spec((PROJ_W, D_MODEL))],
        out_specs=(pl.BlockSpec((tm, PROJ_PAD), lambda i: (i, 0)),),
        out_shape=(jax.ShapeDtypeStruct((t, PROJ_PAD), BF16),))


def _tri(upper):
    r = lax.broadcasted_iota(jnp.int32, (GLA_CHUNK, GLA_CHUNK), 0)
    c = lax.broadcasted_iota(jnp.int32, (GLA_CHUNK, GLA_CHUNK), 1)
    return jnp.where((c >= r) if upper else (c <= r), 1.0, 0.0).astype(BF16)


def _chunk_cumsum(tri, a):
    hi = a.astype(BF16)
    lo = (a - hi.astype(F32)).astype(BF16)
    dot = functools.partial(jnp.dot, preferred_element_type=F32)
    return jnp.concatenate([dot(tri, hi[_chunk_rows(c)]) + dot(tri, lo[_chunk_rows(c)])
                            for c in range(a.shape[0] // GLA_CHUNK)], axis=0)


def _chunk_rows(c):
    return slice(c * GLA_CHUNK, (c + 1) * GLA_CHUNK)


def _gla_masks(rev):
    dk_bits, dv_bits = GLA_DK.bit_length() - 1, GLA_DV.bit_length() - 1
    key_head = lax.broadcasted_iota(jnp.int32, (GLA_CHUNK, KEY_W), 1) >> dk_bits
    val_head = lax.broadcasted_iota(jnp.int32, (GLA_CHUNK, VAL_W), 1) >> dv_bits
    t = lax.broadcasted_iota(jnp.int32, (GLA_HEADS * GLA_CHUNK, GLA_CHUNK), 0) & (GLA_CHUNK - 1)
    s = lax.broadcasted_iota(jnp.int32, (GLA_HEADS * GLA_CHUNK, GLA_CHUNK), 1)
    causal = (s >= t) if rev else (s <= t)
    state_head = lax.broadcasted_iota(jnp.int32, (GLA_DV, KEY_W), 1) >> dk_bits
    return key_head, val_head, causal, state_head


def _stack_heads(a, head_of_lane):
    a = a.astype(BF16)
    return jnp.concatenate([jnp.where(head_of_lane == h, a, jnp.zeros_like(a)) for h in range(GLA_HEADS)], axis=0)


def _rows_by_head(a):
    return jnp.concatenate([a[:, h * GLA_DV:(h + 1) * GLA_DV] for h in range(GLA_HEADS)], axis=0)


def _lanes_by_head(r):
    return jnp.concatenate([r[h * GLA_CHUNK:(h + 1) * GLA_CHUNK] for h in range(GLA_HEADS)], axis=1)


def _head_diagonal(r, head_of_lane):
    rows = r.shape[0] // GLA_HEADS
    out = jnp.where(head_of_lane == 0, r[:rows], 0.0)
    for h in range(1, GLA_HEADS):
        out = out + jnp.where(head_of_lane == h, r[h * rows:(h + 1) * rows], 0.0)
    return out


def _tile_terms(la, q, k, tri, rev):
    nc = la.shape[0] // GLA_CHUNK
    q, k = q.astype(F32), k.astype(F32)
    b = _chunk_cumsum(tri, la)
    ebl = [jnp.exp(b[c * GLA_CHUNK:c * GLA_CHUNK + 1] if rev else b[(c + 1) * GLA_CHUNK - 1:(c + 1) * GLA_CHUNK])
           for c in range(nc)]
    eb = jnp.exp(b)
    enb = jnp.exp(-b)
    ee = enb * jnp.concatenate([jnp.broadcast_to(row, (GLA_CHUNK, KEY_W)) for row in ebl], axis=0)
    return ebl, eb, enb, ee, q * Q_SCALE * eb, k * enb, k * ee


def _log_decay(lr_ref, wd_ref, bd_ref):
    z = _mm(lr_ref[...], wd_ref[...]) + bd_ref[...]
    return z, jax.nn.log_sigmoid(z) * (1.0 / GLA_TAU)


def _p_specs(tg, tile):
    return [pl.BlockSpec((tg, KEY_W), lambda i: (tile(i), 0)),
            pl.BlockSpec((tg, KEY_W), lambda i: (tile(i), 1)),
            pl.BlockSpec((tg, VAL_W), lambda i: (tile(i), 1)),
            pl.BlockSpec((tg, LANE), lambda i: (tile(i), LR_COL // LANE))]


def _gla_fwd_dir(rev, nc, q_ref, k_ref, v_ref, lr_ref, wd_ref, bd_ref, o_ref, st_ref, state):
    key_head, _, causal, state_head = _gla_masks(rev)
    order = range(nc - 1, -1, -1) if rev else range(nc)

    def intra():
        _, la = _log_decay(lr_ref, wd_ref, bd_ref)
        ebl, _, _, _, qd, kd, ke = _tile_terms(la, q_ref[...], k_ref[...], _tri(rev), rev)
        kd = kd.astype(BF16)
        v = {c: v_ref[_chunk_rows(c), :].astype(BF16) for c in order}
        qd_stack = {c: _stack_heads(qd[_chunk_rows(c)], key_head) for c in order}
        ke_stack = {c: _stack_heads(ke[_chunk_rows(c)], key_head) for c in order}
        a_all = {c: _mm_nt(qd_stack[c], kd[_chunk_rows(c)]) for c in order}
        a_all = {c: jnp.where(causal, a_all[c], 0.0).astype(BF16) for c in order}
        r = {c: _mm(a_all[c], v[c]) for c in order}
        upd = {c: _mm_tn(_rows_by_head(v[c]), ke_stack[c]) for c in order}
        return {c: (ebl[c], qd_stack[c], r[c], upd[c]) for c in order}

    def scan(terms):
        st = state[...]
        states = {}
        for c in order:
            states[c] = st
            st_ref[c] = st.astype(BF16)
            st = st * terms[c][0] + terms[c][3]
        state[...] = st
        return states

    def inter(terms, states):
        r_inter = {c: _mm_nt(terms[c][1], states[c]) for c in order}
        for c in order:
            r = terms[c][2]
            o_ref[_chunk_rows(c), :] = jnp.concatenate(
                [r[h * GLA_CHUNK:(h + 1) * GLA_CHUNK, h * GLA_DV:(h + 1) * GLA_DV]
                 + r_inter[c][h * GLA_CHUNK:(h + 1) * GLA_CHUNK] for h in range(GLA_HEADS)], axis=1)

    return intra, scan, inter


def _gla_fwd(p, wd_pad_f, bd_f, wd_pad_b, bd_b, tg, comms=()):
    t = p.shape[0]
    nt = t // tg
    nc = tg // GLA_CHUNK
    up, down = (lambda i: i), (lambda i: nt - 1 - i)

    def body(qf, kf, vf, lrf, qb, kb, vb, lrb, wdf, bdf, wdb, bdb, of, stf, ob, stb, state_f, state_b):
        @pl.when(pl.program_id(0) == 0)
        def _():
            state_f[...] = jnp.zeros_like(state_f)
            state_b[...] = jnp.zeros_like(state_b)

        dirs = [_gla_fwd_dir(False, nc, qf, kf, vf, lrf, wdf, bdf, of, stf, state_f),
                _gla_fwd_dir(True, nc, qb, kb, vb, lrb, wdb, bdb, ob, stb, state_b)]
        terms = [intra() for intra, _, _ in dirs]
        states = [scan(t) for (_, scan, _), t in zip(dirs, terms)]
        for (_, _, inter), t, s in zip(dirs, terms, states):
            inter(t, s)

    wd_spec, bd_spec = _const_spec((LANE, KEY_W)), _const_spec((1, KEY_W))
    outs = lambda tile: (pl.BlockSpec((tg, VAL_W), lambda i: (tile(i), 0)),
                         pl.BlockSpec((nc, GLA_DV, KEY_W), lambda i: (tile(i), 0, 0)))
    out_shape = (jax.ShapeDtypeStruct((t, VAL_W), F32), jax.ShapeDtypeStruct((t // GLA_CHUNK, GLA_DV, KEY_W), BF16))
    return _fused_call(
        body, comms, name="gla_fwd", grid=(nt,), inputs=(p,) * 8 + (wd_pad_f, bd_f, wd_pad_b, bd_b),
        in_specs=_p_specs(tg, up) + _p_specs(tg, down) + [wd_spec, bd_spec, wd_spec, bd_spec],
        out_specs=outs(up) + outs(down), out_shape=out_shape * 2,
        scratch_shapes=[pltpu.VMEM((GLA_DV, KEY_W), F32)] * 2)


def _gla_bwd_dir(rev, nc, q_ref, k_ref, v_ref, lr_ref, wd_ref, bd_ref, st_ref, do_ref,
                 dq_ref, dk_ref, dv_ref, dlr_ref, dwd_ref, dbd_ref, dstate):
    key_head, val_head, causal, state_head = _gla_masks(rev)
    order = range(nc) if rev else range(nc - 1, -1, -1)
    tg = nc * GLA_CHUNK

    def intra():
        z, la = _log_decay(lr_ref, wd_ref, bd_ref)
        tile = _tile_terms(la, q_ref[...], k_ref[...], _tri(rev), rev)
        qd, kd = tile[4], tile[5].astype(BF16)
        v = {c: v_ref[_chunk_rows(c), :].astype(BF16) for c in order}
        d_o = {c: do_ref[_chunk_rows(c), :] for c in order}
        kd_c = {c: kd[_chunk_rows(c)] for c in order}
        qd_stack = {c: _stack_heads(qd[_chunk_rows(c)], key_head) for c in order}
        do_stack = {c: _stack_heads(d_o[c], val_head) for c in order}
        do_rows = {c: _rows_by_head(d_o[c]) for c in order}
        a_all = {c: _mm_nt(qd_stack[c], kd_c[c]) for c in order}
        da_all = {c: _mm_nt(do_stack[c], v[c]) for c in order}
        a_all = {c: jnp.where(causal, a_all[c], 0.0).astype(BF16) for c in order}
        da_all = {c: jnp.where(causal, da_all[c], 0.0).astype(BF16) for c in order}
        dv = {c: _mm_tn(a_all[c], do_stack[c]) for c in order}
        dqd = {c: _mm(da_all[c], kd_c[c]) + _mm(do_rows[c], st_ref[c]) for c in order}
        dkd = {c: _mm_tn(da_all[c], qd_stack[c]) for c in order}
        upd = {c: _mm_tn(do_rows[c], qd_stack[c]) for c in order}
        dqd = {c: _head_diagonal(dqd[c], key_head) for c in order}
        return z, tile, {c: dict(dv=dv[c], dqd=dqd[c], dkd=dkd[c], upd=upd[c]) for c in order}

    def scan(tile, per):
        dst = dstate[...]
        dsts = {}
        for c in order:
            dsts[c] = dst
            dst = dst * tile[0][c] + per[c]["upd"]
        dstate[...] = dst
        return dsts

    def inter(z, tile, per, dsts):
        ebl, eb, enb, ee, qd, kd, ke = tile
        ke_stack = {c: _stack_heads(ke[_chunk_rows(c)], key_head) for c in order}
        v_rows = {c: _rows_by_head(v_ref[_chunk_rows(c), :].astype(BF16)) for c in order}
        dst_b = {c: dsts[c].astype(BF16) for c in order}
        dv_state = {c: _mm_nt(ke_stack[c], dst_b[c]) for c in order}
        dke_c = {c: _mm(v_rows[c], dst_b[c]) for c in order}
        dke_c = {c: _head_diagonal(dke_c[c], key_head) for c in order}
        dbl_c = {}
        for c in order:
            rows = _chunk_rows(c)
            dv_ref[rows, :] = (per[c]["dv"] + _lanes_by_head(dv_state[c])).astype(BF16)
            dbl = (jnp.sum(dsts[c] * st_ref[c].astype(F32), axis=0, keepdims=True) * ebl[c]
                   + jnp.sum(dke_c[c] * ke[rows], axis=0, keepdims=True))
            dbl_c[c] = jnp.broadcast_to(dbl, (GLA_CHUNK, KEY_W))
        tile_of = lambda parts: jnp.concatenate([parts[c] for c in range(nc)], axis=0)
        dqd, dkd = tile_of({c: per[c]["dqd"] for c in order}), tile_of({c: per[c]["dkd"] for c in order})
        dke, dbl = tile_of(dke_c), tile_of(dbl_c)
        dq_ref[...] = (dqd * eb * Q_SCALE).astype(BF16)
        dk_ref[...] = (dkd * enb + dke * ee).astype(BF16)
        db = dqd * qd - dkd * kd - dke * ke
        dz = (_chunk_cumsum(_tri(not rev), db) + dbl) * (jax.nn.sigmoid(-z) * (1.0 / GLA_TAU))
        dlr_ref[...] = _mm_nt(dz, wd_ref[...]).astype(BF16)
        dwd_ref[...] += _mm_tn(lr_ref[...], dz)
        dbd_ref[...] += jnp.sum(dz, axis=0, keepdims=True)

    return intra, scan, inter


def _gla_bwd(p, wd_pad_f, bd_f, wd_pad_b, bd_b, st_f, st_b, d_o, tg, comms=()):
    t = p.shape[0]
    nt = t // tg
    nc = tg // GLA_CHUNK
    up, down = (lambda i: i), (lambda i: nt - 1 - i)

    def body(qf, kf, vf, lrf, stf, dof, qb, kb, vb, lrb, stb, dob, wdf, bdf, wdb, bdb,
             dqf, dkf, dvf, dlrf, dwdf, dbdf, dqb, dkb, dvb, dlrb, dwdb, dbdb, dstate_f, dstate_b):
        @pl.when(pl.program_id(0) == 0)
        def _():
            for ref in (dstate_f, dstate_b, dwdf, dbdf, dwdb, dbdb):
                ref[...] = jnp.zeros_like(ref)

        dirs = [_gla_bwd_dir(False, nc, qf, kf, vf, lrf, wdf, bdf, stf, dof, dqf, dkf, dvf, dlrf, dwdf, dbdf,
                             dstate_f),
                _gla_bwd_dir(True, nc, qb, kb, vb, lrb, wdb, bdb, stb, dob, dqb, dkb, dvb, dlrb, dwdb, dbdb,
                             dstate_b)]
        first = [intra() for intra, _, _ in dirs]
        dsts = [scan(tile, per) for (_, scan, _), (_, tile, per) in zip(dirs, first)]
        for (_, _, inter), (z, tile, per), d in zip(dirs, first, dsts):
            inter(z, tile, per, d)

    wd_spec, bd_spec = _const_spec((LANE, KEY_W)), _const_spec((1, KEY_W))
    ins = lambda tile: _p_specs(tg, tile) + [pl.BlockSpec((nc, GLA_DV, KEY_W), lambda i: (tile(i), 0, 0)),
                                             pl.BlockSpec((tg, VAL_W), lambda i: (tile(i), 0))]
    outs = lambda tile: (pl.BlockSpec((tg, KEY_W), lambda i: (tile(i), 0)),
                         pl.BlockSpec((tg, KEY_W), lambda i: (tile(i), 0)),
                         pl.BlockSpec((tg, VAL_W), lambda i: (tile(i), 0)),
                         pl.BlockSpec((tg, LANE), lambda i: (tile(i), 0)),
                         _acc_spec((LANE, KEY_W)), _acc_spec((1, KEY_W)))
    out_shape = (jax.ShapeDtypeStruct((t, KEY_W), BF16), jax.ShapeDtypeStruct((t, KEY_W), BF16),
                 jax.ShapeDtypeStruct((t, VAL_W), BF16), jax.ShapeDtypeStruct((t, LANE), BF16),
                 jax.ShapeDtypeStruct((LANE, KEY_W), F32), jax.ShapeDtypeStruct((1, KEY_W), F32))
    scratch = [pltpu.VMEM((GLA_DV, KEY_W), F32)]
    return _fused_call(
        body, comms, name="gla_bwd", grid=(nt,),
        inputs=(p, p, p, p, st_f, d_o, p, p, p, p, st_b, d_o, wd_pad_f, bd_f, wd_pad_b, bd_b),
        in_specs=ins(down) + ins(up) + [wd_spec, bd_spec, wd_spec, bd_spec],
        out_specs=outs(down) + outs(up), out_shape=out_shape * 2, scratch_shapes=scratch * 2)


def _head_rms(o):
    parts, scales = [], []
    for h in range(GLA_HEADS):
        oh = o[:, h * GLA_DV:(h + 1) * GLA_DV]
        r = lax.rsqrt(jnp.mean(oh * oh, axis=-1, keepdims=True) + EPS)
        parts.append(oh * r)
        scales.append(jnp.broadcast_to(r, oh.shape))
    return jnp.concatenate(parts, axis=1), jnp.concatenate(scales, axis=1)


def _layernorm_stats(zv):
    mu = jnp.mean(zv, axis=-1, keepdims=True)
    xc = zv - mu
    rs = lax.rsqrt(jnp.mean(xc * xc, axis=-1, keepdims=True) + EPS)
    return xc * rs, rs


def _mix_fwd(x, o_f, o_b, p, gla_g, ln_g, ln_b, w_sp, b_sp, w_out, tm, comms=()):
    t = x.shape[0]
    nch = tm // GMLP_CHUNK

    def body(x_ref, of_ref, ob_ref, pg_ref, pu_ref, pv_ref, gg_ref, lg_ref, lb_ref, ws_ref, bs_ref, wo_ref,
             x1_ref, y_ref, s_scr):
        on, _ = _head_rms(of_ref[...] + ob_ref[...])
        pg = pg_ref[...].astype(F32)
        y_a = on * gg_ref[...] * (pg * jax.nn.sigmoid(pg))
        zu = _gelu(pu_ref[...].astype(F32))
        vhat, _ = _layernorm_stats(_gelu(pv_ref[...].astype(F32)))
        vln = (vhat * lg_ref[...] + lb_ref[...]).astype(BF16)
        for g in range(GMLP_GROUPS):
            w_g = ws_ref[g].astype(BF16)
            b_g = bs_ref[g]
            cols = slice(g * LANE, (g + 1) * LANE)
            for n in range(nch):
                rows = slice(n * GMLP_CHUNK, (n + 1) * GMLP_CHUNK)
                s_scr[rows, cols] = jnp.dot(w_g, vln[rows, cols], preferred_element_type=F32) + b_g
        ycat = jnp.concatenate([y_a, zu * s_scr[...]], axis=1).astype(BF16)
        y_ref[...] = ycat
        x1_ref[...] = x_ref[...] + jnp.dot(ycat, wo_ref[...], preferred_element_type=F32)

    half = lambda j: pl.BlockSpec((tm, VAL_W), lambda i: (i, j))
    return _fused_call(
        body, comms, name="mix_fwd", grid=(t // tm,),
        inputs=(x, o_f, o_b, p, p, p, gla_g, ln_g, ln_b, w_sp, b_sp, w_out),
        in_specs=[pl.BlockSpec((tm, D_MODEL), lambda i: (i, 0)), half(0), half(0), half(2), half(3), half(4),
                  _const_spec((1, VAL_W)), _const_spec((1, GMLP_W)), _const_spec((1, GMLP_W)),
                  _const_spec((GMLP_GROUPS, GMLP_CHUNK, GMLP_CHUNK)), _const_spec((GMLP_GROUPS, GMLP_CHUNK, 1)),
                  _const_spec((D_MODEL, D_MODEL))],
        out_specs=(pl.BlockSpec((tm, D_MODEL), lambda i: (i, 0)), pl.BlockSpec((tm, D_MODEL), lambda i: (i, 0))),
        out_shape=(jax.ShapeDtypeStruct((t, D_MODEL), F32), jax.ShapeDtypeStruct((t, D_MODEL), BF16)),
        scratch_shapes=[pltpu.VMEM((tm, GMLP_W), F32)])


def _mix_bwd(dx1, ycat, o_f, o_b, p, gla_g, ln_g, ln_b, w_sp, b_sp, w_out, tm, comms=()):
    t = dx1.shape[0]
    nch = tm // GMLP_CHUNK

    def body(dx1_ref, y_ref, of_ref, ob_ref, pg_ref, pu_ref, pv_ref, gg_ref, lg_ref, lb_ref, ws_ref, bs_ref, wo_ref,
             do_ref, dpg_ref, dpu_ref, dpv_ref, dwo_ref, dgg_ref, dlg_ref, dlb_ref, dws_ref, dbs_ref,
             s_scr, dvln_scr):
        @pl.when(pl.program_id(0) == 0)
        def _():
            for ref in (dwo_ref, dgg_ref, dlg_ref, dlb_ref, dws_ref, dbs_ref):
                ref[...] = jnp.zeros_like(ref)

        dx1 = dx1_ref[...].astype(BF16)
        dycat = _mm_nt(dx1, wo_ref[...])
        dwo_ref[...] += _mm_tn(y_ref[...], dx1)
        dy_a = dycat[:, :VAL_W]
        dy_b = dycat[:, VAL_W:]
        on, r = _head_rms(of_ref[...] + ob_ref[...])
        pg = pg_ref[...].astype(F32)
        sil, dsil = _silu_and_grad(pg)
        gg = gg_ref[...]
        dgg_ref[...] += jnp.sum(dy_a * sil * on, axis=0, keepdims=True)
        don = dy_a * sil * gg
        prod = don * on
        means = jnp.concatenate(
            [jnp.broadcast_to(jnp.mean(prod[:, h * GLA_DV:(h + 1) * GLA_DV], axis=-1, keepdims=True),
                              (tm, GLA_DV)) for h in range(GLA_HEADS)], axis=1)
        do_ref[...] = (r * (don - on * means)).astype(BF16)
        dpg_ref[...] = (dy_a * on * gg * dsil).astype(BF16)
        pu = pu_ref[...].astype(F32)
        pv = pv_ref[...].astype(F32)
        zu, dzu_dpu = _gelu_and_grad(pu)
        zv, dzv_dpv = _gelu_and_grad(pv)
        vhat, rs = _layernorm_stats(zv)
        lg = lg_ref[...]
        vln = (vhat * lg + lb_ref[...]).astype(BF16)
        ds32 = dy_b * zu
        ds = ds32.astype(BF16)
        blocks = [(g, n) for g in range(GMLP_GROUPS) for n in range(nch)]
        at = lambda g, n: (slice(n * GMLP_CHUNK, (n + 1) * GMLP_CHUNK), slice(g * LANE, (g + 1) * LANE))
        w_sp = [ws_ref[g].astype(BF16) for g in range(GMLP_GROUPS)]
        v_blk = {b: vln[at(*b)] for b in blocks}
        ds_blk = {b: ds[at(*b)] for b in blocks}
        s_blk = {b: jnp.dot(w_sp[b[0]], v_blk[b], preferred_element_type=F32) for b in blocks}
        dw_blk = {b: _mm_nt(ds_blk[b], v_blk[b]) for b in blocks}
        dvln_blk = {b: _mm_tn(w_sp[b[0]], ds_blk[b]) for b in blocks}
        for b in blocks:
            s_scr[at(*b)] = s_blk[b] + bs_ref[b[0]]
            dvln_scr[at(*b)] = dvln_blk[b]
        for g in range(GMLP_GROUPS):
            dws_ref[g] += sum(dw_blk[(g, n)] for n in range(nch))
            dbs_ref[g] += sum(jnp.sum(ds32[at(g, n)], axis=-1, keepdims=True) for n in range(nch))
        dpu_ref[...] = (dy_b * s_scr[...] * dzu_dpu).astype(BF16)
        dvln = dvln_scr[...]
        dlg_ref[...] += jnp.sum(dvln * vhat, axis=0, keepdims=True)
        dlb_ref[...] += jnp.sum(dvln, axis=0, keepdims=True)
        dvhat = dvln * lg
        dzv = rs * (dvhat - jnp.mean(dvhat, axis=-1, keepdims=True)
                    - vhat * jnp.mean(dvhat * vhat, axis=-1, keepdims=True))
        dpv_ref[...] = (dzv * dzv_dpv).astype(BF16)

    half = lambda j: pl.BlockSpec((tm, VAL_W), lambda i: (i, j))
    full = pl.BlockSpec((tm, D_MODEL), lambda i: (i, 0))
    sp_shape = (GMLP_GROUPS, GMLP_CHUNK, GMLP_CHUNK)
    bs_shape = (GMLP_GROUPS, GMLP_CHUNK, 1)
    return _fused_call(
        body, comms, name="mix_bwd", grid=(t // tm,),
        inputs=(dx1, ycat, o_f, o_b, p, p, p, gla_g, ln_g, ln_b, w_sp, b_sp, w_out),
        in_specs=[full, full, half(0), half(0), half(2), half(3), half(4),
                  _const_spec((1, VAL_W)), _const_spec((1, GMLP_W)), _const_spec((1, GMLP_W)),
                  _const_spec(sp_shape), _const_spec(bs_shape), _const_spec((D_MODEL, D_MODEL))],
        out_specs=(half(0), half(0), half(0), half(0), _acc_spec((D_MODEL, D_MODEL)), _acc_spec((1, VAL_W)),
                   _acc_spec((1, GMLP_W)), _acc_spec((1, GMLP_W)), _acc_spec(sp_shape), _acc_spec(bs_shape)),
        out_shape=(jax.ShapeDtypeStruct((t, VAL_W), BF16),) * 4 + (
            jax.ShapeDtypeStruct((D_MODEL, D_MODEL), F32), jax.ShapeDtypeStruct((1, VAL_W), F32),
            jax.ShapeDtypeStruct((1, GMLP_W), F32), jax.ShapeDtypeStruct((1, GMLP_W), F32),
            jax.ShapeDtypeStruct(sp_shape, F32), jax.ShapeDtypeStruct(bs_shape, F32)),
        scratch_shapes=[pltpu.VMEM((tm, GMLP_W), F32), pltpu.VMEM((tm, GMLP_W), F32)])


def _rms_bwd(dy_scaled, xn, r):
    return r * (dy_scaled - xn * jnp.mean(dy_scaled * xn, axis=-1, keepdims=True))


def _ffn(x1, target, g2, gf, w_gate, w_up, w_down, tm):
    t = x1.shape[0]

    def body(x1_ref, tg_ref, g2_ref, gf_ref, wg_ref, wu_ref, wd_ref,
             dx1_ref, h2_ref, dgate_ref, dup_ref, act_ref, dx2_ref, loss_ref, dgf_ref, dg2_ref):
        @pl.when(pl.program_id(0) == 0)
        def _():
            for ref in (loss_ref, dgf_ref, dg2_ref):
                ref[...] = jnp.zeros_like(ref)

        x1v = x1_ref[...]
        g2v = g2_ref[...]
        gfv = gf_ref[...]
        r2 = lax.rsqrt(jnp.mean(x1v * x1v, axis=-1, keepdims=True) + EPS)
        xn1 = x1v * r2
        h2 = (xn1 * g2v).astype(BF16)
        h2_ref[...] = h2
        gate = _mm_nt(h2, wg_ref[...])
        up = _mm_nt(h2, wu_ref[...])
        sil, dsil = _silu_and_grad(gate)
        act = (sil * up).astype(BF16)
        act_ref[...] = act
        x2 = x1v + jnp.dot(act, wd_ref[...], preferred_element_type=F32)
        rf = lax.rsqrt(jnp.mean(x2 * x2, axis=-1, keepdims=True) + EPS)
        xn2 = x2 * rf
        err = xn2 * gfv - tg_ref[...]
        loss_ref[...] += 0.5 * jnp.sum(jnp.mean(err * err, axis=-1, keepdims=True))
        dy = err * (1.0 / D_MODEL)
        dgf_ref[...] += jnp.sum(dy * xn2, axis=0, keepdims=True)
        dx2 = _rms_bwd(dy * gfv, xn2, rf)
        dx2b = dx2.astype(BF16)
        dx2_ref[...] = dx2b
        dact = _mm_nt(dx2b, wd_ref[...])
        dgate = (dact * up * dsil).astype(BF16)
        dup = (dact * sil).astype(BF16)
        dgate_ref[...] = dgate
        dup_ref[...] = dup
        dh2 = _mm(dgate, wg_ref[...]) + _mm(dup, wu_ref[...])
        dg2_ref[...] += jnp.sum(dh2 * xn1, axis=0, keepdims=True)
        dx1_ref[...] = dx2 + _rms_bwd(dh2 * g2v, xn1, r2)

    row = lambda w: pl.BlockSpec((tm, w), lambda i: (i, 0))
    return pl.pallas_call(
        body, name="ffn_fwd_bwd", grid=(t // tm,),
        in_specs=[row(D_MODEL), row(D_MODEL), _const_spec((1, D_MODEL)), _const_spec((1, D_MODEL)),
                  _const_spec((D_FF, D_MODEL)), _const_spec((D_FF, D_MODEL)), _const_spec((D_FF, D_MODEL))],
        out_specs=(row(D_MODEL), row(D_MODEL), row(D_FF), row(D_FF), row(D_FF), row(D_MODEL),
                   _acc_spec((8, LANE)), _acc_spec((1, D_MODEL)), _acc_spec((1, D_MODEL))),
        out_shape=(jax.ShapeDtypeStruct((t, D_MODEL), F32), jax.ShapeDtypeStruct((t, D_MODEL), BF16),
                   jax.ShapeDtypeStruct((t, D_FF), BF16), jax.ShapeDtypeStruct((t, D_FF), BF16),
                   jax.ShapeDtypeStruct((t, D_FF), BF16), jax.ShapeDtypeStruct((t, D_MODEL), BF16),
                   jax.ShapeDtypeStruct((8, LANE), F32), jax.ShapeDtypeStruct((1, D_MODEL), F32),
                   jax.ShapeDtypeStruct((1, D_MODEL), F32)),
        compiler_params=_params(),
    )(x1, target, g2, gf, w_gate, w_up, w_down)


def _matmul_tn(a, b, tm, tk, name, comms=()):
    t, m = a.shape
    n = b.shape[1]

    def body(a_ref, b_ref, o_ref):
        @pl.when(pl.program_id(1) == 0)
        def _():
            o_ref[...] = jnp.zeros_like(o_ref)

        o_ref[...] += _mm_tn(a_ref[...], b_ref[...])

    (out,), comm_results = _fused_call(
        body, comms, name=name, grid=(m // tm, t // tk), inputs=(a, b),
        in_specs=[pl.BlockSpec((tk, tm), lambda j, k: (k, j)), pl.BlockSpec((tk, n), lambda j, k: (k, 0))],
        out_specs=(pl.BlockSpec((tm, n), lambda j, k: (j, 0)),),
        out_shape=(jax.ShapeDtypeStruct((m, n), F32),))
    return out, comm_results


def _in_proj_bwd(x, g1, dx1, dq_f, dq_b, dk_f, dk_b, dv_f, dv_b, dpg, dpu, dpv, dlr_f, dlr_b, w_main, tm, comms=()):
    t = x.shape[0]

    def body(x_ref, g_ref, dx1_ref, dqf, dqb, dkf, dkb, dvf, dvb, dg, du, dv, dlf, dlb, w_ref,
             dx_ref, dp_ref, dg1_ref):
        @pl.when(pl.program_id(0) == 0)
        def _():
            dg1_ref[...] = jnp.zeros_like(dg1_ref)

        both = lambda a, b: (a[...].astype(F32) + b[...].astype(F32)).astype(BF16)
        dp = jnp.concatenate([both(dqf, dqb), both(dkf, dkb), both(dvf, dvb), dg[...], du[...], dv[...],
                              both(dlf, dlb)], axis=1)
        dp_ref[...] = dp
        dh = sum(_mm(dp[:, c0:c0 + r1 - r0], w_ref[r0:r1, :]) for r0, r1, c0 in PROJ_ROWS)
        xv = x_ref[...]
        r = lax.rsqrt(jnp.mean(xv * xv, axis=-1, keepdims=True) + EPS)
        xn = xv * r
        dg1_ref[...] += jnp.sum(dh * xn, axis=0, keepdims=True)
        dx_ref[...] = dx1_ref[...] + _rms_bwd(dh * g_ref[...], xn, r)

    row = lambda w: pl.BlockSpec((tm, w), lambda i: (i, 0))
    return _fused_call(
        body, comms, name="in_proj_bwd", grid=(t // tm,),
        inputs=(x, g1, dx1, dq_f, dq_b, dk_f, dk_b, dv_f, dv_b, dpg, dpu, dpv, dlr_f, dlr_b, w_main),
        in_specs=[row(D_MODEL), _const_spec((1, D_MODEL)), row(D_MODEL), row(KEY_W), row(KEY_W), row(KEY_W),
                  row(KEY_W), row(VAL_W), row(VAL_W), row(VAL_W), row(VAL_W), row(VAL_W), row(LANE), row(LANE),
                  _const_spec((PROJ_W, D_MODEL))],
        out_specs=(row(D_MODEL), row(PROJ_PAD), _acc_spec((1, D_MODEL))),
        out_shape=(jax.ShapeDtypeStruct((t, D_MODEL), F32), jax.ShapeDtypeStruct((t, PROJ_PAD), BF16),
                   jax.ShapeDtypeStruct((1, D_MODEL), F32)))


def _adamw(w, g, m, v):
    m_new = ADAM_B1 * m + (1.0 - ADAM_B1) * g
    v_new = ADAM_B2 * v + (1.0 - ADAM_B2) * (g * g)
    m_hat = m_new / (1.0 - ADAM_B1 ** ADAM_STEP)
    v_hat = v_new / (1.0 - ADAM_B2 ** ADAM_STEP)
    delta = -ADAM_LR * (m_hat / (jnp.sqrt(v_hat) + ADAM_EPS) + ADAM_WD * w)
    return delta, m_new, v_new


def _adamw_shard(own, recv, w, m, v, tr, name):
    r, c = w.shape

    def body(own_ref, recv_ref, w_ref, m_ref, v_ref, g_ref, d_ref, nm_ref, nv_ref):
        g = own_ref[...]
        for k in range(3):
            g = g + recv_ref[k].astype(F32)
        g_ref[...] = g
        d_ref[...], nm_ref[...], nv_ref[...] = _adamw(w_ref[...], g, m_ref[...], v_ref[...])

    row = pl.BlockSpec((tr, c), lambda i: (i, 0))
    return pl.pallas_call(
        body, name=name, grid=(r // tr,),
        in_specs=[row, pl.BlockSpec((3, tr, c), lambda i: (0, i, 0)), row, row, row],
        out_specs=(row,) * 4, out_shape=(jax.ShapeDtypeStruct((r, c), F32),) * 4,
        compiler_params=_params(),
    )(own, recv, w, m, v)


def _adamw_small(entries):
    stacks = []
    for (g, _, _), _, _, _ in entries:
        if not any(g is s for s in stacks):
            stacks.append(g)
    where = [next(i for i, s in enumerate(stacks) if s is g) for (g, _, _), _, _, _ in entries]
    ns, ne = len(stacks), len(entries)

    def body(*refs):
        s_refs, wmv, outs = refs[:ns], refs[ns:ns + 3 * ne], refs[ns + 3 * ne:]
        for e, ((_, r0, nr), _, _, _) in enumerate(entries):
            grad = s_refs[where[e]][r0:r0 + nr, :]
            w_ref, m_ref, v_ref = wmv[3 * e:3 * e + 3]
            g_ref, d_ref, nm_ref, nv_ref = outs[4 * e:4 * e + 4]
            g_ref[...] = grad
            d_ref[...], nm_ref[...], nv_ref[...] = _adamw(w_ref[...], grad, m_ref[...], v_ref[...])

    results = pl.pallas_call(
        body, name="adamw_small",
        out_shape=tuple(jax.ShapeDtypeStruct(w.shape, F32) for _, w, _, _ in entries for _ in range(4)),
        compiler_params=pltpu.CompilerParams(vmem_limit_bytes=VMEM_LIMIT),
    )(*stacks, *[a for _, w, m, v in entries for a in (w, m, v)])
    return [results[4 * e:4 * e + 4] for e in range(ne)]


def _mesh_pos():
    return lax.axis_index("x"), lax.axis_index("y"), lax.axis_index("c")


def _other_chips(x, y):
    return [(x, 1 - y), (1 - x, y), (1 - x, 1 - y)]


_VMEM_WHOLE = pl.BlockSpec(memory_space=pltpu.VMEM)
_HBM_WHOLE = pl.BlockSpec(memory_space=pl.ANY)


def _gather_comm(shards, cast, mid=(1, 2)):
    na = len(shards)
    staged = [a for a in range(na) if cast[a]]

    def phases(in_refs, out_refs, scr):
        stage = dict(zip(staged, scr[:len(staged)]))
        send_sems, recv_sems, local_sems = scr[len(staged):]
        x, y, c = _mesh_pos()
        me, sibling = (x, y, c), (x, y, 1 - c)
        chips = _other_chips(x, y)
        srcs = [stage[a] if cast[a] else in_refs[a] for a in range(na)]

        def rows(a, pos):
            px, py, pc = pos
            return out_refs[a].at[4 * px + 2 * py + pc]

        def copy(a, k, block, to, src=None):
            return pltpu.make_async_remote_copy(
                src_ref=rows(a, block) if src is None else src, dst_ref=rows(a, block),
                send_sem=send_sems.at[a, k], recv_sem=recv_sems.at[a, k], device_id=to, device_id_type=MESH_ID)

        mine = [pltpu.make_async_copy(srcs[a], rows(a, me), local_sems.at[a]) for a in range(na)]
        first = []
        for a in range(na):
            first.append(copy(a, 0, me, sibling, src=srcs[a]))
            first += [copy(a, 1 + j, me, (*chip, c), src=srcs[a]) for j, chip in enumerate(chips)]
        passed = [copy(a, 4 + j, (*chip, c), sibling) for j, chip in enumerate(chips) for a in range(na)]

        def start():
            for a in staged:
                stage[a][...] = in_refs[a][...].astype(BF16)
            for cp in mine + first:
                cp.start()

        def forward():
            i = 0
            for j, chip in enumerate(chips):
                for a in range(na):
                    copy(a, 1 + j, (*chip, c), me).wait_recv()
                    passed[i].start()
                    i += 1

        def finish():
            for a in range(na):
                copy(a, 0, sibling, me).wait_recv()
                for j, chip in enumerate(chips):
                    copy(a, 4 + j, (*chip, 1 - c), me).wait_recv()
            for cp in first + passed:
                cp.wait_send()
            for cp in mine:
                cp.wait()

        return start, forward, finish

    def before(step, nsteps, in_refs, out_refs, scr):
        start, forward, _ = phases(in_refs, out_refs, scr)
        pl.when(step == 0)(start)
        pl.when(step == nsteps * mid[0] // mid[1])(forward)

    def after(step, nsteps, in_refs, out_refs, scr):
        pl.when(step == nsteps - 1)(phases(in_refs, out_refs, scr)[2])

    return _Comm(
        inputs=list(shards), in_specs=[_VMEM_WHOLE] * na,
        out_shape=[jax.ShapeDtypeStruct((N_DEV,) + s.shape, BF16 if cast[a] else s.dtype)
                   for a, s in enumerate(shards)],
        out_specs=[_HBM_WHOLE] * na,
        scratch_shapes=[pltpu.VMEM(shards[a].shape, BF16) for a in staged] + [
            pltpu.SemaphoreType.DMA((na, 7)), pltpu.SemaphoreType.DMA((na, 7)), pltpu.SemaphoreType.DMA((na,))],
        before=before, after=after)


def _exchange_comm(arrays, out_shape, make_copies):
    na = len(arrays)

    def copies(in_refs, out_refs, scr):
        return make_copies(in_refs, out_refs, *scr)

    def before(step, nsteps, in_refs, out_refs, scr):
        @pl.when(step == 0)
        def _():
            for cp in copies(in_refs, out_refs, scr):
                cp.start()

    def after(step, nsteps, in_refs, out_refs, scr):
        @pl.when(step == nsteps - 1)
        def _():
            for cp in copies(in_refs, out_refs, scr):
                cp.wait()

    return _Comm(inputs=list(arrays), in_specs=[_HBM_WHOLE] * na, out_shape=list(out_shape),
                 out_specs=[_HBM_WHOLE] * na,
                 scratch_shapes=[pltpu.SemaphoreType.DMA((na, 3)), pltpu.SemaphoreType.DMA((na, 3))],
                 before=before, after=after)


def _sibling_exchange_comm(grads):
    def make_copies(in_refs, out_refs, send_sems, recv_sems):
        x, y, c = _mesh_pos()
        return [pltpu.make_async_remote_copy(
            src_ref=in_refs[a].at[:, pl.ds(1 - c, 1)], dst_ref=out_refs[a], send_sem=send_sems.at[a, 0],
            recv_sem=recv_sems.at[a, 0], device_id=(x, y, 1 - c), device_id_type=MESH_ID)
            for a in range(len(grads))]

    return _exchange_comm(grads, [jax.ShapeDtypeStruct((4, 1) + g.shape[2:], F32) for g in grads], make_copies)


def _chips_exchange_comm(partials):
    def make_copies(in_refs, out_refs, send_sems, recv_sems):
        x, y, c = _mesh_pos()
        return [pltpu.make_async_remote_copy(
            src_ref=in_refs[a].at[j], dst_ref=out_refs[a].at[j], send_sem=send_sems.at[a, j],
            recv_sem=recv_sems.at[a, j], device_id=(*chip, c), device_id_type=MESH_ID)
            for a in range(len(partials)) for j, chip in enumerate(_other_chips(x, y))]

    return _exchange_comm(partials, [jax.ShapeDtypeStruct(g.shape, BF16) for g in partials], make_copies)


def _comm_only(comms, name):
    return _fused_call(lambda: None, comms, name=name, grid=(1,), inputs=(), in_specs=[], out_specs=(),
                       out_shape=())[1]


def _chip_sum(my_pos, mine, from_sibling, tr, name):
    _, _, r, c = mine.shape

    def body(pos_ref, a_ref, b_ref, own_ref, out_ref):
        s = a_ref[0, 0] + b_ref[0, 0]

        @pl.when(pl.program_id(1) == 0)
        def _():
            own_ref[...] = s

        @pl.when(pl.program_id(1) > 0)
        def _():
            out_ref[0] = s.astype(BF16)

    grid_spec = pltpu.PrefetchScalarGridSpec(
        num_scalar_prefetch=1, grid=(r // tr, 4),
        in_specs=[pl.BlockSpec((1, 1, tr, c), lambda i, k, pos: (pos[0] ^ k, pos[1], i, 0)),
                  pl.BlockSpec((1, 1, tr, c), lambda i, k, pos: (pos[0] ^ k, 0, i, 0))],
        out_specs=(pl.BlockSpec((tr, c), lambda i, k, pos: (i, 0)),
                   pl.BlockSpec((1, tr, c), lambda i, k, pos: (jnp.maximum(k - 1, 0), i, 0))))
    return pl.pallas_call(
        body, name=name, grid_spec=grid_spec,
        out_shape=(jax.ShapeDtypeStruct((r, c), F32), jax.ShapeDtypeStruct((3, r, c), BF16)),
        compiler_params=_params(2),
    )(my_pos, mine, from_sibling)


def _all_reduce_small_comm(parts):
    na = len(parts)

    def copies(in_refs, scr):
        gathered, (send_sems, recv_sems) = scr[:na], scr[na:]
        x, y, c = _mesh_pos()
        my_id = 4 * x + 2 * y + c
        return my_id, [pltpu.make_async_remote_copy(
            src_ref=in_refs[a], dst_ref=gathered[a].at[my_id], send_sem=send_sems.at[a, k - 1],
            recv_sem=recv_sems.at[a, k - 1], device_id=(x ^ (k >> 2), y ^ ((k >> 1) & 1), c ^ (k & 1)),
            device_id_type=MESH_ID) for a in range(na) for k in range(1, N_DEV)]

    def before(step, nsteps, in_refs, out_refs, scr):
        @pl.when(step == 0)
        def _():
            for cp in copies(in_refs, scr)[1]:
                cp.start()

    def after(step, nsteps, in_refs, out_refs, scr):
        @pl.when(step == nsteps - 1)
        def _():
            my_id, cps = copies(in_refs, scr)
            for a in range(na):
                scr[a][my_id] = in_refs[a][...]
            for cp in cps:
                cp.wait()
            for a in range(na):
                acc = scr[a][0]
                for d in range(1, N_DEV):
                    acc = acc + scr[a][d]
                out_refs[a][...] = acc

    return _Comm(inputs=list(parts), in_specs=[_VMEM_WHOLE] * na,
                 out_shape=[jax.ShapeDtypeStruct(p.shape, F32) for p in parts], out_specs=[_VMEM_WHOLE] * na,
                 scratch_shapes=[pltpu.VMEM((N_DEV,) + p.shape, F32) for p in parts] + [
                     pltpu.SemaphoreType.DMA((na, N_DEV - 1)), pltpu.SemaphoreType.DMA((na, N_DEV - 1))],
                 before=before, after=after)


def _unshard_cols(g):
    return jnp.transpose(g, (1, 0, 2)).reshape(g.shape[1], N_DEV * g.shape[2])


def _row_blocks(w):
    return w.reshape(4, 2, w.shape[0] // N_DEV, w.shape[1])


def _stack_rows(parts):
    a = jnp.concatenate(parts, axis=0)
    return jnp.pad(a, ((0, (-a.shape[0]) % 8), (0, 0)))


def _main_proj_grad(dw_main_t):
    return jnp.concatenate([dw_main_t[:LR_REF], dw_main_t[LR_COL:LR_COL + 2 * LOWRANK], dw_main_t[LR_REF:LR_COL]],
                           axis=0)


def _padded_decay_weights(wd_f, wd_b):
    zeros = lambda n: jnp.zeros((n, KEY_W), F32)
    return (jnp.concatenate([wd_f, zeros(LANE - LOWRANK)], axis=0),
            jnp.concatenate([zeros(LOWRANK), wd_b, zeros(LANE - 2 * LOWRANK)], axis=0))


def kernel(x, norm1_g, w_in,w_decay_f, b_decay_f, w_decay_b, b_decay_b, gla_norm_g, gmlp_ln_g, gmlp_ln_b, w_spatial, b_spatial, w_out, norm2_g, w_gate, w_up, w_down, final_norm_g, loss_target, m_norm1_g, m_w_in, m_w_decay_f, m_b_decay_f, m_w_decay_b, m_b_decay_b, m_gla_norm_g, m_gmlp_ln_g, m_gmlp_ln_b, m_w_spatial, m_b_spatial, m_w_out, m_norm2_g, m_w_gate, m_w_up, m_w_down, m_final_norm_g, v_norm1_g, v_w_in, v_w_decay_f, v_b_decay_f, v_w_decay_b, v_b_decay_b, v_gla_norm_g, v_gmlp_ln_g, v_gmlp_ln_b, v_w_spatial, v_b_spatial, v_w_out, v_norm2_g, v_w_gate, v_w_up, v_w_down, v_final_norm_g):
    t = x.shape[1]
    xt = x[0]
    target = loss_target[0]
    pos_x, pos_y, pos_c = _mesh_pos()
    my_pos = jnp.stack([2 * pos_x + pos_y, pos_c]).astype(jnp.int32)
    my_id = 4 * pos_x + 2 * pos_y + pos_c

    tile = lambda n: min(n, t)
    ln_g, ln_b, w_sp = gmlp_ln_g, gmlp_ln_b, w_spatial[0]
    b_sp_col = b_spatial[0][:, :, None]
    shard = {"w_in": w_in[0].T, "w_out": w_out[0], "w_gate": w_gate[0].T, "w_up": w_up[0].T, "w_down": w_down[0]}
    shard_m = {"w_in": m_w_in[0].T, "w_out": m_w_out[0], "w_gate": m_w_gate[0].T, "w_up": m_w_up[0].T,
               "w_down": m_w_down[0]}
    shard_v = {"w_in": v_w_in[0].T, "w_out": v_w_out[0], "w_gate": v_w_gate[0].T, "w_up": v_w_up[0].T,
               "w_down": v_w_down[0]}
    transposed = ("w_in", "w_gate", "w_up")
    chip_sum = lambda n, g, s: _chip_sum(my_pos, g, s[0], g.shape[2], "chip_sum_" + n)

    decay_shard = jnp.stack([w_decay_f[0], w_decay_b[0]])
    (hb,), ((g_in, g_decay),) = _norm1(xt, norm1_g, tile(512),
                                       [_gather_comm([shard["w_in"], decay_shard], [True, False])])
    w_in_t = g_in.reshape(PROJ_W, D_MODEL)
    wd_pad_f, wd_pad_b = _padded_decay_weights(_unshard_cols(g_decay[:, 0]), _unshard_cols(g_decay[:, 1]))
    (p,), ((g_gate,),) = _in_proj(hb, w_in_t, tile(1024), [_gather_comm([shard["w_gate"]], [True])])
    (o_f, st_f, o_b, st_b), ((g_up, g_out),) = _gla_fwd(
        p, wd_pad_f, b_decay_f, wd_pad_b, b_decay_b, tile(512),
        [_gather_comm([shard["w_up"], shard["w_out"]], [True, True])])
    w_out_full = g_out.reshape(D_MODEL, D_MODEL)
    (x1, ycat), ((g_down,),) = _mix_fwd(xt, o_f, o_b, p, gla_norm_g, ln_g, ln_b, w_sp, b_sp_col, w_out_full,
                                        tile(512), [_gather_comm([shard["w_down"]], [True])])

    dx1, h2b, dgate, dup, act, dx2, loss_acc, d_gf, d_g2 = _ffn(
        x1, target, norm2_g, final_norm_g[None, :], g_gate.reshape(D_FF, D_MODEL), g_up.reshape(D_FF, D_MODEL),
        g_down.reshape(D_FF, D_MODEL), tile(256))
    dw_gate, _ = _matmul_tn(dgate, h2b, D_FF // 2, tile(2048), "grad_w_gate")
    dw_up, _ = _matmul_tn(dup, h2b, D_FF // 2, tile(2048), "grad_w_up")
    dw_down, _ = _matmul_tn(act, dx2, D_FF // 2, tile(2048), "grad_w_down")

    ffn_grads = [_row_blocks(dw_gate), _row_blocks(dw_up), _row_blocks(dw_down)]
    (d_o, dpg, dpu, dpv, dw_out, d_gg, d_lg, d_lb, dw_sp, db_sp), (ffn_sib,) = _mix_bwd(
        dx1, ycat, o_f, o_b, p, gla_norm_g, ln_g, ln_b, w_sp, b_sp_col, w_out_full, tile(256),
        [_sibling_exchange_comm(ffn_grads)])
    ffn_names = ["w_gate", "w_up", "w_down"]
    ffn_sums = [chip_sum(n, g, [s]) for n, g, s in zip(ffn_names, ffn_grads, ffn_sib)]
    out_grad = _row_blocks(dw_out)
    (dq_f, dk_f, dv_f, dlr_f, dwd_f, dbd_f, dq_b, dk_b, dv_b, dlr_b, dwd_b, dbd_b), (ffn_recv, out_sib) = _gla_bwd(
        p, wd_pad_f, b_decay_f, wd_pad_b, b_decay_b, st_f, st_b, d_o, tile(512),
        [_chips_exchange_comm([s[1] for s in ffn_sums]), _sibling_exchange_comm([out_grad])])
    out_sum = chip_sum("w_out", out_grad, out_sib)
    (grad_x, dp, d_g1), _ = _in_proj_bwd(
        xt, norm1_g, dx1, dq_f, dq_b, dk_f, dk_b, dv_f, dv_b, dpg, dpu, dpv, dlr_f, dlr_b, w_in_t, tile(512))

    stacks = [_stack_rows([d_g1, d_g2, d_gf]), _stack_rows([d_gg, d_lg, d_lb]),
              _stack_rows([dbd_f, dbd_b, jnp.zeros((6, KEY_W), F32), dwd_f[:LOWRANK], dwd_b[LOWRANK:2 * LOWRANK]]),
              _stack_rows([dw_sp.reshape(GMLP_W, GMLP_CHUNK), db_sp[:, :, 0], loss_acc[:1]])]
    dw_main, (small_sums, out_recv) = _matmul_tn(
        dp, hb, PROJ_PAD // 3, tile(2048), "grad_w_in",
        [_all_reduce_small_comm(stacks), _chips_exchange_comm([out_sum[1]])])
    in_grad = _row_blocks(_main_proj_grad(dw_main))
    (in_sib,) = _comm_only([_sibling_exchange_comm([in_grad])], "grad_w_in_exchange_sibling")
    in_sum = chip_sum("w_in", in_grad, in_sib)
    (in_recv,) = _comm_only([_chips_exchange_comm([in_sum[1]])], "grad_w_in_exchange_chips")

    names = ["w_in", "w_out", "w_gate", "w_up", "w_down"]
    sums = [in_sum, out_sum] + ffn_sums
    received = [in_recv[0], out_recv[0]] + list(ffn_recv)
    big_out = {}
    for n, s, rc in zip(names, sums, received):
        res = _adamw_shard(s[0], rc, shard[n], shard_m[n], shard_v[n], shard[n].shape[0], "adamw_" + n)
        big_out[n] = [r.T if n in transposed else r for r in res]

    s1024, s512, s256, s128 = small_sums
    loss = s128[GMLP_W + GMLP_GROUPS, 0]
    col0 = my_id * (KEY_W // N_DEV)
    decay_cols = lambda row0: lax.dynamic_slice(s256, (row0, col0), (LOWRANK, KEY_W // N_DEV))
    flat = lambda a: a.reshape(-1, a.shape[-1])
    small = {
        "norm1_g": ((s1024, 0, 1), norm1_g, m_norm1_g, v_norm1_g),
        "w_decay_f": ((decay_cols(8), 0, LOWRANK), w_decay_f, m_w_decay_f, v_w_decay_f),
        "b_decay_f": ((s256, 0, 1), b_decay_f, m_b_decay_f, v_b_decay_f),
        "w_decay_b": ((decay_cols(8 + LOWRANK), 0, LOWRANK), w_decay_b, m_w_decay_b, v_w_decay_b),
        "b_decay_b": ((s256, 1, 1), b_decay_b, m_b_decay_b, v_b_decay_b),
        "gla_norm_g": ((s512, 0, 1), gla_norm_g, m_gla_norm_g, v_gla_norm_g),
        "gmlp_ln_g": ((s512, 1, 1), gmlp_ln_g, m_gmlp_ln_g, v_gmlp_ln_g),
        "gmlp_ln_b": ((s512, 2, 1), gmlp_ln_b, m_gmlp_ln_b, v_gmlp_ln_b),
        "w_spatial": ((s128, 0, GMLP_W), w_spatial, m_w_spatial, v_w_spatial),
        "b_spatial": ((s128, GMLP_W, GMLP_GROUPS), b_spatial, m_b_spatial, v_b_spatial),
        "norm2_g": ((s1024, 1, 1), norm2_g, m_norm2_g, v_norm2_g),
        "final_norm_g": ((s1024, 2, 1), final_norm_g, m_final_norm_g, v_final_norm_g),
    }
    small_res = _adamw_small([(g, flat(w), flat(m), flat(v)) for g, w, m, v in small.values()])
    small_out = {n: [r.reshape(small[n][1].shape) for r in res] for n, res in zip(small, small_res)}

    order = ["norm1_g", "w_in", "w_decay_f", "b_decay_f", "w_decay_b", "b_decay_b", "gla_norm_g", "gmlp_ln_g",
             "gmlp_ln_b", "w_spatial", "b_spatial", "w_out", "norm2_g", "w_gate", "w_up", "w_down", "final_norm_g"]
    outs = []
    for kind in range(4):
        for n in order:
            outs.append(big_out[n][kind][None] if n in big_out else small_out[n][kind])
    return (loss, grad_x[None], *outs)
```

```python
import functools
import math

import jax
import jax.numpy as jnp
from jax import lax
from jax.experimental import pallas as pl
from jax.experimental.pallas import tpu as pltpu

F32 = jnp.float32
BF16 = jnp.bfloat16

D_MODEL = 1024
GLA_HEADS = 4
GLA_DK = 64
GLA_DV = 128
KEY_W = GLA_HEADS * GLA_DK
VAL_W = GLA_HEADS * GLA_DV
LOWRANK = 16
GLA_TAU = 16.0
GLA_CHUNK = 64
GMLP_W = 512
GMLP_GROUPS = 4
GMLP_CHUNK = 128
D_FF = 2816
EPS = 1e-6
Q_SCALE = GLA_DK ** -0.5
PROJ_PAD = 2688
LR_COL = 2560
LANE = 128
N_DEV = 8

ADAM_LR = 0.001
ADAM_B1 = 0.9
ADAM_B2 = 0.999
ADAM_EPS = 1e-08
ADAM_WD = 0.01
ADAM_STEP = 10

VMEM_LIMIT = 56 * 1024 * 1024
MESH_ID = pl.DeviceIdType.MESH
INV_SQRT2 = 0.7071067811865476
INV_SQRT_2PI = 0.3989422804014327


def _params(n_axes=1):
    return pltpu.CompilerParams(dimension_semantics=("arbitrary",) * n_axes, vmem_limit_bytes=VMEM_LIMIT)


def _mm(a, b):
    return jnp.dot(a.astype(BF16), b.astype(BF16), preferred_element_type=F32)


def _mm_nt(a, b):
    return lax.dot_general(a.astype(BF16), b.astype(BF16), (((1,), (1,)), ((), ())), preferred_element_type=F32)


def _mm_tn(a, b):
    return lax.dot_general(a.astype(BF16), b.astype(BF16), (((0,), (0,)), ((), ())), preferred_element_type=F32)


def _const_spec(shape):
    nd = len(shape)
    return pl.BlockSpec(shape, lambda *_: (0,) * nd, pipeline_mode=pl.Buffered(1))


def _acc_spec(shape):
    nd = len(shape)
    return pl.BlockSpec(shape, lambda *_: (0,) * nd)


class _Comm:
    def __init__(self, inputs, in_specs, out_shape, out_specs, scratch_shapes, before, after):
        self.inputs, self.in_specs, self.out_shape, self.out_specs = inputs, in_specs, out_shape, out_specs
        self.scratch_shapes, self.before, self.after = scratch_shapes, before, after


def _fused_call(body, comms, *, name, grid, inputs, in_specs, out_specs, out_shape, scratch_shapes=()):
    n_in, n_out, n_scr = len(in_specs), len(out_specs), len(scratch_shapes)
    nsteps = math.prod(grid)
    sizes = [(len(c.inputs), len(c.out_shape), len(c.scratch_shapes)) for c in comms]

    def full_body(*refs):
        step = pl.program_id(0)
        for axis in range(1, len(grid)):
            step = step * grid[axis] + pl.program_id(axis)
        ins, rest = refs[:n_in], refs[n_in:]
        c_ins = []
        for ci, _, _ in sizes:
            c_ins.append(rest[:ci])
            rest = rest[ci:]
        outs, rest = rest[:n_out], rest[n_out:]
        c_outs = []
        for _, co, _ in sizes:
            c_outs.append(rest[:co])
            rest = rest[co:]
        scr, rest = rest[:n_scr], rest[n_scr:]
        c_scr = []
        for _, _, cs in sizes:
            c_scr.append(rest[:cs])
            rest = rest[cs:]
        for c, a, b, s in zip(comms, c_ins, c_outs, c_scr):
            c.before(step, nsteps, a, b, s)
        body(*ins, *outs, *scr)
        for c, a, b, s in zip(comms, c_ins, c_outs, c_scr):
            c.after(step, nsteps, a, b, s)

    results = pl.pallas_call(
        full_body, name=name, grid=grid,
        in_specs=list(in_specs) + [s for c in comms for s in c.in_specs],
        out_specs=tuple(out_specs) + tuple(s for c in comms for s in c.out_specs),
        out_shape=tuple(out_shape) + tuple(s for c in comms for s in c.out_shape),
        scratch_shapes=list(scratch_shapes) + [s for c in comms for s in c.scratch_shapes],
        compiler_params=_params(len(grid)),
    )(*inputs, *[a for c in comms for a in c.inputs])
    own, rest = results[:n_out], results[n_out:]
    comm_results = []
    for _, co, _ in sizes:
        comm_results.append(rest[:co])
        rest = rest[co:]
    return own, comm_results


def _gelu(x):
    return 0.5 * x * (1.0 + lax.erf(x * INV_SQRT2))


def _gelu_and_grad(x):
    cdf = 0.5 * (1.0 + lax.erf(x * INV_SQRT2))
    return x * cdf, cdf + x * jnp.exp(-0.5 * x * x) * INV_SQRT_2PI


def _sigmoid(x):
    return 0.5 + 0.5 * jnp.tanh(0.5 * x)


def _silu_and_grad(x):
    s = _sigmoid(x)
    return x * s, s * (1.0 + x * (1.0 - s))


def _norm1(x, g1, tm, comms=()):
    t = x.shape[0]

    def body(x_ref, g_ref, h_ref):
        xv = x_ref[...]
        r = lax.rsqrt(jnp.mean(xv * xv, axis=-1, keepdims=True) + EPS)
        h_ref[...] = (xv * r * g_ref[...]).astype(BF16)

    row = pl.BlockSpec((tm, D_MODEL), lambda i: (i, 0))
    return _fused_call(body, comms, name="norm1", grid=(t // tm,), inputs=(x, g1),
                       in_specs=[row, _const_spec((1, D_MODEL))], out_specs=(row,),
                       out_shape=(jax.ShapeDtypeStruct((t, D_MODEL), BF16),))


PROJ_W = 2592
LR_REF = 1536
PROJ_ROWS = ((0, LR_REF, 0), (LR_REF + 2 * LOWRANK, PROJ_W, LR_REF), (LR_REF, LR_REF + LANE, LR_COL))


def _in_proj(h, w_in_t, tm, comms=()):
    t = h.shape[0]

    def body(h_ref, w_ref, p_ref):
        hv = h_ref[...]
        for r0, r1, c0 in PROJ_ROWS:
            p_ref[:, c0:c0 + r1 - r0] = _mm_nt(hv, w_ref[r0:r1, :]).astype(BF16)

    return _fused_call(
        body, comms, name="in_proj", grid=(t // tm,), inputs=(h, w_in_t),
        in_specs=[pl.BlockSpec((tm, D_MODEL), lambda i: (i, 0)), _const_spec((PROJ_W, D_MODEL))],
        out_specs=(pl.BlockSpec((tm, PROJ_PAD), lambda i: (i, 0)),),
        out_shape=(jax.ShapeDtypeStruct((t, PROJ_PAD), BF16),))


def _tri(upper):
    r = lax.broadcasted_iota(jnp.int32, (GLA_CHUNK, GLA_CHUNK), 0)
    c = lax.broadcasted_iota(jnp.int32, (GLA_CHUNK, GLA_CHUNK), 1)
    return jnp.where((c >= r) if upper else (c <= r), 1.0, 0.0).astype(BF16)


def _chunk_cumsum(tri, a):
    hi = a.astype(BF16)
    lo = (a - hi.astype(F32)).astype(BF16)
    dot = functools.partial(jnp.dot, preferred_element_type=F32)
    return jnp.concatenate([dot(tri, hi[_chunk_rows(c)]) + dot(tri, lo[_chunk_rows(c)])
                            for c in range(a.shape[0] // GLA_CHUNK)], axis=0)


def _chunk_rows(c):
    return slice(c * GLA_CHUNK, (c + 1) * GLA_CHUNK)


def _gla_masks(rev):
    dk_bits, dv_bits = GLA_DK.bit_length() - 1, GLA_DV.bit_length() - 1
    key_head = lax.broadcasted_iota(jnp.int32, (GLA_CHUNK, KEY_W), 1) >> dk_bits
    val_head = lax.broadcasted_iota(jnp.int32, (GLA_CHUNK, VAL_W), 1) >> dv_bits
    t = lax.broadcasted_iota(jnp.int32, (GLA_HEADS * GLA_CHUNK, GLA_CHUNK), 0) & (GLA_CHUNK - 1)
    s = lax.broadcasted_iota(jnp.int32, (GLA_HEADS * GLA_CHUNK, GLA_CHUNK), 1)
    causal = (s >= t) if rev else (s <= t)
    state_head = lax.broadcasted_iota(jnp.int32, (GLA_DV, KEY_W), 1) >> dk_bits
    return key_head, val_head, causal, state_head


def _stack_heads(a, head_of_lane):
    a = a.astype(BF16)
    return jnp.concatenate([jnp.where(head_of_lane == h, a, jnp.zeros_like(a)) for h in range(GLA_HEADS)], axis=0)


def _rows_by_head(a):
    return jnp.concatenate([a[:, h * GLA_DV:(h + 1) * GLA_DV] for h in range(GLA_HEADS)], axis=0)


def _lanes_by_head(r):
    return jnp.concatenate([r[h * GLA_CHUNK:(h + 1) * GLA_CHUNK] for h in range(GLA_HEADS)], axis=1)


def _head_diagonal(r, head_of_lane):
    rows = r.shape[0] // GLA_HEADS
    out = jnp.where(head_of_lane == 0, r[:rows], 0.0)
    for h in range(1, GLA_HEADS):
        out = out + jnp.where(head_of_lane == h, r[h * rows:(h + 1) * rows], 0.0)
    return out


def _tile_terms(la, q, k, tri, rev):
    nc = la.shape[0] // GLA_CHUNK
    q, k = q.astype(F32), k.astype(F32)
    b = _chunk_cumsum(tri, la)
    ebl = [jnp.exp(b[c * GLA_CHUNK:c * GLA_CHUNK + 1] if rev else b[(c + 1) * GLA_CHUNK - 1:(c + 1) * GLA_CHUNK])
           for c in range(nc)]
    eb = jnp.exp(b)
    enb = jnp.exp(-b)
    ee = enb * jnp.concatenate([jnp.broadcast_to(row, (GLA_CHUNK, KEY_W)) for row in ebl], axis=0)
    return ebl, eb, enb, ee, q * Q_SCALE * eb, k * enb, k * ee


def _log_decay(lr_ref, wd_ref, bd_ref):
    z = _mm(lr_ref[...], wd_ref[...]) + bd_ref[...]
    return z, jax.nn.log_sigmoid(z) * (1.0 / GLA_TAU)


def _p_specs(tg, tile):
    return [pl.BlockSpec((tg, KEY_W), lambda i: (tile(i), 0)),
            pl.BlockSpec((tg, KEY_W), lambda i: (tile(i), 1)),
            pl.BlockSpec((tg, VAL_W), lambda i: (tile(i), 1)),
            pl.BlockSpec((tg, LANE), lambda i: (tile(i), LR_COL // LANE))]


def _gla_fwd_dir(rev, nc, q_ref, k_ref, v_ref, lr_ref, wd_ref, bd_ref, o_ref, st_ref, state):
    key_head, _, causal, state_head = _gla_masks(rev)
    order = range(nc - 1, -1, -1) if rev else range(nc)

    def intra():
        _, la = _log_decay(lr_ref, wd_ref, bd_ref)
        ebl, _, _, _, qd, kd, ke = _tile_terms(la, q_ref[...], k_ref[...], _tri(rev), rev)
        kd = kd.astype(BF16)
        v = {c: v_ref[_chunk_rows(c), :].astype(BF16) for c in order}
        qd_stack = {c: _stack_heads(qd[_chunk_rows(c)], key_head) for c in order}
        ke_stack = {c: _stack_heads(ke[_chunk_rows(c)], key_head) for c in order}
        a_all = {c: _mm_nt(qd_stack[c], kd[_chunk_rows(c)]) for c in order}
        a_all = {c: jnp.where(causal, a_all[c], 0.0).astype(BF16) for c in order}
        r = {c: _mm(a_all[c], v[c]) for c in order}
        upd = {c: _mm_tn(_rows_by_head(v[c]), ke_stack[c]) for c in order}
        return {c: (ebl[c], qd_stack[c], r[c], upd[c]) for c in order}

    def scan(terms):
        st = state[...]
        states = {}
        for c in order:
            states[c] = st
            st_ref[c] = st.astype(BF16)
            st = st * terms[c][0] + terms[c][3]
        state[...] = st
        return states

    def inter(terms, states):
        r_inter = {c: _mm_nt(terms[c][1], states[c]) for c in order}
        for c in order:
            r = terms[c][2]
            o_ref[_chunk_rows(c), :] = jnp.concatenate(
                [r[h * GLA_CHUNK:(h + 1) * GLA_CHUNK, h * GLA_DV:(h + 1) * GLA_DV]
                 + r_inter[c][h * GLA_CHUNK:(h + 1) * GLA_CHUNK] for h in range(GLA_HEADS)], axis=1)

    return intra, scan, inter


def _gla_fwd(p, wd_pad_f, bd_f, wd_pad_b, bd_b, tg, comms=()):
    t = p.shape[0]
    nt = t // tg
    nc = tg // GLA_CHUNK
    up, down = (lambda i: i), (lambda i: nt - 1 - i)

    def body(qf, kf, vf, lrf, qb, kb, vb, lrb, wdf, bdf, wdb, bdb, of, stf, ob, stb, state_f, state_b):
        @pl.when(pl.program_id(0) == 0)
        def _():
            state_f[...] = jnp.zeros_like(state_f)
            state_b[...] = jnp.zeros_like(state_b)

        dirs = [_gla_fwd_dir(False, nc, qf, kf, vf, lrf, wdf, bdf, of, stf, state_f),
                _gla_fwd_dir(True, nc, qb, kb, vb, lrb, wdb, bdb, ob, stb, state_b)]
        terms = [intra() for intra, _, _ in dirs]
        states = [scan(t) for (_, scan, _), t in zip(dirs, terms)]
        for (_, _, inter), t, s in zip(dirs, terms, states):
            inter(t, s)

    wd_spec, bd_spec = _const_spec((LANE, KEY_W)), _const_spec((1, KEY_W))
    outs = lambda tile: (pl.BlockSpec((tg, VAL_W), lambda i: (tile(i), 0)),
                         pl.BlockSpec((nc, GLA_DV, KEY_W), lambda i: (tile(i), 0, 0)))
    out_shape = (jax.ShapeDtypeStruct((t, VAL_W), F32), jax.ShapeDtypeStruct((t // GLA_CHUNK, GLA_DV, KEY_W), BF16))
    return _fused_call(
        body, comms, name="gla_fwd", grid=(nt,), inputs=(p,) * 8 + (wd_pad_f, bd_f, wd_pad_b, bd_b),
        in_specs=_p_specs(tg, up) + _p_specs(tg, down) + [wd_spec, bd_spec, wd_spec, bd_spec],
        out_specs=outs(up) + outs(down), out_shape=out_shape * 2,
        scratch_shapes=[pltpu.VMEM((GLA_DV, KEY_W), F32)] * 2)


def _gla_bwd_dir(rev, nc, q_ref, k_ref, v_ref, lr_ref, wd_ref, bd_ref, st_ref, do_ref,
                 dq_ref, dk_ref, dv_ref, dlr_ref, dwd_ref, dbd_ref, dstate):
    key_head, val_head, causal, state_head = _gla_masks(rev)
    order = range(nc) if rev else range(nc - 1, -1, -1)
    tg = nc * GLA_CHUNK

    def intra():
        z, la = _log_decay(lr_ref, wd_ref, bd_ref)
        tile = _tile_terms(la, q_ref[...], k_ref[...], _tri(rev), rev)
        qd, kd = tile[4], tile[5].astype(BF16)
        v = {c: v_ref[_chunk_rows(c), :].astype(BF16) for c in order}
        d_o = {c: do_ref[_chunk_rows(c), :] for c in order}
        kd_c = {c: kd[_chunk_rows(c)] for c in order}
        qd_stack = {c: _stack_heads(qd[_chunk_rows(c)], key_head) for c in order}
        do_stack = {c: _stack_heads(d_o[c], val_head) for c in order}
        do_rows = {c: _rows_by_head(d_o[c]) for c in order}
        a_all = {c: _mm_nt(qd_stack[c], kd_c[c]) for c in order}
        da_all = {c: _mm_nt(do_stack[c], v[c]) for c in order}
        a_all = {c: jnp.where(causal, a_all[c], 0.0).astype(BF16) for c in order}
        da_all = {c: jnp.where(causal, da_all[c], 0.0).astype(BF16) for c in order}
        dv = {c: _mm_tn(a_all[c], do_stack[c]) for c in order}
        dqd = {c: _mm(da_all[c], kd_c[c]) + _mm(do_rows[c], st_ref[c]) for c in order}
        dkd = {c: _mm_tn(da_all[c], qd_stack[c]) for c in order}
        upd = {c: _mm_tn(do_rows[c], qd_stack[c]) for c in order}
        dqd = {c: _head_diagonal(dqd[c], key_head) for c in order}
        return z, tile, {c: dict(dv=dv[c], dqd=dqd[c], dkd=dkd[c], upd=upd[c]) for c in order}

    def scan(tile, per):
        dst = dstate[...]
        dsts = {}
        for c in order:
            dsts[c] = dst
            dst = dst * tile[0][c] + per[c]["upd"]
        dstate[...] = dst
        return dsts

    def inter(z, tile, per, dsts):
        ebl, eb, enb, ee, qd, kd, ke = tile
        ke_stack = {c: _stack_heads(ke[_chunk_rows(c)], key_head) for c in order}
        v_rows = {c: _rows_by_head(v_ref[_chunk_rows(c), :].astype(BF16)) for c in order}
        dst_b = {c: dsts[c].astype(BF16) for c in order}
        dv_state = {c: _mm_nt(ke_stack[c], dst_b[c]) for c in order}
        dke_c = {c: _mm(v_rows[c], dst_b[c]) for c in order}
        dke_c = {c: _head_diagonal(dke_c[c], key_head) for c in order}
        dbl_c = {}
        for c in order:
            rows = _chunk_rows(c)
            dv_ref[rows, :] = (per[c]["dv"] + _lanes_by_head(dv_state[c])).astype(BF16)
            dbl = (jnp.sum(dsts[c] * st_ref[c].astype(F32), axis=0, keepdims=True) * ebl[c]
                   + jnp.sum(dke_c[c] * ke[rows], axis=0, keepdims=True))
            dbl_c[c] = jnp.broadcast_to(dbl, (GLA_CHUNK, KEY_W))
        tile_of = lambda parts: jnp.concatenate([parts[c] for c in range(nc)], axis=0)
        dqd, dkd = tile_of({c: per[c]["dqd"] for c in order}), tile_of({c: per[c]["dkd"] for c in order})
        dke, dbl = tile_of(dke_c), tile_of(dbl_c)
        dq_ref[...] = (dqd * eb * Q_SCALE).astype(BF16)
        dk_ref[...] = (dkd * enb + dke * ee).astype(BF16)
        db = dqd * qd - dkd * kd - dke * ke
        dz = (_chunk_cumsum(_tri(not rev), db) + dbl) * (_sigmoid(-z) * (1.0 / GLA_TAU))
        dlr_ref[...] = _mm_nt(dz, wd_ref[...]).astype(BF16)
        dwd_ref[...] += _mm_tn(lr_ref[...], dz)
        dbd_ref[...] += jnp.sum(dz, axis=0, keepdims=True)

    return intra, scan, inter


def _gla_bwd(p, wd_pad_f, bd_f, wd_pad_b, bd_b, st_f, st_b, d_o, tg, comms=()):
    t = p.shape[0]
    nt = t // tg
    nc = tg // GLA_CHUNK
    up, down = (lambda i: i), (lambda i: nt - 1 - i)

    def body(qf, kf, vf, lrf, stf, dof, qb, kb, vb, lrb, stb, dob, wdf, bdf, wdb, bdb,
             dqf, dkf, dvf, dlrf, dwdf, dbdf, dqb, dkb, dvb, dlrb, dwdb, dbdb, dstate_f, dstate_b):
        @pl.when(pl.program_id(0) == 0)
        def _():
            for ref in (dstate_f, dstate_b, dwdf, dbdf, dwdb, dbdb):
                ref[...] = jnp.zeros_like(ref)

        dirs = [_gla_bwd_dir(False, nc, qf, kf, vf, lrf, wdf, bdf, stf, dof, dqf, dkf, dvf, dlrf, dwdf, dbdf,
                             dstate_f),
                _gla_bwd_dir(True, nc, qb, kb, vb, lrb, wdb, bdb, stb, dob, dqb, dkb, dvb, dlrb, dwdb, dbdb,
                             dstate_b)]
        first = [intra() for intra, _, _ in dirs]
        dsts = [scan(tile, per) for (_, scan, _), (_, tile, per) in zip(dirs, first)]
        for (_, _, inter), (z, tile, per), d in zip(dirs, first, dsts):
            inter(z, tile, per, d)

    wd_spec, bd_spec = _const_spec((LANE, KEY_W)), _const_spec((1, KEY_W))
    ins = lambda tile: _p_specs(tg, tile) + [pl.BlockSpec((nc, GLA_DV, KEY_W), lambda i: (tile(i), 0, 0)),
                                             pl.BlockSpec((tg, VAL_W), lambda i: (tile(i), 0))]
    outs = lambda tile: (pl.BlockSpec((tg, KEY_W), lambda i: (tile(i), 0)),
                         pl.BlockSpec((tg, KEY_W), lambda i: (tile(i), 0)),
                         pl.BlockSpec((tg, VAL_W), lambda i: (tile(i), 0)),
                         pl.BlockSpec((tg, LANE), lambda i: (tile(i), 0)),
                         _acc_spec((LANE, KEY_W)), _acc_spec((1, KEY_W)))
    out_shape = (jax.ShapeDtypeStruct((t, KEY_W), BF16), jax.ShapeDtypeStruct((t, KEY_W), BF16),
                 jax.ShapeDtypeStruct((t, VAL_W), BF16), jax.ShapeDtypeStruct((t, LANE), BF16),
                 jax.ShapeDtypeStruct((LANE, KEY_W), F32), jax.ShapeDtypeStruct((1, KEY_W), F32))
    scratch = [pltpu.VMEM((GLA_DV, KEY_W), F32)]
    return _fused_call(
        body, comms, name="gla_bwd", grid=(nt,),
        inputs=(p, p, p, p, st_f, d_o, p, p, p, p, st_b, d_o, wd_pad_f, bd_f, wd_pad_b, bd_b),
        in_specs=ins(down) + ins(up) + [wd_spec, bd_spec, wd_spec, bd_spec],
        out_specs=outs(down) + outs(up), out_shape=out_shape * 2, scratch_shapes=scratch * 2)


def _head_rms(o):
    parts, scales = [], []
    for h in range(GLA_HEADS):
        oh = o[:, h * GLA_DV:(h + 1) * GLA_DV]
        r = lax.rsqrt(jnp.mean(oh * oh, axis=-1, keepdims=True) + EPS)
        parts.append(oh * r)
        scales.append(jnp.broadcast_to(r, oh.shape))
    return jnp.concatenate(parts, axis=1), jnp.concatenate(scales, axis=1)


def _layernorm_stats(zv):
    mu = jnp.mean(zv, axis=-1, keepdims=True)
    xc = zv - mu
    rs = lax.rsqrt(jnp.mean(xc * xc, axis=-1, keepdims=True) + EPS)
    return xc * rs, rs


def _mix_fwd(x, o_f, o_b, p, gla_g, ln_g, ln_b, w_sp, b_sp, w_out, tm, comms=()):
    t = x.shape[0]
    nch = tm // GMLP_CHUNK

    def body(x_ref, of_ref, ob_ref, pg_ref, pu_ref, pv_ref, gg_ref, lg_ref, lb_ref, ws_ref, bs_ref, wo_ref,
             x1_ref, y_ref, s_scr):
        on, _ = _head_rms(of_ref[...] + ob_ref[...])
        pg = pg_ref[...].astype(F32)
        y_a = on * gg_ref[...] * (pg * _sigmoid(pg))
        zu = _gelu(pu_ref[...].astype(F32))
        vhat, _ = _layernorm_stats(_gelu(pv_ref[...].astype(F32)))
        vln = (vhat * lg_ref[...] + lb_ref[...]).astype(BF16)
        for g in range(GMLP_GROUPS):
            w_g = ws_ref[g].astype(BF16)
            b_g = bs_ref[g]
            cols = slice(g * LANE, (g + 1) * LANE)
            for n in range(nch):
                rows = slice(n * GMLP_CHUNK, (n + 1) * GMLP_CHUNK)
                s_scr[rows, cols] = jnp.dot(w_g, vln[rows, cols], preferred_element_type=F32) + b_g
        ycat = jnp.concatenate([y_a, zu * s_scr[...]], axis=1).astype(BF16)
        y_ref[...] = ycat
        x1_ref[...] = x_ref[...] + jnp.dot(ycat, wo_ref[...], preferred_element_type=F32)

    half = lambda j: pl.BlockSpec((tm, VAL_W), lambda i: (i, j))
    return _fused_call(
        body, comms, name="mix_fwd", grid=(t // tm,),
        inputs=(x, o_f, o_b, p, p, p, gla_g, ln_g, ln_b, w_sp, b_sp, w_out),
        in_specs=[pl.BlockSpec((tm, D_MODEL), lambda i: (i, 0)), half(0), half(0), half(2), half(3), half(4),
                  _const_spec((1, VAL_W)), _const_spec((1, GMLP_W)), _const_spec((1, GMLP_W)),
                  _const_spec((GMLP_GROUPS, GMLP_CHUNK, GMLP_CHUNK)), _const_spec((GMLP_GROUPS, GMLP_CHUNK, 1)),
                  _const_spec((D_MODEL, D_MODEL))],
        out_specs=(pl.BlockSpec((tm, D_MODEL), lambda i: (i, 0)), pl.BlockSpec((tm, D_MODEL), lambda i: (i, 0))),
        out_shape=(jax.ShapeDtypeStruct((t, D_MODEL), F32), jax.ShapeDtypeStruct((t, D_MODEL), BF16)),
        scratch_shapes=[pltpu.VMEM((tm, GMLP_W), F32)])


def _mix_bwd(dx1, ycat, o_f, o_b, p, gla_g, ln_g, ln_b, w_sp, b_sp, w_out, tm, comms=()):
    t = dx1.shape[0]
    nch = tm // GMLP_CHUNK

    def body(dx1_ref, y_ref, of_ref, ob_ref, pg_ref, pu_ref, pv_ref, gg_ref, lg_ref, lb_ref, ws_ref, bs_ref, wo_ref,
             do_ref, dpg_ref, dpu_ref, dpv_ref, dwo_ref, dgg_ref, dlg_ref, dlb_ref, dws_ref, dbs_ref,
             s_scr, dvln_scr):
        @pl.when(pl.program_id(0) == 0)
        def _():
            for ref in (dwo_ref, dgg_ref, dlg_ref, dlb_ref, dws_ref, dbs_ref):
                ref[...] = jnp.zeros_like(ref)

        dx1 = dx1_ref[...].astype(BF16)
        dycat = _mm_nt(dx1, wo_ref[...])
        dwo_ref[...] += _mm_tn(y_ref[...], dx1)
        dy_a = dycat[:, :VAL_W]
        dy_b = dycat[:, VAL_W:]
        on, r = _head_rms(of_ref[...] + ob_ref[...])
        pg = pg_ref[...].astype(F32)
        sil, dsil = _silu_and_grad(pg)
        gg = gg_ref[...]
        dgg_ref[...] += jnp.sum(dy_a * sil * on, axis=0, keepdims=True)
        don = dy_a * sil * gg
        prod = don * on
        means = jnp.concatenate(
            [jnp.broadcast_to(jnp.mean(prod[:, h * GLA_DV:(h + 1) * GLA_DV], axis=-1, keepdims=True),
                              (tm, GLA_DV)) for h in range(GLA_HEADS)], axis=1)
        do_ref[...] = (r * (don - on * means)).astype(BF16)
        dpg_ref[...] = (dy_a * on * gg * dsil).astype(BF16)
        pu = pu_ref[...].astype(F32)
        pv = pv_ref[...].astype(F32)
        zu, dzu_dpu = _gelu_and_grad(pu)
        zv, dzv_dpv = _gelu_and_grad(pv)
        vhat, rs = _layernorm_stats(zv)
        lg = lg_ref[...]
        vln = (vhat * lg + lb_ref[...]).astype(BF16)
        ds32 = dy_b * zu
        ds = ds32.astype(BF16)
        blocks = [(g, n) for g in range(GMLP_GROUPS) for n in range(nch)]
        at = lambda g, n: (slice(n * GMLP_CHUNK, (n + 1) * GMLP_CHUNK), slice(g * LANE, (g + 1) * LANE))
        w_sp = [ws_ref[g].astype(BF16) for g in range(GMLP_GROUPS)]
        v_blk = {b: vln[at(*b)] for b in blocks}
        ds_blk = {b: ds[at(*b)] for b in blocks}
        s_blk = {b: jnp.dot(w_sp[b[0]], v_blk[b], preferred_element_type=F32) for b in blocks}
        dw_blk = {b: _mm_nt(ds_blk[b], v_blk[b]) for b in blocks}
        dvln_blk = {b: _mm_tn(w_sp[b[0]], ds_blk[b]) for b in blocks}
        for b in blocks:
            s_scr[at(*b)] = s_blk[b] + bs_ref[b[0]]
            dvln_scr[at(*b)] = dvln_blk[b]
        for g in range(GMLP_GROUPS):
            dws_ref[g] += sum(dw_blk[(g, n)] for n in range(nch))
            dbs_ref[g] += sum(jnp.sum(ds32[at(g, n)], axis=-1, keepdims=True) for n in range(nch))
        dpu_ref[...] = (dy_b * s_scr[...] * dzu_dpu).astype(BF16)
        dvln = dvln_scr[...]
        dlg_ref[...] += jnp.sum(dvln * vhat, axis=0, keepdims=True)
        dlb_ref[...] += jnp.sum(dvln, axis=0, keepdims=True)
        dvhat = dvln * lg
        dzv = rs * (dvhat - jnp.mean(dvhat, axis=-1, keepdims=True)
                    - vhat * jnp.mean(dvhat * vhat, axis=-1, keepdims=True))
        dpv_ref[...] = (dzv * dzv_dpv).astype(BF16)

    half = lambda j: pl.BlockSpec((tm, VAL_W), lambda i: (i, j))
    full = pl.BlockSpec((tm, D_MODEL), lambda i: (i, 0))
    sp_shape = (GMLP_GROUPS, GMLP_CHUNK, GMLP_CHUNK)
    bs_shape = (GMLP_GROUPS, GMLP_CHUNK, 1)
    return _fused_call(
        body, comms, name="mix_bwd", grid=(t // tm,),
        inputs=(dx1, ycat, o_f, o_b, p, p, p, gla_g, ln_g, ln_b, w_sp, b_sp, w_out),
        in_specs=[full, full, half(0), half(0), half(2), half(3), half(4),
                  _const_spec((1, VAL_W)), _const_spec((1, GMLP_W)), _const_spec((1, GMLP_W)),
                  _const_spec(sp_shape), _const_spec(bs_shape), _const_spec((D_MODEL, D_MODEL))],
        out_specs=(half(0), half(0), half(0), half(0), _acc_spec((D_MODEL, D_MODEL)), _acc_spec((1, VAL_W)),
                   _acc_spec((1, GMLP_W)), _acc_spec((1, GMLP_W)), _acc_spec(sp_shape), _acc_spec(bs_shape)),
        out_shape=(jax.ShapeDtypeStruct((t, VAL_W), BF16),) * 4 + (
            jax.ShapeDtypeStruct((D_MODEL, D_MODEL), F32), jax.ShapeDtypeStruct((1, VAL_W), F32),
            jax.ShapeDtypeStruct((1, GMLP_W), F32), jax.ShapeDtypeStruct((1, GMLP_W), F32),
            jax.ShapeDtypeStruct(sp_shape, F32), jax.ShapeDtypeStruct(bs_shape, F32)),
        scratch_shapes=[pltpu.VMEM((tm, GMLP_W), F32), pltpu.VMEM((tm, GMLP_W), F32)])


def _rms_bwd(dy_scaled, xn, r):
    return r * (dy_scaled - xn * jnp.mean(dy_scaled * xn, axis=-1, keepdims=True))


def _ffn(x1, target, g2, gf, w_gate, w_up, w_down, tm):
    t = x1.shape[0]

    def body(x1_ref, tg_ref, g2_ref, gf_ref, wg_ref, wu_ref, wd_ref,
             dx1_ref, h2_ref, dgate_ref, dup_ref, act_ref, dx2_ref, loss_ref, dgf_ref, dg2_ref):
        @pl.when(pl.program_id(0) == 0)
        def _():
            for ref in (loss_ref, dgf_ref, dg2_ref):
                ref[...] = jnp.zeros_like(ref)

        x1v = x1_ref[...]
        g2v = g2_ref[...]
        gfv = gf_ref[...]
        r2 = lax.rsqrt(jnp.mean(x1v * x1v, axis=-1, keepdims=True) + EPS)
        xn1 = x1v * r2
        h2 = (xn1 * g2v).astype(BF16)
        h2_ref[...] = h2
        gate = _mm_nt(h2, wg_ref[...])
        up = _mm_nt(h2, wu_ref[...])
        sil, dsil = _silu_and_grad(gate)
        act = (sil * up).astype(BF16)
        act_ref[...] = act
        x2 = x1v + jnp.dot(act, wd_ref[...], preferred_element_type=F32)
        rf = lax.rsqrt(jnp.mean(x2 * x2, axis=-1, keepdims=True) + EPS)
        xn2 = x2 * rf
        err = xn2 * gfv - tg_ref[...]
        loss_ref[...] += 0.5 * jnp.sum(jnp.mean(err * err, axis=-1, keepdims=True))
        dy = err * (1.0 / D_MODEL)
        dgf_ref[...] += jnp.sum(dy * xn2, axis=0, keepdims=True)
        dx2 = _rms_bwd(dy * gfv, xn2, rf)
        dx2b = dx2.astype(BF16)
        dx2_ref[...] = dx2b
        dact = _mm_nt(dx2b, wd_ref[...])
        dgate = (dact * up * dsil).astype(BF16)
        dup = (dact * sil).astype(BF16)
        dgate_ref[...] = dgate
        dup_ref[...] = dup
        dh2 = _mm(dgate, wg_ref[...]) + _mm(dup, wu_ref[...])
        dg2_ref[...] += jnp.sum(dh2 * xn1, axis=0, keepdims=True)
        dx1_ref[...] = dx2 + _rms_bwd(dh2 * g2v, xn1, r2)

    row = lambda w: pl.BlockSpec((tm, w), lambda i: (i, 0))
    return pl.pallas_call(
        body, name="ffn_fwd_bwd", grid=(t // tm,),
        in_specs=[row(D_MODEL), row(D_MODEL), _const_spec((1, D_MODEL)), _const_spec((1, D_MODEL)),
                  _const_spec((D_FF, D_MODEL)), _const_spec((D_FF, D_MODEL)), _const_spec((D_FF, D_MODEL))],
        out_specs=(row(D_MODEL), row(D_MODEL), row(D_FF), row(D_FF), row(D_FF), row(D_MODEL),
                   _acc_spec((8, LANE)), _acc_spec((1, D_MODEL)), _acc_spec((1, D_MODEL))),
        out_shape=(jax.ShapeDtypeStruct((t, D_MODEL), F32), jax.ShapeDtypeStruct((t, D_MODEL), BF16),
                   jax.ShapeDtypeStruct((t, D_FF), BF16), jax.ShapeDtypeStruct((t, D_FF), BF16),
                   jax.ShapeDtypeStruct((t, D_FF), BF16), jax.ShapeDtypeStruct((t, D_MODEL), BF16),
                   jax.ShapeDtypeStruct((8, LANE), F32), jax.ShapeDtypeStruct((1, D_MODEL), F32),
                   jax.ShapeDtypeStruct((1, D_MODEL), F32)),
        compiler_params=_params(),
    )(x1, target, g2, gf, w_gate, w_up, w_down)


def _matmul_tn(a, b, tm, tk, name, comms=()):
    t, m = a.shape
    n = b.shape[1]

    def body(a_ref, b_ref, o_ref):
        @pl.when(pl.program_id(1) == 0)
        def _():
            o_ref[...] = jnp.zeros_like(o_ref)

        o_ref[...] += _mm_tn(a_ref[...], b_ref[...])

    (out,), comm_results = _fused_call(
        body, comms, name=name, grid=(m // tm, t // tk), inputs=(a, b),
        in_specs=[pl.BlockSpec((tk, tm), lambda j, k: (k, j)), pl.BlockSpec((tk, n), lambda j, k: (k, 0))],
        out_specs=(pl.BlockSpec((tm, n), lambda j, k: (j, 0)),),
        out_shape=(jax.ShapeDtypeStruct((m, n), F32),))
    return out, comm_results


def _in_proj_bwd(x, g1, dx1, dq_f, dq_b, dk_f, dk_b, dv_f, dv_b, dpg, dpu, dpv, dlr_f, dlr_b, w_main, tm, comms=()):
    t = x.shape[0]

    def body(x_ref, g_ref, dx1_ref, dqf, dqb, dkf, dkb, dvf, dvb, dg, du, dv, dlf, dlb, w_ref,
             dx_ref, dp_ref, dg1_ref):
        @pl.when(pl.program_id(0) == 0)
        def _():
            dg1_ref[...] = jnp.zeros_like(dg1_ref)

        both = lambda a, b: (a[...].astype(F32) + b[...].astype(F32)).astype(BF16)
        dp = jnp.concatenate([both(dqf, dqb), both(dkf, dkb), both(dvf, dvb), dg[...], du[...], dv[...],
                              both(dlf, dlb)], axis=1)
        dp_ref[...] = dp
        dh = sum(_mm(dp[:, c0:c0 + r1 - r0], w_ref[r0:r1, :]) for r0, r1, c0 in PROJ_ROWS)
        xv = x_ref[...]
        r = lax.rsqrt(jnp.mean(xv * xv, axis=-1, keepdims=True) + EPS)
        xn = xv * r
        dg1_ref[...] += jnp.sum(dh * xn, axis=0, keepdims=True)
        dx_ref[...] = dx1_ref[...] + _rms_bwd(dh * g_ref[...], xn, r)

    row = lambda w: pl.BlockSpec((tm, w), lambda i: (i, 0))
    return _fused_call(
        body, comms, name="in_proj_bwd", grid=(t // tm,),
        inputs=(x, g1, dx1, dq_f, dq_b, dk_f, dk_b, dv_f, dv_b, dpg, dpu, dpv, dlr_f, dlr_b, w_main),
        in_specs=[row(D_MODEL), _const_spec((1, D_MODEL)), row(D_MODEL), row(KEY_W), row(KEY_W), row(KEY_W),
                  row(KEY_W), row(VAL_W), row(VAL_W), row(VAL_W), row(VAL_W), row(VAL_W), row(LANE), row(LANE),
                  _const_spec((PROJ_W, D_MODEL))],
        out_specs=(row(D_MODEL), row(PROJ_PAD), _acc_spec((1, D_MODEL))),
        out_shape=(jax.ShapeDtypeStruct((t, D_MODEL), F32), jax.ShapeDtypeStruct((t, PROJ_PAD), BF16),
                   jax.ShapeDtypeStruct((1, D_MODEL), F32)))


def _adamw(w, g, m, v):
    m_new = ADAM_B1 * m + (1.0 - ADAM_B1) * g
    v_new = ADAM_B2 * v + (1.0 - ADAM_B2) * (g * g)
    m_hat = m_new / (1.0 - ADAM_B1 ** ADAM_STEP)
    v_hat = v_new / (1.0 - ADAM_B2 ** ADAM_STEP)
    delta = -ADAM_LR * (m_hat / (jnp.sqrt(v_hat) + ADAM_EPS) + ADAM_WD * w)
    return delta, m_new, v_new


def _adamw_shard(own, recv, w, m, v, tr, name):
    r, c = w.shape

    def body(own_ref, recv_ref, w_ref, m_ref, v_ref, g_ref, d_ref, nm_ref, nv_ref):
        g = own_ref[...]
        for k in range(3):
            g = g + recv_ref[k].astype(F32)
        g_ref[...] = g
        d_ref[...], nm_ref[...], nv_ref[...] = _adamw(w_ref[...], g, m_ref[...], v_ref[...])

    row = pl.BlockSpec((tr, c), lambda i: (i, 0))
    return pl.pallas_call(
        body, name=name, grid=(r // tr,),
        in_specs=[row, pl.BlockSpec((3, tr, c), lambda i: (0, i, 0)), row, row, row],
        out_specs=(row,) * 4, out_shape=(jax.ShapeDtypeStruct((r, c), F32),) * 4,
        compiler_params=_params(),
    )(own, recv, w, m, v)


def _adamw_small(entries):
    stacks = []
    for (g, _, _), _, _, _ in entries:
        if not any(g is s for s in stacks):
            stacks.append(g)
    where = [next(i for i, s in enumerate(stacks) if s is g) for (g, _, _), _, _, _ in entries]
    ns, ne = len(stacks), len(entries)

    def body(*refs):
        s_refs, wmv, outs = refs[:ns], refs[ns:ns + 3 * ne], refs[ns + 3 * ne:]
        for e, ((_, r0, nr), _, _, _) in enumerate(entries):
            grad = s_refs[where[e]][r0:r0 + nr, :]
            w_ref, m_ref, v_ref = wmv[3 * e:3 * e + 3]
            g_ref, d_ref, nm_ref, nv_ref = outs[4 * e:4 * e + 4]
            g_ref[...] = grad
            d_ref[...], nm_ref[...], nv_ref[...] = _adamw(w_ref[...], grad, m_ref[...], v_ref[...])

    results = pl.pallas_call(
        body, name="adamw_small",
        out_shape=tuple(jax.ShapeDtypeStruct(w.shape, F32) for _, w, _, _ in entries for _ in range(4)),
        compiler_params=pltpu.CompilerParams(vmem_limit_bytes=VMEM_LIMIT),
    )(*stacks, *[a for _, w, m, v in entries for a in (w, m, v)])
    return [results[4 * e:4 * e + 4] for e in range(ne)]


def _mesh_pos():
    return lax.axis_index("x"), lax.axis_index("y"), lax.axis_index("c")


def _other_chips(x, y):
    return [(x, 1 - y), (1 - x, y), (1 - x, 1 - y)]


_VMEM_WHOLE = pl.BlockSpec(memory_space=pltpu.VMEM)
_HBM_WHOLE = pl.BlockSpec(memory_space=pl.ANY)


def _gather_comm(shards, cast, mid=(1, 2)):
    na = len(shards)
    staged = [a for a in range(na) if cast[a]]

    def phases(in_refs, out_refs, scr):
        stage = dict(zip(staged, scr[:len(staged)]))
        send_sems, recv_sems, local_sems = scr[len(staged):]
        x, y, c = _mesh_pos()
        me, sibling = (x, y, c), (x, y, 1 - c)
        chips = _other_chips(x, y)
        srcs = [stage[a] if cast[a] else in_refs[a] for a in range(na)]

        def rows(a, pos):
            px, py, pc = pos
            return out_refs[a].at[4 * px + 2 * py + pc]

        def copy(a, k, block, to, src=None):
            return pltpu.make_async_remote_copy(
                src_ref=rows(a, block) if src is None else src, dst_ref=rows(a, block),
                send_sem=send_sems.at[a, k], recv_sem=recv_sems.at[a, k], device_id=to, device_id_type=MESH_ID)

        mine = [pltpu.make_async_copy(srcs[a], rows(a, me), local_sems.at[a]) for a in range(na)]
        first = []
        for a in range(na):
            first.append(copy(a, 0, me, sibling, src=srcs[a]))
            first += [copy(a, 1 + j, me, (*chip, c), src=srcs[a]) for j, chip in enumerate(chips)]
        passed = [copy(a, 4 + j, (*chip, c), sibling) for j, chip in enumerate(chips) for a in range(na)]

        def start():
            for a in staged:
                stage[a][...] = in_refs[a][...].astype(BF16)
            for cp in mine + first:
                cp.start()

        def forward():
            i = 0
            for j, chip in enumerate(chips):
                for a in range(na):
                    copy(a, 1 + j, (*chip, c), me).wait_recv()
                    passed[i].start()
                    i += 1

        def finish():
            for a in range(na):
                copy(a, 0, sibling, me).wait_recv()
                for j, chip in enumerate(chips):
                    copy(a, 4 + j, (*chip, 1 - c), me).wait_recv()
            for cp in first + passed:
                cp.wait_send()
            for cp in mine:
                cp.wait()

        return start, forward, finish

    def before(step, nsteps, in_refs, out_refs, scr):
        start, forward, _ = phases(in_refs, out_refs, scr)
        pl.when(step == 0)(start)
        pl.when(step == nsteps * mid[0] // mid[1])(forward)

    def after(step, nsteps, in_refs, out_refs, scr):
        pl.when(step == nsteps - 1)(phases(in_refs, out_refs, scr)[2])

    return _Comm(
        inputs=list(shards), in_specs=[_VMEM_WHOLE] * na,
        out_shape=[jax.ShapeDtypeStruct((N_DEV,) + s.shape, BF16 if cast[a] else s.dtype)
                   for a, s in enumerate(shards)],
        out_specs=[_HBM_WHOLE] * na,
        scratch_shapes=[pltpu.VMEM(shards[a].shape, BF16) for a in staged] + [
            pltpu.SemaphoreType.DMA((na, 7)), pltpu.SemaphoreType.DMA((na, 7)), pltpu.SemaphoreType.DMA((na,))],
        before=before, after=after)


def _exchange_comm(arrays, out_shape, make_copies):
    na = len(arrays)

    def copies(in_refs, out_refs, scr):
        return make_copies(in_refs, out_refs, *scr)

    def before(step, nsteps, in_refs, out_refs, scr):
        @pl.when(step == 0)
        def _():
            for cp in copies(in_refs, out_refs, scr):
                cp.start()

    def after(step, nsteps, in_refs, out_refs, scr):
        @pl.when(step == nsteps - 1)
        def _():
            for cp in copies(in_refs, out_refs, scr):
                cp.wait()

    return _Comm(inputs=list(arrays), in_specs=[_HBM_WHOLE] * na, out_shape=list(out_shape),
                 out_specs=[_HBM_WHOLE] * na,
                 scratch_shapes=[pltpu.SemaphoreType.DMA((na, 3)), pltpu.SemaphoreType.DMA((na, 3))],
                 before=before, after=after)


def _sibling_exchange_comm(grads):
    def make_copies(in_refs, out_refs, send_sems, recv_sems):
        x, y, c = _mesh_pos()
        return [pltpu.make_async_remote_copy(
            src_ref=in_refs[a].at[:, pl.ds(1 - c, 1)], dst_ref=out_refs[a], send_sem=send_sems.at[a, 0],
            recv_sem=recv_sems.at[a, 0], device_id=(x, y, 1 - c), device_id_type=MESH_ID)
            for a in range(len(grads))]

    return _exchange_comm(grads, [jax.ShapeDtypeStruct((4, 1) + g.shape[2:], F32) for g in grads], make_copies)


def _chips_exchange_comm(partials):
    def make_copies(in_refs, out_refs, send_sems, recv_sems):
        x, y, c = _mesh_pos()
        return [pltpu.make_async_remote_copy(
            src_ref=in_refs[a].at[j], dst_ref=out_refs[a].at[j], send_sem=send_sems.at[a, j],
            recv_sem=recv_sems.at[a, j], device_id=(*chip, c), device_id_type=MESH_ID)
            for a in range(len(partials)) for j, chip in enumerate(_other_chips(x, y))]

    return _exchange_comm(partials, [jax.ShapeDtypeStruct(g.shape, BF16) for g in partials], make_copies)


def _comm_only(comms, name):
    return _fused_call(lambda: None, comms, name=name, grid=(1,), inputs=(), in_specs=[], out_specs=(),
                       out_shape=())[1]


def _chip_sum(my_pos, mine, from_sibling, tr, name):
    _, _, r, c = mine.shape

    def body(pos_ref, a_ref, b_ref, own_ref, out_ref):
        s = a_ref[0, 0] + b_ref[0, 0]

        @pl.when(pl.program_id(1) == 0)
        def _():
            own_ref[...] = s

        @pl.when(pl.program_id(1) > 0)
        def _():
            out_ref[0] = s.astype(BF16)

    grid_spec = pltpu.PrefetchScalarGridSpec(
        num_scalar_prefetch=1, grid=(r // tr, 4),
        in_specs=[pl.BlockSpec((1, 1, tr, c), lambda i, k, pos: (pos[0] ^ k, pos[1], i, 0)),
                  pl.BlockSpec((1, 1, tr, c), lambda i, k, pos: (pos[0] ^ k, 0, i, 0))],
        out_specs=(pl.BlockSpec((tr, c), lambda i, k, pos: (i, 0)),
                   pl.BlockSpec((1, tr, c), lambda i, k, pos: (jnp.maximum(k - 1, 0), i, 0))))
    return pl.pallas_call(
        body, name=name, grid_spec=grid_spec,
        out_shape=(jax.ShapeDtypeStruct((r, c), F32), jax.ShapeDtypeStruct((3, r, c), BF16)),
        compiler_params=_params(2),
    )(my_pos, mine, from_sibling)


def _all_reduce_small_comm(parts):
    na = len(parts)

    def copies(in_refs, scr):
        gathered, (send_sems, recv_sems) = scr[:na], scr[na:]
        x, y, c = _mesh_pos()
        my_id = 4 * x + 2 * y + c
        return my_id, [pltpu.make_async_remote_copy(
            src_ref=in_refs[a], dst_ref=gathered[a].at[my_id], send_sem=send_sems.at[a, k - 1],
            recv_sem=recv_sems.at[a, k - 1], device_id=(x ^ (k >> 2), y ^ ((k >> 1) & 1), c ^ (k & 1)),
            device_id_type=MESH_ID) for a in range(na) for k in range(1, N_DEV)]

    def before(step, nsteps, in_refs, out_refs, scr):
        @pl.when(step == 0)
        def _():
            for cp in copies(in_refs, scr)[1]:
                cp.start()

    def after(step, nsteps, in_refs, out_refs, scr):
        @pl.when(step == nsteps - 1)
        def _():
            my_id, cps = copies(in_refs, scr)
            for a in range(na):
                scr[a][my_id] = in_refs[a][...]
            for cp in cps:
                cp.wait()
            for a in range(na):
                acc = scr[a][0]
                for d in range(1, N_DEV):
                    acc = acc + scr[a][d]
                out_refs[a][...] = acc

    return _Comm(inputs=list(parts), in_specs=[_VMEM_WHOLE] * na,
                 out_shape=[jax.ShapeDtypeStruct(p.shape, F32) for p in parts], out_specs=[_VMEM_WHOLE] * na,
                 scratch_shapes=[pltpu.VMEM((N_DEV,) + p.shape, F32) for p in parts] + [
                     pltpu.SemaphoreType.DMA((na, N_DEV - 1)), pltpu.SemaphoreType.DMA((na, N_DEV - 1))],
                 before=before, after=after)


def _unshard_cols(g):
    return jnp.transpose(g, (1, 0, 2)).reshape(g.shape[1], N_DEV * g.shape[2])


def _row_blocks(w):
    return w.reshape(4, 2, w.shape[0] // N_DEV, w.shape[1])


def _stack_rows(parts):
    a = jnp.concatenate(parts, axis=0)
    return jnp.pad(a, ((0, (-a.shape[0]) % 8), (0, 0)))


def _main_proj_grad(dw_main_t):
    return jnp.concatenate([dw_main_t[:LR_REF], dw_main_t[LR_COL:LR_COL + 2 * LOWRANK], dw_main_t[LR_REF:LR_COL]],
                           axis=0)


def _padded_decay_weights(wd_f, wd_b):
    zeros = lambda n: jnp.zeros((n, KEY_W), F32)
    return (jnp.concatenate([wd_f, zeros(LANE - LOWRANK)], axis=0),
            jnp.concatenate([zeros(LOWRANK), wd_b, zeros(LANE - 2 * LOWRANK)], axis=0))


def kernel(x, norm1_g, w_in,w_decay_f, b_decay_f, w_decay_b, b_decay_b, gla_norm_g, gmlp_ln_g, gmlp_ln_b, w_spatial, b_spatial, w_out, norm2_g, w_gate, w_up, w_down, final_norm_g, loss_target, m_norm1_g, m_w_in, m_w_decay_f, m_b_decay_f, m_w_decay_b, m_b_decay_b, m_gla_norm_g, m_gmlp_ln_g, m_gmlp_ln_b, m_w_spatial, m_b_spatial, m_w_out, m_norm2_g, m_w_gate, m_w_up, m_w_down, m_final_norm_g, v_norm1_g, v_w_in, v_w_decay_f, v_b_decay_f, v_w_decay_b, v_b_decay_b, v_gla_norm_g, v_gmlp_ln_g, v_gmlp_ln_b, v_w_spatial, v_b_spatial, v_w_out, v_norm2_g, v_w_gate, v_w_up, v_w_down, v_final_norm_g):
    t = x.shape[1]
    xt = x[0]
    target = loss_target[0]
    pos_x, pos_y, pos_c = _mesh_pos()
    my_pos = jnp.stack([2 * pos_x + pos_y, pos_c]).astype(jnp.int32)
    my_id = 4 * pos_x + 2 * pos_y + pos_c

    tile = lambda n: min(n, t)
    ln_g, ln_b, w_sp = gmlp_ln_g, gmlp_ln_b, w_spatial[0]
    b_sp_col = b_spatial[0][:, :, None]
    shard = {"w_in": w_in[0].T, "w_out": w_out[0], "w_gate": w_gate[0].T, "w_up": w_up[0].T, "w_down": w_down[0]}
    shard_m = {"w_in": m_w_in[0].T, "w_out": m_w_out[0], "w_gate": m_w_gate[0].T, "w_up": m_w_up[0].T,
               "w_down": m_w_down[0]}
    shard_v = {"w_in": v_w_in[0].T, "w_out": v_w_out[0], "w_gate": v_w_gate[0].T, "w_up": v_w_up[0].T,
               "w_down": v_w_down[0]}
    transposed = ("w_in", "w_gate", "w_up")
    chip_sum = lambda n, g, s: _chip_sum(my_pos, g, s[0], g.shape[2], "chip_sum_" + n)

    decay_shard = jnp.stack([w_decay_f[0], w_decay_b[0]])
    (hb,), ((g_in, g_decay),) = _norm1(xt, norm1_g, tile(512),
                                       [_gather_comm([shard["w_in"], decay_shard], [True, False])])
    w_in_t = g_in.reshape(PROJ_W, D_MODEL)
    wd_pad_f, wd_pad_b = _padded_decay_weights(_unshard_cols(g_decay[:, 0]), _unshard_cols(g_decay[:, 1]))
    (p,), ((g_gate,),) = _in_proj(hb, w_in_t, tile(512), [_gather_comm([shard["w_gate"]], [True], mid=(7, 8))])
    (o_f, st_f, o_b, st_b), ((g_up, g_out),) = _gla_fwd(
        p, wd_pad_f, b_decay_f, wd_pad_b, b_decay_b, tile(512),
        [_gather_comm([shard["w_up"], shard["w_out"]], [True, True], mid=(3, 4))])
    w_out_full = g_out.reshape(D_MODEL, D_MODEL)
    (x1, ycat), ((g_down,),) = _mix_fwd(
        xt, o_f, o_b, p, gla_norm_g, ln_g, ln_b, w_sp, b_sp_col, w_out_full, tile(512),
        [_gather_comm([shard["w_down"]], [True], mid=(7, 8))])

    dx1, h2b, dgate, dup, act, dx2, loss_acc, d_gf, d_g2 = _ffn(
        x1, target, norm2_g, final_norm_g[None, :], g_gate.reshape(D_FF, D_MODEL), g_up.reshape(D_FF, D_MODEL),
        g_down.reshape(D_FF, D_MODEL), tile(256))
    dw_gate, _ = _matmul_tn(dgate, h2b, D_FF // 2, tile(2048), "grad_w_gate")
    dw_up, _ = _matmul_tn(dup, h2b, D_FF // 2, tile(2048), "grad_w_up")
    dw_down, _ = _matmul_tn(act, dx2, D_FF // 2, tile(2048), "grad_w_down")

    ffn_grads = [_row_blocks(dw_gate), _row_blocks(dw_up), _row_blocks(dw_down)]
    (d_o, dpg, dpu, dpv, dw_out, d_gg, d_lg, d_lb, dw_sp, db_sp), (ffn_sib,) = _mix_bwd(
        dx1, ycat, o_f, o_b, p, gla_norm_g, ln_g, ln_b, w_sp, b_sp_col, w_out_full, tile(256),
        [_sibling_exchange_comm(ffn_grads)])
    ffn_names = ["w_gate", "w_up", "w_down"]
    ffn_sums = [chip_sum(n, g, [s]) for n, g, s in zip(ffn_names, ffn_grads, ffn_sib)]
    out_grad = _row_blocks(dw_out)
    (dq_f, dk_f, dv_f, dlr_f, dwd_f, dbd_f, dq_b, dk_b, dv_b, dlr_b, dwd_b, dbd_b), (ffn_recv, out_sib) = _gla_bwd(
        p, wd_pad_f, b_decay_f, wd_pad_b, b_decay_b, st_f, st_b, d_o, tile(512),
        [_chips_exchange_comm([s[1] for s in ffn_sums]), _sibling_exchange_comm([out_grad])])
    out_sum = chip_sum("w_out", out_grad, out_sib)
    (grad_x, dp, d_g1), _ = _in_proj_bwd(
        xt, norm1_g, dx1, dq_f, dq_b, dk_f, dk_b, dv_f, dv_b, dpg, dpu, dpv, dlr_f, dlr_b, w_in_t, tile(512))

    stacks = [_stack_rows([d_g1, d_g2, d_gf]), _stack_rows([d_gg, d_lg, d_lb]),
              _stack_rows([dbd_f, dbd_b, jnp.zeros((6, KEY_W), F32), dwd_f[:LOWRANK], dwd_b[LOWRANK:2 * LOWRANK]]),
              _stack_rows([dw_sp.reshape(GMLP_W, GMLP_CHUNK), db_sp[:, :, 0], loss_acc[:1]])]
    dw_main, (small_sums, out_recv) = _matmul_tn(
        dp, hb, PROJ_PAD // 3, tile(2048), "grad_w_in",
        [_all_reduce_small_comm(stacks), _chips_exchange_comm([out_sum[1]])])
    in_grad = _row_blocks(_main_proj_grad(dw_main))
    (in_sib,) = _comm_only([_sibling_exchange_comm([in_grad])], "grad_w_in_exchange_sibling")
    in_sum = chip_sum("w_in", in_grad, in_sib)
    (in_recv,) = _comm_only([_chips_exchange_comm([in_sum[1]])], "grad_w_in_exchange_chips")

    names = ["w_in", "w_out", "w_gate", "w_up", "w_down"]
    sums = [in_sum, out_sum] + ffn_sums
    received = [in_recv[0], out_recv[0]] + list(ffn_recv)
    big_out = {}
    for n, s, rc in zip(names, sums, received):
        res = _adamw_shard(s[0], rc, shard[n], shard_m[n], shard_v[n], shard[n].shape[0], "adamw_" + n)
        big_out[n] = [r.T if n in transposed else r for r in res]

    s1024, s512, s256, s128 = small_sums
    loss = s128[GMLP_W + GMLP_GROUPS, 0]
    col0 = my_id * (KEY_W // N_DEV)
    decay_cols = lambda row0: lax.dynamic_slice(s256, (row0, col0), (LOWRANK, KEY_W // N_DEV))
    flat = lambda a: a.reshape(-1, a.shape[-1])
    small = {
        "norm1_g": ((s1024, 0, 1), norm1_g, m_norm1_g, v_norm1_g),
        "w_decay_f": ((decay_cols(8), 0, LOWRANK), w_decay_f, m_w_decay_f, v_w_decay_f),
        "b_decay_f": ((s256, 0, 1), b_decay_f, m_b_decay_f, v_b_decay_f),
        "w_decay_b": ((decay_cols(8 + LOWRANK), 0, LOWRANK), w_decay_b, m_w_decay_b, v_w_decay_b),
        "b_decay_b": ((s256, 1, 1), b_decay_b, m_b_decay_b, v_b_decay_b),
        "gla_norm_g": ((s512, 0, 1), gla_norm_g, m_gla_norm_g, v_gla_norm_g),
        "gmlp_ln_g": ((s512, 1, 1), gmlp_ln_g, m_gmlp_ln_g, v_gmlp_ln_g),
        "gmlp_ln_b": ((s512, 2, 1), gmlp_ln_b, m_gmlp_ln_b, v_gmlp_ln_b),
        "w_spatial": ((s128, 0, GMLP_W), w_spatial, m_w_spatial, v_w_spatial),
        "b_spatial": ((s128, GMLP_W, GMLP_GROUPS), b_spatial, m_b_spatial, v_b_spatial),
        "norm2_g": ((s1024, 1, 1), norm2_g, m_norm2_g, v_norm2_g),
        "final_norm_g": ((s1024, 2, 1), final_norm_g, m_final_norm_g, v_final_norm_g),
    }
    small_res = _adamw_small([(g, flat(w), flat(m), flat(v)) for g, w, m, v in small.values()])
    small_out = {n: [r.reshape(small[n][1].shape) for r in res] for n, res in zip(small, small_res)}

    order = ["norm1_g", "w_in", "w_decay_f", "b_decay_f", "w_decay_b", "b_decay_b", "gla_norm_g", "gmlp_ln_g",
             "gmlp_ln_b", "w_spatial", "b_spatial", "w_out", "norm2_g", "w_gate", "w_up", "w_down", "final_norm_g"]
    outs = []
    for kind in range(4):
        for n in order:
            outs.append(big_out[n][kind][None] if n in big_out else small_out[n][kind])
    return (loss, grad_x[None], *outs)
```

```python
import functools
import math

import jax
import jax.numpy as jnp
from jax import lax
from jax.experimental import pallas as pl
from jax.experimental.pallas import tpu as pltpu

F32 = jnp.float32
BF16 = jnp.bfloat16

D_MODEL = 1024
GLA_HEADS = 4
GLA_DK = 64
GLA_DV = 128
KEY_W = GLA_HEADS * GLA_DK
VAL_W = GLA_HEADS * GLA_DV
LOWRANK = 16
GLA_TAU = 16.0
GLA_CHUNK = 64
GMLP_W = 512
GMLP_GROUPS = 4
GMLP_CHUNK = 128
D_FF = 2816
EPS = 1e-6
Q_SCALE = GLA_DK ** -0.5
PROJ_PAD = 2688
LR_COL = 2560
LANE = 128
N_DEV = 8

ADAM_LR = 0.001
ADAM_B1 = 0.9
ADAM_B2 = 0.999
ADAM_EPS = 1e-08
ADAM_WD = 0.01
ADAM_STEP = 10

VMEM_LIMIT = 56 * 1024 * 1024
MESH_ID = pl.DeviceIdType.MESH
INV_SQRT2 = 0.7071067811865476
INV_SQRT_2PI = 0.3989422804014327


def _params(n_axes=1):
    return pltpu.CompilerParams(dimension_semantics=("arbitrary",) * n_axes, vmem_limit_bytes=VMEM_LIMIT)


def _mm(a, b):
    return jnp.dot(a.astype(BF16), b.astype(BF16), preferred_element_type=F32)


def _mm_nt(a, b):
    return lax.dot_general(a.astype(BF16), b.astype(BF16), (((1,), (1,)), ((), ())), preferred_element_type=F32)


def _mm_tn(a, b):
    return lax.dot_general(a.astype(BF16), b.astype(BF16), (((0,), (0,)), ((), ())), preferred_element_type=F32)


def _const_spec(shape):
    nd = len(shape)
    return pl.BlockSpec(shape, lambda *_: (0,) * nd, pipeline_mode=pl.Buffered(1))


def _acc_spec(shape):
    nd = len(shape)
    return pl.BlockSpec(shape, lambda *_: (0,) * nd)


class _Comm:
    def __init__(self, inputs, in_specs, out_shape, out_specs, scratch_shapes, before, after):
        self.inputs, self.in_specs, self.out_shape, self.out_specs = inputs, in_specs, out_shape, out_specs
        self.scratch_shapes, self.before, self.after = scratch_shapes, before, after


def _fused_call(body, comms, *, name, grid, inputs, in_specs, out_specs, out_shape, scratch_shapes=()):
    n_in, n_out, n_scr = len(in_specs), len(out_specs), len(scratch_shapes)
    nsteps = math.prod(grid)
    sizes = [(len(c.inputs), len(c.out_shape), len(c.scratch_shapes)) for c in comms]

    def full_body(*refs):
        step = pl.program_id(0)
        for axis in range(1, len(grid)):
            step = step * grid[axis] + pl.program_id(axis)
        ins, rest = refs[:n_in], refs[n_in:]
        c_ins = []
        for ci, _, _ in sizes:
            c_ins.append(rest[:ci])
            rest = rest[ci:]
        outs, rest = rest[:n_out], rest[n_out:]
        c_outs = []
        for _, co, _ in sizes:
            c_outs.append(rest[:co])
            rest = rest[co:]
        scr, rest = rest[:n_scr], rest[n_scr:]
        c_scr = []
        for _, _, cs in sizes:
            c_scr.append(rest[:cs])
            rest = rest[cs:]
        for c, a, b, s in zip(comms, c_ins, c_outs, c_scr):
            c.before(step, nsteps, a, b, s)
        body(*ins, *outs, *scr)
        for c, a, b, s in zip(comms, c_ins, c_outs, c_scr):
            c.after(step, nsteps, a, b, s)

    results = pl.pallas_call(
        full_body, name=name, grid=grid,
        in_specs=list(in_specs) + [s for c in comms for s in c.in_specs],
        out_specs=tuple(out_specs) + tuple(s for c in comms for s in c.out_specs),
        out_shape=tuple(out_shape) + tuple(s for c in comms for s in c.out_shape),
        scratch_shapes=list(scratch_shapes) + [s for c in comms for s in c.scratch_shapes],
        compiler_params=_params(len(grid)),
    )(*inputs, *[a for c in comms for a in c.inputs])
    own, rest = results[:n_out], results[n_out:]
    comm_results = []
    for _, co, _ in sizes:
        comm_results.append(rest[:co])
        rest = rest[co:]
    return own, comm_results


def _gelu(x):
    return 0.5 * x * (1.0 + lax.erf(x * INV_SQRT2))


def _gelu_and_grad(x):
    cdf = 0.5 * (1.0 + lax.erf(x * INV_SQRT2))
    return x * cdf, cdf + x * jnp.exp(-0.5 * x * x) * INV_SQRT_2PI


def _sigmoid(x):
    return 0.5 + 0.5 * jnp.tanh(0.5 * x)


def _silu_and_grad(x):
    s = _sigmoid(x)
    return x * s, s * (1.0 + x * (1.0 - s))


def _norm1(x, g1, tm, comms=()):
    t = x.shape[0]

    def body(x_ref, g_ref, h_ref):
        xv = x_ref[...]
        r = lax.rsqrt(jnp.mean(xv * xv, axis=-1, keepdims=True) + EPS)
        h_ref[...] = (xv * r * g_ref[...]).astype(BF16)

    row = pl.BlockSpec((tm, D_MODEL), lambda i: (i, 0))
    return _fused_call(body, comms, name="norm1", grid=(t // tm,), inputs=(x, g1),
                       in_specs=[row, _const_spec((1, D_MODEL))], out_specs=(row,),
                       out_shape=(jax.ShapeDtypeStruct((t, D_MODEL), BF16),))


PROJ_W = 2592
LR_REF = 1536
PROJ_ROWS = ((0, LR_REF, 0), (LR_REF + 2 * LOWRANK, PROJ_W, LR_REF), (LR_REF, LR_REF + LANE, LR_COL))


def _in_proj(h, w_in_t, tm, comms=()):
    t = h.shape[0]

    def body(h_ref, w_ref, p_ref):
        hv = h_ref[...]
        for r0, r1, c0 in PROJ_ROWS:
            p_ref[:, c0:c0 + r1 - r0] = _mm_nt(hv, w_ref[r0:r1, :]).astype(BF16)

    return _fused_call(
        body, comms, name="in_proj", grid=(t // tm,), inputs=(h, w_in_t),
        in_specs=[pl.BlockSpec((tm, D_MODEL), lambda i: (i, 0)), _const_spec((PROJ_W, D_MODEL))],
        out_specs=(pl.BlockSpec((tm, PROJ_PAD), lambda i: (i, 0)),),
        out_shape=(jax.ShapeDtypeStruct((t, PROJ_PAD), BF16),))


def _tri(upper):
    r = lax.broadcasted_iota(jnp.int32, (GLA_CHUNK, GLA_CHUNK), 0)
    c = lax.broadcasted_iota(jnp.int32, (GLA_CHUNK, GLA_CHUNK), 1)
    return jnp.where((c >= r) if upper else (c <= r), 1.0, 0.0).astype(BF16)


def _chunk_cumsum(tri, a):
    hi = a.astype(BF16)
    lo = (a - hi.astype(F32)).astype(BF16)
    dot = functools.partial(jnp.dot, preferred_element_type=F32)
    return jnp.concatenate([dot(tri, hi[_chunk_rows(c)]) + dot(tri, lo[_chunk_rows(c)])
                            for c in range(a.shape[0] // GLA_CHUNK)], axis=0)


def _chunk_rows(c):
    return slice(c * GLA_CHUNK, (c + 1) * GLA_CHUNK)


def _gla_masks(rev):
    dk_bits, dv_bits = GLA_DK.bit_length() - 1, GLA_DV.bit_length() - 1
    key_head = lax.broadcasted_iota(jnp.int32, (GLA_CHUNK, KEY_W), 1) >> dk_bits
    val_head = lax.broadcasted_iota(jnp.int32, (GLA_CHUNK, VAL_W), 1) >> dv_bits
    t = lax.broadcasted_iota(jnp.int32, (GLA_HEADS * GLA_CHUNK, GLA_CHUNK), 0) & (GLA_CHUNK - 1)
    s = lax.broadcasted_iota(jnp.int32, (GLA_HEADS * GLA_CHUNK, GLA_CHUNK), 1)
    causal = (s >= t) if rev else (s <= t)
    state_head = lax.broadcasted_iota(jnp.int32, (GLA_DV, KEY_W), 1) >> dk_bits
    return key_head, val_head, causal, state_head


def _stack_heads(a, head_of_lane):
    a = a.astype(BF16)
    return jnp.concatenate([jnp.where(head_of_lane == h, a, jnp.zeros_like(a)) for h in range(GLA_HEADS)], axis=0)


def _rows_by_head(a):
    return jnp.concatenate([a[:, h * GLA_DV:(h + 1) * GLA_DV] for h in range(GLA_HEADS)], axis=0)


def _lanes_by_head(r):
    return jnp.concatenate([r[h * GLA_CHUNK:(h + 1) * GLA_CHUNK] for h in range(GLA_HEADS)], axis=1)


def _head_diagonal(r, head_of_lane):
    rows = r.shape[0] // GLA_HEADS
    out = jnp.where(head_of_lane == 0, r[:rows], 0.0)
    for h in range(1, GLA_HEADS):
        out = out + jnp.where(head_of_lane == h, r[h * rows:(h + 1) * rows], 0.0)
    return out


def _tile_terms(la, q, k, tri, rev):
    nc = la.shape[0] // GLA_CHUNK
    q, k = q.astype(F32), k.astype(F32)
    b = _chunk_cumsum(tri, la)
    ebl = [jnp.exp(b[c * GLA_CHUNK:c * GLA_CHUNK + 1] if rev else b[(c + 1) * GLA_CHUNK - 1:(c + 1) * GLA_CHUNK])
           for c in range(nc)]
    eb = jnp.exp(b)
    enb = jnp.exp(-b)
    ee = enb * jnp.concatenate([jnp.broadcast_to(row, (GLA_CHUNK, KEY_W)) for row in ebl], axis=0)
    return ebl, eb, enb, ee, q * Q_SCALE * eb, k * enb, k * ee


def _log_decay(lr_ref, wd_ref, bd_ref):
    z = _mm(lr_ref[...], wd_ref[...]) + bd_ref[...]
    return z, jax.nn.log_sigmoid(z) * (1.0 / GLA_TAU)


def _p_specs(tg, tile):
    return [pl.BlockSpec((tg, KEY_W), lambda i: (tile(i), 0)),
            pl.BlockSpec((tg, KEY_W), lambda i: (tile(i), 1)),
            pl.BlockSpec((tg, VAL_W), lambda i: (tile(i), 1)),
            pl.BlockSpec((tg, LANE), lambda i: (tile(i), LR_COL // LANE))]


def _gla_fwd_dir(rev, nc, q_ref, k_ref, v_ref, lr_ref, wd_ref, bd_ref, o_ref, st_ref, state):
    key_head, _, causal, state_head = _gla_masks(rev)
    order = range(nc - 1, -1, -1) if rev else range(nc)

    def intra():
        _, la = _log_decay(lr_ref, wd_ref, bd_ref)
        ebl, _, _, _, qd, kd, ke = _tile_terms(la, q_ref[...], k_ref[...], _tri(rev), rev)
        kd = kd.astype(BF16)
        v = {c: v_ref[_chunk_rows(c), :].astype(BF16) for c in order}
        qd_stack = {c: _stack_heads(qd[_chunk_rows(c)], key_head) for c in order}
        ke_stack = {c: _stack_heads(ke[_chunk_rows(c)], key_head) for c in order}
        a_all = {c: _mm_nt(qd_stack[c], kd[_chunk_rows(c)]) for c in order}
        a_all = {c: jnp.where(causal, a_all[c], 0.0).astype(BF16) for c in order}
        r = {c: _mm(a_all[c], v[c]) for c in order}
        upd = {c: _mm_tn(_rows_by_head(v[c]), ke_stack[c]) for c in order}
        return {c: (ebl[c], qd_stack[c], r[c], upd[c]) for c in order}

    def scan(terms):
        st = state[...]
        states = {}
        for c in order:
            states[c] = st
            st_ref[c] = st.astype(BF16)
            st = st * terms[c][0] + terms[c][3]
        state[...] = st
        return states

    def inter(terms, states):
        r_inter = {c: _mm_nt(terms[c][1], states[c]) for c in order}
        for c in order:
            r = terms[c][2]
            o_ref[_chunk_rows(c), :] = jnp.concatenate(
                [r[h * GLA_CHUNK:(h + 1) * GLA_CHUNK, h * GLA_DV:(h + 1) * GLA_DV]
                 + r_inter[c][h * GLA_CHUNK:(h + 1) * GLA_CHUNK] for h in range(GLA_HEADS)], axis=1)

    return intra, scan, inter


def _gla_fwd(p, wd_pad_f, bd_f, wd_pad_b, bd_b, tg, comms=()):
    t = p.shape[0]
    nt = t // tg
    nc = tg // GLA_CHUNK
    up, down = (lambda i: i), (lambda i: nt - 1 - i)

    def body(qf, kf, vf, lrf, qb, kb, vb, lrb, wdf, bdf, wdb, bdb, of, stf, ob, stb, state_f, state_b):
        @pl.when(pl.program_id(0) == 0)
        def _():
            state_f[...] = jnp.zeros_like(state_f)
            state_b[...] = jnp.zeros_like(state_b)

        dirs = [_gla_fwd_dir(False, nc, qf, kf, vf, lrf, wdf, bdf, of, stf, state_f),
                _gla_fwd_dir(True, nc, qb, kb, vb, lrb, wdb, bdb, ob, stb, state_b)]
        terms = [intra() for intra, _, _ in dirs]
        states = [scan(t) for (_, scan, _), t in zip(dirs, terms)]
        for (_, _, inter), t, s in zip(dirs, terms, states):
            inter(t, s)

    wd_spec, bd_spec = _const_spec((LANE, KEY_W)), _const_spec((1, KEY_W))
    outs = lambda tile: (pl.BlockSpec((tg, VAL_W), lambda i: (tile(i), 0)),
                         pl.BlockSpec((nc, GLA_DV, KEY_W), lambda i: (tile(i), 0, 0)))
    out_shape = (jax.ShapeDtypeStruct((t, VAL_W), F32), jax.ShapeDtypeStruct((t // GLA_CHUNK, GLA_DV, KEY_W), BF16))
    return _fused_call(
        body, comms, name="gla_fwd", grid=(nt,), inputs=(p,) * 8 + (wd_pad_f, bd_f, wd_pad_b, bd_b),
        in_specs=_p_specs(tg, up) + _p_specs(tg, down) + [wd_spec, bd_spec, wd_spec, bd_spec],
        out_specs=outs(up) + outs(down), out_shape=out_shape * 2,
        scratch_shapes=[pltpu.VMEM((GLA_DV, KEY_W), F32)] * 2)


def _gla_bwd_dir(rev, nc, q_ref, k_ref, v_ref, lr_ref, wd_ref, bd_ref, st_ref, do_ref,
                 dq_ref, dk_ref, dv_ref, dlr_ref, dwd_ref, dbd_ref, dstate):
    key_head, val_head, causal, state_head = _gla_masks(rev)
    order = range(nc) if rev else range(nc - 1, -1, -1)
    tg = nc * GLA_CHUNK

    def intra():
        z, la = _log_decay(lr_ref, wd_ref, bd_ref)
        tile = _tile_terms(la, q_ref[...], k_ref[...], _tri(rev), rev)
        qd, kd = tile[4], tile[5].astype(BF16)
        v = {c: v_ref[_chunk_rows(c), :].astype(BF16) for c in order}
        d_o = {c: do_ref[_chunk_rows(c), :] for c in order}
        kd_c = {c: kd[_chunk_rows(c)] for c in order}
        qd_stack = {c: _stack_heads(qd[_chunk_rows(c)], key_head) for c in order}
        do_stack = {c: _stack_heads(d_o[c], val_head) for c in order}
        do_rows = {c: _rows_by_head(d_o[c]) for c in order}
        a_all = {c: _mm_nt(qd_stack[c], kd_c[c]) for c in order}
        da_all = {c: _mm_nt(do_stack[c], v[c]) for c in order}
        a_all = {c: jnp.where(causal, a_all[c], 0.0).astype(BF16) for c in order}
        da_all = {c: jnp.where(causal, da_all[c], 0.0).astype(BF16) for c in order}
        dv = {c: _mm_tn(a_all[c], do_stack[c]) for c in order}
        dqd = {c: _mm(da_all[c], kd_c[c]) + _mm(do_rows[c], st_ref[c]) for c in order}
        dkd = {c: _mm_tn(da_all[c], qd_stack[c]) for c in order}
        upd = {c: _mm_tn(do_rows[c], qd_stack[c]) for c in order}
        dqd = {c: _head_diagonal(dqd[c], key_head) for c in order}
        return z, tile, {c: dict(dv=dv[c], dqd=dqd[c], dkd=dkd[c], upd=upd[c]) for c in order}

    def scan(tile, per):
        dst = dstate[...]
        dsts = {}
        for c in order:
            dsts[c] = dst
            dst = dst * tile[0][c] + per[c]["upd"]
        dstate[...] = dst
        return dsts

    def inter(z, tile, per, dsts):
        ebl, eb, enb, ee, qd, kd, ke = tile
        ke_stack = {c: _stack_heads(ke[_chunk_rows(c)], key_head) for c in order}
        v_rows = {c: _rows_by_head(v_ref[_chunk_rows(c), :].astype(BF16)) for c in order}
        dst_b = {c: dsts[c].astype(BF16) for c in order}
        dv_state = {c: _mm_nt(ke_stack[c], dst_b[c]) for c in order}
        dke_c = {c: _mm(v_rows[c], dst_b[c]) for c in order}
        dke_c = {c: _head_diagonal(dke_c[c], key_head) for c in order}
        dbl_c = {}
        for c in order:
            rows = _chunk_rows(c)
            dv_ref[rows, :] = (per[c]["dv"] + _lanes_by_head(dv_state[c])).astype(BF16)
            dbl = (jnp.sum(dsts[c] * st_ref[c].astype(F32), axis=0, keepdims=True) * ebl[c]
                   + jnp.sum(dke_c[c] * ke[rows], axis=0, keepdims=True))
            dbl_c[c] = jnp.broadcast_to(dbl, (GLA_CHUNK, KEY_W))
        tile_of = lambda parts: jnp.concatenate([parts[c] for c in range(nc)], axis=0)
        dqd, dkd = tile_of({c: per[c]["dqd"] for c in order}), tile_of({c: per[c]["dkd"] for c in order})
        dke, dbl = tile_of(dke_c), tile_of(dbl_c)
        dq_ref[...] = (dqd * eb * Q_SCALE).astype(BF16)
        dk_ref[...] = (dkd * enb + dke * ee).astype(BF16)
        db = dqd * qd - dkd * kd - dke * ke
        dz = (_chunk_cumsum(_tri(not rev), db) + dbl) * (_sigmoid(-z) * (1.0 / GLA_TAU))
        dlr_ref[...] = _mm_nt(dz, wd_ref[...]).astype(BF16)
        dwd_ref[...] += _mm_tn(lr_ref[...], dz)
        dbd_ref[...] += jnp.sum(dz, axis=0, keepdims=True)

    return intra, scan, inter


def _gla_bwd(p, wd_pad_f, bd_f, wd_pad_b, bd_b, st_f, st_b, d_o, tg, comms=()):
    t = p.shape[0]
    nt = t // tg
    nc = tg // GLA_CHUNK
    up, down = (lambda i: i), (lambda i: nt - 1 - i)

    def body(qf, kf, vf, lrf, stf, dof, qb, kb, vb, lrb, stb, dob, wdf, bdf, wdb, bdb,
             dqf, dkf, dvf, dlrf, dwdf, dbdf, dqb, dkb, dvb, dlrb, dwdb, dbdb, dstate_f, dstate_b):
        @pl.when(pl.program_id(0) == 0)
        def _():
            for ref in (dstate_f, dstate_b, dwdf, dbdf, dwdb, dbdb):
                ref[...] = jnp.zeros_like(ref)

        dirs = [_gla_bwd_dir(False, nc, qf, kf, vf, lrf, wdf, bdf, stf, dof, dqf, dkf, dvf, dlrf, dwdf, dbdf,
                             dstate_f),
                _gla_bwd_dir(True, nc, qb, kb, vb, lrb, wdb, bdb, stb, dob, dqb, dkb, dvb, dlrb, dwdb, dbdb,
                             dstate_b)]
        first = [intra() for intra, _, _ in dirs]
        dsts = [scan(tile, per) for (_, scan, _), (_, tile, per) in zip(dirs, first)]
        for (_, _, inter), (z, tile, per), d in zip(dirs, first, dsts):
            inter(z, tile, per, d)

    wd_spec, bd_spec = _const_spec((LANE, KEY_W)), _const_spec((1, KEY_W))
    ins = lambda tile: _p_specs(tg, tile) + [pl.BlockSpec((nc, GLA_DV, KEY_W), lambda i: (tile(i), 0, 0)),
                                             pl.BlockSpec((tg, VAL_W), lambda i: (tile(i), 0))]
    outs = lambda tile: (pl.BlockSpec((tg, KEY_W), lambda i: (tile(i), 0)),
                         pl.BlockSpec((tg, KEY_W), lambda i: (tile(i), 0)),
                         pl.BlockSpec((tg, VAL_W), lambda i: (tile(i), 0)),
                         pl.BlockSpec((tg, LANE), lambda i: (tile(i), 0)),
                         _acc_spec((LANE, KEY_W)), _acc_spec((1, KEY_W)))
    out_shape = (jax.ShapeDtypeStruct((t, KEY_W), BF16), jax.ShapeDtypeStruct((t, KEY_W), BF16),
                 jax.ShapeDtypeStruct((t, VAL_W), BF16), jax.ShapeDtypeStruct((t, LANE), BF16),
                 jax.ShapeDtypeStruct((LANE, KEY_W), F32), jax.ShapeDtypeStruct((1, KEY_W), F32))
    scratch = [pltpu.VMEM((GLA_DV, KEY_W), F32)]
    return _fused_call(
        body, comms, name="gla_bwd", grid=(nt,),
        inputs=(p, p, p, p, st_f, d_o, p, p, p, p, st_b, d_o, wd_pad_f, bd_f, wd_pad_b, bd_b),
        in_specs=ins(down) + ins(up) + [wd_spec, bd_spec, wd_spec, bd_spec],
        out_specs=outs(down) + outs(up), out_shape=out_shape * 2, scratch_shapes=scratch * 2)


def _head_rms(o):
    parts, scales = [], []
    for h in range(GLA_HEADS):
        oh = o[:, h * GLA_DV:(h + 1) * GLA_DV]
        r = lax.rsqrt(jnp.mean(oh * oh, axis=-1, keepdims=True) + EPS)
        parts.append(oh * r)
        scales.append(jnp.broadcast_to(r, oh.shape))
    return jnp.concatenate(parts, axis=1), jnp.concatenate(scales, axis=1)


def _layernorm_stats(zv):
    mu = jnp.mean(zv, axis=-1, keepdims=True)
    xc = zv - mu
    rs = lax.rsqrt(jnp.mean(xc * xc, axis=-1, keepdims=True) + EPS)
    return xc * rs, rs


def _mix_fwd(x, o_f, o_b, p, gla_g, ln_g, ln_b, w_sp, b_sp, w_out, tm, comms=()):
    t = x.shape[0]
    nch = tm // GMLP_CHUNK

    def body(x_ref, of_ref, ob_ref, pg_ref, pu_ref, pv_ref, gg_ref, lg_ref, lb_ref, ws_ref, bs_ref, wo_ref,
             x1_ref, y_ref, s_scr):
        on, _ = _head_rms(of_ref[...] + ob_ref[...])
        pg = pg_ref[...].astype(F32)
        y_a = on * gg_ref[...] * (pg * _sigmoid(pg))
        zu = _gelu(pu_ref[...].astype(F32))
        vhat, _ = _layernorm_stats(_gelu(pv_ref[...].astype(F32)))
        vln = (vhat * lg_ref[...] + lb_ref[...]).astype(BF16)
        for g in range(GMLP_GROUPS):
            w_g = ws_ref[g].astype(BF16)
            b_g = bs_ref[g]
            cols = slice(g * LANE, (g + 1) * LANE)
            for n in range(nch):
                rows = slice(n * GMLP_CHUNK, (n + 1) * GMLP_CHUNK)
                s_scr[rows, cols] = jnp.dot(w_g, vln[rows, cols], preferred_element_type=F32) + b_g
        ycat = jnp.concatenate([y_a, zu * s_scr[...]], axis=1).astype(BF16)
        y_ref[...] = ycat
        x1_ref[...] = x_ref[...] + jnp.dot(ycat, wo_ref[...], preferred_element_type=F32)

    half = lambda j: pl.BlockSpec((tm, VAL_W), lambda i: (i, j))
    return _fused_call(
        body, comms, name="mix_fwd", grid=(t // tm,),
        inputs=(x, o_f, o_b, p, p, p, gla_g, ln_g, ln_b, w_sp, b_sp, w_out),
        in_specs=[pl.BlockSpec((tm, D_MODEL), lambda i: (i, 0)), half(0), half(0), half(2), half(3), half(4),
                  _const_spec((1, VAL_W)), _const_spec((1, GMLP_W)), _const_spec((1, GMLP_W)),
                  _const_spec((GMLP_GROUPS, GMLP_CHUNK, GMLP_CHUNK)), _const_spec((GMLP_GROUPS, GMLP_CHUNK, 1)),
                  _const_spec((D_MODEL, D_MODEL))],
        out_specs=(pl.BlockSpec((tm, D_MODEL), lambda i: (i, 0)), pl.BlockSpec((tm, D_MODEL), lambda i: (i, 0))),
        out_shape=(jax.ShapeDtypeStruct((t, D_MODEL), F32), jax.ShapeDtypeStruct((t, D_MODEL), BF16)),
        scratch_shapes=[pltpu.VMEM((tm, GMLP_W), F32)])


def _mix_bwd(dx1, ycat, o_f, o_b, p, gla_g, ln_g, ln_b, w_sp, b_sp, w_out, tm, comms=()):
    t = dx1.shape[0]
    nch = tm // GMLP_CHUNK

    def body(dx1_ref, y_ref, of_ref, ob_ref, pg_ref, pu_ref, pv_ref, gg_ref, lg_ref, lb_ref, ws_ref, bs_ref, wo_ref,
             do_ref, dpg_ref, dpu_ref, dpv_ref, dwo_ref, dgg_ref, dlg_ref, dlb_ref, dws_ref, dbs_ref,
             s_scr, dvln_scr):
        @pl.when(pl.program_id(0) == 0)
        def _():
            for ref in (dwo_ref, dgg_ref, dlg_ref, dlb_ref, dws_ref, dbs_ref):
                ref[...] = jnp.zeros_like(ref)

        dx1 = dx1_ref[...].astype(BF16)
        dycat = _mm_nt(dx1, wo_ref[...])
        dwo_ref[...] += _mm_tn(y_ref[...], dx1)
        dy_a = dycat[:, :VAL_W]
        dy_b = dycat[:, VAL_W:]
        on, r = _head_rms(of_ref[...] + ob_ref[...])
        pg = pg_ref[...].astype(F32)
        sil, dsil = _silu_and_grad(pg)
        gg = gg_ref[...]
        dgg_ref[...] += jnp.sum(dy_a * sil * on, axis=0, keepdims=True)
        don = dy_a * sil * gg
        prod = don * on
        means = jnp.concatenate(
            [jnp.broadcast_to(jnp.mean(prod[:, h * GLA_DV:(h + 1) * GLA_DV], axis=-1, keepdims=True),
                              (tm, GLA_DV)) for h in range(GLA_HEADS)], axis=1)
        do_ref[...] = (r * (don - on * means)).astype(BF16)
        dpg_ref[...] = (dy_a * on * gg * dsil).astype(BF16)
        pu = pu_ref[...].astype(F32)
        pv = pv_ref[...].astype(F32)
        zu, dzu_dpu = _gelu_and_grad(pu)
        zv, dzv_dpv = _gelu_and_grad(pv)
        vhat, rs = _layernorm_stats(zv)
        lg = lg_ref[...]
        vln = (vhat * lg + lb_ref[...]).astype(BF16)
        ds32 = dy_b * zu
        ds = ds32.astype(BF16)
        blocks = [(g, n) for g in range(GMLP_GROUPS) for n in range(nch)]
        at = lambda g, n: (slice(n * GMLP_CHUNK, (n + 1) * GMLP_CHUNK), slice(g * LANE, (g + 1) * LANE))
        w_sp = [ws_ref[g].astype(BF16) for g in range(GMLP_GROUPS)]
        v_blk = {b: vln[at(*b)] for b in blocks}
        ds_blk = {b: ds[at(*b)] for b in blocks}
        s_blk = {b: jnp.dot(w_sp[b[0]], v_blk[b], preferred_element_type=F32) for b in blocks}
        dw_blk = {b: _mm_nt(ds_blk[b], v_blk[b]) for b in blocks}
        dvln_blk = {b: _mm_tn(w_sp[b[0]], ds_blk[b]) for b in blocks}
        for b in blocks:
            s_scr[at(*b)] = s_blk[b] + bs_ref[b[0]]
            dvln_scr[at(*b)] = dvln_blk[b]
        for g in range(GMLP_GROUPS):
            dws_ref[g] += sum(dw_blk[(g, n)] for n in range(nch))
            dbs_ref[g] += sum(jnp.sum(ds32[at(g, n)], axis=-1, keepdims=True) for n in range(nch))
        dpu_ref[...] = (dy_b * s_scr[...] * dzu_dpu).astype(BF16)
        dvln = dvln_scr[...]
        dlg_ref[...] += jnp.sum(dvln * vhat, axis=0, keepdims=True)
        dlb_ref[...] += jnp.sum(dvln, axis=0, keepdims=True)
        dvhat = dvln * lg
        dzv = rs * (dvhat - jnp.mean(dvhat, axis=-1, keepdims=True)
                    - vhat * jnp.mean(dvhat * vhat, axis=-1, keepdims=True))
        dpv_ref[...] = (dzv * dzv_dpv).astype(BF16)

    half = lambda j: pl.BlockSpec((tm, VAL_W), lambda i: (i, j))
    full = pl.BlockSpec((tm, D_MODEL), lambda i: (i, 0))
    sp_shape = (GMLP_GROUPS, GMLP_CHUNK, GMLP_CHUNK)
    bs_shape = (GMLP_GROUPS, GMLP_CHUNK, 1)
    return _fused_call(
        body, comms, name="mix_bwd", grid=(t // tm,),
        inputs=(dx1, ycat, o_f, o_b, p, p, p, gla_g, ln_g, ln_b, w_sp, b_sp, w_out),
        in_specs=[full, full, half(0), half(0), half(2), half(3), half(4),
                  _const_spec((1, VAL_W)), _const_spec((1, GMLP_W)), _const_spec((1, GMLP_W)),
                  _const_spec(sp_shape), _const_spec(bs_shape), _const_spec((D_MODEL, D_MODEL))],
        out_specs=(half(0), half(0), half(0), half(0), _acc_spec((D_MODEL, D_MODEL)), _acc_spec((1, VAL_W)),
                   _acc_spec((1, GMLP_W)), _acc_spec((1, GMLP_W)), _acc_spec(sp_shape), _acc_spec(bs_shape)),
        out_shape=(jax.ShapeDtypeStruct((t, VAL_W), BF16),) * 4 + (
            jax.ShapeDtypeStruct((D_MODEL, D_MODEL), F32), jax.ShapeDtypeStruct((1, VAL_W), F32),
            jax.ShapeDtypeStruct((1, GMLP_W), F32), jax.ShapeDtypeStruct((1, GMLP_W), F32),
            jax.ShapeDtypeStruct(sp_shape, F32), jax.ShapeDtypeStruct(bs_shape, F32)),
        scratch_shapes=[pltpu.VMEM((tm, GMLP_W), F32), pltpu.VMEM((tm, GMLP_W), F32)])


def _rms_bwd(dy_scaled, xn, r):
    return r * (dy_scaled - xn * jnp.mean(dy_scaled * xn, axis=-1, keepdims=True))


def _ffn(x1, target, g2, gf, w_gate, w_up, w_down, tm):
    t = x1.shape[0]

    def body(x1_ref, tg_ref, g2_ref, gf_ref, wg_ref, wu_ref, wd_ref,
             dx1_ref, h2_ref, dgate_ref, dup_ref, act_ref, dx2_ref, loss_ref, dgf_ref, dg2_ref):
        @pl.when(pl.program_id(0) == 0)
        def _():
            for ref in (loss_ref, dgf_ref, dg2_ref):
                ref[...] = jnp.zeros_like(ref)

        x1v = x1_ref[...]
        g2v = g2_ref[...]
        gfv = gf_ref[...]
        r2 = lax.rsqrt(jnp.mean(x1v * x1v, axis=-1, keepdims=True) + EPS)
        xn1 = x1v * r2
        h2 = (xn1 * g2v).astype(BF16)
        h2_ref[...] = h2
        gate = _mm_nt(h2, wg_ref[...])
        up = _mm_nt(h2, wu_ref[...])
        sil, dsil = _silu_and_grad(gate)
        act = (sil * up).astype(BF16)
        act_ref[...] = act
        x2 = x1v + jnp.dot(act, wd_ref[...], preferred_element_type=F32)
        rf = lax.rsqrt(jnp.mean(x2 * x2, axis=-1, keepdims=True) + EPS)
        xn2 = x2 * rf
        err = xn2 * gfv - tg_ref[...]
        loss_ref[...] += 0.5 * jnp.sum(jnp.mean(err * err, axis=-1, keepdims=True))
        dy = err * (1.0 / D_MODEL)
        dgf_ref[...] += jnp.sum(dy * xn2, axis=0, keepdims=True)
        dx2 = _rms_bwd(dy * gfv, xn2, rf)
        dx2b = dx2.astype(BF16)
        dx2_ref[...] = dx2b
        dact = _mm_nt(dx2b, wd_ref[...])
        dgate = (dact * up * dsil).astype(BF16)
        dup = (dact * sil).astype(BF16)
        dgate_ref[...] = dgate
        dup_ref[...] = dup
        dh2 = _mm(dgate, wg_ref[...]) + _mm(dup, wu_ref[...])
        dg2_ref[...] += jnp.sum(dh2 * xn1, axis=0, keepdims=True)
        dx1_ref[...] = dx2 + _rms_bwd(dh2 * g2v, xn1, r2)

    row = lambda w: pl.BlockSpec((tm, w), lambda i: (i, 0))
    return pl.pallas_call(
        body, name="ffn_fwd_bwd", grid=(t // tm,),
        in_specs=[row(D_MODEL), row(D_MODEL), _const_spec((1, D_MODEL)), _const_spec((1, D_MODEL)),
                  _const_spec((D_FF, D_MODEL)), _const_spec((D_FF, D_MODEL)), _const_spec((D_FF, D_MODEL))],
        out_specs=(row(D_MODEL), row(D_MODEL), row(D_FF), row(D_FF), row(D_FF), row(D_MODEL),
                   _acc_spec((8, LANE)), _acc_spec((1, D_MODEL)), _acc_spec((1, D_MODEL))),
        out_shape=(jax.ShapeDtypeStruct((t, D_MODEL), F32), jax.ShapeDtypeStruct((t, D_MODEL), BF16),
                   jax.ShapeDtypeStruct((t, D_FF), BF16), jax.ShapeDtypeStruct((t, D_FF), BF16),
                   jax.ShapeDtypeStruct((t, D_FF), BF16), jax.ShapeDtypeStruct((t, D_MODEL), BF16),
                   jax.ShapeDtypeStruct((8, LANE), F32), jax.ShapeDtypeStruct((1, D_MODEL), F32),
                   jax.ShapeDtypeStruct((1, D_MODEL), F32)),
        compiler_params=_params(),
    )(x1, target, g2, gf, w_gate, w_up, w_down)


def _matmul_tn(a, b, tm, tk, name, comms=()):
    t, m = a.shape
    n = b.shape[1]

    def body(a_ref, b_ref, o_ref):
        @pl.when(pl.program_id(1) == 0)
        def _():
            o_ref[...] = jnp.zeros_like(o_ref)

        o_ref[...] += _mm_tn(a_ref[...], b_ref[...])

    (out,), comm_results = _fused_call(
        body, comms, name=name, grid=(m // tm, t // tk), inputs=(a, b),
        in_specs=[pl.BlockSpec((tk, tm), lambda j, k: (k, j)), pl.BlockSpec((tk, n), lambda j, k: (k, 0))],
        out_specs=(pl.BlockSpec((tm, n), lambda j, k: (j, 0)),),
        out_shape=(jax.ShapeDtypeStruct((m, n), F32),))
    return out, comm_results


def _in_proj_bwd(x, g1, dx1, dq_f, dq_b, dk_f, dk_b, dv_f, dv_b, dpg, dpu, dpv, dlr_f, dlr_b, w_main, tm, comms=()):
    t = x.shape[0]

    def body(x_ref, g_ref, dx1_ref, dqf, dqb, dkf, dkb, dvf, dvb, dg, du, dv, dlf, dlb, w_ref,
             dx_ref, dp_ref, dg1_ref):
        @pl.when(pl.program_id(0) == 0)
        def _():
            dg1_ref[...] = jnp.zeros_like(dg1_ref)

        both = lambda a, b: (a[...].astype(F32) + b[...].astype(F32)).astype(BF16)
        dp = jnp.concatenate([both(dqf, dqb), both(dkf, dkb), both(dvf, dvb), dg[...], du[...], dv[...],
                              both(dlf, dlb)], axis=1)
        dp_ref[...] = dp
        dh = sum(_mm(dp[:, c0:c0 + r1 - r0], w_ref[r0:r1, :]) for r0, r1, c0 in PROJ_ROWS)
        xv = x_ref[...]
        r = lax.rsqrt(jnp.mean(xv * xv, axis=-1, keepdims=True) + EPS)
        xn = xv * r
        dg1_ref[...] += jnp.sum(dh * xn, axis=0, keepdims=True)
        dx_ref[...] = dx1_ref[...] + _rms_bwd(dh * g_ref[...], xn, r)

    row = lambda w: pl.BlockSpec((tm, w), lambda i: (i, 0))
    return _fused_call(
        body, comms, name="in_proj_bwd", grid=(t // tm,),
        inputs=(x, g1, dx1, dq_f, dq_b, dk_f, dk_b, dv_f, dv_b, dpg, dpu, dpv, dlr_f, dlr_b, w_main),
        in_specs=[row(D_MODEL), _const_spec((1, D_MODEL)), row(D_MODEL), row(KEY_W), row(KEY_W), row(KEY_W),
                  row(KEY_W), row(VAL_W), row(VAL_W), row(VAL_W), row(VAL_W), row(VAL_W), row(LANE), row(LANE),
                  _const_spec((PROJ_W, D_MODEL))],
        out_specs=(row(D_MODEL), row(PROJ_PAD), _acc_spec((1, D_MODEL))),
        out_shape=(jax.ShapeDtypeStruct((t, D_MODEL), F32), jax.ShapeDtypeStruct((t, PROJ_PAD), BF16),
                   jax.ShapeDtypeStruct((1, D_MODEL), F32)))


def _adamw(w, g, m, v):
    m_new = ADAM_B1 * m + (1.0 - ADAM_B1) * g
    v_new = ADAM_B2 * v + (1.0 - ADAM_B2) * (g * g)
    m_hat = m_new / (1.0 - ADAM_B1 ** ADAM_STEP)
    v_hat = v_new / (1.0 - ADAM_B2 ** ADAM_STEP)
    delta = -ADAM_LR * (m_hat / (jnp.sqrt(v_hat) + ADAM_EPS) + ADAM_WD * w)
    return delta, m_new, v_new


def _adamw_shard(own, recv, w, m, v, tr, name):
    r, c = w.shape

    def body(own_ref, recv_ref, w_ref, m_ref, v_ref, g_ref, d_ref, nm_ref, nv_ref):
        g = own_ref[...]
        for k in range(3):
            g = g + recv_ref[k].astype(F32)
        g_ref[...] = g
        d_ref[...], nm_ref[...], nv_ref[...] = _adamw(w_ref[...], g, m_ref[...], v_ref[...])

    row = pl.BlockSpec((tr, c), lambda i: (i, 0))
    return pl.pallas_call(
        body, name=name, grid=(r // tr,),
        in_specs=[row, pl.BlockSpec((3, tr, c), lambda i: (0, i, 0)), row, row, row],
        out_specs=(row,) * 4, out_shape=(jax.ShapeDtypeStruct((r, c), F32),) * 4,
        compiler_params=_params(),
    )(own, recv, w, m, v)


def _adamw_small(entries):
    stacks = []
    for (g, _, _), _, _, _ in entries:
        if not any(g is s for s in stacks):
            stacks.append(g)
    where = [next(i for i, s in enumerate(stacks) if s is g) for (g, _, _), _, _, _ in entries]
    ns, ne = len(stacks), len(entries)

    def body(*refs):
        s_refs, wmv, outs = refs[:ns], refs[ns:ns + 3 * ne], refs[ns + 3 * ne:]
        for e, ((_, r0, nr), _, _, _) in enumerate(entries):
            grad = s_refs[where[e]][r0:r0 + nr, :]
            w_ref, m_ref, v_ref = wmv[3 * e:3 * e + 3]
            g_ref, d_ref, nm_ref, nv_ref = outs[4 * e:4 * e + 4]
            g_ref[...] = grad
            d_ref[...], nm_ref[...], nv_ref[...] = _adamw(w_ref[...], grad, m_ref[...], v_ref[...])

    results = pl.pallas_call(
        body, name="adamw_small",
        out_shape=tuple(jax.ShapeDtypeStruct(w.shape, F32) for _, w, _, _ in entries for _ in range(4)),
        compiler_params=pltpu.CompilerParams(vmem_limit_bytes=VMEM_LIMIT),
    )(*stacks, *[a for _, w, m, v in entries for a in (w, m, v)])
    return [results[4 * e:4 * e + 4] for e in range(ne)]


def _mesh_pos():
    return lax.axis_index("x"), lax.axis_index("y"), lax.axis_index("c")


def _other_chips(x, y):
    return [(x, 1 - y), (1 - x, y), (1 - x, 1 - y)]


_VMEM_WHOLE = pl.BlockSpec(memory_space=pltpu.VMEM)
_HBM_WHOLE = pl.BlockSpec(memory_space=pl.ANY)


def _gather_comm(shards, cast, mid=(1, 2)):
    na = len(shards)
    staged = [a for a in range(na) if cast[a]]

    def phases(in_refs, out_refs, scr):
        stage = dict(zip(staged, scr[:len(staged)]))
        send_sems, recv_sems, local_sems = scr[len(staged):]
        x, y, c = _mesh_pos()
        me, sibling = (x, y, c), (x, y, 1 - c)
        chips = _other_chips(x, y)
        srcs = [stage[a] if cast[a] else in_refs[a] for a in range(na)]

        def rows(a, pos):
            px, py, pc = pos
            return out_refs[a].at[4 * px + 2 * py + pc]

        def copy(a, k, block, to, src=None):
            return pltpu.make_async_remote_copy(
                src_ref=rows(a, block) if src is None else src, dst_ref=rows(a, block),
                send_sem=send_sems.at[a, k], recv_sem=recv_sems.at[a, k], device_id=to, device_id_type=MESH_ID)

        mine = [pltpu.make_async_copy(srcs[a], rows(a, me), local_sems.at[a]) for a in range(na)]
        first = []
        for a in range(na):
            first.append(copy(a, 0, me, sibling, src=srcs[a]))
            first += [copy(a, 1 + j, me, (*chip, c), src=srcs[a]) for j, chip in enumerate(chips)]
        passed = [copy(a, 4 + j, (*chip, c), sibling) for j, chip in enumerate(chips) for a in range(na)]

        def start():
            for a in staged:
                stage[a][...] = in_refs[a][...].astype(BF16)
            for cp in mine + first:
                cp.start()

        def forward():
            i = 0
            for j, chip in enumerate(chips):
                for a in range(na):
                    copy(a, 1 + j, (*chip, c), me).wait_recv()
                    passed[i].start()
                    i += 1

        def finish():
            for a in range(na):
                copy(a, 0, sibling, me).wait_recv()
                for j, chip in enumerate(chips):
                    copy(a, 4 + j, (*chip, 1 - c), me).wait_recv()
            for cp in first + passed:
                cp.wait_send()
            for cp in mine:
                cp.wait()

        return start, forward, finish

    def before(step, nsteps, in_refs, out_refs, scr):
        start, forward, _ = phases(in_refs, out_refs, scr)
        pl.when(step == 0)(start)
        pl.when(step == nsteps * mid[0] // mid[1])(forward)

    def after(step, nsteps, in_refs, out_refs, scr):
        pl.when(step == nsteps - 1)(phases(in_refs, out_refs, scr)[2])

    return _Comm(
        inputs=list(shards), in_specs=[_VMEM_WHOLE] * na,
        out_shape=[jax.ShapeDtypeStruct((N_DEV,) + s.shape, BF16 if cast[a] else s.dtype)
                   for a, s in enumerate(shards)],
        out_specs=[_HBM_WHOLE] * na,
        scratch_shapes=[pltpu.VMEM(shards[a].shape, BF16) for a in staged] + [
            pltpu.SemaphoreType.DMA((na, 7)), pltpu.SemaphoreType.DMA((na, 7)), pltpu.SemaphoreType.DMA((na,))],
        before=before, after=after)


def _exchange_comm(arrays, out_shape, make_copies):
    na = len(arrays)

    def copies(in_refs, out_refs, scr):
        return make_copies(in_refs, out_refs, *scr)

    def before(step, nsteps, in_refs, out_refs, scr):
        @pl.when(step == 0)
        def _():
            for cp in copies(in_refs, out_refs, scr):
                cp.start()

    def after(step, nsteps, in_refs, out_refs, scr):
        @pl.when(step == nsteps - 1)
        def _():
            for cp in copies(in_refs, out_refs, scr):
                cp.wait()

    return _Comm(inputs=list(arrays), in_specs=[_HBM_WHOLE] * na, out_shape=list(out_shape),
                 out_specs=[_HBM_WHOLE] * na,
                 scratch_shapes=[pltpu.SemaphoreType.DMA((na, 3)), pltpu.SemaphoreType.DMA((na, 3))],
                 before=before, after=after)


def _sibling_exchange_comm(grads):
    def make_copies(in_refs, out_refs, send_sems, recv_sems):
        x, y, c = _mesh_pos()
        return [pltpu.make_async_remote_copy(
            src_ref=in_refs[a].at[:, pl.ds(1 - c, 1)], dst_ref=out_refs[a], send_sem=send_sems.at[a, 0],
            recv_sem=recv_sems.at[a, 0], device_id=(x, y, 1 - c), device_id_type=MESH_ID)
            for a in range(len(grads))]

    return _exchange_comm(grads, [jax.ShapeDtypeStruct((4, 1) + g.shape[2:], F32) for g in grads], make_copies)


def _chips_exchange_comm(partials):
    def make_copies(in_refs, out_refs, send_sems, recv_sems):
        x, y, c = _mesh_pos()
        return [pltpu.make_async_remote_copy(
            src_ref=in_refs[a].at[j], dst_ref=out_refs[a].at[j], send_sem=send_sems.at[a, j],
            recv_sem=recv_sems.at[a, j], device_id=(*chip, c), device_id_type=MESH_ID)
            for a in range(len(partials)) for j, chip in enumerate(_other_chips(x, y))]

    return _exchange_comm(partials, [jax.ShapeDtypeStruct(g.shape, BF16) for g in partials], make_copies)


def _comm_only(comms, name):
    return _fused_call(lambda: None, comms, name=name, grid=(1,), inputs=(), in_specs=[], out_specs=(),
                       out_shape=())[1]


def _chip_sum(my_pos, mine, from_sibling, tr, name):
    _, _, r, c = mine.shape

    def body(pos_ref, a_ref, b_ref, own_ref, out_ref):
        s = a_ref[0, 0] + b_ref[0, 0]

        @pl.when(pl.program_id(1) == 0)
        def _():
            own_ref[...] = s

        @pl.when(pl.program_id(1) > 0)
        def _():
            out_ref[0] = s.astype(BF16)

    grid_spec = pltpu.PrefetchScalarGridSpec(
        num_scalar_prefetch=1, grid=(r // tr, 4),
        in_specs=[pl.BlockSpec((1, 1, tr, c), lambda i, k, pos: (pos[0] ^ k, pos[1], i, 0)),
                  pl.BlockSpec((1, 1, tr, c), lambda i, k, pos: (pos[0] ^ k, 0, i, 0))],
        out_specs=(pl.BlockSpec((tr, c), lambda i, k, pos: (i, 0)),
                   pl.BlockSpec((1, tr, c), lambda i, k, pos: (jnp.maximum(k - 1, 0), i, 0))))
    return pl.pallas_call(
        body, name=name, grid_spec=grid_spec,
        out_shape=(jax.ShapeDtypeStruct((r, c), F32), jax.ShapeDtypeStruct((3, r, c), BF16)),
        compiler_params=_params(2),
    )(my_pos, mine, from_sibling)


def _all_reduce_small_comm(parts):
    na = len(parts)

    def copies(in_refs, scr):
        gathered, (send_sems, recv_sems) = scr[:na], scr[na:]
        x, y, c = _mesh_pos()
        my_id = 4 * x + 2 * y + c
        return my_id, [pltpu.make_async_remote_copy(
            src_ref=in_refs[a], dst_ref=gathered[a].at[my_id], send_sem=send_sems.at[a, k - 1],
            recv_sem=recv_sems.at[a, k - 1], device_id=(x ^ (k >> 2), y ^ ((k >> 1) & 1), c ^ (k & 1)),
            device_id_type=MESH_ID) for a in range(na) for k in range(1, N_DEV)]

    def before(step, nsteps, in_refs, out_refs, scr):
        @pl.when(step == 0)
        def _():
            for cp in copies(in_refs, scr)[1]:
                cp.start()

    def after(step, nsteps, in_refs, out_refs, scr):
        @pl.when(step == nsteps - 1)
        def _():
            my_id, cps = copies(in_refs, scr)
            for a in range(na):
                scr[a][my_id] = in_refs[a][...]
            for cp in cps:
                cp.wait()
            for a in range(na):
                acc = scr[a][0]
                for d in range(1, N_DEV):
                    acc = acc + scr[a][d]
                out_refs[a][...] = acc

    return _Comm(inputs=list(parts), in_specs=[_VMEM_WHOLE] * na,
                 out_shape=[jax.ShapeDtypeStruct(p.shape, F32) for p in parts], out_specs=[_VMEM_WHOLE] * na,
                 scratch_shapes=[pltpu.VMEM((N_DEV,) + p.shape, F32) for p in parts] + [
                     pltpu.SemaphoreType.DMA((na, N_DEV - 1)), pltpu.SemaphoreType.DMA((na, N_DEV - 1))],
                 before=before, after=after)


def _unshard_cols(g):
    return jnp.transpose(g, (1, 0, 2)).reshape(g.shape[1], N_DEV * g.shape[2])


def _row_blocks(w):
    return w.reshape(4, 2, w.shape[0] // N_DEV, w.shape[1])


def _stack_rows(parts):
    a = jnp.concatenate(parts, axis=0)
    return jnp.pad(a, ((0, (-a.shape[0]) % 8), (0, 0)))


def _w_in_grad_blocks(dw):
    r = PROJ_W // N_DEV
    pieces = ((0, LR_REF, 0), (LR_REF, LR_REF + 2 * LOWRANK, LR_COL), (LR_REF + 2 * LOWRANK, PROJ_W, LR_REF))

    def ref_rows(a, b):
        parts = [dw[src + max(a, lo) - lo:src + min(b, hi) - lo] for lo, hi, src in pieces if max(a, lo) < min(b, hi)]
        return parts[0] if len(parts) == 1 else jnp.concatenate(parts, axis=0)

    return jnp.stack([ref_rows(d * r, (d + 1) * r) for d in range(N_DEV)]).reshape(4, 2, r, D_MODEL)


def _padded_decay_weights(wd_f, wd_b):
    zeros = lambda n: jnp.zeros((n, KEY_W), F32)
    return (jnp.concatenate([wd_f, zeros(LANE - LOWRANK)], axis=0),
            jnp.concatenate([zeros(LOWRANK), wd_b, zeros(LANE - 2 * LOWRANK)], axis=0))


def kernel(x, norm1_g, w_in,w_decay_f, b_decay_f, w_decay_b, b_decay_b, gla_norm_g, gmlp_ln_g, gmlp_ln_b, w_spatial, b_spatial, w_out, norm2_g, w_gate, w_up, w_down, final_norm_g, loss_target, m_norm1_g, m_w_in, m_w_decay_f, m_b_decay_f, m_w_decay_b, m_b_decay_b, m_gla_norm_g, m_gmlp_ln_g, m_gmlp_ln_b, m_w_spatial, m_b_spatial, m_w_out, m_norm2_g, m_w_gate, m_w_up, m_w_down, m_final_norm_g, v_norm1_g, v_w_in, v_w_decay_f, v_b_decay_f, v_w_decay_b, v_b_decay_b, v_gla_norm_g, v_gmlp_ln_g, v_gmlp_ln_b, v_w_spatial, v_b_spatial, v_w_out, v_norm2_g, v_w_gate, v_w_up, v_w_down, v_final_norm_g):
    t = x.shape[1]
    xt = x[0]
    target = loss_target[0]
    pos_x, pos_y, pos_c = _mesh_pos()
    my_pos = jnp.stack([2 * pos_x + pos_y, pos_c]).astype(jnp.int32)
    my_id = 4 * pos_x + 2 * pos_y + pos_c

    tile = lambda n: min(n, t)
    ln_g, ln_b, w_sp = gmlp_ln_g, gmlp_ln_b, w_spatial[0]
    b_sp_col = b_spatial[0][:, :, None]
    shard = {"w_in": w_in[0].T, "w_out": w_out[0], "w_gate": w_gate[0].T, "w_up": w_up[0].T, "w_down": w_down[0]}
    shard_m = {"w_in": m_w_in[0].T, "w_out": m_w_out[0], "w_gate": m_w_gate[0].T, "w_up": m_w_up[0].T,
               "w_down": m_w_down[0]}
    shard_v = {"w_in": v_w_in[0].T, "w_out": v_w_out[0], "w_gate": v_w_gate[0].T, "w_up": v_w_up[0].T,
               "w_down": v_w_down[0]}
    transposed = ("w_in", "w_gate", "w_up")
    chip_sum = lambda n, g, s: _chip_sum(my_pos, g, s[0], g.shape[2], "chip_sum_" + n)

    decay_shard = jnp.stack([w_decay_f[0], w_decay_b[0]])
    (hb,), ((g_in, g_decay),) = _norm1(xt, norm1_g, tile(512),
                                       [_gather_comm([shard["w_in"], decay_shard], [True, False])])
    w_in_t = g_in.reshape(PROJ_W, D_MODEL)
    wd_pad_f, wd_pad_b = _padded_decay_weights(_unshard_cols(g_decay[:, 0]), _unshard_cols(g_decay[:, 1]))
    (p,), ((g_gate,),) = _in_proj(hb, w_in_t, tile(512), [_gather_comm([shard["w_gate"]], [True], mid=(7, 8))])
    (o_f, st_f, o_b, st_b), ((g_up, g_out),) = _gla_fwd(
        p, wd_pad_f, b_decay_f, wd_pad_b, b_decay_b, tile(512),
        [_gather_comm([shard["w_up"], shard["w_out"]], [True, True], mid=(3, 4))])
    w_out_full = g_out.reshape(D_MODEL, D_MODEL)
    (x1, ycat), ((g_down,),) = _mix_fwd(
        xt, o_f, o_b, p, gla_norm_g, ln_g, ln_b, w_sp, b_sp_col, w_out_full, tile(1024),
        [_gather_comm([shard["w_down"]], [True], mid=(7, 8))])

    dx1, h2b, dgate, dup, act, dx2, loss_acc, d_gf, d_g2 = _ffn(
        x1, target, norm2_g, final_norm_g[None, :], g_gate.reshape(D_FF, D_MODEL), g_up.reshape(D_FF, D_MODEL),
        g_down.reshape(D_FF, D_MODEL), tile(256))
    dw_gate, _ = _matmul_tn(dgate, h2b, D_FF // 2, tile(2048), "grad_w_gate")
    dw_up, _ = _matmul_tn(dup, h2b, D_FF // 2, tile(2048), "grad_w_up")
    dw_down, _ = _matmul_tn(act, dx2, D_FF // 2, tile(2048), "grad_w_down")

    ffn_grads = [_row_blocks(dw_gate), _row_blocks(dw_up), _row_blocks(dw_down)]
    (d_o, dpg, dpu, dpv, dw_out, d_gg, d_lg, d_lb, dw_sp, db_sp), (ffn_sib,) = _mix_bwd(
        dx1, ycat, o_f, o_b, p, gla_norm_g, ln_g, ln_b, w_sp, b_sp_col, w_out_full, tile(512),
        [_sibling_exchange_comm(ffn_grads)])
    ffn_names = ["w_gate", "w_up", "w_down"]
    ffn_sums = [chip_sum(n, g, [s]) for n, g, s in zip(ffn_names, ffn_grads, ffn_sib)]
    out_grad = _row_blocks(dw_out)
    (dq_f, dk_f, dv_f, dlr_f, dwd_f, dbd_f, dq_b, dk_b, dv_b, dlr_b, dwd_b, dbd_b), (ffn_recv, out_sib) = _gla_bwd(
        p, wd_pad_f, b_decay_f, wd_pad_b, b_decay_b, st_f, st_b, d_o, tile(512),
        [_chips_exchange_comm([s[1] for s in ffn_sums]), _sibling_exchange_comm([out_grad])])
    out_sum = chip_sum("w_out", out_grad, out_sib)
    (grad_x, dp, d_g1), _ = _in_proj_bwd(
        xt, norm1_g, dx1, dq_f, dq_b, dk_f, dk_b, dv_f, dv_b, dpg, dpu, dpv, dlr_f, dlr_b, w_in_t, tile(512))

    stacks = [_stack_rows([d_g1, d_g2, d_gf]), _stack_rows([d_gg, d_lg, d_lb]),
              _stack_rows([dbd_f, dbd_b, jnp.zeros((6, KEY_W), F32), dwd_f[:LOWRANK], dwd_b[LOWRANK:2 * LOWRANK]]),
              _stack_rows([dw_sp.reshape(GMLP_W, GMLP_CHUNK), db_sp[:, :, 0], loss_acc[:1]])]
    dw_main, (small_sums, out_recv) = _matmul_tn(
        dp, hb, PROJ_PAD // 3, tile(2048), "grad_w_in",
        [_all_reduce_small_comm(stacks), _chips_exchange_comm([out_sum[1]])])
    in_grad = _w_in_grad_blocks(dw_main)
    (in_sib,) = _comm_only([_sibling_exchange_comm([in_grad])], "grad_w_in_exchange_sibling")
    in_sum = chip_sum("w_in", in_grad, in_sib)
    (in_recv,) = _comm_only([_chips_exchange_comm([in_sum[1]])], "grad_w_in_exchange_chips")

    names = ["w_in", "w_out", "w_gate", "w_up", "w_down"]
    sums = [in_sum, out_sum] + ffn_sums
    received = [in_recv[0], out_recv[0]] + list(ffn_recv)
    big_out = {}
    for n, s, rc in zip(names, sums, received):
        res = _adamw_shard(s[0], rc, shard[n], shard_m[n], shard_v[n], shard[n].shape[0], "adamw_" + n)
        big_out[n] = [r.T if n in transposed else r for r in res]

    s1024, s512, s256, s128 = small_sums
    loss = s128[GMLP_W + GMLP_GROUPS, 0]
    col0 = my_id * (KEY_W // N_DEV)
    decay_cols = lambda row0: lax.dynamic_slice(s256, (row0, col0), (LOWRANK, KEY_W // N_DEV))
    flat = lambda a: a.reshape(-1, a.shape[-1])
    small = {
        "norm1_g": ((s1024, 0, 1), norm1_g, m_norm1_g, v_norm1_g),
        "w_decay_f": ((decay_cols(8), 0, LOWRANK), w_decay_f, m_w_decay_f, v_w_decay_f),
        "b_decay_f": ((s256, 0, 1), b_decay_f, m_b_decay_f, v_b_decay_f),
        "w_decay_b": ((decay_cols(8 + LOWRANK), 0, LOWRANK), w_decay_b, m_w_decay_b, v_w_decay_b),
        "b_decay_b": ((s256, 1, 1), b_decay_b, m_b_decay_b, v_b_decay_b),
        "gla_norm_g": ((s512, 0, 1), gla_norm_g, m_gla_norm_g, v_gla_norm_g),
        "gmlp_ln_g": ((s512, 1, 1), gmlp_ln_g, m_gmlp_ln_g, v_gmlp_ln_g),
        "gmlp_ln_b": ((s512, 2, 1), gmlp_ln_b, m_gmlp_ln_b, v_gmlp_ln_b),
        "w_spatial": ((s128, 0, GMLP_W), w_spatial, m_w_spatial, v_w_spatial),
        "b_spatial": ((s128, GMLP_W, GMLP_GROUPS), b_spatial, m_b_spatial, v_b_spatial),
        "norm2_g": ((s1024, 1, 1), norm2_g, m_norm2_g, v_norm2_g),
        "final_norm_g": ((s1024, 2, 1), final_norm_g, m_final_norm_g, v_final_norm_g),
    }
    small_res = _adamw_small([(g, flat(w), flat(m), flat(v)) for g, w, m, v in small.values()])
    small_out = {n: [r.reshape(small[n][1].shape) for r in res] for n, res in zip(small, small_res)}

    order = ["norm1_g", "w_in", "w_decay_f", "b_decay_f", "w_decay_b", "b_decay_b", "gla_norm_g", "gmlp_ln_g",
             "gmlp_ln_b", "w_spatial", "b_spatial", "w_out", "norm2_g", "w_gate", "w_up", "w_down", "final_norm_g"]
    outs = []
    for kind in range(4):
        for n in order:
            outs.append(big_out[n][kind][None] if n in big_out else small_out[n][kind])
    return (loss, grad_x[None], *outs)
```

```python
import functools
import math

import jax
import jax.numpy as jnp
from jax import lax
from jax.experimental import pallas as pl
from jax.experimental.pallas import tpu as pltpu

F32 = jnp.float32
BF16 = jnp.bfloat16

D_MODEL = 1024
GLA_HEADS = 4
GLA_DK = 64
GLA_DV = 128
KEY_W = GLA_HEADS * GLA_DK
VAL_W = GLA_HEADS * GLA_DV
LOWRANK = 16
GLA_TAU = 16.0
GLA_CHUNK = 64
GMLP_W = 512
GMLP_GROUPS = 4
GMLP_CHUNK = 128
D_FF = 2816
EPS = 1e-6
Q_SCALE = GLA_DK ** -0.5
PROJ_PAD = 2688
LR_COL = 2560
LANE = 128
N_DEV = 8

ADAM_LR = 0.001
ADAM_B1 = 0.9
ADAM_B2 = 0.999
ADAM_EPS = 1e-08
ADAM_WD = 0.01
ADAM_STEP = 10

VMEM_LIMIT = 56 * 1024 * 1024
MESH_ID = pl.DeviceIdType.MESH
INV_SQRT2 = 0.7071067811865476
INV_SQRT_2PI = 0.3989422804014327


def _params(n_axes=1):
    return pltpu.CompilerParams(dimension_semantics=("arbitrary",) * n_axes, vmem_limit_bytes=VMEM_LIMIT)


def _mm(a, b):
    return jnp.dot(a.astype(BF16), b.astype(BF16), preferred_element_type=F32)


def _mm_nt(a, b):
    return lax.dot_general(a.astype(BF16), b.astype(BF16), (((1,), (1,)), ((), ())), preferred_element_type=F32)


def _mm_tn(a, b):
    return lax.dot_general(a.astype(BF16), b.astype(BF16), (((0,), (0,)), ((), ())), preferred_element_type=F32)


def _const_spec(shape):
    nd = len(shape)
    return pl.BlockSpec(shape, lambda *_: (0,) * nd, pipeline_mode=pl.Buffered(1))


def _acc_spec(shape):
    nd = len(shape)
    return pl.BlockSpec(shape, lambda *_: (0,) * nd)


class _Comm:
    def __init__(self, inputs, in_specs, out_shape, out_specs, scratch_shapes, before, after):
        self.inputs, self.in_specs, self.out_shape, self.out_specs = inputs, in_specs, out_shape, out_specs
        self.scratch_shapes, self.before, self.after = scratch_shapes, before, after


def _fused_call(body, comms, *, name, grid, inputs, in_specs, out_specs, out_shape, scratch_shapes=()):
    n_in, n_out, n_scr = len(in_specs), len(out_specs), len(scratch_shapes)
    nsteps = math.prod(grid)
    sizes = [(len(c.inputs), len(c.out_shape), len(c.scratch_shapes)) for c in comms]

    def full_body(*refs):
        step = pl.program_id(0)
        for axis in range(1, len(grid)):
            step = step * grid[axis] + pl.program_id(axis)
        ins, rest = refs[:n_in], refs[n_in:]
        c_ins = []
        for ci, _, _ in sizes:
            c_ins.append(rest[:ci])
            rest = rest[ci:]
        outs, rest = rest[:n_out], rest[n_out:]
        c_outs = []
        for _, co, _ in sizes:
            c_outs.append(rest[:co])
            rest = rest[co:]
        scr, rest = rest[:n_scr], rest[n_scr:]
        c_scr = []
        for _, _, cs in sizes:
            c_scr.append(rest[:cs])
            rest = rest[cs:]
        for c, a, b, s in zip(comms, c_ins, c_outs, c_scr):
            c.before(step, nsteps, a, b, s)
        body(*ins, *outs, *scr)
        for c, a, b, s in zip(comms, c_ins, c_outs, c_scr):
            c.after(step, nsteps, a, b, s)

    results = pl.pallas_call(
        full_body, name=name, grid=grid,
        in_specs=list(in_specs) + [s for c in comms for s in c.in_specs],
        out_specs=tuple(out_specs) + tuple(s for c in comms for s in c.out_specs),
        out_shape=tuple(out_shape) + tuple(s for c in comms for s in c.out_shape),
        scratch_shapes=list(scratch_shapes) + [s for c in comms for s in c.scratch_shapes],
        compiler_params=_params(len(grid)),
    )(*inputs, *[a for c in comms for a in c.inputs])
    own, rest = results[:n_out], results[n_out:]
    comm_results = []
    for _, co, _ in sizes:
        comm_results.append(rest[:co])
        rest = rest[co:]
    return own, comm_results


def _gelu(x):
    return 0.5 * x * (1.0 + lax.erf(x * INV_SQRT2))


def _gelu_and_grad(x):
    cdf = 0.5 * (1.0 + lax.erf(x * INV_SQRT2))
    return x * cdf, cdf + x * jnp.exp(-0.5 * x * x) * INV_SQRT_2PI


def _sigmoid(x):
    return 0.5 + 0.5 * jnp.tanh(0.5 * x)


def _silu_and_grad(x):
    s = _sigmoid(x)
    return x * s, s * (1.0 + x * (1.0 - s))


def _norm1(x, g1, tm, comms=()):
    t = x.shape[0]

    def body(x_ref, g_ref, h_ref):
        xv = x_ref[...]
        r = lax.rsqrt(jnp.mean(xv * xv, axis=-1, keepdims=True) + EPS)
        h_ref[...] = (xv * r * g_ref[...]).astype(BF16)

    row = pl.BlockSpec((tm, D_MODEL), lambda i: (i, 0))
    return _fused_call(body, comms, name="norm1", grid=(t // tm,), inputs=(x, g1),
                       in_specs=[row, _const_spec((1, D_MODEL))], out_specs=(row,),
                       out_shape=(jax.ShapeDtypeStruct((t, D_MODEL), BF16),))


PROJ_W = 2592
LR_REF = 1536
PROJ_ROWS = ((0, LR_REF, 0), (LR_REF + 2 * LOWRANK, PROJ_W, LR_REF), (LR_REF, LR_REF + LANE, LR_COL))


def _in_proj(h, w_in_t, tm, comms=()):
    t = h.shape[0]

    def body(h_ref, w_ref, p_ref):
        hv = h_ref[...]
        for r0, r1, c0 in PROJ_ROWS:
            p_ref[:, c0:c0 + r1 - r0] = _mm_nt(hv, w_ref[r0:r1, :]).astype(BF16)

    return _fused_call(
        body, comms, name="in_proj", grid=(t // tm,), inputs=(h, w_in_t),
        in_specs=[pl.BlockSpec((tm, D_MODEL), lambda i: (i, 0)), _const_spec((PROJ_W, D_MODEL))],
        out_specs=(pl.BlockSpec((tm, PROJ_PAD), lambda i: (i, 0)),),
        out_shape=(jax.ShapeDtypeStruct((t, PROJ_PAD), BF16),))


def _tri(upper):
    r = lax.broadcasted_iota(jnp.int32, (GLA_CHUNK, GLA_CHUNK), 0)
    c = lax.broadcasted_iota(jnp.int32, (GLA_CHUNK, GLA_CHUNK), 1)
    return jnp.where((c >= r) if upper else (c <= r), 1.0, 0.0).astype(BF16)


def _chunk_cumsum(tri, a):
    hi = a.astype(BF16)
    lo = (a - hi.astype(F32)).astype(BF16)
    dot = functools.partial(jnp.dot, preferred_element_type=F32)
    return jnp.concatenate([dot(tri, hi[_chunk_rows(c)]) + dot(tri, lo[_chunk_rows(c)])
                            for c in range(a.shape[0] // GLA_CHUNK)], axis=0)


def _chunk_rows(c):
    return slice(c * GLA_CHUNK, (c + 1) * GLA_CHUNK)


def _gla_masks(rev):
    dk_bits, dv_bits = GLA_DK.bit_length() - 1, GLA_DV.bit_length() - 1
    key_head = lax.broadcasted_iota(jnp.int32, (GLA_CHUNK, KEY_W), 1) >> dk_bits
    val_head = lax.broadcasted_iota(jnp.int32, (GLA_CHUNK, VAL_W), 1) >> dv_bits
    t = lax.broadcasted_iota(jnp.int32, (GLA_HEADS * GLA_CHUNK, GLA_CHUNK), 0) & (GLA_CHUNK - 1)
    s = lax.broadcasted_iota(jnp.int32, (GLA_HEADS * GLA_CHUNK, GLA_CHUNK), 1)
    causal = (s >= t) if rev else (s <= t)
    state_head = lax.broadcasted_iota(jnp.int32, (GLA_DV, KEY_W), 1) >> dk_bits
    return key_head, val_head, causal, state_head


def _stack_heads(a, head_of_lane):
    a = a.astype(BF16)
    return jnp.concatenate([jnp.where(head_of_lane == h, a, jnp.zeros_like(a)) for h in range(GLA_HEADS)], axis=0)


def _rows_by_head(a):
    return jnp.concatenate([a[:, h * GLA_DV:(h + 1) * GLA_DV] for h in range(GLA_HEADS)], axis=0)


def _lanes_by_head(r):
    return jnp.concatenate([r[h * GLA_CHUNK:(h + 1) * GLA_CHUNK] for h in range(GLA_HEADS)], axis=1)


def _head_diagonal(r, head_of_lane):
    rows = r.shape[0] // GLA_HEADS
    out = jnp.where(head_of_lane == 0, r[:rows], 0.0)
    for h in range(1, GLA_HEADS):
        out = out + jnp.where(head_of_lane == h, r[h * rows:(h + 1) * rows], 0.0)
    return out


def _tile_terms(la, q, k, tri, rev):
    nc = la.shape[0] // GLA_CHUNK
    q, k = q.astype(F32), k.astype(F32)
    b = _chunk_cumsum(tri, la)
    ebl = [jnp.exp(b[c * GLA_CHUNK:c * GLA_CHUNK + 1] if rev else b[(c + 1) * GLA_CHUNK - 1:(c + 1) * GLA_CHUNK])
           for c in range(nc)]
    eb = jnp.exp(b)
    enb = jnp.exp(-b)
    ee = enb * jnp.concatenate([jnp.broadcast_to(row, (GLA_CHUNK, KEY_W)) for row in ebl], axis=0)
    return ebl, eb, enb, ee, q * Q_SCALE * eb, k * enb, k * ee


def _log_decay(lr_ref, wd_ref, bd_ref):
    z = _mm(lr_ref[...], wd_ref[...]) + bd_ref[...]
    return z, jax.nn.log_sigmoid(z) * (1.0 / GLA_TAU)


def _p_specs(tg, tile):
    return [pl.BlockSpec((tg, KEY_W), lambda i: (tile(i), 0)),
            pl.BlockSpec((tg, KEY_W), lambda i: (tile(i), 1)),
            pl.BlockSpec((tg, VAL_W), lambda i: (tile(i), 1)),
            pl.BlockSpec((tg, LANE), lambda i: (tile(i), LR_COL // LANE))]


def _gla_fwd_dir(rev, nc, q_ref, k_ref, v_ref, lr_ref, wd_ref, bd_ref, o_ref, st_ref, state):
    key_head, _, causal, state_head = _gla_masks(rev)
    order = range(nc - 1, -1, -1) if rev else range(nc)

    def intra():
        _, la = _log_decay(lr_ref, wd_ref, bd_ref)
        ebl, _, _, _, qd, kd, ke = _tile_terms(la, q_ref[...], k_ref[...], _tri(rev), rev)
        kd = kd.astype(BF16)
        v = {c: v_ref[_chunk_rows(c), :].astype(BF16) for c in order}
        qd_stack = {c: _stack_heads(qd[_chunk_rows(c)], key_head) for c in order}
        ke_stack = {c: _stack_heads(ke[_chunk_rows(c)], key_head) for c in order}
        a_all = {c: _mm_nt(qd_stack[c], kd[_chunk_rows(c)]) for c in order}
        a_all = {c: jnp.where(causal, a_all[c], 0.0).astype(BF16) for c in order}
        r = {c: _mm(a_all[c], v[c]) for c in order}
        upd = {c: _mm_tn(_rows_by_head(v[c]), ke_stack[c]) for c in order}
        return {c: (ebl[c], qd_stack[c], r[c], upd[c]) for c in order}

    def scan(terms):
        st = state[...]
        states = {}
        for c in order:
            states[c] = st
            st_ref[c] = st.astype(BF16)
            st = st * terms[c][0] + terms[c][3]
        state[...] = st
        return states

    def inter(terms, states):
        r_inter = {c: _mm_nt(terms[c][1], states[c]) for c in order}
        for c in order:
            r = terms[c][2]
            o_ref[_chunk_rows(c), :] = jnp.concatenate(
                [r[h * GLA_CHUNK:(h + 1) * GLA_CHUNK, h * GLA_DV:(h + 1) * GLA_DV]
                 + r_inter[c][h * GLA_CHUNK:(h + 1) * GLA_CHUNK] for h in range(GLA_HEADS)], axis=1)

    return intra, scan, inter


def _gla_fwd(p, wd_pad_f, bd_f, wd_pad_b, bd_b, tg, comms=()):
    t = p.shape[0]
    nt = t // tg
    nc = tg // GLA_CHUNK
    up, down = (lambda i: i), (lambda i: nt - 1 - i)

    def body(qf, kf, vf, lrf, qb, kb, vb, lrb, wdf, bdf, wdb, bdb, of, stf, ob, stb, state_f, state_b):
        @pl.when(pl.program_id(0) == 0)
        def _():
            state_f[...] = jnp.zeros_like(state_f)
            state_b[...] = jnp.zeros_like(state_b)

        dirs = [_gla_fwd_dir(False, nc, qf, kf, vf, lrf, wdf, bdf, of, stf, state_f),
                _gla_fwd_dir(True, nc, qb, kb, vb, lrb, wdb, bdb, ob, stb, state_b)]
        terms = [intra() for intra, _, _ in dirs]
        states = [scan(t) for (_, scan, _), t in zip(dirs, terms)]
        for (_, _, inter), t, s in zip(dirs, terms, states):
            inter(t, s)

    wd_spec, bd_spec = _const_spec((LANE, KEY_W)), _const_spec((1, KEY_W))
    outs = lambda tile: (pl.BlockSpec((tg, VAL_W), lambda i: (tile(i), 0)),
                         pl.BlockSpec((nc, GLA_DV, KEY_W), lambda i: (tile(i), 0, 0)))
    out_shape = (jax.ShapeDtypeStruct((t, VAL_W), F32), jax.ShapeDtypeStruct((t // GLA_CHUNK, GLA_DV, KEY_W), BF16))
    return _fused_call(
        body, comms, name="gla_fwd", grid=(nt,), inputs=(p,) * 8 + (wd_pad_f, bd_f, wd_pad_b, bd_b),
        in_specs=_p_specs(tg, up) + _p_specs(tg, down) + [wd_spec, bd_spec, wd_spec, bd_spec],
        out_specs=outs(up) + outs(down), out_shape=out_shape * 2,
        scratch_shapes=[pltpu.VMEM((GLA_DV, KEY_W), F32)] * 2)


def _gla_bwd_dir(rev, nc, q_ref, k_ref, v_ref, lr_ref, wd_ref, bd_ref, st_ref, do_ref,
                 dq_ref, dk_ref, dv_ref, dlr_ref, dwd_ref, dbd_ref, dstate):
    key_head, val_head, causal, state_head = _gla_masks(rev)
    order = range(nc) if rev else range(nc - 1, -1, -1)
    tg = nc * GLA_CHUNK

    def intra():
        z, la = _log_decay(lr_ref, wd_ref, bd_ref)
        tile = _tile_terms(la, q_ref[...], k_ref[...], _tri(rev), rev)
        qd, kd = tile[4], tile[5].astype(BF16)
        v = {c: v_ref[_chunk_rows(c), :].astype(BF16) for c in order}
        d_o = {c: do_ref[_chunk_rows(c), :] for c in order}
        kd_c = {c: kd[_chunk_rows(c)] for c in order}
        qd_stack = {c: _stack_heads(qd[_chunk_rows(c)], key_head) for c in order}
        do_stack = {c: _stack_heads(d_o[c], val_head) for c in order}
        do_rows = {c: _rows_by_head(d_o[c]) for c in order}
        a_all = {c: _mm_nt(qd_stack[c], kd_c[c]) for c in order}
        da_all = {c: _mm_nt(do_stack[c], v[c]) for c in order}
        a_all = {c: jnp.where(causal, a_all[c], 0.0).astype(BF16) for c in order}
        da_all = {c: jnp.where(causal, da_all[c], 0.0).astype(BF16) for c in order}
        dv = {c: _mm_tn(a_all[c], do_stack[c]) for c in order}
        dqd = {c: _mm(da_all[c], kd_c[c]) + _mm(do_rows[c], st_ref[c]) for c in order}
        dkd = {c: _mm_tn(da_all[c], qd_stack[c]) for c in order}
        upd = {c: _mm_tn(do_rows[c], qd_stack[c]) for c in order}
        dqd = {c: _head_diagonal(dqd[c], key_head) for c in order}
        return z, tile, {c: dict(dv=dv[c], dqd=dqd[c], dkd=dkd[c], upd=upd[c]) for c in order}

    def scan(tile, per):
        dst = dstate[...]
        dsts = {}
        for c in order:
            dsts[c] = dst
            dst = dst * tile[0][c] + per[c]["upd"]
        dstate[...] = dst
        return dsts

    def inter(z, tile, per, dsts):
        ebl, eb, enb, ee, qd, kd, ke = tile
        ke_stack = {c: _stack_heads(ke[_chunk_rows(c)], key_head) for c in order}
        v_rows = {c: _rows_by_head(v_ref[_chunk_rows(c), :].astype(BF16)) for c in order}
        dst_b = {c: dsts[c].astype(BF16) for c in order}
        dv_state = {c: _mm_nt(ke_stack[c], dst_b[c]) for c in order}
        dke_c = {c: _mm(v_rows[c], dst_b[c]) for c in order}
        dke_c = {c: _head_diagonal(dke_c[c], key_head) for c in order}
        dbl_c = {}
        for c in order:
            rows = _chunk_rows(c)
            dv_ref[rows, :] = (per[c]["dv"] + _lanes_by_head(dv_state[c])).astype(BF16)
            dbl = (jnp.sum(dsts[c] * st_ref[c].astype(F32), axis=0, keepdims=True) * ebl[c]
                   + jnp.sum(dke_c[c] * ke[rows], axis=0, keepdims=True))
            dbl_c[c] = jnp.broadcast_to(dbl, (GLA_CHUNK, KEY_W))
        tile_of = lambda parts: jnp.concatenate([parts[c] for c in range(nc)], axis=0)
        dqd, dkd = tile_of({c: per[c]["dqd"] for c in order}), tile_of({c: per[c]["dkd"] for c in order})
        dke, dbl = tile_of(dke_c), tile_of(dbl_c)
        dq_ref[...] = (dqd * eb * Q_SCALE).astype(BF16)
        dk_ref[...] = (dkd * enb + dke * ee).astype(BF16)
        db = dqd * qd - dkd * kd - dke * ke
        dz = (_chunk_cumsum(_tri(not rev), db) + dbl) * (_sigmoid(-z) * (1.0 / GLA_TAU))
        dlr_ref[...] = _mm_nt(dz, wd_ref[...]).astype(BF16)
        dwd_ref[...] += _mm_tn(lr_ref[...], dz)
        dbd_ref[...] += jnp.sum(dz, axis=0, keepdims=True)

    return intra, scan, inter


def _gla_bwd(p, wd_pad_f, bd_f, wd_pad_b, bd_b, st_f, st_b, d_o, tg, comms=()):
    t = p.shape[0]
    nt = t // tg
    nc = tg // GLA_CHUNK
    up, down = (lambda i: i), (lambda i: nt - 1 - i)

    def body(qf, kf, vf, lrf, stf, dof, qb, kb, vb, lrb, stb, dob, wdf, bdf, wdb, bdb,
             dqf, dkf, dvf, dlrf, dwdf, dbdf, dqb, dkb, dvb, dlrb, dwdb, dbdb, dstate_f, dstate_b):
        @pl.when(pl.program_id(0) == 0)
        def _():
            for ref in (dstate_f, dstate_b, dwdf, dbdf, dwdb, dbdb):
                ref[...] = jnp.zeros_like(ref)

        dirs = [_gla_bwd_dir(False, nc, qf, kf, vf, lrf, wdf, bdf, stf, dof, dqf, dkf, dvf, dlrf, dwdf, dbdf,
                             dstate_f),
                _gla_bwd_dir(True, nc, qb, kb, vb, lrb, wdb, bdb, stb, dob, dqb, dkb, dvb, dlrb, dwdb, dbdb,
                             dstate_b)]
        first = [intra() for intra, _, _ in dirs]
        dsts = [scan(tile, per) for (_, scan, _), (_, tile, per) in zip(dirs, first)]
        for (_, _, inter), (z, tile, per), d in zip(dirs, first, dsts):
            inter(z, tile, per, d)

    wd_spec, bd_spec = _const_spec((LANE, KEY_W)), _const_spec((1, KEY_W))
    ins = lambda tile: _p_specs(tg, tile) + [pl.BlockSpec((nc, GLA_DV, KEY_W), lambda i: (tile(i), 0, 0)),
                                             pl.BlockSpec((tg, VAL_W), lambda i: (tile(i), 0))]
    outs = lambda tile: (pl.BlockSpec((tg, KEY_W), lambda i: (tile(i), 0)),
                         pl.BlockSpec((tg, KEY_W), lambda i: (tile(i), 0)),
                         pl.BlockSpec((tg, VAL_W), lambda i: (tile(i), 0)),
                         pl.BlockSpec((tg, LANE), lambda i: (tile(i), 0)),
                         _acc_spec((LANE, KEY_W)), _acc_spec((1, KEY_W)))
    out_shape = (jax.ShapeDtypeStruct((t, KEY_W), BF16), jax.ShapeDtypeStruct((t, KEY_W), BF16),
                 jax.ShapeDtypeStruct((t, VAL_W), BF16), jax.ShapeDtypeStruct((t, LANE), BF16),
                 jax.ShapeDtypeStruct((LANE, KEY_W), F32), jax.ShapeDtypeStruct((1, KEY_W), F32))
    scratch = [pltpu.VMEM((GLA_DV, KEY_W), F32)]
    return _fused_call(
        body, comms, name="gla_bwd", grid=(nt,),
        inputs=(p, p, p, p, st_f, d_o, p, p, p, p, st_b, d_o, wd_pad_f, bd_f, wd_pad_b, bd_b),
        in_specs=ins(down) + ins(up) + [wd_spec, bd_spec, wd_spec, bd_spec],
        out_specs=outs(down) + outs(up), out_shape=out_shape * 2, scratch_shapes=scratch * 2)


def _head_rms(o):
    parts, scales = [], []
    for h in range(GLA_HEADS):
        oh = o[:, h * GLA_DV:(h + 1) * GLA_DV]
        r = lax.rsqrt(jnp.mean(oh * oh, axis=-1, keepdims=True) + EPS)
        parts.append(oh * r)
        scales.append(jnp.broadcast_to(r, oh.shape))
    return jnp.concatenate(parts, axis=1), jnp.concatenate(scales, axis=1)


def _layernorm_stats(zv):
    mu = jnp.mean(zv, axis=-1, keepdims=True)
    xc = zv - mu
    rs = lax.rsqrt(jnp.mean(xc * xc, axis=-1, keepdims=True) + EPS)
    return xc * rs, rs


def _mix_fwd(x, o_f, o_b, p, gla_g, ln_g, ln_b, w_sp, b_sp, w_out, tm, comms=()):
    t = x.shape[0]
    nch = tm // GMLP_CHUNK

    def body(x_ref, of_ref, ob_ref, pg_ref, pu_ref, pv_ref, gg_ref, lg_ref, lb_ref, ws_ref, bs_ref, wo_ref,
             x1_ref, y_ref, s_scr):
        on, _ = _head_rms(of_ref[...] + ob_ref[...])
        pg = pg_ref[...].astype(F32)
        y_a = on * gg_ref[...] * (pg * _sigmoid(pg))
        zu = _gelu(pu_ref[...].astype(F32))
        vhat, _ = _layernorm_stats(_gelu(pv_ref[...].astype(F32)))
        vln = (vhat * lg_ref[...] + lb_ref[...]).astype(BF16)
        for g in range(GMLP_GROUPS):
            w_g = ws_ref[g].astype(BF16)
            b_g = bs_ref[g]
            cols = slice(g * LANE, (g + 1) * LANE)
            for n in range(nch):
                rows = slice(n * GMLP_CHUNK, (n + 1) * GMLP_CHUNK)
                s_scr[rows, cols] = jnp.dot(w_g, vln[rows, cols], preferred_element_type=F32) + b_g
        ycat = jnp.concatenate([y_a, zu * s_scr[...]], axis=1).astype(BF16)
        y_ref[...] = ycat
        x1_ref[...] = x_ref[...] + jnp.dot(ycat, wo_ref[...], preferred_element_type=F32)

    half = lambda j: pl.BlockSpec((tm, VAL_W), lambda i: (i, j))
    return _fused_call(
        body, comms, name="mix_fwd", grid=(t // tm,),
        inputs=(x, o_f, o_b, p, p, p, gla_g, ln_g, ln_b, w_sp, b_sp, w_out),
        in_specs=[pl.BlockSpec((tm, D_MODEL), lambda i: (i, 0)), half(0), half(0), half(2), half(3), half(4),
                  _const_spec((1, VAL_W)), _const_spec((1, GMLP_W)), _const_spec((1, GMLP_W)),
                  _const_spec((GMLP_GROUPS, GMLP_CHUNK, GMLP_CHUNK)), _const_spec((GMLP_GROUPS, GMLP_CHUNK, 1)),
                  _const_spec((D_MODEL, D_MODEL))],
        out_specs=(pl.BlockSpec((tm, D_MODEL), lambda i: (i, 0)), pl.BlockSpec((tm, D_MODEL), lambda i: (i, 0))),
        out_shape=(jax.ShapeDtypeStruct((t, D_MODEL), F32), jax.ShapeDtypeStruct((t, D_MODEL), BF16)),
        scratch_shapes=[pltpu.VMEM((tm, GMLP_W), F32)])


def _mix_bwd(dx1, ycat, o_f, o_b, p, gla_g, ln_g, ln_b, w_sp, b_sp, w_out, tm, comms=()):
    t = dx1.shape[0]
    nch = tm // GMLP_CHUNK

    def body(dx1_ref, y_ref, of_ref, ob_ref, pg_ref, pu_ref, pv_ref, gg_ref, lg_ref, lb_ref, ws_ref, bs_ref, wo_ref,
             do_ref, dpg_ref, dpu_ref, dpv_ref, dwo_ref, dgg_ref, dlg_ref, dlb_ref, dws_ref, dbs_ref,
             s_scr, dvln_scr):
        @pl.when(pl.program_id(0) == 0)
        def _():
            for ref in (dwo_ref, dgg_ref, dlg_ref, dlb_ref, dws_ref, dbs_ref):
                ref[...] = jnp.zeros_like(ref)

        dx1 = dx1_ref[...].astype(BF16)
        dycat = _mm_nt(dx1, wo_ref[...])
        dwo_ref[...] += _mm_tn(y_ref[...], dx1)
        dy_a = dycat[:, :VAL_W]
        dy_b = dycat[:, VAL_W:]
        on, r = _head_rms(of_ref[...] + ob_ref[...])
        pg = pg_ref[...].astype(F32)
        sil, dsil = _silu_and_grad(pg)
        gg = gg_ref[...]
        dgg_ref[...] += jnp.sum(dy_a * sil * on, axis=0, keepdims=True)
        don = dy_a * sil * gg
        prod = don * on
        means = jnp.concatenate(
            [jnp.broadcast_to(jnp.mean(prod[:, h * GLA_DV:(h + 1) * GLA_DV], axis=-1, keepdims=True),
                              (tm, GLA_DV)) for h in range(GLA_HEADS)], axis=1)
        do_ref[...] = (r * (don - on * means)).astype(BF16)
        dpg_ref[...] = (dy_a * on * gg * dsil).astype(BF16)
        pu = pu_ref[...].astype(F32)
        pv = pv_ref[...].astype(F32)
        zu, dzu_dpu = _gelu_and_grad(pu)
        zv, dzv_dpv = _gelu_and_grad(pv)
        vhat, rs = _layernorm_stats(zv)
        lg = lg_ref[...]
        vln = (vhat * lg + lb_ref[...]).astype(BF16)
        ds32 = dy_b * zu
        ds = ds32.astype(BF16)
        blocks = [(g, n) for g in range(GMLP_GROUPS) for n in range(nch)]
        at = lambda g, n: (slice(n * GMLP_CHUNK, (n + 1) * GMLP_CHUNK), slice(g * LANE, (g + 1) * LANE))
        w_sp = [ws_ref[g].astype(BF16) for g in range(GMLP_GROUPS)]
        v_blk = {b: vln[at(*b)] for b in blocks}
        ds_blk = {b: ds[at(*b)] for b in blocks}
        s_blk = {b: jnp.dot(w_sp[b[0]], v_blk[b], preferred_element_type=F32) for b in blocks}
        dw_blk = {b: _mm_nt(ds_blk[b], v_blk[b]) for b in blocks}
        dvln_blk = {b: _mm_tn(w_sp[b[0]], ds_blk[b]) for b in blocks}
        for b in blocks:
            s_scr[at(*b)] = s_blk[b] + bs_ref[b[0]]
            dvln_scr[at(*b)] = dvln_blk[b]
        for g in range(GMLP_GROUPS):
            dws_ref[g] += sum(dw_blk[(g, n)] for n in range(nch))
            dbs_ref[g] += sum(jnp.sum(ds32[at(g, n)], axis=-1, keepdims=True) for n in range(nch))
        dpu_ref[...] = (dy_b * s_scr[...] * dzu_dpu).astype(BF16)
        dvln = dvln_scr[...]
        dlg_ref[...] += jnp.sum(dvln * vhat, axis=0, keepdims=True)
        dlb_ref[...] += jnp.sum(dvln, axis=0, keepdims=True)
        dvhat = dvln * lg
        dzv = rs * (dvhat - jnp.mean(dvhat, axis=-1, keepdims=True)
                    - vhat * jnp.mean(dvhat * vhat, axis=-1, keepdims=True))
        dpv_ref[...] = (dzv * dzv_dpv).astype(BF16)

    half = lambda j: pl.BlockSpec((tm, VAL_W), lambda i: (i, j))
    full = pl.BlockSpec((tm, D_MODEL), lambda i: (i, 0))
    sp_shape = (GMLP_GROUPS, GMLP_CHUNK, GMLP_CHUNK)
    bs_shape = (GMLP_GROUPS, GMLP_CHUNK, 1)
    return _fused_call(
        body, comms, name="mix_bwd", grid=(t // tm,),
        inputs=(dx1, ycat, o_f, o_b, p, p, p, gla_g, ln_g, ln_b, w_sp, b_sp, w_out),
        in_specs=[full, full, half(0), half(0), half(2), half(3), half(4),
                  _const_spec((1, VAL_W)), _const_spec((1, GMLP_W)), _const_spec((1, GMLP_W)),
                  _const_spec(sp_shape), _const_spec(bs_shape), _const_spec((D_MODEL, D_MODEL))],
        out_specs=(half(0), half(0), half(0), half(0), _acc_spec((D_MODEL, D_MODEL)), _acc_spec((1, VAL_W)),
                   _acc_spec((1, GMLP_W)), _acc_spec((1, GMLP_W)), _acc_spec(sp_shape), _acc_spec(bs_shape)),
        out_shape=(jax.ShapeDtypeStruct((t, VAL_W), BF16),) * 4 + (
            jax.ShapeDtypeStruct((D_MODEL, D_MODEL), F32), jax.ShapeDtypeStruct((1, VAL_W), F32),
            jax.ShapeDtypeStruct((1, GMLP_W), F32), jax.ShapeDtypeStruct((1, GMLP_W), F32),
            jax.ShapeDtypeStruct(sp_shape, F32), jax.ShapeDtypeStruct(bs_shape, F32)),
        scratch_shapes=[pltpu.VMEM((tm, GMLP_W), F32), pltpu.VMEM((tm, GMLP_W), F32)])


def _rms_bwd(dy_scaled, xn, r):
    return r * (dy_scaled - xn * jnp.mean(dy_scaled * xn, axis=-1, keepdims=True))


def _ffn(x1, target, g2, gf, w_gate, w_up, w_down, tm):
    t = x1.shape[0]

    def body(x1_ref, tg_ref, g2_ref, gf_ref, wg_ref, wu_ref, wd_ref,
             dx1_ref, h2_ref, dgate_ref, dup_ref, act_ref, dx2_ref, loss_ref, dgf_ref, dg2_ref):
        @pl.when(pl.program_id(0) == 0)
        def _():
            for ref in (loss_ref, dgf_ref, dg2_ref):
                ref[...] = jnp.zeros_like(ref)

        x1v = x1_ref[...]
        g2v = g2_ref[...]
        gfv = gf_ref[...]
        r2 = lax.rsqrt(jnp.mean(x1v * x1v, axis=-1, keepdims=True) + EPS)
        xn1 = x1v * r2
        h2 = (xn1 * g2v).astype(BF16)
        h2_ref[...] = h2
        gate = _mm_nt(h2, wg_ref[...])
        up = _mm_nt(h2, wu_ref[...])
        sil, dsil = _silu_and_grad(gate)
        act = (sil * up).astype(BF16)
        act_ref[...] = act
        x2 = x1v + jnp.dot(act, wd_ref[...], preferred_element_type=F32)
        rf = lax.rsqrt(jnp.mean(x2 * x2, axis=-1, keepdims=True) + EPS)
        xn2 = x2 * rf
        err = xn2 * gfv - tg_ref[...]
        loss_ref[...] += 0.5 * jnp.sum(jnp.mean(err * err, axis=-1, keepdims=True))
        dy = err * (1.0 / D_MODEL)
        dgf_ref[...] += jnp.sum(dy * xn2, axis=0, keepdims=True)
        dx2 = _rms_bwd(dy * gfv, xn2, rf)
        dx2b = dx2.astype(BF16)
        dx2_ref[...] = dx2b
        dact = _mm_nt(dx2b, wd_ref[...])
        dgate = (dact * up * dsil).astype(BF16)
        dup = (dact * sil).astype(BF16)
        dgate_ref[...] = dgate
        dup_ref[...] = dup
        dh2 = _mm(dgate, wg_ref[...]) + _mm(dup, wu_ref[...])
        dg2_ref[...] += jnp.sum(dh2 * xn1, axis=0, keepdims=True)
        dx1_ref[...] = dx2 + _rms_bwd(dh2 * g2v, xn1, r2)

    row = lambda w: pl.BlockSpec((tm, w), lambda i: (i, 0))
    return pl.pallas_call(
        body, name="ffn_fwd_bwd", grid=(t // tm,),
        in_specs=[row(D_MODEL), row(D_MODEL), _const_spec((1, D_MODEL)), _const_spec((1, D_MODEL)),
                  _const_spec((D_FF, D_MODEL)), _const_spec((D_FF, D_MODEL)), _const_spec((D_FF, D_MODEL))],
        out_specs=(row(D_MODEL), row(D_MODEL), row(D_FF), row(D_FF), row(D_FF), row(D_MODEL),
                   _acc_spec((8, LANE)), _acc_spec((1, D_MODEL)), _acc_spec((1, D_MODEL))),
        out_shape=(jax.ShapeDtypeStruct((t, D_MODEL), F32), jax.ShapeDtypeStruct((t, D_MODEL), BF16),
                   jax.ShapeDtypeStruct((t, D_FF), BF16), jax.ShapeDtypeStruct((t, D_FF), BF16),
                   jax.ShapeDtypeStruct((t, D_FF), BF16), jax.ShapeDtypeStruct((t, D_MODEL), BF16),
                   jax.ShapeDtypeStruct((8, LANE), F32), jax.ShapeDtypeStruct((1, D_MODEL), F32),
                   jax.ShapeDtypeStruct((1, D_MODEL), F32)),
        compiler_params=_params(),
    )(x1, target, g2, gf, w_gate, w_up, w_down)


def _matmul_tn(a, b, tm, tk, name, comms=()):
    t, m = a.shape
    n = b.shape[1]

    def body(a_ref, b_ref, o_ref):
        @pl.when(pl.program_id(1) == 0)
        def _():
            o_ref[...] = jnp.zeros_like(o_ref)

        o_ref[...] += _mm_tn(a_ref[...], b_ref[...])

    (out,), comm_results = _fused_call(
        body, comms, name=name, grid=(m // tm, t // tk), inputs=(a, b),
        in_specs=[pl.BlockSpec((tk, tm), lambda j, k: (k, j)), pl.BlockSpec((tk, n), lambda j, k: (k, 0))],
        out_specs=(pl.BlockSpec((tm, n), lambda j, k: (j, 0)),),
        out_shape=(jax.ShapeDtypeStruct((m, n), F32),))
    return out, comm_results


def _in_proj_bwd(x, g1, dx1, dq_f, dq_b, dk_f, dk_b, dv_f, dv_b, dpg, dpu, dpv, dlr_f, dlr_b, w_main, tm, comms=()):
    t = x.shape[0]

    def body(x_ref, g_ref, dx1_ref, dqf, dqb, dkf, dkb, dvf, dvb, dg, du, dv, dlf, dlb, w_ref,
             dx_ref, dp_ref, dg1_ref):
        @pl.when(pl.program_id(0) == 0)
        def _():
            dg1_ref[...] = jnp.zeros_like(dg1_ref)

        both = lambda a, b: (a[...].astype(F32) + b[...].astype(F32)).astype(BF16)
        dp = jnp.concatenate([both(dqf, dqb), both(dkf, dkb), both(dvf, dvb), dg[...], du[...], dv[...],
                              both(dlf, dlb)], axis=1)
        dp_ref[...] = dp
        dh = sum(_mm(dp[:, c0:c0 + r1 - r0], w_ref[r0:r1, :]) for r0, r1, c0 in PROJ_ROWS)
        xv = x_ref[...]
        r = lax.rsqrt(jnp.mean(xv * xv, axis=-1, keepdims=True) + EPS)
        xn = xv * r
        dg1_ref[...] += jnp.sum(dh * xn, axis=0, keepdims=True)
        dx_ref[...] = dx1_ref[...] + _rms_bwd(dh * g_ref[...], xn, r)

    row = lambda w: pl.BlockSpec((tm, w), lambda i: (i, 0))
    return _fused_call(
        body, comms, name="in_proj_bwd", grid=(t // tm,),
        inputs=(x, g1, dx1, dq_f, dq_b, dk_f, dk_b, dv_f, dv_b, dpg, dpu, dpv, dlr_f, dlr_b, w_main),
        in_specs=[row(D_MODEL), _const_spec((1, D_MODEL)), row(D_MODEL), row(KEY_W), row(KEY_W), row(KEY_W),
                  row(KEY_W), row(VAL_W), row(VAL_W), row(VAL_W), row(VAL_W), row(VAL_W), row(LANE), row(LANE),
                  _const_spec((PROJ_W, D_MODEL))],
        out_specs=(row(D_MODEL), row(PROJ_PAD), _acc_spec((1, D_MODEL))),
        out_shape=(jax.ShapeDtypeStruct((t, D_MODEL), F32), jax.ShapeDtypeStruct((t, PROJ_PAD), BF16),
                   jax.ShapeDtypeStruct((1, D_MODEL), F32)))


def _adamw(w, g, m, v):
    m_new = ADAM_B1 * m + (1.0 - ADAM_B1) * g
    v_new = ADAM_B2 * v + (1.0 - ADAM_B2) * (g * g)
    m_hat = m_new / (1.0 - ADAM_B1 ** ADAM_STEP)
    v_hat = v_new / (1.0 - ADAM_B2 ** ADAM_STEP)
    delta = -ADAM_LR * (m_hat / (jnp.sqrt(v_hat) + ADAM_EPS) + ADAM_WD * w)
    return delta, m_new, v_new


def _adamw_shard(own, recv, w, m, v, tr, name):
    r, c = w.shape

    def body(own_ref, recv_ref, w_ref, m_ref, v_ref, g_ref, d_ref, nm_ref, nv_ref):
        g = own_ref[...]
        for k in range(3):
            g = g + recv_ref[k].astype(F32)
        g_ref[...] = g
        d_ref[...], nm_ref[...], nv_ref[...] = _adamw(w_ref[...], g, m_ref[...], v_ref[...])

    row = pl.BlockSpec((tr, c), lambda i: (i, 0))
    return pl.pallas_call(
        body, name=name, grid=(r // tr,),
        in_specs=[row, pl.BlockSpec((3, tr, c), lambda i: (0, i, 0)), row, row, row],
        out_specs=(row,) * 4, out_shape=(jax.ShapeDtypeStruct((r, c), F32),) * 4,
        compiler_params=_params(),
    )(own, recv, w, m, v)


def _adamw_small(entries):
    stacks = []
    for (g, _, _), _, _, _ in entries:
        if not any(g is s for s in stacks):
            stacks.append(g)
    where = [next(i for i, s in enumerate(stacks) if s is g) for (g, _, _), _, _, _ in entries]
    ns, ne = len(stacks), len(entries)

    def body(*refs):
        s_refs, wmv, outs = refs[:ns], refs[ns:ns + 3 * ne], refs[ns + 3 * ne:]
        for e, ((_, r0, nr), _, _, _) in enumerate(entries):
            grad = s_refs[where[e]][r0:r0 + nr, :]
            w_ref, m_ref, v_ref = wmv[3 * e:3 * e + 3]
            g_ref, d_ref, nm_ref, nv_ref = outs[4 * e:4 * e + 4]
            g_ref[...] = grad
            d_ref[...], nm_ref[...], nv_ref[...] = _adamw(w_ref[...], grad, m_ref[...], v_ref[...])

    results = pl.pallas_call(
        body, name="adamw_small",
        out_shape=tuple(jax.ShapeDtypeStruct(w.shape, F32) for _, w, _, _ in entries for _ in range(4)),
        compiler_params=pltpu.CompilerParams(vmem_limit_bytes=VMEM_LIMIT),
    )(*stacks, *[a for _, w, m, v in entries for a in (w, m, v)])
    return [results[4 * e:4 * e + 4] for e in range(ne)]


def _mesh_pos():
    return lax.axis_index("x"), lax.axis_index("y"), lax.axis_index("c")


def _other_chips(x, y):
    return [(x, 1 - y), (1 - x, y), (1 - x, 1 - y)]


_VMEM_WHOLE = pl.BlockSpec(memory_space=pltpu.VMEM)
_HBM_WHOLE = pl.BlockSpec(memory_space=pl.ANY)


def _gather_comm(shards, cast, mid=(1, 2)):
    na = len(shards)
    staged = [a for a in range(na) if cast[a]]

    def phases(in_refs, out_refs, scr):
        stage = dict(zip(staged, scr[:len(staged)]))
        send_sems, recv_sems, local_sems = scr[len(staged):]
        x, y, c = _mesh_pos()
        me, sibling = (x, y, c), (x, y, 1 - c)
        chips = _other_chips(x, y)
        srcs = [stage[a] if cast[a] else in_refs[a] for a in range(na)]

        def rows(a, pos):
            px, py, pc = pos
            return out_refs[a].at[4 * px + 2 * py + pc]

        def copy(a, k, block, to, src=None):
            return pltpu.make_async_remote_copy(
                src_ref=rows(a, block) if src is None else src, dst_ref=rows(a, block),
                send_sem=send_sems.at[a, k], recv_sem=recv_sems.at[a, k], device_id=to, device_id_type=MESH_ID)

        mine = [pltpu.make_async_copy(srcs[a], rows(a, me), local_sems.at[a]) for a in range(na)]
        first = []
        for a in range(na):
            first.append(copy(a, 0, me, sibling, src=srcs[a]))
            first += [copy(a, 1 + j, me, (*chip, c), src=srcs[a]) for j, chip in enumerate(chips)]
        passed = [copy(a, 4 + j, (*chip, c), sibling) for j, chip in enumerate(chips) for a in range(na)]

        def start():
            for a in staged:
                stage[a][...] = in_refs[a][...].astype(BF16)
            for cp in mine + first:
                cp.start()

        def forward():
            i = 0
            for j, chip in enumerate(chips):
                for a in range(na):
                    copy(a, 1 + j, (*chip, c), me).wait_recv()
                    passed[i].start()
                    i += 1

        def finish():
            for a in range(na):
                copy(a, 0, sibling, me).wait_recv()
                for j, chip in enumerate(chips):
                    copy(a, 4 + j, (*chip, 1 - c), me).wait_recv()
            for cp in first + passed:
                cp.wait_send()
            for cp in mine:
                cp.wait()

        return start, forward, finish

    def before(step, nsteps, in_refs, out_refs, scr):
        start, forward, _ = phases(in_refs, out_refs, scr)
        pl.when(step == 0)(start)
        pl.when(step == nsteps * mid[0] // mid[1])(forward)

    def after(step, nsteps, in_refs, out_refs, scr):
        pl.when(step == nsteps - 1)(phases(in_refs, out_refs, scr)[2])

    return _Comm(
        inputs=list(shards), in_specs=[_VMEM_WHOLE] * na,
        out_shape=[jax.ShapeDtypeStruct((N_DEV,) + s.shape, BF16 if cast[a] else s.dtype)
                   for a, s in enumerate(shards)],
        out_specs=[_HBM_WHOLE] * na,
        scratch_shapes=[pltpu.VMEM(shards[a].shape, BF16) for a in staged] + [
            pltpu.SemaphoreType.DMA((na, 7)), pltpu.SemaphoreType.DMA((na, 7)), pltpu.SemaphoreType.DMA((na,))],
        before=before, after=after)


def _exchange_comm(arrays, out_shape, make_copies):
    na = len(arrays)

    def copies(in_refs, out_refs, scr):
        return make_copies(in_refs, out_refs, *scr)

    def before(step, nsteps, in_refs, out_refs, scr):
        @pl.when(step == 0)
        def _():
            for cp in copies(in_refs, out_refs, scr):
                cp.start()

    def after(step, nsteps, in_refs, out_refs, scr):
        @pl.when(step == nsteps - 1)
        def _():
            for cp in copies(in_refs, out_refs, scr):
                cp.wait()

    return _Comm(inputs=list(arrays), in_specs=[_HBM_WHOLE] * na, out_shape=list(out_shape),
                 out_specs=[_HBM_WHOLE] * na,
                 scratch_shapes=[pltpu.SemaphoreType.DMA((na, 3)), pltpu.SemaphoreType.DMA((na, 3))],
                 before=before, after=after)


def _sibling_exchange_comm(grads):
    def make_copies(in_refs, out_refs, send_sems, recv_sems):
        x, y, c = _mesh_pos()
        return [pltpu.make_async_remote_copy(
            src_ref=in_refs[a].at[:, pl.ds(1 - c, 1)], dst_ref=out_refs[a], send_sem=send_sems.at[a, 0],
            recv_sem=recv_sems.at[a, 0], device_id=(x, y, 1 - c), device_id_type=MESH_ID)
            for a in range(len(grads))]

    return _exchange_comm(grads, [jax.ShapeDtypeStruct((4, 1) + g.shape[2:], F32) for g in grads], make_copies)


def _chips_exchange_comm(partials):
    def make_copies(in_refs, out_refs, send_sems, recv_sems):
        x, y, c = _mesh_pos()
        return [pltpu.make_async_remote_copy(
            src_ref=in_refs[a].at[j], dst_ref=out_refs[a].at[j], send_sem=send_sems.at[a, j],
            recv_sem=recv_sems.at[a, j], device_id=(*chip, c), device_id_type=MESH_ID)
            for a in range(len(partials)) for j, chip in enumerate(_other_chips(x, y))]

    return _exchange_comm(partials, [jax.ShapeDtypeStruct(g.shape, BF16) for g in partials], make_copies)


def _comm_only(comms, name):
    return _fused_call(lambda: None, comms, name=name, grid=(1,), inputs=(), in_specs=[], out_specs=(),
                       out_shape=())[1]


def _chip_sum(my_pos, mine, from_sibling, tr, name):
    _, _, r, c = mine.shape

    def body(pos_ref, a_ref, b_ref, own_ref, out_ref):
        s = a_ref[0, 0] + b_ref[0, 0]

        @pl.when(pl.program_id(1) == 0)
        def _():
            own_ref[...] = s

        @pl.when(pl.program_id(1) > 0)
        def _():
            out_ref[0] = s.astype(BF16)

    grid_spec = pltpu.PrefetchScalarGridSpec(
        num_scalar_prefetch=1, grid=(r // tr, 4),
        in_specs=[pl.BlockSpec((1, 1, tr, c), lambda i, k, pos: (pos[0] ^ k, pos[1], i, 0)),
                  pl.BlockSpec((1, 1, tr, c), lambda i, k, pos: (pos[0] ^ k, 0, i, 0))],
        out_specs=(pl.BlockSpec((tr, c), lambda i, k, pos: (i, 0)),
                   pl.BlockSpec((1, tr, c), lambda i, k, pos: (jnp.maximum(k - 1, 0), i, 0))))
    return pl.pallas_call(
        body, name=name, grid_spec=grid_spec,
        out_shape=(jax.ShapeDtypeStruct((r, c), F32), jax.ShapeDtypeStruct((3, r, c), BF16)),
        compiler_params=_params(2),
    )(my_pos, mine, from_sibling)


def _all_reduce_small_comm(parts):
    na = len(parts)

    def copies(in_refs, scr):
        gathered, (send_sems, recv_sems) = scr[:na], scr[na:]
        x, y, c = _mesh_pos()
        my_id = 4 * x + 2 * y + c
        return my_id, [pltpu.make_async_remote_copy(
            src_ref=in_refs[a], dst_ref=gathered[a].at[my_id], send_sem=send_sems.at[a, k - 1],
            recv_sem=recv_sems.at[a, k - 1], device_id=(x ^ (k >> 2), y ^ ((k >> 1) & 1), c ^ (k & 1)),
            device_id_type=MESH_ID) for a in range(na) for k in range(1, N_DEV)]

    def before(step, nsteps, in_refs, out_refs, scr):
        @pl.when(step == 0)
        def _():
            for cp in copies(in_refs, scr)[1]:
                cp.start()

    def after(step, nsteps, in_refs, out_refs, scr):
        @pl.when(step == nsteps - 1)
        def _():
            my_id, cps = copies(in_refs, scr)
            for a in range(na):
                scr[a][my_id] = in_refs[a][...]
            for cp in cps:
                cp.wait()
            for a in range(na):
                acc = scr[a][0]
                for d in range(1, N_DEV):
                    acc = acc + scr[a][d]
                out_refs[a][...] = acc

    return _Comm(inputs=list(parts), in_specs=[_VMEM_WHOLE] * na,
                 out_shape=[jax.ShapeDtypeStruct(p.shape, F32) for p in parts], out_specs=[_VMEM_WHOLE] * na,
                 scratch_shapes=[pltpu.VMEM((N_DEV,) + p.shape, F32) for p in parts] + [
                     pltpu.SemaphoreType.DMA((na, N_DEV - 1)), pltpu.SemaphoreType.DMA((na, N_DEV - 1))],
                 before=before, after=after)


def _unshard_cols(g):
    return jnp.transpose(g, (1, 0, 2)).reshape(g.shape[1], N_DEV * g.shape[2])


def _row_blocks(w):
    return w.reshape(4, 2, w.shape[0] // N_DEV, w.shape[1])


def _stack_rows(parts):
    a = jnp.concatenate(parts, axis=0)
    return jnp.pad(a, ((0, (-a.shape[0]) % 8), (0, 0)))


def _w_in_grad_blocks(dw):
    r = PROJ_W // N_DEV
    pieces = ((0, LR_REF, 0), (LR_REF, LR_REF + 2 * LOWRANK, LR_COL), (LR_REF + 2 * LOWRANK, PROJ_W, LR_REF))

    def ref_rows(a, b):
        parts = [dw[src + max(a, lo) - lo:src + min(b, hi) - lo] for lo, hi, src in pieces if max(a, lo) < min(b, hi)]
        return parts[0] if len(parts) == 1 else jnp.concatenate(parts, axis=0)

    return jnp.stack([ref_rows(d * r, (d + 1) * r) for d in range(N_DEV)]).reshape(4, 2, r, D_MODEL)


def _padded_decay_weights(wd_f, wd_b):
    zeros = lambda n: jnp.zeros((n, KEY_W), F32)
    return (jnp.concatenate([wd_f, zeros(LANE - LOWRANK)], axis=0),
            jnp.concatenate([zeros(LOWRANK), wd_b, zeros(LANE - 2 * LOWRANK)], axis=0))


def kernel(x, norm1_g, w_in,w_decay_f, b_decay_f, w_decay_b, b_decay_b, gla_norm_g, gmlp_ln_g, gmlp_ln_b, w_spatial, b_spatial, w_out, norm2_g, w_gate, w_up, w_down, final_norm_g, loss_target, m_norm1_g, m_w_in, m_w_decay_f, m_b_decay_f, m_w_decay_b, m_b_decay_b, m_gla_norm_g, m_gmlp_ln_g, m_gmlp_ln_b, m_w_spatial, m_b_spatial, m_w_out, m_norm2_g, m_w_gate, m_w_up, m_w_down, m_final_norm_g, v_norm1_g, v_w_in, v_w_decay_f, v_b_decay_f, v_w_decay_b, v_b_decay_b, v_gla_norm_g, v_gmlp_ln_g, v_gmlp_ln_b, v_w_spatial, v_b_spatial, v_w_out, v_norm2_g, v_w_gate, v_w_up, v_w_down, v_final_norm_g):
    t = x.shape[1]
    xt = x[0]
    target = loss_target[0]
    pos_x, pos_y, pos_c = _mesh_pos()
    my_pos = jnp.stack([2 * pos_x + pos_y, pos_c]).astype(jnp.int32)
    my_id = 4 * pos_x + 2 * pos_y + pos_c

    tile = lambda n: min(n, t)
    ln_g, ln_b, w_sp = gmlp_ln_g, gmlp_ln_b, w_spatial[0]
    b_sp_col = b_spatial[0][:, :, None]
    shard = {"w_in": w_in[0].T, "w_out": w_out[0], "w_gate": w_gate[0].T, "w_up": w_up[0].T, "w_down": w_down[0]}
    shard_m = {"w_in": m_w_in[0].T, "w_out": m_w_out[0], "w_gate": m_w_gate[0].T, "w_up": m_w_up[0].T,
               "w_down": m_w_down[0]}
    shard_v = {"w_in": v_w_in[0].T, "w_out": v_w_out[0], "w_gate": v_w_gate[0].T, "w_up": v_w_up[0].T,
               "w_down": v_w_down[0]}
    transposed = ("w_in", "w_gate", "w_up")
    chip_sum = lambda n, g, s: _chip_sum(my_pos, g, s[0], g.shape[2], "chip_sum_" + n)

    decay_shard = jnp.stack([w_decay_f[0], w_decay_b[0]])
    (hb,), ((g_in, g_decay),) = _norm1(xt, norm1_g, tile(512),
                                       [_gather_comm([shard["w_in"], decay_shard], [True, False])])
    w_in_t = g_in.reshape(PROJ_W, D_MODEL)
    wd_pad_f, wd_pad_b = _padded_decay_weights(_unshard_cols(g_decay[:, 0]), _unshard_cols(g_decay[:, 1]))
    (p,), ((g_gate, g_out),) = _in_proj(
        hb, w_in_t, tile(512), [_gather_comm([shard["w_gate"], shard["w_out"]], [True, True], mid=(7, 8))])
    (o_f, st_f, o_b, st_b), ((g_up, g_down),) = _gla_fwd(
        p, wd_pad_f, b_decay_f, wd_pad_b, b_decay_b, tile(512),
        [_gather_comm([shard["w_up"], shard["w_down"]], [True, True], mid=(7, 8))])
    w_out_full = g_out.reshape(D_MODEL, D_MODEL)
    (x1, ycat), _ = _mix_fwd(xt, o_f, o_b, p, gla_norm_g, ln_g, ln_b, w_sp, b_sp_col, w_out_full, tile(1024))

    dx1, h2b, dgate, dup, act, dx2, loss_acc, d_gf, d_g2 = _ffn(
        x1, target, norm2_g, final_norm_g[None, :], g_gate.reshape(D_FF, D_MODEL), g_up.reshape(D_FF, D_MODEL),
        g_down.reshape(D_FF, D_MODEL), tile(256))
    dw_gate, _ = _matmul_tn(dgate, h2b, D_FF // 2, tile(2048), "grad_w_gate")
    dw_up, _ = _matmul_tn(dup, h2b, D_FF // 2, tile(2048), "grad_w_up")
    dw_down, _ = _matmul_tn(act, dx2, D_FF // 2, tile(2048), "grad_w_down")

    ffn_grads = [_row_blocks(dw_gate), _row_blocks(dw_up), _row_blocks(dw_down)]
    (d_o, dpg, dpu, dpv, dw_out, d_gg, d_lg, d_lb, dw_sp, db_sp), (ffn_sib,) = _mix_bwd(
        dx1, ycat, o_f, o_b, p, gla_norm_g, ln_g, ln_b, w_sp, b_sp_col, w_out_full, tile(512),
        [_sibling_exchange_comm(ffn_grads)])
    ffn_names = ["w_gate", "w_up", "w_down"]
    ffn_sums = [chip_sum(n, g, [s]) for n, g, s in zip(ffn_names, ffn_grads, ffn_sib)]
    out_grad = _row_blocks(dw_out)
    (dq_f, dk_f, dv_f, dlr_f, dwd_f, dbd_f, dq_b, dk_b, dv_b, dlr_b, dwd_b, dbd_b), (ffn_recv, out_sib) = _gla_bwd(
        p, wd_pad_f, b_decay_f, wd_pad_b, b_decay_b, st_f, st_b, d_o, tile(512),
        [_chips_exchange_comm([s[1] for s in ffn_sums]), _sibling_exchange_comm([out_grad])])
    out_sum = chip_sum("w_out", out_grad, out_sib)
    (grad_x, dp, d_g1), _ = _in_proj_bwd(
        xt, norm1_g, dx1, dq_f, dq_b, dk_f, dk_b, dv_f, dv_b, dpg, dpu, dpv, dlr_f, dlr_b, w_in_t, tile(512))

    stacks = [_stack_rows([d_g1, d_g2, d_gf]), _stack_rows([d_gg, d_lg, d_lb]),
              _stack_rows([dbd_f, dbd_b, jnp.zeros((6, KEY_W), F32), dwd_f[:LOWRANK], dwd_b[LOWRANK:2 * LOWRANK]]),
              _stack_rows([dw_sp.reshape(GMLP_W, GMLP_CHUNK), db_sp[:, :, 0], loss_acc[:1]])]
    dw_main, (small_sums, out_recv) = _matmul_tn(
        dp, hb, PROJ_PAD // 3, tile(2048), "grad_w_in",
        [_all_reduce_small_comm(stacks), _chips_exchange_comm([out_sum[1]])])
    in_grad = _w_in_grad_blocks(dw_main)
    (in_sib,) = _comm_only([_sibling_exchange_comm([in_grad])], "grad_w_in_exchange_sibling")
    in_sum = chip_sum("w_in", in_grad, in_sib)
    (in_recv,) = _comm_only([_chips_exchange_comm([in_sum[1]])], "grad_w_in_exchange_chips")

    names = ["w_in", "w_out", "w_gate", "w_up", "w_down"]
    sums = [in_sum, out_sum] + ffn_sums
    received = [in_recv[0], out_recv[0]] + list(ffn_recv)
    big_out = {}
    for n, s, rc in zip(names, sums, received):
        res = _adamw_shard(s[0], rc, shard[n], shard_m[n], shard_v[n], shard[n].shape[0], "adamw_" + n)
        big_out[n] = [r.T if n in transposed else r for r in res]

    s1024, s512, s256, s128 = small_sums
    loss = s128[GMLP_W + GMLP_GROUPS, 0]
    col0 = my_id * (KEY_W // N_DEV)
    decay_cols = lambda row0: lax.dynamic_slice(s256, (row0, col0), (LOWRANK, KEY_W // N_DEV))
    flat = lambda a: a.reshape(-1, a.shape[-1])
    small = {
        "norm1_g": ((s1024, 0, 1), norm1_g, m_norm1_g, v_norm1_g),
        "w_decay_f": ((decay_cols(8), 0, LOWRANK), w_decay_f, m_w_decay_f, v_w_decay_f),
        "b_decay_f": ((s256, 0, 1), b_decay_f, m_b_decay_f, v_b_decay_f),
        "w_decay_b": ((decay_cols(8 + LOWRANK), 0, LOWRANK), w_decay_b, m_w_decay_b, v_w_decay_b),
        "b_decay_b": ((s256, 1, 1), b_decay_b, m_b_decay_b, v_b_decay_b),
        "gla_norm_g": ((s512, 0, 1), gla_norm_g, m_gla_norm_g, v_gla_norm_g),
        "gmlp_ln_g": ((s512, 1, 1), gmlp_ln_g, m_gmlp_ln_g, v_gmlp_ln_g),
        "gmlp_ln_b": ((s512, 2, 1), gmlp_ln_b, m_gmlp_ln_b, v_gmlp_ln_b),
        "w_spatial": ((s128, 0, GMLP_W), w_spatial, m_w_spatial, v_w_spatial),
        "b_spatial": ((s128, GMLP_W, GMLP_GROUPS), b_spatial, m_b_spatial, v_b_spatial),
        "norm2_g": ((s1024, 1, 1), norm2_g, m_norm2_g, v_norm2_g),
        "final_norm_g": ((s1024, 2, 1), final_norm_g, m_final_norm_g, v_final_norm_g),
    }
    small_res = _adamw_small([(g, flat(w), flat(m), flat(v)) for g, w, m, v in small.values()])
    small_out = {n: [r.reshape(small[n][1].shape) for r in res] for n, res in zip(small, small_res)}

    order = ["norm1_g", "w_in", "w_decay_f", "b_decay_f", "w_decay_b", "b_decay_b", "gla_norm_g", "gmlp_ln_g",
             "gmlp_ln_b", "w_spatial", "b_spatial", "w_out", "norm2_g", "w_gate", "w_up", "w_down", "final_norm_g"]
    outs = []
    for kind in range(4):
        for n in order:
            outs.append(big_out[n][kind][None] if n in big_out else small_out[n][kind])
    return (loss, grad_x[None], *outs)
```

```python
import functools
import math

import jax
import jax.numpy as jnp
from jax import lax
from jax.experimental import pallas as pl
from jax.experimental.pallas import tpu as pltpu

F32 = jnp.float32
BF16 = jnp.bfloat16

D_MODEL = 1024
GLA_HEADS = 4
GLA_DK = 64
GLA_DV = 128
KEY_W = GLA_HEADS * GLA_DK
VAL_W = GLA_HEADS * GLA_DV
LOWRANK = 16
GLA_TAU = 16.0
GLA_CHUNK = 64
GMLP_W = 512
GMLP_GROUPS = 4
GMLP_CHUNK = 128
D_FF = 2816
EPS = 1e-6
Q_SCALE = GLA_DK ** -0.5
PROJ_PAD = 2688
LR_COL = 2560
LANE = 128
N_DEV = 8

ADAM_LR = 0.001
ADAM_B1 = 0.9
ADAM_B2 = 0.999
ADAM_EPS = 1e-08
ADAM_WD = 0.01
ADAM_STEP = 10

VMEM_LIMIT = 56 * 1024 * 1024
MESH_ID = pl.DeviceIdType.MESH
INV_SQRT2 = 0.7071067811865476
INV_SQRT_2PI = 0.3989422804014327


def _params(n_axes=1):
    return pltpu.CompilerParams(dimension_semantics=("arbitrary",) * n_axes, vmem_limit_bytes=VMEM_LIMIT)


def _mm(a, b):
    return jnp.dot(a.astype(BF16), b.astype(BF16), preferred_element_type=F32)


def _mm_nt(a, b):
    return lax.dot_general(a.astype(BF16), b.astype(BF16), (((1,), (1,)), ((), ())), preferred_element_type=F32)


def _mm_tn(a, b):
    return lax.dot_general(a.astype(BF16), b.astype(BF16), (((0,), (0,)), ((), ())), preferred_element_type=F32)


def _const_spec(shape):
    nd = len(shape)
    return pl.BlockSpec(shape, lambda *_: (0,) * nd, pipeline_mode=pl.Buffered(1))


def _acc_spec(shape):
    nd = len(shape)
    return pl.BlockSpec(shape, lambda *_: (0,) * nd)


class _Comm:
    def __init__(self, inputs, in_specs, out_shape, out_specs, scratch_shapes, before, after):
        self.inputs, self.in_specs, self.out_shape, self.out_specs = inputs, in_specs, out_shape, out_specs
        self.scratch_shapes, self.before, self.after = scratch_shapes, before, after


def _fused_call(body, comms, *, name, grid, inputs, in_specs, out_specs, out_shape, scratch_shapes=()):
    n_in, n_out, n_scr = len(in_specs), len(out_specs), len(scratch_shapes)
    nsteps = math.prod(grid)
    sizes = [(len(c.inputs), len(c.out_shape), len(c.scratch_shapes)) for c in comms]

    def full_body(*refs):
        step = pl.program_id(0)
        for axis in range(1, len(grid)):
            step = step * grid[axis] + pl.program_id(axis)
        ins, rest = refs[:n_in], refs[n_in:]
        c_ins = []
        for ci, _, _ in sizes:
            c_ins.append(rest[:ci])
            rest = rest[ci:]
        outs, rest = rest[:n_out], rest[n_out:]
        c_outs = []
        for _, co, _ in sizes:
            c_outs.append(rest[:co])
            rest = rest[co:]
        scr, rest = rest[:n_scr], rest[n_scr:]
        c_scr = []
        for _, _, cs in sizes:
            c_scr.append(rest[:cs])
            rest = rest[cs:]
        for c, a, b, s in zip(comms, c_ins, c_outs, c_scr):
            c.before(step, nsteps, a, b, s)
        body(*ins, *outs, *scr)
        for c, a, b, s in zip(comms, c_ins, c_outs, c_scr):
            c.after(step, nsteps, a, b, s)

    results = pl.pallas_call(
        full_body, name=name, grid=grid,
        in_specs=list(in_specs) + [s for c in comms for s in c.in_specs],
        out_specs=tuple(out_specs) + tuple(s for c in comms for s in c.out_specs),
        out_shape=tuple(out_shape) + tuple(s for c in comms for s in c.out_shape),
        scratch_shapes=list(scratch_shapes) + [s for c in comms for s in c.scratch_shapes],
        compiler_params=_params(len(grid)),
    )(*inputs, *[a for c in comms for a in c.inputs])
    own, rest = results[:n_out], results[n_out:]
    comm_results = []
    for _, co, _ in sizes:
        comm_results.append(rest[:co])
        rest = rest[co:]
    return own, comm_results


def _gelu(x):
    return 0.5 * x * (1.0 + lax.erf(x * INV_SQRT2))


def _gelu_and_grad(x):
    cdf = 0.5 * (1.0 + lax.erf(x * INV_SQRT2))
    return x * cdf, cdf + x * jnp.exp(-0.5 * x * x) * INV_SQRT_2PI


def _sigmoid(x):
    return 0.5 + 0.5 * jnp.tanh(0.5 * x)


def _silu_and_grad(x):
    s = _sigmoid(x)
    return x * s, s * (1.0 + x * (1.0 - s))


def _norm1(x, g1, tm, comms=()):
    t = x.shape[0]

    def body(x_ref, g_ref, h_ref):
        xv = x_ref[...]
        r = lax.rsqrt(jnp.mean(xv * xv, axis=-1, keepdims=True) + EPS)
        h_ref[...] = (xv * r * g_ref[...]).astype(BF16)

    row = pl.BlockSpec((tm, D_MODEL), lambda i: (i, 0))
    return _fused_call(body, comms, name="norm1", grid=(t // tm,), inputs=(x, g1),
                       in_specs=[row, _const_spec((1, D_MODEL))], out_specs=(row,),
                       out_shape=(jax.ShapeDtypeStruct((t, D_MODEL), BF16),))


PROJ_W = 2592
LR_REF = 1536
PROJ_ROWS = ((0, LR_REF, 0), (LR_REF + 2 * LOWRANK, PROJ_W, LR_REF), (LR_REF, LR_REF + LANE, LR_COL))


def _in_proj(h, w_in_t, tm, comms=()):
    t = h.shape[0]

    def body(h_ref, w_ref, p_ref):
        hv = h_ref[...]
        for r0, r1, c0 in PROJ_ROWS:
            p_ref[:, c0:c0 + r1 - r0] = _mm_nt(hv, w_ref[r0:r1, :]).astype(BF16)

    return _fused_call(
        body, comms, name="in_proj", grid=(t // tm,), inputs=(h, w_in_t),
        in_specs=[pl.BlockSpec((tm, D_MODEL), lambda i: (i, 0)), _const_spec((PROJ_W, D_MODEL))],
        out_specs=(pl.BlockSpec((tm, PROJ_PAD), lambda i: (i, 0)),),
        out_shape=(jax.ShapeDtypeStruct((t, PROJ_PAD), BF16),))


def _tri(upper):
    r = lax.broadcasted_iota(jnp.int32, (GLA_CHUNK, GLA_CHUNK), 0)
    c = lax.broadcasted_iota(jnp.int32, (GLA_CHUNK, GLA_CHUNK), 1)
    return jnp.where((c >= r) if upper else (c <= r), 1.0, 0.0).astype(BF16)


def _chunk_cumsum(tri, a):
    hi = a.astype(BF16)
    lo = (a - hi.astype(F32)).astype(BF16)
    dot = functools.partial(jnp.dot, preferred_element_type=F32)
    return jnp.concatenate([dot(tri, hi[_chunk_rows(c)]) + dot(tri, lo[_chunk_rows(c)])
                            for c in range(a.shape[0] // GLA_CHUNK)], axis=0)


def _chunk_rows(c):
    return slice(c * GLA_CHUNK, (c + 1) * GLA_CHUNK)


def _gla_masks(rev):
    dk_bits, dv_bits = GLA_DK.bit_length() - 1, GLA_DV.bit_length() - 1
    key_head = lax.broadcasted_iota(jnp.int32, (GLA_CHUNK, KEY_W), 1) >> dk_bits
    val_head = lax.broadcasted_iota(jnp.int32, (GLA_CHUNK, VAL_W), 1) >> dv_bits
    t = lax.broadcasted_iota(jnp.int32, (GLA_HEADS * GLA_CHUNK, GLA_CHUNK), 0) & (GLA_CHUNK - 1)
    s = lax.broadcasted_iota(jnp.int32, (GLA_HEADS * GLA_CHUNK, GLA_CHUNK), 1)
    causal = (s >= t) if rev else (s <= t)
    state_head = lax.broadcasted_iota(jnp.int32, (GLA_DV, KEY_W), 1) >> dk_bits
    return key_head, val_head, causal, state_head


def _stack_heads(a, head_of_lane):
    a = a.astype(BF16)
    return jnp.concatenate([jnp.where(head_of_lane == h, a, jnp.zeros_like(a)) for h in range(GLA_HEADS)], axis=0)


def _rows_by_head(a):
    return jnp.concatenate([a[:, h * GLA_DV:(h + 1) * GLA_DV] for h in range(GLA_HEADS)], axis=0)


def _lanes_by_head(r):
    return jnp.concatenate([r[h * GLA_CHUNK:(h + 1) * GLA_CHUNK] for h in range(GLA_HEADS)], axis=1)


def _head_diagonal(r, head_of_lane):
    rows = r.shape[0] // GLA_HEADS
    out = jnp.where(head_of_lane == 0, r[:rows], 0.0)
    for h in range(1, GLA_HEADS):
        out = out + jnp.where(head_of_lane == h, r[h * rows:(h + 1) * rows], 0.0)
    return out


def _tile_terms(la, q, k, tri, rev):
    nc = la.shape[0] // GLA_CHUNK
    q, k = q.astype(F32), k.astype(F32)
    b = _chunk_cumsum(tri, la)
    ebl = [jnp.exp(b[c * GLA_CHUNK:c * GLA_CHUNK + 1] if rev else b[(c + 1) * GLA_CHUNK - 1:(c + 1) * GLA_CHUNK])
           for c in range(nc)]
    eb = jnp.exp(b)
    enb = jnp.exp(-b)
    ee = enb * jnp.concatenate([jnp.broadcast_to(row, (GLA_CHUNK, KEY_W)) for row in ebl], axis=0)
    return ebl, eb, enb, ee, q * Q_SCALE * eb, k * enb, k * ee


def _log_decay(lr_ref, wd_ref, bd_ref):
    z = _mm(lr_ref[...], wd_ref[...]) + bd_ref[...]
    return z, jax.nn.log_sigmoid(z) * (1.0 / GLA_TAU)


def _p_specs(tg, tile):
    return [pl.BlockSpec((tg, KEY_W), lambda i: (tile(i), 0)),
            pl.BlockSpec((tg, KEY_W), lambda i: (tile(i), 1)),
            pl.BlockSpec((tg, VAL_W), lambda i: (tile(i), 1)),
            pl.BlockSpec((tg, LANE), lambda i: (tile(i), LR_COL // LANE))]


def _gla_fwd_dir(rev, nc, q_ref, k_ref, v_ref, lr_ref, wd_ref, bd_ref, o_ref, st_ref, state):
    key_head, _, causal, state_head = _gla_masks(rev)
    order = range(nc - 1, -1, -1) if rev else range(nc)

    def intra():
        _, la = _log_decay(lr_ref, wd_ref, bd_ref)
        ebl, _, _, _, qd, kd, ke = _tile_terms(la, q_ref[...], k_ref[...], _tri(rev), rev)
        kd = kd.astype(BF16)
        v = {c: v_ref[_chunk_rows(c), :].astype(BF16) for c in order}
        qd_stack = {c: _stack_heads(qd[_chunk_rows(c)], key_head) for c in order}
        ke_stack = {c: _stack_heads(ke[_chunk_rows(c)], key_head) for c in order}
        a_all = {c: _mm_nt(qd_stack[c], kd[_chunk_rows(c)]) for c in order}
        a_all = {c: jnp.where(causal, a_all[c], 0.0).astype(BF16) for c in order}
        head_rows = lambda a, h: a[h * GLA_CHUNK:(h + 1) * GLA_CHUNK]
        head_vals = lambda a, h: a[:, h * GLA_DV:(h + 1) * GLA_DV]
        r = {c: [_mm(head_rows(a_all[c], h), head_vals(v[c], h)) for h in range(GLA_HEADS)] for c in order}
        upd = {c: _mm_tn(_rows_by_head(v[c]), ke_stack[c]) for c in order}
        return {c: (ebl[c], qd_stack[c], r[c], upd[c]) for c in order}

    def scan(terms):
        st = state[...]
        states = {}
        for c in order:
            states[c] = st
            st_ref[c] = st.astype(BF16)
            st = st * terms[c][0] + terms[c][3]
        state[...] = st
        return states

    def inter(terms, states):
        r_inter = {c: _mm_nt(terms[c][1], states[c]) for c in order}
        for c in order:
            o_ref[_chunk_rows(c), :] = jnp.concatenate(
                [terms[c][2][h] + r_inter[c][h * GLA_CHUNK:(h + 1) * GLA_CHUNK] for h in range(GLA_HEADS)], axis=1)

    return intra, scan, inter


def _gla_fwd(p, wd_pad_f, bd_f, wd_pad_b, bd_b, tg, comms=()):
    t = p.shape[0]
    nt = t // tg
    nc = tg // GLA_CHUNK
    up, down = (lambda i: i), (lambda i: nt - 1 - i)

    def body(qf, kf, vf, lrf, qb, kb, vb, lrb, wdf, bdf, wdb, bdb, of, stf, ob, stb, state_f, state_b):
        @pl.when(pl.program_id(0) == 0)
        def _():
            state_f[...] = jnp.zeros_like(state_f)
            state_b[...] = jnp.zeros_like(state_b)

        dirs = [_gla_fwd_dir(False, nc, qf, kf, vf, lrf, wdf, bdf, of, stf, state_f),
                _gla_fwd_dir(True, nc, qb, kb, vb, lrb, wdb, bdb, ob, stb, state_b)]
        terms = [intra() for intra, _, _ in dirs]
        states = [scan(t) for (_, scan, _), t in zip(dirs, terms)]
        for (_, _, inter), t, s in zip(dirs, terms, states):
            inter(t, s)

    wd_spec, bd_spec = _const_spec((LANE, KEY_W)), _const_spec((1, KEY_W))
    outs = lambda tile: (pl.BlockSpec((tg, VAL_W), lambda i: (tile(i), 0)),
                         pl.BlockSpec((nc, GLA_DV, KEY_W), lambda i: (tile(i), 0, 0)))
    out_shape = (jax.ShapeDtypeStruct((t, VAL_W), F32), jax.ShapeDtypeStruct((t // GLA_CHUNK, GLA_DV, KEY_W), BF16))
    return _fused_call(
        body, comms, name="gla_fwd", grid=(nt,), inputs=(p,) * 8 + (wd_pad_f, bd_f, wd_pad_b, bd_b),
        in_specs=_p_specs(tg, up) + _p_specs(tg, down) + [wd_spec, bd_spec, wd_spec, bd_spec],
        out_specs=outs(up) + outs(down), out_shape=out_shape * 2,
        scratch_shapes=[pltpu.VMEM((GLA_DV, KEY_W), F32)] * 2)


def _gla_bwd_dir(rev, nc, q_ref, k_ref, v_ref, lr_ref, wd_ref, bd_ref, st_ref, do_ref,
                 dq_ref, dk_ref, dv_ref, dlr_ref, dwd_ref, dbd_ref, dstate):
    key_head, val_head, causal, state_head = _gla_masks(rev)
    order = range(nc) if rev else range(nc - 1, -1, -1)
    tg = nc * GLA_CHUNK

    def intra():
        z, la = _log_decay(lr_ref, wd_ref, bd_ref)
        tile = _tile_terms(la, q_ref[...], k_ref[...], _tri(rev), rev)
        qd, kd = tile[4], tile[5].astype(BF16)
        v = {c: v_ref[_chunk_rows(c), :].astype(BF16) for c in order}
        d_o = {c: do_ref[_chunk_rows(c), :] for c in order}
        kd_c = {c: kd[_chunk_rows(c)] for c in order}
        qd_stack = {c: _stack_heads(qd[_chunk_rows(c)], key_head) for c in order}
        do_stack = {c: _stack_heads(d_o[c], val_head) for c in order}
        do_rows = {c: _rows_by_head(d_o[c]) for c in order}
        a_all = {c: _mm_nt(qd_stack[c], kd_c[c]) for c in order}
        head_vals = lambda a, h: a[:, h * GLA_DV:(h + 1) * GLA_DV]
        da_all = {c: jnp.concatenate([_mm_nt(head_vals(d_o[c], h), head_vals(v[c], h)) for h in range(GLA_HEADS)],
                                     axis=0) for c in order}
        a_all = {c: jnp.where(causal, a_all[c], 0.0).astype(BF16) for c in order}
        da_all = {c: jnp.where(causal, da_all[c], 0.0).astype(BF16) for c in order}
        dv = {c: _mm_tn(a_all[c], do_stack[c]) for c in order}
        dqd = {c: _mm(jnp.concatenate([do_rows[c], da_all[c]], axis=1),
                      jnp.concatenate([st_ref[c], kd_c[c]], axis=0)) for c in order}
        dkd = {c: _mm_tn(da_all[c], qd_stack[c]) for c in order}
        upd = {c: _mm_tn(do_rows[c], qd_stack[c]) for c in order}
        dqd = {c: _head_diagonal(dqd[c], key_head) for c in order}
        return z, tile, {c: dict(dv=dv[c], dqd=dqd[c], dkd=dkd[c], upd=upd[c]) for c in order}

    def scan(tile, per):
        dst = dstate[...]
        dsts = {}
        for c in order:
            dsts[c] = dst
            dst = dst * tile[0][c] + per[c]["upd"]
        dstate[...] = dst
        return dsts

    def inter(z, tile, per, dsts):
        ebl, eb, enb, ee, qd, kd, ke = tile
        ke_stack = {c: _stack_heads(ke[_chunk_rows(c)], key_head) for c in order}
        v_rows = {c: _rows_by_head(v_ref[_chunk_rows(c), :].astype(BF16)) for c in order}
        dst_b = {c: dsts[c].astype(BF16) for c in order}
        dv_state = {c: _mm_nt(ke_stack[c], dst_b[c]) for c in order}
        dke_c = {c: _mm(v_rows[c], dst_b[c]) for c in order}
        dke_c = {c: _head_diagonal(dke_c[c], key_head) for c in order}
        dbl_c = {}
        for c in order:
            rows = _chunk_rows(c)
            dv_ref[rows, :] = (per[c]["dv"] + _lanes_by_head(dv_state[c])).astype(BF16)
            dbl = (jnp.sum(dsts[c] * st_ref[c].astype(F32), axis=0, keepdims=True) * ebl[c]
                   + jnp.sum(dke_c[c] * ke[rows], axis=0, keepdims=True))
            dbl_c[c] = jnp.broadcast_to(dbl, (GLA_CHUNK, KEY_W))
        tile_of = lambda parts: jnp.concatenate([parts[c] for c in range(nc)], axis=0)
        dqd, dkd = tile_of({c: per[c]["dqd"] for c in order}), tile_of({c: per[c]["dkd"] for c in order})
        dke, dbl = tile_of(dke_c), tile_of(dbl_c)
        dq_ref[...] = (dqd * eb * Q_SCALE).astype(BF16)
        dk_ref[...] = (dkd * enb + dke * ee).astype(BF16)
        db = dqd * qd - dkd * kd - dke * ke
        dz = (_chunk_cumsum(_tri(not rev), db) + dbl) * (_sigmoid(-z) * (1.0 / GLA_TAU))
        dlr_ref[...] = _mm_nt(dz, wd_ref[...]).astype(BF16)
        dwd_ref[...] += _mm_tn(lr_ref[...], dz)
        dbd_ref[...] += jnp.sum(dz, axis=0, keepdims=True)

    return intra, scan, inter


def _gla_bwd(p, wd_pad_f, bd_f, wd_pad_b, bd_b, st_f, st_b, d_o, tg, comms=()):
    t = p.shape[0]
    nt = t // tg
    nc = tg // GLA_CHUNK
    up, down = (lambda i: i), (lambda i: nt - 1 - i)

    def body(qf, kf, vf, lrf, stf, dof, qb, kb, vb, lrb, stb, dob, wdf, bdf, wdb, bdb,
             dqf, dkf, dvf, dlrf, dwdf, dbdf, dqb, dkb, dvb, dlrb, dwdb, dbdb, dstate_f, dstate_b):
        @pl.when(pl.program_id(0) == 0)
        def _():
            for ref in (dstate_f, dstate_b, dwdf, dbdf, dwdb, dbdb):
                ref[...] = jnp.zeros_like(ref)

        dirs = [_gla_bwd_dir(False, nc, qf, kf, vf, lrf, wdf, bdf, stf, dof, dqf, dkf, dvf, dlrf, dwdf, dbdf,
                             dstate_f),
                _gla_bwd_dir(True, nc, qb, kb, vb, lrb, wdb, bdb, stb, dob, dqb, dkb, dvb, dlrb, dwdb, dbdb,
                             dstate_b)]
        first = [intra() for intra, _, _ in dirs]
        dsts = [scan(tile, per) for (_, scan, _), (_, tile, per) in zip(dirs, first)]
        for (_, _, inter), (z, tile, per), d in zip(dirs, first, dsts):
            inter(z, tile, per, d)

    wd_spec, bd_spec = _const_spec((LANE, KEY_W)), _const_spec((1, KEY_W))
    ins = lambda tile: _p_specs(tg, tile) + [pl.BlockSpec((nc, GLA_DV, KEY_W), lambda i: (tile(i), 0, 0)),
                                             pl.BlockSpec((tg, VAL_W), lambda i: (tile(i), 0))]
    outs = lambda tile: (pl.BlockSpec((tg, KEY_W), lambda i: (tile(i), 0)),
                         pl.BlockSpec((tg, KEY_W), lambda i: (tile(i), 0)),
                         pl.BlockSpec((tg, VAL_W), lambda i: (tile(i), 0)),
                         pl.BlockSpec((tg, LANE), lambda i: (tile(i), 0)),
                         _acc_spec((LANE, KEY_W)), _acc_spec((1, KEY_W)))
    out_shape = (jax.ShapeDtypeStruct((t, KEY_W), BF16), jax.ShapeDtypeStruct((t, KEY_W), BF16),
                 jax.ShapeDtypeStruct((t, VAL_W), BF16), jax.ShapeDtypeStruct((t, LANE), BF16),
                 jax.ShapeDtypeStruct((LANE, KEY_W), F32), jax.ShapeDtypeStruct((1, KEY_W), F32))
    scratch = [pltpu.VMEM((GLA_DV, KEY_W), F32)]
    return _fused_call(
        body, comms, name="gla_bwd", grid=(nt,),
        inputs=(p, p, p, p, st_f, d_o, p, p, p, p, st_b, d_o, wd_pad_f, bd_f, wd_pad_b, bd_b),
        in_specs=ins(down) + ins(up) + [wd_spec, bd_spec, wd_spec, bd_spec],
        out_specs=outs(down) + outs(up), out_shape=out_shape * 2, scratch_shapes=scratch * 2)


def _head_rms(o):
    parts, scales = [], []
    for h in range(GLA_HEADS):
        oh = o[:, h * GLA_DV:(h + 1) * GLA_DV]
        r = lax.rsqrt(jnp.mean(oh * oh, axis=-1, keepdims=True) + EPS)
        parts.append(oh * r)
        scales.append(jnp.broadcast_to(r, oh.shape))
    return jnp.concatenate(parts, axis=1), jnp.concatenate(scales, axis=1)


def _layernorm_stats(zv):
    mu = jnp.mean(zv, axis=-1, keepdims=True)
    xc = zv - mu
    rs = lax.rsqrt(jnp.mean(xc * xc, axis=-1, keepdims=True) + EPS)
    return xc * rs, rs


def _mix_fwd(x, o_f, o_b, p, gla_g, ln_g, ln_b, w_sp, b_sp, w_out, tm, comms=()):
    t = x.shape[0]
    nch = tm // GMLP_CHUNK

    def body(x_ref, of_ref, ob_ref, pg_ref, pu_ref, pv_ref, gg_ref, lg_ref, lb_ref, ws_ref, bs_ref, wo_ref,
             x1_ref, y_ref, s_scr):
        on, _ = _head_rms(of_ref[...] + ob_ref[...])
        pg = pg_ref[...].astype(F32)
        y_a = on * gg_ref[...] * (pg * _sigmoid(pg))
        zu = _gelu(pu_ref[...].astype(F32))
        vhat, _ = _layernorm_stats(_gelu(pv_ref[...].astype(F32)))
        vln = (vhat * lg_ref[...] + lb_ref[...]).astype(BF16)
        for g in range(GMLP_GROUPS):
            w_g = ws_ref[g].astype(BF16)
            b_g = bs_ref[g]
            cols = slice(g * LANE, (g + 1) * LANE)
            for n in range(nch):
                rows = slice(n * GMLP_CHUNK, (n + 1) * GMLP_CHUNK)
                s_scr[rows, cols] = jnp.dot(w_g, vln[rows, cols], preferred_element_type=F32) + b_g
        ycat = jnp.concatenate([y_a, zu * s_scr[...]], axis=1).astype(BF16)
        y_ref[...] = ycat
        x1_ref[...] = x_ref[...] + jnp.dot(ycat, wo_ref[...], preferred_element_type=F32)

    half = lambda j: pl.BlockSpec((tm, VAL_W), lambda i: (i, j))
    return _fused_call(
        body, comms, name="mix_fwd", grid=(t // tm,),
        inputs=(x, o_f, o_b, p, p, p, gla_g, ln_g, ln_b, w_sp, b_sp, w_out),
        in_specs=[pl.BlockSpec((tm, D_MODEL), lambda i: (i, 0)), half(0), half(0), half(2), half(3), half(4),
                  _const_spec((1, VAL_W)), _const_spec((1, GMLP_W)), _const_spec((1, GMLP_W)),
                  _const_spec((GMLP_GROUPS, GMLP_CHUNK, GMLP_CHUNK)), _const_spec((GMLP_GROUPS, GMLP_CHUNK, 1)),
                  _const_spec((D_MODEL, D_MODEL))],
        out_specs=(pl.BlockSpec((tm, D_MODEL), lambda i: (i, 0)), pl.BlockSpec((tm, D_MODEL), lambda i: (i, 0))),
        out_shape=(jax.ShapeDtypeStruct((t, D_MODEL), F32), jax.ShapeDtypeStruct((t, D_MODEL), BF16)),
        scratch_shapes=[pltpu.VMEM((tm, GMLP_W), F32)])


def _mix_bwd(dx1, ycat, o_f, o_b, p, gla_g, ln_g, ln_b, w_sp, b_sp, w_out, tm, comms=()):
    t = dx1.shape[0]
    nch = tm // GMLP_CHUNK

    def body(dx1_ref, y_ref, of_ref, ob_ref, pg_ref, pu_ref, pv_ref, gg_ref, lg_ref, lb_ref, ws_ref, bs_ref, wo_ref,
             do_ref, dpg_ref, dpu_ref, dpv_ref, dwo_ref, dgg_ref, dlg_ref, dlb_ref, dws_ref, dbs_ref,
             s_scr, dvln_scr):
        @pl.when(pl.program_id(0) == 0)
        def _():
            for ref in (dwo_ref, dgg_ref, dlg_ref, dlb_ref, dws_ref, dbs_ref):
                ref[...] = jnp.zeros_like(ref)

        dx1 = dx1_ref[...].astype(BF16)
        dycat = _mm_nt(dx1, wo_ref[...])
        dwo_ref[...] += _mm_tn(y_ref[...], dx1)
        dy_a = dycat[:, :VAL_W]
        dy_b = dycat[:, VAL_W:]
        on, r = _head_rms(of_ref[...] + ob_ref[...])
        pg = pg_ref[...].astype(F32)
        sil, dsil = _silu_and_grad(pg)
        gg = gg_ref[...]
        dgg_ref[...] += jnp.sum(dy_a * sil * on, axis=0, keepdims=True)
        don = dy_a * sil * gg
        prod = don * on
        means = jnp.concatenate(
            [jnp.broadcast_to(jnp.mean(prod[:, h * GLA_DV:(h + 1) * GLA_DV], axis=-1, keepdims=True),
                              (tm, GLA_DV)) for h in range(GLA_HEADS)], axis=1)
        do_ref[...] = (r * (don - on * means)).astype(BF16)
        dpg_ref[...] = (dy_a * on * gg * dsil).astype(BF16)
        pu = pu_ref[...].astype(F32)
        pv = pv_ref[...].astype(F32)
        zu, dzu_dpu = _gelu_and_grad(pu)
        zv, dzv_dpv = _gelu_and_grad(pv)
        vhat, rs = _layernorm_stats(zv)
        lg = lg_ref[...]
        vln = (vhat * lg + lb_ref[...]).astype(BF16)
        ds32 = dy_b * zu
        ds = ds32.astype(BF16)
        blocks = [(g, n) for g in range(GMLP_GROUPS) for n in range(nch)]
        at = lambda g, n: (slice(n * GMLP_CHUNK, (n + 1) * GMLP_CHUNK), slice(g * LANE, (g + 1) * LANE))
        w_sp = [ws_ref[g].astype(BF16) for g in range(GMLP_GROUPS)]
        v_blk = {b: vln[at(*b)] for b in blocks}
        ds_blk = {b: ds[at(*b)] for b in blocks}
        s_blk = {b: jnp.dot(w_sp[b[0]], v_blk[b], preferred_element_type=F32) for b in blocks}
        dw_blk = {b: _mm_nt(ds_blk[b], v_blk[b]) for b in blocks}
        dvln_blk = {b: _mm_tn(w_sp[b[0]], ds_blk[b]) for b in blocks}
        for b in blocks:
            s_scr[at(*b)] = s_blk[b] + bs_ref[b[0]]
            dvln_scr[at(*b)] = dvln_blk[b]
        for g in range(GMLP_GROUPS):
            dws_ref[g] += sum(dw_blk[(g, n)] for n in range(nch))
            dbs_ref[g] += sum(jnp.sum(ds32[at(g, n)], axis=-1, keepdims=True) for n in range(nch))
        dpu_ref[...] = (dy_b * s_scr[...] * dzu_dpu).astype(BF16)
        dvln = dvln_scr[...]
        dlg_ref[...] += jnp.sum(dvln * vhat, axis=0, keepdims=True)
        dlb_ref[...] += jnp.sum(dvln, axis=0, keepdims=True)
        dvhat = dvln * lg
        dzv = rs * (dvhat - jnp.mean(dvhat, axis=-1, keepdims=True)
                    - vhat * jnp.mean(dvhat * vhat, axis=-1, keepdims=True))
        dpv_ref[...] = (dzv * dzv_dpv).astype(BF16)

    half = lambda j: pl.BlockSpec((tm, VAL_W), lambda i: (i, j))
    full = pl.BlockSpec((tm, D_MODEL), lambda i: (i, 0))
    sp_shape = (GMLP_GROUPS, GMLP_CHUNK, GMLP_CHUNK)
    bs_shape = (GMLP_GROUPS, GMLP_CHUNK, 1)
    return _fused_call(
        body, comms, name="mix_bwd", grid=(t // tm,),
        inputs=(dx1, ycat, o_f, o_b, p, p, p, gla_g, ln_g, ln_b, w_sp, b_sp, w_out),
        in_specs=[full, full, half(0), half(0), half(2), half(3), half(4),
                  _const_spec((1, VAL_W)), _const_spec((1, GMLP_W)), _const_spec((1, GMLP_W)),
                  _const_spec(sp_shape), _const_spec(bs_shape), _const_spec((D_MODEL, D_MODEL))],
        out_specs=(half(0), half(0), half(0), half(0), _acc_spec((D_MODEL, D_MODEL)), _acc_spec((1, VAL_W)),
                   _acc_spec((1, GMLP_W)), _acc_spec((1, GMLP_W)), _acc_spec(sp_shape), _acc_spec(bs_shape)),
        out_shape=(jax.ShapeDtypeStruct((t, VAL_W), BF16),) * 4 + (
            jax.ShapeDtypeStruct((D_MODEL, D_MODEL), F32), jax.ShapeDtypeStruct((1, VAL_W), F32),
            jax.ShapeDtypeStruct((1, GMLP_W), F32), jax.ShapeDtypeStruct((1, GMLP_W), F32),
            jax.ShapeDtypeStruct(sp_shape, F32), jax.ShapeDtypeStruct(bs_shape, F32)),
        scratch_shapes=[pltpu.VMEM((tm, GMLP_W), F32), pltpu.VMEM((tm, GMLP_W), F32)])


def _rms_bwd(dy_scaled, xn, r):
    return r * (dy_scaled - xn * jnp.mean(dy_scaled * xn, axis=-1, keepdims=True))


def _ffn(x1, target, g2, gf, w_gate, w_up, w_down, tm):
    t = x1.shape[0]

    def body(x1_ref, tg_ref, g2_ref, gf_ref, wg_ref, wu_ref, wd_ref,
             dx1_ref, h2_ref, dgate_ref, dup_ref, act_ref, dx2_ref, loss_ref, dgf_ref, dg2_ref):
        @pl.when(pl.program_id(0) == 0)
        def _():
            for ref in (loss_ref, dgf_ref, dg2_ref):
                ref[...] = jnp.zeros_like(ref)

        x1v = x1_ref[...]
        g2v = g2_ref[...]
        gfv = gf_ref[...]
        r2 = lax.rsqrt(jnp.mean(x1v * x1v, axis=-1, keepdims=True) + EPS)
        xn1 = x1v * r2
        h2 = (xn1 * g2v).astype(BF16)
        h2_ref[...] = h2
        gate = _mm_nt(h2, wg_ref[...])
        up = _mm_nt(h2, wu_ref[...])
        sil, dsil = _silu_and_grad(gate)
        act = (sil * up).astype(BF16)
        act_ref[...] = act
        x2 = x1v + jnp.dot(act, wd_ref[...], preferred_element_type=F32)
        rf = lax.rsqrt(jnp.mean(x2 * x2, axis=-1, keepdims=True) + EPS)
        xn2 = x2 * rf
        err = xn2 * gfv - tg_ref[...]
        loss_ref[...] += 0.5 * jnp.sum(jnp.mean(err * err, axis=-1, keepdims=True))
        dy = err * (1.0 / D_MODEL)
        dgf_ref[...] += jnp.sum(dy * xn2, axis=0, keepdims=True)
        dx2 = _rms_bwd(dy * gfv, xn2, rf)
        dx2b = dx2.astype(BF16)
        dx2_ref[...] = dx2b
        dact = _mm_nt(dx2b, wd_ref[...])
        dgate = (dact * up * dsil).astype(BF16)
        dup = (dact * sil).astype(BF16)
        dgate_ref[...] = dgate
        dup_ref[...] = dup
        dh2 = _mm(dgate, wg_ref[...]) + _mm(dup, wu_ref[...])
        dg2_ref[...] += jnp.sum(dh2 * xn1, axis=0, keepdims=True)
        dx1_ref[...] = dx2 + _rms_bwd(dh2 * g2v, xn1, r2)

    row = lambda w: pl.BlockSpec((tm, w), lambda i: (i, 0))
    return pl.pallas_call(
        body, name="ffn_fwd_bwd", grid=(t // tm,),
        in_specs=[row(D_MODEL), row(D_MODEL), _const_spec((1, D_MODEL)), _const_spec((1, D_MODEL)),
                  _const_spec((D_FF, D_MODEL)), _const_spec((D_FF, D_MODEL)), _const_spec((D_FF, D_MODEL))],
        out_specs=(row(D_MODEL), row(D_MODEL), row(D_FF), row(D_FF), row(D_FF), row(D_MODEL),
                   _acc_spec((8, LANE)), _acc_spec((1, D_MODEL)), _acc_spec((1, D_MODEL))),
        out_shape=(jax.ShapeDtypeStruct((t, D_MODEL), F32), jax.ShapeDtypeStruct((t, D_MODEL), BF16),
                   jax.ShapeDtypeStruct((t, D_FF), BF16), jax.ShapeDtypeStruct((t, D_FF), BF16),
                   jax.ShapeDtypeStruct((t, D_FF), BF16), jax.ShapeDtypeStruct((t, D_MODEL), BF16),
                   jax.ShapeDtypeStruct((8, LANE), F32), jax.ShapeDtypeStruct((1, D_MODEL), F32),
                   jax.ShapeDtypeStruct((1, D_MODEL), F32)),
        compiler_params=_params(),
    )(x1, target, g2, gf, w_gate, w_up, w_down)


def _matmul_tn(a, b, tm, tk, name, comms=()):
    t, m = a.shape
    n = b.shape[1]

    def body(a_ref, b_ref, o_ref):
        @pl.when(pl.program_id(1) == 0)
        def _():
            o_ref[...] = jnp.zeros_like(o_ref)

        o_ref[...] += _mm_tn(a_ref[...], b_ref[...])

    (out,), comm_results = _fused_call(
        body, comms, name=name, grid=(m // tm, t // tk), inputs=(a, b),
        in_specs=[pl.BlockSpec((tk, tm), lambda j, k: (k, j)), pl.BlockSpec((tk, n), lambda j, k: (k, 0))],
        out_specs=(pl.BlockSpec((tm, n), lambda j, k: (j, 0)),),
        out_shape=(jax.ShapeDtypeStruct((m, n), F32),))
    return out, comm_results


def _in_proj_bwd(x, g1, dx1, dq_f, dq_b, dk_f, dk_b, dv_f, dv_b, dpg, dpu, dpv, dlr_f, dlr_b, w_main, tm, comms=()):
    t = x.shape[0]

    def body(x_ref, g_ref, dx1_ref, dqf, dqb, dkf, dkb, dvf, dvb, dg, du, dv, dlf, dlb, w_ref,
             dx_ref, dp_ref, dg1_ref):
        @pl.when(pl.program_id(0) == 0)
        def _():
            dg1_ref[...] = jnp.zeros_like(dg1_ref)

        both = lambda a, b: (a[...].astype(F32) + b[...].astype(F32)).astype(BF16)
        dp = jnp.concatenate([both(dqf, dqb), both(dkf, dkb), both(dvf, dvb), dg[...], du[...], dv[...],
                              both(dlf, dlb)], axis=1)
        dp_ref[...] = dp
        dh = sum(_mm(dp[:, c0:c0 + r1 - r0], w_ref[r0:r1, :]) for r0, r1, c0 in PROJ_ROWS)
        xv = x_ref[...]
        r = lax.rsqrt(jnp.mean(xv * xv, axis=-1, keepdims=True) + EPS)
        xn = xv * r
        dg1_ref[...] += jnp.sum(dh * xn, axis=0, keepdims=True)
        dx_ref[...] = dx1_ref[...] + _rms_bwd(dh * g_ref[...], xn, r)

    row = lambda w: pl.BlockSpec((tm, w), lambda i: (i, 0))
    return _fused_call(
        body, comms, name="in_proj_bwd", grid=(t // tm,),
        inputs=(x, g1, dx1, dq_f, dq_b, dk_f, dk_b, dv_f, dv_b, dpg, dpu, dpv, dlr_f, dlr_b, w_main),
        in_specs=[row(D_MODEL), _const_spec((1, D_MODEL)), row(D_MODEL), row(KEY_W), row(KEY_W), row(KEY_W),
                  row(KEY_W), row(VAL_W), row(VAL_W), row(VAL_W), row(VAL_W), row(VAL_W), row(LANE), row(LANE),
                  _const_spec((PROJ_W, D_MODEL))],
        out_specs=(row(D_MODEL), row(PROJ_PAD), _acc_spec((1, D_MODEL))),
        out_shape=(jax.ShapeDtypeStruct((t, D_MODEL), F32), jax.ShapeDtypeStruct((t, PROJ_PAD), BF16),
                   jax.ShapeDtypeStruct((1, D_MODEL), F32)))


def _adamw(w, g, m, v):
    m_new = ADAM_B1 * m + (1.0 - ADAM_B1) * g
    v_new = ADAM_B2 * v + (1.0 - ADAM_B2) * (g * g)
    m_hat = m_new / (1.0 - ADAM_B1 ** ADAM_STEP)
    v_hat = v_new / (1.0 - ADAM_B2 ** ADAM_STEP)
    delta = -ADAM_LR * (m_hat / (jnp.sqrt(v_hat) + ADAM_EPS) + ADAM_WD * w)
    return delta, m_new, v_new


def _adamw_shard(own, recv, w, m, v, tr, name):
    r, c = w.shape

    def body(own_ref, recv_ref, w_ref, m_ref, v_ref, g_ref, d_ref, nm_ref, nv_ref):
        g = own_ref[...]
        for k in range(3):
            g = g + recv_ref[k].astype(F32)
        g_ref[...] = g
        d_ref[...], nm_ref[...], nv_ref[...] = _adamw(w_ref[...], g, m_ref[...], v_ref[...])

    row = pl.BlockSpec((tr, c), lambda i: (i, 0))
    return pl.pallas_call(
        body, name=name, grid=(r // tr,),
        in_specs=[row, pl.BlockSpec((3, tr, c), lambda i: (0, i, 0)), row, row, row],
        out_specs=(row,) * 4, out_shape=(jax.ShapeDtypeStruct((r, c), F32),) * 4,
        compiler_params=_params(),
    )(own, recv, w, m, v)


def _adamw_small(entries):
    stacks = []
    for (g, _, _), _, _, _ in entries:
        if not any(g is s for s in stacks):
            stacks.append(g)
    where = [next(i for i, s in enumerate(stacks) if s is g) for (g, _, _), _, _, _ in entries]
    ns, ne = len(stacks), len(entries)

    def body(*refs):
        s_refs, wmv, outs = refs[:ns], refs[ns:ns + 3 * ne], refs[ns + 3 * ne:]
        for e, ((_, r0, nr), _, _, _) in enumerate(entries):
            grad = s_refs[where[e]][r0:r0 + nr, :]
            w_ref, m_ref, v_ref = wmv[3 * e:3 * e + 3]
            g_ref, d_ref, nm_ref, nv_ref = outs[4 * e:4 * e + 4]
            g_ref[...] = grad
            d_ref[...], nm_ref[...], nv_ref[...] = _adamw(w_ref[...], grad, m_ref[...], v_ref[...])

    results = pl.pallas_call(
        body, name="adamw_small",
        out_shape=tuple(jax.ShapeDtypeStruct(w.shape, F32) for _, w, _, _ in entries for _ in range(4)),
        compiler_params=pltpu.CompilerParams(vmem_limit_bytes=VMEM_LIMIT),
    )(*stacks, *[a for _, w, m, v in entries for a in (w, m, v)])
    return [results[4 * e:4 * e + 4] for e in range(ne)]


def _mesh_pos():
    return lax.axis_index("x"), lax.axis_index("y"), lax.axis_index("c")


def _other_chips(x, y):
    return [(x, 1 - y), (1 - x, y), (1 - x, 1 - y)]


_VMEM_WHOLE = pl.BlockSpec(memory_space=pltpu.VMEM)
_HBM_WHOLE = pl.BlockSpec(memory_space=pl.ANY)


def _gather_comm(shards, cast, mid=(1, 2)):
    na = len(shards)
    staged = [a for a in range(na) if cast[a]]

    def phases(in_refs, out_refs, scr):
        stage = dict(zip(staged, scr[:len(staged)]))
        send_sems, recv_sems, local_sems = scr[len(staged):]
        x, y, c = _mesh_pos()
        me, sibling = (x, y, c), (x, y, 1 - c)
        chips = _other_chips(x, y)
        srcs = [stage[a] if cast[a] else in_refs[a] for a in range(na)]

        def rows(a, pos):
            px, py, pc = pos
            return out_refs[a].at[4 * px + 2 * py + pc]

        def copy(a, k, block, to, src=None):
            return pltpu.make_async_remote_copy(
                src_ref=rows(a, block) if src is None else src, dst_ref=rows(a, block),
                send_sem=send_sems.at[a, k], recv_sem=recv_sems.at[a, k], device_id=to, device_id_type=MESH_ID)

        mine = [pltpu.make_async_copy(srcs[a], rows(a, me), local_sems.at[a]) for a in range(na)]
        first = []
        for a in range(na):
            first.append(copy(a, 0, me, sibling, src=srcs[a]))
            first += [copy(a, 1 + j, me, (*chip, c), src=srcs[a]) for j, chip in enumerate(chips)]
        passed = [copy(a, 4 + j, (*chip, c), sibling) for j, chip in enumerate(chips) for a in range(na)]

        def start():
            for a in staged:
                stage[a][...] = in_refs[a][...].astype(BF16)
            for cp in mine + first:
                cp.start()

        def forward():
            i = 0
            for j, chip in enumerate(chips):
                for a in range(na):
                    copy(a, 1 + j, (*chip, c), me).wait_recv()
                    passed[i].start()
                    i += 1

        def finish():
            for a in range(na):
                copy(a, 0, sibling, me).wait_recv()
                for j, chip in enumerate(chips):
                    copy(a, 4 + j, (*chip, 1 - c), me).wait_recv()
            for cp in first + passed:
                cp.wait_send()
            for cp in mine:
                cp.wait()

        return start, forward, finish

    def before(step, nsteps, in_refs, out_refs, scr):
        start, forward, _ = phases(in_refs, out_refs, scr)
        pl.when(step == 0)(start)
        pl.when(step == nsteps * mid[0] // mid[1])(forward)

    def after(step, nsteps, in_refs, out_refs, scr):
        pl.when(step == nsteps - 1)(phases(in_refs, out_refs, scr)[2])

    return _Comm(
        inputs=list(shards), in_specs=[_VMEM_WHOLE] * na,
        out_shape=[jax.ShapeDtypeStruct((N_DEV,) + s.shape, BF16 if cast[a] else s.dtype)
                   for a, s in enumerate(shards)],
        out_specs=[_HBM_WHOLE] * na,
        scratch_shapes=[pltpu.VMEM(shards[a].shape, BF16) for a in staged] + [
            pltpu.SemaphoreType.DMA((na, 7)), pltpu.SemaphoreType.DMA((na, 7)), pltpu.SemaphoreType.DMA((na,))],
        before=before, after=after)


def _exchange_comm(arrays, out_shape, make_copies):
    na = len(arrays)

    def copies(in_refs, out_refs, scr):
        return make_copies(in_refs, out_refs, *scr)

    def before(step, nsteps, in_refs, out_refs, scr):
        @pl.when(step == 0)
        def _():
            for cp in copies(in_refs, out_refs, scr):
                cp.start()

    def after(step, nsteps, in_refs, out_refs, scr):
        @pl.when(step == nsteps - 1)
        def _():
            for cp in copies(in_refs, out_refs, scr):
                cp.wait()

    return _Comm(inputs=list(arrays), in_specs=[_HBM_WHOLE] * na, out_shape=list(out_shape),
                 out_specs=[_HBM_WHOLE] * na,
                 scratch_shapes=[pltpu.SemaphoreType.DMA((na, 3)), pltpu.SemaphoreType.DMA((na, 3))],
                 before=before, after=after)


def _sibling_exchange_comm(grads):
    def make_copies(in_refs, out_refs, send_sems, recv_sems):
        x, y, c = _mesh_pos()
        return [pltpu.make_async_remote_copy(
            src_ref=in_refs[a].at[:, pl.ds(1 - c, 1)], dst_ref=out_refs[a], send_sem=send_sems.at[a, 0],
            recv_sem=recv_sems.at[a, 0], device_id=(x, y, 1 - c), device_id_type=MESH_ID)
            for a in range(len(grads))]

    return _exchange_comm(grads, [jax.ShapeDtypeStruct((4, 1) + g.shape[2:], F32) for g in grads], make_copies)


def _chips_exchange_comm(partials):
    def make_copies(in_refs, out_refs, send_sems, recv_sems):
        x, y, c = _mesh_pos()
        return [pltpu.make_async_remote_copy(
            src_ref=in_refs[a].at[j], dst_ref=out_refs[a].at[j], send_sem=send_sems.at[a, j],
            recv_sem=recv_sems.at[a, j], device_id=(*chip, c), device_id_type=MESH_ID)
            for a in range(len(partials)) for j, chip in enumerate(_other_chips(x, y))]

    return _exchange_comm(partials, [jax.ShapeDtypeStruct(g.shape, BF16) for g in partials], make_copies)


def _comm_only(comms, name):
    return _fused_call(lambda: None, comms, name=name, grid=(1,), inputs=(), in_specs=[], out_specs=(),
                       out_shape=())[1]


def _chip_sum(my_pos, mine, from_sibling, tr, name):
    _, _, r, c = mine.shape

    def body(pos_ref, a_ref, b_ref, own_ref, out_ref):
        s = a_ref[0, 0] + b_ref[0, 0]

        @pl.when(pl.program_id(1) == 0)
        def _():
            own_ref[...] = s

        @pl.when(pl.program_id(1) > 0)
        def _():
            out_ref[0] = s.astype(BF16)

    grid_spec = pltpu.PrefetchScalarGridSpec(
        num_scalar_prefetch=1, grid=(r // tr, 4),
        in_specs=[pl.BlockSpec((1, 1, tr, c), lambda i, k, pos: (pos[0] ^ k, pos[1], i, 0)),
                  pl.BlockSpec((1, 1, tr, c), lambda i, k, pos: (pos[0] ^ k, 0, i, 0))],
        out_specs=(pl.BlockSpec((tr, c), lambda i, k, pos: (i, 0)),
                   pl.BlockSpec((1, tr, c), lambda i, k, pos: (jnp.maximum(k - 1, 0), i, 0))))
    return pl.pallas_call(
        body, name=name, grid_spec=grid_spec,
        out_shape=(jax.ShapeDtypeStruct((r, c), F32), jax.ShapeDtypeStruct((3, r, c), BF16)),
        compiler_params=_params(2),
    )(my_pos, mine, from_sibling)


def _all_reduce_small_comm(parts):
    na = len(parts)

    def copies(in_refs, scr):
        gathered, (send_sems, recv_sems) = scr[:na], scr[na:]
        x, y, c = _mesh_pos()
        my_id = 4 * x + 2 * y + c
        return my_id, [pltpu.make_async_remote_copy(
            src_ref=in_refs[a], dst_ref=gathered[a].at[my_id], send_sem=send_sems.at[a, k - 1],
            recv_sem=recv_sems.at[a, k - 1], device_id=(x ^ (k >> 2), y ^ ((k >> 1) & 1), c ^ (k & 1)),
            device_id_type=MESH_ID) for a in range(na) for k in range(1, N_DEV)]

    def before(step, nsteps, in_refs, out_refs, scr):
        @pl.when(step == 0)
        def _():
            for cp in copies(in_refs, scr)[1]:
                cp.start()

    def after(step, nsteps, in_refs, out_refs, scr):
        @pl.when(step == nsteps - 1)
        def _():
            my_id, cps = copies(in_refs, scr)
            for a in range(na):
                scr[a][my_id] = in_refs[a][...]
            for cp in cps:
                cp.wait()
            for a in range(na):
                acc = scr[a][0]
                for d in range(1, N_DEV):
                    acc = acc + scr[a][d]
                out_refs[a][...] = acc

    return _Comm(inputs=list(parts), in_specs=[_VMEM_WHOLE] * na,
                 out_shape=[jax.ShapeDtypeStruct(p.shape, F32) for p in parts], out_specs=[_VMEM_WHOLE] * na,
                 scratch_shapes=[pltpu.VMEM((N_DEV,) + p.shape, F32) for p in parts] + [
                     pltpu.SemaphoreType.DMA((na, N_DEV - 1)), pltpu.SemaphoreType.DMA((na, N_DEV - 1))],
                 before=before, after=after)


def _unshard_cols(g):
    return jnp.transpose(g, (1, 0, 2)).reshape(g.shape[1], N_DEV * g.shape[2])


def _row_blocks(w):
    return w.reshape(4, 2, w.shape[0] // N_DEV, w.shape[1])


def _stack_rows(parts):
    a = jnp.concatenate(parts, axis=0)
    return jnp.pad(a, ((0, (-a.shape[0]) % 8), (0, 0)))


def _w_in_grad_blocks(dw):
    r = PROJ_W // N_DEV
    pieces = ((0, LR_REF, 0), (LR_REF, LR_REF + 2 * LOWRANK, LR_COL), (LR_REF + 2 * LOWRANK, PROJ_W, LR_REF))

    def ref_rows(a, b):
        parts = [dw[src + max(a, lo) - lo:src + min(b, hi) - lo] for lo, hi, src in pieces if max(a, lo) < min(b, hi)]
        return parts[0] if len(parts) == 1 else jnp.concatenate(parts, axis=0)

    return jnp.stack([ref_rows(d * r, (d + 1) * r) for d in range(N_DEV)]).reshape(4, 2, r, D_MODEL)


def _padded_decay_weights(wd_f, wd_b):
    zeros = lambda n: jnp.zeros((n, KEY_W), F32)
    return (jnp.concatenate([wd_f, zeros(LANE - LOWRANK)], axis=0),
            jnp.concatenate([zeros(LOWRANK), wd_b, zeros(LANE - 2 * LOWRANK)], axis=0))


def kernel(x, norm1_g, w_in,w_decay_f, b_decay_f, w_decay_b, b_decay_b, gla_norm_g, gmlp_ln_g, gmlp_ln_b, w_spatial, b_spatial, w_out, norm2_g, w_gate, w_up, w_down, final_norm_g, loss_target, m_norm1_g, m_w_in, m_w_decay_f, m_b_decay_f, m_w_decay_b, m_b_decay_b, m_gla_norm_g, m_gmlp_ln_g, m_gmlp_ln_b, m_w_spatial, m_b_spatial, m_w_out, m_norm2_g, m_w_gate, m_w_up, m_w_down, m_final_norm_g, v_norm1_g, v_w_in, v_w_decay_f, v_b_decay_f, v_w_decay_b, v_b_decay_b, v_gla_norm_g, v_gmlp_ln_g, v_gmlp_ln_b, v_w_spatial, v_b_spatial, v_w_out, v_norm2_g, v_w_gate, v_w_up, v_w_down, v_final_norm_g):
    t = x.shape[1]
    xt = x[0]
    target = loss_target[0]
    pos_x, pos_y, pos_c = _mesh_pos()
    my_pos = jnp.stack([2 * pos_x + pos_y, pos_c]).astype(jnp.int32)
    my_id = 4 * pos_x + 2 * pos_y + pos_c

    tile = lambda n: min(n, t)
    ln_g, ln_b, w_sp = gmlp_ln_g, gmlp_ln_b, w_spatial[0]
    b_sp_col = b_spatial[0][:, :, None]
    shard = {"w_in": w_in[0].T, "w_out": w_out[0], "w_gate": w_gate[0].T, "w_up": w_up[0].T, "w_down": w_down[0]}
    shard_m = {"w_in": m_w_in[0].T, "w_out": m_w_out[0], "w_gate": m_w_gate[0].T, "w_up": m_w_up[0].T,
               "w_down": m_w_down[0]}
    shard_v = {"w_in": v_w_in[0].T, "w_out": v_w_out[0], "w_gate": v_w_gate[0].T, "w_up": v_w_up[0].T,
               "w_down": v_w_down[0]}
    transposed = ("w_in", "w_gate", "w_up")
    chip_sum = lambda n, g, s: _chip_sum(my_pos, g, s[0], g.shape[2], "chip_sum_" + n)

    decay_shard = jnp.stack([w_decay_f[0], w_decay_b[0]])
    (hb,), ((g_in, g_decay),) = _norm1(xt, norm1_g, tile(512),
                                       [_gather_comm([shard["w_in"], decay_shard], [True, False])])
    w_in_t = g_in.reshape(PROJ_W, D_MODEL)
    wd_pad_f, wd_pad_b = _padded_decay_weights(_unshard_cols(g_decay[:, 0]), _unshard_cols(g_decay[:, 1]))
    (p,), ((g_gate, g_out),) = _in_proj(
        hb, w_in_t, tile(512), [_gather_comm([shard["w_gate"], shard["w_out"]], [True, True], mid=(7, 8))])
    (o_f, st_f, o_b, st_b), ((g_up, g_down),) = _gla_fwd(
        p, wd_pad_f, b_decay_f, wd_pad_b, b_decay_b, tile(512),
        [_gather_comm([shard["w_up"], shard["w_down"]], [True, True], mid=(7, 8))])
    w_out_full = g_out.reshape(D_MODEL, D_MODEL)
    (x1, ycat), _ = _mix_fwd(xt, o_f, o_b, p, gla_norm_g, ln_g, ln_b, w_sp, b_sp_col, w_out_full, tile(1024))

    dx1, h2b, dgate, dup, act, dx2, loss_acc, d_gf, d_g2 = _ffn(
        x1, target, norm2_g, final_norm_g[None, :], g_gate.reshape(D_FF, D_MODEL), g_up.reshape(D_FF, D_MODEL),
        g_down.reshape(D_FF, D_MODEL), tile(256))
    dw_gate, _ = _matmul_tn(dgate, h2b, D_FF // 2, tile(2048), "grad_w_gate")
    dw_up, _ = _matmul_tn(dup, h2b, D_FF // 2, tile(2048), "grad_w_up")
    dw_down, _ = _matmul_tn(act, dx2, D_FF // 2, tile(2048), "grad_w_down")

    ffn_grads = [_row_blocks(dw_gate), _row_blocks(dw_up), _row_blocks(dw_down)]
    (d_o, dpg, dpu, dpv, dw_out, d_gg, d_lg, d_lb, dw_sp, db_sp), (ffn_sib,) = _mix_bwd(
        dx1, ycat, o_f, o_b, p, gla_norm_g, ln_g, ln_b, w_sp, b_sp_col, w_out_full, tile(512),
        [_sibling_exchange_comm(ffn_grads)])
    ffn_names = ["w_gate", "w_up", "w_down"]
    ffn_sums = [chip_sum(n, g, [s]) for n, g, s in zip(ffn_names, ffn_grads, ffn_sib)]
    out_grad = _row_blocks(dw_out)
    (dq_f, dk_f, dv_f, dlr_f, dwd_f, dbd_f, dq_b, dk_b, dv_b, dlr_b, dwd_b, dbd_b), (ffn_recv, out_sib) = _gla_bwd(
        p, wd_pad_f, b_decay_f, wd_pad_b, b_decay_b, st_f, st_b, d_o, tile(512),
        [_chips_exchange_comm([s[1] for s in ffn_sums]), _sibling_exchange_comm([out_grad])])
    out_sum = chip_sum("w_out", out_grad, out_sib)
    (grad_x, dp, d_g1), _ = _in_proj_bwd(
        xt, norm1_g, dx1, dq_f, dq_b, dk_f, dk_b, dv_f, dv_b, dpg, dpu, dpv, dlr_f, dlr_b, w_in_t, tile(512))

    stacks = [_stack_rows([d_g1, d_g2, d_gf]), _stack_rows([d_gg, d_lg, d_lb]),
              _stack_rows([dbd_f, dbd_b, jnp.zeros((6, KEY_W), F32), dwd_f[:LOWRANK], dwd_b[LOWRANK:2 * LOWRANK]]),
              _stack_rows([dw_sp.reshape(GMLP_W, GMLP_CHUNK), db_sp[:, :, 0], loss_acc[:1]])]
    dw_main, (small_sums, out_recv) = _matmul_tn(
        dp, hb, PROJ_PAD // 3, tile(2048), "grad_w_in",
        [_all_reduce_small_comm(stacks), _chips_exchange_comm([out_sum[1]])])
    in_grad = _w_in_grad_blocks(dw_main)
    (in_sib,) = _comm_only([_sibling_exchange_comm([in_grad])], "grad_w_in_exchange_sibling")
    in_sum = chip_sum("w_in", in_grad, in_sib)
    (in_recv,) = _comm_only([_chips_exchange_comm([in_sum[1]])], "grad_w_in_exchange_chips")

    names = ["w_in", "w_out", "w_gate", "w_up", "w_down"]
    sums = [in_sum, out_sum] + ffn_sums
    received = [in_recv[0], out_recv[0]] + list(ffn_recv)
    big_out = {}
    for n, s, rc in zip(names, sums, received):
        res = _adamw_shard(s[0], rc, shard[n], shard_m[n], shard_v[n], shard[n].shape[0], "adamw_" + n)
        big_out[n] = [r.T if n in transposed else r for r in res]

    s1024, s512, s256, s128 = small_sums
    loss = s128[GMLP_W + GMLP_GROUPS, 0]
    col0 = my_id * (KEY_W // N_DEV)
    decay_cols = lambda row0: lax.dynamic_slice(s256, (row0, col0), (LOWRANK, KEY_W // N_DEV))
    flat = lambda a: a.reshape(-1, a.shape[-1])
    small = {
        "norm1_g": ((s1024, 0, 1), norm1_g, m_norm1_g, v_norm1_g),
        "w_decay_f": ((decay_cols(8), 0, LOWRANK), w_decay_f, m_w_decay_f, v_w_decay_f),
        "b_decay_f": ((s256, 0, 1), b_decay_f, m_b_decay_f, v_b_decay_f),
        "w_decay_b": ((decay_cols(8 + LOWRANK), 0, LOWRANK), w_decay_b, m_w_decay_b, v_w_decay_b),
        "b_decay_b": ((s256, 1, 1), b_decay_b, m_b_decay_b, v_b_decay_b),
        "gla_norm_g": ((s512, 0, 1), gla_norm_g, m_gla_norm_g, v_gla_norm_g),
        "gmlp_ln_g": ((s512, 1, 1), gmlp_ln_g, m_gmlp_ln_g, v_gmlp_ln_g),
        "gmlp_ln_b": ((s512, 2, 1), gmlp_ln_b, m_gmlp_ln_b, v_gmlp_ln_b),
        "w_spatial": ((s128, 0, GMLP_W), w_spatial, m_w_spatial, v_w_spatial),
        "b_spatial": ((s128, GMLP_W, GMLP_GROUPS), b_spatial, m_b_spatial, v_b_spatial),
        "norm2_g": ((s1024, 1, 1), norm2_g, m_norm2_g, v_norm2_g),
        "final_norm_g": ((s1024, 2, 1), final_norm_g, m_final_norm_g, v_final_norm_g),
    }
    small_res = _adamw_small([(g, flat(w), flat(m), flat(v)) for g, w, m, v in small.values()])
    small_out = {n: [r.reshape(small[n][1].shape) for r in res] for n, res in zip(small, small_res)}

    order = ["norm1_g", "w_in", "w_decay_f", "b_decay_f", "w_decay_b", "b_decay_b", "gla_norm_g", "gmlp_ln_g",
             "gmlp_ln_b", "w_spatial", "b_spatial", "w_out", "norm2_g", "w_gate", "w_up", "w_down", "final_norm_g"]
    outs = []
    for kind in range(4):
        for n in order:
            outs.append(big_out[n][kind][None] if n in big_out else small_out[n][kind])
    return (loss, grad_x[None], *outs)
```

```python
import functools
import math

import jax
import jax.numpy as jnp
from jax import lax
from jax.experimental import pallas as pl
from jax.experimental.pallas import tpu as pltpu

F32 = jnp.float32
BF16 = jnp.bfloat16

D_MODEL = 1024
GLA_HEADS = 4
GLA_DK = 64
GLA_DV = 128
KEY_W = GLA_HEADS * GLA_DK
VAL_W = GLA_HEADS * GLA_DV
LOWRANK = 16
GLA_TAU = 16.0
GLA_CHUNK = 64
GMLP_W = 512
GMLP_GROUPS = 4
GMLP_CHUNK = 128
D_FF = 2816
EPS = 1e-6
Q_SCALE = GLA_DK ** -0.5
PROJ_PAD = 2688
LR_COL = 2560
LANE = 128
N_DEV = 8

ADAM_LR = 0.001
ADAM_B1 = 0.9
ADAM_B2 = 0.999
ADAM_EPS = 1e-08
ADAM_WD = 0.01
ADAM_STEP = 10

VMEM_LIMIT = 56 * 1024 * 1024
MESH_ID = pl.DeviceIdType.MESH
INV_SQRT2 = 0.7071067811865476
INV_SQRT_2PI = 0.3989422804014327


def _params(n_axes=1):
    return pltpu.CompilerParams(dimension_semantics=("arbitrary",) * n_axes, vmem_limit_bytes=VMEM_LIMIT)


def _mm(a, b):
    return jnp.dot(a.astype(BF16), b.astype(BF16), preferred_element_type=F32)


def _mm_nt(a, b):
    return lax.dot_general(a.astype(BF16), b.astype(BF16), (((1,), (1,)), ((), ())), preferred_element_type=F32)


def _mm_tn(a, b):
    return lax.dot_general(a.astype(BF16), b.astype(BF16), (((0,), (0,)), ((), ())), preferred_element_type=F32)


def _const_spec(shape):
    nd = len(shape)
    return pl.BlockSpec(shape, lambda *_: (0,) * nd, pipeline_mode=pl.Buffered(1))


def _acc_spec(shape):
    nd = len(shape)
    return pl.BlockSpec(shape, lambda *_: (0,) * nd)


class _Comm:
    def __init__(self, inputs, in_specs, out_shape, out_specs, scratch_shapes, before, after):
        self.inputs, self.in_specs, self.out_shape, self.out_specs = inputs, in_specs, out_shape, out_specs
        self.scratch_shapes, self.before, self.after = scratch_shapes, before, after


def _fused_call(body, comms, *, name, grid, inputs, in_specs, out_specs, out_shape, scratch_shapes=()):
    n_in, n_out, n_scr = len(in_specs), len(out_specs), len(scratch_shapes)
    nsteps = math.prod(grid)
    sizes = [(len(c.inputs), len(c.out_shape), len(c.scratch_shapes)) for c in comms]

    def full_body(*refs):
        step = pl.program_id(0)
        for axis in range(1, len(grid)):
            step = step * grid[axis] + pl.program_id(axis)
        ins, rest = refs[:n_in], refs[n_in:]
        c_ins = []
        for ci, _, _ in sizes:
            c_ins.append(rest[:ci])
            rest = rest[ci:]
        outs, rest = rest[:n_out], rest[n_out:]
        c_outs = []
        for _, co, _ in sizes:
            c_outs.append(rest[:co])
            rest = rest[co:]
        scr, rest = rest[:n_scr], rest[n_scr:]
        c_scr = []
        for _, _, cs in sizes:
            c_scr.append(rest[:cs])
            rest = rest[cs:]
        for c, a, b, s in zip(comms, c_ins, c_outs, c_scr):
            c.before(step, nsteps, a, b, s)
        body(*ins, *outs, *scr)
        for c, a, b, s in zip(comms, c_ins, c_outs, c_scr):
            c.after(step, nsteps, a, b, s)

    results = pl.pallas_call(
        full_body, name=name, grid=grid,
        in_specs=list(in_specs) + [s for c in comms for s in c.in_specs],
        out_specs=tuple(out_specs) + tuple(s for c in comms for s in c.out_specs),
        out_shape=tuple(out_shape) + tuple(s for c in comms for s in c.out_shape),
        scratch_shapes=list(scratch_shapes) + [s for c in comms for s in c.scratch_shapes],
        compiler_params=_params(len(grid)),
    )(*inputs, *[a for c in comms for a in c.inputs])
    own, rest = results[:n_out], results[n_out:]
    comm_results = []
    for _, co, _ in sizes:
        comm_results.append(rest[:co])
        rest = rest[co:]
    return own, comm_results


def _gelu(x):
    return 0.5 * x * (1.0 + lax.erf(x * INV_SQRT2))


def _gelu_and_grad(x):
    cdf = 0.5 * (1.0 + lax.erf(x * INV_SQRT2))
    return x * cdf, cdf + x * jnp.exp(-0.5 * x * x) * INV_SQRT_2PI


def _sigmoid(x):
    return 0.5 + 0.5 * jnp.tanh(0.5 * x)


def _silu_and_grad(x):
    s = _sigmoid(x)
    return x * s, s * (1.0 + x * (1.0 - s))


def _norm1(x, g1, tm, comms=()):
    t = x.shape[0]

    def body(x_ref, g_ref, h_ref):
        xv = x_ref[...]
        r = lax.rsqrt(jnp.mean(xv * xv, axis=-1, keepdims=True) + EPS)
        h_ref[...] = (xv * r * g_ref[...]).astype(BF16)

    row = pl.BlockSpec((tm, D_MODEL), lambda i: (i, 0))
    return _fused_call(body, comms, name="norm1", grid=(t // tm,), inputs=(x, g1),
                       in_specs=[row, _const_spec((1, D_MODEL))], out_specs=(row,),
                       out_shape=(jax.ShapeDtypeStruct((t, D_MODEL), BF16),))


PROJ_W = 2592
LR_REF = 1536
PROJ_ROWS = ((0, LR_REF, 0), (LR_REF + 2 * LOWRANK, PROJ_W, LR_REF), (LR_REF, LR_REF + LANE, LR_COL))


def _in_proj(h, w_in_t, tm, comms=()):
    t = h.shape[0]

    def body(h_ref, w_ref, p_ref):
        hv = h_ref[...]
        for r0, r1, c0 in PROJ_ROWS:
            p_ref[:, c0:c0 + r1 - r0] = _mm_nt(hv, w_ref[r0:r1, :]).astype(BF16)

    return _fused_call(
        body, comms, name="in_proj", grid=(t // tm,), inputs=(h, w_in_t),
        in_specs=[pl.BlockSpec((tm, D_MODEL), lambda i: (i, 0)), _const_spec((PROJ_W, D_MODEL))],
        out_specs=(pl.BlockSpec((tm, PROJ_PAD), lambda i: (i, 0)),),
        out_shape=(jax.ShapeDtypeStruct((t, PROJ_PAD), BF16),))


def _tri(upper):
    r = lax.broadcasted_iota(jnp.int32, (GLA_CHUNK, GLA_CHUNK), 0)
    c = lax.broadcasted_iota(jnp.int32, (GLA_CHUNK, GLA_CHUNK), 1)
    return jnp.where((c >= r) if upper else (c <= r), 1.0, 0.0).astype(BF16)


def _chunk_cumsum(tri, a):
    hi = a.astype(BF16)
    lo = (a - hi.astype(F32)).astype(BF16)
    dot = functools.partial(jnp.dot, preferred_element_type=F32)
    return jnp.concatenate([dot(tri, hi[_chunk_rows(c)]) + dot(tri, lo[_chunk_rows(c)])
                            for c in range(a.shape[0] // GLA_CHUNK)], axis=0)


def _chunk_rows(c):
    return slice(c * GLA_CHUNK, (c + 1) * GLA_CHUNK)


def _gla_masks(rev):
    dk_bits, dv_bits = GLA_DK.bit_length() - 1, GLA_DV.bit_length() - 1
    key_head = lax.broadcasted_iota(jnp.int32, (GLA_CHUNK, KEY_W), 1) >> dk_bits
    val_head = lax.broadcasted_iota(jnp.int32, (GLA_CHUNK, VAL_W), 1) >> dv_bits
    t = lax.broadcasted_iota(jnp.int32, (GLA_HEADS * GLA_CHUNK, GLA_CHUNK), 0) & (GLA_CHUNK - 1)
    s = lax.broadcasted_iota(jnp.int32, (GLA_HEADS * GLA_CHUNK, GLA_CHUNK), 1)
    causal = (s >= t) if rev else (s <= t)
    state_head = lax.broadcasted_iota(jnp.int32, (GLA_DV, KEY_W), 1) >> dk_bits
    return key_head, val_head, causal, state_head


def _stack_heads(a, head_of_lane):
    a = a.astype(BF16)
    return jnp.concatenate([jnp.where(head_of_lane == h, a, jnp.zeros_like(a)) for h in range(GLA_HEADS)], axis=0)


def _rows_by_head(a):
    return jnp.concatenate([a[:, h * GLA_DV:(h + 1) * GLA_DV] for h in range(GLA_HEADS)], axis=0)


def _lanes_by_head(r):
    return jnp.concatenate([r[h * GLA_CHUNK:(h + 1) * GLA_CHUNK] for h in range(GLA_HEADS)], axis=1)


def _head_diagonal(r, head_of_lane):
    rows = r.shape[0] // GLA_HEADS
    out = jnp.where(head_of_lane == 0, r[:rows], 0.0)
    for h in range(1, GLA_HEADS):
        out = out + jnp.where(head_of_lane == h, r[h * rows:(h + 1) * rows], 0.0)
    return out


def _tile_terms(la, q, k, tri, rev):
    nc = la.shape[0] // GLA_CHUNK
    q, k = q.astype(F32), k.astype(F32)
    b = _chunk_cumsum(tri, la)
    ebl = [jnp.exp(b[c * GLA_CHUNK:c * GLA_CHUNK + 1] if rev else b[(c + 1) * GLA_CHUNK - 1:(c + 1) * GLA_CHUNK])
           for c in range(nc)]
    eb = jnp.exp(b)
    enb = jnp.exp(-b)
    ee = enb * jnp.concatenate([jnp.broadcast_to(row, (GLA_CHUNK, KEY_W)) for row in ebl], axis=0)
    return ebl, eb, enb, ee, q * Q_SCALE * eb, k * enb, k * ee


def _log_decay(lr_ref, wd_ref, bd_ref):
    z = _mm(lr_ref[...], wd_ref[...]) + bd_ref[...]
    return z, jax.nn.log_sigmoid(z) * (1.0 / GLA_TAU)


def _p_specs(tg, tile):
    return [pl.BlockSpec((tg, KEY_W), lambda i: (tile(i), 0)),
            pl.BlockSpec((tg, KEY_W), lambda i: (tile(i), 1)),
            pl.BlockSpec((tg, VAL_W), lambda i: (tile(i), 1)),
            pl.BlockSpec((tg, LANE), lambda i: (tile(i), LR_COL // LANE))]


def _gla_fwd_dir(rev, nc, q_ref, k_ref, v_ref, lr_ref, wd_ref, bd_ref, o_ref, st_ref, state):
    key_head, _, causal, state_head = _gla_masks(rev)
    order = range(nc - 1, -1, -1) if rev else range(nc)

    def intra():
        _, la = _log_decay(lr_ref, wd_ref, bd_ref)
        ebl, _, _, _, qd, kd, ke = _tile_terms(la, q_ref[...], k_ref[...], _tri(rev), rev)
        kd = kd.astype(BF16)
        v = {c: v_ref[_chunk_rows(c), :].astype(BF16) for c in order}
        qd_stack = {c: _stack_heads(qd[_chunk_rows(c)], key_head) for c in order}
        ke_stack = {c: _stack_heads(ke[_chunk_rows(c)], key_head) for c in order}
        a_all = {c: _mm_nt(qd_stack[c], kd[_chunk_rows(c)]) for c in order}
        a_all = {c: jnp.where(causal, a_all[c], 0.0).astype(BF16) for c in order}
        head_rows = lambda a, h: a[h * GLA_CHUNK:(h + 1) * GLA_CHUNK]
        head_vals = lambda a, h: a[:, h * GLA_DV:(h + 1) * GLA_DV]
        r = {c: [_mm(head_rows(a_all[c], h), head_vals(v[c], h)) for h in range(GLA_HEADS)] for c in order}
        upd = {c: _mm_tn(_rows_by_head(v[c]), ke_stack[c]) for c in order}
        return {c: (ebl[c], qd_stack[c], r[c], upd[c]) for c in order}

    def scan(terms):
        st = state[...]
        states = {}
        for c in order:
            states[c] = st
            st_ref[c] = st.astype(BF16)
            st = st * terms[c][0] + terms[c][3]
        state[...] = st
        return states

    def inter(terms, states):
        r_inter = {c: _mm_nt(terms[c][1], states[c]) for c in order}
        for c in order:
            o_ref[_chunk_rows(c), :] = jnp.concatenate(
                [terms[c][2][h] + r_inter[c][h * GLA_CHUNK:(h + 1) * GLA_CHUNK] for h in range(GLA_HEADS)], axis=1)

    return intra, scan, inter


def _gla_fwd(p, wd_pad_f, bd_f, wd_pad_b, bd_b, tg, comms=()):
    t = p.shape[0]
    nt = t // tg
    nc = tg // GLA_CHUNK
    up, down = (lambda i: i), (lambda i: nt - 1 - i)

    def body(qf, kf, vf, lrf, qb, kb, vb, lrb, wdf, bdf, wdb, bdb, of, stf, ob, stb, state_f, state_b):
        @pl.when(pl.program_id(0) == 0)
        def _():
            state_f[...] = jnp.zeros_like(state_f)
            state_b[...] = jnp.zeros_like(state_b)

        dirs = [_gla_fwd_dir(False, nc, qf, kf, vf, lrf, wdf, bdf, of, stf, state_f),
                _gla_fwd_dir(True, nc, qb, kb, vb, lrb, wdb, bdb, ob, stb, state_b)]
        terms = [intra() for intra, _, _ in dirs]
        states = [scan(t) for (_, scan, _), t in zip(dirs, terms)]
        for (_, _, inter), t, s in zip(dirs, terms, states):
            inter(t, s)

    wd_spec, bd_spec = _const_spec((LANE, KEY_W)), _const_spec((1, KEY_W))
    outs = lambda tile: (pl.BlockSpec((tg, VAL_W), lambda i: (tile(i), 0)),
                         pl.BlockSpec((nc, GLA_DV, KEY_W), lambda i: (tile(i), 0, 0)))
    out_shape = (jax.ShapeDtypeStruct((t, VAL_W), F32), jax.ShapeDtypeStruct((t // GLA_CHUNK, GLA_DV, KEY_W), BF16))
    return _fused_call(
        body, comms, name="gla_fwd", grid=(nt,), inputs=(p,) * 8 + (wd_pad_f, bd_f, wd_pad_b, bd_b),
        in_specs=_p_specs(tg, up) + _p_specs(tg, down) + [wd_spec, bd_spec, wd_spec, bd_spec],
        out_specs=outs(up) + outs(down), out_shape=out_shape * 2,
        scratch_shapes=[pltpu.VMEM((GLA_DV, KEY_W), F32)] * 2)


def _gla_bwd_dir(rev, nc, q_ref, k_ref, v_ref, lr_ref, wd_ref, bd_ref, st_ref, do_ref,
                 dq_ref, dk_ref, dv_ref, dlr_ref, dwd_ref, dbd_ref, dstate):
    key_head, val_head, causal, state_head = _gla_masks(rev)
    order = range(nc) if rev else range(nc - 1, -1, -1)
    tg = nc * GLA_CHUNK

    def intra():
        z, la = _log_decay(lr_ref, wd_ref, bd_ref)
        tile = _tile_terms(la, q_ref[...], k_ref[...], _tri(rev), rev)
        qd, kd = tile[4], tile[5].astype(BF16)
        v = {c: v_ref[_chunk_rows(c), :].astype(BF16) for c in order}
        d_o = {c: do_ref[_chunk_rows(c), :] for c in order}
        kd_c = {c: kd[_chunk_rows(c)] for c in order}
        qd_stack = {c: _stack_heads(qd[_chunk_rows(c)], key_head) for c in order}
        do_stack = {c: _stack_heads(d_o[c], val_head) for c in order}
        do_rows = {c: _rows_by_head(d_o[c]) for c in order}
        a_all = {c: _mm_nt(qd_stack[c], kd_c[c]) for c in order}
        head_vals = lambda a, h: a[:, h * GLA_DV:(h + 1) * GLA_DV]
        da_all = {c: jnp.concatenate([_mm_nt(head_vals(d_o[c], h), head_vals(v[c], h)) for h in range(GLA_HEADS)],
                                     axis=0) for c in order}
        a_all = {c: jnp.where(causal, a_all[c], 0.0).astype(BF16) for c in order}
        da_all = {c: jnp.where(causal, da_all[c], 0.0).astype(BF16) for c in order}
        dv = {c: _mm_tn(a_all[c], do_stack[c]) for c in order}
        dqd = {c: _mm(jnp.concatenate([do_rows[c], da_all[c]], axis=1),
                      jnp.concatenate([st_ref[c], kd_c[c]], axis=0)) for c in order}
        dkd = {c: _mm_tn(da_all[c], qd_stack[c]) for c in order}
        upd = {c: _mm_tn(do_rows[c], qd_stack[c]) for c in order}
        dqd = {c: _head_diagonal(dqd[c], key_head) for c in order}
        return z, tile, {c: dict(dv=dv[c], dqd=dqd[c], dkd=dkd[c], upd=upd[c]) for c in order}

    def scan(tile, per):
        dst = dstate[...]
        dsts = {}
        for c in order:
            dsts[c] = dst
            dst = dst * tile[0][c] + per[c]["upd"]
        dstate[...] = dst
        return dsts

    def inter(z, tile, per, dsts):
        ebl, eb, enb, ee, qd, kd, ke = tile
        ke_stack = {c: _stack_heads(ke[_chunk_rows(c)], key_head) for c in order}
        v_rows = {c: _rows_by_head(v_ref[_chunk_rows(c), :].astype(BF16)) for c in order}
        dst_b = {c: dsts[c].astype(BF16) for c in order}
        dv_state = {c: _mm_nt(ke_stack[c], dst_b[c]) for c in order}
        dke_c = {c: _mm(v_rows[c], dst_b[c]) for c in order}
        dke_c = {c: _head_diagonal(dke_c[c], key_head) for c in order}
        dbl_c = {}
        for c in order:
            rows = _chunk_rows(c)
            dv_ref[rows, :] = (per[c]["dv"] + _lanes_by_head(dv_state[c])).astype(BF16)
            dbl = (jnp.sum(dsts[c] * st_ref[c].astype(F32), axis=0, keepdims=True) * ebl[c]
                   + jnp.sum(dke_c[c] * ke[rows], axis=0, keepdims=True))
            dbl_c[c] = jnp.broadcast_to(dbl, (GLA_CHUNK, KEY_W))
        tile_of = lambda parts: jnp.concatenate([parts[c] for c in range(nc)], axis=0)
        dqd, dkd = tile_of({c: per[c]["dqd"] for c in order}), tile_of({c: per[c]["dkd"] for c in order})
        dke, dbl = tile_of(dke_c), tile_of(dbl_c)
        dq_ref[...] = (dqd * eb * Q_SCALE).astype(BF16)
        dk_ref[...] = (dkd * enb + dke * ee).astype(BF16)
        db = dqd * qd - dkd * kd - dke * ke
        dz = (_chunk_cumsum(_tri(not rev), db) + dbl) * (_sigmoid(-z) * (1.0 / GLA_TAU))
        dlr_ref[...] = _mm_nt(dz, wd_ref[...]).astype(BF16)
        dwd_ref[...] += _mm_tn(lr_ref[...], dz)
        dbd_ref[...] += jnp.sum(dz, axis=0, keepdims=True)

    return intra, scan, inter


def _gla_bwd(p, wd_pad_f, bd_f, wd_pad_b, bd_b, st_f, st_b, d_o, tg, comms=()):
    t = p.shape[0]
    nt = t // tg
    nc = tg // GLA_CHUNK
    up, down = (lambda i: i), (lambda i: nt - 1 - i)

    def body(qf, kf, vf, lrf, stf, dof, qb, kb, vb, lrb, stb, dob, wdf, bdf, wdb, bdb,
             dqf, dkf, dvf, dlrf, dwdf, dbdf, dqb, dkb, dvb, dlrb, dwdb, dbdb, dstate_f, dstate_b):
        @pl.when(pl.program_id(0) == 0)
        def _():
            for ref in (dstate_f, dstate_b, dwdf, dbdf, dwdb, dbdb):
                ref[...] = jnp.zeros_like(ref)

        dirs = [_gla_bwd_dir(False, nc, qf, kf, vf, lrf, wdf, bdf, stf, dof, dqf, dkf, dvf, dlrf, dwdf, dbdf,
                             dstate_f),
                _gla_bwd_dir(True, nc, qb, kb, vb, lrb, wdb, bdb, stb, dob, dqb, dkb, dvb, dlrb, dwdb, dbdb,
                             dstate_b)]
        first = [intra() for intra, _, _ in dirs]
        dsts = [scan(tile, per) for (_, scan, _), (_, tile, per) in zip(dirs, first)]
        for (_, _, inter), (z, tile, per), d in zip(dirs, first, dsts):
            inter(z, tile, per, d)

    wd_spec, bd_spec = _const_spec((LANE, KEY_W)), _const_spec((1, KEY_W))
    ins = lambda tile: _p_specs(tg, tile) + [pl.BlockSpec((nc, GLA_DV, KEY_W), lambda i: (tile(i), 0, 0)),
                                             pl.BlockSpec((tg, VAL_W), lambda i: (tile(i), 0))]
    outs = lambda tile: (pl.BlockSpec((tg, KEY_W), lambda i: (tile(i), 0)),
                         pl.BlockSpec((tg, KEY_W), lambda i: (tile(i), 0)),
                         pl.BlockSpec((tg, VAL_W), lambda i: (tile(i), 0)),
                         pl.BlockSpec((tg, LANE), lambda i: (tile(i), 0)),
                         _acc_spec((LANE, KEY_W)), _acc_spec((1, KEY_W)))
    out_shape = (jax.ShapeDtypeStruct((t, KEY_W), BF16), jax.ShapeDtypeStruct((t, KEY_W), BF16),
                 jax.ShapeDtypeStruct((t, VAL_W), BF16), jax.ShapeDtypeStruct((t, LANE), BF16),
                 jax.ShapeDtypeStruct((LANE, KEY_W), F32), jax.ShapeDtypeStruct((1, KEY_W), F32))
    scratch = [pltpu.VMEM((GLA_DV, KEY_W), F32)]
    return _fused_call(
        body, comms, name="gla_bwd", grid=(nt,),
        inputs=(p, p, p, p, st_f, d_o, p, p, p, p, st_b, d_o, wd_pad_f, bd_f, wd_pad_b, bd_b),
        in_specs=ins(down) + ins(up) + [wd_spec, bd_spec, wd_spec, bd_spec],
        out_specs=outs(down) + outs(up), out_shape=out_shape * 2, scratch_shapes=scratch * 2)


def _head_rms(o):
    parts, scales = [], []
    for h in range(GLA_HEADS):
        oh = o[:, h * GLA_DV:(h + 1) * GLA_DV]
        r = lax.rsqrt(jnp.mean(oh * oh, axis=-1, keepdims=True) + EPS)
        parts.append(oh * r)
        scales.append(jnp.broadcast_to(r, oh.shape))
    return jnp.concatenate(parts, axis=1), jnp.concatenate(scales, axis=1)


def _layernorm_stats(zv):
    mu = jnp.mean(zv, axis=-1, keepdims=True)
    xc = zv - mu
    rs = lax.rsqrt(jnp.mean(xc * xc, axis=-1, keepdims=True) + EPS)
    return xc * rs, rs


def _mix_fwd(x, o_f, o_b, p, gla_g, ln_g, ln_b, w_sp, b_sp, w_out, tm, comms=()):
    t = x.shape[0]
    nch = tm // GMLP_CHUNK

    def body(x_ref, of_ref, ob_ref, pg_ref, pu_ref, pv_ref, gg_ref, lg_ref, lb_ref, ws_ref, bs_ref, wo_ref,
             x1_ref, y_ref, s_scr):
        on, _ = _head_rms(of_ref[...] + ob_ref[...])
        pg = pg_ref[...].astype(F32)
        y_a = on * gg_ref[...] * (pg * _sigmoid(pg))
        zu = _gelu(pu_ref[...].astype(F32))
        vhat, _ = _layernorm_stats(_gelu(pv_ref[...].astype(F32)))
        vln = (vhat * lg_ref[...] + lb_ref[...]).astype(BF16)
        for g in range(GMLP_GROUPS):
            w_g = ws_ref[g].astype(BF16)
            b_g = bs_ref[g]
            cols = slice(g * LANE, (g + 1) * LANE)
            for n in range(nch):
                rows = slice(n * GMLP_CHUNK, (n + 1) * GMLP_CHUNK)
                s_scr[rows, cols] = jnp.dot(w_g, vln[rows, cols], preferred_element_type=F32) + b_g
        ycat = jnp.concatenate([y_a, zu * s_scr[...]], axis=1).astype(BF16)
        y_ref[...] = ycat
        x1_ref[...] = x_ref[...] + jnp.dot(ycat, wo_ref[...], preferred_element_type=F32)

    half = lambda j: pl.BlockSpec((tm, VAL_W), lambda i: (i, j))
    return _fused_call(
        body, comms, name="mix_fwd", grid=(t // tm,),
        inputs=(x, o_f, o_b, p, p, p, gla_g, ln_g, ln_b, w_sp, b_sp, w_out),
        in_specs=[pl.BlockSpec((tm, D_MODEL), lambda i: (i, 0)), half(0), half(0), half(2), half(3), half(4),
                  _const_spec((1, VAL_W)), _const_spec((1, GMLP_W)), _const_spec((1, GMLP_W)),
                  _const_spec((GMLP_GROUPS, GMLP_CHUNK, GMLP_CHUNK)), _const_spec((GMLP_GROUPS, GMLP_CHUNK, 1)),
                  _const_spec((D_MODEL, D_MODEL))],
        out_specs=(pl.BlockSpec((tm, D_MODEL), lambda i: (i, 0)), pl.BlockSpec((tm, D_MODEL), lambda i: (i, 0))),
        out_shape=(jax.ShapeDtypeStruct((t, D_MODEL), F32), jax.ShapeDtypeStruct((t, D_MODEL), BF16)),
        scratch_shapes=[pltpu.VMEM((tm, GMLP_W), F32)])


def _mix_bwd(dx1, ycat, o_f, o_b, p, gla_g, ln_g, ln_b, w_sp, b_sp, w_out, tm, comms=()):
    t = dx1.shape[0]
    nch = tm // GMLP_CHUNK

    def body(dx1_ref, y_ref, of_ref, ob_ref, pg_ref, pu_ref, pv_ref, gg_ref, lg_ref, lb_ref, ws_ref, bs_ref, wo_ref,
             do_ref, dpg_ref, dpu_ref, dpv_ref, dwo_ref, dgg_ref, dlg_ref, dlb_ref, dws_ref, dbs_ref,
             s_scr, dvln_scr):
        @pl.when(pl.program_id(0) == 0)
        def _():
            for ref in (dwo_ref, dgg_ref, dlg_ref, dlb_ref, dws_ref, dbs_ref):
                ref[...] = jnp.zeros_like(ref)

        dx1 = dx1_ref[...].astype(BF16)
        dycat = _mm_nt(dx1, wo_ref[...])
        dwo_ref[...] += _mm_tn(y_ref[...], dx1)
        dy_a = dycat[:, :VAL_W]
        dy_b = dycat[:, VAL_W:]
        on, r = _head_rms(of_ref[...] + ob_ref[...])
        pg = pg_ref[...].astype(F32)
        sil, dsil = _silu_and_grad(pg)
        gg = gg_ref[...]
        dgg_ref[...] += jnp.sum(dy_a * sil * on, axis=0, keepdims=True)
        don = dy_a * sil * gg
        prod = don * on
        means = jnp.concatenate(
            [jnp.broadcast_to(jnp.mean(prod[:, h * GLA_DV:(h + 1) * GLA_DV], axis=-1, keepdims=True),
                              (tm, GLA_DV)) for h in range(GLA_HEADS)], axis=1)
        do_ref[...] = (r * (don - on * means)).astype(BF16)
        dpg_ref[...] = (dy_a * on * gg * dsil).astype(BF16)
        pu = pu_ref[...].astype(F32)
        pv = pv_ref[...].astype(F32)
        zu, dzu_dpu = _gelu_and_grad(pu)
        zv, dzv_dpv = _gelu_and_grad(pv)
        vhat, rs = _layernorm_stats(zv)
        lg = lg_ref[...]
        vln = (vhat * lg + lb_ref[...]).astype(BF16)
        ds32 = dy_b * zu
        ds = ds32.astype(BF16)
        blocks = [(g, n) for g in range(GMLP_GROUPS) for n in range(nch)]
        at = lambda g, n: (slice(n * GMLP_CHUNK, (n + 1) * GMLP_CHUNK), slice(g * LANE, (g + 1) * LANE))
        w_sp = [ws_ref[g].astype(BF16) for g in range(GMLP_GROUPS)]
        v_blk = {b: vln[at(*b)] for b in blocks}
        ds_blk = {b: ds[at(*b)] for b in blocks}
        s_blk = {b: jnp.dot(w_sp[b[0]], v_blk[b], preferred_element_type=F32) for b in blocks}
        dw_blk = {b: _mm_nt(ds_blk[b], v_blk[b]) for b in blocks}
        dvln_blk = {b: _mm_tn(w_sp[b[0]], ds_blk[b]) for b in blocks}
        for b in blocks:
            s_scr[at(*b)] = s_blk[b] + bs_ref[b[0]]
            dvln_scr[at(*b)] = dvln_blk[b]
        for g in range(GMLP_GROUPS):
            dws_ref[g] += sum(dw_blk[(g, n)] for n in range(nch))
            dbs_ref[g] += sum(jnp.sum(ds32[at(g, n)], axis=-1, keepdims=True) for n in range(nch))
        dpu_ref[...] = (dy_b * s_scr[...] * dzu_dpu).astype(BF16)
        dvln = dvln_scr[...]
        dlg_ref[...] += jnp.sum(dvln * vhat, axis=0, keepdims=True)
        dlb_ref[...] += jnp.sum(dvln, axis=0, keepdims=True)
        dvhat = dvln * lg
        dzv = rs * (dvhat - jnp.mean(dvhat, axis=-1, keepdims=True)
                    - vhat * jnp.mean(dvhat * vhat, axis=-1, keepdims=True))
        dpv_ref[...] = (dzv * dzv_dpv).astype(BF16)

    half = lambda j: pl.BlockSpec((tm, VAL_W), lambda i: (i, j))
    full = pl.BlockSpec((tm, D_MODEL), lambda i: (i, 0))
    sp_shape = (GMLP_GROUPS, GMLP_CHUNK, GMLP_CHUNK)
    bs_shape = (GMLP_GROUPS, GMLP_CHUNK, 1)
    return _fused_call(
        body, comms, name="mix_bwd", grid=(t // tm,),
        inputs=(dx1, ycat, o_f, o_b, p, p, p, gla_g, ln_g, ln_b, w_sp, b_sp, w_out),
        in_specs=[full, full, half(0), half(0), half(2), half(3), half(4),
                  _const_spec((1, VAL_W)), _const_spec((1, GMLP_W)), _const_spec((1, GMLP_W)),
                  _const_spec(sp_shape), _const_spec(bs_shape), _const_spec((D_MODEL, D_MODEL))],
        out_specs=(half(0), half(0), half(0), half(0), _acc_spec((D_MODEL, D_MODEL)), _acc_spec((1, VAL_W)),
                   _acc_spec((1, GMLP_W)), _acc_spec((1, GMLP_W)), _acc_spec(sp_shape), _acc_spec(bs_shape)),
        out_shape=(jax.ShapeDtypeStruct((t, VAL_W), BF16),) * 4 + (
            jax.ShapeDtypeStruct((D_MODEL, D_MODEL), F32), jax.ShapeDtypeStruct((1, VAL_W), F32),
            jax.ShapeDtypeStruct((1, GMLP_W), F32), jax.ShapeDtypeStruct((1, GMLP_W), F32),
            jax.ShapeDtypeStruct(sp_shape, F32), jax.ShapeDtypeStruct(bs_shape, F32)),
        scratch_shapes=[pltpu.VMEM((tm, GMLP_W), F32), pltpu.VMEM((tm, GMLP_W), F32)])


def _rms_bwd(dy_scaled, xn, r):
    return r * (dy_scaled - xn * jnp.mean(dy_scaled * xn, axis=-1, keepdims=True))


def _ffn(x1, target, g2, gf, w_gate, w_up, w_down, tm):
    t = x1.shape[0]

    def body(x1_ref, tg_ref, g2_ref, gf_ref, wg_ref, wu_ref, wd_ref,
             dx1_ref, h2_ref, dgate_ref, dup_ref, act_ref, dx2_ref, loss_ref, dgf_ref, dg2_ref):
        @pl.when(pl.program_id(0) == 0)
        def _():
            for ref in (loss_ref, dgf_ref, dg2_ref):
                ref[...] = jnp.zeros_like(ref)

        x1v = x1_ref[...]
        g2v = g2_ref[...]
        gfv = gf_ref[...]
        r2 = lax.rsqrt(jnp.mean(x1v * x1v, axis=-1, keepdims=True) + EPS)
        xn1 = x1v * r2
        h2 = (xn1 * g2v).astype(BF16)
        h2_ref[...] = h2
        gate = _mm_nt(h2, wg_ref[...])
        up = _mm_nt(h2, wu_ref[...])
        sil, dsil = _silu_and_grad(gate)
        act = (sil * up).astype(BF16)
        act_ref[...] = act
        x2 = x1v + jnp.dot(act, wd_ref[...], preferred_element_type=F32)
        rf = lax.rsqrt(jnp.mean(x2 * x2, axis=-1, keepdims=True) + EPS)
        xn2 = x2 * rf
        err = xn2 * gfv - tg_ref[...]
        loss_ref[...] += 0.5 * jnp.sum(jnp.mean(err * err, axis=-1, keepdims=True))
        dy = err * (1.0 / D_MODEL)
        dgf_ref[...] += jnp.sum(dy * xn2, axis=0, keepdims=True)
        dx2 = _rms_bwd(dy * gfv, xn2, rf)
        dx2b = dx2.astype(BF16)
        dx2_ref[...] = dx2b
        dact = _mm_nt(dx2b, wd_ref[...])
        dgate = (dact * up * dsil).astype(BF16)
        dup = (dact * sil).astype(BF16)
        dgate_ref[...] = dgate
        dup_ref[...] = dup
        dh2 = _mm(dgate, wg_ref[...]) + _mm(dup, wu_ref[...])
        dg2_ref[...] += jnp.sum(dh2 * xn1, axis=0, keepdims=True)
        dx1_ref[...] = dx2 + _rms_bwd(dh2 * g2v, xn1, r2)

    row = lambda w: pl.BlockSpec((tm, w), lambda i: (i, 0))
    return pl.pallas_call(
        body, name="ffn_fwd_bwd", grid=(t // tm,),
        in_specs=[row(D_MODEL), row(D_MODEL), _const_spec((1, D_MODEL)), _const_spec((1, D_MODEL)),
                  _const_spec((D_FF, D_MODEL)), _const_spec((D_FF, D_MODEL)), _const_spec((D_FF, D_MODEL))],
        out_specs=(row(D_MODEL), row(D_MODEL), row(D_FF), row(D_FF), row(D_FF), row(D_MODEL),
                   _acc_spec((8, LANE)), _acc_spec((1, D_MODEL)), _acc_spec((1, D_MODEL))),
        out_shape=(jax.ShapeDtypeStruct((t, D_MODEL), F32), jax.ShapeDtypeStruct((t, D_MODEL), BF16),
                   jax.ShapeDtypeStruct((t, D_FF), BF16), jax.ShapeDtypeStruct((t, D_FF), BF16),
                   jax.ShapeDtypeStruct((t, D_FF), BF16), jax.ShapeDtypeStruct((t, D_MODEL), BF16),
                   jax.ShapeDtypeStruct((8, LANE), F32), jax.ShapeDtypeStruct((1, D_MODEL), F32),
                   jax.ShapeDtypeStruct((1, D_MODEL), F32)),
        compiler_params=_params(),
    )(x1, target, g2, gf, w_gate, w_up, w_down)


def _matmul_tn(a, b, tm, tk, name, comms=()):
    t, m = a.shape
    n = b.shape[1]

    def body(a_ref, b_ref, o_ref):
        @pl.when(pl.program_id(1) == 0)
        def _():
            o_ref[...] = jnp.zeros_like(o_ref)

        o_ref[...] += _mm_tn(a_ref[...], b_ref[...])

    (out,), comm_results = _fused_call(
        body, comms, name=name, grid=(m // tm, t // tk), inputs=(a, b),
        in_specs=[pl.BlockSpec((tk, tm), lambda j, k: (k, j)), pl.BlockSpec((tk, n), lambda j, k: (k, 0))],
        out_specs=(pl.BlockSpec((tm, n), lambda j, k: (j, 0)),),
        out_shape=(jax.ShapeDtypeStruct((m, n), F32),))
    return out, comm_results


def _in_proj_bwd(x, g1, dx1, dq_f, dq_b, dk_f, dk_b, dv_f, dv_b, dpg, dpu, dpv, dlr_f, dlr_b, w_main, tm, comms=()):
    t = x.shape[0]

    def body(x_ref, g_ref, dx1_ref, dqf, dqb, dkf, dkb, dvf, dvb, dg, du, dv, dlf, dlb, w_ref,
             dx_ref, dp_ref, dg1_ref):
        @pl.when(pl.program_id(0) == 0)
        def _():
            dg1_ref[...] = jnp.zeros_like(dg1_ref)

        both = lambda a, b: (a[...].astype(F32) + b[...].astype(F32)).astype(BF16)
        dp = jnp.concatenate([both(dqf, dqb), both(dkf, dkb), both(dvf, dvb), dg[...], du[...], dv[...],
                              both(dlf, dlb)], axis=1)
        dp_ref[...] = dp
        dh = sum(_mm(dp[:, c0:c0 + r1 - r0], w_ref[r0:r1, :]) for r0, r1, c0 in PROJ_ROWS)
        xv = x_ref[...]
        r = lax.rsqrt(jnp.mean(xv * xv, axis=-1, keepdims=True) + EPS)
        xn = xv * r
        dg1_ref[...] += jnp.sum(dh * xn, axis=0, keepdims=True)
        dx_ref[...] = dx1_ref[...] + _rms_bwd(dh * g_ref[...], xn, r)

    row = lambda w: pl.BlockSpec((tm, w), lambda i: (i, 0))
    return _fused_call(
        body, comms, name="in_proj_bwd", grid=(t // tm,),
        inputs=(x, g1, dx1, dq_f, dq_b, dk_f, dk_b, dv_f, dv_b, dpg, dpu, dpv, dlr_f, dlr_b, w_main),
        in_specs=[row(D_MODEL), _const_spec((1, D_MODEL)), row(D_MODEL), row(KEY_W), row(KEY_W), row(KEY_W),
                  row(KEY_W), row(VAL_W), row(VAL_W), row(VAL_W), row(VAL_W), row(VAL_W), row(LANE), row(LANE),
                  _const_spec((PROJ_W, D_MODEL))],
        out_specs=(row(D_MODEL), row(PROJ_PAD), _acc_spec((1, D_MODEL))),
        out_shape=(jax.ShapeDtypeStruct((t, D_MODEL), F32), jax.ShapeDtypeStruct((t, PROJ_PAD), BF16),
                   jax.ShapeDtypeStruct((1, D_MODEL), F32)))


def _adamw(w, g, m, v):
    m_new = ADAM_B1 * m + (1.0 - ADAM_B1) * g
    v_new = ADAM_B2 * v + (1.0 - ADAM_B2) * (g * g)
    m_hat = m_new / (1.0 - ADAM_B1 ** ADAM_STEP)
    v_hat = v_new / (1.0 - ADAM_B2 ** ADAM_STEP)
    delta = -ADAM_LR * (m_hat / (jnp.sqrt(v_hat) + ADAM_EPS) + ADAM_WD * w)
    return delta, m_new, v_new


def _adamw_shard(own, recv, w, m, v, tr, name):
    r, c = w.shape

    def body(own_ref, recv_ref, w_ref, m_ref, v_ref, g_ref, d_ref, nm_ref, nv_ref):
        g = own_ref[...]
        for k in range(3):
            g = g + recv_ref[k].astype(F32)
        g_ref[...] = g
        d_ref[...], nm_ref[...], nv_ref[...] = _adamw(w_ref[...], g, m_ref[...], v_ref[...])

    row = pl.BlockSpec((tr, c), lambda i: (i, 0))
    return pl.pallas_call(
        body, name=name, grid=(r // tr,),
        in_specs=[row, pl.BlockSpec((3, tr, c), lambda i: (0, i, 0)), row, row, row],
        out_specs=(row,) * 4, out_shape=(jax.ShapeDtypeStruct((r, c), F32),) * 4,
        compiler_params=_params(),
    )(own, recv, w, m, v)


def _adamw_small(entries):
    stacks = []
    for (g, _, _), _, _, _ in entries:
        if not any(g is s for s in stacks):
            stacks.append(g)
    where = [next(i for i, s in enumerate(stacks) if s is g) for (g, _, _), _, _, _ in entries]
    ns, ne = len(stacks), len(entries)

    def body(*refs):
        s_refs, wmv, outs = refs[:ns], refs[ns:ns + 3 * ne], refs[ns + 3 * ne:]
        for e, ((_, r0, nr), _, _, _) in enumerate(entries):
            grad = s_refs[where[e]][r0:r0 + nr, :]
            w_ref, m_ref, v_ref = wmv[3 * e:3 * e + 3]
            g_ref, d_ref, nm_ref, nv_ref = outs[4 * e:4 * e + 4]
            g_ref[...] = grad
            d_ref[...], nm_ref[...], nv_ref[...] = _adamw(w_ref[...], grad, m_ref[...], v_ref[...])

    results = pl.pallas_call(
        body, name="adamw_small",
        out_shape=tuple(jax.ShapeDtypeStruct(w.shape, F32) for _, w, _, _ in entries for _ in range(4)),
        compiler_params=pltpu.CompilerParams(vmem_limit_bytes=VMEM_LIMIT),
    )(*stacks, *[a for _, w, m, v in entries for a in (w, m, v)])
    return [results[4 * e:4 * e + 4] for e in range(ne)]


def _mesh_pos():
    return lax.axis_index("x"), lax.axis_index("y"), lax.axis_index("c")


def _other_chips(x, y):
    return [(x, 1 - y), (1 - x, y), (1 - x, 1 - y)]


_VMEM_WHOLE = pl.BlockSpec(memory_space=pltpu.VMEM)
_HBM_WHOLE = pl.BlockSpec(memory_space=pl.ANY)


def _gather_comm(shards, cast, mid=((1, 2), (3, 4))):
    na = len(shards)
    staged = [a for a in range(na) if cast[a]]

    def phases(in_refs, out_refs, scr):
        stage = dict(zip(staged, scr[:len(staged)]))
        send_sems, recv_sems, local_sems = scr[len(staged):]
        x, y, c = _mesh_pos()
        me, sibling = (x, y, c), (x, y, 1 - c)
        chip_a, chip_b, diagonal = (x ^ c, y ^ (1 - c)), (x ^ (1 - c), y ^ c), (1 - x, 1 - y)
        srcs = [stage[a] if cast[a] else in_refs[a] for a in range(na)]

        def rows(a, pos):
            px, py, pc = pos
            return out_refs[a].at[4 * px + 2 * py + pc]

        def copy(a, k, block, to, src=None):
            return pltpu.make_async_remote_copy(
                src_ref=rows(a, block) if src is None else src, dst_ref=rows(a, block),
                send_sem=send_sems.at[a, k], recv_sem=recv_sems.at[a, k], device_id=to, device_id_type=MESH_ID)

        mine = [pltpu.make_async_copy(srcs[a], rows(a, me), local_sems.at[a]) for a in range(na)]
        own = [copy(a, k, me, to, src=srcs[a]) for a in range(na)
               for k, to in ((0, sibling), (1, (*chip_a, c)), (2, (*chip_b, c)))]
        onward = [copy(a, 3, (*chip_a, c), (*chip_b, c)) for a in range(na)]
        to_sibling = {k: [copy(a, k, (*chip, c), sibling) for a in range(na)]
                      for k, chip in ((4, chip_a), (5, chip_b), (6, diagonal))}

        def start():
            for a in staged:
                stage[a][...] = in_refs[a][...].astype(BF16)
            for cp in mine + own:
                cp.start()

        def forward_neighbours():
            for a in range(na):
                copy(a, 1, (*chip_a, c), me).wait_recv()
                onward[a].start()
                to_sibling[4][a].start()
            for a in range(na):
                copy(a, 2, (*chip_b, c), me).wait_recv()
                to_sibling[5][a].start()

        def forward_diagonal():
            for a in range(na):
                copy(a, 3, (*diagonal, c), me).wait_recv()
                to_sibling[6][a].start()

        def finish():
            for a in range(na):
                for k, chip in ((0, (x, y)), (4, chip_b), (5, chip_a), (6, diagonal)):
                    copy(a, k, (*chip, 1 - c), me).wait_recv()
            for cp in own + onward + to_sibling[4] + to_sibling[5] + to_sibling[6]:
                cp.wait_send()
            for cp in mine:
                cp.wait()

        return start, forward_neighbours, forward_diagonal, finish

    def before(step, nsteps, in_refs, out_refs, scr):
        start, forward_neighbours, forward_diagonal, _ = phases(in_refs, out_refs, scr)
        pl.when(step == 0)(start)
        pl.when(step == nsteps * mid[0][0] // mid[0][1])(forward_neighbours)
        pl.when(step == nsteps * mid[1][0] // mid[1][1])(forward_diagonal)

    def after(step, nsteps, in_refs, out_refs, scr):
        pl.when(step == nsteps - 1)(phases(in_refs, out_refs, scr)[3])

    return _Comm(
        inputs=list(shards), in_specs=[_VMEM_WHOLE] * na,
        out_shape=[jax.ShapeDtypeStruct((N_DEV,) + s.shape, BF16 if cast[a] else s.dtype)
                   for a, s in enumerate(shards)],
        out_specs=[_HBM_WHOLE] * na,
        scratch_shapes=[pltpu.VMEM(shards[a].shape, BF16) for a in staged] + [
            pltpu.SemaphoreType.DMA((na, 7)), pltpu.SemaphoreType.DMA((na, 7)), pltpu.SemaphoreType.DMA((na,))],
        before=before, after=after)


def _exchange_comm(arrays, out_shape, make_copies):
    na = len(arrays)

    def copies(in_refs, out_refs, scr):
        return make_copies(in_refs, out_refs, *scr)

    def before(step, nsteps, in_refs, out_refs, scr):
        @pl.when(step == 0)
        def _():
            for cp in copies(in_refs, out_refs, scr):
                cp.start()

    def after(step, nsteps, in_refs, out_refs, scr):
        @pl.when(step == nsteps - 1)
        def _():
            for cp in copies(in_refs, out_refs, scr):
                cp.wait()

    return _Comm(inputs=list(arrays), in_specs=[_HBM_WHOLE] * na, out_shape=list(out_shape),
                 out_specs=[_HBM_WHOLE] * na,
                 scratch_shapes=[pltpu.SemaphoreType.DMA((na, 3)), pltpu.SemaphoreType.DMA((na, 3))],
                 before=before, after=after)


def _sibling_exchange_comm(grads):
    def make_copies(in_refs, out_refs, send_sems, recv_sems):
        x, y, c = _mesh_pos()
        return [pltpu.make_async_remote_copy(
            src_ref=in_refs[a].at[:, pl.ds(1 - c, 1)], dst_ref=out_refs[a], send_sem=send_sems.at[a, 0],
            recv_sem=recv_sems.at[a, 0], device_id=(x, y, 1 - c), device_id_type=MESH_ID)
            for a in range(len(grads))]

    return _exchange_comm(grads, [jax.ShapeDtypeStruct((4, 1) + g.shape[2:], F32) for g in grads], make_copies)


def _chips_exchange_comm(partials):
    def make_copies(in_refs, out_refs, send_sems, recv_sems):
        x, y, c = _mesh_pos()
        return [pltpu.make_async_remote_copy(
            src_ref=in_refs[a].at[j], dst_ref=out_refs[a].at[j], send_sem=send_sems.at[a, j],
            recv_sem=recv_sems.at[a, j], device_id=(*chip, c), device_id_type=MESH_ID)
            for a in range(len(partials)) for j, chip in enumerate(_other_chips(x, y))]

    return _exchange_comm(partials, [jax.ShapeDtypeStruct(g.shape, BF16) for g in partials], make_copies)


def _comm_only(comms, name):
    return _fused_call(lambda: None, comms, name=name, grid=(1,), inputs=(), in_specs=[], out_specs=(),
                       out_shape=())[1]


def _chip_sum(my_pos, mine, from_sibling, tr, name):
    _, _, r, c = mine.shape

    def body(pos_ref, a_ref, b_ref, own_ref, out_ref):
        s = a_ref[0, 0] + b_ref[0, 0]

        @pl.when(pl.program_id(1) == 0)
        def _():
            own_ref[...] = s

        @pl.when(pl.program_id(1) > 0)
        def _():
            out_ref[0] = s.astype(BF16)

    grid_spec = pltpu.PrefetchScalarGridSpec(
        num_scalar_prefetch=1, grid=(r // tr, 4),
        in_specs=[pl.BlockSpec((1, 1, tr, c), lambda i, k, pos: (pos[0] ^ k, pos[1], i, 0)),
                  pl.BlockSpec((1, 1, tr, c), lambda i, k, pos: (pos[0] ^ k, 0, i, 0))],
        out_specs=(pl.BlockSpec((tr, c), lambda i, k, pos: (i, 0)),
                   pl.BlockSpec((1, tr, c), lambda i, k, pos: (jnp.maximum(k - 1, 0), i, 0))))
    return pl.pallas_call(
        body, name=name, grid_spec=grid_spec,
        out_shape=(jax.ShapeDtypeStruct((r, c), F32), jax.ShapeDtypeStruct((3, r, c), BF16)),
        compiler_params=_params(2),
    )(my_pos, mine, from_sibling)


def _all_reduce_small_comm(parts):
    na = len(parts)

    def copies(in_refs, scr):
        gathered, (send_sems, recv_sems) = scr[:na], scr[na:]
        x, y, c = _mesh_pos()
        my_id = 4 * x + 2 * y + c
        return my_id, [pltpu.make_async_remote_copy(
            src_ref=in_refs[a], dst_ref=gathered[a].at[my_id], send_sem=send_sems.at[a, k - 1],
            recv_sem=recv_sems.at[a, k - 1], device_id=(x ^ (k >> 2), y ^ ((k >> 1) & 1), c ^ (k & 1)),
            device_id_type=MESH_ID) for a in range(na) for k in range(1, N_DEV)]

    def before(step, nsteps, in_refs, out_refs, scr):
        @pl.when(step == 0)
        def _():
            for cp in copies(in_refs, scr)[1]:
                cp.start()

    def after(step, nsteps, in_refs, out_refs, scr):
        @pl.when(step == nsteps - 1)
        def _():
            my_id, cps = copies(in_refs, scr)
            for a in range(na):
                scr[a][my_id] = in_refs[a][...]
            for cp in cps:
                cp.wait()
            for a in range(na):
                acc = scr[a][0]
                for d in range(1, N_DEV):
                    acc = acc + scr[a][d]
                out_refs[a][...] = acc

    return _Comm(inputs=list(parts), in_specs=[_VMEM_WHOLE] * na,
                 out_shape=[jax.ShapeDtypeStruct(p.shape, F32) for p in parts], out_specs=[_VMEM_WHOLE] * na,
                 scratch_shapes=[pltpu.VMEM((N_DEV,) + p.shape, F32) for p in parts] + [
                     pltpu.SemaphoreType.DMA((na, N_DEV - 1)), pltpu.SemaphoreType.DMA((na, N_DEV - 1))],
                 before=before, after=after)


def _unshard_cols(g):
    return jnp.transpose(g, (1, 0, 2)).reshape(g.shape[1], N_DEV * g.shape[2])


def _row_blocks(w):
    return w.reshape(4, 2, w.shape[0] // N_DEV, w.shape[1])


def _stack_rows(parts):
    a = jnp.concatenate(parts, axis=0)
    return jnp.pad(a, ((0, (-a.shape[0]) % 8), (0, 0)))


def _w_in_grad_blocks(dw):
    r = PROJ_W // N_DEV
    pieces = ((0, LR_REF, 0), (LR_REF, LR_REF + 2 * LOWRANK, LR_COL), (LR_REF + 2 * LOWRANK, PROJ_W, LR_REF))

    def ref_rows(a, b):
        parts = [dw[src + max(a, lo) - lo:src + min(b, hi) - lo] for lo, hi, src in pieces if max(a, lo) < min(b, hi)]
        return parts[0] if len(parts) == 1 else jnp.concatenate(parts, axis=0)

    return jnp.stack([ref_rows(d * r, (d + 1) * r) for d in range(N_DEV)]).reshape(4, 2, r, D_MODEL)


def _padded_decay_weights(wd_f, wd_b):
    zeros = lambda n: jnp.zeros((n, KEY_W), F32)
    return (jnp.concatenate([wd_f, zeros(LANE - LOWRANK)], axis=0),
            jnp.concatenate([zeros(LOWRANK), wd_b, zeros(LANE - 2 * LOWRANK)], axis=0))


def kernel(x, norm1_g, w_in,w_decay_f, b_decay_f, w_decay_b, b_decay_b, gla_norm_g, gmlp_ln_g, gmlp_ln_b, w_spatial, b_spatial, w_out, norm2_g, w_gate, w_up, w_down, final_norm_g, loss_target, m_norm1_g, m_w_in, m_w_decay_f, m_b_decay_f, m_w_decay_b, m_b_decay_b, m_gla_norm_g, m_gmlp_ln_g, m_gmlp_ln_b, m_w_spatial, m_b_spatial, m_w_out, m_norm2_g, m_w_gate, m_w_up, m_w_down, m_final_norm_g, v_norm1_g, v_w_in, v_w_decay_f, v_b_decay_f, v_w_decay_b, v_b_decay_b, v_gla_norm_g, v_gmlp_ln_g, v_gmlp_ln_b, v_w_spatial, v_b_spatial, v_w_out, v_norm2_g, v_w_gate, v_w_up, v_w_down, v_final_norm_g):
    t = x.shape[1]
    xt = x[0]
    target = loss_target[0]
    pos_x, pos_y, pos_c = _mesh_pos()
    my_pos = jnp.stack([2 * pos_x + pos_y, pos_c]).astype(jnp.int32)
    my_id = 4 * pos_x + 2 * pos_y + pos_c

    tile = lambda n: min(n, t)
    ln_g, ln_b, w_sp = gmlp_ln_g, gmlp_ln_b, w_spatial[0]
    b_sp_col = b_spatial[0][:, :, None]
    shard = {"w_in": w_in[0].T, "w_out": w_out[0], "w_gate": w_gate[0].T, "w_up": w_up[0].T, "w_down": w_down[0]}
    shard_m = {"w_in": m_w_in[0].T, "w_out": m_w_out[0], "w_gate": m_w_gate[0].T, "w_up": m_w_up[0].T,
               "w_down": m_w_down[0]}
    shard_v = {"w_in": v_w_in[0].T, "w_out": v_w_out[0], "w_gate": v_w_gate[0].T, "w_up": v_w_up[0].T,
               "w_down": v_w_down[0]}
    transposed = ("w_in", "w_gate", "w_up")
    chip_sum = lambda n, g, s: _chip_sum(my_pos, g, s[0], g.shape[2], "chip_sum_" + n)

    decay_shard = jnp.stack([w_decay_f[0], w_decay_b[0]])
    (hb,), ((g_in, g_decay),) = _norm1(xt, norm1_g, tile(512),
                                       [_gather_comm([shard["w_in"], decay_shard], [True, False])])
    w_in_t = g_in.reshape(PROJ_W, D_MODEL)
    wd_pad_f, wd_pad_b = _padded_decay_weights(_unshard_cols(g_decay[:, 0]), _unshard_cols(g_decay[:, 1]))
    (p,), ((g_gate, g_out),) = _in_proj(
        hb, w_in_t, tile(512), [_gather_comm([shard["w_gate"], shard["w_out"]], [True, True])])
    (o_f, st_f, o_b, st_b), ((g_up, g_down),) = _gla_fwd(
        p, wd_pad_f, b_decay_f, wd_pad_b, b_decay_b, tile(512),
        [_gather_comm([shard["w_up"], shard["w_down"]], [True, True])])
    w_out_full = g_out.reshape(D_MODEL, D_MODEL)
    (x1, ycat), _ = _mix_fwd(xt, o_f, o_b, p, gla_norm_g, ln_g, ln_b, w_sp, b_sp_col, w_out_full, tile(1024))

    dx1, h2b, dgate, dup, act, dx2, loss_acc, d_gf, d_g2 = _ffn(
        x1, target, norm2_g, final_norm_g[None, :], g_gate.reshape(D_FF, D_MODEL), g_up.reshape(D_FF, D_MODEL),
        g_down.reshape(D_FF, D_MODEL), tile(256))
    dw_gate, _ = _matmul_tn(dgate, h2b, D_FF // 2, tile(2048), "grad_w_gate")
    dw_up, _ = _matmul_tn(dup, h2b, D_FF // 2, tile(2048), "grad_w_up")
    dw_down, _ = _matmul_tn(act, dx2, D_FF // 2, tile(2048), "grad_w_down")

    ffn_grads = [_row_blocks(dw_gate), _row_blocks(dw_up), _row_blocks(dw_down)]
    (d_o, dpg, dpu, dpv, dw_out, d_gg, d_lg, d_lb, dw_sp, db_sp), (ffn_sib,) = _mix_bwd(
        dx1, ycat, o_f, o_b, p, gla_norm_g, ln_g, ln_b, w_sp, b_sp_col, w_out_full, tile(512),
        [_sibling_exchange_comm(ffn_grads)])
    ffn_names = ["w_gate", "w_up", "w_down"]
    ffn_sums = [chip_sum(n, g, [s]) for n, g, s in zip(ffn_names, ffn_grads, ffn_sib)]
    out_grad = _row_blocks(dw_out)
    (dq_f, dk_f, dv_f, dlr_f, dwd_f, dbd_f, dq_b, dk_b, dv_b, dlr_b, dwd_b, dbd_b), (ffn_recv, out_sib) = _gla_bwd(
        p, wd_pad_f, b_decay_f, wd_pad_b, b_decay_b, st_f, st_b, d_o, tile(512),
        [_chips_exchange_comm([s[1] for s in ffn_sums]), _sibling_exchange_comm([out_grad])])
    out_sum = chip_sum("w_out", out_grad, out_sib)
    (grad_x, dp, d_g1), _ = _in_proj_bwd(
        xt, norm1_g, dx1, dq_f, dq_b, dk_f, dk_b, dv_f, dv_b, dpg, dpu, dpv, dlr_f, dlr_b, w_in_t, tile(512))

    stacks = [_stack_rows([d_g1, d_g2, d_gf]), _stack_rows([d_gg, d_lg, d_lb]),
              _stack_rows([dbd_f, dbd_b, jnp.zeros((6, KEY_W), F32), dwd_f[:LOWRANK], dwd_b[LOWRANK:2 * LOWRANK]]),
              _stack_rows([dw_sp.reshape(GMLP_W, GMLP_CHUNK), db_sp[:, :, 0], loss_acc[:1]])]
    dw_main, (small_sums, out_recv) = _matmul_tn(
        dp, hb, PROJ_PAD // 3, tile(2048), "grad_w_in",
        [_all_reduce_small_comm(stacks), _chips_exchange_comm([out_sum[1]])])
    in_grad = _w_in_grad_blocks(dw_main)
    (in_sib,) = _comm_only([_sibling_exchange_comm([in_grad])], "grad_w_in_exchange_sibling")
    in_sum = chip_sum("w_in", in_grad, in_sib)
    (in_recv,) = _comm_only([_chips_exchange_comm([in_sum[1]])], "grad_w_in_exchange_chips")

    names = ["w_in", "w_out", "w_gate", "w_up", "w_down"]
    sums = [in_sum, out_sum] + ffn_sums
    received = [in_recv[0], out_recv[0]] + list(ffn_recv)
    big_out = {}
    for n, s, rc in zip(names, sums, received):
        res = _adamw_shard(s[0], rc, shard[n], shard_m[n], shard_v[n], shard[n].shape[0], "adamw_" + n)
        big_out[n] = [r.T if n in transposed else r for r in res]

    s1024, s512, s256, s128 = small_sums
    loss = s128[GMLP_W + GMLP_GROUPS, 0]
    col0 = my_id * (KEY_W // N_DEV)
    decay_cols = lambda row0: lax.dynamic_slice(s256, (row0, col0), (LOWRANK, KEY_W // N_DEV))
    flat = lambda a: a.reshape(-1, a.shape[-1])
    small = {
        "norm1_g": ((s1024, 0, 1), norm1_g, m_norm1_g, v_norm1_g),
        "w_decay_f": ((decay_cols(8), 0, LOWRANK), w_decay_f, m_w_decay_f, v_w_decay_f),
        "b_decay_f": ((s256, 0, 1), b_decay_f, m_b_decay_f, v_b_decay_f),
        "w_decay_b": ((decay_cols(8 + LOWRANK), 0, LOWRANK), w_decay_b, m_w_decay_b, v_w_decay_b),
        "b_decay_b": ((s256, 1, 1), b_decay_b, m_b_decay_b, v_b_decay_b),
        "gla_norm_g": ((s512, 0, 1), gla_norm_g, m_gla_norm_g, v_gla_norm_g),
        "gmlp_ln_g": ((s512, 1, 1), gmlp_ln_g, m_gmlp_ln_g, v_gmlp_ln_g),
        "gmlp_ln_b": ((s512, 2, 1), gmlp_ln_b, m_gmlp_ln_b, v_gmlp_ln_b),
        "w_spatial": ((s128, 0, GMLP_W), w_spatial, m_w_spatial, v_w_spatial),
        "b_spatial": ((s128, GMLP_W, GMLP_GROUPS), b_spatial, m_b_spatial, v_b_spatial),
        "norm2_g": ((s1024, 1, 1), norm2_g, m_norm2_g, v_norm2_g),
        "final_norm_g": ((s1024, 2, 1), final_norm_g, m_final_norm_g, v_final_norm_g),
    }
    small_res = _adamw_small([(g, flat(w), flat(m), flat(v)) for g, w, m, v in small.values()])
    small_out = {n: [r.reshape(small[n][1].shape) for r in res] for n, res in zip(small, small_res)}

    order = ["norm1_g", "w_in", "w_decay_f", "b_decay_f", "w_decay_b", "b_decay_b", "gla_norm_g", "gmlp_ln_g",
             "gmlp_ln_b", "w_spatial", "b_spatial", "w_out", "norm2_g", "w_gate", "w_up", "w_down", "final_norm_g"]
    outs = []
    for kind in range(4):
        for n in order:
            outs.append(big_out[n][kind][None] if n in big_out else small_out[n][kind])
    return (loss, grad_x[None], *outs)
```

```python
import functools
import math

import jax
import jax.numpy as jnp
from jax import lax
from jax.experimental import pallas as pl
from jax.experimental.pallas import tpu as pltpu

F32 = jnp.float32
BF16 = jnp.bfloat16

D_MODEL = 1024
GLA_HEADS = 4
GLA_DK = 64
GLA_DV = 128
KEY_W = GLA_HEADS * GLA_DK
VAL_W = GLA_HEADS * GLA_DV
LOWRANK = 16
GLA_TAU = 16.0
GLA_CHUNK = 64
GMLP_W = 512
GMLP_GROUPS = 4
GMLP_CHUNK = 128
D_FF = 2816
EPS = 1e-6
Q_SCALE = GLA_DK ** -0.5
PROJ_PAD = 2688
LR_COL = 2560
LANE = 128
N_DEV = 8

ADAM_LR = 0.001
ADAM_B1 = 0.9
ADAM_B2 = 0.999
ADAM_EPS = 1e-08
ADAM_WD = 0.01
ADAM_STEP = 10

VMEM_LIMIT = 56 * 1024 * 1024
TOKEN_TILE = {"norm1": 512, "in_proj": 512, "gla": 1024, "mix_fwd": 1024, "ffn": 256, "mix_bwd": 512,
              "in_proj_bwd": 512, "dw": 2048}
DECAY_W_ROW = 8
MESH_ID = pl.DeviceIdType.MESH
INV_SQRT2 = 0.7071067811865476
INV_SQRT_2PI = 0.3989422804014327


def _params(n_axes=1):
    return pltpu.CompilerParams(dimension_semantics=("arbitrary",) * n_axes, vmem_limit_bytes=VMEM_LIMIT)


def _mm(a, b):
    return jnp.dot(a.astype(BF16), b.astype(BF16), preferred_element_type=F32)


def _mm_nt(a, b):
    return lax.dot_general(a.astype(BF16), b.astype(BF16), (((1,), (1,)), ((), ())), preferred_element_type=F32)


def _mm_tn(a, b):
    return lax.dot_general(a.astype(BF16), b.astype(BF16), (((0,), (0,)), ((), ())), preferred_element_type=F32)


def _const_spec(shape):
    nd = len(shape)
    return pl.BlockSpec(shape, lambda *_: (0,) * nd, pipeline_mode=pl.Buffered(1))


def _acc_spec(shape):
    nd = len(shape)
    return pl.BlockSpec(shape, lambda *_: (0,) * nd)


class _Comm:
    def __init__(self, inputs, in_specs, out_shape, out_specs, scratch_shapes, before, after):
        self.inputs, self.in_specs, self.out_shape, self.out_specs = inputs, in_specs, out_shape, out_specs
        self.scratch_shapes, self.before, self.after = scratch_shapes, before, after


def _fused_call(body, comms, *, name, grid, inputs, in_specs, out_specs, out_shape, scratch_shapes=()):
    n_in, n_out, n_scr = len(in_specs), len(out_specs), len(scratch_shapes)
    nsteps = math.prod(grid)
    sizes = [(len(c.inputs), len(c.out_shape), len(c.scratch_shapes)) for c in comms]

    def full_body(*refs):
        step = pl.program_id(0)
        for axis in range(1, len(grid)):
            step = step * grid[axis] + pl.program_id(axis)
        ins, rest = refs[:n_in], refs[n_in:]
        c_ins = []
        for ci, _, _ in sizes:
            c_ins.append(rest[:ci])
            rest = rest[ci:]
        outs, rest = rest[:n_out], rest[n_out:]
        c_outs = []
        for _, co, _ in sizes:
            c_outs.append(rest[:co])
            rest = rest[co:]
        scr, rest = rest[:n_scr], rest[n_scr:]
        c_scr = []
        for _, _, cs in sizes:
            c_scr.append(rest[:cs])
            rest = rest[cs:]
        for c, a, b, s in zip(comms, c_ins, c_outs, c_scr):
            c.before(step, nsteps, a, b, s)
        body(*ins, *outs, *scr)
        for c, a, b, s in zip(comms, c_ins, c_outs, c_scr):
            c.after(step, nsteps, a, b, s)

    results = pl.pallas_call(
        full_body, name=name, grid=grid,
        in_specs=list(in_specs) + [s for c in comms for s in c.in_specs],
        out_specs=tuple(out_specs) + tuple(s for c in comms for s in c.out_specs),
        out_shape=tuple(out_shape) + tuple(s for c in comms for s in c.out_shape),
        scratch_shapes=list(scratch_shapes) + [s for c in comms for s in c.scratch_shapes],
        compiler_params=_params(len(grid)),
    )(*inputs, *[a for c in comms for a in c.inputs])
    own, rest = results[:n_out], results[n_out:]
    comm_results = []
    for _, co, _ in sizes:
        comm_results.append(rest[:co])
        rest = rest[co:]
    return own, comm_results


def _gelu(x):
    return 0.5 * x * (1.0 + lax.erf(x * INV_SQRT2))


def _gelu_and_grad(x):
    cdf = 0.5 * (1.0 + lax.erf(x * INV_SQRT2))
    return x * cdf, cdf + x * jnp.exp(-0.5 * x * x) * INV_SQRT_2PI


def _sigmoid(x):
    return 0.5 + 0.5 * jnp.tanh(0.5 * x)


def _silu_and_grad(x):
    s = _sigmoid(x)
    return x * s, s * (1.0 + x * (1.0 - s))


def _norm1(x, g1, tm, comms=()):
    t = x.shape[0]

    def body(x_ref, g_ref, h_ref):
        xv = x_ref[...]
        r = lax.rsqrt(jnp.mean(xv * xv, axis=-1, keepdims=True) + EPS)
        h_ref[...] = (xv * r * g_ref[...]).astype(BF16)

    row = pl.BlockSpec((tm, D_MODEL), lambda i: (i, 0))
    return _fused_call(body, comms, name="norm1", grid=(t // tm,), inputs=(x, g1),
                       in_specs=[row, _const_spec((1, D_MODEL))], out_specs=(row,),
                       out_shape=(jax.ShapeDtypeStruct((t, D_MODEL), BF16),))


PROJ_W = 2592
LR_REF = 1536
PROJ_ROWS = ((0, LR_REF, 0), (LR_REF + 2 * LOWRANK, PROJ_W, LR_REF), (LR_REF, LR_REF + LANE, LR_COL))


def _in_proj(h, w_in_t, tm, comms=()):
    t = h.shape[0]

    def body(h_ref, w_ref, p_ref):
        hv = h_ref[...]
        for r0, r1, c0 in PROJ_ROWS:
            p_ref[:, c0:c0 + r1 - r0] = _mm_nt(hv, w_ref[r0:r1, :]).astype(BF16)

    return _fused_call(
        body, comms, name="in_proj", grid=(t // tm,), inputs=(h, w_in_t),
        in_specs=[pl.BlockSpec((tm, D_MODEL), lambda i: (i, 0)), _const_spec((PROJ_W, D_MODEL))],
        out_specs=(pl.BlockSpec((tm, PROJ_PAD), lambda i: (i, 0)),),
        out_shape=(jax.ShapeDtypeStruct((t, PROJ_PAD), BF16),))


def _tri(upper):
    r = lax.broadcasted_iota(jnp.int32, (GLA_CHUNK, GLA_CHUNK), 0)
    c = lax.broadcasted_iota(jnp.int32, (GLA_CHUNK, GLA_CHUNK), 1)
    return jnp.where((c >= r) if upper else (c <= r), 1.0, 0.0).astype(BF16)


def _chunk_cumsum(tri, a):
    hi = a.astype(BF16)
    lo = (a - hi.astype(F32)).astype(BF16)
    dot = functools.partial(jnp.dot, preferred_element_type=F32)
    return jnp.concatenate([dot(tri, hi[_chunk_rows(c)]) + dot(tri, lo[_chunk_rows(c)])
                            for c in range(a.shape[0] // GLA_CHUNK)], axis=0)


def _chunk_rows(c):
    return slice(c * GLA_CHUNK, (c + 1) * GLA_CHUNK)


def _gla_masks(rev):
    dk_bits, dv_bits = GLA_DK.bit_length() - 1, GLA_DV.bit_length() - 1
    key_head = lax.broadcasted_iota(jnp.int32, (GLA_CHUNK, KEY_W), 1) >> dk_bits
    val_head = lax.broadcasted_iota(jnp.int32, (GLA_CHUNK, VAL_W), 1) >> dv_bits
    t = lax.broadcasted_iota(jnp.int32, (GLA_HEADS * GLA_CHUNK, GLA_CHUNK), 0) & (GLA_CHUNK - 1)
    s = lax.broadcasted_iota(jnp.int32, (GLA_HEADS * GLA_CHUNK, GLA_CHUNK), 1)
    return key_head, val_head, (s >= t) if rev else (s <= t)


def _stack_heads(a, head_of_lane):
    a = a.astype(BF16)
    return jnp.concatenate([jnp.where(head_of_lane == h, a, jnp.zeros_like(a)) for h in range(GLA_HEADS)], axis=0)


def _rows_by_head(a):
    return jnp.concatenate([a[:, h * GLA_DV:(h + 1) * GLA_DV] for h in range(GLA_HEADS)], axis=0)


def _lanes_by_head(r):
    return jnp.concatenate([r[h * GLA_CHUNK:(h + 1) * GLA_CHUNK] for h in range(GLA_HEADS)], axis=1)


def _head_diagonal(r, head_of_lane):
    rows = r.shape[0] // GLA_HEADS
    out = jnp.where(head_of_lane == 0, r[:rows], 0.0)
    for h in range(1, GLA_HEADS):
        out = out + jnp.where(head_of_lane == h, r[h * rows:(h + 1) * rows], 0.0)
    return out


def _tile_terms(la, q, k, tri, rev):
    nc = la.shape[0] // GLA_CHUNK
    q, k = q.astype(F32), k.astype(F32)
    b = _chunk_cumsum(tri, la)
    ebl = [jnp.exp(b[c * GLA_CHUNK:c * GLA_CHUNK + 1] if rev else b[(c + 1) * GLA_CHUNK - 1:(c + 1) * GLA_CHUNK])
           for c in range(nc)]
    eb = jnp.exp(b)
    enb = jnp.exp(-b)
    ee = enb * jnp.concatenate([jnp.broadcast_to(row, (GLA_CHUNK, KEY_W)) for row in ebl], axis=0)
    return ebl, eb, enb, ee, q * Q_SCALE * eb, k * enb, k * ee


def _log_decay(lr_ref, wd_ref, bd_ref):
    z = _mm(lr_ref[...], wd_ref[...]) + bd_ref[...]
    return z, jax.nn.log_sigmoid(z) * (1.0 / GLA_TAU)


def _p_specs(tg, tile):
    return [pl.BlockSpec((tg, KEY_W), lambda i: (tile(i), 0)),
            pl.BlockSpec((tg, KEY_W), lambda i: (tile(i), 1)),
            pl.BlockSpec((tg, VAL_W), lambda i: (tile(i), 1)),
            pl.BlockSpec((tg, LANE), lambda i: (tile(i), LR_COL // LANE))]


def _gla_fwd_dir(rev, nc, q_ref, k_ref, v_ref, lr_ref, wd_ref, bd_ref, o_ref, st_ref, state):
    key_head, _, causal = _gla_masks(rev)
    order = range(nc - 1, -1, -1) if rev else range(nc)

    def intra():
        _, la = _log_decay(lr_ref, wd_ref, bd_ref)
        ebl, _, _, _, qd, kd, ke = _tile_terms(la, q_ref[...], k_ref[...], _tri(rev), rev)
        kd = kd.astype(BF16)
        v = {c: v_ref[_chunk_rows(c), :].astype(BF16) for c in order}
        qd_stack = {c: _stack_heads(qd[_chunk_rows(c)], key_head) for c in order}
        ke_stack = {c: _stack_heads(ke[_chunk_rows(c)], key_head) for c in order}
        a_all = {c: _mm_nt(qd_stack[c], kd[_chunk_rows(c)]) for c in order}
        a_all = {c: jnp.where(causal, a_all[c], 0.0).astype(BF16) for c in order}
        head_rows = lambda a, h: a[h * GLA_CHUNK:(h + 1) * GLA_CHUNK]
        head_vals = lambda a, h: a[:, h * GLA_DV:(h + 1) * GLA_DV]
        r = {c: [_mm(head_rows(a_all[c], h), head_vals(v[c], h)) for h in range(GLA_HEADS)] for c in order}
        upd = {c: _mm_tn(_rows_by_head(v[c]), ke_stack[c]) for c in order}
        return {c: (ebl[c], qd_stack[c], r[c], upd[c]) for c in order}

    def scan(terms):
        st = state[...]
        states = {}
        for c in order:
            states[c] = st
            st_ref[c] = st.astype(BF16)
            st = st * terms[c][0] + terms[c][3]
        state[...] = st
        return states

    def inter(terms, states):
        r_inter = {c: _mm_nt(terms[c][1], states[c]) for c in order}
        for c in order:
            o_ref[_chunk_rows(c), :] = jnp.concatenate(
                [terms[c][2][h] + r_inter[c][h * GLA_CHUNK:(h + 1) * GLA_CHUNK] for h in range(GLA_HEADS)], axis=1)

    return intra, scan, inter


def _gla_fwd(p, wd_pad_f, bd_f, wd_pad_b, bd_b, tg, comms=()):
    t = p.shape[0]
    nt = t // tg
    nc = tg // GLA_CHUNK
    up, down = (lambda i: i), (lambda i: nt - 1 - i)

    def body(qf, kf, vf, lrf, qb, kb, vb, lrb, wdf, bdf, wdb, bdb, of, stf, ob, stb, state_f, state_b):
        @pl.when(pl.program_id(0) == 0)
        def _():
            state_f[...] = jnp.zeros_like(state_f)
            state_b[...] = jnp.zeros_like(state_b)

        dirs = [_gla_fwd_dir(False, nc, qf, kf, vf, lrf, wdf, bdf, of, stf, state_f),
                _gla_fwd_dir(True, nc, qb, kb, vb, lrb, wdb, bdb, ob, stb, state_b)]
        terms = [intra() for intra, _, _ in dirs]
        states = [scan(t) for (_, scan, _), t in zip(dirs, terms)]
        for (_, _, inter), t, s in zip(dirs, terms, states):
            inter(t, s)

    wd_spec, bd_spec = _const_spec((LANE, KEY_W)), _const_spec((1, KEY_W))
    outs = lambda tile: (pl.BlockSpec((tg, VAL_W), lambda i: (tile(i), 0)),
                         pl.BlockSpec((nc, GLA_DV, KEY_W), lambda i: (tile(i), 0, 0)))
    out_shape = (jax.ShapeDtypeStruct((t, VAL_W), F32), jax.ShapeDtypeStruct((t // GLA_CHUNK, GLA_DV, KEY_W), BF16))
    return _fused_call(
        body, comms, name="gla_fwd", grid=(nt,), inputs=(p,) * 8 + (wd_pad_f, bd_f, wd_pad_b, bd_b),
        in_specs=_p_specs(tg, up) + _p_specs(tg, down) + [wd_spec, bd_spec, wd_spec, bd_spec],
        out_specs=outs(up) + outs(down), out_shape=out_shape * 2,
        scratch_shapes=[pltpu.VMEM((GLA_DV, KEY_W), F32)] * 2)


def _gla_bwd_dir(rev, nc, q_ref, k_ref, v_ref, lr_ref, wd_ref, bd_ref, st_ref, do_ref,
                 dq_ref, dk_ref, dv_ref, dlr_ref, dwd_ref, dbd_ref, dstate):
    key_head, val_head, causal = _gla_masks(rev)
    order = range(nc) if rev else range(nc - 1, -1, -1)

    def intra():
        z, la = _log_decay(lr_ref, wd_ref, bd_ref)
        tile = _tile_terms(la, q_ref[...], k_ref[...], _tri(rev), rev)
        qd, kd = tile[4], tile[5].astype(BF16)
        v = {c: v_ref[_chunk_rows(c), :].astype(BF16) for c in order}
        d_o = {c: do_ref[_chunk_rows(c), :] for c in order}
        kd_c = {c: kd[_chunk_rows(c)] for c in order}
        qd_stack = {c: _stack_heads(qd[_chunk_rows(c)], key_head) for c in order}
        do_stack = {c: _stack_heads(d_o[c], val_head) for c in order}
        do_rows = {c: _rows_by_head(d_o[c]) for c in order}
        a_all = {c: _mm_nt(qd_stack[c], kd_c[c]) for c in order}
        head_vals = lambda a, h: a[:, h * GLA_DV:(h + 1) * GLA_DV]
        da_all = {c: jnp.concatenate([_mm_nt(head_vals(d_o[c], h), head_vals(v[c], h)) for h in range(GLA_HEADS)],
                                     axis=0) for c in order}
        a_all = {c: jnp.where(causal, a_all[c], 0.0).astype(BF16) for c in order}
        da_all = {c: jnp.where(causal, da_all[c], 0.0).astype(BF16) for c in order}
        dv = {c: _mm_tn(a_all[c], do_stack[c]) for c in order}
        dqd = {c: _mm(jnp.concatenate([do_rows[c], da_all[c]], axis=1),
                      jnp.concatenate([st_ref[c], kd_c[c]], axis=0)) for c in order}
        dkd = {c: _mm_tn(da_all[c], qd_stack[c]) for c in order}
        upd = {c: _mm_tn(do_rows[c], qd_stack[c]) for c in order}
        dqd = {c: _head_diagonal(dqd[c], key_head) for c in order}
        return z, tile, {c: dict(dv=dv[c], dqd=dqd[c], dkd=dkd[c], upd=upd[c]) for c in order}

    def scan(tile, per):
        dst = dstate[...]
        dsts = {}
        for c in order:
            dsts[c] = dst
            dst = dst * tile[0][c] + per[c]["upd"]
        dstate[...] = dst
        return dsts

    def inter(z, tile, per, dsts):
        ebl, eb, enb, ee, qd, kd, ke = tile
        ke_stack = {c: _stack_heads(ke[_chunk_rows(c)], key_head) for c in order}
        v_rows = {c: _rows_by_head(v_ref[_chunk_rows(c), :].astype(BF16)) for c in order}
        dst_b = {c: dsts[c].astype(BF16) for c in order}
        dv_state = {c: _mm_nt(ke_stack[c], dst_b[c]) for c in order}
        dke_c = {c: _mm(v_rows[c], dst_b[c]) for c in order}
        dke_c = {c: _head_diagonal(dke_c[c], key_head) for c in order}
        dbl_c = {}
        for c in order:
            rows = _chunk_rows(c)
            dv_ref[rows, :] = (per[c]["dv"] + _lanes_by_head(dv_state[c])).astype(BF16)
            dbl = (jnp.sum(dsts[c] * st_ref[c].astype(F32), axis=0, keepdims=True) * ebl[c]
                   + jnp.sum(dke_c[c] * ke[rows], axis=0, keepdims=True))
            dbl_c[c] = jnp.broadcast_to(dbl, (GLA_CHUNK, KEY_W))
        tile_of = lambda parts: jnp.concatenate([parts[c] for c in range(nc)], axis=0)
        dqd, dkd = tile_of({c: per[c]["dqd"] for c in order}), tile_of({c: per[c]["dkd"] for c in order})
        dke, dbl = tile_of(dke_c), tile_of(dbl_c)
        dq_ref[...] = (dqd * eb * Q_SCALE).astype(BF16)
        dk_ref[...] = (dkd * enb + dke * ee).astype(BF16)
        db = dqd * qd - dkd * kd - dke * ke
        dz = (_chunk_cumsum(_tri(not rev), db) + dbl) * (_sigmoid(-z) * (1.0 / GLA_TAU))
        dlr_ref[...] = _mm_nt(dz, wd_ref[...]).astype(BF16)
        dwd_ref[...] += _mm_tn(lr_ref[...], dz)
        dbd_ref[...] += jnp.sum(dz, axis=0, keepdims=True)

    return intra, scan, inter


def _gla_bwd(p, wd_pad_f, bd_f, wd_pad_b, bd_b, st_f, st_b, d_o, tg, comms=()):
    t = p.shape[0]
    nt = t // tg
    nc = tg // GLA_CHUNK
    up, down = (lambda i: i), (lambda i: nt - 1 - i)

    def body(qf, kf, vf, lrf, stf, dof, qb, kb, vb, lrb, stb, dob, wdf, bdf, wdb, bdb,
             dqf, dkf, dvf, dlrf, dwdf, dbdf, dqb, dkb, dvb, dlrb, dwdb, dbdb, dstate_f, dstate_b):
        @pl.when(pl.program_id(0) == 0)
        def _():
            for ref in (dstate_f, dstate_b, dwdf, dbdf, dwdb, dbdb):
                ref[...] = jnp.zeros_like(ref)

        dirs = [_gla_bwd_dir(False, nc, qf, kf, vf, lrf, wdf, bdf, stf, dof, dqf, dkf, dvf, dlrf, dwdf, dbdf,
                             dstate_f),
                _gla_bwd_dir(True, nc, qb, kb, vb, lrb, wdb, bdb, stb, dob, dqb, dkb, dvb, dlrb, dwdb, dbdb,
                             dstate_b)]
        first = [intra() for intra, _, _ in dirs]
        dsts = [scan(tile, per) for (_, scan, _), (_, tile, per) in zip(dirs, first)]
        for (_, _, inter), (z, tile, per), d in zip(dirs, first, dsts):
            inter(z, tile, per, d)

    wd_spec, bd_spec = _const_spec((LANE, KEY_W)), _const_spec((1, KEY_W))
    ins = lambda tile: _p_specs(tg, tile) + [pl.BlockSpec((nc, GLA_DV, KEY_W), lambda i: (tile(i), 0, 0)),
                                             pl.BlockSpec((tg, VAL_W), lambda i: (tile(i), 0))]
    outs = lambda tile: (pl.BlockSpec((tg, KEY_W), lambda i: (tile(i), 0)),
                         pl.BlockSpec((tg, KEY_W), lambda i: (tile(i), 0)),
                         pl.BlockSpec((tg, VAL_W), lambda i: (tile(i), 0)),
                         pl.BlockSpec((tg, LANE), lambda i: (tile(i), 0)),
                         _acc_spec((LANE, KEY_W)), _acc_spec((1, KEY_W)))
    out_shape = (jax.ShapeDtypeStruct((t, KEY_W), BF16), jax.ShapeDtypeStruct((t, KEY_W), BF16),
                 jax.ShapeDtypeStruct((t, VAL_W), BF16), jax.ShapeDtypeStruct((t, LANE), BF16),
                 jax.ShapeDtypeStruct((LANE, KEY_W), F32), jax.ShapeDtypeStruct((1, KEY_W), F32))
    scratch = [pltpu.VMEM((GLA_DV, KEY_W), F32)]
    return _fused_call(
        body, comms, name="gla_bwd", grid=(nt,),
        inputs=(p, p, p, p, st_f, d_o, p, p, p, p, st_b, d_o, wd_pad_f, bd_f, wd_pad_b, bd_b),
        in_specs=ins(down) + ins(up) + [wd_spec, bd_spec, wd_spec, bd_spec],
        out_specs=outs(down) + outs(up), out_shape=out_shape * 2, scratch_shapes=scratch * 2)


def _head_rms(o):
    parts, scales = [], []
    for h in range(GLA_HEADS):
        oh = o[:, h * GLA_DV:(h + 1) * GLA_DV]
        r = lax.rsqrt(jnp.mean(oh * oh, axis=-1, keepdims=True) + EPS)
        parts.append(oh * r)
        scales.append(jnp.broadcast_to(r, oh.shape))
    return jnp.concatenate(parts, axis=1), jnp.concatenate(scales, axis=1)


def _layernorm_stats(zv):
    mu = jnp.mean(zv, axis=-1, keepdims=True)
    xc = zv - mu
    rs = lax.rsqrt(jnp.mean(xc * xc, axis=-1, keepdims=True) + EPS)
    return xc * rs, rs


def _mix_fwd(x, o_f, o_b, p, gla_g, ln_g, ln_b, w_sp, b_sp, w_out, tm, comms=()):
    t = x.shape[0]
    nch = tm // GMLP_CHUNK

    def body(x_ref, of_ref, ob_ref, pg_ref, pu_ref, pv_ref, gg_ref, lg_ref, lb_ref, ws_ref, bs_ref, wo_ref,
             x1_ref, y_ref, s_scr):
        on, _ = _head_rms(of_ref[...] + ob_ref[...])
        pg = pg_ref[...].astype(F32)
        y_a = on * gg_ref[...] * (pg * _sigmoid(pg))
        zu = _gelu(pu_ref[...].astype(F32))
        vhat, _ = _layernorm_stats(_gelu(pv_ref[...].astype(F32)))
        vln = (vhat * lg_ref[...] + lb_ref[...]).astype(BF16)
        for g in range(GMLP_GROUPS):
            w_g = ws_ref[g].astype(BF16)
            b_g = bs_ref[g]
            cols = slice(g * LANE, (g + 1) * LANE)
            for n in range(nch):
                rows = slice(n * GMLP_CHUNK, (n + 1) * GMLP_CHUNK)
                s_scr[rows, cols] = jnp.dot(w_g, vln[rows, cols], preferred_element_type=F32) + b_g
        ycat = jnp.concatenate([y_a, zu * s_scr[...]], axis=1).astype(BF16)
        y_ref[...] = ycat
        x1_ref[...] = x_ref[...] + jnp.dot(ycat, wo_ref[...], preferred_element_type=F32)

    half = lambda j: pl.BlockSpec((tm, VAL_W), lambda i: (i, j))
    return _fused_call(
        body, comms, name="mix_fwd", grid=(t // tm,),
        inputs=(x, o_f, o_b, p, p, p, gla_g, ln_g, ln_b, w_sp, b_sp, w_out),
        in_specs=[pl.BlockSpec((tm, D_MODEL), lambda i: (i, 0)), half(0), half(0), half(2), half(3), half(4),
                  _const_spec((1, VAL_W)), _const_spec((1, GMLP_W)), _const_spec((1, GMLP_W)),
                  _const_spec((GMLP_GROUPS, GMLP_CHUNK, GMLP_CHUNK)), _const_spec((GMLP_GROUPS, GMLP_CHUNK, 1)),
                  _const_spec((D_MODEL, D_MODEL))],
        out_specs=(pl.BlockSpec((tm, D_MODEL), lambda i: (i, 0)), pl.BlockSpec((tm, D_MODEL), lambda i: (i, 0))),
        out_shape=(jax.ShapeDtypeStruct((t, D_MODEL), F32), jax.ShapeDtypeStruct((t, D_MODEL), BF16)),
        scratch_shapes=[pltpu.VMEM((tm, GMLP_W), F32)])


def _mix_bwd(dx1, ycat, o_f, o_b, p, gla_g, ln_g, ln_b, w_sp, b_sp, w_out, tm, comms=()):
    t = dx1.shape[0]
    nch = tm // GMLP_CHUNK

    def body(dx1_ref, y_ref, of_ref, ob_ref, pg_ref, pu_ref, pv_ref, gg_ref, lg_ref, lb_ref, ws_ref, bs_ref, wo_ref,
             do_ref, dpg_ref, dpu_ref, dpv_ref, dwo_ref, dgg_ref, dlg_ref, dlb_ref, dws_ref, dbs_ref,
             s_scr, dvln_scr):
        @pl.when(pl.program_id(0) == 0)
        def _():
            for ref in (dwo_ref, dgg_ref, dlg_ref, dlb_ref, dws_ref, dbs_ref):
                ref[...] = jnp.zeros_like(ref)

        dx1 = dx1_ref[...].astype(BF16)
        dycat = _mm_nt(dx1, wo_ref[...])
        dwo_ref[...] += _mm_tn(y_ref[...], dx1)
        dy_a = dycat[:, :VAL_W]
        dy_b = dycat[:, VAL_W:]
        on, r = _head_rms(of_ref[...] + ob_ref[...])
        pg = pg_ref[...].astype(F32)
        sil, dsil = _silu_and_grad(pg)
        gg = gg_ref[...]
        dgg_ref[...] += jnp.sum(dy_a * sil * on, axis=0, keepdims=True)
        don = dy_a * sil * gg
        prod = don * on
        means = jnp.concatenate(
            [jnp.broadcast_to(jnp.mean(prod[:, h * GLA_DV:(h + 1) * GLA_DV], axis=-1, keepdims=True),
                              (tm, GLA_DV)) for h in range(GLA_HEADS)], axis=1)
        do_ref[...] = (r * (don - on * means)).astype(BF16)
        dpg_ref[...] = (dy_a * on * gg * dsil).astype(BF16)
        pu = pu_ref[...].astype(F32)
        pv = pv_ref[...].astype(F32)
        zu, dzu_dpu = _gelu_and_grad(pu)
        zv, dzv_dpv = _gelu_and_grad(pv)
        vhat, rs = _layernorm_stats(zv)
        lg = lg_ref[...]
        vln = (vhat * lg + lb_ref[...]).astype(BF16)
        ds32 = dy_b * zu
        ds = ds32.astype(BF16)
        blocks = [(g, n) for g in range(GMLP_GROUPS) for n in range(nch)]
        at = lambda g, n: (slice(n * GMLP_CHUNK, (n + 1) * GMLP_CHUNK), slice(g * LANE, (g + 1) * LANE))
        w_sp = [ws_ref[g].astype(BF16) for g in range(GMLP_GROUPS)]
        v_blk = {b: vln[at(*b)] for b in blocks}
        ds_blk = {b: ds[at(*b)] for b in blocks}
        s_blk = {b: jnp.dot(w_sp[b[0]], v_blk[b], preferred_element_type=F32) for b in blocks}
        dw_blk = {b: _mm_nt(ds_blk[b], v_blk[b]) for b in blocks}
        dvln_blk = {b: _mm_tn(w_sp[b[0]], ds_blk[b]) for b in blocks}
        for b in blocks:
            s_scr[at(*b)] = s_blk[b] + bs_ref[b[0]]
            dvln_scr[at(*b)] = dvln_blk[b]
        for g in range(GMLP_GROUPS):
            dws_ref[g] += sum(dw_blk[(g, n)] for n in range(nch))
            dbs_ref[g] += sum(jnp.sum(ds32[at(g, n)], axis=-1, keepdims=True) for n in range(nch))
        dpu_ref[...] = (dy_b * s_scr[...] * dzu_dpu).astype(BF16)
        dvln = dvln_scr[...]
        dlg_ref[...] += jnp.sum(dvln * vhat, axis=0, keepdims=True)
        dlb_ref[...] += jnp.sum(dvln, axis=0, keepdims=True)
        dvhat = dvln * lg
        dzv = rs * (dvhat - jnp.mean(dvhat, axis=-1, keepdims=True)
                    - vhat * jnp.mean(dvhat * vhat, axis=-1, keepdims=True))
        dpv_ref[...] = (dzv * dzv_dpv).astype(BF16)

    half = lambda j: pl.BlockSpec((tm, VAL_W), lambda i: (i, j))
    full = pl.BlockSpec((tm, D_MODEL), lambda i: (i, 0))
    sp_shape = (GMLP_GROUPS, GMLP_CHUNK, GMLP_CHUNK)
    bs_shape = (GMLP_GROUPS, GMLP_CHUNK, 1)
    return _fused_call(
        body, comms, name="mix_bwd", grid=(t // tm,),
        inputs=(dx1, ycat, o_f, o_b, p, p, p, gla_g, ln_g, ln_b, w_sp, b_sp, w_out),
        in_specs=[full, full, half(0), half(0), half(2), half(3), half(4),
                  _const_spec((1, VAL_W)), _const_spec((1, GMLP_W)), _const_spec((1, GMLP_W)),
                  _const_spec(sp_shape), _const_spec(bs_shape), _const_spec((D_MODEL, D_MODEL))],
        out_specs=(half(0), half(0), half(0), half(0), _acc_spec((D_MODEL, D_MODEL)), _acc_spec((1, VAL_W)),
                   _acc_spec((1, GMLP_W)), _acc_spec((1, GMLP_W)), _acc_spec(sp_shape), _acc_spec(bs_shape)),
        out_shape=(jax.ShapeDtypeStruct((t, VAL_W), BF16),) * 4 + (
            jax.ShapeDtypeStruct((D_MODEL, D_MODEL), F32), jax.ShapeDtypeStruct((1, VAL_W), F32),
            jax.ShapeDtypeStruct((1, GMLP_W), F32), jax.ShapeDtypeStruct((1, GMLP_W), F32),
            jax.ShapeDtypeStruct(sp_shape, F32), jax.ShapeDtypeStruct(bs_shape, F32)),
        scratch_shapes=[pltpu.VMEM((tm, GMLP_W), F32), pltpu.VMEM((tm, GMLP_W), F32)])


def _rms_bwd(dy_scaled, xn, r):
    return r * (dy_scaled - xn * jnp.mean(dy_scaled * xn, axis=-1, keepdims=True))


def _ffn(x1, target, g2, gf, w_gate, w_up, w_down, tm):
    t = x1.shape[0]

    def body(x1_ref, tg_ref, g2_ref, gf_ref, wg_ref, wu_ref, wd_ref,
             dx1_ref, h2_ref, dgate_ref, dup_ref, act_ref, dx2_ref, loss_ref, dgf_ref, dg2_ref):
        @pl.when(pl.program_id(0) == 0)
        def _():
            for ref in (loss_ref, dgf_ref, dg2_ref):
                ref[...] = jnp.zeros_like(ref)

        x1v = x1_ref[...]
        g2v = g2_ref[...]
        gfv = gf_ref[...]
        r2 = lax.rsqrt(jnp.mean(x1v * x1v, axis=-1, keepdims=True) + EPS)
        xn1 = x1v * r2
        h2 = (xn1 * g2v).astype(BF16)
        h2_ref[...] = h2
        gate = _mm_nt(h2, wg_ref[...])
        up = _mm_nt(h2, wu_ref[...])
        sil, dsil = _silu_and_grad(gate)
        act = (sil * up).astype(BF16)
        act_ref[...] = act
        x2 = x1v + jnp.dot(act, wd_ref[...], preferred_element_type=F32)
        rf = lax.rsqrt(jnp.mean(x2 * x2, axis=-1, keepdims=True) + EPS)
        xn2 = x2 * rf
        err = xn2 * gfv - tg_ref[...]
        loss_ref[...] += 0.5 * jnp.sum(jnp.mean(err * err, axis=-1, keepdims=True))
        dy = err * (1.0 / D_MODEL)
        dgf_ref[...] += jnp.sum(dy * xn2, axis=0, keepdims=True)
        dx2 = _rms_bwd(dy * gfv, xn2, rf)
        dx2b = dx2.astype(BF16)
        dx2_ref[...] = dx2b
        dact = _mm_nt(dx2b, wd_ref[...])
        dgate = (dact * up * dsil).astype(BF16)
        dup = (dact * sil).astype(BF16)
        dgate_ref[...] = dgate
        dup_ref[...] = dup
        dh2 = _mm(dgate, wg_ref[...]) + _mm(dup, wu_ref[...])
        dg2_ref[...] += jnp.sum(dh2 * xn1, axis=0, keepdims=True)
        dx1_ref[...] = dx2 + _rms_bwd(dh2 * g2v, xn1, r2)

    row = lambda w: pl.BlockSpec((tm, w), lambda i: (i, 0))
    return pl.pallas_call(
        body, name="ffn_fwd_bwd", grid=(t // tm,),
        in_specs=[row(D_MODEL), row(D_MODEL), _const_spec((1, D_MODEL)), _const_spec((1, D_MODEL)),
                  _const_spec((D_FF, D_MODEL)), _const_spec((D_FF, D_MODEL)), _const_spec((D_FF, D_MODEL))],
        out_specs=(row(D_MODEL), row(D_MODEL), row(D_FF), row(D_FF), row(D_FF), row(D_MODEL),
                   _acc_spec((8, LANE)), _acc_spec((1, D_MODEL)), _acc_spec((1, D_MODEL))),
        out_shape=(jax.ShapeDtypeStruct((t, D_MODEL), F32), jax.ShapeDtypeStruct((t, D_MODEL), BF16),
                   jax.ShapeDtypeStruct((t, D_FF), BF16), jax.ShapeDtypeStruct((t, D_FF), BF16),
                   jax.ShapeDtypeStruct((t, D_FF), BF16), jax.ShapeDtypeStruct((t, D_MODEL), BF16),
                   jax.ShapeDtypeStruct((8, LANE), F32), jax.ShapeDtypeStruct((1, D_MODEL), F32),
                   jax.ShapeDtypeStruct((1, D_MODEL), F32)),
        compiler_params=_params(),
    )(x1, target, g2, gf, w_gate, w_up, w_down)


def _matmul_tn(a, b, tm, tk, name, comms=()):
    t, m = a.shape
    n = b.shape[1]

    def body(a_ref, b_ref, o_ref):
        @pl.when(pl.program_id(1) == 0)
        def _():
            o_ref[...] = jnp.zeros_like(o_ref)

        o_ref[...] += _mm_tn(a_ref[...], b_ref[...])

    (out,), comm_results = _fused_call(
        body, comms, name=name, grid=(m // tm, t // tk), inputs=(a, b),
        in_specs=[pl.BlockSpec((tk, tm), lambda j, k: (k, j)), pl.BlockSpec((tk, n), lambda j, k: (k, 0))],
        out_specs=(pl.BlockSpec((tm, n), lambda j, k: (j, 0)),),
        out_shape=(jax.ShapeDtypeStruct((m, n), F32),))
    return out, comm_results


def _in_proj_bwd(x, g1, dx1, dq_f, dq_b, dk_f, dk_b, dv_f, dv_b, dpg, dpu, dpv, dlr_f, dlr_b, w_main, tm, comms=()):
    t = x.shape[0]

    def body(x_ref, g_ref, dx1_ref, dqf, dqb, dkf, dkb, dvf, dvb, dg, du, dv, dlf, dlb, w_ref,
             dx_ref, dp_ref, dg1_ref):
        @pl.when(pl.program_id(0) == 0)
        def _():
            dg1_ref[...] = jnp.zeros_like(dg1_ref)

        both = lambda a, b: (a[...].astype(F32) + b[...].astype(F32)).astype(BF16)
        dp = jnp.concatenate([both(dqf, dqb), both(dkf, dkb), both(dvf, dvb), dg[...], du[...], dv[...],
                              both(dlf, dlb)], axis=1)
        dp_ref[...] = dp
        dh = sum(_mm(dp[:, c0:c0 + r1 - r0], w_ref[r0:r1, :]) for r0, r1, c0 in PROJ_ROWS)
        xv = x_ref[...]
        r = lax.rsqrt(jnp.mean(xv * xv, axis=-1, keepdims=True) + EPS)
        xn = xv * r
        dg1_ref[...] += jnp.sum(dh * xn, axis=0, keepdims=True)
        dx_ref[...] = dx1_ref[...] + _rms_bwd(dh * g_ref[...], xn, r)

    row = lambda w: pl.BlockSpec((tm, w), lambda i: (i, 0))
    return _fused_call(
        body, comms, name="in_proj_bwd", grid=(t // tm,),
        inputs=(x, g1, dx1, dq_f, dq_b, dk_f, dk_b, dv_f, dv_b, dpg, dpu, dpv, dlr_f, dlr_b, w_main),
        in_specs=[row(D_MODEL), _const_spec((1, D_MODEL)), row(D_MODEL), row(KEY_W), row(KEY_W), row(KEY_W),
                  row(KEY_W), row(VAL_W), row(VAL_W), row(VAL_W), row(VAL_W), row(VAL_W), row(LANE), row(LANE),
                  _const_spec((PROJ_W, D_MODEL))],
        out_specs=(row(D_MODEL), row(PROJ_PAD), _acc_spec((1, D_MODEL))),
        out_shape=(jax.ShapeDtypeStruct((t, D_MODEL), F32), jax.ShapeDtypeStruct((t, PROJ_PAD), BF16),
                   jax.ShapeDtypeStruct((1, D_MODEL), F32)))


def _adamw(w, g, m, v):
    m_new = ADAM_B1 * m + (1.0 - ADAM_B1) * g
    v_new = ADAM_B2 * v + (1.0 - ADAM_B2) * (g * g)
    m_hat = m_new / (1.0 - ADAM_B1 ** ADAM_STEP)
    v_hat = v_new / (1.0 - ADAM_B2 ** ADAM_STEP)
    delta = -ADAM_LR * (m_hat / (jnp.sqrt(v_hat) + ADAM_EPS) + ADAM_WD * w)
    return delta, m_new, v_new


def _adamw_shard(own, recv, w, m, v, tr, name):
    r, c = w.shape

    def body(own_ref, recv_ref, w_ref, m_ref, v_ref, g_ref, d_ref, nm_ref, nv_ref):
        g = own_ref[...]
        for k in range(3):
            g = g + recv_ref[k].astype(F32)
        g_ref[...] = g
        d_ref[...], nm_ref[...], nv_ref[...] = _adamw(w_ref[...], g, m_ref[...], v_ref[...])

    row = pl.BlockSpec((tr, c), lambda i: (i, 0))
    return pl.pallas_call(
        body, name=name, grid=(r // tr,),
        in_specs=[row, pl.BlockSpec((3, tr, c), lambda i: (0, i, 0)), row, row, row],
        out_specs=(row,) * 4, out_shape=(jax.ShapeDtypeStruct((r, c), F32),) * 4,
        compiler_params=_params(),
    )(own, recv, w, m, v)


def _adamw_small(entries):
    stacks = []
    for (g, _, _), _, _, _ in entries:
        if not any(g is s for s in stacks):
            stacks.append(g)
    where = [next(i for i, s in enumerate(stacks) if s is g) for (g, _, _), _, _, _ in entries]
    ns, ne = len(stacks), len(entries)

    def body(*refs):
        s_refs, wmv, outs = refs[:ns], refs[ns:ns + 3 * ne], refs[ns + 3 * ne:]
        for e, ((_, r0, nr), _, _, _) in enumerate(entries):
            grad = s_refs[where[e]][r0:r0 + nr, :]
            w_ref, m_ref, v_ref = wmv[3 * e:3 * e + 3]
            g_ref, d_ref, nm_ref, nv_ref = outs[4 * e:4 * e + 4]
            g_ref[...] = grad
            d_ref[...], nm_ref[...], nv_ref[...] = _adamw(w_ref[...], grad, m_ref[...], v_ref[...])

    results = pl.pallas_call(
        body, name="adamw_small",
        out_shape=tuple(jax.ShapeDtypeStruct(w.shape, F32) for _, w, _, _ in entries for _ in range(4)),
        compiler_params=pltpu.CompilerParams(vmem_limit_bytes=VMEM_LIMIT),
    )(*stacks, *[a for _, w, m, v in entries for a in (w, m, v)])
    return [results[4 * e:4 * e + 4] for e in range(ne)]


def _mesh_pos():
    return lax.axis_index("x"), lax.axis_index("y"), lax.axis_index("c")


def _other_chips(x, y):
    return [(x, 1 - y), (1 - x, y), (1 - x, 1 - y)]


_VMEM_WHOLE = pl.BlockSpec(memory_space=pltpu.VMEM)
_HBM_WHOLE = pl.BlockSpec(memory_space=pl.ANY)


def _gather_comm(shards, cast, mid=((1, 2), (3, 4))):
    na = len(shards)
    staged = [a for a in range(na) if cast[a]]

    def phases(in_refs, out_refs, scr):
        stage = dict(zip(staged, scr[:len(staged)]))
        send_sems, recv_sems, local_sems = scr[len(staged):]
        x, y, c = _mesh_pos()
        me, sibling = (x, y, c), (x, y, 1 - c)
        chip_a, chip_b, diagonal = (x ^ c, y ^ (1 - c)), (x ^ (1 - c), y ^ c), (1 - x, 1 - y)
        srcs = [stage[a] if cast[a] else in_refs[a] for a in range(na)]

        def rows(a, pos):
            px, py, pc = pos
            return out_refs[a].at[4 * px + 2 * py + pc]

        def copy(a, k, block, to, src=None):
            return pltpu.make_async_remote_copy(
                src_ref=rows(a, block) if src is None else src, dst_ref=rows(a, block),
                send_sem=send_sems.at[a, k], recv_sem=recv_sems.at[a, k], device_id=to, device_id_type=MESH_ID)

        mine = [pltpu.make_async_copy(srcs[a], rows(a, me), local_sems.at[a]) for a in range(na)]
        own = [copy(a, k, me, to, src=srcs[a]) for a in range(na)
               for k, to in ((0, sibling), (1, (*chip_a, c)), (2, (*chip_b, c)))]
        onward = [copy(a, 3, (*chip_a, c), (*chip_b, c)) for a in range(na)]
        to_sibling = {k: [copy(a, k, (*chip, c), sibling) for a in range(na)]
                      for k, chip in ((4, chip_a), (5, chip_b), (6, diagonal))}

        def start():
            for a in staged:
                stage[a][...] = in_refs[a][...].astype(BF16)
            for cp in mine + own:
                cp.start()

        def forward_neighbours():
            for a in range(na):
                copy(a, 1, (*chip_a, c), me).wait_recv()
                onward[a].start()
                to_sibling[4][a].start()
            for a in range(na):
                copy(a, 2, (*chip_b, c), me).wait_recv()
                to_sibling[5][a].start()

        def forward_diagonal():
            for a in range(na):
                copy(a, 3, (*diagonal, c), me).wait_recv()
                to_sibling[6][a].start()

        def finish():
            for a in range(na):
                for k, chip in ((0, (x, y)), (4, chip_b), (5, chip_a), (6, diagonal)):
                    copy(a, k, (*chip, 1 - c), me).wait_recv()
            for cp in own + onward + to_sibling[4] + to_sibling[5] + to_sibling[6]:
                cp.wait_send()
            for cp in mine:
                cp.wait()

        return start, forward_neighbours, forward_diagonal, finish

    def before(step, nsteps, in_refs, out_refs, scr):
        start, forward_neighbours, forward_diagonal, _ = phases(in_refs, out_refs, scr)
        pl.when(step == 0)(start)
        pl.when(step == nsteps * mid[0][0] // mid[0][1])(forward_neighbours)
        pl.when(step == nsteps * mid[1][0] // mid[1][1])(forward_diagonal)

    def after(step, nsteps, in_refs, out_refs, scr):
        pl.when(step == nsteps - 1)(phases(in_refs, out_refs, scr)[3])

    return _Comm(
        inputs=list(shards), in_specs=[_VMEM_WHOLE] * na,
        out_shape=[jax.ShapeDtypeStruct((N_DEV,) + s.shape, BF16 if cast[a] else s.dtype)
                   for a, s in enumerate(shards)],
        out_specs=[_HBM_WHOLE] * na,
        scratch_shapes=[pltpu.VMEM(shards[a].shape, BF16) for a in staged] + [
            pltpu.SemaphoreType.DMA((na, 7)), pltpu.SemaphoreType.DMA((na, 7)), pltpu.SemaphoreType.DMA((na,))],
        before=before, after=after)


def _exchange_comm(arrays, out_shape, make_copies):
    na = len(arrays)

    def copies(in_refs, out_refs, scr):
        return make_copies(in_refs, out_refs, *scr)

    def before(step, nsteps, in_refs, out_refs, scr):
        @pl.when(step == 0)
        def _():
            for cp in copies(in_refs, out_refs, scr):
                cp.start()

    def after(step, nsteps, in_refs, out_refs, scr):
        @pl.when(step == nsteps - 1)
        def _():
            for cp in copies(in_refs, out_refs, scr):
                cp.wait()

    return _Comm(inputs=list(arrays), in_specs=[_HBM_WHOLE] * na, out_shape=list(out_shape),
                 out_specs=[_HBM_WHOLE] * na,
                 scratch_shapes=[pltpu.SemaphoreType.DMA((na, 3)), pltpu.SemaphoreType.DMA((na, 3))],
                 before=before, after=after)


def _sibling_exchange_comm(grads):
    def make_copies(in_refs, out_refs, send_sems, recv_sems):
        x, y, c = _mesh_pos()
        return [pltpu.make_async_remote_copy(
            src_ref=in_refs[a].at[:, pl.ds(1 - c, 1)], dst_ref=out_refs[a], send_sem=send_sems.at[a, 0],
            recv_sem=recv_sems.at[a, 0], device_id=(x, y, 1 - c), device_id_type=MESH_ID)
            for a in range(len(grads))]

    return _exchange_comm(grads, [jax.ShapeDtypeStruct((4, 1) + g.shape[2:], F32) for g in grads], make_copies)


def _chips_exchange_comm(partials):
    def make_copies(in_refs, out_refs, send_sems, recv_sems):
        x, y, c = _mesh_pos()
        return [pltpu.make_async_remote_copy(
            src_ref=in_refs[a].at[j], dst_ref=out_refs[a].at[j], send_sem=send_sems.at[a, j],
            recv_sem=recv_sems.at[a, j], device_id=(*chip, c), device_id_type=MESH_ID)
            for a in range(len(partials)) for j, chip in enumerate(_other_chips(x, y))]

    return _exchange_comm(partials, [jax.ShapeDtypeStruct(g.shape, BF16) for g in partials], make_copies)


def _comm_only(comms, name):
    return _fused_call(lambda: None, comms, name=name, grid=(1,), inputs=(), in_specs=[], out_specs=(),
                       out_shape=())[1]


def _chip_sum(my_pos, mine, from_sibling, tr, name):
    _, _, r, c = mine.shape

    def body(pos_ref, a_ref, b_ref, own_ref, out_ref):
        s = a_ref[0, 0] + b_ref[0, 0]

        @pl.when(pl.program_id(1) == 0)
        def _():
            own_ref[...] = s

        @pl.when(pl.program_id(1) > 0)
        def _():
            out_ref[0] = s.astype(BF16)

    grid_spec = pltpu.PrefetchScalarGridSpec(
        num_scalar_prefetch=1, grid=(r // tr, 4),
        in_specs=[pl.BlockSpec((1, 1, tr, c), lambda i, k, pos: (pos[0] ^ k, pos[1], i, 0)),
                  pl.BlockSpec((1, 1, tr, c), lambda i, k, pos: (pos[0] ^ k, 0, i, 0))],
        out_specs=(pl.BlockSpec((tr, c), lambda i, k, pos: (i, 0)),
                   pl.BlockSpec((1, tr, c), lambda i, k, pos: (jnp.maximum(k - 1, 0), i, 0))))
    return pl.pallas_call(
        body, name=name, grid_spec=grid_spec,
        out_shape=(jax.ShapeDtypeStruct((r, c), F32), jax.ShapeDtypeStruct((3, r, c), BF16)),
        compiler_params=_params(2),
    )(my_pos, mine, from_sibling)


def _all_reduce_small_comm(parts):
    na = len(parts)

    def copies(in_refs, scr):
        gathered, (send_sems, recv_sems) = scr[:na], scr[na:]
        x, y, c = _mesh_pos()
        my_id = 4 * x + 2 * y + c
        return my_id, [pltpu.make_async_remote_copy(
            src_ref=in_refs[a], dst_ref=gathered[a].at[my_id], send_sem=send_sems.at[a, k - 1],
            recv_sem=recv_sems.at[a, k - 1], device_id=(x ^ (k >> 2), y ^ ((k >> 1) & 1), c ^ (k & 1)),
            device_id_type=MESH_ID) for a in range(na) for k in range(1, N_DEV)]

    def before(step, nsteps, in_refs, out_refs, scr):
        @pl.when(step == 0)
        def _():
            for cp in copies(in_refs, scr)[1]:
                cp.start()

    def after(step, nsteps, in_refs, out_refs, scr):
        @pl.when(step == nsteps - 1)
        def _():
            my_id, cps = copies(in_refs, scr)
            for a in range(na):
                scr[a][my_id] = in_refs[a][...]
            for cp in cps:
                cp.wait()
            for a in range(na):
                acc = scr[a][0]
                for d in range(1, N_DEV):
                    acc = acc + scr[a][d]
                out_refs[a][...] = acc

    return _Comm(inputs=list(parts), in_specs=[_VMEM_WHOLE] * na,
                 out_shape=[jax.ShapeDtypeStruct(p.shape, F32) for p in parts], out_specs=[_VMEM_WHOLE] * na,
                 scratch_shapes=[pltpu.VMEM((N_DEV,) + p.shape, F32) for p in parts] + [
                     pltpu.SemaphoreType.DMA((na, N_DEV - 1)), pltpu.SemaphoreType.DMA((na, N_DEV - 1))],
                 before=before, after=after)


def _unshard_cols(g):
    return jnp.transpose(g, (1, 0, 2)).reshape(g.shape[1], N_DEV * g.shape[2])


def _row_blocks(w):
    return w.reshape(4, 2, w.shape[0] // N_DEV, w.shape[1])


def _stack_rows(parts):
    a = jnp.concatenate(parts, axis=0)
    return jnp.pad(a, ((0, (-a.shape[0]) % 8), (0, 0)))


def _w_in_grad_blocks(dw):
    return _row_blocks(jnp.concatenate([dw[:LR_REF], dw[LR_COL:LR_COL + 2 * LOWRANK], dw[LR_REF:LR_COL]], axis=0))


def _padded_decay_weights(wd_f, wd_b):
    zeros = lambda n: jnp.zeros((n, KEY_W), F32)
    return (jnp.concatenate([wd_f, zeros(LANE - LOWRANK)], axis=0),
            jnp.concatenate([zeros(LOWRANK), wd_b, zeros(LANE - 2 * LOWRANK)], axis=0))


def kernel(x, norm1_g, w_in,w_decay_f, b_decay_f, w_decay_b, b_decay_b, gla_norm_g, gmlp_ln_g, gmlp_ln_b, w_spatial, b_spatial, w_out, norm2_g, w_gate, w_up, w_down, final_norm_g, loss_target, m_norm1_g, m_w_in, m_w_decay_f, m_b_decay_f, m_w_decay_b, m_b_decay_b, m_gla_norm_g, m_gmlp_ln_g, m_gmlp_ln_b, m_w_spatial, m_b_spatial, m_w_out, m_norm2_g, m_w_gate, m_w_up, m_w_down, m_final_norm_g, v_norm1_g, v_w_in, v_w_decay_f, v_b_decay_f, v_w_decay_b, v_b_decay_b, v_gla_norm_g, v_gmlp_ln_g, v_gmlp_ln_b, v_w_spatial, v_b_spatial, v_w_out, v_norm2_g, v_w_gate, v_w_up, v_w_down, v_final_norm_g):
    t = x.shape[1]
    xt = x[0]
    target = loss_target[0]
    pos_x, pos_y, pos_c = _mesh_pos()
    my_pos = jnp.stack([2 * pos_x + pos_y, pos_c]).astype(jnp.int32)
    my_id = 4 * pos_x + 2 * pos_y + pos_c

    tile = lambda n: min(n, t)
    ln_g, ln_b, w_sp = gmlp_ln_g, gmlp_ln_b, w_spatial[0]
    b_sp_col = b_spatial[0][:, :, None]
    shard = {"w_in": w_in[0].T, "w_out": w_out[0], "w_gate": w_gate[0].T, "w_up": w_up[0].T, "w_down": w_down[0]}
    shard_m = {"w_in": m_w_in[0].T, "w_out": m_w_out[0], "w_gate": m_w_gate[0].T, "w_up": m_w_up[0].T,
               "w_down": m_w_down[0]}
    shard_v = {"w_in": v_w_in[0].T, "w_out": v_w_out[0], "w_gate": v_w_gate[0].T, "w_up": v_w_up[0].T,
               "w_down": v_w_down[0]}
    transposed = ("w_in", "w_gate", "w_up")
    chip_sum = lambda n, g, s: _chip_sum(my_pos, g, s[0], g.shape[2], "chip_sum_" + n)

    decay_shard = jnp.stack([w_decay_f[0], w_decay_b[0]])
    (hb,), ((g_in, g_decay),) = _norm1(xt, norm1_g, tile(TOKEN_TILE["norm1"]),
                                       [_gather_comm([shard["w_in"], decay_shard], [True, False])])
    w_in_t = g_in.reshape(PROJ_W, D_MODEL)
    wd_pad_f, wd_pad_b = _padded_decay_weights(_unshard_cols(g_decay[:, 0]), _unshard_cols(g_decay[:, 1]))
    (p,), ((g_gate, g_out),) = _in_proj(
        hb, w_in_t, tile(TOKEN_TILE["in_proj"]), [_gather_comm([shard["w_gate"], shard["w_out"]], [True, True])])
    (o_f, st_f, o_b, st_b), ((g_up, g_down),) = _gla_fwd(
        p, wd_pad_f, b_decay_f, wd_pad_b, b_decay_b, tile(TOKEN_TILE["gla"]),
        [_gather_comm([shard["w_up"], shard["w_down"]], [True, True])])
    w_out_full = g_out.reshape(D_MODEL, D_MODEL)
    (x1, ycat), _ = _mix_fwd(xt, o_f, o_b, p, gla_norm_g, ln_g, ln_b, w_sp, b_sp_col, w_out_full,
                             tile(TOKEN_TILE["mix_fwd"]))

    dx1, h2b, dgate, dup, act, dx2, loss_acc, d_gf, d_g2 = _ffn(
        x1, target, norm2_g, final_norm_g[None, :], g_gate.reshape(D_FF, D_MODEL), g_up.reshape(D_FF, D_MODEL),
        g_down.reshape(D_FF, D_MODEL), tile(TOKEN_TILE["ffn"]))
    dw_gate, _ = _matmul_tn(dgate, h2b, D_FF // 2, tile(TOKEN_TILE["dw"]), "grad_w_gate")
    dw_up, _ = _matmul_tn(dup, h2b, D_FF // 2, tile(TOKEN_TILE["dw"]), "grad_w_up")
    dw_down, _ = _matmul_tn(act, dx2, D_FF // 2, tile(TOKEN_TILE["dw"]), "grad_w_down")

    ffn_grads = [_row_blocks(dw_gate), _row_blocks(dw_up), _row_blocks(dw_down)]
    (d_o, dpg, dpu, dpv, dw_out, d_gg, d_lg, d_lb, dw_sp, db_sp), (ffn_sib,) = _mix_bwd(
        dx1, ycat, o_f, o_b, p, gla_norm_g, ln_g, ln_b, w_sp, b_sp_col, w_out_full,
        tile(TOKEN_TILE["mix_bwd"]),
        [_sibling_exchange_comm(ffn_grads)])
    ffn_names = ["w_gate", "w_up", "w_down"]
    ffn_sums = [chip_sum(n, g, [s]) for n, g, s in zip(ffn_names, ffn_grads, ffn_sib)]
    out_grad = _row_blocks(dw_out)
    (dq_f, dk_f, dv_f, dlr_f, dwd_f, dbd_f, dq_b, dk_b, dv_b, dlr_b, dwd_b, dbd_b), (ffn_recv, out_sib) = _gla_bwd(
        p, wd_pad_f, b_decay_f, wd_pad_b, b_decay_b, st_f, st_b, d_o, tile(TOKEN_TILE["gla"]),
        [_chips_exchange_comm([s[1] for s in ffn_sums]), _sibling_exchange_comm([out_grad])])
    out_sum = chip_sum("w_out", out_grad, out_sib)
    (grad_x, dp, d_g1), _ = _in_proj_bwd(
        xt, norm1_g, dx1, dq_f, dq_b, dk_f, dk_b, dv_f, dv_b, dpg, dpu, dpv, dlr_f, dlr_b, w_in_t,
        tile(TOKEN_TILE["in_proj_bwd"]))

    stacks = [_stack_rows([d_g1, d_g2, d_gf]), _stack_rows([d_gg, d_lg, d_lb]),
              _stack_rows([dbd_f, dbd_b, jnp.zeros((DECAY_W_ROW - 2, KEY_W), F32), dwd_f[:LOWRANK],
                           dwd_b[LOWRANK:2 * LOWRANK]]),
              _stack_rows([dw_sp.reshape(GMLP_W, GMLP_CHUNK), db_sp[:, :, 0], loss_acc[:1]])]
    dw_main, (small_sums, out_recv) = _matmul_tn(
        dp, hb, PROJ_PAD // 3, tile(TOKEN_TILE["dw"]), "grad_w_in",
        [_all_reduce_small_comm(stacks), _chips_exchange_comm([out_sum[1]])])
    in_grad = _w_in_grad_blocks(dw_main)
    (in_sib,) = _comm_only([_sibling_exchange_comm([in_grad])], "grad_w_in_exchange_sibling")
    in_sum = chip_sum("w_in", in_grad, in_sib)
    (in_recv,) = _comm_only([_chips_exchange_comm([in_sum[1]])], "grad_w_in_exchange_chips")

    names = ["w_in", "w_out", "w_gate", "w_up", "w_down"]
    sums = [in_sum, out_sum] + ffn_sums
    received = [in_recv[0], out_recv[0]] + list(ffn_recv)
    big_out = {}
    for n, s, rc in zip(names, sums, received):
        res = _adamw_shard(s[0], rc, shard[n], shard_m[n], shard_v[n], shard[n].shape[0], "adamw_" + n)
        big_out[n] = [r.T if n in transposed else r for r in res]

    s1024, s512, s256, s128 = small_sums
    loss = s128[GMLP_W + GMLP_GROUPS, 0]
    col0 = my_id * (KEY_W // N_DEV)
    decay_cols = lambda row0: lax.dynamic_slice(s256, (row0, col0), (LOWRANK, KEY_W // N_DEV))
    flat = lambda a: a.reshape(-1, a.shape[-1])
    small = {
        "norm1_g": ((s1024, 0, 1), norm1_g, m_norm1_g, v_norm1_g),
        "w_decay_f": ((decay_cols(DECAY_W_ROW), 0, LOWRANK), w_decay_f, m_w_decay_f, v_w_decay_f),
        "b_decay_f": ((s256, 0, 1), b_decay_f, m_b_decay_f, v_b_decay_f),
        "w_decay_b": ((decay_cols(DECAY_W_ROW + LOWRANK), 0, LOWRANK), w_decay_b, m_w_decay_b, v_w_decay_b),
        "b_decay_b": ((s256, 1, 1), b_decay_b, m_b_decay_b, v_b_decay_b),
        "gla_norm_g": ((s512, 0, 1), gla_norm_g, m_gla_norm_g, v_gla_norm_g),
        "gmlp_ln_g": ((s512, 1, 1), gmlp_ln_g, m_gmlp_ln_g, v_gmlp_ln_g),
        "gmlp_ln_b": ((s512, 2, 1), gmlp_ln_b, m_gmlp_ln_b, v_gmlp_ln_b),
        "w_spatial": ((s128, 0, GMLP_W), w_spatial, m_w_spatial, v_w_spatial),
        "b_spatial": ((s128, GMLP_W, GMLP_GROUPS), b_spatial, m_b_spatial, v_b_spatial),
        "norm2_g": ((s1024, 1, 1), norm2_g, m_norm2_g, v_norm2_g),
        "final_norm_g": ((s1024, 2, 1), final_norm_g, m_final_norm_g, v_final_norm_g),
    }
    small_res = _adamw_small([(g, flat(w), flat(m), flat(v)) for g, w, m, v in small.values()])
    small_out = {n: [r.reshape(small[n][1].shape) for r in res] for n, res in zip(small, small_res)}

    order = ["norm1_g", "w_in", "w_decay_f", "b_decay_f", "w_decay_b", "b_decay_b", "gla_norm_g", "gmlp_ln_g",
             "gmlp_ln_b", "w_spatial", "b_spatial", "w_out", "norm2_g", "w_gate", "w_up", "w_down", "final_norm_g"]
    outs = []
    for kind in range(4):
        for n in order:
            outs.append(big_out[n][kind][None] if n in big_out else small_out[n][kind])
    return (loss, grad_x[None], *outs)
```

```python
import functools
import math

import jax
import jax.numpy as jnp
from jax import lax
from jax.experimental import pallas as pl
from jax.experimental.pallas import tpu as pltpu

F32 = jnp.float32
BF16 = jnp.bfloat16

D_MODEL = 1024
GLA_HEADS = 4
GLA_DK = 64
GLA_DV = 128
KEY_W = GLA_HEADS * GLA_DK
VAL_W = GLA_HEADS * GLA_DV
LOWRANK = 16
GLA_TAU = 16.0
GLA_CHUNK = 64
GMLP_W = 512
GMLP_GROUPS = 4
GMLP_CHUNK = 128
D_FF = 2816
EPS = 1e-6
Q_SCALE = GLA_DK ** -0.5
PROJ_PAD = 2688
LR_COL = 2560
LANE = 128
N_DEV = 8

ADAM_LR = 0.001
ADAM_B1 = 0.9
ADAM_B2 = 0.999
ADAM_EPS = 1e-08
ADAM_WD = 0.01
ADAM_STEP = 10

VMEM_LIMIT = 56 * 1024 * 1024
TOKEN_TILE = {"norm1": 512, "in_proj": 512, "gla": 1024, "mix_fwd": 1024, "ffn": 256, "mix_bwd": 512,
              "in_proj_bwd": 512, "dw": 2048}
DECAY_W_ROW = 8
MESH_ID = pl.DeviceIdType.MESH
INV_SQRT2 = 0.7071067811865476
INV_SQRT_2PI = 0.3989422804014327


def _params(n_axes=1):
    return pltpu.CompilerParams(dimension_semantics=("arbitrary",) * n_axes, vmem_limit_bytes=VMEM_LIMIT)


def _mm(a, b):
    return jnp.dot(a.astype(BF16), b.astype(BF16), preferred_element_type=F32)


def _mm_nt(a, b):
    return lax.dot_general(a.astype(BF16), b.astype(BF16), (((1,), (1,)), ((), ())), preferred_element_type=F32)


def _mm_tn(a, b):
    return lax.dot_general(a.astype(BF16), b.astype(BF16), (((0,), (0,)), ((), ())), preferred_element_type=F32)


def _const_spec(shape):
    nd = len(shape)
    return pl.BlockSpec(shape, lambda *_: (0,) * nd, pipeline_mode=pl.Buffered(1))


def _acc_spec(shape):
    nd = len(shape)
    return pl.BlockSpec(shape, lambda *_: (0,) * nd)


class _Comm:
    def __init__(self, inputs, in_specs, out_shape, out_specs, scratch_shapes, before, after):
        self.inputs, self.in_specs, self.out_shape, self.out_specs = inputs, in_specs, out_shape, out_specs
        self.scratch_shapes, self.before, self.after = scratch_shapes, before, after


def _fused_call(body, comms, *, name, grid, inputs, in_specs, out_specs, out_shape, scratch_shapes=()):
    n_in, n_out, n_scr = len(in_specs), len(out_specs), len(scratch_shapes)
    nsteps = math.prod(grid)
    sizes = [(len(c.inputs), len(c.out_shape), len(c.scratch_shapes)) for c in comms]

    def full_body(*refs):
        step = pl.program_id(0)
        for axis in range(1, len(grid)):
            step = step * grid[axis] + pl.program_id(axis)
        ins, rest = refs[:n_in], refs[n_in:]
        c_ins = []
        for ci, _, _ in sizes:
            c_ins.append(rest[:ci])
            rest = rest[ci:]
        outs, rest = rest[:n_out], rest[n_out:]
        c_outs = []
        for _, co, _ in sizes:
            c_outs.append(rest[:co])
            rest = rest[co:]
        scr, rest = rest[:n_scr], rest[n_scr:]
        c_scr = []
        for _, _, cs in sizes:
            c_scr.append(rest[:cs])
            rest = rest[cs:]
        for c, a, b, s in zip(comms, c_ins, c_outs, c_scr):
            c.before(step, nsteps, a, b, s)
        body(*ins, *outs, *scr)
        for c, a, b, s in zip(comms, c_ins, c_outs, c_scr):
            c.after(step, nsteps, a, b, s)

    results = pl.pallas_call(
        full_body, name=name, grid=grid,
        in_specs=list(in_specs) + [s for c in comms for s in c.in_specs],
        out_specs=tuple(out_specs) + tuple(s for c in comms for s in c.out_specs),
        out_shape=tuple(out_shape) + tuple(s for c in comms for s in c.out_shape),
        scratch_shapes=list(scratch_shapes) + [s for c in comms for s in c.scratch_shapes],
        compiler_params=_params(len(grid)),
    )(*inputs, *[a for c in comms for a in c.inputs])
    own, rest = results[:n_out], results[n_out:]
    comm_results = []
    for _, co, _ in sizes:
        comm_results.append(rest[:co])
        rest = rest[co:]
    return own, comm_results


def _gelu(x):
    return 0.5 * x * (1.0 + lax.erf(x * INV_SQRT2))


def _gelu_and_grad(x):
    cdf = 0.5 * (1.0 + lax.erf(x * INV_SQRT2))
    return x * cdf, cdf + x * jnp.exp(-0.5 * x * x) * INV_SQRT_2PI


def _sigmoid(x):
    return 0.5 + 0.5 * jnp.tanh(0.5 * x)


def _silu_and_grad(x):
    s = _sigmoid(x)
    return x * s, s * (1.0 + x * (1.0 - s))


def _norm1(x, g1, tm, comms=()):
    t = x.shape[0]

    def body(x_ref, g_ref, h_ref):
        xv = x_ref[...]
        r = lax.rsqrt(jnp.mean(xv * xv, axis=-1, keepdims=True) + EPS)
        h_ref[...] = (xv * r * g_ref[...]).astype(BF16)

    row = pl.BlockSpec((tm, D_MODEL), lambda i: (i, 0))
    return _fused_call(body, comms, name="norm1", grid=(t // tm,), inputs=(x, g1),
                       in_specs=[row, _const_spec((1, D_MODEL))], out_specs=(row,),
                       out_shape=(jax.ShapeDtypeStruct((t, D_MODEL), BF16),))


PROJ_W = 2592
LR_REF = 1536
PROJ_ROWS = ((0, LR_REF, 0), (LR_REF + 2 * LOWRANK, PROJ_W, LR_REF), (LR_REF, LR_REF + LANE, LR_COL))


def _in_proj(h, w_in_t, tm, comms=()):
    t = h.shape[0]

    def body(h_ref, w_ref, p_ref):
        hv = h_ref[...]
        for r0, r1, c0 in PROJ_ROWS:
            p_ref[:, c0:c0 + r1 - r0] = _mm_nt(hv, w_ref[r0:r1, :]).astype(BF16)

    return _fused_call(
        body, comms, name="in_proj", grid=(t // tm,), inputs=(h, w_in_t),
        in_specs=[pl.BlockSpec((tm, D_MODEL), lambda i: (i, 0)), _const_spec((PROJ_W, D_MODEL))],
        out_specs=(pl.BlockSpec((tm, PROJ_PAD), lambda i: (i, 0)),),
        out_shape=(jax.ShapeDtypeStruct((t, PROJ_PAD), BF16),))


def _tri(upper):
    r = lax.broadcasted_iota(jnp.int32, (GLA_CHUNK, GLA_CHUNK), 0)
    c = lax.broadcasted_iota(jnp.int32, (GLA_CHUNK, GLA_CHUNK), 1)
    return jnp.where((c >= r) if upper else (c <= r), 1.0, 0.0).astype(BF16)


def _chunk_cumsum(tri, a):
    hi = a.astype(BF16)
    lo = (a - hi.astype(F32)).astype(BF16)
    dot = functools.partial(jnp.dot, preferred_element_type=F32)
    return jnp.concatenate([dot(tri, hi[_chunk_rows(c)]) + dot(tri, lo[_chunk_rows(c)])
                            for c in range(a.shape[0] // GLA_CHUNK)], axis=0)


def _chunk_rows(c):
    return slice(c * GLA_CHUNK, (c + 1) * GLA_CHUNK)


def _gla_masks(rev):
    dk_bits, dv_bits = GLA_DK.bit_length() - 1, GLA_DV.bit_length() - 1
    key_head = lax.broadcasted_iota(jnp.int32, (GLA_CHUNK, KEY_W), 1) >> dk_bits
    val_head = lax.broadcasted_iota(jnp.int32, (GLA_CHUNK, VAL_W), 1) >> dv_bits
    t = lax.broadcasted_iota(jnp.int32, (GLA_HEADS * GLA_CHUNK, GLA_CHUNK), 0) & (GLA_CHUNK - 1)
    s = lax.broadcasted_iota(jnp.int32, (GLA_HEADS * GLA_CHUNK, GLA_CHUNK), 1)
    return key_head, val_head, (s >= t) if rev else (s <= t)


def _stack_heads(a, head_of_lane):
    a = a.astype(BF16)
    return jnp.concatenate([jnp.where(head_of_lane == h, a, jnp.zeros_like(a)) for h in range(GLA_HEADS)], axis=0)


def _rows_by_head(a):
    return jnp.concatenate([a[:, h * GLA_DV:(h + 1) * GLA_DV] for h in range(GLA_HEADS)], axis=0)


def _lanes_by_head(r):
    return jnp.concatenate([r[h * GLA_CHUNK:(h + 1) * GLA_CHUNK] for h in range(GLA_HEADS)], axis=1)


def _head_diagonal(r, head_of_lane):
    rows = r.shape[0] // GLA_HEADS
    out = jnp.where(head_of_lane == 0, r[:rows], 0.0)
    for h in range(1, GLA_HEADS):
        out = out + jnp.where(head_of_lane == h, r[h * rows:(h + 1) * rows], 0.0)
    return out


def _tile_terms(la, q, k, tri, rev):
    nc = la.shape[0] // GLA_CHUNK
    q, k = q.astype(F32), k.astype(F32)
    b = _chunk_cumsum(tri, la)
    ebl = [jnp.exp(b[c * GLA_CHUNK:c * GLA_CHUNK + 1] if rev else b[(c + 1) * GLA_CHUNK - 1:(c + 1) * GLA_CHUNK])
           for c in range(nc)]
    eb = jnp.exp(b)
    enb = jnp.exp(-b)
    ee = enb * jnp.concatenate([jnp.broadcast_to(row, (GLA_CHUNK, KEY_W)) for row in ebl], axis=0)
    return ebl, eb, enb, ee, q * Q_SCALE * eb, k * enb, k * ee


def _log_decay(lr_ref, wd_ref, bd_ref):
    z = _mm(lr_ref[...], wd_ref[...]) + bd_ref[...]
    return z, jax.nn.log_sigmoid(z) * (1.0 / GLA_TAU)


def _p_specs(tg, tile):
    return [pl.BlockSpec((tg, KEY_W), lambda i: (tile(i), 0)),
            pl.BlockSpec((tg, KEY_W), lambda i: (tile(i), 1)),
            pl.BlockSpec((tg, VAL_W), lambda i: (tile(i), 1)),
            pl.BlockSpec((tg, LANE), lambda i: (tile(i), LR_COL // LANE))]


def _gla_fwd_dir(rev, nc, q_ref, k_ref, v_ref, lr_ref, wd_ref, bd_ref, o_ref, st_ref, state):
    key_head, _, causal = _gla_masks(rev)
    order = range(nc - 1, -1, -1) if rev else range(nc)

    def intra():
        _, la = _log_decay(lr_ref, wd_ref, bd_ref)
        ebl, _, _, _, qd, kd, ke = _tile_terms(la, q_ref[...], k_ref[...], _tri(rev), rev)
        kd = kd.astype(BF16)
        v = {c: v_ref[_chunk_rows(c), :].astype(BF16) for c in order}
        qd_stack = {c: _stack_heads(qd[_chunk_rows(c)], key_head) for c in order}
        ke_stack = {c: _stack_heads(ke[_chunk_rows(c)], key_head) for c in order}
        a_all = {c: _mm_nt(qd_stack[c], kd[_chunk_rows(c)]) for c in order}
        a_all = {c: jnp.where(causal, a_all[c], 0.0).astype(BF16) for c in order}
        head_rows = lambda a, h: a[h * GLA_CHUNK:(h + 1) * GLA_CHUNK]
        head_vals = lambda a, h: a[:, h * GLA_DV:(h + 1) * GLA_DV]
        r = {c: [_mm(head_rows(a_all[c], h), head_vals(v[c], h)) for h in range(GLA_HEADS)] for c in order}
        upd = {c: _mm_tn(_rows_by_head(v[c]), ke_stack[c]) for c in order}
        return {c: (ebl[c], qd_stack[c], r[c], upd[c]) for c in order}

    def scan(terms):
        st = state[...]
        states = {}
        for c in order:
            states[c] = st
            st_ref[c] = st.astype(BF16)
            st = st * terms[c][0] + terms[c][3]
        state[...] = st
        return states

    def inter(terms, states):
        r_inter = {c: _mm_nt(terms[c][1], states[c]) for c in order}
        for c in order:
            o_ref[_chunk_rows(c), :] = jnp.concatenate(
                [terms[c][2][h] + r_inter[c][h * GLA_CHUNK:(h + 1) * GLA_CHUNK] for h in range(GLA_HEADS)], axis=1)

    return intra, scan, inter


def _gla_fwd(p, wd_pad_f, bd_f, wd_pad_b, bd_b, tg, comms=()):
    t = p.shape[0]
    nt = t // tg
    nc = tg // GLA_CHUNK
    up, down = (lambda i: i), (lambda i: nt - 1 - i)

    def body(qf, kf, vf, lrf, qb, kb, vb, lrb, wdf, bdf, wdb, bdb, of, stf, ob, stb, state_f, state_b):
        @pl.when(pl.program_id(0) == 0)
        def _():
            state_f[...] = jnp.zeros_like(state_f)
            state_b[...] = jnp.zeros_like(state_b)

        dirs = [_gla_fwd_dir(False, nc, qf, kf, vf, lrf, wdf, bdf, of, stf, state_f),
                _gla_fwd_dir(True, nc, qb, kb, vb, lrb, wdb, bdb, ob, stb, state_b)]
        terms = [intra() for intra, _, _ in dirs]
        states = [scan(t) for (_, scan, _), t in zip(dirs, terms)]
        for (_, _, inter), t, s in zip(dirs, terms, states):
            inter(t, s)

    wd_spec, bd_spec = _const_spec((LANE, KEY_W)), _const_spec((1, KEY_W))
    outs = lambda tile: (pl.BlockSpec((tg, VAL_W), lambda i: (tile(i), 0)),
                         pl.BlockSpec((nc, GLA_DV, KEY_W), lambda i: (tile(i), 0, 0)))
    out_shape = (jax.ShapeDtypeStruct((t, VAL_W), F32), jax.ShapeDtypeStruct((t // GLA_CHUNK, GLA_DV, KEY_W), BF16))
    return _fused_call(
        body, comms, name="gla_fwd", grid=(nt,), inputs=(p,) * 8 + (wd_pad_f, bd_f, wd_pad_b, bd_b),
        in_specs=_p_specs(tg, up) + _p_specs(tg, down) + [wd_spec, bd_spec, wd_spec, bd_spec],
        out_specs=outs(up) + outs(down), out_shape=out_shape * 2,
        scratch_shapes=[pltpu.VMEM((GLA_DV, KEY_W), F32)] * 2)


def _gla_bwd_dir(rev, nc, q_ref, k_ref, v_ref, lr_ref, wd_ref, bd_ref, st_ref, do_ref,
                 dq_ref, dk_ref, dv_ref, dlr_ref, dwd_ref, dbd_ref, dstate):
    key_head, val_head, causal = _gla_masks(rev)
    order = range(nc) if rev else range(nc - 1, -1, -1)

    def intra():
        z, la = _log_decay(lr_ref, wd_ref, bd_ref)
        tile = _tile_terms(la, q_ref[...], k_ref[...], _tri(rev), rev)
        qd, kd = tile[4], tile[5].astype(BF16)
        v = {c: v_ref[_chunk_rows(c), :].astype(BF16) for c in order}
        d_o = {c: do_ref[_chunk_rows(c), :] for c in order}
        kd_c = {c: kd[_chunk_rows(c)] for c in order}
        qd_stack = {c: _stack_heads(qd[_chunk_rows(c)], key_head) for c in order}
        do_stack = {c: _stack_heads(d_o[c], val_head) for c in order}
        do_rows = {c: _rows_by_head(d_o[c]) for c in order}
        a_all = {c: _mm_nt(qd_stack[c], kd_c[c]) for c in order}
        head_vals = lambda a, h: a[:, h * GLA_DV:(h + 1) * GLA_DV]
        da_all = {c: jnp.concatenate([_mm_nt(head_vals(d_o[c], h), head_vals(v[c], h)) for h in range(GLA_HEADS)],
                                     axis=0) for c in order}
        a_all = {c: jnp.where(causal, a_all[c], 0.0).astype(BF16) for c in order}
        da_all = {c: jnp.where(causal, da_all[c], 0.0).astype(BF16) for c in order}
        dv = {c: _mm_tn(a_all[c], do_stack[c]) for c in order}
        dqd = {c: _mm(jnp.concatenate([do_rows[c], da_all[c]], axis=1),
                      jnp.concatenate([st_ref[c], kd_c[c]], axis=0)) for c in order}
        dkd = {c: _mm_tn(da_all[c], qd_stack[c]) for c in order}
        upd = {c: _mm_tn(do_rows[c], qd_stack[c]) for c in order}
        dqd = {c: _head_diagonal(dqd[c], key_head) for c in order}
        return z, tile, {c: dict(dv=dv[c], dqd=dqd[c], dkd=dkd[c], upd=upd[c]) for c in order}

    def scan(tile, per):
        dst = dstate[...]
        dsts = {}
        for c in order:
            dsts[c] = dst
            dst = dst * tile[0][c] + per[c]["upd"]
        dstate[...] = dst
        return dsts

    def inter(z, tile, per, dsts):
        ebl, eb, enb, ee, qd, kd, ke = tile
        ke_stack = {c: _stack_heads(ke[_chunk_rows(c)], key_head) for c in order}
        v_rows = {c: _rows_by_head(v_ref[_chunk_rows(c), :].astype(BF16)) for c in order}
        dst_b = {c: dsts[c].astype(BF16) for c in order}
        dv_state = {c: _mm_nt(ke_stack[c], dst_b[c]) for c in order}
        dke_c = {c: _mm(v_rows[c], dst_b[c]) for c in order}
        dke_c = {c: _head_diagonal(dke_c[c], key_head) for c in order}
        dbl_c = {}
        for c in order:
            rows = _chunk_rows(c)
            dv_ref[rows, :] = (per[c]["dv"] + _lanes_by_head(dv_state[c])).astype(BF16)
            dbl = (jnp.sum(dsts[c] * st_ref[c].astype(F32), axis=0, keepdims=True) * ebl[c]
                   + jnp.sum(dke_c[c] * ke[rows], axis=0, keepdims=True))
            dbl_c[c] = jnp.broadcast_to(dbl, (GLA_CHUNK, KEY_W))
        tile_of = lambda parts: jnp.concatenate([parts[c] for c in range(nc)], axis=0)
        dqd, dkd = tile_of({c: per[c]["dqd"] for c in order}), tile_of({c: per[c]["dkd"] for c in order})
        dke, dbl = tile_of(dke_c), tile_of(dbl_c)
        dq_ref[...] = (dqd * eb * Q_SCALE).astype(BF16)
        dk_ref[...] = (dkd * enb + dke * ee).astype(BF16)
        db = dqd * qd - dkd * kd - dke * ke
        dz = (_chunk_cumsum(_tri(not rev), db) + dbl) * (_sigmoid(-z) * (1.0 / GLA_TAU))
        dlr_ref[...] = _mm_nt(dz, wd_ref[...]).astype(BF16)
        dwd_ref[...] += _mm_tn(lr_ref[...], dz)
        dbd_ref[...] += jnp.sum(dz, axis=0, keepdims=True)

    return intra, scan, inter


def _gla_bwd(p, wd_pad_f, bd_f, wd_pad_b, bd_b, st_f, st_b, d_o, tg, comms=()):
    t = p.shape[0]
    nt = t // tg
    nc = tg // GLA_CHUNK
    up, down = (lambda i: i), (lambda i: nt - 1 - i)

    def body(qf, kf, vf, lrf, stf, dof, qb, kb, vb, lrb, stb, dob, wdf, bdf, wdb, bdb,
             dqf, dkf, dvf, dlrf, dwdf, dbdf, dqb, dkb, dvb, dlrb, dwdb, dbdb, dstate_f, dstate_b):
        @pl.when(pl.program_id(0) == 0)
        def _():
            for ref in (dstate_f, dstate_b, dwdf, dbdf, dwdb, dbdb):
                ref[...] = jnp.zeros_like(ref)

        dirs = [_gla_bwd_dir(False, nc, qf, kf, vf, lrf, wdf, bdf, stf, dof, dqf, dkf, dvf, dlrf, dwdf, dbdf,
                             dstate_f),
                _gla_bwd_dir(True, nc, qb, kb, vb, lrb, wdb, bdb, stb, dob, dqb, dkb, dvb, dlrb, dwdb, dbdb,
                             dstate_b)]
        first = [intra() for intra, _, _ in dirs]
        dsts = [scan(tile, per) for (_, scan, _), (_, tile, per) in zip(dirs, first)]
        for (_, _, inter), (z, tile, per), d in zip(dirs, first, dsts):
            inter(z, tile, per, d)

    wd_spec, bd_spec = _const_spec((LANE, KEY_W)), _const_spec((1, KEY_W))
    ins = lambda tile: _p_specs(tg, tile) + [pl.BlockSpec((nc, GLA_DV, KEY_W), lambda i: (tile(i), 0, 0)),
                                             pl.BlockSpec((tg, VAL_W), lambda i: (tile(i), 0))]
    outs = lambda tile: (pl.BlockSpec((tg, KEY_W), lambda i: (tile(i), 0)),
                         pl.BlockSpec((tg, KEY_W), lambda i: (tile(i), 0)),
                         pl.BlockSpec((tg, VAL_W), lambda i: (tile(i), 0)),
                         pl.BlockSpec((tg, LANE), lambda i: (tile(i), 0)),
                         _acc_spec((LANE, KEY_W)), _acc_spec((1, KEY_W)))
    out_shape = (jax.ShapeDtypeStruct((t, KEY_W), BF16), jax.ShapeDtypeStruct((t, KEY_W), BF16),
                 jax.ShapeDtypeStruct((t, VAL_W), BF16), jax.ShapeDtypeStruct((t, LANE), BF16),
                 jax.ShapeDtypeStruct((LANE, KEY_W), F32), jax.ShapeDtypeStruct((1, KEY_W), F32))
    scratch = [pltpu.VMEM((GLA_DV, KEY_W), F32)]
    return _fused_call(
        body, comms, name="gla_bwd", grid=(nt,),
        inputs=(p, p, p, p, st_f, d_o, p, p, p, p, st_b, d_o, wd_pad_f, bd_f, wd_pad_b, bd_b),
        in_specs=ins(down) + ins(up) + [wd_spec, bd_spec, wd_spec, bd_spec],
        out_specs=outs(down) + outs(up), out_shape=out_shape * 2, scratch_shapes=scratch * 2)


def _head_rms(o):
    parts, scales = [], []
    for h in range(GLA_HEADS):
        oh = o[:, h * GLA_DV:(h + 1) * GLA_DV]
        r = lax.rsqrt(jnp.mean(oh * oh, axis=-1, keepdims=True) + EPS)
        parts.append(oh * r)
        scales.append(jnp.broadcast_to(r, oh.shape))
    return jnp.concatenate(parts, axis=1), jnp.concatenate(scales, axis=1)


def _layernorm_stats(zv):
    mu = jnp.mean(zv, axis=-1, keepdims=True)
    xc = zv - mu
    rs = lax.rsqrt(jnp.mean(xc * xc, axis=-1, keepdims=True) + EPS)
    return xc * rs, rs


def _mix_fwd(x, o_f, o_b, p, gla_g, ln_g, ln_b, w_sp, b_sp, w_out, tm, comms=()):
    t = x.shape[0]
    nch = tm // GMLP_CHUNK

    def body(x_ref, of_ref, ob_ref, pg_ref, pu_ref, pv_ref, gg_ref, lg_ref, lb_ref, ws_ref, bs_ref, wo_ref,
             x1_ref, y_ref, s_scr):
        on, _ = _head_rms(of_ref[...] + ob_ref[...])
        pg = pg_ref[...].astype(F32)
        y_a = on * gg_ref[...] * (pg * _sigmoid(pg))
        zu = _gelu(pu_ref[...].astype(F32))
        vhat, _ = _layernorm_stats(_gelu(pv_ref[...].astype(F32)))
        vln = (vhat * lg_ref[...] + lb_ref[...]).astype(BF16)
        for g in range(GMLP_GROUPS):
            w_g = ws_ref[g].astype(BF16)
            b_g = bs_ref[g]
            cols = slice(g * LANE, (g + 1) * LANE)
            for n in range(nch):
                rows = slice(n * GMLP_CHUNK, (n + 1) * GMLP_CHUNK)
                s_scr[rows, cols] = jnp.dot(w_g, vln[rows, cols], preferred_element_type=F32) + b_g
        ycat = jnp.concatenate([y_a, zu * s_scr[...]], axis=1).astype(BF16)
        y_ref[...] = ycat
        x1_ref[...] = x_ref[...] + jnp.dot(ycat, wo_ref[...], preferred_element_type=F32)

    half = lambda j: pl.BlockSpec((tm, VAL_W), lambda i: (i, j))
    return _fused_call(
        body, comms, name="mix_fwd", grid=(t // tm,),
        inputs=(x, o_f, o_b, p, p, p, gla_g, ln_g, ln_b, w_sp, b_sp, w_out),
        in_specs=[pl.BlockSpec((tm, D_MODEL), lambda i: (i, 0)), half(0), half(0), half(2), half(3), half(4),
                  _const_spec((1, VAL_W)), _const_spec((1, GMLP_W)), _const_spec((1, GMLP_W)),
                  _const_spec((GMLP_GROUPS, GMLP_CHUNK, GMLP_CHUNK)), _const_spec((GMLP_GROUPS, GMLP_CHUNK, 1)),
                  _const_spec((D_MODEL, D_MODEL))],
        out_specs=(pl.BlockSpec((tm, D_MODEL), lambda i: (i, 0)), pl.BlockSpec((tm, D_MODEL), lambda i: (i, 0))),
        out_shape=(jax.ShapeDtypeStruct((t, D_MODEL), F32), jax.ShapeDtypeStruct((t, D_MODEL), BF16)),
        scratch_shapes=[pltpu.VMEM((tm, GMLP_W), F32)])


def _mix_bwd(dx1, ycat, o_f, o_b, p, gla_g, ln_g, ln_b, w_sp, b_sp, w_out, tm, comms=()):
    t = dx1.shape[0]
    nch = tm // GMLP_CHUNK

    def body(dx1_ref, y_ref, of_ref, ob_ref, pg_ref, pu_ref, pv_ref, gg_ref, lg_ref, lb_ref, ws_ref, bs_ref, wo_ref,
             do_ref, dpg_ref, dpu_ref, dpv_ref, dwo_ref, dgg_ref, dlg_ref, dlb_ref, dws_ref, dbs_ref,
             s_scr, dvln_scr):
        @pl.when(pl.program_id(0) == 0)
        def _():
            for ref in (dwo_ref, dgg_ref, dlg_ref, dlb_ref, dws_ref, dbs_ref):
                ref[...] = jnp.zeros_like(ref)

        dx1 = dx1_ref[...].astype(BF16)
        dycat = _mm_nt(dx1, wo_ref[...])
        dwo_ref[...] += _mm_tn(y_ref[...], dx1)
        dy_a = dycat[:, :VAL_W]
        dy_b = dycat[:, VAL_W:]
        on, r = _head_rms(of_ref[...] + ob_ref[...])
        pg = pg_ref[...].astype(F32)
        sil, dsil = _silu_and_grad(pg)
        gg = gg_ref[...]
        dgg_ref[...] += jnp.sum(dy_a * sil * on, axis=0, keepdims=True)
        don = dy_a * sil * gg
        prod = don * on
        means = jnp.concatenate(
            [jnp.broadcast_to(jnp.mean(prod[:, h * GLA_DV:(h + 1) * GLA_DV], axis=-1, keepdims=True),
                              (tm, GLA_DV)) for h in range(GLA_HEADS)], axis=1)
        do_ref[...] = (r * (don - on * means)).astype(BF16)
        dpg_ref[...] = (dy_a * on * gg * dsil).astype(BF16)
        pu = pu_ref[...].astype(F32)
        pv = pv_ref[...].astype(F32)
        zu, dzu_dpu = _gelu_and_grad(pu)
        zv, dzv_dpv = _gelu_and_grad(pv)
        vhat, rs = _layernorm_stats(zv)
        lg = lg_ref[...]
        vln = (vhat * lg + lb_ref[...]).astype(BF16)
        ds32 = dy_b * zu
        ds = ds32.astype(BF16)
        blocks = [(g, n) for g in range(GMLP_GROUPS) for n in range(nch)]
        at = lambda g, n: (slice(n * GMLP_CHUNK, (n + 1) * GMLP_CHUNK), slice(g * LANE, (g + 1) * LANE))
        w_sp = [ws_ref[g].astype(BF16) for g in range(GMLP_GROUPS)]
        v_blk = {b: vln[at(*b)] for b in blocks}
        ds_blk = {b: ds[at(*b)] for b in blocks}
        s_blk = {b: jnp.dot(w_sp[b[0]], v_blk[b], preferred_element_type=F32) for b in blocks}
        dw_blk = {b: _mm_nt(ds_blk[b], v_blk[b]) for b in blocks}
        dvln_blk = {b: _mm_tn(w_sp[b[0]], ds_blk[b]) for b in blocks}
        for b in blocks:
            s_scr[at(*b)] = s_blk[b] + bs_ref[b[0]]
            dvln_scr[at(*b)] = dvln_blk[b]
        for g in range(GMLP_GROUPS):
            dws_ref[g] += sum(dw_blk[(g, n)] for n in range(nch))
            dbs_ref[g] += sum(jnp.sum(ds32[at(g, n)], axis=-1, keepdims=True) for n in range(nch))
        dpu_ref[...] = (dy_b * s_scr[...] * dzu_dpu).astype(BF16)
        dvln = dvln_scr[...]
        dlg_ref[...] += jnp.sum(dvln * vhat, axis=0, keepdims=True)
        dlb_ref[...] += jnp.sum(dvln, axis=0, keepdims=True)
        dvhat = dvln * lg
        dzv = rs * (dvhat - jnp.mean(dvhat, axis=-1, keepdims=True)
                    - vhat * jnp.mean(dvhat * vhat, axis=-1, keepdims=True))
        dpv_ref[...] = (dzv * dzv_dpv).astype(BF16)

    half = lambda j: pl.BlockSpec((tm, VAL_W), lambda i: (i, j))
    full = pl.BlockSpec((tm, D_MODEL), lambda i: (i, 0))
    sp_shape = (GMLP_GROUPS, GMLP_CHUNK, GMLP_CHUNK)
    bs_shape = (GMLP_GROUPS, GMLP_CHUNK, 1)
    return _fused_call(
        body, comms, name="mix_bwd", grid=(t // tm,),
        inputs=(dx1, ycat, o_f, o_b, p, p, p, gla_g, ln_g, ln_b, w_sp, b_sp, w_out),
        in_specs=[full, full, half(0), half(0), half(2), half(3), half(4),
                  _const_spec((1, VAL_W)), _const_spec((1, GMLP_W)), _const_spec((1, GMLP_W)),
                  _const_spec(sp_shape), _const_spec(bs_shape), _const_spec((D_MODEL, D_MODEL))],
        out_specs=(half(0), half(0), half(0), half(0), _acc_spec((D_MODEL, D_MODEL)), _acc_spec((1, VAL_W)),
                   _acc_spec((1, GMLP_W)), _acc_spec((1, GMLP_W)), _acc_spec(sp_shape), _acc_spec(bs_shape)),
        out_shape=(jax.ShapeDtypeStruct((t, VAL_W), BF16),) * 4 + (
            jax.ShapeDtypeStruct((D_MODEL, D_MODEL), F32), jax.ShapeDtypeStruct((1, VAL_W), F32),
            jax.ShapeDtypeStruct((1, GMLP_W), F32), jax.ShapeDtypeStruct((1, GMLP_W), F32),
            jax.ShapeDtypeStruct(sp_shape, F32), jax.ShapeDtypeStruct(bs_shape, F32)),
        scratch_shapes=[pltpu.VMEM((tm, GMLP_W), F32), pltpu.VMEM((tm, GMLP_W), F32)])


def _rms_bwd(dy_scaled, xn, r):
    return r * (dy_scaled - xn * jnp.mean(dy_scaled * xn, axis=-1, keepdims=True))


def _ffn(x1, target, g2, gf, w_gate, w_up, w_down, tm):
    t = x1.shape[0]

    def body(x1_ref, tg_ref, g2_ref, gf_ref, wg_ref, wu_ref, wd_ref,
             dx1_ref, h2_ref, dgate_ref, dup_ref, act_ref, dx2_ref, loss_ref, dgf_ref, dg2_ref):
        @pl.when(pl.program_id(0) == 0)
        def _():
            for ref in (loss_ref, dgf_ref, dg2_ref):
                ref[...] = jnp.zeros_like(ref)

        x1v = x1_ref[...]
        g2v = g2_ref[...]
        gfv = gf_ref[...]
        r2 = lax.rsqrt(jnp.mean(x1v * x1v, axis=-1, keepdims=True) + EPS)
        xn1 = x1v * r2
        h2 = (xn1 * g2v).astype(BF16)
        h2_ref[...] = h2
        gate = _mm_nt(h2, wg_ref[...])
        up = _mm_nt(h2, wu_ref[...])
        sil, dsil = _silu_and_grad(gate)
        act = (sil * up).astype(BF16)
        act_ref[...] = act
        x2 = x1v + jnp.dot(act, wd_ref[...], preferred_element_type=F32)
        rf = lax.rsqrt(jnp.mean(x2 * x2, axis=-1, keepdims=True) + EPS)
        xn2 = x2 * rf
        err = xn2 * gfv - tg_ref[...]
        loss_ref[...] += 0.5 * jnp.sum(jnp.mean(err * err, axis=-1, keepdims=True))
        dy = err * (1.0 / D_MODEL)
        dgf_ref[...] += jnp.sum(dy * xn2, axis=0, keepdims=True)
        dx2 = _rms_bwd(dy * gfv, xn2, rf)
        dx2b = dx2.astype(BF16)
        dx2_ref[...] = dx2b
        dact = _mm_nt(dx2b, wd_ref[...])
        dgate = (dact * up * dsil).astype(BF16)
        dup = (dact * sil).astype(BF16)
        dgate_ref[...] = dgate
        dup_ref[...] = dup
        dh2 = _mm(dgate, wg_ref[...]) + _mm(dup, wu_ref[...])
        dg2_ref[...] += jnp.sum(dh2 * xn1, axis=0, keepdims=True)
        dx1_ref[...] = dx2 + _rms_bwd(dh2 * g2v, xn1, r2)

    row = lambda w: pl.BlockSpec((tm, w), lambda i: (i, 0))
    return pl.pallas_call(
        body, name="ffn_fwd_bwd", grid=(t // tm,),
        in_specs=[row(D_MODEL), row(D_MODEL), _const_spec((1, D_MODEL)), _const_spec((1, D_MODEL)),
                  _const_spec((D_FF, D_MODEL)), _const_spec((D_FF, D_MODEL)), _const_spec((D_FF, D_MODEL))],
        out_specs=(row(D_MODEL), row(D_MODEL), row(D_FF), row(D_FF), row(D_FF), row(D_MODEL),
                   _acc_spec((8, LANE)), _acc_spec((1, D_MODEL)), _acc_spec((1, D_MODEL))),
        out_shape=(jax.ShapeDtypeStruct((t, D_MODEL), F32), jax.ShapeDtypeStruct((t, D_MODEL), BF16),
                   jax.ShapeDtypeStruct((t, D_FF), BF16), jax.ShapeDtypeStruct((t, D_FF), BF16),
                   jax.ShapeDtypeStruct((t, D_FF), BF16), jax.ShapeDtypeStruct((t, D_MODEL), BF16),
                   jax.ShapeDtypeStruct((8, LANE), F32), jax.ShapeDtypeStruct((1, D_MODEL), F32),
                   jax.ShapeDtypeStruct((1, D_MODEL), F32)),
        compiler_params=_params(),
    )(x1, target, g2, gf, w_gate, w_up, w_down)


def _matmul_tn(a, b, tm, tk, name, comms=()):
    t, m = a.shape
    n = b.shape[1]

    def body(a_ref, b_ref, o_ref):
        @pl.when(pl.program_id(1) == 0)
        def _():
            o_ref[...] = jnp.zeros_like(o_ref)

        o_ref[...] += _mm_tn(a_ref[...], b_ref[...])

    (out,), comm_results = _fused_call(
        body, comms, name=name, grid=(m // tm, t // tk), inputs=(a, b),
        in_specs=[pl.BlockSpec((tk, tm), lambda j, k: (k, j)), pl.BlockSpec((tk, n), lambda j, k: (k, 0))],
        out_specs=(pl.BlockSpec((tm, n), lambda j, k: (j, 0)),),
        out_shape=(jax.ShapeDtypeStruct((m, n), F32),))
    return out, comm_results


def _in_proj_bwd(x, g1, dx1, dq_f, dq_b, dk_f, dk_b, dv_f, dv_b, dpg, dpu, dpv, dlr_f, dlr_b, w_main, tm, comms=()):
    t = x.shape[0]

    def body(x_ref, g_ref, dx1_ref, dqf, dqb, dkf, dkb, dvf, dvb, dg, du, dv, dlf, dlb, w_ref,
             dx_ref, dp_ref, dg1_ref):
        @pl.when(pl.program_id(0) == 0)
        def _():
            dg1_ref[...] = jnp.zeros_like(dg1_ref)

        both = lambda a, b: (a[...].astype(F32) + b[...].astype(F32)).astype(BF16)
        dp = jnp.concatenate([both(dqf, dqb), both(dkf, dkb), both(dvf, dvb), dg[...], du[...], dv[...],
                              both(dlf, dlb)], axis=1)
        dp_ref[...] = dp
        dh = sum(_mm(dp[:, c0:c0 + r1 - r0], w_ref[r0:r1, :]) for r0, r1, c0 in PROJ_ROWS)
        xv = x_ref[...]
        r = lax.rsqrt(jnp.mean(xv * xv, axis=-1, keepdims=True) + EPS)
        xn = xv * r
        dg1_ref[...] += jnp.sum(dh * xn, axis=0, keepdims=True)
        dx_ref[...] = dx1_ref[...] + _rms_bwd(dh * g_ref[...], xn, r)

    row = lambda w: pl.BlockSpec((tm, w), lambda i: (i, 0))
    return _fused_call(
        body, comms, name="in_proj_bwd", grid=(t // tm,),
        inputs=(x, g1, dx1, dq_f, dq_b, dk_f, dk_b, dv_f, dv_b, dpg, dpu, dpv, dlr_f, dlr_b, w_main),
        in_specs=[row(D_MODEL), _const_spec((1, D_MODEL)), row(D_MODEL), row(KEY_W), row(KEY_W), row(KEY_W),
                  row(KEY_W), row(VAL_W), row(VAL_W), row(VAL_W), row(VAL_W), row(VAL_W), row(LANE), row(LANE),
                  _const_spec((PROJ_W, D_MODEL))],
        out_specs=(row(D_MODEL), row(PROJ_PAD), _acc_spec((1, D_MODEL))),
        out_shape=(jax.ShapeDtypeStruct((t, D_MODEL), F32), jax.ShapeDtypeStruct((t, PROJ_PAD), BF16),
                   jax.ShapeDtypeStruct((1, D_MODEL), F32)))


def _adamw(w, g, m, v):
    m_new = ADAM_B1 * m + (1.0 - ADAM_B1) * g
    v_new = ADAM_B2 * v + (1.0 - ADAM_B2) * (g * g)
    m_hat = m_new / (1.0 - ADAM_B1 ** ADAM_STEP)
    v_hat = v_new / (1.0 - ADAM_B2 ** ADAM_STEP)
    delta = -ADAM_LR * (m_hat / (jnp.sqrt(v_hat) + ADAM_EPS) + ADAM_WD * w)
    return delta, m_new, v_new


def _adamw_shard(own, recv, w, m, v, tr, name):
    r, c = w.shape

    def body(own_ref, recv_ref, w_ref, m_ref, v_ref, g_ref, d_ref, nm_ref, nv_ref):
        g = own_ref[...]
        for k in range(3):
            g = g + recv_ref[k].astype(F32)
        g_ref[...] = g
        d_ref[...], nm_ref[...], nv_ref[...] = _adamw(w_ref[...], g, m_ref[...], v_ref[...])

    row = pl.BlockSpec((tr, c), lambda i: (i, 0))
    return pl.pallas_call(
        body, name=name, grid=(r // tr,),
        in_specs=[row, pl.BlockSpec((3, tr, c), lambda i: (0, i, 0)), row, row, row],
        out_specs=(row,) * 4, out_shape=(jax.ShapeDtypeStruct((r, c), F32),) * 4,
        compiler_params=_params(),
    )(own, recv, w, m, v)


def _adamw_small(entries):
    stacks = []
    for (g, _, _), _, _, _ in entries:
        if not any(g is s for s in stacks):
            stacks.append(g)
    where = [next(i for i, s in enumerate(stacks) if s is g) for (g, _, _), _, _, _ in entries]
    ns, ne = len(stacks), len(entries)

    def body(*refs):
        s_refs, wmv, outs = refs[:ns], refs[ns:ns + 3 * ne], refs[ns + 3 * ne:]
        for e, ((_, r0, nr), _, _, _) in enumerate(entries):
            grad = s_refs[where[e]][r0:r0 + nr, :]
            w_ref, m_ref, v_ref = wmv[3 * e:3 * e + 3]
            g_ref, d_ref, nm_ref, nv_ref = outs[4 * e:4 * e + 4]
            g_ref[...] = grad
            d_ref[...], nm_ref[...], nv_ref[...] = _adamw(w_ref[...], grad, m_ref[...], v_ref[...])

    results = pl.pallas_call(
        body, name="adamw_small",
        out_shape=tuple(jax.ShapeDtypeStruct(w.shape, F32) for _, w, _, _ in entries for _ in range(4)),
        compiler_params=pltpu.CompilerParams(vmem_limit_bytes=VMEM_LIMIT),
    )(*stacks, *[a for _, w, m, v in entries for a in (w, m, v)])
    return [results[4 * e:4 * e + 4] for e in range(ne)]


def _mesh_pos():
    return lax.axis_index("x"), lax.axis_index("y"), lax.axis_index("c")


def _other_chips(x, y):
    return [(x, 1 - y), (1 - x, y), (1 - x, 1 - y)]


_VMEM_WHOLE = pl.BlockSpec(memory_space=pltpu.VMEM)
_HBM_WHOLE = pl.BlockSpec(memory_space=pl.ANY)


def _gather_comm(shards, cast, mid=((1, 2), (3, 4))):
    na = len(shards)
    staged = [a for a in range(na) if cast[a]]

    def phases(in_refs, out_refs, scr):
        stage = dict(zip(staged, scr[:len(staged)]))
        send_sems, recv_sems, local_sems = scr[len(staged):]
        x, y, c = _mesh_pos()
        me, sibling = (x, y, c), (x, y, 1 - c)
        chip_a, chip_b, diagonal = (x ^ c, y ^ (1 - c)), (x ^ (1 - c), y ^ c), (1 - x, 1 - y)
        srcs = [stage[a] if cast[a] else in_refs[a] for a in range(na)]

        def rows(a, pos):
            px, py, pc = pos
            return out_refs[a].at[4 * px + 2 * py + pc]

        def copy(a, k, block, to, src=None):
            return pltpu.make_async_remote_copy(
                src_ref=rows(a, block) if src is None else src, dst_ref=rows(a, block),
                send_sem=send_sems.at[a, k], recv_sem=recv_sems.at[a, k], device_id=to, device_id_type=MESH_ID)

        mine = [pltpu.make_async_copy(srcs[a], rows(a, me), local_sems.at[a]) for a in range(na)]
        own = [copy(a, k, me, to, src=srcs[a]) for a in range(na)
               for k, to in ((0, sibling), (1, (*chip_a, c)), (2, (*chip_b, c)))]
        onward = [copy(a, 3, (*chip_a, c), (*chip_b, c)) for a in range(na)]
        to_sibling = {k: [copy(a, k, (*chip, c), sibling) for a in range(na)]
                      for k, chip in ((4, chip_a), (5, chip_b), (6, diagonal))}

        def start():
            for a in staged:
                stage[a][...] = in_refs[a][...].astype(BF16)
            for cp in mine + own:
                cp.start()

        def forward_neighbours():
            for a in range(na):
                copy(a, 1, (*chip_a, c), me).wait_recv()
                onward[a].start()
                to_sibling[4][a].start()
            for a in range(na):
                copy(a, 2, (*chip_b, c), me).wait_recv()
                to_sibling[5][a].start()

        def forward_diagonal():
            for a in range(na):
                copy(a, 3, (*diagonal, c), me).wait_recv()
                to_sibling[6][a].start()

        def finish():
            for a in range(na):
                for k, chip in ((0, (x, y)), (4, chip_b), (5, chip_a), (6, diagonal)):
                    copy(a, k, (*chip, 1 - c), me).wait_recv()
            for cp in own + onward + to_sibling[4] + to_sibling[5] + to_sibling[6]:
                cp.wait_send()
            for cp in mine:
                cp.wait()

        return start, forward_neighbours, forward_diagonal, finish

    def before(step, nsteps, in_refs, out_refs, scr):
        start, forward_neighbours, forward_diagonal, _ = phases(in_refs, out_refs, scr)
        pl.when(step == 0)(start)
        pl.when(step == nsteps * mid[0][0] // mid[0][1])(forward_neighbours)
        pl.when(step == nsteps * mid[1][0] // mid[1][1])(forward_diagonal)

    def after(step, nsteps, in_refs, out_refs, scr):
        pl.when(step == nsteps - 1)(phases(in_refs, out_refs, scr)[3])

    return _Comm(
        inputs=list(shards), in_specs=[_VMEM_WHOLE] * na,
        out_shape=[jax.ShapeDtypeStruct((N_DEV,) + s.shape, BF16 if cast[a] else s.dtype)
                   for a, s in enumerate(shards)],
        out_specs=[_HBM_WHOLE] * na,
        scratch_shapes=[pltpu.VMEM(shards[a].shape, BF16) for a in staged] + [
            pltpu.SemaphoreType.DMA((na, 7)), pltpu.SemaphoreType.DMA((na, 7)), pltpu.SemaphoreType.DMA((na,))],
        before=before, after=after)


def _exchange_comm(arrays, out_shape, make_copies):
    na = len(arrays)

    def copies(in_refs, out_refs, scr):
        return make_copies(in_refs, out_refs, *scr)

    def before(step, nsteps, in_refs, out_refs, scr):
        @pl.when(step == 0)
        def _():
            for cp in copies(in_refs, out_refs, scr):
                cp.start()

    def after(step, nsteps, in_refs, out_refs, scr):
        @pl.when(step == nsteps - 1)
        def _():
            for cp in copies(in_refs, out_refs, scr):
                cp.wait()

    return _Comm(inputs=list(arrays), in_specs=[_HBM_WHOLE] * na, out_shape=list(out_shape),
                 out_specs=[_HBM_WHOLE] * na,
                 scratch_shapes=[pltpu.SemaphoreType.DMA((na, 3)), pltpu.SemaphoreType.DMA((na, 3))],
                 before=before, after=after)


def _sibling_exchange_comm(grads):
    def make_copies(in_refs, out_refs, send_sems, recv_sems):
        x, y, c = _mesh_pos()
        return [pltpu.make_async_remote_copy(
            src_ref=in_refs[a].at[:, pl.ds(1 - c, 1)], dst_ref=out_refs[a], send_sem=send_sems.at[a, 0],
            recv_sem=recv_sems.at[a, 0], device_id=(x, y, 1 - c), device_id_type=MESH_ID)
            for a in range(len(grads))]

    return _exchange_comm(grads, [jax.ShapeDtypeStruct((4, 1) + g.shape[2:], F32) for g in grads], make_copies)


def _chips_exchange_comm(partials):
    def make_copies(in_refs, out_refs, send_sems, recv_sems):
        x, y, c = _mesh_pos()
        return [pltpu.make_async_remote_copy(
            src_ref=in_refs[a].at[j], dst_ref=out_refs[a].at[j], send_sem=send_sems.at[a, j],
            recv_sem=recv_sems.at[a, j], device_id=(*chip, c), device_id_type=MESH_ID)
            for a in range(len(partials)) for j, chip in enumerate(_other_chips(x, y))]

    return _exchange_comm(partials, [jax.ShapeDtypeStruct(g.shape, BF16) for g in partials], make_copies)


def _comm_only(comms, name):
    return _fused_call(lambda: None, comms, name=name, grid=(1,), inputs=(), in_specs=[], out_specs=(),
                       out_shape=())[1]


def _chip_sum(my_pos, mine, from_sibling, tr, name):
    _, _, r, c = mine.shape

    def body(pos_ref, a_ref, b_ref, own_ref, out_ref):
        s = a_ref[0, 0] + b_ref[0, 0]

        @pl.when(pl.program_id(1) == 0)
        def _():
            own_ref[...] = s

        @pl.when(pl.program_id(1) > 0)
        def _():
            out_ref[0] = s.astype(BF16)

    grid_spec = pltpu.PrefetchScalarGridSpec(
        num_scalar_prefetch=1, grid=(r // tr, 4),
        in_specs=[pl.BlockSpec((1, 1, tr, c), lambda i, k, pos: (pos[0] ^ k, pos[1], i, 0)),
                  pl.BlockSpec((1, 1, tr, c), lambda i, k, pos: (pos[0] ^ k, 0, i, 0))],
        out_specs=(pl.BlockSpec((tr, c), lambda i, k, pos: (i, 0)),
                   pl.BlockSpec((1, tr, c), lambda i, k, pos: (jnp.maximum(k - 1, 0), i, 0))))
    return pl.pallas_call(
        body, name=name, grid_spec=grid_spec,
        out_shape=(jax.ShapeDtypeStruct((r, c), F32), jax.ShapeDtypeStruct((3, r, c), BF16)),
        compiler_params=_params(2),
    )(my_pos, mine, from_sibling)


def _all_reduce_small_comm(parts):
    na = len(parts)

    def copies(in_refs, scr):
        gathered, (send_sems, recv_sems) = scr[:na], scr[na:]
        x, y, c = _mesh_pos()
        my_id = 4 * x + 2 * y + c
        return my_id, [pltpu.make_async_remote_copy(
            src_ref=in_refs[a], dst_ref=gathered[a].at[my_id], send_sem=send_sems.at[a, k - 1],
            recv_sem=recv_sems.at[a, k - 1], device_id=(x ^ (k >> 2), y ^ ((k >> 1) & 1), c ^ (k & 1)),
            device_id_type=MESH_ID) for a in range(na) for k in range(1, N_DEV)]

    def before(step, nsteps, in_refs, out_refs, scr):
        @pl.when(step == 0)
        def _():
            for cp in copies(in_refs, scr)[1]:
                cp.start()

    def after(step, nsteps, in_refs, out_refs, scr):
        @pl.when(step == nsteps - 1)
        def _():
            my_id, cps = copies(in_refs, scr)
            for a in range(na):
                scr[a][my_id] = in_refs[a][...]
            for cp in cps:
                cp.wait()
            for a in range(na):
                acc = scr[a][0]
                for d in range(1, N_DEV):
                    acc = acc + scr[a][d]
                out_refs[a][...] = acc

    return _Comm(inputs=list(parts), in_specs=[_VMEM_WHOLE] * na,
                 out_shape=[jax.ShapeDtypeStruct(p.shape, F32) for p in parts], out_specs=[_VMEM_WHOLE] * na,
                 scratch_shapes=[pltpu.VMEM((N_DEV,) + p.shape, F32) for p in parts] + [
                     pltpu.SemaphoreType.DMA((na, N_DEV - 1)), pltpu.SemaphoreType.DMA((na, N_DEV - 1))],
                 before=before, after=after)


def _unshard_cols(g):
    return jnp.transpose(g, (1, 0, 2)).reshape(g.shape[1], N_DEV * g.shape[2])


def _row_blocks(w):
    return w.reshape(4, 2, w.shape[0] // N_DEV, w.shape[1])


def _stack_rows(parts):
    a = jnp.concatenate(parts, axis=0)
    return jnp.pad(a, ((0, (-a.shape[0]) % 8), (0, 0)))


def _w_in_grad_blocks(dw):
    return _row_blocks(jnp.concatenate([dw[:LR_REF], dw[LR_COL:LR_COL + 2 * LOWRANK], dw[LR_REF:LR_COL]], axis=0))


def _padded_decay_weights(wd_f, wd_b):
    zeros = lambda n: jnp.zeros((n, KEY_W), F32)
    return (jnp.concatenate([wd_f, zeros(LANE - LOWRANK)], axis=0),
            jnp.concatenate([zeros(LOWRANK), wd_b, zeros(LANE - 2 * LOWRANK)], axis=0))


def kernel(x, norm1_g, w_in,w_decay_f, b_decay_f, w_decay_b, b_decay_b, gla_norm_g, gmlp_ln_g, gmlp_ln_b, w_spatial, b_spatial, w_out, norm2_g, w_gate, w_up, w_down, final_norm_g, loss_target, m_norm1_g, m_w_in, m_w_decay_f, m_b_decay_f, m_w_decay_b, m_b_decay_b, m_gla_norm_g, m_gmlp_ln_g, m_gmlp_ln_b, m_w_spatial, m_b_spatial, m_w_out, m_norm2_g, m_w_gate, m_w_up, m_w_down, m_final_norm_g, v_norm1_g, v_w_in, v_w_decay_f, v_b_decay_f, v_w_decay_b, v_b_decay_b, v_gla_norm_g, v_gmlp_ln_g, v_gmlp_ln_b, v_w_spatial, v_b_spatial, v_w_out, v_norm2_g, v_w_gate, v_w_up, v_w_down, v_final_norm_g):
    t = x.shape[1]
    xt = x[0]
    target = loss_target[0]
    pos_x, pos_y, pos_c = _mesh_pos()
    my_pos = jnp.stack([2 * pos_x + pos_y, pos_c]).astype(jnp.int32)
    my_id = 4 * pos_x + 2 * pos_y + pos_c

    tile = lambda n: min(n, t)
    ln_g, ln_b, w_sp = gmlp_ln_g, gmlp_ln_b, w_spatial[0]
    b_sp_col = b_spatial[0][:, :, None]
    shard = {"w_in": w_in[0].T, "w_out": w_out[0], "w_gate": w_gate[0].T, "w_up": w_up[0].T, "w_down": w_down[0]}
    shard_m = {"w_in": m_w_in[0].T, "w_out": m_w_out[0], "w_gate": m_w_gate[0].T, "w_up": m_w_up[0].T,
               "w_down": m_w_down[0]}
    shard_v = {"w_in": v_w_in[0].T, "w_out": v_w_out[0], "w_gate": v_w_gate[0].T, "w_up": v_w_up[0].T,
               "w_down": v_w_down[0]}
    transposed = ("w_in", "w_gate", "w_up")
    chip_sum = lambda n, g, s: _chip_sum(my_pos, g, s[0], g.shape[2], "chip_sum_" + n)

    decay_shard = jnp.stack([w_decay_f[0], w_decay_b[0]])
    (hb,), ((g_in, g_decay),) = _norm1(xt, norm1_g, tile(TOKEN_TILE["norm1"]),
                                       [_gather_comm([shard["w_in"], decay_shard], [True, False])])
    w_in_t = g_in.reshape(PROJ_W, D_MODEL)
    wd_pad_f, wd_pad_b = _padded_decay_weights(_unshard_cols(g_decay[:, 0]), _unshard_cols(g_decay[:, 1]))
    (p,), ((g_gate, g_out),) = _in_proj(
        hb, w_in_t, tile(TOKEN_TILE["in_proj"]), [_gather_comm([shard["w_gate"], shard["w_out"]], [True, True])])
    (o_f, st_f, o_b, st_b), ((g_up,),) = _gla_fwd(
        p, wd_pad_f, b_decay_f, wd_pad_b, b_decay_b, tile(TOKEN_TILE["gla"]), [_gather_comm([shard["w_up"]], [True])])
    w_out_full = g_out.reshape(D_MODEL, D_MODEL)
    (x1, ycat), ((g_down,),) = _mix_fwd(xt, o_f, o_b, p, gla_norm_g, ln_g, ln_b, w_sp, b_sp_col, w_out_full,
                                        tile(TOKEN_TILE["mix_fwd"]), [_gather_comm([shard["w_down"]], [True])])

    dx1, h2b, dgate, dup, act, dx2, loss_acc, d_gf, d_g2 = _ffn(
        x1, target, norm2_g, final_norm_g[None, :], g_gate.reshape(D_FF, D_MODEL), g_up.reshape(D_FF, D_MODEL),
        g_down.reshape(D_FF, D_MODEL), tile(TOKEN_TILE["ffn"]))
    dw_gate, _ = _matmul_tn(dgate, h2b, D_FF // 2, tile(TOKEN_TILE["dw"]), "grad_w_gate")
    dw_up, _ = _matmul_tn(dup, h2b, D_FF // 2, tile(TOKEN_TILE["dw"]), "grad_w_up")
    dw_down, _ = _matmul_tn(act, dx2, D_FF // 2, tile(TOKEN_TILE["dw"]), "grad_w_down")

    ffn_grads = [_row_blocks(dw_gate), _row_blocks(dw_up), _row_blocks(dw_down)]
    (d_o, dpg, dpu, dpv, dw_out, d_gg, d_lg, d_lb, dw_sp, db_sp), (ffn_sib,) = _mix_bwd(
        dx1, ycat, o_f, o_b, p, gla_norm_g, ln_g, ln_b, w_sp, b_sp_col, w_out_full,
        tile(TOKEN_TILE["mix_bwd"]),
        [_sibling_exchange_comm(ffn_grads)])
    ffn_names = ["w_gate", "w_up", "w_down"]
    ffn_sums = [chip_sum(n, g, [s]) for n, g, s in zip(ffn_names, ffn_grads, ffn_sib)]
    out_grad = _row_blocks(dw_out)
    (dq_f, dk_f, dv_f, dlr_f, dwd_f, dbd_f, dq_b, dk_b, dv_b, dlr_b, dwd_b, dbd_b), (ffn_recv, out_sib) = _gla_bwd(
        p, wd_pad_f, b_decay_f, wd_pad_b, b_decay_b, st_f, st_b, d_o, tile(TOKEN_TILE["gla"]),
        [_chips_exchange_comm([s[1] for s in ffn_sums]), _sibling_exchange_comm([out_grad])])
    out_sum = chip_sum("w_out", out_grad, out_sib)
    (grad_x, dp, d_g1), _ = _in_proj_bwd(
        xt, norm1_g, dx1, dq_f, dq_b, dk_f, dk_b, dv_f, dv_b, dpg, dpu, dpv, dlr_f, dlr_b, w_in_t,
        tile(TOKEN_TILE["in_proj_bwd"]))

    stacks = [_stack_rows([d_g1, d_g2, d_gf]), _stack_rows([d_gg, d_lg, d_lb]),
              _stack_rows([dbd_f, dbd_b, jnp.zeros((DECAY_W_ROW - 2, KEY_W), F32), dwd_f[:LOWRANK],
                           dwd_b[LOWRANK:2 * LOWRANK]]),
              _stack_rows([dw_sp.reshape(GMLP_W, GMLP_CHUNK), db_sp[:, :, 0], loss_acc[:1]])]
    dw_main, (small_sums, out_recv) = _matmul_tn(
        dp, hb, PROJ_PAD // 3, tile(TOKEN_TILE["dw"]), "grad_w_in",
        [_all_reduce_small_comm(stacks), _chips_exchange_comm([out_sum[1]])])
    in_grad = _w_in_grad_blocks(dw_main)
    (in_sib,) = _comm_only([_sibling_exchange_comm([in_grad])], "grad_w_in_exchange_sibling")
    in_sum = chip_sum("w_in", in_grad, in_sib)
    (in_recv,) = _comm_only([_chips_exchange_comm([in_sum[1]])], "grad_w_in_exchange_chips")

    names = ["w_in", "w_out", "w_gate", "w_up", "w_down"]
    sums = [in_sum, out_sum] + ffn_sums
    received = [in_recv[0], out_recv[0]] + list(ffn_recv)
    big_out = {}
    for n, s, rc in zip(names, sums, received):
        res = _adamw_shard(s[0], rc, shard[n], shard_m[n], shard_v[n], shard[n].shape[0], "adamw_" + n)
        big_out[n] = [r.T if n in transposed else r for r in res]

    s1024, s512, s256, s128 = small_sums
    loss = s128[GMLP_W + GMLP_GROUPS, 0]
    col0 = my_id * (KEY_W // N_DEV)
    decay_cols = lambda row0: lax.dynamic_slice(s256, (row0, col0), (LOWRANK, KEY_W // N_DEV))
    flat = lambda a: a.reshape(-1, a.shape[-1])
    small = {
        "norm1_g": ((s1024, 0, 1), norm1_g, m_norm1_g, v_norm1_g),
        "w_decay_f": ((decay_cols(DECAY_W_ROW), 0, LOWRANK), w_decay_f, m_w_decay_f, v_w_decay_f),
        "b_decay_f": ((s256, 0, 1), b_decay_f, m_b_decay_f, v_b_decay_f),
        "w_decay_b": ((decay_cols(DECAY_W_ROW + LOWRANK), 0, LOWRANK), w_decay_b, m_w_decay_b, v_w_decay_b),
        "b_decay_b": ((s256, 1, 1), b_decay_b, m_b_decay_b, v_b_decay_b),
        "gla_norm_g": ((s512, 0, 1), gla_norm_g, m_gla_norm_g, v_gla_norm_g),
        "gmlp_ln_g": ((s512, 1, 1), gmlp_ln_g, m_gmlp_ln_g, v_gmlp_ln_g),
        "gmlp_ln_b": ((s512, 2, 1), gmlp_ln_b, m_gmlp_ln_b, v_gmlp_ln_b),
        "w_spatial": ((s128, 0, GMLP_W), w_spatial, m_w_spatial, v_w_spatial),
        "b_spatial": ((s128, GMLP_W, GMLP_GROUPS), b_spatial, m_b_spatial, v_b_spatial),
        "norm2_g": ((s1024, 1, 1), norm2_g, m_norm2_g, v_norm2_g),
        "final_norm_g": ((s1024, 2, 1), final_norm_g, m_final_norm_g, v_final_norm_g),
    }
    small_res = _adamw_small([(g, flat(w), flat(m), flat(v)) for g, w, m, v in small.values()])
    small_out = {n: [r.reshape(small[n][1].shape) for r in res] for n, res in zip(small, small_res)}

    order = ["norm1_g", "w_in", "w_decay_f", "b_decay_f", "w_decay_b", "b_decay_b", "gla_norm_g", "gmlp_ln_g",
             "gmlp_ln_b", "w_spatial", "b_spatial", "w_out", "norm2_g", "w_gate", "w_up", "w_down", "final_norm_g"]
    outs = []
    for kind in range(4):
        for n in order:
            outs.append(big_out[n][kind][None] if n in big_out else small_out[n][kind])
    return (loss, grad_x[None], *outs)
```

```python
import functools
import math

import jax
import jax.numpy as jnp
from jax import lax
from jax.experimental import pallas as pl
from jax.experimental.pallas import tpu as pltpu

F32 = jnp.float32
BF16 = jnp.bfloat16

D_MODEL = 1024
GLA_HEADS = 4
GLA_DK = 64
GLA_DV = 128
KEY_W = GLA_HEADS * GLA_DK
VAL_W = GLA_HEADS * GLA_DV
LOWRANK = 16
GLA_TAU = 16.0
GLA_CHUNK = 64
GMLP_W = 512
GMLP_GROUPS = 4
GMLP_CHUNK = 128
D_FF = 2816
EPS = 1e-6
Q_SCALE = GLA_DK ** -0.5
PROJ_PAD = 2688
LR_COL = 2560
LANE = 128
N_DEV = 8

ADAM_LR = 0.001
ADAM_B1 = 0.9
ADAM_B2 = 0.999
ADAM_EPS = 1e-08
ADAM_WD = 0.01
ADAM_STEP = 10

VMEM_LIMIT = 56 * 1024 * 1024
TOKEN_TILE = {"norm1": 512, "in_proj": 512, "gla": 1024, "mix_fwd": 1024, "ffn": 256, "mix_bwd": 512,
              "in_proj_bwd": 512, "dw": 2048}
DECAY_W_ROW = 8
MESH_ID = pl.DeviceIdType.MESH
INV_SQRT2 = 0.7071067811865476
INV_SQRT_2PI = 0.3989422804014327


def _params(n_axes=1):
    return pltpu.CompilerParams(dimension_semantics=("arbitrary",) * n_axes, vmem_limit_bytes=VMEM_LIMIT)


def _mm(a, b):
    return jnp.dot(a.astype(BF16), b.astype(BF16), preferred_element_type=F32)


def _mm_nt(a, b):
    return lax.dot_general(a.astype(BF16), b.astype(BF16), (((1,), (1,)), ((), ())), preferred_element_type=F32)


def _mm_tn(a, b):
    return lax.dot_general(a.astype(BF16), b.astype(BF16), (((0,), (0,)), ((), ())), preferred_element_type=F32)


def _const_spec(shape):
    nd = len(shape)
    return pl.BlockSpec(shape, lambda *_: (0,) * nd, pipeline_mode=pl.Buffered(1))


def _acc_spec(shape):
    nd = len(shape)
    return pl.BlockSpec(shape, lambda *_: (0,) * nd)


class _Comm:
    def __init__(self, inputs, in_specs, out_shape, out_specs, scratch_shapes, before, after):
        self.inputs, self.in_specs, self.out_shape, self.out_specs = inputs, in_specs, out_shape, out_specs
        self.scratch_shapes, self.before, self.after = scratch_shapes, before, after


def _fused_call(body, comms, *, name, grid, inputs, in_specs, out_specs, out_shape, scratch_shapes=()):
    n_in, n_out, n_scr = len(in_specs), len(out_specs), len(scratch_shapes)
    nsteps = math.prod(grid)
    sizes = [(len(c.inputs), len(c.out_shape), len(c.scratch_shapes)) for c in comms]

    def full_body(*refs):
        step = pl.program_id(0)
        for axis in range(1, len(grid)):
            step = step * grid[axis] + pl.program_id(axis)
        ins, rest = refs[:n_in], refs[n_in:]
        c_ins = []
        for ci, _, _ in sizes:
            c_ins.append(rest[:ci])
            rest = rest[ci:]
        outs, rest = rest[:n_out], rest[n_out:]
        c_outs = []
        for _, co, _ in sizes:
            c_outs.append(rest[:co])
            rest = rest[co:]
        scr, rest = rest[:n_scr], rest[n_scr:]
        c_scr = []
        for _, _, cs in sizes:
            c_scr.append(rest[:cs])
            rest = rest[cs:]
        for c, a, b, s in zip(comms, c_ins, c_outs, c_scr):
            c.before(step, nsteps, a, b, s)
        body(*ins, *outs, *scr)
        for c, a, b, s in zip(comms, c_ins, c_outs, c_scr):
            c.after(step, nsteps, a, b, s)

    results = pl.pallas_call(
        full_body, name=name, grid=grid,
        in_specs=list(in_specs) + [s for c in comms for s in c.in_specs],
        out_specs=tuple(out_specs) + tuple(s for c in comms for s in c.out_specs),
        out_shape=tuple(out_shape) + tuple(s for c in comms for s in c.out_shape),
        scratch_shapes=list(scratch_shapes) + [s for c in comms for s in c.scratch_shapes],
        compiler_params=_params(len(grid)),
    )(*inputs, *[a for c in comms for a in c.inputs])
    own, rest = results[:n_out], results[n_out:]
    comm_results = []
    for _, co, _ in sizes:
        comm_results.append(rest[:co])
        rest = rest[co:]
    return own, comm_results


def _gelu(x):
    return 0.5 * x * (1.0 + lax.erf(x * INV_SQRT2))


def _gelu_and_grad(x):
    cdf = 0.5 * (1.0 + lax.erf(x * INV_SQRT2))
    return x * cdf, cdf + x * jnp.exp(-0.5 * x * x) * INV_SQRT_2PI


def _sigmoid(x):
    return 0.5 + 0.5 * jnp.tanh(0.5 * x)


def _silu_and_grad(x):
    s = _sigmoid(x)
    return x * s, s * (1.0 + x * (1.0 - s))


def _norm1(x, g1, tm, comms=()):
    t = x.shape[0]

    def body(x_ref, g_ref, h_ref):
        xv = x_ref[...]
        r = lax.rsqrt(jnp.mean(xv * xv, axis=-1, keepdims=True) + EPS)
        h_ref[...] = (xv * r * g_ref[...]).astype(BF16)

    row = pl.BlockSpec((tm, D_MODEL), lambda i: (i, 0))
    return _fused_call(body, comms, name="norm1", grid=(t // tm,), inputs=(x, g1),
                       in_specs=[row, _const_spec((1, D_MODEL))], out_specs=(row,),
                       out_shape=(jax.ShapeDtypeStruct((t, D_MODEL), BF16),))


PROJ_W = 2592
LR_REF = 1536
PROJ_ROWS = ((0, LR_REF, 0), (LR_REF + 2 * LOWRANK, PROJ_W, LR_REF), (LR_REF, LR_REF + LANE, LR_COL))


def _in_proj(h, w_in_t, tm, comms=()):
    t = h.shape[0]

    def body(h_ref, w_ref, p_ref):
        hv = h_ref[...]
        for r0, r1, c0 in PROJ_ROWS:
            p_ref[:, c0:c0 + r1 - r0] = _mm_nt(hv, w_ref[r0:r1, :]).astype(BF16)

    return _fused_call(
        body, comms, name="in_proj", grid=(t // tm,), inputs=(h, w_in_t),
        in_specs=[pl.BlockSpec((tm, D_MODEL), lambda i: (i, 0)), _const_spec((PROJ_W, D_MODEL))],
        out_specs=(pl.BlockSpec((tm, PROJ_PAD), lambda i: (i, 0)),),
        out_shape=(jax.ShapeDtypeStruct((t, PROJ_PAD), BF16),))


def _tri(upper):
    r = lax.broadcasted_iota(jnp.int32, (GLA_CHUNK, GLA_CHUNK), 0)
    c = lax.broadcasted_iota(jnp.int32, (GLA_CHUNK, GLA_CHUNK), 1)
    return jnp.where((c >= r) if upper else (c <= r), 1.0, 0.0).astype(BF16)


def _chunk_cumsum(tri, a, add=None):
    hi = a.astype(BF16)
    lo = (a - hi.astype(F32)).astype(BF16)
    dot = functools.partial(jnp.dot, preferred_element_type=F32)
    sums = [dot(tri, hi[_chunk_rows(c)]) + dot(tri, lo[_chunk_rows(c)]) for c in range(a.shape[0] // GLA_CHUNK)]
    return jnp.concatenate(sums if add is None else [s + r for s, r in zip(sums, add)], axis=0)


def _chunk_rows(c):
    return slice(c * GLA_CHUNK, (c + 1) * GLA_CHUNK)


def _gla_masks(rev):
    dk_bits, dv_bits = GLA_DK.bit_length() - 1, GLA_DV.bit_length() - 1
    key_head = lax.broadcasted_iota(jnp.int32, (GLA_CHUNK, KEY_W), 1) >> dk_bits
    val_head = lax.broadcasted_iota(jnp.int32, (GLA_CHUNK, VAL_W), 1) >> dv_bits
    t = lax.broadcasted_iota(jnp.int32, (GLA_HEADS * GLA_CHUNK, GLA_CHUNK), 0) & (GLA_CHUNK - 1)
    s = lax.broadcasted_iota(jnp.int32, (GLA_HEADS * GLA_CHUNK, GLA_CHUNK), 1)
    return key_head, val_head, (s >= t) if rev else (s <= t)


def _stack_heads(a, head_of_lane):
    a = a.astype(BF16)
    return jnp.concatenate([jnp.where(head_of_lane == h, a, jnp.zeros_like(a)) for h in range(GLA_HEADS)], axis=0)


def _rows_by_head(a):
    return jnp.concatenate([a[:, h * GLA_DV:(h + 1) * GLA_DV] for h in range(GLA_HEADS)], axis=0)


def _lanes_by_head(r):
    return jnp.concatenate([r[h * GLA_CHUNK:(h + 1) * GLA_CHUNK] for h in range(GLA_HEADS)], axis=1)


def _head_diagonal(r, head_of_lane):
    rows = r.shape[0] // GLA_HEADS
    out = jnp.where(head_of_lane == 0, r[:rows], 0.0)
    for h in range(1, GLA_HEADS):
        out = out + jnp.where(head_of_lane == h, r[h * rows:(h + 1) * rows], 0.0)
    return out


def _tile_terms(la, q, k, tri, rev):
    nc = la.shape[0] // GLA_CHUNK
    q, k = q.astype(F32), k.astype(F32)
    b = _chunk_cumsum(tri, la)
    ebl = [jnp.exp(b[c * GLA_CHUNK:c * GLA_CHUNK + 1] if rev else b[(c + 1) * GLA_CHUNK - 1:(c + 1) * GLA_CHUNK])
           for c in range(nc)]
    eb = jnp.exp(b)
    enb = jnp.exp(-b)
    kd = k * enb
    ke = jnp.concatenate([kd[_chunk_rows(c)] * ebl[c] for c in range(nc)], axis=0)
    return ebl, eb, enb, q * Q_SCALE * eb, kd, ke


def _log_decay(lr_ref, wd_ref, bd_ref):
    z = _mm(lr_ref[...], wd_ref[...]) + bd_ref[...]
    return z, jax.nn.log_sigmoid(z) * (1.0 / GLA_TAU)


def _p_specs(tg, tile):
    return [pl.BlockSpec((tg, KEY_W), lambda i: (tile(i), 0)),
            pl.BlockSpec((tg, KEY_W), lambda i: (tile(i), 1)),
            pl.BlockSpec((tg, VAL_W), lambda i: (tile(i), 1)),
            pl.BlockSpec((tg, LANE), lambda i: (tile(i), LR_COL // LANE))]


def _gla_fwd_dir(rev, nc, q_ref, k_ref, v_ref, lr_ref, wd_ref, bd_ref, o_ref, st_ref, state):
    key_head, _, causal = _gla_masks(rev)
    order = range(nc - 1, -1, -1) if rev else range(nc)

    def intra():
        _, la = _log_decay(lr_ref, wd_ref, bd_ref)
        ebl, _, _, qd, kd, ke = _tile_terms(la, q_ref[...], k_ref[...], _tri(rev), rev)
        kd = kd.astype(BF16)
        v = {c: v_ref[_chunk_rows(c), :].astype(BF16) for c in order}
        qd_stack = {c: _stack_heads(qd[_chunk_rows(c)], key_head) for c in order}
        ke_stack = {c: _stack_heads(ke[_chunk_rows(c)], key_head) for c in order}
        a_all = {c: _mm_nt(qd_stack[c], kd[_chunk_rows(c)]) for c in order}
        a_all = {c: jnp.where(causal, a_all[c], 0.0).astype(BF16) for c in order}
        head_rows = lambda a, h: a[h * GLA_CHUNK:(h + 1) * GLA_CHUNK]
        head_vals = lambda a, h: a[:, h * GLA_DV:(h + 1) * GLA_DV]
        r = {c: [_mm(head_rows(a_all[c], h), head_vals(v[c], h)) for h in range(GLA_HEADS)] for c in order}
        upd = {c: _mm_tn(_rows_by_head(v[c]), ke_stack[c]) for c in order}
        return {c: (ebl[c], qd_stack[c], r[c], upd[c]) for c in order}

    def scan(terms):
        st = state[...]
        states = {}
        for c in order:
            states[c] = st
            st_ref[c] = st.astype(BF16)
            st = st * terms[c][0] + terms[c][3]
        state[...] = st
        return states

    def inter(terms, states):
        r_inter = {c: _mm_nt(terms[c][1], states[c]) for c in order}
        for c in order:
            o_ref[_chunk_rows(c), :] = jnp.concatenate(
                [terms[c][2][h] + r_inter[c][h * GLA_CHUNK:(h + 1) * GLA_CHUNK] for h in range(GLA_HEADS)], axis=1)

    return intra, scan, inter


def _gla_fwd(p, wd_pad_f, bd_f, wd_pad_b, bd_b, tg, comms=()):
    t = p.shape[0]
    nt = t // tg
    nc = tg // GLA_CHUNK
    up, down = (lambda i: i), (lambda i: nt - 1 - i)

    def body(qf, kf, vf, lrf, qb, kb, vb, lrb, wdf, bdf, wdb, bdb, of, stf, ob, stb, state_f, state_b):
        @pl.when(pl.program_id(0) == 0)
        def _():
            state_f[...] = jnp.zeros_like(state_f)
            state_b[...] = jnp.zeros_like(state_b)

        dirs = [_gla_fwd_dir(False, nc, qf, kf, vf, lrf, wdf, bdf, of, stf, state_f),
                _gla_fwd_dir(True, nc, qb, kb, vb, lrb, wdb, bdb, ob, stb, state_b)]
        terms = [intra() for intra, _, _ in dirs]
        states = [scan(t) for (_, scan, _), t in zip(dirs, terms)]
        for (_, _, inter), t, s in zip(dirs, terms, states):
            inter(t, s)

    wd_spec, bd_spec = _const_spec((LANE, KEY_W)), _const_spec((1, KEY_W))
    outs = lambda tile: (pl.BlockSpec((tg, VAL_W), lambda i: (tile(i), 0)),
                         pl.BlockSpec((nc, GLA_DV, KEY_W), lambda i: (tile(i), 0, 0)))
    out_shape = (jax.ShapeDtypeStruct((t, VAL_W), F32), jax.ShapeDtypeStruct((t // GLA_CHUNK, GLA_DV, KEY_W), BF16))
    return _fused_call(
        body, comms, name="gla_fwd", grid=(nt,), inputs=(p,) * 8 + (wd_pad_f, bd_f, wd_pad_b, bd_b),
        in_specs=_p_specs(tg, up) + _p_specs(tg, down) + [wd_spec, bd_spec, wd_spec, bd_spec],
        out_specs=outs(up) + outs(down), out_shape=out_shape * 2,
        scratch_shapes=[pltpu.VMEM((GLA_DV, KEY_W), F32)] * 2)


def _gla_bwd_dir(rev, nc, q_ref, k_ref, v_ref, lr_ref, wd_ref, bd_ref, st_ref, do_ref,
                 dq_ref, dk_ref, dv_ref, dlr_ref, dwd_ref, dbd_ref, dstate):
    key_head, val_head, causal = _gla_masks(rev)
    order = range(nc) if rev else range(nc - 1, -1, -1)

    def intra():
        z, la = _log_decay(lr_ref, wd_ref, bd_ref)
        tile = _tile_terms(la, q_ref[...], k_ref[...], _tri(rev), rev)
        qd, kd = tile[3], tile[4].astype(BF16)
        v = {c: v_ref[_chunk_rows(c), :].astype(BF16) for c in order}
        d_o = {c: do_ref[_chunk_rows(c), :] for c in order}
        kd_c = {c: kd[_chunk_rows(c)] for c in order}
        qd_stack = {c: _stack_heads(qd[_chunk_rows(c)], key_head) for c in order}
        do_stack = {c: _stack_heads(d_o[c], val_head) for c in order}
        do_rows = {c: _rows_by_head(d_o[c]) for c in order}
        a_all = {c: _mm_nt(qd_stack[c], kd_c[c]) for c in order}
        head_vals = lambda a, h: a[:, h * GLA_DV:(h + 1) * GLA_DV]
        da_all = {c: jnp.concatenate([_mm_nt(head_vals(d_o[c], h), head_vals(v[c], h)) for h in range(GLA_HEADS)],
                                     axis=0) for c in order}
        a_all = {c: jnp.where(causal, a_all[c], 0.0).astype(BF16) for c in order}
        da_all = {c: jnp.where(causal, da_all[c], 0.0).astype(BF16) for c in order}
        dv = {c: _mm_tn(a_all[c], do_stack[c]) for c in order}
        dqd = {c: _mm(jnp.concatenate([do_rows[c], da_all[c]], axis=1),
                      jnp.concatenate([st_ref[c], kd_c[c]], axis=0)) for c in order}
        dkd = {c: _mm_tn(da_all[c], qd_stack[c]) for c in order}
        upd = {c: _mm_tn(do_rows[c], qd_stack[c]) for c in order}
        dqd = {c: _head_diagonal(dqd[c], key_head) for c in order}
        return z, tile, {c: dict(dv=dv[c], dqd=dqd[c], dkd=dkd[c], upd=upd[c]) for c in order}

    def scan(tile, per):
        dst = dstate[...]
        dsts = {}
        for c in order:
            dsts[c] = dst
            dst = dst * tile[0][c] + per[c]["upd"]
        dstate[...] = dst
        return dsts

    def inter(z, tile, per, dsts):
        ebl, eb, enb, qd, kd, ke = tile
        ke_stack = {c: _stack_heads(ke[_chunk_rows(c)], key_head) for c in order}
        v_rows = {c: _rows_by_head(v_ref[_chunk_rows(c), :].astype(BF16)) for c in order}
        dst_b = {c: dsts[c].astype(BF16) for c in order}
        dv_state = {c: _mm_nt(ke_stack[c], dst_b[c]) for c in order}
        dke_c = {c: _mm(v_rows[c], dst_b[c]) for c in order}
        dke_c = {c: _head_diagonal(dke_c[c], key_head) for c in order}
        dbl_c = {}
        for c in order:
            rows = _chunk_rows(c)
            dv_ref[rows, :] = (per[c]["dv"] + _lanes_by_head(dv_state[c])).astype(BF16)
            dbl_c[c] = (jnp.sum(dsts[c] * st_ref[c].astype(F32), axis=0, keepdims=True) * ebl[c]
                        + jnp.sum(dke_c[c] * ke[rows], axis=0, keepdims=True))
        tile_of = lambda parts: jnp.concatenate([parts[c] for c in range(nc)], axis=0)
        dqd, dkd = tile_of({c: per[c]["dqd"] for c in order}), tile_of({c: per[c]["dkd"] for c in order})
        dke = tile_of(dke_c)
        dke_end = tile_of({c: dke_c[c] * ebl[c] for c in order})
        dq_ref[...] = (dqd * eb * Q_SCALE).astype(BF16)
        dk_ref[...] = ((dkd + dke_end) * enb).astype(BF16)
        db = dqd * qd - dkd * kd - dke * ke
        dla = _chunk_cumsum(_tri(not rev), db, [dbl_c[c] for c in range(nc)])
        dz = dla * (_sigmoid(-z) * (1.0 / GLA_TAU))
        dlr_ref[...] = _mm_nt(dz, wd_ref[...]).astype(BF16)
        dwd_ref[...] += _mm_tn(lr_ref[...], dz)
        dbd_ref[...] += jnp.sum(dz, axis=0, keepdims=True)

    return intra, scan, inter


def _gla_bwd(p, wd_pad_f, bd_f, wd_pad_b, bd_b, st_f, st_b, d_o, tg, comms=()):
    t = p.shape[0]
    nt = t // tg
    nc = tg // GLA_CHUNK
    up, down = (lambda i: i), (lambda i: nt - 1 - i)

    def body(qf, kf, vf, lrf, stf, dof, qb, kb, vb, lrb, stb, dob, wdf, bdf, wdb, bdb,
             dqf, dkf, dvf, dlrf, dwdf, dbdf, dqb, dkb, dvb, dlrb, dwdb, dbdb, dstate_f, dstate_b):
        @pl.when(pl.program_id(0) == 0)
        def _():
            for ref in (dstate_f, dstate_b, dwdf, dbdf, dwdb, dbdb):
                ref[...] = jnp.zeros_like(ref)

        dirs = [_gla_bwd_dir(False, nc, qf, kf, vf, lrf, wdf, bdf, stf, dof, dqf, dkf, dvf, dlrf, dwdf, dbdf,
                             dstate_f),
                _gla_bwd_dir(True, nc, qb, kb, vb, lrb, wdb, bdb, stb, dob, dqb, dkb, dvb, dlrb, dwdb, dbdb,
                             dstate_b)]
        first = [intra() for intra, _, _ in dirs]
        dsts = [scan(tile, per) for (_, scan, _), (_, tile, per) in zip(dirs, first)]
        for (_, _, inter), (z, tile, per), d in zip(dirs, first, dsts):
            inter(z, tile, per, d)

    wd_spec, bd_spec = _const_spec((LANE, KEY_W)), _const_spec((1, KEY_W))
    ins = lambda tile: _p_specs(tg, tile) + [pl.BlockSpec((nc, GLA_DV, KEY_W), lambda i: (tile(i), 0, 0)),
                                             pl.BlockSpec((tg, VAL_W), lambda i: (tile(i), 0))]
    outs = lambda tile: (pl.BlockSpec((tg, KEY_W), lambda i: (tile(i), 0)),
                         pl.BlockSpec((tg, KEY_W), lambda i: (tile(i), 0)),
                         pl.BlockSpec((tg, VAL_W), lambda i: (tile(i), 0)),
                         pl.BlockSpec((tg, LANE), lambda i: (tile(i), 0)),
                         _acc_spec((LANE, KEY_W)), _acc_spec((1, KEY_W)))
    out_shape = (jax.ShapeDtypeStruct((t, KEY_W), BF16), jax.ShapeDtypeStruct((t, KEY_W), BF16),
                 jax.ShapeDtypeStruct((t, VAL_W), BF16), jax.ShapeDtypeStruct((t, LANE), BF16),
                 jax.ShapeDtypeStruct((LANE, KEY_W), F32), jax.ShapeDtypeStruct((1, KEY_W), F32))
    scratch = [pltpu.VMEM((GLA_DV, KEY_W), F32)]
    return _fused_call(
        body, comms, name="gla_bwd", grid=(nt,),
        inputs=(p, p, p, p, st_f, d_o, p, p, p, p, st_b, d_o, wd_pad_f, bd_f, wd_pad_b, bd_b),
        in_specs=ins(down) + ins(up) + [wd_spec, bd_spec, wd_spec, bd_spec],
        out_specs=outs(down) + outs(up), out_shape=out_shape * 2, scratch_shapes=scratch * 2)


def _head_rms(o):
    parts, scales = [], []
    for h in range(GLA_HEADS):
        oh = o[:, h * GLA_DV:(h + 1) * GLA_DV]
        r = lax.rsqrt(jnp.mean(oh * oh, axis=-1, keepdims=True) + EPS)
        parts.append(oh * r)
        scales.append(jnp.broadcast_to(r, oh.shape))
    return jnp.concatenate(parts, axis=1), jnp.concatenate(scales, axis=1)


def _layernorm_stats(zv):
    mu = jnp.mean(zv, axis=-1, keepdims=True)
    xc = zv - mu
    rs = lax.rsqrt(jnp.mean(xc * xc, axis=-1, keepdims=True) + EPS)
    return xc * rs, rs


def _mix_fwd(x, o_f, o_b, p, gla_g, ln_g, ln_b, w_sp, b_sp, w_out, tm, comms=()):
    t = x.shape[0]
    nch = tm // GMLP_CHUNK

    def body(x_ref, of_ref, ob_ref, pg_ref, pu_ref, pv_ref, gg_ref, lg_ref, lb_ref, ws_ref, bs_ref, wo_ref,
             x1_ref, y_ref, s_scr):
        on, _ = _head_rms(of_ref[...] + ob_ref[...])
        pg = pg_ref[...].astype(F32)
        y_a = on * gg_ref[...] * (pg * _sigmoid(pg))
        zu = _gelu(pu_ref[...].astype(F32))
        vhat, _ = _layernorm_stats(_gelu(pv_ref[...].astype(F32)))
        vln = (vhat * lg_ref[...] + lb_ref[...]).astype(BF16)
        for g in range(GMLP_GROUPS):
            w_g = ws_ref[g].astype(BF16)
            b_g = bs_ref[g]
            cols = slice(g * LANE, (g + 1) * LANE)
            for n in range(nch):
                rows = slice(n * GMLP_CHUNK, (n + 1) * GMLP_CHUNK)
                s_scr[rows, cols] = jnp.dot(w_g, vln[rows, cols], preferred_element_type=F32) + b_g
        ycat = jnp.concatenate([y_a, zu * s_scr[...]], axis=1).astype(BF16)
        y_ref[...] = ycat
        x1_ref[...] = x_ref[...] + jnp.dot(ycat, wo_ref[...], preferred_element_type=F32)

    half = lambda j: pl.BlockSpec((tm, VAL_W), lambda i: (i, j))
    return _fused_call(
        body, comms, name="mix_fwd", grid=(t // tm,),
        inputs=(x, o_f, o_b, p, p, p, gla_g, ln_g, ln_b, w_sp, b_sp, w_out),
        in_specs=[pl.BlockSpec((tm, D_MODEL), lambda i: (i, 0)), half(0), half(0), half(2), half(3), half(4),
                  _const_spec((1, VAL_W)), _const_spec((1, GMLP_W)), _const_spec((1, GMLP_W)),
                  _const_spec((GMLP_GROUPS, GMLP_CHUNK, GMLP_CHUNK)), _const_spec((GMLP_GROUPS, GMLP_CHUNK, 1)),
                  _const_spec((D_MODEL, D_MODEL))],
        out_specs=(pl.BlockSpec((tm, D_MODEL), lambda i: (i, 0)), pl.BlockSpec((tm, D_MODEL), lambda i: (i, 0))),
        out_shape=(jax.ShapeDtypeStruct((t, D_MODEL), F32), jax.ShapeDtypeStruct((t, D_MODEL), BF16)),
        scratch_shapes=[pltpu.VMEM((tm, GMLP_W), F32)])


def _mix_bwd(dx1, ycat, o_f, o_b, p, gla_g, ln_g, ln_b, w_sp, b_sp, w_out, tm, comms=()):
    t = dx1.shape[0]
    nch = tm // GMLP_CHUNK

    def body(dx1_ref, y_ref, of_ref, ob_ref, pg_ref, pu_ref, pv_ref, gg_ref, lg_ref, lb_ref, ws_ref, bs_ref, wo_ref,
             do_ref, dpg_ref, dpu_ref, dpv_ref, dwo_ref, dgg_ref, dlg_ref, dlb_ref, dws_ref, dbs_ref,
             s_scr, dvln_scr):
        @pl.when(pl.program_id(0) == 0)
        def _():
            for ref in (dwo_ref, dgg_ref, dlg_ref, dlb_ref, dws_ref, dbs_ref):
                ref[...] = jnp.zeros_like(ref)

        dx1 = dx1_ref[...].astype(BF16)
        dycat = _mm_nt(dx1, wo_ref[...])
        dwo_ref[...] += _mm_tn(y_ref[...], dx1)
        dy_a = dycat[:, :VAL_W]
        dy_b = dycat[:, VAL_W:]
        on, r = _head_rms(of_ref[...] + ob_ref[...])
        pg = pg_ref[...].astype(F32)
        sil, dsil = _silu_and_grad(pg)
        gg = gg_ref[...]
        dgg_ref[...] += jnp.sum(dy_a * sil * on, axis=0, keepdims=True)
        don = dy_a * sil * gg
        prod = don * on
        means = jnp.concatenate(
            [jnp.broadcast_to(jnp.mean(prod[:, h * GLA_DV:(h + 1) * GLA_DV], axis=-1, keepdims=True),
                              (tm, GLA_DV)) for h in range(GLA_HEADS)], axis=1)
        do_ref[...] = (r * (don - on * means)).astype(BF16)
        dpg_ref[...] = (dy_a * on * gg * dsil).astype(BF16)
        pu = pu_ref[...].astype(F32)
        pv = pv_ref[...].astype(F32)
        zu, dzu_dpu = _gelu_and_grad(pu)
        zv, dzv_dpv = _gelu_and_grad(pv)
        vhat, rs = _layernorm_stats(zv)
        lg = lg_ref[...]
        vln = (vhat * lg + lb_ref[...]).astype(BF16)
        ds32 = dy_b * zu
        ds = ds32.astype(BF16)
        blocks = [(g, n) for g in range(GMLP_GROUPS) for n in range(nch)]
        at = lambda g, n: (slice(n * GMLP_CHUNK, (n + 1) * GMLP_CHUNK), slice(g * LANE, (g + 1) * LANE))
        w_sp = [ws_ref[g].astype(BF16) for g in range(GMLP_GROUPS)]
        v_blk = {b: vln[at(*b)] for b in blocks}
        ds_blk = {b: ds[at(*b)] for b in blocks}
        s_blk = {b: jnp.dot(w_sp[b[0]], v_blk[b], preferred_element_type=F32) for b in blocks}
        dw_blk = {b: _mm_nt(ds_blk[b], v_blk[b]) for b in blocks}
        dvln_blk = {b: _mm_tn(w_sp[b[0]], ds_blk[b]) for b in blocks}
        for b in blocks:
            s_scr[at(*b)] = s_blk[b] + bs_ref[b[0]]
            dvln_scr[at(*b)] = dvln_blk[b]
        for g in range(GMLP_GROUPS):
            dws_ref[g] += sum(dw_blk[(g, n)] for n in range(nch))
            dbs_ref[g] += sum(jnp.sum(ds32[at(g, n)], axis=-1, keepdims=True) for n in range(nch))
        dpu_ref[...] = (dy_b * s_scr[...] * dzu_dpu).astype(BF16)
        dvln = dvln_scr[...]
        dlg_ref[...] += jnp.sum(dvln * vhat, axis=0, keepdims=True)
        dlb_ref[...] += jnp.sum(dvln, axis=0, keepdims=True)
        dvhat = dvln * lg
        dzv = rs * (dvhat - jnp.mean(dvhat, axis=-1, keepdims=True)
                    - vhat * jnp.mean(dvhat * vhat, axis=-1, keepdims=True))
        dpv_ref[...] = (dzv * dzv_dpv).astype(BF16)

    half = lambda j: pl.BlockSpec((tm, VAL_W), lambda i: (i, j))
    full = pl.BlockSpec((tm, D_MODEL), lambda i: (i, 0))
    sp_shape = (GMLP_GROUPS, GMLP_CHUNK, GMLP_CHUNK)
    bs_shape = (GMLP_GROUPS, GMLP_CHUNK, 1)
    return _fused_call(
        body, comms, name="mix_bwd", grid=(t // tm,),
        inputs=(dx1, ycat, o_f, o_b, p, p, p, gla_g, ln_g, ln_b, w_sp, b_sp, w_out),
        in_specs=[full, full, half(0), half(0), half(2), half(3), half(4),
                  _const_spec((1, VAL_W)), _const_spec((1, GMLP_W)), _const_spec((1, GMLP_W)),
                  _const_spec(sp_shape), _const_spec(bs_shape), _const_spec((D_MODEL, D_MODEL))],
        out_specs=(half(0), half(0), half(0), half(0), _acc_spec((D_MODEL, D_MODEL)), _acc_spec((1, VAL_W)),
                   _acc_spec((1, GMLP_W)), _acc_spec((1, GMLP_W)), _acc_spec(sp_shape), _acc_spec(bs_shape)),
        out_shape=(jax.ShapeDtypeStruct((t, VAL_W), BF16),) * 4 + (
            jax.ShapeDtypeStruct((D_MODEL, D_MODEL), F32), jax.ShapeDtypeStruct((1, VAL_W), F32),
            jax.ShapeDtypeStruct((1, GMLP_W), F32), jax.ShapeDtypeStruct((1, GMLP_W), F32),
            jax.ShapeDtypeStruct(sp_shape, F32), jax.ShapeDtypeStruct(bs_shape, F32)),
        scratch_shapes=[pltpu.VMEM((tm, GMLP_W), F32), pltpu.VMEM((tm, GMLP_W), F32)])


def _rms_bwd(dy_scaled, xn, r):
    return r * (dy_scaled - xn * jnp.mean(dy_scaled * xn, axis=-1, keepdims=True))


def _ffn(x1, target, g2, gf, w_gate, w_up, w_down, tm):
    t = x1.shape[0]

    def body(x1_ref, tg_ref, g2_ref, gf_ref, wg_ref, wu_ref, wd_ref,
             dx1_ref, h2_ref, dgate_ref, dup_ref, act_ref, dx2_ref, loss_ref, dgf_ref, dg2_ref):
        @pl.when(pl.program_id(0) == 0)
        def _():
            for ref in (loss_ref, dgf_ref, dg2_ref):
                ref[...] = jnp.zeros_like(ref)

        x1v = x1_ref[...]
        g2v = g2_ref[...]
        gfv = gf_ref[...]
        r2 = lax.rsqrt(jnp.mean(x1v * x1v, axis=-1, keepdims=True) + EPS)
        xn1 = x1v * r2
        h2 = (xn1 * g2v).astype(BF16)
        h2_ref[...] = h2
        gate = _mm_nt(h2, wg_ref[...])
        up = _mm_nt(h2, wu_ref[...])
        sil, dsil = _silu_and_grad(gate)
        act = (sil * up).astype(BF16)
        act_ref[...] = act
        x2 = x1v + jnp.dot(act, wd_ref[...], preferred_element_type=F32)
        rf = lax.rsqrt(jnp.mean(x2 * x2, axis=-1, keepdims=True) + EPS)
        xn2 = x2 * rf
        err = xn2 * gfv - tg_ref[...]
        loss_ref[...] += 0.5 * jnp.sum(jnp.mean(err * err, axis=-1, keepdims=True))
        dy = err * (1.0 / D_MODEL)
        dgf_ref[...] += jnp.sum(dy * xn2, axis=0, keepdims=True)
        dx2 = _rms_bwd(dy * gfv, xn2, rf)
        dx2b = dx2.astype(BF16)
        dx2_ref[...] = dx2b
        dact = _mm_nt(dx2b, wd_ref[...])
        dgate = (dact * up * dsil).astype(BF16)
        dup = (dact * sil).astype(BF16)
        dgate_ref[...] = dgate
        dup_ref[...] = dup
        dh2 = _mm(dgate, wg_ref[...]) + _mm(dup, wu_ref[...])
        dg2_ref[...] += jnp.sum(dh2 * xn1, axis=0, keepdims=True)
        dx1_ref[...] = dx2 + _rms_bwd(dh2 * g2v, xn1, r2)

    row = lambda w: pl.BlockSpec((tm, w), lambda i: (i, 0))
    return pl.pallas_call(
        body, name="ffn_fwd_bwd", grid=(t // tm,),
        in_specs=[row(D_MODEL), row(D_MODEL), _const_spec((1, D_MODEL)), _const_spec((1, D_MODEL)),
                  _const_spec((D_FF, D_MODEL)), _const_spec((D_FF, D_MODEL)), _const_spec((D_FF, D_MODEL))],
        out_specs=(row(D_MODEL), row(D_MODEL), row(D_FF), row(D_FF), row(D_FF), row(D_MODEL),
                   _acc_spec((8, LANE)), _acc_spec((1, D_MODEL)), _acc_spec((1, D_MODEL))),
        out_shape=(jax.ShapeDtypeStruct((t, D_MODEL), F32), jax.ShapeDtypeStruct((t, D_MODEL), BF16),
                   jax.ShapeDtypeStruct((t, D_FF), BF16), jax.ShapeDtypeStruct((t, D_FF), BF16),
                   jax.ShapeDtypeStruct((t, D_FF), BF16), jax.ShapeDtypeStruct((t, D_MODEL), BF16),
                   jax.ShapeDtypeStruct((8, LANE), F32), jax.ShapeDtypeStruct((1, D_MODEL), F32),
                   jax.ShapeDtypeStruct((1, D_MODEL), F32)),
        compiler_params=_params(),
    )(x1, target, g2, gf, w_gate, w_up, w_down)


def _matmul_tn(a, b, tm, tk, name, comms=()):
    t, m = a.shape
    n = b.shape[1]

    def body(a_ref, b_ref, o_ref):
        @pl.when(pl.program_id(1) == 0)
        def _():
            o_ref[...] = jnp.zeros_like(o_ref)

        o_ref[...] += _mm_tn(a_ref[...], b_ref[...])

    (out,), comm_results = _fused_call(
        body, comms, name=name, grid=(m // tm, t // tk), inputs=(a, b),
        in_specs=[pl.BlockSpec((tk, tm), lambda j, k: (k, j)), pl.BlockSpec((tk, n), lambda j, k: (k, 0))],
        out_specs=(pl.BlockSpec((tm, n), lambda j, k: (j, 0)),),
        out_shape=(jax.ShapeDtypeStruct((m, n), F32),))
    return out, comm_results


def _in_proj_bwd(x, g1, dx1, dq_f, dq_b, dk_f, dk_b, dv_f, dv_b, dpg, dpu, dpv, dlr_f, dlr_b, w_main, tm, comms=()):
    t = x.shape[0]

    def body(x_ref, g_ref, dx1_ref, dqf, dqb, dkf, dkb, dvf, dvb, dg, du, dv, dlf, dlb, w_ref,
             dx_ref, dp_ref, dg1_ref):
        @pl.when(pl.program_id(0) == 0)
        def _():
            dg1_ref[...] = jnp.zeros_like(dg1_ref)

        both = lambda a, b: (a[...].astype(F32) + b[...].astype(F32)).astype(BF16)
        dp = jnp.concatenate([both(dqf, dqb), both(dkf, dkb), both(dvf, dvb), dg[...], du[...], dv[...],
                              both(dlf, dlb)], axis=1)
        dp_ref[...] = dp
        dh = sum(_mm(dp[:, c0:c0 + r1 - r0], w_ref[r0:r1, :]) for r0, r1, c0 in PROJ_ROWS)
        xv = x_ref[...]
        r = lax.rsqrt(jnp.mean(xv * xv, axis=-1, keepdims=True) + EPS)
        xn = xv * r
        dg1_ref[...] += jnp.sum(dh * xn, axis=0, keepdims=True)
        dx_ref[...] = dx1_ref[...] + _rms_bwd(dh * g_ref[...], xn, r)

    row = lambda w: pl.BlockSpec((tm, w), lambda i: (i, 0))
    return _fused_call(
        body, comms, name="in_proj_bwd", grid=(t // tm,),
        inputs=(x, g1, dx1, dq_f, dq_b, dk_f, dk_b, dv_f, dv_b, dpg, dpu, dpv, dlr_f, dlr_b, w_main),
        in_specs=[row(D_MODEL), _const_spec((1, D_MODEL)), row(D_MODEL), row(KEY_W), row(KEY_W), row(KEY_W),
                  row(KEY_W), row(VAL_W), row(VAL_W), row(VAL_W), row(VAL_W), row(VAL_W), row(LANE), row(LANE),
                  _const_spec((PROJ_W, D_MODEL))],
        out_specs=(row(D_MODEL), row(PROJ_PAD), _acc_spec((1, D_MODEL))),
        out_shape=(jax.ShapeDtypeStruct((t, D_MODEL), F32), jax.ShapeDtypeStruct((t, PROJ_PAD), BF16),
                   jax.ShapeDtypeStruct((1, D_MODEL), F32)))


def _adamw(w, g, m, v):
    m_new = ADAM_B1 * m + (1.0 - ADAM_B1) * g
    v_new = ADAM_B2 * v + (1.0 - ADAM_B2) * (g * g)
    m_hat = m_new / (1.0 - ADAM_B1 ** ADAM_STEP)
    v_hat = v_new / (1.0 - ADAM_B2 ** ADAM_STEP)
    delta = -ADAM_LR * (m_hat / (jnp.sqrt(v_hat) + ADAM_EPS) + ADAM_WD * w)
    return delta, m_new, v_new


def _adamw_shard(own, recv, w, m, v, tr, name):
    r, c = w.shape

    def body(own_ref, recv_ref, w_ref, m_ref, v_ref, g_ref, d_ref, nm_ref, nv_ref):
        g = own_ref[...]
        for k in range(3):
            g = g + recv_ref[k].astype(F32)
        g_ref[...] = g
        d_ref[...], nm_ref[...], nv_ref[...] = _adamw(w_ref[...], g, m_ref[...], v_ref[...])

    row = pl.BlockSpec((tr, c), lambda i: (i, 0))
    return pl.pallas_call(
        body, name=name, grid=(r // tr,),
        in_specs=[row, pl.BlockSpec((3, tr, c), lambda i: (0, i, 0)), row, row, row],
        out_specs=(row,) * 4, out_shape=(jax.ShapeDtypeStruct((r, c), F32),) * 4,
        compiler_params=_params(),
    )(own, recv, w, m, v)


def _adamw_small(entries):
    stacks = []
    for (g, _, _), _, _, _ in entries:
        if not any(g is s for s in stacks):
            stacks.append(g)
    where = [next(i for i, s in enumerate(stacks) if s is g) for (g, _, _), _, _, _ in entries]
    ns, ne = len(stacks), len(entries)

    def body(*refs):
        s_refs, wmv, outs = refs[:ns], refs[ns:ns + 3 * ne], refs[ns + 3 * ne:]
        for e, ((_, r0, nr), _, _, _) in enumerate(entries):
            grad = s_refs[where[e]][r0:r0 + nr, :]
            w_ref, m_ref, v_ref = wmv[3 * e:3 * e + 3]
            g_ref, d_ref, nm_ref, nv_ref = outs[4 * e:4 * e + 4]
            g_ref[...] = grad
            d_ref[...], nm_ref[...], nv_ref[...] = _adamw(w_ref[...], grad, m_ref[...], v_ref[...])

    results = pl.pallas_call(
        body, name="adamw_small",
        out_shape=tuple(jax.ShapeDtypeStruct(w.shape, F32) for _, w, _, _ in entries for _ in range(4)),
        compiler_params=pltpu.CompilerParams(vmem_limit_bytes=VMEM_LIMIT),
    )(*stacks, *[a for _, w, m, v in entries for a in (w, m, v)])
    return [results[4 * e:4 * e + 4] for e in range(ne)]


def _mesh_pos():
    return lax.axis_index("x"), lax.axis_index("y"), lax.axis_index("c")


def _other_chips(x, y):
    return [(x, 1 - y), (1 - x, y), (1 - x, 1 - y)]


_VMEM_WHOLE = pl.BlockSpec(memory_space=pltpu.VMEM)
_HBM_WHOLE = pl.BlockSpec(memory_space=pl.ANY)


def _gather_comm(shards, cast, mid=((1, 2), (3, 4))):
    na = len(shards)
    staged = [a for a in range(na) if cast[a]]

    def phases(in_refs, out_refs, scr):
        stage = dict(zip(staged, scr[:len(staged)]))
        send_sems, recv_sems, local_sems = scr[len(staged):]
        x, y, c = _mesh_pos()
        me, sibling = (x, y, c), (x, y, 1 - c)
        chip_a, chip_b, diagonal = (x ^ c, y ^ (1 - c)), (x ^ (1 - c), y ^ c), (1 - x, 1 - y)
        srcs = [stage[a] if cast[a] else in_refs[a] for a in range(na)]

        def rows(a, pos):
            px, py, pc = pos
            return out_refs[a].at[4 * px + 2 * py + pc]

        def copy(a, k, block, to, src=None):
            return pltpu.make_async_remote_copy(
                src_ref=rows(a, block) if src is None else src, dst_ref=rows(a, block),
                send_sem=send_sems.at[a, k], recv_sem=recv_sems.at[a, k], device_id=to, device_id_type=MESH_ID)

        mine = [pltpu.make_async_copy(srcs[a], rows(a, me), local_sems.at[a]) for a in range(na)]
        own = [copy(a, k, me, to, src=srcs[a]) for a in range(na)
               for k, to in ((0, sibling), (1, (*chip_a, c)), (2, (*chip_b, c)))]
        onward = [copy(a, 3, (*chip_a, c), (*chip_b, c)) for a in range(na)]
        to_sibling = {k: [copy(a, k, (*chip, c), sibling) for a in range(na)]
                      for k, chip in ((4, chip_a), (5, chip_b), (6, diagonal))}

        def start():
            for a in staged:
                stage[a][...] = in_refs[a][...].astype(BF16)
            for cp in mine + own:
                cp.start()

        def forward_neighbours():
            for a in range(na):
                copy(a, 1, (*chip_a, c), me).wait_recv()
                onward[a].start()
                to_sibling[4][a].start()
            for a in range(na):
                copy(a, 2, (*chip_b, c), me).wait_recv()
                to_sibling[5][a].start()

        def forward_diagonal():
            for a in range(na):
                copy(a, 3, (*diagonal, c), me).wait_recv()
                to_sibling[6][a].start()

        def finish():
            for a in range(na):
                for k, chip in ((0, (x, y)), (4, chip_b), (5, chip_a), (6, diagonal)):
                    copy(a, k, (*chip, 1 - c), me).wait_recv()
            for cp in own + onward + to_sibling[4] + to_sibling[5] + to_sibling[6]:
                cp.wait_send()
            for cp in mine:
                cp.wait()

        return start, forward_neighbours, forward_diagonal, finish

    def before(step, nsteps, in_refs, out_refs, scr):
        start, forward_neighbours, forward_diagonal, _ = phases(in_refs, out_refs, scr)
        pl.when(step == 0)(start)
        pl.when(step == nsteps * mid[0][0] // mid[0][1])(forward_neighbours)
        pl.when(step == nsteps * mid[1][0] // mid[1][1])(forward_diagonal)

    def after(step, nsteps, in_refs, out_refs, scr):
        pl.when(step == nsteps - 1)(phases(in_refs, out_refs, scr)[3])

    return _Comm(
        inputs=list(shards), in_specs=[_VMEM_WHOLE] * na,
        out_shape=[jax.ShapeDtypeStruct((N_DEV,) + s.shape, BF16 if cast[a] else s.dtype)
                   for a, s in enumerate(shards)],
        out_specs=[_HBM_WHOLE] * na,
        scratch_shapes=[pltpu.VMEM(shards[a].shape, BF16) for a in staged] + [
            pltpu.SemaphoreType.DMA((na, 7)), pltpu.SemaphoreType.DMA((na, 7)), pltpu.SemaphoreType.DMA((na,))],
        before=before, after=after)


def _exchange_comm(arrays, out_shape, make_copies):
    na = len(arrays)

    def copies(in_refs, out_refs, scr):
        return make_copies(in_refs, out_refs, *scr)

    def before(step, nsteps, in_refs, out_refs, scr):
        @pl.when(step == 0)
        def _():
            for cp in copies(in_refs, out_refs, scr):
                cp.start()

    def after(step, nsteps, in_refs, out_refs, scr):
        @pl.when(step == nsteps - 1)
        def _():
            for cp in copies(in_refs, out_refs, scr):
                cp.wait()

    return _Comm(inputs=list(arrays), in_specs=[_HBM_WHOLE] * na, out_shape=list(out_shape),
                 out_specs=[_HBM_WHOLE] * na,
                 scratch_shapes=[pltpu.SemaphoreType.DMA((na, 3)), pltpu.SemaphoreType.DMA((na, 3))],
                 before=before, after=after)


def _sibling_exchange_comm(grads):
    def make_copies(in_refs, out_refs, send_sems, recv_sems):
        x, y, c = _mesh_pos()
        return [pltpu.make_async_remote_copy(
            src_ref=in_refs[a].at[:, pl.ds(1 - c, 1)], dst_ref=out_refs[a], send_sem=send_sems.at[a, 0],
            recv_sem=recv_sems.at[a, 0], device_id=(x, y, 1 - c), device_id_type=MESH_ID)
            for a in range(len(grads))]

    return _exchange_comm(grads, [jax.ShapeDtypeStruct((4, 1) + g.shape[2:], F32) for g in grads], make_copies)


def _chips_exchange_comm(partials):
    def make_copies(in_refs, out_refs, send_sems, recv_sems):
        x, y, c = _mesh_pos()
        return [pltpu.make_async_remote_copy(
            src_ref=in_refs[a].at[j], dst_ref=out_refs[a].at[j], send_sem=send_sems.at[a, j],
            recv_sem=recv_sems.at[a, j], device_id=(*chip, c), device_id_type=MESH_ID)
            for a in range(len(partials)) for j, chip in enumerate(_other_chips(x, y))]

    return _exchange_comm(partials, [jax.ShapeDtypeStruct(g.shape, BF16) for g in partials], make_copies)


def _comm_only(comms, name):
    return _fused_call(lambda: None, comms, name=name, grid=(1,), inputs=(), in_specs=[], out_specs=(),
                       out_shape=())[1]


def _chip_sum(my_pos, mine, from_sibling, tr, name):
    _, _, r, c = mine.shape

    def body(pos_ref, a_ref, b_ref, own_ref, out_ref):
        s = a_ref[0, 0] + b_ref[0, 0]

        @pl.when(pl.program_id(1) == 0)
        def _():
            own_ref[...] = s

        @pl.when(pl.program_id(1) > 0)
        def _():
            out_ref[0] = s.astype(BF16)

    grid_spec = pltpu.PrefetchScalarGridSpec(
        num_scalar_prefetch=1, grid=(r // tr, 4),
        in_specs=[pl.BlockSpec((1, 1, tr, c), lambda i, k, pos: (pos[0] ^ k, pos[1], i, 0)),
                  pl.BlockSpec((1, 1, tr, c), lambda i, k, pos: (pos[0] ^ k, 0, i, 0))],
        out_specs=(pl.BlockSpec((tr, c), lambda i, k, pos: (i, 0)),
                   pl.BlockSpec((1, tr, c), lambda i, k, pos: (jnp.maximum(k - 1, 0), i, 0))))
    return pl.pallas_call(
        body, name=name, grid_spec=grid_spec,
        out_shape=(jax.ShapeDtypeStruct((r, c), F32), jax.ShapeDtypeStruct((3, r, c), BF16)),
        compiler_params=_params(2),
    )(my_pos, mine, from_sibling)


def _all_reduce_small_comm(parts):
    na = len(parts)

    def copies(in_refs, scr):
        gathered, (send_sems, recv_sems) = scr[:na], scr[na:]
        x, y, c = _mesh_pos()
        my_id = 4 * x + 2 * y + c
        return my_id, [pltpu.make_async_remote_copy(
            src_ref=in_refs[a], dst_ref=gathered[a].at[my_id], send_sem=send_sems.at[a, k - 1],
            recv_sem=recv_sems.at[a, k - 1], device_id=(x ^ (k >> 2), y ^ ((k >> 1) & 1), c ^ (k & 1)),
            device_id_type=MESH_ID) for a in range(na) for k in range(1, N_DEV)]

    def before(step, nsteps, in_refs, out_refs, scr):
        @pl.when(step == 0)
        def _():
            for cp in copies(in_refs, scr)[1]:
                cp.start()

    def after(step, nsteps, in_refs, out_refs, scr):
        @pl.when(step == nsteps - 1)
        def _():
            my_id, cps = copies(in_refs, scr)
            for a in range(na):
                scr[a][my_id] = in_refs[a][...]
            for cp in cps:
                cp.wait()
            for a in range(na):
                acc = scr[a][0]
                for d in range(1, N_DEV):
                    acc = acc + scr[a][d]
                out_refs[a][...] = acc

    return _Comm(inputs=list(parts), in_specs=[_VMEM_WHOLE] * na,
                 out_shape=[jax.ShapeDtypeStruct(p.shape, F32) for p in parts], out_specs=[_VMEM_WHOLE] * na,
                 scratch_shapes=[pltpu.VMEM((N_DEV,) + p.shape, F32) for p in parts] + [
                     pltpu.SemaphoreType.DMA((na, N_DEV - 1)), pltpu.SemaphoreType.DMA((na, N_DEV - 1))],
                 before=before, after=after)


def _unshard_cols(g):
    return jnp.transpose(g, (1, 0, 2)).reshape(g.shape[1], N_DEV * g.shape[2])


def _row_blocks(w):
    return w.reshape(4, 2, w.shape[0] // N_DEV, w.shape[1])


def _stack_rows(parts):
    a = jnp.concatenate(parts, axis=0)
    return jnp.pad(a, ((0, (-a.shape[0]) % 8), (0, 0)))


def _w_in_grad_blocks(dw):
    return _row_blocks(jnp.concatenate([dw[:LR_REF], dw[LR_COL:LR_COL + 2 * LOWRANK], dw[LR_REF:LR_COL]], axis=0))


def _padded_decay_weights(wd_f, wd_b):
    zeros = lambda n: jnp.zeros((n, KEY_W), F32)
    return (jnp.concatenate([wd_f, zeros(LANE - LOWRANK)], axis=0),
            jnp.concatenate([zeros(LOWRANK), wd_b, zeros(LANE - 2 * LOWRANK)], axis=0))


def kernel(x, norm1_g, w_in,w_decay_f, b_decay_f, w_decay_b, b_decay_b, gla_norm_g, gmlp_ln_g, gmlp_ln_b, w_spatial, b_spatial, w_out, norm2_g, w_gate, w_up, w_down, final_norm_g, loss_target, m_norm1_g, m_w_in, m_w_decay_f, m_b_decay_f, m_w_decay_b, m_b_decay_b, m_gla_norm_g, m_gmlp_ln_g, m_gmlp_ln_b, m_w_spatial, m_b_spatial, m_w_out, m_norm2_g, m_w_gate, m_w_up, m_w_down, m_final_norm_g, v_norm1_g, v_w_in, v_w_decay_f, v_b_decay_f, v_w_decay_b, v_b_decay_b, v_gla_norm_g, v_gmlp_ln_g, v_gmlp_ln_b, v_w_spatial, v_b_spatial, v_w_out, v_norm2_g, v_w_gate, v_w_up, v_w_down, v_final_norm_g):
    t = x.shape[1]
    xt = x[0]
    target = loss_target[0]
    pos_x, pos_y, pos_c = _mesh_pos()
    my_pos = jnp.stack([2 * pos_x + pos_y, pos_c]).astype(jnp.int32)
    my_id = 4 * pos_x + 2 * pos_y + pos_c

    tile = lambda n: min(n, t)
    ln_g, ln_b, w_sp = gmlp_ln_g, gmlp_ln_b, w_spatial[0]
    b_sp_col = b_spatial[0][:, :, None]
    shard = {"w_in": w_in[0].T, "w_out": w_out[0], "w_gate": w_gate[0].T, "w_up": w_up[0].T, "w_down": w_down[0]}
    shard_m = {"w_in": m_w_in[0].T, "w_out": m_w_out[0], "w_gate": m_w_gate[0].T, "w_up": m_w_up[0].T,
               "w_down": m_w_down[0]}
    shard_v = {"w_in": v_w_in[0].T, "w_out": v_w_out[0], "w_gate": v_w_gate[0].T, "w_up": v_w_up[0].T,
               "w_down": v_w_down[0]}
    transposed = ("w_in", "w_gate", "w_up")
    chip_sum = lambda n, g, s: _chip_sum(my_pos, g, s[0], g.shape[2], "chip_sum_" + n)

    decay_shard = jnp.stack([w_decay_f[0], w_decay_b[0]])
    (hb,), ((g_in, g_decay),) = _norm1(xt, norm1_g, tile(TOKEN_TILE["norm1"]),
                                       [_gather_comm([shard["w_in"], decay_shard], [True, False])])
    w_in_t = g_in.reshape(PROJ_W, D_MODEL)
    wd_pad_f, wd_pad_b = _padded_decay_weights(_unshard_cols(g_decay[:, 0]), _unshard_cols(g_decay[:, 1]))
    (p,), ((g_gate, g_out),) = _in_proj(
        hb, w_in_t, tile(TOKEN_TILE["in_proj"]), [_gather_comm([shard["w_gate"], shard["w_out"]], [True, True])])
    (o_f, st_f, o_b, st_b), ((g_up,),) = _gla_fwd(
        p, wd_pad_f, b_decay_f, wd_pad_b, b_decay_b, tile(TOKEN_TILE["gla"]), [_gather_comm([shard["w_up"]], [True])])
    w_out_full = g_out.reshape(D_MODEL, D_MODEL)
    (x1, ycat), ((g_down,),) = _mix_fwd(xt, o_f, o_b, p, gla_norm_g, ln_g, ln_b, w_sp, b_sp_col, w_out_full,
                                        tile(TOKEN_TILE["mix_fwd"]), [_gather_comm([shard["w_down"]], [True])])

    dx1, h2b, dgate, dup, act, dx2, loss_acc, d_gf, d_g2 = _ffn(
        x1, target, norm2_g, final_norm_g[None, :], g_gate.reshape(D_FF, D_MODEL), g_up.reshape(D_FF, D_MODEL),
        g_down.reshape(D_FF, D_MODEL), tile(TOKEN_TILE["ffn"]))
    dw_gate, _ = _matmul_tn(dgate, h2b, D_FF // 2, tile(TOKEN_TILE["dw"]), "grad_w_gate")
    dw_up, _ = _matmul_tn(dup, h2b, D_FF // 2, tile(TOKEN_TILE["dw"]), "grad_w_up")
    dw_down, _ = _matmul_tn(act, dx2, D_FF // 2, tile(TOKEN_TILE["dw"]), "grad_w_down")

    ffn_grads = [_row_blocks(dw_gate), _row_blocks(dw_up), _row_blocks(dw_down)]
    (d_o, dpg, dpu, dpv, dw_out, d_gg, d_lg, d_lb, dw_sp, db_sp), (ffn_sib,) = _mix_bwd(
        dx1, ycat, o_f, o_b, p, gla_norm_g, ln_g, ln_b, w_sp, b_sp_col, w_out_full,
        tile(TOKEN_TILE["mix_bwd"]),
        [_sibling_exchange_comm(ffn_grads)])
    ffn_names = ["w_gate", "w_up", "w_down"]
    ffn_sums = [chip_sum(n, g, [s]) for n, g, s in zip(ffn_names, ffn_grads, ffn_sib)]
    out_grad = _row_blocks(dw_out)
    (dq_f, dk_f, dv_f, dlr_f, dwd_f, dbd_f, dq_b, dk_b, dv_b, dlr_b, dwd_b, dbd_b), (ffn_recv, out_sib) = _gla_bwd(
        p, wd_pad_f, b_decay_f, wd_pad_b, b_decay_b, st_f, st_b, d_o, tile(TOKEN_TILE["gla"]),
        [_chips_exchange_comm([s[1] for s in ffn_sums]), _sibling_exchange_comm([out_grad])])
    out_sum = chip_sum("w_out", out_grad, out_sib)
    (grad_x, dp, d_g1), _ = _in_proj_bwd(
        xt, norm1_g, dx1, dq_f, dq_b, dk_f, dk_b, dv_f, dv_b, dpg, dpu, dpv, dlr_f, dlr_b, w_in_t,
        tile(TOKEN_TILE["in_proj_bwd"]))

    stacks = [_stack_rows([d_g1, d_g2, d_gf]), _stack_rows([d_gg, d_lg, d_lb]),
              _stack_rows([dbd_f, dbd_b, jnp.zeros((DECAY_W_ROW - 2, KEY_W), F32), dwd_f[:LOWRANK],
                           dwd_b[LOWRANK:2 * LOWRANK]]),
              _stack_rows([dw_sp.reshape(GMLP_W, GMLP_CHUNK), db_sp[:, :, 0], loss_acc[:1]])]
    dw_main, (small_sums, out_recv) = _matmul_tn(
        dp, hb, PROJ_PAD // 3, tile(TOKEN_TILE["dw"]), "grad_w_in",
        [_all_reduce_small_comm(stacks), _chips_exchange_comm([out_sum[1]])])
    in_grad = _w_in_grad_blocks(dw_main)
    (in_sib,) = _comm_only([_sibling_exchange_comm([in_grad])], "grad_w_in_exchange_sibling")
    in_sum = chip_sum("w_in", in_grad, in_sib)
    (in_recv,) = _comm_only([_chips_exchange_comm([in_sum[1]])], "grad_w_in_exchange_chips")

    names = ["w_in", "w_out", "w_gate", "w_up", "w_down"]
    sums = [in_sum, out_sum] + ffn_sums
    received = [in_recv[0], out_recv[0]] + list(ffn_recv)
    big_out = {}
    for n, s, rc in zip(names, sums, received):
        rows = shard[n].shape[0]
        half = rows // 2 if rows % 32 == 0 else rows
        res = _adamw_shard(s[0], rc, shard[n], shard_m[n], shard_v[n], half, "adamw_" + n)
        big_out[n] = [r.T if n in transposed else r for r in res]

    s1024, s512, s256, s128 = small_sums
    loss = s128[GMLP_W + GMLP_GROUPS, 0]
    col0 = my_id * (KEY_W // N_DEV)
    decay_cols = lambda row0: lax.dynamic_slice(s256, (row0, col0), (LOWRANK, KEY_W // N_DEV))
    flat = lambda a: a.reshape(-1, a.shape[-1])
    small = {
        "norm1_g": ((s1024, 0, 1), norm1_g, m_norm1_g, v_norm1_g),
        "w_decay_f": ((decay_cols(DECAY_W_ROW), 0, LOWRANK), w_decay_f, m_w_decay_f, v_w_decay_f),
        "b_decay_f": ((s256, 0, 1), b_decay_f, m_b_decay_f, v_b_decay_f),
        "w_decay_b": ((decay_cols(DECAY_W_ROW + LOWRANK), 0, LOWRANK), w_decay_b, m_w_decay_b, v_w_decay_b),
        "b_decay_b": ((s256, 1, 1), b_decay_b, m_b_decay_b, v_b_decay_b),
        "gla_norm_g": ((s512, 0, 1), gla_norm_g, m_gla_norm_g, v_gla_norm_g),
        "gmlp_ln_g": ((s512, 1, 1), gmlp_ln_g, m_gmlp_ln_g, v_gmlp_ln_g),
        "gmlp_ln_b": ((s512, 2, 1), gmlp_ln_b, m_gmlp_ln_b, v_gmlp_ln_b),
        "w_spatial": ((s128, 0, GMLP_W), w_spatial, m_w_spatial, v_w_spatial),
        "b_spatial": ((s128, GMLP_W, GMLP_GROUPS), b_spatial, m_b_spatial, v_b_spatial),
        "norm2_g": ((s1024, 1, 1), norm2_g, m_norm2_g, v_norm2_g),
        "final_norm_g": ((s1024, 2, 1), final_norm_g, m_final_norm_g, v_final_norm_g),
    }
    small_res = _adamw_small([(g, flat(w), flat(m), flat(v)) for g, w, m, v in small.values()])
    small_out = {n: [r.reshape(small[n][1].shape) for r in res] for n, res in zip(small, small_res)}

    order = ["norm1_g", "w_in", "w_decay_f", "b_decay_f", "w_decay_b", "b_decay_b", "gla_norm_g", "gmlp_ln_g",
             "gmlp_ln_b", "w_spatial", "b_spatial", "w_out", "norm2_g", "w_gate", "w_up", "w_down", "final_norm_g"]
    outs = []
    for kind in range(4):
        for n in order:
            outs.append(big_out[n][kind][None] if n in big_out else small_out[n][kind])
    return (loss, grad_x[None], *outs)
```

```python
import functools
import math

import jax
import jax.numpy as jnp
from jax import lax
from jax.experimental import pallas as pl
from jax.experimental.pallas import tpu as pltpu

F32 = jnp.float32
BF16 = jnp.bfloat16

D_MODEL = 1024
GLA_HEADS = 4
GLA_DK = 64
GLA_DV = 128
KEY_W = GLA_HEADS * GLA_DK
VAL_W = GLA_HEADS * GLA_DV
LOWRANK = 16
GLA_TAU = 16.0
GLA_CHUNK = 64
GMLP_W = 512
GMLP_GROUPS = 4
GMLP_CHUNK = 128
D_FF = 2816
EPS = 1e-6
Q_SCALE = GLA_DK ** -0.5
PROJ_PAD = 2688
LR_COL = 2560
LANE = 128
N_DEV = 8

ADAM_LR = 0.001
ADAM_B1 = 0.9
ADAM_B2 = 0.999
ADAM_EPS = 1e-08
ADAM_WD = 0.01
ADAM_STEP = 10

VMEM_LIMIT = 56 * 1024 * 1024
TOKEN_TILE = {"norm1": 512, "in_proj": 512, "gla": 1024, "mix_fwd": 1024, "ffn": 256, "mix_bwd": 512,
              "in_proj_bwd": 512, "dw": 2048}
DECAY_W_ROW = 8
MESH_ID = pl.DeviceIdType.MESH
INV_SQRT2 = 0.7071067811865476
INV_SQRT_2PI = 0.3989422804014327


def _params(n_axes=1):
    return pltpu.CompilerParams(dimension_semantics=("arbitrary",) * n_axes, vmem_limit_bytes=VMEM_LIMIT)


def _mm(a, b):
    return jnp.dot(a.astype(BF16), b.astype(BF16), preferred_element_type=F32)


def _mm_nt(a, b):
    return lax.dot_general(a.astype(BF16), b.astype(BF16), (((1,), (1,)), ((), ())), preferred_element_type=F32)


def _mm_tn(a, b):
    return lax.dot_general(a.astype(BF16), b.astype(BF16), (((0,), (0,)), ((), ())), preferred_element_type=F32)


def _const_spec(shape):
    nd = len(shape)
    return pl.BlockSpec(shape, lambda *_: (0,) * nd, pipeline_mode=pl.Buffered(1))


def _acc_spec(shape):
    nd = len(shape)
    return pl.BlockSpec(shape, lambda *_: (0,) * nd)


class _Comm:
    def __init__(self, inputs, in_specs, out_shape, out_specs, scratch_shapes, before, after):
        self.inputs, self.in_specs, self.out_shape, self.out_specs = inputs, in_specs, out_shape, out_specs
        self.scratch_shapes, self.before, self.after = scratch_shapes, before, after


def _fused_call(body, comms, *, name, grid, inputs, in_specs, out_specs, out_shape, scratch_shapes=()):
    n_in, n_out, n_scr = len(in_specs), len(out_specs), len(scratch_shapes)
    nsteps = math.prod(grid)
    sizes = [(len(c.inputs), len(c.out_shape), len(c.scratch_shapes)) for c in comms]

    def full_body(*refs):
        step = pl.program_id(0)
        for axis in range(1, len(grid)):
            step = step * grid[axis] + pl.program_id(axis)
        ins, rest = refs[:n_in], refs[n_in:]
        c_ins = []
        for ci, _, _ in sizes:
            c_ins.append(rest[:ci])
            rest = rest[ci:]
        outs, rest = rest[:n_out], rest[n_out:]
        c_outs = []
        for _, co, _ in sizes:
            c_outs.append(rest[:co])
            rest = rest[co:]
        scr, rest = rest[:n_scr], rest[n_scr:]
        c_scr = []
        for _, _, cs in sizes:
            c_scr.append(rest[:cs])
            rest = rest[cs:]
        for c, a, b, s in zip(comms, c_ins, c_outs, c_scr):
            c.before(step, nsteps, a, b, s)
        body(*ins, *outs, *scr)
        for c, a, b, s in zip(comms, c_ins, c_outs, c_scr):
            c.after(step, nsteps, a, b, s)

    results = pl.pallas_call(
        full_body, name=name, grid=grid,
        in_specs=list(in_specs) + [s for c in comms for s in c.in_specs],
        out_specs=tuple(out_specs) + tuple(s for c in comms for s in c.out_specs),
        out_shape=tuple(out_shape) + tuple(s for c in comms for s in c.out_shape),
        scratch_shapes=list(scratch_shapes) + [s for c in comms for s in c.scratch_shapes],
        compiler_params=_params(len(grid)),
    )(*inputs, *[a for c in comms for a in c.inputs])
    own, rest = results[:n_out], results[n_out:]
    comm_results = []
    for _, co, _ in sizes:
        comm_results.append(rest[:co])
        rest = rest[co:]
    return own, comm_results


def _gelu(x):
    return 0.5 * x * (1.0 + lax.erf(x * INV_SQRT2))


def _gelu_and_grad(x):
    cdf = 0.5 * (1.0 + lax.erf(x * INV_SQRT2))
    return x * cdf, cdf + x * jnp.exp(-0.5 * x * x) * INV_SQRT_2PI


def _sigmoid(x):
    return 0.5 + 0.5 * jnp.tanh(0.5 * x)


def _silu_and_grad(x):
    s = _sigmoid(x)
    return x * s, s * (1.0 + x * (1.0 - s))


def _norm1(x, g1, tm, comms=()):
    t = x.shape[0]

    def body(x_ref, g_ref, h_ref):
        xv = x_ref[...]
        r = lax.rsqrt(jnp.mean(xv * xv, axis=-1, keepdims=True) + EPS)
        h_ref[...] = (xv * r * g_ref[...]).astype(BF16)

    row = pl.BlockSpec((tm, D_MODEL), lambda i: (i, 0))
    return _fused_call(body, comms, name="norm1", grid=(t // tm,), inputs=(x, g1),
                       in_specs=[row, _const_spec((1, D_MODEL))], out_specs=(row,),
                       out_shape=(jax.ShapeDtypeStruct((t, D_MODEL), BF16),))


PROJ_W = 2592
LR_REF = 1536
PROJ_ROWS = ((0, LR_REF, 0), (LR_REF + 2 * LOWRANK, PROJ_W, LR_REF), (LR_REF, LR_REF + LANE, LR_COL))


def _in_proj(h, w_in_t, tm, comms=()):
    t = h.shape[0]

    def body(h_ref, w_ref, p_ref):
        hv = h_ref[...]
        for r0, r1, c0 in PROJ_ROWS:
            p_ref[:, c0:c0 + r1 - r0] = _mm_nt(hv, w_ref[r0:r1, :]).astype(BF16)

    return _fused_call(
        body, comms, name="in_proj", grid=(t // tm,), inputs=(h, w_in_t),
        in_specs=[pl.BlockSpec((tm, D_MODEL), lambda i: (i, 0)), _const_spec((PROJ_W, D_MODEL))],
        out_specs=(pl.BlockSpec((tm, PROJ_PAD), lambda i: (i, 0)),),
        out_shape=(jax.ShapeDtypeStruct((t, PROJ_PAD), BF16),))


def _tri(upper):
    r = lax.broadcasted_iota(jnp.int32, (GLA_CHUNK, GLA_CHUNK), 0)
    c = lax.broadcasted_iota(jnp.int32, (GLA_CHUNK, GLA_CHUNK), 1)
    return jnp.where((c >= r) if upper else (c <= r), 1.0, 0.0).astype(BF16)


def _chunk_cumsum(tri, a, add=None):
    hi = a.astype(BF16)
    lo = (a - hi.astype(F32)).astype(BF16)
    dot = functools.partial(jnp.dot, preferred_element_type=F32)
    sums = [dot(tri, hi[_chunk_rows(c)]) + dot(tri, lo[_chunk_rows(c)]) for c in range(a.shape[0] // GLA_CHUNK)]
    return jnp.concatenate(sums if add is None else [s + r for s, r in zip(sums, add)], axis=0)


def _chunk_rows(c):
    return slice(c * GLA_CHUNK, (c + 1) * GLA_CHUNK)


def _gla_masks(rev):
    dk_bits, dv_bits = GLA_DK.bit_length() - 1, GLA_DV.bit_length() - 1
    key_head = lax.broadcasted_iota(jnp.int32, (GLA_CHUNK, KEY_W), 1) >> dk_bits
    val_head = lax.broadcasted_iota(jnp.int32, (GLA_CHUNK, VAL_W), 1) >> dv_bits
    t = lax.broadcasted_iota(jnp.int32, (GLA_HEADS * GLA_CHUNK, GLA_CHUNK), 0) & (GLA_CHUNK - 1)
    s = lax.broadcasted_iota(jnp.int32, (GLA_HEADS * GLA_CHUNK, GLA_CHUNK), 1)
    return key_head, val_head, (s >= t) if rev else (s <= t)


def _stack_heads(a, head_of_lane):
    a = a.astype(BF16)
    return jnp.concatenate([jnp.where(head_of_lane == h, a, jnp.zeros_like(a)) for h in range(GLA_HEADS)], axis=0)


def _rows_by_head(a):
    return jnp.concatenate([a[:, h * GLA_DV:(h + 1) * GLA_DV] for h in range(GLA_HEADS)], axis=0)


def _lanes_by_head(r):
    return jnp.concatenate([r[h * GLA_CHUNK:(h + 1) * GLA_CHUNK] for h in range(GLA_HEADS)], axis=1)


def _head_diagonal(r, head_of_lane):
    rows = r.shape[0] // GLA_HEADS
    out = jnp.where(head_of_lane == 0, r[:rows], 0.0)
    for h in range(1, GLA_HEADS):
        out = out + jnp.where(head_of_lane == h, r[h * rows:(h + 1) * rows], 0.0)
    return out


def _tile_terms(la, q, k, tri, rev):
    nc = la.shape[0] // GLA_CHUNK
    q, k = q.astype(F32), k.astype(F32)
    b = _chunk_cumsum(tri, la)
    ebl = [jnp.exp(b[c * GLA_CHUNK:c * GLA_CHUNK + 1] if rev else b[(c + 1) * GLA_CHUNK - 1:(c + 1) * GLA_CHUNK])
           for c in range(nc)]
    eb = jnp.exp(b)
    enb = jnp.exp(-b)
    kd = k * enb
    ke = jnp.concatenate([kd[_chunk_rows(c)] * ebl[c] for c in range(nc)], axis=0)
    return ebl, eb, enb, q * Q_SCALE * eb, kd, ke


def _log_decay(lr_ref, wd_ref, bd_ref):
    z = _mm(lr_ref[...], wd_ref[...]) + bd_ref[...]
    return z, jax.nn.log_sigmoid(z) * (1.0 / GLA_TAU)


def _p_specs(tg, tile):
    return [pl.BlockSpec((tg, KEY_W), lambda i: (tile(i), 0)),
            pl.BlockSpec((tg, KEY_W), lambda i: (tile(i), 1)),
            pl.BlockSpec((tg, VAL_W), lambda i: (tile(i), 1)),
            pl.BlockSpec((tg, LANE), lambda i: (tile(i), LR_COL // LANE))]


def _gla_fwd_dir(rev, nc, q_ref, k_ref, v_ref, lr_ref, wd_ref, bd_ref, o_ref, st_ref, state):
    key_head, _, causal = _gla_masks(rev)
    order = range(nc - 1, -1, -1) if rev else range(nc)

    def intra():
        _, la = _log_decay(lr_ref, wd_ref, bd_ref)
        ebl, _, _, qd, kd, ke = _tile_terms(la, q_ref[...], k_ref[...], _tri(rev), rev)
        kd = kd.astype(BF16)
        v = {c: v_ref[_chunk_rows(c), :].astype(BF16) for c in order}
        qd_stack = {c: _stack_heads(qd[_chunk_rows(c)], key_head) for c in order}
        ke_stack = {c: _stack_heads(ke[_chunk_rows(c)], key_head) for c in order}
        a_all = {c: _mm_nt(qd_stack[c], kd[_chunk_rows(c)]) for c in order}
        a_all = {c: jnp.where(causal, a_all[c], 0.0).astype(BF16) for c in order}
        head_rows = lambda a, h: a[h * GLA_CHUNK:(h + 1) * GLA_CHUNK]
        head_vals = lambda a, h: a[:, h * GLA_DV:(h + 1) * GLA_DV]
        r = {c: [_mm(head_rows(a_all[c], h), head_vals(v[c], h)) for h in range(GLA_HEADS)] for c in order}
        upd = {c: _mm_tn(_rows_by_head(v[c]), ke_stack[c]) for c in order}
        return {c: (ebl[c], qd_stack[c], r[c], upd[c]) for c in order}

    def scan(terms):
        st = state[...]
        states = {}
        for c in order:
            states[c] = st
            st_ref[c] = st.astype(BF16)
            st = st * terms[c][0] + terms[c][3]
        state[...] = st
        return states

    def inter(terms, states):
        r_inter = {c: _mm_nt(terms[c][1], states[c]) for c in order}
        for c in order:
            o_ref[_chunk_rows(c), :] = jnp.concatenate(
                [terms[c][2][h] + r_inter[c][h * GLA_CHUNK:(h + 1) * GLA_CHUNK] for h in range(GLA_HEADS)], axis=1)

    return intra, scan, inter


def _gla_fwd(p, wd_pad_f, bd_f, wd_pad_b, bd_b, tg, comms=()):
    t = p.shape[0]
    nt = t // tg
    nc = tg // GLA_CHUNK
    up, down = (lambda i: i), (lambda i: nt - 1 - i)

    def body(qf, kf, vf, lrf, qb, kb, vb, lrb, wdf, bdf, wdb, bdb, of, stf, ob, stb, state_f, state_b):
        @pl.when(pl.program_id(0) == 0)
        def _():
            state_f[...] = jnp.zeros_like(state_f)
            state_b[...] = jnp.zeros_like(state_b)

        dirs = [_gla_fwd_dir(False, nc, qf, kf, vf, lrf, wdf, bdf, of, stf, state_f),
                _gla_fwd_dir(True, nc, qb, kb, vb, lrb, wdb, bdb, ob, stb, state_b)]
        terms = [intra() for intra, _, _ in dirs]
        states = [scan(t) for (_, scan, _), t in zip(dirs, terms)]
        for (_, _, inter), t, s in zip(dirs, terms, states):
            inter(t, s)

    wd_spec, bd_spec = _const_spec((LANE, KEY_W)), _const_spec((1, KEY_W))
    outs = lambda tile: (pl.BlockSpec((tg, VAL_W), lambda i: (tile(i), 0)),
                         pl.BlockSpec((nc, GLA_DV, KEY_W), lambda i: (tile(i), 0, 0)))
    out_shape = (jax.ShapeDtypeStruct((t, VAL_W), F32), jax.ShapeDtypeStruct((t // GLA_CHUNK, GLA_DV, KEY_W), BF16))
    return _fused_call(
        body, comms, name="gla_fwd", grid=(nt,), inputs=(p,) * 8 + (wd_pad_f, bd_f, wd_pad_b, bd_b),
        in_specs=_p_specs(tg, up) + _p_specs(tg, down) + [wd_spec, bd_spec, wd_spec, bd_spec],
        out_specs=outs(up) + outs(down), out_shape=out_shape * 2,
        scratch_shapes=[pltpu.VMEM((GLA_DV, KEY_W), F32)] * 2)


def _gla_bwd_dir(rev, nc, q_ref, k_ref, v_ref, lr_ref, wd_ref, bd_ref, st_ref, do_ref,
                 dq_ref, dk_ref, dv_ref, dlr_ref, dwd_ref, dbd_ref, dstate):
    key_head, val_head, causal = _gla_masks(rev)
    order = range(nc) if rev else range(nc - 1, -1, -1)

    def intra():
        z, la = _log_decay(lr_ref, wd_ref, bd_ref)
        tile = _tile_terms(la, q_ref[...], k_ref[...], _tri(rev), rev)
        qd, kd = tile[3], tile[4].astype(BF16)
        v = {c: v_ref[_chunk_rows(c), :].astype(BF16) for c in order}
        d_o = {c: do_ref[_chunk_rows(c), :] for c in order}
        kd_c = {c: kd[_chunk_rows(c)] for c in order}
        qd_stack = {c: _stack_heads(qd[_chunk_rows(c)], key_head) for c in order}
        do_stack = {c: _stack_heads(d_o[c], val_head) for c in order}
        do_rows = {c: _rows_by_head(d_o[c]) for c in order}
        a_all = {c: _mm_nt(qd_stack[c], kd_c[c]) for c in order}
        head_vals = lambda a, h: a[:, h * GLA_DV:(h + 1) * GLA_DV]
        da_all = {c: jnp.concatenate([_mm_nt(head_vals(d_o[c], h), head_vals(v[c], h)) for h in range(GLA_HEADS)],
                                     axis=0) for c in order}
        a_all = {c: jnp.where(causal, a_all[c], 0.0).astype(BF16) for c in order}
        da_all = {c: jnp.where(causal, da_all[c], 0.0).astype(BF16) for c in order}
        dv = {c: _mm_tn(a_all[c], do_stack[c]) for c in order}
        dqd = {c: _mm(jnp.concatenate([do_rows[c], da_all[c]], axis=1),
                      jnp.concatenate([st_ref[c], kd_c[c]], axis=0)) for c in order}
        dkd = {c: _mm_tn(da_all[c], qd_stack[c]) for c in order}
        upd = {c: _mm_tn(do_rows[c], qd_stack[c]) for c in order}
        dqd = {c: _head_diagonal(dqd[c], key_head) for c in order}
        return z, tile, {c: dict(dv=dv[c], dqd=dqd[c], dkd=dkd[c], upd=upd[c]) for c in order}

    def scan(tile, per):
        dst = dstate[...]
        dsts = {}
        for c in order:
            dsts[c] = dst
            dst = dst * tile[0][c] + per[c]["upd"]
        dstate[...] = dst
        return dsts

    def inter(z, tile, per, dsts):
        ebl, eb, enb, qd, kd, ke = tile
        ke_stack = {c: _stack_heads(ke[_chunk_rows(c)], key_head) for c in order}
        v_rows = {c: _rows_by_head(v_ref[_chunk_rows(c), :].astype(BF16)) for c in order}
        dst_b = {c: dsts[c].astype(BF16) for c in order}
        dv_state = {c: _mm_nt(ke_stack[c], dst_b[c]) for c in order}
        dke_c = {c: _mm(v_rows[c], dst_b[c]) for c in order}
        dke_c = {c: _head_diagonal(dke_c[c], key_head) for c in order}
        dbl_c = {}
        for c in order:
            rows = _chunk_rows(c)
            dv_ref[rows, :] = (per[c]["dv"] + _lanes_by_head(dv_state[c])).astype(BF16)
            dbl_c[c] = (jnp.sum(dsts[c] * st_ref[c].astype(F32), axis=0, keepdims=True) * ebl[c]
                        + jnp.sum(dke_c[c] * ke[rows], axis=0, keepdims=True))
        tile_of = lambda parts: jnp.concatenate([parts[c] for c in range(nc)], axis=0)
        dqd, dkd = tile_of({c: per[c]["dqd"] for c in order}), tile_of({c: per[c]["dkd"] for c in order})
        dke = tile_of(dke_c)
        dke_end = tile_of({c: dke_c[c] * ebl[c] for c in order})
        dq_ref[...] = (dqd * eb * Q_SCALE).astype(BF16)
        dk_ref[...] = ((dkd + dke_end) * enb).astype(BF16)
        db = dqd * qd - dkd * kd - dke * ke
        dla = _chunk_cumsum(_tri(not rev), db, [dbl_c[c] for c in range(nc)])
        dz = dla * (_sigmoid(-z) * (1.0 / GLA_TAU))
        dlr_ref[...] = _mm_nt(dz, wd_ref[...]).astype(BF16)
        dwd_ref[...] += _mm_tn(lr_ref[...], dz)
        dbd_ref[...] += jnp.sum(dz, axis=0, keepdims=True)

    return intra, scan, inter


def _gla_bwd(p, wd_pad_f, bd_f, wd_pad_b, bd_b, st_f, st_b, d_o, tg, comms=()):
    t = p.shape[0]
    nt = t // tg
    nc = tg // GLA_CHUNK
    up, down = (lambda i: i), (lambda i: nt - 1 - i)

    def body(qf, kf, vf, lrf, stf, dof, qb, kb, vb, lrb, stb, dob, wdf, bdf, wdb, bdb,
             dqf, dkf, dvf, dlrf, dwdf, dbdf, dqb, dkb, dvb, dlrb, dwdb, dbdb, dstate_f, dstate_b):
        @pl.when(pl.program_id(0) == 0)
        def _():
            for ref in (dstate_f, dstate_b, dwdf, dbdf, dwdb, dbdb):
                ref[...] = jnp.zeros_like(ref)

        dirs = [_gla_bwd_dir(False, nc, qf, kf, vf, lrf, wdf, bdf, stf, dof, dqf, dkf, dvf, dlrf, dwdf, dbdf,
                             dstate_f),
                _gla_bwd_dir(True, nc, qb, kb, vb, lrb, wdb, bdb, stb, dob, dqb, dkb, dvb, dlrb, dwdb, dbdb,
                             dstate_b)]
        first = [intra() for intra, _, _ in dirs]
        dsts = [scan(tile, per) for (_, scan, _), (_, tile, per) in zip(dirs, first)]
        for (_, _, inter), (z, tile, per), d in zip(dirs, first, dsts):
            inter(z, tile, per, d)

    wd_spec, bd_spec = _const_spec((LANE, KEY_W)), _const_spec((1, KEY_W))
    ins = lambda tile: _p_specs(tg, tile) + [pl.BlockSpec((nc, GLA_DV, KEY_W), lambda i: (tile(i), 0, 0)),
                                             pl.BlockSpec((tg, VAL_W), lambda i: (tile(i), 0))]
    outs = lambda tile: (pl.BlockSpec((tg, KEY_W), lambda i: (tile(i), 0)),
                         pl.BlockSpec((tg, KEY_W), lambda i: (tile(i), 0)),
                         pl.BlockSpec((tg, VAL_W), lambda i: (tile(i), 0)),
                         pl.BlockSpec((tg, LANE), lambda i: (tile(i), 0)),
                         _acc_spec((LANE, KEY_W)), _acc_spec((1, KEY_W)))
    out_shape = (jax.ShapeDtypeStruct((t, KEY_W), BF16), jax.ShapeDtypeStruct((t, KEY_W), BF16),
                 jax.ShapeDtypeStruct((t, VAL_W), BF16), jax.ShapeDtypeStruct((t, LANE), BF16),
                 jax.ShapeDtypeStruct((LANE, KEY_W), F32), jax.ShapeDtypeStruct((1, KEY_W), F32))
    scratch = [pltpu.VMEM((GLA_DV, KEY_W), F32)]
    return _fused_call(
        body, comms, name="gla_bwd", grid=(nt,),
        inputs=(p, p, p, p, st_f, d_o, p, p, p, p, st_b, d_o, wd_pad_f, bd_f, wd_pad_b, bd_b),
        in_specs=ins(down) + ins(up) + [wd_spec, bd_spec, wd_spec, bd_spec],
        out_specs=outs(down) + outs(up), out_shape=out_shape * 2, scratch_shapes=scratch * 2)


def _head_rms(o):
    parts, scales = [], []
    for h in range(GLA_HEADS):
        oh = o[:, h * GLA_DV:(h + 1) * GLA_DV]
        r = lax.rsqrt(jnp.mean(oh * oh, axis=-1, keepdims=True) + EPS)
        parts.append(oh * r)
        scales.append(jnp.broadcast_to(r, oh.shape))
    return jnp.concatenate(parts, axis=1), jnp.concatenate(scales, axis=1)


def _layernorm_stats(zv):
    mu = jnp.mean(zv, axis=-1, keepdims=True)
    xc = zv - mu
    rs = lax.rsqrt(jnp.mean(xc * xc, axis=-1, keepdims=True) + EPS)
    return xc * rs, rs


def _mix_fwd(x, o_f, o_b, p, gla_g, ln_g, ln_b, w_sp, b_sp, w_out, tm, comms=()):
    t = x.shape[0]
    nch = tm // GMLP_CHUNK

    def body(x_ref, of_ref, ob_ref, pg_ref, pu_ref, pv_ref, gg_ref, lg_ref, lb_ref, ws_ref, bs_ref, wo_ref,
             x1_ref, y_ref, s_scr):
        on, _ = _head_rms(of_ref[...] + ob_ref[...])
        pg = pg_ref[...].astype(F32)
        y_a = on * gg_ref[...] * (pg * _sigmoid(pg))
        zu = _gelu(pu_ref[...].astype(F32))
        vhat, _ = _layernorm_stats(_gelu(pv_ref[...].astype(F32)))
        vln = (vhat * lg_ref[...] + lb_ref[...]).astype(BF16)
        for g in range(GMLP_GROUPS):
            w_g = ws_ref[g].astype(BF16)
            b_g = bs_ref[g]
            cols = slice(g * LANE, (g + 1) * LANE)
            for n in range(nch):
                rows = slice(n * GMLP_CHUNK, (n + 1) * GMLP_CHUNK)
                s_scr[rows, cols] = jnp.dot(w_g, vln[rows, cols], preferred_element_type=F32) + b_g
        ycat = jnp.concatenate([y_a, zu * s_scr[...]], axis=1).astype(BF16)
        y_ref[...] = ycat
        x1_ref[...] = x_ref[...] + jnp.dot(ycat, wo_ref[...], preferred_element_type=F32)

    half = lambda j: pl.BlockSpec((tm, VAL_W), lambda i: (i, j))
    return _fused_call(
        body, comms, name="mix_fwd", grid=(t // tm,),
        inputs=(x, o_f, o_b, p, p, p, gla_g, ln_g, ln_b, w_sp, b_sp, w_out),
        in_specs=[pl.BlockSpec((tm, D_MODEL), lambda i: (i, 0)), half(0), half(0), half(2), half(3), half(4),
                  _const_spec((1, VAL_W)), _const_spec((1, GMLP_W)), _const_spec((1, GMLP_W)),
                  _const_spec((GMLP_GROUPS, GMLP_CHUNK, GMLP_CHUNK)), _const_spec((GMLP_GROUPS, GMLP_CHUNK, 1)),
                  _const_spec((D_MODEL, D_MODEL))],
        out_specs=(pl.BlockSpec((tm, D_MODEL), lambda i: (i, 0)), pl.BlockSpec((tm, D_MODEL), lambda i: (i, 0))),
        out_shape=(jax.ShapeDtypeStruct((t, D_MODEL), F32), jax.ShapeDtypeStruct((t, D_MODEL), BF16)),
        scratch_shapes=[pltpu.VMEM((tm, GMLP_W), F32)])


def _mix_bwd(dx1, ycat, o_f, o_b, p, gla_g, ln_g, ln_b, w_sp, b_sp, w_out, tm, comms=()):
    t = dx1.shape[0]
    nch = tm // GMLP_CHUNK

    def body(dx1_ref, y_ref, of_ref, ob_ref, pg_ref, pu_ref, pv_ref, gg_ref, lg_ref, lb_ref, ws_ref, bs_ref, wo_ref,
             do_ref, dpg_ref, dpu_ref, dpv_ref, dwo_ref, dgg_ref, dlg_ref, dlb_ref, dws_ref, dbs_ref,
             s_scr, dvln_scr):
        @pl.when(pl.program_id(0) == 0)
        def _():
            for ref in (dwo_ref, dgg_ref, dlg_ref, dlb_ref, dws_ref, dbs_ref):
                ref[...] = jnp.zeros_like(ref)

        dx1 = dx1_ref[...].astype(BF16)
        dycat = _mm_nt(dx1, wo_ref[...])
        dwo_ref[...] += _mm_tn(y_ref[...], dx1)
        dy_a = dycat[:, :VAL_W]
        dy_b = dycat[:, VAL_W:]
        on, r = _head_rms(of_ref[...] + ob_ref[...])
        pg = pg_ref[...].astype(F32)
        sil, dsil = _silu_and_grad(pg)
        gg = gg_ref[...]
        dgg_ref[...] += jnp.sum(dy_a * sil * on, axis=0, keepdims=True)
        don = dy_a * sil * gg
        prod = don * on
        means = jnp.concatenate(
            [jnp.broadcast_to(jnp.mean(prod[:, h * GLA_DV:(h + 1) * GLA_DV], axis=-1, keepdims=True),
                              (tm, GLA_DV)) for h in range(GLA_HEADS)], axis=1)
        do_ref[...] = (r * (don - on * means)).astype(BF16)
        dpg_ref[...] = (dy_a * on * gg * dsil).astype(BF16)
        pu = pu_ref[...].astype(F32)
        pv = pv_ref[...].astype(F32)
        zu, dzu_dpu = _gelu_and_grad(pu)
        zv, dzv_dpv = _gelu_and_grad(pv)
        vhat, rs = _layernorm_stats(zv)
        lg = lg_ref[...]
        vln = (vhat * lg + lb_ref[...]).astype(BF16)
        ds32 = dy_b * zu
        ds = ds32.astype(BF16)
        blocks = [(g, n) for g in range(GMLP_GROUPS) for n in range(nch)]
        at = lambda g, n: (slice(n * GMLP_CHUNK, (n + 1) * GMLP_CHUNK), slice(g * LANE, (g + 1) * LANE))
        w_sp = [ws_ref[g].astype(BF16) for g in range(GMLP_GROUPS)]
        v_blk = {b: vln[at(*b)] for b in blocks}
        ds_blk = {b: ds[at(*b)] for b in blocks}
        s_blk = {b: jnp.dot(w_sp[b[0]], v_blk[b], preferred_element_type=F32) for b in blocks}
        dw_blk = {b: _mm_nt(ds_blk[b], v_blk[b]) for b in blocks}
        dvln_blk = {b: _mm_tn(w_sp[b[0]], ds_blk[b]) for b in blocks}
        for b in blocks:
            s_scr[at(*b)] = s_blk[b] + bs_ref[b[0]]
            dvln_scr[at(*b)] = dvln_blk[b]
        for g in range(GMLP_GROUPS):
            dws_ref[g] += sum(dw_blk[(g, n)] for n in range(nch))
            dbs_ref[g] += sum(jnp.sum(ds32[at(g, n)], axis=-1, keepdims=True) for n in range(nch))
        dpu_ref[...] = (dy_b * s_scr[...] * dzu_dpu).astype(BF16)
        dvln = dvln_scr[...]
        dlg_ref[...] += jnp.sum(dvln * vhat, axis=0, keepdims=True)
        dlb_ref[...] += jnp.sum(dvln, axis=0, keepdims=True)
        dvhat = dvln * lg
        dzv = rs * (dvhat - jnp.mean(dvhat, axis=-1, keepdims=True)
                    - vhat * jnp.mean(dvhat * vhat, axis=-1, keepdims=True))
        dpv_ref[...] = (dzv * dzv_dpv).astype(BF16)

    half = lambda j: pl.BlockSpec((tm, VAL_W), lambda i: (i, j))
    full = pl.BlockSpec((tm, D_MODEL), lambda i: (i, 0))
    sp_shape = (GMLP_GROUPS, GMLP_CHUNK, GMLP_CHUNK)
    bs_shape = (GMLP_GROUPS, GMLP_CHUNK, 1)
    return _fused_call(
        body, comms, name="mix_bwd", grid=(t // tm,),
        inputs=(dx1, ycat, o_f, o_b, p, p, p, gla_g, ln_g, ln_b, w_sp, b_sp, w_out),
        in_specs=[full, full, half(0), half(0), half(2), half(3), half(4),
                  _const_spec((1, VAL_W)), _const_spec((1, GMLP_W)), _const_spec((1, GMLP_W)),
                  _const_spec(sp_shape), _const_spec(bs_shape), _const_spec((D_MODEL, D_MODEL))],
        out_specs=(half(0), half(0), half(0), half(0), _acc_spec((D_MODEL, D_MODEL)), _acc_spec((1, VAL_W)),
                   _acc_spec((1, GMLP_W)), _acc_spec((1, GMLP_W)), _acc_spec(sp_shape), _acc_spec(bs_shape)),
        out_shape=(jax.ShapeDtypeStruct((t, VAL_W), BF16),) * 4 + (
            jax.ShapeDtypeStruct((D_MODEL, D_MODEL), F32), jax.ShapeDtypeStruct((1, VAL_W), F32),
            jax.ShapeDtypeStruct((1, GMLP_W), F32), jax.ShapeDtypeStruct((1, GMLP_W), F32),
            jax.ShapeDtypeStruct(sp_shape, F32), jax.ShapeDtypeStruct(bs_shape, F32)),
        scratch_shapes=[pltpu.VMEM((tm, GMLP_W), F32), pltpu.VMEM((tm, GMLP_W), F32)])


def _rms_bwd(dy_scaled, xn, r):
    return r * (dy_scaled - xn * jnp.mean(dy_scaled * xn, axis=-1, keepdims=True))


def _ffn(x1, target, g2, gf, w_gate, w_up, w_down, tm):
    t = x1.shape[0]

    def body(x1_ref, tg_ref, g2_ref, gf_ref, wg_ref, wu_ref, wd_ref,
             dx1_ref, h2_ref, dgate_ref, dup_ref, act_ref, dx2_ref, loss_ref, dgf_ref, dg2_ref):
        @pl.when(pl.program_id(0) == 0)
        def _():
            for ref in (loss_ref, dgf_ref, dg2_ref):
                ref[...] = jnp.zeros_like(ref)

        x1v = x1_ref[...]
        g2v = g2_ref[...]
        gfv = gf_ref[...]
        r2 = lax.rsqrt(jnp.mean(x1v * x1v, axis=-1, keepdims=True) + EPS)
        xn1 = x1v * r2
        h2 = (xn1 * g2v).astype(BF16)
        h2_ref[...] = h2
        gate = _mm_nt(h2, wg_ref[...])
        up = _mm_nt(h2, wu_ref[...])
        sil, dsil = _silu_and_grad(gate)
        act = (sil * up).astype(BF16)
        act_ref[...] = act
        x2 = x1v + jnp.dot(act, wd_ref[...], preferred_element_type=F32)
        rf = lax.rsqrt(jnp.mean(x2 * x2, axis=-1, keepdims=True) + EPS)
        xn2 = x2 * rf
        err = xn2 * gfv - tg_ref[...]
        loss_ref[...] += 0.5 * jnp.sum(jnp.mean(err * err, axis=-1, keepdims=True))
        dy = err * (1.0 / D_MODEL)
        dgf_ref[...] += jnp.sum(dy * xn2, axis=0, keepdims=True)
        dx2 = _rms_bwd(dy * gfv, xn2, rf)
        dx2b = dx2.astype(BF16)
        dx2_ref[...] = dx2b
        dact = _mm_nt(dx2b, wd_ref[...])
        dgate = (dact * up * dsil).astype(BF16)
        dup = (dact * sil).astype(BF16)
        dgate_ref[...] = dgate
        dup_ref[...] = dup
        dh2 = _mm(dgate, wg_ref[...]) + _mm(dup, wu_ref[...])
        dg2_ref[...] += jnp.sum(dh2 * xn1, axis=0, keepdims=True)
        dx1_ref[...] = dx2 + _rms_bwd(dh2 * g2v, xn1, r2)

    row = lambda w: pl.BlockSpec((tm, w), lambda i: (i, 0))
    return pl.pallas_call(
        body, name="ffn_fwd_bwd", grid=(t // tm,),
        in_specs=[row(D_MODEL), row(D_MODEL), _const_spec((1, D_MODEL)), _const_spec((1, D_MODEL)),
                  _const_spec((D_FF, D_MODEL)), _const_spec((D_FF, D_MODEL)), _const_spec((D_FF, D_MODEL))],
        out_specs=(row(D_MODEL), row(D_MODEL), row(D_FF), row(D_FF), row(D_FF), row(D_MODEL),
                   _acc_spec((8, LANE)), _acc_spec((1, D_MODEL)), _acc_spec((1, D_MODEL))),
        out_shape=(jax.ShapeDtypeStruct((t, D_MODEL), F32), jax.ShapeDtypeStruct((t, D_MODEL), BF16),
                   jax.ShapeDtypeStruct((t, D_FF), BF16), jax.ShapeDtypeStruct((t, D_FF), BF16),
                   jax.ShapeDtypeStruct((t, D_FF), BF16), jax.ShapeDtypeStruct((t, D_MODEL), BF16),
                   jax.ShapeDtypeStruct((8, LANE), F32), jax.ShapeDtypeStruct((1, D_MODEL), F32),
                   jax.ShapeDtypeStruct((1, D_MODEL), F32)),
        compiler_params=_params(),
    )(x1, target, g2, gf, w_gate, w_up, w_down)


def _matmul_tn(a, b, tm, tk, name, comms=()):
    t, m = a.shape
    n = b.shape[1]

    def body(a_ref, b_ref, o_ref):
        @pl.when(pl.program_id(1) == 0)
        def _():
            o_ref[...] = jnp.zeros_like(o_ref)

        o_ref[...] += _mm_tn(a_ref[...], b_ref[...])

    (out,), comm_results = _fused_call(
        body, comms, name=name, grid=(m // tm, t // tk), inputs=(a, b),
        in_specs=[pl.BlockSpec((tk, tm), lambda j, k: (k, j)), pl.BlockSpec((tk, n), lambda j, k: (k, 0))],
        out_specs=(pl.BlockSpec((tm, n), lambda j, k: (j, 0)),),
        out_shape=(jax.ShapeDtypeStruct((m, n), F32),))
    return out, comm_results


def _in_proj_bwd(x, g1, dx1, dq_f, dq_b, dk_f, dk_b, dv_f, dv_b, dpg, dpu, dpv, dlr_f, dlr_b, w_main, tm, comms=()):
    t = x.shape[0]

    def body(x_ref, g_ref, dx1_ref, dqf, dqb, dkf, dkb, dvf, dvb, dg, du, dv, dlf, dlb, w_ref,
             dx_ref, dp_ref, dg1_ref):
        @pl.when(pl.program_id(0) == 0)
        def _():
            dg1_ref[...] = jnp.zeros_like(dg1_ref)

        both = lambda a, b: (a[...].astype(F32) + b[...].astype(F32)).astype(BF16)
        dp = jnp.concatenate([both(dqf, dqb), both(dkf, dkb), both(dvf, dvb), dg[...], du[...], dv[...],
                              both(dlf, dlb)], axis=1)
        dp_ref[...] = dp
        dh = sum(_mm(dp[:, c0:c0 + r1 - r0], w_ref[r0:r1, :]) for r0, r1, c0 in PROJ_ROWS)
        xv = x_ref[...]
        r = lax.rsqrt(jnp.mean(xv * xv, axis=-1, keepdims=True) + EPS)
        xn = xv * r
        dg1_ref[...] += jnp.sum(dh * xn, axis=0, keepdims=True)
        dx_ref[...] = dx1_ref[...] + _rms_bwd(dh * g_ref[...], xn, r)

    row = lambda w: pl.BlockSpec((tm, w), lambda i: (i, 0))
    return _fused_call(
        body, comms, name="in_proj_bwd", grid=(t // tm,),
        inputs=(x, g1, dx1, dq_f, dq_b, dk_f, dk_b, dv_f, dv_b, dpg, dpu, dpv, dlr_f, dlr_b, w_main),
        in_specs=[row(D_MODEL), _const_spec((1, D_MODEL)), row(D_MODEL), row(KEY_W), row(KEY_W), row(KEY_W),
                  row(KEY_W), row(VAL_W), row(VAL_W), row(VAL_W), row(VAL_W), row(VAL_W), row(LANE), row(LANE),
                  _const_spec((PROJ_W, D_MODEL))],
        out_specs=(row(D_MODEL), row(PROJ_PAD), _acc_spec((1, D_MODEL))),
        out_shape=(jax.ShapeDtypeStruct((t, D_MODEL), F32), jax.ShapeDtypeStruct((t, PROJ_PAD), BF16),
                   jax.ShapeDtypeStruct((1, D_MODEL), F32)))


def _adamw(w, g, m, v):
    m_new = ADAM_B1 * m + (1.0 - ADAM_B1) * g
    v_new = ADAM_B2 * v + (1.0 - ADAM_B2) * (g * g)
    m_hat = m_new / (1.0 - ADAM_B1 ** ADAM_STEP)
    v_hat = v_new / (1.0 - ADAM_B2 ** ADAM_STEP)
    delta = -ADAM_LR * (m_hat / (jnp.sqrt(v_hat) + ADAM_EPS) + ADAM_WD * w)
    return delta, m_new, v_new


def _adamw_shard(own, recv, w, m, v, tr, name, comms=()):
    r, c = w.shape

    def body(own_ref, recv_ref, w_ref, m_ref, v_ref, g_ref, d_ref, nm_ref, nv_ref):
        g = own_ref[...]
        for k in range(3):
            g = g + recv_ref[k].astype(F32)
        g_ref[...] = g
        d_ref[...], nm_ref[...], nv_ref[...] = _adamw(w_ref[...], g, m_ref[...], v_ref[...])

    row = pl.BlockSpec((tr, c), lambda i: (i, 0))
    return _fused_call(
        body, comms, name=name, grid=(r // tr,), inputs=(own, recv, w, m, v),
        in_specs=[row, pl.BlockSpec((3, tr, c), lambda i: (0, i, 0)), row, row, row],
        out_specs=(row,) * 4, out_shape=(jax.ShapeDtypeStruct((r, c), F32),) * 4)


def _adamw_small(entries):
    stacks = []
    for (g, _, _), _, _, _ in entries:
        if not any(g is s for s in stacks):
            stacks.append(g)
    where = [next(i for i, s in enumerate(stacks) if s is g) for (g, _, _), _, _, _ in entries]
    ns, ne = len(stacks), len(entries)

    def body(*refs):
        s_refs, wmv, outs = refs[:ns], refs[ns:ns + 3 * ne], refs[ns + 3 * ne:]
        for e, ((_, r0, nr), _, _, _) in enumerate(entries):
            grad = s_refs[where[e]][r0:r0 + nr, :]
            w_ref, m_ref, v_ref = wmv[3 * e:3 * e + 3]
            g_ref, d_ref, nm_ref, nv_ref = outs[4 * e:4 * e + 4]
            g_ref[...] = grad
            d_ref[...], nm_ref[...], nv_ref[...] = _adamw(w_ref[...], grad, m_ref[...], v_ref[...])

    results = pl.pallas_call(
        body, name="adamw_small",
        out_shape=tuple(jax.ShapeDtypeStruct(w.shape, F32) for _, w, _, _ in entries for _ in range(4)),
        compiler_params=pltpu.CompilerParams(vmem_limit_bytes=VMEM_LIMIT),
    )(*stacks, *[a for _, w, m, v in entries for a in (w, m, v)])
    return [results[4 * e:4 * e + 4] for e in range(ne)]


def _mesh_pos():
    return lax.axis_index("x"), lax.axis_index("y"), lax.axis_index("c")


def _other_chips(x, y):
    return [(x, 1 - y), (1 - x, y), (1 - x, 1 - y)]


_VMEM_WHOLE = pl.BlockSpec(memory_space=pltpu.VMEM)
_HBM_WHOLE = pl.BlockSpec(memory_space=pl.ANY)


def _gather_comm(shards, cast, mid=((1, 2), (3, 4))):
    na = len(shards)
    staged = [a for a in range(na) if cast[a]]

    def phases(in_refs, out_refs, scr):
        stage = dict(zip(staged, scr[:len(staged)]))
        send_sems, recv_sems, local_sems = scr[len(staged):]
        x, y, c = _mesh_pos()
        me, sibling = (x, y, c), (x, y, 1 - c)
        chip_a, chip_b, diagonal = (x ^ c, y ^ (1 - c)), (x ^ (1 - c), y ^ c), (1 - x, 1 - y)
        srcs = [stage[a] if cast[a] else in_refs[a] for a in range(na)]

        def rows(a, pos):
            px, py, pc = pos
            return out_refs[a].at[4 * px + 2 * py + pc]

        def copy(a, k, block, to, src=None):
            return pltpu.make_async_remote_copy(
                src_ref=rows(a, block) if src is None else src, dst_ref=rows(a, block),
                send_sem=send_sems.at[a, k], recv_sem=recv_sems.at[a, k], device_id=to, device_id_type=MESH_ID)

        mine = [pltpu.make_async_copy(srcs[a], rows(a, me), local_sems.at[a]) for a in range(na)]
        own = [copy(a, k, me, to, src=srcs[a]) for a in range(na)
               for k, to in ((0, sibling), (1, (*chip_a, c)), (2, (*chip_b, c)))]
        onward = [copy(a, 3, (*chip_a, c), (*chip_b, c)) for a in range(na)]
        to_sibling = {k: [copy(a, k, (*chip, c), sibling) for a in range(na)]
                      for k, chip in ((4, chip_a), (5, chip_b), (6, diagonal))}

        def start():
            for a in staged:
                stage[a][...] = in_refs[a][...].astype(BF16)
            for cp in mine + own:
                cp.start()

        def forward_neighbours():
            for a in range(na):
                copy(a, 1, (*chip_a, c), me).wait_recv()
                onward[a].start()
                to_sibling[4][a].start()
            for a in range(na):
                copy(a, 2, (*chip_b, c), me).wait_recv()
                to_sibling[5][a].start()

        def forward_diagonal():
            for a in range(na):
                copy(a, 3, (*diagonal, c), me).wait_recv()
                to_sibling[6][a].start()

        def finish():
            for a in range(na):
                for k, chip in ((0, (x, y)), (4, chip_b), (5, chip_a), (6, diagonal)):
                    copy(a, k, (*chip, 1 - c), me).wait_recv()
            for cp in own + onward + to_sibling[4] + to_sibling[5] + to_sibling[6]:
                cp.wait_send()
            for cp in mine:
                cp.wait()

        return start, forward_neighbours, forward_diagonal, finish

    def before(step, nsteps, in_refs, out_refs, scr):
        start, forward_neighbours, forward_diagonal, _ = phases(in_refs, out_refs, scr)
        pl.when(step == 0)(start)
        pl.when(step == nsteps * mid[0][0] // mid[0][1])(forward_neighbours)
        pl.when(step == nsteps * mid[1][0] // mid[1][1])(forward_diagonal)

    def after(step, nsteps, in_refs, out_refs, scr):
        pl.when(step == nsteps - 1)(phases(in_refs, out_refs, scr)[3])

    return _Comm(
        inputs=list(shards), in_specs=[_VMEM_WHOLE] * na,
        out_shape=[jax.ShapeDtypeStruct((N_DEV,) + s.shape, BF16 if cast[a] else s.dtype)
                   for a, s in enumerate(shards)],
        out_specs=[_HBM_WHOLE] * na,
        scratch_shapes=[pltpu.VMEM(shards[a].shape, BF16) for a in staged] + [
            pltpu.SemaphoreType.DMA((na, 7)), pltpu.SemaphoreType.DMA((na, 7)), pltpu.SemaphoreType.DMA((na,))],
        before=before, after=after)


def _exchange_comm(arrays, out_shape, make_copies):
    na = len(arrays)

    def copies(in_refs, out_refs, scr):
        return make_copies(in_refs, out_refs, *scr)

    def before(step, nsteps, in_refs, out_refs, scr):
        @pl.when(step == 0)
        def _():
            for cp in copies(in_refs, out_refs, scr):
                cp.start()

    def after(step, nsteps, in_refs, out_refs, scr):
        @pl.when(step == nsteps - 1)
        def _():
            for cp in copies(in_refs, out_refs, scr):
                cp.wait()

    return _Comm(inputs=list(arrays), in_specs=[_HBM_WHOLE] * na, out_shape=list(out_shape),
                 out_specs=[_HBM_WHOLE] * na,
                 scratch_shapes=[pltpu.SemaphoreType.DMA((na, 3)), pltpu.SemaphoreType.DMA((na, 3))],
                 before=before, after=after)


def _sibling_exchange_comm(grads):
    def make_copies(in_refs, out_refs, send_sems, recv_sems):
        x, y, c = _mesh_pos()
        return [pltpu.make_async_remote_copy(
            src_ref=in_refs[a].at[:, pl.ds(1 - c, 1)], dst_ref=out_refs[a], send_sem=send_sems.at[a, 0],
            recv_sem=recv_sems.at[a, 0], device_id=(x, y, 1 - c), device_id_type=MESH_ID)
            for a in range(len(grads))]

    return _exchange_comm(grads, [jax.ShapeDtypeStruct((4, 1) + g.shape[2:], F32) for g in grads], make_copies)


def _chips_exchange_comm(partials):
    def make_copies(in_refs, out_refs, send_sems, recv_sems):
        x, y, c = _mesh_pos()
        return [pltpu.make_async_remote_copy(
            src_ref=in_refs[a].at[j], dst_ref=out_refs[a].at[j], send_sem=send_sems.at[a, j],
            recv_sem=recv_sems.at[a, j], device_id=(*chip, c), device_id_type=MESH_ID)
            for a in range(len(partials)) for j, chip in enumerate(_other_chips(x, y))]

    return _exchange_comm(partials, [jax.ShapeDtypeStruct(g.shape, BF16) for g in partials], make_copies)


def _comm_only(comms, name):
    return _fused_call(lambda: None, comms, name=name, grid=(1,), inputs=(), in_specs=[], out_specs=(),
                       out_shape=())[1]


def _chip_sum(my_pos, mine, from_sibling, tr, name):
    _, _, r, c = mine.shape

    def body(pos_ref, a_ref, b_ref, own_ref, out_ref):
        s = a_ref[0, 0] + b_ref[0, 0]

        @pl.when(pl.program_id(1) == 0)
        def _():
            own_ref[...] = s

        @pl.when(pl.program_id(1) > 0)
        def _():
            out_ref[0] = s.astype(BF16)

    grid_spec = pltpu.PrefetchScalarGridSpec(
        num_scalar_prefetch=1, grid=(r // tr, 4),
        in_specs=[pl.BlockSpec((1, 1, tr, c), lambda i, k, pos: (pos[0] ^ k, pos[1], i, 0)),
                  pl.BlockSpec((1, 1, tr, c), lambda i, k, pos: (pos[0] ^ k, 0, i, 0))],
        out_specs=(pl.BlockSpec((tr, c), lambda i, k, pos: (i, 0)),
                   pl.BlockSpec((1, tr, c), lambda i, k, pos: (jnp.maximum(k - 1, 0), i, 0))))
    return pl.pallas_call(
        body, name=name, grid_spec=grid_spec,
        out_shape=(jax.ShapeDtypeStruct((r, c), F32), jax.ShapeDtypeStruct((3, r, c), BF16)),
        compiler_params=_params(2),
    )(my_pos, mine, from_sibling)


def _all_reduce_small_comm(parts):
    na = len(parts)

    def copies(in_refs, scr):
        gathered, (send_sems, recv_sems) = scr[:na], scr[na:]
        x, y, c = _mesh_pos()
        my_id = 4 * x + 2 * y + c
        return my_id, [pltpu.make_async_remote_copy(
            src_ref=in_refs[a], dst_ref=gathered[a].at[my_id], send_sem=send_sems.at[a, k - 1],
            recv_sem=recv_sems.at[a, k - 1], device_id=(x ^ (k >> 2), y ^ ((k >> 1) & 1), c ^ (k & 1)),
            device_id_type=MESH_ID) for a in range(na) for k in range(1, N_DEV)]

    def before(step, nsteps, in_refs, out_refs, scr):
        @pl.when(step == 0)
        def _():
            for cp in copies(in_refs, scr)[1]:
                cp.start()

    def after(step, nsteps, in_refs, out_refs, scr):
        @pl.when(step == nsteps - 1)
        def _():
            my_id, cps = copies(in_refs, scr)
            for a in range(na):
                scr[a][my_id] = in_refs[a][...]
            for cp in cps:
                cp.wait()
            for a in range(na):
                acc = scr[a][0]
                for d in range(1, N_DEV):
                    acc = acc + scr[a][d]
                out_refs[a][...] = acc

    return _Comm(inputs=list(parts), in_specs=[_VMEM_WHOLE] * na,
                 out_shape=[jax.ShapeDtypeStruct(p.shape, F32) for p in parts], out_specs=[_VMEM_WHOLE] * na,
                 scratch_shapes=[pltpu.VMEM((N_DEV,) + p.shape, F32) for p in parts] + [
                     pltpu.SemaphoreType.DMA((na, N_DEV - 1)), pltpu.SemaphoreType.DMA((na, N_DEV - 1))],
                 before=before, after=after)


def _unshard_cols(g):
    return jnp.transpose(g, (1, 0, 2)).reshape(g.shape[1], N_DEV * g.shape[2])


def _row_blocks(w):
    return w.reshape(4, 2, w.shape[0] // N_DEV, w.shape[1])


def _stack_rows(parts):
    a = jnp.concatenate(parts, axis=0)
    return jnp.pad(a, ((0, (-a.shape[0]) % 8), (0, 0)))


def _w_in_grad_blocks(dw):
    return _row_blocks(jnp.concatenate([dw[:LR_REF], dw[LR_COL:LR_COL + 2 * LOWRANK], dw[LR_REF:LR_COL]], axis=0))


def _padded_decay_weights(wd_f, wd_b):
    zeros = lambda n: jnp.zeros((n, KEY_W), F32)
    return (jnp.concatenate([wd_f, zeros(LANE - LOWRANK)], axis=0),
            jnp.concatenate([zeros(LOWRANK), wd_b, zeros(LANE - 2 * LOWRANK)], axis=0))


def kernel(x, norm1_g, w_in,w_decay_f, b_decay_f, w_decay_b, b_decay_b, gla_norm_g, gmlp_ln_g, gmlp_ln_b, w_spatial, b_spatial, w_out, norm2_g, w_gate, w_up, w_down, final_norm_g, loss_target, m_norm1_g, m_w_in, m_w_decay_f, m_b_decay_f, m_w_decay_b, m_b_decay_b, m_gla_norm_g, m_gmlp_ln_g, m_gmlp_ln_b, m_w_spatial, m_b_spatial, m_w_out, m_norm2_g, m_w_gate, m_w_up, m_w_down, m_final_norm_g, v_norm1_g, v_w_in, v_w_decay_f, v_b_decay_f, v_w_decay_b, v_b_decay_b, v_gla_norm_g, v_gmlp_ln_g, v_gmlp_ln_b, v_w_spatial, v_b_spatial, v_w_out, v_norm2_g, v_w_gate, v_w_up, v_w_down, v_final_norm_g):
    t = x.shape[1]
    xt = x[0]
    target = loss_target[0]
    pos_x, pos_y, pos_c = _mesh_pos()
    my_pos = jnp.stack([2 * pos_x + pos_y, pos_c]).astype(jnp.int32)
    my_id = 4 * pos_x + 2 * pos_y + pos_c

    tile = lambda n: min(n, t)
    ln_g, ln_b, w_sp = gmlp_ln_g, gmlp_ln_b, w_spatial[0]
    b_sp_col = b_spatial[0][:, :, None]
    shard = {"w_in": w_in[0].T, "w_out": w_out[0], "w_gate": w_gate[0].T, "w_up": w_up[0].T, "w_down": w_down[0]}
    shard_m = {"w_in": m_w_in[0].T, "w_out": m_w_out[0], "w_gate": m_w_gate[0].T, "w_up": m_w_up[0].T,
               "w_down": m_w_down[0]}
    shard_v = {"w_in": v_w_in[0].T, "w_out": v_w_out[0], "w_gate": v_w_gate[0].T, "w_up": v_w_up[0].T,
               "w_down": v_w_down[0]}
    transposed = ("w_in", "w_gate", "w_up")
    chip_sum = lambda n, g, s: _chip_sum(my_pos, g, s[0], g.shape[2], "chip_sum_" + n)

    decay_shard = jnp.stack([w_decay_f[0], w_decay_b[0]])
    (hb,), ((g_in, g_decay),) = _norm1(xt, norm1_g, tile(TOKEN_TILE["norm1"]),
                                       [_gather_comm([shard["w_in"], decay_shard], [True, False])])
    w_in_t = g_in.reshape(PROJ_W, D_MODEL)
    wd_pad_f, wd_pad_b = _padded_decay_weights(_unshard_cols(g_decay[:, 0]), _unshard_cols(g_decay[:, 1]))
    (p,), ((g_gate, g_out),) = _in_proj(
        hb, w_in_t, tile(TOKEN_TILE["in_proj"]), [_gather_comm([shard["w_gate"], shard["w_out"]], [True, True])])
    (o_f, st_f, o_b, st_b), ((g_up,),) = _gla_fwd(
        p, wd_pad_f, b_decay_f, wd_pad_b, b_decay_b, tile(TOKEN_TILE["gla"]), [_gather_comm([shard["w_up"]], [True])])
    w_out_full = g_out.reshape(D_MODEL, D_MODEL)
    (x1, ycat), ((g_down,),) = _mix_fwd(xt, o_f, o_b, p, gla_norm_g, ln_g, ln_b, w_sp, b_sp_col, w_out_full,
                                        tile(TOKEN_TILE["mix_fwd"]), [_gather_comm([shard["w_down"]], [True])])

    dx1, h2b, dgate, dup, act, dx2, loss_acc, d_gf, d_g2 = _ffn(
        x1, target, norm2_g, final_norm_g[None, :], g_gate.reshape(D_FF, D_MODEL), g_up.reshape(D_FF, D_MODEL),
        g_down.reshape(D_FF, D_MODEL), tile(TOKEN_TILE["ffn"]))
    dw_gate, _ = _matmul_tn(dgate, h2b, D_FF // 2, tile(TOKEN_TILE["dw"]), "grad_w_gate")
    dw_up, _ = _matmul_tn(dup, h2b, D_FF // 2, tile(TOKEN_TILE["dw"]), "grad_w_up")
    dw_down, _ = _matmul_tn(act, dx2, D_FF // 2, tile(TOKEN_TILE["dw"]), "grad_w_down")

    ffn_grads = [_row_blocks(dw_gate), _row_blocks(dw_up), _row_blocks(dw_down)]
    (d_o, dpg, dpu, dpv, dw_out, d_gg, d_lg, d_lb, dw_sp, db_sp), (ffn_sib,) = _mix_bwd(
        dx1, ycat, o_f, o_b, p, gla_norm_g, ln_g, ln_b, w_sp, b_sp_col, w_out_full,
        tile(TOKEN_TILE["mix_bwd"]),
        [_sibling_exchange_comm(ffn_grads)])
    ffn_names = ["w_gate", "w_up", "w_down"]
    ffn_sums = [chip_sum(n, g, [s]) for n, g, s in zip(ffn_names, ffn_grads, ffn_sib)]
    out_grad = _row_blocks(dw_out)
    (dq_f, dk_f, dv_f, dlr_f, dwd_f, dbd_f, dq_b, dk_b, dv_b, dlr_b, dwd_b, dbd_b), (ffn_recv, out_sib) = _gla_bwd(
        p, wd_pad_f, b_decay_f, wd_pad_b, b_decay_b, st_f, st_b, d_o, tile(TOKEN_TILE["gla"]),
        [_chips_exchange_comm([s[1] for s in ffn_sums]), _sibling_exchange_comm([out_grad])])
    out_sum = chip_sum("w_out", out_grad, out_sib)
    (grad_x, dp, d_g1), _ = _in_proj_bwd(
        xt, norm1_g, dx1, dq_f, dq_b, dk_f, dk_b, dv_f, dv_b, dpg, dpu, dpv, dlr_f, dlr_b, w_in_t,
        tile(TOKEN_TILE["in_proj_bwd"]))

    stacks = [_stack_rows([d_g1, d_g2, d_gf]), _stack_rows([d_gg, d_lg, d_lb]),
              _stack_rows([dbd_f, dbd_b, jnp.zeros((DECAY_W_ROW - 2, KEY_W), F32), dwd_f[:LOWRANK],
                           dwd_b[LOWRANK:2 * LOWRANK]]),
              _stack_rows([dw_sp.reshape(GMLP_W, GMLP_CHUNK), db_sp[:, :, 0], loss_acc[:1]])]
    dw_main, (small_sums, out_recv) = _matmul_tn(
        dp, hb, PROJ_PAD // 3, tile(TOKEN_TILE["dw"]), "grad_w_in",
        [_all_reduce_small_comm(stacks), _chips_exchange_comm([out_sum[1]])])
    def adamw(n, s, rc, comms=()):
        rows = shard[n].shape[0]
        half = rows // 2 if rows % 32 == 0 else rows
        res, comm_res = _adamw_shard(s[0], rc, shard[n], shard_m[n], shard_v[n], half, "adamw_" + n, comms)
        return [r.T if n in transposed else r for r in res], comm_res

    in_grad = _w_in_grad_blocks(dw_main)
    big_out = {}
    big_out["w_gate"], (in_sib,) = adamw("w_gate", ffn_sums[0], ffn_recv[0], [_sibling_exchange_comm([in_grad])])
    in_sum = chip_sum("w_in", in_grad, in_sib)
    big_out["w_up"], (in_recv,) = adamw("w_up", ffn_sums[1], ffn_recv[1], [_chips_exchange_comm([in_sum[1]])])
    big_out["w_down"], _ = adamw("w_down", ffn_sums[2], ffn_recv[2])
    big_out["w_out"], _ = adamw("w_out", out_sum, out_recv[0])
    big_out["w_in"], _ = adamw("w_in", in_sum, in_recv[0])

    s1024, s512, s256, s128 = small_sums
    loss = s128[GMLP_W + GMLP_GROUPS, 0]
    col0 = my_id * (KEY_W // N_DEV)
    decay_cols = lambda row0: lax.dynamic_slice(s256, (row0, col0), (LOWRANK, KEY_W // N_DEV))
    flat = lambda a: a.reshape(-1, a.shape[-1])
    small = {
        "norm1_g": ((s1024, 0, 1), norm1_g, m_norm1_g, v_norm1_g),
        "w_decay_f": ((decay_cols(DECAY_W_ROW), 0, LOWRANK), w_decay_f, m_w_decay_f, v_w_decay_f),
        "b_decay_f": ((s256, 0, 1), b_decay_f, m_b_decay_f, v_b_decay_f),
        "w_decay_b": ((decay_cols(DECAY_W_ROW + LOWRANK), 0, LOWRANK), w_decay_b, m_w_decay_b, v_w_decay_b),
        "b_decay_b": ((s256, 1, 1), b_decay_b, m_b_decay_b, v_b_decay_b),
        "gla_norm_g": ((s512, 0, 1), gla_norm_g, m_gla_norm_g, v_gla_norm_g),
        "gmlp_ln_g": ((s512, 1, 1), gmlp_ln_g, m_gmlp_ln_g, v_gmlp_ln_g),
        "gmlp_ln_b": ((s512, 2, 1), gmlp_ln_b, m_gmlp_ln_b, v_gmlp_ln_b),
        "w_spatial": ((s128, 0, GMLP_W), w_spatial, m_w_spatial, v_w_spatial),
        "b_spatial": ((s128, GMLP_W, GMLP_GROUPS), b_spatial, m_b_spatial, v_b_spatial),
        "norm2_g": ((s1024, 1, 1), norm2_g, m_norm2_g, v_norm2_g),
        "final_norm_g": ((s1024, 2, 1), final_norm_g, m_final_norm_g, v_final_norm_g),
    }
    small_res = _adamw_small([(g, flat(w), flat(m), flat(v)) for g, w, m, v in small.values()])
    small_out = {n: [r.reshape(small[n][1].shape) for r in res] for n, res in zip(small, small_res)}

    order = ["norm1_g", "w_in", "w_decay_f", "b_decay_f", "w_decay_b", "b_decay_b", "gla_norm_g", "gmlp_ln_g",
             "gmlp_ln_b", "w_spatial", "b_spatial", "w_out", "norm2_g", "w_gate", "w_up", "w_down", "final_norm_g"]
    outs = []
    for kind in range(4):
        for n in order:
            outs.append(big_out[n][kind][None] if n in big_out else small_out[n][kind])
    return (loss, grad_x[None], *outs)
```

```python
import functools
import math

import jax
import jax.numpy as jnp
from jax import lax
from jax.experimental import pallas as pl
from jax.experimental.pallas import tpu as pltpu

F32 = jnp.float32
BF16 = jnp.bfloat16

D_MODEL = 1024
GLA_HEADS = 4
GLA_DK = 64
GLA_DV = 128
KEY_W = GLA_HEADS * GLA_DK
VAL_W = GLA_HEADS * GLA_DV
LOWRANK = 16
GLA_TAU = 16.0
GLA_CHUNK = 64
GMLP_W = 512
GMLP_GROUPS = 4
GMLP_CHUNK = 128
D_FF = 2816
EPS = 1e-6
Q_SCALE = GLA_DK ** -0.5
PROJ_PAD = 2688
LR_COL = 2560
LANE = 128
N_DEV = 8

ADAM_LR = 0.001
ADAM_B1 = 0.9
ADAM_B2 = 0.999
ADAM_EPS = 1e-08
ADAM_WD = 0.01
ADAM_STEP = 10

VMEM_LIMIT = 56 * 1024 * 1024
TOKEN_TILE = {"norm1": 512, "in_proj": 512, "gla": 1024, "mix_fwd": 1024, "ffn": 256, "mix_bwd": 512,
              "in_proj_bwd": 512, "dw": 2048}
DECAY_W_ROW = 8
MESH_ID = pl.DeviceIdType.MESH
INV_SQRT2 = 0.7071067811865476
INV_SQRT_2PI = 0.3989422804014327


def _params(n_axes=1):
    return pltpu.CompilerParams(dimension_semantics=("arbitrary",) * n_axes, vmem_limit_bytes=VMEM_LIMIT)


def _mm(a, b):
    return jnp.dot(a.astype(BF16), b.astype(BF16), preferred_element_type=F32)


def _mm_nt(a, b):
    return lax.dot_general(a.astype(BF16), b.astype(BF16), (((1,), (1,)), ((), ())), preferred_element_type=F32)


def _mm_tn(a, b):
    return lax.dot_general(a.astype(BF16), b.astype(BF16), (((0,), (0,)), ((), ())), preferred_element_type=F32)


def _const_spec(shape):
    nd = len(shape)
    return pl.BlockSpec(shape, lambda *_: (0,) * nd, pipeline_mode=pl.Buffered(1))


def _acc_spec(shape):
    nd = len(shape)
    return pl.BlockSpec(shape, lambda *_: (0,) * nd)


class _Comm:
    def __init__(self, inputs, in_specs, out_shape, out_specs, scratch_shapes, before, after):
        self.inputs, self.in_specs, self.out_shape, self.out_specs = inputs, in_specs, out_shape, out_specs
        self.scratch_shapes, self.before, self.after = scratch_shapes, before, after


def _fused_call(body, comms, *, name, grid, inputs, in_specs, out_specs, out_shape, scratch_shapes=()):
    n_in, n_out, n_scr = len(in_specs), len(out_specs), len(scratch_shapes)
    nsteps = math.prod(grid)
    sizes = [(len(c.inputs), len(c.out_shape), len(c.scratch_shapes)) for c in comms]

    def full_body(*refs):
        step = pl.program_id(0)
        for axis in range(1, len(grid)):
            step = step * grid[axis] + pl.program_id(axis)
        ins, rest = refs[:n_in], refs[n_in:]
        c_ins = []
        for ci, _, _ in sizes:
            c_ins.append(rest[:ci])
            rest = rest[ci:]
        outs, rest = rest[:n_out], rest[n_out:]
        c_outs = []
        for _, co, _ in sizes:
            c_outs.append(rest[:co])
            rest = rest[co:]
        scr, rest = rest[:n_scr], rest[n_scr:]
        c_scr = []
        for _, _, cs in sizes:
            c_scr.append(rest[:cs])
            rest = rest[cs:]
        for c, a, b, s in zip(comms, c_ins, c_outs, c_scr):
            c.before(step, nsteps, a, b, s)
        body(*ins, *outs, *scr)
        for c, a, b, s in zip(comms, c_ins, c_outs, c_scr):
            c.after(step, nsteps, a, b, s)

    results = pl.pallas_call(
        full_body, name=name, grid=grid,
        in_specs=list(in_specs) + [s for c in comms for s in c.in_specs],
        out_specs=tuple(out_specs) + tuple(s for c in comms for s in c.out_specs),
        out_shape=tuple(out_shape) + tuple(s for c in comms for s in c.out_shape),
        scratch_shapes=list(scratch_shapes) + [s for c in comms for s in c.scratch_shapes],
        compiler_params=_params(len(grid)),
    )(*inputs, *[a for c in comms for a in c.inputs])
    own, rest = results[:n_out], results[n_out:]
    comm_results = []
    for _, co, _ in sizes:
        comm_results.append(rest[:co])
        rest = rest[co:]
    return own, comm_results


def _gelu(x):
    return 0.5 * x * (1.0 + lax.erf(x * INV_SQRT2))


def _gelu_and_grad(x):
    cdf = 0.5 * (1.0 + lax.erf(x * INV_SQRT2))
    return x * cdf, cdf + x * jnp.exp(-0.5 * x * x) * INV_SQRT_2PI


def _sigmoid(x):
    return 0.5 + 0.5 * jnp.tanh(0.5 * x)


def _silu_and_grad(x):
    s = _sigmoid(x)
    return x * s, s * (1.0 + x * (1.0 - s))


def _norm1(x, g1, tm, comms=()):
    t = x.shape[0]

    def body(x_ref, g_ref, h_ref):
        xv = x_ref[...]
        r = lax.rsqrt(jnp.mean(xv * xv, axis=-1, keepdims=True) + EPS)
        h_ref[...] = (xv * r * g_ref[...]).astype(BF16)

    row = pl.BlockSpec((tm, D_MODEL), lambda i: (i, 0))
    return _fused_call(body, comms, name="norm1", grid=(t // tm,), inputs=(x, g1),
                       in_specs=[row, _const_spec((1, D_MODEL))], out_specs=(row,),
                       out_shape=(jax.ShapeDtypeStruct((t, D_MODEL), BF16),))


PROJ_W = 2592
LR_REF = 1536
PROJ_ROWS = ((0, LR_REF, 0), (LR_REF + 2 * LOWRANK, PROJ_W, LR_REF), (LR_REF, LR_REF + LANE, LR_COL))


def _in_proj(h, w_in_t, tm, comms=()):
    t = h.shape[0]

    def body(h_ref, w_ref, p_ref):
        hv = h_ref[...]
        for r0, r1, c0 in PROJ_ROWS:
            p_ref[:, c0:c0 + r1 - r0] = _mm_nt(hv, w_ref[r0:r1, :]).astype(BF16)

    return _fused_call(
        body, comms, name="in_proj", grid=(t // tm,), inputs=(h, w_in_t),
        in_specs=[pl.BlockSpec((tm, D_MODEL), lambda i: (i, 0)), _const_spec((PROJ_W, D_MODEL))],
        out_specs=(pl.BlockSpec((tm, PROJ_PAD), lambda i: (i, 0)),),
        out_shape=(jax.ShapeDtypeStruct((t, PROJ_PAD), BF16),))


def _tri(upper):
    r = lax.broadcasted_iota(jnp.int32, (GLA_CHUNK, GLA_CHUNK), 0)
    c = lax.broadcasted_iota(jnp.int32, (GLA_CHUNK, GLA_CHUNK), 1)
    return jnp.where((c >= r) if upper else (c <= r), 1.0, 0.0).astype(BF16)


def _chunk_cumsum(tri, a, add=None):
    hi = a.astype(BF16)
    lo = (a - hi.astype(F32)).astype(BF16)
    dot = functools.partial(jnp.dot, preferred_element_type=F32)
    sums = [dot(tri, hi[_chunk_rows(c)]) + dot(tri, lo[_chunk_rows(c)]) for c in range(a.shape[0] // GLA_CHUNK)]
    return jnp.concatenate(sums if add is None else [s + r for s, r in zip(sums, add)], axis=0)


def _chunk_rows(c):
    return slice(c * GLA_CHUNK, (c + 1) * GLA_CHUNK)


def _gla_masks(rev):
    dk_bits, dv_bits = GLA_DK.bit_length() - 1, GLA_DV.bit_length() - 1
    key_head = lax.broadcasted_iota(jnp.int32, (GLA_CHUNK, KEY_W), 1) >> dk_bits
    val_head = lax.broadcasted_iota(jnp.int32, (GLA_CHUNK, VAL_W), 1) >> dv_bits
    t = lax.broadcasted_iota(jnp.int32, (GLA_HEADS * GLA_CHUNK, GLA_CHUNK), 0) & (GLA_CHUNK - 1)
    s = lax.broadcasted_iota(jnp.int32, (GLA_HEADS * GLA_CHUNK, GLA_CHUNK), 1)
    return key_head, val_head, (s >= t) if rev else (s <= t)


def _stack_heads(a, head_of_lane):
    a = a.astype(BF16)
    return jnp.concatenate([jnp.where(head_of_lane == h, a, jnp.zeros_like(a)) for h in range(GLA_HEADS)], axis=0)


def _rows_by_head(a):
    return jnp.concatenate([a[:, h * GLA_DV:(h + 1) * GLA_DV] for h in range(GLA_HEADS)], axis=0)


def _lanes_by_head(r):
    return jnp.concatenate([r[h * GLA_CHUNK:(h + 1) * GLA_CHUNK] for h in range(GLA_HEADS)], axis=1)


def _head_diagonal(r, head_of_lane):
    rows = r.shape[0] // GLA_HEADS
    out = jnp.where(head_of_lane == 0, r[:rows], 0.0)
    for h in range(1, GLA_HEADS):
        out = out + jnp.where(head_of_lane == h, r[h * rows:(h + 1) * rows], 0.0)
    return out


def _tile_terms(la, q, k, tri, rev):
    nc = la.shape[0] // GLA_CHUNK
    q, k = q.astype(F32), k.astype(F32)
    b = _chunk_cumsum(tri, la)
    ebl = [jnp.exp(b[c * GLA_CHUNK:c * GLA_CHUNK + 1] if rev else b[(c + 1) * GLA_CHUNK - 1:(c + 1) * GLA_CHUNK])
           for c in range(nc)]
    eb = jnp.exp(b)
    enb = jnp.exp(-b)
    kd = k * enb
    ke = jnp.concatenate([kd[_chunk_rows(c)] * ebl[c] for c in range(nc)], axis=0)
    return ebl, eb, enb, q * Q_SCALE * eb, kd, ke


def _log_decay(lr_ref, wd_ref, bd_ref):
    z = _mm(lr_ref[...], wd_ref[...]) + bd_ref[...]
    return z, jax.nn.log_sigmoid(z) * (1.0 / GLA_TAU)


def _p_specs(tg, tile):
    return [pl.BlockSpec((tg, KEY_W), lambda i: (tile(i), 0)),
            pl.BlockSpec((tg, KEY_W), lambda i: (tile(i), 1)),
            pl.BlockSpec((tg, VAL_W), lambda i: (tile(i), 1)),
            pl.BlockSpec((tg, LANE), lambda i: (tile(i), LR_COL // LANE))]


def _gla_fwd_dir(rev, nc, q_ref, k_ref, v_ref, lr_ref, wd_ref, bd_ref, o_ref, st_ref, state):
    key_head, _, causal = _gla_masks(rev)
    order = range(nc - 1, -1, -1) if rev else range(nc)

    def intra():
        _, la = _log_decay(lr_ref, wd_ref, bd_ref)
        ebl, _, _, qd, kd, ke = _tile_terms(la, q_ref[...], k_ref[...], _tri(rev), rev)
        kd = kd.astype(BF16)
        v = {c: v_ref[_chunk_rows(c), :].astype(BF16) for c in order}
        qd_stack = {c: _stack_heads(qd[_chunk_rows(c)], key_head) for c in order}
        ke_stack = {c: _stack_heads(ke[_chunk_rows(c)], key_head) for c in order}
        a_all = {c: _mm_nt(qd_stack[c], kd[_chunk_rows(c)]) for c in order}
        a_all = {c: jnp.where(causal, a_all[c], 0.0).astype(BF16) for c in order}
        head_rows = lambda a, h: a[h * GLA_CHUNK:(h + 1) * GLA_CHUNK]
        head_vals = lambda a, h: a[:, h * GLA_DV:(h + 1) * GLA_DV]
        r = {c: [_mm(head_rows(a_all[c], h), head_vals(v[c], h)) for h in range(GLA_HEADS)] for c in order}
        upd = {c: _mm_tn(_rows_by_head(v[c]), ke_stack[c]) for c in order}
        return {c: (ebl[c], qd_stack[c], r[c], upd[c]) for c in order}

    def scan(terms):
        st = state[...]
        states = {}
        for c in order:
            states[c] = st
            st_ref[c] = st.astype(BF16)
            st = st * terms[c][0] + terms[c][3]
        state[...] = st
        return states

    def inter(terms, states):
        r_inter = {c: _mm_nt(terms[c][1], states[c]) for c in order}
        for c in order:
            o_ref[_chunk_rows(c), :] = jnp.concatenate(
                [terms[c][2][h] + r_inter[c][h * GLA_CHUNK:(h + 1) * GLA_CHUNK] for h in range(GLA_HEADS)], axis=1)

    return intra, scan, inter


def _gla_fwd(p, wd_pad_f, bd_f, wd_pad_b, bd_b, tg, comms=()):
    t = p.shape[0]
    nt = t // tg
    nc = tg // GLA_CHUNK
    up, down = (lambda i: i), (lambda i: nt - 1 - i)

    def body(qf, kf, vf, lrf, qb, kb, vb, lrb, wdf, bdf, wdb, bdb, of, stf, ob, stb, state_f, state_b):
        @pl.when(pl.program_id(0) == 0)
        def _():
            state_f[...] = jnp.zeros_like(state_f)
            state_b[...] = jnp.zeros_like(state_b)

        dirs = [_gla_fwd_dir(False, nc, qf, kf, vf, lrf, wdf, bdf, of, stf, state_f),
                _gla_fwd_dir(True, nc, qb, kb, vb, lrb, wdb, bdb, ob, stb, state_b)]
        terms = [intra() for intra, _, _ in dirs]
        states = [scan(t) for (_, scan, _), t in zip(dirs, terms)]
        for (_, _, inter), t, s in zip(dirs, terms, states):
            inter(t, s)

    wd_spec, bd_spec = _const_spec((LANE, KEY_W)), _const_spec((1, KEY_W))
    outs = lambda tile: (pl.BlockSpec((tg, VAL_W), lambda i: (tile(i), 0)),
                         pl.BlockSpec((nc, GLA_DV, KEY_W), lambda i: (tile(i), 0, 0)))
    out_shape = (jax.ShapeDtypeStruct((t, VAL_W), F32), jax.ShapeDtypeStruct((t // GLA_CHUNK, GLA_DV, KEY_W), BF16))
    return _fused_call(
        body, comms, name="gla_fwd", grid=(nt,), inputs=(p,) * 8 + (wd_pad_f, bd_f, wd_pad_b, bd_b),
        in_specs=_p_specs(tg, up) + _p_specs(tg, down) + [wd_spec, bd_spec, wd_spec, bd_spec],
        out_specs=outs(up) + outs(down), out_shape=out_shape * 2,
        scratch_shapes=[pltpu.VMEM((GLA_DV, KEY_W), F32)] * 2)


def _gla_bwd_dir(rev, nc, q_ref, k_ref, v_ref, lr_ref, wd_ref, bd_ref, st_ref, do_ref,
                 dq_ref, dk_ref, dv_ref, dlr_ref, dwd_ref, dbd_ref, dstate):
    key_head, val_head, causal = _gla_masks(rev)
    order = range(nc) if rev else range(nc - 1, -1, -1)

    def intra():
        z, la = _log_decay(lr_ref, wd_ref, bd_ref)
        tile = _tile_terms(la, q_ref[...], k_ref[...], _tri(rev), rev)
        qd, kd = tile[3], tile[4].astype(BF16)
        v = {c: v_ref[_chunk_rows(c), :].astype(BF16) for c in order}
        d_o = {c: do_ref[_chunk_rows(c), :] for c in order}
        kd_c = {c: kd[_chunk_rows(c)] for c in order}
        qd_stack = {c: _stack_heads(qd[_chunk_rows(c)], key_head) for c in order}
        do_stack = {c: _stack_heads(d_o[c], val_head) for c in order}
        do_rows = {c: _rows_by_head(d_o[c]) for c in order}
        a_all = {c: _mm_nt(qd_stack[c], kd_c[c]) for c in order}
        head_vals = lambda a, h: a[:, h * GLA_DV:(h + 1) * GLA_DV]
        da_all = {c: jnp.concatenate([_mm_nt(head_vals(d_o[c], h), head_vals(v[c], h)) for h in range(GLA_HEADS)],
                                     axis=0) for c in order}
        a_all = {c: jnp.where(causal, a_all[c], 0.0).astype(BF16) for c in order}
        da_all = {c: jnp.where(causal, da_all[c], 0.0).astype(BF16) for c in order}
        dv = {c: _mm_tn(a_all[c], do_stack[c]) for c in order}
        dqd = {c: _mm(jnp.concatenate([do_rows[c], da_all[c]], axis=1),
                      jnp.concatenate([st_ref[c], kd_c[c]], axis=0)) for c in order}
        dkd = {c: _mm_tn(da_all[c], qd_stack[c]) for c in order}
        upd = {c: _mm_tn(do_rows[c], qd_stack[c]) for c in order}
        dqd = {c: _head_diagonal(dqd[c], key_head) for c in order}
        return z, tile, {c: dict(dv=dv[c], dqd=dqd[c], dkd=dkd[c], upd=upd[c]) for c in order}

    def scan(tile, per):
        dst = dstate[...]
        dsts = {}
        for c in order:
            dsts[c] = dst
            dst = dst * tile[0][c] + per[c]["upd"]
        dstate[...] = dst
        return dsts

    def inter(z, tile, per, dsts):
        ebl, eb, enb, qd, kd, ke = tile
        ke_stack = {c: _stack_heads(ke[_chunk_rows(c)], key_head) for c in order}
        v_rows = {c: _rows_by_head(v_ref[_chunk_rows(c), :].astype(BF16)) for c in order}
        dst_b = {c: dsts[c].astype(BF16) for c in order}
        dv_state = {c: _mm_nt(ke_stack[c], dst_b[c]) for c in order}
        dke_c = {c: _mm(v_rows[c], dst_b[c]) for c in order}
        dke_c = {c: _head_diagonal(dke_c[c], key_head) for c in order}
        dbl_c = {}
        for c in order:
            rows = _chunk_rows(c)
            dv_ref[rows, :] = (per[c]["dv"] + _lanes_by_head(dv_state[c])).astype(BF16)
            dbl_c[c] = (jnp.sum(dsts[c] * st_ref[c].astype(F32), axis=0, keepdims=True) * ebl[c]
                        + jnp.sum(dke_c[c] * ke[rows], axis=0, keepdims=True))
        tile_of = lambda parts: jnp.concatenate([parts[c] for c in range(nc)], axis=0)
        dqd, dkd = tile_of({c: per[c]["dqd"] for c in order}), tile_of({c: per[c]["dkd"] for c in order})
        dke = tile_of(dke_c)
        dke_end = tile_of({c: dke_c[c] * ebl[c] for c in order})
        dq_ref[...] = (dqd * eb * Q_SCALE).astype(BF16)
        dk_ref[...] = ((dkd + dke_end) * enb).astype(BF16)
        db = dqd * qd - dkd * kd - dke * ke
        dla = _chunk_cumsum(_tri(not rev), db, [dbl_c[c] for c in range(nc)])
        dz = dla * (_sigmoid(-z) * (1.0 / GLA_TAU))
        dlr_ref[...] = _mm_nt(dz, wd_ref[...]).astype(BF16)
        dwd_ref[...] += _mm_tn(lr_ref[...], dz)
        dbd_ref[...] += jnp.sum(dz, axis=0, keepdims=True)

    return intra, scan, inter


def _gla_bwd(p, wd_pad_f, bd_f, wd_pad_b, bd_b, st_f, st_b, d_o, tg, comms=()):
    t = p.shape[0]
    nt = t // tg
    nc = tg // GLA_CHUNK
    up, down = (lambda i: i), (lambda i: nt - 1 - i)

    def body(qf, kf, vf, lrf, stf, dof, qb, kb, vb, lrb, stb, dob, wdf, bdf, wdb, bdb,
             dqf, dkf, dvf, dlrf, dwdf, dbdf, dqb, dkb, dvb, dlrb, dwdb, dbdb, dstate_f, dstate_b):
        @pl.when(pl.program_id(0) == 0)
        def _():
            for ref in (dstate_f, dstate_b, dwdf, dbdf, dwdb, dbdb):
                ref[...] = jnp.zeros_like(ref)

        dirs = [_gla_bwd_dir(False, nc, qf, kf, vf, lrf, wdf, bdf, stf, dof, dqf, dkf, dvf, dlrf, dwdf, dbdf,
                             dstate_f),
                _gla_bwd_dir(True, nc, qb, kb, vb, lrb, wdb, bdb, stb, dob, dqb, dkb, dvb, dlrb, dwdb, dbdb,
                             dstate_b)]
        first = [intra() for intra, _, _ in dirs]
        dsts = [scan(tile, per) for (_, scan, _), (_, tile, per) in zip(dirs, first)]
        for (_, _, inter), (z, tile, per), d in zip(dirs, first, dsts):
            inter(z, tile, per, d)

    wd_spec, bd_spec = _const_spec((LANE, KEY_W)), _const_spec((1, KEY_W))
    ins = lambda tile: _p_specs(tg, tile) + [pl.BlockSpec((nc, GLA_DV, KEY_W), lambda i: (tile(i), 0, 0)),
                                             pl.BlockSpec((tg, VAL_W), lambda i: (tile(i), 0))]
    outs = lambda tile: (pl.BlockSpec((tg, KEY_W), lambda i: (tile(i), 0)),
                         pl.BlockSpec((tg, KEY_W), lambda i: (tile(i), 0)),
                         pl.BlockSpec((tg, VAL_W), lambda i: (tile(i), 0)),
                         pl.BlockSpec((tg, LANE), lambda i: (tile(i), 0)),
                         _acc_spec((LANE, KEY_W)), _acc_spec((1, KEY_W)))
    out_shape = (jax.ShapeDtypeStruct((t, KEY_W), BF16), jax.ShapeDtypeStruct((t, KEY_W), BF16),
                 jax.ShapeDtypeStruct((t, VAL_W), BF16), jax.ShapeDtypeStruct((t, LANE), BF16),
                 jax.ShapeDtypeStruct((LANE, KEY_W), F32), jax.ShapeDtypeStruct((1, KEY_W), F32))
    scratch = [pltpu.VMEM((GLA_DV, KEY_W), F32)]
    return _fused_call(
        body, comms, name="gla_bwd", grid=(nt,),
        inputs=(p, p, p, p, st_f, d_o, p, p, p, p, st_b, d_o, wd_pad_f, bd_f, wd_pad_b, bd_b),
        in_specs=ins(down) + ins(up) + [wd_spec, bd_spec, wd_spec, bd_spec],
        out_specs=outs(down) + outs(up), out_shape=out_shape * 2, scratch_shapes=scratch * 2)


def _head_rms(o):
    parts, scales = [], []
    for h in range(GLA_HEADS):
        oh = o[:, h * GLA_DV:(h + 1) * GLA_DV]
        r = lax.rsqrt(jnp.mean(oh * oh, axis=-1, keepdims=True) + EPS)
        parts.append(oh * r)
        scales.append(jnp.broadcast_to(r, oh.shape))
    return jnp.concatenate(parts, axis=1), jnp.concatenate(scales, axis=1)


def _layernorm_stats(zv):
    mu = jnp.mean(zv, axis=-1, keepdims=True)
    xc = zv - mu
    rs = lax.rsqrt(jnp.mean(xc * xc, axis=-1, keepdims=True) + EPS)
    return xc * rs, rs


def _mix_fwd(x, o_f, o_b, p, gla_g, ln_g, ln_b, w_sp, b_sp, w_out, tm, comms=()):
    t = x.shape[0]
    nch = tm // GMLP_CHUNK

    def body(x_ref, of_ref, ob_ref, pg_ref, pu_ref, pv_ref, gg_ref, lg_ref, lb_ref, ws_ref, bs_ref, wo_ref,
             x1_ref, y_ref, s_scr):
        on, _ = _head_rms(of_ref[...] + ob_ref[...])
        pg = pg_ref[...].astype(F32)
        y_a = on * gg_ref[...] * (pg * _sigmoid(pg))
        zu = _gelu(pu_ref[...].astype(F32))
        vhat, _ = _layernorm_stats(_gelu(pv_ref[...].astype(F32)))
        vln = (vhat * lg_ref[...] + lb_ref[...]).astype(BF16)
        for g in range(GMLP_GROUPS):
            w_g = ws_ref[g].astype(BF16)
            b_g = bs_ref[g]
            cols = slice(g * LANE, (g + 1) * LANE)
            for n in range(nch):
                rows = slice(n * GMLP_CHUNK, (n + 1) * GMLP_CHUNK)
                s_scr[rows, cols] = jnp.dot(w_g, vln[rows, cols], preferred_element_type=F32) + b_g
        ycat = jnp.concatenate([y_a, zu * s_scr[...]], axis=1).astype(BF16)
        y_ref[...] = ycat
        x1_ref[...] = x_ref[...] + jnp.dot(ycat, wo_ref[...], preferred_element_type=F32)

    half = lambda j: pl.BlockSpec((tm, VAL_W), lambda i: (i, j))
    return _fused_call(
        body, comms, name="mix_fwd", grid=(t // tm,),
        inputs=(x, o_f, o_b, p, p, p, gla_g, ln_g, ln_b, w_sp, b_sp, w_out),
        in_specs=[pl.BlockSpec((tm, D_MODEL), lambda i: (i, 0)), half(0), half(0), half(2), half(3), half(4),
                  _const_spec((1, VAL_W)), _const_spec((1, GMLP_W)), _const_spec((1, GMLP_W)),
                  _const_spec((GMLP_GROUPS, GMLP_CHUNK, GMLP_CHUNK)), _const_spec((GMLP_GROUPS, GMLP_CHUNK, 1)),
                  _const_spec((D_MODEL, D_MODEL))],
        out_specs=(pl.BlockSpec((tm, D_MODEL), lambda i: (i, 0)), pl.BlockSpec((tm, D_MODEL), lambda i: (i, 0))),
        out_shape=(jax.ShapeDtypeStruct((t, D_MODEL), F32), jax.ShapeDtypeStruct((t, D_MODEL), BF16)),
        scratch_shapes=[pltpu.VMEM((tm, GMLP_W), F32)])


def _mix_bwd(dx1, ycat, o_f, o_b, p, gla_g, ln_g, ln_b, w_sp, b_sp, w_out, tm, comms=()):
    t = dx1.shape[0]
    nch = tm // GMLP_CHUNK

    def body(dx1_ref, y_ref, of_ref, ob_ref, pg_ref, pu_ref, pv_ref, gg_ref, lg_ref, lb_ref, ws_ref, bs_ref, wo_ref,
             do_ref, dpg_ref, dpu_ref, dpv_ref, dwo_ref, dgg_ref, dlg_ref, dlb_ref, dws_ref, dbs_ref,
             s_scr, dvln_scr):
        @pl.when(pl.program_id(0) == 0)
        def _():
            for ref in (dwo_ref, dgg_ref, dlg_ref, dlb_ref, dws_ref, dbs_ref):
                ref[...] = jnp.zeros_like(ref)

        dx1 = dx1_ref[...].astype(BF16)
        dycat = _mm_nt(dx1, wo_ref[...])
        dwo_ref[...] += _mm_tn(y_ref[...], dx1)
        dy_a = dycat[:, :VAL_W]
        dy_b = dycat[:, VAL_W:]
        on, r = _head_rms(of_ref[...] + ob_ref[...])
        pg = pg_ref[...].astype(F32)
        sil, dsil = _silu_and_grad(pg)
        gg = gg_ref[...]
        dgg_ref[...] += jnp.sum(dy_a * sil * on, axis=0, keepdims=True)
        don = dy_a * sil * gg
        prod = don * on
        means = jnp.concatenate(
            [jnp.broadcast_to(jnp.mean(prod[:, h * GLA_DV:(h + 1) * GLA_DV], axis=-1, keepdims=True),
                              (tm, GLA_DV)) for h in range(GLA_HEADS)], axis=1)
        do_ref[...] = (r * (don - on * means)).astype(BF16)
        dpg_ref[...] = (dy_a * on * gg * dsil).astype(BF16)
        pu = pu_ref[...].astype(F32)
        pv = pv_ref[...].astype(F32)
        zu, dzu_dpu = _gelu_and_grad(pu)
        zv, dzv_dpv = _gelu_and_grad(pv)
        vhat, rs = _layernorm_stats(zv)
        lg = lg_ref[...]
        vln = (vhat * lg + lb_ref[...]).astype(BF16)
        ds32 = dy_b * zu
        ds = ds32.astype(BF16)
        blocks = [(g, n) for g in range(GMLP_GROUPS) for n in range(nch)]
        at = lambda g, n: (slice(n * GMLP_CHUNK, (n + 1) * GMLP_CHUNK), slice(g * LANE, (g + 1) * LANE))
        w_sp = [ws_ref[g].astype(BF16) for g in range(GMLP_GROUPS)]
        v_blk = {b: vln[at(*b)] for b in blocks}
        ds_blk = {b: ds[at(*b)] for b in blocks}
        s_blk = {b: jnp.dot(w_sp[b[0]], v_blk[b], preferred_element_type=F32) for b in blocks}
        dw_blk = {b: _mm_nt(ds_blk[b], v_blk[b]) for b in blocks}
        dvln_blk = {b: _mm_tn(w_sp[b[0]], ds_blk[b]) for b in blocks}
        for b in blocks:
            s_scr[at(*b)] = s_blk[b] + bs_ref[b[0]]
            dvln_scr[at(*b)] = dvln_blk[b]
        for g in range(GMLP_GROUPS):
            dws_ref[g] += sum(dw_blk[(g, n)] for n in range(nch))
            dbs_ref[g] += sum(jnp.sum(ds32[at(g, n)], axis=-1, keepdims=True) for n in range(nch))
        dpu_ref[...] = (dy_b * s_scr[...] * dzu_dpu).astype(BF16)
        dvln = dvln_scr[...]
        dlg_ref[...] += jnp.sum(dvln * vhat, axis=0, keepdims=True)
        dlb_ref[...] += jnp.sum(dvln, axis=0, keepdims=True)
        dvhat = dvln * lg
        dzv = rs * (dvhat - jnp.mean(dvhat, axis=-1, keepdims=True)
                    - vhat * jnp.mean(dvhat * vhat, axis=-1, keepdims=True))
        dpv_ref[...] = (dzv * dzv_dpv).astype(BF16)

    half = lambda j: pl.BlockSpec((tm, VAL_W), lambda i: (i, j))
    full = pl.BlockSpec((tm, D_MODEL), lambda i: (i, 0))
    sp_shape = (GMLP_GROUPS, GMLP_CHUNK, GMLP_CHUNK)
    bs_shape = (GMLP_GROUPS, GMLP_CHUNK, 1)
    return _fused_call(
        body, comms, name="mix_bwd", grid=(t // tm,),
        inputs=(dx1, ycat, o_f, o_b, p, p, p, gla_g, ln_g, ln_b, w_sp, b_sp, w_out),
        in_specs=[full, full, half(0), half(0), half(2), half(3), half(4),
                  _const_spec((1, VAL_W)), _const_spec((1, GMLP_W)), _const_spec((1, GMLP_W)),
                  _const_spec(sp_shape), _const_spec(bs_shape), _const_spec((D_MODEL, D_MODEL))],
        out_specs=(half(0), half(0), half(0), half(0), _acc_spec((D_MODEL, D_MODEL)), _acc_spec((1, VAL_W)),
                   _acc_spec((1, GMLP_W)), _acc_spec((1, GMLP_W)), _acc_spec(sp_shape), _acc_spec(bs_shape)),
        out_shape=(jax.ShapeDtypeStruct((t, VAL_W), BF16),) * 4 + (
            jax.ShapeDtypeStruct((D_MODEL, D_MODEL), F32), jax.ShapeDtypeStruct((1, VAL_W), F32),
            jax.ShapeDtypeStruct((1, GMLP_W), F32), jax.ShapeDtypeStruct((1, GMLP_W), F32),
            jax.ShapeDtypeStruct(sp_shape, F32), jax.ShapeDtypeStruct(bs_shape, F32)),
        scratch_shapes=[pltpu.VMEM((tm, GMLP_W), F32), pltpu.VMEM((tm, GMLP_W), F32)])


def _rms_bwd(dy_scaled, xn, r):
    return r * (dy_scaled - xn * jnp.mean(dy_scaled * xn, axis=-1, keepdims=True))


def _ffn(x1, target, g2, gf, w_gate, w_up, w_down, tm):
    t = x1.shape[0]

    def body(x1_ref, tg_ref, g2_ref, gf_ref, wg_ref, wu_ref, wd_ref,
             dx1_ref, h2_ref, dgate_ref, dup_ref, act_ref, dx2_ref, loss_ref, dgf_ref, dg2_ref):
        @pl.when(pl.program_id(0) == 0)
        def _():
            for ref in (loss_ref, dgf_ref, dg2_ref):
                ref[...] = jnp.zeros_like(ref)

        x1v = x1_ref[...]
        g2v = g2_ref[...]
        gfv = gf_ref[...]
        r2 = lax.rsqrt(jnp.mean(x1v * x1v, axis=-1, keepdims=True) + EPS)
        xn1 = x1v * r2
        h2 = (xn1 * g2v).astype(BF16)
        h2_ref[...] = h2
        gate = _mm_nt(h2, wg_ref[...])
        up = _mm_nt(h2, wu_ref[...])
        sil, dsil = _silu_and_grad(gate)
        act = (sil * up).astype(BF16)
        act_ref[...] = act
        x2 = x1v + jnp.dot(act, wd_ref[...], preferred_element_type=F32)
        rf = lax.rsqrt(jnp.mean(x2 * x2, axis=-1, keepdims=True) + EPS)
        xn2 = x2 * rf
        err = xn2 * gfv - tg_ref[...]
        loss_ref[...] += 0.5 * jnp.sum(jnp.mean(err * err, axis=-1, keepdims=True))
        dy = err * (1.0 / D_MODEL)
        dgf_ref[...] += jnp.sum(dy * xn2, axis=0, keepdims=True)
        dx2 = _rms_bwd(dy * gfv, xn2, rf)
        dx2b = dx2.astype(BF16)
        dx2_ref[...] = dx2b
        dact = _mm_nt(dx2b, wd_ref[...])
        dgate = (dact * up * dsil).astype(BF16)
        dup = (dact * sil).astype(BF16)
        dgate_ref[...] = dgate
        dup_ref[...] = dup
        dh2 = _mm(dgate, wg_ref[...]) + _mm(dup, wu_ref[...])
        dg2_ref[...] += jnp.sum(dh2 * xn1, axis=0, keepdims=True)
        dx1_ref[...] = dx2 + _rms_bwd(dh2 * g2v, xn1, r2)

    row = lambda w: pl.BlockSpec((tm, w), lambda i: (i, 0))
    return pl.pallas_call(
        body, name="ffn_fwd_bwd", grid=(t // tm,),
        in_specs=[row(D_MODEL), row(D_MODEL), _const_spec((1, D_MODEL)), _const_spec((1, D_MODEL)),
                  _const_spec((D_FF, D_MODEL)), _const_spec((D_FF, D_MODEL)), _const_spec((D_FF, D_MODEL))],
        out_specs=(row(D_MODEL), row(D_MODEL), row(D_FF), row(D_FF), row(D_FF), row(D_MODEL),
                   _acc_spec((8, LANE)), _acc_spec((1, D_MODEL)), _acc_spec((1, D_MODEL))),
        out_shape=(jax.ShapeDtypeStruct((t, D_MODEL), F32), jax.ShapeDtypeStruct((t, D_MODEL), BF16),
                   jax.ShapeDtypeStruct((t, D_FF), BF16), jax.ShapeDtypeStruct((t, D_FF), BF16),
                   jax.ShapeDtypeStruct((t, D_FF), BF16), jax.ShapeDtypeStruct((t, D_MODEL), BF16),
                   jax.ShapeDtypeStruct((8, LANE), F32), jax.ShapeDtypeStruct((1, D_MODEL), F32),
                   jax.ShapeDtypeStruct((1, D_MODEL), F32)),
        compiler_params=_params(),
    )(x1, target, g2, gf, w_gate, w_up, w_down)


def _matmul_tn(a, b, tm, tk, name, comms=()):
    t, m = a.shape
    n = b.shape[1]

    def body(a_ref, b_ref, o_ref):
        @pl.when(pl.program_id(1) == 0)
        def _():
            o_ref[...] = jnp.zeros_like(o_ref)

        o_ref[...] += _mm_tn(a_ref[...], b_ref[...])

    (out,), comm_results = _fused_call(
        body, comms, name=name, grid=(m // tm, t // tk), inputs=(a, b),
        in_specs=[pl.BlockSpec((tk, tm), lambda j, k: (k, j)), pl.BlockSpec((tk, n), lambda j, k: (k, 0))],
        out_specs=(pl.BlockSpec((tm, n), lambda j, k: (j, 0)),),
        out_shape=(jax.ShapeDtypeStruct((m, n), F32),))
    return out, comm_results


def _sum_directions(refs):
    if len(refs) == 1:
        return refs[0][...]
    return (refs[0][...].astype(F32) + refs[1][...].astype(F32)).astype(BF16)


def _grad_w_in_tile(terms, hb, tk, name, comms=()):
    t = hb.shape[0]
    flat = [a for arrays, _, _ in terms for a in arrays]
    rows = sum(width for _, width, _ in terms)

    def body(*refs):
        h_ref, o_ref = refs[len(flat)], refs[len(flat) + 1]

        @pl.when(pl.program_id(0) == 0)
        def _():
            o_ref[...] = jnp.zeros_like(o_ref)

        pieces, i = [], 0
        for arrays, _, _ in terms:
            pieces.append(_sum_directions(refs[i:i + len(arrays)]))
            i += len(arrays)
        o_ref[...] += _mm_tn(jnp.concatenate(pieces, axis=1), h_ref[...])

    (out,), comm_results = _fused_call(
        body, comms, name=name, grid=(t // tk,), inputs=(*flat, hb),
        in_specs=[pl.BlockSpec((tk, width), lambda k, b=block: (k, b)) for arrays, width, block in terms
                  for _ in arrays] + [pl.BlockSpec((tk, D_MODEL), lambda k: (k, 0))],
        out_specs=(_acc_spec((rows, D_MODEL)),), out_shape=(jax.ShapeDtypeStruct((rows, D_MODEL), F32),))
    return out, comm_results


def _in_proj_bwd(x, g1, dx1, dq_f, dq_b, dk_f, dk_b, dv_f, dv_b, dpg, dpu, dpv, dlr_f, dlr_b, w_main, tm, comms=()):
    t = x.shape[0]

    def body(x_ref, g_ref, dx1_ref, dqf, dqb, dkf, dkb, dvf, dvb, dg, du, dv, dlf, dlb, w_ref, dx_ref, dg1_ref):
        @pl.when(pl.program_id(0) == 0)
        def _():
            dg1_ref[...] = jnp.zeros_like(dg1_ref)

        dp = jnp.concatenate([_sum_directions(r) for r in ((dqf, dqb), (dkf, dkb), (dvf, dvb), (dg,), (du,), (dv,),
                                                             (dlf, dlb))], axis=1)
        dh = sum(_mm(dp[:, c0:c0 + r1 - r0], w_ref[r0:r1, :]) for r0, r1, c0 in PROJ_ROWS)
        xv = x_ref[...]
        r = lax.rsqrt(jnp.mean(xv * xv, axis=-1, keepdims=True) + EPS)
        xn = xv * r
        dg1_ref[...] += jnp.sum(dh * xn, axis=0, keepdims=True)
        dx_ref[...] = dx1_ref[...] + _rms_bwd(dh * g_ref[...], xn, r)

    row = lambda w: pl.BlockSpec((tm, w), lambda i: (i, 0))
    return _fused_call(
        body, comms, name="in_proj_bwd", grid=(t // tm,),
        inputs=(x, g1, dx1, dq_f, dq_b, dk_f, dk_b, dv_f, dv_b, dpg, dpu, dpv, dlr_f, dlr_b, w_main),
        in_specs=[row(D_MODEL), _const_spec((1, D_MODEL)), row(D_MODEL), row(KEY_W), row(KEY_W), row(KEY_W),
                  row(KEY_W), row(VAL_W), row(VAL_W), row(VAL_W), row(VAL_W), row(VAL_W), row(LANE), row(LANE),
                  _const_spec((PROJ_W, D_MODEL))],
        out_specs=(row(D_MODEL), _acc_spec((1, D_MODEL))),
        out_shape=(jax.ShapeDtypeStruct((t, D_MODEL), F32), jax.ShapeDtypeStruct((1, D_MODEL), F32)))


def _adamw(w, g, m, v):
    m_new = ADAM_B1 * m + (1.0 - ADAM_B1) * g
    v_new = ADAM_B2 * v + (1.0 - ADAM_B2) * (g * g)
    m_hat = m_new / (1.0 - ADAM_B1 ** ADAM_STEP)
    v_hat = v_new / (1.0 - ADAM_B2 ** ADAM_STEP)
    delta = -ADAM_LR * (m_hat / (jnp.sqrt(v_hat) + ADAM_EPS) + ADAM_WD * w)
    return delta, m_new, v_new


def _adamw_shard(own, recv, w, m, v, tr, name):
    r, c = w.shape

    def body(own_ref, recv_ref, w_ref, m_ref, v_ref, g_ref, d_ref, nm_ref, nv_ref):
        g = own_ref[...]
        for k in range(3):
            g = g + recv_ref[k].astype(F32)
        g_ref[...] = g
        d_ref[...], nm_ref[...], nv_ref[...] = _adamw(w_ref[...], g, m_ref[...], v_ref[...])

    row = pl.BlockSpec((tr, c), lambda i: (i, 0))
    return pl.pallas_call(
        body, name=name, grid=(r // tr,),
        in_specs=[row, pl.BlockSpec((3, tr, c), lambda i: (0, i, 0)), row, row, row],
        out_specs=(row,) * 4, out_shape=(jax.ShapeDtypeStruct((r, c), F32),) * 4,
        compiler_params=_params(),
    )(own, recv, w, m, v)


def _adamw_small(entries):
    stacks = []
    for (g, _, _), _, _, _ in entries:
        if not any(g is s for s in stacks):
            stacks.append(g)
    where = [next(i for i, s in enumerate(stacks) if s is g) for (g, _, _), _, _, _ in entries]
    ns, ne = len(stacks), len(entries)

    def body(*refs):
        s_refs, wmv, outs = refs[:ns], refs[ns:ns + 3 * ne], refs[ns + 3 * ne:]
        for e, ((_, r0, nr), _, _, _) in enumerate(entries):
            grad = s_refs[where[e]][r0:r0 + nr, :]
            w_ref, m_ref, v_ref = wmv[3 * e:3 * e + 3]
            g_ref, d_ref, nm_ref, nv_ref = outs[4 * e:4 * e + 4]
            g_ref[...] = grad
            d_ref[...], nm_ref[...], nv_ref[...] = _adamw(w_ref[...], grad, m_ref[...], v_ref[...])

    results = pl.pallas_call(
        body, name="adamw_small",
        out_shape=tuple(jax.ShapeDtypeStruct(w.shape, F32) for _, w, _, _ in entries for _ in range(4)),
        compiler_params=pltpu.CompilerParams(vmem_limit_bytes=VMEM_LIMIT),
    )(*stacks, *[a for _, w, m, v in entries for a in (w, m, v)])
    return [results[4 * e:4 * e + 4] for e in range(ne)]


def _mesh_pos():
    return lax.axis_index("x"), lax.axis_index("y"), lax.axis_index("c")


def _other_chips(x, y):
    return [(x, 1 - y), (1 - x, y), (1 - x, 1 - y)]


_VMEM_WHOLE = pl.BlockSpec(memory_space=pltpu.VMEM)
_HBM_WHOLE = pl.BlockSpec(memory_space=pl.ANY)


def _gather_comm(shards, cast, mid=((1, 2), (3, 4))):
    na = len(shards)
    staged = [a for a in range(na) if cast[a]]

    def phases(in_refs, out_refs, scr):
        stage = dict(zip(staged, scr[:len(staged)]))
        send_sems, recv_sems, local_sems = scr[len(staged):]
        x, y, c = _mesh_pos()
        me, sibling = (x, y, c), (x, y, 1 - c)
        chip_a, chip_b, diagonal = (x ^ c, y ^ (1 - c)), (x ^ (1 - c), y ^ c), (1 - x, 1 - y)
        srcs = [stage[a] if cast[a] else in_refs[a] for a in range(na)]

        def rows(a, pos):
            px, py, pc = pos
            return out_refs[a].at[4 * px + 2 * py + pc]

        def copy(a, k, block, to, src=None):
            return pltpu.make_async_remote_copy(
                src_ref=rows(a, block) if src is None else src, dst_ref=rows(a, block),
                send_sem=send_sems.at[a, k], recv_sem=recv_sems.at[a, k], device_id=to, device_id_type=MESH_ID)

        mine = [pltpu.make_async_copy(srcs[a], rows(a, me), local_sems.at[a]) for a in range(na)]
        own = [copy(a, k, me, to, src=srcs[a]) for a in range(na)
               for k, to in ((0, sibling), (1, (*chip_a, c)), (2, (*chip_b, c)))]
        onward = [copy(a, 3, (*chip_a, c), (*chip_b, c)) for a in range(na)]
        to_sibling = {k: [copy(a, k, (*chip, c), sibling) for a in range(na)]
                      for k, chip in ((4, chip_a), (5, chip_b), (6, diagonal))}

        def start():
            for a in staged:
                stage[a][...] = in_refs[a][...].astype(BF16)
            for cp in mine + own:
                cp.start()

        def forward_neighbours():
            for a in range(na):
                copy(a, 1, (*chip_a, c), me).wait_recv()
                onward[a].start()
                to_sibling[4][a].start()
            for a in range(na):
                copy(a, 2, (*chip_b, c), me).wait_recv()
                to_sibling[5][a].start()

        def forward_diagonal():
            for a in range(na):
                copy(a, 3, (*diagonal, c), me).wait_recv()
                to_sibling[6][a].start()

        def finish():
            for a in range(na):
                for k, chip in ((0, (x, y)), (4, chip_b), (5, chip_a), (6, diagonal)):
                    copy(a, k, (*chip, 1 - c), me).wait_recv()
            for cp in own + onward + to_sibling[4] + to_sibling[5] + to_sibling[6]:
                cp.wait_send()
            for cp in mine:
                cp.wait()

        return start, forward_neighbours, forward_diagonal, finish

    def before(step, nsteps, in_refs, out_refs, scr):
        start, forward_neighbours, forward_diagonal, _ = phases(in_refs, out_refs, scr)
        pl.when(step == 0)(start)
        pl.when(step == nsteps * mid[0][0] // mid[0][1])(forward_neighbours)
        pl.when(step == nsteps * mid[1][0] // mid[1][1])(forward_diagonal)

    def after(step, nsteps, in_refs, out_refs, scr):
        pl.when(step == nsteps - 1)(phases(in_refs, out_refs, scr)[3])

    return _Comm(
        inputs=list(shards), in_specs=[_VMEM_WHOLE] * na,
        out_shape=[jax.ShapeDtypeStruct((N_DEV,) + s.shape, BF16 if cast[a] else s.dtype)
                   for a, s in enumerate(shards)],
        out_specs=[_HBM_WHOLE] * na,
        scratch_shapes=[pltpu.VMEM(shards[a].shape, BF16) for a in staged] + [
            pltpu.SemaphoreType.DMA((na, 7)), pltpu.SemaphoreType.DMA((na, 7)), pltpu.SemaphoreType.DMA((na,))],
        before=before, after=after)


def _exchange_comm(arrays, out_shape, make_copies):
    na = len(arrays)

    def copies(in_refs, out_refs, scr):
        return make_copies(in_refs, out_refs, *scr)

    def before(step, nsteps, in_refs, out_refs, scr):
        @pl.when(step == 0)
        def _():
            for cp in copies(in_refs, out_refs, scr):
                cp.start()

    def after(step, nsteps, in_refs, out_refs, scr):
        @pl.when(step == nsteps - 1)
        def _():
            for cp in copies(in_refs, out_refs, scr):
                cp.wait()

    return _Comm(inputs=list(arrays), in_specs=[_HBM_WHOLE] * na, out_shape=list(out_shape),
                 out_specs=[_HBM_WHOLE] * na,
                 scratch_shapes=[pltpu.SemaphoreType.DMA((na, 3)), pltpu.SemaphoreType.DMA((na, 3))],
                 before=before, after=after)


def _sibling_exchange_comm(grads):
    def make_copies(in_refs, out_refs, send_sems, recv_sems):
        x, y, c = _mesh_pos()
        return [pltpu.make_async_remote_copy(
            src_ref=in_refs[a].at[:, pl.ds(1 - c, 1)], dst_ref=out_refs[a], send_sem=send_sems.at[a, 0],
            recv_sem=recv_sems.at[a, 0], device_id=(x, y, 1 - c), device_id_type=MESH_ID)
            for a in range(len(grads))]

    return _exchange_comm(grads, [jax.ShapeDtypeStruct((4, 1) + g.shape[2:], F32) for g in grads], make_copies)


def _chips_exchange_comm(partials):
    def make_copies(in_refs, out_refs, send_sems, recv_sems):
        x, y, c = _mesh_pos()
        return [pltpu.make_async_remote_copy(
            src_ref=in_refs[a].at[j], dst_ref=out_refs[a].at[j], send_sem=send_sems.at[a, j],
            recv_sem=recv_sems.at[a, j], device_id=(*chip, c), device_id_type=MESH_ID)
            for a in range(len(partials)) for j, chip in enumerate(_other_chips(x, y))]

    return _exchange_comm(partials, [jax.ShapeDtypeStruct(g.shape, BF16) for g in partials], make_copies)


def _comm_only(comms, name):
    return _fused_call(lambda: None, comms, name=name, grid=(1,), inputs=(), in_specs=[], out_specs=(),
                       out_shape=())[1]


def _chip_sum(my_pos, mine, from_sibling, tr, name):
    _, _, r, c = mine.shape

    def body(pos_ref, a_ref, b_ref, own_ref, out_ref):
        s = a_ref[0, 0] + b_ref[0, 0]

        @pl.when(pl.program_id(1) == 0)
        def _():
            own_ref[...] = s

        @pl.when(pl.program_id(1) > 0)
        def _():
            out_ref[0] = s.astype(BF16)

    grid_spec = pltpu.PrefetchScalarGridSpec(
        num_scalar_prefetch=1, grid=(r // tr, 4),
        in_specs=[pl.BlockSpec((1, 1, tr, c), lambda i, k, pos: (pos[0] ^ k, pos[1], i, 0)),
                  pl.BlockSpec((1, 1, tr, c), lambda i, k, pos: (pos[0] ^ k, 0, i, 0))],
        out_specs=(pl.BlockSpec((tr, c), lambda i, k, pos: (i, 0)),
                   pl.BlockSpec((1, tr, c), lambda i, k, pos: (jnp.maximum(k - 1, 0), i, 0))))
    return pl.pallas_call(
        body, name=name, grid_spec=grid_spec,
        out_shape=(jax.ShapeDtypeStruct((r, c), F32), jax.ShapeDtypeStruct((3, r, c), BF16)),
        compiler_params=_params(2),
    )(my_pos, mine, from_sibling)


def _all_reduce_small_comm(parts):
    na = len(parts)

    def copies(in_refs, scr):
        gathered, (send_sems, recv_sems) = scr[:na], scr[na:]
        x, y, c = _mesh_pos()
        my_id = 4 * x + 2 * y + c
        return my_id, [pltpu.make_async_remote_copy(
            src_ref=in_refs[a], dst_ref=gathered[a].at[my_id], send_sem=send_sems.at[a, k - 1],
            recv_sem=recv_sems.at[a, k - 1], device_id=(x ^ (k >> 2), y ^ ((k >> 1) & 1), c ^ (k & 1)),
            device_id_type=MESH_ID) for a in range(na) for k in range(1, N_DEV)]

    def before(step, nsteps, in_refs, out_refs, scr):
        @pl.when(step == 0)
        def _():
            for cp in copies(in_refs, scr)[1]:
                cp.start()

    def after(step, nsteps, in_refs, out_refs, scr):
        @pl.when(step == nsteps - 1)
        def _():
            my_id, cps = copies(in_refs, scr)
            for a in range(na):
                scr[a][my_id] = in_refs[a][...]
            for cp in cps:
                cp.wait()
            for a in range(na):
                acc = scr[a][0]
                for d in range(1, N_DEV):
                    acc = acc + scr[a][d]
                out_refs[a][...] = acc

    return _Comm(inputs=list(parts), in_specs=[_VMEM_WHOLE] * na,
                 out_shape=[jax.ShapeDtypeStruct(p.shape, F32) for p in parts], out_specs=[_VMEM_WHOLE] * na,
                 scratch_shapes=[pltpu.VMEM((N_DEV,) + p.shape, F32) for p in parts] + [
                     pltpu.SemaphoreType.DMA((na, N_DEV - 1)), pltpu.SemaphoreType.DMA((na, N_DEV - 1))],
                 before=before, after=after)


def _unshard_cols(g):
    return jnp.transpose(g, (1, 0, 2)).reshape(g.shape[1], N_DEV * g.shape[2])


def _row_blocks(w):
    return w.reshape(4, 2, w.shape[0] // N_DEV, w.shape[1])


def _stack_rows(parts):
    a = jnp.concatenate(parts, axis=0)
    return jnp.pad(a, ((0, (-a.shape[0]) % 8), (0, 0)))


W_IN_GRAD_TILE = PROJ_PAD // 3


def _w_in_grad_blocks(tiles):
    def padded(a, b):
        out = []
        for j, tile in enumerate(tiles):
            lo, hi = max(a, j * W_IN_GRAD_TILE), min(b, (j + 1) * W_IN_GRAD_TILE)
            if lo < hi:
                out.append(tile[lo - j * W_IN_GRAD_TILE:hi - j * W_IN_GRAD_TILE])
        return out

    return _row_blocks(jnp.concatenate(
        padded(0, LR_REF) + padded(LR_COL, LR_COL + 2 * LOWRANK) + padded(LR_REF, LR_COL), axis=0))


def _padded_decay_weights(wd_f, wd_b):
    zeros = lambda n: jnp.zeros((n, KEY_W), F32)
    return (jnp.concatenate([wd_f, zeros(LANE - LOWRANK)], axis=0),
            jnp.concatenate([zeros(LOWRANK), wd_b, zeros(LANE - 2 * LOWRANK)], axis=0))


def kernel(x, norm1_g, w_in,w_decay_f, b_decay_f, w_decay_b, b_decay_b, gla_norm_g, gmlp_ln_g, gmlp_ln_b, w_spatial, b_spatial, w_out, norm2_g, w_gate, w_up, w_down, final_norm_g, loss_target, m_norm1_g, m_w_in, m_w_decay_f, m_b_decay_f, m_w_decay_b, m_b_decay_b, m_gla_norm_g, m_gmlp_ln_g, m_gmlp_ln_b, m_w_spatial, m_b_spatial, m_w_out, m_norm2_g, m_w_gate, m_w_up, m_w_down, m_final_norm_g, v_norm1_g, v_w_in, v_w_decay_f, v_b_decay_f, v_w_decay_b, v_b_decay_b, v_gla_norm_g, v_gmlp_ln_g, v_gmlp_ln_b, v_w_spatial, v_b_spatial, v_w_out, v_norm2_g, v_w_gate, v_w_up, v_w_down, v_final_norm_g):
    t = x.shape[1]
    xt = x[0]
    target = loss_target[0]
    pos_x, pos_y, pos_c = _mesh_pos()
    my_pos = jnp.stack([2 * pos_x + pos_y, pos_c]).astype(jnp.int32)
    my_id = 4 * pos_x + 2 * pos_y + pos_c

    tile = lambda n: min(n, t)
    ln_g, ln_b, w_sp = gmlp_ln_g, gmlp_ln_b, w_spatial[0]
    b_sp_col = b_spatial[0][:, :, None]
    shard = {"w_in": w_in[0].T, "w_out": w_out[0], "w_gate": w_gate[0].T, "w_up": w_up[0].T, "w_down": w_down[0]}
    shard_m = {"w_in": m_w_in[0].T, "w_out": m_w_out[0], "w_gate": m_w_gate[0].T, "w_up": m_w_up[0].T,
               "w_down": m_w_down[0]}
    shard_v = {"w_in": v_w_in[0].T, "w_out": v_w_out[0], "w_gate": v_w_gate[0].T, "w_up": v_w_up[0].T,
               "w_down": v_w_down[0]}
    transposed = ("w_in", "w_gate", "w_up")
    chip_sum = lambda n, g, s: _chip_sum(my_pos, g, s[0], g.shape[2], "chip_sum_" + n)

    decay_shard = jnp.stack([w_decay_f[0], w_decay_b[0]])
    (hb,), ((g_in, g_decay),) = _norm1(xt, norm1_g, tile(TOKEN_TILE["norm1"]),
                                       [_gather_comm([shard["w_in"], decay_shard], [True, False])])
    w_in_t = g_in.reshape(PROJ_W, D_MODEL)
    wd_pad_f, wd_pad_b = _padded_decay_weights(_unshard_cols(g_decay[:, 0]), _unshard_cols(g_decay[:, 1]))
    (p,), ((g_gate, g_out),) = _in_proj(
        hb, w_in_t, tile(TOKEN_TILE["in_proj"]), [_gather_comm([shard["w_gate"], shard["w_out"]], [True, True])])
    (o_f, st_f, o_b, st_b), ((g_up,),) = _gla_fwd(
        p, wd_pad_f, b_decay_f, wd_pad_b, b_decay_b, tile(TOKEN_TILE["gla"]), [_gather_comm([shard["w_up"]], [True])])
    w_out_full = g_out.reshape(D_MODEL, D_MODEL)
    (x1, ycat), ((g_down,),) = _mix_fwd(xt, o_f, o_b, p, gla_norm_g, ln_g, ln_b, w_sp, b_sp_col, w_out_full,
                                        tile(TOKEN_TILE["mix_fwd"]), [_gather_comm([shard["w_down"]], [True])])

    dx1, h2b, dgate, dup, act, dx2, loss_acc, d_gf, d_g2 = _ffn(
        x1, target, norm2_g, final_norm_g[None, :], g_gate.reshape(D_FF, D_MODEL), g_up.reshape(D_FF, D_MODEL),
        g_down.reshape(D_FF, D_MODEL), tile(TOKEN_TILE["ffn"]))
    dw_gate, _ = _matmul_tn(dgate, h2b, D_FF // 2, tile(TOKEN_TILE["dw"]), "grad_w_gate")
    dw_up, _ = _matmul_tn(dup, h2b, D_FF // 2, tile(TOKEN_TILE["dw"]), "grad_w_up")
    dw_down, _ = _matmul_tn(act, dx2, D_FF // 2, tile(TOKEN_TILE["dw"]), "grad_w_down")

    ffn_grads = [_row_blocks(dw_gate), _row_blocks(dw_up), _row_blocks(dw_down)]
    (d_o, dpg, dpu, dpv, dw_out, d_gg, d_lg, d_lb, dw_sp, db_sp), (ffn_sib,) = _mix_bwd(
        dx1, ycat, o_f, o_b, p, gla_norm_g, ln_g, ln_b, w_sp, b_sp_col, w_out_full,
        tile(TOKEN_TILE["mix_bwd"]),
        [_sibling_exchange_comm(ffn_grads)])
    ffn_names = ["w_gate", "w_up", "w_down"]
    ffn_sums = [chip_sum(n, g, [s]) for n, g, s in zip(ffn_names, ffn_grads, ffn_sib)]
    out_grad = _row_blocks(dw_out)
    (dq_f, dk_f, dv_f, dlr_f, dwd_f, dbd_f, dq_b, dk_b, dv_b, dlr_b, dwd_b, dbd_b), (ffn_recv, out_sib) = _gla_bwd(
        p, wd_pad_f, b_decay_f, wd_pad_b, b_decay_b, st_f, st_b, d_o, tile(TOKEN_TILE["gla"]),
        [_chips_exchange_comm([s[1] for s in ffn_sums]), _sibling_exchange_comm([out_grad])])
    out_sum = chip_sum("w_out", out_grad, out_sib)
    (grad_x, d_g1), _ = _in_proj_bwd(
        xt, norm1_g, dx1, dq_f, dq_b, dk_f, dk_b, dv_f, dv_b, dpg, dpu, dpv, dlr_f, dlr_b, w_in_t,
        tile(TOKEN_TILE["in_proj_bwd"]))

    stacks = [_stack_rows([d_g1, d_g2, d_gf]), _stack_rows([d_gg, d_lg, d_lb]),
              _stack_rows([dbd_f, dbd_b, jnp.zeros((DECAY_W_ROW - 2, KEY_W), F32), dwd_f[:LOWRANK],
                           dwd_b[LOWRANK:2 * LOWRANK]]),
              _stack_rows([dw_sp.reshape(GMLP_W, GMLP_CHUNK), db_sp[:, :, 0], loss_acc[:1]])]
    gla_dirs = lambda f, b, width, block: ((f, b), width, block)
    in_tiles = [
        [gla_dirs(dq_f, dq_b, KEY_W, 0), gla_dirs(dk_f, dk_b, KEY_W, 0), gla_dirs(dv_f, dv_b, 3 * LANE, 0)],
        [gla_dirs(dv_f, dv_b, LANE, 3), ((dpg,), VAL_W, 0), ((dpu,), GMLP_W // 2, 0)],
        [((dpu,), GMLP_W // 2, 1), ((dpv,), GMLP_W, 0), gla_dirs(dlr_f, dlr_b, LANE, 0)]]
    dw_in_0, _ = _grad_w_in_tile(in_tiles[0], hb, tile(TOKEN_TILE["dw"]), "grad_w_in_0")
    dw_in_1, _ = _grad_w_in_tile(in_tiles[1], hb, tile(TOKEN_TILE["dw"]), "grad_w_in_1")
    dw_in_2, (small_sums, out_recv) = _grad_w_in_tile(
        in_tiles[2], hb, tile(TOKEN_TILE["dw"]), "grad_w_in_2",
        [_all_reduce_small_comm(stacks), _chips_exchange_comm([out_sum[1]])])
    in_grad = _w_in_grad_blocks([dw_in_0, dw_in_1, dw_in_2])
    (in_sib,) = _comm_only([_sibling_exchange_comm([in_grad])], "grad_w_in_exchange_sibling")
    in_sum = chip_sum("w_in", in_grad, in_sib)
    (in_recv,) = _comm_only([_chips_exchange_comm([in_sum[1]])], "grad_w_in_exchange_chips")

    names = ["w_in", "w_out", "w_gate", "w_up", "w_down"]
    sums = [in_sum, out_sum] + ffn_sums
    received = [in_recv[0], out_recv[0]] + list(ffn_recv)
    big_out = {}
    for n, s, rc in zip(names, sums, received):
        rows = shard[n].shape[0]
        half = rows // 2 if rows % 32 == 0 else rows
        res = _adamw_shard(s[0], rc, shard[n], shard_m[n], shard_v[n], half, "adamw_" + n)
        big_out[n] = [r.T if n in transposed else r for r in res]

    s1024, s512, s256, s128 = small_sums
    loss = s128[GMLP_W + GMLP_GROUPS, 0]
    col0 = my_id * (KEY_W // N_DEV)
    decay_cols = lambda row0: lax.dynamic_slice(s256, (row0, col0), (LOWRANK, KEY_W // N_DEV))
    flat = lambda a: a.reshape(-1, a.shape[-1])
    small = {
        "norm1_g": ((s1024, 0, 1), norm1_g, m_norm1_g, v_norm1_g),
        "w_decay_f": ((decay_cols(DECAY_W_ROW), 0, LOWRANK), w_decay_f, m_w_decay_f, v_w_decay_f),
        "b_decay_f": ((s256, 0, 1), b_decay_f, m_b_decay_f, v_b_decay_f),
        "w_decay_b": ((decay_cols(DECAY_W_ROW + LOWRANK), 0, LOWRANK), w_decay_b, m_w_decay_b, v_w_decay_b),
        "b_decay_b": ((s256, 1, 1), b_decay_b, m_b_decay_b, v_b_decay_b),
        "gla_norm_g": ((s512, 0, 1), gla_norm_g, m_gla_norm_g, v_gla_norm_g),
        "gmlp_ln_g": ((s512, 1, 1), gmlp_ln_g, m_gmlp_ln_g, v_gmlp_ln_g),
        "gmlp_ln_b": ((s512, 2, 1), gmlp_ln_b, m_gmlp_ln_b, v_gmlp_ln_b),
        "w_spatial": ((s128, 0, GMLP_W), w_spatial, m_w_spatial, v_w_spatial),
        "b_spatial": ((s128, GMLP_W, GMLP_GROUPS), b_spatial, m_b_spatial, v_b_spatial),
        "norm2_g": ((s1024, 1, 1), norm2_g, m_norm2_g, v_norm2_g),
        "final_norm_g": ((s1024, 2, 1), final_norm_g, m_final_norm_g, v_final_norm_g),
    }
    small_res = _adamw_small([(g, flat(w), flat(m), flat(v)) for g, w, m, v in small.values()])
    small_out = {n: [r.reshape(small[n][1].shape) for r in res] for n, res in zip(small, small_res)}

    order = ["norm1_g", "w_in", "w_decay_f", "b_decay_f", "w_decay_b", "b_decay_b", "gla_norm_g", "gmlp_ln_g",
             "gmlp_ln_b", "w_spatial", "b_spatial", "w_out", "norm2_g", "w_gate", "w_up", "w_down", "final_norm_g"]
    outs = []
    for kind in range(4):
        for n in order:
            outs.append(big_out[n][kind][None] if n in big_out else small_out[n][kind])
    return (loss, grad_x[None], *outs)
```

```python
import functools
import math

import jax
import jax.numpy as jnp
from jax import lax
from jax.experimental import pallas as pl
from jax.experimental.pallas import tpu as pltpu

F32 = jnp.float32
BF16 = jnp.bfloat16

D_MODEL = 1024
GLA_HEADS = 4
GLA_DK = 64
GLA_DV = 128
KEY_W = GLA_HEADS * GLA_DK
VAL_W = GLA_HEADS * GLA_DV
LOWRANK = 16
GLA_TAU = 16.0
GLA_CHUNK = 64
GMLP_W = 512
GMLP_GROUPS = 4
GMLP_CHUNK = 128
D_FF = 2816
EPS = 1e-6
Q_SCALE = GLA_DK ** -0.5
PROJ_PAD = 2688
LR_COL = 2560
LANE = 128
N_DEV = 8

ADAM_LR = 0.001
ADAM_B1 = 0.9
ADAM_B2 = 0.999
ADAM_EPS = 1e-08
ADAM_WD = 0.01
ADAM_STEP = 10

VMEM_LIMIT = 56 * 1024 * 1024
TOKEN_TILE = {"norm1": 512, "in_proj": 512, "gla": 1024, "mix_fwd": 1024, "ffn": 256, "mix_bwd": 512,
              "in_proj_bwd": 512, "dw": 2048, "dw_in": 1024}
DECAY_W_ROW = 8
MESH_ID = pl.DeviceIdType.MESH
INV_SQRT2 = 0.7071067811865476
INV_SQRT_2PI = 0.3989422804014327


def _params(n_axes=1):
    return pltpu.CompilerParams(dimension_semantics=("arbitrary",) * n_axes, vmem_limit_bytes=VMEM_LIMIT)


def _mm(a, b):
    return jnp.dot(a.astype(BF16), b.astype(BF16), preferred_element_type=F32)


def _mm_nt(a, b):
    return lax.dot_general(a.astype(BF16), b.astype(BF16), (((1,), (1,)), ((), ())), preferred_element_type=F32)


def _mm_tn(a, b):
    return lax.dot_general(a.astype(BF16), b.astype(BF16), (((0,), (0,)), ((), ())), preferred_element_type=F32)


def _const_spec(shape):
    nd = len(shape)
    return pl.BlockSpec(shape, lambda *_: (0,) * nd, pipeline_mode=pl.Buffered(1))


def _acc_spec(shape):
    nd = len(shape)
    return pl.BlockSpec(shape, lambda *_: (0,) * nd)


class _Comm:
    def __init__(self, inputs, in_specs, out_shape, out_specs, scratch_shapes, before, after):
        self.inputs, self.in_specs, self.out_shape, self.out_specs = inputs, in_specs, out_shape, out_specs
        self.scratch_shapes, self.before, self.after = scratch_shapes, before, after


def _fused_call(body, comms, *, name, grid, inputs, in_specs, out_specs, out_shape, scratch_shapes=()):
    n_in, n_out, n_scr = len(in_specs), len(out_specs), len(scratch_shapes)
    nsteps = math.prod(grid)
    sizes = [(len(c.inputs), len(c.out_shape), len(c.scratch_shapes)) for c in comms]

    def full_body(*refs):
        step = pl.program_id(0)
        for axis in range(1, len(grid)):
            step = step * grid[axis] + pl.program_id(axis)
        ins, rest = refs[:n_in], refs[n_in:]
        c_ins = []
        for ci, _, _ in sizes:
            c_ins.append(rest[:ci])
            rest = rest[ci:]
        outs, rest = rest[:n_out], rest[n_out:]
        c_outs = []
        for _, co, _ in sizes:
            c_outs.append(rest[:co])
            rest = rest[co:]
        scr, rest = rest[:n_scr], rest[n_scr:]
        c_scr = []
        for _, _, cs in sizes:
            c_scr.append(rest[:cs])
            rest = rest[cs:]
        for c, a, b, s in zip(comms, c_ins, c_outs, c_scr):
            c.before(step, nsteps, a, b, s)
        body(*ins, *outs, *scr)
        for c, a, b, s in zip(comms, c_ins, c_outs, c_scr):
            c.after(step, nsteps, a, b, s)

    results = pl.pallas_call(
        full_body, name=name, grid=grid,
        in_specs=list(in_specs) + [s for c in comms for s in c.in_specs],
        out_specs=tuple(out_specs) + tuple(s for c in comms for s in c.out_specs),
        out_shape=tuple(out_shape) + tuple(s for c in comms for s in c.out_shape),
        scratch_shapes=list(scratch_shapes) + [s for c in comms for s in c.scratch_shapes],
        compiler_params=_params(len(grid)),
    )(*inputs, *[a for c in comms for a in c.inputs])
    own, rest = results[:n_out], results[n_out:]
    comm_results = []
    for _, co, _ in sizes:
        comm_results.append(rest[:co])
        rest = rest[co:]
    return own, comm_results


def _gelu(x):
    return 0.5 * x * (1.0 + lax.erf(x * INV_SQRT2))


def _gelu_and_grad(x):
    cdf = 0.5 * (1.0 + lax.erf(x * INV_SQRT2))
    return x * cdf, cdf + x * jnp.exp(-0.5 * x * x) * INV_SQRT_2PI


def _sigmoid(x):
    return 0.5 + 0.5 * jnp.tanh(0.5 * x)


def _silu_and_grad(x):
    s = _sigmoid(x)
    return x * s, s * (1.0 + x * (1.0 - s))


def _norm1(x, g1, tm, comms=()):
    t = x.shape[0]

    def body(x_ref, g_ref, h_ref):
        xv = x_ref[...]
        r = lax.rsqrt(jnp.mean(xv * xv, axis=-1, keepdims=True) + EPS)
        h_ref[...] = (xv * r * g_ref[...]).astype(BF16)

    row = pl.BlockSpec((tm, D_MODEL), lambda i: (i, 0))
    return _fused_call(body, comms, name="norm1", grid=(t // tm,), inputs=(x, g1),
                       in_specs=[row, _const_spec((1, D_MODEL))], out_specs=(row,),
                       out_shape=(jax.ShapeDtypeStruct((t, D_MODEL), BF16),))


PROJ_W = 2592
LR_REF = 1536
PROJ_ROWS = ((0, LR_REF, 0), (LR_REF + 2 * LOWRANK, PROJ_W, LR_REF), (LR_REF, LR_REF + LANE, LR_COL))


def _in_proj(h, w_in_t, tm, comms=()):
    t = h.shape[0]

    def body(h_ref, w_ref, p_ref):
        hv = h_ref[...]
        for r0, r1, c0 in PROJ_ROWS:
            p_ref[:, c0:c0 + r1 - r0] = _mm_nt(hv, w_ref[r0:r1, :]).astype(BF16)

    return _fused_call(
        body, comms, name="in_proj", grid=(t // tm,), inputs=(h, w_in_t),
        in_specs=[pl.BlockSpec((tm, D_MODEL), lambda i: (i, 0)), _const_spec((PROJ_W, D_MODEL))],
        out_specs=(pl.BlockSpec((tm, PROJ_PAD), lambda i: (i, 0)),),
        out_shape=(jax.ShapeDtypeStruct((t, PROJ_PAD), BF16),))


def _tri(upper):
    r = lax.broadcasted_iota(jnp.int32, (GLA_CHUNK, GLA_CHUNK), 0)
    c = lax.broadcasted_iota(jnp.int32, (GLA_CHUNK, GLA_CHUNK), 1)
    return jnp.where((c >= r) if upper else (c <= r), 1.0, 0.0).astype(BF16)


def _chunk_cumsum(tri, a, add=None):
    hi = a.astype(BF16)
    lo = (a - hi.astype(F32)).astype(BF16)
    dot = functools.partial(jnp.dot, preferred_element_type=F32)
    sums = [dot(tri, hi[_chunk_rows(c)]) + dot(tri, lo[_chunk_rows(c)]) for c in range(a.shape[0] // GLA_CHUNK)]
    return jnp.concatenate(sums if add is None else [s + r for s, r in zip(sums, add)], axis=0)


def _chunk_rows(c):
    return slice(c * GLA_CHUNK, (c + 1) * GLA_CHUNK)


def _gla_masks(rev):
    dk_bits, dv_bits = GLA_DK.bit_length() - 1, GLA_DV.bit_length() - 1
    key_head = lax.broadcasted_iota(jnp.int32, (GLA_CHUNK, KEY_W), 1) >> dk_bits
    val_head = lax.broadcasted_iota(jnp.int32, (GLA_CHUNK, VAL_W), 1) >> dv_bits
    t = lax.broadcasted_iota(jnp.int32, (GLA_HEADS * GLA_CHUNK, GLA_CHUNK), 0) & (GLA_CHUNK - 1)
    s = lax.broadcasted_iota(jnp.int32, (GLA_HEADS * GLA_CHUNK, GLA_CHUNK), 1)
    return key_head, val_head, (s >= t) if rev else (s <= t)


def _stack_heads(a, head_of_lane):
    a = a.astype(BF16)
    return jnp.concatenate([jnp.where(head_of_lane == h, a, jnp.zeros_like(a)) for h in range(GLA_HEADS)], axis=0)


def _rows_by_head(a):
    return jnp.concatenate([a[:, h * GLA_DV:(h + 1) * GLA_DV] for h in range(GLA_HEADS)], axis=0)


def _lanes_by_head(r):
    return jnp.concatenate([r[h * GLA_CHUNK:(h + 1) * GLA_CHUNK] for h in range(GLA_HEADS)], axis=1)


def _head_diagonal(r, head_of_lane):
    rows = r.shape[0] // GLA_HEADS
    out = jnp.where(head_of_lane == 0, r[:rows], 0.0)
    for h in range(1, GLA_HEADS):
        out = out + jnp.where(head_of_lane == h, r[h * rows:(h + 1) * rows], 0.0)
    return out


def _tile_terms(la, q, k, tri, rev):
    nc = la.shape[0] // GLA_CHUNK
    q, k = q.astype(F32), k.astype(F32)
    b = _chunk_cumsum(tri, la)
    ebl = [jnp.exp(b[c * GLA_CHUNK:c * GLA_CHUNK + 1] if rev else b[(c + 1) * GLA_CHUNK - 1:(c + 1) * GLA_CHUNK])
           for c in range(nc)]
    eb = jnp.exp(b)
    enb = jnp.exp(-b)
    kd = k * enb
    ke = jnp.concatenate([kd[_chunk_rows(c)] * ebl[c] for c in range(nc)], axis=0)
    return ebl, eb, enb, q * Q_SCALE * eb, kd, ke


def _log_decay(lr_ref, wd_ref, bd_ref):
    z = _mm(lr_ref[...], wd_ref[...]) + bd_ref[...]
    return z, jax.nn.log_sigmoid(z) * (1.0 / GLA_TAU)


def _p_specs(tg, tile):
    return [pl.BlockSpec((tg, KEY_W), lambda i: (tile(i), 0)),
            pl.BlockSpec((tg, KEY_W), lambda i: (tile(i), 1)),
            pl.BlockSpec((tg, VAL_W), lambda i: (tile(i), 1)),
            pl.BlockSpec((tg, LANE), lambda i: (tile(i), LR_COL // LANE))]


def _gla_fwd_dir(rev, nc, q_ref, k_ref, v_ref, lr_ref, wd_ref, bd_ref, o_ref, st_ref, state):
    key_head, _, causal = _gla_masks(rev)
    order = range(nc - 1, -1, -1) if rev else range(nc)

    def intra():
        _, la = _log_decay(lr_ref, wd_ref, bd_ref)
        ebl, _, _, qd, kd, ke = _tile_terms(la, q_ref[...], k_ref[...], _tri(rev), rev)
        kd = kd.astype(BF16)
        v = {c: v_ref[_chunk_rows(c), :].astype(BF16) for c in order}
        qd_stack = {c: _stack_heads(qd[_chunk_rows(c)], key_head) for c in order}
        ke_stack = {c: _stack_heads(ke[_chunk_rows(c)], key_head) for c in order}
        a_all = {c: _mm_nt(qd_stack[c], kd[_chunk_rows(c)]) for c in order}
        a_all = {c: jnp.where(causal, a_all[c], 0.0).astype(BF16) for c in order}
        head_rows = lambda a, h: a[h * GLA_CHUNK:(h + 1) * GLA_CHUNK]
        head_vals = lambda a, h: a[:, h * GLA_DV:(h + 1) * GLA_DV]
        r = {c: [_mm(head_rows(a_all[c], h), head_vals(v[c], h)) for h in range(GLA_HEADS)] for c in order}
        upd = {c: _mm_tn(_rows_by_head(v[c]), ke_stack[c]) for c in order}
        return {c: (ebl[c], qd_stack[c], r[c], upd[c]) for c in order}

    def scan(terms):
        st = state[...]
        states = {}
        for c in order:
            states[c] = st
            st_ref[c] = st.astype(BF16)
            st = st * terms[c][0] + terms[c][3]
        state[...] = st
        return states

    def inter(terms, states):
        r_inter = {c: _mm_nt(terms[c][1], states[c]) for c in order}
        for c in order:
            o_ref[_chunk_rows(c), :] = jnp.concatenate(
                [terms[c][2][h] + r_inter[c][h * GLA_CHUNK:(h + 1) * GLA_CHUNK] for h in range(GLA_HEADS)], axis=1)

    return intra, scan, inter


def _gla_fwd(p, wd_pad_f, bd_f, wd_pad_b, bd_b, tg, comms=()):
    t = p.shape[0]
    nt = t // tg
    nc = tg // GLA_CHUNK
    up, down = (lambda i: i), (lambda i: nt - 1 - i)

    def body(qf, kf, vf, lrf, qb, kb, vb, lrb, wdf, bdf, wdb, bdb, of, stf, ob, stb, state_f, state_b):
        @pl.when(pl.program_id(0) == 0)
        def _():
            state_f[...] = jnp.zeros_like(state_f)
            state_b[...] = jnp.zeros_like(state_b)

        dirs = [_gla_fwd_dir(False, nc, qf, kf, vf, lrf, wdf, bdf, of, stf, state_f),
                _gla_fwd_dir(True, nc, qb, kb, vb, lrb, wdb, bdb, ob, stb, state_b)]
        terms = [intra() for intra, _, _ in dirs]
        states = [scan(t) for (_, scan, _), t in zip(dirs, terms)]
        for (_, _, inter), t, s in zip(dirs, terms, states):
            inter(t, s)

    wd_spec, bd_spec = _const_spec((LANE, KEY_W)), _const_spec((1, KEY_W))
    outs = lambda tile: (pl.BlockSpec((tg, VAL_W), lambda i: (tile(i), 0)),
                         pl.BlockSpec((nc, GLA_DV, KEY_W), lambda i: (tile(i), 0, 0)))
    out_shape = (jax.ShapeDtypeStruct((t, VAL_W), F32), jax.ShapeDtypeStruct((t // GLA_CHUNK, GLA_DV, KEY_W), BF16))
    return _fused_call(
        body, comms, name="gla_fwd", grid=(nt,), inputs=(p,) * 8 + (wd_pad_f, bd_f, wd_pad_b, bd_b),
        in_specs=_p_specs(tg, up) + _p_specs(tg, down) + [wd_spec, bd_spec, wd_spec, bd_spec],
        out_specs=outs(up) + outs(down), out_shape=out_shape * 2,
        scratch_shapes=[pltpu.VMEM((GLA_DV, KEY_W), F32)] * 2)


def _gla_bwd_dir(rev, nc, q_ref, k_ref, v_ref, lr_ref, wd_ref, bd_ref, st_ref, do_ref,
                 dq_ref, dk_ref, dv_ref, dlr_ref, dwd_ref, dbd_ref, dstate):
    key_head, val_head, causal = _gla_masks(rev)
    order = range(nc) if rev else range(nc - 1, -1, -1)

    def intra():
        z, la = _log_decay(lr_ref, wd_ref, bd_ref)
        tile = _tile_terms(la, q_ref[...], k_ref[...], _tri(rev), rev)
        qd, kd = tile[3], tile[4].astype(BF16)
        v = {c: v_ref[_chunk_rows(c), :].astype(BF16) for c in order}
        d_o = {c: do_ref[_chunk_rows(c), :] for c in order}
        kd_c = {c: kd[_chunk_rows(c)] for c in order}
        qd_stack = {c: _stack_heads(qd[_chunk_rows(c)], key_head) for c in order}
        do_stack = {c: _stack_heads(d_o[c], val_head) for c in order}
        do_rows = {c: _rows_by_head(d_o[c]) for c in order}
        a_all = {c: _mm_nt(qd_stack[c], kd_c[c]) for c in order}
        head_vals = lambda a, h: a[:, h * GLA_DV:(h + 1) * GLA_DV]
        da_all = {c: jnp.concatenate([_mm_nt(head_vals(d_o[c], h), head_vals(v[c], h)) for h in range(GLA_HEADS)],
                                     axis=0) for c in order}
        a_all = {c: jnp.where(causal, a_all[c], 0.0).astype(BF16) for c in order}
        da_all = {c: jnp.where(causal, da_all[c], 0.0).astype(BF16) for c in order}
        dv = {c: _mm_tn(a_all[c], do_stack[c]) for c in order}
        dqd = {c: _mm(jnp.concatenate([do_rows[c], da_all[c]], axis=1),
                      jnp.concatenate([st_ref[c], kd_c[c]], axis=0)) for c in order}
        dkd = {c: _mm_tn(da_all[c], qd_stack[c]) for c in order}
        upd = {c: _mm_tn(do_rows[c], qd_stack[c]) for c in order}
        dqd = {c: _head_diagonal(dqd[c], key_head) for c in order}
        return z, tile, {c: dict(dv=dv[c], dqd=dqd[c], dkd=dkd[c], upd=upd[c]) for c in order}

    def scan(tile, per):
        dst = dstate[...]
        dsts = {}
        for c in order:
            dsts[c] = dst
            dst = dst * tile[0][c] + per[c]["upd"]
        dstate[...] = dst
        return dsts

    def inter(z, tile, per, dsts):
        ebl, eb, enb, qd, kd, ke = tile
        ke_stack = {c: _stack_heads(ke[_chunk_rows(c)], key_head) for c in order}
        v_rows = {c: _rows_by_head(v_ref[_chunk_rows(c), :].astype(BF16)) for c in order}
        dst_b = {c: dsts[c].astype(BF16) for c in order}
        dv_state = {c: _mm_nt(ke_stack[c], dst_b[c]) for c in order}
        dke_c = {c: _mm(v_rows[c], dst_b[c]) for c in order}
        dke_c = {c: _head_diagonal(dke_c[c], key_head) for c in order}
        dbl_c = {}
        for c in order:
            rows = _chunk_rows(c)
            dv_ref[rows, :] = (per[c]["dv"] + _lanes_by_head(dv_state[c])).astype(BF16)
            dbl_c[c] = (jnp.sum(dsts[c] * st_ref[c].astype(F32), axis=0, keepdims=True) * ebl[c]
                        + jnp.sum(dke_c[c] * ke[rows], axis=0, keepdims=True))
        tile_of = lambda parts: jnp.concatenate([parts[c] for c in range(nc)], axis=0)
        dqd, dkd = tile_of({c: per[c]["dqd"] for c in order}), tile_of({c: per[c]["dkd"] for c in order})
        dke = tile_of(dke_c)
        dke_end = tile_of({c: dke_c[c] * ebl[c] for c in order})
        dq_ref[...] = (dqd * eb * Q_SCALE).astype(BF16)
        dk_ref[...] = ((dkd + dke_end) * enb).astype(BF16)
        db = dqd * qd - dkd * kd - dke * ke
        dla = _chunk_cumsum(_tri(not rev), db, [dbl_c[c] for c in range(nc)])
        dz = dla * (_sigmoid(-z) * (1.0 / GLA_TAU))
        dlr_ref[...] = _mm_nt(dz, wd_ref[...]).astype(BF16)
        dwd_ref[...] += _mm_tn(lr_ref[...], dz)
        dbd_ref[...] += jnp.sum(dz, axis=0, keepdims=True)

    return intra, scan, inter


def _gla_bwd(p, wd_pad_f, bd_f, wd_pad_b, bd_b, st_f, st_b, d_o, tg, comms=()):
    t = p.shape[0]
    nt = t // tg
    nc = tg // GLA_CHUNK
    up, down = (lambda i: i), (lambda i: nt - 1 - i)

    def body(qf, kf, vf, lrf, stf, dof, qb, kb, vb, lrb, stb, dob, wdf, bdf, wdb, bdb,
             dqf, dkf, dvf, dlrf, dwdf, dbdf, dqb, dkb, dvb, dlrb, dwdb, dbdb, dstate_f, dstate_b):
        @pl.when(pl.program_id(0) == 0)
        def _():
            for ref in (dstate_f, dstate_b, dwdf, dbdf, dwdb, dbdb):
                ref[...] = jnp.zeros_like(ref)

        dirs = [_gla_bwd_dir(False, nc, qf, kf, vf, lrf, wdf, bdf, stf, dof, dqf, dkf, dvf, dlrf, dwdf, dbdf,
                             dstate_f),
                _gla_bwd_dir(True, nc, qb, kb, vb, lrb, wdb, bdb, stb, dob, dqb, dkb, dvb, dlrb, dwdb, dbdb,
                             dstate_b)]
        first = [intra() for intra, _, _ in dirs]
        dsts = [scan(tile, per) for (_, scan, _), (_, tile, per) in zip(dirs, first)]
        for (_, _, inter), (z, tile, per), d in zip(dirs, first, dsts):
            inter(z, tile, per, d)

    wd_spec, bd_spec = _const_spec((LANE, KEY_W)), _const_spec((1, KEY_W))
    ins = lambda tile: _p_specs(tg, tile) + [pl.BlockSpec((nc, GLA_DV, KEY_W), lambda i: (tile(i), 0, 0)),
                                             pl.BlockSpec((tg, VAL_W), lambda i: (tile(i), 0))]
    outs = lambda tile: (pl.BlockSpec((tg, KEY_W), lambda i: (tile(i), 0)),
                         pl.BlockSpec((tg, KEY_W), lambda i: (tile(i), 0)),
                         pl.BlockSpec((tg, VAL_W), lambda i: (tile(i), 0)),
                         pl.BlockSpec((tg, LANE), lambda i: (tile(i), 0)),
                         _acc_spec((LANE, KEY_W)), _acc_spec((1, KEY_W)))
    out_shape = (jax.ShapeDtypeStruct((t, KEY_W), BF16), jax.ShapeDtypeStruct((t, KEY_W), BF16),
                 jax.ShapeDtypeStruct((t, VAL_W), BF16), jax.ShapeDtypeStruct((t, LANE), BF16),
                 jax.ShapeDtypeStruct((LANE, KEY_W), F32), jax.ShapeDtypeStruct((1, KEY_W), F32))
    scratch = [pltpu.VMEM((GLA_DV, KEY_W), F32)]
    return _fused_call(
        body, comms, name="gla_bwd", grid=(nt,),
        inputs=(p, p, p, p, st_f, d_o, p, p, p, p, st_b, d_o, wd_pad_f, bd_f, wd_pad_b, bd_b),
        in_specs=ins(down) + ins(up) + [wd_spec, bd_spec, wd_spec, bd_spec],
        out_specs=outs(down) + outs(up), out_shape=out_shape * 2, scratch_shapes=scratch * 2)


def _head_rms(o):
    parts, scales = [], []
    for h in range(GLA_HEADS):
        oh = o[:, h * GLA_DV:(h + 1) * GLA_DV]
        r = lax.rsqrt(jnp.mean(oh * oh, axis=-1, keepdims=True) + EPS)
        parts.append(oh * r)
        scales.append(jnp.broadcast_to(r, oh.shape))
    return jnp.concatenate(parts, axis=1), jnp.concatenate(scales, axis=1)


def _layernorm_stats(zv):
    mu = jnp.mean(zv, axis=-1, keepdims=True)
    xc = zv - mu
    rs = lax.rsqrt(jnp.mean(xc * xc, axis=-1, keepdims=True) + EPS)
    return xc * rs, rs


def _mix_fwd(x, o_f, o_b, p, gla_g, ln_g, ln_b, w_sp, b_sp, w_out, tm, comms=()):
    t = x.shape[0]
    nch = tm // GMLP_CHUNK

    def body(x_ref, of_ref, ob_ref, pg_ref, pu_ref, pv_ref, gg_ref, lg_ref, lb_ref, ws_ref, bs_ref, wo_ref,
             x1_ref, y_ref, s_scr):
        on, _ = _head_rms(of_ref[...] + ob_ref[...])
        pg = pg_ref[...].astype(F32)
        y_a = on * gg_ref[...] * (pg * _sigmoid(pg))
        zu = _gelu(pu_ref[...].astype(F32))
        vhat, _ = _layernorm_stats(_gelu(pv_ref[...].astype(F32)))
        vln = (vhat * lg_ref[...] + lb_ref[...]).astype(BF16)
        for g in range(GMLP_GROUPS):
            w_g = ws_ref[g].astype(BF16)
            b_g = bs_ref[g]
            cols = slice(g * LANE, (g + 1) * LANE)
            for n in range(nch):
                rows = slice(n * GMLP_CHUNK, (n + 1) * GMLP_CHUNK)
                s_scr[rows, cols] = jnp.dot(w_g, vln[rows, cols], preferred_element_type=F32) + b_g
        ycat = jnp.concatenate([y_a, zu * s_scr[...]], axis=1).astype(BF16)
        y_ref[...] = ycat
        x1_ref[...] = x_ref[...] + jnp.dot(ycat, wo_ref[...], preferred_element_type=F32)

    half = lambda j: pl.BlockSpec((tm, VAL_W), lambda i: (i, j))
    return _fused_call(
        body, comms, name="mix_fwd", grid=(t // tm,),
        inputs=(x, o_f, o_b, p, p, p, gla_g, ln_g, ln_b, w_sp, b_sp, w_out),
        in_specs=[pl.BlockSpec((tm, D_MODEL), lambda i: (i, 0)), half(0), half(0), half(2), half(3), half(4),
                  _const_spec((1, VAL_W)), _const_spec((1, GMLP_W)), _const_spec((1, GMLP_W)),
                  _const_spec((GMLP_GROUPS, GMLP_CHUNK, GMLP_CHUNK)), _const_spec((GMLP_GROUPS, GMLP_CHUNK, 1)),
                  _const_spec((D_MODEL, D_MODEL))],
        out_specs=(pl.BlockSpec((tm, D_MODEL), lambda i: (i, 0)), pl.BlockSpec((tm, D_MODEL), lambda i: (i, 0))),
        out_shape=(jax.ShapeDtypeStruct((t, D_MODEL), F32), jax.ShapeDtypeStruct((t, D_MODEL), BF16)),
        scratch_shapes=[pltpu.VMEM((tm, GMLP_W), F32)])


def _mix_bwd(dx1, ycat, o_f, o_b, p, gla_g, ln_g, ln_b, w_sp, b_sp, w_out, tm, comms=()):
    t = dx1.shape[0]
    nch = tm // GMLP_CHUNK

    def body(dx1_ref, y_ref, of_ref, ob_ref, pg_ref, pu_ref, pv_ref, gg_ref, lg_ref, lb_ref, ws_ref, bs_ref, wo_ref,
             do_ref, dpg_ref, dpu_ref, dpv_ref, dwo_ref, dgg_ref, dlg_ref, dlb_ref, dws_ref, dbs_ref,
             s_scr, dvln_scr):
        @pl.when(pl.program_id(0) == 0)
        def _():
            for ref in (dwo_ref, dgg_ref, dlg_ref, dlb_ref, dws_ref, dbs_ref):
                ref[...] = jnp.zeros_like(ref)

        dx1 = dx1_ref[...].astype(BF16)
        dycat = _mm_nt(dx1, wo_ref[...])
        dwo_ref[...] += _mm_tn(y_ref[...], dx1)
        dy_a = dycat[:, :VAL_W]
        dy_b = dycat[:, VAL_W:]
        on, r = _head_rms(of_ref[...] + ob_ref[...])
        pg = pg_ref[...].astype(F32)
        sil, dsil = _silu_and_grad(pg)
        gg = gg_ref[...]
        dgg_ref[...] += jnp.sum(dy_a * sil * on, axis=0, keepdims=True)
        don = dy_a * sil * gg
        prod = don * on
        means = jnp.concatenate(
            [jnp.broadcast_to(jnp.mean(prod[:, h * GLA_DV:(h + 1) * GLA_DV], axis=-1, keepdims=True),
                              (tm, GLA_DV)) for h in range(GLA_HEADS)], axis=1)
        do_ref[...] = (r * (don - on * means)).astype(BF16)
        dpg_ref[...] = (dy_a * on * gg * dsil).astype(BF16)
        pu = pu_ref[...].astype(F32)
        pv = pv_ref[...].astype(F32)
        zu, dzu_dpu = _gelu_and_grad(pu)
        zv, dzv_dpv = _gelu_and_grad(pv)
        vhat, rs = _layernorm_stats(zv)
        lg = lg_ref[...]
        vln = (vhat * lg + lb_ref[...]).astype(BF16)
        ds32 = dy_b * zu
        ds = ds32.astype(BF16)
        blocks = [(g, n) for g in range(GMLP_GROUPS) for n in range(nch)]
        at = lambda g, n: (slice(n * GMLP_CHUNK, (n + 1) * GMLP_CHUNK), slice(g * LANE, (g + 1) * LANE))
        w_sp = [ws_ref[g].astype(BF16) for g in range(GMLP_GROUPS)]
        v_blk = {b: vln[at(*b)] for b in blocks}
        ds_blk = {b: ds[at(*b)] for b in blocks}
        s_blk = {b: jnp.dot(w_sp[b[0]], v_blk[b], preferred_element_type=F32) for b in blocks}
        dw_blk = {b: _mm_nt(ds_blk[b], v_blk[b]) for b in blocks}
        dvln_blk = {b: _mm_tn(w_sp[b[0]], ds_blk[b]) for b in blocks}
        for b in blocks:
            s_scr[at(*b)] = s_blk[b] + bs_ref[b[0]]
            dvln_scr[at(*b)] = dvln_blk[b]
        for g in range(GMLP_GROUPS):
            dws_ref[g] += sum(dw_blk[(g, n)] for n in range(nch))
            dbs_ref[g] += sum(jnp.sum(ds32[at(g, n)], axis=-1, keepdims=True) for n in range(nch))
        dpu_ref[...] = (dy_b * s_scr[...] * dzu_dpu).astype(BF16)
        dvln = dvln_scr[...]
        dlg_ref[...] += jnp.sum(dvln * vhat, axis=0, keepdims=True)
        dlb_ref[...] += jnp.sum(dvln, axis=0, keepdims=True)
        dvhat = dvln * lg
        dzv = rs * (dvhat - jnp.mean(dvhat, axis=-1, keepdims=True)
                    - vhat * jnp.mean(dvhat * vhat, axis=-1, keepdims=True))
        dpv_ref[...] = (dzv * dzv_dpv).astype(BF16)

    half = lambda j: pl.BlockSpec((tm, VAL_W), lambda i: (i, j))
    full = pl.BlockSpec((tm, D_MODEL), lambda i: (i, 0))
    sp_shape = (GMLP_GROUPS, GMLP_CHUNK, GMLP_CHUNK)
    bs_shape = (GMLP_GROUPS, GMLP_CHUNK, 1)
    return _fused_call(
        body, comms, name="mix_bwd", grid=(t // tm,),
        inputs=(dx1, ycat, o_f, o_b, p, p, p, gla_g, ln_g, ln_b, w_sp, b_sp, w_out),
        in_specs=[full, full, half(0), half(0), half(2), half(3), half(4),
                  _const_spec((1, VAL_W)), _const_spec((1, GMLP_W)), _const_spec((1, GMLP_W)),
                  _const_spec(sp_shape), _const_spec(bs_shape), _const_spec((D_MODEL, D_MODEL))],
        out_specs=(half(0), half(0), half(0), half(0), _acc_spec((D_MODEL, D_MODEL)), _acc_spec((1, VAL_W)),
                   _acc_spec((1, GMLP_W)), _acc_spec((1, GMLP_W)), _acc_spec(sp_shape), _acc_spec(bs_shape)),
        out_shape=(jax.ShapeDtypeStruct((t, VAL_W), BF16),) * 4 + (
            jax.ShapeDtypeStruct((D_MODEL, D_MODEL), F32), jax.ShapeDtypeStruct((1, VAL_W), F32),
            jax.ShapeDtypeStruct((1, GMLP_W), F32), jax.ShapeDtypeStruct((1, GMLP_W), F32),
            jax.ShapeDtypeStruct(sp_shape, F32), jax.ShapeDtypeStruct(bs_shape, F32)),
        scratch_shapes=[pltpu.VMEM((tm, GMLP_W), F32), pltpu.VMEM((tm, GMLP_W), F32)])


def _rms_bwd(dy_scaled, xn, r):
    return r * (dy_scaled - xn * jnp.mean(dy_scaled * xn, axis=-1, keepdims=True))


def _ffn(x1, target, g2, gf, w_gate, w_up, w_down, tm):
    t = x1.shape[0]

    def body(x1_ref, tg_ref, g2_ref, gf_ref, wg_ref, wu_ref, wd_ref,
             dx1_ref, h2_ref, dgate_ref, dup_ref, act_ref, dx2_ref, loss_ref, dgf_ref, dg2_ref):
        @pl.when(pl.program_id(0) == 0)
        def _():
            for ref in (loss_ref, dgf_ref, dg2_ref):
                ref[...] = jnp.zeros_like(ref)

        x1v = x1_ref[...]
        g2v = g2_ref[...]
        gfv = gf_ref[...]
        r2 = lax.rsqrt(jnp.mean(x1v * x1v, axis=-1, keepdims=True) + EPS)
        xn1 = x1v * r2
        h2 = (xn1 * g2v).astype(BF16)
        h2_ref[...] = h2
        gate = _mm_nt(h2, wg_ref[...])
        up = _mm_nt(h2, wu_ref[...])
        sil, dsil = _silu_and_grad(gate)
        act = (sil * up).astype(BF16)
        act_ref[...] = act
        x2 = x1v + jnp.dot(act, wd_ref[...], preferred_element_type=F32)
        rf = lax.rsqrt(jnp.mean(x2 * x2, axis=-1, keepdims=True) + EPS)
        xn2 = x2 * rf
        err = xn2 * gfv - tg_ref[...]
        loss_ref[...] += 0.5 * jnp.sum(jnp.mean(err * err, axis=-1, keepdims=True))
        dy = err * (1.0 / D_MODEL)
        dgf_ref[...] += jnp.sum(dy * xn2, axis=0, keepdims=True)
        dx2 = _rms_bwd(dy * gfv, xn2, rf)
        dx2b = dx2.astype(BF16)
        dx2_ref[...] = dx2b
        dact = _mm_nt(dx2b, wd_ref[...])
        dgate = (dact * up * dsil).astype(BF16)
        dup = (dact * sil).astype(BF16)
        dgate_ref[...] = dgate
        dup_ref[...] = dup
        dh2 = _mm(dgate, wg_ref[...]) + _mm(dup, wu_ref[...])
        dg2_ref[...] += jnp.sum(dh2 * xn1, axis=0, keepdims=True)
        dx1_ref[...] = dx2 + _rms_bwd(dh2 * g2v, xn1, r2)

    row = lambda w: pl.BlockSpec((tm, w), lambda i: (i, 0))
    return pl.pallas_call(
        body, name="ffn_fwd_bwd", grid=(t // tm,),
        in_specs=[row(D_MODEL), row(D_MODEL), _const_spec((1, D_MODEL)), _const_spec((1, D_MODEL)),
                  _const_spec((D_FF, D_MODEL)), _const_spec((D_FF, D_MODEL)), _const_spec((D_FF, D_MODEL))],
        out_specs=(row(D_MODEL), row(D_MODEL), row(D_FF), row(D_FF), row(D_FF), row(D_MODEL),
                   _acc_spec((8, LANE)), _acc_spec((1, D_MODEL)), _acc_spec((1, D_MODEL))),
        out_shape=(jax.ShapeDtypeStruct((t, D_MODEL), F32), jax.ShapeDtypeStruct((t, D_MODEL), BF16),
                   jax.ShapeDtypeStruct((t, D_FF), BF16), jax.ShapeDtypeStruct((t, D_FF), BF16),
                   jax.ShapeDtypeStruct((t, D_FF), BF16), jax.ShapeDtypeStruct((t, D_MODEL), BF16),
                   jax.ShapeDtypeStruct((8, LANE), F32), jax.ShapeDtypeStruct((1, D_MODEL), F32),
                   jax.ShapeDtypeStruct((1, D_MODEL), F32)),
        compiler_params=_params(),
    )(x1, target, g2, gf, w_gate, w_up, w_down)


def _matmul_tn(a, b, tm, tk, name, comms=()):
    t, m = a.shape
    n = b.shape[1]

    def body(a_ref, b_ref, o_ref):
        @pl.when(pl.program_id(1) == 0)
        def _():
            o_ref[...] = jnp.zeros_like(o_ref)

        o_ref[...] += _mm_tn(a_ref[...], b_ref[...])

    (out,), comm_results = _fused_call(
        body, comms, name=name, grid=(m // tm, t // tk), inputs=(a, b),
        in_specs=[pl.BlockSpec((tk, tm), lambda j, k: (k, j)), pl.BlockSpec((tk, n), lambda j, k: (k, 0))],
        out_specs=(pl.BlockSpec((tm, n), lambda j, k: (j, 0)),),
        out_shape=(jax.ShapeDtypeStruct((m, n), F32),))
    return out, comm_results


def _sum_directions(refs):
    if len(refs) == 1:
        return refs[0][...]
    return (refs[0][...].astype(F32) + refs[1][...].astype(F32)).astype(BF16)


def _grad_w_in(tiles, hb, tk, comms=()):
    t = hb.shape[0]
    nk = t // tk
    flat = [(a, width, block, j) for j, terms in enumerate(tiles) for arrays, width, block in terms for a in arrays]
    rows = sum(width for _, width, _ in tiles[0])

    def body(*refs):
        h_ref, o_ref = refs[len(flat)], refs[len(flat) + 1]

        @pl.when(pl.program_id(1) == 0)
        def _():
            o_ref[...] = jnp.zeros_like(o_ref)

        i = 0
        for j, terms in enumerate(tiles):
            piece_refs = []
            for arrays, _, _ in terms:
                piece_refs.append(refs[i:i + len(arrays)])
                i += len(arrays)

            @pl.when(pl.program_id(0) == j)
            def _(piece_refs=piece_refs):
                dp = jnp.concatenate([_sum_directions(r) for r in piece_refs], axis=1)
                o_ref[...] += _mm_tn(dp, h_ref[...])

    def piece_spec(width, block, tile):
        return pl.BlockSpec((tk, width), lambda j, k: (jnp.where(j == tile, k, jnp.where(j < tile, 0, nk - 1)), block))

    (out,), comm_results = _fused_call(
        body, comms, name="grad_w_in", grid=(len(tiles), nk), inputs=(*[a for a, _, _, _ in flat], hb),
        in_specs=[piece_spec(width, block, j) for _, width, block, j in flat] + [
            pl.BlockSpec((tk, D_MODEL), lambda j, k: (k, 0))],
        out_specs=(pl.BlockSpec((rows, D_MODEL), lambda j, k: (j, 0)),),
        out_shape=(jax.ShapeDtypeStruct((len(tiles) * rows, D_MODEL), F32),))
    return out, comm_results


def _in_proj_bwd(x, g1, dx1, dq_f, dq_b, dk_f, dk_b, dv_f, dv_b, dpg, dpu, dpv, dlr_f, dlr_b, w_main, tm, comms=()):
    t = x.shape[0]

    def body(x_ref, g_ref, dx1_ref, dqf, dqb, dkf, dkb, dvf, dvb, dg, du, dv, dlf, dlb, w_ref, dx_ref, dg1_ref):
        @pl.when(pl.program_id(0) == 0)
        def _():
            dg1_ref[...] = jnp.zeros_like(dg1_ref)

        dp = jnp.concatenate([_sum_directions(r) for r in ((dqf, dqb), (dkf, dkb), (dvf, dvb), (dg,), (du,), (dv,),
                                                             (dlf, dlb))], axis=1)
        dh = sum(_mm(dp[:, c0:c0 + r1 - r0], w_ref[r0:r1, :]) for r0, r1, c0 in PROJ_ROWS)
        xv = x_ref[...]
        r = lax.rsqrt(jnp.mean(xv * xv, axis=-1, keepdims=True) + EPS)
        xn = xv * r
        dg1_ref[...] += jnp.sum(dh * xn, axis=0, keepdims=True)
        dx_ref[...] = dx1_ref[...] + _rms_bwd(dh * g_ref[...], xn, r)

    row = lambda w: pl.BlockSpec((tm, w), lambda i: (i, 0))
    return _fused_call(
        body, comms, name="in_proj_bwd", grid=(t // tm,),
        inputs=(x, g1, dx1, dq_f, dq_b, dk_f, dk_b, dv_f, dv_b, dpg, dpu, dpv, dlr_f, dlr_b, w_main),
        in_specs=[row(D_MODEL), _const_spec((1, D_MODEL)), row(D_MODEL), row(KEY_W), row(KEY_W), row(KEY_W),
                  row(KEY_W), row(VAL_W), row(VAL_W), row(VAL_W), row(VAL_W), row(VAL_W), row(LANE), row(LANE),
                  _const_spec((PROJ_W, D_MODEL))],
        out_specs=(row(D_MODEL), _acc_spec((1, D_MODEL))),
        out_shape=(jax.ShapeDtypeStruct((t, D_MODEL), F32), jax.ShapeDtypeStruct((1, D_MODEL), F32)))


def _adamw(w, g, m, v):
    m_new = ADAM_B1 * m + (1.0 - ADAM_B1) * g
    v_new = ADAM_B2 * v + (1.0 - ADAM_B2) * (g * g)
    m_hat = m_new / (1.0 - ADAM_B1 ** ADAM_STEP)
    v_hat = v_new / (1.0 - ADAM_B2 ** ADAM_STEP)
    delta = -ADAM_LR * (m_hat / (jnp.sqrt(v_hat) + ADAM_EPS) + ADAM_WD * w)
    return delta, m_new, v_new


def _adamw_shard(own, recv, w, m, v, tr, name):
    r, c = w.shape

    def body(own_ref, recv_ref, w_ref, m_ref, v_ref, g_ref, d_ref, nm_ref, nv_ref):
        g = own_ref[...]
        for k in range(3):
            g = g + recv_ref[k].astype(F32)
        g_ref[...] = g
        d_ref[...], nm_ref[...], nv_ref[...] = _adamw(w_ref[...], g, m_ref[...], v_ref[...])

    row = pl.BlockSpec((tr, c), lambda i: (i, 0))
    return pl.pallas_call(
        body, name=name, grid=(r // tr,),
        in_specs=[row, pl.BlockSpec((3, tr, c), lambda i: (0, i, 0)), row, row, row],
        out_specs=(row,) * 4, out_shape=(jax.ShapeDtypeStruct((r, c), F32),) * 4,
        compiler_params=_params(),
    )(own, recv, w, m, v)


def _adamw_small(entries):
    stacks = []
    for (g, _, _), _, _, _ in entries:
        if not any(g is s for s in stacks):
            stacks.append(g)
    where = [next(i for i, s in enumerate(stacks) if s is g) for (g, _, _), _, _, _ in entries]
    ns, ne = len(stacks), len(entries)

    def body(*refs):
        s_refs, wmv, outs = refs[:ns], refs[ns:ns + 3 * ne], refs[ns + 3 * ne:]
        for e, ((_, r0, nr), _, _, _) in enumerate(entries):
            grad = s_refs[where[e]][r0:r0 + nr, :]
            w_ref, m_ref, v_ref = wmv[3 * e:3 * e + 3]
            g_ref, d_ref, nm_ref, nv_ref = outs[4 * e:4 * e + 4]
            g_ref[...] = grad
            d_ref[...], nm_ref[...], nv_ref[...] = _adamw(w_ref[...], grad, m_ref[...], v_ref[...])

    results = pl.pallas_call(
        body, name="adamw_small",
        out_shape=tuple(jax.ShapeDtypeStruct(w.shape, F32) for _, w, _, _ in entries for _ in range(4)),
        compiler_params=pltpu.CompilerParams(vmem_limit_bytes=VMEM_LIMIT),
    )(*stacks, *[a for _, w, m, v in entries for a in (w, m, v)])
    return [results[4 * e:4 * e + 4] for e in range(ne)]


def _mesh_pos():
    return lax.axis_index("x"), lax.axis_index("y"), lax.axis_index("c")


def _other_chips(x, y):
    return [(x, 1 - y), (1 - x, y), (1 - x, 1 - y)]


_VMEM_WHOLE = pl.BlockSpec(memory_space=pltpu.VMEM)
_HBM_WHOLE = pl.BlockSpec(memory_space=pl.ANY)


def _gather_comm(shards, cast, mid=((1, 2), (3, 4))):
    na = len(shards)
    staged = [a for a in range(na) if cast[a]]

    def phases(in_refs, out_refs, scr):
        stage = dict(zip(staged, scr[:len(staged)]))
        send_sems, recv_sems, local_sems = scr[len(staged):]
        x, y, c = _mesh_pos()
        me, sibling = (x, y, c), (x, y, 1 - c)
        chip_a, chip_b, diagonal = (x ^ c, y ^ (1 - c)), (x ^ (1 - c), y ^ c), (1 - x, 1 - y)
        srcs = [stage[a] if cast[a] else in_refs[a] for a in range(na)]

        def rows(a, pos):
            px, py, pc = pos
            return out_refs[a].at[4 * px + 2 * py + pc]

        def copy(a, k, block, to, src=None):
            return pltpu.make_async_remote_copy(
                src_ref=rows(a, block) if src is None else src, dst_ref=rows(a, block),
                send_sem=send_sems.at[a, k], recv_sem=recv_sems.at[a, k], device_id=to, device_id_type=MESH_ID)

        mine = [pltpu.make_async_copy(srcs[a], rows(a, me), local_sems.at[a]) for a in range(na)]
        own = [copy(a, k, me, to, src=srcs[a]) for a in range(na)
               for k, to in ((0, sibling), (1, (*chip_a, c)), (2, (*chip_b, c)))]
        onward = [copy(a, 3, (*chip_a, c), (*chip_b, c)) for a in range(na)]
        to_sibling = {k: [copy(a, k, (*chip, c), sibling) for a in range(na)]
                      for k, chip in ((4, chip_a), (5, chip_b), (6, diagonal))}

        def start():
            for a in staged:
                stage[a][...] = in_refs[a][...].astype(BF16)
            for cp in mine + own:
                cp.start()

        def forward_neighbours():
            for a in range(na):
                copy(a, 1, (*chip_a, c), me).wait_recv()
                onward[a].start()
                to_sibling[4][a].start()
            for a in range(na):
                copy(a, 2, (*chip_b, c), me).wait_recv()
                to_sibling[5][a].start()

        def forward_diagonal():
            for a in range(na):
                copy(a, 3, (*diagonal, c), me).wait_recv()
                to_sibling[6][a].start()

        def finish():
            for a in range(na):
                for k, chip in ((0, (x, y)), (4, chip_b), (5, chip_a), (6, diagonal)):
                    copy(a, k, (*chip, 1 - c), me).wait_recv()
            for cp in own + onward + to_sibling[4] + to_sibling[5] + to_sibling[6]:
                cp.wait_send()
            for cp in mine:
                cp.wait()

        return start, forward_neighbours, forward_diagonal, finish

    def before(step, nsteps, in_refs, out_refs, scr):
        start, forward_neighbours, forward_diagonal, _ = phases(in_refs, out_refs, scr)
        pl.when(step == 0)(start)
        pl.when(step == nsteps * mid[0][0] // mid[0][1])(forward_neighbours)
        pl.when(step == nsteps * mid[1][0] // mid[1][1])(forward_diagonal)

    def after(step, nsteps, in_refs, out_refs, scr):
        pl.when(step == nsteps - 1)(phases(in_refs, out_refs, scr)[3])

    return _Comm(
        inputs=list(shards), in_specs=[_VMEM_WHOLE] * na,
        out_shape=[jax.ShapeDtypeStruct((N_DEV,) + s.shape, BF16 if cast[a] else s.dtype)
                   for a, s in enumerate(shards)],
        out_specs=[_HBM_WHOLE] * na,
        scratch_shapes=[pltpu.VMEM(shards[a].shape, BF16) for a in staged] + [
            pltpu.SemaphoreType.DMA((na, 7)), pltpu.SemaphoreType.DMA((na, 7)), pltpu.SemaphoreType.DMA((na,))],
        before=before, after=after)


def _exchange_comm(arrays, out_shape, make_copies):
    na = len(arrays)

    def copies(in_refs, out_refs, scr):
        return make_copies(in_refs, out_refs, *scr)

    def before(step, nsteps, in_refs, out_refs, scr):
        @pl.when(step == 0)
        def _():
            for cp in copies(in_refs, out_refs, scr):
                cp.start()

    def after(step, nsteps, in_refs, out_refs, scr):
        @pl.when(step == nsteps - 1)
        def _():
            for cp in copies(in_refs, out_refs, scr):
                cp.wait()

    return _Comm(inputs=list(arrays), in_specs=[_HBM_WHOLE] * na, out_shape=list(out_shape),
                 out_specs=[_HBM_WHOLE] * na,
                 scratch_shapes=[pltpu.SemaphoreType.DMA((na, 3)), pltpu.SemaphoreType.DMA((na, 3))],
                 before=before, after=after)


def _sibling_exchange_comm(grads):
    def make_copies(in_refs, out_refs, send_sems, recv_sems):
        x, y, c = _mesh_pos()
        return [pltpu.make_async_remote_copy(
            src_ref=in_refs[a].at[:, pl.ds(1 - c, 1)], dst_ref=out_refs[a], send_sem=send_sems.at[a, 0],
            recv_sem=recv_sems.at[a, 0], device_id=(x, y, 1 - c), device_id_type=MESH_ID)
            for a in range(len(grads))]

    return _exchange_comm(grads, [jax.ShapeDtypeStruct((4, 1) + g.shape[2:], F32) for g in grads], make_copies)


def _chips_exchange_comm(partials):
    def make_copies(in_refs, out_refs, send_sems, recv_sems):
        x, y, c = _mesh_pos()
        return [pltpu.make_async_remote_copy(
            src_ref=in_refs[a].at[j], dst_ref=out_refs[a].at[j], send_sem=send_sems.at[a, j],
            recv_sem=recv_sems.at[a, j], device_id=(*chip, c), device_id_type=MESH_ID)
            for a in range(len(partials)) for j, chip in enumerate(_other_chips(x, y))]

    return _exchange_comm(partials, [jax.ShapeDtypeStruct(g.shape, BF16) for g in partials], make_copies)


def _comm_only(comms, name):
    return _fused_call(lambda: None, comms, name=name, grid=(1,), inputs=(), in_specs=[], out_specs=(),
                       out_shape=())[1]


def _chip_sum(my_pos, mine, from_sibling, tr, name):
    _, _, r, c = mine.shape

    def body(pos_ref, a_ref, b_ref, own_ref, out_ref):
        s = a_ref[0, 0] + b_ref[0, 0]

        @pl.when(pl.program_id(1) == 0)
        def _():
            own_ref[...] = s

        @pl.when(pl.program_id(1) > 0)
        def _():
            out_ref[0] = s.astype(BF16)

    grid_spec = pltpu.PrefetchScalarGridSpec(
        num_scalar_prefetch=1, grid=(r // tr, 4),
        in_specs=[pl.BlockSpec((1, 1, tr, c), lambda i, k, pos: (pos[0] ^ k, pos[1], i, 0)),
                  pl.BlockSpec((1, 1, tr, c), lambda i, k, pos: (pos[0] ^ k, 0, i, 0))],
        out_specs=(pl.BlockSpec((tr, c), lambda i, k, pos: (i, 0)),
                   pl.BlockSpec((1, tr, c), lambda i, k, pos: (jnp.maximum(k - 1, 0), i, 0))))
    return pl.pallas_call(
        body, name=name, grid_spec=grid_spec,
        out_shape=(jax.ShapeDtypeStruct((r, c), F32), jax.ShapeDtypeStruct((3, r, c), BF16)),
        compiler_params=_params(2),
    )(my_pos, mine, from_sibling)


def _all_reduce_small_comm(parts):
    na = len(parts)

    def copies(in_refs, scr):
        gathered, (send_sems, recv_sems) = scr[:na], scr[na:]
        x, y, c = _mesh_pos()
        my_id = 4 * x + 2 * y + c
        return my_id, [pltpu.make_async_remote_copy(
            src_ref=in_refs[a], dst_ref=gathered[a].at[my_id], send_sem=send_sems.at[a, k - 1],
            recv_sem=recv_sems.at[a, k - 1], device_id=(x ^ (k >> 2), y ^ ((k >> 1) & 1), c ^ (k & 1)),
            device_id_type=MESH_ID) for a in range(na) for k in range(1, N_DEV)]

    def before(step, nsteps, in_refs, out_refs, scr):
        @pl.when(step == 0)
        def _():
            for cp in copies(in_refs, scr)[1]:
                cp.start()

    def after(step, nsteps, in_refs, out_refs, scr):
        @pl.when(step == nsteps - 1)
        def _():
            my_id, cps = copies(in_refs, scr)
            for a in range(na):
                scr[a][my_id] = in_refs[a][...]
            for cp in cps:
                cp.wait()
            for a in range(na):
                acc = scr[a][0]
                for d in range(1, N_DEV):
                    acc = acc + scr[a][d]
                out_refs[a][...] = acc

    return _Comm(inputs=list(parts), in_specs=[_VMEM_WHOLE] * na,
                 out_shape=[jax.ShapeDtypeStruct(p.shape, F32) for p in parts], out_specs=[_VMEM_WHOLE] * na,
                 scratch_shapes=[pltpu.VMEM((N_DEV,) + p.shape, F32) for p in parts] + [
                     pltpu.SemaphoreType.DMA((na, N_DEV - 1)), pltpu.SemaphoreType.DMA((na, N_DEV - 1))],
                 before=before, after=after)


def _unshard_cols(g):
    return jnp.transpose(g, (1, 0, 2)).reshape(g.shape[1], N_DEV * g.shape[2])


def _row_blocks(w):
    return w.reshape(4, 2, w.shape[0] // N_DEV, w.shape[1])


def _stack_rows(parts):
    a = jnp.concatenate(parts, axis=0)
    return jnp.pad(a, ((0, (-a.shape[0]) % 8), (0, 0)))


def _w_in_grad_blocks(dw):
    return _row_blocks(jnp.concatenate([dw[:LR_REF], dw[LR_COL:LR_COL + 2 * LOWRANK], dw[LR_REF:LR_COL]], axis=0))


def _padded_decay_weights(wd_f, wd_b):
    zeros = lambda n: jnp.zeros((n, KEY_W), F32)
    return (jnp.concatenate([wd_f, zeros(LANE - LOWRANK)], axis=0),
            jnp.concatenate([zeros(LOWRANK), wd_b, zeros(LANE - 2 * LOWRANK)], axis=0))


def kernel(x, norm1_g, w_in,w_decay_f, b_decay_f, w_decay_b, b_decay_b, gla_norm_g, gmlp_ln_g, gmlp_ln_b, w_spatial, b_spatial, w_out, norm2_g, w_gate, w_up, w_down, final_norm_g, loss_target, m_norm1_g, m_w_in, m_w_decay_f, m_b_decay_f, m_w_decay_b, m_b_decay_b, m_gla_norm_g, m_gmlp_ln_g, m_gmlp_ln_b, m_w_spatial, m_b_spatial, m_w_out, m_norm2_g, m_w_gate, m_w_up, m_w_down, m_final_norm_g, v_norm1_g, v_w_in, v_w_decay_f, v_b_decay_f, v_w_decay_b, v_b_decay_b, v_gla_norm_g, v_gmlp_ln_g, v_gmlp_ln_b, v_w_spatial, v_b_spatial, v_w_out, v_norm2_g, v_w_gate, v_w_up, v_w_down, v_final_norm_g):
    t = x.shape[1]
    xt = x[0]
    target = loss_target[0]
    pos_x, pos_y, pos_c = _mesh_pos()
    my_pos = jnp.stack([2 * pos_x + pos_y, pos_c]).astype(jnp.int32)
    my_id = 4 * pos_x + 2 * pos_y + pos_c

    tile = lambda n: min(n, t)
    ln_g, ln_b, w_sp = gmlp_ln_g, gmlp_ln_b, w_spatial[0]
    b_sp_col = b_spatial[0][:, :, None]
    shard = {"w_in": w_in[0].T, "w_out": w_out[0], "w_gate": w_gate[0].T, "w_up": w_up[0].T, "w_down": w_down[0]}
    shard_m = {"w_in": m_w_in[0].T, "w_out": m_w_out[0], "w_gate": m_w_gate[0].T, "w_up": m_w_up[0].T,
               "w_down": m_w_down[0]}
    shard_v = {"w_in": v_w_in[0].T, "w_out": v_w_out[0], "w_gate": v_w_gate[0].T, "w_up": v_w_up[0].T,
               "w_down": v_w_down[0]}
    transposed = ("w_in", "w_gate", "w_up")
    chip_sum = lambda n, g, s: _chip_sum(my_pos, g, s[0], g.shape[2], "chip_sum_" + n)

    decay_shard = jnp.stack([w_decay_f[0], w_decay_b[0]])
    (hb,), ((g_in, g_decay),) = _norm1(xt, norm1_g, tile(TOKEN_TILE["norm1"]),
                                       [_gather_comm([shard["w_in"], decay_shard], [True, False])])
    w_in_t = g_in.reshape(PROJ_W, D_MODEL)
    wd_pad_f, wd_pad_b = _padded_decay_weights(_unshard_cols(g_decay[:, 0]), _unshard_cols(g_decay[:, 1]))
    (p,), ((g_gate, g_out),) = _in_proj(
        hb, w_in_t, tile(TOKEN_TILE["in_proj"]), [_gather_comm([shard["w_gate"], shard["w_out"]], [True, True])])
    (o_f, st_f, o_b, st_b), ((g_up,),) = _gla_fwd(
        p, wd_pad_f, b_decay_f, wd_pad_b, b_decay_b, tile(TOKEN_TILE["gla"]), [_gather_comm([shard["w_up"]], [True])])
    w_out_full = g_out.reshape(D_MODEL, D_MODEL)
    (x1, ycat), ((g_down,),) = _mix_fwd(xt, o_f, o_b, p, gla_norm_g, ln_g, ln_b, w_sp, b_sp_col, w_out_full,
                                        tile(TOKEN_TILE["mix_fwd"]), [_gather_comm([shard["w_down"]], [True])])

    dx1, h2b, dgate, dup, act, dx2, loss_acc, d_gf, d_g2 = _ffn(
        x1, target, norm2_g, final_norm_g[None, :], g_gate.reshape(D_FF, D_MODEL), g_up.reshape(D_FF, D_MODEL),
        g_down.reshape(D_FF, D_MODEL), tile(TOKEN_TILE["ffn"]))
    dw_gate, _ = _matmul_tn(dgate, h2b, D_FF // 2, tile(TOKEN_TILE["dw"]), "grad_w_gate")
    dw_up, _ = _matmul_tn(dup, h2b, D_FF // 2, tile(TOKEN_TILE["dw"]), "grad_w_up")
    dw_down, _ = _matmul_tn(act, dx2, D_FF // 2, tile(TOKEN_TILE["dw"]), "grad_w_down")

    ffn_grads = [_row_blocks(dw_gate), _row_blocks(dw_up), _row_blocks(dw_down)]
    (d_o, dpg, dpu, dpv, dw_out, d_gg, d_lg, d_lb, dw_sp, db_sp), (ffn_sib,) = _mix_bwd(
        dx1, ycat, o_f, o_b, p, gla_norm_g, ln_g, ln_b, w_sp, b_sp_col, w_out_full,
        tile(TOKEN_TILE["mix_bwd"]),
        [_sibling_exchange_comm(ffn_grads)])
    ffn_names = ["w_gate", "w_up", "w_down"]
    ffn_sums = [chip_sum(n, g, [s]) for n, g, s in zip(ffn_names, ffn_grads, ffn_sib)]
    out_grad = _row_blocks(dw_out)
    (dq_f, dk_f, dv_f, dlr_f, dwd_f, dbd_f, dq_b, dk_b, dv_b, dlr_b, dwd_b, dbd_b), (ffn_recv, out_sib) = _gla_bwd(
        p, wd_pad_f, b_decay_f, wd_pad_b, b_decay_b, st_f, st_b, d_o, tile(TOKEN_TILE["gla"]),
        [_chips_exchange_comm([s[1] for s in ffn_sums]), _sibling_exchange_comm([out_grad])])
    out_sum = chip_sum("w_out", out_grad, out_sib)
    (grad_x, d_g1), _ = _in_proj_bwd(
        xt, norm1_g, dx1, dq_f, dq_b, dk_f, dk_b, dv_f, dv_b, dpg, dpu, dpv, dlr_f, dlr_b, w_in_t,
        tile(TOKEN_TILE["in_proj_bwd"]))

    stacks = [_stack_rows([d_g1, d_g2, d_gf]), _stack_rows([d_gg, d_lg, d_lb]),
              _stack_rows([dbd_f, dbd_b, jnp.zeros((DECAY_W_ROW - 2, KEY_W), F32), dwd_f[:LOWRANK],
                           dwd_b[LOWRANK:2 * LOWRANK]]),
              _stack_rows([dw_sp.reshape(GMLP_W, GMLP_CHUNK), db_sp[:, :, 0], loss_acc[:1]])]
    gla_dirs = lambda f, b, width, block: ((f, b), width, block)
    in_tiles = [
        [gla_dirs(dq_f, dq_b, KEY_W, 0), gla_dirs(dk_f, dk_b, KEY_W, 0), gla_dirs(dv_f, dv_b, 3 * LANE, 0)],
        [gla_dirs(dv_f, dv_b, LANE, 3), ((dpg,), VAL_W, 0), ((dpu,), GMLP_W // 2, 0)],
        [((dpu,), GMLP_W // 2, 1), ((dpv,), GMLP_W, 0), gla_dirs(dlr_f, dlr_b, LANE, 0)]]
    dw_main, (small_sums, out_recv) = _grad_w_in(
        in_tiles, hb, tile(TOKEN_TILE["dw_in"]), [_all_reduce_small_comm(stacks), _chips_exchange_comm([out_sum[1]])])
    in_grad = _w_in_grad_blocks(dw_main)
    (in_sib,) = _comm_only([_sibling_exchange_comm([in_grad])], "grad_w_in_exchange_sibling")
    in_sum = chip_sum("w_in", in_grad, in_sib)
    (in_recv,) = _comm_only([_chips_exchange_comm([in_sum[1]])], "grad_w_in_exchange_chips")

    names = ["w_in", "w_out", "w_gate", "w_up", "w_down"]
    sums = [in_sum, out_sum] + ffn_sums
    received = [in_recv[0], out_recv[0]] + list(ffn_recv)
    big_out = {}
    for n, s, rc in zip(names, sums, received):
        rows = shard[n].shape[0]
        half = rows // 2 if rows % 32 == 0 else rows
        res = _adamw_shard(s[0], rc, shard[n], shard_m[n], shard_v[n], half, "adamw_" + n)
        big_out[n] = [r.T if n in transposed else r for r in res]

    s1024, s512, s256, s128 = small_sums
    loss = s128[GMLP_W + GMLP_GROUPS, 0]
    col0 = my_id * (KEY_W // N_DEV)
    decay_cols = lambda row0: lax.dynamic_slice(s256, (row0, col0), (LOWRANK, KEY_W // N_DEV))
    flat = lambda a: a.reshape(-1, a.shape[-1])
    small = {
        "norm1_g": ((s1024, 0, 1), norm1_g, m_norm1_g, v_norm1_g),
        "w_decay_f": ((decay_cols(DECAY_W_ROW), 0, LOWRANK), w_decay_f, m_w_decay_f, v_w_decay_f),
        "b_decay_f": ((s256, 0, 1), b_decay_f, m_b_decay_f, v_b_decay_f),
        "w_decay_b": ((decay_cols(DECAY_W_ROW + LOWRANK), 0, LOWRANK), w_decay_b, m_w_decay_b, v_w_decay_b),
        "b_decay_b": ((s256, 1, 1), b_decay_b, m_b_decay_b, v_b_decay_b),
        "gla_norm_g": ((s512, 0, 1), gla_norm_g, m_gla_norm_g, v_gla_norm_g),
        "gmlp_ln_g": ((s512, 1, 1), gmlp_ln_g, m_gmlp_ln_g, v_gmlp_ln_g),
        "gmlp_ln_b": ((s512, 2, 1), gmlp_ln_b, m_gmlp_ln_b, v_gmlp_ln_b),
        "w_spatial": ((s128, 0, GMLP_W), w_spatial, m_w_spatial, v_w_spatial),
        "b_spatial": ((s128, GMLP_W, GMLP_GROUPS), b_spatial, m_b_spatial, v_b_spatial),
        "norm2_g": ((s1024, 1, 1), norm2_g, m_norm2_g, v_norm2_g),
        "final_norm_g": ((s1024, 2, 1), final_norm_g, m_final_norm_g, v_final_norm_g),
    }
    small_res = _adamw_small([(g, flat(w), flat(m), flat(v)) for g, w, m, v in small.values()])
    small_out = {n: [r.reshape(small[n][1].shape) for r in res] for n, res in zip(small, small_res)}

    order = ["norm1_g", "w_in", "w_decay_f", "b_decay_f", "w_decay_b", "b_decay_b", "gla_norm_g", "gmlp_ln_g",
             "gmlp_ln_b", "w_spatial", "b_spatial", "w_out", "norm2_g", "w_gate", "w_up", "w_down", "final_norm_g"]
    outs = []
    for kind in range(4):
        for n in order:
            outs.append(big_out[n][kind][None] if n in big_out else small_out[n][kind])
    return (loss, grad_x[None], *outs)
```

```python
import functools
import math

import jax
import jax.numpy as jnp
from jax import lax
from jax.experimental import pallas as pl
from jax.experimental.pallas import tpu as pltpu

F32 = jnp.float32
BF16 = jnp.bfloat16

D_MODEL = 1024
GLA_HEADS = 4
GLA_DK = 64
GLA_DV = 128
KEY_W = GLA_HEADS * GLA_DK
VAL_W = GLA_HEADS * GLA_DV
LOWRANK = 16
GLA_TAU = 16.0
GLA_CHUNK = 64
GMLP_W = 512
GMLP_GROUPS = 4
GMLP_CHUNK = 128
D_FF = 2816
EPS = 1e-6
Q_SCALE = GLA_DK ** -0.5
PROJ_PAD = 2688
LR_COL = 2560
LANE = 128
N_DEV = 8

ADAM_LR = 0.001
ADAM_B1 = 0.9
ADAM_B2 = 0.999
ADAM_EPS = 1e-08
ADAM_WD = 0.01
ADAM_STEP = 10

VMEM_LIMIT = 56 * 1024 * 1024
TOKEN_TILE = {"norm1": 512, "in_proj": 512, "gla": 1024, "mix_fwd": 1024, "ffn": 256, "mix_bwd": 512,
              "in_proj_bwd": 512, "dw": 2048}
DECAY_W_ROW = 8
MESH_ID = pl.DeviceIdType.MESH
INV_SQRT2 = 0.7071067811865476
INV_SQRT_2PI = 0.3989422804014327


def _params(n_axes=1):
    return pltpu.CompilerParams(dimension_semantics=("arbitrary",) * n_axes, vmem_limit_bytes=VMEM_LIMIT)


def _mm(a, b):
    return jnp.dot(a.astype(BF16), b.astype(BF16), preferred_element_type=F32)


def _mm_nt(a, b):
    return lax.dot_general(a.astype(BF16), b.astype(BF16), (((1,), (1,)), ((), ())), preferred_element_type=F32)


def _mm_tn(a, b):
    return lax.dot_general(a.astype(BF16), b.astype(BF16), (((0,), (0,)), ((), ())), preferred_element_type=F32)


def _const_spec(shape):
    nd = len(shape)
    return pl.BlockSpec(shape, lambda *_: (0,) * nd, pipeline_mode=pl.Buffered(1))


def _acc_spec(shape):
    nd = len(shape)
    return pl.BlockSpec(shape, lambda *_: (0,) * nd)


class _Comm:
    def __init__(self, inputs, in_specs, out_shape, out_specs, scratch_shapes, before, after):
        self.inputs, self.in_specs, self.out_shape, self.out_specs = inputs, in_specs, out_shape, out_specs
        self.scratch_shapes, self.before, self.after = scratch_shapes, before, after


def _fused_call(body, comms, *, name, grid, inputs, in_specs, out_specs, out_shape, scratch_shapes=()):
    n_in, n_out, n_scr = len(in_specs), len(out_specs), len(scratch_shapes)
    nsteps = math.prod(grid)
    sizes = [(len(c.inputs), len(c.out_shape), len(c.scratch_shapes)) for c in comms]

    def full_body(*refs):
        step = pl.program_id(0)
        for axis in range(1, len(grid)):
            step = step * grid[axis] + pl.program_id(axis)
        ins, rest = refs[:n_in], refs[n_in:]
        c_ins = []
        for ci, _, _ in sizes:
            c_ins.append(rest[:ci])
            rest = rest[ci:]
        outs, rest = rest[:n_out], rest[n_out:]
        c_outs = []
        for _, co, _ in sizes:
            c_outs.append(rest[:co])
            rest = rest[co:]
        scr, rest = rest[:n_scr], rest[n_scr:]
        c_scr = []
        for _, _, cs in sizes:
            c_scr.append(rest[:cs])
            rest = rest[cs:]
        for c, a, b, s in zip(comms, c_ins, c_outs, c_scr):
            c.before(step, nsteps, a, b, s)
        body(*ins, *outs, *scr)
        for c, a, b, s in zip(comms, c_ins, c_outs, c_scr):
            c.after(step, nsteps, a, b, s)

    results = pl.pallas_call(
        full_body, name=name, grid=grid,
        in_specs=list(in_specs) + [s for c in comms for s in c.in_specs],
        out_specs=tuple(out_specs) + tuple(s for c in comms for s in c.out_specs),
        out_shape=tuple(out_shape) + tuple(s for c in comms for s in c.out_shape),
        scratch_shapes=list(scratch_shapes) + [s for c in comms for s in c.scratch_shapes],
        compiler_params=_params(len(grid)),
    )(*inputs, *[a for c in comms for a in c.inputs])
    own, rest = results[:n_out], results[n_out:]
    comm_results = []
    for _, co, _ in sizes:
        comm_results.append(rest[:co])
        rest = rest[co:]
    return own, comm_results


def _gelu(x):
    return 0.5 * x * (1.0 + lax.erf(x * INV_SQRT2))


def _gelu_and_grad(x):
    cdf = 0.5 * (1.0 + lax.erf(x * INV_SQRT2))
    return x * cdf, cdf + x * jnp.exp(-0.5 * x * x) * INV_SQRT_2PI


def _sigmoid(x):
    return 0.5 + 0.5 * jnp.tanh(0.5 * x)


def _silu_and_grad(x):
    s = _sigmoid(x)
    return x * s, s * (1.0 + x * (1.0 - s))


def _norm1(x, g1, tm, comms=()):
    t = x.shape[0]

    def body(x_ref, g_ref, h_ref):
        xv = x_ref[...]
        r = lax.rsqrt(jnp.mean(xv * xv, axis=-1, keepdims=True) + EPS)
        h_ref[...] = (xv * r * g_ref[...]).astype(BF16)

    row = pl.BlockSpec((tm, D_MODEL), lambda i: (i, 0))
    return _fused_call(body, comms, name="norm1", grid=(t // tm,), inputs=(x, g1),
                       in_specs=[row, _const_spec((1, D_MODEL))], out_specs=(row,),
                       out_shape=(jax.ShapeDtypeStruct((t, D_MODEL), BF16),))


PROJ_W = 2592
LR_REF = 1536
PROJ_ROWS = ((0, LR_REF, 0), (LR_REF + 2 * LOWRANK, PROJ_W, LR_REF), (LR_REF, LR_REF + LANE, LR_COL))


def _in_proj(h, w_in_t, tm, comms=()):
    t = h.shape[0]

    def body(h_ref, w_ref, p_ref):
        hv = h_ref[...]
        for r0, r1, c0 in PROJ_ROWS:
            p_ref[:, c0:c0 + r1 - r0] = _mm_nt(hv, w_ref[r0:r1, :]).astype(BF16)

    return _fused_call(
        body, comms, name="in_proj", grid=(t // tm,), inputs=(h, w_in_t),
        in_specs=[pl.BlockSpec((tm, D_MODEL), lambda i: (i, 0)), _const_spec((PROJ_W, D_MODEL))],
        out_specs=(pl.BlockSpec((tm, PROJ_PAD), lambda i: (i, 0)),),
        out_shape=(jax.ShapeDtypeStruct((t, PROJ_PAD), BF16),))


def _tri(upper):
    r = lax.broadcasted_iota(jnp.int32, (GLA_CHUNK, GLA_CHUNK), 0)
    c = lax.broadcasted_iota(jnp.int32, (GLA_CHUNK, GLA_CHUNK), 1)
    return jnp.where((c >= r) if upper else (c <= r), 1.0, 0.0).astype(BF16)


def _chunk_cumsum(tri, a, add=None):
    hi = a.astype(BF16)
    lo = (a - hi.astype(F32)).astype(BF16)
    dot = functools.partial(jnp.dot, preferred_element_type=F32)
    sums = [dot(tri, hi[_chunk_rows(c)]) + dot(tri, lo[_chunk_rows(c)]) for c in range(a.shape[0] // GLA_CHUNK)]
    return jnp.concatenate(sums if add is None else [s + r for s, r in zip(sums, add)], axis=0)


def _chunk_rows(c):
    return slice(c * GLA_CHUNK, (c + 1) * GLA_CHUNK)


def _gla_masks(rev):
    dk_bits, dv_bits = GLA_DK.bit_length() - 1, GLA_DV.bit_length() - 1
    key_head = lax.broadcasted_iota(jnp.int32, (GLA_CHUNK, KEY_W), 1) >> dk_bits
    val_head = lax.broadcasted_iota(jnp.int32, (GLA_CHUNK, VAL_W), 1) >> dv_bits
    t = lax.broadcasted_iota(jnp.int32, (GLA_HEADS * GLA_CHUNK, GLA_CHUNK), 0) & (GLA_CHUNK - 1)
    s = lax.broadcasted_iota(jnp.int32, (GLA_HEADS * GLA_CHUNK, GLA_CHUNK), 1)
    return key_head, val_head, (s >= t) if rev else (s <= t)


def _stack_heads(a, head_of_lane):
    a = a.astype(BF16)
    return jnp.concatenate([jnp.where(head_of_lane == h, a, jnp.zeros_like(a)) for h in range(GLA_HEADS)], axis=0)


def _rows_by_head(a):
    return jnp.concatenate([a[:, h * GLA_DV:(h + 1) * GLA_DV] for h in range(GLA_HEADS)], axis=0)


def _lanes_by_head(r):
    return jnp.concatenate([r[h * GLA_CHUNK:(h + 1) * GLA_CHUNK] for h in range(GLA_HEADS)], axis=1)


def _head_diagonal(r, head_of_lane):
    rows = r.shape[0] // GLA_HEADS
    out = jnp.where(head_of_lane == 0, r[:rows], 0.0)
    for h in range(1, GLA_HEADS):
        out = out + jnp.where(head_of_lane == h, r[h * rows:(h + 1) * rows], 0.0)
    return out


def _tile_terms(la, q, k, tri, rev):
    nc = la.shape[0] // GLA_CHUNK
    q, k = q.astype(F32), k.astype(F32)
    b = _chunk_cumsum(tri, la)
    ebl = [jnp.exp(b[c * GLA_CHUNK:c * GLA_CHUNK + 1] if rev else b[(c + 1) * GLA_CHUNK - 1:(c + 1) * GLA_CHUNK])
           for c in range(nc)]
    eb = jnp.exp(b)
    enb = jnp.exp(-b)
    kd = k * enb
    ke = jnp.concatenate([kd[_chunk_rows(c)] * ebl[c] for c in range(nc)], axis=0)
    return ebl, eb, enb, q * Q_SCALE * eb, kd, ke


def _log_decay(lr_ref, wd_ref, bd_ref):
    z = _mm(lr_ref[...], wd_ref[...]) + bd_ref[...]
    return z, jax.nn.log_sigmoid(z) * (1.0 / GLA_TAU)


def _p_specs(tg, tile):
    return [pl.BlockSpec((tg, KEY_W), lambda i: (tile(i), 0)),
            pl.BlockSpec((tg, KEY_W), lambda i: (tile(i), 1)),
            pl.BlockSpec((tg, VAL_W), lambda i: (tile(i), 1)),
            pl.BlockSpec((tg, LANE), lambda i: (tile(i), LR_COL // LANE))]


def _gla_fwd_dir(rev, nc, q_ref, k_ref, v_ref, lr_ref, wd_ref, bd_ref, o_ref, st_ref, state):
    key_head, _, causal = _gla_masks(rev)
    order = range(nc - 1, -1, -1) if rev else range(nc)

    def intra():
        _, la = _log_decay(lr_ref, wd_ref, bd_ref)
        ebl, _, _, qd, kd, ke = _tile_terms(la, q_ref[...], k_ref[...], _tri(rev), rev)
        kd = kd.astype(BF16)
        v = {c: v_ref[_chunk_rows(c), :].astype(BF16) for c in order}
        qd_stack = {c: _stack_heads(qd[_chunk_rows(c)], key_head) for c in order}
        ke_stack = {c: _stack_heads(ke[_chunk_rows(c)], key_head) for c in order}
        a_all = {c: _mm_nt(qd_stack[c], kd[_chunk_rows(c)]) for c in order}
        a_all = {c: jnp.where(causal, a_all[c], 0.0).astype(BF16) for c in order}
        head_rows = lambda a, h: a[h * GLA_CHUNK:(h + 1) * GLA_CHUNK]
        head_vals = lambda a, h: a[:, h * GLA_DV:(h + 1) * GLA_DV]
        r = {c: [_mm(head_rows(a_all[c], h), head_vals(v[c], h)) for h in range(GLA_HEADS)] for c in order}
        upd = {c: _mm_tn(_rows_by_head(v[c]), ke_stack[c]) for c in order}
        return {c: (ebl[c], qd_stack[c], r[c], upd[c]) for c in order}

    def scan(terms):
        st = state[...]
        states = {}
        for c in order:
            states[c] = st
            st_ref[c] = st.astype(BF16)
            st = st * terms[c][0] + terms[c][3]
        state[...] = st
        return states

    def inter(terms, states):
        r_inter = {c: _mm_nt(terms[c][1], states[c]) for c in order}
        for c in order:
            o_ref[_chunk_rows(c), :] = jnp.concatenate(
                [terms[c][2][h] + r_inter[c][h * GLA_CHUNK:(h + 1) * GLA_CHUNK] for h in range(GLA_HEADS)], axis=1)

    return intra, scan, inter


def _gla_fwd(p, wd_pad_f, bd_f, wd_pad_b, bd_b, tg, comms=()):
    t = p.shape[0]
    nt = t // tg
    nc = tg // GLA_CHUNK
    up, down = (lambda i: i), (lambda i: nt - 1 - i)

    def body(qf, kf, vf, lrf, qb, kb, vb, lrb, wdf, bdf, wdb, bdb, of, stf, ob, stb, state_f, state_b):
        @pl.when(pl.program_id(0) == 0)
        def _():
            state_f[...] = jnp.zeros_like(state_f)
            state_b[...] = jnp.zeros_like(state_b)

        dirs = [_gla_fwd_dir(False, nc, qf, kf, vf, lrf, wdf, bdf, of, stf, state_f),
                _gla_fwd_dir(True, nc, qb, kb, vb, lrb, wdb, bdb, ob, stb, state_b)]
        terms = [intra() for intra, _, _ in dirs]
        states = [scan(t) for (_, scan, _), t in zip(dirs, terms)]
        for (_, _, inter), t, s in zip(dirs, terms, states):
            inter(t, s)

    wd_spec, bd_spec = _const_spec((LANE, KEY_W)), _const_spec((1, KEY_W))
    outs = lambda tile: (pl.BlockSpec((tg, VAL_W), lambda i: (tile(i), 0)),
                         pl.BlockSpec((nc, GLA_DV, KEY_W), lambda i: (tile(i), 0, 0)))
    out_shape = (jax.ShapeDtypeStruct((t, VAL_W), F32), jax.ShapeDtypeStruct((t // GLA_CHUNK, GLA_DV, KEY_W), BF16))
    return _fused_call(
        body, comms, name="gla_fwd", grid=(nt,), inputs=(p,) * 8 + (wd_pad_f, bd_f, wd_pad_b, bd_b),
        in_specs=_p_specs(tg, up) + _p_specs(tg, down) + [wd_spec, bd_spec, wd_spec, bd_spec],
        out_specs=outs(up) + outs(down), out_shape=out_shape * 2,
        scratch_shapes=[pltpu.VMEM((GLA_DV, KEY_W), F32)] * 2)


def _gla_bwd_dir(rev, nc, q_ref, k_ref, v_ref, lr_ref, wd_ref, bd_ref, st_ref, do_ref,
                 dq_ref, dk_ref, dv_ref, dlr_ref, dwd_ref, dbd_ref, dstate):
    key_head, val_head, causal = _gla_masks(rev)
    order = range(nc) if rev else range(nc - 1, -1, -1)

    def intra():
        z, la = _log_decay(lr_ref, wd_ref, bd_ref)
        tile = _tile_terms(la, q_ref[...], k_ref[...], _tri(rev), rev)
        qd, kd = tile[3], tile[4].astype(BF16)
        v = {c: v_ref[_chunk_rows(c), :].astype(BF16) for c in order}
        d_o = {c: do_ref[_chunk_rows(c), :] for c in order}
        kd_c = {c: kd[_chunk_rows(c)] for c in order}
        qd_stack = {c: _stack_heads(qd[_chunk_rows(c)], key_head) for c in order}
        do_stack = {c: _stack_heads(d_o[c], val_head) for c in order}
        do_rows = {c: _rows_by_head(d_o[c]) for c in order}
        a_all = {c: _mm_nt(qd_stack[c], kd_c[c]) for c in order}
        head_vals = lambda a, h: a[:, h * GLA_DV:(h + 1) * GLA_DV]
        da_all = {c: jnp.concatenate([_mm_nt(head_vals(d_o[c], h), head_vals(v[c], h)) for h in range(GLA_HEADS)],
                                     axis=0) for c in order}
        a_all = {c: jnp.where(causal, a_all[c], 0.0).astype(BF16) for c in order}
        da_all = {c: jnp.where(causal, da_all[c], 0.0).astype(BF16) for c in order}
        dv = {c: _mm_tn(a_all[c], do_stack[c]) for c in order}
        dqd = {c: _mm(jnp.concatenate([do_rows[c], da_all[c]], axis=1),
                      jnp.concatenate([st_ref[c], kd_c[c]], axis=0)) for c in order}
        dkd = {c: _mm_tn(da_all[c], qd_stack[c]) for c in order}
        upd = {c: _mm_tn(do_rows[c], qd_stack[c]) for c in order}
        dqd = {c: _head_diagonal(dqd[c], key_head) for c in order}
        return z, tile, {c: dict(dv=dv[c], dqd=dqd[c], dkd=dkd[c], upd=upd[c]) for c in order}

    def scan(tile, per):
        dst = dstate[...]
        dsts = {}
        for c in order:
            dsts[c] = dst
            dst = dst * tile[0][c] + per[c]["upd"]
        dstate[...] = dst
        return dsts

    def inter(z, tile, per, dsts):
        ebl, eb, enb, qd, kd, ke = tile
        ke_stack = {c: _stack_heads(ke[_chunk_rows(c)], key_head) for c in order}
        v_rows = {c: _rows_by_head(v_ref[_chunk_rows(c), :].astype(BF16)) for c in order}
        dst_b = {c: dsts[c].astype(BF16) for c in order}
        dv_state = {c: _mm_nt(ke_stack[c], dst_b[c]) for c in order}
        dke_c = {c: _mm(v_rows[c], dst_b[c]) for c in order}
        dke_c = {c: _head_diagonal(dke_c[c], key_head) for c in order}
        dbl_c = {}
        for c in order:
            rows = _chunk_rows(c)
            dv_ref[rows, :] = (per[c]["dv"] + _lanes_by_head(dv_state[c])).astype(BF16)
            dbl_c[c] = (jnp.sum(dsts[c] * st_ref[c].astype(F32), axis=0, keepdims=True) * ebl[c]
                        + jnp.sum(dke_c[c] * ke[rows], axis=0, keepdims=True))
        tile_of = lambda parts: jnp.concatenate([parts[c] for c in range(nc)], axis=0)
        dqd, dkd = tile_of({c: per[c]["dqd"] for c in order}), tile_of({c: per[c]["dkd"] for c in order})
        dke = tile_of(dke_c)
        dke_end = tile_of({c: dke_c[c] * ebl[c] for c in order})
        dq_ref[...] = (dqd * eb * Q_SCALE).astype(BF16)
        dk_ref[...] = ((dkd + dke_end) * enb).astype(BF16)
        db = dqd * qd - dkd * kd - dke * ke
        dla = _chunk_cumsum(_tri(not rev), db, [dbl_c[c] for c in range(nc)])
        dz = dla * (_sigmoid(-z) * (1.0 / GLA_TAU))
        dlr_ref[...] = _mm_nt(dz, wd_ref[...]).astype(BF16)
        dwd_ref[...] += _mm_tn(lr_ref[...], dz)
        dbd_ref[...] += jnp.sum(dz, axis=0, keepdims=True)

    return intra, scan, inter


def _gla_bwd(p, wd_pad_f, bd_f, wd_pad_b, bd_b, st_f, st_b, d_o, tg, comms=()):
    t = p.shape[0]
    nt = t // tg
    nc = tg // GLA_CHUNK
    up, down = (lambda i: i), (lambda i: nt - 1 - i)

    def body(qf, kf, vf, lrf, stf, dof, qb, kb, vb, lrb, stb, dob, wdf, bdf, wdb, bdb,
             dqf, dkf, dvf, dlrf, dwdf, dbdf, dqb, dkb, dvb, dlrb, dwdb, dbdb, dstate_f, dstate_b):
        @pl.when(pl.program_id(0) == 0)
        def _():
            for ref in (dstate_f, dstate_b, dwdf, dbdf, dwdb, dbdb):
                ref[...] = jnp.zeros_like(ref)

        dirs = [_gla_bwd_dir(False, nc, qf, kf, vf, lrf, wdf, bdf, stf, dof, dqf, dkf, dvf, dlrf, dwdf, dbdf,
                             dstate_f),
                _gla_bwd_dir(True, nc, qb, kb, vb, lrb, wdb, bdb, stb, dob, dqb, dkb, dvb, dlrb, dwdb, dbdb,
                             dstate_b)]
        first = [intra() for intra, _, _ in dirs]
        dsts = [scan(tile, per) for (_, scan, _), (_, tile, per) in zip(dirs, first)]
        for (_, _, inter), (z, tile, per), d in zip(dirs, first, dsts):
            inter(z, tile, per, d)

    wd_spec, bd_spec = _const_spec((LANE, KEY_W)), _const_spec((1, KEY_W))
    ins = lambda tile: _p_specs(tg, tile) + [pl.BlockSpec((nc, GLA_DV, KEY_W), lambda i: (tile(i), 0, 0)),
                                             pl.BlockSpec((tg, VAL_W), lambda i: (tile(i), 0))]
    outs = lambda tile: (pl.BlockSpec((tg, KEY_W), lambda i: (tile(i), 0)),
                         pl.BlockSpec((tg, KEY_W), lambda i: (tile(i), 0)),
                         pl.BlockSpec((tg, VAL_W), lambda i: (tile(i), 0)),
                         pl.BlockSpec((tg, LANE), lambda i: (tile(i), 0)),
                         _acc_spec((LANE, KEY_W)), _acc_spec((1, KEY_W)))
    out_shape = (jax.ShapeDtypeStruct((t, KEY_W), BF16), jax.ShapeDtypeStruct((t, KEY_W), BF16),
                 jax.ShapeDtypeStruct((t, VAL_W), BF16), jax.ShapeDtypeStruct((t, LANE), BF16),
                 jax.ShapeDtypeStruct((LANE, KEY_W), F32), jax.ShapeDtypeStruct((1, KEY_W), F32))
    scratch = [pltpu.VMEM((GLA_DV, KEY_W), F32)]
    return _fused_call(
        body, comms, name="gla_bwd", grid=(nt,),
        inputs=(p, p, p, p, st_f, d_o, p, p, p, p, st_b, d_o, wd_pad_f, bd_f, wd_pad_b, bd_b),
        in_specs=ins(down) + ins(up) + [wd_spec, bd_spec, wd_spec, bd_spec],
        out_specs=outs(down) + outs(up), out_shape=out_shape * 2, scratch_shapes=scratch * 2)


def _head_rms(o):
    parts, scales = [], []
    for h in range(GLA_HEADS):
        oh = o[:, h * GLA_DV:(h + 1) * GLA_DV]
        r = lax.rsqrt(jnp.mean(oh * oh, axis=-1, keepdims=True) + EPS)
        parts.append(oh * r)
        scales.append(jnp.broadcast_to(r, oh.shape))
    return jnp.concatenate(parts, axis=1), jnp.concatenate(scales, axis=1)


def _layernorm_stats(zv):
    mu = jnp.mean(zv, axis=-1, keepdims=True)
    xc = zv - mu
    rs = lax.rsqrt(jnp.mean(xc * xc, axis=-1, keepdims=True) + EPS)
    return xc * rs, rs


def _mix_fwd(x, o_f, o_b, p, gla_g, ln_g, ln_b, w_sp, b_sp, w_out, tm, comms=()):
    t = x.shape[0]
    nch = tm // GMLP_CHUNK

    def body(x_ref, of_ref, ob_ref, pg_ref, pu_ref, pv_ref, gg_ref, lg_ref, lb_ref, ws_ref, bs_ref, wo_ref,
             x1_ref, y_ref, s_scr):
        on, _ = _head_rms(of_ref[...] + ob_ref[...])
        pg = pg_ref[...].astype(F32)
        y_a = on * gg_ref[...] * (pg * _sigmoid(pg))
        zu = _gelu(pu_ref[...].astype(F32))
        vhat, _ = _layernorm_stats(_gelu(pv_ref[...].astype(F32)))
        vln = (vhat * lg_ref[...] + lb_ref[...]).astype(BF16)
        for g in range(GMLP_GROUPS):
            w_g = ws_ref[g].astype(BF16)
            b_g = bs_ref[g]
            cols = slice(g * LANE, (g + 1) * LANE)
            for n in range(nch):
                rows = slice(n * GMLP_CHUNK, (n + 1) * GMLP_CHUNK)
                s_scr[rows, cols] = jnp.dot(w_g, vln[rows, cols], preferred_element_type=F32) + b_g
        ycat = jnp.concatenate([y_a, zu * s_scr[...]], axis=1).astype(BF16)
        y_ref[...] = ycat
        x1_ref[...] = x_ref[...] + jnp.dot(ycat, wo_ref[...], preferred_element_type=F32)

    half = lambda j: pl.BlockSpec((tm, VAL_W), lambda i: (i, j))
    return _fused_call(
        body, comms, name="mix_fwd", grid=(t // tm,),
        inputs=(x, o_f, o_b, p, p, p, gla_g, ln_g, ln_b, w_sp, b_sp, w_out),
        in_specs=[pl.BlockSpec((tm, D_MODEL), lambda i: (i, 0)), half(0), half(0), half(2), half(3), half(4),
                  _const_spec((1, VAL_W)), _const_spec((1, GMLP_W)), _const_spec((1, GMLP_W)),
                  _const_spec((GMLP_GROUPS, GMLP_CHUNK, GMLP_CHUNK)), _const_spec((GMLP_GROUPS, GMLP_CHUNK, 1)),
                  _const_spec((D_MODEL, D_MODEL))],
        out_specs=(pl.BlockSpec((tm, D_MODEL), lambda i: (i, 0)), pl.BlockSpec((tm, D_MODEL), lambda i: (i, 0))),
        out_shape=(jax.ShapeDtypeStruct((t, D_MODEL), F32), jax.ShapeDtypeStruct((t, D_MODEL), BF16)),
        scratch_shapes=[pltpu.VMEM((tm, GMLP_W), F32)])


def _mix_bwd(dx1, ycat, o_f, o_b, p, gla_g, ln_g, ln_b, w_sp, b_sp, w_out, tm, comms=()):
    t = dx1.shape[0]
    nch = tm // GMLP_CHUNK

    def body(dx1_ref, y_ref, of_ref, ob_ref, pg_ref, pu_ref, pv_ref, gg_ref, lg_ref, lb_ref, ws_ref, bs_ref, wo_ref,
             do_ref, dpg_ref, dpu_ref, dpv_ref, dwo_ref, dgg_ref, dlg_ref, dlb_ref, dws_ref, dbs_ref,
             s_scr, dvln_scr):
        @pl.when(pl.program_id(0) == 0)
        def _():
            for ref in (dwo_ref, dgg_ref, dlg_ref, dlb_ref, dws_ref, dbs_ref):
                ref[...] = jnp.zeros_like(ref)

        dx1 = dx1_ref[...].astype(BF16)
        dycat = _mm_nt(dx1, wo_ref[...])
        dwo_ref[...] += _mm_tn(y_ref[...], dx1)
        dy_a = dycat[:, :VAL_W]
        dy_b = dycat[:, VAL_W:]
        on, r = _head_rms(of_ref[...] + ob_ref[...])
        pg = pg_ref[...].astype(F32)
        sil, dsil = _silu_and_grad(pg)
        gg = gg_ref[...]
        dgg_ref[...] += jnp.sum(dy_a * sil * on, axis=0, keepdims=True)
        don = dy_a * sil * gg
        prod = don * on
        means = jnp.concatenate(
            [jnp.broadcast_to(jnp.mean(prod[:, h * GLA_DV:(h + 1) * GLA_DV], axis=-1, keepdims=True),
                              (tm, GLA_DV)) for h in range(GLA_HEADS)], axis=1)
        do_ref[...] = (r * (don - on * means)).astype(BF16)
        dpg_ref[...] = (dy_a * on * gg * dsil).astype(BF16)
        pu = pu_ref[...].astype(F32)
        pv = pv_ref[...].astype(F32)
        zu, dzu_dpu = _gelu_and_grad(pu)
        zv, dzv_dpv = _gelu_and_grad(pv)
        vhat, rs = _layernorm_stats(zv)
        lg = lg_ref[...]
        vln = (vhat * lg + lb_ref[...]).astype(BF16)
        ds32 = dy_b * zu
        ds = ds32.astype(BF16)
        blocks = [(g, n) for g in range(GMLP_GROUPS) for n in range(nch)]
        at = lambda g, n: (slice(n * GMLP_CHUNK, (n + 1) * GMLP_CHUNK), slice(g * LANE, (g + 1) * LANE))
        w_sp = [ws_ref[g].astype(BF16) for g in range(GMLP_GROUPS)]
        v_blk = {b: vln[at(*b)] for b in blocks}
        ds_blk = {b: ds[at(*b)] for b in blocks}
        s_blk = {b: jnp.dot(w_sp[b[0]], v_blk[b], preferred_element_type=F32) for b in blocks}
        dw_blk = {b: _mm_nt(ds_blk[b], v_blk[b]) for b in blocks}
        dvln_blk = {b: _mm_tn(w_sp[b[0]], ds_blk[b]) for b in blocks}
        for b in blocks:
            s_scr[at(*b)] = s_blk[b] + bs_ref[b[0]]
            dvln_scr[at(*b)] = dvln_blk[b]
        for g in range(GMLP_GROUPS):
            dws_ref[g] += sum(dw_blk[(g, n)] for n in range(nch))
            dbs_ref[g] += sum(jnp.sum(ds32[at(g, n)], axis=-1, keepdims=True) for n in range(nch))
        dpu_ref[...] = (dy_b * s_scr[...] * dzu_dpu).astype(BF16)
        dvln = dvln_scr[...]
        dlg_ref[...] += jnp.sum(dvln * vhat, axis=0, keepdims=True)
        dlb_ref[...] += jnp.sum(dvln, axis=0, keepdims=True)
        dvhat = dvln * lg
        dzv = rs * (dvhat - jnp.mean(dvhat, axis=-1, keepdims=True)
                    - vhat * jnp.mean(dvhat * vhat, axis=-1, keepdims=True))
        dpv_ref[...] = (dzv * dzv_dpv).astype(BF16)

    half = lambda j: pl.BlockSpec((tm, VAL_W), lambda i: (i, j))
    full = pl.BlockSpec((tm, D_MODEL), lambda i: (i, 0))
    sp_shape = (GMLP_GROUPS, GMLP_CHUNK, GMLP_CHUNK)
    bs_shape = (GMLP_GROUPS, GMLP_CHUNK, 1)
    return _fused_call(
        body, comms, name="mix_bwd", grid=(t // tm,),
        inputs=(dx1, ycat, o_f, o_b, p, p, p, gla_g, ln_g, ln_b, w_sp, b_sp, w_out),
        in_specs=[full, full, half(0), half(0), half(2), half(3), half(4),
                  _const_spec((1, VAL_W)), _const_spec((1, GMLP_W)), _const_spec((1, GMLP_W)),
                  _const_spec(sp_shape), _const_spec(bs_shape), _const_spec((D_MODEL, D_MODEL))],
        out_specs=(half(0), half(0), half(0), half(0), _acc_spec((D_MODEL, D_MODEL)), _acc_spec((1, VAL_W)),
                   _acc_spec((1, GMLP_W)), _acc_spec((1, GMLP_W)), _acc_spec(sp_shape), _acc_spec(bs_shape)),
        out_shape=(jax.ShapeDtypeStruct((t, VAL_W), BF16),) * 4 + (
            jax.ShapeDtypeStruct((D_MODEL, D_MODEL), F32), jax.ShapeDtypeStruct((1, VAL_W), F32),
            jax.ShapeDtypeStruct((1, GMLP_W), F32), jax.ShapeDtypeStruct((1, GMLP_W), F32),
            jax.ShapeDtypeStruct(sp_shape, F32), jax.ShapeDtypeStruct(bs_shape, F32)),
        scratch_shapes=[pltpu.VMEM((tm, GMLP_W), F32), pltpu.VMEM((tm, GMLP_W), F32)])


def _rms_bwd(dy_scaled, xn, r):
    return r * (dy_scaled - xn * jnp.mean(dy_scaled * xn, axis=-1, keepdims=True))


def _ffn(x1, target, g2, gf, w_gate, w_up, w_down, tm):
    t = x1.shape[0]

    def body(x1_ref, tg_ref, g2_ref, gf_ref, wg_ref, wu_ref, wd_ref,
             dx1_ref, h2_ref, dgate_ref, dup_ref, act_ref, dx2_ref, loss_ref, dgf_ref, dg2_ref):
        @pl.when(pl.program_id(0) == 0)
        def _():
            for ref in (loss_ref, dgf_ref, dg2_ref):
                ref[...] = jnp.zeros_like(ref)

        x1v = x1_ref[...]
        g2v = g2_ref[...]
        gfv = gf_ref[...]
        r2 = lax.rsqrt(jnp.mean(x1v * x1v, axis=-1, keepdims=True) + EPS)
        xn1 = x1v * r2
        h2 = (xn1 * g2v).astype(BF16)
        h2_ref[...] = h2
        gate = _mm_nt(h2, wg_ref[...])
        up = _mm_nt(h2, wu_ref[...])
        sil, dsil = _silu_and_grad(gate)
        act = (sil * up).astype(BF16)
        act_ref[...] = act
        x2 = x1v + jnp.dot(act, wd_ref[...], preferred_element_type=F32)
        rf = lax.rsqrt(jnp.mean(x2 * x2, axis=-1, keepdims=True) + EPS)
        xn2 = x2 * rf
        err = xn2 * gfv - tg_ref[...]
        loss_ref[...] += 0.5 * jnp.sum(jnp.mean(err * err, axis=-1, keepdims=True))
        dy = err * (1.0 / D_MODEL)
        dgf_ref[...] += jnp.sum(dy * xn2, axis=0, keepdims=True)
        dx2 = _rms_bwd(dy * gfv, xn2, rf)
        dx2b = dx2.astype(BF16)
        dx2_ref[...] = dx2b
        dact = _mm_nt(dx2b, wd_ref[...])
        dgate = (dact * up * dsil).astype(BF16)
        dup = (dact * sil).astype(BF16)
        dgate_ref[...] = dgate
        dup_ref[...] = dup
        dh2 = _mm(dgate, wg_ref[...]) + _mm(dup, wu_ref[...])
        dg2_ref[...] += jnp.sum(dh2 * xn1, axis=0, keepdims=True)
        dx1_ref[...] = dx2 + _rms_bwd(dh2 * g2v, xn1, r2)

    row = lambda w: pl.BlockSpec((tm, w), lambda i: (i, 0))
    return pl.pallas_call(
        body, name="ffn_fwd_bwd", grid=(t // tm,),
        in_specs=[row(D_MODEL), row(D_MODEL), _const_spec((1, D_MODEL)), _const_spec((1, D_MODEL)),
                  _const_spec((D_FF, D_MODEL)), _const_spec((D_FF, D_MODEL)), _const_spec((D_FF, D_MODEL))],
        out_specs=(row(D_MODEL), row(D_MODEL), row(D_FF), row(D_FF), row(D_FF), row(D_MODEL),
                   _acc_spec((8, LANE)), _acc_spec((1, D_MODEL)), _acc_spec((1, D_MODEL))),
        out_shape=(jax.ShapeDtypeStruct((t, D_MODEL), F32), jax.ShapeDtypeStruct((t, D_MODEL), BF16),
                   jax.ShapeDtypeStruct((t, D_FF), BF16), jax.ShapeDtypeStruct((t, D_FF), BF16),
                   jax.ShapeDtypeStruct((t, D_FF), BF16), jax.ShapeDtypeStruct((t, D_MODEL), BF16),
                   jax.ShapeDtypeStruct((8, LANE), F32), jax.ShapeDtypeStruct((1, D_MODEL), F32),
                   jax.ShapeDtypeStruct((1, D_MODEL), F32)),
        compiler_params=_params(),
    )(x1, target, g2, gf, w_gate, w_up, w_down)


def _matmul_tn(a, b, tm, tk, name, comms=()):
    t, m = a.shape
    n = b.shape[1]

    def body(a_ref, b_ref, o_ref):
        @pl.when(pl.program_id(1) == 0)
        def _():
            o_ref[...] = jnp.zeros_like(o_ref)

        o_ref[...] += _mm_tn(a_ref[...], b_ref[...])

    (out,), comm_results = _fused_call(
        body, comms, name=name, grid=(m // tm, t // tk), inputs=(a, b),
        in_specs=[pl.BlockSpec((tk, tm), lambda j, k: (k, j)), pl.BlockSpec((tk, n), lambda j, k: (k, 0))],
        out_specs=(pl.BlockSpec((tm, n), lambda j, k: (j, 0)),),
        out_shape=(jax.ShapeDtypeStruct((m, n), F32),))
    return out, comm_results


def _in_proj_bwd(x, g1, dx1, dq_f, dq_b, dk_f, dk_b, dv_f, dv_b, dpg, dpu, dpv, dlr_f, dlr_b, w_main, tm, comms=()):
    t = x.shape[0]

    def body(x_ref, g_ref, dx1_ref, dqf, dqb, dkf, dkb, dvf, dvb, dg, du, dv, dlf, dlb, w_ref,
             dx_ref, dp_ref, dg1_ref):
        @pl.when(pl.program_id(0) == 0)
        def _():
            dg1_ref[...] = jnp.zeros_like(dg1_ref)

        both = lambda a, b: (a[...].astype(F32) + b[...].astype(F32)).astype(BF16)
        dp = jnp.concatenate([both(dqf, dqb), both(dkf, dkb), both(dvf, dvb), dg[...], du[...], dv[...],
                              both(dlf, dlb)], axis=1)
        dp_ref[...] = dp
        dh = sum(_mm(dp[:, c0:c0 + r1 - r0], w_ref[r0:r1, :]) for r0, r1, c0 in PROJ_ROWS)
        xv = x_ref[...]
        r = lax.rsqrt(jnp.mean(xv * xv, axis=-1, keepdims=True) + EPS)
        xn = xv * r
        dg1_ref[...] += jnp.sum(dh * xn, axis=0, keepdims=True)
        dx_ref[...] = dx1_ref[...] + _rms_bwd(dh * g_ref[...], xn, r)

    row = lambda w: pl.BlockSpec((tm, w), lambda i: (i, 0))
    return _fused_call(
        body, comms, name="in_proj_bwd", grid=(t // tm,),
        inputs=(x, g1, dx1, dq_f, dq_b, dk_f, dk_b, dv_f, dv_b, dpg, dpu, dpv, dlr_f, dlr_b, w_main),
        in_specs=[row(D_MODEL), _const_spec((1, D_MODEL)), row(D_MODEL), row(KEY_W), row(KEY_W), row(KEY_W),
                  row(KEY_W), row(VAL_W), row(VAL_W), row(VAL_W), row(VAL_W), row(VAL_W), row(LANE), row(LANE),
                  _const_spec((PROJ_W, D_MODEL))],
        out_specs=(row(D_MODEL), row(PROJ_PAD), _acc_spec((1, D_MODEL))),
        out_shape=(jax.ShapeDtypeStruct((t, D_MODEL), F32), jax.ShapeDtypeStruct((t, PROJ_PAD), BF16),
                   jax.ShapeDtypeStruct((1, D_MODEL), F32)))


def _adamw(w, g, m, v):
    m_new = ADAM_B1 * m + (1.0 - ADAM_B1) * g
    v_new = ADAM_B2 * v + (1.0 - ADAM_B2) * (g * g)
    m_hat = m_new / (1.0 - ADAM_B1 ** ADAM_STEP)
    v_hat = v_new / (1.0 - ADAM_B2 ** ADAM_STEP)
    delta = -ADAM_LR * (m_hat / (jnp.sqrt(v_hat) + ADAM_EPS) + ADAM_WD * w)
    return delta, m_new, v_new


def _adamw_shard(my_pos, mine, from_sibling, recv, w, m, v, tr, name):
    r, c = w.shape

    def body(pos_ref, a_ref, b_ref, recv_ref, w_ref, m_ref, v_ref, g_ref, d_ref, nm_ref, nv_ref):
        g = a_ref[0, 0] + b_ref[0, 0]
        for k in range(3):
            g = g + recv_ref[k].astype(F32)
        g_ref[...] = g
        d_ref[...], nm_ref[...], nv_ref[...] = _adamw(w_ref[...], g, m_ref[...], v_ref[...])

    row = pl.BlockSpec((tr, c), lambda i, pos: (i, 0))
    grid_spec = pltpu.PrefetchScalarGridSpec(
        num_scalar_prefetch=1, grid=(r // tr,),
        in_specs=[pl.BlockSpec((1, 1, tr, c), lambda i, pos: (pos[0], pos[1], i, 0)),
                  pl.BlockSpec((1, 1, tr, c), lambda i, pos: (pos[0], 0, i, 0)),
                  pl.BlockSpec((3, tr, c), lambda i, pos: (0, i, 0)), row, row, row],
        out_specs=(row,) * 4)
    return pl.pallas_call(
        body, name=name, grid_spec=grid_spec, out_shape=(jax.ShapeDtypeStruct((r, c), F32),) * 4,
        compiler_params=_params(),
    )(my_pos, mine, from_sibling, recv, w, m, v)


def _adamw_small(entries):
    stacks = []
    for (g, _, _), _, _, _ in entries:
        if not any(g is s for s in stacks):
            stacks.append(g)
    where = [next(i for i, s in enumerate(stacks) if s is g) for (g, _, _), _, _, _ in entries]
    ns, ne = len(stacks), len(entries)

    def body(*refs):
        s_refs, wmv, outs = refs[:ns], refs[ns:ns + 3 * ne], refs[ns + 3 * ne:]
        for e, ((_, r0, nr), _, _, _) in enumerate(entries):
            grad = s_refs[where[e]][r0:r0 + nr, :]
            w_ref, m_ref, v_ref = wmv[3 * e:3 * e + 3]
            g_ref, d_ref, nm_ref, nv_ref = outs[4 * e:4 * e + 4]
            g_ref[...] = grad
            d_ref[...], nm_ref[...], nv_ref[...] = _adamw(w_ref[...], grad, m_ref[...], v_ref[...])

    results = pl.pallas_call(
        body, name="adamw_small",
        out_shape=tuple(jax.ShapeDtypeStruct(w.shape, F32) for _, w, _, _ in entries for _ in range(4)),
        compiler_params=pltpu.CompilerParams(vmem_limit_bytes=VMEM_LIMIT),
    )(*stacks, *[a for _, w, m, v in entries for a in (w, m, v)])
    return [results[4 * e:4 * e + 4] for e in range(ne)]


def _mesh_pos():
    return lax.axis_index("x"), lax.axis_index("y"), lax.axis_index("c")


def _other_chips(x, y):
    return [(x, 1 - y), (1 - x, y), (1 - x, 1 - y)]


_VMEM_WHOLE = pl.BlockSpec(memory_space=pltpu.VMEM)
_HBM_WHOLE = pl.BlockSpec(memory_space=pl.ANY)


def _gather_comm(shards, cast, mid=((1, 2), (3, 4))):
    na = len(shards)
    staged = [a for a in range(na) if cast[a]]

    def phases(in_refs, out_refs, scr):
        stage = dict(zip(staged, scr[:len(staged)]))
        send_sems, recv_sems, local_sems = scr[len(staged):]
        x, y, c = _mesh_pos()
        me, sibling = (x, y, c), (x, y, 1 - c)
        chip_a, chip_b, diagonal = (x ^ c, y ^ (1 - c)), (x ^ (1 - c), y ^ c), (1 - x, 1 - y)
        srcs = [stage[a] if cast[a] else in_refs[a] for a in range(na)]

        def rows(a, pos):
            px, py, pc = pos
            return out_refs[a].at[4 * px + 2 * py + pc]

        def copy(a, k, block, to, src=None):
            return pltpu.make_async_remote_copy(
                src_ref=rows(a, block) if src is None else src, dst_ref=rows(a, block),
                send_sem=send_sems.at[a, k], recv_sem=recv_sems.at[a, k], device_id=to, device_id_type=MESH_ID)

        mine = [pltpu.make_async_copy(srcs[a], rows(a, me), local_sems.at[a]) for a in range(na)]
        own = [copy(a, k, me, to, src=srcs[a]) for a in range(na)
               for k, to in ((0, sibling), (1, (*chip_a, c)), (2, (*chip_b, c)))]
        onward = [copy(a, 3, (*chip_a, c), (*chip_b, c)) for a in range(na)]
        to_sibling = {k: [copy(a, k, (*chip, c), sibling) for a in range(na)]
                      for k, chip in ((4, chip_a), (5, chip_b), (6, diagonal))}

        def start():
            for a in staged:
                stage[a][...] = in_refs[a][...].astype(BF16)
            for cp in mine + own:
                cp.start()

        def forward_neighbours():
            for a in range(na):
                copy(a, 1, (*chip_a, c), me).wait_recv()
                onward[a].start()
                to_sibling[4][a].start()
            for a in range(na):
                copy(a, 2, (*chip_b, c), me).wait_recv()
                to_sibling[5][a].start()

        def forward_diagonal():
            for a in range(na):
                copy(a, 3, (*diagonal, c), me).wait_recv()
                to_sibling[6][a].start()

        def finish():
            for a in range(na):
                for k, chip in ((0, (x, y)), (4, chip_b), (5, chip_a), (6, diagonal)):
                    copy(a, k, (*chip, 1 - c), me).wait_recv()
            for cp in own + onward + to_sibling[4] + to_sibling[5] + to_sibling[6]:
                cp.wait_send()
            for cp in mine:
                cp.wait()

        return start, forward_neighbours, forward_diagonal, finish

    def before(step, nsteps, in_refs, out_refs, scr):
        start, forward_neighbours, forward_diagonal, _ = phases(in_refs, out_refs, scr)
        pl.when(step == 0)(start)
        pl.when(step == nsteps * mid[0][0] // mid[0][1])(forward_neighbours)
        pl.when(step == nsteps * mid[1][0] // mid[1][1])(forward_diagonal)

    def after(step, nsteps, in_refs, out_refs, scr):
        pl.when(step == nsteps - 1)(phases(in_refs, out_refs, scr)[3])

    return _Comm(
        inputs=list(shards), in_specs=[_VMEM_WHOLE] * na,
        out_shape=[jax.ShapeDtypeStruct((N_DEV,) + s.shape, BF16 if cast[a] else s.dtype)
                   for a, s in enumerate(shards)],
        out_specs=[_HBM_WHOLE] * na,
        scratch_shapes=[pltpu.VMEM(shards[a].shape, BF16) for a in staged] + [
            pltpu.SemaphoreType.DMA((na, 7)), pltpu.SemaphoreType.DMA((na, 7)), pltpu.SemaphoreType.DMA((na,))],
        before=before, after=after)


def _exchange_comm(arrays, out_shape, make_copies):
    na = len(arrays)

    def copies(in_refs, out_refs, scr):
        return make_copies(in_refs, out_refs, *scr)

    def before(step, nsteps, in_refs, out_refs, scr):
        @pl.when(step == 0)
        def _():
            for cp in copies(in_refs, out_refs, scr):
                cp.start()

    def after(step, nsteps, in_refs, out_refs, scr):
        @pl.when(step == nsteps - 1)
        def _():
            for cp in copies(in_refs, out_refs, scr):
                cp.wait()

    return _Comm(inputs=list(arrays), in_specs=[_HBM_WHOLE] * na, out_shape=list(out_shape),
                 out_specs=[_HBM_WHOLE] * na,
                 scratch_shapes=[pltpu.SemaphoreType.DMA((na, 3)), pltpu.SemaphoreType.DMA((na, 3))],
                 before=before, after=after)


def _sibling_exchange_comm(grads):
    def make_copies(in_refs, out_refs, send_sems, recv_sems):
        x, y, c = _mesh_pos()
        return [pltpu.make_async_remote_copy(
            src_ref=in_refs[a].at[:, pl.ds(1 - c, 1)], dst_ref=out_refs[a], send_sem=send_sems.at[a, 0],
            recv_sem=recv_sems.at[a, 0], device_id=(x, y, 1 - c), device_id_type=MESH_ID)
            for a in range(len(grads))]

    return _exchange_comm(grads, [jax.ShapeDtypeStruct((4, 1) + g.shape[2:], F32) for g in grads], make_copies)


def _chips_exchange_comm(partials):
    def make_copies(in_refs, out_refs, send_sems, recv_sems):
        x, y, c = _mesh_pos()
        return [pltpu.make_async_remote_copy(
            src_ref=in_refs[a].at[j], dst_ref=out_refs[a].at[j], send_sem=send_sems.at[a, j],
            recv_sem=recv_sems.at[a, j], device_id=(*chip, c), device_id_type=MESH_ID)
            for a in range(len(partials)) for j, chip in enumerate(_other_chips(x, y))]

    return _exchange_comm(partials, [jax.ShapeDtypeStruct(g.shape, BF16) for g in partials], make_copies)


def _comm_only(comms, name):
    return _fused_call(lambda: None, comms, name=name, grid=(1,), inputs=(), in_specs=[], out_specs=(),
                       out_shape=())[1]


def _chip_sum(my_pos, mine, from_sibling, tr, name):
    _, _, r, c = mine.shape

    def body(pos_ref, a_ref, b_ref, out_ref):
        out_ref[0] = (a_ref[0, 0] + b_ref[0, 0]).astype(BF16)

    grid_spec = pltpu.PrefetchScalarGridSpec(
        num_scalar_prefetch=1, grid=(r // tr, 3),
        in_specs=[pl.BlockSpec((1, 1, tr, c), lambda i, k, pos: (pos[0] ^ (k + 1), pos[1], i, 0)),
                  pl.BlockSpec((1, 1, tr, c), lambda i, k, pos: (pos[0] ^ (k + 1), 0, i, 0))],
        out_specs=pl.BlockSpec((1, tr, c), lambda i, k, pos: (k, i, 0)))
    return pl.pallas_call(
        body, name=name, grid_spec=grid_spec, out_shape=jax.ShapeDtypeStruct((3, r, c), BF16),
        compiler_params=_params(2),
    )(my_pos, mine, from_sibling)


def _all_reduce_small_comm(parts):
    na = len(parts)

    def copies(in_refs, scr):
        gathered, (send_sems, recv_sems) = scr[:na], scr[na:]
        x, y, c = _mesh_pos()
        my_id = 4 * x + 2 * y + c
        return my_id, [pltpu.make_async_remote_copy(
            src_ref=in_refs[a], dst_ref=gathered[a].at[my_id], send_sem=send_sems.at[a, k - 1],
            recv_sem=recv_sems.at[a, k - 1], device_id=(x ^ (k >> 2), y ^ ((k >> 1) & 1), c ^ (k & 1)),
            device_id_type=MESH_ID) for a in range(na) for k in range(1, N_DEV)]

    def before(step, nsteps, in_refs, out_refs, scr):
        @pl.when(step == 0)
        def _():
            for cp in copies(in_refs, scr)[1]:
                cp.start()

    def after(step, nsteps, in_refs, out_refs, scr):
        @pl.when(step == nsteps - 1)
        def _():
            my_id, cps = copies(in_refs, scr)
            for a in range(na):
                scr[a][my_id] = in_refs[a][...]
            for cp in cps:
                cp.wait()
            for a in range(na):
                acc = scr[a][0]
                for d in range(1, N_DEV):
                    acc = acc + scr[a][d]
                out_refs[a][...] = acc

    return _Comm(inputs=list(parts), in_specs=[_VMEM_WHOLE] * na,
                 out_shape=[jax.ShapeDtypeStruct(p.shape, F32) for p in parts], out_specs=[_VMEM_WHOLE] * na,
                 scratch_shapes=[pltpu.VMEM((N_DEV,) + p.shape, F32) for p in parts] + [
                     pltpu.SemaphoreType.DMA((na, N_DEV - 1)), pltpu.SemaphoreType.DMA((na, N_DEV - 1))],
                 before=before, after=after)


def _unshard_cols(g):
    return jnp.transpose(g, (1, 0, 2)).reshape(g.shape[1], N_DEV * g.shape[2])


def _row_blocks(w):
    return w.reshape(4, 2, w.shape[0] // N_DEV, w.shape[1])


def _stack_rows(parts):
    a = jnp.concatenate(parts, axis=0)
    return jnp.pad(a, ((0, (-a.shape[0]) % 8), (0, 0)))


def _w_in_grad_blocks(dw):
    return _row_blocks(jnp.concatenate([dw[:LR_REF], dw[LR_COL:LR_COL + 2 * LOWRANK], dw[LR_REF:LR_COL]], axis=0))


def _padded_decay_weights(wd_f, wd_b):
    zeros = lambda n: jnp.zeros((n, KEY_W), F32)
    return (jnp.concatenate([wd_f, zeros(LANE - LOWRANK)], axis=0),
            jnp.concatenate([zeros(LOWRANK), wd_b, zeros(LANE - 2 * LOWRANK)], axis=0))


def kernel(x, norm1_g, w_in,w_decay_f, b_decay_f, w_decay_b, b_decay_b, gla_norm_g, gmlp_ln_g, gmlp_ln_b, w_spatial, b_spatial, w_out, norm2_g, w_gate, w_up, w_down, final_norm_g, loss_target, m_norm1_g, m_w_in, m_w_decay_f, m_b_decay_f, m_w_decay_b, m_b_decay_b, m_gla_norm_g, m_gmlp_ln_g, m_gmlp_ln_b, m_w_spatial, m_b_spatial, m_w_out, m_norm2_g, m_w_gate, m_w_up, m_w_down, m_final_norm_g, v_norm1_g, v_w_in, v_w_decay_f, v_b_decay_f, v_w_decay_b, v_b_decay_b, v_gla_norm_g, v_gmlp_ln_g, v_gmlp_ln_b, v_w_spatial, v_b_spatial, v_w_out, v_norm2_g, v_w_gate, v_w_up, v_w_down, v_final_norm_g):
    t = x.shape[1]
    xt = x[0]
    target = loss_target[0]
    pos_x, pos_y, pos_c = _mesh_pos()
    my_pos = jnp.stack([2 * pos_x + pos_y, pos_c]).astype(jnp.int32)
    my_id = 4 * pos_x + 2 * pos_y + pos_c

    tile = lambda n: min(n, t)
    ln_g, ln_b, w_sp = gmlp_ln_g, gmlp_ln_b, w_spatial[0]
    b_sp_col = b_spatial[0][:, :, None]
    shard = {"w_in": w_in[0].T, "w_out": w_out[0], "w_gate": w_gate[0].T, "w_up": w_up[0].T, "w_down": w_down[0]}
    shard_m = {"w_in": m_w_in[0].T, "w_out": m_w_out[0], "w_gate": m_w_gate[0].T, "w_up": m_w_up[0].T,
               "w_down": m_w_down[0]}
    shard_v = {"w_in": v_w_in[0].T, "w_out": v_w_out[0], "w_gate": v_w_gate[0].T, "w_up": v_w_up[0].T,
               "w_down": v_w_down[0]}
    transposed = ("w_in", "w_gate", "w_up")
    chip_sum = lambda n, g, s: _chip_sum(my_pos, g, s[0], g.shape[2], "chip_sum_" + n)

    decay_shard = jnp.stack([w_decay_f[0], w_decay_b[0]])
    (hb,), ((g_in, g_decay),) = _norm1(xt, norm1_g, tile(TOKEN_TILE["norm1"]),
                                       [_gather_comm([shard["w_in"], decay_shard], [True, False])])
    w_in_t = g_in.reshape(PROJ_W, D_MODEL)
    wd_pad_f, wd_pad_b = _padded_decay_weights(_unshard_cols(g_decay[:, 0]), _unshard_cols(g_decay[:, 1]))
    (p,), ((g_gate, g_out),) = _in_proj(
        hb, w_in_t, tile(TOKEN_TILE["in_proj"]), [_gather_comm([shard["w_gate"], shard["w_out"]], [True, True])])
    (o_f, st_f, o_b, st_b), ((g_up,),) = _gla_fwd(
        p, wd_pad_f, b_decay_f, wd_pad_b, b_decay_b, tile(TOKEN_TILE["gla"]), [_gather_comm([shard["w_up"]], [True])])
    w_out_full = g_out.reshape(D_MODEL, D_MODEL)
    (x1, ycat), ((g_down,),) = _mix_fwd(xt, o_f, o_b, p, gla_norm_g, ln_g, ln_b, w_sp, b_sp_col, w_out_full,
                                        tile(TOKEN_TILE["mix_fwd"]), [_gather_comm([shard["w_down"]], [True])])

    dx1, h2b, dgate, dup, act, dx2, loss_acc, d_gf, d_g2 = _ffn(
        x1, target, norm2_g, final_norm_g[None, :], g_gate.reshape(D_FF, D_MODEL), g_up.reshape(D_FF, D_MODEL),
        g_down.reshape(D_FF, D_MODEL), tile(TOKEN_TILE["ffn"]))
    dw_gate, _ = _matmul_tn(dgate, h2b, D_FF // 2, tile(TOKEN_TILE["dw"]), "grad_w_gate")
    dw_up, _ = _matmul_tn(dup, h2b, D_FF // 2, tile(TOKEN_TILE["dw"]), "grad_w_up")
    dw_down, _ = _matmul_tn(act, dx2, D_FF // 2, tile(TOKEN_TILE["dw"]), "grad_w_down")

    ffn_grads = [_row_blocks(dw_gate), _row_blocks(dw_up), _row_blocks(dw_down)]
    (d_o, dpg, dpu, dpv, dw_out, d_gg, d_lg, d_lb, dw_sp, db_sp), (ffn_sib,) = _mix_bwd(
        dx1, ycat, o_f, o_b, p, gla_norm_g, ln_g, ln_b, w_sp, b_sp_col, w_out_full,
        tile(TOKEN_TILE["mix_bwd"]),
        [_sibling_exchange_comm(ffn_grads)])
    ffn_names = ["w_gate", "w_up", "w_down"]
    ffn_sums = [chip_sum(n, g, [s]) for n, g, s in zip(ffn_names, ffn_grads, ffn_sib)]
    out_grad = _row_blocks(dw_out)
    (dq_f, dk_f, dv_f, dlr_f, dwd_f, dbd_f, dq_b, dk_b, dv_b, dlr_b, dwd_b, dbd_b), (ffn_recv, out_sib) = _gla_bwd(
        p, wd_pad_f, b_decay_f, wd_pad_b, b_decay_b, st_f, st_b, d_o, tile(TOKEN_TILE["gla"]),
        [_chips_exchange_comm(ffn_sums), _sibling_exchange_comm([out_grad])])
    out_sum = chip_sum("w_out", out_grad, out_sib)
    (grad_x, dp, d_g1), _ = _in_proj_bwd(
        xt, norm1_g, dx1, dq_f, dq_b, dk_f, dk_b, dv_f, dv_b, dpg, dpu, dpv, dlr_f, dlr_b, w_in_t,
        tile(TOKEN_TILE["in_proj_bwd"]))

    stacks = [_stack_rows([d_g1, d_g2, d_gf]), _stack_rows([d_gg, d_lg, d_lb]),
              _stack_rows([dbd_f, dbd_b, jnp.zeros((DECAY_W_ROW - 2, KEY_W), F32), dwd_f[:LOWRANK],
                           dwd_b[LOWRANK:2 * LOWRANK]]),
              _stack_rows([dw_sp.reshape(GMLP_W, GMLP_CHUNK), db_sp[:, :, 0], loss_acc[:1]])]
    dw_main, (small_sums, out_recv) = _matmul_tn(
        dp, hb, PROJ_PAD // 3, tile(TOKEN_TILE["dw"]), "grad_w_in",
        [_all_reduce_small_comm(stacks), _chips_exchange_comm([out_sum])])
    in_grad = _w_in_grad_blocks(dw_main)
    (in_sib,) = _comm_only([_sibling_exchange_comm([in_grad])], "grad_w_in_exchange_sibling")
    in_sum = chip_sum("w_in", in_grad, in_sib)
    (in_recv,) = _comm_only([_chips_exchange_comm([in_sum])], "grad_w_in_exchange_chips")

    names = ["w_in", "w_out", "w_gate", "w_up", "w_down"]
    blocks = [in_grad, out_grad] + ffn_grads
    from_sibling = [in_sib[0], out_sib[0]] + list(ffn_sib)
    received = [in_recv[0], out_recv[0]] + list(ffn_recv)
    big_out = {}
    for n, g, sib, rc in zip(names, blocks, from_sibling, received):
        rows = shard[n].shape[0]
        half = rows // 2 if rows % 32 == 0 else rows
        res = _adamw_shard(my_pos, g, sib, rc, shard[n], shard_m[n], shard_v[n], half, "adamw_" + n)
        big_out[n] = [r.T if n in transposed else r for r in res]

    s1024, s512, s256, s128 = small_sums
    loss = s128[GMLP_W + GMLP_GROUPS, 0]
    col0 = my_id * (KEY_W // N_DEV)
    decay_cols = lambda row0: lax.dynamic_slice(s256, (row0, col0), (LOWRANK, KEY_W // N_DEV))
    flat = lambda a: a.reshape(-1, a.shape[-1])
    small = {
        "norm1_g": ((s1024, 0, 1), norm1_g, m_norm1_g, v_norm1_g),
        "w_decay_f": ((decay_cols(DECAY_W_ROW), 0, LOWRANK), w_decay_f, m_w_decay_f, v_w_decay_f),
        "b_decay_f": ((s256, 0, 1), b_decay_f, m_b_decay_f, v_b_decay_f),
        "w_decay_b": ((decay_cols(DECAY_W_ROW + LOWRANK), 0, LOWRANK), w_decay_b, m_w_decay_b, v_w_decay_b),
        "b_decay_b": ((s256, 1, 1), b_decay_b, m_b_decay_b, v_b_decay_b),
        "gla_norm_g": ((s512, 0, 1), gla_norm_g, m_gla_norm_g, v_gla_norm_g),
        "gmlp_ln_g": ((s512, 1, 1), gmlp_ln_g, m_gmlp_ln_g, v_gmlp_ln_g),
        "gmlp_ln_b": ((s512, 2, 1), gmlp_ln_b, m_gmlp_ln_b, v_gmlp_ln_b),
        "w_spatial": ((s128, 0, GMLP_W), w_spatial, m_w_spatial, v_w_spatial),
        "b_spatial": ((s128, GMLP_W, GMLP_GROUPS), b_spatial, m_b_spatial, v_b_spatial),
        "norm2_g": ((s1024, 1, 1), norm2_g, m_norm2_g, v_norm2_g),
        "final_norm_g": ((s1024, 2, 1), final_norm_g, m_final_norm_g, v_final_norm_g),
    }
    small_res = _adamw_small([(g, flat(w), flat(m), flat(v)) for g, w, m, v in small.values()])
    small_out = {n: [r.reshape(small[n][1].shape) for r in res] for n, res in zip(small, small_res)}

    order = ["norm1_g", "w_in", "w_decay_f", "b_decay_f", "w_decay_b", "b_decay_b", "gla_norm_g", "gmlp_ln_g",
             "gmlp_ln_b", "w_spatial", "b_spatial", "w_out", "norm2_g", "w_gate", "w_up", "w_down", "final_norm_g"]
    outs = []
    for kind in range(4):
        for n in order:
            outs.append(big_out[n][kind][None] if n in big_out else small_out[n][kind])
    return (loss, grad_x[None], *outs)
```

```python
import functools
import math

import jax
import jax.numpy as jnp
from jax import lax
from jax.experimental import pallas as pl
from jax.experimental.pallas import tpu as pltpu

F32 = jnp.float32
BF16 = jnp.bfloat16

D_MODEL = 1024
GLA_HEADS = 4
GLA_DK = 64
GLA_DV = 128
KEY_W = GLA_HEADS * GLA_DK
VAL_W = GLA_HEADS * GLA_DV
LOWRANK = 16
GLA_TAU = 16.0
GLA_CHUNK = 64
GMLP_W = 512
GMLP_GROUPS = 4
GMLP_CHUNK = 128
D_FF = 2816
EPS = 1e-6
Q_SCALE = GLA_DK ** -0.5
PROJ_PAD = 2688
LR_COL = 2560
LANE = 128
N_DEV = 8

ADAM_LR = 0.001
ADAM_B1 = 0.9
ADAM_B2 = 0.999
ADAM_EPS = 1e-08
ADAM_WD = 0.01
ADAM_STEP = 10

VMEM_LIMIT = 56 * 1024 * 1024
TOKEN_TILE = {"norm1": 512, "in_proj": 512, "gla": 1024, "mix_fwd": 1024, "ffn": 256, "mix_bwd": 512,
              "in_proj_bwd": 512, "dw": 2048}
DECAY_W_ROW = 8
MESH_ID = pl.DeviceIdType.MESH
INV_SQRT2 = 0.7071067811865476
INV_SQRT_2PI = 0.3989422804014327


def _params(n_axes=1):
    return pltpu.CompilerParams(dimension_semantics=("arbitrary",) * n_axes, vmem_limit_bytes=VMEM_LIMIT)


def _mm(a, b):
    return jnp.dot(a.astype(BF16), b.astype(BF16), preferred_element_type=F32)


def _mm_nt(a, b):
    return lax.dot_general(a.astype(BF16), b.astype(BF16), (((1,), (1,)), ((), ())), preferred_element_type=F32)


def _mm_tn(a, b):
    return lax.dot_general(a.astype(BF16), b.astype(BF16), (((0,), (0,)), ((), ())), preferred_element_type=F32)


def _const_spec(shape):
    nd = len(shape)
    return pl.BlockSpec(shape, lambda *_: (0,) * nd, pipeline_mode=pl.Buffered(1))


def _acc_spec(shape):
    nd = len(shape)
    return pl.BlockSpec(shape, lambda *_: (0,) * nd)


class _Comm:
    def __init__(self, inputs, in_specs, out_shape, out_specs, scratch_shapes, before, after):
        self.inputs, self.in_specs, self.out_shape, self.out_specs = inputs, in_specs, out_shape, out_specs
        self.scratch_shapes, self.before, self.after = scratch_shapes, before, after


def _fused_call(body, comms, *, name, grid, inputs, in_specs, out_specs, out_shape, scratch_shapes=(), prefetch=()):
    n_pre, n_in, n_out, n_scr = len(prefetch), len(in_specs), len(out_specs), len(scratch_shapes)
    nsteps = math.prod(grid)
    sizes = [(len(c.inputs), len(c.out_shape), len(c.scratch_shapes)) for c in comms]

    def full_body(*refs):
        step = pl.program_id(0)
        for axis in range(1, len(grid)):
            step = step * grid[axis] + pl.program_id(axis)
        pre, refs = refs[:n_pre], refs[n_pre:]
        ins, rest = refs[:n_in], refs[n_in:]
        c_ins = []
        for ci, _, _ in sizes:
            c_ins.append(rest[:ci])
            rest = rest[ci:]
        outs, rest = rest[:n_out], rest[n_out:]
        c_outs = []
        for _, co, _ in sizes:
            c_outs.append(rest[:co])
            rest = rest[co:]
        scr, rest = rest[:n_scr], rest[n_scr:]
        c_scr = []
        for _, _, cs in sizes:
            c_scr.append(rest[:cs])
            rest = rest[cs:]
        for c, a, b, s in zip(comms, c_ins, c_outs, c_scr):
            c.before(step, nsteps, a, b, s)
        body(*pre, *ins, *outs, *scr)
        for c, a, b, s in zip(comms, c_ins, c_outs, c_scr):
            c.after(step, nsteps, a, b, s)

    specs = dict(
        grid=grid, in_specs=list(in_specs) + [s for c in comms for s in c.in_specs],
        out_specs=tuple(out_specs) + tuple(s for c in comms for s in c.out_specs),
        scratch_shapes=list(scratch_shapes) + [s for c in comms for s in c.scratch_shapes])
    if n_pre:
        specs = dict(grid_spec=pltpu.PrefetchScalarGridSpec(num_scalar_prefetch=n_pre, **specs))
    results = pl.pallas_call(
        full_body, name=name, **specs,
        out_shape=tuple(out_shape) + tuple(s for c in comms for s in c.out_shape),
        compiler_params=_params(len(grid)),
    )(*prefetch, *inputs, *[a for c in comms for a in c.inputs])
    own, rest = results[:n_out], results[n_out:]
    comm_results = []
    for _, co, _ in sizes:
        comm_results.append(rest[:co])
        rest = rest[co:]
    return own, comm_results


def _gelu(x):
    return 0.5 * x * (1.0 + lax.erf(x * INV_SQRT2))


def _gelu_and_grad(x):
    cdf = 0.5 * (1.0 + lax.erf(x * INV_SQRT2))
    return x * cdf, cdf + x * jnp.exp(-0.5 * x * x) * INV_SQRT_2PI


def _sigmoid(x):
    return 0.5 + 0.5 * jnp.tanh(0.5 * x)


def _silu_and_grad(x):
    s = _sigmoid(x)
    return x * s, s * (1.0 + x * (1.0 - s))


def _norm1(x, g1, tm, comms=()):
    t = x.shape[0]

    def body(x_ref, g_ref, h_ref):
        xv = x_ref[...]
        r = lax.rsqrt(jnp.mean(xv * xv, axis=-1, keepdims=True) + EPS)
        h_ref[...] = (xv * r * g_ref[...]).astype(BF16)

    row = pl.BlockSpec((tm, D_MODEL), lambda i: (i, 0))
    return _fused_call(body, comms, name="norm1", grid=(t // tm,), inputs=(x, g1),
                       in_specs=[row, _const_spec((1, D_MODEL))], out_specs=(row,),
                       out_shape=(jax.ShapeDtypeStruct((t, D_MODEL), BF16),))


PROJ_W = 2592
LR_REF = 1536
PROJ_ROWS = ((0, LR_REF, 0), (LR_REF + 2 * LOWRANK, PROJ_W, LR_REF), (LR_REF, LR_REF + LANE, LR_COL))


def _in_proj(h, w_in_t, tm, comms=()):
    t = h.shape[0]

    def body(h_ref, w_ref, p_ref):
        hv = h_ref[...]
        for r0, r1, c0 in PROJ_ROWS:
            p_ref[:, c0:c0 + r1 - r0] = _mm_nt(hv, w_ref[r0:r1, :]).astype(BF16)

    return _fused_call(
        body, comms, name="in_proj", grid=(t // tm,), inputs=(h, w_in_t),
        in_specs=[pl.BlockSpec((tm, D_MODEL), lambda i: (i, 0)), _const_spec((PROJ_W, D_MODEL))],
        out_specs=(pl.BlockSpec((tm, PROJ_PAD), lambda i: (i, 0)),),
        out_shape=(jax.ShapeDtypeStruct((t, PROJ_PAD), BF16),))


def _tri(upper):
    r = lax.broadcasted_iota(jnp.int32, (GLA_CHUNK, GLA_CHUNK), 0)
    c = lax.broadcasted_iota(jnp.int32, (GLA_CHUNK, GLA_CHUNK), 1)
    return jnp.where((c >= r) if upper else (c <= r), 1.0, 0.0).astype(BF16)


def _chunk_cumsum(tri, a, add=None):
    hi = a.astype(BF16)
    lo = (a - hi.astype(F32)).astype(BF16)
    dot = functools.partial(jnp.dot, preferred_element_type=F32)
    sums = [dot(tri, hi[_chunk_rows(c)]) + dot(tri, lo[_chunk_rows(c)]) for c in range(a.shape[0] // GLA_CHUNK)]
    return jnp.concatenate(sums if add is None else [s + r for s, r in zip(sums, add)], axis=0)


def _chunk_rows(c):
    return slice(c * GLA_CHUNK, (c + 1) * GLA_CHUNK)


def _gla_masks(rev):
    dk_bits, dv_bits = GLA_DK.bit_length() - 1, GLA_DV.bit_length() - 1
    key_head = lax.broadcasted_iota(jnp.int32, (GLA_CHUNK, KEY_W), 1) >> dk_bits
    val_head = lax.broadcasted_iota(jnp.int32, (GLA_CHUNK, VAL_W), 1) >> dv_bits
    t = lax.broadcasted_iota(jnp.int32, (GLA_HEADS * GLA_CHUNK, GLA_CHUNK), 0) & (GLA_CHUNK - 1)
    s = lax.broadcasted_iota(jnp.int32, (GLA_HEADS * GLA_CHUNK, GLA_CHUNK), 1)
    return key_head, val_head, (s >= t) if rev else (s <= t)


def _stack_heads(a, head_of_lane):
    a = a.astype(BF16)
    return jnp.concatenate([jnp.where(head_of_lane == h, a, jnp.zeros_like(a)) for h in range(GLA_HEADS)], axis=0)


def _rows_by_head(a):
    return jnp.concatenate([a[:, h * GLA_DV:(h + 1) * GLA_DV] for h in range(GLA_HEADS)], axis=0)


def _lanes_by_head(r):
    return jnp.concatenate([r[h * GLA_CHUNK:(h + 1) * GLA_CHUNK] for h in range(GLA_HEADS)], axis=1)


def _head_diagonal(r, head_of_lane):
    rows = r.shape[0] // GLA_HEADS
    out = jnp.where(head_of_lane == 0, r[:rows], 0.0)
    for h in range(1, GLA_HEADS):
        out = out + jnp.where(head_of_lane == h, r[h * rows:(h + 1) * rows], 0.0)
    return out


def _tile_terms(la, q, k, tri, rev):
    nc = la.shape[0] // GLA_CHUNK
    q, k = q.astype(F32), k.astype(F32)
    b = _chunk_cumsum(tri, la)
    ebl = [jnp.exp(b[c * GLA_CHUNK:c * GLA_CHUNK + 1] if rev else b[(c + 1) * GLA_CHUNK - 1:(c + 1) * GLA_CHUNK])
           for c in range(nc)]
    eb = jnp.exp(b)
    enb = jnp.exp(-b)
    kd = k * enb
    ke = jnp.concatenate([kd[_chunk_rows(c)] * ebl[c] for c in range(nc)], axis=0)
    return ebl, eb, enb, q * Q_SCALE * eb, kd, ke


def _log_decay(lr_ref, wd_ref, bd_ref):
    z = _mm(lr_ref[...], wd_ref[...]) + bd_ref[...]
    return z, jax.nn.log_sigmoid(z) * (1.0 / GLA_TAU)


def _p_specs(tg, tile):
    return [pl.BlockSpec((tg, KEY_W), lambda i: (tile(i), 0)),
            pl.BlockSpec((tg, KEY_W), lambda i: (tile(i), 1)),
            pl.BlockSpec((tg, VAL_W), lambda i: (tile(i), 1)),
            pl.BlockSpec((tg, LANE), lambda i: (tile(i), LR_COL // LANE))]


def _gla_fwd_dir(rev, nc, q_ref, k_ref, v_ref, lr_ref, wd_ref, bd_ref, o_ref, st_ref, state):
    key_head, _, causal = _gla_masks(rev)
    order = range(nc - 1, -1, -1) if rev else range(nc)

    def intra():
        _, la = _log_decay(lr_ref, wd_ref, bd_ref)
        ebl, _, _, qd, kd, ke = _tile_terms(la, q_ref[...], k_ref[...], _tri(rev), rev)
        kd = kd.astype(BF16)
        v = {c: v_ref[_chunk_rows(c), :].astype(BF16) for c in order}
        qd_stack = {c: _stack_heads(qd[_chunk_rows(c)], key_head) for c in order}
        ke_stack = {c: _stack_heads(ke[_chunk_rows(c)], key_head) for c in order}
        a_all = {c: _mm_nt(qd_stack[c], kd[_chunk_rows(c)]) for c in order}
        a_all = {c: jnp.where(causal, a_all[c], 0.0).astype(BF16) for c in order}
        head_rows = lambda a, h: a[h * GLA_CHUNK:(h + 1) * GLA_CHUNK]
        head_vals = lambda a, h: a[:, h * GLA_DV:(h + 1) * GLA_DV]
        r = {c: [_mm(head_rows(a_all[c], h), head_vals(v[c], h)) for h in range(GLA_HEADS)] for c in order}
        upd = {c: _mm_tn(_rows_by_head(v[c]), ke_stack[c]) for c in order}
        return {c: (ebl[c], qd_stack[c], r[c], upd[c]) for c in order}

    def scan(terms):
        st = state[...]
        states = {}
        for c in order:
            states[c] = st
            st_ref[c] = st.astype(BF16)
            st = st * terms[c][0] + terms[c][3]
        state[...] = st
        return states

    def inter(terms, states):
        r_inter = {c: _mm_nt(terms[c][1], states[c]) for c in order}
        for c in order:
            o_ref[_chunk_rows(c), :] = jnp.concatenate(
                [terms[c][2][h] + r_inter[c][h * GLA_CHUNK:(h + 1) * GLA_CHUNK] for h in range(GLA_HEADS)], axis=1)

    return intra, scan, inter


def _gla_fwd(p, wd_pad_f, bd_f, wd_pad_b, bd_b, tg, comms=()):
    t = p.shape[0]
    nt = t // tg
    nc = tg // GLA_CHUNK
    up, down = (lambda i: i), (lambda i: nt - 1 - i)

    def body(qf, kf, vf, lrf, qb, kb, vb, lrb, wdf, bdf, wdb, bdb, of, stf, ob, stb, state_f, state_b):
        @pl.when(pl.program_id(0) == 0)
        def _():
            state_f[...] = jnp.zeros_like(state_f)
            state_b[...] = jnp.zeros_like(state_b)

        dirs = [_gla_fwd_dir(False, nc, qf, kf, vf, lrf, wdf, bdf, of, stf, state_f),
                _gla_fwd_dir(True, nc, qb, kb, vb, lrb, wdb, bdb, ob, stb, state_b)]
        terms = [intra() for intra, _, _ in dirs]
        states = [scan(t) for (_, scan, _), t in zip(dirs, terms)]
        for (_, _, inter), t, s in zip(dirs, terms, states):
            inter(t, s)

    wd_spec, bd_spec = _const_spec((LANE, KEY_W)), _const_spec((1, KEY_W))
    outs = lambda tile: (pl.BlockSpec((tg, VAL_W), lambda i: (tile(i), 0)),
                         pl.BlockSpec((nc, GLA_DV, KEY_W), lambda i: (tile(i), 0, 0)))
    out_shape = (jax.ShapeDtypeStruct((t, VAL_W), F32), jax.ShapeDtypeStruct((t // GLA_CHUNK, GLA_DV, KEY_W), BF16))
    return _fused_call(
        body, comms, name="gla_fwd", grid=(nt,), inputs=(p,) * 8 + (wd_pad_f, bd_f, wd_pad_b, bd_b),
        in_specs=_p_specs(tg, up) + _p_specs(tg, down) + [wd_spec, bd_spec, wd_spec, bd_spec],
        out_specs=outs(up) + outs(down), out_shape=out_shape * 2,
        scratch_shapes=[pltpu.VMEM((GLA_DV, KEY_W), F32)] * 2)


def _gla_bwd_dir(rev, nc, q_ref, k_ref, v_ref, lr_ref, wd_ref, bd_ref, st_ref, do_ref,
                 dq_ref, dk_ref, dv_ref, dlr_ref, dwd_ref, dbd_ref, dstate):
    key_head, val_head, causal = _gla_masks(rev)
    order = range(nc) if rev else range(nc - 1, -1, -1)

    def intra():
        z, la = _log_decay(lr_ref, wd_ref, bd_ref)
        tile = _tile_terms(la, q_ref[...], k_ref[...], _tri(rev), rev)
        qd, kd = tile[3], tile[4].astype(BF16)
        v = {c: v_ref[_chunk_rows(c), :].astype(BF16) for c in order}
        d_o = {c: do_ref[_chunk_rows(c), :] for c in order}
        kd_c = {c: kd[_chunk_rows(c)] for c in order}
        qd_stack = {c: _stack_heads(qd[_chunk_rows(c)], key_head) for c in order}
        do_stack = {c: _stack_heads(d_o[c], val_head) for c in order}
        do_rows = {c: _rows_by_head(d_o[c]) for c in order}
        a_all = {c: _mm_nt(qd_stack[c], kd_c[c]) for c in order}
        head_vals = lambda a, h: a[:, h * GLA_DV:(h + 1) * GLA_DV]
        da_all = {c: jnp.concatenate([_mm_nt(head_vals(d_o[c], h), head_vals(v[c], h)) for h in range(GLA_HEADS)],
                                     axis=0) for c in order}
        a_all = {c: jnp.where(causal, a_all[c], 0.0).astype(BF16) for c in order}
        da_all = {c: jnp.where(causal, da_all[c], 0.0).astype(BF16) for c in order}
        dv = {c: _mm_tn(a_all[c], do_stack[c]) for c in order}
        dqd = {c: _mm(jnp.concatenate([do_rows[c], da_all[c]], axis=1),
                      jnp.concatenate([st_ref[c], kd_c[c]], axis=0)) for c in order}
        dkd = {c: _mm_tn(da_all[c], qd_stack[c]) for c in order}
        upd = {c: _mm_tn(do_rows[c], qd_stack[c]) for c in order}
        dqd = {c: _head_diagonal(dqd[c], key_head) for c in order}
        return z, tile, {c: dict(dv=dv[c], dqd=dqd[c], dkd=dkd[c], upd=upd[c]) for c in order}

    def scan(tile, per):
        dst = dstate[...]
        dsts = {}
        for c in order:
            dsts[c] = dst
            dst = dst * tile[0][c] + per[c]["upd"]
        dstate[...] = dst
        return dsts

    def inter(z, tile, per, dsts):
        ebl, eb, enb, qd, kd, ke = tile
        ke_stack = {c: _stack_heads(ke[_chunk_rows(c)], key_head) for c in order}
        v_rows = {c: _rows_by_head(v_ref[_chunk_rows(c), :].astype(BF16)) for c in order}
        dst_b = {c: dsts[c].astype(BF16) for c in order}
        dv_state = {c: _mm_nt(ke_stack[c], dst_b[c]) for c in order}
        dke_c = {c: _mm(v_rows[c], dst_b[c]) for c in order}
        dke_c = {c: _head_diagonal(dke_c[c], key_head) for c in order}
        dbl_c = {}
        for c in order:
            rows = _chunk_rows(c)
            dv_ref[rows, :] = (per[c]["dv"] + _lanes_by_head(dv_state[c])).astype(BF16)
            dbl_c[c] = (jnp.sum(dsts[c] * st_ref[c].astype(F32), axis=0, keepdims=True) * ebl[c]
                        + jnp.sum(dke_c[c] * ke[rows], axis=0, keepdims=True))
        tile_of = lambda parts: jnp.concatenate([parts[c] for c in range(nc)], axis=0)
        dqd, dkd = tile_of({c: per[c]["dqd"] for c in order}), tile_of({c: per[c]["dkd"] for c in order})
        dke = tile_of(dke_c)
        dke_end = tile_of({c: dke_c[c] * ebl[c] for c in order})
        dq_ref[...] = (dqd * eb * Q_SCALE).astype(BF16)
        dk_ref[...] = ((dkd + dke_end) * enb).astype(BF16)
        db = dqd * qd - dkd * kd - dke * ke
        dla = _chunk_cumsum(_tri(not rev), db, [dbl_c[c] for c in range(nc)])
        dz = dla * (_sigmoid(-z) * (1.0 / GLA_TAU))
        dlr_ref[...] = _mm_nt(dz, wd_ref[...]).astype(BF16)
        dwd_ref[...] += _mm_tn(lr_ref[...], dz)
        dbd_ref[...] += jnp.sum(dz, axis=0, keepdims=True)

    return intra, scan, inter


def _gla_bwd(p, wd_pad_f, bd_f, wd_pad_b, bd_b, st_f, st_b, d_o, tg, comms=()):
    t = p.shape[0]
    nt = t // tg
    nc = tg // GLA_CHUNK
    up, down = (lambda i: i), (lambda i: nt - 1 - i)

    def body(qf, kf, vf, lrf, stf, dof, qb, kb, vb, lrb, stb, dob, wdf, bdf, wdb, bdb,
             dqf, dkf, dvf, dlrf, dwdf, dbdf, dqb, dkb, dvb, dlrb, dwdb, dbdb, dstate_f, dstate_b):
        @pl.when(pl.program_id(0) == 0)
        def _():
            for ref in (dstate_f, dstate_b, dwdf, dbdf, dwdb, dbdb):
                ref[...] = jnp.zeros_like(ref)

        dirs = [_gla_bwd_dir(False, nc, qf, kf, vf, lrf, wdf, bdf, stf, dof, dqf, dkf, dvf, dlrf, dwdf, dbdf,
                             dstate_f),
                _gla_bwd_dir(True, nc, qb, kb, vb, lrb, wdb, bdb, stb, dob, dqb, dkb, dvb, dlrb, dwdb, dbdb,
                             dstate_b)]
        first = [intra() for intra, _, _ in dirs]
        dsts = [scan(tile, per) for (_, scan, _), (_, tile, per) in zip(dirs, first)]
        for (_, _, inter), (z, tile, per), d in zip(dirs, first, dsts):
            inter(z, tile, per, d)

    wd_spec, bd_spec = _const_spec((LANE, KEY_W)), _const_spec((1, KEY_W))
    ins = lambda tile: _p_specs(tg, tile) + [pl.BlockSpec((nc, GLA_DV, KEY_W), lambda i: (tile(i), 0, 0)),
                                             pl.BlockSpec((tg, VAL_W), lambda i: (tile(i), 0))]
    outs = lambda tile: (pl.BlockSpec((tg, KEY_W), lambda i: (tile(i), 0)),
                         pl.BlockSpec((tg, KEY_W), lambda i: (tile(i), 0)),
                         pl.BlockSpec((tg, VAL_W), lambda i: (tile(i), 0)),
                         pl.BlockSpec((tg, LANE), lambda i: (tile(i), 0)),
                         _acc_spec((LANE, KEY_W)), _acc_spec((1, KEY_W)))
    out_shape = (jax.ShapeDtypeStruct((t, KEY_W), BF16), jax.ShapeDtypeStruct((t, KEY_W), BF16),
                 jax.ShapeDtypeStruct((t, VAL_W), BF16), jax.ShapeDtypeStruct((t, LANE), BF16),
                 jax.ShapeDtypeStruct((LANE, KEY_W), F32), jax.ShapeDtypeStruct((1, KEY_W), F32))
    scratch = [pltpu.VMEM((GLA_DV, KEY_W), F32)]
    return _fused_call(
        body, comms, name="gla_bwd", grid=(nt,),
        inputs=(p, p, p, p, st_f, d_o, p, p, p, p, st_b, d_o, wd_pad_f, bd_f, wd_pad_b, bd_b),
        in_specs=ins(down) + ins(up) + [wd_spec, bd_spec, wd_spec, bd_spec],
        out_specs=outs(down) + outs(up), out_shape=out_shape * 2, scratch_shapes=scratch * 2)


def _head_rms(o):
    parts, scales = [], []
    for h in range(GLA_HEADS):
        oh = o[:, h * GLA_DV:(h + 1) * GLA_DV]
        r = lax.rsqrt(jnp.mean(oh * oh, axis=-1, keepdims=True) + EPS)
        parts.append(oh * r)
        scales.append(jnp.broadcast_to(r, oh.shape))
    return jnp.concatenate(parts, axis=1), jnp.concatenate(scales, axis=1)


def _layernorm_stats(zv):
    mu = jnp.mean(zv, axis=-1, keepdims=True)
    xc = zv - mu
    rs = lax.rsqrt(jnp.mean(xc * xc, axis=-1, keepdims=True) + EPS)
    return xc * rs, rs


def _mix_fwd(x, o_f, o_b, p, gla_g, ln_g, ln_b, w_sp, b_sp, w_out, tm, comms=()):
    t = x.shape[0]
    nch = tm // GMLP_CHUNK

    def body(x_ref, of_ref, ob_ref, pg_ref, pu_ref, pv_ref, gg_ref, lg_ref, lb_ref, ws_ref, bs_ref, wo_ref,
             x1_ref, y_ref, s_scr):
        on, _ = _head_rms(of_ref[...] + ob_ref[...])
        pg = pg_ref[...].astype(F32)
        y_a = on * gg_ref[...] * (pg * _sigmoid(pg))
        zu = _gelu(pu_ref[...].astype(F32))
        vhat, _ = _layernorm_stats(_gelu(pv_ref[...].astype(F32)))
        vln = (vhat * lg_ref[...] + lb_ref[...]).astype(BF16)
        for g in range(GMLP_GROUPS):
            w_g = ws_ref[g].astype(BF16)
            b_g = bs_ref[g]
            cols = slice(g * LANE, (g + 1) * LANE)
            for n in range(nch):
                rows = slice(n * GMLP_CHUNK, (n + 1) * GMLP_CHUNK)
                s_scr[rows, cols] = jnp.dot(w_g, vln[rows, cols], preferred_element_type=F32) + b_g
        ycat = jnp.concatenate([y_a, zu * s_scr[...]], axis=1).astype(BF16)
        y_ref[...] = ycat
        x1_ref[...] = x_ref[...] + jnp.dot(ycat, wo_ref[...], preferred_element_type=F32)

    half = lambda j: pl.BlockSpec((tm, VAL_W), lambda i: (i, j))
    return _fused_call(
        body, comms, name="mix_fwd", grid=(t // tm,),
        inputs=(x, o_f, o_b, p, p, p, gla_g, ln_g, ln_b, w_sp, b_sp, w_out),
        in_specs=[pl.BlockSpec((tm, D_MODEL), lambda i: (i, 0)), half(0), half(0), half(2), half(3), half(4),
                  _const_spec((1, VAL_W)), _const_spec((1, GMLP_W)), _const_spec((1, GMLP_W)),
                  _const_spec((GMLP_GROUPS, GMLP_CHUNK, GMLP_CHUNK)), _const_spec((GMLP_GROUPS, GMLP_CHUNK, 1)),
                  _const_spec((D_MODEL, D_MODEL))],
        out_specs=(pl.BlockSpec((tm, D_MODEL), lambda i: (i, 0)), pl.BlockSpec((tm, D_MODEL), lambda i: (i, 0))),
        out_shape=(jax.ShapeDtypeStruct((t, D_MODEL), F32), jax.ShapeDtypeStruct((t, D_MODEL), BF16)),
        scratch_shapes=[pltpu.VMEM((tm, GMLP_W), F32)])


def _mix_bwd(dx1, ycat, o_f, o_b, p, gla_g, ln_g, ln_b, w_sp, b_sp, w_out, tm, comms=()):
    t = dx1.shape[0]
    nch = tm // GMLP_CHUNK

    def body(dx1_ref, y_ref, of_ref, ob_ref, pg_ref, pu_ref, pv_ref, gg_ref, lg_ref, lb_ref, ws_ref, bs_ref, wo_ref,
             do_ref, dpg_ref, dpu_ref, dpv_ref, dwo_ref, dgg_ref, dlg_ref, dlb_ref, dws_ref, dbs_ref,
             s_scr, dvln_scr):
        @pl.when(pl.program_id(0) == 0)
        def _():
            for ref in (dwo_ref, dgg_ref, dlg_ref, dlb_ref, dws_ref, dbs_ref):
                ref[...] = jnp.zeros_like(ref)

        dx1 = dx1_ref[...].astype(BF16)
        dycat = _mm_nt(dx1, wo_ref[...])
        dwo_ref[...] += _mm_tn(y_ref[...], dx1)
        dy_a = dycat[:, :VAL_W]
        dy_b = dycat[:, VAL_W:]
        on, r = _head_rms(of_ref[...] + ob_ref[...])
        pg = pg_ref[...].astype(F32)
        sil, dsil = _silu_and_grad(pg)
        gg = gg_ref[...]
        dgg_ref[...] += jnp.sum(dy_a * sil * on, axis=0, keepdims=True)
        don = dy_a * sil * gg
        prod = don * on
        means = jnp.concatenate(
            [jnp.broadcast_to(jnp.mean(prod[:, h * GLA_DV:(h + 1) * GLA_DV], axis=-1, keepdims=True),
                              (tm, GLA_DV)) for h in range(GLA_HEADS)], axis=1)
        do_ref[...] = (r * (don - on * means)).astype(BF16)
        dpg_ref[...] = (dy_a * on * gg * dsil).astype(BF16)
        pu = pu_ref[...].astype(F32)
        pv = pv_ref[...].astype(F32)
        zu, dzu_dpu = _gelu_and_grad(pu)
        zv, dzv_dpv = _gelu_and_grad(pv)
        vhat, rs = _layernorm_stats(zv)
        lg = lg_ref[...]
        vln = (vhat * lg + lb_ref[...]).astype(BF16)
        ds32 = dy_b * zu
        ds = ds32.astype(BF16)
        blocks = [(g, n) for g in range(GMLP_GROUPS) for n in range(nch)]
        at = lambda g, n: (slice(n * GMLP_CHUNK, (n + 1) * GMLP_CHUNK), slice(g * LANE, (g + 1) * LANE))
        w_sp = [ws_ref[g].astype(BF16) for g in range(GMLP_GROUPS)]
        v_blk = {b: vln[at(*b)] for b in blocks}
        ds_blk = {b: ds[at(*b)] for b in blocks}
        s_blk = {b: jnp.dot(w_sp[b[0]], v_blk[b], preferred_element_type=F32) for b in blocks}
        dw_blk = {b: _mm_nt(ds_blk[b], v_blk[b]) for b in blocks}
        dvln_blk = {b: _mm_tn(w_sp[b[0]], ds_blk[b]) for b in blocks}
        for b in blocks:
            s_scr[at(*b)] = s_blk[b] + bs_ref[b[0]]
            dvln_scr[at(*b)] = dvln_blk[b]
        for g in range(GMLP_GROUPS):
            dws_ref[g] += sum(dw_blk[(g, n)] for n in range(nch))
            dbs_ref[g] += sum(jnp.sum(ds32[at(g, n)], axis=-1, keepdims=True) for n in range(nch))
        dpu_ref[...] = (dy_b * s_scr[...] * dzu_dpu).astype(BF16)
        dvln = dvln_scr[...]
        dlg_ref[...] += jnp.sum(dvln * vhat, axis=0, keepdims=True)
        dlb_ref[...] += jnp.sum(dvln, axis=0, keepdims=True)
        dvhat = dvln * lg
        dzv = rs * (dvhat - jnp.mean(dvhat, axis=-1, keepdims=True)
                    - vhat * jnp.mean(dvhat * vhat, axis=-1, keepdims=True))
        dpv_ref[...] = (dzv * dzv_dpv).astype(BF16)

    half = lambda j: pl.BlockSpec((tm, VAL_W), lambda i: (i, j))
    full = pl.BlockSpec((tm, D_MODEL), lambda i: (i, 0))
    sp_shape = (GMLP_GROUPS, GMLP_CHUNK, GMLP_CHUNK)
    bs_shape = (GMLP_GROUPS, GMLP_CHUNK, 1)
    return _fused_call(
        body, comms, name="mix_bwd", grid=(t // tm,),
        inputs=(dx1, ycat, o_f, o_b, p, p, p, gla_g, ln_g, ln_b, w_sp, b_sp, w_out),
        in_specs=[full, full, half(0), half(0), half(2), half(3), half(4),
                  _const_spec((1, VAL_W)), _const_spec((1, GMLP_W)), _const_spec((1, GMLP_W)),
                  _const_spec(sp_shape), _const_spec(bs_shape), _const_spec((D_MODEL, D_MODEL))],
        out_specs=(half(0), half(0), half(0), half(0), _acc_spec((D_MODEL, D_MODEL)), _acc_spec((1, VAL_W)),
                   _acc_spec((1, GMLP_W)), _acc_spec((1, GMLP_W)), _acc_spec(sp_shape), _acc_spec(bs_shape)),
        out_shape=(jax.ShapeDtypeStruct((t, VAL_W), BF16),) * 4 + (
            jax.ShapeDtypeStruct((D_MODEL, D_MODEL), F32), jax.ShapeDtypeStruct((1, VAL_W), F32),
            jax.ShapeDtypeStruct((1, GMLP_W), F32), jax.ShapeDtypeStruct((1, GMLP_W), F32),
            jax.ShapeDtypeStruct(sp_shape, F32), jax.ShapeDtypeStruct(bs_shape, F32)),
        scratch_shapes=[pltpu.VMEM((tm, GMLP_W), F32), pltpu.VMEM((tm, GMLP_W), F32)])


def _rms_bwd(dy_scaled, xn, r):
    return r * (dy_scaled - xn * jnp.mean(dy_scaled * xn, axis=-1, keepdims=True))


def _ffn(x1, target, g2, gf, w_gate, w_up, w_down, tm):
    t = x1.shape[0]

    def body(x1_ref, tg_ref, g2_ref, gf_ref, wg_ref, wu_ref, wd_ref,
             dx1_ref, h2_ref, dgate_ref, dup_ref, act_ref, dx2_ref, loss_ref, dgf_ref, dg2_ref):
        @pl.when(pl.program_id(0) == 0)
        def _():
            for ref in (loss_ref, dgf_ref, dg2_ref):
                ref[...] = jnp.zeros_like(ref)

        x1v = x1_ref[...]
        g2v = g2_ref[...]
        gfv = gf_ref[...]
        r2 = lax.rsqrt(jnp.mean(x1v * x1v, axis=-1, keepdims=True) + EPS)
        xn1 = x1v * r2
        h2 = (xn1 * g2v).astype(BF16)
        h2_ref[...] = h2
        gate = _mm_nt(h2, wg_ref[...])
        up = _mm_nt(h2, wu_ref[...])
        sil, dsil = _silu_and_grad(gate)
        act = (sil * up).astype(BF16)
        act_ref[...] = act
        x2 = x1v + jnp.dot(act, wd_ref[...], preferred_element_type=F32)
        rf = lax.rsqrt(jnp.mean(x2 * x2, axis=-1, keepdims=True) + EPS)
        xn2 = x2 * rf
        err = xn2 * gfv - tg_ref[...]
        loss_ref[...] += 0.5 * jnp.sum(jnp.mean(err * err, axis=-1, keepdims=True))
        dy = err * (1.0 / D_MODEL)
        dgf_ref[...] += jnp.sum(dy * xn2, axis=0, keepdims=True)
        dx2 = _rms_bwd(dy * gfv, xn2, rf)
        dx2b = dx2.astype(BF16)
        dx2_ref[...] = dx2b
        dact = _mm_nt(dx2b, wd_ref[...])
        dgate = (dact * up * dsil).astype(BF16)
        dup = (dact * sil).astype(BF16)
        dgate_ref[...] = dgate
        dup_ref[...] = dup
        dh2 = _mm(dgate, wg_ref[...]) + _mm(dup, wu_ref[...])
        dg2_ref[...] += jnp.sum(dh2 * xn1, axis=0, keepdims=True)
        dx1_ref[...] = dx2 + _rms_bwd(dh2 * g2v, xn1, r2)

    row = lambda w: pl.BlockSpec((tm, w), lambda i: (i, 0))
    return pl.pallas_call(
        body, name="ffn_fwd_bwd", grid=(t // tm,),
        in_specs=[row(D_MODEL), row(D_MODEL), _const_spec((1, D_MODEL)), _const_spec((1, D_MODEL)),
                  _const_spec((D_FF, D_MODEL)), _const_spec((D_FF, D_MODEL)), _const_spec((D_FF, D_MODEL))],
        out_specs=(row(D_MODEL), row(D_MODEL), row(D_FF), row(D_FF), row(D_FF), row(D_MODEL),
                   _acc_spec((8, LANE)), _acc_spec((1, D_MODEL)), _acc_spec((1, D_MODEL))),
        out_shape=(jax.ShapeDtypeStruct((t, D_MODEL), F32), jax.ShapeDtypeStruct((t, D_MODEL), BF16),
                   jax.ShapeDtypeStruct((t, D_FF), BF16), jax.ShapeDtypeStruct((t, D_FF), BF16),
                   jax.ShapeDtypeStruct((t, D_FF), BF16), jax.ShapeDtypeStruct((t, D_MODEL), BF16),
                   jax.ShapeDtypeStruct((8, LANE), F32), jax.ShapeDtypeStruct((1, D_MODEL), F32),
                   jax.ShapeDtypeStruct((1, D_MODEL), F32)),
        compiler_params=_params(),
    )(x1, target, g2, gf, w_gate, w_up, w_down)


def _matmul_tn(a, b, tm, tk, name, comms=(), cols=None):
    t, m = a.shape
    cb, n = (0, b.shape[1]) if cols is None else cols

    def body(a_ref, b_ref, o_ref):
        @pl.when(pl.program_id(1) == 0)
        def _():
            o_ref[...] = jnp.zeros_like(o_ref)

        o_ref[...] += _mm_tn(a_ref[...], b_ref[...])

    (out,), comm_results = _fused_call(
        body, comms, name=name, grid=(m // tm, t // tk), inputs=(a, b),
        in_specs=[pl.BlockSpec((tk, tm), lambda j, k: (k, j)), pl.BlockSpec((tk, n), lambda j, k: (k, cb))],
        out_specs=(pl.BlockSpec((tm, n), lambda j, k: (j, 0)),),
        out_shape=(jax.ShapeDtypeStruct((m, n), F32),))
    return out, comm_results


def _in_proj_bwd(x, g1, dx1, dq_f, dq_b, dk_f, dk_b, dv_f, dv_b, dpg, dpu, dpv, dlr_f, dlr_b, w_main, tm, comms=()):
    t = x.shape[0]

    def body(x_ref, g_ref, dx1_ref, dqf, dqb, dkf, dkb, dvf, dvb, dg, du, dv, dlf, dlb, w_ref,
             dx_ref, dp_ref, dg1_ref):
        @pl.when(pl.program_id(0) == 0)
        def _():
            dg1_ref[...] = jnp.zeros_like(dg1_ref)

        both = lambda a, b: (a[...].astype(F32) + b[...].astype(F32)).astype(BF16)
        dp = jnp.concatenate([both(dqf, dqb), both(dkf, dkb), both(dvf, dvb), dg[...], du[...], dv[...],
                              both(dlf, dlb)], axis=1)
        dp_ref[...] = dp
        dh = sum(_mm(dp[:, c0:c0 + r1 - r0], w_ref[r0:r1, :]) for r0, r1, c0 in PROJ_ROWS)
        xv = x_ref[...]
        r = lax.rsqrt(jnp.mean(xv * xv, axis=-1, keepdims=True) + EPS)
        xn = xv * r
        dg1_ref[...] += jnp.sum(dh * xn, axis=0, keepdims=True)
        dx_ref[...] = dx1_ref[...] + _rms_bwd(dh * g_ref[...], xn, r)

    row = lambda w: pl.BlockSpec((tm, w), lambda i: (i, 0))
    return _fused_call(
        body, comms, name="in_proj_bwd", grid=(t // tm,),
        inputs=(x, g1, dx1, dq_f, dq_b, dk_f, dk_b, dv_f, dv_b, dpg, dpu, dpv, dlr_f, dlr_b, w_main),
        in_specs=[row(D_MODEL), _const_spec((1, D_MODEL)), row(D_MODEL), row(KEY_W), row(KEY_W), row(KEY_W),
                  row(KEY_W), row(VAL_W), row(VAL_W), row(VAL_W), row(VAL_W), row(VAL_W), row(LANE), row(LANE),
                  _const_spec((PROJ_W, D_MODEL))],
        out_specs=(row(D_MODEL), row(PROJ_PAD), _acc_spec((1, D_MODEL))),
        out_shape=(jax.ShapeDtypeStruct((t, D_MODEL), F32), jax.ShapeDtypeStruct((t, PROJ_PAD), BF16),
                   jax.ShapeDtypeStruct((1, D_MODEL), F32)))


def _adamw(w, g, m, v):
    m_new = ADAM_B1 * m + (1.0 - ADAM_B1) * g
    v_new = ADAM_B2 * v + (1.0 - ADAM_B2) * (g * g)
    m_hat = m_new / (1.0 - ADAM_B1 ** ADAM_STEP)
    v_hat = v_new / (1.0 - ADAM_B2 ** ADAM_STEP)
    delta = -ADAM_LR * (m_hat / (jnp.sqrt(v_hat) + ADAM_EPS) + ADAM_WD * w)
    return delta, m_new, v_new


def _adamw_shard(own, recv, w, m, v, tr, name):
    r, c = w.shape

    def body(own_ref, recv_ref, w_ref, m_ref, v_ref, g_ref, d_ref, nm_ref, nv_ref):
        g = own_ref[...]
        for k in range(3):
            g = g + recv_ref[k].astype(F32)
        g_ref[...] = g
        d_ref[...], nm_ref[...], nv_ref[...] = _adamw(w_ref[...], g, m_ref[...], v_ref[...])

    row = pl.BlockSpec((tr, c), lambda i: (i, 0))
    return pl.pallas_call(
        body, name=name, grid=(r // tr,),
        in_specs=[row, pl.BlockSpec((3, tr, c), lambda i: (0, i, 0)), row, row, row],
        out_specs=(row,) * 4, out_shape=(jax.ShapeDtypeStruct((r, c), F32),) * 4,
        compiler_params=_params(),
    )(own, recv, w, m, v)


def _adamw_col_halves(own, recv, w, m, v, name):
    r, c = w.shape
    h = c // 2

    def body(own_lo, own_hi, recv_lo, recv_hi, w_ref, m_ref, v_ref, g_ref, d_ref, nm_ref, nv_ref):
        lo = pl.program_id(0) == 0
        g = jnp.where(lo, own_lo[...], own_hi[...])
        for k in range(3):
            g = g + jnp.where(lo, recv_lo[k], recv_hi[k]).astype(F32)
        g_ref[...] = g
        d_ref[...], nm_ref[...], nv_ref[...] = _adamw(w_ref[...], g, m_ref[...], v_ref[...])

    whole, whole3 = pl.BlockSpec((r, h), lambda i: (0, 0)), pl.BlockSpec((3, r, h), lambda i: (0, 0, 0))
    col = pl.BlockSpec((r, h), lambda i: (0, i))
    return pl.pallas_call(
        body, name=name, grid=(2,),
        in_specs=[whole, whole, whole3, whole3, col, col, col],
        out_specs=(col,) * 4, out_shape=(jax.ShapeDtypeStruct((r, c), F32),) * 4,
        compiler_params=_params(),
    )(*own, *recv, w, m, v)


def _adamw_small(entries):
    stacks = []
    for (g, _, _), _, _, _ in entries:
        if not any(g is s for s in stacks):
            stacks.append(g)
    where = [next(i for i, s in enumerate(stacks) if s is g) for (g, _, _), _, _, _ in entries]
    ns, ne = len(stacks), len(entries)

    def body(*refs):
        s_refs, wmv, outs = refs[:ns], refs[ns:ns + 3 * ne], refs[ns + 3 * ne:]
        for e, ((_, r0, nr), _, _, _) in enumerate(entries):
            grad = s_refs[where[e]][r0:r0 + nr, :]
            w_ref, m_ref, v_ref = wmv[3 * e:3 * e + 3]
            g_ref, d_ref, nm_ref, nv_ref = outs[4 * e:4 * e + 4]
            g_ref[...] = grad
            d_ref[...], nm_ref[...], nv_ref[...] = _adamw(w_ref[...], grad, m_ref[...], v_ref[...])

    results = pl.pallas_call(
        body, name="adamw_small",
        out_shape=tuple(jax.ShapeDtypeStruct(w.shape, F32) for _, w, _, _ in entries for _ in range(4)),
        compiler_params=pltpu.CompilerParams(vmem_limit_bytes=VMEM_LIMIT),
    )(*stacks, *[a for _, w, m, v in entries for a in (w, m, v)])
    return [results[4 * e:4 * e + 4] for e in range(ne)]


def _mesh_pos():
    return lax.axis_index("x"), lax.axis_index("y"), lax.axis_index("c")


def _other_chips(x, y):
    return [(x, 1 - y), (1 - x, y), (1 - x, 1 - y)]


_VMEM_WHOLE = pl.BlockSpec(memory_space=pltpu.VMEM)
_HBM_WHOLE = pl.BlockSpec(memory_space=pl.ANY)


def _gather_comm(shards, cast, mid=((1, 2), (3, 4))):
    na = len(shards)
    staged = [a for a in range(na) if cast[a]]

    def phases(in_refs, out_refs, scr):
        stage = dict(zip(staged, scr[:len(staged)]))
        send_sems, recv_sems, local_sems = scr[len(staged):]
        x, y, c = _mesh_pos()
        me, sibling = (x, y, c), (x, y, 1 - c)
        chip_a, chip_b, diagonal = (x ^ c, y ^ (1 - c)), (x ^ (1 - c), y ^ c), (1 - x, 1 - y)
        srcs = [stage[a] if cast[a] else in_refs[a] for a in range(na)]

        def rows(a, pos):
            px, py, pc = pos
            return out_refs[a].at[4 * px + 2 * py + pc]

        def copy(a, k, block, to, src=None):
            return pltpu.make_async_remote_copy(
                src_ref=rows(a, block) if src is None else src, dst_ref=rows(a, block),
                send_sem=send_sems.at[a, k], recv_sem=recv_sems.at[a, k], device_id=to, device_id_type=MESH_ID)

        mine = [pltpu.make_async_copy(srcs[a], rows(a, me), local_sems.at[a]) for a in range(na)]
        own = [copy(a, k, me, to, src=srcs[a]) for a in range(na)
               for k, to in ((0, sibling), (1, (*chip_a, c)), (2, (*chip_b, c)))]
        onward = [copy(a, 3, (*chip_a, c), (*chip_b, c)) for a in range(na)]
        to_sibling = {k: [copy(a, k, (*chip, c), sibling) for a in range(na)]
                      for k, chip in ((4, chip_a), (5, chip_b), (6, diagonal))}

        def start():
            for a in staged:
                stage[a][...] = in_refs[a][...].astype(BF16)
            for cp in mine + own:
                cp.start()

        def forward_neighbours():
            for a in range(na):
                copy(a, 1, (*chip_a, c), me).wait_recv()
                onward[a].start()
                to_sibling[4][a].start()
            for a in range(na):
                copy(a, 2, (*chip_b, c), me).wait_recv()
                to_sibling[5][a].start()

        def forward_diagonal():
            for a in range(na):
                copy(a, 3, (*diagonal, c), me).wait_recv()
                to_sibling[6][a].start()

        def finish():
            for a in range(na):
                for k, chip in ((0, (x, y)), (4, chip_b), (5, chip_a), (6, diagonal)):
                    copy(a, k, (*chip, 1 - c), me).wait_recv()
            for cp in own + onward + to_sibling[4] + to_sibling[5] + to_sibling[6]:
                cp.wait_send()
            for cp in mine:
                cp.wait()

        return start, forward_neighbours, forward_diagonal, finish

    def before(step, nsteps, in_refs, out_refs, scr):
        start, forward_neighbours, forward_diagonal, _ = phases(in_refs, out_refs, scr)
        pl.when(step == 0)(start)
        pl.when(step == nsteps * mid[0][0] // mid[0][1])(forward_neighbours)
        pl.when(step == nsteps * mid[1][0] // mid[1][1])(forward_diagonal)

    def after(step, nsteps, in_refs, out_refs, scr):
        pl.when(step == nsteps - 1)(phases(in_refs, out_refs, scr)[3])

    return _Comm(
        inputs=list(shards), in_specs=[_VMEM_WHOLE] * na,
        out_shape=[jax.ShapeDtypeStruct((N_DEV,) + s.shape, BF16 if cast[a] else s.dtype)
                   for a, s in enumerate(shards)],
        out_specs=[_HBM_WHOLE] * na,
        scratch_shapes=[pltpu.VMEM(shards[a].shape, BF16) for a in staged] + [
            pltpu.SemaphoreType.DMA((na, 7)), pltpu.SemaphoreType.DMA((na, 7)), pltpu.SemaphoreType.DMA((na,))],
        before=before, after=after)


def _exchange_comm(arrays, out_shape, make_copies):
    na = len(arrays)

    def copies(in_refs, out_refs, scr):
        return make_copies(in_refs, out_refs, *scr)

    def before(step, nsteps, in_refs, out_refs, scr):
        @pl.when(step == 0)
        def _():
            for cp in copies(in_refs, out_refs, scr):
                cp.start()

    def after(step, nsteps, in_refs, out_refs, scr):
        @pl.when(step == nsteps - 1)
        def _():
            for cp in copies(in_refs, out_refs, scr):
                cp.wait()

    return _Comm(inputs=list(arrays), in_specs=[_HBM_WHOLE] * na, out_shape=list(out_shape),
                 out_specs=[_HBM_WHOLE] * na,
                 scratch_shapes=[pltpu.SemaphoreType.DMA((na, 3)), pltpu.SemaphoreType.DMA((na, 3))],
                 before=before, after=after)


def _sibling_exchange_comm(grads):
    def make_copies(in_refs, out_refs, send_sems, recv_sems):
        x, y, c = _mesh_pos()
        return [pltpu.make_async_remote_copy(
            src_ref=in_refs[a].at[:, pl.ds(1 - c, 1)], dst_ref=out_refs[a], send_sem=send_sems.at[a, 0],
            recv_sem=recv_sems.at[a, 0], device_id=(x, y, 1 - c), device_id_type=MESH_ID)
            for a in range(len(grads))]

    return _exchange_comm(grads, [jax.ShapeDtypeStruct((4, 1) + g.shape[2:], F32) for g in grads], make_copies)


def _chips_exchange_comm(partials):
    def make_copies(in_refs, out_refs, send_sems, recv_sems):
        x, y, c = _mesh_pos()
        return [pltpu.make_async_remote_copy(
            src_ref=in_refs[a].at[j], dst_ref=out_refs[a].at[j], send_sem=send_sems.at[a, j],
            recv_sem=recv_sems.at[a, j], device_id=(*chip, c), device_id_type=MESH_ID)
            for a in range(len(partials)) for j, chip in enumerate(_other_chips(x, y))]

    return _exchange_comm(partials, [jax.ShapeDtypeStruct(g.shape, BF16) for g in partials], make_copies)


def _comm_only(comms, name):
    return _fused_call(lambda: None, comms, name=name, grid=(1,), inputs=(), in_specs=[], out_specs=(),
                       out_shape=())[1]


def _chip_sum(my_pos, mine, from_sibling, tr, name, comms=()):
    _, _, r, c = mine.shape

    def body(pos_ref, a_ref, b_ref, own_ref, out_ref):
        s = a_ref[0, 0] + b_ref[0, 0]

        @pl.when(pl.program_id(1) == 0)
        def _():
            own_ref[...] = s

        @pl.when(pl.program_id(1) > 0)
        def _():
            out_ref[0] = s.astype(BF16)

    return _fused_call(
        body, comms, name=name, grid=(r // tr, 4), prefetch=(my_pos,), inputs=(mine, from_sibling),
        in_specs=[pl.BlockSpec((1, 1, tr, c), lambda i, k, pos: (pos[0] ^ k, pos[1], i, 0)),
                  pl.BlockSpec((1, 1, tr, c), lambda i, k, pos: (pos[0] ^ k, 0, i, 0))],
        out_specs=(pl.BlockSpec((tr, c), lambda i, k, pos: (i, 0)),
                   pl.BlockSpec((1, tr, c), lambda i, k, pos: (jnp.maximum(k - 1, 0), i, 0))),
        out_shape=(jax.ShapeDtypeStruct((r, c), F32), jax.ShapeDtypeStruct((3, r, c), BF16)))


def _all_reduce_small_comm(parts):
    na = len(parts)

    def copies(in_refs, scr):
        gathered, (send_sems, recv_sems) = scr[:na], scr[na:]
        x, y, c = _mesh_pos()
        my_id = 4 * x + 2 * y + c
        return my_id, [pltpu.make_async_remote_copy(
            src_ref=in_refs[a], dst_ref=gathered[a].at[my_id], send_sem=send_sems.at[a, k - 1],
            recv_sem=recv_sems.at[a, k - 1], device_id=(x ^ (k >> 2), y ^ ((k >> 1) & 1), c ^ (k & 1)),
            device_id_type=MESH_ID) for a in range(na) for k in range(1, N_DEV)]

    def before(step, nsteps, in_refs, out_refs, scr):
        @pl.when(step == 0)
        def _():
            for cp in copies(in_refs, scr)[1]:
                cp.start()

    def after(step, nsteps, in_refs, out_refs, scr):
        @pl.when(step == nsteps - 1)
        def _():
            my_id, cps = copies(in_refs, scr)
            for a in range(na):
                scr[a][my_id] = in_refs[a][...]
            for cp in cps:
                cp.wait()
            for a in range(na):
                acc = scr[a][0]
                for d in range(1, N_DEV):
                    acc = acc + scr[a][d]
                out_refs[a][...] = acc

    return _Comm(inputs=list(parts), in_specs=[_VMEM_WHOLE] * na,
                 out_shape=[jax.ShapeDtypeStruct(p.shape, F32) for p in parts], out_specs=[_VMEM_WHOLE] * na,
                 scratch_shapes=[pltpu.VMEM((N_DEV,) + p.shape, F32) for p in parts] + [
                     pltpu.SemaphoreType.DMA((na, N_DEV - 1)), pltpu.SemaphoreType.DMA((na, N_DEV - 1))],
                 before=before, after=after)


def _unshard_cols(g):
    return jnp.transpose(g, (1, 0, 2)).reshape(g.shape[1], N_DEV * g.shape[2])


def _row_blocks(w):
    return w.reshape(4, 2, w.shape[0] // N_DEV, w.shape[1])


def _stack_rows(parts):
    a = jnp.concatenate(parts, axis=0)
    return jnp.pad(a, ((0, (-a.shape[0]) % 8), (0, 0)))


def _w_in_grad_blocks(dw):
    return _row_blocks(jnp.concatenate([dw[:LR_REF], dw[LR_COL:LR_COL + 2 * LOWRANK], dw[LR_REF:LR_COL]], axis=0))


def _padded_decay_weights(wd_f, wd_b):
    zeros = lambda n: jnp.zeros((n, KEY_W), F32)
    return (jnp.concatenate([wd_f, zeros(LANE - LOWRANK)], axis=0),
            jnp.concatenate([zeros(LOWRANK), wd_b, zeros(LANE - 2 * LOWRANK)], axis=0))


def kernel(x, norm1_g, w_in,w_decay_f, b_decay_f, w_decay_b, b_decay_b, gla_norm_g, gmlp_ln_g, gmlp_ln_b, w_spatial, b_spatial, w_out, norm2_g, w_gate, w_up, w_down, final_norm_g, loss_target, m_norm1_g, m_w_in, m_w_decay_f, m_b_decay_f, m_w_decay_b, m_b_decay_b, m_gla_norm_g, m_gmlp_ln_g, m_gmlp_ln_b, m_w_spatial, m_b_spatial, m_w_out, m_norm2_g, m_w_gate, m_w_up, m_w_down, m_final_norm_g, v_norm1_g, v_w_in, v_w_decay_f, v_b_decay_f, v_w_decay_b, v_b_decay_b, v_gla_norm_g, v_gmlp_ln_g, v_gmlp_ln_b, v_w_spatial, v_b_spatial, v_w_out, v_norm2_g, v_w_gate, v_w_up, v_w_down, v_final_norm_g):
    t = x.shape[1]
    xt = x[0]
    target = loss_target[0]
    pos_x, pos_y, pos_c = _mesh_pos()
    my_pos = jnp.stack([2 * pos_x + pos_y, pos_c]).astype(jnp.int32)
    my_id = 4 * pos_x + 2 * pos_y + pos_c

    tile = lambda n: min(n, t)
    ln_g, ln_b, w_sp = gmlp_ln_g, gmlp_ln_b, w_spatial[0]
    b_sp_col = b_spatial[0][:, :, None]
    shard = {"w_in": w_in[0].T, "w_out": w_out[0], "w_gate": w_gate[0].T, "w_up": w_up[0].T, "w_down": w_down[0]}
    shard_m = {"w_in": m_w_in[0].T, "w_out": m_w_out[0], "w_gate": m_w_gate[0].T, "w_up": m_w_up[0].T,
               "w_down": m_w_down[0]}
    shard_v = {"w_in": v_w_in[0].T, "w_out": v_w_out[0], "w_gate": v_w_gate[0].T, "w_up": v_w_up[0].T,
               "w_down": v_w_down[0]}
    transposed = ("w_in", "w_gate", "w_up")
    chip_sum_with = lambda n, g, s, comms: _chip_sum(my_pos, g, s[0], g.shape[2], "chip_sum_" + n, comms)
    chip_sum = lambda n, g, s: chip_sum_with(n, g, s, ())[0]

    decay_shard = jnp.stack([w_decay_f[0], w_decay_b[0]])
    (hb,), ((g_in, g_decay),) = _norm1(xt, norm1_g, tile(TOKEN_TILE["norm1"]),
                                       [_gather_comm([shard["w_in"], decay_shard], [True, False])])
    w_in_t = g_in.reshape(PROJ_W, D_MODEL)
    wd_pad_f, wd_pad_b = _padded_decay_weights(_unshard_cols(g_decay[:, 0]), _unshard_cols(g_decay[:, 1]))
    (p,), ((g_gate, g_out),) = _in_proj(
        hb, w_in_t, tile(TOKEN_TILE["in_proj"]), [_gather_comm([shard["w_gate"], shard["w_out"]], [True, True])])
    (o_f, st_f, o_b, st_b), ((g_up,),) = _gla_fwd(
        p, wd_pad_f, b_decay_f, wd_pad_b, b_decay_b, tile(TOKEN_TILE["gla"]), [_gather_comm([shard["w_up"]], [True])])
    w_out_full = g_out.reshape(D_MODEL, D_MODEL)
    (x1, ycat), ((g_down,),) = _mix_fwd(xt, o_f, o_b, p, gla_norm_g, ln_g, ln_b, w_sp, b_sp_col, w_out_full,
                                        tile(TOKEN_TILE["mix_fwd"]), [_gather_comm([shard["w_down"]], [True])])

    dx1, h2b, dgate, dup, act, dx2, loss_acc, d_gf, d_g2 = _ffn(
        x1, target, norm2_g, final_norm_g[None, :], g_gate.reshape(D_FF, D_MODEL), g_up.reshape(D_FF, D_MODEL),
        g_down.reshape(D_FF, D_MODEL), tile(TOKEN_TILE["ffn"]))
    dw_gate, _ = _matmul_tn(dgate, h2b, D_FF // 2, tile(TOKEN_TILE["dw"]), "grad_w_gate")
    dw_up, _ = _matmul_tn(dup, h2b, D_FF // 2, tile(TOKEN_TILE["dw"]), "grad_w_up")
    dw_down, _ = _matmul_tn(act, dx2, D_FF // 2, tile(TOKEN_TILE["dw"]), "grad_w_down")

    ffn_grads = [_row_blocks(dw_gate), _row_blocks(dw_up), _row_blocks(dw_down)]
    (d_o, dpg, dpu, dpv, dw_out, d_gg, d_lg, d_lb, dw_sp, db_sp), (ffn_sib,) = _mix_bwd(
        dx1, ycat, o_f, o_b, p, gla_norm_g, ln_g, ln_b, w_sp, b_sp_col, w_out_full,
        tile(TOKEN_TILE["mix_bwd"]),
        [_sibling_exchange_comm(ffn_grads)])
    ffn_names = ["w_gate", "w_up", "w_down"]
    ffn_sums = [chip_sum(n, g, [s]) for n, g, s in zip(ffn_names, ffn_grads, ffn_sib)]
    out_grad = _row_blocks(dw_out)
    (dq_f, dk_f, dv_f, dlr_f, dwd_f, dbd_f, dq_b, dk_b, dv_b, dlr_b, dwd_b, dbd_b), (ffn_recv, out_sib) = _gla_bwd(
        p, wd_pad_f, b_decay_f, wd_pad_b, b_decay_b, st_f, st_b, d_o, tile(TOKEN_TILE["gla"]),
        [_chips_exchange_comm([s[1] for s in ffn_sums]), _sibling_exchange_comm([out_grad])])
    out_sum = chip_sum("w_out", out_grad, out_sib)
    (grad_x, dp, d_g1), _ = _in_proj_bwd(
        xt, norm1_g, dx1, dq_f, dq_b, dk_f, dk_b, dv_f, dv_b, dpg, dpu, dpv, dlr_f, dlr_b, w_in_t,
        tile(TOKEN_TILE["in_proj_bwd"]))

    stacks = [_stack_rows([d_g1, d_g2, d_gf]), _stack_rows([d_gg, d_lg, d_lb]),
              _stack_rows([dbd_f, dbd_b, jnp.zeros((DECAY_W_ROW - 2, KEY_W), F32), dwd_f[:LOWRANK],
                           dwd_b[LOWRANK:2 * LOWRANK]]),
              _stack_rows([dw_sp.reshape(GMLP_W, GMLP_CHUNK), db_sp[:, :, 0], loss_acc[:1]])]
    half_w = D_MODEL // 2
    dw_lo, (small_sums, out_recv) = _matmul_tn(
        dp, hb, PROJ_PAD // 3, tile(TOKEN_TILE["dw"]), "grad_w_in_lo",
        [_all_reduce_small_comm(stacks), _chips_exchange_comm([out_sum[1]])], cols=(0, half_w))
    in_lo = _w_in_grad_blocks(dw_lo)
    dw_hi, (lo_sib,) = _matmul_tn(
        dp, hb, PROJ_PAD // 3, tile(TOKEN_TILE["dw"]), "grad_w_in_hi",
        [_sibling_exchange_comm([in_lo])], cols=(1, half_w))
    in_hi = _w_in_grad_blocks(dw_hi)
    lo_sum, (hi_sib,) = chip_sum_with("w_in_lo", in_lo, lo_sib, [_sibling_exchange_comm([in_hi])])
    hi_sum, (lo_recv,) = chip_sum_with("w_in_hi", in_hi, hi_sib, [_chips_exchange_comm([lo_sum[1]])])
    (hi_recv,) = _comm_only([_chips_exchange_comm([hi_sum[1]])], "grad_w_in_exchange_chips")

    big_out = {"w_in": [r.T for r in _adamw_col_halves(
        (lo_sum[0], hi_sum[0]), (lo_recv[0], hi_recv[0]), shard["w_in"], shard_m["w_in"], shard_v["w_in"],
        "adamw_w_in")]}
    names = ["w_out", "w_gate", "w_up", "w_down"]
    sums = [out_sum] + ffn_sums
    received = [out_recv[0]] + list(ffn_recv)
    for n, s, rc in zip(names, sums, received):
        rows = shard[n].shape[0]
        half = rows // 2 if rows % 32 == 0 else rows
        res = _adamw_shard(s[0], rc, shard[n], shard_m[n], shard_v[n], half, "adamw_" + n)
        big_out[n] = [r.T if n in transposed else r for r in res]

    s1024, s512, s256, s128 = small_sums
    loss = s128[GMLP_W + GMLP_GROUPS, 0]
    col0 = my_id * (KEY_W // N_DEV)
    decay_cols = lambda row0: lax.dynamic_slice(s256, (row0, col0), (LOWRANK, KEY_W // N_DEV))
    flat = lambda a: a.reshape(-1, a.shape[-1])
    small = {
        "norm1_g": ((s1024, 0, 1), norm1_g, m_norm1_g, v_norm1_g),
        "w_decay_f": ((decay_cols(DECAY_W_ROW), 0, LOWRANK), w_decay_f, m_w_decay_f, v_w_decay_f),
        "b_decay_f": ((s256, 0, 1), b_decay_f, m_b_decay_f, v_b_decay_f),
        "w_decay_b": ((decay_cols(DECAY_W_ROW + LOWRANK), 0, LOWRANK), w_decay_b, m_w_decay_b, v_w_decay_b),
        "b_decay_b": ((s256, 1, 1), b_decay_b, m_b_decay_b, v_b_decay_b),
        "gla_norm_g": ((s512, 0, 1), gla_norm_g, m_gla_norm_g, v_gla_norm_g),
        "gmlp_ln_g": ((s512, 1, 1), gmlp_ln_g, m_gmlp_ln_g, v_gmlp_ln_g),
        "gmlp_ln_b": ((s512, 2, 1), gmlp_ln_b, m_gmlp_ln_b, v_gmlp_ln_b),
        "w_spatial": ((s128, 0, GMLP_W), w_spatial, m_w_spatial, v_w_spatial),
        "b_spatial": ((s128, GMLP_W, GMLP_GROUPS), b_spatial, m_b_spatial, v_b_spatial),
        "norm2_g": ((s1024, 1, 1), norm2_g, m_norm2_g, v_norm2_g),
        "final_norm_g": ((s1024, 2, 1), final_norm_g, m_final_norm_g, v_final_norm_g),
    }
    small_res = _adamw_small([(g, flat(w), flat(m), flat(v)) for g, w, m, v in small.values()])
    small_out = {n: [r.reshape(small[n][1].shape) for r in res] for n, res in zip(small, small_res)}

    order = ["norm1_g", "w_in", "w_decay_f", "b_decay_f", "w_decay_b", "b_decay_b", "gla_norm_g", "gmlp_ln_g",
             "gmlp_ln_b", "w_spatial", "b_spatial", "w_out", "norm2_g", "w_gate", "w_up", "w_down", "final_norm_g"]
    outs = []
    for kind in range(4):
        for n in order:
            outs.append(big_out[n][kind][None] if n in big_out else small_out[n][kind])
    return (loss, grad_x[None], *outs)
```

```python
import functools
import math

import jax
import jax.numpy as jnp
from jax import lax
from jax.experimental import pallas as pl
from jax.experimental.pallas import tpu as pltpu

F32 = jnp.float32
BF16 = jnp.bfloat16

D_MODEL = 1024
GLA_HEADS = 4
GLA_DK = 64
GLA_DV = 128
KEY_W = GLA_HEADS * GLA_DK
VAL_W = GLA_HEADS * GLA_DV
LOWRANK = 16
GLA_TAU = 16.0
GLA_CHUNK = 64
GMLP_W = 512
GMLP_GROUPS = 4
GMLP_CHUNK = 128
D_FF = 2816
EPS = 1e-6
Q_SCALE = GLA_DK ** -0.5
PROJ_PAD = 2688
LR_COL = 2560
LANE = 128
N_DEV = 8

ADAM_LR = 0.001
ADAM_B1 = 0.9
ADAM_B2 = 0.999
ADAM_EPS = 1e-08
ADAM_WD = 0.01
ADAM_STEP = 10

VMEM_LIMIT = 56 * 1024 * 1024
TOKEN_TILE = {"norm1": 512, "in_proj": 512, "gla": 1024, "mix_fwd": 1024, "ffn": 256, "mix_bwd": 512,
              "in_proj_bwd": 512, "dw": 2048, "dw_host": 1024}
DECAY_W_ROW = 8
MESH_ID = pl.DeviceIdType.MESH
INV_SQRT2 = 0.7071067811865476
INV_SQRT_2PI = 0.3989422804014327


def _params(n_axes=1):
    return pltpu.CompilerParams(dimension_semantics=("arbitrary",) * n_axes, vmem_limit_bytes=VMEM_LIMIT)


def _mm(a, b):
    return jnp.dot(a.astype(BF16), b.astype(BF16), preferred_element_type=F32)


def _mm_nt(a, b):
    return lax.dot_general(a.astype(BF16), b.astype(BF16), (((1,), (1,)), ((), ())), preferred_element_type=F32)


def _mm_tn(a, b):
    return lax.dot_general(a.astype(BF16), b.astype(BF16), (((0,), (0,)), ((), ())), preferred_element_type=F32)


def _const_spec(shape):
    nd = len(shape)
    return pl.BlockSpec(shape, lambda *_: (0,) * nd, pipeline_mode=pl.Buffered(1))


def _acc_spec(shape):
    nd = len(shape)
    return pl.BlockSpec(shape, lambda *_: (0,) * nd)


class _Comm:
    def __init__(self, inputs, in_specs, out_shape, out_specs, scratch_shapes, before, after):
        self.inputs, self.in_specs, self.out_shape, self.out_specs = inputs, in_specs, out_shape, out_specs
        self.scratch_shapes, self.before, self.after = scratch_shapes, before, after


def _fused_call(body, comms, *, name, grid, inputs, in_specs, out_specs, out_shape, scratch_shapes=(), prefetch=()):
    n_pre, n_in, n_out, n_scr = len(prefetch), len(in_specs), len(out_specs), len(scratch_shapes)
    nsteps = math.prod(grid)
    sizes = [(len(c.inputs), len(c.out_shape), len(c.scratch_shapes)) for c in comms]

    def full_body(*refs):
        step = pl.program_id(0)
        for axis in range(1, len(grid)):
            step = step * grid[axis] + pl.program_id(axis)
        pre, refs = refs[:n_pre], refs[n_pre:]
        ins, rest = refs[:n_in], refs[n_in:]
        c_ins = []
        for ci, _, _ in sizes:
            c_ins.append(rest[:ci])
            rest = rest[ci:]
        outs, rest = rest[:n_out], rest[n_out:]
        c_outs = []
        for _, co, _ in sizes:
            c_outs.append(rest[:co])
            rest = rest[co:]
        scr, rest = rest[:n_scr], rest[n_scr:]
        c_scr = []
        for _, _, cs in sizes:
            c_scr.append(rest[:cs])
            rest = rest[cs:]
        for c, a, b, s in zip(comms, c_ins, c_outs, c_scr):
            c.before(step, nsteps, a, b, s)
        body(*pre, *ins, *outs, *scr)
        for c, a, b, s in zip(comms, c_ins, c_outs, c_scr):
            c.after(step, nsteps, a, b, s)

    specs = dict(
        grid=grid, in_specs=list(in_specs) + [s for c in comms for s in c.in_specs],
        out_specs=tuple(out_specs) + tuple(s for c in comms for s in c.out_specs),
        scratch_shapes=list(scratch_shapes) + [s for c in comms for s in c.scratch_shapes])
    if n_pre:
        specs = dict(grid_spec=pltpu.PrefetchScalarGridSpec(num_scalar_prefetch=n_pre, **specs))
    results = pl.pallas_call(
        full_body, name=name, **specs,
        out_shape=tuple(out_shape) + tuple(s for c in comms for s in c.out_shape),
        compiler_params=_params(len(grid)),
    )(*prefetch, *inputs, *[a for c in comms for a in c.inputs])
    own, rest = results[:n_out], results[n_out:]
    comm_results = []
    for _, co, _ in sizes:
        comm_results.append(rest[:co])
        rest = rest[co:]
    return own, comm_results


def _gelu(x):
    return 0.5 * x * (1.0 + lax.erf(x * INV_SQRT2))


def _gelu_and_grad(x):
    cdf = 0.5 * (1.0 + lax.erf(x * INV_SQRT2))
    return x * cdf, cdf + x * jnp.exp(-0.5 * x * x) * INV_SQRT_2PI


def _sigmoid(x):
    return 0.5 + 0.5 * jnp.tanh(0.5 * x)


def _silu_and_grad(x):
    s = _sigmoid(x)
    return x * s, s * (1.0 + x * (1.0 - s))


def _norm1(x, g1, tm, comms=()):
    t = x.shape[0]

    def body(x_ref, g_ref, h_ref):
        xv = x_ref[...]
        r = lax.rsqrt(jnp.mean(xv * xv, axis=-1, keepdims=True) + EPS)
        h_ref[...] = (xv * r * g_ref[...]).astype(BF16)

    row = pl.BlockSpec((tm, D_MODEL), lambda i: (i, 0))
    return _fused_call(body, comms, name="norm1", grid=(t // tm,), inputs=(x, g1),
                       in_specs=[row, _const_spec((1, D_MODEL))], out_specs=(row,),
                       out_shape=(jax.ShapeDtypeStruct((t, D_MODEL), BF16),))


PROJ_W = 2592
LR_REF = 1536
PROJ_ROWS = ((0, LR_REF, 0), (LR_REF + 2 * LOWRANK, PROJ_W, LR_REF), (LR_REF, LR_REF + LANE, LR_COL))


def _in_proj(h, w_in_t, tm, comms=()):
    t = h.shape[0]

    def body(h_ref, w_ref, p_ref):
        hv = h_ref[...]
        for r0, r1, c0 in PROJ_ROWS:
            p_ref[:, c0:c0 + r1 - r0] = _mm_nt(hv, w_ref[r0:r1, :]).astype(BF16)

    return _fused_call(
        body, comms, name="in_proj", grid=(t // tm,), inputs=(h, w_in_t),
        in_specs=[pl.BlockSpec((tm, D_MODEL), lambda i: (i, 0)), _const_spec((PROJ_W, D_MODEL))],
        out_specs=(pl.BlockSpec((tm, PROJ_PAD), lambda i: (i, 0)),),
        out_shape=(jax.ShapeDtypeStruct((t, PROJ_PAD), BF16),))


def _tri(upper):
    r = lax.broadcasted_iota(jnp.int32, (GLA_CHUNK, GLA_CHUNK), 0)
    c = lax.broadcasted_iota(jnp.int32, (GLA_CHUNK, GLA_CHUNK), 1)
    return jnp.where((c >= r) if upper else (c <= r), 1.0, 0.0).astype(BF16)


def _chunk_cumsum(tri, a, add=None):
    hi = a.astype(BF16)
    lo = (a - hi.astype(F32)).astype(BF16)
    dot = functools.partial(jnp.dot, preferred_element_type=F32)
    sums = [dot(tri, hi[_chunk_rows(c)]) + dot(tri, lo[_chunk_rows(c)]) for c in range(a.shape[0] // GLA_CHUNK)]
    return jnp.concatenate(sums if add is None else [s + r for s, r in zip(sums, add)], axis=0)


def _chunk_rows(c):
    return slice(c * GLA_CHUNK, (c + 1) * GLA_CHUNK)


def _gla_masks(rev):
    dk_bits, dv_bits = GLA_DK.bit_length() - 1, GLA_DV.bit_length() - 1
    key_head = lax.broadcasted_iota(jnp.int32, (GLA_CHUNK, KEY_W), 1) >> dk_bits
    val_head = lax.broadcasted_iota(jnp.int32, (GLA_CHUNK, VAL_W), 1) >> dv_bits
    t = lax.broadcasted_iota(jnp.int32, (GLA_HEADS * GLA_CHUNK, GLA_CHUNK), 0) & (GLA_CHUNK - 1)
    s = lax.broadcasted_iota(jnp.int32, (GLA_HEADS * GLA_CHUNK, GLA_CHUNK), 1)
    return key_head, val_head, (s >= t) if rev else (s <= t)


def _stack_heads(a, head_of_lane):
    a = a.astype(BF16)
    return jnp.concatenate([jnp.where(head_of_lane == h, a, jnp.zeros_like(a)) for h in range(GLA_HEADS)], axis=0)


def _rows_by_head(a):
    return jnp.concatenate([a[:, h * GLA_DV:(h + 1) * GLA_DV] for h in range(GLA_HEADS)], axis=0)


def _lanes_by_head(r):
    return jnp.concatenate([r[h * GLA_CHUNK:(h + 1) * GLA_CHUNK] for h in range(GLA_HEADS)], axis=1)


def _head_diagonal(r, head_of_lane):
    rows = r.shape[0] // GLA_HEADS
    out = jnp.where(head_of_lane == 0, r[:rows], 0.0)
    for h in range(1, GLA_HEADS):
        out = out + jnp.where(head_of_lane == h, r[h * rows:(h + 1) * rows], 0.0)
    return out


def _tile_terms(la, q, k, tri, rev):
    nc = la.shape[0] // GLA_CHUNK
    q, k = q.astype(F32), k.astype(F32)
    b = _chunk_cumsum(tri, la)
    ebl = [jnp.exp(b[c * GLA_CHUNK:c * GLA_CHUNK + 1] if rev else b[(c + 1) * GLA_CHUNK - 1:(c + 1) * GLA_CHUNK])
           for c in range(nc)]
    eb = jnp.exp(b)
    enb = jnp.exp(-b)
    kd = k * enb
    ke = jnp.concatenate([kd[_chunk_rows(c)] * ebl[c] for c in range(nc)], axis=0)
    return ebl, eb, enb, q * Q_SCALE * eb, kd, ke


def _log_decay(lr_ref, wd_ref, bd_ref):
    z = _mm(lr_ref[...], wd_ref[...]) + bd_ref[...]
    return z, jax.nn.log_sigmoid(z) * (1.0 / GLA_TAU)


def _p_specs(tg, tile):
    return [pl.BlockSpec((tg, KEY_W), lambda i: (tile(i), 0)),
            pl.BlockSpec((tg, KEY_W), lambda i: (tile(i), 1)),
            pl.BlockSpec((tg, VAL_W), lambda i: (tile(i), 1)),
            pl.BlockSpec((tg, LANE), lambda i: (tile(i), LR_COL // LANE))]


def _gla_fwd_dir(rev, nc, q_ref, k_ref, v_ref, lr_ref, wd_ref, bd_ref, o_ref, st_ref, state):
    key_head, _, causal = _gla_masks(rev)
    order = range(nc - 1, -1, -1) if rev else range(nc)

    def intra():
        _, la = _log_decay(lr_ref, wd_ref, bd_ref)
        ebl, _, _, qd, kd, ke = _tile_terms(la, q_ref[...], k_ref[...], _tri(rev), rev)
        kd = kd.astype(BF16)
        v = {c: v_ref[_chunk_rows(c), :].astype(BF16) for c in order}
        qd_stack = {c: _stack_heads(qd[_chunk_rows(c)], key_head) for c in order}
        ke_stack = {c: _stack_heads(ke[_chunk_rows(c)], key_head) for c in order}
        a_all = {c: _mm_nt(qd_stack[c], kd[_chunk_rows(c)]) for c in order}
        a_all = {c: jnp.where(causal, a_all[c], 0.0).astype(BF16) for c in order}
        head_rows = lambda a, h: a[h * GLA_CHUNK:(h + 1) * GLA_CHUNK]
        head_vals = lambda a, h: a[:, h * GLA_DV:(h + 1) * GLA_DV]
        r = {c: [_mm(head_rows(a_all[c], h), head_vals(v[c], h)) for h in range(GLA_HEADS)] for c in order}
        upd = {c: _mm_tn(_rows_by_head(v[c]), ke_stack[c]) for c in order}
        return {c: (ebl[c], qd_stack[c], r[c], upd[c]) for c in order}

    def scan(terms):
        st = state[...]
        states = {}
        for c in order:
            states[c] = st
            st_ref[c] = st.astype(BF16)
            st = st * terms[c][0] + terms[c][3]
        state[...] = st
        return states

    def inter(terms, states):
        r_inter = {c: _mm_nt(terms[c][1], states[c]) for c in order}
        for c in order:
            o_ref[_chunk_rows(c), :] = jnp.concatenate(
                [terms[c][2][h] + r_inter[c][h * GLA_CHUNK:(h + 1) * GLA_CHUNK] for h in range(GLA_HEADS)], axis=1)

    return intra, scan, inter


def _gla_fwd(p, wd_pad_f, bd_f, wd_pad_b, bd_b, tg, comms=()):
    t = p.shape[0]
    nt = t // tg
    nc = tg // GLA_CHUNK
    up, down = (lambda i: i), (lambda i: nt - 1 - i)

    def body(qf, kf, vf, lrf, qb, kb, vb, lrb, wdf, bdf, wdb, bdb, of, stf, ob, stb, state_f, state_b):
        @pl.when(pl.program_id(0) == 0)
        def _():
            state_f[...] = jnp.zeros_like(state_f)
            state_b[...] = jnp.zeros_like(state_b)

        dirs = [_gla_fwd_dir(False, nc, qf, kf, vf, lrf, wdf, bdf, of, stf, state_f),
                _gla_fwd_dir(True, nc, qb, kb, vb, lrb, wdb, bdb, ob, stb, state_b)]
        terms = [intra() for intra, _, _ in dirs]
        states = [scan(t) for (_, scan, _), t in zip(dirs, terms)]
        for (_, _, inter), t, s in zip(dirs, terms, states):
            inter(t, s)

    wd_spec, bd_spec = _const_spec((LANE, KEY_W)), _const_spec((1, KEY_W))
    outs = lambda tile: (pl.BlockSpec((tg, VAL_W), lambda i: (tile(i), 0)),
                         pl.BlockSpec((nc, GLA_DV, KEY_W), lambda i: (tile(i), 0, 0)))
    out_shape = (jax.ShapeDtypeStruct((t, VAL_W), F32), jax.ShapeDtypeStruct((t // GLA_CHUNK, GLA_DV, KEY_W), BF16))
    return _fused_call(
        body, comms, name="gla_fwd", grid=(nt,), inputs=(p,) * 8 + (wd_pad_f, bd_f, wd_pad_b, bd_b),
        in_specs=_p_specs(tg, up) + _p_specs(tg, down) + [wd_spec, bd_spec, wd_spec, bd_spec],
        out_specs=outs(up) + outs(down), out_shape=out_shape * 2,
        scratch_shapes=[pltpu.VMEM((GLA_DV, KEY_W), F32)] * 2)


def _gla_bwd_dir(rev, nc, q_ref, k_ref, v_ref, lr_ref, wd_ref, bd_ref, st_ref, do_ref,
                 dq_ref, dk_ref, dv_ref, dlr_ref, dwd_ref, dbd_ref, dstate):
    key_head, val_head, causal = _gla_masks(rev)
    order = range(nc) if rev else range(nc - 1, -1, -1)

    def intra():
        z, la = _log_decay(lr_ref, wd_ref, bd_ref)
        tile = _tile_terms(la, q_ref[...], k_ref[...], _tri(rev), rev)
        qd, kd = tile[3], tile[4].astype(BF16)
        v = {c: v_ref[_chunk_rows(c), :].astype(BF16) for c in order}
        d_o = {c: do_ref[_chunk_rows(c), :] for c in order}
        kd_c = {c: kd[_chunk_rows(c)] for c in order}
        qd_stack = {c: _stack_heads(qd[_chunk_rows(c)], key_head) for c in order}
        do_stack = {c: _stack_heads(d_o[c], val_head) for c in order}
        do_rows = {c: _rows_by_head(d_o[c]) for c in order}
        a_all = {c: _mm_nt(qd_stack[c], kd_c[c]) for c in order}
        head_vals = lambda a, h: a[:, h * GLA_DV:(h + 1) * GLA_DV]
        da_all = {c: jnp.concatenate([_mm_nt(head_vals(d_o[c], h), head_vals(v[c], h)) for h in range(GLA_HEADS)],
                                     axis=0) for c in order}
        a_all = {c: jnp.where(causal, a_all[c], 0.0).astype(BF16) for c in order}
        da_all = {c: jnp.where(causal, da_all[c], 0.0).astype(BF16) for c in order}
        dv = {c: _mm_tn(a_all[c], do_stack[c]) for c in order}
        dqd = {c: _mm(jnp.concatenate([do_rows[c], da_all[c]], axis=1),
                      jnp.concatenate([st_ref[c], kd_c[c]], axis=0)) for c in order}
        dkd = {c: _mm_tn(da_all[c], qd_stack[c]) for c in order}
        upd = {c: _mm_tn(do_rows[c], qd_stack[c]) for c in order}
        dqd = {c: _head_diagonal(dqd[c], key_head) for c in order}
        return z, tile, {c: dict(dv=dv[c], dqd=dqd[c], dkd=dkd[c], upd=upd[c]) for c in order}

    def scan(tile, per):
        dst = dstate[...]
        dsts = {}
        for c in order:
            dsts[c] = dst
            dst = dst * tile[0][c] + per[c]["upd"]
        dstate[...] = dst
        return dsts

    def inter(z, tile, per, dsts):
        ebl, eb, enb, qd, kd, ke = tile
        ke_stack = {c: _stack_heads(ke[_chunk_rows(c)], key_head) for c in order}
        v_rows = {c: _rows_by_head(v_ref[_chunk_rows(c), :].astype(BF16)) for c in order}
        dst_b = {c: dsts[c].astype(BF16) for c in order}
        dv_state = {c: _mm_nt(ke_stack[c], dst_b[c]) for c in order}
        dke_c = {c: _mm(v_rows[c], dst_b[c]) for c in order}
        dke_c = {c: _head_diagonal(dke_c[c], key_head) for c in order}
        dbl_c = {}
        for c in order:
            rows = _chunk_rows(c)
            dv_ref[rows, :] = (per[c]["dv"] + _lanes_by_head(dv_state[c])).astype(BF16)
            dbl_c[c] = (jnp.sum(dsts[c] * st_ref[c].astype(F32), axis=0, keepdims=True) * ebl[c]
                        + jnp.sum(dke_c[c] * ke[rows], axis=0, keepdims=True))
        tile_of = lambda parts: jnp.concatenate([parts[c] for c in range(nc)], axis=0)
        dqd, dkd = tile_of({c: per[c]["dqd"] for c in order}), tile_of({c: per[c]["dkd"] for c in order})
        dke = tile_of(dke_c)
        dke_end = tile_of({c: dke_c[c] * ebl[c] for c in order})
        dq_ref[...] = (dqd * eb * Q_SCALE).astype(BF16)
        dk_ref[...] = ((dkd + dke_end) * enb).astype(BF16)
        db = dqd * qd - dkd * kd - dke * ke
        dla = _chunk_cumsum(_tri(not rev), db, [dbl_c[c] for c in range(nc)])
        dz = dla * (_sigmoid(-z) * (1.0 / GLA_TAU))
        dlr_ref[...] = _mm_nt(dz, wd_ref[...]).astype(BF16)
        dwd_ref[...] += _mm_tn(lr_ref[...], dz)
        dbd_ref[...] += jnp.sum(dz, axis=0, keepdims=True)

    return intra, scan, inter


def _gla_bwd(p, wd_pad_f, bd_f, wd_pad_b, bd_b, st_f, st_b, d_o, tg, comms=()):
    t = p.shape[0]
    nt = t // tg
    nc = tg // GLA_CHUNK
    up, down = (lambda i: i), (lambda i: nt - 1 - i)

    def body(qf, kf, vf, lrf, stf, dof, qb, kb, vb, lrb, stb, dob, wdf, bdf, wdb, bdb,
             dqf, dkf, dvf, dlrf, dwdf, dbdf, dqb, dkb, dvb, dlrb, dwdb, dbdb, dstate_f, dstate_b):
        @pl.when(pl.program_id(0) == 0)
        def _():
            for ref in (dstate_f, dstate_b, dwdf, dbdf, dwdb, dbdb):
                ref[...] = jnp.zeros_like(ref)

        dirs = [_gla_bwd_dir(False, nc, qf, kf, vf, lrf, wdf, bdf, stf, dof, dqf, dkf, dvf, dlrf, dwdf, dbdf,
                             dstate_f),
                _gla_bwd_dir(True, nc, qb, kb, vb, lrb, wdb, bdb, stb, dob, dqb, dkb, dvb, dlrb, dwdb, dbdb,
                             dstate_b)]
        first = [intra() for intra, _, _ in dirs]
        dsts = [scan(tile, per) for (_, scan, _), (_, tile, per) in zip(dirs, first)]
        for (_, _, inter), (z, tile, per), d in zip(dirs, first, dsts):
            inter(z, tile, per, d)

    wd_spec, bd_spec = _const_spec((LANE, KEY_W)), _const_spec((1, KEY_W))
    ins = lambda tile: _p_specs(tg, tile) + [pl.BlockSpec((nc, GLA_DV, KEY_W), lambda i: (tile(i), 0, 0)),
                                             pl.BlockSpec((tg, VAL_W), lambda i: (tile(i), 0))]
    outs = lambda tile: (pl.BlockSpec((tg, KEY_W), lambda i: (tile(i), 0)),
                         pl.BlockSpec((tg, KEY_W), lambda i: (tile(i), 0)),
                         pl.BlockSpec((tg, VAL_W), lambda i: (tile(i), 0)),
                         pl.BlockSpec((tg, LANE), lambda i: (tile(i), 0)),
                         _acc_spec((LANE, KEY_W)), _acc_spec((1, KEY_W)))
    out_shape = (jax.ShapeDtypeStruct((t, KEY_W), BF16), jax.ShapeDtypeStruct((t, KEY_W), BF16),
                 jax.ShapeDtypeStruct((t, VAL_W), BF16), jax.ShapeDtypeStruct((t, LANE), BF16),
                 jax.ShapeDtypeStruct((LANE, KEY_W), F32), jax.ShapeDtypeStruct((1, KEY_W), F32))
    scratch = [pltpu.VMEM((GLA_DV, KEY_W), F32)]
    return _fused_call(
        body, comms, name="gla_bwd", grid=(nt,),
        inputs=(p, p, p, p, st_f, d_o, p, p, p, p, st_b, d_o, wd_pad_f, bd_f, wd_pad_b, bd_b),
        in_specs=ins(down) + ins(up) + [wd_spec, bd_spec, wd_spec, bd_spec],
        out_specs=outs(down) + outs(up), out_shape=out_shape * 2, scratch_shapes=scratch * 2)


def _head_rms(o):
    parts, scales = [], []
    for h in range(GLA_HEADS):
        oh = o[:, h * GLA_DV:(h + 1) * GLA_DV]
        r = lax.rsqrt(jnp.mean(oh * oh, axis=-1, keepdims=True) + EPS)
        parts.append(oh * r)
        scales.append(jnp.broadcast_to(r, oh.shape))
    return jnp.concatenate(parts, axis=1), jnp.concatenate(scales, axis=1)


def _layernorm_stats(zv):
    mu = jnp.mean(zv, axis=-1, keepdims=True)
    xc = zv - mu
    rs = lax.rsqrt(jnp.mean(xc * xc, axis=-1, keepdims=True) + EPS)
    return xc * rs, rs


def _mix_fwd(x, o_f, o_b, p, gla_g, ln_g, ln_b, w_sp, b_sp, w_out, tm, comms=()):
    t = x.shape[0]
    nch = tm // GMLP_CHUNK

    def body(x_ref, of_ref, ob_ref, pg_ref, pu_ref, pv_ref, gg_ref, lg_ref, lb_ref, ws_ref, bs_ref, wo_ref,
             x1_ref, y_ref, s_scr):
        on, _ = _head_rms(of_ref[...] + ob_ref[...])
        pg = pg_ref[...].astype(F32)
        y_a = on * gg_ref[...] * (pg * _sigmoid(pg))
        zu = _gelu(pu_ref[...].astype(F32))
        vhat, _ = _layernorm_stats(_gelu(pv_ref[...].astype(F32)))
        vln = (vhat * lg_ref[...] + lb_ref[...]).astype(BF16)
        for g in range(GMLP_GROUPS):
            w_g = ws_ref[g].astype(BF16)
            b_g = bs_ref[g]
            cols = slice(g * LANE, (g + 1) * LANE)
            for n in range(nch):
                rows = slice(n * GMLP_CHUNK, (n + 1) * GMLP_CHUNK)
                s_scr[rows, cols] = jnp.dot(w_g, vln[rows, cols], preferred_element_type=F32) + b_g
        ycat = jnp.concatenate([y_a, zu * s_scr[...]], axis=1).astype(BF16)
        y_ref[...] = ycat
        x1_ref[...] = x_ref[...] + jnp.dot(ycat, wo_ref[...], preferred_element_type=F32)

    half = lambda j: pl.BlockSpec((tm, VAL_W), lambda i: (i, j))
    return _fused_call(
        body, comms, name="mix_fwd", grid=(t // tm,),
        inputs=(x, o_f, o_b, p, p, p, gla_g, ln_g, ln_b, w_sp, b_sp, w_out),
        in_specs=[pl.BlockSpec((tm, D_MODEL), lambda i: (i, 0)), half(0), half(0), half(2), half(3), half(4),
                  _const_spec((1, VAL_W)), _const_spec((1, GMLP_W)), _const_spec((1, GMLP_W)),
                  _const_spec((GMLP_GROUPS, GMLP_CHUNK, GMLP_CHUNK)), _const_spec((GMLP_GROUPS, GMLP_CHUNK, 1)),
                  _const_spec((D_MODEL, D_MODEL))],
        out_specs=(pl.BlockSpec((tm, D_MODEL), lambda i: (i, 0)), pl.BlockSpec((tm, D_MODEL), lambda i: (i, 0))),
        out_shape=(jax.ShapeDtypeStruct((t, D_MODEL), F32), jax.ShapeDtypeStruct((t, D_MODEL), BF16)),
        scratch_shapes=[pltpu.VMEM((tm, GMLP_W), F32)])


def _mix_bwd(dx1, ycat, o_f, o_b, p, gla_g, ln_g, ln_b, w_sp, b_sp, w_out, tm, comms=()):
    t = dx1.shape[0]
    nch = tm // GMLP_CHUNK

    def body(dx1_ref, y_ref, of_ref, ob_ref, pg_ref, pu_ref, pv_ref, gg_ref, lg_ref, lb_ref, ws_ref, bs_ref, wo_ref,
             do_ref, dpg_ref, dpu_ref, dpv_ref, dwo_ref, dgg_ref, dlg_ref, dlb_ref, dws_ref, dbs_ref,
             s_scr, dvln_scr):
        @pl.when(pl.program_id(0) == 0)
        def _():
            for ref in (dwo_ref, dgg_ref, dlg_ref, dlb_ref, dws_ref, dbs_ref):
                ref[...] = jnp.zeros_like(ref)

        dx1 = dx1_ref[...].astype(BF16)
        dycat = _mm_nt(dx1, wo_ref[...])
        dwo_ref[...] += _mm_tn(y_ref[...], dx1)
        dy_a = dycat[:, :VAL_W]
        dy_b = dycat[:, VAL_W:]
        on, r = _head_rms(of_ref[...] + ob_ref[...])
        pg = pg_ref[...].astype(F32)
        sil, dsil = _silu_and_grad(pg)
        gg = gg_ref[...]
        dgg_ref[...] += jnp.sum(dy_a * sil * on, axis=0, keepdims=True)
        don = dy_a * sil * gg
        prod = don * on
        means = jnp.concatenate(
            [jnp.broadcast_to(jnp.mean(prod[:, h * GLA_DV:(h + 1) * GLA_DV], axis=-1, keepdims=True),
                              (tm, GLA_DV)) for h in range(GLA_HEADS)], axis=1)
        do_ref[...] = (r * (don - on * means)).astype(BF16)
        dpg_ref[...] = (dy_a * on * gg * dsil).astype(BF16)
        pu = pu_ref[...].astype(F32)
        pv = pv_ref[...].astype(F32)
        zu, dzu_dpu = _gelu_and_grad(pu)
        zv, dzv_dpv = _gelu_and_grad(pv)
        vhat, rs = _layernorm_stats(zv)
        lg = lg_ref[...]
        vln = (vhat * lg + lb_ref[...]).astype(BF16)
        ds32 = dy_b * zu
        ds = ds32.astype(BF16)
        blocks = [(g, n) for g in range(GMLP_GROUPS) for n in range(nch)]
        at = lambda g, n: (slice(n * GMLP_CHUNK, (n + 1) * GMLP_CHUNK), slice(g * LANE, (g + 1) * LANE))
        w_sp = [ws_ref[g].astype(BF16) for g in range(GMLP_GROUPS)]
        v_blk = {b: vln[at(*b)] for b in blocks}
        ds_blk = {b: ds[at(*b)] for b in blocks}
        s_blk = {b: jnp.dot(w_sp[b[0]], v_blk[b], preferred_element_type=F32) for b in blocks}
        dw_blk = {b: _mm_nt(ds_blk[b], v_blk[b]) for b in blocks}
        dvln_blk = {b: _mm_tn(w_sp[b[0]], ds_blk[b]) for b in blocks}
        for b in blocks:
            s_scr[at(*b)] = s_blk[b] + bs_ref[b[0]]
            dvln_scr[at(*b)] = dvln_blk[b]
        for g in range(GMLP_GROUPS):
            dws_ref[g] += sum(dw_blk[(g, n)] for n in range(nch))
            dbs_ref[g] += sum(jnp.sum(ds32[at(g, n)], axis=-1, keepdims=True) for n in range(nch))
        dpu_ref[...] = (dy_b * s_scr[...] * dzu_dpu).astype(BF16)
        dvln = dvln_scr[...]
        dlg_ref[...] += jnp.sum(dvln * vhat, axis=0, keepdims=True)
        dlb_ref[...] += jnp.sum(dvln, axis=0, keepdims=True)
        dvhat = dvln * lg
        dzv = rs * (dvhat - jnp.mean(dvhat, axis=-1, keepdims=True)
                    - vhat * jnp.mean(dvhat * vhat, axis=-1, keepdims=True))
        dpv_ref[...] = (dzv * dzv_dpv).astype(BF16)

    half = lambda j: pl.BlockSpec((tm, VAL_W), lambda i: (i, j))
    full = pl.BlockSpec((tm, D_MODEL), lambda i: (i, 0))
    sp_shape = (GMLP_GROUPS, GMLP_CHUNK, GMLP_CHUNK)
    bs_shape = (GMLP_GROUPS, GMLP_CHUNK, 1)
    return _fused_call(
        body, comms, name="mix_bwd", grid=(t // tm,),
        inputs=(dx1, ycat, o_f, o_b, p, p, p, gla_g, ln_g, ln_b, w_sp, b_sp, w_out),
        in_specs=[full, full, half(0), half(0), half(2), half(3), half(4),
                  _const_spec((1, VAL_W)), _const_spec((1, GMLP_W)), _const_spec((1, GMLP_W)),
                  _const_spec(sp_shape), _const_spec(bs_shape), _const_spec((D_MODEL, D_MODEL))],
        out_specs=(half(0), half(0), half(0), half(0), _acc_spec((D_MODEL, D_MODEL)), _acc_spec((1, VAL_W)),
                   _acc_spec((1, GMLP_W)), _acc_spec((1, GMLP_W)), _acc_spec(sp_shape), _acc_spec(bs_shape)),
        out_shape=(jax.ShapeDtypeStruct((t, VAL_W), BF16),) * 4 + (
            jax.ShapeDtypeStruct((D_MODEL, D_MODEL), F32), jax.ShapeDtypeStruct((1, VAL_W), F32),
            jax.ShapeDtypeStruct((1, GMLP_W), F32), jax.ShapeDtypeStruct((1, GMLP_W), F32),
            jax.ShapeDtypeStruct(sp_shape, F32), jax.ShapeDtypeStruct(bs_shape, F32)),
        scratch_shapes=[pltpu.VMEM((tm, GMLP_W), F32), pltpu.VMEM((tm, GMLP_W), F32)])


def _rms_bwd(dy_scaled, xn, r):
    return r * (dy_scaled - xn * jnp.mean(dy_scaled * xn, axis=-1, keepdims=True))


def _ffn(x1, target, g2, gf, w_gate, w_up, w_down, tm):
    t = x1.shape[0]

    def body(x1_ref, tg_ref, g2_ref, gf_ref, wg_ref, wu_ref, wd_ref,
             dx1_ref, h2_ref, dgate_ref, dup_ref, act_ref, dx2_ref, loss_ref, dgf_ref, dg2_ref):
        @pl.when(pl.program_id(0) == 0)
        def _():
            for ref in (loss_ref, dgf_ref, dg2_ref):
                ref[...] = jnp.zeros_like(ref)

        x1v = x1_ref[...]
        g2v = g2_ref[...]
        gfv = gf_ref[...]
        r2 = lax.rsqrt(jnp.mean(x1v * x1v, axis=-1, keepdims=True) + EPS)
        xn1 = x1v * r2
        h2 = (xn1 * g2v).astype(BF16)
        h2_ref[...] = h2
        gate = _mm_nt(h2, wg_ref[...])
        up = _mm_nt(h2, wu_ref[...])
        sil, dsil = _silu_and_grad(gate)
        act = (sil * up).astype(BF16)
        act_ref[...] = act
        x2 = x1v + jnp.dot(act, wd_ref[...], preferred_element_type=F32)
        rf = lax.rsqrt(jnp.mean(x2 * x2, axis=-1, keepdims=True) + EPS)
        xn2 = x2 * rf
        err = xn2 * gfv - tg_ref[...]
        loss_ref[...] += 0.5 * jnp.sum(jnp.mean(err * err, axis=-1, keepdims=True))
        dy = err * (1.0 / D_MODEL)
        dgf_ref[...] += jnp.sum(dy * xn2, axis=0, keepdims=True)
        dx2 = _rms_bwd(dy * gfv, xn2, rf)
        dx2b = dx2.astype(BF16)
        dx2_ref[...] = dx2b
        dact = _mm_nt(dx2b, wd_ref[...])
        dgate = (dact * up * dsil).astype(BF16)
        dup = (dact * sil).astype(BF16)
        dgate_ref[...] = dgate
        dup_ref[...] = dup
        dh2 = _mm(dgate, wg_ref[...]) + _mm(dup, wu_ref[...])
        dg2_ref[...] += jnp.sum(dh2 * xn1, axis=0, keepdims=True)
        dx1_ref[...] = dx2 + _rms_bwd(dh2 * g2v, xn1, r2)

    row = lambda w: pl.BlockSpec((tm, w), lambda i: (i, 0))
    return pl.pallas_call(
        body, name="ffn_fwd_bwd", grid=(t // tm,),
        in_specs=[row(D_MODEL), row(D_MODEL), _const_spec((1, D_MODEL)), _const_spec((1, D_MODEL)),
                  _const_spec((D_FF, D_MODEL)), _const_spec((D_FF, D_MODEL)), _const_spec((D_FF, D_MODEL))],
        out_specs=(row(D_MODEL), row(D_MODEL), row(D_FF), row(D_FF), row(D_FF), row(D_MODEL),
                   _acc_spec((8, LANE)), _acc_spec((1, D_MODEL)), _acc_spec((1, D_MODEL))),
        out_shape=(jax.ShapeDtypeStruct((t, D_MODEL), F32), jax.ShapeDtypeStruct((t, D_MODEL), BF16),
                   jax.ShapeDtypeStruct((t, D_FF), BF16), jax.ShapeDtypeStruct((t, D_FF), BF16),
                   jax.ShapeDtypeStruct((t, D_FF), BF16), jax.ShapeDtypeStruct((t, D_MODEL), BF16),
                   jax.ShapeDtypeStruct((8, LANE), F32), jax.ShapeDtypeStruct((1, D_MODEL), F32),
                   jax.ShapeDtypeStruct((1, D_MODEL), F32)),
        compiler_params=_params(),
    )(x1, target, g2, gf, w_gate, w_up, w_down)


def _matmul_tn(a, b, tm, tk, name, comms=(), cols=None):
    t, m = a.shape
    cb, n = (0, b.shape[1]) if cols is None else cols

    def body(a_ref, b_ref, o_ref):
        @pl.when(pl.program_id(1) == 0)
        def _():
            o_ref[...] = jnp.zeros_like(o_ref)

        o_ref[...] += _mm_tn(a_ref[...], b_ref[...])

    (out,), comm_results = _fused_call(
        body, comms, name=name, grid=(m // tm, t // tk), inputs=(a, b),
        in_specs=[pl.BlockSpec((tk, tm), lambda j, k: (k, j)), pl.BlockSpec((tk, n), lambda j, k: (k, cb))],
        out_specs=(pl.BlockSpec((tm, n), lambda j, k: (j, 0)),),
        out_shape=(jax.ShapeDtypeStruct((m, n), F32),))
    return out, comm_results


def _in_proj_bwd(x, g1, dx1, dq_f, dq_b, dk_f, dk_b, dv_f, dv_b, dpg, dpu, dpv, dlr_f, dlr_b, w_main, tm, comms=()):
    t = x.shape[0]

    def body(x_ref, g_ref, dx1_ref, dqf, dqb, dkf, dkb, dvf, dvb, dg, du, dv, dlf, dlb, w_ref,
             dx_ref, dp_ref, dg1_ref):
        @pl.when(pl.program_id(0) == 0)
        def _():
            dg1_ref[...] = jnp.zeros_like(dg1_ref)

        both = lambda a, b: (a[...].astype(F32) + b[...].astype(F32)).astype(BF16)
        dp = jnp.concatenate([both(dqf, dqb), both(dkf, dkb), both(dvf, dvb), dg[...], du[...], dv[...],
                              both(dlf, dlb)], axis=1)
        dp_ref[...] = dp
        dh = sum(_mm(dp[:, c0:c0 + r1 - r0], w_ref[r0:r1, :]) for r0, r1, c0 in PROJ_ROWS)
        xv = x_ref[...]
        r = lax.rsqrt(jnp.mean(xv * xv, axis=-1, keepdims=True) + EPS)
        xn = xv * r
        dg1_ref[...] += jnp.sum(dh * xn, axis=0, keepdims=True)
        dx_ref[...] = dx1_ref[...] + _rms_bwd(dh * g_ref[...], xn, r)

    row = lambda w: pl.BlockSpec((tm, w), lambda i: (i, 0))
    return _fused_call(
        body, comms, name="in_proj_bwd", grid=(t // tm,),
        inputs=(x, g1, dx1, dq_f, dq_b, dk_f, dk_b, dv_f, dv_b, dpg, dpu, dpv, dlr_f, dlr_b, w_main),
        in_specs=[row(D_MODEL), _const_spec((1, D_MODEL)), row(D_MODEL), row(KEY_W), row(KEY_W), row(KEY_W),
                  row(KEY_W), row(VAL_W), row(VAL_W), row(VAL_W), row(VAL_W), row(VAL_W), row(LANE), row(LANE),
                  _const_spec((PROJ_W, D_MODEL))],
        out_specs=(row(D_MODEL), row(PROJ_PAD), _acc_spec((1, D_MODEL))),
        out_shape=(jax.ShapeDtypeStruct((t, D_MODEL), F32), jax.ShapeDtypeStruct((t, PROJ_PAD), BF16),
                   jax.ShapeDtypeStruct((1, D_MODEL), F32)))


def _adamw(w, g, m, v):
    m_new = ADAM_B1 * m + (1.0 - ADAM_B1) * g
    v_new = ADAM_B2 * v + (1.0 - ADAM_B2) * (g * g)
    m_hat = m_new / (1.0 - ADAM_B1 ** ADAM_STEP)
    v_hat = v_new / (1.0 - ADAM_B2 ** ADAM_STEP)
    delta = -ADAM_LR * (m_hat / (jnp.sqrt(v_hat) + ADAM_EPS) + ADAM_WD * w)
    return delta, m_new, v_new


def _adamw_shard(own, recv, w, m, v, tr, name):
    r, c = w.shape

    def body(own_ref, recv_ref, w_ref, m_ref, v_ref, g_ref, d_ref, nm_ref, nv_ref):
        g = own_ref[...]
        for k in range(3):
            g = g + recv_ref[k].astype(F32)
        g_ref[...] = g
        d_ref[...], nm_ref[...], nv_ref[...] = _adamw(w_ref[...], g, m_ref[...], v_ref[...])

    row = pl.BlockSpec((tr, c), lambda i: (i, 0))
    return pl.pallas_call(
        body, name=name, grid=(r // tr,),
        in_specs=[row, pl.BlockSpec((3, tr, c), lambda i: (0, i, 0)), row, row, row],
        out_specs=(row,) * 4, out_shape=(jax.ShapeDtypeStruct((r, c), F32),) * 4,
        compiler_params=_params(),
    )(own, recv, w, m, v)


def _adamw_small(entries):
    stacks = []
    for (g, _, _), _, _, _ in entries:
        if not any(g is s for s in stacks):
            stacks.append(g)
    where = [next(i for i, s in enumerate(stacks) if s is g) for (g, _, _), _, _, _ in entries]
    ns, ne = len(stacks), len(entries)

    def body(*refs):
        s_refs, wmv, outs = refs[:ns], refs[ns:ns + 3 * ne], refs[ns + 3 * ne:]
        for e, ((_, r0, nr), _, _, _) in enumerate(entries):
            grad = s_refs[where[e]][r0:r0 + nr, :]
            w_ref, m_ref, v_ref = wmv[3 * e:3 * e + 3]
            g_ref, d_ref, nm_ref, nv_ref = outs[4 * e:4 * e + 4]
            g_ref[...] = grad
            d_ref[...], nm_ref[...], nv_ref[...] = _adamw(w_ref[...], grad, m_ref[...], v_ref[...])

    results = pl.pallas_call(
        body, name="adamw_small",
        out_shape=tuple(jax.ShapeDtypeStruct(w.shape, F32) for _, w, _, _ in entries for _ in range(4)),
        compiler_params=pltpu.CompilerParams(vmem_limit_bytes=VMEM_LIMIT),
    )(*stacks, *[a for _, w, m, v in entries for a in (w, m, v)])
    return [results[4 * e:4 * e + 4] for e in range(ne)]


def _mesh_pos():
    return lax.axis_index("x"), lax.axis_index("y"), lax.axis_index("c")


def _other_chips(x, y):
    return [(x, 1 - y), (1 - x, y), (1 - x, 1 - y)]


_VMEM_WHOLE = pl.BlockSpec(memory_space=pltpu.VMEM)
_HBM_WHOLE = pl.BlockSpec(memory_space=pl.ANY)


def _gather_comm(shards, cast, mid=((1, 2), (3, 4))):
    na = len(shards)
    staged = [a for a in range(na) if cast[a]]

    def phases(in_refs, out_refs, scr):
        stage = dict(zip(staged, scr[:len(staged)]))
        send_sems, recv_sems, local_sems = scr[len(staged):]
        x, y, c = _mesh_pos()
        me, sibling = (x, y, c), (x, y, 1 - c)
        chip_a, chip_b, diagonal = (x ^ c, y ^ (1 - c)), (x ^ (1 - c), y ^ c), (1 - x, 1 - y)
        srcs = [stage[a] if cast[a] else in_refs[a] for a in range(na)]

        def rows(a, pos):
            px, py, pc = pos
            return out_refs[a].at[4 * px + 2 * py + pc]

        def copy(a, k, block, to, src=None):
            return pltpu.make_async_remote_copy(
                src_ref=rows(a, block) if src is None else src, dst_ref=rows(a, block),
                send_sem=send_sems.at[a, k], recv_sem=recv_sems.at[a, k], device_id=to, device_id_type=MESH_ID)

        mine = [pltpu.make_async_copy(srcs[a], rows(a, me), local_sems.at[a]) for a in range(na)]
        own = [copy(a, k, me, to, src=srcs[a]) for a in range(na)
               for k, to in ((0, sibling), (1, (*chip_a, c)), (2, (*chip_b, c)))]
        onward = [copy(a, 3, (*chip_a, c), (*chip_b, c)) for a in range(na)]
        to_sibling = {k: [copy(a, k, (*chip, c), sibling) for a in range(na)]
                      for k, chip in ((4, chip_a), (5, chip_b), (6, diagonal))}

        def start():
            for a in staged:
                stage[a][...] = in_refs[a][...].astype(BF16)
            for cp in mine + own:
                cp.start()

        def forward_neighbours():
            for a in range(na):
                copy(a, 1, (*chip_a, c), me).wait_recv()
                onward[a].start()
                to_sibling[4][a].start()
            for a in range(na):
                copy(a, 2, (*chip_b, c), me).wait_recv()
                to_sibling[5][a].start()

        def forward_diagonal():
            for a in range(na):
                copy(a, 3, (*diagonal, c), me).wait_recv()
                to_sibling[6][a].start()

        def finish():
            for a in range(na):
                for k, chip in ((0, (x, y)), (4, chip_b), (5, chip_a), (6, diagonal)):
                    copy(a, k, (*chip, 1 - c), me).wait_recv()
            for cp in own + onward + to_sibling[4] + to_sibling[5] + to_sibling[6]:
                cp.wait_send()
            for cp in mine:
                cp.wait()

        return start, forward_neighbours, forward_diagonal, finish

    def before(step, nsteps, in_refs, out_refs, scr):
        start, forward_neighbours, forward_diagonal, _ = phases(in_refs, out_refs, scr)
        pl.when(step == 0)(start)
        pl.when(step == nsteps * mid[0][0] // mid[0][1])(forward_neighbours)
        pl.when(step == nsteps * mid[1][0] // mid[1][1])(forward_diagonal)

    def after(step, nsteps, in_refs, out_refs, scr):
        pl.when(step == nsteps - 1)(phases(in_refs, out_refs, scr)[3])

    return _Comm(
        inputs=list(shards), in_specs=[_VMEM_WHOLE] * na,
        out_shape=[jax.ShapeDtypeStruct((N_DEV,) + s.shape, BF16 if cast[a] else s.dtype)
                   for a, s in enumerate(shards)],
        out_specs=[_HBM_WHOLE] * na,
        scratch_shapes=[pltpu.VMEM(shards[a].shape, BF16) for a in staged] + [
            pltpu.SemaphoreType.DMA((na, 7)), pltpu.SemaphoreType.DMA((na, 7)), pltpu.SemaphoreType.DMA((na,))],
        before=before, after=after)


def _reduce_scatter_comm(grads):
    _, _, r, c = grads.shape
    order = (3, 1, 2, 0)

    def sibling_copies(in_refs, scr):
        (g_ref,), (sib, own, _, sems) = in_refs, scr
        x, y, core = _mesh_pos()
        mine = 2 * x + y
        remote = {k: pltpu.make_async_remote_copy(
            src_ref=g_ref.at[mine ^ k, 1 - core], dst_ref=sib.at[k], send_sem=sems.at[0, k], recv_sem=sems.at[1, k],
            device_id=(x, y, 1 - core), device_id_type=MESH_ID) for k in order}
        local = {k: pltpu.make_async_copy(g_ref.at[mine ^ k, core], own.at[k], sems.at[2, k]) for k in order}
        return remote, local

    def chip_copies(out_refs, scr):
        (_, recv), (_, _, part, sems) = out_refs, scr
        x, y, core = _mesh_pos()
        return [pltpu.make_async_remote_copy(
            src_ref=part.at[j], dst_ref=recv.at[j], send_sem=sems.at[3, j], recv_sem=sems.at[4, j],
            device_id=(*chip, core), device_id_type=MESH_ID) for j, chip in enumerate(_other_chips(x, y))]

    def before(step, nsteps, in_refs, out_refs, scr):
        @pl.when(step == 0)
        def _():
            remote, local = sibling_copies(in_refs, scr)
            for k in order:
                remote[k].start()
                local[k].start()

    def after(step, nsteps, in_refs, out_refs, scr):
        sib, own, part, _ = scr

        @pl.when(step == (nsteps - 1) // 2)
        def _():
            remote, local = sibling_copies(in_refs, scr)
            to_chips = chip_copies(out_refs, scr)
            for k in order:
                remote[k].wait_recv()
                local[k].wait()
                if k:
                    part[k - 1] = (own[k] + sib[k]).astype(BF16)
                    to_chips[k - 1].start()
                else:
                    out_refs[0][...] = own[0] + sib[0]
            for k in order:
                remote[k].wait_send()

        @pl.when(step == nsteps - 1)
        def _():
            for cp in chip_copies(out_refs, scr):
                cp.wait()

    return _Comm(inputs=[grads], in_specs=[_HBM_WHOLE],
                 out_shape=[jax.ShapeDtypeStruct((r, c), F32), jax.ShapeDtypeStruct((3, r, c), BF16)],
                 out_specs=[_VMEM_WHOLE, _HBM_WHOLE],
                 scratch_shapes=[pltpu.VMEM((4, r, c), F32), pltpu.VMEM((4, r, c), F32), pltpu.VMEM((3, r, c), BF16),
                                 pltpu.SemaphoreType.DMA((5, 4))],
                 before=before, after=after)


def _comm_only(comms, name):
    return _fused_call(lambda: None, comms, name=name, grid=(1,), inputs=(), in_specs=[], out_specs=(),
                       out_shape=())[1]


def _all_reduce_small_comm(parts):
    na = len(parts)

    def copies(in_refs, scr):
        gathered, (send_sems, recv_sems) = scr[:na], scr[na:]
        x, y, c = _mesh_pos()
        my_id = 4 * x + 2 * y + c
        return my_id, [pltpu.make_async_remote_copy(
            src_ref=in_refs[a], dst_ref=gathered[a].at[my_id], send_sem=send_sems.at[a, k - 1],
            recv_sem=recv_sems.at[a, k - 1], device_id=(x ^ (k >> 2), y ^ ((k >> 1) & 1), c ^ (k & 1)),
            device_id_type=MESH_ID) for a in range(na) for k in range(1, N_DEV)]

    def before(step, nsteps, in_refs, out_refs, scr):
        @pl.when(step == 0)
        def _():
            for cp in copies(in_refs, scr)[1]:
                cp.start()

    def after(step, nsteps, in_refs, out_refs, scr):
        @pl.when(step == nsteps - 1)
        def _():
            my_id, cps = copies(in_refs, scr)
            for a in range(na):
                scr[a][my_id] = in_refs[a][...]
            for cp in cps:
                cp.wait()
            for a in range(na):
                acc = scr[a][0]
                for d in range(1, N_DEV):
                    acc = acc + scr[a][d]
                out_refs[a][...] = acc

    return _Comm(inputs=list(parts), in_specs=[_VMEM_WHOLE] * na,
                 out_shape=[jax.ShapeDtypeStruct(p.shape, F32) for p in parts], out_specs=[_VMEM_WHOLE] * na,
                 scratch_shapes=[pltpu.VMEM((N_DEV,) + p.shape, F32) for p in parts] + [
                     pltpu.SemaphoreType.DMA((na, N_DEV - 1)), pltpu.SemaphoreType.DMA((na, N_DEV - 1))],
                 before=before, after=after)


def _unshard_cols(g):
    return jnp.transpose(g, (1, 0, 2)).reshape(g.shape[1], N_DEV * g.shape[2])


def _row_blocks(w):
    return w.reshape(4, 2, w.shape[0] // N_DEV, w.shape[1])


def _stack_rows(parts):
    a = jnp.concatenate(parts, axis=0)
    return jnp.pad(a, ((0, (-a.shape[0]) % 8), (0, 0)))


def _w_in_grad_blocks(dw):
    return _row_blocks(jnp.concatenate([dw[:LR_REF], dw[LR_COL:LR_COL + 2 * LOWRANK], dw[LR_REF:LR_COL]], axis=0))


def _padded_decay_weights(wd_f, wd_b):
    zeros = lambda n: jnp.zeros((n, KEY_W), F32)
    return (jnp.concatenate([wd_f, zeros(LANE - LOWRANK)], axis=0),
            jnp.concatenate([zeros(LOWRANK), wd_b, zeros(LANE - 2 * LOWRANK)], axis=0))


def kernel(x, norm1_g, w_in,w_decay_f, b_decay_f, w_decay_b, b_decay_b, gla_norm_g, gmlp_ln_g, gmlp_ln_b, w_spatial, b_spatial, w_out, norm2_g, w_gate, w_up, w_down, final_norm_g, loss_target, m_norm1_g, m_w_in, m_w_decay_f, m_b_decay_f, m_w_decay_b, m_b_decay_b, m_gla_norm_g, m_gmlp_ln_g, m_gmlp_ln_b, m_w_spatial, m_b_spatial, m_w_out, m_norm2_g, m_w_gate, m_w_up, m_w_down, m_final_norm_g, v_norm1_g, v_w_in, v_w_decay_f, v_b_decay_f, v_w_decay_b, v_b_decay_b, v_gla_norm_g, v_gmlp_ln_g, v_gmlp_ln_b, v_w_spatial, v_b_spatial, v_w_out, v_norm2_g, v_w_gate, v_w_up, v_w_down, v_final_norm_g):
    t = x.shape[1]
    xt = x[0]
    target = loss_target[0]
    pos_x, pos_y, pos_c = _mesh_pos()
    my_id = 4 * pos_x + 2 * pos_y + pos_c

    tile = lambda n: min(n, t)
    ln_g, ln_b, w_sp = gmlp_ln_g, gmlp_ln_b, w_spatial[0]
    b_sp_col = b_spatial[0][:, :, None]
    shard = {"w_in": w_in[0].T, "w_out": w_out[0], "w_gate": w_gate[0].T, "w_up": w_up[0].T, "w_down": w_down[0]}
    shard_m = {"w_in": m_w_in[0].T, "w_out": m_w_out[0], "w_gate": m_w_gate[0].T, "w_up": m_w_up[0].T,
               "w_down": m_w_down[0]}
    shard_v = {"w_in": v_w_in[0].T, "w_out": v_w_out[0], "w_gate": v_w_gate[0].T, "w_up": v_w_up[0].T,
               "w_down": v_w_down[0]}
    transposed = ("w_in", "w_gate", "w_up")

    decay_shard = jnp.stack([w_decay_f[0], w_decay_b[0]])
    (hb,), ((g_in, g_decay),) = _norm1(xt, norm1_g, tile(TOKEN_TILE["norm1"]),
                                       [_gather_comm([shard["w_in"], decay_shard], [True, False])])
    w_in_t = g_in.reshape(PROJ_W, D_MODEL)
    wd_pad_f, wd_pad_b = _padded_decay_weights(_unshard_cols(g_decay[:, 0]), _unshard_cols(g_decay[:, 1]))
    (p,), ((g_gate, g_out),) = _in_proj(
        hb, w_in_t, tile(TOKEN_TILE["in_proj"]), [_gather_comm([shard["w_gate"], shard["w_out"]], [True, True])])
    (o_f, st_f, o_b, st_b), ((g_up,),) = _gla_fwd(
        p, wd_pad_f, b_decay_f, wd_pad_b, b_decay_b, tile(TOKEN_TILE["gla"]), [_gather_comm([shard["w_up"]], [True])])
    w_out_full = g_out.reshape(D_MODEL, D_MODEL)
    (x1, ycat), ((g_down,),) = _mix_fwd(xt, o_f, o_b, p, gla_norm_g, ln_g, ln_b, w_sp, b_sp_col, w_out_full,
                                        tile(TOKEN_TILE["mix_fwd"]), [_gather_comm([shard["w_down"]], [True])])

    dx1, h2b, dgate, dup, act, dx2, loss_acc, d_gf, d_g2 = _ffn(
        x1, target, norm2_g, final_norm_g[None, :], g_gate.reshape(D_FF, D_MODEL), g_up.reshape(D_FF, D_MODEL),
        g_down.reshape(D_FF, D_MODEL), tile(TOKEN_TILE["ffn"]))
    reduced = {}
    dw_gate, _ = _matmul_tn(dgate, h2b, D_FF // 2, tile(TOKEN_TILE["dw"]), "grad_w_gate")
    dw_up, (reduced["w_gate"],) = _matmul_tn(dup, h2b, D_FF // 2, tile(TOKEN_TILE["dw_host"]), "grad_w_up",
                                             [_reduce_scatter_comm(_row_blocks(dw_gate))])
    dw_down, (reduced["w_up"],) = _matmul_tn(act, dx2, D_FF // 2, tile(TOKEN_TILE["dw_host"]), "grad_w_down",
                                             [_reduce_scatter_comm(_row_blocks(dw_up))])

    (d_o, dpg, dpu, dpv, dw_out, d_gg, d_lg, d_lb, dw_sp, db_sp), (reduced["w_down"],) = _mix_bwd(
        dx1, ycat, o_f, o_b, p, gla_norm_g, ln_g, ln_b, w_sp, b_sp_col, w_out_full,
        tile(TOKEN_TILE["mix_bwd"]), [_reduce_scatter_comm(_row_blocks(dw_down))])
    (dq_f, dk_f, dv_f, dlr_f, dwd_f, dbd_f, dq_b, dk_b, dv_b, dlr_b, dwd_b, dbd_b), (reduced["w_out"],) = _gla_bwd(
        p, wd_pad_f, b_decay_f, wd_pad_b, b_decay_b, st_f, st_b, d_o, tile(TOKEN_TILE["gla"]),
        [_reduce_scatter_comm(_row_blocks(dw_out))])
    (grad_x, dp, d_g1), _ = _in_proj_bwd(
        xt, norm1_g, dx1, dq_f, dq_b, dk_f, dk_b, dv_f, dv_b, dpg, dpu, dpv, dlr_f, dlr_b, w_in_t,
        tile(TOKEN_TILE["in_proj_bwd"]))

    stacks = [_stack_rows([d_g1, d_g2, d_gf]), _stack_rows([d_gg, d_lg, d_lb]),
              _stack_rows([dbd_f, dbd_b, jnp.zeros((DECAY_W_ROW - 2, KEY_W), F32), dwd_f[:LOWRANK],
                           dwd_b[LOWRANK:2 * LOWRANK]]),
              _stack_rows([dw_sp.reshape(GMLP_W, GMLP_CHUNK), db_sp[:, :, 0], loss_acc[:1]])]
    dw_main, (small_sums,) = _matmul_tn(dp, hb, PROJ_PAD // 3, tile(TOKEN_TILE["dw"]), "grad_w_in",
                                        [_all_reduce_small_comm(stacks)])
    (reduced["w_in"],) = _comm_only([_reduce_scatter_comm(_w_in_grad_blocks(dw_main))], "grad_w_in_reduce_scatter")

    big_out = {}
    for n, (own_sum, recv) in reduced.items():
        rows = shard[n].shape[0]
        half = rows // 2 if rows % 32 == 0 else rows
        res = _adamw_shard(own_sum, recv, shard[n], shard_m[n], shard_v[n], half, "adamw_" + n)
        big_out[n] = [r.T if n in transposed else r for r in res]

    s1024, s512, s256, s128 = small_sums
    loss = s128[GMLP_W + GMLP_GROUPS, 0]
    col0 = my_id * (KEY_W // N_DEV)
    decay_cols = lambda row0: lax.dynamic_slice(s256, (row0, col0), (LOWRANK, KEY_W // N_DEV))
    flat = lambda a: a.reshape(-1, a.shape[-1])
    small = {
        "norm1_g": ((s1024, 0, 1), norm1_g, m_norm1_g, v_norm1_g),
        "w_decay_f": ((decay_cols(DECAY_W_ROW), 0, LOWRANK), w_decay_f, m_w_decay_f, v_w_decay_f),
        "b_decay_f": ((s256, 0, 1), b_decay_f, m_b_decay_f, v_b_decay_f),
        "w_decay_b": ((decay_cols(DECAY_W_ROW + LOWRANK), 0, LOWRANK), w_decay_b, m_w_decay_b, v_w_decay_b),
        "b_decay_b": ((s256, 1, 1), b_decay_b, m_b_decay_b, v_b_decay_b),
        "gla_norm_g": ((s512, 0, 1), gla_norm_g, m_gla_norm_g, v_gla_norm_g),
        "gmlp_ln_g": ((s512, 1, 1), gmlp_ln_g, m_gmlp_ln_g, v_gmlp_ln_g),
        "gmlp_ln_b": ((s512, 2, 1), gmlp_ln_b, m_gmlp_ln_b, v_gmlp_ln_b),
        "w_spatial": ((s128, 0, GMLP_W), w_spatial, m_w_spatial, v_w_spatial),
        "b_spatial": ((s128, GMLP_W, GMLP_GROUPS), b_spatial, m_b_spatial, v_b_spatial),
        "norm2_g": ((s1024, 1, 1), norm2_g, m_norm2_g, v_norm2_g),
        "final_norm_g": ((s1024, 2, 1), final_norm_g, m_final_norm_g, v_final_norm_g),
    }
    small_res = _adamw_small([(g, flat(w), flat(m), flat(v)) for g, w, m, v in small.values()])
    small_out = {n: [r.reshape(small[n][1].shape) for r in res] for n, res in zip(small, small_res)}

    order = ["norm1_g", "w_in", "w_decay_f", "b_decay_f", "w_decay_b", "b_decay_b", "gla_norm_g", "gmlp_ln_g",
             "gmlp_ln_b", "w_spatial", "b_spatial", "w_out", "norm2_g", "w_gate", "w_up", "w_down", "final_norm_g"]
    outs = []
    for kind in range(4):
        for n in order:
            outs.append(big_out[n][kind][None] if n in big_out else small_out[n][kind])
    return (loss, grad_x[None], *outs)
```

```python
import functools
import math

import jax
import jax.numpy as jnp
from jax import lax
from jax.experimental import pallas as pl
from jax.experimental.pallas import tpu as pltpu

F32 = jnp.float32
BF16 = jnp.bfloat16

D_MODEL = 1024
GLA_HEADS = 4
GLA_DK = 64
GLA_DV = 128
KEY_W = GLA_HEADS * GLA_DK
VAL_W = GLA_HEADS * GLA_DV
LOWRANK = 16
GLA_TAU = 16.0
GLA_CHUNK = 64
GMLP_W = 512
GMLP_GROUPS = 4
GMLP_CHUNK = 128
D_FF = 2816
EPS = 1e-6
Q_SCALE = GLA_DK ** -0.5
PROJ_PAD = 2688
LR_COL = 2560
LANE = 128
N_DEV = 8

ADAM_LR = 0.001
ADAM_B1 = 0.9
ADAM_B2 = 0.999
ADAM_EPS = 1e-08
ADAM_WD = 0.01
ADAM_STEP = 10

VMEM_LIMIT = 56 * 1024 * 1024
TOKEN_TILE = {"norm1": 512, "in_proj": 512, "gla": 1024, "mix_fwd": 1024, "ffn": 256, "mix_bwd": 512,
              "in_proj_bwd": 512, "dw": 2048, "dw_host": 2048}
DECAY_W_ROW = 8
MESH_ID = pl.DeviceIdType.MESH
INV_SQRT2 = 0.7071067811865476
INV_SQRT_2PI = 0.3989422804014327


def _params(n_axes=1):
    return pltpu.CompilerParams(dimension_semantics=("arbitrary",) * n_axes, vmem_limit_bytes=VMEM_LIMIT)


def _mm(a, b):
    return jnp.dot(a.astype(BF16), b.astype(BF16), preferred_element_type=F32)


def _mm_nt(a, b):
    return lax.dot_general(a.astype(BF16), b.astype(BF16), (((1,), (1,)), ((), ())), preferred_element_type=F32)


def _mm_tn(a, b):
    return lax.dot_general(a.astype(BF16), b.astype(BF16), (((0,), (0,)), ((), ())), preferred_element_type=F32)


def _const_spec(shape):
    nd = len(shape)
    return pl.BlockSpec(shape, lambda *_: (0,) * nd, pipeline_mode=pl.Buffered(1))


def _acc_spec(shape):
    nd = len(shape)
    return pl.BlockSpec(shape, lambda *_: (0,) * nd)


class _Comm:
    def __init__(self, inputs, in_specs, out_shape, out_specs, scratch_shapes, before, after):
        self.inputs, self.in_specs, self.out_shape, self.out_specs = inputs, in_specs, out_shape, out_specs
        self.scratch_shapes, self.before, self.after = scratch_shapes, before, after


def _fused_call(body, comms, *, name, grid, inputs, in_specs, out_specs, out_shape, scratch_shapes=(), prefetch=()):
    n_pre, n_in, n_out, n_scr = len(prefetch), len(in_specs), len(out_specs), len(scratch_shapes)
    nsteps = math.prod(grid)
    sizes = [(len(c.inputs), len(c.out_shape), len(c.scratch_shapes)) for c in comms]

    def full_body(*refs):
        step = pl.program_id(0)
        for axis in range(1, len(grid)):
            step = step * grid[axis] + pl.program_id(axis)
        pre, refs = refs[:n_pre], refs[n_pre:]
        ins, rest = refs[:n_in], refs[n_in:]
        c_ins = []
        for ci, _, _ in sizes:
            c_ins.append(rest[:ci])
            rest = rest[ci:]
        outs, rest = rest[:n_out], rest[n_out:]
        c_outs = []
        for _, co, _ in sizes:
            c_outs.append(rest[:co])
            rest = rest[co:]
        scr, rest = rest[:n_scr], rest[n_scr:]
        c_scr = []
        for _, _, cs in sizes:
            c_scr.append(rest[:cs])
            rest = rest[cs:]
        for c, a, b, s in zip(comms, c_ins, c_outs, c_scr):
            c.before(step, nsteps, a, b, s)
        body(*pre, *ins, *outs, *scr)
        for c, a, b, s in zip(comms, c_ins, c_outs, c_scr):
            c.after(step, nsteps, a, b, s)

    specs = dict(
        grid=grid, in_specs=list(in_specs) + [s for c in comms for s in c.in_specs],
        out_specs=tuple(out_specs) + tuple(s for c in comms for s in c.out_specs),
        scratch_shapes=list(scratch_shapes) + [s for c in comms for s in c.scratch_shapes])
    if n_pre:
        specs = dict(grid_spec=pltpu.PrefetchScalarGridSpec(num_scalar_prefetch=n_pre, **specs))
    results = pl.pallas_call(
        full_body, name=name, **specs,
        out_shape=tuple(out_shape) + tuple(s for c in comms for s in c.out_shape),
        compiler_params=_params(len(grid)),
    )(*prefetch, *inputs, *[a for c in comms for a in c.inputs])
    own, rest = results[:n_out], results[n_out:]
    comm_results = []
    for _, co, _ in sizes:
        comm_results.append(rest[:co])
        rest = rest[co:]
    return own, comm_results


def _gelu(x):
    return 0.5 * x * (1.0 + lax.erf(x * INV_SQRT2))


def _gelu_and_grad(x):
    cdf = 0.5 * (1.0 + lax.erf(x * INV_SQRT2))
    return x * cdf, cdf + x * jnp.exp(-0.5 * x * x) * INV_SQRT_2PI


def _sigmoid(x):
    return 0.5 + 0.5 * jnp.tanh(0.5 * x)


def _silu_and_grad(x):
    s = _sigmoid(x)
    return x * s, s * (1.0 + x * (1.0 - s))


def _norm1(x, g1, tm, comms=()):
    t = x.shape[0]

    def body(x_ref, g_ref, h_ref):
        xv = x_ref[...]
        r = lax.rsqrt(jnp.mean(xv * xv, axis=-1, keepdims=True) + EPS)
        h_ref[...] = (xv * r * g_ref[...]).astype(BF16)

    row = pl.BlockSpec((tm, D_MODEL), lambda i: (i, 0))
    return _fused_call(body, comms, name="norm1", grid=(t // tm,), inputs=(x, g1),
                       in_specs=[row, _const_spec((1, D_MODEL))], out_specs=(row,),
                       out_shape=(jax.ShapeDtypeStruct((t, D_MODEL), BF16),))


PROJ_W = 2592
LR_REF = 1536
PROJ_ROWS = ((0, LR_REF, 0), (LR_REF + 2 * LOWRANK, PROJ_W, LR_REF), (LR_REF, LR_REF + LANE, LR_COL))


def _in_proj(h, w_in_t, tm, comms=()):
    t = h.shape[0]

    def body(h_ref, w_ref, p_ref):
        hv = h_ref[...]
        for r0, r1, c0 in PROJ_ROWS:
            p_ref[:, c0:c0 + r1 - r0] = _mm_nt(hv, w_ref[r0:r1, :]).astype(BF16)

    return _fused_call(
        body, comms, name="in_proj", grid=(t // tm,), inputs=(h, w_in_t),
        in_specs=[pl.BlockSpec((tm, D_MODEL), lambda i: (i, 0)), _const_spec((PROJ_W, D_MODEL))],
        out_specs=(pl.BlockSpec((tm, PROJ_PAD), lambda i: (i, 0)),),
        out_shape=(jax.ShapeDtypeStruct((t, PROJ_PAD), BF16),))


def _tri(upper):
    r = lax.broadcasted_iota(jnp.int32, (GLA_CHUNK, GLA_CHUNK), 0)
    c = lax.broadcasted_iota(jnp.int32, (GLA_CHUNK, GLA_CHUNK), 1)
    return jnp.where((c >= r) if upper else (c <= r), 1.0, 0.0).astype(BF16)


def _chunk_cumsum(tri, a, add=None):
    hi = a.astype(BF16)
    lo = (a - hi.astype(F32)).astype(BF16)
    dot = functools.partial(jnp.dot, preferred_element_type=F32)
    sums = [dot(tri, hi[_chunk_rows(c)]) + dot(tri, lo[_chunk_rows(c)]) for c in range(a.shape[0] // GLA_CHUNK)]
    return jnp.concatenate(sums if add is None else [s + r for s, r in zip(sums, add)], axis=0)


def _chunk_rows(c):
    return slice(c * GLA_CHUNK, (c + 1) * GLA_CHUNK)


def _gla_masks(rev):
    dk_bits, dv_bits = GLA_DK.bit_length() - 1, GLA_DV.bit_length() - 1
    key_head = lax.broadcasted_iota(jnp.int32, (GLA_CHUNK, KEY_W), 1) >> dk_bits
    val_head = lax.broadcasted_iota(jnp.int32, (GLA_CHUNK, VAL_W), 1) >> dv_bits
    t = lax.broadcasted_iota(jnp.int32, (GLA_HEADS * GLA_CHUNK, GLA_CHUNK), 0) & (GLA_CHUNK - 1)
    s = lax.broadcasted_iota(jnp.int32, (GLA_HEADS * GLA_CHUNK, GLA_CHUNK), 1)
    return key_head, val_head, (s >= t) if rev else (s <= t)


def _stack_heads(a, head_of_lane):
    a = a.astype(BF16)
    return jnp.concatenate([jnp.where(head_of_lane == h, a, jnp.zeros_like(a)) for h in range(GLA_HEADS)], axis=0)


def _rows_by_head(a):
    return jnp.concatenate([a[:, h * GLA_DV:(h + 1) * GLA_DV] for h in range(GLA_HEADS)], axis=0)


def _lanes_by_head(r):
    return jnp.concatenate([r[h * GLA_CHUNK:(h + 1) * GLA_CHUNK] for h in range(GLA_HEADS)], axis=1)


def _head_diagonal(r, head_of_lane):
    rows = r.shape[0] // GLA_HEADS
    out = jnp.where(head_of_lane == 0, r[:rows], 0.0)
    for h in range(1, GLA_HEADS):
        out = out + jnp.where(head_of_lane == h, r[h * rows:(h + 1) * rows], 0.0)
    return out


def _tile_terms(la, q, k, tri, rev):
    nc = la.shape[0] // GLA_CHUNK
    q, k = q.astype(F32), k.astype(F32)
    b = _chunk_cumsum(tri, la)
    ebl = [jnp.exp(b[c * GLA_CHUNK:c * GLA_CHUNK + 1] if rev else b[(c + 1) * GLA_CHUNK - 1:(c + 1) * GLA_CHUNK])
           for c in range(nc)]
    eb = jnp.exp(b)
    enb = jnp.exp(-b)
    kd = k * enb
    ke = jnp.concatenate([kd[_chunk_rows(c)] * ebl[c] for c in range(nc)], axis=0)
    return ebl, eb, enb, q * Q_SCALE * eb, kd, ke


def _log_decay(lr_ref, wd_ref, bd_ref):
    z = _mm(lr_ref[...], wd_ref[...]) + bd_ref[...]
    return z, jax.nn.log_sigmoid(z) * (1.0 / GLA_TAU)


def _p_specs(tg, tile):
    return [pl.BlockSpec((tg, KEY_W), lambda i: (tile(i), 0)),
            pl.BlockSpec((tg, KEY_W), lambda i: (tile(i), 1)),
            pl.BlockSpec((tg, VAL_W), lambda i: (tile(i), 1)),
            pl.BlockSpec((tg, LANE), lambda i: (tile(i), LR_COL // LANE))]


def _gla_fwd_dir(rev, nc, q_ref, k_ref, v_ref, lr_ref, wd_ref, bd_ref, o_ref, st_ref, state):
    key_head, _, causal = _gla_masks(rev)
    order = range(nc - 1, -1, -1) if rev else range(nc)

    def intra():
        _, la = _log_decay(lr_ref, wd_ref, bd_ref)
        ebl, _, _, qd, kd, ke = _tile_terms(la, q_ref[...], k_ref[...], _tri(rev), rev)
        kd = kd.astype(BF16)
        v = {c: v_ref[_chunk_rows(c), :].astype(BF16) for c in order}
        qd_stack = {c: _stack_heads(qd[_chunk_rows(c)], key_head) for c in order}
        ke_stack = {c: _stack_heads(ke[_chunk_rows(c)], key_head) for c in order}
        a_all = {c: _mm_nt(qd_stack[c], kd[_chunk_rows(c)]) for c in order}
        a_all = {c: jnp.where(causal, a_all[c], 0.0).astype(BF16) for c in order}
        head_rows = lambda a, h: a[h * GLA_CHUNK:(h + 1) * GLA_CHUNK]
        head_vals = lambda a, h: a[:, h * GLA_DV:(h + 1) * GLA_DV]
        r = {c: [_mm(head_rows(a_all[c], h), head_vals(v[c], h)) for h in range(GLA_HEADS)] for c in order}
        upd = {c: _mm_tn(_rows_by_head(v[c]), ke_stack[c]) for c in order}
        return {c: (ebl[c], qd_stack[c], r[c], upd[c]) for c in order}

    def scan(terms):
        st = state[...]
        states = {}
        for c in order:
            states[c] = st
            st_ref[c] = st.astype(BF16)
            st = st * terms[c][0] + terms[c][3]
        state[...] = st
        return states

    def inter(terms, states):
        r_inter = {c: _mm_nt(terms[c][1], states[c]) for c in order}
        for c in order:
            o_ref[_chunk_rows(c), :] = jnp.concatenate(
                [terms[c][2][h] + r_inter[c][h * GLA_CHUNK:(h + 1) * GLA_CHUNK] for h in range(GLA_HEADS)], axis=1)

    return intra, scan, inter


def _gla_fwd(p, wd_pad_f, bd_f, wd_pad_b, bd_b, tg, comms=()):
    t = p.shape[0]
    nt = t // tg
    nc = tg // GLA_CHUNK
    up, down = (lambda i: i), (lambda i: nt - 1 - i)

    def body(qf, kf, vf, lrf, qb, kb, vb, lrb, wdf, bdf, wdb, bdb, of, stf, ob, stb, state_f, state_b):
        @pl.when(pl.program_id(0) == 0)
        def _():
            state_f[...] = jnp.zeros_like(state_f)
            state_b[...] = jnp.zeros_like(state_b)

        dirs = [_gla_fwd_dir(False, nc, qf, kf, vf, lrf, wdf, bdf, of, stf, state_f),
                _gla_fwd_dir(True, nc, qb, kb, vb, lrb, wdb, bdb, ob, stb, state_b)]
        terms = [intra() for intra, _, _ in dirs]
        states = [scan(t) for (_, scan, _), t in zip(dirs, terms)]
        for (_, _, inter), t, s in zip(dirs, terms, states):
            inter(t, s)

    wd_spec, bd_spec = _const_spec((LANE, KEY_W)), _const_spec((1, KEY_W))
    outs = lambda tile: (pl.BlockSpec((tg, VAL_W), lambda i: (tile(i), 0)),
                         pl.BlockSpec((nc, GLA_DV, KEY_W), lambda i: (tile(i), 0, 0)))
    out_shape = (jax.ShapeDtypeStruct((t, VAL_W), F32), jax.ShapeDtypeStruct((t // GLA_CHUNK, GLA_DV, KEY_W), BF16))
    return _fused_call(
        body, comms, name="gla_fwd", grid=(nt,), inputs=(p,) * 8 + (wd_pad_f, bd_f, wd_pad_b, bd_b),
        in_specs=_p_specs(tg, up) + _p_specs(tg, down) + [wd_spec, bd_spec, wd_spec, bd_spec],
        out_specs=outs(up) + outs(down), out_shape=out_shape * 2,
        scratch_shapes=[pltpu.VMEM((GLA_DV, KEY_W), F32)] * 2)


def _gla_bwd_dir(rev, nc, q_ref, k_ref, v_ref, lr_ref, wd_ref, bd_ref, st_ref, do_ref,
                 dq_ref, dk_ref, dv_ref, dlr_ref, dwd_ref, dbd_ref, dstate):
    key_head, val_head, causal = _gla_masks(rev)
    order = range(nc) if rev else range(nc - 1, -1, -1)

    def intra():
        z, la = _log_decay(lr_ref, wd_ref, bd_ref)
        tile = _tile_terms(la, q_ref[...], k_ref[...], _tri(rev), rev)
        qd, kd = tile[3], tile[4].astype(BF16)
        v = {c: v_ref[_chunk_rows(c), :].astype(BF16) for c in order}
        d_o = {c: do_ref[_chunk_rows(c), :] for c in order}
        kd_c = {c: kd[_chunk_rows(c)] for c in order}
        qd_stack = {c: _stack_heads(qd[_chunk_rows(c)], key_head) for c in order}
        do_stack = {c: _stack_heads(d_o[c], val_head) for c in order}
        do_rows = {c: _rows_by_head(d_o[c]) for c in order}
        a_all = {c: _mm_nt(qd_stack[c], kd_c[c]) for c in order}
        head_vals = lambda a, h: a[:, h * GLA_DV:(h + 1) * GLA_DV]
        da_all = {c: jnp.concatenate([_mm_nt(head_vals(d_o[c], h), head_vals(v[c], h)) for h in range(GLA_HEADS)],
                                     axis=0) for c in order}
        a_all = {c: jnp.where(causal, a_all[c], 0.0).astype(BF16) for c in order}
        da_all = {c: jnp.where(causal, da_all[c], 0.0).astype(BF16) for c in order}
        dv = {c: _mm_tn(a_all[c], do_stack[c]) for c in order}
        dqd = {c: _mm(jnp.concatenate([do_rows[c], da_all[c]], axis=1),
                      jnp.concatenate([st_ref[c], kd_c[c]], axis=0)) for c in order}
        dkd = {c: _mm_tn(da_all[c], qd_stack[c]) for c in order}
        upd = {c: _mm_tn(do_rows[c], qd_stack[c]) for c in order}
        dqd = {c: _head_diagonal(dqd[c], key_head) for c in order}
        return z, tile, {c: dict(dv=dv[c], dqd=dqd[c], dkd=dkd[c], upd=upd[c]) for c in order}

    def scan(tile, per):
        dst = dstate[...]
        dsts = {}
        for c in order:
            dsts[c] = dst
            dst = dst * tile[0][c] + per[c]["upd"]
        dstate[...] = dst
        return dsts

    def inter(z, tile, per, dsts):
        ebl, eb, enb, qd, kd, ke = tile
        ke_stack = {c: _stack_heads(ke[_chunk_rows(c)], key_head) for c in order}
        v_rows = {c: _rows_by_head(v_ref[_chunk_rows(c), :].astype(BF16)) for c in order}
        dst_b = {c: dsts[c].astype(BF16) for c in order}
        dv_state = {c: _mm_nt(ke_stack[c], dst_b[c]) for c in order}
        dke_c = {c: _mm(v_rows[c], dst_b[c]) for c in order}
        dke_c = {c: _head_diagonal(dke_c[c], key_head) for c in order}
        dbl_c = {}
        for c in order:
            rows = _chunk_rows(c)
            dv_ref[rows, :] = (per[c]["dv"] + _lanes_by_head(dv_state[c])).astype(BF16)
            dbl_c[c] = (jnp.sum(dsts[c] * st_ref[c].astype(F32), axis=0, keepdims=True) * ebl[c]
                        + jnp.sum(dke_c[c] * ke[rows], axis=0, keepdims=True))
        tile_of = lambda parts: jnp.concatenate([parts[c] for c in range(nc)], axis=0)
        dqd, dkd = tile_of({c: per[c]["dqd"] for c in order}), tile_of({c: per[c]["dkd"] for c in order})
        dke = tile_of(dke_c)
        dke_end = tile_of({c: dke_c[c] * ebl[c] for c in order})
        dq_ref[...] = (dqd * eb * Q_SCALE).astype(BF16)
        dk_ref[...] = ((dkd + dke_end) * enb).astype(BF16)
        db = dqd * qd - dkd * kd - dke * ke
        dla = _chunk_cumsum(_tri(not rev), db, [dbl_c[c] for c in range(nc)])
        dz = dla * (_sigmoid(-z) * (1.0 / GLA_TAU))
        dlr_ref[...] = _mm_nt(dz, wd_ref[...]).astype(BF16)
        dwd_ref[...] += _mm_tn(lr_ref[...], dz)
        dbd_ref[...] += jnp.sum(dz, axis=0, keepdims=True)

    return intra, scan, inter


def _gla_bwd(p, wd_pad_f, bd_f, wd_pad_b, bd_b, st_f, st_b, d_o, tg, comms=()):
    t = p.shape[0]
    nt = t // tg
    nc = tg // GLA_CHUNK
    up, down = (lambda i: i), (lambda i: nt - 1 - i)

    def body(qf, kf, vf, lrf, stf, dof, qb, kb, vb, lrb, stb, dob, wdf, bdf, wdb, bdb,
             dqf, dkf, dvf, dlrf, dwdf, dbdf, dqb, dkb, dvb, dlrb, dwdb, dbdb, dstate_f, dstate_b):
        @pl.when(pl.program_id(0) == 0)
        def _():
            for ref in (dstate_f, dstate_b, dwdf, dbdf, dwdb, dbdb):
                ref[...] = jnp.zeros_like(ref)

        dirs = [_gla_bwd_dir(False, nc, qf, kf, vf, lrf, wdf, bdf, stf, dof, dqf, dkf, dvf, dlrf, dwdf, dbdf,
                             dstate_f),
                _gla_bwd_dir(True, nc, qb, kb, vb, lrb, wdb, bdb, stb, dob, dqb, dkb, dvb, dlrb, dwdb, dbdb,
                             dstate_b)]
        first = [intra() for intra, _, _ in dirs]
        dsts = [scan(tile, per) for (_, scan, _), (_, tile, per) in zip(dirs, first)]
        for (_, _, inter), (z, tile, per), d in zip(dirs, first, dsts):
            inter(z, tile, per, d)

    wd_spec, bd_spec = _const_spec((LANE, KEY_W)), _const_spec((1, KEY_W))
    ins = lambda tile: _p_specs(tg, tile) + [pl.BlockSpec((nc, GLA_DV, KEY_W), lambda i: (tile(i), 0, 0)),
                                             pl.BlockSpec((tg, VAL_W), lambda i: (tile(i), 0))]
    outs = lambda tile: (pl.BlockSpec((tg, KEY_W), lambda i: (tile(i), 0)),
                         pl.BlockSpec((tg, KEY_W), lambda i: (tile(i), 0)),
                         pl.BlockSpec((tg, VAL_W), lambda i: (tile(i), 0)),
                         pl.BlockSpec((tg, LANE), lambda i: (tile(i), 0)),
                         _acc_spec((LANE, KEY_W)), _acc_spec((1, KEY_W)))
    out_shape = (jax.ShapeDtypeStruct((t, KEY_W), BF16), jax.ShapeDtypeStruct((t, KEY_W), BF16),
                 jax.ShapeDtypeStruct((t, VAL_W), BF16), jax.ShapeDtypeStruct((t, LANE), BF16),
                 jax.ShapeDtypeStruct((LANE, KEY_W), F32), jax.ShapeDtypeStruct((1, KEY_W), F32))
    scratch = [pltpu.VMEM((GLA_DV, KEY_W), F32)]
    return _fused_call(
        body, comms, name="gla_bwd", grid=(nt,),
        inputs=(p, p, p, p, st_f, d_o, p, p, p, p, st_b, d_o, wd_pad_f, bd_f, wd_pad_b, bd_b),
        in_specs=ins(down) + ins(up) + [wd_spec, bd_spec, wd_spec, bd_spec],
        out_specs=outs(down) + outs(up), out_shape=out_shape * 2, scratch_shapes=scratch * 2)


def _head_rms(o):
    parts, scales = [], []
    for h in range(GLA_HEADS):
        oh = o[:, h * GLA_DV:(h + 1) * GLA_DV]
        r = lax.rsqrt(jnp.mean(oh * oh, axis=-1, keepdims=True) + EPS)
        parts.append(oh * r)
        scales.append(jnp.broadcast_to(r, oh.shape))
    return jnp.concatenate(parts, axis=1), jnp.concatenate(scales, axis=1)


def _layernorm_stats(zv):
    mu = jnp.mean(zv, axis=-1, keepdims=True)
    xc = zv - mu
    rs = lax.rsqrt(jnp.mean(xc * xc, axis=-1, keepdims=True) + EPS)
    return xc * rs, rs


def _mix_fwd(x, o_f, o_b, p, gla_g, ln_g, ln_b, w_sp, b_sp, w_out, tm, comms=()):
    t = x.shape[0]
    nch = tm // GMLP_CHUNK

    def body(x_ref, of_ref, ob_ref, pg_ref, pu_ref, pv_ref, gg_ref, lg_ref, lb_ref, ws_ref, bs_ref, wo_ref,
             x1_ref, y_ref, s_scr):
        on, _ = _head_rms(of_ref[...] + ob_ref[...])
        pg = pg_ref[...].astype(F32)
        y_a = on * gg_ref[...] * (pg * _sigmoid(pg))
        zu = _gelu(pu_ref[...].astype(F32))
        vhat, _ = _layernorm_stats(_gelu(pv_ref[...].astype(F32)))
        vln = (vhat * lg_ref[...] + lb_ref[...]).astype(BF16)
        for g in range(GMLP_GROUPS):
            w_g = ws_ref[g].astype(BF16)
            b_g = bs_ref[g]
            cols = slice(g * LANE, (g + 1) * LANE)
            for n in range(nch):
                rows = slice(n * GMLP_CHUNK, (n + 1) * GMLP_CHUNK)
                s_scr[rows, cols] = jnp.dot(w_g, vln[rows, cols], preferred_element_type=F32) + b_g
        ycat = jnp.concatenate([y_a, zu * s_scr[...]], axis=1).astype(BF16)
        y_ref[...] = ycat
        x1_ref[...] = x_ref[...] + jnp.dot(ycat, wo_ref[...], preferred_element_type=F32)

    half = lambda j: pl.BlockSpec((tm, VAL_W), lambda i: (i, j))
    return _fused_call(
        body, comms, name="mix_fwd", grid=(t // tm,),
        inputs=(x, o_f, o_b, p, p, p, gla_g, ln_g, ln_b, w_sp, b_sp, w_out),
        in_specs=[pl.BlockSpec((tm, D_MODEL), lambda i: (i, 0)), half(0), half(0), half(2), half(3), half(4),
                  _const_spec((1, VAL_W)), _const_spec((1, GMLP_W)), _const_spec((1, GMLP_W)),
                  _const_spec((GMLP_GROUPS, GMLP_CHUNK, GMLP_CHUNK)), _const_spec((GMLP_GROUPS, GMLP_CHUNK, 1)),
                  _const_spec((D_MODEL, D_MODEL))],
        out_specs=(pl.BlockSpec((tm, D_MODEL), lambda i: (i, 0)), pl.BlockSpec((tm, D_MODEL), lambda i: (i, 0))),
        out_shape=(jax.ShapeDtypeStruct((t, D_MODEL), F32), jax.ShapeDtypeStruct((t, D_MODEL), BF16)),
        scratch_shapes=[pltpu.VMEM((tm, GMLP_W), F32)])


def _mix_bwd(dx1, ycat, o_f, o_b, p, gla_g, ln_g, ln_b, w_sp, b_sp, w_out, tm, comms=()):
    t = dx1.shape[0]
    nch = tm // GMLP_CHUNK

    def body(dx1_ref, y_ref, of_ref, ob_ref, pg_ref, pu_ref, pv_ref, gg_ref, lg_ref, lb_ref, ws_ref, bs_ref, wo_ref,
             do_ref, dpg_ref, dpu_ref, dpv_ref, dwo_ref, dgg_ref, dlg_ref, dlb_ref, dws_ref, dbs_ref,
             s_scr, dvln_scr):
        @pl.when(pl.program_id(0) == 0)
        def _():
            for ref in (dwo_ref, dgg_ref, dlg_ref, dlb_ref, dws_ref, dbs_ref):
                ref[...] = jnp.zeros_like(ref)

        dx1 = dx1_ref[...].astype(BF16)
        dycat = _mm_nt(dx1, wo_ref[...])
        dwo_ref[...] += _mm_tn(y_ref[...], dx1)
        dy_a = dycat[:, :VAL_W]
        dy_b = dycat[:, VAL_W:]
        on, r = _head_rms(of_ref[...] + ob_ref[...])
        pg = pg_ref[...].astype(F32)
        sil, dsil = _silu_and_grad(pg)
        gg = gg_ref[...]
        dgg_ref[...] += jnp.sum(dy_a * sil * on, axis=0, keepdims=True)
        don = dy_a * sil * gg
        prod = don * on
        means = jnp.concatenate(
            [jnp.broadcast_to(jnp.mean(prod[:, h * GLA_DV:(h + 1) * GLA_DV], axis=-1, keepdims=True),
                              (tm, GLA_DV)) for h in range(GLA_HEADS)], axis=1)
        do_ref[...] = (r * (don - on * means)).astype(BF16)
        dpg_ref[...] = (dy_a * on * gg * dsil).astype(BF16)
        pu = pu_ref[...].astype(F32)
        pv = pv_ref[...].astype(F32)
        zu, dzu_dpu = _gelu_and_grad(pu)
        zv, dzv_dpv = _gelu_and_grad(pv)
        vhat, rs = _layernorm_stats(zv)
        lg = lg_ref[...]
        vln = (vhat * lg + lb_ref[...]).astype(BF16)
        ds32 = dy_b * zu
        ds = ds32.astype(BF16)
        blocks = [(g, n) for g in range(GMLP_GROUPS) for n in range(nch)]
        at = lambda g, n: (slice(n * GMLP_CHUNK, (n + 1) * GMLP_CHUNK), slice(g * LANE, (g + 1) * LANE))
        w_sp = [ws_ref[g].astype(BF16) for g in range(GMLP_GROUPS)]
        v_blk = {b: vln[at(*b)] for b in blocks}
        ds_blk = {b: ds[at(*b)] for b in blocks}
        s_blk = {b: jnp.dot(w_sp[b[0]], v_blk[b], preferred_element_type=F32) for b in blocks}
        dw_blk = {b: _mm_nt(ds_blk[b], v_blk[b]) for b in blocks}
        dvln_blk = {b: _mm_tn(w_sp[b[0]], ds_blk[b]) for b in blocks}
        for b in blocks:
            s_scr[at(*b)] = s_blk[b] + bs_ref[b[0]]
            dvln_scr[at(*b)] = dvln_blk[b]
        for g in range(GMLP_GROUPS):
            dws_ref[g] += sum(dw_blk[(g, n)] for n in range(nch))
            dbs_ref[g] += sum(jnp.sum(ds32[at(g, n)], axis=-1, keepdims=True) for n in range(nch))
        dpu_ref[...] = (dy_b * s_scr[...] * dzu_dpu).astype(BF16)
        dvln = dvln_scr[...]
        dlg_ref[...] += jnp.sum(dvln * vhat, axis=0, keepdims=True)
        dlb_ref[...] += jnp.sum(dvln, axis=0, keepdims=True)
        dvhat = dvln * lg
        dzv = rs * (dvhat - jnp.mean(dvhat, axis=-1, keepdims=True)
                    - vhat * jnp.mean(dvhat * vhat, axis=-1, keepdims=True))
        dpv_ref[...] = (dzv * dzv_dpv).astype(BF16)

    half = lambda j: pl.BlockSpec((tm, VAL_W), lambda i: (i, j))
    full = pl.BlockSpec((tm, D_MODEL), lambda i: (i, 0))
    sp_shape = (GMLP_GROUPS, GMLP_CHUNK, GMLP_CHUNK)
    bs_shape = (GMLP_GROUPS, GMLP_CHUNK, 1)
    return _fused_call(
        body, comms, name="mix_bwd", grid=(t // tm,),
        inputs=(dx1, ycat, o_f, o_b, p, p, p, gla_g, ln_g, ln_b, w_sp, b_sp, w_out),
        in_specs=[full, full, half(0), half(0), half(2), half(3), half(4),
                  _const_spec((1, VAL_W)), _const_spec((1, GMLP_W)), _const_spec((1, GMLP_W)),
                  _const_spec(sp_shape), _const_spec(bs_shape), _const_spec((D_MODEL, D_MODEL))],
        out_specs=(half(0), half(0), half(0), half(0), _acc_spec((D_MODEL, D_MODEL)), _acc_spec((1, VAL_W)),
                   _acc_spec((1, GMLP_W)), _acc_spec((1, GMLP_W)), _acc_spec(sp_shape), _acc_spec(bs_shape)),
        out_shape=(jax.ShapeDtypeStruct((t, VAL_W), BF16),) * 4 + (
            jax.ShapeDtypeStruct((D_MODEL, D_MODEL), F32), jax.ShapeDtypeStruct((1, VAL_W), F32),
            jax.ShapeDtypeStruct((1, GMLP_W), F32), jax.ShapeDtypeStruct((1, GMLP_W), F32),
            jax.ShapeDtypeStruct(sp_shape, F32), jax.ShapeDtypeStruct(bs_shape, F32)),
        scratch_shapes=[pltpu.VMEM((tm, GMLP_W), F32), pltpu.VMEM((tm, GMLP_W), F32)])


def _rms_bwd(dy_scaled, xn, r):
    return r * (dy_scaled - xn * jnp.mean(dy_scaled * xn, axis=-1, keepdims=True))


def _ffn(x1, target, g2, gf, w_gate, w_up, w_down, tm):
    t = x1.shape[0]

    def body(x1_ref, tg_ref, g2_ref, gf_ref, wg_ref, wu_ref, wd_ref,
             dx1_ref, h2_ref, dgate_ref, dup_ref, act_ref, dx2_ref, loss_ref, dgf_ref, dg2_ref):
        @pl.when(pl.program_id(0) == 0)
        def _():
            for ref in (loss_ref, dgf_ref, dg2_ref):
                ref[...] = jnp.zeros_like(ref)

        x1v = x1_ref[...]
        g2v = g2_ref[...]
        gfv = gf_ref[...]
        r2 = lax.rsqrt(jnp.mean(x1v * x1v, axis=-1, keepdims=True) + EPS)
        xn1 = x1v * r2
        h2 = (xn1 * g2v).astype(BF16)
        h2_ref[...] = h2
        gate = _mm_nt(h2, wg_ref[...])
        up = _mm_nt(h2, wu_ref[...])
        sil, dsil = _silu_and_grad(gate)
        act = (sil * up).astype(BF16)
        act_ref[...] = act
        x2 = x1v + jnp.dot(act, wd_ref[...], preferred_element_type=F32)
        rf = lax.rsqrt(jnp.mean(x2 * x2, axis=-1, keepdims=True) + EPS)
        xn2 = x2 * rf
        err = xn2 * gfv - tg_ref[...]
        loss_ref[...] += 0.5 * jnp.sum(jnp.mean(err * err, axis=-1, keepdims=True))
        dy = err * (1.0 / D_MODEL)
        dgf_ref[...] += jnp.sum(dy * xn2, axis=0, keepdims=True)
        dx2 = _rms_bwd(dy * gfv, xn2, rf)
        dx2b = dx2.astype(BF16)
        dx2_ref[...] = dx2b
        dact = _mm_nt(dx2b, wd_ref[...])
        dgate = (dact * up * dsil).astype(BF16)
        dup = (dact * sil).astype(BF16)
        dgate_ref[...] = dgate
        dup_ref[...] = dup
        dh2 = _mm(dgate, wg_ref[...]) + _mm(dup, wu_ref[...])
        dg2_ref[...] += jnp.sum(dh2 * xn1, axis=0, keepdims=True)
        dx1_ref[...] = dx2 + _rms_bwd(dh2 * g2v, xn1, r2)

    row = lambda w: pl.BlockSpec((tm, w), lambda i: (i, 0))
    return pl.pallas_call(
        body, name="ffn_fwd_bwd", grid=(t // tm,),
        in_specs=[row(D_MODEL), row(D_MODEL), _const_spec((1, D_MODEL)), _const_spec((1, D_MODEL)),
                  _const_spec((D_FF, D_MODEL)), _const_spec((D_FF, D_MODEL)), _const_spec((D_FF, D_MODEL))],
        out_specs=(row(D_MODEL), row(D_MODEL), row(D_FF), row(D_FF), row(D_FF), row(D_MODEL),
                   _acc_spec((8, LANE)), _acc_spec((1, D_MODEL)), _acc_spec((1, D_MODEL))),
        out_shape=(jax.ShapeDtypeStruct((t, D_MODEL), F32), jax.ShapeDtypeStruct((t, D_MODEL), BF16),
                   jax.ShapeDtypeStruct((t, D_FF), BF16), jax.ShapeDtypeStruct((t, D_FF), BF16),
                   jax.ShapeDtypeStruct((t, D_FF), BF16), jax.ShapeDtypeStruct((t, D_MODEL), BF16),
                   jax.ShapeDtypeStruct((8, LANE), F32), jax.ShapeDtypeStruct((1, D_MODEL), F32),
                   jax.ShapeDtypeStruct((1, D_MODEL), F32)),
        compiler_params=_params(),
    )(x1, target, g2, gf, w_gate, w_up, w_down)


def _matmul_tn(a, b, tm, tk, name, comms=(), cols=None):
    t, m = a.shape
    cb, n = (0, b.shape[1]) if cols is None else cols

    def body(a_ref, b_ref, o_ref):
        @pl.when(pl.program_id(1) == 0)
        def _():
            o_ref[...] = jnp.zeros_like(o_ref)

        o_ref[...] += _mm_tn(a_ref[...], b_ref[...])

    (out,), comm_results = _fused_call(
        body, comms, name=name, grid=(m // tm, t // tk), inputs=(a, b),
        in_specs=[pl.BlockSpec((tk, tm), lambda j, k: (k, j)), pl.BlockSpec((tk, n), lambda j, k: (k, cb))],
        out_specs=(pl.BlockSpec((tm, n), lambda j, k: (j, 0)),),
        out_shape=(jax.ShapeDtypeStruct((m, n), F32),))
    return out, comm_results


def _in_proj_bwd(x, g1, dx1, dq_f, dq_b, dk_f, dk_b, dv_f, dv_b, dpg, dpu, dpv, dlr_f, dlr_b, w_main, tm, comms=()):
    t = x.shape[0]

    def body(x_ref, g_ref, dx1_ref, dqf, dqb, dkf, dkb, dvf, dvb, dg, du, dv, dlf, dlb, w_ref,
             dx_ref, dp_ref, dg1_ref):
        @pl.when(pl.program_id(0) == 0)
        def _():
            dg1_ref[...] = jnp.zeros_like(dg1_ref)

        both = lambda a, b: (a[...].astype(F32) + b[...].astype(F32)).astype(BF16)
        dp = jnp.concatenate([both(dqf, dqb), both(dkf, dkb), both(dvf, dvb), dg[...], du[...], dv[...],
                              both(dlf, dlb)], axis=1)
        dp_ref[...] = dp
        dh = sum(_mm(dp[:, c0:c0 + r1 - r0], w_ref[r0:r1, :]) for r0, r1, c0 in PROJ_ROWS)
        xv = x_ref[...]
        r = lax.rsqrt(jnp.mean(xv * xv, axis=-1, keepdims=True) + EPS)
        xn = xv * r
        dg1_ref[...] += jnp.sum(dh * xn, axis=0, keepdims=True)
        dx_ref[...] = dx1_ref[...] + _rms_bwd(dh * g_ref[...], xn, r)

    row = lambda w: pl.BlockSpec((tm, w), lambda i: (i, 0))
    return _fused_call(
        body, comms, name="in_proj_bwd", grid=(t // tm,),
        inputs=(x, g1, dx1, dq_f, dq_b, dk_f, dk_b, dv_f, dv_b, dpg, dpu, dpv, dlr_f, dlr_b, w_main),
        in_specs=[row(D_MODEL), _const_spec((1, D_MODEL)), row(D_MODEL), row(KEY_W), row(KEY_W), row(KEY_W),
                  row(KEY_W), row(VAL_W), row(VAL_W), row(VAL_W), row(VAL_W), row(VAL_W), row(LANE), row(LANE),
                  _const_spec((PROJ_W, D_MODEL))],
        out_specs=(row(D_MODEL), row(PROJ_PAD), _acc_spec((1, D_MODEL))),
        out_shape=(jax.ShapeDtypeStruct((t, D_MODEL), F32), jax.ShapeDtypeStruct((t, PROJ_PAD), BF16),
                   jax.ShapeDtypeStruct((1, D_MODEL), F32)))


def _adamw(w, g, m, v):
    m_new = ADAM_B1 * m + (1.0 - ADAM_B1) * g
    v_new = ADAM_B2 * v + (1.0 - ADAM_B2) * (g * g)
    m_hat = m_new / (1.0 - ADAM_B1 ** ADAM_STEP)
    v_hat = v_new / (1.0 - ADAM_B2 ** ADAM_STEP)
    delta = -ADAM_LR * (m_hat / (jnp.sqrt(v_hat) + ADAM_EPS) + ADAM_WD * w)
    return delta, m_new, v_new


def _adamw_shard(own, recv, w, m, v, tr, name, also=()):
    r, c = w.shape

    def body(own_ref, recv_ref, w_ref, m_ref, v_ref, *rest):
        g_ref, d_ref, nm_ref, nv_ref = rest[len(also):]
        g = own_ref[...]
        for k in range(3):
            g = g + recv_ref[k].astype(F32)
        g_ref[...] = g
        d_ref[...], nm_ref[...], nv_ref[...] = _adamw(w_ref[...], g, m_ref[...], v_ref[...])

    row = pl.BlockSpec((tr, c), lambda i: (i, 0))
    return pl.pallas_call(
        body, name=name, grid=(r // tr,),
        in_specs=[row, pl.BlockSpec((3, tr, c), lambda i: (0, i, 0)), row, row, row] + [
            pl.BlockSpec((1, 1, 8, a.shape[3]), lambda i: (0, 0, 0, 0)) for a in also],
        out_specs=(row,) * 4, out_shape=(jax.ShapeDtypeStruct((r, c), F32),) * 4,
        compiler_params=_params(),
    )(own, recv, w, m, v, *also)


def _adamw_small(entries):
    stacks = []
    for (g, _, _), _, _, _ in entries:
        if not any(g is s for s in stacks):
            stacks.append(g)
    where = [next(i for i, s in enumerate(stacks) if s is g) for (g, _, _), _, _, _ in entries]
    ns, ne = len(stacks), len(entries)

    def body(*refs):
        s_refs, wmv, outs = refs[:ns], refs[ns:ns + 3 * ne], refs[ns + 3 * ne:]
        for e, ((_, r0, nr), _, _, _) in enumerate(entries):
            grad = s_refs[where[e]][r0:r0 + nr, :]
            w_ref, m_ref, v_ref = wmv[3 * e:3 * e + 3]
            g_ref, d_ref, nm_ref, nv_ref = outs[4 * e:4 * e + 4]
            g_ref[...] = grad
            d_ref[...], nm_ref[...], nv_ref[...] = _adamw(w_ref[...], grad, m_ref[...], v_ref[...])

    results = pl.pallas_call(
        body, name="adamw_small",
        out_shape=tuple(jax.ShapeDtypeStruct(w.shape, F32) for _, w, _, _ in entries for _ in range(4)),
        compiler_params=pltpu.CompilerParams(vmem_limit_bytes=VMEM_LIMIT),
    )(*stacks, *[a for _, w, m, v in entries for a in (w, m, v)])
    return [results[4 * e:4 * e + 4] for e in range(ne)]


def _mesh_pos():
    return lax.axis_index("x"), lax.axis_index("y"), lax.axis_index("c")


def _other_chips(x, y):
    return [(x, 1 - y), (1 - x, y), (1 - x, 1 - y)]


_VMEM_WHOLE = pl.BlockSpec(memory_space=pltpu.VMEM)
_HBM_WHOLE = pl.BlockSpec(memory_space=pl.ANY)


def _gather_comm(shards, cast, mid=((1, 2), (3, 4))):
    na = len(shards)
    staged = [a for a in range(na) if cast[a]]

    def phases(in_refs, out_refs, scr):
        stage = dict(zip(staged, scr[:len(staged)]))
        send_sems, recv_sems, local_sems = scr[len(staged):]
        x, y, c = _mesh_pos()
        me, sibling = (x, y, c), (x, y, 1 - c)
        chip_a, chip_b, diagonal = (x ^ c, y ^ (1 - c)), (x ^ (1 - c), y ^ c), (1 - x, 1 - y)
        srcs = [stage[a] if cast[a] else in_refs[a] for a in range(na)]

        def rows(a, pos):
            px, py, pc = pos
            return out_refs[a].at[4 * px + 2 * py + pc]

        def copy(a, k, block, to, src=None):
            return pltpu.make_async_remote_copy(
                src_ref=rows(a, block) if src is None else src, dst_ref=rows(a, block),
                send_sem=send_sems.at[a, k], recv_sem=recv_sems.at[a, k], device_id=to, device_id_type=MESH_ID)

        mine = [pltpu.make_async_copy(srcs[a], rows(a, me), local_sems.at[a]) for a in range(na)]
        own = [copy(a, k, me, to, src=srcs[a]) for a in range(na)
               for k, to in ((0, sibling), (1, (*chip_a, c)), (2, (*chip_b, c)))]
        onward = [copy(a, 3, (*chip_a, c), (*chip_b, c)) for a in range(na)]
        to_sibling = {k: [copy(a, k, (*chip, c), sibling) for a in range(na)]
                      for k, chip in ((4, chip_a), (5, chip_b), (6, diagonal))}

        def start():
            for a in staged:
                stage[a][...] = in_refs[a][...].astype(BF16)
            for cp in mine + own:
                cp.start()

        def forward_neighbours():
            for a in range(na):
                copy(a, 1, (*chip_a, c), me).wait_recv()
                onward[a].start()
                to_sibling[4][a].start()
            for a in range(na):
                copy(a, 2, (*chip_b, c), me).wait_recv()
                to_sibling[5][a].start()

        def forward_diagonal():
            for a in range(na):
                copy(a, 3, (*diagonal, c), me).wait_recv()
                to_sibling[6][a].start()

        def finish():
            for a in range(na):
                for k, chip in ((0, (x, y)), (4, chip_b), (5, chip_a), (6, diagonal)):
                    copy(a, k, (*chip, 1 - c), me).wait_recv()
            for cp in own + onward + to_sibling[4] + to_sibling[5] + to_sibling[6]:
                cp.wait_send()
            for cp in mine:
                cp.wait()

        return start, forward_neighbours, forward_diagonal, finish

    def before(step, nsteps, in_refs, out_refs, scr):
        start, forward_neighbours, forward_diagonal, _ = phases(in_refs, out_refs, scr)
        pl.when(step == 0)(start)
        pl.when(step == nsteps * mid[0][0] // mid[0][1])(forward_neighbours)
        pl.when(step == nsteps * mid[1][0] // mid[1][1])(forward_diagonal)

    def after(step, nsteps, in_refs, out_refs, scr):
        pl.when(step == nsteps - 1)(phases(in_refs, out_refs, scr)[3])

    return _Comm(
        inputs=list(shards), in_specs=[_VMEM_WHOLE] * na,
        out_shape=[jax.ShapeDtypeStruct((N_DEV,) + s.shape, BF16 if cast[a] else s.dtype)
                   for a, s in enumerate(shards)],
        out_specs=[_HBM_WHOLE] * na,
        scratch_shapes=[pltpu.VMEM(shards[a].shape, BF16) for a in staged] + [
            pltpu.SemaphoreType.DMA((na, 7)), pltpu.SemaphoreType.DMA((na, 7)), pltpu.SemaphoreType.DMA((na,))],
        before=before, after=after)


def _reduce_scatter_comm(grads):
    _, _, r, c = grads.shape
    order = (3, 1, 2, 0)

    def sibling_copies(in_refs, scr):
        (g_ref,), (sib, own, _, sems) = in_refs, scr
        x, y, core = _mesh_pos()
        mine = 2 * x + y
        remote = {k: pltpu.make_async_remote_copy(
            src_ref=g_ref.at[mine ^ k, 1 - core], dst_ref=sib.at[k], send_sem=sems.at[0, k], recv_sem=sems.at[1, k],
            device_id=(x, y, 1 - core), device_id_type=MESH_ID) for k in order}
        local = {k: pltpu.make_async_copy(g_ref.at[mine ^ k, core], own.at[k], sems.at[2, k]) for k in order}
        return remote, local

    def chip_copies(out_refs, scr):
        (_, recv), (_, _, part, sems) = out_refs, scr
        x, y, core = _mesh_pos()
        return [pltpu.make_async_remote_copy(
            src_ref=part.at[j], dst_ref=recv.at[j], send_sem=sems.at[3, j], recv_sem=sems.at[4, j],
            device_id=(*chip, core), device_id_type=MESH_ID) for j, chip in enumerate(_other_chips(x, y))]

    def before(step, nsteps, in_refs, out_refs, scr):
        @pl.when(step == 0)
        def _():
            remote, local = sibling_copies(in_refs, scr)
            for k in order:
                remote[k].start()
                local[k].start()

    def after(step, nsteps, in_refs, out_refs, scr):
        sib, own, part, _ = scr

        @pl.when(step == (nsteps - 1) // 2)
        def _():
            remote, local = sibling_copies(in_refs, scr)
            to_chips = chip_copies(out_refs, scr)
            for k in order:
                remote[k].wait_recv()
                local[k].wait()
                if k:
                    part[k - 1] = (own[k] + sib[k]).astype(BF16)
                    to_chips[k - 1].start()
                else:
                    out_refs[0][...] = own[0] + sib[0]
            for k in order:
                remote[k].wait_send()

        @pl.when(step == nsteps - 1)
        def _():
            for cp in chip_copies(out_refs, scr):
                cp.wait()

    return _Comm(inputs=[grads], in_specs=[_HBM_WHOLE],
                 out_shape=[jax.ShapeDtypeStruct((r, c), F32), jax.ShapeDtypeStruct((3, r, c), BF16)],
                 out_specs=[_VMEM_WHOLE, _HBM_WHOLE],
                 scratch_shapes=[pltpu.VMEM((4, r, c), F32), pltpu.VMEM((4, r, c), F32), pltpu.VMEM((3, r, c), BF16),
                                 pltpu.SemaphoreType.DMA((5, 4))],
                 before=before, after=after)


def _comm_only(comms, name):
    return _fused_call(lambda: None, comms, name=name, grid=(1,), inputs=(), in_specs=[], out_specs=(),
                       out_shape=())[1]


def _all_reduce_small_comm(parts):
    na = len(parts)

    def copies(in_refs, scr):
        gathered, (send_sems, recv_sems) = scr[:na], scr[na:]
        x, y, c = _mesh_pos()
        my_id = 4 * x + 2 * y + c
        return my_id, [pltpu.make_async_remote_copy(
            src_ref=in_refs[a], dst_ref=gathered[a].at[my_id], send_sem=send_sems.at[a, k - 1],
            recv_sem=recv_sems.at[a, k - 1], device_id=(x ^ (k >> 2), y ^ ((k >> 1) & 1), c ^ (k & 1)),
            device_id_type=MESH_ID) for a in range(na) for k in range(1, N_DEV)]

    def before(step, nsteps, in_refs, out_refs, scr):
        @pl.when(step == 0)
        def _():
            for cp in copies(in_refs, scr)[1]:
                cp.start()

    def after(step, nsteps, in_refs, out_refs, scr):
        @pl.when(step == nsteps - 1)
        def _():
            my_id, cps = copies(in_refs, scr)
            for a in range(na):
                scr[a][my_id] = in_refs[a][...]
            for cp in cps:
                cp.wait()
            for a in range(na):
                acc = scr[a][0]
                for d in range(1, N_DEV):
                    acc = acc + scr[a][d]
                out_refs[a][...] = acc

    return _Comm(inputs=list(parts), in_specs=[_VMEM_WHOLE] * na,
                 out_shape=[jax.ShapeDtypeStruct(p.shape, F32) for p in parts], out_specs=[_VMEM_WHOLE] * na,
                 scratch_shapes=[pltpu.VMEM((N_DEV,) + p.shape, F32) for p in parts] + [
                     pltpu.SemaphoreType.DMA((na, N_DEV - 1)), pltpu.SemaphoreType.DMA((na, N_DEV - 1))],
                 before=before, after=after)


def _unshard_cols(g):
    return jnp.transpose(g, (1, 0, 2)).reshape(g.shape[1], N_DEV * g.shape[2])


def _row_blocks(w):
    return w.reshape(4, 2, w.shape[0] // N_DEV, w.shape[1])


def _stack_rows(parts):
    a = jnp.concatenate(parts, axis=0)
    return jnp.pad(a, ((0, (-a.shape[0]) % 8), (0, 0)))


def _w_in_grad_blocks(dw):
    return _row_blocks(jnp.concatenate([dw[:LR_REF], dw[LR_COL:LR_COL + 2 * LOWRANK], dw[LR_REF:LR_COL]], axis=0))


def _padded_decay_weights(wd_f, wd_b):
    zeros = lambda n: jnp.zeros((n, KEY_W), F32)
    return (jnp.concatenate([wd_f, zeros(LANE - LOWRANK)], axis=0),
            jnp.concatenate([zeros(LOWRANK), wd_b, zeros(LANE - 2 * LOWRANK)], axis=0))


def kernel(x, norm1_g, w_in,w_decay_f, b_decay_f, w_decay_b, b_decay_b, gla_norm_g, gmlp_ln_g, gmlp_ln_b, w_spatial, b_spatial, w_out, norm2_g, w_gate, w_up, w_down, final_norm_g, loss_target, m_norm1_g, m_w_in, m_w_decay_f, m_b_decay_f, m_w_decay_b, m_b_decay_b, m_gla_norm_g, m_gmlp_ln_g, m_gmlp_ln_b, m_w_spatial, m_b_spatial, m_w_out, m_norm2_g, m_w_gate, m_w_up, m_w_down, m_final_norm_g, v_norm1_g, v_w_in, v_w_decay_f, v_b_decay_f, v_w_decay_b, v_b_decay_b, v_gla_norm_g, v_gmlp_ln_g, v_gmlp_ln_b, v_w_spatial, v_b_spatial, v_w_out, v_norm2_g, v_w_gate, v_w_up, v_w_down, v_final_norm_g):
    t = x.shape[1]
    xt = x[0]
    target = loss_target[0]
    pos_x, pos_y, pos_c = _mesh_pos()
    my_id = 4 * pos_x + 2 * pos_y + pos_c

    tile = lambda n: min(n, t)
    ln_g, ln_b, w_sp = gmlp_ln_g, gmlp_ln_b, w_spatial[0]
    b_sp_col = b_spatial[0][:, :, None]
    shard = {"w_in": w_in[0].T, "w_out": w_out[0], "w_gate": w_gate[0].T, "w_up": w_up[0].T, "w_down": w_down[0]}
    shard_m = {"w_in": m_w_in[0].T, "w_out": m_w_out[0], "w_gate": m_w_gate[0].T, "w_up": m_w_up[0].T,
               "w_down": m_w_down[0]}
    shard_v = {"w_in": v_w_in[0].T, "w_out": v_w_out[0], "w_gate": v_w_gate[0].T, "w_up": v_w_up[0].T,
               "w_down": v_w_down[0]}
    transposed = ("w_in", "w_gate", "w_up")

    decay_shard = jnp.stack([w_decay_f[0], w_decay_b[0]])
    (hb,), ((g_in, g_decay),) = _norm1(xt, norm1_g, tile(TOKEN_TILE["norm1"]),
                                       [_gather_comm([shard["w_in"], decay_shard], [True, False])])
    w_in_t = g_in.reshape(PROJ_W, D_MODEL)
    wd_pad_f, wd_pad_b = _padded_decay_weights(_unshard_cols(g_decay[:, 0]), _unshard_cols(g_decay[:, 1]))
    (p,), ((g_gate, g_out),) = _in_proj(
        hb, w_in_t, tile(TOKEN_TILE["in_proj"]), [_gather_comm([shard["w_gate"], shard["w_out"]], [True, True])])
    (o_f, st_f, o_b, st_b), ((g_up,),) = _gla_fwd(
        p, wd_pad_f, b_decay_f, wd_pad_b, b_decay_b, tile(TOKEN_TILE["gla"]), [_gather_comm([shard["w_up"]], [True])])
    w_out_full = g_out.reshape(D_MODEL, D_MODEL)
    (x1, ycat), ((g_down,),) = _mix_fwd(xt, o_f, o_b, p, gla_norm_g, ln_g, ln_b, w_sp, b_sp_col, w_out_full,
                                        tile(TOKEN_TILE["mix_fwd"]), [_gather_comm([shard["w_down"]], [True])])

    dx1, h2b, dgate, dup, act, dx2, loss_acc, d_gf, d_g2 = _ffn(
        x1, target, norm2_g, final_norm_g[None, :], g_gate.reshape(D_FF, D_MODEL), g_up.reshape(D_FF, D_MODEL),
        g_down.reshape(D_FF, D_MODEL), tile(TOKEN_TILE["ffn"]))
    reduced = {}
    dw_gate, _ = _matmul_tn(dgate, h2b, D_FF // 2, tile(TOKEN_TILE["dw"]), "grad_w_gate")
    dw_up, (reduced["w_gate"],) = _matmul_tn(dup, h2b, D_FF // 2, tile(TOKEN_TILE["dw_host"]), "grad_w_up",
                                             [_reduce_scatter_comm(_row_blocks(dw_gate))])
    dw_down, (reduced["w_up"],) = _matmul_tn(act, dx2, D_FF // 2, tile(TOKEN_TILE["dw_host"]), "grad_w_down",
                                             [_reduce_scatter_comm(_row_blocks(dw_up))])

    (d_o, dpg, dpu, dpv, dw_out, d_gg, d_lg, d_lb, dw_sp, db_sp), (reduced["w_down"],) = _mix_bwd(
        dx1, ycat, o_f, o_b, p, gla_norm_g, ln_g, ln_b, w_sp, b_sp_col, w_out_full,
        tile(TOKEN_TILE["mix_bwd"]), [_reduce_scatter_comm(_row_blocks(dw_down))])
    (dq_f, dk_f, dv_f, dlr_f, dwd_f, dbd_f, dq_b, dk_b, dv_b, dlr_b, dwd_b, dbd_b), (reduced["w_out"],) = _gla_bwd(
        p, wd_pad_f, b_decay_f, wd_pad_b, b_decay_b, st_f, st_b, d_o, tile(TOKEN_TILE["gla"]),
        [_reduce_scatter_comm(_row_blocks(dw_out))])
    (grad_x, dp, d_g1), _ = _in_proj_bwd(
        xt, norm1_g, dx1, dq_f, dq_b, dk_f, dk_b, dv_f, dv_b, dpg, dpu, dpv, dlr_f, dlr_b, w_in_t,
        tile(TOKEN_TILE["in_proj_bwd"]))

    stacks = [_stack_rows([d_g1, d_g2, d_gf]), _stack_rows([d_gg, d_lg, d_lb]),
              _stack_rows([dbd_f, dbd_b, jnp.zeros((DECAY_W_ROW - 2, KEY_W), F32), dwd_f[:LOWRANK],
                           dwd_b[LOWRANK:2 * LOWRANK]]),
              _stack_rows([dw_sp.reshape(GMLP_W, GMLP_CHUNK), db_sp[:, :, 0], loss_acc[:1]])]
    dw_main, (small_sums,) = _matmul_tn(dp, hb, PROJ_PAD // 3, tile(TOKEN_TILE["dw"]), "grad_w_in",
                                        [_all_reduce_small_comm(stacks)])
    in_grad = _w_in_grad_blocks(dw_main)
    (reduced["w_in"],) = _comm_only([_reduce_scatter_comm(in_grad)], "grad_w_in_reduce_scatter")

    big_out = {}
    for n, (own_sum, recv) in reduced.items():
        rows = shard[n].shape[0]
        half = rows // 2 if rows % 32 == 0 else rows
        res = _adamw_shard(own_sum, recv, shard[n], shard_m[n], shard_v[n], half, "adamw_" + n,
                           also=(in_grad,) if n == "w_in" else ())
        big_out[n] = [r.T if n in transposed else r for r in res]

    s1024, s512, s256, s128 = small_sums
    loss = s128[GMLP_W + GMLP_GROUPS, 0]
    col0 = my_id * (KEY_W // N_DEV)
    decay_cols = lambda row0: lax.dynamic_slice(s256, (row0, col0), (LOWRANK, KEY_W // N_DEV))
    flat = lambda a: a.reshape(-1, a.shape[-1])
    small = {
        "norm1_g": ((s1024, 0, 1), norm1_g, m_norm1_g, v_norm1_g),
        "w_decay_f": ((decay_cols(DECAY_W_ROW), 0, LOWRANK), w_decay_f, m_w_decay_f, v_w_decay_f),
        "b_decay_f": ((s256, 0, 1), b_decay_f, m_b_decay_f, v_b_decay_f),
        "w_decay_b": ((decay_cols(DECAY_W_ROW + LOWRANK), 0, LOWRANK), w_decay_b, m_w_decay_b, v_w_decay_b),
        "b_decay_b": ((s256, 1, 1), b_decay_b, m_b_decay_b, v_b_decay_b),
        "gla_norm_g": ((s512, 0, 1), gla_norm_g, m_gla_norm_g, v_gla_norm_g),
        "gmlp_ln_g": ((s512, 1, 1), gmlp_ln_g, m_gmlp_ln_g, v_gmlp_ln_g),
        "gmlp_ln_b": ((s512, 2, 1), gmlp_ln_b, m_gmlp_ln_b, v_gmlp_ln_b),
        "w_spatial": ((s128, 0, GMLP_W), w_spatial, m_w_spatial, v_w_spatial),
        "b_spatial": ((s128, GMLP_W, GMLP_GROUPS), b_spatial, m_b_spatial, v_b_spatial),
        "norm2_g": ((s1024, 1, 1), norm2_g, m_norm2_g, v_norm2_g),
        "final_norm_g": ((s1024, 2, 1), final_norm_g, m_final_norm_g, v_final_norm_g),
    }
    small_res = _adamw_small([(g, flat(w), flat(m), flat(v)) for g, w, m, v in small.values()])
    small_out = {n: [r.reshape(small[n][1].shape) for r in res] for n, res in zip(small, small_res)}

    order = ["norm1_g", "w_in", "w_decay_f", "b_decay_f", "w_decay_b", "b_decay_b", "gla_norm_g", "gmlp_ln_g",
             "gmlp_ln_b", "w_spatial", "b_spatial", "w_out", "norm2_g", "w_gate", "w_up", "w_down", "final_norm_g"]
    outs = []
    for kind in range(4):
        for n in order:
            outs.append(big_out[n][kind][None] if n in big_out else small_out[n][kind])
    return (loss, grad_x[None], *outs)
```

```python
import functools
import math

import jax
import jax.numpy as jnp
from jax import lax
from jax.experimental import pallas as pl
from jax.experimental.pallas import tpu as pltpu

F32 = jnp.float32
BF16 = jnp.bfloat16

D_MODEL = 1024
GLA_HEADS = 4
GLA_DK = 64
GLA_DV = 128
KEY_W = GLA_HEADS * GLA_DK
VAL_W = GLA_HEADS * GLA_DV
LOWRANK = 16
GLA_TAU = 16.0
GLA_CHUNK = 64
GMLP_W = 512
GMLP_GROUPS = 4
GMLP_CHUNK = 128
D_FF = 2816
EPS = 1e-6
Q_SCALE = GLA_DK ** -0.5
PROJ_PAD = 2688
LR_COL = 2560
LANE = 128
N_DEV = 8

ADAM_LR = 0.001
ADAM_B1 = 0.9
ADAM_B2 = 0.999
ADAM_EPS = 1e-08
ADAM_WD = 0.01
ADAM_STEP = 10

VMEM_LIMIT = 56 * 1024 * 1024
TOKEN_TILE = {"norm1": 512, "in_proj": 512, "gla": 1024, "mix_fwd": 1024, "ffn": 256, "mix_bwd": 512,
              "in_proj_bwd": 512, "dw": 2048, "dw_host": 2048}
DECAY_W_ROW = 8
MESH_ID = pl.DeviceIdType.MESH
INV_SQRT2 = 0.7071067811865476
INV_SQRT_2PI = 0.3989422804014327


def _params(n_axes=1):
    return pltpu.CompilerParams(dimension_semantics=("arbitrary",) * n_axes, vmem_limit_bytes=VMEM_LIMIT)


def _mm(a, b):
    return jnp.dot(a.astype(BF16), b.astype(BF16), preferred_element_type=F32)


def _mm_nt(a, b):
    return lax.dot_general(a.astype(BF16), b.astype(BF16), (((1,), (1,)), ((), ())), preferred_element_type=F32)


def _mm_tn(a, b):
    return lax.dot_general(a.astype(BF16), b.astype(BF16), (((0,), (0,)), ((), ())), preferred_element_type=F32)


def _const_spec(shape):
    nd = len(shape)
    return pl.BlockSpec(shape, lambda *_: (0,) * nd, pipeline_mode=pl.Buffered(1))


def _acc_spec(shape):
    nd = len(shape)
    return pl.BlockSpec(shape, lambda *_: (0,) * nd)


class _Comm:
    def __init__(self, inputs, in_specs, out_shape, out_specs, scratch_shapes, before, after):
        self.inputs, self.in_specs, self.out_shape, self.out_specs = inputs, in_specs, out_shape, out_specs
        self.scratch_shapes, self.before, self.after = scratch_shapes, before, after


def _fused_call(body, comms, *, name, grid, inputs, in_specs, out_specs, out_shape, scratch_shapes=(), prefetch=()):
    n_pre, n_in, n_out, n_scr = len(prefetch), len(in_specs), len(out_specs), len(scratch_shapes)
    nsteps = math.prod(grid)
    sizes = [(len(c.inputs), len(c.out_shape), len(c.scratch_shapes)) for c in comms]

    def full_body(*refs):
        step = pl.program_id(0)
        for axis in range(1, len(grid)):
            step = step * grid[axis] + pl.program_id(axis)
        pre, refs = refs[:n_pre], refs[n_pre:]
        ins, rest = refs[:n_in], refs[n_in:]
        c_ins = []
        for ci, _, _ in sizes:
            c_ins.append(rest[:ci])
            rest = rest[ci:]
        outs, rest = rest[:n_out], rest[n_out:]
        c_outs = []
        for _, co, _ in sizes:
            c_outs.append(rest[:co])
            rest = rest[co:]
        scr, rest = rest[:n_scr], rest[n_scr:]
        c_scr = []
        for _, _, cs in sizes:
            c_scr.append(rest[:cs])
            rest = rest[cs:]
        for c, a, b, s in zip(comms, c_ins, c_outs, c_scr):
            c.before(step, nsteps, a, b, s)
        body(*pre, *ins, *outs, *scr)
        for c, a, b, s in zip(comms, c_ins, c_outs, c_scr):
            c.after(step, nsteps, a, b, s)

    specs = dict(
        grid=grid, in_specs=list(in_specs) + [s for c in comms for s in c.in_specs],
        out_specs=tuple(out_specs) + tuple(s for c in comms for s in c.out_specs),
        scratch_shapes=list(scratch_shapes) + [s for c in comms for s in c.scratch_shapes])
    if n_pre:
        specs = dict(grid_spec=pltpu.PrefetchScalarGridSpec(num_scalar_prefetch=n_pre, **specs))
    results = pl.pallas_call(
        full_body, name=name, **specs,
        out_shape=tuple(out_shape) + tuple(s for c in comms for s in c.out_shape),
        compiler_params=_params(len(grid)),
    )(*prefetch, *inputs, *[a for c in comms for a in c.inputs])
    own, rest = results[:n_out], results[n_out:]
    comm_results = []
    for _, co, _ in sizes:
        comm_results.append(rest[:co])
        rest = rest[co:]
    return own, comm_results


def _gelu(x):
    return 0.5 * x * (1.0 + lax.erf(x * INV_SQRT2))


def _gelu_and_grad(x):
    cdf = 0.5 * (1.0 + lax.erf(x * INV_SQRT2))
    return x * cdf, cdf + x * jnp.exp(-0.5 * x * x) * INV_SQRT_2PI


def _sigmoid(x):
    return 0.5 + 0.5 * jnp.tanh(0.5 * x)


def _silu_and_grad(x):
    s = _sigmoid(x)
    return x * s, s * (1.0 + x * (1.0 - s))


def _norm1(x, g1, tm, comms=()):
    t = x.shape[0]

    def body(x_ref, g_ref, h_ref):
        xv = x_ref[...]
        r = lax.rsqrt(jnp.mean(xv * xv, axis=-1, keepdims=True) + EPS)
        h_ref[...] = (xv * r * g_ref[...]).astype(BF16)

    row = pl.BlockSpec((tm, D_MODEL), lambda i: (i, 0))
    return _fused_call(body, comms, name="norm1", grid=(t // tm,), inputs=(x, g1),
                       in_specs=[row, _const_spec((1, D_MODEL))], out_specs=(row,),
                       out_shape=(jax.ShapeDtypeStruct((t, D_MODEL), BF16),))


PROJ_W = 2592
LR_REF = 1536
PROJ_ROWS = ((0, LR_REF, 0), (LR_REF + 2 * LOWRANK, PROJ_W, LR_REF), (LR_REF, LR_REF + LANE, LR_COL))


def _in_proj(h, w_in_t, tm, comms=()):
    t = h.shape[0]

    def body(h_ref, w_ref, p_ref):
        hv = h_ref[...]
        for r0, r1, c0 in PROJ_ROWS:
            p_ref[:, c0:c0 + r1 - r0] = _mm_nt(hv, w_ref[r0:r1, :]).astype(BF16)

    return _fused_call(
        body, comms, name="in_proj", grid=(t // tm,), inputs=(h, w_in_t),
        in_specs=[pl.BlockSpec((tm, D_MODEL), lambda i: (i, 0)), _const_spec((PROJ_W, D_MODEL))],
        out_specs=(pl.BlockSpec((tm, PROJ_PAD), lambda i: (i, 0)),),
        out_shape=(jax.ShapeDtypeStruct((t, PROJ_PAD), BF16),))


def _tri(upper):
    r = lax.broadcasted_iota(jnp.int32, (GLA_CHUNK, GLA_CHUNK), 0)
    c = lax.broadcasted_iota(jnp.int32, (GLA_CHUNK, GLA_CHUNK), 1)
    return jnp.where((c >= r) if upper else (c <= r), 1.0, 0.0).astype(BF16)


def _chunk_cumsum(tri, a, add=None):
    hi = a.astype(BF16)
    lo = (a - hi.astype(F32)).astype(BF16)
    dot = functools.partial(jnp.dot, preferred_element_type=F32)
    sums = [dot(tri, hi[_chunk_rows(c)]) + dot(tri, lo[_chunk_rows(c)]) for c in range(a.shape[0] // GLA_CHUNK)]
    return jnp.concatenate(sums if add is None else [s + r for s, r in zip(sums, add)], axis=0)


def _chunk_rows(c):
    return slice(c * GLA_CHUNK, (c + 1) * GLA_CHUNK)


def _gla_masks(rev):
    dk_bits, dv_bits = GLA_DK.bit_length() - 1, GLA_DV.bit_length() - 1
    key_head = lax.broadcasted_iota(jnp.int32, (GLA_CHUNK, KEY_W), 1) >> dk_bits
    val_head = lax.broadcasted_iota(jnp.int32, (GLA_CHUNK, VAL_W), 1) >> dv_bits
    t = lax.broadcasted_iota(jnp.int32, (GLA_HEADS * GLA_CHUNK, GLA_CHUNK), 0) & (GLA_CHUNK - 1)
    s = lax.broadcasted_iota(jnp.int32, (GLA_HEADS * GLA_CHUNK, GLA_CHUNK), 1)
    return key_head, val_head, (s >= t) if rev else (s <= t)


def _stack_heads(a, head_of_lane):
    a = a.astype(BF16)
    return jnp.concatenate([jnp.where(head_of_lane == h, a, jnp.zeros_like(a)) for h in range(GLA_HEADS)], axis=0)


def _rows_by_head(a):
    return jnp.concatenate([a[:, h * GLA_DV:(h + 1) * GLA_DV] for h in range(GLA_HEADS)], axis=0)


def _lanes_by_head(r):
    return jnp.concatenate([r[h * GLA_CHUNK:(h + 1) * GLA_CHUNK] for h in range(GLA_HEADS)], axis=1)


def _head_diagonal(r, head_of_lane):
    rows = r.shape[0] // GLA_HEADS
    out = jnp.where(head_of_lane == 0, r[:rows], 0.0)
    for h in range(1, GLA_HEADS):
        out = out + jnp.where(head_of_lane == h, r[h * rows:(h + 1) * rows], 0.0)
    return out


def _tile_terms(la, q, k, tri, rev):
    nc = la.shape[0] // GLA_CHUNK
    q, k = q.astype(F32), k.astype(F32)
    b = _chunk_cumsum(tri, la)
    ebl = [jnp.exp(b[c * GLA_CHUNK:c * GLA_CHUNK + 1] if rev else b[(c + 1) * GLA_CHUNK - 1:(c + 1) * GLA_CHUNK])
           for c in range(nc)]
    eb = jnp.exp(b)
    enb = jnp.exp(-b)
    kd = k * enb
    ke = jnp.concatenate([kd[_chunk_rows(c)] * ebl[c] for c in range(nc)], axis=0)
    return ebl, eb, enb, q * Q_SCALE * eb, kd, ke


def _log_decay(lr_ref, wd_ref, bd_ref):
    z = _mm(lr_ref[...], wd_ref[...]) + bd_ref[...]
    return z, jax.nn.log_sigmoid(z) * (1.0 / GLA_TAU)


def _p_specs(tg, tile):
    return [pl.BlockSpec((tg, KEY_W), lambda i: (tile(i), 0)),
            pl.BlockSpec((tg, KEY_W), lambda i: (tile(i), 1)),
            pl.BlockSpec((tg, VAL_W), lambda i: (tile(i), 1)),
            pl.BlockSpec((tg, LANE), lambda i: (tile(i), LR_COL // LANE))]


def _gla_fwd_dir(rev, nc, q_ref, k_ref, v_ref, lr_ref, wd_ref, bd_ref, o_ref, st_ref, state):
    key_head, _, causal = _gla_masks(rev)
    order = range(nc - 1, -1, -1) if rev else range(nc)

    def intra():
        _, la = _log_decay(lr_ref, wd_ref, bd_ref)
        ebl, _, _, qd, kd, ke = _tile_terms(la, q_ref[...], k_ref[...], _tri(rev), rev)
        kd = kd.astype(BF16)
        v = {c: v_ref[_chunk_rows(c), :].astype(BF16) for c in order}
        qd_stack = {c: _stack_heads(qd[_chunk_rows(c)], key_head) for c in order}
        ke_stack = {c: _stack_heads(ke[_chunk_rows(c)], key_head) for c in order}
        a_all = {c: _mm_nt(qd_stack[c], kd[_chunk_rows(c)]) for c in order}
        a_all = {c: jnp.where(causal, a_all[c], 0.0).astype(BF16) for c in order}
        head_rows = lambda a, h: a[h * GLA_CHUNK:(h + 1) * GLA_CHUNK]
        head_vals = lambda a, h: a[:, h * GLA_DV:(h + 1) * GLA_DV]
        r = {c: [_mm(head_rows(a_all[c], h), head_vals(v[c], h)) for h in range(GLA_HEADS)] for c in order}
        upd = {c: _mm_tn(_rows_by_head(v[c]), ke_stack[c]) for c in order}
        return {c: (ebl[c], qd_stack[c], r[c], upd[c]) for c in order}

    def scan(terms):
        st = state[...]
        states = {}
        for c in order:
            states[c] = st
            st_ref[c] = st.astype(BF16)
            st = st * terms[c][0] + terms[c][3]
        state[...] = st
        return states

    def inter(terms, states):
        r_inter = {c: _mm_nt(terms[c][1], states[c]) for c in order}
        for c in order:
            o_ref[_chunk_rows(c), :] = jnp.concatenate(
                [terms[c][2][h] + r_inter[c][h * GLA_CHUNK:(h + 1) * GLA_CHUNK] for h in range(GLA_HEADS)], axis=1)

    return intra, scan, inter


def _gla_fwd(p, wd_pad_f, bd_f, wd_pad_b, bd_b, tg, comms=()):
    t = p.shape[0]
    nt = t // tg
    nc = tg // GLA_CHUNK
    up, down = (lambda i: i), (lambda i: nt - 1 - i)

    def body(qf, kf, vf, lrf, qb, kb, vb, lrb, wdf, bdf, wdb, bdb, of, stf, ob, stb, state_f, state_b):
        @pl.when(pl.program_id(0) == 0)
        def _():
            state_f[...] = jnp.zeros_like(state_f)
            state_b[...] = jnp.zeros_like(state_b)

        dirs = [_gla_fwd_dir(False, nc, qf, kf, vf, lrf, wdf, bdf, of, stf, state_f),
                _gla_fwd_dir(True, nc, qb, kb, vb, lrb, wdb, bdb, ob, stb, state_b)]
        terms = [intra() for intra, _, _ in dirs]
        states = [scan(t) for (_, scan, _), t in zip(dirs, terms)]
        for (_, _, inter), t, s in zip(dirs, terms, states):
            inter(t, s)

    wd_spec, bd_spec = _const_spec((LANE, KEY_W)), _const_spec((1, KEY_W))
    outs = lambda tile: (pl.BlockSpec((tg, VAL_W), lambda i: (tile(i), 0)),
                         pl.BlockSpec((nc, GLA_DV, KEY_W), lambda i: (tile(i), 0, 0)))
    out_shape = (jax.ShapeDtypeStruct((t, VAL_W), F32), jax.ShapeDtypeStruct((t // GLA_CHUNK, GLA_DV, KEY_W), BF16))
    return _fused_call(
        body, comms, name="gla_fwd", grid=(nt,), inputs=(p,) * 8 + (wd_pad_f, bd_f, wd_pad_b, bd_b),
        in_specs=_p_specs(tg, up) + _p_specs(tg, down) + [wd_spec, bd_spec, wd_spec, bd_spec],
        out_specs=outs(up) + outs(down), out_shape=out_shape * 2,
        scratch_shapes=[pltpu.VMEM((GLA_DV, KEY_W), F32)] * 2)


def _gla_bwd_dir(rev, nc, q_ref, k_ref, v_ref, lr_ref, wd_ref, bd_ref, st_ref, do_ref,
                 dq_ref, dk_ref, dv_ref, dlr_ref, dwd_ref, dbd_ref, dstate):
    key_head, val_head, causal = _gla_masks(rev)
    order = range(nc) if rev else range(nc - 1, -1, -1)

    def intra():
        z, la = _log_decay(lr_ref, wd_ref, bd_ref)
        tile = _tile_terms(la, q_ref[...], k_ref[...], _tri(rev), rev)
        qd, kd = tile[3], tile[4].astype(BF16)
        v = {c: v_ref[_chunk_rows(c), :].astype(BF16) for c in order}
        d_o = {c: do_ref[_chunk_rows(c), :] for c in order}
        kd_c = {c: kd[_chunk_rows(c)] for c in order}
        qd_stack = {c: _stack_heads(qd[_chunk_rows(c)], key_head) for c in order}
        do_stack = {c: _stack_heads(d_o[c], val_head) for c in order}
        do_rows = {c: _rows_by_head(d_o[c]) for c in order}
        a_all = {c: _mm_nt(qd_stack[c], kd_c[c]) for c in order}
        head_vals = lambda a, h: a[:, h * GLA_DV:(h + 1) * GLA_DV]
        da_all = {c: jnp.concatenate([_mm_nt(head_vals(d_o[c], h), head_vals(v[c], h)) for h in range(GLA_HEADS)],
                                     axis=0) for c in order}
        a_all = {c: jnp.where(causal, a_all[c], 0.0).astype(BF16) for c in order}
        da_all = {c: jnp.where(causal, da_all[c], 0.0).astype(BF16) for c in order}
        dv = {c: _mm_tn(a_all[c], do_stack[c]) for c in order}
        dqd = {c: _mm(jnp.concatenate([do_rows[c], da_all[c]], axis=1),
                      jnp.concatenate([st_ref[c], kd_c[c]], axis=0)) for c in order}
        dkd = {c: _mm_tn(da_all[c], qd_stack[c]) for c in order}
        upd = {c: _mm_tn(do_rows[c], qd_stack[c]) for c in order}
        dqd = {c: _head_diagonal(dqd[c], key_head) for c in order}
        return z, tile, {c: dict(dv=dv[c], dqd=dqd[c], dkd=dkd[c], upd=upd[c]) for c in order}

    def scan(tile, per):
        dst = dstate[...]
        dsts = {}
        for c in order:
            dsts[c] = dst
            dst = dst * tile[0][c] + per[c]["upd"]
        dstate[...] = dst
        return dsts

    def inter(z, tile, per, dsts):
        ebl, eb, enb, qd, kd, ke = tile
        ke_stack = {c: _stack_heads(ke[_chunk_rows(c)], key_head) for c in order}
        v_rows = {c: _rows_by_head(v_ref[_chunk_rows(c), :].astype(BF16)) for c in order}
        dst_b = {c: dsts[c].astype(BF16) for c in order}
        dv_state = {c: _mm_nt(ke_stack[c], dst_b[c]) for c in order}
        dke_c = {c: _mm(v_rows[c], dst_b[c]) for c in order}
        dke_c = {c: _head_diagonal(dke_c[c], key_head) for c in order}
        dbl_c = {}
        for c in order:
            rows = _chunk_rows(c)
            dv_ref[rows, :] = (per[c]["dv"] + _lanes_by_head(dv_state[c])).astype(BF16)
            dbl_c[c] = (jnp.sum(dsts[c] * st_ref[c].astype(F32), axis=0, keepdims=True) * ebl[c]
                        + jnp.sum(dke_c[c] * ke[rows], axis=0, keepdims=True))
        tile_of = lambda parts: jnp.concatenate([parts[c] for c in range(nc)], axis=0)
        dqd, dkd = tile_of({c: per[c]["dqd"] for c in order}), tile_of({c: per[c]["dkd"] for c in order})
        dke = tile_of(dke_c)
        dke_end = tile_of({c: dke_c[c] * ebl[c] for c in order})
        dq_ref[...] = (dqd * eb * Q_SCALE).astype(BF16)
        dk_ref[...] = ((dkd + dke_end) * enb).astype(BF16)
        db = dqd * qd - dkd * kd - dke * ke
        dla = _chunk_cumsum(_tri(not rev), db, [dbl_c[c] for c in range(nc)])
        dz = dla * (_sigmoid(-z) * (1.0 / GLA_TAU))
        dlr_ref[...] = _mm_nt(dz, wd_ref[...]).astype(BF16)
        dwd_ref[...] += _mm_tn(lr_ref[...], dz)
        dbd_ref[...] += jnp.sum(dz, axis=0, keepdims=True)

    return intra, scan, inter


def _gla_bwd(p, wd_pad_f, bd_f, wd_pad_b, bd_b, st_f, st_b, d_o, tg, comms=()):
    t = p.shape[0]
    nt = t // tg
    nc = tg // GLA_CHUNK
    up, down = (lambda i: i), (lambda i: nt - 1 - i)

    def body(qf, kf, vf, lrf, stf, dof, qb, kb, vb, lrb, stb, dob, wdf, bdf, wdb, bdb,
             dqf, dkf, dvf, dlrf, dwdf, dbdf, dqb, dkb, dvb, dlrb, dwdb, dbdb, dstate_f, dstate_b):
        @pl.when(pl.program_id(0) == 0)
        def _():
            for ref in (dstate_f, dstate_b, dwdf, dbdf, dwdb, dbdb):
                ref[...] = jnp.zeros_like(ref)

        dirs = [_gla_bwd_dir(False, nc, qf, kf, vf, lrf, wdf, bdf, stf, dof, dqf, dkf, dvf, dlrf, dwdf, dbdf,
                             dstate_f),
                _gla_bwd_dir(True, nc, qb, kb, vb, lrb, wdb, bdb, stb, dob, dqb, dkb, dvb, dlrb, dwdb, dbdb,
                             dstate_b)]
        first = [intra() for intra, _, _ in dirs]
        dsts = [scan(tile, per) for (_, scan, _), (_, tile, per) in zip(dirs, first)]
        for (_, _, inter), (z, tile, per), d in zip(dirs, first, dsts):
            inter(z, tile, per, d)

    wd_spec, bd_spec = _const_spec((LANE, KEY_W)), _const_spec((1, KEY_W))
    ins = lambda tile: _p_specs(tg, tile) + [pl.BlockSpec((nc, GLA_DV, KEY_W), lambda i: (tile(i), 0, 0)),
                                             pl.BlockSpec((tg, VAL_W), lambda i: (tile(i), 0))]
    outs = lambda tile: (pl.BlockSpec((tg, KEY_W), lambda i: (tile(i), 0)),
                         pl.BlockSpec((tg, KEY_W), lambda i: (tile(i), 0)),
                         pl.BlockSpec((tg, VAL_W), lambda i: (tile(i), 0)),
                         pl.BlockSpec((tg, LANE), lambda i: (tile(i), 0)),
                         _acc_spec((LANE, KEY_W)), _acc_spec((1, KEY_W)))
    out_shape = (jax.ShapeDtypeStruct((t, KEY_W), BF16), jax.ShapeDtypeStruct((t, KEY_W), BF16),
                 jax.ShapeDtypeStruct((t, VAL_W), BF16), jax.ShapeDtypeStruct((t, LANE), BF16),
                 jax.ShapeDtypeStruct((LANE, KEY_W), F32), jax.ShapeDtypeStruct((1, KEY_W), F32))
    scratch = [pltpu.VMEM((GLA_DV, KEY_W), F32)]
    return _fused_call(
        body, comms, name="gla_bwd", grid=(nt,),
        inputs=(p, p, p, p, st_f, d_o, p, p, p, p, st_b, d_o, wd_pad_f, bd_f, wd_pad_b, bd_b),
        in_specs=ins(down) + ins(up) + [wd_spec, bd_spec, wd_spec, bd_spec],
        out_specs=outs(down) + outs(up), out_shape=out_shape * 2, scratch_shapes=scratch * 2)


def _head_rms(o):
    parts, scales = [], []
    for h in range(GLA_HEADS):
        oh = o[:, h * GLA_DV:(h + 1) * GLA_DV]
        r = lax.rsqrt(jnp.mean(oh * oh, axis=-1, keepdims=True) + EPS)
        parts.append(oh * r)
        scales.append(jnp.broadcast_to(r, oh.shape))
    return jnp.concatenate(parts, axis=1), jnp.concatenate(scales, axis=1)


def _layernorm_stats(zv):
    mu = jnp.mean(zv, axis=-1, keepdims=True)
    xc = zv - mu
    rs = lax.rsqrt(jnp.mean(xc * xc, axis=-1, keepdims=True) + EPS)
    return xc * rs, rs


def _mix_fwd(x, o_f, o_b, p, gla_g, ln_g, ln_b, w_sp, b_sp, w_out, tm, comms=()):
    t = x.shape[0]
    nch = tm // GMLP_CHUNK

    def body(x_ref, of_ref, ob_ref, pg_ref, pu_ref, pv_ref, gg_ref, lg_ref, lb_ref, ws_ref, bs_ref, wo_ref,
             x1_ref, y_ref, s_scr):
        on, _ = _head_rms(of_ref[...] + ob_ref[...])
        pg = pg_ref[...].astype(F32)
        y_a = on * gg_ref[...] * (pg * _sigmoid(pg))
        zu = _gelu(pu_ref[...].astype(F32))
        vhat, _ = _layernorm_stats(_gelu(pv_ref[...].astype(F32)))
        vln = (vhat * lg_ref[...] + lb_ref[...]).astype(BF16)
        for g in range(GMLP_GROUPS):
            w_g = ws_ref[g].astype(BF16)
            b_g = bs_ref[g]
            cols = slice(g * LANE, (g + 1) * LANE)
            for n in range(nch):
                rows = slice(n * GMLP_CHUNK, (n + 1) * GMLP_CHUNK)
                s_scr[rows, cols] = jnp.dot(w_g, vln[rows, cols], preferred_element_type=F32) + b_g
        ycat = jnp.concatenate([y_a, zu * s_scr[...]], axis=1).astype(BF16)
        y_ref[...] = ycat
        x1_ref[...] = x_ref[...] + jnp.dot(ycat, wo_ref[...], preferred_element_type=F32)

    half = lambda j: pl.BlockSpec((tm, VAL_W), lambda i: (i, j))
    return _fused_call(
        body, comms, name="mix_fwd", grid=(t // tm,),
        inputs=(x, o_f, o_b, p, p, p, gla_g, ln_g, ln_b, w_sp, b_sp, w_out),
        in_specs=[pl.BlockSpec((tm, D_MODEL), lambda i: (i, 0)), half(0), half(0), half(2), half(3), half(4),
                  _const_spec((1, VAL_W)), _const_spec((1, GMLP_W)), _const_spec((1, GMLP_W)),
                  _const_spec((GMLP_GROUPS, GMLP_CHUNK, GMLP_CHUNK)), _const_spec((GMLP_GROUPS, GMLP_CHUNK, 1)),
                  _const_spec((D_MODEL, D_MODEL))],
        out_specs=(pl.BlockSpec((tm, D_MODEL), lambda i: (i, 0)), pl.BlockSpec((tm, D_MODEL), lambda i: (i, 0))),
        out_shape=(jax.ShapeDtypeStruct((t, D_MODEL), F32), jax.ShapeDtypeStruct((t, D_MODEL), BF16)),
        scratch_shapes=[pltpu.VMEM((tm, GMLP_W), F32)])


def _mix_bwd(dx1, ycat, o_f, o_b, p, gla_g, ln_g, ln_b, w_sp, b_sp, w_out, tm, comms=()):
    t = dx1.shape[0]
    nch = tm // GMLP_CHUNK

    def body(dx1_ref, y_ref, of_ref, ob_ref, pg_ref, pu_ref, pv_ref, gg_ref, lg_ref, lb_ref, ws_ref, bs_ref, wo_ref,
             do_ref, dpg_ref, dpu_ref, dpv_ref, dwo_ref, dgg_ref, dlg_ref, dlb_ref, dws_ref, dbs_ref,
             s_scr, dvln_scr):
        @pl.when(pl.program_id(0) == 0)
        def _():
            for ref in (dwo_ref, dgg_ref, dlg_ref, dlb_ref, dws_ref, dbs_ref):
                ref[...] = jnp.zeros_like(ref)

        dx1 = dx1_ref[...].astype(BF16)
        dycat = _mm_nt(dx1, wo_ref[...])
        dwo_ref[...] += _mm_tn(y_ref[...], dx1)
        dy_a = dycat[:, :VAL_W]
        dy_b = dycat[:, VAL_W:]
        on, r = _head_rms(of_ref[...] + ob_ref[...])
        pg = pg_ref[...].astype(F32)
        sil, dsil = _silu_and_grad(pg)
        gg = gg_ref[...]
        dgg_ref[...] += jnp.sum(dy_a * sil * on, axis=0, keepdims=True)
        don = dy_a * sil * gg
        prod = don * on
        means = jnp.concatenate(
            [jnp.broadcast_to(jnp.mean(prod[:, h * GLA_DV:(h + 1) * GLA_DV], axis=-1, keepdims=True),
                              (tm, GLA_DV)) for h in range(GLA_HEADS)], axis=1)
        do_ref[...] = (r * (don - on * means)).astype(BF16)
        dpg_ref[...] = (dy_a * on * gg * dsil).astype(BF16)
        pu = pu_ref[...].astype(F32)
        pv = pv_ref[...].astype(F32)
        zu, dzu_dpu = _gelu_and_grad(pu)
        zv, dzv_dpv = _gelu_and_grad(pv)
        vhat, rs = _layernorm_stats(zv)
        lg = lg_ref[...]
        vln = (vhat * lg + lb_ref[...]).astype(BF16)
        ds32 = dy_b * zu
        ds = ds32.astype(BF16)
        blocks = [(g, n) for g in range(GMLP_GROUPS) for n in range(nch)]
        at = lambda g, n: (slice(n * GMLP_CHUNK, (n + 1) * GMLP_CHUNK), slice(g * LANE, (g + 1) * LANE))
        w_sp = [ws_ref[g].astype(BF16) for g in range(GMLP_GROUPS)]
        v_blk = {b: vln[at(*b)] for b in blocks}
        ds_blk = {b: ds[at(*b)] for b in blocks}
        s_blk = {b: jnp.dot(w_sp[b[0]], v_blk[b], preferred_element_type=F32) for b in blocks}
        dw_blk = {b: _mm_nt(ds_blk[b], v_blk[b]) for b in blocks}
        dvln_blk = {b: _mm_tn(w_sp[b[0]], ds_blk[b]) for b in blocks}
        for b in blocks:
            s_scr[at(*b)] = s_blk[b] + bs_ref[b[0]]
            dvln_scr[at(*b)] = dvln_blk[b]
        for g in range(GMLP_GROUPS):
            dws_ref[g] += sum(dw_blk[(g, n)] for n in range(nch))
            dbs_ref[g] += sum(jnp.sum(ds32[at(g, n)], axis=-1, keepdims=True) for n in range(nch))
        dpu_ref[...] = (dy_b * s_scr[...] * dzu_dpu).astype(BF16)
        dvln = dvln_scr[...]
        dlg_ref[...] += jnp.sum(dvln * vhat, axis=0, keepdims=True)
        dlb_ref[...] += jnp.sum(dvln, axis=0, keepdims=True)
        dvhat = dvln * lg
        dzv = rs * (dvhat - jnp.mean(dvhat, axis=-1, keepdims=True)
                    - vhat * jnp.mean(dvhat * vhat, axis=-1, keepdims=True))
        dpv_ref[...] = (dzv * dzv_dpv).astype(BF16)

    half = lambda j: pl.BlockSpec((tm, VAL_W), lambda i: (i, j))
    full = pl.BlockSpec((tm, D_MODEL), lambda i: (i, 0))
    sp_shape = (GMLP_GROUPS, GMLP_CHUNK, GMLP_CHUNK)
    bs_shape = (GMLP_GROUPS, GMLP_CHUNK, 1)
    return _fused_call(
        body, comms, name="mix_bwd", grid=(t // tm,),
        inputs=(dx1, ycat, o_f, o_b, p, p, p, gla_g, ln_g, ln_b, w_sp, b_sp, w_out),
        in_specs=[full, full, half(0), half(0), half(2), half(3), half(4),
                  _const_spec((1, VAL_W)), _const_spec((1, GMLP_W)), _const_spec((1, GMLP_W)),
                  _const_spec(sp_shape), _const_spec(bs_shape), _const_spec((D_MODEL, D_MODEL))],
        out_specs=(half(0), half(0), half(0), half(0), _acc_spec((D_MODEL, D_MODEL)), _acc_spec((1, VAL_W)),
                   _acc_spec((1, GMLP_W)), _acc_spec((1, GMLP_W)), _acc_spec(sp_shape), _acc_spec(bs_shape)),
        out_shape=(jax.ShapeDtypeStruct((t, VAL_W), BF16),) * 4 + (
            jax.ShapeDtypeStruct((D_MODEL, D_MODEL), F32), jax.ShapeDtypeStruct((1, VAL_W), F32),
            jax.ShapeDtypeStruct((1, GMLP_W), F32), jax.ShapeDtypeStruct((1, GMLP_W), F32),
            jax.ShapeDtypeStruct(sp_shape, F32), jax.ShapeDtypeStruct(bs_shape, F32)),
        scratch_shapes=[pltpu.VMEM((tm, GMLP_W), F32), pltpu.VMEM((tm, GMLP_W), F32)])


def _rms_bwd(dy_scaled, xn, r):
    return r * (dy_scaled - xn * jnp.mean(dy_scaled * xn, axis=-1, keepdims=True))


def _ffn(x1, target, g2, gf, w_gate, w_up, w_down, tm):
    t = x1.shape[0]

    def body(x1_ref, tg_ref, g2_ref, gf_ref, wg_ref, wu_ref, wd_ref,
             dx1_ref, h2_ref, dgate_ref, dup_ref, act_ref, dx2_ref, loss_ref, dgf_ref, dg2_ref):
        @pl.when(pl.program_id(0) == 0)
        def _():
            for ref in (loss_ref, dgf_ref, dg2_ref):
                ref[...] = jnp.zeros_like(ref)

        x1v = x1_ref[...]
        g2v = g2_ref[...]
        gfv = gf_ref[...]
        r2 = lax.rsqrt(jnp.mean(x1v * x1v, axis=-1, keepdims=True) + EPS)
        xn1 = x1v * r2
        h2 = (xn1 * g2v).astype(BF16)
        h2_ref[...] = h2
        gate = _mm_nt(h2, wg_ref[...])
        up = _mm_nt(h2, wu_ref[...])
        sil, dsil = _silu_and_grad(gate)
        act = (sil * up).astype(BF16)
        act_ref[...] = act
        x2 = x1v + jnp.dot(act, wd_ref[...], preferred_element_type=F32)
        rf = lax.rsqrt(jnp.mean(x2 * x2, axis=-1, keepdims=True) + EPS)
        xn2 = x2 * rf
        err = xn2 * gfv - tg_ref[...]
        loss_ref[...] += 0.5 * jnp.sum(jnp.mean(err * err, axis=-1, keepdims=True))
        dy = err * (1.0 / D_MODEL)
        dgf_ref[...] += jnp.sum(dy * xn2, axis=0, keepdims=True)
        dx2 = _rms_bwd(dy * gfv, xn2, rf)
        dx2b = dx2.astype(BF16)
        dx2_ref[...] = dx2b
        dact = _mm_nt(dx2b, wd_ref[...])
        dgate = (dact * up * dsil).astype(BF16)
        dup = (dact * sil).astype(BF16)
        dgate_ref[...] = dgate
        dup_ref[...] = dup
        dh2 = _mm(dgate, wg_ref[...]) + _mm(dup, wu_ref[...])
        dg2_ref[...] += jnp.sum(dh2 * xn1, axis=0, keepdims=True)
        dx1_ref[...] = dx2 + _rms_bwd(dh2 * g2v, xn1, r2)

    row = lambda w: pl.BlockSpec((tm, w), lambda i: (i, 0))
    return pl.pallas_call(
        body, name="ffn_fwd_bwd", grid=(t // tm,),
        in_specs=[row(D_MODEL), row(D_MODEL), _const_spec((1, D_MODEL)), _const_spec((1, D_MODEL)),
                  _const_spec((D_FF, D_MODEL)), _const_spec((D_FF, D_MODEL)), _const_spec((D_FF, D_MODEL))],
        out_specs=(row(D_MODEL), row(D_MODEL), row(D_FF), row(D_FF), row(D_FF), row(D_MODEL),
                   _acc_spec((8, LANE)), _acc_spec((1, D_MODEL)), _acc_spec((1, D_MODEL))),
        out_shape=(jax.ShapeDtypeStruct((t, D_MODEL), F32), jax.ShapeDtypeStruct((t, D_MODEL), BF16),
                   jax.ShapeDtypeStruct((t, D_FF), BF16), jax.ShapeDtypeStruct((t, D_FF), BF16),
                   jax.ShapeDtypeStruct((t, D_FF), BF16), jax.ShapeDtypeStruct((t, D_MODEL), BF16),
                   jax.ShapeDtypeStruct((8, LANE), F32), jax.ShapeDtypeStruct((1, D_MODEL), F32),
                   jax.ShapeDtypeStruct((1, D_MODEL), F32)),
        compiler_params=_params(),
    )(x1, target, g2, gf, w_gate, w_up, w_down)


def _matmul_tn(a, b, tm, tk, name, comms=(), cols=None):
    t, m = a.shape
    cb, n = (0, b.shape[1]) if cols is None else cols

    def body(a_ref, b_ref, o_ref):
        @pl.when(pl.program_id(1) == 0)
        def _():
            o_ref[...] = jnp.zeros_like(o_ref)

        o_ref[...] += _mm_tn(a_ref[...], b_ref[...])

    (out,), comm_results = _fused_call(
        body, comms, name=name, grid=(m // tm, t // tk), inputs=(a, b),
        in_specs=[pl.BlockSpec((tk, tm), lambda j, k: (k, j)), pl.BlockSpec((tk, n), lambda j, k: (k, cb))],
        out_specs=(pl.BlockSpec((tm, n), lambda j, k: (j, 0)),),
        out_shape=(jax.ShapeDtypeStruct((m, n), F32),))
    return out, comm_results


def _in_proj_bwd(x, g1, dx1, dq_f, dq_b, dk_f, dk_b, dv_f, dv_b, dpg, dpu, dpv, dlr_f, dlr_b, w_main, tm, comms=()):
    t = x.shape[0]

    def body(x_ref, g_ref, dx1_ref, dqf, dqb, dkf, dkb, dvf, dvb, dg, du, dv, dlf, dlb, w_ref,
             dx_ref, dp_ref, dg1_ref):
        @pl.when(pl.program_id(0) == 0)
        def _():
            dg1_ref[...] = jnp.zeros_like(dg1_ref)

        both = lambda a, b: (a[...].astype(F32) + b[...].astype(F32)).astype(BF16)
        dp = jnp.concatenate([both(dqf, dqb), both(dkf, dkb), both(dvf, dvb), dg[...], du[...], dv[...],
                              both(dlf, dlb)], axis=1)
        dp_ref[...] = dp
        dh = sum(_mm(dp[:, c0:c0 + r1 - r0], w_ref[r0:r1, :]) for r0, r1, c0 in PROJ_ROWS)
        xv = x_ref[...]
        r = lax.rsqrt(jnp.mean(xv * xv, axis=-1, keepdims=True) + EPS)
        xn = xv * r
        dg1_ref[...] += jnp.sum(dh * xn, axis=0, keepdims=True)
        dx_ref[...] = dx1_ref[...] + _rms_bwd(dh * g_ref[...], xn, r)

    row = lambda w: pl.BlockSpec((tm, w), lambda i: (i, 0))
    return _fused_call(
        body, comms, name="in_proj_bwd", grid=(t // tm,),
        inputs=(x, g1, dx1, dq_f, dq_b, dk_f, dk_b, dv_f, dv_b, dpg, dpu, dpv, dlr_f, dlr_b, w_main),
        in_specs=[row(D_MODEL), _const_spec((1, D_MODEL)), row(D_MODEL), row(KEY_W), row(KEY_W), row(KEY_W),
                  row(KEY_W), row(VAL_W), row(VAL_W), row(VAL_W), row(VAL_W), row(VAL_W), row(LANE), row(LANE),
                  _const_spec((PROJ_W, D_MODEL))],
        out_specs=(row(D_MODEL), row(PROJ_PAD), _acc_spec((1, D_MODEL))),
        out_shape=(jax.ShapeDtypeStruct((t, D_MODEL), F32), jax.ShapeDtypeStruct((t, PROJ_PAD), BF16),
                   jax.ShapeDtypeStruct((1, D_MODEL), F32)))


def _adamw(w, g, m, v):
    m_new = ADAM_B1 * m + (1.0 - ADAM_B1) * g
    v_new = ADAM_B2 * v + (1.0 - ADAM_B2) * (g * g)
    m_hat = m_new / (1.0 - ADAM_B1 ** ADAM_STEP)
    v_hat = v_new / (1.0 - ADAM_B2 ** ADAM_STEP)
    delta = -ADAM_LR * (m_hat / (jnp.sqrt(v_hat) + ADAM_EPS) + ADAM_WD * w)
    return delta, m_new, v_new


def _adamw_window(own, recv, w, m, v, name):
    r, c = w.shape
    rows = own.shape[0]

    def body(own_ref, recv_ref, w_ref, m_ref, v_ref, g_ref, d_ref, nm_ref, nv_ref):
        g = own_ref[...]
        for k in range(3):
            g = g + recv_ref[k].astype(F32)

        def update(g):
            g_ref[...] = g[:r]
            d_ref[...], nm_ref[...], nv_ref[...] = _adamw(w_ref[...], g[:r], m_ref[...], v_ref[...])

        core = lax.axis_index("c")
        pl.when(core == 0)(lambda: update(g))
        pl.when(core == 1)(lambda: update(pltpu.roll(g, rows - 4, 0)))

    whole = lambda *shape: pl.BlockSpec(shape, lambda i: (0,) * len(shape))
    return pl.pallas_call(
        body, name=name, grid=(1,),
        in_specs=[whole(rows, c), whole(3, rows, c), whole(r, c), whole(r, c), whole(r, c)],
        out_specs=(whole(r, c),) * 4, out_shape=(jax.ShapeDtypeStruct((r, c), F32),) * 4,
        compiler_params=_params(),
    )(own, recv, w, m, v)


def _adamw_shard(own, recv, w, m, v, tr, name):
    r, c = w.shape

    def body(own_ref, recv_ref, w_ref, m_ref, v_ref, g_ref, d_ref, nm_ref, nv_ref):
        g = own_ref[...]
        for k in range(3):
            g = g + recv_ref[k].astype(F32)
        g_ref[...] = g
        d_ref[...], nm_ref[...], nv_ref[...] = _adamw(w_ref[...], g, m_ref[...], v_ref[...])

    row = pl.BlockSpec((tr, c), lambda i: (i, 0))
    return pl.pallas_call(
        body, name=name, grid=(r // tr,),
        in_specs=[row, pl.BlockSpec((3, tr, c), lambda i: (0, i, 0)), row, row, row],
        out_specs=(row,) * 4, out_shape=(jax.ShapeDtypeStruct((r, c), F32),) * 4,
        compiler_params=_params(),
    )(own, recv, w, m, v)


def _adamw_small(entries):
    stacks = []
    for (g, _, _), _, _, _ in entries:
        if not any(g is s for s in stacks):
            stacks.append(g)
    where = [next(i for i, s in enumerate(stacks) if s is g) for (g, _, _), _, _, _ in entries]
    ns, ne = len(stacks), len(entries)

    def body(*refs):
        s_refs, wmv, outs = refs[:ns], refs[ns:ns + 3 * ne], refs[ns + 3 * ne:]
        for e, ((_, r0, nr), _, _, _) in enumerate(entries):
            grad = s_refs[where[e]][r0:r0 + nr, :]
            w_ref, m_ref, v_ref = wmv[3 * e:3 * e + 3]
            g_ref, d_ref, nm_ref, nv_ref = outs[4 * e:4 * e + 4]
            g_ref[...] = grad
            d_ref[...], nm_ref[...], nv_ref[...] = _adamw(w_ref[...], grad, m_ref[...], v_ref[...])

    results = pl.pallas_call(
        body, name="adamw_small",
        out_shape=tuple(jax.ShapeDtypeStruct(w.shape, F32) for _, w, _, _ in entries for _ in range(4)),
        compiler_params=pltpu.CompilerParams(vmem_limit_bytes=VMEM_LIMIT),
    )(*stacks, *[a for _, w, m, v in entries for a in (w, m, v)])
    return [results[4 * e:4 * e + 4] for e in range(ne)]


def _mesh_pos():
    return lax.axis_index("x"), lax.axis_index("y"), lax.axis_index("c")


def _other_chips(x, y):
    return [(x, 1 - y), (1 - x, y), (1 - x, 1 - y)]


_VMEM_WHOLE = pl.BlockSpec(memory_space=pltpu.VMEM)
_HBM_WHOLE = pl.BlockSpec(memory_space=pl.ANY)


def _gather_comm(shards, cast, mid=((1, 2), (3, 4))):
    na = len(shards)
    staged = [a for a in range(na) if cast[a]]

    def phases(in_refs, out_refs, scr):
        stage = dict(zip(staged, scr[:len(staged)]))
        send_sems, recv_sems, local_sems = scr[len(staged):]
        x, y, c = _mesh_pos()
        me, sibling = (x, y, c), (x, y, 1 - c)
        chip_a, chip_b, diagonal = (x ^ c, y ^ (1 - c)), (x ^ (1 - c), y ^ c), (1 - x, 1 - y)
        srcs = [stage[a] if cast[a] else in_refs[a] for a in range(na)]

        def rows(a, pos):
            px, py, pc = pos
            return out_refs[a].at[4 * px + 2 * py + pc]

        def copy(a, k, block, to, src=None):
            return pltpu.make_async_remote_copy(
                src_ref=rows(a, block) if src is None else src, dst_ref=rows(a, block),
                send_sem=send_sems.at[a, k], recv_sem=recv_sems.at[a, k], device_id=to, device_id_type=MESH_ID)

        mine = [pltpu.make_async_copy(srcs[a], rows(a, me), local_sems.at[a]) for a in range(na)]
        own = [copy(a, k, me, to, src=srcs[a]) for a in range(na)
               for k, to in ((0, sibling), (1, (*chip_a, c)), (2, (*chip_b, c)))]
        onward = [copy(a, 3, (*chip_a, c), (*chip_b, c)) for a in range(na)]
        to_sibling = {k: [copy(a, k, (*chip, c), sibling) for a in range(na)]
                      for k, chip in ((4, chip_a), (5, chip_b), (6, diagonal))}

        def start():
            for a in staged:
                stage[a][...] = in_refs[a][...].astype(BF16)
            for cp in mine + own:
                cp.start()

        def forward_neighbours():
            for a in range(na):
                copy(a, 1, (*chip_a, c), me).wait_recv()
                onward[a].start()
                to_sibling[4][a].start()
            for a in range(na):
                copy(a, 2, (*chip_b, c), me).wait_recv()
                to_sibling[5][a].start()

        def forward_diagonal():
            for a in range(na):
                copy(a, 3, (*diagonal, c), me).wait_recv()
                to_sibling[6][a].start()

        def finish():
            for a in range(na):
                for k, chip in ((0, (x, y)), (4, chip_b), (5, chip_a), (6, diagonal)):
                    copy(a, k, (*chip, 1 - c), me).wait_recv()
            for cp in own + onward + to_sibling[4] + to_sibling[5] + to_sibling[6]:
                cp.wait_send()
            for cp in mine:
                cp.wait()

        return start, forward_neighbours, forward_diagonal, finish

    def before(step, nsteps, in_refs, out_refs, scr):
        start, forward_neighbours, forward_diagonal, _ = phases(in_refs, out_refs, scr)
        pl.when(step == 0)(start)
        pl.when(step == nsteps * mid[0][0] // mid[0][1])(forward_neighbours)
        pl.when(step == nsteps * mid[1][0] // mid[1][1])(forward_diagonal)

    def after(step, nsteps, in_refs, out_refs, scr):
        pl.when(step == nsteps - 1)(phases(in_refs, out_refs, scr)[3])

    return _Comm(
        inputs=list(shards), in_specs=[_VMEM_WHOLE] * na,
        out_shape=[jax.ShapeDtypeStruct((N_DEV,) + s.shape, BF16 if cast[a] else s.dtype)
                   for a, s in enumerate(shards)],
        out_specs=[_HBM_WHOLE] * na,
        scratch_shapes=[pltpu.VMEM(shards[a].shape, BF16) for a in staged] + [
            pltpu.SemaphoreType.DMA((na, 7)), pltpu.SemaphoreType.DMA((na, 7)), pltpu.SemaphoreType.DMA((na,))],
        before=before, after=after)


W_IN_WINDOW = 336


def _w_in_block_pieces(g_ref, chip, core):
    j = 2 * chip + core
    rows = PROJ_W // N_DEV
    first = rows * j - jnp.where(j > 4, 2 * LOWRANK, 0)
    start = pl.multiple_of((first >> 3) << 3, 8)
    head = LR_REF - 4 * rows
    split = [(g_ref.at[pl.ds(4 * rows, head)], 0, head), (g_ref.at[pl.ds(LR_COL, 2 * LOWRANK)], head, 2 * LOWRANK),
             (g_ref.at[pl.ds(LR_REF, 64)], head + 2 * LOWRANK, 64)]
    return [(j != 4, [(g_ref.at[pl.ds(start, W_IN_WINDOW)], 0, W_IN_WINDOW)]), (j == 4, split)]


def _reduce_scatter_comm(grads, rows=None, pieces=None):
    if pieces is None:
        _, _, r, c = grads.shape
        pieces = lambda g_ref, chip, core: [(None, [(g_ref.at[chip, core], 0, r)])]
    else:
        r, c = rows, grads.shape[1]
    order = (3, 1, 2, 0)

    def transfer(k, kind, in_refs, scr, act):
        (g_ref,), (sib, own, _, sems) = in_refs, scr
        x, y, core = _mesh_pos()
        chip = (2 * x + y) ^ k
        for cond, parts in pieces(g_ref, chip, 1 - core if kind == "send" else core):
            def run(parts=parts):
                for i, (src, row0, n) in enumerate(parts):
                    if kind == "local":
                        act(pltpu.make_async_copy(src, own.at[k, pl.ds(row0, n)], sems.at[2, 3 * k + i]))
                    else:
                        act(pltpu.make_async_remote_copy(
                            src_ref=src, dst_ref=sib.at[k, pl.ds(row0, n)], send_sem=sems.at[0, 3 * k + i],
                            recv_sem=sems.at[1, 3 * k + i], device_id=(x, y, 1 - core), device_id_type=MESH_ID))

            run() if cond is None else pl.when(cond)(run)

    def chip_copies(out_refs, scr):
        (_, recv), (_, _, part, sems) = out_refs, scr
        x, y, core = _mesh_pos()
        return [pltpu.make_async_remote_copy(
            src_ref=part.at[j], dst_ref=recv.at[j], send_sem=sems.at[3, j], recv_sem=sems.at[4, j],
            device_id=(*chip, core), device_id_type=MESH_ID) for j, chip in enumerate(_other_chips(x, y))]

    def before(step, nsteps, in_refs, out_refs, scr):
        @pl.when(step == 0)
        def _():
            for k in order:
                transfer(k, "send", in_refs, scr, lambda cp: cp.start())
                transfer(k, "local", in_refs, scr, lambda cp: cp.start())

    def after(step, nsteps, in_refs, out_refs, scr):
        sib, own, part, _ = scr

        @pl.when(step == (nsteps - 1) // 2)
        def _():
            to_chips = chip_copies(out_refs, scr)
            for k in order:
                transfer(k, "recv", in_refs, scr, lambda cp: cp.wait_recv())
                transfer(k, "local", in_refs, scr, lambda cp: cp.wait())
                if k:
                    part[k - 1] = (own[k] + sib[k]).astype(BF16)
                    to_chips[k - 1].start()
                else:
                    out_refs[0][...] = own[0] + sib[0]
            for k in order:
                transfer(k, "send", in_refs, scr, lambda cp: cp.wait_send())

        @pl.when(step == nsteps - 1)
        def _():
            for cp in chip_copies(out_refs, scr):
                cp.wait()

    return _Comm(inputs=[grads], in_specs=[_HBM_WHOLE],
                 out_shape=[jax.ShapeDtypeStruct((r, c), F32), jax.ShapeDtypeStruct((3, r, c), BF16)],
                 out_specs=[_VMEM_WHOLE, _HBM_WHOLE],
                 scratch_shapes=[pltpu.VMEM((4, r, c), F32), pltpu.VMEM((4, r, c), F32), pltpu.VMEM((3, r, c), BF16),
                                 pltpu.SemaphoreType.DMA((5, 12))],
                 before=before, after=after)


def _comm_only(comms, name):
    return _fused_call(lambda: None, comms, name=name, grid=(1,), inputs=(), in_specs=[], out_specs=(),
                       out_shape=())[1]


def _all_reduce_small_comm(parts):
    na = len(parts)

    def copies(in_refs, scr):
        gathered, (send_sems, recv_sems) = scr[:na], scr[na:]
        x, y, c = _mesh_pos()
        my_id = 4 * x + 2 * y + c
        return my_id, [pltpu.make_async_remote_copy(
            src_ref=in_refs[a], dst_ref=gathered[a].at[my_id], send_sem=send_sems.at[a, k - 1],
            recv_sem=recv_sems.at[a, k - 1], device_id=(x ^ (k >> 2), y ^ ((k >> 1) & 1), c ^ (k & 1)),
            device_id_type=MESH_ID) for a in range(na) for k in range(1, N_DEV)]

    def before(step, nsteps, in_refs, out_refs, scr):
        @pl.when(step == 0)
        def _():
            for cp in copies(in_refs, scr)[1]:
                cp.start()

    def after(step, nsteps, in_refs, out_refs, scr):
        @pl.when(step == nsteps - 1)
        def _():
            my_id, cps = copies(in_refs, scr)
            for a in range(na):
                scr[a][my_id] = in_refs[a][...]
            for cp in cps:
                cp.wait()
            for a in range(na):
                acc = scr[a][0]
                for d in range(1, N_DEV):
                    acc = acc + scr[a][d]
                out_refs[a][...] = acc

    return _Comm(inputs=list(parts), in_specs=[_VMEM_WHOLE] * na,
                 out_shape=[jax.ShapeDtypeStruct(p.shape, F32) for p in parts], out_specs=[_VMEM_WHOLE] * na,
                 scratch_shapes=[pltpu.VMEM((N_DEV,) + p.shape, F32) for p in parts] + [
                     pltpu.SemaphoreType.DMA((na, N_DEV - 1)), pltpu.SemaphoreType.DMA((na, N_DEV - 1))],
                 before=before, after=after)


def _unshard_cols(g):
    return jnp.transpose(g, (1, 0, 2)).reshape(g.shape[1], N_DEV * g.shape[2])


def _row_blocks(w):
    return w.reshape(4, 2, w.shape[0] // N_DEV, w.shape[1])


def _stack_rows(parts):
    a = jnp.concatenate(parts, axis=0)
    return jnp.pad(a, ((0, (-a.shape[0]) % 8), (0, 0)))


def _w_in_grad_blocks(dw):
    return _row_blocks(jnp.concatenate([dw[:LR_REF], dw[LR_COL:LR_COL + 2 * LOWRANK], dw[LR_REF:LR_COL]], axis=0))


def _padded_decay_weights(wd_f, wd_b):
    zeros = lambda n: jnp.zeros((n, KEY_W), F32)
    return (jnp.concatenate([wd_f, zeros(LANE - LOWRANK)], axis=0),
            jnp.concatenate([zeros(LOWRANK), wd_b, zeros(LANE - 2 * LOWRANK)], axis=0))


def kernel(x, norm1_g, w_in,w_decay_f, b_decay_f, w_decay_b, b_decay_b, gla_norm_g, gmlp_ln_g, gmlp_ln_b, w_spatial, b_spatial, w_out, norm2_g, w_gate, w_up, w_down, final_norm_g, loss_target, m_norm1_g, m_w_in, m_w_decay_f, m_b_decay_f, m_w_decay_b, m_b_decay_b, m_gla_norm_g, m_gmlp_ln_g, m_gmlp_ln_b, m_w_spatial, m_b_spatial, m_w_out, m_norm2_g, m_w_gate, m_w_up, m_w_down, m_final_norm_g, v_norm1_g, v_w_in, v_w_decay_f, v_b_decay_f, v_w_decay_b, v_b_decay_b, v_gla_norm_g, v_gmlp_ln_g, v_gmlp_ln_b, v_w_spatial, v_b_spatial, v_w_out, v_norm2_g, v_w_gate, v_w_up, v_w_down, v_final_norm_g):
    t = x.shape[1]
    xt = x[0]
    target = loss_target[0]
    pos_x, pos_y, pos_c = _mesh_pos()
    my_id = 4 * pos_x + 2 * pos_y + pos_c

    tile = lambda n: min(n, t)
    ln_g, ln_b, w_sp = gmlp_ln_g, gmlp_ln_b, w_spatial[0]
    b_sp_col = b_spatial[0][:, :, None]
    shard = {"w_in": w_in[0].T, "w_out": w_out[0], "w_gate": w_gate[0].T, "w_up": w_up[0].T, "w_down": w_down[0]}
    shard_m = {"w_in": m_w_in[0].T, "w_out": m_w_out[0], "w_gate": m_w_gate[0].T, "w_up": m_w_up[0].T,
               "w_down": m_w_down[0]}
    shard_v = {"w_in": v_w_in[0].T, "w_out": v_w_out[0], "w_gate": v_w_gate[0].T, "w_up": v_w_up[0].T,
               "w_down": v_w_down[0]}
    transposed = ("w_in", "w_gate", "w_up")

    decay_shard = jnp.stack([w_decay_f[0], w_decay_b[0]])
    (hb,), ((g_in, g_decay),) = _norm1(xt, norm1_g, tile(TOKEN_TILE["norm1"]),
                                       [_gather_comm([shard["w_in"], decay_shard], [True, False])])
    w_in_t = g_in.reshape(PROJ_W, D_MODEL)
    wd_pad_f, wd_pad_b = _padded_decay_weights(_unshard_cols(g_decay[:, 0]), _unshard_cols(g_decay[:, 1]))
    (p,), ((g_gate, g_out),) = _in_proj(
        hb, w_in_t, tile(TOKEN_TILE["in_proj"]), [_gather_comm([shard["w_gate"], shard["w_out"]], [True, True])])
    (o_f, st_f, o_b, st_b), ((g_up,),) = _gla_fwd(
        p, wd_pad_f, b_decay_f, wd_pad_b, b_decay_b, tile(TOKEN_TILE["gla"]), [_gather_comm([shard["w_up"]], [True])])
    w_out_full = g_out.reshape(D_MODEL, D_MODEL)
    (x1, ycat), ((g_down,),) = _mix_fwd(xt, o_f, o_b, p, gla_norm_g, ln_g, ln_b, w_sp, b_sp_col, w_out_full,
                                        tile(TOKEN_TILE["mix_fwd"]), [_gather_comm([shard["w_down"]], [True])])

    dx1, h2b, dgate, dup, act, dx2, loss_acc, d_gf, d_g2 = _ffn(
        x1, target, norm2_g, final_norm_g[None, :], g_gate.reshape(D_FF, D_MODEL), g_up.reshape(D_FF, D_MODEL),
        g_down.reshape(D_FF, D_MODEL), tile(TOKEN_TILE["ffn"]))
    reduced = {}
    dw_gate, _ = _matmul_tn(dgate, h2b, D_FF // 2, tile(TOKEN_TILE["dw"]), "grad_w_gate")
    dw_up, (reduced["w_gate"],) = _matmul_tn(dup, h2b, D_FF // 2, tile(TOKEN_TILE["dw_host"]), "grad_w_up",
                                             [_reduce_scatter_comm(_row_blocks(dw_gate))])
    dw_down, (reduced["w_up"],) = _matmul_tn(act, dx2, D_FF // 2, tile(TOKEN_TILE["dw_host"]), "grad_w_down",
                                             [_reduce_scatter_comm(_row_blocks(dw_up))])

    (d_o, dpg, dpu, dpv, dw_out, d_gg, d_lg, d_lb, dw_sp, db_sp), (reduced["w_down"],) = _mix_bwd(
        dx1, ycat, o_f, o_b, p, gla_norm_g, ln_g, ln_b, w_sp, b_sp_col, w_out_full,
        tile(TOKEN_TILE["mix_bwd"]), [_reduce_scatter_comm(_row_blocks(dw_down))])
    (dq_f, dk_f, dv_f, dlr_f, dwd_f, dbd_f, dq_b, dk_b, dv_b, dlr_b, dwd_b, dbd_b), (reduced["w_out"],) = _gla_bwd(
        p, wd_pad_f, b_decay_f, wd_pad_b, b_decay_b, st_f, st_b, d_o, tile(TOKEN_TILE["gla"]),
        [_reduce_scatter_comm(_row_blocks(dw_out))])
    (grad_x, dp, d_g1), _ = _in_proj_bwd(
        xt, norm1_g, dx1, dq_f, dq_b, dk_f, dk_b, dv_f, dv_b, dpg, dpu, dpv, dlr_f, dlr_b, w_in_t,
        tile(TOKEN_TILE["in_proj_bwd"]))

    stacks = [_stack_rows([d_g1, d_g2, d_gf]), _stack_rows([d_gg, d_lg, d_lb]),
              _stack_rows([dbd_f, dbd_b, jnp.zeros((DECAY_W_ROW - 2, KEY_W), F32), dwd_f[:LOWRANK],
                           dwd_b[LOWRANK:2 * LOWRANK]]),
              _stack_rows([dw_sp.reshape(GMLP_W, GMLP_CHUNK), db_sp[:, :, 0], loss_acc[:1]])]
    dw_main, (small_sums,) = _matmul_tn(dp, hb, PROJ_PAD // 3, tile(TOKEN_TILE["dw"]), "grad_w_in",
                                        [_all_reduce_small_comm(stacks)])
    ((in_own, in_recv),) = _comm_only(
        [_reduce_scatter_comm(dw_main, W_IN_WINDOW, _w_in_block_pieces)], "grad_w_in_reduce_scatter")

    big_out = {"w_in": [r.T for r in _adamw_window(in_own, in_recv, shard["w_in"], shard_m["w_in"], shard_v["w_in"],
                                                   "adamw_w_in")]}
    for n, (own_sum, recv) in reduced.items():
        rows = shard[n].shape[0]
        half = rows // 2 if rows % 32 == 0 else rows
        res = _adamw_shard(own_sum, recv, shard[n], shard_m[n], shard_v[n], half, "adamw_" + n)
        big_out[n] = [r.T if n in transposed else r for r in res]

    s1024, s512, s256, s128 = small_sums
    loss = s128[GMLP_W + GMLP_GROUPS, 0]
    col0 = my_id * (KEY_W // N_DEV)
    decay_cols = lambda row0: lax.dynamic_slice(s256, (row0, col0), (LOWRANK, KEY_W // N_DEV))
    flat = lambda a: a.reshape(-1, a.shape[-1])
    small = {
        "norm1_g": ((s1024, 0, 1), norm1_g, m_norm1_g, v_norm1_g),
        "w_decay_f": ((decay_cols(DECAY_W_ROW), 0, LOWRANK), w_decay_f, m_w_decay_f, v_w_decay_f),
        "b_decay_f": ((s256, 0, 1), b_decay_f, m_b_decay_f, v_b_decay_f),
        "w_decay_b": ((decay_cols(DECAY_W_ROW + LOWRANK), 0, LOWRANK), w_decay_b, m_w_decay_b, v_w_decay_b),
        "b_decay_b": ((s256, 1, 1), b_decay_b, m_b_decay_b, v_b_decay_b),
        "gla_norm_g": ((s512, 0, 1), gla_norm_g, m_gla_norm_g, v_gla_norm_g),
        "gmlp_ln_g": ((s512, 1, 1), gmlp_ln_g, m_gmlp_ln_g, v_gmlp_ln_g),
        "gmlp_ln_b": ((s512, 2, 1), gmlp_ln_b, m_gmlp_ln_b, v_gmlp_ln_b),
        "w_spatial": ((s128, 0, GMLP_W), w_spatial, m_w_spatial, v_w_spatial),
        "b_spatial": ((s128, GMLP_W, GMLP_GROUPS), b_spatial, m_b_spatial, v_b_spatial),
        "norm2_g": ((s1024, 1, 1), norm2_g, m_norm2_g, v_norm2_g),
        "final_norm_g": ((s1024, 2, 1), final_norm_g, m_final_norm_g, v_final_norm_g),
    }
    small_res = _adamw_small([(g, flat(w), flat(m), flat(v)) for g, w, m, v in small.values()])
    small_out = {n: [r.reshape(small[n][1].shape) for r in res] for n, res in zip(small, small_res)}

    order = ["norm1_g", "w_in", "w_decay_f", "b_decay_f", "w_decay_b", "b_decay_b", "gla_norm_g", "gmlp_ln_g",
             "gmlp_ln_b", "w_spatial", "b_spatial", "w_out", "norm2_g", "w_gate", "w_up", "w_down", "final_norm_g"]
    outs = []
    for kind in range(4):
        for n in order:
            outs.append(big_out[n][kind][None] if n in big_out else small_out[n][kind])
    return (loss, grad_x[None], *outs)
```

```python
import functools
import math

import jax
import jax.numpy as jnp
from jax import lax
from jax.experimental import pallas as pl
from jax.experimental.pallas import tpu as pltpu

F32 = jnp.float32
BF16 = jnp.bfloat16

D_MODEL = 1024
GLA_HEADS = 4
GLA_DK = 64
GLA_DV = 128
KEY_W = GLA_HEADS * GLA_DK
VAL_W = GLA_HEADS * GLA_DV
LOWRANK = 16
GLA_TAU = 16.0
GLA_CHUNK = 64
GMLP_W = 512
GMLP_GROUPS = 4
GMLP_CHUNK = 128
D_FF = 2816
EPS = 1e-6
Q_SCALE = GLA_DK ** -0.5
PROJ_PAD = 2688
LR_COL = 2560
LANE = 128
N_DEV = 8

ADAM_LR = 0.001
ADAM_B1 = 0.9
ADAM_B2 = 0.999
ADAM_EPS = 1e-08
ADAM_WD = 0.01
ADAM_STEP = 10

VMEM_LIMIT = 56 * 1024 * 1024
TOKEN_TILE = {"norm1": 512, "in_proj": 512, "gla": 1024, "mix_fwd": 1024, "ffn": 256, "mix_bwd": 512,
              "in_proj_bwd": 512, "dw": 2048, "dw_host": 2048}
DW_HOST_MID = (4, 7)
DECAY_W_ROW = 8
MESH_ID = pl.DeviceIdType.MESH
INV_SQRT2 = 0.7071067811865476
INV_SQRT_2PI = 0.3989422804014327


def _params(n_axes=1):
    return pltpu.CompilerParams(dimension_semantics=("arbitrary",) * n_axes, vmem_limit_bytes=VMEM_LIMIT)


def _mm(a, b):
    return jnp.dot(a.astype(BF16), b.astype(BF16), preferred_element_type=F32)


def _mm_nt(a, b):
    return lax.dot_general(a.astype(BF16), b.astype(BF16), (((1,), (1,)), ((), ())), preferred_element_type=F32)


def _mm_tn(a, b):
    return lax.dot_general(a.astype(BF16), b.astype(BF16), (((0,), (0,)), ((), ())), preferred_element_type=F32)


def _const_spec(shape):
    nd = len(shape)
    return pl.BlockSpec(shape, lambda *_: (0,) * nd, pipeline_mode=pl.Buffered(1))


def _acc_spec(shape):
    nd = len(shape)
    return pl.BlockSpec(shape, lambda *_: (0,) * nd)


class _Comm:
    def __init__(self, inputs, in_specs, out_shape, out_specs, scratch_shapes, before, after):
        self.inputs, self.in_specs, self.out_shape, self.out_specs = inputs, in_specs, out_shape, out_specs
        self.scratch_shapes, self.before, self.after = scratch_shapes, before, after


def _fused_call(body, comms, *, name, grid, inputs, in_specs, out_specs, out_shape, scratch_shapes=(), prefetch=()):
    n_pre, n_in, n_out, n_scr = len(prefetch), len(in_specs), len(out_specs), len(scratch_shapes)
    nsteps = math.prod(grid)
    sizes = [(len(c.inputs), len(c.out_shape), len(c.scratch_shapes)) for c in comms]

    def full_body(*refs):
        step = pl.program_id(0)
        for axis in range(1, len(grid)):
            step = step * grid[axis] + pl.program_id(axis)
        pre, refs = refs[:n_pre], refs[n_pre:]
        ins, rest = refs[:n_in], refs[n_in:]
        c_ins = []
        for ci, _, _ in sizes:
            c_ins.append(rest[:ci])
            rest = rest[ci:]
        outs, rest = rest[:n_out], rest[n_out:]
        c_outs = []
        for _, co, _ in sizes:
            c_outs.append(rest[:co])
            rest = rest[co:]
        scr, rest = rest[:n_scr], rest[n_scr:]
        c_scr = []
        for _, _, cs in sizes:
            c_scr.append(rest[:cs])
            rest = rest[cs:]
        for c, a, b, s in zip(comms, c_ins, c_outs, c_scr):
            c.before(step, nsteps, a, b, s)
        body(*pre, *ins, *outs, *scr)
        for c, a, b, s in zip(comms, c_ins, c_outs, c_scr):
            c.after(step, nsteps, a, b, s)

    specs = dict(
        grid=grid, in_specs=list(in_specs) + [s for c in comms for s in c.in_specs],
        out_specs=tuple(out_specs) + tuple(s for c in comms for s in c.out_specs),
        scratch_shapes=list(scratch_shapes) + [s for c in comms for s in c.scratch_shapes])
    if n_pre:
        specs = dict(grid_spec=pltpu.PrefetchScalarGridSpec(num_scalar_prefetch=n_pre, **specs))
    results = pl.pallas_call(
        full_body, name=name, **specs,
        out_shape=tuple(out_shape) + tuple(s for c in comms for s in c.out_shape),
        compiler_params=_params(len(grid)),
    )(*prefetch, *inputs, *[a for c in comms for a in c.inputs])
    own, rest = results[:n_out], results[n_out:]
    comm_results = []
    for _, co, _ in sizes:
        comm_results.append(rest[:co])
        rest = rest[co:]
    return own, comm_results


def _gelu(x):
    return 0.5 * x * (1.0 + lax.erf(x * INV_SQRT2))


def _gelu_and_grad(x):
    cdf = 0.5 * (1.0 + lax.erf(x * INV_SQRT2))
    return x * cdf, cdf + x * jnp.exp(-0.5 * x * x) * INV_SQRT_2PI


def _sigmoid(x):
    return 0.5 + 0.5 * jnp.tanh(0.5 * x)


def _silu_and_grad(x):
    s = _sigmoid(x)
    return x * s, s * (1.0 + x * (1.0 - s))


def _norm1(x, g1, tm, comms=()):
    t = x.shape[0]

    def body(x_ref, g_ref, h_ref):
        xv = x_ref[...]
        r = lax.rsqrt(jnp.mean(xv * xv, axis=-1, keepdims=True) + EPS)
        h_ref[...] = (xv * r * g_ref[...]).astype(BF16)

    row = pl.BlockSpec((tm, D_MODEL), lambda i: (i, 0))
    return _fused_call(body, comms, name="norm1", grid=(t // tm,), inputs=(x, g1),
                       in_specs=[row, _const_spec((1, D_MODEL))], out_specs=(row,),
                       out_shape=(jax.ShapeDtypeStruct((t, D_MODEL), BF16),))


PROJ_W = 2592
LR_REF = 1536
PROJ_ROWS = ((0, LR_REF, 0), (LR_REF + 2 * LOWRANK, PROJ_W, LR_REF), (LR_REF, LR_REF + LANE, LR_COL))


def _in_proj(h, w_in_t, tm, comms=()):
    t = h.shape[0]

    def body(h_ref, w_ref, p_ref):
        hv = h_ref[...]
        for r0, r1, c0 in PROJ_ROWS:
            p_ref[:, c0:c0 + r1 - r0] = _mm_nt(hv, w_ref[r0:r1, :]).astype(BF16)

    return _fused_call(
        body, comms, name="in_proj", grid=(t // tm,), inputs=(h, w_in_t),
        in_specs=[pl.BlockSpec((tm, D_MODEL), lambda i: (i, 0)), _const_spec((PROJ_W, D_MODEL))],
        out_specs=(pl.BlockSpec((tm, PROJ_PAD), lambda i: (i, 0)),),
        out_shape=(jax.ShapeDtypeStruct((t, PROJ_PAD), BF16),))


def _tri(upper):
    r = lax.broadcasted_iota(jnp.int32, (GLA_CHUNK, GLA_CHUNK), 0)
    c = lax.broadcasted_iota(jnp.int32, (GLA_CHUNK, GLA_CHUNK), 1)
    return jnp.where((c >= r) if upper else (c <= r), 1.0, 0.0).astype(BF16)


def _chunk_cumsum(tri, a, add=None):
    hi = a.astype(BF16)
    lo = (a - hi.astype(F32)).astype(BF16)
    dot = functools.partial(jnp.dot, preferred_element_type=F32)
    sums = [dot(tri, hi[_chunk_rows(c)]) + dot(tri, lo[_chunk_rows(c)]) for c in range(a.shape[0] // GLA_CHUNK)]
    return jnp.concatenate(sums if add is None else [s + r for s, r in zip(sums, add)], axis=0)


def _chunk_rows(c):
    return slice(c * GLA_CHUNK, (c + 1) * GLA_CHUNK)


def _gla_masks(rev):
    dk_bits, dv_bits = GLA_DK.bit_length() - 1, GLA_DV.bit_length() - 1
    key_head = lax.broadcasted_iota(jnp.int32, (GLA_CHUNK, KEY_W), 1) >> dk_bits
    val_head = lax.broadcasted_iota(jnp.int32, (GLA_CHUNK, VAL_W), 1) >> dv_bits
    t = lax.broadcasted_iota(jnp.int32, (GLA_HEADS * GLA_CHUNK, GLA_CHUNK), 0) & (GLA_CHUNK - 1)
    s = lax.broadcasted_iota(jnp.int32, (GLA_HEADS * GLA_CHUNK, GLA_CHUNK), 1)
    return key_head, val_head, (s >= t) if rev else (s <= t)


def _stack_heads(a, head_of_lane):
    a = a.astype(BF16)
    return jnp.concatenate([jnp.where(head_of_lane == h, a, jnp.zeros_like(a)) for h in range(GLA_HEADS)], axis=0)


def _rows_by_head(a):
    return jnp.concatenate([a[:, h * GLA_DV:(h + 1) * GLA_DV] for h in range(GLA_HEADS)], axis=0)


def _lanes_by_head(r):
    return jnp.concatenate([r[h * GLA_CHUNK:(h + 1) * GLA_CHUNK] for h in range(GLA_HEADS)], axis=1)


def _head_diagonal(r, head_of_lane):
    rows = r.shape[0] // GLA_HEADS
    out = jnp.where(head_of_lane == 0, r[:rows], 0.0)
    for h in range(1, GLA_HEADS):
        out = out + jnp.where(head_of_lane == h, r[h * rows:(h + 1) * rows], 0.0)
    return out


def _tile_terms(la, q, k, tri, rev):
    nc = la.shape[0] // GLA_CHUNK
    q, k = q.astype(F32), k.astype(F32)
    b = _chunk_cumsum(tri, la)
    ebl = [jnp.exp(b[c * GLA_CHUNK:c * GLA_CHUNK + 1] if rev else b[(c + 1) * GLA_CHUNK - 1:(c + 1) * GLA_CHUNK])
           for c in range(nc)]
    eb = jnp.exp(b)
    enb = jnp.exp(-b)
    kd = k * enb
    ke = jnp.concatenate([kd[_chunk_rows(c)] * ebl[c] for c in range(nc)], axis=0)
    return ebl, eb, enb, q * Q_SCALE * eb, kd, ke


def _log_decay(lr_ref, wd_ref, bd_ref):
    z = _mm(lr_ref[...], wd_ref[...]) + bd_ref[...]
    return z, jax.nn.log_sigmoid(z) * (1.0 / GLA_TAU)


def _p_specs(tg, tile):
    return [pl.BlockSpec((tg, KEY_W), lambda i: (tile(i), 0)),
            pl.BlockSpec((tg, KEY_W), lambda i: (tile(i), 1)),
            pl.BlockSpec((tg, VAL_W), lambda i: (tile(i), 1)),
            pl.BlockSpec((tg, LANE), lambda i: (tile(i), LR_COL // LANE))]


def _gla_fwd_dir(rev, nc, q_ref, k_ref, v_ref, lr_ref, wd_ref, bd_ref, o_ref, st_ref, state):
    key_head, _, causal = _gla_masks(rev)
    order = range(nc - 1, -1, -1) if rev else range(nc)

    def intra():
        _, la = _log_decay(lr_ref, wd_ref, bd_ref)
        ebl, _, _, qd, kd, ke = _tile_terms(la, q_ref[...], k_ref[...], _tri(rev), rev)
        kd = kd.astype(BF16)
        v = {c: v_ref[_chunk_rows(c), :].astype(BF16) for c in order}
        qd_stack = {c: _stack_heads(qd[_chunk_rows(c)], key_head) for c in order}
        ke_stack = {c: _stack_heads(ke[_chunk_rows(c)], key_head) for c in order}
        a_all = {c: _mm_nt(qd_stack[c], kd[_chunk_rows(c)]) for c in order}
        a_all = {c: jnp.where(causal, a_all[c], 0.0).astype(BF16) for c in order}
        head_rows = lambda a, h: a[h * GLA_CHUNK:(h + 1) * GLA_CHUNK]
        head_vals = lambda a, h: a[:, h * GLA_DV:(h + 1) * GLA_DV]
        r = {c: [_mm(head_rows(a_all[c], h), head_vals(v[c], h)) for h in range(GLA_HEADS)] for c in order}
        upd = {c: _mm_tn(_rows_by_head(v[c]), ke_stack[c]) for c in order}
        return {c: (ebl[c], qd_stack[c], r[c], upd[c]) for c in order}

    def scan(terms):
        st = state[...]
        states = {}
        for c in order:
            states[c] = st
            st_ref[c] = st.astype(BF16)
            st = st * terms[c][0] + terms[c][3]
        state[...] = st
        return states

    def inter(terms, states):
        r_inter = {c: _mm_nt(terms[c][1], states[c]) for c in order}
        for c in order:
            o_ref[_chunk_rows(c), :] = jnp.concatenate(
                [terms[c][2][h] + r_inter[c][h * GLA_CHUNK:(h + 1) * GLA_CHUNK] for h in range(GLA_HEADS)], axis=1)

    return intra, scan, inter


def _gla_fwd(p, wd_pad_f, bd_f, wd_pad_b, bd_b, tg, comms=()):
    t = p.shape[0]
    nt = t // tg
    nc = tg // GLA_CHUNK
    up, down = (lambda i: i), (lambda i: nt - 1 - i)

    def body(qf, kf, vf, lrf, qb, kb, vb, lrb, wdf, bdf, wdb, bdb, of, stf, ob, stb, state_f, state_b):
        @pl.when(pl.program_id(0) == 0)
        def _():
            state_f[...] = jnp.zeros_like(state_f)
            state_b[...] = jnp.zeros_like(state_b)

        dirs = [_gla_fwd_dir(False, nc, qf, kf, vf, lrf, wdf, bdf, of, stf, state_f),
                _gla_fwd_dir(True, nc, qb, kb, vb, lrb, wdb, bdb, ob, stb, state_b)]
        terms = [intra() for intra, _, _ in dirs]
        states = [scan(t) for (_, scan, _), t in zip(dirs, terms)]
        for (_, _, inter), t, s in zip(dirs, terms, states):
            inter(t, s)

    wd_spec, bd_spec = _const_spec((LANE, KEY_W)), _const_spec((1, KEY_W))
    outs = lambda tile: (pl.BlockSpec((tg, VAL_W), lambda i: (tile(i), 0)),
                         pl.BlockSpec((nc, GLA_DV, KEY_W), lambda i: (tile(i), 0, 0)))
    out_shape = (jax.ShapeDtypeStruct((t, VAL_W), F32), jax.ShapeDtypeStruct((t // GLA_CHUNK, GLA_DV, KEY_W), BF16))
    return _fused_call(
        body, comms, name="gla_fwd", grid=(nt,), inputs=(p,) * 8 + (wd_pad_f, bd_f, wd_pad_b, bd_b),
        in_specs=_p_specs(tg, up) + _p_specs(tg, down) + [wd_spec, bd_spec, wd_spec, bd_spec],
        out_specs=outs(up) + outs(down), out_shape=out_shape * 2,
        scratch_shapes=[pltpu.VMEM((GLA_DV, KEY_W), F32)] * 2)


def _gla_bwd_dir(rev, nc, q_ref, k_ref, v_ref, lr_ref, wd_ref, bd_ref, st_ref, do_ref,
                 dq_ref, dk_ref, dv_ref, dlr_ref, dwd_ref, dbd_ref, dstate):
    key_head, val_head, causal = _gla_masks(rev)
    order = range(nc) if rev else range(nc - 1, -1, -1)

    def intra():
        z, la = _log_decay(lr_ref, wd_ref, bd_ref)
        tile = _tile_terms(la, q_ref[...], k_ref[...], _tri(rev), rev)
        qd, kd = tile[3], tile[4].astype(BF16)
        v = {c: v_ref[_chunk_rows(c), :].astype(BF16) for c in order}
        d_o = {c: do_ref[_chunk_rows(c), :] for c in order}
        kd_c = {c: kd[_chunk_rows(c)] for c in order}
        qd_stack = {c: _stack_heads(qd[_chunk_rows(c)], key_head) for c in order}
        do_stack = {c: _stack_heads(d_o[c], val_head) for c in order}
        do_rows = {c: _rows_by_head(d_o[c]) for c in order}
        a_all = {c: _mm_nt(qd_stack[c], kd_c[c]) for c in order}
        head_vals = lambda a, h: a[:, h * GLA_DV:(h + 1) * GLA_DV]
        da_all = {c: jnp.concatenate([_mm_nt(head_vals(d_o[c], h), head_vals(v[c], h)) for h in range(GLA_HEADS)],
                                     axis=0) for c in order}
        a_all = {c: jnp.where(causal, a_all[c], 0.0).astype(BF16) for c in order}
        da_all = {c: jnp.where(causal, da_all[c], 0.0).astype(BF16) for c in order}
        dv = {c: _mm_tn(a_all[c], do_stack[c]) for c in order}
        dqd = {c: _mm(jnp.concatenate([do_rows[c], da_all[c]], axis=1),
                      jnp.concatenate([st_ref[c], kd_c[c]], axis=0)) for c in order}
        dkd = {c: _mm_tn(da_all[c], qd_stack[c]) for c in order}
        upd = {c: _mm_tn(do_rows[c], qd_stack[c]) for c in order}
        dqd = {c: _head_diagonal(dqd[c], key_head) for c in order}
        return z, tile, {c: dict(dv=dv[c], dqd=dqd[c], dkd=dkd[c], upd=upd[c]) for c in order}

    def scan(tile, per):
        dst = dstate[...]
        dsts = {}
        for c in order:
            dsts[c] = dst
            dst = dst * tile[0][c] + per[c]["upd"]
        dstate[...] = dst
        return dsts

    def inter(z, tile, per, dsts):
        ebl, eb, enb, qd, kd, ke = tile
        ke_stack = {c: _stack_heads(ke[_chunk_rows(c)], key_head) for c in order}
        v_rows = {c: _rows_by_head(v_ref[_chunk_rows(c), :].astype(BF16)) for c in order}
        dst_b = {c: dsts[c].astype(BF16) for c in order}
        dv_state = {c: _mm_nt(ke_stack[c], dst_b[c]) for c in order}
        dke_c = {c: _mm(v_rows[c], dst_b[c]) for c in order}
        dke_c = {c: _head_diagonal(dke_c[c], key_head) for c in order}
        dbl_c = {}
        for c in order:
            rows = _chunk_rows(c)
            dv_ref[rows, :] = (per[c]["dv"] + _lanes_by_head(dv_state[c])).astype(BF16)
            dbl_c[c] = (jnp.sum(dsts[c] * st_ref[c].astype(F32), axis=0, keepdims=True) * ebl[c]
                        + jnp.sum(dke_c[c] * ke[rows], axis=0, keepdims=True))
        tile_of = lambda parts: jnp.concatenate([parts[c] for c in range(nc)], axis=0)
        dqd, dkd = tile_of({c: per[c]["dqd"] for c in order}), tile_of({c: per[c]["dkd"] for c in order})
        dke = tile_of(dke_c)
        dke_end = tile_of({c: dke_c[c] * ebl[c] for c in order})
        dq_ref[...] = (dqd * eb * Q_SCALE).astype(BF16)
        dk_ref[...] = ((dkd + dke_end) * enb).astype(BF16)
        db = dqd * qd - dkd * kd - dke * ke
        dla = _chunk_cumsum(_tri(not rev), db, [dbl_c[c] for c in range(nc)])
        dz = dla * (_sigmoid(-z) * (1.0 / GLA_TAU))
        dlr_ref[...] = _mm_nt(dz, wd_ref[...]).astype(BF16)
        dwd_ref[...] += _mm_tn(lr_ref[...], dz)
        dbd_ref[...] += jnp.sum(dz, axis=0, keepdims=True)

    return intra, scan, inter


def _gla_bwd(p, wd_pad_f, bd_f, wd_pad_b, bd_b, st_f, st_b, d_o, tg, comms=()):
    t = p.shape[0]
    nt = t // tg
    nc = tg // GLA_CHUNK
    up, down = (lambda i: i), (lambda i: nt - 1 - i)

    def body(qf, kf, vf, lrf, stf, dof, qb, kb, vb, lrb, stb, dob, wdf, bdf, wdb, bdb,
             dqf, dkf, dvf, dlrf, dwdf, dbdf, dqb, dkb, dvb, dlrb, dwdb, dbdb, dstate_f, dstate_b):
        @pl.when(pl.program_id(0) == 0)
        def _():
            for ref in (dstate_f, dstate_b, dwdf, dbdf, dwdb, dbdb):
                ref[...] = jnp.zeros_like(ref)

        dirs = [_gla_bwd_dir(False, nc, qf, kf, vf, lrf, wdf, bdf, stf, dof, dqf, dkf, dvf, dlrf, dwdf, dbdf,
                             dstate_f),
                _gla_bwd_dir(True, nc, qb, kb, vb, lrb, wdb, bdb, stb, dob, dqb, dkb, dvb, dlrb, dwdb, dbdb,
                             dstate_b)]
        first = [intra() for intra, _, _ in dirs]
        dsts = [scan(tile, per) for (_, scan, _), (_, tile, per) in zip(dirs, first)]
        for (_, _, inter), (z, tile, per), d in zip(dirs, first, dsts):
            inter(z, tile, per, d)

    wd_spec, bd_spec = _const_spec((LANE, KEY_W)), _const_spec((1, KEY_W))
    ins = lambda tile: _p_specs(tg, tile) + [pl.BlockSpec((nc, GLA_DV, KEY_W), lambda i: (tile(i), 0, 0)),
                                             pl.BlockSpec((tg, VAL_W), lambda i: (tile(i), 0))]
    outs = lambda tile: (pl.BlockSpec((tg, KEY_W), lambda i: (tile(i), 0)),
                         pl.BlockSpec((tg, KEY_W), lambda i: (tile(i), 0)),
                         pl.BlockSpec((tg, VAL_W), lambda i: (tile(i), 0)),
                         pl.BlockSpec((tg, LANE), lambda i: (tile(i), 0)),
                         _acc_spec((LANE, KEY_W)), _acc_spec((1, KEY_W)))
    out_shape = (jax.ShapeDtypeStruct((t, KEY_W), BF16), jax.ShapeDtypeStruct((t, KEY_W), BF16),
                 jax.ShapeDtypeStruct((t, VAL_W), BF16), jax.ShapeDtypeStruct((t, LANE), BF16),
                 jax.ShapeDtypeStruct((LANE, KEY_W), F32), jax.ShapeDtypeStruct((1, KEY_W), F32))
    scratch = [pltpu.VMEM((GLA_DV, KEY_W), F32)]
    return _fused_call(
        body, comms, name="gla_bwd", grid=(nt,),
        inputs=(p, p, p, p, st_f, d_o, p, p, p, p, st_b, d_o, wd_pad_f, bd_f, wd_pad_b, bd_b),
        in_specs=ins(down) + ins(up) + [wd_spec, bd_spec, wd_spec, bd_spec],
        out_specs=outs(down) + outs(up), out_shape=out_shape * 2, scratch_shapes=scratch * 2)


def _head_rms(o):
    parts, scales = [], []
    for h in range(GLA_HEADS):
        oh = o[:, h * GLA_DV:(h + 1) * GLA_DV]
        r = lax.rsqrt(jnp.mean(oh * oh, axis=-1, keepdims=True) + EPS)
        parts.append(oh * r)
        scales.append(jnp.broadcast_to(r, oh.shape))
    return jnp.concatenate(parts, axis=1), jnp.concatenate(scales, axis=1)


def _layernorm_stats(zv):
    mu = jnp.mean(zv, axis=-1, keepdims=True)
    xc = zv - mu
    rs = lax.rsqrt(jnp.mean(xc * xc, axis=-1, keepdims=True) + EPS)
    return xc * rs, rs


def _mix_fwd(x, o_f, o_b, p, gla_g, ln_g, ln_b, w_sp, b_sp, w_out, tm, comms=()):
    t = x.shape[0]
    nch = tm // GMLP_CHUNK

    def body(x_ref, of_ref, ob_ref, pg_ref, pu_ref, pv_ref, gg_ref, lg_ref, lb_ref, ws_ref, bs_ref, wo_ref,
             x1_ref, y_ref, s_scr):
        on, _ = _head_rms(of_ref[...] + ob_ref[...])
        pg = pg_ref[...].astype(F32)
        y_a = on * gg_ref[...] * (pg * _sigmoid(pg))
        zu = _gelu(pu_ref[...].astype(F32))
        vhat, _ = _layernorm_stats(_gelu(pv_ref[...].astype(F32)))
        vln = (vhat * lg_ref[...] + lb_ref[...]).astype(BF16)
        for g in range(GMLP_GROUPS):
            w_g = ws_ref[g].astype(BF16)
            b_g = bs_ref[g]
            cols = slice(g * LANE, (g + 1) * LANE)
            for n in range(nch):
                rows = slice(n * GMLP_CHUNK, (n + 1) * GMLP_CHUNK)
                s_scr[rows, cols] = jnp.dot(w_g, vln[rows, cols], preferred_element_type=F32) + b_g
        ycat = jnp.concatenate([y_a, zu * s_scr[...]], axis=1).astype(BF16)
        y_ref[...] = ycat
        x1_ref[...] = x_ref[...] + jnp.dot(ycat, wo_ref[...], preferred_element_type=F32)

    half = lambda j: pl.BlockSpec((tm, VAL_W), lambda i: (i, j))
    return _fused_call(
        body, comms, name="mix_fwd", grid=(t // tm,),
        inputs=(x, o_f, o_b, p, p, p, gla_g, ln_g, ln_b, w_sp, b_sp, w_out),
        in_specs=[pl.BlockSpec((tm, D_MODEL), lambda i: (i, 0)), half(0), half(0), half(2), half(3), half(4),
                  _const_spec((1, VAL_W)), _const_spec((1, GMLP_W)), _const_spec((1, GMLP_W)),
                  _const_spec((GMLP_GROUPS, GMLP_CHUNK, GMLP_CHUNK)), _const_spec((GMLP_GROUPS, GMLP_CHUNK, 1)),
                  _const_spec((D_MODEL, D_MODEL))],
        out_specs=(pl.BlockSpec((tm, D_MODEL), lambda i: (i, 0)), pl.BlockSpec((tm, D_MODEL), lambda i: (i, 0))),
        out_shape=(jax.ShapeDtypeStruct((t, D_MODEL), F32), jax.ShapeDtypeStruct((t, D_MODEL), BF16)),
        scratch_shapes=[pltpu.VMEM((tm, GMLP_W), F32)])


def _mix_bwd(dx1, ycat, o_f, o_b, p, gla_g, ln_g, ln_b, w_sp, b_sp, w_out, tm, comms=()):
    t = dx1.shape[0]
    nch = tm // GMLP_CHUNK

    def body(dx1_ref, y_ref, of_ref, ob_ref, pg_ref, pu_ref, pv_ref, gg_ref, lg_ref, lb_ref, ws_ref, bs_ref, wo_ref,
             do_ref, dpg_ref, dpu_ref, dpv_ref, dwo_ref, dgg_ref, dlg_ref, dlb_ref, dws_ref, dbs_ref,
             s_scr, dvln_scr):
        @pl.when(pl.program_id(0) == 0)
        def _():
            for ref in (dwo_ref, dgg_ref, dlg_ref, dlb_ref, dws_ref, dbs_ref):
                ref[...] = jnp.zeros_like(ref)

        dx1 = dx1_ref[...].astype(BF16)
        dycat = _mm_nt(dx1, wo_ref[...])
        dwo_ref[...] += _mm_tn(y_ref[...], dx1)
        dy_a = dycat[:, :VAL_W]
        dy_b = dycat[:, VAL_W:]
        on, r = _head_rms(of_ref[...] + ob_ref[...])
        pg = pg_ref[...].astype(F32)
        sil, dsil = _silu_and_grad(pg)
        gg = gg_ref[...]
        dgg_ref[...] += jnp.sum(dy_a * sil * on, axis=0, keepdims=True)
        don = dy_a * sil * gg
        prod = don * on
        means = jnp.concatenate(
            [jnp.broadcast_to(jnp.mean(prod[:, h * GLA_DV:(h + 1) * GLA_DV], axis=-1, keepdims=True),
                              (tm, GLA_DV)) for h in range(GLA_HEADS)], axis=1)
        do_ref[...] = (r * (don - on * means)).astype(BF16)
        dpg_ref[...] = (dy_a * on * gg * dsil).astype(BF16)
        pu = pu_ref[...].astype(F32)
        pv = pv_ref[...].astype(F32)
        zu, dzu_dpu = _gelu_and_grad(pu)
        zv, dzv_dpv = _gelu_and_grad(pv)
        vhat, rs = _layernorm_stats(zv)
        lg = lg_ref[...]
        vln = (vhat * lg + lb_ref[...]).astype(BF16)
        ds32 = dy_b * zu
        ds = ds32.astype(BF16)
        blocks = [(g, n) for g in range(GMLP_GROUPS) for n in range(nch)]
        at = lambda g, n: (slice(n * GMLP_CHUNK, (n + 1) * GMLP_CHUNK), slice(g * LANE, (g + 1) * LANE))
        w_sp = [ws_ref[g].astype(BF16) for g in range(GMLP_GROUPS)]
        v_blk = {b: vln[at(*b)] for b in blocks}
        ds_blk = {b: ds[at(*b)] for b in blocks}
        s_blk = {b: jnp.dot(w_sp[b[0]], v_blk[b], preferred_element_type=F32) for b in blocks}
        dw_blk = {b: _mm_nt(ds_blk[b], v_blk[b]) for b in blocks}
        dvln_blk = {b: _mm_tn(w_sp[b[0]], ds_blk[b]) for b in blocks}
        for b in blocks:
            s_scr[at(*b)] = s_blk[b] + bs_ref[b[0]]
            dvln_scr[at(*b)] = dvln_blk[b]
        for g in range(GMLP_GROUPS):
            dws_ref[g] += sum(dw_blk[(g, n)] for n in range(nch))
            dbs_ref[g] += sum(jnp.sum(ds32[at(g, n)], axis=-1, keepdims=True) for n in range(nch))
        dpu_ref[...] = (dy_b * s_scr[...] * dzu_dpu).astype(BF16)
        dvln = dvln_scr[...]
        dlg_ref[...] += jnp.sum(dvln * vhat, axis=0, keepdims=True)
        dlb_ref[...] += jnp.sum(dvln, axis=0, keepdims=True)
        dvhat = dvln * lg
        dzv = rs * (dvhat - jnp.mean(dvhat, axis=-1, keepdims=True)
                    - vhat * jnp.mean(dvhat * vhat, axis=-1, keepdims=True))
        dpv_ref[...] = (dzv * dzv_dpv).astype(BF16)

    half = lambda j: pl.BlockSpec((tm, VAL_W), lambda i: (i, j))
    full = pl.BlockSpec((tm, D_MODEL), lambda i: (i, 0))
    sp_shape = (GMLP_GROUPS, GMLP_CHUNK, GMLP_CHUNK)
    bs_shape = (GMLP_GROUPS, GMLP_CHUNK, 1)
    return _fused_call(
        body, comms, name="mix_bwd", grid=(t // tm,),
        inputs=(dx1, ycat, o_f, o_b, p, p, p, gla_g, ln_g, ln_b, w_sp, b_sp, w_out),
        in_specs=[full, full, half(0), half(0), half(2), half(3), half(4),
                  _const_spec((1, VAL_W)), _const_spec((1, GMLP_W)), _const_spec((1, GMLP_W)),
                  _const_spec(sp_shape), _const_spec(bs_shape), _const_spec((D_MODEL, D_MODEL))],
        out_specs=(half(0), half(0), half(0), half(0), _acc_spec((D_MODEL, D_MODEL)), _acc_spec((1, VAL_W)),
                   _acc_spec((1, GMLP_W)), _acc_spec((1, GMLP_W)), _acc_spec(sp_shape), _acc_spec(bs_shape)),
        out_shape=(jax.ShapeDtypeStruct((t, VAL_W), BF16),) * 4 + (
            jax.ShapeDtypeStruct((D_MODEL, D_MODEL), F32), jax.ShapeDtypeStruct((1, VAL_W), F32),
            jax.ShapeDtypeStruct((1, GMLP_W), F32), jax.ShapeDtypeStruct((1, GMLP_W), F32),
            jax.ShapeDtypeStruct(sp_shape, F32), jax.ShapeDtypeStruct(bs_shape, F32)),
        scratch_shapes=[pltpu.VMEM((tm, GMLP_W), F32), pltpu.VMEM((tm, GMLP_W), F32)])


def _rms_bwd(dy_scaled, xn, r):
    return r * (dy_scaled - xn * jnp.mean(dy_scaled * xn, axis=-1, keepdims=True))


def _ffn(x1, target, g2, gf, w_gate, w_up, w_down, tm):
    t = x1.shape[0]

    def body(x1_ref, tg_ref, g2_ref, gf_ref, wg_ref, wu_ref, wd_ref,
             dx1_ref, h2_ref, dgate_ref, dup_ref, act_ref, dx2_ref, loss_ref, dgf_ref, dg2_ref):
        @pl.when(pl.program_id(0) == 0)
        def _():
            for ref in (loss_ref, dgf_ref, dg2_ref):
                ref[...] = jnp.zeros_like(ref)

        x1v = x1_ref[...]
        g2v = g2_ref[...]
        gfv = gf_ref[...]
        r2 = lax.rsqrt(jnp.mean(x1v * x1v, axis=-1, keepdims=True) + EPS)
        xn1 = x1v * r2
        h2 = (xn1 * g2v).astype(BF16)
        h2_ref[...] = h2
        gate = _mm_nt(h2, wg_ref[...])
        up = _mm_nt(h2, wu_ref[...])
        sil, dsil = _silu_and_grad(gate)
        act = (sil * up).astype(BF16)
        act_ref[...] = act
        x2 = x1v + jnp.dot(act, wd_ref[...], preferred_element_type=F32)
        rf = lax.rsqrt(jnp.mean(x2 * x2, axis=-1, keepdims=True) + EPS)
        xn2 = x2 * rf
        err = xn2 * gfv - tg_ref[...]
        loss_ref[...] += 0.5 * jnp.sum(jnp.mean(err * err, axis=-1, keepdims=True))
        dy = err * (1.0 / D_MODEL)
        dgf_ref[...] += jnp.sum(dy * xn2, axis=0, keepdims=True)
        dx2 = _rms_bwd(dy * gfv, xn2, rf)
        dx2b = dx2.astype(BF16)
        dx2_ref[...] = dx2b
        dact = _mm_nt(dx2b, wd_ref[...])
        dgate = (dact * up * dsil).astype(BF16)
        dup = (dact * sil).astype(BF16)
        dgate_ref[...] = dgate
        dup_ref[...] = dup
        dh2 = _mm(dgate, wg_ref[...]) + _mm(dup, wu_ref[...])
        dg2_ref[...] += jnp.sum(dh2 * xn1, axis=0, keepdims=True)
        dx1_ref[...] = dx2 + _rms_bwd(dh2 * g2v, xn1, r2)

    row = lambda w: pl.BlockSpec((tm, w), lambda i: (i, 0))
    return pl.pallas_call(
        body, name="ffn_fwd_bwd", grid=(t // tm,),
        in_specs=[row(D_MODEL), row(D_MODEL), _const_spec((1, D_MODEL)), _const_spec((1, D_MODEL)),
                  _const_spec((D_FF, D_MODEL)), _const_spec((D_FF, D_MODEL)), _const_spec((D_FF, D_MODEL))],
        out_specs=(row(D_MODEL), row(D_MODEL), row(D_FF), row(D_FF), row(D_FF), row(D_MODEL),
                   _acc_spec((8, LANE)), _acc_spec((1, D_MODEL)), _acc_spec((1, D_MODEL))),
        out_shape=(jax.ShapeDtypeStruct((t, D_MODEL), F32), jax.ShapeDtypeStruct((t, D_MODEL), BF16),
                   jax.ShapeDtypeStruct((t, D_FF), BF16), jax.ShapeDtypeStruct((t, D_FF), BF16),
                   jax.ShapeDtypeStruct((t, D_FF), BF16), jax.ShapeDtypeStruct((t, D_MODEL), BF16),
                   jax.ShapeDtypeStruct((8, LANE), F32), jax.ShapeDtypeStruct((1, D_MODEL), F32),
                   jax.ShapeDtypeStruct((1, D_MODEL), F32)),
        compiler_params=_params(),
    )(x1, target, g2, gf, w_gate, w_up, w_down)


def _matmul_tn(a, b, tm, tk, name, comms=(), cols=None):
    t, m = a.shape
    cb, n = (0, b.shape[1]) if cols is None else cols

    def body(a_ref, b_ref, o_ref):
        @pl.when(pl.program_id(1) == 0)
        def _():
            o_ref[...] = jnp.zeros_like(o_ref)

        o_ref[...] += _mm_tn(a_ref[...], b_ref[...])

    (out,), comm_results = _fused_call(
        body, comms, name=name, grid=(m // tm, t // tk), inputs=(a, b),
        in_specs=[pl.BlockSpec((tk, tm), lambda j, k: (k, j)), pl.BlockSpec((tk, n), lambda j, k: (k, cb))],
        out_specs=(pl.BlockSpec((tm, n), lambda j, k: (j, 0)),),
        out_shape=(jax.ShapeDtypeStruct((m, n), F32),))
    return out, comm_results


def _in_proj_bwd(x, g1, dx1, dq_f, dq_b, dk_f, dk_b, dv_f, dv_b, dpg, dpu, dpv, dlr_f, dlr_b, w_main, tm, comms=()):
    t = x.shape[0]

    def body(x_ref, g_ref, dx1_ref, dqf, dqb, dkf, dkb, dvf, dvb, dg, du, dv, dlf, dlb, w_ref,
             dx_ref, dp_ref, dg1_ref):
        @pl.when(pl.program_id(0) == 0)
        def _():
            dg1_ref[...] = jnp.zeros_like(dg1_ref)

        both = lambda a, b: (a[...].astype(F32) + b[...].astype(F32)).astype(BF16)
        dp = jnp.concatenate([both(dqf, dqb), both(dkf, dkb), both(dvf, dvb), dg[...], du[...], dv[...],
                              both(dlf, dlb)], axis=1)
        dp_ref[...] = dp
        dh = sum(_mm(dp[:, c0:c0 + r1 - r0], w_ref[r0:r1, :]) for r0, r1, c0 in PROJ_ROWS)
        xv = x_ref[...]
        r = lax.rsqrt(jnp.mean(xv * xv, axis=-1, keepdims=True) + EPS)
        xn = xv * r
        dg1_ref[...] += jnp.sum(dh * xn, axis=0, keepdims=True)
        dx_ref[...] = dx1_ref[...] + _rms_bwd(dh * g_ref[...], xn, r)

    row = lambda w: pl.BlockSpec((tm, w), lambda i: (i, 0))
    return _fused_call(
        body, comms, name="in_proj_bwd", grid=(t // tm,),
        inputs=(x, g1, dx1, dq_f, dq_b, dk_f, dk_b, dv_f, dv_b, dpg, dpu, dpv, dlr_f, dlr_b, w_main),
        in_specs=[row(D_MODEL), _const_spec((1, D_MODEL)), row(D_MODEL), row(KEY_W), row(KEY_W), row(KEY_W),
                  row(KEY_W), row(VAL_W), row(VAL_W), row(VAL_W), row(VAL_W), row(VAL_W), row(LANE), row(LANE),
                  _const_spec((PROJ_W, D_MODEL))],
        out_specs=(row(D_MODEL), row(PROJ_PAD), _acc_spec((1, D_MODEL))),
        out_shape=(jax.ShapeDtypeStruct((t, D_MODEL), F32), jax.ShapeDtypeStruct((t, PROJ_PAD), BF16),
                   jax.ShapeDtypeStruct((1, D_MODEL), F32)))


def _adamw(w, g, m, v):
    m_new = ADAM_B1 * m + (1.0 - ADAM_B1) * g
    v_new = ADAM_B2 * v + (1.0 - ADAM_B2) * (g * g)
    m_hat = m_new / (1.0 - ADAM_B1 ** ADAM_STEP)
    v_hat = v_new / (1.0 - ADAM_B2 ** ADAM_STEP)
    delta = -ADAM_LR * (m_hat / (jnp.sqrt(v_hat) + ADAM_EPS) + ADAM_WD * w)
    return delta, m_new, v_new


def _adamw_window(own, recv, w, m, v, name):
    r, c = w.shape
    rows = own.shape[0]

    def body(own_ref, recv_ref, w_ref, m_ref, v_ref, g_ref, d_ref, nm_ref, nv_ref):
        g = own_ref[...]
        for k in range(3):
            g = g + recv_ref[k].astype(F32)

        def update(g):
            g_ref[...] = g[:r]
            d_ref[...], nm_ref[...], nv_ref[...] = _adamw(w_ref[...], g[:r], m_ref[...], v_ref[...])

        core = lax.axis_index("c")
        pl.when(core == 0)(lambda: update(g))
        pl.when(core == 1)(lambda: update(pltpu.roll(g, rows - 4, 0)))

    whole = lambda *shape: pl.BlockSpec(shape, lambda i: (0,) * len(shape))
    return pl.pallas_call(
        body, name=name, grid=(1,),
        in_specs=[whole(rows, c), whole(3, rows, c), whole(r, c), whole(r, c), whole(r, c)],
        out_specs=(whole(r, c),) * 4, out_shape=(jax.ShapeDtypeStruct((r, c), F32),) * 4,
        compiler_params=_params(),
    )(own, recv, w, m, v)


def _adamw_shard(own, recv, w, m, v, tr, name):
    r, c = w.shape

    def body(own_ref, recv_ref, w_ref, m_ref, v_ref, g_ref, d_ref, nm_ref, nv_ref):
        g = own_ref[...]
        for k in range(3):
            g = g + recv_ref[k].astype(F32)
        g_ref[...] = g
        d_ref[...], nm_ref[...], nv_ref[...] = _adamw(w_ref[...], g, m_ref[...], v_ref[...])

    row = pl.BlockSpec((tr, c), lambda i: (i, 0))
    return pl.pallas_call(
        body, name=name, grid=(r // tr,),
        in_specs=[row, pl.BlockSpec((3, tr, c), lambda i: (0, i, 0)), row, row, row],
        out_specs=(row,) * 4, out_shape=(jax.ShapeDtypeStruct((r, c), F32),) * 4,
        compiler_params=_params(),
    )(own, recv, w, m, v)


def _adamw_small(entries):
    stacks = []
    for (g, _, _), _, _, _ in entries:
        if not any(g is s for s in stacks):
            stacks.append(g)
    where = [next(i for i, s in enumerate(stacks) if s is g) for (g, _, _), _, _, _ in entries]
    ns, ne = len(stacks), len(entries)

    def body(*refs):
        s_refs, wmv, outs = refs[:ns], refs[ns:ns + 3 * ne], refs[ns + 3 * ne:]
        for e, ((_, r0, nr), _, _, _) in enumerate(entries):
            grad = s_refs[where[e]][r0:r0 + nr, :]
            w_ref, m_ref, v_ref = wmv[3 * e:3 * e + 3]
            g_ref, d_ref, nm_ref, nv_ref = outs[4 * e:4 * e + 4]
            g_ref[...] = grad
            d_ref[...], nm_ref[...], nv_ref[...] = _adamw(w_ref[...], grad, m_ref[...], v_ref[...])

    results = pl.pallas_call(
        body, name="adamw_small",
        out_shape=tuple(jax.ShapeDtypeStruct(w.shape, F32) for _, w, _, _ in entries for _ in range(4)),
        compiler_params=pltpu.CompilerParams(vmem_limit_bytes=VMEM_LIMIT),
    )(*stacks, *[a for _, w, m, v in entries for a in (w, m, v)])
    return [results[4 * e:4 * e + 4] for e in range(ne)]


def _mesh_pos():
    return lax.axis_index("x"), lax.axis_index("y"), lax.axis_index("c")


def _other_chips(x, y):
    return [(x, 1 - y), (1 - x, y), (1 - x, 1 - y)]


_VMEM_WHOLE = pl.BlockSpec(memory_space=pltpu.VMEM)
_HBM_WHOLE = pl.BlockSpec(memory_space=pl.ANY)


def _gather_comm(shards, cast, mid=((1, 2), (3, 4))):
    na = len(shards)
    staged = [a for a in range(na) if cast[a]]

    def phases(in_refs, out_refs, scr):
        stage = dict(zip(staged, scr[:len(staged)]))
        send_sems, recv_sems, local_sems = scr[len(staged):]
        x, y, c = _mesh_pos()
        me, sibling = (x, y, c), (x, y, 1 - c)
        chip_a, chip_b, diagonal = (x ^ c, y ^ (1 - c)), (x ^ (1 - c), y ^ c), (1 - x, 1 - y)
        srcs = [stage[a] if cast[a] else in_refs[a] for a in range(na)]

        def rows(a, pos):
            px, py, pc = pos
            return out_refs[a].at[4 * px + 2 * py + pc]

        def copy(a, k, block, to, src=None):
            return pltpu.make_async_remote_copy(
                src_ref=rows(a, block) if src is None else src, dst_ref=rows(a, block),
                send_sem=send_sems.at[a, k], recv_sem=recv_sems.at[a, k], device_id=to, device_id_type=MESH_ID)

        mine = [pltpu.make_async_copy(srcs[a], rows(a, me), local_sems.at[a]) for a in range(na)]
        own = [copy(a, k, me, to, src=srcs[a]) for a in range(na)
               for k, to in ((0, sibling), (1, (*chip_a, c)), (2, (*chip_b, c)))]
        onward = [copy(a, 3, (*chip_a, c), (*chip_b, c)) for a in range(na)]
        to_sibling = {k: [copy(a, k, (*chip, c), sibling) for a in range(na)]
                      for k, chip in ((4, chip_a), (5, chip_b), (6, diagonal))}

        def start():
            for a in staged:
                stage[a][...] = in_refs[a][...].astype(BF16)
            for cp in mine + own:
                cp.start()

        def forward_neighbours():
            for a in range(na):
                copy(a, 1, (*chip_a, c), me).wait_recv()
                onward[a].start()
                to_sibling[4][a].start()
            for a in range(na):
                copy(a, 2, (*chip_b, c), me).wait_recv()
                to_sibling[5][a].start()

        def forward_diagonal():
            for a in range(na):
                copy(a, 3, (*diagonal, c), me).wait_recv()
                to_sibling[6][a].start()

        def finish():
            for a in range(na):
                for k, chip in ((0, (x, y)), (4, chip_b), (5, chip_a), (6, diagonal)):
                    copy(a, k, (*chip, 1 - c), me).wait_recv()
            for cp in own + onward + to_sibling[4] + to_sibling[5] + to_sibling[6]:
                cp.wait_send()
            for cp in mine:
                cp.wait()

        return start, forward_neighbours, forward_diagonal, finish

    def before(step, nsteps, in_refs, out_refs, scr):
        start, forward_neighbours, forward_diagonal, _ = phases(in_refs, out_refs, scr)
        pl.when(step == 0)(start)
        pl.when(step == nsteps * mid[0][0] // mid[0][1])(forward_neighbours)
        pl.when(step == nsteps * mid[1][0] // mid[1][1])(forward_diagonal)

    def after(step, nsteps, in_refs, out_refs, scr):
        pl.when(step == nsteps - 1)(phases(in_refs, out_refs, scr)[3])

    return _Comm(
        inputs=list(shards), in_specs=[_VMEM_WHOLE] * na,
        out_shape=[jax.ShapeDtypeStruct((N_DEV,) + s.shape, BF16 if cast[a] else s.dtype)
                   for a, s in enumerate(shards)],
        out_specs=[_HBM_WHOLE] * na,
        scratch_shapes=[pltpu.VMEM(shards[a].shape, BF16) for a in staged] + [
            pltpu.SemaphoreType.DMA((na, 7)), pltpu.SemaphoreType.DMA((na, 7)), pltpu.SemaphoreType.DMA((na,))],
        before=before, after=after)


W_IN_WINDOW = 336


def _w_in_block_pieces(g_ref, chip, core):
    j = 2 * chip + core
    rows = PROJ_W // N_DEV
    first = rows * j - jnp.where(j > 4, 2 * LOWRANK, 0)
    start = pl.multiple_of((first >> 3) << 3, 8)
    head = LR_REF - 4 * rows
    split = [(g_ref.at[pl.ds(4 * rows, head)], 0, head), (g_ref.at[pl.ds(LR_COL, 2 * LOWRANK)], head, 2 * LOWRANK),
             (g_ref.at[pl.ds(LR_REF, 64)], head + 2 * LOWRANK, 64)]
    return [(j != 4, [(g_ref.at[pl.ds(start, W_IN_WINDOW)], 0, W_IN_WINDOW)]), (j == 4, split)]


def _reduce_scatter_comm(grads, rows=None, pieces=None, mid=(1, 2)):
    if pieces is None:
        _, _, r, c = grads.shape
        pieces = lambda g_ref, chip, core: [(None, [(g_ref.at[chip, core], 0, r)])]
    else:
        r, c = rows, grads.shape[1]
    order = (3, 1, 2, 0)

    def transfer(k, kind, in_refs, scr, act):
        (g_ref,), (sib, own, _, sems) = in_refs, scr
        x, y, core = _mesh_pos()
        chip = (2 * x + y) ^ k
        for cond, parts in pieces(g_ref, chip, 1 - core if kind == "send" else core):
            def run(parts=parts):
                for i, (src, row0, n) in enumerate(parts):
                    if kind == "local":
                        act(pltpu.make_async_copy(src, own.at[k, pl.ds(row0, n)], sems.at[2, 3 * k + i]))
                    else:
                        act(pltpu.make_async_remote_copy(
                            src_ref=src, dst_ref=sib.at[k, pl.ds(row0, n)], send_sem=sems.at[0, 3 * k + i],
                            recv_sem=sems.at[1, 3 * k + i], device_id=(x, y, 1 - core), device_id_type=MESH_ID))

            run() if cond is None else pl.when(cond)(run)

    def chip_copies(out_refs, scr):
        (_, recv), (_, _, part, sems) = out_refs, scr
        x, y, core = _mesh_pos()
        return [pltpu.make_async_remote_copy(
            src_ref=part.at[j], dst_ref=recv.at[j], send_sem=sems.at[3, j], recv_sem=sems.at[4, j],
            device_id=(*chip, core), device_id_type=MESH_ID) for j, chip in enumerate(_other_chips(x, y))]

    def before(step, nsteps, in_refs, out_refs, scr):
        @pl.when(step == 0)
        def _():
            for k in order:
                transfer(k, "send", in_refs, scr, lambda cp: cp.start())
                transfer(k, "local", in_refs, scr, lambda cp: cp.start())

    def after(step, nsteps, in_refs, out_refs, scr):
        sib, own, part, _ = scr

        @pl.when(step == (nsteps - 1) * mid[0] // mid[1])
        def _():
            to_chips = chip_copies(out_refs, scr)
            for k in order:
                transfer(k, "recv", in_refs, scr, lambda cp: cp.wait_recv())
                transfer(k, "local", in_refs, scr, lambda cp: cp.wait())
                if k:
                    part[k - 1] = (own[k] + sib[k]).astype(BF16)
                    to_chips[k - 1].start()
                else:
                    out_refs[0][...] = own[0] + sib[0]
            for k in order:
                transfer(k, "send", in_refs, scr, lambda cp: cp.wait_send())

        @pl.when(step == nsteps - 1)
        def _():
            for cp in chip_copies(out_refs, scr):
                cp.wait()

    return _Comm(inputs=[grads], in_specs=[_HBM_WHOLE],
                 out_shape=[jax.ShapeDtypeStruct((r, c), F32), jax.ShapeDtypeStruct((3, r, c), BF16)],
                 out_specs=[_VMEM_WHOLE, _HBM_WHOLE],
                 scratch_shapes=[pltpu.VMEM((4, r, c), F32), pltpu.VMEM((4, r, c), F32), pltpu.VMEM((3, r, c), BF16),
                                 pltpu.SemaphoreType.DMA((5, 12))],
                 before=before, after=after)


def _comm_only(comms, name):
    return _fused_call(lambda: None, comms, name=name, grid=(1,), inputs=(), in_specs=[], out_specs=(),
                       out_shape=())[1]


def _all_reduce_small_comm(parts):
    na = len(parts)

    def copies(in_refs, scr):
        gathered, (send_sems, recv_sems) = scr[:na], scr[na:]
        x, y, c = _mesh_pos()
        my_id = 4 * x + 2 * y + c
        return my_id, [pltpu.make_async_remote_copy(
            src_ref=in_refs[a], dst_ref=gathered[a].at[my_id], send_sem=send_sems.at[a, k - 1],
            recv_sem=recv_sems.at[a, k - 1], device_id=(x ^ (k >> 2), y ^ ((k >> 1) & 1), c ^ (k & 1)),
            device_id_type=MESH_ID) for a in range(na) for k in range(1, N_DEV)]

    def before(step, nsteps, in_refs, out_refs, scr):
        @pl.when(step == 0)
        def _():
            for cp in copies(in_refs, scr)[1]:
                cp.start()

    def after(step, nsteps, in_refs, out_refs, scr):
        @pl.when(step == nsteps - 1)
        def _():
            my_id, cps = copies(in_refs, scr)
            for a in range(na):
                scr[a][my_id] = in_refs[a][...]
            for cp in cps:
                cp.wait()
            for a in range(na):
                acc = scr[a][0]
                for d in range(1, N_DEV):
                    acc = acc + scr[a][d]
                out_refs[a][...] = acc

    return _Comm(inputs=list(parts), in_specs=[_VMEM_WHOLE] * na,
                 out_shape=[jax.ShapeDtypeStruct(p.shape, F32) for p in parts], out_specs=[_VMEM_WHOLE] * na,
                 scratch_shapes=[pltpu.VMEM((N_DEV,) + p.shape, F32) for p in parts] + [
                     pltpu.SemaphoreType.DMA((na, N_DEV - 1)), pltpu.SemaphoreType.DMA((na, N_DEV - 1))],
                 before=before, after=after)


def _unshard_cols(g):
    return jnp.transpose(g, (1, 0, 2)).reshape(g.shape[1], N_DEV * g.shape[2])


def _row_blocks(w):
    return w.reshape(4, 2, w.shape[0] // N_DEV, w.shape[1])


def _stack_rows(parts):
    a = jnp.concatenate(parts, axis=0)
    return jnp.pad(a, ((0, (-a.shape[0]) % 8), (0, 0)))


def _w_in_grad_blocks(dw):
    return _row_blocks(jnp.concatenate([dw[:LR_REF], dw[LR_COL:LR_COL + 2 * LOWRANK], dw[LR_REF:LR_COL]], axis=0))


def _padded_decay_weights(wd_f, wd_b):
    zeros = lambda n: jnp.zeros((n, KEY_W), F32)
    return (jnp.concatenate([wd_f, zeros(LANE - LOWRANK)], axis=0),
            jnp.concatenate([zeros(LOWRANK), wd_b, zeros(LANE - 2 * LOWRANK)], axis=0))


def kernel(x, norm1_g, w_in,w_decay_f, b_decay_f, w_decay_b, b_decay_b, gla_norm_g, gmlp_ln_g, gmlp_ln_b, w_spatial, b_spatial, w_out, norm2_g, w_gate, w_up, w_down, final_norm_g, loss_target, m_norm1_g, m_w_in, m_w_decay_f, m_b_decay_f, m_w_decay_b, m_b_decay_b, m_gla_norm_g, m_gmlp_ln_g, m_gmlp_ln_b, m_w_spatial, m_b_spatial, m_w_out, m_norm2_g, m_w_gate, m_w_up, m_w_down, m_final_norm_g, v_norm1_g, v_w_in, v_w_decay_f, v_b_decay_f, v_w_decay_b, v_b_decay_b, v_gla_norm_g, v_gmlp_ln_g, v_gmlp_ln_b, v_w_spatial, v_b_spatial, v_w_out, v_norm2_g, v_w_gate, v_w_up, v_w_down, v_final_norm_g):
    t = x.shape[1]
    xt = x[0]
    target = loss_target[0]
    pos_x, pos_y, pos_c = _mesh_pos()
    my_id = 4 * pos_x + 2 * pos_y + pos_c

    tile = lambda n: min(n, t)
    ln_g, ln_b, w_sp = gmlp_ln_g, gmlp_ln_b, w_spatial[0]
    b_sp_col = b_spatial[0][:, :, None]
    shard = {"w_in": w_in[0].T, "w_out": w_out[0], "w_gate": w_gate[0].T, "w_up": w_up[0].T, "w_down": w_down[0]}
    shard_m = {"w_in": m_w_in[0].T, "w_out": m_w_out[0], "w_gate": m_w_gate[0].T, "w_up": m_w_up[0].T,
               "w_down": m_w_down[0]}
    shard_v = {"w_in": v_w_in[0].T, "w_out": v_w_out[0], "w_gate": v_w_gate[0].T, "w_up": v_w_up[0].T,
               "w_down": v_w_down[0]}
    transposed = ("w_in", "w_gate", "w_up")

    decay_shard = jnp.stack([w_decay_f[0], w_decay_b[0]])
    (hb,), ((g_in, g_decay),) = _norm1(xt, norm1_g, tile(TOKEN_TILE["norm1"]),
                                       [_gather_comm([shard["w_in"], decay_shard], [True, False])])
    w_in_t = g_in.reshape(PROJ_W, D_MODEL)
    wd_pad_f, wd_pad_b = _padded_decay_weights(_unshard_cols(g_decay[:, 0]), _unshard_cols(g_decay[:, 1]))
    (p,), ((g_gate, g_out),) = _in_proj(
        hb, w_in_t, tile(TOKEN_TILE["in_proj"]), [_gather_comm([shard["w_gate"], shard["w_out"]], [True, True])])
    (o_f, st_f, o_b, st_b), ((g_up,),) = _gla_fwd(
        p, wd_pad_f, b_decay_f, wd_pad_b, b_decay_b, tile(TOKEN_TILE["gla"]), [_gather_comm([shard["w_up"]], [True])])
    w_out_full = g_out.reshape(D_MODEL, D_MODEL)
    (x1, ycat), ((g_down,),) = _mix_fwd(xt, o_f, o_b, p, gla_norm_g, ln_g, ln_b, w_sp, b_sp_col, w_out_full,
                                        tile(TOKEN_TILE["mix_fwd"]), [_gather_comm([shard["w_down"]], [True])])

    dx1, h2b, dgate, dup, act, dx2, loss_acc, d_gf, d_g2 = _ffn(
        x1, target, norm2_g, final_norm_g[None, :], g_gate.reshape(D_FF, D_MODEL), g_up.reshape(D_FF, D_MODEL),
        g_down.reshape(D_FF, D_MODEL), tile(TOKEN_TILE["ffn"]))
    reduced = {}
    dw_gate, _ = _matmul_tn(dgate, h2b, D_FF // 2, tile(TOKEN_TILE["dw"]), "grad_w_gate")
    dw_up, (reduced["w_gate"],) = _matmul_tn(dup, h2b, D_FF // 2, tile(TOKEN_TILE["dw_host"]), "grad_w_up",
                                             [_reduce_scatter_comm(_row_blocks(dw_gate), mid=DW_HOST_MID)])
    dw_down, (reduced["w_up"],) = _matmul_tn(act, dx2, D_FF // 2, tile(TOKEN_TILE["dw_host"]), "grad_w_down",
                                             [_reduce_scatter_comm(_row_blocks(dw_up), mid=DW_HOST_MID)])

    (d_o, dpg, dpu, dpv, dw_out, d_gg, d_lg, d_lb, dw_sp, db_sp), (reduced["w_down"],) = _mix_bwd(
        dx1, ycat, o_f, o_b, p, gla_norm_g, ln_g, ln_b, w_sp, b_sp_col, w_out_full,
        tile(TOKEN_TILE["mix_bwd"]), [_reduce_scatter_comm(_row_blocks(dw_down))])
    (dq_f, dk_f, dv_f, dlr_f, dwd_f, dbd_f, dq_b, dk_b, dv_b, dlr_b, dwd_b, dbd_b), (reduced["w_out"],) = _gla_bwd(
        p, wd_pad_f, b_decay_f, wd_pad_b, b_decay_b, st_f, st_b, d_o, tile(TOKEN_TILE["gla"]),
        [_reduce_scatter_comm(_row_blocks(dw_out))])
    (grad_x, dp, d_g1), _ = _in_proj_bwd(
        xt, norm1_g, dx1, dq_f, dq_b, dk_f, dk_b, dv_f, dv_b, dpg, dpu, dpv, dlr_f, dlr_b, w_in_t,
        tile(TOKEN_TILE["in_proj_bwd"]))

    stacks = [_stack_rows([d_g1, d_g2, d_gf]), _stack_rows([d_gg, d_lg, d_lb]),
              _stack_rows([dbd_f, dbd_b, jnp.zeros((DECAY_W_ROW - 2, KEY_W), F32), dwd_f[:LOWRANK],
                           dwd_b[LOWRANK:2 * LOWRANK]]),
              _stack_rows([dw_sp.reshape(GMLP_W, GMLP_CHUNK), db_sp[:, :, 0], loss_acc[:1]])]
    dw_main, (small_sums,) = _matmul_tn(dp, hb, PROJ_PAD // 3, tile(TOKEN_TILE["dw"]), "grad_w_in",
                                        [_all_reduce_small_comm(stacks)])
    ((in_own, in_recv),) = _comm_only(
        [_reduce_scatter_comm(dw_main, W_IN_WINDOW, _w_in_block_pieces)], "grad_w_in_reduce_scatter")

    big_out = {"w_in": [r.T for r in _adamw_window(in_own, in_recv, shard["w_in"], shard_m["w_in"], shard_v["w_in"],
                                                   "adamw_w_in")]}
    for n, (own_sum, recv) in reduced.items():
        rows = shard[n].shape[0]
        half = rows // 2 if rows % 32 == 0 else rows
        res = _adamw_shard(own_sum, recv, shard[n], shard_m[n], shard_v[n], half, "adamw_" + n)
        big_out[n] = [r.T if n in transposed else r for r in res]

    s1024, s512, s256, s128 = small_sums
    loss = s128[GMLP_W + GMLP_GROUPS, 0]
    col0 = my_id * (KEY_W // N_DEV)
    decay_cols = lambda row0: lax.dynamic_slice(s256, (row0, col0), (LOWRANK, KEY_W // N_DEV))
    flat = lambda a: a.reshape(-1, a.shape[-1])
    small = {
        "norm1_g": ((s1024, 0, 1), norm1_g, m_norm1_g, v_norm1_g),
        "w_decay_f": ((decay_cols(DECAY_W_ROW), 0, LOWRANK), w_decay_f, m_w_decay_f, v_w_decay_f),
        "b_decay_f": ((s256, 0, 1), b_decay_f, m_b_decay_f, v_b_decay_f),
        "w_decay_b": ((decay_cols(DECAY_W_ROW + LOWRANK), 0, LOWRANK), w_decay_b, m_w_decay_b, v_w_decay_b),
        "b_decay_b": ((s256, 1, 1), b_decay_b, m_b_decay_b, v_b_decay_b),
        "gla_norm_g": ((s512, 0, 1), gla_norm_g, m_gla_norm_g, v_gla_norm_g),
        "gmlp_ln_g": ((s512, 1, 1), gmlp_ln_g, m_gmlp_ln_g, v_gmlp_ln_g),
        "gmlp_ln_b": ((s512, 2, 1), gmlp_ln_b, m_gmlp_ln_b, v_gmlp_ln_b),
        "w_spatial": ((s128, 0, GMLP_W), w_spatial, m_w_spatial, v_w_spatial),
        "b_spatial": ((s128, GMLP_W, GMLP_GROUPS), b_spatial, m_b_spatial, v_b_spatial),
        "norm2_g": ((s1024, 1, 1), norm2_g, m_norm2_g, v_norm2_g),
        "final_norm_g": ((s1024, 2, 1), final_norm_g, m_final_norm_g, v_final_norm_g),
    }
    small_res = _adamw_small([(g, flat(w), flat(m), flat(v)) for g, w, m, v in small.values()])
    small_out = {n: [r.reshape(small[n][1].shape) for r in res] for n, res in zip(small, small_res)}

    order = ["norm1_g", "w_in", "w_decay_f", "b_decay_f", "w_decay_b", "b_decay_b", "gla_norm_g", "gmlp_ln_g",
             "gmlp_ln_b", "w_spatial", "b_spatial", "w_out", "norm2_g", "w_gate", "w_up", "w_down", "final_norm_g"]
    outs = []
    for kind in range(4):
        for n in order:
            outs.append(big_out[n][kind][None] if n in big_out else small_out[n][kind])
    return (loss, grad_x[None], *outs)
```

```python
import functools
import math

import jax
import jax.numpy as jnp
from jax import lax
from jax.experimental import pallas as pl
from jax.experimental.pallas import tpu as pltpu

F32 = jnp.float32
BF16 = jnp.bfloat16

D_MODEL = 1024
GLA_HEADS = 4
GLA_DK = 64
GLA_DV = 128
KEY_W = GLA_HEADS * GLA_DK
VAL_W = GLA_HEADS * GLA_DV
LOWRANK = 16
GLA_TAU = 16.0
GLA_CHUNK = 64
GMLP_W = 512
GMLP_GROUPS = 4
GMLP_CHUNK = 128
D_FF = 2816
EPS = 1e-6
Q_SCALE = GLA_DK ** -0.5
PROJ_PAD = 2688
LR_COL = 2560
LANE = 128
N_DEV = 8

ADAM_LR = 0.001
ADAM_B1 = 0.9
ADAM_B2 = 0.999
ADAM_EPS = 1e-08
ADAM_WD = 0.01
ADAM_STEP = 10

VMEM_LIMIT = 56 * 1024 * 1024
TOKEN_TILE = {"norm1": 512, "in_proj": 512, "gla": 1024, "mix_fwd": 1024, "ffn": 256, "mix_bwd": 512,
              "in_proj_bwd": 512, "dw": 2048, "dw_host": 2048}
DECAY_W_ROW = 8
MESH_ID = pl.DeviceIdType.MESH
INV_SQRT2 = 0.7071067811865476
INV_SQRT_2PI = 0.3989422804014327


def _params(n_axes=1):
    return pltpu.CompilerParams(dimension_semantics=("arbitrary",) * n_axes, vmem_limit_bytes=VMEM_LIMIT)


def _mm(a, b):
    return jnp.dot(a.astype(BF16), b.astype(BF16), preferred_element_type=F32)


def _mm_nt(a, b):
    return lax.dot_general(a.astype(BF16), b.astype(BF16), (((1,), (1,)), ((), ())), preferred_element_type=F32)


def _mm_tn(a, b):
    return lax.dot_general(a.astype(BF16), b.astype(BF16), (((0,), (0,)), ((), ())), preferred_element_type=F32)


def _const_spec(shape):
    nd = len(shape)
    return pl.BlockSpec(shape, lambda *_: (0,) * nd, pipeline_mode=pl.Buffered(1))


def _acc_spec(shape):
    nd = len(shape)
    return pl.BlockSpec(shape, lambda *_: (0,) * nd)


class _Comm:
    def __init__(self, inputs, in_specs, out_shape, out_specs, scratch_shapes, before, after):
        self.inputs, self.in_specs, self.out_shape, self.out_specs = inputs, in_specs, out_shape, out_specs
        self.scratch_shapes, self.before, self.after = scratch_shapes, before, after


def _fused_call(body, comms, *, name, grid, inputs, in_specs, out_specs, out_shape, scratch_shapes=(), prefetch=()):
    n_pre, n_in, n_out, n_scr = len(prefetch), len(in_specs), len(out_specs), len(scratch_shapes)
    nsteps = math.prod(grid)
    sizes = [(len(c.inputs), len(c.out_shape), len(c.scratch_shapes)) for c in comms]

    def full_body(*refs):
        step = pl.program_id(0)
        for axis in range(1, len(grid)):
            step = step * grid[axis] + pl.program_id(axis)
        pre, refs = refs[:n_pre], refs[n_pre:]
        ins, rest = refs[:n_in], refs[n_in:]
        c_ins = []
        for ci, _, _ in sizes:
            c_ins.append(rest[:ci])
            rest = rest[ci:]
        outs, rest = rest[:n_out], rest[n_out:]
        c_outs = []
        for _, co, _ in sizes:
            c_outs.append(rest[:co])
            rest = rest[co:]
        scr, rest = rest[:n_scr], rest[n_scr:]
        c_scr = []
        for _, _, cs in sizes:
            c_scr.append(rest[:cs])
            rest = rest[cs:]
        for c, a, b, s in zip(comms, c_ins, c_outs, c_scr):
            c.before(step, nsteps, a, b, s)
        body(*pre, *ins, *outs, *scr)
        for c, a, b, s in zip(comms, c_ins, c_outs, c_scr):
            c.after(step, nsteps, a, b, s)

    specs = dict(
        grid=grid, in_specs=list(in_specs) + [s for c in comms for s in c.in_specs],
        out_specs=tuple(out_specs) + tuple(s for c in comms for s in c.out_specs),
        scratch_shapes=list(scratch_shapes) + [s for c in comms for s in c.scratch_shapes])
    if n_pre:
        specs = dict(grid_spec=pltpu.PrefetchScalarGridSpec(num_scalar_prefetch=n_pre, **specs))
    results = pl.pallas_call(
        full_body, name=name, **specs,
        out_shape=tuple(out_shape) + tuple(s for c in comms for s in c.out_shape),
        compiler_params=_params(len(grid)),
    )(*prefetch, *inputs, *[a for c in comms for a in c.inputs])
    own, rest = results[:n_out], results[n_out:]
    comm_results = []
    for _, co, _ in sizes:
        comm_results.append(rest[:co])
        rest = rest[co:]
    return own, comm_results


def _gelu(x):
    return 0.5 * x * (1.0 + lax.erf(x * INV_SQRT2))


def _gelu_and_grad(x):
    cdf = 0.5 * (1.0 + lax.erf(x * INV_SQRT2))
    return x * cdf, cdf + x * jnp.exp(-0.5 * x * x) * INV_SQRT_2PI


def _sigmoid(x):
    return 0.5 + 0.5 * jnp.tanh(0.5 * x)


def _silu_and_grad(x):
    s = _sigmoid(x)
    return x * s, s * (1.0 + x * (1.0 - s))


def _norm1(x, g1, tm, comms=()):
    t = x.shape[0]

    def body(x_ref, g_ref, h_ref):
        xv = x_ref[...]
        r = lax.rsqrt(jnp.mean(xv * xv, axis=-1, keepdims=True) + EPS)
        h_ref[...] = (xv * r * g_ref[...]).astype(BF16)

    row = pl.BlockSpec((tm, D_MODEL), lambda i: (i, 0))
    return _fused_call(body, comms, name="norm1", grid=(t // tm,), inputs=(x, g1),
                       in_specs=[row, _const_spec((1, D_MODEL))], out_specs=(row,),
                       out_shape=(jax.ShapeDtypeStruct((t, D_MODEL), BF16),))


PROJ_W = 2592
LR_REF = 1536
PROJ_ROWS = ((0, LR_REF, 0), (LR_REF + 2 * LOWRANK, PROJ_W, LR_REF), (LR_REF, LR_REF + LANE, LR_COL))


def _in_proj(h, w_in_t, tm, comms=()):
    t = h.shape[0]

    def body(h_ref, w_ref, p_ref):
        hv = h_ref[...]
        for r0, r1, c0 in PROJ_ROWS:
            p_ref[:, c0:c0 + r1 - r0] = _mm_nt(hv, w_ref[r0:r1, :]).astype(BF16)

    return _fused_call(
        body, comms, name="in_proj", grid=(t // tm,), inputs=(h, w_in_t),
        in_specs=[pl.BlockSpec((tm, D_MODEL), lambda i: (i, 0)), _const_spec((PROJ_W, D_MODEL))],
        out_specs=(pl.BlockSpec((tm, PROJ_PAD), lambda i: (i, 0)),),
        out_shape=(jax.ShapeDtypeStruct((t, PROJ_PAD), BF16),))


def _tri(upper):
    r = lax.broadcasted_iota(jnp.int32, (GLA_CHUNK, GLA_CHUNK), 0)
    c = lax.broadcasted_iota(jnp.int32, (GLA_CHUNK, GLA_CHUNK), 1)
    return jnp.where((c >= r) if upper else (c <= r), 1.0, 0.0).astype(BF16)


def _chunk_cumsum(tri, a, add=None):
    hi = a.astype(BF16)
    lo = (a - hi.astype(F32)).astype(BF16)
    dot = functools.partial(jnp.dot, preferred_element_type=F32)
    sums = [dot(tri, hi[_chunk_rows(c)]) + dot(tri, lo[_chunk_rows(c)]) for c in range(a.shape[0] // GLA_CHUNK)]
    return jnp.concatenate(sums if add is None else [s + r for s, r in zip(sums, add)], axis=0)


def _chunk_rows(c):
    return slice(c * GLA_CHUNK, (c + 1) * GLA_CHUNK)


def _gla_masks(rev):
    dk_bits, dv_bits = GLA_DK.bit_length() - 1, GLA_DV.bit_length() - 1
    key_head = lax.broadcasted_iota(jnp.int32, (GLA_CHUNK, KEY_W), 1) >> dk_bits
    val_head = lax.broadcasted_iota(jnp.int32, (GLA_CHUNK, VAL_W), 1) >> dv_bits
    t = lax.broadcasted_iota(jnp.int32, (GLA_HEADS * GLA_CHUNK, GLA_CHUNK), 0) & (GLA_CHUNK - 1)
    s = lax.broadcasted_iota(jnp.int32, (GLA_HEADS * GLA_CHUNK, GLA_CHUNK), 1)
    return key_head, val_head, (s >= t) if rev else (s <= t)


def _stack_heads(a, head_of_lane):
    a = a.astype(BF16)
    return jnp.concatenate([jnp.where(head_of_lane == h, a, jnp.zeros_like(a)) for h in range(GLA_HEADS)], axis=0)


def _rows_by_head(a):
    return jnp.concatenate([a[:, h * GLA_DV:(h + 1) * GLA_DV] for h in range(GLA_HEADS)], axis=0)


def _lanes_by_head(r):
    return jnp.concatenate([r[h * GLA_CHUNK:(h + 1) * GLA_CHUNK] for h in range(GLA_HEADS)], axis=1)


def _head_diagonal(r, head_of_lane):
    rows = r.shape[0] // GLA_HEADS
    out = jnp.where(head_of_lane == 0, r[:rows], 0.0)
    for h in range(1, GLA_HEADS):
        out = out + jnp.where(head_of_lane == h, r[h * rows:(h + 1) * rows], 0.0)
    return out


def _tile_terms(la, q, k, tri, rev):
    nc = la.shape[0] // GLA_CHUNK
    q, k = q.astype(F32), k.astype(F32)
    b = _chunk_cumsum(tri, la)
    ebl = [jnp.exp(b[c * GLA_CHUNK:c * GLA_CHUNK + 1] if rev else b[(c + 1) * GLA_CHUNK - 1:(c + 1) * GLA_CHUNK])
           for c in range(nc)]
    eb = jnp.exp(b)
    enb = jnp.exp(-b)
    kd = k * enb
    ke = jnp.concatenate([kd[_chunk_rows(c)] * ebl[c] for c in range(nc)], axis=0)
    return ebl, eb, enb, q * Q_SCALE * eb, kd, ke


def _log_decay(lr_ref, wd_ref, bd_ref):
    z = _mm(lr_ref[...], wd_ref[...]) + bd_ref[...]
    return z, jax.nn.log_sigmoid(z) * (1.0 / GLA_TAU)


def _p_specs(tg, tile):
    return [pl.BlockSpec((tg, KEY_W), lambda i: (tile(i), 0)),
            pl.BlockSpec((tg, KEY_W), lambda i: (tile(i), 1)),
            pl.BlockSpec((tg, VAL_W), lambda i: (tile(i), 1)),
            pl.BlockSpec((tg, LANE), lambda i: (tile(i), LR_COL // LANE))]


def _gla_fwd_dir(rev, nc, q_ref, k_ref, v_ref, lr_ref, wd_ref, bd_ref, o_ref, st_ref, state):
    key_head, _, causal = _gla_masks(rev)
    order = range(nc - 1, -1, -1) if rev else range(nc)

    def intra():
        _, la = _log_decay(lr_ref, wd_ref, bd_ref)
        ebl, _, _, qd, kd, ke = _tile_terms(la, q_ref[...], k_ref[...], _tri(rev), rev)
        kd = kd.astype(BF16)
        v = {c: v_ref[_chunk_rows(c), :].astype(BF16) for c in order}
        qd_stack = {c: _stack_heads(qd[_chunk_rows(c)], key_head) for c in order}
        ke_stack = {c: _stack_heads(ke[_chunk_rows(c)], key_head) for c in order}
        a_all = {c: _mm_nt(qd_stack[c], kd[_chunk_rows(c)]) for c in order}
        a_all = {c: jnp.where(causal, a_all[c], 0.0).astype(BF16) for c in order}
        head_rows = lambda a, h: a[h * GLA_CHUNK:(h + 1) * GLA_CHUNK]
        head_vals = lambda a, h: a[:, h * GLA_DV:(h + 1) * GLA_DV]
        r = {c: [_mm(head_rows(a_all[c], h), head_vals(v[c], h)) for h in range(GLA_HEADS)] for c in order}
        upd = {c: _mm_tn(_rows_by_head(v[c]), ke_stack[c]) for c in order}
        return {c: (ebl[c], qd_stack[c], r[c], upd[c]) for c in order}

    def scan(terms):
        st = state[...]
        states = {}
        for c in order:
            states[c] = st
            st_ref[c] = st.astype(BF16)
            st = st * terms[c][0] + terms[c][3]
        state[...] = st
        return states

    def inter(terms, states):
        r_inter = {c: _mm_nt(terms[c][1], states[c]) for c in order}
        for c in order:
            o_ref[_chunk_rows(c), :] = jnp.concatenate(
                [terms[c][2][h] + r_inter[c][h * GLA_CHUNK:(h + 1) * GLA_CHUNK] for h in range(GLA_HEADS)], axis=1)

    return intra, scan, inter


def _gla_fwd(p, wd_pad_f, bd_f, wd_pad_b, bd_b, tg, comms=()):
    t = p.shape[0]
    nt = t // tg
    nc = tg // GLA_CHUNK
    up, down = (lambda i: i), (lambda i: nt - 1 - i)

    def body(qf, kf, vf, lrf, qb, kb, vb, lrb, wdf, bdf, wdb, bdb, of, stf, ob, stb, state_f, state_b):
        @pl.when(pl.program_id(0) == 0)
        def _():
            state_f[...] = jnp.zeros_like(state_f)
            state_b[...] = jnp.zeros_like(state_b)

        dirs = [_gla_fwd_dir(False, nc, qf, kf, vf, lrf, wdf, bdf, of, stf, state_f),
                _gla_fwd_dir(True, nc, qb, kb, vb, lrb, wdb, bdb, ob, stb, state_b)]
        terms = [intra() for intra, _, _ in dirs]
        states = [scan(t) for (_, scan, _), t in zip(dirs, terms)]
        for (_, _, inter), t, s in zip(dirs, terms, states):
            inter(t, s)

    wd_spec, bd_spec = _const_spec((LANE, KEY_W)), _const_spec((1, KEY_W))
    outs = lambda tile: (pl.BlockSpec((tg, VAL_W), lambda i: (tile(i), 0)),
                         pl.BlockSpec((nc, GLA_DV, KEY_W), lambda i: (tile(i), 0, 0)))
    out_shape = (jax.ShapeDtypeStruct((t, VAL_W), F32), jax.ShapeDtypeStruct((t // GLA_CHUNK, GLA_DV, KEY_W), BF16))
    return _fused_call(
        body, comms, name="gla_fwd", grid=(nt,), inputs=(p,) * 8 + (wd_pad_f, bd_f, wd_pad_b, bd_b),
        in_specs=_p_specs(tg, up) + _p_specs(tg, down) + [wd_spec, bd_spec, wd_spec, bd_spec],
        out_specs=outs(up) + outs(down), out_shape=out_shape * 2,
        scratch_shapes=[pltpu.VMEM((GLA_DV, KEY_W), F32)] * 2)


def _gla_bwd_dir(rev, nc, q_ref, k_ref, v_ref, lr_ref, wd_ref, bd_ref, st_ref, do_ref,
                 dq_ref, dk_ref, dv_ref, dlr_ref, dwd_ref, dbd_ref, dstate):
    key_head, val_head, causal = _gla_masks(rev)
    order = range(nc) if rev else range(nc - 1, -1, -1)

    def intra():
        z, la = _log_decay(lr_ref, wd_ref, bd_ref)
        tile = _tile_terms(la, q_ref[...], k_ref[...], _tri(rev), rev)
        qd, kd = tile[3], tile[4].astype(BF16)
        v = {c: v_ref[_chunk_rows(c), :].astype(BF16) for c in order}
        d_o = {c: do_ref[_chunk_rows(c), :] for c in order}
        kd_c = {c: kd[_chunk_rows(c)] for c in order}
        qd_stack = {c: _stack_heads(qd[_chunk_rows(c)], key_head) for c in order}
        do_stack = {c: _stack_heads(d_o[c], val_head) for c in order}
        do_rows = {c: _rows_by_head(d_o[c]) for c in order}
        a_all = {c: _mm_nt(qd_stack[c], kd_c[c]) for c in order}
        head_vals = lambda a, h: a[:, h * GLA_DV:(h + 1) * GLA_DV]
        da_all = {c: jnp.concatenate([_mm_nt(head_vals(d_o[c], h), head_vals(v[c], h)) for h in range(GLA_HEADS)],
                                     axis=0) for c in order}
        a_all = {c: jnp.where(causal, a_all[c], 0.0).astype(BF16) for c in order}
        da_all = {c: jnp.where(causal, da_all[c], 0.0).astype(BF16) for c in order}
        dv = {c: _mm_tn(a_all[c], do_stack[c]) for c in order}
        dqd = {c: _mm(jnp.concatenate([do_rows[c], da_all[c]], axis=1),
                      jnp.concatenate([st_ref[c], kd_c[c]], axis=0)) for c in order}
        dkd = {c: _mm_tn(da_all[c], qd_stack[c]) for c in order}
        upd = {c: _mm_tn(do_rows[c], qd_stack[c]) for c in order}
        dqd = {c: _head_diagonal(dqd[c], key_head) for c in order}
        return z, tile, {c: dict(dv=dv[c], dqd=dqd[c], dkd=dkd[c], upd=upd[c]) for c in order}

    def scan(tile, per):
        dst = dstate[...]
        dsts = {}
        for c in order:
            dsts[c] = dst
            dst = dst * tile[0][c] + per[c]["upd"]
        dstate[...] = dst
        return dsts

    def inter(z, tile, per, dsts):
        ebl, eb, enb, qd, kd, ke = tile
        ke_stack = {c: _stack_heads(ke[_chunk_rows(c)], key_head) for c in order}
        v_rows = {c: _rows_by_head(v_ref[_chunk_rows(c), :].astype(BF16)) for c in order}
        dst_b = {c: dsts[c].astype(BF16) for c in order}
        dv_state = {c: _mm_nt(ke_stack[c], dst_b[c]) for c in order}
        dke_c = {c: _mm(v_rows[c], dst_b[c]) for c in order}
        dke_c = {c: _head_diagonal(dke_c[c], key_head) for c in order}
        dbl_c = {}
        for c in order:
            rows = _chunk_rows(c)
            dv_ref[rows, :] = (per[c]["dv"] + _lanes_by_head(dv_state[c])).astype(BF16)
            dbl_c[c] = (jnp.sum(dsts[c] * st_ref[c].astype(F32), axis=0, keepdims=True) * ebl[c]
                        + jnp.sum(dke_c[c] * ke[rows], axis=0, keepdims=True))
        tile_of = lambda parts: jnp.concatenate([parts[c] for c in range(nc)], axis=0)
        dqd, dkd = tile_of({c: per[c]["dqd"] for c in order}), tile_of({c: per[c]["dkd"] for c in order})
        dke = tile_of(dke_c)
        dke_end = tile_of({c: dke_c[c] * ebl[c] for c in order})
        dq_ref[...] = (dqd * eb * Q_SCALE).astype(BF16)
        dk_ref[...] = ((dkd + dke_end) * enb).astype(BF16)
        db = dqd * qd - dkd * kd - dke * ke
        dla = _chunk_cumsum(_tri(not rev), db, [dbl_c[c] for c in range(nc)])
        dz = dla * (_sigmoid(-z) * (1.0 / GLA_TAU))
        dlr_ref[...] = _mm_nt(dz, wd_ref[...]).astype(BF16)
        dwd_ref[...] += _mm_tn(lr_ref[...], dz)
        dbd_ref[...] += jnp.sum(dz, axis=0, keepdims=True)

    return intra, scan, inter


def _gla_bwd(p, wd_pad_f, bd_f, wd_pad_b, bd_b, st_f, st_b, d_o, tg, comms=()):
    t = p.shape[0]
    nt = t // tg
    nc = tg // GLA_CHUNK
    up, down = (lambda i: i), (lambda i: nt - 1 - i)

    def body(qf, kf, vf, lrf, stf, dof, qb, kb, vb, lrb, stb, dob, wdf, bdf, wdb, bdb,
             dqf, dkf, dvf, dlrf, dwdf, dbdf, dqb, dkb, dvb, dlrb, dwdb, dbdb, dstate_f, dstate_b):
        @pl.when(pl.program_id(0) == 0)
        def _():
            for ref in (dstate_f, dstate_b, dwdf, dbdf, dwdb, dbdb):
                ref[...] = jnp.zeros_like(ref)

        dirs = [_gla_bwd_dir(False, nc, qf, kf, vf, lrf, wdf, bdf, stf, dof, dqf, dkf, dvf, dlrf, dwdf, dbdf,
                             dstate_f),
                _gla_bwd_dir(True, nc, qb, kb, vb, lrb, wdb, bdb, stb, dob, dqb, dkb, dvb, dlrb, dwdb, dbdb,
                             dstate_b)]
        first = [intra() for intra, _, _ in dirs]
        dsts = [scan(tile, per) for (_, scan, _), (_, tile, per) in zip(dirs, first)]
        for (_, _, inter), (z, tile, per), d in zip(dirs, first, dsts):
            inter(z, tile, per, d)

    wd_spec, bd_spec = _const_spec((LANE, KEY_W)), _const_spec((1, KEY_W))
    ins = lambda tile: _p_specs(tg, tile) + [pl.BlockSpec((nc, GLA_DV, KEY_W), lambda i: (tile(i), 0, 0)),
                                             pl.BlockSpec((tg, VAL_W), lambda i: (tile(i), 0))]
    outs = lambda tile: (pl.BlockSpec((tg, KEY_W), lambda i: (tile(i), 0)),
                         pl.BlockSpec((tg, KEY_W), lambda i: (tile(i), 0)),
                         pl.BlockSpec((tg, VAL_W), lambda i: (tile(i), 0)),
                         pl.BlockSpec((tg, LANE), lambda i: (tile(i), 0)),
                         _acc_spec((LANE, KEY_W)), _acc_spec((1, KEY_W)))
    out_shape = (jax.ShapeDtypeStruct((t, KEY_W), BF16), jax.ShapeDtypeStruct((t, KEY_W), BF16),
                 jax.ShapeDtypeStruct((t, VAL_W), BF16), jax.ShapeDtypeStruct((t, LANE), BF16),
                 jax.ShapeDtypeStruct((LANE, KEY_W), F32), jax.ShapeDtypeStruct((1, KEY_W), F32))
    scratch = [pltpu.VMEM((GLA_DV, KEY_W), F32)]
    return _fused_call(
        body, comms, name="gla_bwd", grid=(nt,),
        inputs=(p, p, p, p, st_f, d_o, p, p, p, p, st_b, d_o, wd_pad_f, bd_f, wd_pad_b, bd_b),
        in_specs=ins(down) + ins(up) + [wd_spec, bd_spec, wd_spec, bd_spec],
        out_specs=outs(down) + outs(up), out_shape=out_shape * 2, scratch_shapes=scratch * 2)


def _head_rms(o):
    parts, scales = [], []
    for h in range(GLA_HEADS):
        oh = o[:, h * GLA_DV:(h + 1) * GLA_DV]
        r = lax.rsqrt(jnp.mean(oh * oh, axis=-1, keepdims=True) + EPS)
        parts.append(oh * r)
        scales.append(jnp.broadcast_to(r, oh.shape))
    return jnp.concatenate(parts, axis=1), jnp.concatenate(scales, axis=1)


def _layernorm_stats(zv):
    mu = jnp.mean(zv, axis=-1, keepdims=True)
    xc = zv - mu
    rs = lax.rsqrt(jnp.mean(xc * xc, axis=-1, keepdims=True) + EPS)
    return xc * rs, rs


def _mix_fwd(x, o_f, o_b, p, gla_g, ln_g, ln_b, w_sp, b_sp, w_out, tm, comms=()):
    t = x.shape[0]
    nch = tm // GMLP_CHUNK

    def body(x_ref, of_ref, ob_ref, pg_ref, pu_ref, pv_ref, gg_ref, lg_ref, lb_ref, ws_ref, bs_ref, wo_ref,
             x1_ref, y_ref, s_scr):
        on, _ = _head_rms(of_ref[...] + ob_ref[...])
        pg = pg_ref[...].astype(F32)
        y_a = on * gg_ref[...] * (pg * _sigmoid(pg))
        zu = _gelu(pu_ref[...].astype(F32))
        vhat, _ = _layernorm_stats(_gelu(pv_ref[...].astype(F32)))
        vln = (vhat * lg_ref[...] + lb_ref[...]).astype(BF16)
        for g in range(GMLP_GROUPS):
            w_g = ws_ref[g].astype(BF16)
            b_g = bs_ref[g]
            cols = slice(g * LANE, (g + 1) * LANE)
            for n in range(nch):
                rows = slice(n * GMLP_CHUNK, (n + 1) * GMLP_CHUNK)
                s_scr[rows, cols] = jnp.dot(w_g, vln[rows, cols], preferred_element_type=F32) + b_g
        ycat = jnp.concatenate([y_a, zu * s_scr[...]], axis=1).astype(BF16)
        y_ref[...] = ycat
        x1_ref[...] = x_ref[...] + jnp.dot(ycat, wo_ref[...], preferred_element_type=F32)

    half = lambda j: pl.BlockSpec((tm, VAL_W), lambda i: (i, j))
    return _fused_call(
        body, comms, name="mix_fwd", grid=(t // tm,),
        inputs=(x, o_f, o_b, p, p, p, gla_g, ln_g, ln_b, w_sp, b_sp, w_out),
        in_specs=[pl.BlockSpec((tm, D_MODEL), lambda i: (i, 0)), half(0), half(0), half(2), half(3), half(4),
                  _const_spec((1, VAL_W)), _const_spec((1, GMLP_W)), _const_spec((1, GMLP_W)),
                  _const_spec((GMLP_GROUPS, GMLP_CHUNK, GMLP_CHUNK)), _const_spec((GMLP_GROUPS, GMLP_CHUNK, 1)),
                  _const_spec((D_MODEL, D_MODEL))],
        out_specs=(pl.BlockSpec((tm, D_MODEL), lambda i: (i, 0)), pl.BlockSpec((tm, D_MODEL), lambda i: (i, 0))),
        out_shape=(jax.ShapeDtypeStruct((t, D_MODEL), F32), jax.ShapeDtypeStruct((t, D_MODEL), BF16)),
        scratch_shapes=[pltpu.VMEM((tm, GMLP_W), F32)])


def _mix_bwd(dx1, ycat, o_f, o_b, p, gla_g, ln_g, ln_b, w_sp, b_sp, w_out, tm, comms=()):
    t = dx1.shape[0]
    nch = tm // GMLP_CHUNK

    def body(dx1_ref, y_ref, of_ref, ob_ref, pg_ref, pu_ref, pv_ref, gg_ref, lg_ref, lb_ref, ws_ref, bs_ref, wo_ref,
             do_ref, dpg_ref, dpu_ref, dpv_ref, dwo_ref, dgg_ref, dlg_ref, dlb_ref, dws_ref, dbs_ref,
             s_scr, dvln_scr):
        @pl.when(pl.program_id(0) == 0)
        def _():
            for ref in (dwo_ref, dgg_ref, dlg_ref, dlb_ref, dws_ref, dbs_ref):
                ref[...] = jnp.zeros_like(ref)

        dx1 = dx1_ref[...].astype(BF16)
        dycat = _mm_nt(dx1, wo_ref[...])
        dwo_ref[...] += _mm_tn(y_ref[...], dx1)
        dy_a = dycat[:, :VAL_W]
        dy_b = dycat[:, VAL_W:]
        on, r = _head_rms(of_ref[...] + ob_ref[...])
        pg = pg_ref[...].astype(F32)
        sil, dsil = _silu_and_grad(pg)
        gg = gg_ref[...]
        dgg_ref[...] += jnp.sum(dy_a * sil * on, axis=0, keepdims=True)
        don = dy_a * sil * gg
        prod = don * on
        means = jnp.concatenate(
            [jnp.broadcast_to(jnp.mean(prod[:, h * GLA_DV:(h + 1) * GLA_DV], axis=-1, keepdims=True),
                              (tm, GLA_DV)) for h in range(GLA_HEADS)], axis=1)
        do_ref[...] = (r * (don - on * means)).astype(BF16)
        dpg_ref[...] = (dy_a * on * gg * dsil).astype(BF16)
        pu = pu_ref[...].astype(F32)
        pv = pv_ref[...].astype(F32)
        zu, dzu_dpu = _gelu_and_grad(pu)
        zv, dzv_dpv = _gelu_and_grad(pv)
        vhat, rs = _layernorm_stats(zv)
        lg = lg_ref[...]
        vln = (vhat * lg + lb_ref[...]).astype(BF16)
        ds32 = dy_b * zu
        ds = ds32.astype(BF16)
        blocks = [(g, n) for g in range(GMLP_GROUPS) for n in range(nch)]
        at = lambda g, n: (slice(n * GMLP_CHUNK, (n + 1) * GMLP_CHUNK), slice(g * LANE, (g + 1) * LANE))
        w_sp = [ws_ref[g].astype(BF16) for g in range(GMLP_GROUPS)]
        v_blk = {b: vln[at(*b)] for b in blocks}
        ds_blk = {b: ds[at(*b)] for b in blocks}
        s_blk = {b: jnp.dot(w_sp[b[0]], v_blk[b], preferred_element_type=F32) for b in blocks}
        dw_blk = {b: _mm_nt(ds_blk[b], v_blk[b]) for b in blocks}
        dvln_blk = {b: _mm_tn(w_sp[b[0]], ds_blk[b]) for b in blocks}
        for b in blocks:
            s_scr[at(*b)] = s_blk[b] + bs_ref[b[0]]
            dvln_scr[at(*b)] = dvln_blk[b]
        for g in range(GMLP_GROUPS):
            dws_ref[g] += sum(dw_blk[(g, n)] for n in range(nch))
            dbs_ref[g] += sum(jnp.sum(ds32[at(g, n)], axis=-1, keepdims=True) for n in range(nch))
        dpu_ref[...] = (dy_b * s_scr[...] * dzu_dpu).astype(BF16)
        dvln = dvln_scr[...]
        dlg_ref[...] += jnp.sum(dvln * vhat, axis=0, keepdims=True)
        dlb_ref[...] += jnp.sum(dvln, axis=0, keepdims=True)
        dvhat = dvln * lg
        dzv = rs * (dvhat - jnp.mean(dvhat, axis=-1, keepdims=True)
                    - vhat * jnp.mean(dvhat * vhat, axis=-1, keepdims=True))
        dpv_ref[...] = (dzv * dzv_dpv).astype(BF16)

    half = lambda j: pl.BlockSpec((tm, VAL_W), lambda i: (i, j))
    full = pl.BlockSpec((tm, D_MODEL), lambda i: (i, 0))
    sp_shape = (GMLP_GROUPS, GMLP_CHUNK, GMLP_CHUNK)
    bs_shape = (GMLP_GROUPS, GMLP_CHUNK, 1)
    return _fused_call(
        body, comms, name="mix_bwd", grid=(t // tm,),
        inputs=(dx1, ycat, o_f, o_b, p, p, p, gla_g, ln_g, ln_b, w_sp, b_sp, w_out),
        in_specs=[full, full, half(0), half(0), half(2), half(3), half(4),
                  _const_spec((1, VAL_W)), _const_spec((1, GMLP_W)), _const_spec((1, GMLP_W)),
                  _const_spec(sp_shape), _const_spec(bs_shape), _const_spec((D_MODEL, D_MODEL))],
        out_specs=(half(0), half(0), half(0), half(0), _acc_spec((D_MODEL, D_MODEL)), _acc_spec((1, VAL_W)),
                   _acc_spec((1, GMLP_W)), _acc_spec((1, GMLP_W)), _acc_spec(sp_shape), _acc_spec(bs_shape)),
        out_shape=(jax.ShapeDtypeStruct((t, VAL_W), BF16),) * 4 + (
            jax.ShapeDtypeStruct((D_MODEL, D_MODEL), F32), jax.ShapeDtypeStruct((1, VAL_W), F32),
            jax.ShapeDtypeStruct((1, GMLP_W), F32), jax.ShapeDtypeStruct((1, GMLP_W), F32),
            jax.ShapeDtypeStruct(sp_shape, F32), jax.ShapeDtypeStruct(bs_shape, F32)),
        scratch_shapes=[pltpu.VMEM((tm, GMLP_W), F32), pltpu.VMEM((tm, GMLP_W), F32)])


def _rms_bwd(dy_scaled, xn, r):
    return r * (dy_scaled - xn * jnp.mean(dy_scaled * xn, axis=-1, keepdims=True))


def _ffn(x1, target, g2, gf, w_gate, w_up, w_down, tm):
    t = x1.shape[0]

    def body(x1_ref, tg_ref, g2_ref, gf_ref, wg_ref, wu_ref, wd_ref,
             dx1_ref, h2_ref, dgate_ref, dup_ref, act_ref, dx2_ref, loss_ref, dgf_ref, dg2_ref):
        @pl.when(pl.program_id(0) == 0)
        def _():
            for ref in (loss_ref, dgf_ref, dg2_ref):
                ref[...] = jnp.zeros_like(ref)

        x1v = x1_ref[...]
        g2v = g2_ref[...]
        gfv = gf_ref[...]
        r2 = lax.rsqrt(jnp.mean(x1v * x1v, axis=-1, keepdims=True) + EPS)
        xn1 = x1v * r2
        h2 = (xn1 * g2v).astype(BF16)
        h2_ref[...] = h2
        gate = _mm_nt(h2, wg_ref[...])
        up = _mm_nt(h2, wu_ref[...])
        sil, dsil = _silu_and_grad(gate)
        act = (sil * up).astype(BF16)
        act_ref[...] = act
        x2 = x1v + jnp.dot(act, wd_ref[...], preferred_element_type=F32)
        rf = lax.rsqrt(jnp.mean(x2 * x2, axis=-1, keepdims=True) + EPS)
        xn2 = x2 * rf
        err = xn2 * gfv - tg_ref[...]
        loss_ref[...] += 0.5 * jnp.sum(jnp.mean(err * err, axis=-1, keepdims=True))
        dy = err * (1.0 / D_MODEL)
        dgf_ref[...] += jnp.sum(dy * xn2, axis=0, keepdims=True)
        dx2 = _rms_bwd(dy * gfv, xn2, rf)
        dx2b = dx2.astype(BF16)
        dx2_ref[...] = dx2b
        dact = _mm_nt(dx2b, wd_ref[...])
        dgate = (dact * up * dsil).astype(BF16)
        dup = (dact * sil).astype(BF16)
        dgate_ref[...] = dgate
        dup_ref[...] = dup
        dh2 = _mm(dgate, wg_ref[...]) + _mm(dup, wu_ref[...])
        dg2_ref[...] += jnp.sum(dh2 * xn1, axis=0, keepdims=True)
        dx1_ref[...] = dx2 + _rms_bwd(dh2 * g2v, xn1, r2)

    row = lambda w: pl.BlockSpec((tm, w), lambda i: (i, 0))
    return pl.pallas_call(
        body, name="ffn_fwd_bwd", grid=(t // tm,),
        in_specs=[row(D_MODEL), row(D_MODEL), _const_spec((1, D_MODEL)), _const_spec((1, D_MODEL)),
                  _const_spec((D_FF, D_MODEL)), _const_spec((D_FF, D_MODEL)), _const_spec((D_FF, D_MODEL))],
        out_specs=(row(D_MODEL), row(D_MODEL), row(D_FF), row(D_FF), row(D_FF), row(D_MODEL),
                   _acc_spec((8, LANE)), _acc_spec((1, D_MODEL)), _acc_spec((1, D_MODEL))),
        out_shape=(jax.ShapeDtypeStruct((t, D_MODEL), F32), jax.ShapeDtypeStruct((t, D_MODEL), BF16),
                   jax.ShapeDtypeStruct((t, D_FF), BF16), jax.ShapeDtypeStruct((t, D_FF), BF16),
                   jax.ShapeDtypeStruct((t, D_FF), BF16), jax.ShapeDtypeStruct((t, D_MODEL), BF16),
                   jax.ShapeDtypeStruct((8, LANE), F32), jax.ShapeDtypeStruct((1, D_MODEL), F32),
                   jax.ShapeDtypeStruct((1, D_MODEL), F32)),
        compiler_params=_params(),
    )(x1, target, g2, gf, w_gate, w_up, w_down)


def _matmul_tn(a, b, tm, tk, name, comms=(), cols=None):
    t, m = a.shape
    cb, n = (0, b.shape[1]) if cols is None else cols

    def body(a_ref, b_ref, o_ref):
        @pl.when(pl.program_id(1) == 0)
        def _():
            o_ref[...] = jnp.zeros_like(o_ref)

        o_ref[...] += _mm_tn(a_ref[...], b_ref[...])

    (out,), comm_results = _fused_call(
        body, comms, name=name, grid=(m // tm, t // tk), inputs=(a, b),
        in_specs=[pl.BlockSpec((tk, tm), lambda j, k: (k, j)), pl.BlockSpec((tk, n), lambda j, k: (k, cb))],
        out_specs=(pl.BlockSpec((tm, n), lambda j, k: (j, 0)),),
        out_shape=(jax.ShapeDtypeStruct((m, n), F32),))
    return out, comm_results


def _in_proj_bwd(x, g1, dx1, dq_f, dq_b, dk_f, dk_b, dv_f, dv_b, dpg, dpu, dpv, dlr_f, dlr_b, w_main, tm, comms=()):
    t = x.shape[0]

    def body(x_ref, g_ref, dx1_ref, dqf, dqb, dkf, dkb, dvf, dvb, dg, du, dv, dlf, dlb, w_ref,
             dx_ref, dp_ref, dg1_ref):
        @pl.when(pl.program_id(0) == 0)
        def _():
            dg1_ref[...] = jnp.zeros_like(dg1_ref)

        both = lambda a, b: (a[...].astype(F32) + b[...].astype(F32)).astype(BF16)
        dp = jnp.concatenate([both(dqf, dqb), both(dkf, dkb), both(dvf, dvb), dg[...], du[...], dv[...],
                              both(dlf, dlb)], axis=1)
        dp_ref[...] = dp
        dh = sum(_mm(dp[:, c0:c0 + r1 - r0], w_ref[r0:r1, :]) for r0, r1, c0 in PROJ_ROWS)
        xv = x_ref[...]
        r = lax.rsqrt(jnp.mean(xv * xv, axis=-1, keepdims=True) + EPS)
        xn = xv * r
        dg1_ref[...] += jnp.sum(dh * xn, axis=0, keepdims=True)
        dx_ref[...] = dx1_ref[...] + _rms_bwd(dh * g_ref[...], xn, r)

    row = lambda w: pl.BlockSpec((tm, w), lambda i: (i, 0))
    return _fused_call(
        body, comms, name="in_proj_bwd", grid=(t // tm,),
        inputs=(x, g1, dx1, dq_f, dq_b, dk_f, dk_b, dv_f, dv_b, dpg, dpu, dpv, dlr_f, dlr_b, w_main),
        in_specs=[row(D_MODEL), _const_spec((1, D_MODEL)), row(D_MODEL), row(KEY_W), row(KEY_W), row(KEY_W),
                  row(KEY_W), row(VAL_W), row(VAL_W), row(VAL_W), row(VAL_W), row(VAL_W), row(LANE), row(LANE),
                  _const_spec((PROJ_W, D_MODEL))],
        out_specs=(row(D_MODEL), row(PROJ_PAD), _acc_spec((1, D_MODEL))),
        out_shape=(jax.ShapeDtypeStruct((t, D_MODEL), F32), jax.ShapeDtypeStruct((t, PROJ_PAD), BF16),
                   jax.ShapeDtypeStruct((1, D_MODEL), F32)))


def _adamw(w, g, m, v):
    m_new = ADAM_B1 * m + (1.0 - ADAM_B1) * g
    v_new = ADAM_B2 * v + (1.0 - ADAM_B2) * (g * g)
    m_hat = m_new / (1.0 - ADAM_B1 ** ADAM_STEP)
    v_hat = v_new / (1.0 - ADAM_B2 ** ADAM_STEP)
    delta = -ADAM_LR * (m_hat / (jnp.sqrt(v_hat) + ADAM_EPS) + ADAM_WD * w)
    return delta, m_new, v_new


def _adamw_window(own, recv, w, m, v, name):
    r, c = w.shape
    rows = own.shape[0]

    def body(own_ref, recv_ref, w_ref, m_ref, v_ref, g_ref, d_ref, nm_ref, nv_ref):
        g = own_ref[...]
        for k in range(3):
            g = g + recv_ref[k].astype(F32)

        def update(g):
            g_ref[...] = g[:r]
            d_ref[...], nm_ref[...], nv_ref[...] = _adamw(w_ref[...], g[:r], m_ref[...], v_ref[...])

        core = lax.axis_index("c")
        pl.when(core == 0)(lambda: update(g))
        pl.when(core == 1)(lambda: update(pltpu.roll(g, rows - 4, 0)))

    whole = lambda *shape: pl.BlockSpec(shape, lambda i: (0,) * len(shape))
    return pl.pallas_call(
        body, name=name, grid=(1,),
        in_specs=[whole(rows, c), whole(3, rows, c), whole(r, c), whole(r, c), whole(r, c)],
        out_specs=(whole(r, c),) * 4, out_shape=(jax.ShapeDtypeStruct((r, c), F32),) * 4,
        compiler_params=_params(),
    )(own, recv, w, m, v)


def _adamw_shard(own, recv, w, m, v, tr, name):
    r, c = w.shape

    def body(own_ref, recv_ref, w_ref, m_ref, v_ref, g_ref, d_ref, nm_ref, nv_ref):
        g = own_ref[...]
        for k in range(3):
            g = g + recv_ref[k].astype(F32)
        g_ref[...] = g
        d_ref[...], nm_ref[...], nv_ref[...] = _adamw(w_ref[...], g, m_ref[...], v_ref[...])

    row = pl.BlockSpec((tr, c), lambda i: (i, 0))
    return pl.pallas_call(
        body, name=name, grid=(r // tr,),
        in_specs=[row, pl.BlockSpec((3, tr, c), lambda i: (0, i, 0)), row, row, row],
        out_specs=(row,) * 4, out_shape=(jax.ShapeDtypeStruct((r, c), F32),) * 4,
        compiler_params=_params(),
    )(own, recv, w, m, v)


def _adamw_small(entries):
    stacks = []
    for (g, _, _), _, _, _ in entries:
        if not any(g is s for s in stacks):
            stacks.append(g)
    where = [next(i for i, s in enumerate(stacks) if s is g) for (g, _, _), _, _, _ in entries]
    ns, ne = len(stacks), len(entries)

    def body(*refs):
        s_refs, wmv, outs = refs[:ns], refs[ns:ns + 3 * ne], refs[ns + 3 * ne:]
        for e, ((_, r0, nr), _, _, _) in enumerate(entries):
            grad = s_refs[where[e]][r0:r0 + nr, :]
            w_ref, m_ref, v_ref = wmv[3 * e:3 * e + 3]
            g_ref, d_ref, nm_ref, nv_ref = outs[4 * e:4 * e + 4]
            g_ref[...] = grad
            d_ref[...], nm_ref[...], nv_ref[...] = _adamw(w_ref[...], grad, m_ref[...], v_ref[...])

    results = pl.pallas_call(
        body, name="adamw_small",
        out_shape=tuple(jax.ShapeDtypeStruct(w.shape, F32) for _, w, _, _ in entries for _ in range(4)),
        compiler_params=pltpu.CompilerParams(vmem_limit_bytes=VMEM_LIMIT),
    )(*stacks, *[a for _, w, m, v in entries for a in (w, m, v)])
    return [results[4 * e:4 * e + 4] for e in range(ne)]


def _mesh_pos():
    return lax.axis_index("x"), lax.axis_index("y"), lax.axis_index("c")


def _other_chips(x, y):
    return [(x, 1 - y), (1 - x, y), (1 - x, 1 - y)]


_VMEM_WHOLE = pl.BlockSpec(memory_space=pltpu.VMEM)
_HBM_WHOLE = pl.BlockSpec(memory_space=pl.ANY)


def _gather_comm(shards, cast, mid=((1, 2), (3, 4))):
    na = len(shards)
    staged = [a for a in range(na) if cast[a]]

    def phases(in_refs, out_refs, scr):
        stage = dict(zip(staged, scr[:len(staged)]))
        send_sems, recv_sems, local_sems = scr[len(staged):]
        x, y, c = _mesh_pos()
        me, sibling = (x, y, c), (x, y, 1 - c)
        chip_a, chip_b, diagonal = (x ^ c, y ^ (1 - c)), (x ^ (1 - c), y ^ c), (1 - x, 1 - y)
        srcs = [stage[a] if cast[a] else in_refs[a] for a in range(na)]

        def rows(a, pos):
            px, py, pc = pos
            return out_refs[a].at[4 * px + 2 * py + pc]

        def copy(a, k, block, to, src=None):
            return pltpu.make_async_remote_copy(
                src_ref=rows(a, block) if src is None else src, dst_ref=rows(a, block),
                send_sem=send_sems.at[a, k], recv_sem=recv_sems.at[a, k], device_id=to, device_id_type=MESH_ID)

        mine = [pltpu.make_async_copy(srcs[a], rows(a, me), local_sems.at[a]) for a in range(na)]
        own = [copy(a, k, me, to, src=srcs[a]) for a in range(na)
               for k, to in ((0, sibling), (1, (*chip_a, c)), (2, (*chip_b, c)))]
        onward = [copy(a, 3, (*chip_a, c), (*chip_b, c)) for a in range(na)]
        to_sibling = {k: [copy(a, k, (*chip, c), sibling) for a in range(na)]
                      for k, chip in ((4, chip_a), (5, chip_b), (6, diagonal))}

        def start():
            for a in staged:
                stage[a][...] = in_refs[a][...].astype(BF16)
            for cp in mine + own:
                cp.start()

        def forward_neighbours():
            for a in range(na):
                copy(a, 1, (*chip_a, c), me).wait_recv()
                onward[a].start()
                to_sibling[4][a].start()
            for a in range(na):
                copy(a, 2, (*chip_b, c), me).wait_recv()
                to_sibling[5][a].start()

        def forward_diagonal():
            for a in range(na):
                copy(a, 3, (*diagonal, c), me).wait_recv()
                to_sibling[6][a].start()

        def finish():
            for a in range(na):
                for k, chip in ((0, (x, y)), (4, chip_b), (5, chip_a), (6, diagonal)):
                    copy(a, k, (*chip, 1 - c), me).wait_recv()
            for cp in own + onward + to_sibling[4] + to_sibling[5] + to_sibling[6]:
                cp.wait_send()
            for cp in mine:
                cp.wait()

        return start, forward_neighbours, forward_diagonal, finish

    def before(step, nsteps, in_refs, out_refs, scr):
        start, forward_neighbours, forward_diagonal, _ = phases(in_refs, out_refs, scr)
        pl.when(step == 0)(start)
        pl.when(step == nsteps * mid[0][0] // mid[0][1])(forward_neighbours)
        pl.when(step == nsteps * mid[1][0] // mid[1][1])(forward_diagonal)

    def after(step, nsteps, in_refs, out_refs, scr):
        pl.when(step == nsteps - 1)(phases(in_refs, out_refs, scr)[3])

    return _Comm(
        inputs=list(shards), in_specs=[_VMEM_WHOLE] * na,
        out_shape=[jax.ShapeDtypeStruct((N_DEV,) + s.shape, BF16 if cast[a] else s.dtype)
                   for a, s in enumerate(shards)],
        out_specs=[_HBM_WHOLE] * na,
        scratch_shapes=[pltpu.VMEM(shards[a].shape, BF16) for a in staged] + [
            pltpu.SemaphoreType.DMA((na, 7)), pltpu.SemaphoreType.DMA((na, 7)), pltpu.SemaphoreType.DMA((na,))],
        before=before, after=after)


W_IN_WINDOW = 336


def _w_in_block_pieces(g_ref, chip, core):
    j = 2 * chip + core
    rows = PROJ_W // N_DEV
    first = rows * j - jnp.where(j > 4, 2 * LOWRANK, 0)
    start = pl.multiple_of((first >> 3) << 3, 8)
    head = LR_REF - 4 * rows
    split = [(g_ref.at[pl.ds(4 * rows, head)], 0, head), (g_ref.at[pl.ds(LR_COL, 2 * LOWRANK)], head, 2 * LOWRANK),
             (g_ref.at[pl.ds(LR_REF, 64)], head + 2 * LOWRANK, 64)]
    return [(j != 4, [(g_ref.at[pl.ds(start, W_IN_WINDOW)], 0, W_IN_WINDOW)]), (j == 4, split)]


def _reduce_scatter_comm(grads, rows=None, pieces=None):
    if pieces is None:
        _, _, r, c = grads.shape
        pieces = lambda g_ref, chip, core: [(None, [(g_ref.at[chip, core], 0, r)])]
    else:
        r, c = rows, grads.shape[1]
    order = (3, 1, 2, 0)

    def transfer(k, kind, in_refs, scr, act):
        (g_ref,), (sib, own, _, sems) = in_refs, scr
        x, y, core = _mesh_pos()
        chip = (2 * x + y) ^ k
        for cond, parts in pieces(g_ref, chip, 1 - core if kind == "send" else core):
            def run(parts=parts):
                for i, (src, row0, n) in enumerate(parts):
                    if kind == "local":
                        act(pltpu.make_async_copy(src, own.at[k, pl.ds(row0, n)], sems.at[2, 3 * k + i]))
                    else:
                        act(pltpu.make_async_remote_copy(
                            src_ref=src, dst_ref=sib.at[k, pl.ds(row0, n)], send_sem=sems.at[0, 3 * k + i],
                            recv_sem=sems.at[1, 3 * k + i], device_id=(x, y, 1 - core), device_id_type=MESH_ID))

            run() if cond is None else pl.when(cond)(run)

    def chip_copies(out_refs, scr):
        (_, recv), (_, _, part, sems) = out_refs, scr
        x, y, core = _mesh_pos()
        return [pltpu.make_async_remote_copy(
            src_ref=part.at[j], dst_ref=recv.at[j], send_sem=sems.at[3, j], recv_sem=sems.at[4, j],
            device_id=(*chip, core), device_id_type=MESH_ID) for j, chip in enumerate(_other_chips(x, y))]

    def before(step, nsteps, in_refs, out_refs, scr):
        @pl.when(step == 0)
        def _():
            for k in order:
                transfer(k, "send", in_refs, scr, lambda cp: cp.start())
                transfer(k, "local", in_refs, scr, lambda cp: cp.start())

    def after(step, nsteps, in_refs, out_refs, scr):
        sib, own, part, _ = scr

        @pl.when(step == (nsteps - 1) // 2)
        def _():
            to_chips = chip_copies(out_refs, scr)
            for k in order:
                transfer(k, "recv", in_refs, scr, lambda cp: cp.wait_recv())
                transfer(k, "local", in_refs, scr, lambda cp: cp.wait())
                if k:
                    part[k - 1] = (own[k] + sib[k]).astype(BF16)
                    to_chips[k - 1].start()
                else:
                    out_refs[0][...] = own[0] + sib[0]
            for k in order:
                transfer(k, "send", in_refs, scr, lambda cp: cp.wait_send())

        @pl.when(step == nsteps - 1)
        def _():
            for cp in chip_copies(out_refs, scr):
                cp.wait()

    return _Comm(inputs=[grads], in_specs=[_HBM_WHOLE],
                 out_shape=[jax.ShapeDtypeStruct((r, c), F32), jax.ShapeDtypeStruct((3, r, c), BF16)],
                 out_specs=[_VMEM_WHOLE, _HBM_WHOLE],
                 scratch_shapes=[pltpu.VMEM((4, r, c), F32), pltpu.VMEM((4, r, c), F32), pltpu.VMEM((3, r, c), BF16),
                                 pltpu.SemaphoreType.DMA((5, 12))],
                 before=before, after=after)


def _exchange_comm(arrays, out_shape, make_copies):
    na = len(arrays)

    def copies(in_refs, out_refs, scr):
        return make_copies(in_refs, out_refs, *scr)

    def before(step, nsteps, in_refs, out_refs, scr):
        @pl.when(step == 0)
        def _():
            for cp in copies(in_refs, out_refs, scr):
                cp.start()

    def after(step, nsteps, in_refs, out_refs, scr):
        @pl.when(step == nsteps - 1)
        def _():
            for cp in copies(in_refs, out_refs, scr):
                cp.wait()

    return _Comm(inputs=list(arrays), in_specs=[_HBM_WHOLE] * na, out_shape=list(out_shape),
                 out_specs=[_HBM_WHOLE] * na,
                 scratch_shapes=[pltpu.SemaphoreType.DMA((na, 3)), pltpu.SemaphoreType.DMA((na, 3))],
                 before=before, after=after)


def _sibling_exchange_comm(grads):
    def make_copies(in_refs, out_refs, send_sems, recv_sems):
        x, y, c = _mesh_pos()
        return [pltpu.make_async_remote_copy(
            src_ref=in_refs[a].at[:, pl.ds(1 - c, 1)], dst_ref=out_refs[a], send_sem=send_sems.at[a, 0],
            recv_sem=recv_sems.at[a, 0], device_id=(x, y, 1 - c), device_id_type=MESH_ID)
            for a in range(len(grads))]

    return _exchange_comm(grads, [jax.ShapeDtypeStruct((4, 1) + g.shape[2:], F32) for g in grads], make_copies)


def _chip_sum(my_pos, mine, from_sibling, tr, name):
    _, _, r, c = mine.shape

    def body(pos_ref, a_ref, b_ref, own_ref, out_ref):
        s = a_ref[0, 0] + b_ref[0, 0]

        @pl.when(pl.program_id(1) == 0)
        def _():
            own_ref[...] = s

        @pl.when(pl.program_id(1) > 0)
        def _():
            out_ref[0] = s.astype(BF16)

    grid_spec = pltpu.PrefetchScalarGridSpec(
        num_scalar_prefetch=1, grid=(r // tr, 4),
        in_specs=[pl.BlockSpec((1, 1, tr, c), lambda i, k, pos: (pos[0] ^ k, pos[1], i, 0)),
                  pl.BlockSpec((1, 1, tr, c), lambda i, k, pos: (pos[0] ^ k, 0, i, 0))],
        out_specs=(pl.BlockSpec((tr, c), lambda i, k, pos: (i, 0)),
                   pl.BlockSpec((1, tr, c), lambda i, k, pos: (jnp.maximum(k - 1, 0), i, 0))))
    return pl.pallas_call(
        body, name=name, grid_spec=grid_spec,
        out_shape=(jax.ShapeDtypeStruct((r, c), F32), jax.ShapeDtypeStruct((3, r, c), BF16)),
        compiler_params=_params(2),
    )(my_pos, mine, from_sibling)


def _chips_exchange_comm(partials):
    def make_copies(in_refs, out_refs, send_sems, recv_sems):
        x, y, c = _mesh_pos()
        return [pltpu.make_async_remote_copy(
            src_ref=in_refs[a].at[j], dst_ref=out_refs[a].at[j], send_sem=send_sems.at[a, j],
            recv_sem=recv_sems.at[a, j], device_id=(*chip, c), device_id_type=MESH_ID)
            for a in range(len(partials)) for j, chip in enumerate(_other_chips(x, y))]

    return _exchange_comm(partials, [jax.ShapeDtypeStruct(g.shape, BF16) for g in partials], make_copies)


def _comm_only(comms, name):
    return _fused_call(lambda: None, comms, name=name, grid=(1,), inputs=(), in_specs=[], out_specs=(),
                       out_shape=())[1]


def _all_reduce_small_comm(parts):
    na = len(parts)

    def copies(in_refs, scr):
        gathered, (send_sems, recv_sems) = scr[:na], scr[na:]
        x, y, c = _mesh_pos()
        my_id = 4 * x + 2 * y + c
        return my_id, [pltpu.make_async_remote_copy(
            src_ref=in_refs[a], dst_ref=gathered[a].at[my_id], send_sem=send_sems.at[a, k - 1],
            recv_sem=recv_sems.at[a, k - 1], device_id=(x ^ (k >> 2), y ^ ((k >> 1) & 1), c ^ (k & 1)),
            device_id_type=MESH_ID) for a in range(na) for k in range(1, N_DEV)]

    def before(step, nsteps, in_refs, out_refs, scr):
        @pl.when(step == 0)
        def _():
            for cp in copies(in_refs, scr)[1]:
                cp.start()

    def after(step, nsteps, in_refs, out_refs, scr):
        @pl.when(step == nsteps - 1)
        def _():
            my_id, cps = copies(in_refs, scr)
            for a in range(na):
                scr[a][my_id] = in_refs[a][...]
            for cp in cps:
                cp.wait()
            for a in range(na):
                acc = scr[a][0]
                for d in range(1, N_DEV):
                    acc = acc + scr[a][d]
                out_refs[a][...] = acc

    return _Comm(inputs=list(parts), in_specs=[_VMEM_WHOLE] * na,
                 out_shape=[jax.ShapeDtypeStruct(p.shape, F32) for p in parts], out_specs=[_VMEM_WHOLE] * na,
                 scratch_shapes=[pltpu.VMEM((N_DEV,) + p.shape, F32) for p in parts] + [
                     pltpu.SemaphoreType.DMA((na, N_DEV - 1)), pltpu.SemaphoreType.DMA((na, N_DEV - 1))],
                 before=before, after=after)


def _unshard_cols(g):
    return jnp.transpose(g, (1, 0, 2)).reshape(g.shape[1], N_DEV * g.shape[2])


def _row_blocks(w):
    return w.reshape(4, 2, w.shape[0] // N_DEV, w.shape[1])


def _stack_rows(parts):
    a = jnp.concatenate(parts, axis=0)
    return jnp.pad(a, ((0, (-a.shape[0]) % 8), (0, 0)))


def _w_in_grad_blocks(dw):
    return _row_blocks(jnp.concatenate([dw[:LR_REF], dw[LR_COL:LR_COL + 2 * LOWRANK], dw[LR_REF:LR_COL]], axis=0))


def _padded_decay_weights(wd_f, wd_b):
    zeros = lambda n: jnp.zeros((n, KEY_W), F32)
    return (jnp.concatenate([wd_f, zeros(LANE - LOWRANK)], axis=0),
            jnp.concatenate([zeros(LOWRANK), wd_b, zeros(LANE - 2 * LOWRANK)], axis=0))


def kernel(x, norm1_g, w_in,w_decay_f, b_decay_f, w_decay_b, b_decay_b, gla_norm_g, gmlp_ln_g, gmlp_ln_b, w_spatial, b_spatial, w_out, norm2_g, w_gate, w_up, w_down, final_norm_g, loss_target, m_norm1_g, m_w_in, m_w_decay_f, m_b_decay_f, m_w_decay_b, m_b_decay_b, m_gla_norm_g, m_gmlp_ln_g, m_gmlp_ln_b, m_w_spatial, m_b_spatial, m_w_out, m_norm2_g, m_w_gate, m_w_up, m_w_down, m_final_norm_g, v_norm1_g, v_w_in, v_w_decay_f, v_b_decay_f, v_w_decay_b, v_b_decay_b, v_gla_norm_g, v_gmlp_ln_g, v_gmlp_ln_b, v_w_spatial, v_b_spatial, v_w_out, v_norm2_g, v_w_gate, v_w_up, v_w_down, v_final_norm_g):
    t = x.shape[1]
    xt = x[0]
    target = loss_target[0]
    pos_x, pos_y, pos_c = _mesh_pos()
    my_pos = jnp.stack([2 * pos_x + pos_y, pos_c]).astype(jnp.int32)
    my_id = 4 * pos_x + 2 * pos_y + pos_c

    tile = lambda n: min(n, t)
    ln_g, ln_b, w_sp = gmlp_ln_g, gmlp_ln_b, w_spatial[0]
    b_sp_col = b_spatial[0][:, :, None]
    shard = {"w_in": w_in[0].T, "w_out": w_out[0], "w_gate": w_gate[0].T, "w_up": w_up[0].T, "w_down": w_down[0]}
    shard_m = {"w_in": m_w_in[0].T, "w_out": m_w_out[0], "w_gate": m_w_gate[0].T, "w_up": m_w_up[0].T,
               "w_down": m_w_down[0]}
    shard_v = {"w_in": v_w_in[0].T, "w_out": v_w_out[0], "w_gate": v_w_gate[0].T, "w_up": v_w_up[0].T,
               "w_down": v_w_down[0]}
    transposed = ("w_in", "w_gate", "w_up")

    decay_shard = jnp.stack([w_decay_f[0], w_decay_b[0]])
    (hb,), ((g_in, g_decay),) = _norm1(xt, norm1_g, tile(TOKEN_TILE["norm1"]),
                                       [_gather_comm([shard["w_in"], decay_shard], [True, False])])
    w_in_t = g_in.reshape(PROJ_W, D_MODEL)
    wd_pad_f, wd_pad_b = _padded_decay_weights(_unshard_cols(g_decay[:, 0]), _unshard_cols(g_decay[:, 1]))
    (p,), ((g_gate, g_out),) = _in_proj(
        hb, w_in_t, tile(TOKEN_TILE["in_proj"]), [_gather_comm([shard["w_gate"], shard["w_out"]], [True, True])])
    (o_f, st_f, o_b, st_b), ((g_up,),) = _gla_fwd(
        p, wd_pad_f, b_decay_f, wd_pad_b, b_decay_b, tile(TOKEN_TILE["gla"]), [_gather_comm([shard["w_up"]], [True])])
    w_out_full = g_out.reshape(D_MODEL, D_MODEL)
    (x1, ycat), ((g_down,),) = _mix_fwd(xt, o_f, o_b, p, gla_norm_g, ln_g, ln_b, w_sp, b_sp_col, w_out_full,
                                        tile(TOKEN_TILE["mix_fwd"]), [_gather_comm([shard["w_down"]], [True])])

    dx1, h2b, dgate, dup, act, dx2, loss_acc, d_gf, d_g2 = _ffn(
        x1, target, norm2_g, final_norm_g[None, :], g_gate.reshape(D_FF, D_MODEL), g_up.reshape(D_FF, D_MODEL),
        g_down.reshape(D_FF, D_MODEL), tile(TOKEN_TILE["ffn"]))
    dw_gate, _ = _matmul_tn(dgate, h2b, D_FF // 2, tile(TOKEN_TILE["dw"]), "grad_w_gate")
    dw_up, _ = _matmul_tn(dup, h2b, D_FF // 2, tile(TOKEN_TILE["dw"]), "grad_w_up")
    dw_down, _ = _matmul_tn(act, dx2, D_FF // 2, tile(TOKEN_TILE["dw"]), "grad_w_down")

    reduced = {}
    staged = {"w_gate": _row_blocks(dw_gate), "w_up": _row_blocks(dw_up)}
    (d_o, dpg, dpu, dpv, dw_out, d_gg, d_lg, d_lb, dw_sp, db_sp), (reduced["w_down"], staged_sib) = _mix_bwd(
        dx1, ycat, o_f, o_b, p, gla_norm_g, ln_g, ln_b, w_sp, b_sp_col, w_out_full,
        tile(TOKEN_TILE["mix_bwd"]),
        [_reduce_scatter_comm(_row_blocks(dw_down)), _sibling_exchange_comm(list(staged.values()))])
    staged_sums = [_chip_sum(my_pos, g, s, g.shape[2], "chip_sum_" + n) for (n, g), s in zip(staged.items(), staged_sib)]
    (dq_f, dk_f, dv_f, dlr_f, dwd_f, dbd_f, dq_b, dk_b, dv_b, dlr_b, dwd_b, dbd_b), (reduced["w_out"], staged_recv) = (
        _gla_bwd(p, wd_pad_f, b_decay_f, wd_pad_b, b_decay_b, st_f, st_b, d_o, tile(TOKEN_TILE["gla"]),
                 [_reduce_scatter_comm(_row_blocks(dw_out)), _chips_exchange_comm([s[1] for s in staged_sums])]))
    for n, s, rc in zip(staged, staged_sums, staged_recv):
        reduced[n] = (s[0], rc)
    (grad_x, dp, d_g1), _ = _in_proj_bwd(
        xt, norm1_g, dx1, dq_f, dq_b, dk_f, dk_b, dv_f, dv_b, dpg, dpu, dpv, dlr_f, dlr_b, w_in_t,
        tile(TOKEN_TILE["in_proj_bwd"]))

    stacks = [_stack_rows([d_g1, d_g2, d_gf]), _stack_rows([d_gg, d_lg, d_lb]),
              _stack_rows([dbd_f, dbd_b, jnp.zeros((DECAY_W_ROW - 2, KEY_W), F32), dwd_f[:LOWRANK],
                           dwd_b[LOWRANK:2 * LOWRANK]]),
              _stack_rows([dw_sp.reshape(GMLP_W, GMLP_CHUNK), db_sp[:, :, 0], loss_acc[:1]])]
    dw_main, (small_sums,) = _matmul_tn(dp, hb, PROJ_PAD // 3, tile(TOKEN_TILE["dw"]), "grad_w_in",
                                        [_all_reduce_small_comm(stacks)])
    ((in_own, in_recv),) = _comm_only(
        [_reduce_scatter_comm(dw_main, W_IN_WINDOW, _w_in_block_pieces)], "grad_w_in_reduce_scatter")

    big_out = {"w_in": [r.T for r in _adamw_window(in_own, in_recv, shard["w_in"], shard_m["w_in"], shard_v["w_in"],
                                                   "adamw_w_in")]}
    for n, (own_sum, recv) in reduced.items():
        rows = shard[n].shape[0]
        half = rows // 2 if rows % 32 == 0 else rows
        res = _adamw_shard(own_sum, recv, shard[n], shard_m[n], shard_v[n], half, "adamw_" + n)
        big_out[n] = [r.T if n in transposed else r for r in res]

    s1024, s512, s256, s128 = small_sums
    loss = s128[GMLP_W + GMLP_GROUPS, 0]
    col0 = my_id * (KEY_W // N_DEV)
    decay_cols = lambda row0: lax.dynamic_slice(s256, (row0, col0), (LOWRANK, KEY_W // N_DEV))
    flat = lambda a: a.reshape(-1, a.shape[-1])
    small = {
        "norm1_g": ((s1024, 0, 1), norm1_g, m_norm1_g, v_norm1_g),
        "w_decay_f": ((decay_cols(DECAY_W_ROW), 0, LOWRANK), w_decay_f, m_w_decay_f, v_w_decay_f),
        "b_decay_f": ((s256, 0, 1), b_decay_f, m_b_decay_f, v_b_decay_f),
        "w_decay_b": ((decay_cols(DECAY_W_ROW + LOWRANK), 0, LOWRANK), w_decay_b, m_w_decay_b, v_w_decay_b),
        "b_decay_b": ((s256, 1, 1), b_decay_b, m_b_decay_b, v_b_decay_b),
        "gla_norm_g": ((s512, 0, 1), gla_norm_g, m_gla_norm_g, v_gla_norm_g),
        "gmlp_ln_g": ((s512, 1, 1), gmlp_ln_g, m_gmlp_ln_g, v_gmlp_ln_g),
        "gmlp_ln_b": ((s512, 2, 1), gmlp_ln_b, m_gmlp_ln_b, v_gmlp_ln_b),
        "w_spatial": ((s128, 0, GMLP_W), w_spatial, m_w_spatial, v_w_spatial),
        "b_spatial": ((s128, GMLP_W, GMLP_GROUPS), b_spatial, m_b_spatial, v_b_spatial),
        "norm2_g": ((s1024, 1, 1), norm2_g, m_norm2_g, v_norm2_g),
        "final_norm_g": ((s1024, 2, 1), final_norm_g, m_final_norm_g, v_final_norm_g),
    }
    small_res = _adamw_small([(g, flat(w), flat(m), flat(v)) for g, w, m, v in small.values()])
    small_out = {n: [r.reshape(small[n][1].shape) for r in res] for n, res in zip(small, small_res)}

    order = ["norm1_g", "w_in", "w_decay_f", "b_decay_f", "w_decay_b", "b_decay_b", "gla_norm_g", "gmlp_ln_g",
             "gmlp_ln_b", "w_spatial", "b_spatial", "w_out", "norm2_g", "w_gate", "w_up", "w_down", "final_norm_g"]
    outs = []
    for kind in range(4):
        for n in order:
            outs.append(big_out[n][kind][None] if n in big_out else small_out[n][kind])
    return (loss, grad_x[None], *outs)
```

```python
import functools
import math

import jax
import jax.numpy as jnp
from jax import lax
from jax.experimental import pallas as pl
from jax.experimental.pallas import tpu as pltpu

F32 = jnp.float32
BF16 = jnp.bfloat16

D_MODEL = 1024
GLA_HEADS = 4
GLA_DK = 64
GLA_DV = 128
KEY_W = GLA_HEADS * GLA_DK
VAL_W = GLA_HEADS * GLA_DV
LOWRANK = 16
GLA_TAU = 16.0
GLA_CHUNK = 64
GMLP_W = 512
GMLP_GROUPS = 4
GMLP_CHUNK = 128
D_FF = 2816
EPS = 1e-6
Q_SCALE = GLA_DK ** -0.5
PROJ_PAD = 2688
LR_COL = 2560
LANE = 128
N_DEV = 8

ADAM_LR = 0.001
ADAM_B1 = 0.9
ADAM_B2 = 0.999
ADAM_EPS = 1e-08
ADAM_WD = 0.01
ADAM_STEP = 10

VMEM_LIMIT = 56 * 1024 * 1024
TOKEN_TILE = {"norm1": 512, "in_proj": 512, "gla": 1024, "mix_fwd": 1024, "ffn": 256, "mix_bwd": 512,
              "in_proj_bwd": 512, "dw": 2048}
DECAY_W_ROW = 8
MESH_ID = pl.DeviceIdType.MESH
INV_SQRT2 = 0.7071067811865476
INV_SQRT_2PI = 0.3989422804014327


def _params(n_axes=1):
    return pltpu.CompilerParams(dimension_semantics=("arbitrary",) * n_axes, vmem_limit_bytes=VMEM_LIMIT)


def _mm(a, b):
    return jnp.dot(a.astype(BF16), b.astype(BF16), preferred_element_type=F32)


def _mm_nt(a, b):
    return lax.dot_general(a.astype(BF16), b.astype(BF16), (((1,), (1,)), ((), ())), preferred_element_type=F32)


def _mm_tn(a, b):
    return lax.dot_general(a.astype(BF16), b.astype(BF16), (((0,), (0,)), ((), ())), preferred_element_type=F32)


def _const_spec(shape):
    nd = len(shape)
    return pl.BlockSpec(shape, lambda *_: (0,) * nd, pipeline_mode=pl.Buffered(1))


def _acc_spec(shape):
    nd = len(shape)
    return pl.BlockSpec(shape, lambda *_: (0,) * nd)


class _Comm:
    def __init__(self, inputs, in_specs, out_shape, out_specs, scratch_shapes, before, after):
        self.inputs, self.in_specs, self.out_shape, self.out_specs = inputs, in_specs, out_shape, out_specs
        self.scratch_shapes, self.before, self.after = scratch_shapes, before, after


def _fused_call(body, comms, *, name, grid, inputs, in_specs, out_specs, out_shape, scratch_shapes=()):
    n_in, n_out, n_scr = len(in_specs), len(out_specs), len(scratch_shapes)
    nsteps = math.prod(grid)
    sizes = [(len(c.inputs), len(c.out_shape), len(c.scratch_shapes)) for c in comms]

    def full_body(*refs):
        step = pl.program_id(0)
        for axis in range(1, len(grid)):
            step = step * grid[axis] + pl.program_id(axis)
        ins, rest = refs[:n_in], refs[n_in:]
        c_ins = []
        for ci, _, _ in sizes:
            c_ins.append(rest[:ci])
            rest = rest[ci:]
        outs, rest = rest[:n_out], rest[n_out:]
        c_outs = []
        for _, co, _ in sizes:
            c_outs.append(rest[:co])
            rest = rest[co:]
        scr, rest = rest[:n_scr], rest[n_scr:]
        c_scr = []
        for _, _, cs in sizes:
            c_scr.append(rest[:cs])
            rest = rest[cs:]
        for c, a, b, s in zip(comms, c_ins, c_outs, c_scr):
            c.before(step, nsteps, a, b, s)
        body(*ins, *outs, *scr)
        for c, a, b, s in zip(comms, c_ins, c_outs, c_scr):
            c.after(step, nsteps, a, b, s)

    results = pl.pallas_call(
        full_body, name=name, grid=grid,
        in_specs=list(in_specs) + [s for c in comms for s in c.in_specs],
        out_specs=tuple(out_specs) + tuple(s for c in comms for s in c.out_specs),
        out_shape=tuple(out_shape) + tuple(s for c in comms for s in c.out_shape),
        scratch_shapes=list(scratch_shapes) + [s for c in comms for s in c.scratch_shapes],
        compiler_params=_params(len(grid)),
    )(*inputs, *[a for c in comms for a in c.inputs])
    own, rest = results[:n_out], results[n_out:]
    comm_results = []
    for _, co, _ in sizes:
        comm_results.append(rest[:co])
        rest = rest[co:]
    return own, comm_results


def _gelu(x):
    return 0.5 * x * (1.0 + lax.erf(x * INV_SQRT2))


def _gelu_and_grad(x):
    cdf = 0.5 * (1.0 + lax.erf(x * INV_SQRT2))
    return x * cdf, cdf + x * jnp.exp(-0.5 * x * x) * INV_SQRT_2PI


def _sigmoid(x):
    return 0.5 + 0.5 * jnp.tanh(0.5 * x)


def _silu_and_grad(x):
    s = _sigmoid(x)
    return x * s, s * (1.0 + x * (1.0 - s))


def _norm1(x, g1, tm, comms=()):
    t = x.shape[0]

    def body(x_ref, g_ref, h_ref):
        xv = x_ref[...]
        r = lax.rsqrt(jnp.mean(xv * xv, axis=-1, keepdims=True) + EPS)
        h_ref[...] = (xv * r * g_ref[...]).astype(BF16)

    row = pl.BlockSpec((tm, D_MODEL), lambda i: (i, 0))
    return _fused_call(body, comms, name="norm1", grid=(t // tm,), inputs=(x, g1),
                       in_specs=[row, _const_spec((1, D_MODEL))], out_specs=(row,),
                       out_shape=(jax.ShapeDtypeStruct((t, D_MODEL), BF16),))


PROJ_W = 2592
LR_REF = 1536
PROJ_ROWS = ((0, LR_REF, 0), (LR_REF + 2 * LOWRANK, PROJ_W, LR_REF), (LR_REF, LR_REF + LANE, LR_COL))


def _in_proj(h, w_in_t, tm, comms=()):
    t = h.shape[0]

    def body(h_ref, w_ref, p_ref):
        hv = h_ref[...]
        for r0, r1, c0 in PROJ_ROWS:
            p_ref[:, c0:c0 + r1 - r0] = _mm_nt(hv, w_ref[r0:r1, :]).astype(BF16)

    return _fused_call(
        body, comms, name="in_proj", grid=(t // tm,), inputs=(h, w_in_t),
        in_specs=[pl.BlockSpec((tm, D_MODEL), lambda i: (i, 0)), _const_spec((PROJ_W, D_MODEL))],
        out_specs=(pl.BlockSpec((tm, PROJ_PAD), lambda i: (i, 0)),),
        out_shape=(jax.ShapeDtypeStruct((t, PROJ_PAD), BF16),))


def _tri(upper):
    r = lax.broadcasted_iota(jnp.int32, (GLA_CHUNK, GLA_CHUNK), 0)
    c = lax.broadcasted_iota(jnp.int32, (GLA_CHUNK, GLA_CHUNK), 1)
    return jnp.where((c >= r) if upper else (c <= r), 1.0, 0.0).astype(BF16)


def _chunk_cumsum(tri, a, add=None):
    hi = a.astype(BF16)
    lo = (a - hi.astype(F32)).astype(BF16)
    dot = functools.partial(jnp.dot, preferred_element_type=F32)
    sums = [dot(tri, hi[_chunk_rows(c)]) + dot(tri, lo[_chunk_rows(c)]) for c in range(a.shape[0] // GLA_CHUNK)]
    return jnp.concatenate(sums if add is None else [s + r for s, r in zip(sums, add)], axis=0)


def _chunk_rows(c):
    return slice(c * GLA_CHUNK, (c + 1) * GLA_CHUNK)


def _gla_masks(rev):
    dk_bits, dv_bits = GLA_DK.bit_length() - 1, GLA_DV.bit_length() - 1
    key_head = lax.broadcasted_iota(jnp.int32, (GLA_CHUNK, KEY_W), 1) >> dk_bits
    val_head = lax.broadcasted_iota(jnp.int32, (GLA_CHUNK, VAL_W), 1) >> dv_bits
    t = lax.broadcasted_iota(jnp.int32, (GLA_HEADS * GLA_CHUNK, GLA_CHUNK), 0) & (GLA_CHUNK - 1)
    s = lax.broadcasted_iota(jnp.int32, (GLA_HEADS * GLA_CHUNK, GLA_CHUNK), 1)
    return key_head, val_head, (s >= t) if rev else (s <= t)


def _stack_heads(a, head_of_lane):
    a = a.astype(BF16)
    return jnp.concatenate([jnp.where(head_of_lane == h, a, jnp.zeros_like(a)) for h in range(GLA_HEADS)], axis=0)


def _rows_by_head(a):
    return jnp.concatenate([a[:, h * GLA_DV:(h + 1) * GLA_DV] for h in range(GLA_HEADS)], axis=0)


def _lanes_by_head(r):
    return jnp.concatenate([r[h * GLA_CHUNK:(h + 1) * GLA_CHUNK] for h in range(GLA_HEADS)], axis=1)


def _head_diagonal(r, head_of_lane):
    rows = r.shape[0] // GLA_HEADS
    out = jnp.where(head_of_lane == 0, r[:rows], 0.0)
    for h in range(1, GLA_HEADS):
        out = out + jnp.where(head_of_lane == h, r[h * rows:(h + 1) * rows], 0.0)
    return out


def _tile_terms(la, q, k, tri, rev):
    nc = la.shape[0] // GLA_CHUNK
    q, k = q.astype(F32), k.astype(F32)
    b = _chunk_cumsum(tri, la)
    ebl = [jnp.exp(b[c * GLA_CHUNK:c * GLA_CHUNK + 1] if rev else b[(c + 1) * GLA_CHUNK - 1:(c + 1) * GLA_CHUNK])
           for c in range(nc)]
    eb = jnp.exp(b)
    enb = jnp.exp(-b)
    kd = k * enb
    ke = jnp.concatenate([kd[_chunk_rows(c)] * ebl[c] for c in range(nc)], axis=0)
    return ebl, eb, enb, q * Q_SCALE * eb, kd, ke


def _log_decay(lr_ref, wd_ref, bd_ref):
    z = _mm(lr_ref[...], wd_ref[...]) + bd_ref[...]
    return z, jax.nn.log_sigmoid(z) * (1.0 / GLA_TAU)


def _p_specs(tg, tile):
    return [pl.BlockSpec((tg, KEY_W), lambda i: (tile(i), 0)),
            pl.BlockSpec((tg, KEY_W), lambda i: (tile(i), 1)),
            pl.BlockSpec((tg, VAL_W), lambda i: (tile(i), 1)),
            pl.BlockSpec((tg, LANE), lambda i: (tile(i), LR_COL // LANE))]


def _gla_fwd_dir(rev, nc, q_ref, k_ref, v_ref, lr_ref, wd_ref, bd_ref, o_ref, st_ref, state):
    key_head, _, causal = _gla_masks(rev)
    order = range(nc - 1, -1, -1) if rev else range(nc)

    def intra():
        _, la = _log_decay(lr_ref, wd_ref, bd_ref)
        ebl, _, _, qd, kd, ke = _tile_terms(la, q_ref[...], k_ref[...], _tri(rev), rev)
        kd = kd.astype(BF16)
        v = {c: v_ref[_chunk_rows(c), :].astype(BF16) for c in order}
        qd_stack = {c: _stack_heads(qd[_chunk_rows(c)], key_head) for c in order}
        ke_stack = {c: _stack_heads(ke[_chunk_rows(c)], key_head) for c in order}
        a_all = {c: _mm_nt(qd_stack[c], kd[_chunk_rows(c)]) for c in order}
        a_all = {c: jnp.where(causal, a_all[c], 0.0).astype(BF16) for c in order}
        head_rows = lambda a, h: a[h * GLA_CHUNK:(h + 1) * GLA_CHUNK]
        head_vals = lambda a, h: a[:, h * GLA_DV:(h + 1) * GLA_DV]
        r = {c: [_mm(head_rows(a_all[c], h), head_vals(v[c], h)) for h in range(GLA_HEADS)] for c in order}
        upd = {c: _mm_tn(_rows_by_head(v[c]), ke_stack[c]) for c in order}
        return {c: (ebl[c], qd_stack[c], r[c], upd[c]) for c in order}

    def scan(terms):
        st = state[...]
        states = {}
        for c in order:
            states[c] = st
            st_ref[c] = st.astype(BF16)
            st = st * terms[c][0] + terms[c][3]
        state[...] = st
        return states

    def inter(terms, states):
        r_inter = {c: _mm_nt(terms[c][1], states[c]) for c in order}
        for c in order:
            o_ref[_chunk_rows(c), :] = jnp.concatenate(
                [terms[c][2][h] + r_inter[c][h * GLA_CHUNK:(h + 1) * GLA_CHUNK] for h in range(GLA_HEADS)], axis=1)

    return intra, scan, inter


def _gla_fwd(p, wd_pad_f, bd_f, wd_pad_b, bd_b, tg, comms=()):
    t = p.shape[0]
    nt = t // tg
    nc = tg // GLA_CHUNK
    up, down = (lambda i: i), (lambda i: nt - 1 - i)

    def body(qf, kf, vf, lrf, qb, kb, vb, lrb, wdf, bdf, wdb, bdb, of, stf, ob, stb, state_f, state_b):
        @pl.when(pl.program_id(0) == 0)
        def _():
            state_f[...] = jnp.zeros_like(state_f)
            state_b[...] = jnp.zeros_like(state_b)

        dirs = [_gla_fwd_dir(False, nc, qf, kf, vf, lrf, wdf, bdf, of, stf, state_f),
                _gla_fwd_dir(True, nc, qb, kb, vb, lrb, wdb, bdb, ob, stb, state_b)]
        terms = [intra() for intra, _, _ in dirs]
        states = [scan(t) for (_, scan, _), t in zip(dirs, terms)]
        for (_, _, inter), t, s in zip(dirs, terms, states):
            inter(t, s)

    wd_spec, bd_spec = _const_spec((LANE, KEY_W)), _const_spec((1, KEY_W))
    outs = lambda tile: (pl.BlockSpec((tg, VAL_W), lambda i: (tile(i), 0)),
                         pl.BlockSpec((nc, GLA_DV, KEY_W), lambda i: (tile(i), 0, 0)))
    out_shape = (jax.ShapeDtypeStruct((t, VAL_W), F32), jax.ShapeDtypeStruct((t // GLA_CHUNK, GLA_DV, KEY_W), BF16))
    return _fused_call(
        body, comms, name="gla_fwd", grid=(nt,), inputs=(p,) * 8 + (wd_pad_f, bd_f, wd_pad_b, bd_b),
        in_specs=_p_specs(tg, up) + _p_specs(tg, down) + [wd_spec, bd_spec, wd_spec, bd_spec],
        out_specs=outs(up) + outs(down), out_shape=out_shape * 2,
        scratch_shapes=[pltpu.VMEM((GLA_DV, KEY_W), F32)] * 2)


def _gla_bwd_dir(rev, nc, q_ref, k_ref, v_ref, lr_ref, wd_ref, bd_ref, st_ref, do_ref,
                 dq_ref, dk_ref, dv_ref, dlr_ref, dwd_ref, dbd_ref, dstate):
    key_head, val_head, causal = _gla_masks(rev)
    order = range(nc) if rev else range(nc - 1, -1, -1)

    def intra():
        z, la = _log_decay(lr_ref, wd_ref, bd_ref)
        tile = _tile_terms(la, q_ref[...], k_ref[...], _tri(rev), rev)
        qd, kd = tile[3], tile[4].astype(BF16)
        v = {c: v_ref[_chunk_rows(c), :].astype(BF16) for c in order}
        d_o = {c: do_ref[_chunk_rows(c), :] for c in order}
        kd_c = {c: kd[_chunk_rows(c)] for c in order}
        qd_stack = {c: _stack_heads(qd[_chunk_rows(c)], key_head) for c in order}
        do_stack = {c: _stack_heads(d_o[c], val_head) for c in order}
        do_rows = {c: _rows_by_head(d_o[c]) for c in order}
        a_all = {c: _mm_nt(qd_stack[c], kd_c[c]) for c in order}
        head_vals = lambda a, h: a[:, h * GLA_DV:(h + 1) * GLA_DV]
        da_all = {c: jnp.concatenate([_mm_nt(head_vals(d_o[c], h), head_vals(v[c], h)) for h in range(GLA_HEADS)],
                                     axis=0) for c in order}
        a_all = {c: jnp.where(causal, a_all[c], 0.0).astype(BF16) for c in order}
        da_all = {c: jnp.where(causal, da_all[c], 0.0).astype(BF16) for c in order}
        dv = {c: _mm_tn(a_all[c], do_stack[c]) for c in order}
        dqd = {c: _mm(jnp.concatenate([do_rows[c], da_all[c]], axis=1),
                      jnp.concatenate([st_ref[c], kd_c[c]], axis=0)) for c in order}
        dkd = {c: _mm_tn(da_all[c], qd_stack[c]) for c in order}
        upd = {c: _mm_tn(do_rows[c], qd_stack[c]) for c in order}
        dqd = {c: _head_diagonal(dqd[c], key_head) for c in order}
        return z, tile, {c: dict(dv=dv[c], dqd=dqd[c], dkd=dkd[c], upd=upd[c]) for c in order}

    def scan(tile, per):
        dst = dstate[...]
        dsts = {}
        for c in order:
            dsts[c] = dst
            dst = dst * tile[0][c] + per[c]["upd"]
        dstate[...] = dst
        return dsts

    def inter(z, tile, per, dsts):
        ebl, eb, enb, qd, kd, ke = tile
        ke_stack = {c: _stack_heads(ke[_chunk_rows(c)], key_head) for c in order}
        v_rows = {c: _rows_by_head(v_ref[_chunk_rows(c), :].astype(BF16)) for c in order}
        dst_b = {c: dsts[c].astype(BF16) for c in order}
        dv_state = {c: _mm_nt(ke_stack[c], dst_b[c]) for c in order}
        dke_c = {c: _mm(v_rows[c], dst_b[c]) for c in order}
        dke_c = {c: _head_diagonal(dke_c[c], key_head) for c in order}
        dbl_c = {}
        for c in order:
            rows = _chunk_rows(c)
            dv_ref[rows, :] = (per[c]["dv"] + _lanes_by_head(dv_state[c])).astype(BF16)
            dbl_c[c] = (jnp.sum(dsts[c] * st_ref[c].astype(F32), axis=0, keepdims=True) * ebl[c]
                        + jnp.sum(dke_c[c] * ke[rows], axis=0, keepdims=True))
        tile_of = lambda parts: jnp.concatenate([parts[c] for c in range(nc)], axis=0)
        dqd, dkd = tile_of({c: per[c]["dqd"] for c in order}), tile_of({c: per[c]["dkd"] for c in order})
        dke = tile_of(dke_c)
        dke_end = tile_of({c: dke_c[c] * ebl[c] for c in order})
        dq_ref[...] = (dqd * eb * Q_SCALE).astype(BF16)
        dk_ref[...] = ((dkd + dke_end) * enb).astype(BF16)
        db = dqd * qd - dkd * kd - dke * ke
        dla = _chunk_cumsum(_tri(not rev), db, [dbl_c[c] for c in range(nc)])
        dz = dla * (_sigmoid(-z) * (1.0 / GLA_TAU))
        dlr_ref[...] = _mm_nt(dz, wd_ref[...]).astype(BF16)
        dwd_ref[...] += _mm_tn(lr_ref[...], dz)
        dbd_ref[...] += jnp.sum(dz, axis=0, keepdims=True)

    return intra, scan, inter


def _gla_bwd(p, wd_pad_f, bd_f, wd_pad_b, bd_b, st_f, st_b, d_o, tg, comms=()):
    t = p.shape[0]
    nt = t // tg
    nc = tg // GLA_CHUNK
    up, down = (lambda i: i), (lambda i: nt - 1 - i)

    def body(qf, kf, vf, lrf, stf, dof, qb, kb, vb, lrb, stb, dob, wdf, bdf, wdb, bdb,
             dqf, dkf, dvf, dlrf, dwdf, dbdf, dqb, dkb, dvb, dlrb, dwdb, dbdb, dstate_f, dstate_b):
        @pl.when(pl.program_id(0) == 0)
        def _():
            for ref in (dstate_f, dstate_b, dwdf, dbdf, dwdb, dbdb):
                ref[...] = jnp.zeros_like(ref)

        dirs = [_gla_bwd_dir(False, nc, qf, kf, vf, lrf, wdf, bdf, stf, dof, dqf, dkf, dvf, dlrf, dwdf, dbdf,
                             dstate_f),
                _gla_bwd_dir(True, nc, qb, kb, vb, lrb, wdb, bdb, stb, dob, dqb, dkb, dvb, dlrb, dwdb, dbdb,
                             dstate_b)]
        first = [intra() for intra, _, _ in dirs]
        dsts = [scan(tile, per) for (_, scan, _), (_, tile, per) in zip(dirs, first)]
        for (_, _, inter), (z, tile, per), d in zip(dirs, first, dsts):
            inter(z, tile, per, d)

    wd_spec, bd_spec = _const_spec((LANE, KEY_W)), _const_spec((1, KEY_W))
    ins = lambda tile: _p_specs(tg, tile) + [pl.BlockSpec((nc, GLA_DV, KEY_W), lambda i: (tile(i), 0, 0)),
                                             pl.BlockSpec((tg, VAL_W), lambda i: (tile(i), 0))]
    outs = lambda tile: (pl.BlockSpec((tg, KEY_W), lambda i: (tile(i), 0)),
                         pl.BlockSpec((tg, KEY_W), lambda i: (tile(i), 0)),
                         pl.BlockSpec((tg, VAL_W), lambda i: (tile(i), 0)),
                         pl.BlockSpec((tg, LANE), lambda i: (tile(i), 0)),
                         _acc_spec((LANE, KEY_W)), _acc_spec((1, KEY_W)))
    out_shape = (jax.ShapeDtypeStruct((t, KEY_W), BF16), jax.ShapeDtypeStruct((t, KEY_W), BF16),
                 jax.ShapeDtypeStruct((t, VAL_W), BF16), jax.ShapeDtypeStruct((t, LANE), BF16),
                 jax.ShapeDtypeStruct((LANE, KEY_W), F32), jax.ShapeDtypeStruct((1, KEY_W), F32))
    scratch = [pltpu.VMEM((GLA_DV, KEY_W), F32)]
    return _fused_call(
        body, comms, name="gla_bwd", grid=(nt,),
        inputs=(p, p, p, p, st_f, d_o, p, p, p, p, st_b, d_o, wd_pad_f, bd_f, wd_pad_b, bd_b),
        in_specs=ins(down) + ins(up) + [wd_spec, bd_spec, wd_spec, bd_spec],
        out_specs=outs(down) + outs(up), out_shape=out_shape * 2, scratch_shapes=scratch * 2)


def _head_rms(o):
    parts, scales = [], []
    for h in range(GLA_HEADS):
        oh = o[:, h * GLA_DV:(h + 1) * GLA_DV]
        r = lax.rsqrt(jnp.mean(oh * oh, axis=-1, keepdims=True) + EPS)
        parts.append(oh * r)
        scales.append(jnp.broadcast_to(r, oh.shape))
    return jnp.concatenate(parts, axis=1), jnp.concatenate(scales, axis=1)


def _layernorm_stats(zv):
    mu = jnp.mean(zv, axis=-1, keepdims=True)
    xc = zv - mu
    rs = lax.rsqrt(jnp.mean(xc * xc, axis=-1, keepdims=True) + EPS)
    return xc * rs, rs


def _mix_fwd(x, o_f, o_b, p, gla_g, ln_g, ln_b, w_sp, b_sp, w_out, tm, comms=()):
    t = x.shape[0]
    nch = tm // GMLP_CHUNK

    def body(x_ref, of_ref, ob_ref, pg_ref, pu_ref, pv_ref, gg_ref, lg_ref, lb_ref, ws_ref, bs_ref, wo_ref,
             x1_ref, y_ref, s_scr):
        on, _ = _head_rms(of_ref[...] + ob_ref[...])
        pg = pg_ref[...].astype(F32)
        y_a = on * gg_ref[...] * (pg * _sigmoid(pg))
        zu = _gelu(pu_ref[...].astype(F32))
        vhat, _ = _layernorm_stats(_gelu(pv_ref[...].astype(F32)))
        vln = (vhat * lg_ref[...] + lb_ref[...]).astype(BF16)
        for g in range(GMLP_GROUPS):
            w_g = ws_ref[g].astype(BF16)
            b_g = bs_ref[g]
            cols = slice(g * LANE, (g + 1) * LANE)
            for n in range(nch):
                rows = slice(n * GMLP_CHUNK, (n + 1) * GMLP_CHUNK)
                s_scr[rows, cols] = jnp.dot(w_g, vln[rows, cols], preferred_element_type=F32) + b_g
        ycat = jnp.concatenate([y_a, zu * s_scr[...]], axis=1).astype(BF16)
        y_ref[...] = ycat
        x1_ref[...] = x_ref[...] + jnp.dot(ycat, wo_ref[...], preferred_element_type=F32)

    half = lambda j: pl.BlockSpec((tm, VAL_W), lambda i: (i, j))
    return _fused_call(
        body, comms, name="mix_fwd", grid=(t // tm,),
        inputs=(x, o_f, o_b, p, p, p, gla_g, ln_g, ln_b, w_sp, b_sp, w_out),
        in_specs=[pl.BlockSpec((tm, D_MODEL), lambda i: (i, 0)), half(0), half(0), half(2), half(3), half(4),
                  _const_spec((1, VAL_W)), _const_spec((1, GMLP_W)), _const_spec((1, GMLP_W)),
                  _const_spec((GMLP_GROUPS, GMLP_CHUNK, GMLP_CHUNK)), _const_spec((GMLP_GROUPS, GMLP_CHUNK, 1)),
                  _const_spec((D_MODEL, D_MODEL))],
        out_specs=(pl.BlockSpec((tm, D_MODEL), lambda i: (i, 0)), pl.BlockSpec((tm, D_MODEL), lambda i: (i, 0))),
        out_shape=(jax.ShapeDtypeStruct((t, D_MODEL), F32), jax.ShapeDtypeStruct((t, D_MODEL), BF16)),
        scratch_shapes=[pltpu.VMEM((tm, GMLP_W), F32)])


def _mix_bwd(dx1, ycat, o_f, o_b, p, gla_g, ln_g, ln_b, w_sp, b_sp, w_out, tm, comms=()):
    t = dx1.shape[0]
    nch = tm // GMLP_CHUNK

    def body(dx1_ref, y_ref, of_ref, ob_ref, pg_ref, pu_ref, pv_ref, gg_ref, lg_ref, lb_ref, ws_ref, bs_ref, wo_ref,
             do_ref, dpg_ref, dpu_ref, dpv_ref, dwo_ref, dgg_ref, dlg_ref, dlb_ref, dws_ref, dbs_ref,
             s_scr, dvln_scr):
        @pl.when(pl.program_id(0) == 0)
        def _():
            for ref in (dwo_ref, dgg_ref, dlg_ref, dlb_ref, dws_ref, dbs_ref):
                ref[...] = jnp.zeros_like(ref)

        dx1 = dx1_ref[...].astype(BF16)
        dycat = _mm_nt(dx1, wo_ref[...])
        dwo_ref[...] += _mm_tn(y_ref[...], dx1)
        dy_a = dycat[:, :VAL_W]
        dy_b = dycat[:, VAL_W:]
        on, r = _head_rms(of_ref[...] + ob_ref[...])
        pg = pg_ref[...].astype(F32)
        sil, dsil = _silu_and_grad(pg)
        gg = gg_ref[...]
        dgg_ref[...] += jnp.sum(dy_a * sil * on, axis=0, keepdims=True)
        don = dy_a * sil * gg
        prod = don * on
        means = jnp.concatenate(
            [jnp.broadcast_to(jnp.mean(prod[:, h * GLA_DV:(h + 1) * GLA_DV], axis=-1, keepdims=True),
                              (tm, GLA_DV)) for h in range(GLA_HEADS)], axis=1)
        do_ref[...] = (r * (don - on * means)).astype(BF16)
        dpg_ref[...] = (dy_a * on * gg * dsil).astype(BF16)
        pu = pu_ref[...].astype(F32)
        pv = pv_ref[...].astype(F32)
        zu, dzu_dpu = _gelu_and_grad(pu)
        zv, dzv_dpv = _gelu_and_grad(pv)
        vhat, rs = _layernorm_stats(zv)
        lg = lg_ref[...]
        vln = (vhat * lg + lb_ref[...]).astype(BF16)
        ds32 = dy_b * zu
        ds = ds32.astype(BF16)
        blocks = [(g, n) for g in range(GMLP_GROUPS) for n in range(nch)]
        at = lambda g, n: (slice(n * GMLP_CHUNK, (n + 1) * GMLP_CHUNK), slice(g * LANE, (g + 1) * LANE))
        w_sp = [ws_ref[g].astype(BF16) for g in range(GMLP_GROUPS)]
        v_blk = {b: vln[at(*b)] for b in blocks}
        ds_blk = {b: ds[at(*b)] for b in blocks}
        s_blk = {b: jnp.dot(w_sp[b[0]], v_blk[b], preferred_element_type=F32) for b in blocks}
        dw_blk = {b: _mm_nt(ds_blk[b], v_blk[b]) for b in blocks}
        dvln_blk = {b: _mm_tn(w_sp[b[0]], ds_blk[b]) for b in blocks}
        for b in blocks:
            s_scr[at(*b)] = s_blk[b] + bs_ref[b[0]]
            dvln_scr[at(*b)] = dvln_blk[b]
        for g in range(GMLP_GROUPS):
            dws_ref[g] += sum(dw_blk[(g, n)] for n in range(nch))
            dbs_ref[g] += sum(jnp.sum(ds32[at(g, n)], axis=-1, keepdims=True) for n in range(nch))
        dpu_ref[...] = (dy_b * s_scr[...] * dzu_dpu).astype(BF16)
        dvln = dvln_scr[...]
        dlg_ref[...] += jnp.sum(dvln * vhat, axis=0, keepdims=True)
        dlb_ref[...] += jnp.sum(dvln, axis=0, keepdims=True)
        dvhat = dvln * lg
        dzv = rs * (dvhat - jnp.mean(dvhat, axis=-1, keepdims=True)
                    - vhat * jnp.mean(dvhat * vhat, axis=-1, keepdims=True))
        dpv_ref[...] = (dzv * dzv_dpv).astype(BF16)

    half = lambda j: pl.BlockSpec((tm, VAL_W), lambda i: (i, j))
    full = pl.BlockSpec((tm, D_MODEL), lambda i: (i, 0))
    sp_shape = (GMLP_GROUPS, GMLP_CHUNK, GMLP_CHUNK)
    bs_shape = (GMLP_GROUPS, GMLP_CHUNK, 1)
    return _fused_call(
        body, comms, name="mix_bwd", grid=(t // tm,),
        inputs=(dx1, ycat, o_f, o_b, p, p, p, gla_g, ln_g, ln_b, w_sp, b_sp, w_out),
        in_specs=[full, full, half(0), half(0), half(2), half(3), half(4),
                  _const_spec((1, VAL_W)), _const_spec((1, GMLP_W)), _const_spec((1, GMLP_W)),
                  _const_spec(sp_shape), _const_spec(bs_shape), _const_spec((D_MODEL, D_MODEL))],
        out_specs=(half(0), half(0), half(0), half(0), _acc_spec((D_MODEL, D_MODEL)), _acc_spec((1, VAL_W)),
                   _acc_spec((1, GMLP_W)), _acc_spec((1, GMLP_W)), _acc_spec(sp_shape), _acc_spec(bs_shape)),
        out_shape=(jax.ShapeDtypeStruct((t, VAL_W), BF16),) * 4 + (
            jax.ShapeDtypeStruct((D_MODEL, D_MODEL), F32), jax.ShapeDtypeStruct((1, VAL_W), F32),
            jax.ShapeDtypeStruct((1, GMLP_W), F32), jax.ShapeDtypeStruct((1, GMLP_W), F32),
            jax.ShapeDtypeStruct(sp_shape, F32), jax.ShapeDtypeStruct(bs_shape, F32)),
        scratch_shapes=[pltpu.VMEM((tm, GMLP_W), F32), pltpu.VMEM((tm, GMLP_W), F32)])


def _rms_bwd(dy_scaled, xn, r):
    return r * (dy_scaled - xn * jnp.mean(dy_scaled * xn, axis=-1, keepdims=True))


def _ffn(x1, target, g2, gf, w_gate, w_up, w_down, tm):
    t = x1.shape[0]

    def body(x1_ref, tg_ref, g2_ref, gf_ref, wg_ref, wu_ref, wd_ref,
             dx1_ref, h2_ref, dgate_ref, dup_ref, act_ref, dx2_ref, loss_ref, dgf_ref, dg2_ref):
        @pl.when(pl.program_id(0) == 0)
        def _():
            for ref in (loss_ref, dgf_ref, dg2_ref):
                ref[...] = jnp.zeros_like(ref)

        x1v = x1_ref[...]
        g2v = g2_ref[...]
        gfv = gf_ref[...]
        r2 = lax.rsqrt(jnp.mean(x1v * x1v, axis=-1, keepdims=True) + EPS)
        xn1 = x1v * r2
        h2 = (xn1 * g2v).astype(BF16)
        h2_ref[...] = h2
        gate = _mm_nt(h2, wg_ref[...])
        up = _mm_nt(h2, wu_ref[...])
        sil, dsil = _silu_and_grad(gate)
        act = (sil * up).astype(BF16)
        act_ref[...] = act
        x2 = x1v + jnp.dot(act, wd_ref[...], preferred_element_type=F32)
        rf = lax.rsqrt(jnp.mean(x2 * x2, axis=-1, keepdims=True) + EPS)
        xn2 = x2 * rf
        err = xn2 * gfv - tg_ref[...]
        loss_ref[...] += 0.5 * jnp.sum(jnp.mean(err * err, axis=-1, keepdims=True))
        dy = err * (1.0 / D_MODEL)
        dgf_ref[...] += jnp.sum(dy * xn2, axis=0, keepdims=True)
        dx2 = _rms_bwd(dy * gfv, xn2, rf)
        dx2b = dx2.astype(BF16)
        dx2_ref[...] = dx2b
        dact = _mm_nt(dx2b, wd_ref[...])
        dgate = (dact * up * dsil).astype(BF16)
        dup = (dact * sil).astype(BF16)
        dgate_ref[...] = dgate
        dup_ref[...] = dup
        dh2 = _mm(dgate, wg_ref[...]) + _mm(dup, wu_ref[...])
        dg2_ref[...] += jnp.sum(dh2 * xn1, axis=0, keepdims=True)
        dx1_ref[...] = dx2 + _rms_bwd(dh2 * g2v, xn1, r2)

    row = lambda w: pl.BlockSpec((tm, w), lambda i: (i, 0))
    return pl.pallas_call(
        body, name="ffn_fwd_bwd", grid=(t // tm,),
        in_specs=[row(D_MODEL), row(D_MODEL), _const_spec((1, D_MODEL)), _const_spec((1, D_MODEL)),
                  _const_spec((D_FF, D_MODEL)), _const_spec((D_FF, D_MODEL)), _const_spec((D_FF, D_MODEL))],
        out_specs=(row(D_MODEL), row(D_MODEL), row(D_FF), row(D_FF), row(D_FF), row(D_MODEL),
                   _acc_spec((8, LANE)), _acc_spec((1, D_MODEL)), _acc_spec((1, D_MODEL))),
        out_shape=(jax.ShapeDtypeStruct((t, D_MODEL), F32), jax.ShapeDtypeStruct((t, D_MODEL), BF16),
                   jax.ShapeDtypeStruct((t, D_FF), BF16), jax.ShapeDtypeStruct((t, D_FF), BF16),
                   jax.ShapeDtypeStruct((t, D_FF), BF16), jax.ShapeDtypeStruct((t, D_MODEL), BF16),
                   jax.ShapeDtypeStruct((8, LANE), F32), jax.ShapeDtypeStruct((1, D_MODEL), F32),
                   jax.ShapeDtypeStruct((1, D_MODEL), F32)),
        compiler_params=_params(),
    )(x1, target, g2, gf, w_gate, w_up, w_down)


def _matmul_tn(a, b, tm, tk, name, comms=()):
    t, m = a.shape
    n = b.shape[1]
    nj, nk = m // tm, t // tk
    depth = 3

    def body(a_ref, b_ref, o_ref, a_buf, b_buf, sems):
        step = pl.program_id(0) * nk + pl.program_id(1)

        def copies(s):
            if isinstance(s, int):
                j, k, slot = s // nk, s % nk, s % depth
                row0, col0 = k * tk, j * tm
            else:
                j, k, slot = lax.div(s, nk), lax.rem(s, nk), lax.rem(s, depth)
                row0, col0 = pl.multiple_of(k * tk, tk), pl.multiple_of(j * tm, LANE)
            rows = pl.ds(row0, tk)
            return (pltpu.make_async_copy(a_ref.at[rows, pl.ds(col0, tm)], a_buf.at[slot], sems.at[0, slot]),
                    pltpu.make_async_copy(b_ref.at[rows], b_buf.at[slot], sems.at[1, slot]))

        @pl.when(step == 0)
        def _():
            for s in range(min(depth - 1, nj * nk)):
                for cp in copies(s):
                    cp.start()

        @pl.when(step + depth - 1 < nj * nk)
        def _():
            for cp in copies(step + depth - 1):
                cp.start()

        @pl.when(pl.program_id(1) == 0)
        def _():
            o_ref[...] = jnp.zeros_like(o_ref)

        for cp in copies(step):
            cp.wait()
        slot = lax.rem(step, depth)
        o_ref[...] += _mm_tn(a_buf[slot], b_buf[slot])

    (out,), comm_results = _fused_call(
        body, comms, name=name, grid=(nj, nk), inputs=(a, b),
        in_specs=[_HBM_WHOLE, _HBM_WHOLE],
        out_specs=(pl.BlockSpec((tm, n), lambda j, k: (j, 0)),),
        out_shape=(jax.ShapeDtypeStruct((m, n), F32),),
        scratch_shapes=[pltpu.VMEM((depth, tk, tm), BF16), pltpu.VMEM((depth, tk, n), BF16),
                        pltpu.SemaphoreType.DMA((2, depth))])
    return out, comm_results


def _in_proj_bwd(x, g1, dx1, dq_f, dq_b, dk_f, dk_b, dv_f, dv_b, dpg, dpu, dpv, dlr_f, dlr_b, w_main, tm, comms=()):
    t = x.shape[0]

    def body(x_ref, g_ref, dx1_ref, dqf, dqb, dkf, dkb, dvf, dvb, dg, du, dv, dlf, dlb, w_ref,
             dx_ref, dp_ref, dg1_ref):
        @pl.when(pl.program_id(0) == 0)
        def _():
            dg1_ref[...] = jnp.zeros_like(dg1_ref)

        both = lambda a, b: (a[...].astype(F32) + b[...].astype(F32)).astype(BF16)
        dp = jnp.concatenate([both(dqf, dqb), both(dkf, dkb), both(dvf, dvb), dg[...], du[...], dv[...],
                              both(dlf, dlb)], axis=1)
        dp_ref[...] = dp
        dh = sum(_mm(dp[:, c0:c0 + r1 - r0], w_ref[r0:r1, :]) for r0, r1, c0 in PROJ_ROWS)
        xv = x_ref[...]
        r = lax.rsqrt(jnp.mean(xv * xv, axis=-1, keepdims=True) + EPS)
        xn = xv * r
        dg1_ref[...] += jnp.sum(dh * xn, axis=0, keepdims=True)
        dx_ref[...] = dx1_ref[...] + _rms_bwd(dh * g_ref[...], xn, r)

    row = lambda w: pl.BlockSpec((tm, w), lambda i: (i, 0))
    return _fused_call(
        body, comms, name="in_proj_bwd", grid=(t // tm,),
        inputs=(x, g1, dx1, dq_f, dq_b, dk_f, dk_b, dv_f, dv_b, dpg, dpu, dpv, dlr_f, dlr_b, w_main),
        in_specs=[row(D_MODEL), _const_spec((1, D_MODEL)), row(D_MODEL), row(KEY_W), row(KEY_W), row(KEY_W),
                  row(KEY_W), row(VAL_W), row(VAL_W), row(VAL_W), row(VAL_W), row(VAL_W), row(LANE), row(LANE),
                  _const_spec((PROJ_W, D_MODEL))],
        out_specs=(row(D_MODEL), row(PROJ_PAD), _acc_spec((1, D_MODEL))),
        out_shape=(jax.ShapeDtypeStruct((t, D_MODEL), F32), jax.ShapeDtypeStruct((t, PROJ_PAD), BF16),
                   jax.ShapeDtypeStruct((1, D_MODEL), F32)))


def _adamw(w, g, m, v):
    m_new = ADAM_B1 * m + (1.0 - ADAM_B1) * g
    v_new = ADAM_B2 * v + (1.0 - ADAM_B2) * (g * g)
    m_hat = m_new / (1.0 - ADAM_B1 ** ADAM_STEP)
    v_hat = v_new / (1.0 - ADAM_B2 ** ADAM_STEP)
    delta = -ADAM_LR * (m_hat / (jnp.sqrt(v_hat) + ADAM_EPS) + ADAM_WD * w)
    return delta, m_new, v_new


def _adamw_window(own, recv, w, m, v, name):
    r, c = w.shape
    rows = own.shape[0]

    def body(own_ref, recv_ref, w_ref, m_ref, v_ref, g_ref, d_ref, nm_ref, nv_ref):
        g = own_ref[...]
        for k in range(3):
            g = g + recv_ref[k].astype(F32)

        def update(g):
            g_ref[...] = g[:r]
            d_ref[...], nm_ref[...], nv_ref[...] = _adamw(w_ref[...], g[:r], m_ref[...], v_ref[...])

        core = lax.axis_index("c")
        pl.when(core == 0)(lambda: update(g))
        pl.when(core == 1)(lambda: update(pltpu.roll(g, rows - 4, 0)))

    whole = lambda *shape: pl.BlockSpec(shape, lambda i: (0,) * len(shape))
    return pl.pallas_call(
        body, name=name, grid=(1,),
        in_specs=[whole(rows, c), whole(3, rows, c), whole(r, c), whole(r, c), whole(r, c)],
        out_specs=(whole(r, c),) * 4, out_shape=(jax.ShapeDtypeStruct((r, c), F32),) * 4,
        compiler_params=_params(),
    )(own, recv, w, m, v)


def _adamw_shard(own, recv, w, m, v, tr, name):
    r, c = w.shape

    def body(own_ref, recv_ref, w_ref, m_ref, v_ref, g_ref, d_ref, nm_ref, nv_ref):
        g = own_ref[...]
        for k in range(3):
            g = g + recv_ref[k].astype(F32)
        g_ref[...] = g
        d_ref[...], nm_ref[...], nv_ref[...] = _adamw(w_ref[...], g, m_ref[...], v_ref[...])

    row = pl.BlockSpec((tr, c), lambda i: (i, 0))
    return pl.pallas_call(
        body, name=name, grid=(r // tr,),
        in_specs=[row, pl.BlockSpec((3, tr, c), lambda i: (0, i, 0)), row, row, row],
        out_specs=(row,) * 4, out_shape=(jax.ShapeDtypeStruct((r, c), F32),) * 4,
        compiler_params=_params(),
    )(own, recv, w, m, v)


def _adamw_small(entries):
    stacks = []
    for (g, _, _), _, _, _ in entries:
        if not any(g is s for s in stacks):
            stacks.append(g)
    where = [next(i for i, s in enumerate(stacks) if s is g) for (g, _, _), _, _, _ in entries]
    ns, ne = len(stacks), len(entries)

    def body(*refs):
        s_refs, wmv, outs = refs[:ns], refs[ns:ns + 3 * ne], refs[ns + 3 * ne:]
        for e, ((_, r0, nr), _, _, _) in enumerate(entries):
            grad = s_refs[where[e]][r0:r0 + nr, :]
            w_ref, m_ref, v_ref = wmv[3 * e:3 * e + 3]
            g_ref, d_ref, nm_ref, nv_ref = outs[4 * e:4 * e + 4]
            g_ref[...] = grad
            d_ref[...], nm_ref[...], nv_ref[...] = _adamw(w_ref[...], grad, m_ref[...], v_ref[...])

    results = pl.pallas_call(
        body, name="adamw_small",
        out_shape=tuple(jax.ShapeDtypeStruct(w.shape, F32) for _, w, _, _ in entries for _ in range(4)),
        compiler_params=pltpu.CompilerParams(vmem_limit_bytes=VMEM_LIMIT),
    )(*stacks, *[a for _, w, m, v in entries for a in (w, m, v)])
    return [results[4 * e:4 * e + 4] for e in range(ne)]


def _mesh_pos():
    return lax.axis_index("x"), lax.axis_index("y"), lax.axis_index("c")


def _other_chips(x, y):
    return [(x, 1 - y), (1 - x, y), (1 - x, 1 - y)]


_VMEM_WHOLE = pl.BlockSpec(memory_space=pltpu.VMEM)
_HBM_WHOLE = pl.BlockSpec(memory_space=pl.ANY)


def _gather_comm(shards, cast, mid=((1, 2), (3, 4))):
    na = len(shards)
    staged = [a for a in range(na) if cast[a]]

    def phases(in_refs, out_refs, scr):
        stage = dict(zip(staged, scr[:len(staged)]))
        send_sems, recv_sems, local_sems = scr[len(staged):]
        x, y, c = _mesh_pos()
        me, sibling = (x, y, c), (x, y, 1 - c)
        chip_a, chip_b, diagonal = (x ^ c, y ^ (1 - c)), (x ^ (1 - c), y ^ c), (1 - x, 1 - y)
        srcs = [stage[a] if cast[a] else in_refs[a] for a in range(na)]

        def rows(a, pos):
            px, py, pc = pos
            return out_refs[a].at[4 * px + 2 * py + pc]

        def copy(a, k, block, to, src=None):
            return pltpu.make_async_remote_copy(
                src_ref=rows(a, block) if src is None else src, dst_ref=rows(a, block),
                send_sem=send_sems.at[a, k], recv_sem=recv_sems.at[a, k], device_id=to, device_id_type=MESH_ID)

        mine = [pltpu.make_async_copy(srcs[a], rows(a, me), local_sems.at[a]) for a in range(na)]
        own = [copy(a, k, me, to, src=srcs[a]) for a in range(na)
               for k, to in ((0, sibling), (1, (*chip_a, c)), (2, (*chip_b, c)))]
        onward = [copy(a, 3, (*chip_a, c), (*chip_b, c)) for a in range(na)]
        to_sibling = {k: [copy(a, k, (*chip, c), sibling) for a in range(na)]
                      for k, chip in ((4, chip_a), (5, chip_b), (6, diagonal))}

        def start():
            for a in staged:
                stage[a][...] = in_refs[a][...].astype(BF16)
            for cp in mine + own:
                cp.start()

        def forward_neighbours():
            for a in range(na):
                copy(a, 1, (*chip_a, c), me).wait_recv()
                onward[a].start()
                to_sibling[4][a].start()
            for a in range(na):
                copy(a, 2, (*chip_b, c), me).wait_recv()
                to_sibling[5][a].start()

        def forward_diagonal():
            for a in range(na):
                copy(a, 3, (*diagonal, c), me).wait_recv()
                to_sibling[6][a].start()

        def finish():
            for a in range(na):
                for k, chip in ((0, (x, y)), (4, chip_b), (5, chip_a), (6, diagonal)):
                    copy(a, k, (*chip, 1 - c), me).wait_recv()
            for cp in own + onward + to_sibling[4] + to_sibling[5] + to_sibling[6]:
                cp.wait_send()
            for cp in mine:
                cp.wait()

        return start, forward_neighbours, forward_diagonal, finish

    def before(step, nsteps, in_refs, out_refs, scr):
        start, forward_neighbours, forward_diagonal, _ = phases(in_refs, out_refs, scr)
        pl.when(step == 0)(start)
        pl.when(step == nsteps * mid[0][0] // mid[0][1])(forward_neighbours)
        pl.when(step == nsteps * mid[1][0] // mid[1][1])(forward_diagonal)

    def after(step, nsteps, in_refs, out_refs, scr):
        pl.when(step == nsteps - 1)(phases(in_refs, out_refs, scr)[3])

    return _Comm(
        inputs=list(shards), in_specs=[_VMEM_WHOLE] * na,
        out_shape=[jax.ShapeDtypeStruct((N_DEV,) + s.shape, BF16 if cast[a] else s.dtype)
                   for a, s in enumerate(shards)],
        out_specs=[_HBM_WHOLE] * na,
        scratch_shapes=[pltpu.VMEM(shards[a].shape, BF16) for a in staged] + [
            pltpu.SemaphoreType.DMA((na, 7)), pltpu.SemaphoreType.DMA((na, 7)), pltpu.SemaphoreType.DMA((na,))],
        before=before, after=after)


W_IN_WINDOW = 336


def _w_in_block_pieces(g_ref, chip, core):
    j = 2 * chip + core
    rows = PROJ_W // N_DEV
    first = rows * j - jnp.where(j > 4, 2 * LOWRANK, 0)
    start = pl.multiple_of((first >> 3) << 3, 8)
    head = LR_REF - 4 * rows
    split = [(g_ref.at[pl.ds(4 * rows, head)], 0, head), (g_ref.at[pl.ds(LR_COL, 2 * LOWRANK)], head, 2 * LOWRANK),
             (g_ref.at[pl.ds(LR_REF, 64)], head + 2 * LOWRANK, 64)]
    return [(j != 4, [(g_ref.at[pl.ds(start, W_IN_WINDOW)], 0, W_IN_WINDOW)]), (j == 4, split)]


def _reduce_scatter_comm(grads, rows=None, pieces=None):
    if pieces is None:
        _, _, r, c = grads.shape
        pieces = lambda g_ref, chip, core: [(None, [(g_ref.at[chip, core], 0, r)])]
    else:
        r, c = rows, grads.shape[1]
    order = (3, 1, 2, 0)

    def transfer(k, kind, in_refs, scr, act):
        (g_ref,), (sib, own, _, sems) = in_refs, scr
        x, y, core = _mesh_pos()
        chip = (2 * x + y) ^ k
        for cond, parts in pieces(g_ref, chip, 1 - core if kind == "send" else core):
            def run(parts=parts):
                for i, (src, row0, n) in enumerate(parts):
                    if kind == "local":
                        act(pltpu.make_async_copy(src, own.at[k, pl.ds(row0, n)], sems.at[2, 3 * k + i]))
                    else:
                        act(pltpu.make_async_remote_copy(
                            src_ref=src, dst_ref=sib.at[k, pl.ds(row0, n)], send_sem=sems.at[0, 3 * k + i],
                            recv_sem=sems.at[1, 3 * k + i], device_id=(x, y, 1 - core), device_id_type=MESH_ID))

            run() if cond is None else pl.when(cond)(run)

    def chip_copies(out_refs, scr):
        (_, recv), (_, _, part, sems) = out_refs, scr
        x, y, core = _mesh_pos()
        return [pltpu.make_async_remote_copy(
            src_ref=part.at[j], dst_ref=recv.at[j], send_sem=sems.at[3, j], recv_sem=sems.at[4, j],
            device_id=(*chip, core), device_id_type=MESH_ID) for j, chip in enumerate(_other_chips(x, y))]

    def before(step, nsteps, in_refs, out_refs, scr):
        @pl.when(step == 0)
        def _():
            for k in order:
                transfer(k, "send", in_refs, scr, lambda cp: cp.start())
                transfer(k, "local", in_refs, scr, lambda cp: cp.start())

    def after(step, nsteps, in_refs, out_refs, scr):
        sib, own, part, _ = scr

        @pl.when(step == (nsteps - 1) // 2)
        def _():
            to_chips = chip_copies(out_refs, scr)
            for k in order:
                transfer(k, "recv", in_refs, scr, lambda cp: cp.wait_recv())
                transfer(k, "local", in_refs, scr, lambda cp: cp.wait())
                if k:
                    part[k - 1] = (own[k] + sib[k]).astype(BF16)
                    to_chips[k - 1].start()
                else:
                    out_refs[0][...] = own[0] + sib[0]
            for k in order:
                transfer(k, "send", in_refs, scr, lambda cp: cp.wait_send())

        @pl.when(step == nsteps - 1)
        def _():
            for cp in chip_copies(out_refs, scr):
                cp.wait()

    return _Comm(inputs=[grads], in_specs=[_HBM_WHOLE],
                 out_shape=[jax.ShapeDtypeStruct((r, c), F32), jax.ShapeDtypeStruct((3, r, c), BF16)],
                 out_specs=[_VMEM_WHOLE, _HBM_WHOLE],
                 scratch_shapes=[pltpu.VMEM((4, r, c), F32), pltpu.VMEM((4, r, c), F32), pltpu.VMEM((3, r, c), BF16),
                                 pltpu.SemaphoreType.DMA((5, 12))],
                 before=before, after=after)


def _exchange_comm(arrays, out_shape, make_copies):
    na = len(arrays)

    def copies(in_refs, out_refs, scr):
        return make_copies(in_refs, out_refs, *scr)

    def before(step, nsteps, in_refs, out_refs, scr):
        @pl.when(step == 0)
        def _():
            for cp in copies(in_refs, out_refs, scr):
                cp.start()

    def after(step, nsteps, in_refs, out_refs, scr):
        @pl.when(step == nsteps - 1)
        def _():
            for cp in copies(in_refs, out_refs, scr):
                cp.wait()

    return _Comm(inputs=list(arrays), in_specs=[_HBM_WHOLE] * na, out_shape=list(out_shape),
                 out_specs=[_HBM_WHOLE] * na,
                 scratch_shapes=[pltpu.SemaphoreType.DMA((na, 3)), pltpu.SemaphoreType.DMA((na, 3))],
                 before=before, after=after)


def _sibling_exchange_comm(grads):
    def make_copies(in_refs, out_refs, send_sems, recv_sems):
        x, y, c = _mesh_pos()
        return [pltpu.make_async_remote_copy(
            src_ref=in_refs[a].at[:, pl.ds(1 - c, 1)], dst_ref=out_refs[a], send_sem=send_sems.at[a, 0],
            recv_sem=recv_sems.at[a, 0], device_id=(x, y, 1 - c), device_id_type=MESH_ID)
            for a in range(len(grads))]

    return _exchange_comm(grads, [jax.ShapeDtypeStruct((4, 1) + g.shape[2:], F32) for g in grads], make_copies)


def _chip_sum(my_pos, mine, from_sibling, tr, name):
    _, _, r, c = mine.shape

    def body(pos_ref, a_ref, b_ref, own_ref, out_ref):
        s = a_ref[0, 0] + b_ref[0, 0]

        @pl.when(pl.program_id(1) == 0)
        def _():
            own_ref[...] = s

        @pl.when(pl.program_id(1) > 0)
        def _():
            out_ref[0] = s.astype(BF16)

    grid_spec = pltpu.PrefetchScalarGridSpec(
        num_scalar_prefetch=1, grid=(r // tr, 4),
        in_specs=[pl.BlockSpec((1, 1, tr, c), lambda i, k, pos: (pos[0] ^ k, pos[1], i, 0)),
                  pl.BlockSpec((1, 1, tr, c), lambda i, k, pos: (pos[0] ^ k, 0, i, 0))],
        out_specs=(pl.BlockSpec((tr, c), lambda i, k, pos: (i, 0)),
                   pl.BlockSpec((1, tr, c), lambda i, k, pos: (jnp.maximum(k - 1, 0), i, 0))))
    return pl.pallas_call(
        body, name=name, grid_spec=grid_spec,
        out_shape=(jax.ShapeDtypeStruct((r, c), F32), jax.ShapeDtypeStruct((3, r, c), BF16)),
        compiler_params=_params(2),
    )(my_pos, mine, from_sibling)


def _chips_exchange_comm(partials):
    def make_copies(in_refs, out_refs, send_sems, recv_sems):
        x, y, c = _mesh_pos()
        return [pltpu.make_async_remote_copy(
            src_ref=in_refs[a].at[j], dst_ref=out_refs[a].at[j], send_sem=send_sems.at[a, j],
            recv_sem=recv_sems.at[a, j], device_id=(*chip, c), device_id_type=MESH_ID)
            for a in range(len(partials)) for j, chip in enumerate(_other_chips(x, y))]

    return _exchange_comm(partials, [jax.ShapeDtypeStruct(g.shape, BF16) for g in partials], make_copies)


def _comm_only(comms, name):
    return _fused_call(lambda: None, comms, name=name, grid=(1,), inputs=(), in_specs=[], out_specs=(),
                       out_shape=())[1]


def _all_reduce_small_comm(parts):
    na = len(parts)

    def copies(in_refs, scr):
        gathered, (send_sems, recv_sems) = scr[:na], scr[na:]
        x, y, c = _mesh_pos()
        my_id = 4 * x + 2 * y + c
        return my_id, [pltpu.make_async_remote_copy(
            src_ref=in_refs[a], dst_ref=gathered[a].at[my_id], send_sem=send_sems.at[a, k - 1],
            recv_sem=recv_sems.at[a, k - 1], device_id=(x ^ (k >> 2), y ^ ((k >> 1) & 1), c ^ (k & 1)),
            device_id_type=MESH_ID) for a in range(na) for k in range(1, N_DEV)]

    def before(step, nsteps, in_refs, out_refs, scr):
        @pl.when(step == 0)
        def _():
            for cp in copies(in_refs, scr)[1]:
                cp.start()

    def after(step, nsteps, in_refs, out_refs, scr):
        @pl.when(step == nsteps - 1)
        def _():
            my_id, cps = copies(in_refs, scr)
            for a in range(na):
                scr[a][my_id] = in_refs[a][...]
            for cp in cps:
                cp.wait()
            for a in range(na):
                acc = scr[a][0]
                for d in range(1, N_DEV):
                    acc = acc + scr[a][d]
                out_refs[a][...] = acc

    return _Comm(inputs=list(parts), in_specs=[_VMEM_WHOLE] * na,
                 out_shape=[jax.ShapeDtypeStruct(p.shape, F32) for p in parts], out_specs=[_VMEM_WHOLE] * na,
                 scratch_shapes=[pltpu.VMEM((N_DEV,) + p.shape, F32) for p in parts] + [
                     pltpu.SemaphoreType.DMA((na, N_DEV - 1)), pltpu.SemaphoreType.DMA((na, N_DEV - 1))],
                 before=before, after=after)


def _unshard_cols(g):
    return jnp.transpose(g, (1, 0, 2)).reshape(g.shape[1], N_DEV * g.shape[2])


def _row_blocks(w):
    return w.reshape(4, 2, w.shape[0] // N_DEV, w.shape[1])


def _stack_rows(parts):
    a = jnp.concatenate(parts, axis=0)
    return jnp.pad(a, ((0, (-a.shape[0]) % 8), (0, 0)))


def _padded_decay_weights(wd_f, wd_b):
    zeros = lambda n: jnp.zeros((n, KEY_W), F32)
    return (jnp.concatenate([wd_f, zeros(LANE - LOWRANK)], axis=0),
            jnp.concatenate([zeros(LOWRANK), wd_b, zeros(LANE - 2 * LOWRANK)], axis=0))


def kernel(x, norm1_g, w_in,w_decay_f, b_decay_f, w_decay_b, b_decay_b, gla_norm_g, gmlp_ln_g, gmlp_ln_b, w_spatial, b_spatial, w_out, norm2_g, w_gate, w_up, w_down, final_norm_g, loss_target, m_norm1_g, m_w_in, m_w_decay_f, m_b_decay_f, m_w_decay_b, m_b_decay_b, m_gla_norm_g, m_gmlp_ln_g, m_gmlp_ln_b, m_w_spatial, m_b_spatial, m_w_out, m_norm2_g, m_w_gate, m_w_up, m_w_down, m_final_norm_g, v_norm1_g, v_w_in, v_w_decay_f, v_b_decay_f, v_w_decay_b, v_b_decay_b, v_gla_norm_g, v_gmlp_ln_g, v_gmlp_ln_b, v_w_spatial, v_b_spatial, v_w_out, v_norm2_g, v_w_gate, v_w_up, v_w_down, v_final_norm_g):
    t = x.shape[1]
    xt = x[0]
    target = loss_target[0]
    pos_x, pos_y, pos_c = _mesh_pos()
    my_pos = jnp.stack([2 * pos_x + pos_y, pos_c]).astype(jnp.int32)
    my_id = 4 * pos_x + 2 * pos_y + pos_c

    tile = lambda n: min(n, t)
    ln_g, ln_b, w_sp = gmlp_ln_g, gmlp_ln_b, w_spatial[0]
    b_sp_col = b_spatial[0][:, :, None]
    shard = {"w_in": w_in[0].T, "w_out": w_out[0], "w_gate": w_gate[0].T, "w_up": w_up[0].T, "w_down": w_down[0]}
    shard_m = {"w_in": m_w_in[0].T, "w_out": m_w_out[0], "w_gate": m_w_gate[0].T, "w_up": m_w_up[0].T,
               "w_down": m_w_down[0]}
    shard_v = {"w_in": v_w_in[0].T, "w_out": v_w_out[0], "w_gate": v_w_gate[0].T, "w_up": v_w_up[0].T,
               "w_down": v_w_down[0]}
    transposed = ("w_in", "w_gate", "w_up")

    decay_shard = jnp.stack([w_decay_f[0], w_decay_b[0]])
    (hb,), ((g_in, g_decay),) = _norm1(xt, norm1_g, tile(TOKEN_TILE["norm1"]),
                                       [_gather_comm([shard["w_in"], decay_shard], [True, False])])
    w_in_t = g_in.reshape(PROJ_W, D_MODEL)
    wd_pad_f, wd_pad_b = _padded_decay_weights(_unshard_cols(g_decay[:, 0]), _unshard_cols(g_decay[:, 1]))
    (p,), ((g_gate, g_out),) = _in_proj(
        hb, w_in_t, tile(TOKEN_TILE["in_proj"]), [_gather_comm([shard["w_gate"], shard["w_out"]], [True, True])])
    (o_f, st_f, o_b, st_b), ((g_up,),) = _gla_fwd(
        p, wd_pad_f, b_decay_f, wd_pad_b, b_decay_b, tile(TOKEN_TILE["gla"]), [_gather_comm([shard["w_up"]], [True])])
    w_out_full = g_out.reshape(D_MODEL, D_MODEL)
    (x1, ycat), ((g_down,),) = _mix_fwd(xt, o_f, o_b, p, gla_norm_g, ln_g, ln_b, w_sp, b_sp_col, w_out_full,
                                        tile(TOKEN_TILE["mix_fwd"]), [_gather_comm([shard["w_down"]], [True])])

    dx1, h2b, dgate, dup, act, dx2, loss_acc, d_gf, d_g2 = _ffn(
        x1, target, norm2_g, final_norm_g[None, :], g_gate.reshape(D_FF, D_MODEL), g_up.reshape(D_FF, D_MODEL),
        g_down.reshape(D_FF, D_MODEL), tile(TOKEN_TILE["ffn"]))
    dw_gate, _ = _matmul_tn(dgate, h2b, D_FF // 2, tile(TOKEN_TILE["dw"]), "grad_w_gate")
    dw_up, _ = _matmul_tn(dup, h2b, D_FF // 2, tile(TOKEN_TILE["dw"]), "grad_w_up")
    dw_down, _ = _matmul_tn(act, dx2, D_FF // 2, tile(TOKEN_TILE["dw"]), "grad_w_down")

    reduced = {}
    staged = {"w_gate": _row_blocks(dw_gate), "w_up": _row_blocks(dw_up)}
    (d_o, dpg, dpu, dpv, dw_out, d_gg, d_lg, d_lb, dw_sp, db_sp), (reduced["w_down"], staged_sib) = _mix_bwd(
        dx1, ycat, o_f, o_b, p, gla_norm_g, ln_g, ln_b, w_sp, b_sp_col, w_out_full,
        tile(TOKEN_TILE["mix_bwd"]),
        [_reduce_scatter_comm(_row_blocks(dw_down)), _sibling_exchange_comm(list(staged.values()))])
    staged_sums = [_chip_sum(my_pos, g, s, g.shape[2], "chip_sum_" + n)
                   for (n, g), s in zip(staged.items(), staged_sib)]
    (dq_f, dk_f, dv_f, dlr_f, dwd_f, dbd_f, dq_b, dk_b, dv_b, dlr_b, dwd_b, dbd_b), (reduced["w_out"], staged_recv) = (
        _gla_bwd(p, wd_pad_f, b_decay_f, wd_pad_b, b_decay_b, st_f, st_b, d_o, tile(TOKEN_TILE["gla"]),
                 [_reduce_scatter_comm(_row_blocks(dw_out)), _chips_exchange_comm([s[1] for s in staged_sums])]))
    for n, s, rc in zip(staged, staged_sums, staged_recv):
        reduced[n] = (s[0], rc)
    (grad_x, dp, d_g1), _ = _in_proj_bwd(
        xt, norm1_g, dx1, dq_f, dq_b, dk_f, dk_b, dv_f, dv_b, dpg, dpu, dpv, dlr_f, dlr_b, w_in_t,
        tile(TOKEN_TILE["in_proj_bwd"]))

    stacks = [_stack_rows([d_g1, d_g2, d_gf]), _stack_rows([d_gg, d_lg, d_lb]),
              _stack_rows([dbd_f, dbd_b, jnp.zeros((DECAY_W_ROW - 2, KEY_W), F32), dwd_f[:LOWRANK],
                           dwd_b[LOWRANK:2 * LOWRANK]]),
              _stack_rows([dw_sp.reshape(GMLP_W, GMLP_CHUNK), db_sp[:, :, 0], loss_acc[:1]])]
    dw_main, (small_sums,) = _matmul_tn(dp, hb, PROJ_PAD // 3, tile(TOKEN_TILE["dw"]), "grad_w_in",
                                        [_all_reduce_small_comm(stacks)])
    ((in_own, in_recv),) = _comm_only(
        [_reduce_scatter_comm(dw_main, W_IN_WINDOW, _w_in_block_pieces)], "grad_w_in_reduce_scatter")

    big_out = {"w_in": [r.T for r in _adamw_window(in_own, in_recv, shard["w_in"], shard_m["w_in"], shard_v["w_in"],
                                                   "adamw_w_in")]}
    for n, (own_sum, recv) in reduced.items():
        rows = shard[n].shape[0]
        half = rows // 2 if rows % 32 == 0 else rows
        res = _adamw_shard(own_sum, recv, shard[n], shard_m[n], shard_v[n], half, "adamw_" + n)
        big_out[n] = [r.T if n in transposed else r for r in res]

    s1024, s512, s256, s128 = small_sums
    loss = s128[GMLP_W + GMLP_GROUPS, 0]
    col0 = my_id * (KEY_W // N_DEV)
    decay_cols = lambda row0: lax.dynamic_slice(s256, (row0, col0), (LOWRANK, KEY_W // N_DEV))
    flat = lambda a: a.reshape(-1, a.shape[-1])
    small = {
        "norm1_g": ((s1024, 0, 1), norm1_g, m_norm1_g, v_norm1_g),
        "w_decay_f": ((decay_cols(DECAY_W_ROW), 0, LOWRANK), w_decay_f, m_w_decay_f, v_w_decay_f),
        "b_decay_f": ((s256, 0, 1), b_decay_f, m_b_decay_f, v_b_decay_f),
        "w_decay_b": ((decay_cols(DECAY_W_ROW + LOWRANK), 0, LOWRANK), w_decay_b, m_w_decay_b, v_w_decay_b),
        "b_decay_b": ((s256, 1, 1), b_decay_b, m_b_decay_b, v_b_decay_b),
        "gla_norm_g": ((s512, 0, 1), gla_norm_g, m_gla_norm_g, v_gla_norm_g),
        "gmlp_ln_g": ((s512, 1, 1), gmlp_ln_g, m_gmlp_ln_g, v_gmlp_ln_g),
        "gmlp_ln_b": ((s512, 2, 1), gmlp_ln_b, m_gmlp_ln_b, v_gmlp_ln_b),
        "w_spatial": ((s128, 0, GMLP_W), w_spatial, m_w_spatial, v_w_spatial),
        "b_spatial": ((s128, GMLP_W, GMLP_GROUPS), b_spatial, m_b_spatial, v_b_spatial),
        "norm2_g": ((s1024, 1, 1), norm2_g, m_norm2_g, v_norm2_g),
        "final_norm_g": ((s1024, 2, 1), final_norm_g, m_final_norm_g, v_final_norm_g),
    }
    small_res = _adamw_small([(g, flat(w), flat(m), flat(v)) for g, w, m, v in small.values()])
    small_out = {n: [r.reshape(small[n][1].shape) for r in res] for n, res in zip(small, small_res)}

    order = ["norm1_g", "w_in", "w_decay_f", "b_decay_f", "w_decay_b", "b_decay_b", "gla_norm_g", "gmlp_ln_g",
             "gmlp_ln_b", "w_spatial", "b_spatial", "w_out", "norm2_g", "w_gate", "w_up", "w_down", "final_norm_g"]
    outs = []
    for kind in range(4):
        for n in order:
            outs.append(big_out[n][kind][None] if n in big_out else small_out[n][kind])
    return (loss, grad_x[None], *outs)
```

```python
import functools
import math

import jax
import jax.numpy as jnp
from jax import lax
from jax.experimental import pallas as pl
from jax.experimental.pallas import tpu as pltpu

F32 = jnp.float32
BF16 = jnp.bfloat16

D_MODEL = 1024
GLA_HEADS = 4
GLA_DK = 64
GLA_DV = 128
KEY_W = GLA_HEADS * GLA_DK
VAL_W = GLA_HEADS * GLA_DV
LOWRANK = 16
GLA_TAU = 16.0
GLA_CHUNK = 64
GMLP_W = 512
GMLP_GROUPS = 4
GMLP_CHUNK = 128
D_FF = 2816
EPS = 1e-6
Q_SCALE = GLA_DK ** -0.5
PROJ_PAD = 2688
LR_COL = 2560
LANE = 128
N_DEV = 8

ADAM_LR = 0.001
ADAM_B1 = 0.9
ADAM_B2 = 0.999
ADAM_EPS = 1e-08
ADAM_WD = 0.01
ADAM_STEP = 10

VMEM_LIMIT = 56 * 1024 * 1024
TOKEN_TILE = {"norm1": 512, "in_proj": 1024, "gla": 1024, "mix_fwd": 1024, "ffn": 256, "mix_bwd": 512,
              "in_proj_bwd": 512, "dw": 2048}
DECAY_W_ROW = 8
MESH_ID = pl.DeviceIdType.MESH
INV_SQRT2 = 0.7071067811865476
INV_SQRT_2PI = 0.3989422804014327


def _params(n_axes=1):
    return pltpu.CompilerParams(dimension_semantics=("arbitrary",) * n_axes, vmem_limit_bytes=VMEM_LIMIT)


def _mm(a, b):
    return jnp.dot(a.astype(BF16), b.astype(BF16), preferred_element_type=F32)


def _mm_nt(a, b):
    return lax.dot_general(a.astype(BF16), b.astype(BF16), (((1,), (1,)), ((), ())), preferred_element_type=F32)


def _mm_tn(a, b):
    return lax.dot_general(a.astype(BF16), b.astype(BF16), (((0,), (0,)), ((), ())), preferred_element_type=F32)


def _const_spec(shape):
    nd = len(shape)
    return pl.BlockSpec(shape, lambda *_: (0,) * nd, pipeline_mode=pl.Buffered(1))


def _acc_spec(shape):
    nd = len(shape)
    return pl.BlockSpec(shape, lambda *_: (0,) * nd)


class _Comm:
    def __init__(self, inputs, in_specs, out_shape, out_specs, scratch_shapes, before, after):
        self.inputs, self.in_specs, self.out_shape, self.out_specs = inputs, in_specs, out_shape, out_specs
        self.scratch_shapes, self.before, self.after = scratch_shapes, before, after


def _fused_call(body, comms, *, name, grid, inputs, in_specs, out_specs, out_shape, scratch_shapes=()):
    n_in, n_out, n_scr = len(in_specs), len(out_specs), len(scratch_shapes)
    nsteps = math.prod(grid)
    sizes = [(len(c.inputs), len(c.out_shape), len(c.scratch_shapes)) for c in comms]

    def full_body(*refs):
        step = pl.program_id(0)
        for axis in range(1, len(grid)):
            step = step * grid[axis] + pl.program_id(axis)
        ins, rest = refs[:n_in], refs[n_in:]
        c_ins = []
        for ci, _, _ in sizes:
            c_ins.append(rest[:ci])
            rest = rest[ci:]
        outs, rest = rest[:n_out], rest[n_out:]
        c_outs = []
        for _, co, _ in sizes:
            c_outs.append(rest[:co])
            rest = rest[co:]
        scr, rest = rest[:n_scr], rest[n_scr:]
        c_scr = []
        for _, _, cs in sizes:
            c_scr.append(rest[:cs])
            rest = rest[cs:]
        for c, a, b, s in zip(comms, c_ins, c_outs, c_scr):
            c.before(step, nsteps, a, b, s)
        body(*ins, *outs, *scr)
        for c, a, b, s in zip(comms, c_ins, c_outs, c_scr):
            c.after(step, nsteps, a, b, s)

    results = pl.pallas_call(
        full_body, name=name, grid=grid,
        in_specs=list(in_specs) + [s for c in comms for s in c.in_specs],
        out_specs=tuple(out_specs) + tuple(s for c in comms for s in c.out_specs),
        out_shape=tuple(out_shape) + tuple(s for c in comms for s in c.out_shape),
        scratch_shapes=list(scratch_shapes) + [s for c in comms for s in c.scratch_shapes],
        compiler_params=_params(len(grid)),
    )(*inputs, *[a for c in comms for a in c.inputs])
    own, rest = results[:n_out], results[n_out:]
    comm_results = []
    for _, co, _ in sizes:
        comm_results.append(rest[:co])
        rest = rest[co:]
    return own, comm_results


def _gelu(x):
    return 0.5 * x * (1.0 + lax.erf(x * INV_SQRT2))


def _gelu_and_grad(x):
    cdf = 0.5 * (1.0 + lax.erf(x * INV_SQRT2))
    return x * cdf, cdf + x * jnp.exp(-0.5 * x * x) * INV_SQRT_2PI


def _sigmoid(x):
    return 0.5 + 0.5 * jnp.tanh(0.5 * x)


def _silu_and_grad(x):
    s = _sigmoid(x)
    return x * s, s * (1.0 + x * (1.0 - s))


def _norm1(x, g1, tm, comms=()):
    t = x.shape[0]

    def body(x_ref, g_ref, h_ref):
        xv = x_ref[...]
        r = lax.rsqrt(jnp.mean(xv * xv, axis=-1, keepdims=True) + EPS)
        h_ref[...] = (xv * r * g_ref[...]).astype(BF16)

    row = pl.BlockSpec((tm, D_MODEL), lambda i: (i, 0))
    return _fused_call(body, comms, name="norm1", grid=(t // tm,), inputs=(x, g1),
                       in_specs=[row, _const_spec((1, D_MODEL))], out_specs=(row,),
                       out_shape=(jax.ShapeDtypeStruct((t, D_MODEL), BF16),))


PROJ_W = 2592
LR_REF = 1536
PROJ_ROWS = ((0, LR_REF, 0), (LR_REF + 2 * LOWRANK, PROJ_W, LR_REF), (LR_REF, LR_REF + LANE, LR_COL))


def _in_proj(h, w_in_t, tm, comms=()):
    t = h.shape[0]

    def body(h_ref, w_ref, p_ref):
        hv = h_ref[...]
        for r0, r1, c0 in PROJ_ROWS:
            p_ref[:, c0:c0 + r1 - r0] = _mm_nt(hv, w_ref[r0:r1, :]).astype(BF16)

    return _fused_call(
        body, comms, name="in_proj", grid=(t // tm,), inputs=(h, w_in_t),
        in_specs=[pl.BlockSpec((tm, D_MODEL), lambda i: (i, 0)), _const_spec((PROJ_W, D_MODEL))],
        out_specs=(pl.BlockSpec((tm, PROJ_PAD), lambda i: (i, 0)),),
        out_shape=(jax.ShapeDtypeStruct((t, PROJ_PAD), BF16),))


def _tri(upper):
    r = lax.broadcasted_iota(jnp.int32, (GLA_CHUNK, GLA_CHUNK), 0)
    c = lax.broadcasted_iota(jnp.int32, (GLA_CHUNK, GLA_CHUNK), 1)
    return jnp.where((c >= r) if upper else (c <= r), 1.0, 0.0).astype(BF16)


def _chunk_cumsum(tri, a, add=None):
    hi = a.astype(BF16)
    lo = (a - hi.astype(F32)).astype(BF16)
    dot = functools.partial(jnp.dot, preferred_element_type=F32)
    sums = [dot(tri, hi[_chunk_rows(c)]) + dot(tri, lo[_chunk_rows(c)]) for c in range(a.shape[0] // GLA_CHUNK)]
    return jnp.concatenate(sums if add is None else [s + r for s, r in zip(sums, add)], axis=0)


def _chunk_rows(c):
    return slice(c * GLA_CHUNK, (c + 1) * GLA_CHUNK)


def _gla_masks(rev):
    dk_bits, dv_bits = GLA_DK.bit_length() - 1, GLA_DV.bit_length() - 1
    key_head = lax.broadcasted_iota(jnp.int32, (GLA_CHUNK, KEY_W), 1) >> dk_bits
    val_head = lax.broadcasted_iota(jnp.int32, (GLA_CHUNK, VAL_W), 1) >> dv_bits
    t = lax.broadcasted_iota(jnp.int32, (GLA_HEADS * GLA_CHUNK, GLA_CHUNK), 0) & (GLA_CHUNK - 1)
    s = lax.broadcasted_iota(jnp.int32, (GLA_HEADS * GLA_CHUNK, GLA_CHUNK), 1)
    return key_head, val_head, (s >= t) if rev else (s <= t)


def _stack_heads(a, head_of_lane):
    a = a.astype(BF16)
    return jnp.concatenate([jnp.where(head_of_lane == h, a, jnp.zeros_like(a)) for h in range(GLA_HEADS)], axis=0)


def _rows_by_head(a):
    return jnp.concatenate([a[:, h * GLA_DV:(h + 1) * GLA_DV] for h in range(GLA_HEADS)], axis=0)


def _lanes_by_head(r):
    return jnp.concatenate([r[h * GLA_CHUNK:(h + 1) * GLA_CHUNK] for h in range(GLA_HEADS)], axis=1)


def _head_diagonal(r, head_of_lane):
    rows = r.shape[0] // GLA_HEADS
    out = jnp.where(head_of_lane == 0, r[:rows], 0.0)
    for h in range(1, GLA_HEADS):
        out = out + jnp.where(head_of_lane == h, r[h * rows:(h + 1) * rows], 0.0)
    return out


def _tile_terms(la, q, k, tri, rev):
    nc = la.shape[0] // GLA_CHUNK
    q, k = q.astype(F32), k.astype(F32)
    b = _chunk_cumsum(tri, la)
    ebl = [jnp.exp(b[c * GLA_CHUNK:c * GLA_CHUNK + 1] if rev else b[(c + 1) * GLA_CHUNK - 1:(c + 1) * GLA_CHUNK])
           for c in range(nc)]
    eb = jnp.exp(b)
    enb = jnp.exp(-b)
    kd = k * enb
    ke = jnp.concatenate([kd[_chunk_rows(c)] * ebl[c] for c in range(nc)], axis=0)
    return ebl, eb, enb, q * Q_SCALE * eb, kd, ke


def _log_decay(lr_ref, wd_ref, bd_ref):
    z = _mm(lr_ref[...], wd_ref[...]) + bd_ref[...]
    return z, jax.nn.log_sigmoid(z) * (1.0 / GLA_TAU)


def _p_specs(tg, tile):
    return [pl.BlockSpec((tg, KEY_W), lambda i: (tile(i), 0)),
            pl.BlockSpec((tg, KEY_W), lambda i: (tile(i), 1)),
            pl.BlockSpec((tg, VAL_W), lambda i: (tile(i), 1)),
            pl.BlockSpec((tg, LANE), lambda i: (tile(i), LR_COL // LANE))]


def _gla_fwd_dir(rev, nc, q_ref, k_ref, v_ref, lr_ref, wd_ref, bd_ref, o_ref, st_ref, state):
    key_head, _, causal = _gla_masks(rev)
    order = range(nc - 1, -1, -1) if rev else range(nc)

    def intra():
        _, la = _log_decay(lr_ref, wd_ref, bd_ref)
        ebl, _, _, qd, kd, ke = _tile_terms(la, q_ref[...], k_ref[...], _tri(rev), rev)
        kd = kd.astype(BF16)
        v = {c: v_ref[_chunk_rows(c), :].astype(BF16) for c in order}
        qd_stack = {c: _stack_heads(qd[_chunk_rows(c)], key_head) for c in order}
        ke_stack = {c: _stack_heads(ke[_chunk_rows(c)], key_head) for c in order}
        a_all = {c: _mm_nt(qd_stack[c], kd[_chunk_rows(c)]) for c in order}
        a_all = {c: jnp.where(causal, a_all[c], 0.0).astype(BF16) for c in order}
        head_rows = lambda a, h: a[h * GLA_CHUNK:(h + 1) * GLA_CHUNK]
        head_vals = lambda a, h: a[:, h * GLA_DV:(h + 1) * GLA_DV]
        r = {c: [_mm(head_rows(a_all[c], h), head_vals(v[c], h)) for h in range(GLA_HEADS)] for c in order}
        upd = {c: _mm_tn(_rows_by_head(v[c]), ke_stack[c]) for c in order}
        return {c: (ebl[c], qd_stack[c], r[c], upd[c]) for c in order}

    def scan(terms):
        st = state[...]
        states = {}
        for c in order:
            states[c] = st
            st_ref[c] = st.astype(BF16)
            st = st * terms[c][0] + terms[c][3]
        state[...] = st
        return states

    def inter(terms, states):
        r_inter = {c: _mm_nt(terms[c][1], states[c]) for c in order}
        for c in order:
            o_ref[_chunk_rows(c), :] = jnp.concatenate(
                [terms[c][2][h] + r_inter[c][h * GLA_CHUNK:(h + 1) * GLA_CHUNK] for h in range(GLA_HEADS)], axis=1)

    return intra, scan, inter


def _gla_fwd(p, wd_pad_f, bd_f, wd_pad_b, bd_b, tg, comms=()):
    t = p.shape[0]
    nt = t // tg
    nc = tg // GLA_CHUNK
    up, down = (lambda i: i), (lambda i: nt - 1 - i)

    def body(qf, kf, vf, lrf, qb, kb, vb, lrb, wdf, bdf, wdb, bdb, of, stf, ob, stb, state_f, state_b):
        @pl.when(pl.program_id(0) == 0)
        def _():
            state_f[...] = jnp.zeros_like(state_f)
            state_b[...] = jnp.zeros_like(state_b)

        dirs = [_gla_fwd_dir(False, nc, qf, kf, vf, lrf, wdf, bdf, of, stf, state_f),
                _gla_fwd_dir(True, nc, qb, kb, vb, lrb, wdb, bdb, ob, stb, state_b)]
        terms = [intra() for intra, _, _ in dirs]
        states = [scan(t) for (_, scan, _), t in zip(dirs, terms)]
        for (_, _, inter), t, s in zip(dirs, terms, states):
            inter(t, s)

    wd_spec, bd_spec = _const_spec((LANE, KEY_W)), _const_spec((1, KEY_W))
    outs = lambda tile: (pl.BlockSpec((tg, VAL_W), lambda i: (tile(i), 0)),
                         pl.BlockSpec((nc, GLA_DV, KEY_W), lambda i: (tile(i), 0, 0)))
    out_shape = (jax.ShapeDtypeStruct((t, VAL_W), F32), jax.ShapeDtypeStruct((t // GLA_CHUNK, GLA_DV, KEY_W), BF16))
    return _fused_call(
        body, comms, name="gla_fwd", grid=(nt,), inputs=(p,) * 8 + (wd_pad_f, bd_f, wd_pad_b, bd_b),
        in_specs=_p_specs(tg, up) + _p_specs(tg, down) + [wd_spec, bd_spec, wd_spec, bd_spec],
        out_specs=outs(up) + outs(down), out_shape=out_shape * 2,
        scratch_shapes=[pltpu.VMEM((GLA_DV, KEY_W), F32)] * 2)


def _gla_bwd_dir(rev, nc, q_ref, k_ref, v_ref, lr_ref, wd_ref, bd_ref, st_ref, do_ref,
                 dq_ref, dk_ref, dv_ref, dlr_ref, dwd_ref, dbd_ref, dstate):
    key_head, val_head, causal = _gla_masks(rev)
    order = range(nc) if rev else range(nc - 1, -1, -1)

    def intra():
        z, la = _log_decay(lr_ref, wd_ref, bd_ref)
        tile = _tile_terms(la, q_ref[...], k_ref[...], _tri(rev), rev)
        qd, kd = tile[3], tile[4].astype(BF16)
        v = {c: v_ref[_chunk_rows(c), :].astype(BF16) for c in order}
        d_o = {c: do_ref[_chunk_rows(c), :] for c in order}
        kd_c = {c: kd[_chunk_rows(c)] for c in order}
        qd_stack = {c: _stack_heads(qd[_chunk_rows(c)], key_head) for c in order}
        do_stack = {c: _stack_heads(d_o[c], val_head) for c in order}
        do_rows = {c: _rows_by_head(d_o[c]) for c in order}
        a_all = {c: _mm_nt(qd_stack[c], kd_c[c]) for c in order}
        head_vals = lambda a, h: a[:, h * GLA_DV:(h + 1) * GLA_DV]
        da_all = {c: jnp.concatenate([_mm_nt(head_vals(d_o[c], h), head_vals(v[c], h)) for h in range(GLA_HEADS)],
                                     axis=0) for c in order}
        a_all = {c: jnp.where(causal, a_all[c], 0.0).astype(BF16) for c in order}
        da_all = {c: jnp.where(causal, da_all[c], 0.0).astype(BF16) for c in order}
        dv = {c: _mm_tn(a_all[c], do_stack[c]) for c in order}
        dqd = {c: _mm(jnp.concatenate([do_rows[c], da_all[c]], axis=1),
                      jnp.concatenate([st_ref[c], kd_c[c]], axis=0)) for c in order}
        dkd = {c: _mm_tn(da_all[c], qd_stack[c]) for c in order}
        upd = {c: _mm_tn(do_rows[c], qd_stack[c]) for c in order}
        dqd = {c: _head_diagonal(dqd[c], key_head) for c in order}
        return z, tile, {c: dict(dv=dv[c], dqd=dqd[c], dkd=dkd[c], upd=upd[c]) for c in order}

    def scan(tile, per):
        dst = dstate[...]
        dsts = {}
        for c in order:
            dsts[c] = dst
            dst = dst * tile[0][c] + per[c]["upd"]
        dstate[...] = dst
        return dsts

    def inter(z, tile, per, dsts):
        ebl, eb, enb, qd, kd, ke = tile
        ke_stack = {c: _stack_heads(ke[_chunk_rows(c)], key_head) for c in order}
        v_rows = {c: _rows_by_head(v_ref[_chunk_rows(c), :].astype(BF16)) for c in order}
        dst_b = {c: dsts[c].astype(BF16) for c in order}
        dv_state = {c: _mm_nt(ke_stack[c], dst_b[c]) for c in order}
        dke_c = {c: _mm(v_rows[c], dst_b[c]) for c in order}
        dke_c = {c: _head_diagonal(dke_c[c], key_head) for c in order}
        dbl_c = {}
        for c in order:
            rows = _chunk_rows(c)
            dv_ref[rows, :] = (per[c]["dv"] + _lanes_by_head(dv_state[c])).astype(BF16)
            dbl_c[c] = (jnp.sum(dsts[c] * st_ref[c].astype(F32), axis=0, keepdims=True) * ebl[c]
                        + jnp.sum(dke_c[c] * ke[rows], axis=0, keepdims=True))
        tile_of = lambda parts: jnp.concatenate([parts[c] for c in range(nc)], axis=0)
        dqd, dkd = tile_of({c: per[c]["dqd"] for c in order}), tile_of({c: per[c]["dkd"] for c in order})
        dke = tile_of(dke_c)
        dke_end = tile_of({c: dke_c[c] * ebl[c] for c in order})
        dq_ref[...] = (dqd * eb * Q_SCALE).astype(BF16)
        dk_ref[...] = ((dkd + dke_end) * enb).astype(BF16)
        db = dqd * qd - dkd * kd - dke * ke
        dla = _chunk_cumsum(_tri(not rev), db, [dbl_c[c] for c in range(nc)])
        dz = dla * (_sigmoid(-z) * (1.0 / GLA_TAU))
        dlr_ref[...] = _mm_nt(dz, wd_ref[...]).astype(BF16)
        dwd_ref[...] += _mm_tn(lr_ref[...], dz)
        dbd_ref[...] += jnp.sum(dz, axis=0, keepdims=True)

    return intra, scan, inter


def _gla_bwd(p, wd_pad_f, bd_f, wd_pad_b, bd_b, st_f, st_b, d_o, tg, comms=()):
    t = p.shape[0]
    nt = t // tg
    nc = tg // GLA_CHUNK
    up, down = (lambda i: i), (lambda i: nt - 1 - i)

    def body(qf, kf, vf, lrf, stf, dof, qb, kb, vb, lrb, stb, dob, wdf, bdf, wdb, bdb,
             dqf, dkf, dvf, dlrf, dwdf, dbdf, dqb, dkb, dvb, dlrb, dwdb, dbdb, dstate_f, dstate_b):
        @pl.when(pl.program_id(0) == 0)
        def _():
            for ref in (dstate_f, dstate_b, dwdf, dbdf, dwdb, dbdb):
                ref[...] = jnp.zeros_like(ref)

        dirs = [_gla_bwd_dir(False, nc, qf, kf, vf, lrf, wdf, bdf, stf, dof, dqf, dkf, dvf, dlrf, dwdf, dbdf,
                             dstate_f),
                _gla_bwd_dir(True, nc, qb, kb, vb, lrb, wdb, bdb, stb, dob, dqb, dkb, dvb, dlrb, dwdb, dbdb,
                             dstate_b)]
        first = [intra() for intra, _, _ in dirs]
        dsts = [scan(tile, per) for (_, scan, _), (_, tile, per) in zip(dirs, first)]
        for (_, _, inter), (z, tile, per), d in zip(dirs, first, dsts):
            inter(z, tile, per, d)

    wd_spec, bd_spec = _const_spec((LANE, KEY_W)), _const_spec((1, KEY_W))
    ins = lambda tile: _p_specs(tg, tile) + [pl.BlockSpec((nc, GLA_DV, KEY_W), lambda i: (tile(i), 0, 0)),
                                             pl.BlockSpec((tg, VAL_W), lambda i: (tile(i), 0))]
    outs = lambda tile: (pl.BlockSpec((tg, KEY_W), lambda i: (tile(i), 0)),
                         pl.BlockSpec((tg, KEY_W), lambda i: (tile(i), 0)),
                         pl.BlockSpec((tg, VAL_W), lambda i: (tile(i), 0)),
                         pl.BlockSpec((tg, LANE), lambda i: (tile(i), 0)),
                         _acc_spec((LANE, KEY_W)), _acc_spec((1, KEY_W)))
    out_shape = (jax.ShapeDtypeStruct((t, KEY_W), BF16), jax.ShapeDtypeStruct((t, KEY_W), BF16),
                 jax.ShapeDtypeStruct((t, VAL_W), BF16), jax.ShapeDtypeStruct((t, LANE), BF16),
                 jax.ShapeDtypeStruct((LANE, KEY_W), F32), jax.ShapeDtypeStruct((1, KEY_W), F32))
    scratch = [pltpu.VMEM((GLA_DV, KEY_W), F32)]
    return _fused_call(
        body, comms, name="gla_bwd", grid=(nt,),
        inputs=(p, p, p, p, st_f, d_o, p, p, p, p, st_b, d_o, wd_pad_f, bd_f, wd_pad_b, bd_b),
        in_specs=ins(down) + ins(up) + [wd_spec, bd_spec, wd_spec, bd_spec],
        out_specs=outs(down) + outs(up), out_shape=out_shape * 2, scratch_shapes=scratch * 2)


def _head_rms(o):
    parts, scales = [], []
    for h in range(GLA_HEADS):
        oh = o[:, h * GLA_DV:(h + 1) * GLA_DV]
        r = lax.rsqrt(jnp.mean(oh * oh, axis=-1, keepdims=True) + EPS)
        parts.append(oh * r)
        scales.append(jnp.broadcast_to(r, oh.shape))
    return jnp.concatenate(parts, axis=1), jnp.concatenate(scales, axis=1)


def _layernorm_stats(zv):
    mu = jnp.mean(zv, axis=-1, keepdims=True)
    xc = zv - mu
    rs = lax.rsqrt(jnp.mean(xc * xc, axis=-1, keepdims=True) + EPS)
    return xc * rs, rs


def _mix_fwd(x, o_f, o_b, p, gla_g, ln_g, ln_b, w_sp, b_sp, w_out, tm, comms=()):
    t = x.shape[0]
    nch = tm // GMLP_CHUNK

    def body(x_ref, of_ref, ob_ref, pg_ref, pu_ref, pv_ref, gg_ref, lg_ref, lb_ref, ws_ref, bs_ref, wo_ref,
             x1_ref, y_ref, s_scr):
        on, _ = _head_rms(of_ref[...] + ob_ref[...])
        pg = pg_ref[...].astype(F32)
        y_a = on * gg_ref[...] * (pg * _sigmoid(pg))
        zu = _gelu(pu_ref[...].astype(F32))
        vhat, _ = _layernorm_stats(_gelu(pv_ref[...].astype(F32)))
        vln = (vhat * lg_ref[...] + lb_ref[...]).astype(BF16)
        for g in range(GMLP_GROUPS):
            w_g = ws_ref[g].astype(BF16)
            b_g = bs_ref[g]
            cols = slice(g * LANE, (g + 1) * LANE)
            for n in range(nch):
                rows = slice(n * GMLP_CHUNK, (n + 1) * GMLP_CHUNK)
                s_scr[rows, cols] = jnp.dot(w_g, vln[rows, cols], preferred_element_type=F32) + b_g
        ycat = jnp.concatenate([y_a, zu * s_scr[...]], axis=1).astype(BF16)
        y_ref[...] = ycat
        x1_ref[...] = x_ref[...] + jnp.dot(ycat, wo_ref[...], preferred_element_type=F32)

    half = lambda j: pl.BlockSpec((tm, VAL_W), lambda i: (i, j))
    return _fused_call(
        body, comms, name="mix_fwd", grid=(t // tm,),
        inputs=(x, o_f, o_b, p, p, p, gla_g, ln_g, ln_b, w_sp, b_sp, w_out),
        in_specs=[pl.BlockSpec((tm, D_MODEL), lambda i: (i, 0)), half(0), half(0), half(2), half(3), half(4),
                  _const_spec((1, VAL_W)), _const_spec((1, GMLP_W)), _const_spec((1, GMLP_W)),
                  _const_spec((GMLP_GROUPS, GMLP_CHUNK, GMLP_CHUNK)), _const_spec((GMLP_GROUPS, GMLP_CHUNK, 1)),
                  _const_spec((D_MODEL, D_MODEL))],
        out_specs=(pl.BlockSpec((tm, D_MODEL), lambda i: (i, 0)), pl.BlockSpec((tm, D_MODEL), lambda i: (i, 0))),
        out_shape=(jax.ShapeDtypeStruct((t, D_MODEL), F32), jax.ShapeDtypeStruct((t, D_MODEL), BF16)),
        scratch_shapes=[pltpu.VMEM((tm, GMLP_W), F32)])


def _mix_bwd(dx1, ycat, o_f, o_b, p, gla_g, ln_g, ln_b, w_sp, b_sp, w_out, tm, comms=()):
    t = dx1.shape[0]
    nch = tm // GMLP_CHUNK

    def body(dx1_ref, y_ref, of_ref, ob_ref, pg_ref, pu_ref, pv_ref, gg_ref, lg_ref, lb_ref, ws_ref, bs_ref, wo_ref,
             do_ref, dpg_ref, dpu_ref, dpv_ref, dwo_ref, dgg_ref, dlg_ref, dlb_ref, dws_ref, dbs_ref,
             s_scr, dvln_scr):
        @pl.when(pl.program_id(0) == 0)
        def _():
            for ref in (dwo_ref, dgg_ref, dlg_ref, dlb_ref, dws_ref, dbs_ref):
                ref[...] = jnp.zeros_like(ref)

        dx1 = dx1_ref[...].astype(BF16)
        dycat = _mm_nt(dx1, wo_ref[...])
        dwo_ref[...] += _mm_tn(y_ref[...], dx1)
        dy_a = dycat[:, :VAL_W]
        dy_b = dycat[:, VAL_W:]
        on, r = _head_rms(of_ref[...] + ob_ref[...])
        pg = pg_ref[...].astype(F32)
        sil, dsil = _silu_and_grad(pg)
        gg = gg_ref[...]
        dgg_ref[...] += jnp.sum(dy_a * sil * on, axis=0, keepdims=True)
        don = dy_a * sil * gg
        prod = don * on
        means = jnp.concatenate(
            [jnp.broadcast_to(jnp.mean(prod[:, h * GLA_DV:(h + 1) * GLA_DV], axis=-1, keepdims=True),
                              (tm, GLA_DV)) for h in range(GLA_HEADS)], axis=1)
        do_ref[...] = (r * (don - on * means)).astype(BF16)
        dpg_ref[...] = (dy_a * on * gg * dsil).astype(BF16)
        pu = pu_ref[...].astype(F32)
        pv = pv_ref[...].astype(F32)
        zu, dzu_dpu = _gelu_and_grad(pu)
        zv, dzv_dpv = _gelu_and_grad(pv)
        vhat, rs = _layernorm_stats(zv)
        lg = lg_ref[...]
        vln = (vhat * lg + lb_ref[...]).astype(BF16)
        ds32 = dy_b * zu
        ds = ds32.astype(BF16)
        blocks = [(g, n) for g in range(GMLP_GROUPS) for n in range(nch)]
        at = lambda g, n: (slice(n * GMLP_CHUNK, (n + 1) * GMLP_CHUNK), slice(g * LANE, (g + 1) * LANE))
        w_sp = [ws_ref[g].astype(BF16) for g in range(GMLP_GROUPS)]
        v_blk = {b: vln[at(*b)] for b in blocks}
        ds_blk = {b: ds[at(*b)] for b in blocks}
        s_blk = {b: jnp.dot(w_sp[b[0]], v_blk[b], preferred_element_type=F32) for b in blocks}
        dw_blk = {b: _mm_nt(ds_blk[b], v_blk[b]) for b in blocks}
        dvln_blk = {b: _mm_tn(w_sp[b[0]], ds_blk[b]) for b in blocks}
        for b in blocks:
            s_scr[at(*b)] = s_blk[b] + bs_ref[b[0]]
            dvln_scr[at(*b)] = dvln_blk[b]
        for g in range(GMLP_GROUPS):
            dws_ref[g] += sum(dw_blk[(g, n)] for n in range(nch))
            dbs_ref[g] += sum(jnp.sum(ds32[at(g, n)], axis=-1, keepdims=True) for n in range(nch))
        dpu_ref[...] = (dy_b * s_scr[...] * dzu_dpu).astype(BF16)
        dvln = dvln_scr[...]
        dlg_ref[...] += jnp.sum(dvln * vhat, axis=0, keepdims=True)
        dlb_ref[...] += jnp.sum(dvln, axis=0, keepdims=True)
        dvhat = dvln * lg
        dzv = rs * (dvhat - jnp.mean(dvhat, axis=-1, keepdims=True)
                    - vhat * jnp.mean(dvhat * vhat, axis=-1, keepdims=True))
        dpv_ref[...] = (dzv * dzv_dpv).astype(BF16)

    half = lambda j: pl.BlockSpec((tm, VAL_W), lambda i: (i, j))
    full = pl.BlockSpec((tm, D_MODEL), lambda i: (i, 0))
    sp_shape = (GMLP_GROUPS, GMLP_CHUNK, GMLP_CHUNK)
    bs_shape = (GMLP_GROUPS, GMLP_CHUNK, 1)
    return _fused_call(
        body, comms, name="mix_bwd", grid=(t // tm,),
        inputs=(dx1, ycat, o_f, o_b, p, p, p, gla_g, ln_g, ln_b, w_sp, b_sp, w_out),
        in_specs=[full, full, half(0), half(0), half(2), half(3), half(4),
                  _const_spec((1, VAL_W)), _const_spec((1, GMLP_W)), _const_spec((1, GMLP_W)),
                  _const_spec(sp_shape), _const_spec(bs_shape), _const_spec((D_MODEL, D_MODEL))],
        out_specs=(half(0), half(0), half(0), half(0), _acc_spec((D_MODEL, D_MODEL)), _acc_spec((1, VAL_W)),
                   _acc_spec((1, GMLP_W)), _acc_spec((1, GMLP_W)), _acc_spec(sp_shape), _acc_spec(bs_shape)),
        out_shape=(jax.ShapeDtypeStruct((t, VAL_W), BF16),) * 4 + (
            jax.ShapeDtypeStruct((D_MODEL, D_MODEL), F32), jax.ShapeDtypeStruct((1, VAL_W), F32),
            jax.ShapeDtypeStruct((1, GMLP_W), F32), jax.ShapeDtypeStruct((1, GMLP_W), F32),
            jax.ShapeDtypeStruct(sp_shape, F32), jax.ShapeDtypeStruct(bs_shape, F32)),
        scratch_shapes=[pltpu.VMEM((tm, GMLP_W), F32), pltpu.VMEM((tm, GMLP_W), F32)])


def _rms_bwd(dy_scaled, xn, r):
    return r * (dy_scaled - xn * jnp.mean(dy_scaled * xn, axis=-1, keepdims=True))


def _ffn(x1, target, g2, gf, w_gate, w_up, w_down, tm):
    t = x1.shape[0]

    def body(x1_ref, tg_ref, g2_ref, gf_ref, wg_ref, wu_ref, wd_ref,
             dx1_ref, h2_ref, dgate_ref, dup_ref, act_ref, dx2_ref, loss_ref, dgf_ref, dg2_ref):
        @pl.when(pl.program_id(0) == 0)
        def _():
            for ref in (loss_ref, dgf_ref, dg2_ref):
                ref[...] = jnp.zeros_like(ref)

        x1v = x1_ref[...]
        g2v = g2_ref[...]
        gfv = gf_ref[...]
        r2 = lax.rsqrt(jnp.mean(x1v * x1v, axis=-1, keepdims=True) + EPS)
        xn1 = x1v * r2
        h2 = (xn1 * g2v).astype(BF16)
        h2_ref[...] = h2
        gate = _mm_nt(h2, wg_ref[...])
        up = _mm_nt(h2, wu_ref[...])
        sil, dsil = _silu_and_grad(gate)
        act = (sil * up).astype(BF16)
        act_ref[...] = act
        x2 = x1v + jnp.dot(act, wd_ref[...], preferred_element_type=F32)
        rf = lax.rsqrt(jnp.mean(x2 * x2, axis=-1, keepdims=True) + EPS)
        xn2 = x2 * rf
        err = xn2 * gfv - tg_ref[...]
        loss_ref[...] += 0.5 * jnp.sum(jnp.mean(err * err, axis=-1, keepdims=True))
        dy = err * (1.0 / D_MODEL)
        dgf_ref[...] += jnp.sum(dy * xn2, axis=0, keepdims=True)
        dx2 = _rms_bwd(dy * gfv, xn2, rf)
        dx2b = dx2.astype(BF16)
        dx2_ref[...] = dx2b
        dact = _mm_nt(dx2b, wd_ref[...])
        dgate = (dact * up * dsil).astype(BF16)
        dup = (dact * sil).astype(BF16)
        dgate_ref[...] = dgate
        dup_ref[...] = dup
        dh2 = _mm(dgate, wg_ref[...]) + _mm(dup, wu_ref[...])
        dg2_ref[...] += jnp.sum(dh2 * xn1, axis=0, keepdims=True)
        dx1_ref[...] = dx2 + _rms_bwd(dh2 * g2v, xn1, r2)

    row = lambda w: pl.BlockSpec((tm, w), lambda i: (i, 0))
    return pl.pallas_call(
        body, name="ffn_fwd_bwd", grid=(t // tm,),
        in_specs=[row(D_MODEL), row(D_MODEL), _const_spec((1, D_MODEL)), _const_spec((1, D_MODEL)),
                  _const_spec((D_FF, D_MODEL)), _const_spec((D_FF, D_MODEL)), _const_spec((D_FF, D_MODEL))],
        out_specs=(row(D_MODEL), row(D_MODEL), row(D_FF), row(D_FF), row(D_FF), row(D_MODEL),
                   _acc_spec((8, LANE)), _acc_spec((1, D_MODEL)), _acc_spec((1, D_MODEL))),
        out_shape=(jax.ShapeDtypeStruct((t, D_MODEL), F32), jax.ShapeDtypeStruct((t, D_MODEL), BF16),
                   jax.ShapeDtypeStruct((t, D_FF), BF16), jax.ShapeDtypeStruct((t, D_FF), BF16),
                   jax.ShapeDtypeStruct((t, D_FF), BF16), jax.ShapeDtypeStruct((t, D_MODEL), BF16),
                   jax.ShapeDtypeStruct((8, LANE), F32), jax.ShapeDtypeStruct((1, D_MODEL), F32),
                   jax.ShapeDtypeStruct((1, D_MODEL), F32)),
        compiler_params=_params(),
    )(x1, target, g2, gf, w_gate, w_up, w_down)


def _matmul_tn(a, b, tm, tk, name, comms=()):
    t, m = a.shape
    n = b.shape[1]

    def body(a_ref, b_ref, o_ref):
        @pl.when(pl.program_id(1) == 0)
        def _():
            o_ref[...] = jnp.zeros_like(o_ref)

        o_ref[...] += _mm_tn(a_ref[...], b_ref[...])

    (out,), comm_results = _fused_call(
        body, comms, name=name, grid=(m // tm, t // tk), inputs=(a, b),
        in_specs=[pl.BlockSpec((tk, tm), lambda j, k: (k, j)), pl.BlockSpec((tk, n), lambda j, k: (k, 0))],
        out_specs=(pl.BlockSpec((tm, n), lambda j, k: (j, 0)),),
        out_shape=(jax.ShapeDtypeStruct((m, n), F32),))
    return out, comm_results


def _in_proj_bwd(x, g1, dx1, dq_f, dq_b, dk_f, dk_b, dv_f, dv_b, dpg, dpu, dpv, dlr_f, dlr_b, w_main, tm, comms=()):
    t = x.shape[0]

    def body(x_ref, g_ref, dx1_ref, dqf, dqb, dkf, dkb, dvf, dvb, dg, du, dv, dlf, dlb, w_ref,
             dx_ref, dp_ref, dg1_ref):
        @pl.when(pl.program_id(0) == 0)
        def _():
            dg1_ref[...] = jnp.zeros_like(dg1_ref)

        both = lambda a, b: (a[...].astype(F32) + b[...].astype(F32)).astype(BF16)
        dp = jnp.concatenate([both(dqf, dqb), both(dkf, dkb), both(dvf, dvb), dg[...], du[...], dv[...],
                              both(dlf, dlb)], axis=1)
        dp_ref[...] = dp
        dh = sum(_mm(dp[:, c0:c0 + r1 - r0], w_ref[r0:r1, :]) for r0, r1, c0 in PROJ_ROWS)
        xv = x_ref[...]
        r = lax.rsqrt(jnp.mean(xv * xv, axis=-1, keepdims=True) + EPS)
        xn = xv * r
        dg1_ref[...] += jnp.sum(dh * xn, axis=0, keepdims=True)
        dx_ref[...] = dx1_ref[...] + _rms_bwd(dh * g_ref[...], xn, r)

    row = lambda w: pl.BlockSpec((tm, w), lambda i: (i, 0))
    return _fused_call(
        body, comms, name="in_proj_bwd", grid=(t // tm,),
        inputs=(x, g1, dx1, dq_f, dq_b, dk_f, dk_b, dv_f, dv_b, dpg, dpu, dpv, dlr_f, dlr_b, w_main),
        in_specs=[row(D_MODEL), _const_spec((1, D_MODEL)), row(D_MODEL), row(KEY_W), row(KEY_W), row(KEY_W),
                  row(KEY_W), row(VAL_W), row(VAL_W), row(VAL_W), row(VAL_W), row(VAL_W), row(LANE), row(LANE),
                  _const_spec((PROJ_W, D_MODEL))],
        out_specs=(row(D_MODEL), row(PROJ_PAD), _acc_spec((1, D_MODEL))),
        out_shape=(jax.ShapeDtypeStruct((t, D_MODEL), F32), jax.ShapeDtypeStruct((t, PROJ_PAD), BF16),
                   jax.ShapeDtypeStruct((1, D_MODEL), F32)))


def _adamw(w, g, m, v):
    m_new = ADAM_B1 * m + (1.0 - ADAM_B1) * g
    v_new = ADAM_B2 * v + (1.0 - ADAM_B2) * (g * g)
    m_hat = m_new / (1.0 - ADAM_B1 ** ADAM_STEP)
    v_hat = v_new / (1.0 - ADAM_B2 ** ADAM_STEP)
    delta = -ADAM_LR * (m_hat / (jnp.sqrt(v_hat) + ADAM_EPS) + ADAM_WD * w)
    return delta, m_new, v_new


def _adamw_window(own, recv, w, m, v, name):
    r, c = w.shape
    rows = own.shape[0]

    def body(own_ref, recv_ref, w_ref, m_ref, v_ref, g_ref, d_ref, nm_ref, nv_ref):
        g = own_ref[...]
        for k in range(3):
            g = g + recv_ref[k].astype(F32)

        def update(g):
            g_ref[...] = g[:r]
            d_ref[...], nm_ref[...], nv_ref[...] = _adamw(w_ref[...], g[:r], m_ref[...], v_ref[...])

        core = lax.axis_index("c")
        pl.when(core == 0)(lambda: update(g))
        pl.when(core == 1)(lambda: update(pltpu.roll(g, rows - 4, 0)))

    whole = lambda *shape: pl.BlockSpec(shape, lambda i: (0,) * len(shape))
    return pl.pallas_call(
        body, name=name, grid=(1,),
        in_specs=[whole(rows, c), whole(3, rows, c), whole(r, c), whole(r, c), whole(r, c)],
        out_specs=(whole(r, c),) * 4, out_shape=(jax.ShapeDtypeStruct((r, c), F32),) * 4,
        compiler_params=_params(),
    )(own, recv, w, m, v)


def _adamw_shard(own, recv, w, m, v, tr, name):
    r, c = w.shape

    def body(own_ref, recv_ref, w_ref, m_ref, v_ref, g_ref, d_ref, nm_ref, nv_ref):
        g = own_ref[...]
        for k in range(3):
            g = g + recv_ref[k].astype(F32)
        g_ref[...] = g
        d_ref[...], nm_ref[...], nv_ref[...] = _adamw(w_ref[...], g, m_ref[...], v_ref[...])

    row = pl.BlockSpec((tr, c), lambda i: (i, 0))
    return pl.pallas_call(
        body, name=name, grid=(r // tr,),
        in_specs=[row, pl.BlockSpec((3, tr, c), lambda i: (0, i, 0)), row, row, row],
        out_specs=(row,) * 4, out_shape=(jax.ShapeDtypeStruct((r, c), F32),) * 4,
        compiler_params=_params(),
    )(own, recv, w, m, v)


def _adamw_small(entries):
    stacks = []
    for (g, _, _), _, _, _ in entries:
        if not any(g is s for s in stacks):
            stacks.append(g)
    where = [next(i for i, s in enumerate(stacks) if s is g) for (g, _, _), _, _, _ in entries]
    ns, ne = len(stacks), len(entries)

    def body(*refs):
        s_refs, wmv, outs = refs[:ns], refs[ns:ns + 3 * ne], refs[ns + 3 * ne:]
        for e, ((_, r0, nr), _, _, _) in enumerate(entries):
            grad = s_refs[where[e]][r0:r0 + nr, :]
            w_ref, m_ref, v_ref = wmv[3 * e:3 * e + 3]
            g_ref, d_ref, nm_ref, nv_ref = outs[4 * e:4 * e + 4]
            g_ref[...] = grad
            d_ref[...], nm_ref[...], nv_ref[...] = _adamw(w_ref[...], grad, m_ref[...], v_ref[...])

    results = pl.pallas_call(
        body, name="adamw_small",
        out_shape=tuple(jax.ShapeDtypeStruct(w.shape, F32) for _, w, _, _ in entries for _ in range(4)),
        compiler_params=pltpu.CompilerParams(vmem_limit_bytes=VMEM_LIMIT),
    )(*stacks, *[a for _, w, m, v in entries for a in (w, m, v)])
    return [results[4 * e:4 * e + 4] for e in range(ne)]


def _mesh_pos():
    return lax.axis_index("x"), lax.axis_index("y"), lax.axis_index("c")


def _other_chips(x, y):
    return [(x, 1 - y), (1 - x, y), (1 - x, 1 - y)]


_VMEM_WHOLE = pl.BlockSpec(memory_space=pltpu.VMEM)
_HBM_WHOLE = pl.BlockSpec(memory_space=pl.ANY)


def _gather_comm(shards, cast, mid=((1, 2), (3, 4))):
    na = len(shards)
    staged = [a for a in range(na) if cast[a]]

    def phases(in_refs, out_refs, scr):
        stage = dict(zip(staged, scr[:len(staged)]))
        send_sems, recv_sems, local_sems = scr[len(staged):]
        x, y, c = _mesh_pos()
        me, sibling = (x, y, c), (x, y, 1 - c)
        chip_a, chip_b, diagonal = (x ^ c, y ^ (1 - c)), (x ^ (1 - c), y ^ c), (1 - x, 1 - y)
        srcs = [stage[a] if cast[a] else in_refs[a] for a in range(na)]

        def rows(a, pos):
            px, py, pc = pos
            return out_refs[a].at[4 * px + 2 * py + pc]

        def copy(a, k, block, to, src=None):
            return pltpu.make_async_remote_copy(
                src_ref=rows(a, block) if src is None else src, dst_ref=rows(a, block),
                send_sem=send_sems.at[a, k], recv_sem=recv_sems.at[a, k], device_id=to, device_id_type=MESH_ID)

        mine = [pltpu.make_async_copy(srcs[a], rows(a, me), local_sems.at[a]) for a in range(na)]
        own = [copy(a, k, me, to, src=srcs[a]) for a in range(na)
               for k, to in ((0, sibling), (1, (*chip_a, c)), (2, (*chip_b, c)))]
        onward = [copy(a, 3, (*chip_a, c), (*chip_b, c)) for a in range(na)]
        to_sibling = {k: [copy(a, k, (*chip, c), sibling) for a in range(na)]
                      for k, chip in ((4, chip_a), (5, chip_b), (6, diagonal))}

        def start():
            for a in staged:
                stage[a][...] = in_refs[a][...].astype(BF16)
            for cp in mine + own:
                cp.start()

        def forward_neighbours():
            for a in range(na):
                copy(a, 1, (*chip_a, c), me).wait_recv()
                onward[a].start()
                to_sibling[4][a].start()
            for a in range(na):
                copy(a, 2, (*chip_b, c), me).wait_recv()
                to_sibling[5][a].start()

        def forward_diagonal():
            for a in range(na):
                copy(a, 3, (*diagonal, c), me).wait_recv()
                to_sibling[6][a].start()

        def finish():
            for a in range(na):
                for k, chip in ((0, (x, y)), (4, chip_b), (5, chip_a), (6, diagonal)):
                    copy(a, k, (*chip, 1 - c), me).wait_recv()
            for cp in own + onward + to_sibling[4] + to_sibling[5] + to_sibling[6]:
                cp.wait_send()
            for cp in mine:
                cp.wait()

        return start, forward_neighbours, forward_diagonal, finish

    def before(step, nsteps, in_refs, out_refs, scr):
        start, forward_neighbours, forward_diagonal, _ = phases(in_refs, out_refs, scr)
        pl.when(step == 0)(start)
        pl.when(step == nsteps * mid[0][0] // mid[0][1])(forward_neighbours)
        pl.when(step == nsteps * mid[1][0] // mid[1][1])(forward_diagonal)

    def after(step, nsteps, in_refs, out_refs, scr):
        pl.when(step == nsteps - 1)(phases(in_refs, out_refs, scr)[3])

    return _Comm(
        inputs=list(shards), in_specs=[_VMEM_WHOLE] * na,
        out_shape=[jax.ShapeDtypeStruct((N_DEV,) + s.shape, BF16 if cast[a] else s.dtype)
                   for a, s in enumerate(shards)],
        out_specs=[_HBM_WHOLE] * na,
        scratch_shapes=[pltpu.VMEM(shards[a].shape, BF16) for a in staged] + [
            pltpu.SemaphoreType.DMA((na, 7)), pltpu.SemaphoreType.DMA((na, 7)), pltpu.SemaphoreType.DMA((na,))],
        before=before, after=after)


W_IN_WINDOW = 336


def _w_in_block_pieces(g_ref, chip, core):
    j = 2 * chip + core
    rows = PROJ_W // N_DEV
    first = rows * j - jnp.where(j > 4, 2 * LOWRANK, 0)
    start = pl.multiple_of((first >> 3) << 3, 8)
    head = LR_REF - 4 * rows
    split = [(g_ref.at[pl.ds(4 * rows, head)], 0, head), (g_ref.at[pl.ds(LR_COL, 2 * LOWRANK)], head, 2 * LOWRANK),
             (g_ref.at[pl.ds(LR_REF, 64)], head + 2 * LOWRANK, 64)]
    return [(j != 4, [(g_ref.at[pl.ds(start, W_IN_WINDOW)], 0, W_IN_WINDOW)]), (j == 4, split)]


def _reduce_scatter_comm(grads, rows=None, pieces=None):
    if pieces is None:
        _, _, r, c = grads.shape
        pieces = lambda g_ref, chip, core: [(None, [(g_ref.at[chip, core], 0, r)])]
    else:
        r, c = rows, grads.shape[1]
    order = (3, 1, 2, 0)

    def transfer(k, kind, in_refs, scr, act):
        (g_ref,), (sib, own, _, sems) = in_refs, scr
        x, y, core = _mesh_pos()
        chip = (2 * x + y) ^ k
        for cond, parts in pieces(g_ref, chip, 1 - core if kind == "send" else core):
            def run(parts=parts):
                for i, (src, row0, n) in enumerate(parts):
                    if kind == "local":
                        act(pltpu.make_async_copy(src, own.at[k, pl.ds(row0, n)], sems.at[2, 3 * k + i]))
                    else:
                        act(pltpu.make_async_remote_copy(
                            src_ref=src, dst_ref=sib.at[k, pl.ds(row0, n)], send_sem=sems.at[0, 3 * k + i],
                            recv_sem=sems.at[1, 3 * k + i], device_id=(x, y, 1 - core), device_id_type=MESH_ID))

            run() if cond is None else pl.when(cond)(run)

    def chip_copies(out_refs, scr):
        (_, recv), (_, _, part, sems) = out_refs, scr
        x, y, core = _mesh_pos()
        return [pltpu.make_async_remote_copy(
            src_ref=part.at[j], dst_ref=recv.at[j], send_sem=sems.at[3, j], recv_sem=sems.at[4, j],
            device_id=(*chip, core), device_id_type=MESH_ID) for j, chip in enumerate(_other_chips(x, y))]

    def before(step, nsteps, in_refs, out_refs, scr):
        @pl.when(step == 0)
        def _():
            for k in order:
                transfer(k, "send", in_refs, scr, lambda cp: cp.start())
                transfer(k, "local", in_refs, scr, lambda cp: cp.start())

    def after(step, nsteps, in_refs, out_refs, scr):
        sib, own, part, _ = scr

        @pl.when(step == (nsteps - 1) // 2)
        def _():
            to_chips = chip_copies(out_refs, scr)
            for k in order:
                transfer(k, "recv", in_refs, scr, lambda cp: cp.wait_recv())
                transfer(k, "local", in_refs, scr, lambda cp: cp.wait())
                if k:
                    part[k - 1] = (own[k] + sib[k]).astype(BF16)
                    to_chips[k - 1].start()
                else:
                    out_refs[0][...] = own[0] + sib[0]
            for k in order:
                transfer(k, "send", in_refs, scr, lambda cp: cp.wait_send())

        @pl.when(step == nsteps - 1)
        def _():
            for cp in chip_copies(out_refs, scr):
                cp.wait()

    return _Comm(inputs=[grads], in_specs=[_HBM_WHOLE],
                 out_shape=[jax.ShapeDtypeStruct((r, c), F32), jax.ShapeDtypeStruct((3, r, c), BF16)],
                 out_specs=[_VMEM_WHOLE, _HBM_WHOLE],
                 scratch_shapes=[pltpu.VMEM((4, r, c), F32), pltpu.VMEM((4, r, c), F32), pltpu.VMEM((3, r, c), BF16),
                                 pltpu.SemaphoreType.DMA((5, 12))],
                 before=before, after=after)


def _exchange_comm(arrays, out_shape, make_copies):
    na = len(arrays)

    def copies(in_refs, out_refs, scr):
        return make_copies(in_refs, out_refs, *scr)

    def before(step, nsteps, in_refs, out_refs, scr):
        @pl.when(step == 0)
        def _():
            for cp in copies(in_refs, out_refs, scr):
                cp.start()

    def after(step, nsteps, in_refs, out_refs, scr):
        @pl.when(step == nsteps - 1)
        def _():
            for cp in copies(in_refs, out_refs, scr):
                cp.wait()

    return _Comm(inputs=list(arrays), in_specs=[_HBM_WHOLE] * na, out_shape=list(out_shape),
                 out_specs=[_HBM_WHOLE] * na,
                 scratch_shapes=[pltpu.SemaphoreType.DMA((na, 3)), pltpu.SemaphoreType.DMA((na, 3))],
                 before=before, after=after)


def _sibling_exchange_comm(grads):
    def make_copies(in_refs, out_refs, send_sems, recv_sems):
        x, y, c = _mesh_pos()
        return [pltpu.make_async_remote_copy(
            src_ref=in_refs[a].at[:, pl.ds(1 - c, 1)], dst_ref=out_refs[a], send_sem=send_sems.at[a, 0],
            recv_sem=recv_sems.at[a, 0], device_id=(x, y, 1 - c), device_id_type=MESH_ID)
            for a in range(len(grads))]

    return _exchange_comm(grads, [jax.ShapeDtypeStruct((4, 1) + g.shape[2:], F32) for g in grads], make_copies)


def _chip_sum(my_pos, mine, from_sibling, tr, name):
    _, _, r, c = mine.shape

    def body(pos_ref, a_ref, b_ref, own_ref, out_ref):
        s = a_ref[0, 0] + b_ref[0, 0]

        @pl.when(pl.program_id(1) == 0)
        def _():
            own_ref[...] = s

        @pl.when(pl.program_id(1) > 0)
        def _():
            out_ref[0] = s.astype(BF16)

    grid_spec = pltpu.PrefetchScalarGridSpec(
        num_scalar_prefetch=1, grid=(r // tr, 4),
        in_specs=[pl.BlockSpec((1, 1, tr, c), lambda i, k, pos: (pos[0] ^ k, pos[1], i, 0)),
                  pl.BlockSpec((1, 1, tr, c), lambda i, k, pos: (pos[0] ^ k, 0, i, 0))],
        out_specs=(pl.BlockSpec((tr, c), lambda i, k, pos: (i, 0)),
                   pl.BlockSpec((1, tr, c), lambda i, k, pos: (jnp.maximum(k - 1, 0), i, 0))))
    return pl.pallas_call(
        body, name=name, grid_spec=grid_spec,
        out_shape=(jax.ShapeDtypeStruct((r, c), F32), jax.ShapeDtypeStruct((3, r, c), BF16)),
        compiler_params=_params(2),
    )(my_pos, mine, from_sibling)


def _chips_exchange_comm(partials):
    def make_copies(in_refs, out_refs, send_sems, recv_sems):
        x, y, c = _mesh_pos()
        return [pltpu.make_async_remote_copy(
            src_ref=in_refs[a].at[j], dst_ref=out_refs[a].at[j], send_sem=send_sems.at[a, j],
            recv_sem=recv_sems.at[a, j], device_id=(*chip, c), device_id_type=MESH_ID)
            for a in range(len(partials)) for j, chip in enumerate(_other_chips(x, y))]

    return _exchange_comm(partials, [jax.ShapeDtypeStruct(g.shape, BF16) for g in partials], make_copies)


def _comm_only(comms, name):
    return _fused_call(lambda: None, comms, name=name, grid=(1,), inputs=(), in_specs=[], out_specs=(),
                       out_shape=())[1]


def _all_reduce_small_comm(parts):
    na = len(parts)

    def copies(in_refs, scr):
        gathered, (send_sems, recv_sems) = scr[:na], scr[na:]
        x, y, c = _mesh_pos()
        my_id = 4 * x + 2 * y + c
        return my_id, [pltpu.make_async_remote_copy(
            src_ref=in_refs[a], dst_ref=gathered[a].at[my_id], send_sem=send_sems.at[a, k - 1],
            recv_sem=recv_sems.at[a, k - 1], device_id=(x ^ (k >> 2), y ^ ((k >> 1) & 1), c ^ (k & 1)),
            device_id_type=MESH_ID) for a in range(na) for k in range(1, N_DEV)]

    def before(step, nsteps, in_refs, out_refs, scr):
        @pl.when(step == 0)
        def _():
            for cp in copies(in_refs, scr)[1]:
                cp.start()

    def after(step, nsteps, in_refs, out_refs, scr):
        @pl.when(step == nsteps - 1)
        def _():
            my_id, cps = copies(in_refs, scr)
            for a in range(na):
                scr[a][my_id] = in_refs[a][...]
            for cp in cps:
                cp.wait()
            for a in range(na):
                acc = scr[a][0]
                for d in range(1, N_DEV):
                    acc = acc + scr[a][d]
                out_refs[a][...] = acc

    return _Comm(inputs=list(parts), in_specs=[_VMEM_WHOLE] * na,
                 out_shape=[jax.ShapeDtypeStruct(p.shape, F32) for p in parts], out_specs=[_VMEM_WHOLE] * na,
                 scratch_shapes=[pltpu.VMEM((N_DEV,) + p.shape, F32) for p in parts] + [
                     pltpu.SemaphoreType.DMA((na, N_DEV - 1)), pltpu.SemaphoreType.DMA((na, N_DEV - 1))],
                 before=before, after=after)


def _unshard_cols(g):
    return jnp.transpose(g, (1, 0, 2)).reshape(g.shape[1], N_DEV * g.shape[2])


def _row_blocks(w):
    return w.reshape(4, 2, w.shape[0] // N_DEV, w.shape[1])


def _stack_rows(parts):
    a = jnp.concatenate(parts, axis=0)
    return jnp.pad(a, ((0, (-a.shape[0]) % 8), (0, 0)))


def _padded_decay_weights(wd_f, wd_b):
    zeros = lambda n: jnp.zeros((n, KEY_W), F32)
    return (jnp.concatenate([wd_f, zeros(LANE - LOWRANK)], axis=0),
            jnp.concatenate([zeros(LOWRANK), wd_b, zeros(LANE - 2 * LOWRANK)], axis=0))


def kernel(x, norm1_g, w_in,w_decay_f, b_decay_f, w_decay_b, b_decay_b, gla_norm_g, gmlp_ln_g, gmlp_ln_b, w_spatial, b_spatial, w_out, norm2_g, w_gate, w_up, w_down, final_norm_g, loss_target, m_norm1_g, m_w_in, m_w_decay_f, m_b_decay_f, m_w_decay_b, m_b_decay_b, m_gla_norm_g, m_gmlp_ln_g, m_gmlp_ln_b, m_w_spatial, m_b_spatial, m_w_out, m_norm2_g, m_w_gate, m_w_up, m_w_down, m_final_norm_g, v_norm1_g, v_w_in, v_w_decay_f, v_b_decay_f, v_w_decay_b, v_b_decay_b, v_gla_norm_g, v_gmlp_ln_g, v_gmlp_ln_b, v_w_spatial, v_b_spatial, v_w_out, v_norm2_g, v_w_gate, v_w_up, v_w_down, v_final_norm_g):
    t = x.shape[1]
    xt = x[0]
    target = loss_target[0]
    pos_x, pos_y, pos_c = _mesh_pos()
    my_pos = jnp.stack([2 * pos_x + pos_y, pos_c]).astype(jnp.int32)
    my_id = 4 * pos_x + 2 * pos_y + pos_c

    tile = lambda n: min(n, t)
    ln_g, ln_b, w_sp = gmlp_ln_g, gmlp_ln_b, w_spatial[0]
    b_sp_col = b_spatial[0][:, :, None]
    shard = {"w_in": w_in[0].T, "w_out": w_out[0], "w_gate": w_gate[0].T, "w_up": w_up[0].T, "w_down": w_down[0]}
    shard_m = {"w_in": m_w_in[0].T, "w_out": m_w_out[0], "w_gate": m_w_gate[0].T, "w_up": m_w_up[0].T,
               "w_down": m_w_down[0]}
    shard_v = {"w_in": v_w_in[0].T, "w_out": v_w_out[0], "w_gate": v_w_gate[0].T, "w_up": v_w_up[0].T,
               "w_down": v_w_down[0]}
    transposed = ("w_in", "w_gate", "w_up")

    decay_shard = jnp.stack([w_decay_f[0], w_decay_b[0]])
    (hb,), ((g_in, g_decay),) = _norm1(xt, norm1_g, tile(TOKEN_TILE["norm1"]),
                                       [_gather_comm([shard["w_in"], decay_shard], [True, False])])
    w_in_t = g_in.reshape(PROJ_W, D_MODEL)
    wd_pad_f, wd_pad_b = _padded_decay_weights(_unshard_cols(g_decay[:, 0]), _unshard_cols(g_decay[:, 1]))
    (p,), ((g_gate, g_out),) = _in_proj(
        hb, w_in_t, tile(TOKEN_TILE["in_proj"]), [_gather_comm([shard["w_gate"], shard["w_out"]], [True, True])])
    (o_f, st_f, o_b, st_b), ((g_up,),) = _gla_fwd(
        p, wd_pad_f, b_decay_f, wd_pad_b, b_decay_b, tile(TOKEN_TILE["gla"]), [_gather_comm([shard["w_up"]], [True])])
    w_out_full = g_out.reshape(D_MODEL, D_MODEL)
    (x1, ycat), ((g_down,),) = _mix_fwd(xt, o_f, o_b, p, gla_norm_g, ln_g, ln_b, w_sp, b_sp_col, w_out_full,
                                        tile(TOKEN_TILE["mix_fwd"]), [_gather_comm([shard["w_down"]], [True])])

    dx1, h2b, dgate, dup, act, dx2, loss_acc, d_gf, d_g2 = _ffn(
        x1, target, norm2_g, final_norm_g[None, :], g_gate.reshape(D_FF, D_MODEL), g_up.reshape(D_FF, D_MODEL),
        g_down.reshape(D_FF, D_MODEL), tile(TOKEN_TILE["ffn"]))
    dw_gate, _ = _matmul_tn(dgate, h2b, D_FF // 2, tile(TOKEN_TILE["dw"]), "grad_w_gate")
    dw_up, _ = _matmul_tn(dup, h2b, D_FF // 2, tile(TOKEN_TILE["dw"]), "grad_w_up")
    dw_down, _ = _matmul_tn(act, dx2, D_FF // 2, tile(TOKEN_TILE["dw"]), "grad_w_down")

    reduced = {}
    staged = {"w_gate": _row_blocks(dw_gate), "w_up": _row_blocks(dw_up)}
    (d_o, dpg, dpu, dpv, dw_out, d_gg, d_lg, d_lb, dw_sp, db_sp), (reduced["w_down"], staged_sib) = _mix_bwd(
        dx1, ycat, o_f, o_b, p, gla_norm_g, ln_g, ln_b, w_sp, b_sp_col, w_out_full,
        tile(TOKEN_TILE["mix_bwd"]),
        [_reduce_scatter_comm(_row_blocks(dw_down)), _sibling_exchange_comm(list(staged.values()))])
    staged_sums = [_chip_sum(my_pos, g, s, g.shape[2], "chip_sum_" + n)
                   for (n, g), s in zip(staged.items(), staged_sib)]
    (dq_f, dk_f, dv_f, dlr_f, dwd_f, dbd_f, dq_b, dk_b, dv_b, dlr_b, dwd_b, dbd_b), (reduced["w_out"], staged_recv) = (
        _gla_bwd(p, wd_pad_f, b_decay_f, wd_pad_b, b_decay_b, st_f, st_b, d_o, tile(TOKEN_TILE["gla"]),
                 [_reduce_scatter_comm(_row_blocks(dw_out)), _chips_exchange_comm([s[1] for s in staged_sums])]))
    for n, s, rc in zip(staged, staged_sums, staged_recv):
        reduced[n] = (s[0], rc)
    (grad_x, dp, d_g1), _ = _in_proj_bwd(
        xt, norm1_g, dx1, dq_f, dq_b, dk_f, dk_b, dv_f, dv_b, dpg, dpu, dpv, dlr_f, dlr_b, w_in_t,
        tile(TOKEN_TILE["in_proj_bwd"]))

    stacks = [_stack_rows([d_g1, d_g2, d_gf]), _stack_rows([d_gg, d_lg, d_lb]),
              _stack_rows([dbd_f, dbd_b, jnp.zeros((DECAY_W_ROW - 2, KEY_W), F32), dwd_f[:LOWRANK],
                           dwd_b[LOWRANK:2 * LOWRANK]]),
              _stack_rows([dw_sp.reshape(GMLP_W, GMLP_CHUNK), db_sp[:, :, 0], loss_acc[:1]])]
    dw_main, (small_sums,) = _matmul_tn(dp, hb, PROJ_PAD // 3, tile(TOKEN_TILE["dw"]), "grad_w_in",
                                        [_all_reduce_small_comm(stacks)])
    ((in_own, in_recv),) = _comm_only(
        [_reduce_scatter_comm(dw_main, W_IN_WINDOW, _w_in_block_pieces)], "grad_w_in_reduce_scatter")

    big_out = {"w_in": [r.T for r in _adamw_window(in_own, in_recv, shard["w_in"], shard_m["w_in"], shard_v["w_in"],
                                                   "adamw_w_in")]}
    for n, (own_sum, recv) in reduced.items():
        rows = shard[n].shape[0]
        half = rows // 2 if rows % 32 == 0 else rows
        res = _adamw_shard(own_sum, recv, shard[n], shard_m[n], shard_v[n], half, "adamw_" + n)
        big_out[n] = [r.T if n in transposed else r for r in res]

    s1024, s512, s256, s128 = small_sums
    loss = s128[GMLP_W + GMLP_GROUPS, 0]
    col0 = my_id * (KEY_W // N_DEV)
    decay_cols = lambda row0: lax.dynamic_slice(s256, (row0, col0), (LOWRANK, KEY_W // N_DEV))
    flat = lambda a: a.reshape(-1, a.shape[-1])
    small = {
        "norm1_g": ((s1024, 0, 1), norm1_g, m_norm1_g, v_norm1_g),
        "w_decay_f": ((decay_cols(DECAY_W_ROW), 0, LOWRANK), w_decay_f, m_w_decay_f, v_w_decay_f),
        "b_decay_f": ((s256, 0, 1), b_decay_f, m_b_decay_f, v_b_decay_f),
        "w_decay_b": ((decay_cols(DECAY_W_ROW + LOWRANK), 0, LOWRANK), w_decay_b, m_w_decay_b, v_w_decay_b),
        "b_decay_b": ((s256, 1, 1), b_decay_b, m_b_decay_b, v_b_decay_b),
        "gla_norm_g": ((s512, 0, 1), gla_norm_g, m_gla_norm_g, v_gla_norm_g),
        "gmlp_ln_g": ((s512, 1, 1), gmlp_ln_g, m_gmlp_ln_g, v_gmlp_ln_g),
        "gmlp_ln_b": ((s512, 2, 1), gmlp_ln_b, m_gmlp_ln_b, v_gmlp_ln_b),
        "w_spatial": ((s128, 0, GMLP_W), w_spatial, m_w_spatial, v_w_spatial),
        "b_spatial": ((s128, GMLP_W, GMLP_GROUPS), b_spatial, m_b_spatial, v_b_spatial),
        "norm2_g": ((s1024, 1, 1), norm2_g, m_norm2_g, v_norm2_g),
        "final_norm_g": ((s1024, 2, 1), final_norm_g, m_final_norm_g, v_final_norm_g),
    }
    small_res = _adamw_small([(g, flat(w), flat(m), flat(v)) for g, w, m, v in small.values()])
    small_out = {n: [r.reshape(small[n][1].shape) for r in res] for n, res in zip(small, small_res)}

    order = ["norm1_g", "w_in", "w_decay_f", "b_decay_f", "w_decay_b", "b_decay_b", "gla_norm_g", "gmlp_ln_g",
             "gmlp_ln_b", "w_spatial", "b_spatial", "w_out", "norm2_g", "w_gate", "w_up", "w_down", "final_norm_g"]
    outs = []
    for kind in range(4):
        for n in order:
            outs.append(big_out[n][kind][None] if n in big_out else small_out[n][kind])
    return (loss, grad_x[None], *outs)
```

```python
import functools
import math

import jax
import jax.numpy as jnp
from jax import lax
from jax.experimental import pallas as pl
from jax.experimental.pallas import tpu as pltpu

F32 = jnp.float32
BF16 = jnp.bfloat16

D_MODEL = 1024
GLA_HEADS = 4
GLA_DK = 64
GLA_DV = 128
KEY_W = GLA_HEADS * GLA_DK
VAL_W = GLA_HEADS * GLA_DV
LOWRANK = 16
GLA_TAU = 16.0
GLA_CHUNK = 64
GMLP_W = 512
GMLP_GROUPS = 4
GMLP_CHUNK = 128
D_FF = 2816
EPS = 1e-6
Q_SCALE = GLA_DK ** -0.5
PROJ_PAD = 2688
LR_COL = 2560
LANE = 128
N_DEV = 8

ADAM_LR = 0.001
ADAM_B1 = 0.9
ADAM_B2 = 0.999
ADAM_EPS = 1e-08
ADAM_WD = 0.01
ADAM_STEP = 10

VMEM_LIMIT = 56 * 1024 * 1024
TOKEN_TILE = {"norm1": 512, "in_proj": 1024, "gla": 1024, "mix_fwd": 1024, "ffn": 256, "mix_bwd": 512,
              "in_proj_bwd": 512, "dw": 2048}
DECAY_W_ROW = 8
MESH_ID = pl.DeviceIdType.MESH
INV_SQRT2 = 0.7071067811865476
INV_SQRT_2PI = 0.3989422804014327


def _params(n_axes=1):
    return pltpu.CompilerParams(dimension_semantics=("arbitrary",) * n_axes, vmem_limit_bytes=VMEM_LIMIT)


def _mm(a, b):
    return jnp.dot(a.astype(BF16), b.astype(BF16), preferred_element_type=F32)


def _mm_nt(a, b):
    return lax.dot_general(a.astype(BF16), b.astype(BF16), (((1,), (1,)), ((), ())), preferred_element_type=F32)


def _mm_tn(a, b):
    return lax.dot_general(a.astype(BF16), b.astype(BF16), (((0,), (0,)), ((), ())), preferred_element_type=F32)


def _const_spec(shape):
    nd = len(shape)
    return pl.BlockSpec(shape, lambda *_: (0,) * nd, pipeline_mode=pl.Buffered(1))


def _acc_spec(shape):
    nd = len(shape)
    return pl.BlockSpec(shape, lambda *_: (0,) * nd)


class _Comm:
    def __init__(self, inputs, in_specs, out_shape, out_specs, scratch_shapes, before, after):
        self.inputs, self.in_specs, self.out_shape, self.out_specs = inputs, in_specs, out_shape, out_specs
        self.scratch_shapes, self.before, self.after = scratch_shapes, before, after


def _fused_call(body, comms, *, name, grid, inputs, in_specs, out_specs, out_shape, scratch_shapes=()):
    n_in, n_out, n_scr = len(in_specs), len(out_specs), len(scratch_shapes)
    nsteps = math.prod(grid)
    sizes = [(len(c.inputs), len(c.out_shape), len(c.scratch_shapes)) for c in comms]

    def full_body(*refs):
        step = pl.program_id(0)
        for axis in range(1, len(grid)):
            step = step * grid[axis] + pl.program_id(axis)
        ins, rest = refs[:n_in], refs[n_in:]
        c_ins = []
        for ci, _, _ in sizes:
            c_ins.append(rest[:ci])
            rest = rest[ci:]
        outs, rest = rest[:n_out], rest[n_out:]
        c_outs = []
        for _, co, _ in sizes:
            c_outs.append(rest[:co])
            rest = rest[co:]
        scr, rest = rest[:n_scr], rest[n_scr:]
        c_scr = []
        for _, _, cs in sizes:
            c_scr.append(rest[:cs])
            rest = rest[cs:]
        for c, a, b, s in zip(comms, c_ins, c_outs, c_scr):
            c.before(step, nsteps, a, b, s)
        body(*ins, *outs, *scr)
        for c, a, b, s in zip(comms, c_ins, c_outs, c_scr):
            c.after(step, nsteps, a, b, s)

    results = pl.pallas_call(
        full_body, name=name, grid=grid,
        in_specs=list(in_specs) + [s for c in comms for s in c.in_specs],
        out_specs=tuple(out_specs) + tuple(s for c in comms for s in c.out_specs),
        out_shape=tuple(out_shape) + tuple(s for c in comms for s in c.out_shape),
        scratch_shapes=list(scratch_shapes) + [s for c in comms for s in c.scratch_shapes],
        compiler_params=_params(len(grid)),
    )(*inputs, *[a for c in comms for a in c.inputs])
    own, rest = results[:n_out], results[n_out:]
    comm_results = []
    for _, co, _ in sizes:
        comm_results.append(rest[:co])
        rest = rest[co:]
    return own, comm_results


def _gelu(x):
    return 0.5 * x * (1.0 + lax.erf(x * INV_SQRT2))


def _gelu_and_grad(x):
    cdf = 0.5 * (1.0 + lax.erf(x * INV_SQRT2))
    return x * cdf, cdf + x * jnp.exp(-0.5 * x * x) * INV_SQRT_2PI


def _sigmoid(x):
    return 0.5 + 0.5 * jnp.tanh(0.5 * x)


def _silu_and_grad(x):
    s = _sigmoid(x)
    return x * s, s * (1.0 + x * (1.0 - s))


def _norm1(x, g1, tm, comms=()):
    t = x.shape[0]

    def body(x_ref, g_ref, h_ref):
        xv = x_ref[...]
        r = lax.rsqrt(jnp.mean(xv * xv, axis=-1, keepdims=True) + EPS)
        h_ref[...] = (xv * r * g_ref[...]).astype(BF16)

    row = pl.BlockSpec((tm, D_MODEL), lambda i: (i, 0))
    return _fused_call(body, comms, name="norm1", grid=(t // tm,), inputs=(x, g1),
                       in_specs=[row, _const_spec((1, D_MODEL))], out_specs=(row,),
                       out_shape=(jax.ShapeDtypeStruct((t, D_MODEL), BF16),))


PROJ_W = 2592
LR_REF = 1536
PROJ_ROWS = ((0, LR_REF, 0), (LR_REF + 2 * LOWRANK, PROJ_W, LR_REF), (LR_REF, LR_REF + LANE, LR_COL))


def _in_proj(h, w_in_t, tm, comms=()):
    t = h.shape[0]

    def body(h_ref, w_ref, p_ref):
        hv = h_ref[...]
        for r0, r1, c0 in PROJ_ROWS:
            p_ref[:, c0:c0 + r1 - r0] = _mm_nt(hv, w_ref[r0:r1, :]).astype(BF16)

    return _fused_call(
        body, comms, name="in_proj", grid=(t // tm,), inputs=(h, w_in_t),
        in_specs=[pl.BlockSpec((tm, D_MODEL), lambda i: (i, 0)), _const_spec((PROJ_W, D_MODEL))],
        out_specs=(pl.BlockSpec((tm, PROJ_PAD), lambda i: (i, 0)),),
        out_shape=(jax.ShapeDtypeStruct((t, PROJ_PAD), BF16),))


def _tri(upper):
    r = lax.broadcasted_iota(jnp.int32, (GLA_CHUNK, GLA_CHUNK), 0)
    c = lax.broadcasted_iota(jnp.int32, (GLA_CHUNK, GLA_CHUNK), 1)
    return jnp.where((c >= r) if upper else (c <= r), 1.0, 0.0).astype(BF16)


def _chunk_cumsum(tri, a, add=None):
    hi = a.astype(BF16)
    lo = (a - hi.astype(F32)).astype(BF16)
    dot = functools.partial(jnp.dot, preferred_element_type=F32)
    sums = [dot(tri, hi[_chunk_rows(c)]) + dot(tri, lo[_chunk_rows(c)]) for c in range(a.shape[0] // GLA_CHUNK)]
    return jnp.concatenate(sums if add is None else [s + r for s, r in zip(sums, add)], axis=0)


def _chunk_rows(c):
    return slice(c * GLA_CHUNK, (c + 1) * GLA_CHUNK)


def _gla_masks(rev):
    dk_bits, dv_bits = GLA_DK.bit_length() - 1, GLA_DV.bit_length() - 1
    key_head = lax.broadcasted_iota(jnp.int32, (GLA_CHUNK, KEY_W), 1) >> dk_bits
    val_head = lax.broadcasted_iota(jnp.int32, (GLA_CHUNK, VAL_W), 1) >> dv_bits
    t = lax.broadcasted_iota(jnp.int32, (GLA_HEADS * GLA_CHUNK, GLA_CHUNK), 0) & (GLA_CHUNK - 1)
    s = lax.broadcasted_iota(jnp.int32, (GLA_HEADS * GLA_CHUNK, GLA_CHUNK), 1)
    return key_head, val_head, (s >= t) if rev else (s <= t)


def _stack_heads(a, head_of_lane):
    a = a.astype(BF16)
    return jnp.concatenate([jnp.where(head_of_lane == h, a, jnp.zeros_like(a)) for h in range(GLA_HEADS)], axis=0)


def _rows_by_head(a):
    return jnp.concatenate([a[:, h * GLA_DV:(h + 1) * GLA_DV] for h in range(GLA_HEADS)], axis=0)


def _lanes_by_head(r):
    return jnp.concatenate([r[h * GLA_CHUNK:(h + 1) * GLA_CHUNK] for h in range(GLA_HEADS)], axis=1)


def _head_diagonal(r, head_of_lane):
    rows = r.shape[0] // GLA_HEADS
    out = jnp.where(head_of_lane == 0, r[:rows], 0.0)
    for h in range(1, GLA_HEADS):
        out = out + jnp.where(head_of_lane == h, r[h * rows:(h + 1) * rows], 0.0)
    return out


def _tile_terms(la, q, k, tri, rev):
    nc = la.shape[0] // GLA_CHUNK
    q, k = q.astype(F32), k.astype(F32)
    b = _chunk_cumsum(tri, la)
    ebl = [jnp.exp(b[c * GLA_CHUNK:c * GLA_CHUNK + 1] if rev else b[(c + 1) * GLA_CHUNK - 1:(c + 1) * GLA_CHUNK])
           for c in range(nc)]
    eb = jnp.exp(b)
    enb = jnp.exp(-b)
    kd = k * enb
    ke = jnp.concatenate([kd[_chunk_rows(c)] * ebl[c] for c in range(nc)], axis=0)
    return ebl, eb, enb, q * Q_SCALE * eb, kd, ke


def _log_decay(lr_ref, wd_ref, bd_ref):
    z = _mm(lr_ref[...], wd_ref[...]) + bd_ref[...]
    return z, jax.nn.log_sigmoid(z) * (1.0 / GLA_TAU)


def _p_specs(tg, tile):
    return [pl.BlockSpec((tg, KEY_W), lambda i: (tile(i), 0)),
            pl.BlockSpec((tg, KEY_W), lambda i: (tile(i), 1)),
            pl.BlockSpec((tg, VAL_W), lambda i: (tile(i), 1)),
            pl.BlockSpec((tg, LANE), lambda i: (tile(i), LR_COL // LANE))]


def _gla_fwd_dir(rev, nc, q_ref, k_ref, v_ref, lr_ref, wd_ref, bd_ref, o_ref, st_ref, state):
    key_head, _, causal = _gla_masks(rev)
    order = range(nc - 1, -1, -1) if rev else range(nc)

    def intra():
        _, la = _log_decay(lr_ref, wd_ref, bd_ref)
        ebl, _, _, qd, kd, ke = _tile_terms(la, q_ref[...], k_ref[...], _tri(rev), rev)
        kd = kd.astype(BF16)
        v = {c: v_ref[_chunk_rows(c), :].astype(BF16) for c in order}
        qd_stack = {c: _stack_heads(qd[_chunk_rows(c)], key_head) for c in order}
        ke_stack = {c: _stack_heads(ke[_chunk_rows(c)], key_head) for c in order}
        a_all = {c: _mm_nt(qd_stack[c], kd[_chunk_rows(c)]) for c in order}
        a_all = {c: jnp.where(causal, a_all[c], 0.0).astype(BF16) for c in order}
        head_rows = lambda a, h: a[h * GLA_CHUNK:(h + 1) * GLA_CHUNK]
        head_vals = lambda a, h: a[:, h * GLA_DV:(h + 1) * GLA_DV]
        r = {c: [_mm(head_rows(a_all[c], h), head_vals(v[c], h)) for h in range(GLA_HEADS)] for c in order}
        upd = {c: _mm_tn(_rows_by_head(v[c]), ke_stack[c]) for c in order}
        return {c: (ebl[c], qd_stack[c], r[c], upd[c]) for c in order}

    def scan(terms):
        st = state[...]
        states = {}
        for c in order:
            states[c] = st
            st_ref[c] = st.astype(BF16)
            st = st * terms[c][0] + terms[c][3]
        state[...] = st
        return states

    def inter(terms, states):
        r_inter = {c: _mm_nt(terms[c][1], states[c]) for c in order}
        for c in order:
            o_ref[_chunk_rows(c), :] = jnp.concatenate(
                [terms[c][2][h] + r_inter[c][h * GLA_CHUNK:(h + 1) * GLA_CHUNK] for h in range(GLA_HEADS)], axis=1)

    return intra, scan, inter


def _gla_fwd(p, wd_pad_f, bd_f, wd_pad_b, bd_b, tg, comms=()):
    t = p.shape[0]
    nt = t // tg
    nc = tg // GLA_CHUNK
    up, down = (lambda i: i), (lambda i: nt - 1 - i)

    def body(qf, kf, vf, lrf, qb, kb, vb, lrb, wdf, bdf, wdb, bdb, of, stf, ob, stb, state_f, state_b):
        @pl.when(pl.program_id(0) == 0)
        def _():
            state_f[...] = jnp.zeros_like(state_f)
            state_b[...] = jnp.zeros_like(state_b)

        dirs = [_gla_fwd_dir(False, nc, qf, kf, vf, lrf, wdf, bdf, of, stf, state_f),
                _gla_fwd_dir(True, nc, qb, kb, vb, lrb, wdb, bdb, ob, stb, state_b)]
        terms = [intra() for intra, _, _ in dirs]
        states = [scan(t) for (_, scan, _), t in zip(dirs, terms)]
        for (_, _, inter), t, s in zip(dirs, terms, states):
            inter(t, s)

    wd_spec, bd_spec = _const_spec((LANE, KEY_W)), _const_spec((1, KEY_W))
    outs = lambda tile: (pl.BlockSpec((tg, VAL_W), lambda i: (tile(i), 0)),
                         pl.BlockSpec((nc, GLA_DV, KEY_W), lambda i: (tile(i), 0, 0)))
    out_shape = (jax.ShapeDtypeStruct((t, VAL_W), F32), jax.ShapeDtypeStruct((t // GLA_CHUNK, GLA_DV, KEY_W), BF16))
    return _fused_call(
        body, comms, name="gla_fwd", grid=(nt,), inputs=(p,) * 8 + (wd_pad_f, bd_f, wd_pad_b, bd_b),
        in_specs=_p_specs(tg, up) + _p_specs(tg, down) + [wd_spec, bd_spec, wd_spec, bd_spec],
        out_specs=outs(up) + outs(down), out_shape=out_shape * 2,
        scratch_shapes=[pltpu.VMEM((GLA_DV, KEY_W), F32)] * 2)


def _gla_bwd_dir(rev, nc, q_ref, k_ref, v_ref, lr_ref, wd_ref, bd_ref, st_ref, do_ref,
                 dq_ref, dk_ref, dv_ref, dlr_ref, dwd_ref, dbd_ref, dstate):
    key_head, val_head, causal = _gla_masks(rev)
    order = range(nc) if rev else range(nc - 1, -1, -1)

    def intra():
        z, la = _log_decay(lr_ref, wd_ref, bd_ref)
        tile = _tile_terms(la, q_ref[...], k_ref[...], _tri(rev), rev)
        qd, kd = tile[3], tile[4].astype(BF16)
        v = {c: v_ref[_chunk_rows(c), :].astype(BF16) for c in order}
        d_o = {c: do_ref[_chunk_rows(c), :] for c in order}
        kd_c = {c: kd[_chunk_rows(c)] for c in order}
        qd_stack = {c: _stack_heads(qd[_chunk_rows(c)], key_head) for c in order}
        do_stack = {c: _stack_heads(d_o[c], val_head) for c in order}
        do_rows = {c: _rows_by_head(d_o[c]) for c in order}
        a_all = {c: _mm_nt(qd_stack[c], kd_c[c]) for c in order}
        head_vals = lambda a, h: a[:, h * GLA_DV:(h + 1) * GLA_DV]
        da_all = {c: jnp.concatenate([_mm_nt(head_vals(d_o[c], h), head_vals(v[c], h)) for h in range(GLA_HEADS)],
                                     axis=0) for c in order}
        a_all = {c: jnp.where(causal, a_all[c], 0.0).astype(BF16) for c in order}
        da_all = {c: jnp.where(causal, da_all[c], 0.0).astype(BF16) for c in order}
        dv = {c: _mm_tn(a_all[c], do_stack[c]) for c in order}
        dqd = {c: _mm(jnp.concatenate([do_rows[c], da_all[c]], axis=1),
                      jnp.concatenate([st_ref[c], kd_c[c]], axis=0)) for c in order}
        dkd = {c: _mm_tn(da_all[c], qd_stack[c]) for c in order}
        upd = {c: _mm_tn(do_rows[c], qd_stack[c]) for c in order}
        dqd = {c: _head_diagonal(dqd[c], key_head) for c in order}
        return z, tile, {c: dict(dv=dv[c], dqd=dqd[c], dkd=dkd[c], upd=upd[c]) for c in order}

    def scan(tile, per):
        dst = dstate[...]
        dsts = {}
        for c in order:
            dsts[c] = dst
            dst = dst * tile[0][c] + per[c]["upd"]
        dstate[...] = dst
        return dsts

    def inter(z, tile, per, dsts):
        ebl, eb, enb, qd, kd, ke = tile
        ke_stack = {c: _stack_heads(ke[_chunk_rows(c)], key_head) for c in order}
        v_rows = {c: _rows_by_head(v_ref[_chunk_rows(c), :].astype(BF16)) for c in order}
        dst_b = {c: dsts[c].astype(BF16) for c in order}
        dv_state = {c: _mm_nt(ke_stack[c], dst_b[c]) for c in order}
        dke_c = {c: _mm(v_rows[c], dst_b[c]) for c in order}
        dke_c = {c: _head_diagonal(dke_c[c], key_head) for c in order}
        dbl_c = {}
        for c in order:
            rows = _chunk_rows(c)
            dv_ref[rows, :] = (per[c]["dv"] + _lanes_by_head(dv_state[c])).astype(BF16)
            dbl_c[c] = (jnp.sum(dsts[c] * st_ref[c].astype(F32), axis=0, keepdims=True) * ebl[c]
                        + jnp.sum(dke_c[c] * ke[rows], axis=0, keepdims=True))
        tile_of = lambda parts: jnp.concatenate([parts[c] for c in range(nc)], axis=0)
        dqd, dkd = tile_of({c: per[c]["dqd"] for c in order}), tile_of({c: per[c]["dkd"] for c in order})
        dke = tile_of(dke_c)
        dke_end = tile_of({c: dke_c[c] * ebl[c] for c in order})
        dq_ref[...] = (dqd * eb * Q_SCALE).astype(BF16)
        dk_ref[...] = ((dkd + dke_end) * enb).astype(BF16)
        db = dqd * qd - dkd * kd - dke * ke
        dla = _chunk_cumsum(_tri(not rev), db, [dbl_c[c] for c in range(nc)])
        dz = dla * (_sigmoid(-z) * (1.0 / GLA_TAU))
        dlr_ref[...] = _mm_nt(dz, wd_ref[...]).astype(BF16)
        dwd_ref[...] += _mm_tn(lr_ref[...], dz)
        dbd_ref[...] += jnp.sum(dz, axis=0, keepdims=True)

    return intra, scan, inter


def _gla_bwd(p, wd_pad_f, bd_f, wd_pad_b, bd_b, st_f, st_b, d_o, tg, comms=()):
    t = p.shape[0]
    nt = t // tg
    nc = tg // GLA_CHUNK
    up, down = (lambda i: i), (lambda i: nt - 1 - i)

    def body(qf, kf, vf, lrf, stf, dof, qb, kb, vb, lrb, stb, dob, wdf, bdf, wdb, bdb,
             dqf, dkf, dvf, dlrf, dwdf, dbdf, dqb, dkb, dvb, dlrb, dwdb, dbdb, dstate_f, dstate_b):
        @pl.when(pl.program_id(0) == 0)
        def _():
            for ref in (dstate_f, dstate_b, dwdf, dbdf, dwdb, dbdb):
                ref[...] = jnp.zeros_like(ref)

        dirs = [_gla_bwd_dir(False, nc, qf, kf, vf, lrf, wdf, bdf, stf, dof, dqf, dkf, dvf, dlrf, dwdf, dbdf,
                             dstate_f),
                _gla_bwd_dir(True, nc, qb, kb, vb, lrb, wdb, bdb, stb, dob, dqb, dkb, dvb, dlrb, dwdb, dbdb,
                             dstate_b)]
        first = [intra() for intra, _, _ in dirs]
        dsts = [scan(tile, per) for (_, scan, _), (_, tile, per) in zip(dirs, first)]
        for (_, _, inter), (z, tile, per), d in zip(dirs, first, dsts):
            inter(z, tile, per, d)

    wd_spec, bd_spec = _const_spec((LANE, KEY_W)), _const_spec((1, KEY_W))
    ins = lambda tile: _p_specs(tg, tile) + [pl.BlockSpec((nc, GLA_DV, KEY_W), lambda i: (tile(i), 0, 0)),
                                             pl.BlockSpec((tg, VAL_W), lambda i: (tile(i), 0))]
    outs = lambda tile: (pl.BlockSpec((tg, KEY_W), lambda i: (tile(i), 0)),
                         pl.BlockSpec((tg, KEY_W), lambda i: (tile(i), 0)),
                         pl.BlockSpec((tg, VAL_W), lambda i: (tile(i), 0)),
                         pl.BlockSpec((tg, LANE), lambda i: (tile(i), 0)),
                         _acc_spec((LANE, KEY_W)), _acc_spec((1, KEY_W)))
    out_shape = (jax.ShapeDtypeStruct((t, KEY_W), BF16), jax.ShapeDtypeStruct((t, KEY_W), BF16),
                 jax.ShapeDtypeStruct((t, VAL_W), BF16), jax.ShapeDtypeStruct((t, LANE), BF16),
                 jax.ShapeDtypeStruct((LANE, KEY_W), F32), jax.ShapeDtypeStruct((1, KEY_W), F32))
    scratch = [pltpu.VMEM((GLA_DV, KEY_W), F32)]
    return _fused_call(
        body, comms, name="gla_bwd", grid=(nt,),
        inputs=(p, p, p, p, st_f, d_o, p, p, p, p, st_b, d_o, wd_pad_f, bd_f, wd_pad_b, bd_b),
        in_specs=ins(down) + ins(up) + [wd_spec, bd_spec, wd_spec, bd_spec],
        out_specs=outs(down) + outs(up), out_shape=out_shape * 2, scratch_shapes=scratch * 2)


def _head_rms(o):
    parts, scales = [], []
    for h in range(GLA_HEADS):
        oh = o[:, h * GLA_DV:(h + 1) * GLA_DV]
        r = lax.rsqrt(jnp.mean(oh * oh, axis=-1, keepdims=True) + EPS)
        parts.append(oh * r)
        scales.append(jnp.broadcast_to(r, oh.shape))
    return jnp.concatenate(parts, axis=1), jnp.concatenate(scales, axis=1)


def _layernorm_stats(zv):
    mu = jnp.mean(zv, axis=-1, keepdims=True)
    xc = zv - mu
    rs = lax.rsqrt(jnp.mean(xc * xc, axis=-1, keepdims=True) + EPS)
    return xc * rs, rs


def _mix_fwd(x, o_f, o_b, p, gla_g, ln_g, ln_b, w_sp, b_sp, w_out, tm, comms=()):
    t = x.shape[0]
    nch = tm // GMLP_CHUNK

    def body(x_ref, of_ref, ob_ref, pg_ref, pu_ref, pv_ref, gg_ref, lg_ref, lb_ref, ws_ref, bs_ref, wo_ref,
             x1_ref, y_ref, s_scr):
        on, _ = _head_rms(of_ref[...] + ob_ref[...])
        pg = pg_ref[...].astype(F32)
        y_a = on * gg_ref[...] * (pg * _sigmoid(pg))
        zu = _gelu(pu_ref[...].astype(F32))
        vhat, _ = _layernorm_stats(_gelu(pv_ref[...].astype(F32)))
        vln = (vhat * lg_ref[...] + lb_ref[...]).astype(BF16)
        for g in range(GMLP_GROUPS):
            w_g = ws_ref[g].astype(BF16)
            b_g = bs_ref[g]
            cols = slice(g * LANE, (g + 1) * LANE)
            for n in range(nch):
                rows = slice(n * GMLP_CHUNK, (n + 1) * GMLP_CHUNK)
                s_scr[rows, cols] = jnp.dot(w_g, vln[rows, cols], preferred_element_type=F32) + b_g
        ycat = jnp.concatenate([y_a, zu * s_scr[...]], axis=1).astype(BF16)
        y_ref[...] = ycat
        x1_ref[...] = x_ref[...] + jnp.dot(ycat, wo_ref[...], preferred_element_type=F32)

    half = lambda j: pl.BlockSpec((tm, VAL_W), lambda i: (i, j))
    return _fused_call(
        body, comms, name="mix_fwd", grid=(t // tm,),
        inputs=(x, o_f, o_b, p, p, p, gla_g, ln_g, ln_b, w_sp, b_sp, w_out),
        in_specs=[pl.BlockSpec((tm, D_MODEL), lambda i: (i, 0)), half(0), half(0), half(2), half(3), half(4),
                  _const_spec((1, VAL_W)), _const_spec((1, GMLP_W)), _const_spec((1, GMLP_W)),
                  _const_spec((GMLP_GROUPS, GMLP_CHUNK, GMLP_CHUNK)), _const_spec((GMLP_GROUPS, GMLP_CHUNK, 1)),
                  _const_spec((D_MODEL, D_MODEL))],
        out_specs=(pl.BlockSpec((tm, D_MODEL), lambda i: (i, 0)), pl.BlockSpec((tm, D_MODEL), lambda i: (i, 0))),
        out_shape=(jax.ShapeDtypeStruct((t, D_MODEL), F32), jax.ShapeDtypeStruct((t, D_MODEL), BF16)),
        scratch_shapes=[pltpu.VMEM((tm, GMLP_W), F32)])


def _mix_bwd(dx1, ycat, o_f, o_b, p, gla_g, ln_g, ln_b, w_sp, b_sp, w_out, tm, comms=()):
    t = dx1.shape[0]
    nch = tm // GMLP_CHUNK

    def body(dx1_ref, y_ref, of_ref, ob_ref, pg_ref, pu_ref, pv_ref, gg_ref, lg_ref, lb_ref, ws_ref, bs_ref, wo_ref,
             do_ref, dpg_ref, dpu_ref, dpv_ref, dwo_ref, dgg_ref, dlg_ref, dlb_ref, dws_ref, dbs_ref,
             s_scr, dvln_scr):
        @pl.when(pl.program_id(0) == 0)
        def _():
            for ref in (dwo_ref, dgg_ref, dlg_ref, dlb_ref, dws_ref, dbs_ref):
                ref[...] = jnp.zeros_like(ref)

        dx1 = dx1_ref[...].astype(BF16)
        dycat = _mm_nt(dx1, wo_ref[...])
        dwo_ref[...] += _mm_tn(y_ref[...], dx1)
        dy_a = dycat[:, :VAL_W]
        dy_b = dycat[:, VAL_W:]
        on, r = _head_rms(of_ref[...] + ob_ref[...])
        pg = pg_ref[...].astype(F32)
        sil, dsil = _silu_and_grad(pg)
        gg = gg_ref[...]
        dgg_ref[...] += jnp.sum(dy_a * sil * on, axis=0, keepdims=True)
        don = dy_a * sil * gg
        prod = don * on
        means = jnp.concatenate(
            [jnp.broadcast_to(jnp.mean(prod[:, h * GLA_DV:(h + 1) * GLA_DV], axis=-1, keepdims=True),
                              (tm, GLA_DV)) for h in range(GLA_HEADS)], axis=1)
        do_ref[...] = (r * (don - on * means)).astype(BF16)
        dpg_ref[...] = (dy_a * on * gg * dsil).astype(BF16)
        pu = pu_ref[...].astype(F32)
        pv = pv_ref[...].astype(F32)
        zu, dzu_dpu = _gelu_and_grad(pu)
        zv, dzv_dpv = _gelu_and_grad(pv)
        vhat, rs = _layernorm_stats(zv)
        lg = lg_ref[...]
        vln = (vhat * lg + lb_ref[...]).astype(BF16)
        ds32 = dy_b * zu
        ds = ds32.astype(BF16)
        blocks = [(g, n) for g in range(GMLP_GROUPS) for n in range(nch)]
        at = lambda g, n: (slice(n * GMLP_CHUNK, (n + 1) * GMLP_CHUNK), slice(g * LANE, (g + 1) * LANE))
        w_sp = [ws_ref[g].astype(BF16) for g in range(GMLP_GROUPS)]
        v_blk = {b: vln[at(*b)] for b in blocks}
        ds_blk = {b: ds[at(*b)] for b in blocks}
        s_blk = {b: jnp.dot(w_sp[b[0]], v_blk[b], preferred_element_type=F32) for b in blocks}
        dw_blk = {b: _mm_nt(ds_blk[b], v_blk[b]) for b in blocks}
        dvln_blk = {b: _mm_tn(w_sp[b[0]], ds_blk[b]) for b in blocks}
        for b in blocks:
            s_scr[at(*b)] = s_blk[b] + bs_ref[b[0]]
            dvln_scr[at(*b)] = dvln_blk[b]
        for g in range(GMLP_GROUPS):
            dws_ref[g] += sum(dw_blk[(g, n)] for n in range(nch))
            dbs_ref[g] += sum(jnp.sum(ds32[at(g, n)], axis=-1, keepdims=True) for n in range(nch))
        dpu_ref[...] = (dy_b * s_scr[...] * dzu_dpu).astype(BF16)
        dvln = dvln_scr[...]
        dlg_ref[...] += jnp.sum(dvln * vhat, axis=0, keepdims=True)
        dlb_ref[...] += jnp.sum(dvln, axis=0, keepdims=True)
        dvhat = dvln * lg
        dzv = rs * (dvhat - jnp.mean(dvhat, axis=-1, keepdims=True)
                    - vhat * jnp.mean(dvhat * vhat, axis=-1, keepdims=True))
        dpv_ref[...] = (dzv * dzv_dpv).astype(BF16)

    half = lambda j: pl.BlockSpec((tm, VAL_W), lambda i: (i, j))
    full = pl.BlockSpec((tm, D_MODEL), lambda i: (i, 0))
    sp_shape = (GMLP_GROUPS, GMLP_CHUNK, GMLP_CHUNK)
    bs_shape = (GMLP_GROUPS, GMLP_CHUNK, 1)
    return _fused_call(
        body, comms, name="mix_bwd", grid=(t // tm,),
        inputs=(dx1, ycat, o_f, o_b, p, p, p, gla_g, ln_g, ln_b, w_sp, b_sp, w_out),
        in_specs=[full, full, half(0), half(0), half(2), half(3), half(4),
                  _const_spec((1, VAL_W)), _const_spec((1, GMLP_W)), _const_spec((1, GMLP_W)),
                  _const_spec(sp_shape), _const_spec(bs_shape), _const_spec((D_MODEL, D_MODEL))],
        out_specs=(half(0), half(0), half(0), half(0), _acc_spec((D_MODEL, D_MODEL)), _acc_spec((1, VAL_W)),
                   _acc_spec((1, GMLP_W)), _acc_spec((1, GMLP_W)), _acc_spec(sp_shape), _acc_spec(bs_shape)),
        out_shape=(jax.ShapeDtypeStruct((t, VAL_W), BF16),) * 4 + (
            jax.ShapeDtypeStruct((D_MODEL, D_MODEL), F32), jax.ShapeDtypeStruct((1, VAL_W), F32),
            jax.ShapeDtypeStruct((1, GMLP_W), F32), jax.ShapeDtypeStruct((1, GMLP_W), F32),
            jax.ShapeDtypeStruct(sp_shape, F32), jax.ShapeDtypeStruct(bs_shape, F32)),
        scratch_shapes=[pltpu.VMEM((tm, GMLP_W), F32), pltpu.VMEM((tm, GMLP_W), F32)])


def _rms_bwd(dy_scaled, xn, r):
    return r * (dy_scaled - xn * jnp.mean(dy_scaled * xn, axis=-1, keepdims=True))


def _ffn(x1, target, g2, gf, w_gate, w_up, w_down, tm):
    t = x1.shape[0]

    def body(x1_ref, tg_ref, g2_ref, gf_ref, wg_ref, wu_ref, wd_ref,
             dx1_ref, h2_ref, dgate_ref, dup_ref, act_ref, dx2_ref, loss_ref, dgf_ref, dg2_ref):
        @pl.when(pl.program_id(0) == 0)
        def _():
            for ref in (loss_ref, dgf_ref, dg2_ref):
                ref[...] = jnp.zeros_like(ref)

        x1v = x1_ref[...]
        g2v = g2_ref[...]
        gfv = gf_ref[...]
        r2 = lax.rsqrt(jnp.mean(x1v * x1v, axis=-1, keepdims=True) + EPS)
        xn1 = x1v * r2
        h2 = (xn1 * g2v).astype(BF16)
        h2_ref[...] = h2
        gate = _mm_nt(h2, wg_ref[...])
        up = _mm_nt(h2, wu_ref[...])
        sil, dsil = _silu_and_grad(gate)
        act = (sil * up).astype(BF16)
        act_ref[...] = act
        x2 = x1v + jnp.dot(act, wd_ref[...], preferred_element_type=F32)
        rf = lax.rsqrt(jnp.mean(x2 * x2, axis=-1, keepdims=True) + EPS)
        xn2 = x2 * rf
        err = xn2 * gfv - tg_ref[...]
        loss_ref[...] += 0.5 * jnp.sum(jnp.mean(err * err, axis=-1, keepdims=True))
        dy = err * (1.0 / D_MODEL)
        dgf_ref[...] += jnp.sum(dy * xn2, axis=0, keepdims=True)
        dx2 = _rms_bwd(dy * gfv, xn2, rf)
        dx2b = dx2.astype(BF16)
        dx2_ref[...] = dx2b
        dact = _mm_nt(dx2b, wd_ref[...])
        dgate = (dact * up * dsil).astype(BF16)
        dup = (dact * sil).astype(BF16)
        dgate_ref[...] = dgate
        dup_ref[...] = dup
        dh2 = _mm(dgate, wg_ref[...]) + _mm(dup, wu_ref[...])
        dg2_ref[...] += jnp.sum(dh2 * xn1, axis=0, keepdims=True)
        dx1_ref[...] = dx2 + _rms_bwd(dh2 * g2v, xn1, r2)

    row = lambda w: pl.BlockSpec((tm, w), lambda i: (i, 0))
    return pl.pallas_call(
        body, name="ffn_fwd_bwd", grid=(t // tm,),
        in_specs=[row(D_MODEL), row(D_MODEL), _const_spec((1, D_MODEL)), _const_spec((1, D_MODEL)),
                  _const_spec((D_FF, D_MODEL)), _const_spec((D_FF, D_MODEL)), _const_spec((D_FF, D_MODEL))],
        out_specs=(row(D_MODEL), row(D_MODEL), row(D_FF), row(D_FF), row(D_FF), row(D_MODEL),
                   _acc_spec((8, LANE)), _acc_spec((1, D_MODEL)), _acc_spec((1, D_MODEL))),
        out_shape=(jax.ShapeDtypeStruct((t, D_MODEL), F32), jax.ShapeDtypeStruct((t, D_MODEL), BF16),
                   jax.ShapeDtypeStruct((t, D_FF), BF16), jax.ShapeDtypeStruct((t, D_FF), BF16),
                   jax.ShapeDtypeStruct((t, D_FF), BF16), jax.ShapeDtypeStruct((t, D_MODEL), BF16),
                   jax.ShapeDtypeStruct((8, LANE), F32), jax.ShapeDtypeStruct((1, D_MODEL), F32),
                   jax.ShapeDtypeStruct((1, D_MODEL), F32)),
        compiler_params=_params(),
    )(x1, target, g2, gf, w_gate, w_up, w_down)


def _matmul_tn(a, b, tm, tk, name, comms=()):
    t, m = a.shape
    n = b.shape[1]

    def body(a_ref, b_ref, o_ref):
        @pl.when(pl.program_id(1) == 0)
        def _():
            o_ref[...] = jnp.zeros_like(o_ref)

        o_ref[...] += _mm_tn(a_ref[...], b_ref[...])

    (out,), comm_results = _fused_call(
        body, comms, name=name, grid=(m // tm, t // tk), inputs=(a, b),
        in_specs=[pl.BlockSpec((tk, tm), lambda j, k: (k, j)), pl.BlockSpec((tk, n), lambda j, k: (k, 0))],
        out_specs=(pl.BlockSpec((tm, n), lambda j, k: (j, 0)),),
        out_shape=(jax.ShapeDtypeStruct((m, n), F32),))
    return out, comm_results


def _in_proj_bwd(x, g1, dx1, dq_f, dq_b, dk_f, dk_b, dv_f, dv_b, dpg, dpu, dpv, dlr_f, dlr_b, w_main, tm, comms=()):
    t = x.shape[0]

    def body(x_ref, g_ref, dx1_ref, dqf, dqb, dkf, dkb, dvf, dvb, dg, du, dv, dlf, dlb, w_ref,
             dx_ref, dp_ref, dg1_ref):
        @pl.when(pl.program_id(0) == 0)
        def _():
            dg1_ref[...] = jnp.zeros_like(dg1_ref)

        both = lambda a, b: (a[...].astype(F32) + b[...].astype(F32)).astype(BF16)
        dp = jnp.concatenate([both(dqf, dqb), both(dkf, dkb), both(dvf, dvb), dg[...], du[...], dv[...],
                              both(dlf, dlb)], axis=1)
        dp_ref[...] = dp
        dh = sum(_mm(dp[:, c0:c0 + r1 - r0], w_ref[r0:r1, :]) for r0, r1, c0 in PROJ_ROWS)
        xv = x_ref[...]
        r = lax.rsqrt(jnp.mean(xv * xv, axis=-1, keepdims=True) + EPS)
        xn = xv * r
        dg1_ref[...] += jnp.sum(dh * xn, axis=0, keepdims=True)
        dx_ref[...] = dx1_ref[...] + _rms_bwd(dh * g_ref[...], xn, r)

    row = lambda w: pl.BlockSpec((tm, w), lambda i: (i, 0))
    return _fused_call(
        body, comms, name="in_proj_bwd", grid=(t // tm,),
        inputs=(x, g1, dx1, dq_f, dq_b, dk_f, dk_b, dv_f, dv_b, dpg, dpu, dpv, dlr_f, dlr_b, w_main),
        in_specs=[row(D_MODEL), _const_spec((1, D_MODEL)), row(D_MODEL), row(KEY_W), row(KEY_W), row(KEY_W),
                  row(KEY_W), row(VAL_W), row(VAL_W), row(VAL_W), row(VAL_W), row(VAL_W), row(LANE), row(LANE),
                  _const_spec((PROJ_W, D_MODEL))],
        out_specs=(row(D_MODEL), row(PROJ_PAD), _acc_spec((1, D_MODEL))),
        out_shape=(jax.ShapeDtypeStruct((t, D_MODEL), F32), jax.ShapeDtypeStruct((t, PROJ_PAD), BF16),
                   jax.ShapeDtypeStruct((1, D_MODEL), F32)))


def _adamw(w, g, m, v):
    m_new = ADAM_B1 * m + (1.0 - ADAM_B1) * g
    v_new = ADAM_B2 * v + (1.0 - ADAM_B2) * (g * g)
    m_hat = m_new / (1.0 - ADAM_B1 ** ADAM_STEP)
    v_hat = v_new / (1.0 - ADAM_B2 ** ADAM_STEP)
    delta = -ADAM_LR * (m_hat / (jnp.sqrt(v_hat) + ADAM_EPS) + ADAM_WD * w)
    return delta, m_new, v_new


def _adamw_window(own, recv, w, m, v, name):
    r, c = w.shape
    rows = own.shape[0]

    def body(own_ref, recv_ref, w_ref, m_ref, v_ref, g_ref, d_ref, nm_ref, nv_ref):
        g = own_ref[...]
        for k in range(3):
            g = g + recv_ref[k].astype(F32)

        def update(g):
            g_ref[...] = g[:r]
            d_ref[...], nm_ref[...], nv_ref[...] = _adamw(w_ref[...], g[:r], m_ref[...], v_ref[...])

        core = lax.axis_index("c")
        pl.when(core == 0)(lambda: update(g))
        pl.when(core == 1)(lambda: update(pltpu.roll(g, rows - 4, 0)))

    whole = lambda *shape: pl.BlockSpec(shape, lambda i: (0,) * len(shape))
    return pl.pallas_call(
        body, name=name, grid=(1,),
        in_specs=[whole(rows, c), whole(3, rows, c), whole(r, c), whole(r, c), whole(r, c)],
        out_specs=(whole(r, c),) * 4, out_shape=(jax.ShapeDtypeStruct((r, c), F32),) * 4,
        compiler_params=_params(),
    )(own, recv, w, m, v)


def _adamw_shard(own, recv, w, m, v, tr, name):
    r, c = w.shape

    def body(own_ref, recv_ref, w_ref, m_ref, v_ref, g_ref, d_ref, nm_ref, nv_ref):
        g = own_ref[...]
        for k in range(3):
            g = g + recv_ref[k].astype(F32)
        g_ref[...] = g
        d_ref[...], nm_ref[...], nv_ref[...] = _adamw(w_ref[...], g, m_ref[...], v_ref[...])

    row = pl.BlockSpec((tr, c), lambda i: (i, 0))
    return pl.pallas_call(
        body, name=name, grid=(r // tr,),
        in_specs=[row, pl.BlockSpec((3, tr, c), lambda i: (0, i, 0)), row, row, row],
        out_specs=(row,) * 4, out_shape=(jax.ShapeDtypeStruct((r, c), F32),) * 4,
        compiler_params=_params(),
    )(own, recv, w, m, v)


def _adamw_small(entries):
    stacks = []
    for (g, _, _), _, _, _ in entries:
        if not any(g is s for s in stacks):
            stacks.append(g)
    where = [next(i for i, s in enumerate(stacks) if s is g) for (g, _, _), _, _, _ in entries]
    ns, ne = len(stacks), len(entries)

    def body(*refs):
        s_refs, wmv, outs = refs[:ns], refs[ns:ns + 3 * ne], refs[ns + 3 * ne:]
        for e, ((_, r0, nr), _, _, _) in enumerate(entries):
            grad = s_refs[where[e]][r0:r0 + nr, :]
            w_ref, m_ref, v_ref = wmv[3 * e:3 * e + 3]
            g_ref, d_ref, nm_ref, nv_ref = outs[4 * e:4 * e + 4]
            g_ref[...] = grad
            d_ref[...], nm_ref[...], nv_ref[...] = _adamw(w_ref[...], grad, m_ref[...], v_ref[...])

    results = pl.pallas_call(
        body, name="adamw_small",
        out_shape=tuple(jax.ShapeDtypeStruct(w.shape, F32) for _, w, _, _ in entries for _ in range(4)),
        compiler_params=pltpu.CompilerParams(vmem_limit_bytes=VMEM_LIMIT),
    )(*stacks, *[a for _, w, m, v in entries for a in (w, m, v)])
    return [results[4 * e:4 * e + 4] for e in range(ne)]


def _mesh_pos():
    return lax.axis_index("x"), lax.axis_index("y"), lax.axis_index("c")


def _other_chips(x, y):
    return [(x, 1 - y), (1 - x, y), (1 - x, 1 - y)]


_VMEM_WHOLE = pl.BlockSpec(memory_space=pltpu.VMEM)
_HBM_WHOLE = pl.BlockSpec(memory_space=pl.ANY)


def _gather_comm(shards, cast, mid=((1, 2), (3, 4))):
    na = len(shards)
    staged = [a for a in range(na) if cast[a]]

    def phases(in_refs, out_refs, scr):
        stage = dict(zip(staged, scr[:len(staged)]))
        send_sems, recv_sems, local_sems = scr[len(staged):]
        x, y, c = _mesh_pos()
        me, sibling = (x, y, c), (x, y, 1 - c)
        chip_a, chip_b, diagonal = (x ^ c, y ^ (1 - c)), (x ^ (1 - c), y ^ c), (1 - x, 1 - y)
        srcs = [stage[a] if cast[a] else in_refs[a] for a in range(na)]

        def rows(a, pos):
            px, py, pc = pos
            return out_refs[a].at[4 * px + 2 * py + pc]

        def copy(a, k, block, to, src=None):
            return pltpu.make_async_remote_copy(
                src_ref=rows(a, block) if src is None else src, dst_ref=rows(a, block),
                send_sem=send_sems.at[a, k], recv_sem=recv_sems.at[a, k], device_id=to, device_id_type=MESH_ID)

        mine = [pltpu.make_async_copy(srcs[a], rows(a, me), local_sems.at[a]) for a in range(na)]
        own = [copy(a, k, me, to, src=srcs[a]) for a in range(na)
               for k, to in ((0, sibling), (1, (*chip_a, c)), (2, (*chip_b, c)))]
        onward = [copy(a, 3, (*chip_a, c), (*chip_b, c)) for a in range(na)]
        to_sibling = {k: [copy(a, k, (*chip, c), sibling) for a in range(na)]
                      for k, chip in ((4, chip_a), (5, chip_b), (6, diagonal))}

        def start():
            for a in staged:
                stage[a][...] = in_refs[a][...].astype(BF16)
            for cp in mine + own:
                cp.start()

        def forward_neighbours():
            for a in range(na):
                copy(a, 1, (*chip_a, c), me).wait_recv()
                onward[a].start()
                to_sibling[4][a].start()
            for a in range(na):
                copy(a, 2, (*chip_b, c), me).wait_recv()
                to_sibling[5][a].start()

        def forward_diagonal():
            for a in range(na):
                copy(a, 3, (*diagonal, c), me).wait_recv()
                to_sibling[6][a].start()

        def finish():
            for a in range(na):
                for k, chip in ((0, (x, y)), (4, chip_b), (5, chip_a), (6, diagonal)):
                    copy(a, k, (*chip, 1 - c), me).wait_recv()
            for cp in own + onward + to_sibling[4] + to_sibling[5] + to_sibling[6]:
                cp.wait_send()
            for cp in mine:
                cp.wait()

        return start, forward_neighbours, forward_diagonal, finish

    def before(step, nsteps, in_refs, out_refs, scr):
        start, forward_neighbours, forward_diagonal, _ = phases(in_refs, out_refs, scr)
        pl.when(step == 0)(start)
        pl.when(step == nsteps * mid[0][0] // mid[0][1])(forward_neighbours)
        pl.when(step == nsteps * mid[1][0] // mid[1][1])(forward_diagonal)

    def after(step, nsteps, in_refs, out_refs, scr):
        pl.when(step == nsteps - 1)(phases(in_refs, out_refs, scr)[3])

    return _Comm(
        inputs=list(shards), in_specs=[_VMEM_WHOLE] * na,
        out_shape=[jax.ShapeDtypeStruct((N_DEV,) + s.shape, BF16 if cast[a] else s.dtype)
                   for a, s in enumerate(shards)],
        out_specs=[_HBM_WHOLE] * na,
        scratch_shapes=[pltpu.VMEM(shards[a].shape, BF16) for a in staged] + [
            pltpu.SemaphoreType.DMA((na, 7)), pltpu.SemaphoreType.DMA((na, 7)), pltpu.SemaphoreType.DMA((na,))],
        before=before, after=after)


W_IN_WINDOW = 336


def _w_in_block_pieces(g_ref, chip, core):
    j = 2 * chip + core
    rows = PROJ_W // N_DEV
    first = rows * j - jnp.where(j > 4, 2 * LOWRANK, 0)
    start = pl.multiple_of((first >> 3) << 3, 8)
    head = LR_REF - 4 * rows
    split = [(g_ref.at[pl.ds(4 * rows, head)], 0, head), (g_ref.at[pl.ds(LR_COL, 2 * LOWRANK)], head, 2 * LOWRANK),
             (g_ref.at[pl.ds(LR_REF, 64)], head + 2 * LOWRANK, 64)]
    return [(j != 4, [(g_ref.at[pl.ds(start, W_IN_WINDOW)], 0, W_IN_WINDOW)]), (j == 4, split)]


def _reduce_scatter_comm(grads, rows=None, pieces=None):
    if pieces is None:
        _, _, r, c = grads.shape
        pieces = lambda g_ref, chip, core: [(None, [(g_ref.at[chip, core], 0, r)])]
    else:
        r, c = rows, grads.shape[1]
    order = (3, 1, 2, 0)

    def transfer(k, kind, in_refs, scr, act):
        (g_ref,), (sib, own, _, sems) = in_refs, scr
        x, y, core = _mesh_pos()
        chip = (2 * x + y) ^ k
        for cond, parts in pieces(g_ref, chip, 1 - core if kind == "send" else core):
            def run(parts=parts):
                for i, (src, row0, n) in enumerate(parts):
                    if kind == "local":
                        act(pltpu.make_async_copy(src, own.at[k, pl.ds(row0, n)], sems.at[2, 3 * k + i]))
                    else:
                        act(pltpu.make_async_remote_copy(
                            src_ref=src, dst_ref=sib.at[k, pl.ds(row0, n)], send_sem=sems.at[0, 3 * k + i],
                            recv_sem=sems.at[1, 3 * k + i], device_id=(x, y, 1 - core), device_id_type=MESH_ID))

            run() if cond is None else pl.when(cond)(run)

    def chip_copies(out_refs, scr):
        (_, recv), (_, _, part, sems) = out_refs, scr
        x, y, core = _mesh_pos()
        return [pltpu.make_async_remote_copy(
            src_ref=part.at[j], dst_ref=recv.at[j], send_sem=sems.at[3, j], recv_sem=sems.at[4, j],
            device_id=(*chip, core), device_id_type=MESH_ID) for j, chip in enumerate(_other_chips(x, y))]

    def before(step, nsteps, in_refs, out_refs, scr):
        @pl.when(step == 0)
        def _():
            for k in order:
                transfer(k, "send", in_refs, scr, lambda cp: cp.start())
                transfer(k, "local", in_refs, scr, lambda cp: cp.start(priority=1))

    def after(step, nsteps, in_refs, out_refs, scr):
        sib, own, part, _ = scr

        @pl.when(step == (nsteps - 1) // 2)
        def _():
            to_chips = chip_copies(out_refs, scr)
            for k in order:
                transfer(k, "recv", in_refs, scr, lambda cp: cp.wait_recv())
                transfer(k, "local", in_refs, scr, lambda cp: cp.wait())
                if k:
                    part[k - 1] = (own[k] + sib[k]).astype(BF16)
                    to_chips[k - 1].start()
                else:
                    out_refs[0][...] = own[0] + sib[0]
            for k in order:
                transfer(k, "send", in_refs, scr, lambda cp: cp.wait_send())

        @pl.when(step == nsteps - 1)
        def _():
            for cp in chip_copies(out_refs, scr):
                cp.wait()

    return _Comm(inputs=[grads], in_specs=[_HBM_WHOLE],
                 out_shape=[jax.ShapeDtypeStruct((r, c), F32), jax.ShapeDtypeStruct((3, r, c), BF16)],
                 out_specs=[_VMEM_WHOLE, _HBM_WHOLE],
                 scratch_shapes=[pltpu.VMEM((4, r, c), F32), pltpu.VMEM((4, r, c), F32), pltpu.VMEM((3, r, c), BF16),
                                 pltpu.SemaphoreType.DMA((5, 12))],
                 before=before, after=after)


def _exchange_comm(arrays, out_shape, make_copies):
    na = len(arrays)

    def copies(in_refs, out_refs, scr):
        return make_copies(in_refs, out_refs, *scr)

    def before(step, nsteps, in_refs, out_refs, scr):
        @pl.when(step == 0)
        def _():
            for cp in copies(in_refs, out_refs, scr):
                cp.start()

    def after(step, nsteps, in_refs, out_refs, scr):
        @pl.when(step == nsteps - 1)
        def _():
            for cp in copies(in_refs, out_refs, scr):
                cp.wait()

    return _Comm(inputs=list(arrays), in_specs=[_HBM_WHOLE] * na, out_shape=list(out_shape),
                 out_specs=[_HBM_WHOLE] * na,
                 scratch_shapes=[pltpu.SemaphoreType.DMA((na, 3)), pltpu.SemaphoreType.DMA((na, 3))],
                 before=before, after=after)


def _sibling_exchange_comm(grads):
    def make_copies(in_refs, out_refs, send_sems, recv_sems):
        x, y, c = _mesh_pos()
        return [pltpu.make_async_remote_copy(
            src_ref=in_refs[a].at[:, pl.ds(1 - c, 1)], dst_ref=out_refs[a], send_sem=send_sems.at[a, 0],
            recv_sem=recv_sems.at[a, 0], device_id=(x, y, 1 - c), device_id_type=MESH_ID)
            for a in range(len(grads))]

    return _exchange_comm(grads, [jax.ShapeDtypeStruct((4, 1) + g.shape[2:], F32) for g in grads], make_copies)


def _chip_sum(my_pos, mine, from_sibling, tr, name):
    _, _, r, c = mine.shape

    def body(pos_ref, a_ref, b_ref, own_ref, out_ref):
        s = a_ref[0, 0] + b_ref[0, 0]

        @pl.when(pl.program_id(1) == 0)
        def _():
            own_ref[...] = s

        @pl.when(pl.program_id(1) > 0)
        def _():
            out_ref[0] = s.astype(BF16)

    grid_spec = pltpu.PrefetchScalarGridSpec(
        num_scalar_prefetch=1, grid=(r // tr, 4),
        in_specs=[pl.BlockSpec((1, 1, tr, c), lambda i, k, pos: (pos[0] ^ k, pos[1], i, 0)),
                  pl.BlockSpec((1, 1, tr, c), lambda i, k, pos: (pos[0] ^ k, 0, i, 0))],
        out_specs=(pl.BlockSpec((tr, c), lambda i, k, pos: (i, 0)),
                   pl.BlockSpec((1, tr, c), lambda i, k, pos: (jnp.maximum(k - 1, 0), i, 0))))
    return pl.pallas_call(
        body, name=name, grid_spec=grid_spec,
        out_shape=(jax.ShapeDtypeStruct((r, c), F32), jax.ShapeDtypeStruct((3, r, c), BF16)),
        compiler_params=_params(2),
    )(my_pos, mine, from_sibling)


def _chips_exchange_comm(partials):
    def make_copies(in_refs, out_refs, send_sems, recv_sems):
        x, y, c = _mesh_pos()
        return [pltpu.make_async_remote_copy(
            src_ref=in_refs[a].at[j], dst_ref=out_refs[a].at[j], send_sem=send_sems.at[a, j],
            recv_sem=recv_sems.at[a, j], device_id=(*chip, c), device_id_type=MESH_ID)
            for a in range(len(partials)) for j, chip in enumerate(_other_chips(x, y))]

    return _exchange_comm(partials, [jax.ShapeDtypeStruct(g.shape, BF16) for g in partials], make_copies)


def _comm_only(comms, name):
    return _fused_call(lambda: None, comms, name=name, grid=(1,), inputs=(), in_specs=[], out_specs=(),
                       out_shape=())[1]


def _all_reduce_small_comm(parts):
    na = len(parts)

    def copies(in_refs, scr):
        gathered, (send_sems, recv_sems) = scr[:na], scr[na:]
        x, y, c = _mesh_pos()
        my_id = 4 * x + 2 * y + c
        return my_id, [pltpu.make_async_remote_copy(
            src_ref=in_refs[a], dst_ref=gathered[a].at[my_id], send_sem=send_sems.at[a, k - 1],
            recv_sem=recv_sems.at[a, k - 1], device_id=(x ^ (k >> 2), y ^ ((k >> 1) & 1), c ^ (k & 1)),
            device_id_type=MESH_ID) for a in range(na) for k in range(1, N_DEV)]

    def before(step, nsteps, in_refs, out_refs, scr):
        @pl.when(step == 0)
        def _():
            for cp in copies(in_refs, scr)[1]:
                cp.start()

    def after(step, nsteps, in_refs, out_refs, scr):
        @pl.when(step == nsteps - 1)
        def _():
            my_id, cps = copies(in_refs, scr)
            for a in range(na):
                scr[a][my_id] = in_refs[a][...]
            for cp in cps:
                cp.wait()
            for a in range(na):
                acc = scr[a][0]
                for d in range(1, N_DEV):
                    acc = acc + scr[a][d]
                out_refs[a][...] = acc

    return _Comm(inputs=list(parts), in_specs=[_VMEM_WHOLE] * na,
                 out_shape=[jax.ShapeDtypeStruct(p.shape, F32) for p in parts], out_specs=[_VMEM_WHOLE] * na,
                 scratch_shapes=[pltpu.VMEM((N_DEV,) + p.shape, F32) for p in parts] + [
                     pltpu.SemaphoreType.DMA((na, N_DEV - 1)), pltpu.SemaphoreType.DMA((na, N_DEV - 1))],
                 before=before, after=after)


def _unshard_cols(g):
    return jnp.transpose(g, (1, 0, 2)).reshape(g.shape[1], N_DEV * g.shape[2])


def _row_blocks(w):
    return w.reshape(4, 2, w.shape[0] // N_DEV, w.shape[1])


def _stack_rows(parts):
    a = jnp.concatenate(parts, axis=0)
    return jnp.pad(a, ((0, (-a.shape[0]) % 8), (0, 0)))


def _padded_decay_weights(wd_f, wd_b):
    zeros = lambda n: jnp.zeros((n, KEY_W), F32)
    return (jnp.concatenate([wd_f, zeros(LANE - LOWRANK)], axis=0),
            jnp.concatenate([zeros(LOWRANK), wd_b, zeros(LANE - 2 * LOWRANK)], axis=0))


def kernel(x, norm1_g, w_in,w_decay_f, b_decay_f, w_decay_b, b_decay_b, gla_norm_g, gmlp_ln_g, gmlp_ln_b, w_spatial, b_spatial, w_out, norm2_g, w_gate, w_up, w_down, final_norm_g, loss_target, m_norm1_g, m_w_in, m_w_decay_f, m_b_decay_f, m_w_decay_b, m_b_decay_b, m_gla_norm_g, m_gmlp_ln_g, m_gmlp_ln_b, m_w_spatial, m_b_spatial, m_w_out, m_norm2_g, m_w_gate, m_w_up, m_w_down, m_final_norm_g, v_norm1_g, v_w_in, v_w_decay_f, v_b_decay_f, v_w_decay_b, v_b_decay_b, v_gla_norm_g, v_gmlp_ln_g, v_gmlp_ln_b, v_w_spatial, v_b_spatial, v_w_out, v_norm2_g, v_w_gate, v_w_up, v_w_down, v_final_norm_g):
    t = x.shape[1]
    xt = x[0]
    target = loss_target[0]
    pos_x, pos_y, pos_c = _mesh_pos()
    my_pos = jnp.stack([2 * pos_x + pos_y, pos_c]).astype(jnp.int32)
    my_id = 4 * pos_x + 2 * pos_y + pos_c

    tile = lambda n: min(n, t)
    ln_g, ln_b, w_sp = gmlp_ln_g, gmlp_ln_b, w_spatial[0]
    b_sp_col = b_spatial[0][:, :, None]
    shard = {"w_in": w_in[0].T, "w_out": w_out[0], "w_gate": w_gate[0].T, "w_up": w_up[0].T, "w_down": w_down[0]}
    shard_m = {"w_in": m_w_in[0].T, "w_out": m_w_out[0], "w_gate": m_w_gate[0].T, "w_up": m_w_up[0].T,
               "w_down": m_w_down[0]}
    shard_v = {"w_in": v_w_in[0].T, "w_out": v_w_out[0], "w_gate": v_w_gate[0].T, "w_up": v_w_up[0].T,
               "w_down": v_w_down[0]}
    transposed = ("w_in", "w_gate", "w_up")

    decay_shard = jnp.stack([w_decay_f[0], w_decay_b[0]])
    (hb,), ((g_in, g_decay),) = _norm1(xt, norm1_g, tile(TOKEN_TILE["norm1"]),
                                       [_gather_comm([shard["w_in"], decay_shard], [True, False])])
    w_in_t = g_in.reshape(PROJ_W, D_MODEL)
    wd_pad_f, wd_pad_b = _padded_decay_weights(_unshard_cols(g_decay[:, 0]), _unshard_cols(g_decay[:, 1]))
    (p,), ((g_gate, g_out),) = _in_proj(
        hb, w_in_t, tile(TOKEN_TILE["in_proj"]), [_gather_comm([shard["w_gate"], shard["w_out"]], [True, True])])
    (o_f, st_f, o_b, st_b), ((g_up,),) = _gla_fwd(
        p, wd_pad_f, b_decay_f, wd_pad_b, b_decay_b, tile(TOKEN_TILE["gla"]), [_gather_comm([shard["w_up"]], [True])])
    w_out_full = g_out.reshape(D_MODEL, D_MODEL)
    (x1, ycat), ((g_down,),) = _mix_fwd(xt, o_f, o_b, p, gla_norm_g, ln_g, ln_b, w_sp, b_sp_col, w_out_full,
                                        tile(TOKEN_TILE["mix_fwd"]), [_gather_comm([shard["w_down"]], [True])])

    dx1, h2b, dgate, dup, act, dx2, loss_acc, d_gf, d_g2 = _ffn(
        x1, target, norm2_g, final_norm_g[None, :], g_gate.reshape(D_FF, D_MODEL), g_up.reshape(D_FF, D_MODEL),
        g_down.reshape(D_FF, D_MODEL), tile(TOKEN_TILE["ffn"]))
    dw_gate, _ = _matmul_tn(dgate, h2b, D_FF // 2, tile(TOKEN_TILE["dw"]), "grad_w_gate")
    dw_up, _ = _matmul_tn(dup, h2b, D_FF // 2, tile(TOKEN_TILE["dw"]), "grad_w_up")
    dw_down, _ = _matmul_tn(act, dx2, D_FF // 2, tile(TOKEN_TILE["dw"]), "grad_w_down")

    reduced = {}
    staged = {"w_gate": _row_blocks(dw_gate), "w_up": _row_blocks(dw_up)}
    (d_o, dpg, dpu, dpv, dw_out, d_gg, d_lg, d_lb, dw_sp, db_sp), (reduced["w_down"], staged_sib) = _mix_bwd(
        dx1, ycat, o_f, o_b, p, gla_norm_g, ln_g, ln_b, w_sp, b_sp_col, w_out_full,
        tile(TOKEN_TILE["mix_bwd"]),
        [_reduce_scatter_comm(_row_blocks(dw_down)), _sibling_exchange_comm(list(staged.values()))])
    staged_sums = [_chip_sum(my_pos, g, s, g.shape[2], "chip_sum_" + n)
                   for (n, g), s in zip(staged.items(), staged_sib)]
    (dq_f, dk_f, dv_f, dlr_f, dwd_f, dbd_f, dq_b, dk_b, dv_b, dlr_b, dwd_b, dbd_b), (reduced["w_out"], staged_recv) = (
        _gla_bwd(p, wd_pad_f, b_decay_f, wd_pad_b, b_decay_b, st_f, st_b, d_o, tile(TOKEN_TILE["gla"]),
                 [_reduce_scatter_comm(_row_blocks(dw_out)), _chips_exchange_comm([s[1] for s in staged_sums])]))
    for n, s, rc in zip(staged, staged_sums, staged_recv):
        reduced[n] = (s[0], rc)
    (grad_x, dp, d_g1), _ = _in_proj_bwd(
        xt, norm1_g, dx1, dq_f, dq_b, dk_f, dk_b, dv_f, dv_b, dpg, dpu, dpv, dlr_f, dlr_b, w_in_t,
        tile(TOKEN_TILE["in_proj_bwd"]))

    stacks = [_stack_rows([d_g1, d_g2, d_gf]), _stack_rows([d_gg, d_lg, d_lb]),
              _stack_rows([dbd_f, dbd_b, jnp.zeros((DECAY_W_ROW - 2, KEY_W), F32), dwd_f[:LOWRANK],
                           dwd_b[LOWRANK:2 * LOWRANK]]),
              _stack_rows([dw_sp.reshape(GMLP_W, GMLP_CHUNK), db_sp[:, :, 0], loss_acc[:1]])]
    dw_main, (small_sums,) = _matmul_tn(dp, hb, PROJ_PAD // 3, tile(TOKEN_TILE["dw"]), "grad_w_in",
                                        [_all_reduce_small_comm(stacks)])
    ((in_own, in_recv),) = _comm_only(
        [_reduce_scatter_comm(dw_main, W_IN_WINDOW, _w_in_block_pieces)], "grad_w_in_reduce_scatter")

    big_out = {"w_in": [r.T for r in _adamw_window(in_own, in_recv, shard["w_in"], shard_m["w_in"], shard_v["w_in"],
                                                   "adamw_w_in")]}
    for n, (own_sum, recv) in reduced.items():
        rows = shard[n].shape[0]
        half = rows // 2 if rows % 32 == 0 else rows
        res = _adamw_shard(own_sum, recv, shard[n], shard_m[n], shard_v[n], half, "adamw_" + n)
        big_out[n] = [r.T if n in transposed else r for r in res]

    s1024, s512, s256, s128 = small_sums
    loss = s128[GMLP_W + GMLP_GROUPS, 0]
    col0 = my_id * (KEY_W // N_DEV)
    decay_cols = lambda row0: lax.dynamic_slice(s256, (row0, col0), (LOWRANK, KEY_W // N_DEV))
    flat = lambda a: a.reshape(-1, a.shape[-1])
    small = {
        "norm1_g": ((s1024, 0, 1), norm1_g, m_norm1_g, v_norm1_g),
        "w_decay_f": ((decay_cols(DECAY_W_ROW), 0, LOWRANK), w_decay_f, m_w_decay_f, v_w_decay_f),
        "b_decay_f": ((s256, 0, 1), b_decay_f, m_b_decay_f, v_b_decay_f),
        "w_decay_b": ((decay_cols(DECAY_W_ROW + LOWRANK), 0, LOWRANK), w_decay_b, m_w_decay_b, v_w_decay_b),
        "b_decay_b": ((s256, 1, 1), b_decay_b, m_b_decay_b, v_b_decay_b),
        "gla_norm_g": ((s512, 0, 1), gla_norm_g, m_gla_norm_g, v_gla_norm_g),
        "gmlp_ln_g": ((s512, 1, 1), gmlp_ln_g, m_gmlp_ln_g, v_gmlp_ln_g),
        "gmlp_ln_b": ((s512, 2, 1), gmlp_ln_b, m_gmlp_ln_b, v_gmlp_ln_b),
        "w_spatial": ((s128, 0, GMLP_W), w_spatial, m_w_spatial, v_w_spatial),
        "b_spatial": ((s128, GMLP_W, GMLP_GROUPS), b_spatial, m_b_spatial, v_b_spatial),
        "norm2_g": ((s1024, 1, 1), norm2_g, m_norm2_g, v_norm2_g),
        "final_norm_g": ((s1024, 2, 1), final_norm_g, m_final_norm_g, v_final_norm_g),
    }
    small_res = _adamw_small([(g, flat(w), flat(m), flat(v)) for g, w, m, v in small.values()])
    small_out = {n: [r.reshape(small[n][1].shape) for r in res] for n, res in zip(small, small_res)}

    order = ["norm1_g", "w_in", "w_decay_f", "b_decay_f", "w_decay_b", "b_decay_b", "gla_norm_g", "gmlp_ln_g",
             "gmlp_ln_b", "w_spatial", "b_spatial", "w_out", "norm2_g", "w_gate", "w_up", "w_down", "final_norm_g"]
    outs = []
    for kind in range(4):
        for n in order:
            outs.append(big_out[n][kind][None] if n in big_out else small_out[n][kind])
    return (loss, grad_x[None], *outs)
```

```python
import functools
import math

import jax
import jax.numpy as jnp
from jax import lax
from jax.experimental import pallas as pl
from jax.experimental.pallas import tpu as pltpu

F32 = jnp.float32
BF16 = jnp.bfloat16

D_MODEL = 1024
GLA_HEADS = 4
GLA_DK = 64
GLA_DV = 128
KEY_W = GLA_HEADS * GLA_DK
VAL_W = GLA_HEADS * GLA_DV
LOWRANK = 16
GLA_TAU = 16.0
GLA_CHUNK = 64
GMLP_W = 512
GMLP_GROUPS = 4
GMLP_CHUNK = 128
D_FF = 2816
EPS = 1e-6
Q_SCALE = GLA_DK ** -0.5
PROJ_PAD = 2688
LR_COL = 2560
LANE = 128
N_DEV = 8

ADAM_LR = 0.001
ADAM_B1 = 0.9
ADAM_B2 = 0.999
ADAM_EPS = 1e-08
ADAM_WD = 0.01
ADAM_STEP = 10

VMEM_LIMIT = 56 * 1024 * 1024
TOKEN_TILE = {"norm1": 512, "in_proj": 1024, "gla": 1024, "mix_fwd": 1024, "ffn": 256, "mix_bwd": 512,
              "in_proj_bwd": 512, "dw": 2048}
DECAY_W_ROW = 8
MESH_ID = pl.DeviceIdType.MESH
INV_SQRT2 = 0.7071067811865476
INV_SQRT_2PI = 0.3989422804014327


def _params(n_axes=1):
    return pltpu.CompilerParams(dimension_semantics=("arbitrary",) * n_axes, vmem_limit_bytes=VMEM_LIMIT)


def _mm(a, b):
    return jnp.dot(a.astype(BF16), b.astype(BF16), preferred_element_type=F32)


def _mm_nt(a, b):
    return lax.dot_general(a.astype(BF16), b.astype(BF16), (((1,), (1,)), ((), ())), preferred_element_type=F32)


def _mm_tn(a, b):
    return lax.dot_general(a.astype(BF16), b.astype(BF16), (((0,), (0,)), ((), ())), preferred_element_type=F32)


def _const_spec(shape):
    nd = len(shape)
    return pl.BlockSpec(shape, lambda *_: (0,) * nd, pipeline_mode=pl.Buffered(1))


def _acc_spec(shape):
    nd = len(shape)
    return pl.BlockSpec(shape, lambda *_: (0,) * nd)


class _Comm:
    def __init__(self, inputs, in_specs, out_shape, out_specs, scratch_shapes, before, after):
        self.inputs, self.in_specs, self.out_shape, self.out_specs = inputs, in_specs, out_shape, out_specs
        self.scratch_shapes, self.before, self.after = scratch_shapes, before, after


def _fused_call(body, comms, *, name, grid, inputs, in_specs, out_specs, out_shape, scratch_shapes=()):
    n_in, n_out, n_scr = len(in_specs), len(out_specs), len(scratch_shapes)
    nsteps = math.prod(grid)
    sizes = [(len(c.inputs), len(c.out_shape), len(c.scratch_shapes)) for c in comms]

    def full_body(*refs):
        step = pl.program_id(0)
        for axis in range(1, len(grid)):
            step = step * grid[axis] + pl.program_id(axis)
        ins, rest = refs[:n_in], refs[n_in:]
        c_ins = []
        for ci, _, _ in sizes:
            c_ins.append(rest[:ci])
            rest = rest[ci:]
        outs, rest = rest[:n_out], rest[n_out:]
        c_outs = []
        for _, co, _ in sizes:
            c_outs.append(rest[:co])
            rest = rest[co:]
        scr, rest = rest[:n_scr], rest[n_scr:]
        c_scr = []
        for _, _, cs in sizes:
            c_scr.append(rest[:cs])
            rest = rest[cs:]
        for c, a, b, s in zip(comms, c_ins, c_outs, c_scr):
            c.before(step, nsteps, a, b, s)
        body(*ins, *outs, *scr)
        for c, a, b, s in zip(comms, c_ins, c_outs, c_scr):
            c.after(step, nsteps, a, b, s)

    results = pl.pallas_call(
        full_body, name=name, grid=grid,
        in_specs=list(in_specs) + [s for c in comms for s in c.in_specs],
        out_specs=tuple(out_specs) + tuple(s for c in comms for s in c.out_specs),
        out_shape=tuple(out_shape) + tuple(s for c in comms for s in c.out_shape),
        scratch_shapes=list(scratch_shapes) + [s for c in comms for s in c.scratch_shapes],
        compiler_params=_params(len(grid)),
    )(*inputs, *[a for c in comms for a in c.inputs])
    own, rest = results[:n_out], results[n_out:]
    comm_results = []
    for _, co, _ in sizes:
        comm_results.append(rest[:co])
        rest = rest[co:]
    return own, comm_results


def _gelu(x):
    return 0.5 * x * (1.0 + lax.erf(x * INV_SQRT2))


def _gelu_and_grad(x):
    cdf = 0.5 * (1.0 + lax.erf(x * INV_SQRT2))
    return x * cdf, cdf + x * jnp.exp(-0.5 * x * x) * INV_SQRT_2PI


def _sigmoid(x):
    return 0.5 + 0.5 * jnp.tanh(0.5 * x)


def _silu_and_grad(x):
    s = _sigmoid(x)
    return x * s, s * (1.0 + x * (1.0 - s))


def _norm1(x, g1, tm, comms=()):
    t = x.shape[0]

    def body(x_ref, g_ref, h_ref):
        xv = x_ref[...]
        r = lax.rsqrt(jnp.mean(xv * xv, axis=-1, keepdims=True) + EPS)
        h_ref[...] = (xv * r * g_ref[...]).astype(BF16)

    row = pl.BlockSpec((tm, D_MODEL), lambda i: (i, 0))
    return _fused_call(body, comms, name="norm1", grid=(t // tm,), inputs=(x, g1),
                       in_specs=[row, _const_spec((1, D_MODEL))], out_specs=(row,),
                       out_shape=(jax.ShapeDtypeStruct((t, D_MODEL), BF16),))


PROJ_W = 2592
LR_REF = 1536
PROJ_ROWS = ((0, LR_REF, 0), (LR_REF + 2 * LOWRANK, PROJ_W, LR_REF), (LR_REF, LR_REF + LANE, LR_COL))


def _in_proj(h, w_in_t, tm, comms=()):
    t = h.shape[0]

    def body(h_ref, w_ref, p_ref):
        hv = h_ref[...]
        for r0, r1, c0 in PROJ_ROWS:
            p_ref[:, c0:c0 + r1 - r0] = _mm_nt(hv, w_ref[r0:r1, :]).astype(BF16)

    return _fused_call(
        body, comms, name="in_proj", grid=(t // tm,), inputs=(h, w_in_t),
        in_specs=[pl.BlockSpec((tm, D_MODEL), lambda i: (i, 0)), _const_spec((PROJ_W, D_MODEL))],
        out_specs=(pl.BlockSpec((tm, PROJ_PAD), lambda i: (i, 0)),),
        out_shape=(jax.ShapeDtypeStruct((t, PROJ_PAD), BF16),))


def _tri(upper):
    r = lax.broadcasted_iota(jnp.int32, (GLA_CHUNK, GLA_CHUNK), 0)
    c = lax.broadcasted_iota(jnp.int32, (GLA_CHUNK, GLA_CHUNK), 1)
    return jnp.where((c >= r) if upper else (c <= r), 1.0, 0.0).astype(BF16)


def _chunk_cumsum(tri, a, add=None):
    hi = a.astype(BF16)
    lo = (a - hi.astype(F32)).astype(BF16)
    dot = functools.partial(jnp.dot, preferred_element_type=F32)
    sums = [dot(tri, hi[_chunk_rows(c)]) + dot(tri, lo[_chunk_rows(c)]) for c in range(a.shape[0] // GLA_CHUNK)]
    return jnp.concatenate(sums if add is None else [s + r for s, r in zip(sums, add)], axis=0)


def _chunk_rows(c):
    return slice(c * GLA_CHUNK, (c + 1) * GLA_CHUNK)


def _gla_masks(rev):
    dk_bits, dv_bits = GLA_DK.bit_length() - 1, GLA_DV.bit_length() - 1
    key_head = lax.broadcasted_iota(jnp.int32, (GLA_CHUNK, KEY_W), 1) >> dk_bits
    val_head = lax.broadcasted_iota(jnp.int32, (GLA_CHUNK, VAL_W), 1) >> dv_bits
    t = lax.broadcasted_iota(jnp.int32, (GLA_HEADS * GLA_CHUNK, GLA_CHUNK), 0) & (GLA_CHUNK - 1)
    s = lax.broadcasted_iota(jnp.int32, (GLA_HEADS * GLA_CHUNK, GLA_CHUNK), 1)
    return key_head, val_head, (s >= t) if rev else (s <= t)


def _stack_heads(a, head_of_lane):
    a = a.astype(BF16)
    return jnp.concatenate([jnp.where(head_of_lane == h, a, jnp.zeros_like(a)) for h in range(GLA_HEADS)], axis=0)


def _rows_by_head(a):
    return jnp.concatenate([a[:, h * GLA_DV:(h + 1) * GLA_DV] for h in range(GLA_HEADS)], axis=0)


def _lanes_by_head(r):
    return jnp.concatenate([r[h * GLA_CHUNK:(h + 1) * GLA_CHUNK] for h in range(GLA_HEADS)], axis=1)


def _head_diagonal(r, head_of_lane):
    rows = r.shape[0] // GLA_HEADS
    out = jnp.where(head_of_lane == 0, r[:rows], 0.0)
    for h in range(1, GLA_HEADS):
        out = out + jnp.where(head_of_lane == h, r[h * rows:(h + 1) * rows], 0.0)
    return out


def _tile_terms(la, q, k, tri, rev):
    nc = la.shape[0] // GLA_CHUNK
    q, k = q.astype(F32), k.astype(F32)
    b = _chunk_cumsum(tri, la)
    ebl = [jnp.exp(b[c * GLA_CHUNK:c * GLA_CHUNK + 1] if rev else b[(c + 1) * GLA_CHUNK - 1:(c + 1) * GLA_CHUNK])
           for c in range(nc)]
    eb = jnp.exp(b)
    enb = jnp.exp(-b)
    kd = k * enb
    ke = jnp.concatenate([kd[_chunk_rows(c)] * ebl[c] for c in range(nc)], axis=0)
    return ebl, eb, enb, q * Q_SCALE * eb, kd, ke


def _log_decay(lr_ref, wd_ref, bd_ref):
    z = _mm(lr_ref[...], wd_ref[...]) + bd_ref[...]
    return z, jax.nn.log_sigmoid(z) * (1.0 / GLA_TAU)


def _p_specs(tg, tile):
    return [pl.BlockSpec((tg, KEY_W), lambda i: (tile(i), 0)),
            pl.BlockSpec((tg, KEY_W), lambda i: (tile(i), 1)),
            pl.BlockSpec((tg, VAL_W), lambda i: (tile(i), 1)),
            pl.BlockSpec((tg, LANE), lambda i: (tile(i), LR_COL // LANE))]


def _gla_fwd_dir(rev, nc, q_ref, k_ref, v_ref, lr_ref, wd_ref, bd_ref, o_ref, st_ref, state):
    key_head, _, causal = _gla_masks(rev)
    order = range(nc - 1, -1, -1) if rev else range(nc)

    def intra():
        _, la = _log_decay(lr_ref, wd_ref, bd_ref)
        ebl, _, _, qd, kd, ke = _tile_terms(la, q_ref[...], k_ref[...], _tri(rev), rev)
        kd = kd.astype(BF16)
        v = {c: v_ref[_chunk_rows(c), :].astype(BF16) for c in order}
        qd_stack = {c: _stack_heads(qd[_chunk_rows(c)], key_head) for c in order}
        ke_stack = {c: _stack_heads(ke[_chunk_rows(c)], key_head) for c in order}
        a_all = {c: _mm_nt(qd_stack[c], kd[_chunk_rows(c)]) for c in order}
        a_all = {c: jnp.where(causal, a_all[c], 0.0).astype(BF16) for c in order}
        head_rows = lambda a, h: a[h * GLA_CHUNK:(h + 1) * GLA_CHUNK]
        head_vals = lambda a, h: a[:, h * GLA_DV:(h + 1) * GLA_DV]
        r = {c: [_mm(head_rows(a_all[c], h), head_vals(v[c], h)) for h in range(GLA_HEADS)] for c in order}
        upd = {c: _mm_tn(_rows_by_head(v[c]), ke_stack[c]) for c in order}
        return {c: (ebl[c], qd_stack[c], r[c], upd[c]) for c in order}

    def scan(terms):
        st = state[...]
        states = {}
        for c in order:
            states[c] = st
            st_ref[c] = st.astype(BF16)
            st = st * terms[c][0] + terms[c][3]
        state[...] = st
        return states

    def inter(terms, states):
        r_inter = {c: _mm_nt(terms[c][1], states[c]) for c in order}
        for c in order:
            o_ref[_chunk_rows(c), :] = jnp.concatenate(
                [terms[c][2][h] + r_inter[c][h * GLA_CHUNK:(h + 1) * GLA_CHUNK] for h in range(GLA_HEADS)], axis=1)

    return intra, scan, inter


def _gla_fwd(p, wd_pad_f, bd_f, wd_pad_b, bd_b, tg, comms=()):
    t = p.shape[0]
    nt = t // tg
    nc = tg // GLA_CHUNK
    up, down = (lambda i: i), (lambda i: nt - 1 - i)

    def body(qf, kf, vf, lrf, qb, kb, vb, lrb, wdf, bdf, wdb, bdb, of, stf, ob, stb, state_f, state_b):
        @pl.when(pl.program_id(0) == 0)
        def _():
            state_f[...] = jnp.zeros_like(state_f)
            state_b[...] = jnp.zeros_like(state_b)

        dirs = [_gla_fwd_dir(False, nc, qf, kf, vf, lrf, wdf, bdf, of, stf, state_f),
                _gla_fwd_dir(True, nc, qb, kb, vb, lrb, wdb, bdb, ob, stb, state_b)]
        terms = [intra() for intra, _, _ in dirs]
        states = [scan(t) for (_, scan, _), t in zip(dirs, terms)]
        for (_, _, inter), t, s in zip(dirs, terms, states):
            inter(t, s)

    wd_spec, bd_spec = _const_spec((LANE, KEY_W)), _const_spec((1, KEY_W))
    outs = lambda tile: (pl.BlockSpec((tg, VAL_W), lambda i: (tile(i), 0)),
                         pl.BlockSpec((nc, GLA_DV, KEY_W), lambda i: (tile(i), 0, 0)))
    out_shape = (jax.ShapeDtypeStruct((t, VAL_W), F32), jax.ShapeDtypeStruct((t // GLA_CHUNK, GLA_DV, KEY_W), BF16))
    return _fused_call(
        body, comms, name="gla_fwd", grid=(nt,), inputs=(p,) * 8 + (wd_pad_f, bd_f, wd_pad_b, bd_b),
        in_specs=_p_specs(tg, up) + _p_specs(tg, down) + [wd_spec, bd_spec, wd_spec, bd_spec],
        out_specs=outs(up) + outs(down), out_shape=out_shape * 2,
        scratch_shapes=[pltpu.VMEM((GLA_DV, KEY_W), F32)] * 2)


def _gla_bwd_dir(rev, nc, q_ref, k_ref, v_ref, lr_ref, wd_ref, bd_ref, st_ref, do_ref,
                 dq_ref, dk_ref, dv_ref, dlr_ref, dwd_ref, dbd_ref, dstate):
    key_head, val_head, causal = _gla_masks(rev)
    order = range(nc) if rev else range(nc - 1, -1, -1)

    def intra():
        z, la = _log_decay(lr_ref, wd_ref, bd_ref)
        tile = _tile_terms(la, q_ref[...], k_ref[...], _tri(rev), rev)
        qd, kd = tile[3], tile[4].astype(BF16)
        v = {c: v_ref[_chunk_rows(c), :].astype(BF16) for c in order}
        d_o = {c: do_ref[_chunk_rows(c), :] for c in order}
        kd_c = {c: kd[_chunk_rows(c)] for c in order}
        qd_stack = {c: _stack_heads(qd[_chunk_rows(c)], key_head) for c in order}
        do_stack = {c: _stack_heads(d_o[c], val_head) for c in order}
        do_rows = {c: _rows_by_head(d_o[c]) for c in order}
        a_all = {c: _mm_nt(qd_stack[c], kd_c[c]) for c in order}
        head_vals = lambda a, h: a[:, h * GLA_DV:(h + 1) * GLA_DV]
        da_all = {c: jnp.concatenate([_mm_nt(head_vals(d_o[c], h), head_vals(v[c], h)) for h in range(GLA_HEADS)],
                                     axis=0) for c in order}
        a_all = {c: jnp.where(causal, a_all[c], 0.0).astype(BF16) for c in order}
        da_all = {c: jnp.where(causal, da_all[c], 0.0).astype(BF16) for c in order}
        dv = {c: _mm_tn(a_all[c], do_stack[c]) for c in order}
        dqd = {c: _mm(jnp.concatenate([do_rows[c], da_all[c]], axis=1),
                      jnp.concatenate([st_ref[c], kd_c[c]], axis=0)) for c in order}
        dkd = {c: _mm_tn(da_all[c], qd_stack[c]) for c in order}
        upd = {c: _mm_tn(do_rows[c], qd_stack[c]) for c in order}
        dqd = {c: _head_diagonal(dqd[c], key_head) for c in order}
        return z, tile, {c: dict(dv=dv[c], dqd=dqd[c], dkd=dkd[c], upd=upd[c]) for c in order}

    def scan(tile, per):
        dst = dstate[...]
        dsts = {}
        for c in order:
            dsts[c] = dst
            dst = dst * tile[0][c] + per[c]["upd"]
        dstate[...] = dst
        return dsts

    def inter(z, tile, per, dsts):
        ebl, eb, enb, qd, kd, ke = tile
        ke_stack = {c: _stack_heads(ke[_chunk_rows(c)], key_head) for c in order}
        v_rows = {c: _rows_by_head(v_ref[_chunk_rows(c), :].astype(BF16)) for c in order}
        dst_b = {c: dsts[c].astype(BF16) for c in order}
        dv_state = {c: _mm_nt(ke_stack[c], dst_b[c]) for c in order}
        dke_c = {c: _mm(v_rows[c], dst_b[c]) for c in order}
        dke_c = {c: _head_diagonal(dke_c[c], key_head) for c in order}
        dbl_c = {}
        for c in order:
            rows = _chunk_rows(c)
            dv_ref[rows, :] = (per[c]["dv"] + _lanes_by_head(dv_state[c])).astype(BF16)
            dbl_c[c] = (jnp.sum(dsts[c] * st_ref[c].astype(F32), axis=0, keepdims=True) * ebl[c]
                        + jnp.sum(dke_c[c] * ke[rows], axis=0, keepdims=True))
        tile_of = lambda parts: jnp.concatenate([parts[c] for c in range(nc)], axis=0)
        dqd, dkd = tile_of({c: per[c]["dqd"] for c in order}), tile_of({c: per[c]["dkd"] for c in order})
        dke = tile_of(dke_c)
        dke_end = tile_of({c: dke_c[c] * ebl[c] for c in order})
        dq_ref[...] = (dqd * eb * Q_SCALE).astype(BF16)
        dk_ref[...] = ((dkd + dke_end) * enb).astype(BF16)
        db = dqd * qd - dkd * kd - dke * ke
        dla = _chunk_cumsum(_tri(not rev), db, [dbl_c[c] for c in range(nc)])
        dz = dla * (_sigmoid(-z) * (1.0 / GLA_TAU))
        dlr_ref[...] = _mm_nt(dz, wd_ref[...]).astype(BF16)
        dwd_ref[...] += _mm_tn(lr_ref[...], dz)
        dbd_ref[...] += jnp.sum(dz, axis=0, keepdims=True)

    return intra, scan, inter


def _gla_bwd(p, wd_pad_f, bd_f, wd_pad_b, bd_b, st_f, st_b, d_o, tg, comms=()):
    t = p.shape[0]
    nt = t // tg
    nc = tg // GLA_CHUNK
    up, down = (lambda i: i), (lambda i: nt - 1 - i)

    def body(qf, kf, vf, lrf, stf, dof, qb, kb, vb, lrb, stb, dob, wdf, bdf, wdb, bdb,
             dqf, dkf, dvf, dlrf, dwdf, dbdf, dqb, dkb, dvb, dlrb, dwdb, dbdb, dstate_f, dstate_b):
        @pl.when(pl.program_id(0) == 0)
        def _():
            for ref in (dstate_f, dstate_b, dwdf, dbdf, dwdb, dbdb):
                ref[...] = jnp.zeros_like(ref)

        dirs = [_gla_bwd_dir(False, nc, qf, kf, vf, lrf, wdf, bdf, stf, dof, dqf, dkf, dvf, dlrf, dwdf, dbdf,
                             dstate_f),
                _gla_bwd_dir(True, nc, qb, kb, vb, lrb, wdb, bdb, stb, dob, dqb, dkb, dvb, dlrb, dwdb, dbdb,
                             dstate_b)]
        first = [intra() for intra, _, _ in dirs]
        dsts = [scan(tile, per) for (_, scan, _), (_, tile, per) in zip(dirs, first)]
        for (_, _, inter), (z, tile, per), d in zip(dirs, first, dsts):
            inter(z, tile, per, d)

    wd_spec, bd_spec = _const_spec((LANE, KEY_W)), _const_spec((1, KEY_W))
    ins = lambda tile: _p_specs(tg, tile) + [pl.BlockSpec((nc, GLA_DV, KEY_W), lambda i: (tile(i), 0, 0)),
                                             pl.BlockSpec((tg, VAL_W), lambda i: (tile(i), 0))]
    outs = lambda tile: (pl.BlockSpec((tg, KEY_W), lambda i: (tile(i), 0)),
                         pl.BlockSpec((tg, KEY_W), lambda i: (tile(i), 0)),
                         pl.BlockSpec((tg, VAL_W), lambda i: (tile(i), 0)),
                         pl.BlockSpec((tg, LANE), lambda i: (tile(i), 0)),
                         _acc_spec((LANE, KEY_W)), _acc_spec((1, KEY_W)))
    out_shape = (jax.ShapeDtypeStruct((t, KEY_W), BF16), jax.ShapeDtypeStruct((t, KEY_W), BF16),
                 jax.ShapeDtypeStruct((t, VAL_W), BF16), jax.ShapeDtypeStruct((t, LANE), BF16),
                 jax.ShapeDtypeStruct((LANE, KEY_W), F32), jax.ShapeDtypeStruct((1, KEY_W), F32))
    scratch = [pltpu.VMEM((GLA_DV, KEY_W), F32)]
    return _fused_call(
        body, comms, name="gla_bwd", grid=(nt,),
        inputs=(p, p, p, p, st_f, d_o, p, p, p, p, st_b, d_o, wd_pad_f, bd_f, wd_pad_b, bd_b),
        in_specs=ins(down) + ins(up) + [wd_spec, bd_spec, wd_spec, bd_spec],
        out_specs=outs(down) + outs(up), out_shape=out_shape * 2, scratch_shapes=scratch * 2)


def _head_rms(o):
    parts, scales = [], []
    for h in range(GLA_HEADS):
        oh = o[:, h * GLA_DV:(h + 1) * GLA_DV]
        r = lax.rsqrt(jnp.mean(oh * oh, axis=-1, keepdims=True) + EPS)
        parts.append(oh * r)
        scales.append(jnp.broadcast_to(r, oh.shape))
    return jnp.concatenate(parts, axis=1), jnp.concatenate(scales, axis=1)


def _layernorm_stats(zv):
    mu = jnp.mean(zv, axis=-1, keepdims=True)
    xc = zv - mu
    rs = lax.rsqrt(jnp.mean(xc * xc, axis=-1, keepdims=True) + EPS)
    return xc * rs, rs


def _mix_fwd(x, o_f, o_b, p, gla_g, ln_g, ln_b, w_sp, b_sp, w_out, tm, comms=()):
    t = x.shape[0]
    nch = tm // GMLP_CHUNK

    def body(x_ref, of_ref, ob_ref, pg_ref, pu_ref, pv_ref, gg_ref, lg_ref, lb_ref, ws_ref, bs_ref, wo_ref,
             x1_ref, y_ref, s_scr):
        on, _ = _head_rms(of_ref[...] + ob_ref[...])
        pg = pg_ref[...].astype(F32)
        y_a = on * gg_ref[...] * (pg * _sigmoid(pg))
        zu = _gelu(pu_ref[...].astype(F32))
        vhat, _ = _layernorm_stats(_gelu(pv_ref[...].astype(F32)))
        vln = (vhat * lg_ref[...] + lb_ref[...]).astype(BF16)
        for g in range(GMLP_GROUPS):
            w_g = ws_ref[g].astype(BF16)
            b_g = bs_ref[g]
            cols = slice(g * LANE, (g + 1) * LANE)
            for n in range(nch):
                rows = slice(n * GMLP_CHUNK, (n + 1) * GMLP_CHUNK)
                s_scr[rows, cols] = jnp.dot(w_g, vln[rows, cols], preferred_element_type=F32) + b_g
        ycat = jnp.concatenate([y_a, zu * s_scr[...]], axis=1).astype(BF16)
        y_ref[...] = ycat
        x1_ref[...] = x_ref[...] + jnp.dot(ycat, wo_ref[...], preferred_element_type=F32)

    half = lambda j: pl.BlockSpec((tm, VAL_W), lambda i: (i, j))
    return _fused_call(
        body, comms, name="mix_fwd", grid=(t // tm,),
        inputs=(x, o_f, o_b, p, p, p, gla_g, ln_g, ln_b, w_sp, b_sp, w_out),
        in_specs=[pl.BlockSpec((tm, D_MODEL), lambda i: (i, 0)), half(0), half(0), half(2), half(3), half(4),
                  _const_spec((1, VAL_W)), _const_spec((1, GMLP_W)), _const_spec((1, GMLP_W)),
                  _const_spec((GMLP_GROUPS, GMLP_CHUNK, GMLP_CHUNK)), _const_spec((GMLP_GROUPS, GMLP_CHUNK, 1)),
                  _const_spec((D_MODEL, D_MODEL))],
        out_specs=(pl.BlockSpec((tm, D_MODEL), lambda i: (i, 0)), pl.BlockSpec((tm, D_MODEL), lambda i: (i, 0))),
        out_shape=(jax.ShapeDtypeStruct((t, D_MODEL), F32), jax.ShapeDtypeStruct((t, D_MODEL), BF16)),
        scratch_shapes=[pltpu.VMEM((tm, GMLP_W), F32)])


def _mix_bwd(dx1, ycat, o_f, o_b, p, gla_g, ln_g, ln_b, w_sp, b_sp, w_out, tm, comms=()):
    t = dx1.shape[0]
    nch = tm // GMLP_CHUNK

    def body(dx1_ref, y_ref, of_ref, ob_ref, pg_ref, pu_ref, pv_ref, gg_ref, lg_ref, lb_ref, ws_ref, bs_ref, wo_ref,
             do_ref, dpg_ref, dpu_ref, dpv_ref, dwo_ref, dgg_ref, dlg_ref, dlb_ref, dws_ref, dbs_ref,
             s_scr, dvln_scr):
        @pl.when(pl.program_id(0) == 0)
        def _():
            for ref in (dwo_ref, dgg_ref, dlg_ref, dlb_ref, dws_ref, dbs_ref):
                ref[...] = jnp.zeros_like(ref)

        dx1 = dx1_ref[...].astype(BF16)
        dycat = _mm_nt(dx1, wo_ref[...])
        dwo_ref[...] += _mm_tn(y_ref[...], dx1)
        dy_a = dycat[:, :VAL_W]
        dy_b = dycat[:, VAL_W:]
        on, r = _head_rms(of_ref[...] + ob_ref[...])
        pg = pg_ref[...].astype(F32)
        sil, dsil = _silu_and_grad(pg)
        gg = gg_ref[...]
        dgg_ref[...] += jnp.sum(dy_a * sil * on, axis=0, keepdims=True)
        don = dy_a * sil * gg
        prod = don * on
        means = jnp.concatenate(
            [jnp.broadcast_to(jnp.mean(prod[:, h * GLA_DV:(h + 1) * GLA_DV], axis=-1, keepdims=True),
                              (tm, GLA_DV)) for h in range(GLA_HEADS)], axis=1)
        do_ref[...] = (r * (don - on * means)).astype(BF16)
        dpg_ref[...] = (dy_a * on * gg * dsil).astype(BF16)
        pu = pu_ref[...].astype(F32)
        pv = pv_ref[...].astype(F32)
        zu, dzu_dpu = _gelu_and_grad(pu)
        zv, dzv_dpv = _gelu_and_grad(pv)
        vhat, rs = _layernorm_stats(zv)
        lg = lg_ref[...]
        vln = (vhat * lg + lb_ref[...]).astype(BF16)
        ds32 = dy_b * zu
        ds = ds32.astype(BF16)
        blocks = [(g, n) for g in range(GMLP_GROUPS) for n in range(nch)]
        at = lambda g, n: (slice(n * GMLP_CHUNK, (n + 1) * GMLP_CHUNK), slice(g * LANE, (g + 1) * LANE))
        w_sp = [ws_ref[g].astype(BF16) for g in range(GMLP_GROUPS)]
        v_blk = {b: vln[at(*b)] for b in blocks}
        ds_blk = {b: ds[at(*b)] for b in blocks}
        s_blk = {b: jnp.dot(w_sp[b[0]], v_blk[b], preferred_element_type=F32) for b in blocks}
        dw_blk = {b: _mm_nt(ds_blk[b], v_blk[b]) for b in blocks}
        dvln_blk = {b: _mm_tn(w_sp[b[0]], ds_blk[b]) for b in blocks}
        for b in blocks:
            s_scr[at(*b)] = s_blk[b] + bs_ref[b[0]]
            dvln_scr[at(*b)] = dvln_blk[b]
        for g in range(GMLP_GROUPS):
            dws_ref[g] += sum(dw_blk[(g, n)] for n in range(nch))
            dbs_ref[g] += sum(jnp.sum(ds32[at(g, n)], axis=-1, keepdims=True) for n in range(nch))
        dpu_ref[...] = (dy_b * s_scr[...] * dzu_dpu).astype(BF16)
        dvln = dvln_scr[...]
        dlg_ref[...] += jnp.sum(dvln * vhat, axis=0, keepdims=True)
        dlb_ref[...] += jnp.sum(dvln, axis=0, keepdims=True)
        dvhat = dvln * lg
        dzv = rs * (dvhat - jnp.mean(dvhat, axis=-1, keepdims=True)
                    - vhat * jnp.mean(dvhat * vhat, axis=-1, keepdims=True))
        dpv_ref[...] = (dzv * dzv_dpv).astype(BF16)

    half = lambda j: pl.BlockSpec((tm, VAL_W), lambda i: (i, j))
    full = pl.BlockSpec((tm, D_MODEL), lambda i: (i, 0))
    sp_shape = (GMLP_GROUPS, GMLP_CHUNK, GMLP_CHUNK)
    bs_shape = (GMLP_GROUPS, GMLP_CHUNK, 1)
    return _fused_call(
        body, comms, name="mix_bwd", grid=(t // tm,),
        inputs=(dx1, ycat, o_f, o_b, p, p, p, gla_g, ln_g, ln_b, w_sp, b_sp, w_out),
        in_specs=[full, full, half(0), half(0), half(2), half(3), half(4),
                  _const_spec((1, VAL_W)), _const_spec((1, GMLP_W)), _const_spec((1, GMLP_W)),
                  _const_spec(sp_shape), _const_spec(bs_shape), _const_spec((D_MODEL, D_MODEL))],
        out_specs=(half(0), half(0), half(0), half(0), _acc_spec((D_MODEL, D_MODEL)), _acc_spec((1, VAL_W)),
                   _acc_spec((1, GMLP_W)), _acc_spec((1, GMLP_W)), _acc_spec(sp_shape), _acc_spec(bs_shape)),
        out_shape=(jax.ShapeDtypeStruct((t, VAL_W), BF16),) * 4 + (
            jax.ShapeDtypeStruct((D_MODEL, D_MODEL), F32), jax.ShapeDtypeStruct((1, VAL_W), F32),
            jax.ShapeDtypeStruct((1, GMLP_W), F32), jax.ShapeDtypeStruct((1, GMLP_W), F32),
            jax.ShapeDtypeStruct(sp_shape, F32), jax.ShapeDtypeStruct(bs_shape, F32)),
        scratch_shapes=[pltpu.VMEM((tm, GMLP_W), F32), pltpu.VMEM((tm, GMLP_W), F32)])


def _rms_bwd(dy_scaled, xn, r):
    return r * (dy_scaled - xn * jnp.mean(dy_scaled * xn, axis=-1, keepdims=True))


def _ffn(x1, target, g2, gf, w_gate, w_up, w_down, tm):
    t = x1.shape[0]

    def body(x1_ref, tg_ref, g2_ref, gf_ref, wg_ref, wu_ref, wd_ref,
             dx1_ref, h2_ref, dgate_ref, dup_ref, act_ref, dx2_ref, loss_ref, dgf_ref, dg2_ref):
        @pl.when(pl.program_id(0) == 0)
        def _():
            for ref in (loss_ref, dgf_ref, dg2_ref):
                ref[...] = jnp.zeros_like(ref)

        x1v = x1_ref[...]
        g2v = g2_ref[...]
        gfv = gf_ref[...]
        r2 = lax.rsqrt(jnp.mean(x1v * x1v, axis=-1, keepdims=True) + EPS)
        xn1 = x1v * r2
        h2 = (xn1 * g2v).astype(BF16)
        h2_ref[...] = h2
        gate = _mm_nt(h2, wg_ref[...])
        up = _mm_nt(h2, wu_ref[...])
        sil, dsil = _silu_and_grad(gate)
        act = (sil * up).astype(BF16)
        act_ref[...] = act
        x2 = x1v + jnp.dot(act, wd_ref[...], preferred_element_type=F32)
        rf = lax.rsqrt(jnp.mean(x2 * x2, axis=-1, keepdims=True) + EPS)
        xn2 = x2 * rf
        err = xn2 * gfv - tg_ref[...]
        loss_ref[...] += 0.5 * jnp.sum(jnp.mean(err * err, axis=-1, keepdims=True))
        dy = err * (1.0 / D_MODEL)
        dgf_ref[...] += jnp.sum(dy * xn2, axis=0, keepdims=True)
        dx2 = _rms_bwd(dy * gfv, xn2, rf)
        dx2b = dx2.astype(BF16)
        dx2_ref[...] = dx2b
        dact = _mm_nt(dx2b, wd_ref[...])
        dgate = (dact * up * dsil).astype(BF16)
        dup = (dact * sil).astype(BF16)
        dgate_ref[...] = dgate
        dup_ref[...] = dup
        dh2 = _mm(dgate, wg_ref[...]) + _mm(dup, wu_ref[...])
        dg2_ref[...] += jnp.sum(dh2 * xn1, axis=0, keepdims=True)
        dx1_ref[...] = dx2 + _rms_bwd(dh2 * g2v, xn1, r2)

    row = lambda w: pl.BlockSpec((tm, w), lambda i: (i, 0))
    return pl.pallas_call(
        body, name="ffn_fwd_bwd", grid=(t // tm,),
        in_specs=[row(D_MODEL), row(D_MODEL), _const_spec((1, D_MODEL)), _const_spec((1, D_MODEL)),
                  _const_spec((D_FF, D_MODEL)), _const_spec((D_FF, D_MODEL)), _const_spec((D_FF, D_MODEL))],
        out_specs=(row(D_MODEL), row(D_MODEL), row(D_FF), row(D_FF), row(D_FF), row(D_MODEL),
                   _acc_spec((8, LANE)), _acc_spec((1, D_MODEL)), _acc_spec((1, D_MODEL))),
        out_shape=(jax.ShapeDtypeStruct((t, D_MODEL), F32), jax.ShapeDtypeStruct((t, D_MODEL), BF16),
                   jax.ShapeDtypeStruct((t, D_FF), BF16), jax.ShapeDtypeStruct((t, D_FF), BF16),
                   jax.ShapeDtypeStruct((t, D_FF), BF16), jax.ShapeDtypeStruct((t, D_MODEL), BF16),
                   jax.ShapeDtypeStruct((8, LANE), F32), jax.ShapeDtypeStruct((1, D_MODEL), F32),
                   jax.ShapeDtypeStruct((1, D_MODEL), F32)),
        compiler_params=_params(),
    )(x1, target, g2, gf, w_gate, w_up, w_down)


def _matmul_tn(a, b, tm, tk, name, comms=()):
    t, m = a.shape
    n = b.shape[1]

    def body(a_ref, b_ref, o_ref):
        @pl.when(pl.program_id(1) == 0)
        def _():
            o_ref[...] = jnp.zeros_like(o_ref)

        o_ref[...] += _mm_tn(a_ref[...], b_ref[...])

    (out,), comm_results = _fused_call(
        body, comms, name=name, grid=(m // tm, t // tk), inputs=(a, b),
        in_specs=[pl.BlockSpec((tk, tm), lambda j, k: (k, j)), pl.BlockSpec((tk, n), lambda j, k: (k, 0))],
        out_specs=(pl.BlockSpec((tm, n), lambda j, k: (j, 0)),),
        out_shape=(jax.ShapeDtypeStruct((m, n), F32),))
    return out, comm_results


def _in_proj_bwd(x, g1, dx1, dq_f, dq_b, dk_f, dk_b, dv_f, dv_b, dpg, dpu, dpv, dlr_f, dlr_b, w_main, tm, comms=()):
    t = x.shape[0]

    def body(x_ref, g_ref, dx1_ref, dqf, dqb, dkf, dkb, dvf, dvb, dg, du, dv, dlf, dlb, w_ref,
             dx_ref, dp_ref, dg1_ref):
        @pl.when(pl.program_id(0) == 0)
        def _():
            dg1_ref[...] = jnp.zeros_like(dg1_ref)

        both = lambda a, b: (a[...].astype(F32) + b[...].astype(F32)).astype(BF16)
        dp = jnp.concatenate([both(dqf, dqb), both(dkf, dkb), both(dvf, dvb), dg[...], du[...], dv[...],
                              both(dlf, dlb)], axis=1)
        dp_ref[...] = dp
        dh = sum(_mm(dp[:, c0:c0 + r1 - r0], w_ref[r0:r1, :]) for r0, r1, c0 in PROJ_ROWS)
        xv = x_ref[...]
        r = lax.rsqrt(jnp.mean(xv * xv, axis=-1, keepdims=True) + EPS)
        xn = xv * r
        dg1_ref[...] += jnp.sum(dh * xn, axis=0, keepdims=True)
        dx_ref[...] = dx1_ref[...] + _rms_bwd(dh * g_ref[...], xn, r)

    row = lambda w: pl.BlockSpec((tm, w), lambda i: (i, 0))
    return _fused_call(
        body, comms, name="in_proj_bwd", grid=(t // tm,),
        inputs=(x, g1, dx1, dq_f, dq_b, dk_f, dk_b, dv_f, dv_b, dpg, dpu, dpv, dlr_f, dlr_b, w_main),
        in_specs=[row(D_MODEL), _const_spec((1, D_MODEL)), row(D_MODEL), row(KEY_W), row(KEY_W), row(KEY_W),
                  row(KEY_W), row(VAL_W), row(VAL_W), row(VAL_W), row(VAL_W), row(VAL_W), row(LANE), row(LANE),
                  _const_spec((PROJ_W, D_MODEL))],
        out_specs=(row(D_MODEL), row(PROJ_PAD), _acc_spec((1, D_MODEL))),
        out_shape=(jax.ShapeDtypeStruct((t, D_MODEL), F32), jax.ShapeDtypeStruct((t, PROJ_PAD), BF16),
                   jax.ShapeDtypeStruct((1, D_MODEL), F32)))


def _adamw(w, g, m, v):
    m_new = ADAM_B1 * m + (1.0 - ADAM_B1) * g
    v_new = ADAM_B2 * v + (1.0 - ADAM_B2) * (g * g)
    m_hat = m_new / (1.0 - ADAM_B1 ** ADAM_STEP)
    v_hat = v_new / (1.0 - ADAM_B2 ** ADAM_STEP)
    delta = -ADAM_LR * (m_hat / (jnp.sqrt(v_hat) + ADAM_EPS) + ADAM_WD * w)
    return delta, m_new, v_new


def _adamw_window(own, recv, w, m, v, name):
    r, c = w.shape
    rows = own.shape[0]

    def body(own_ref, recv_ref, w_ref, m_ref, v_ref, g_ref, d_ref, nm_ref, nv_ref):
        g = own_ref[...]
        for k in range(3):
            g = g + recv_ref[k].astype(F32)

        def update(g):
            g_ref[...] = g[:r]
            d_ref[...], nm_ref[...], nv_ref[...] = _adamw(w_ref[...], g[:r], m_ref[...], v_ref[...])

        core = lax.axis_index("c")
        pl.when(core == 0)(lambda: update(g))
        pl.when(core == 1)(lambda: update(pltpu.roll(g, rows - 4, 0)))

    whole = lambda *shape: pl.BlockSpec(shape, lambda i: (0,) * len(shape))
    return pl.pallas_call(
        body, name=name, grid=(1,),
        in_specs=[whole(rows, c), whole(3, rows, c), whole(r, c), whole(r, c), whole(r, c)],
        out_specs=(whole(r, c),) * 4, out_shape=(jax.ShapeDtypeStruct((r, c), F32),) * 4,
        compiler_params=_params(),
    )(own, recv, w, m, v)


def _adamw_shard(own, recv, w, m, v, tr, name):
    r, c = w.shape

    def body(own_ref, recv_ref, w_ref, m_ref, v_ref, g_ref, d_ref, nm_ref, nv_ref):
        g = own_ref[...]
        for k in range(3):
            g = g + recv_ref[k].astype(F32)
        g_ref[...] = g
        d_ref[...], nm_ref[...], nv_ref[...] = _adamw(w_ref[...], g, m_ref[...], v_ref[...])

    row = pl.BlockSpec((tr, c), lambda i: (i, 0))
    return pl.pallas_call(
        body, name=name, grid=(r // tr,),
        in_specs=[row, pl.BlockSpec((3, tr, c), lambda i: (0, i, 0)), row, row, row],
        out_specs=(row,) * 4, out_shape=(jax.ShapeDtypeStruct((r, c), F32),) * 4,
        compiler_params=_params(),
    )(own, recv, w, m, v)


def _adamw_small(entries):
    stacks = []
    for (g, _, _), _, _, _ in entries:
        if not any(g is s for s in stacks):
            stacks.append(g)
    where = [next(i for i, s in enumerate(stacks) if s is g) for (g, _, _), _, _, _ in entries]
    ns, ne = len(stacks), len(entries)

    def body(*refs):
        s_refs, wmv, outs = refs[:ns], refs[ns:ns + 3 * ne], refs[ns + 3 * ne:]
        for e, ((_, r0, nr), _, _, _) in enumerate(entries):
            grad = s_refs[where[e]][r0:r0 + nr, :]
            w_ref, m_ref, v_ref = wmv[3 * e:3 * e + 3]
            g_ref, d_ref, nm_ref, nv_ref = outs[4 * e:4 * e + 4]
            g_ref[...] = grad
            d_ref[...], nm_ref[...], nv_ref[...] = _adamw(w_ref[...], grad, m_ref[...], v_ref[...])

    results = pl.pallas_call(
        body, name="adamw_small",
        out_shape=tuple(jax.ShapeDtypeStruct(w.shape, F32) for _, w, _, _ in entries for _ in range(4)),
        compiler_params=pltpu.CompilerParams(vmem_limit_bytes=VMEM_LIMIT),
    )(*stacks, *[a for _, w, m, v in entries for a in (w, m, v)])
    return [results[4 * e:4 * e + 4] for e in range(ne)]


def _mesh_pos():
    return lax.axis_index("x"), lax.axis_index("y"), lax.axis_index("c")


def _other_chips(x, y):
    return [(x, 1 - y), (1 - x, y), (1 - x, 1 - y)]


_VMEM_WHOLE = pl.BlockSpec(memory_space=pltpu.VMEM)
_HBM_WHOLE = pl.BlockSpec(memory_space=pl.ANY)


def _gather_comm(shards, cast, mid=((1, 2), (3, 4))):
    na = len(shards)
    staged = [a for a in range(na) if cast[a]]

    def phases(in_refs, out_refs, scr):
        stage = dict(zip(staged, scr[:len(staged)]))
        send_sems, recv_sems, local_sems = scr[len(staged):]
        x, y, c = _mesh_pos()
        me, sibling = (x, y, c), (x, y, 1 - c)
        chip_a, chip_b, diagonal = (x ^ c, y ^ (1 - c)), (x ^ (1 - c), y ^ c), (1 - x, 1 - y)
        srcs = [stage[a] if cast[a] else in_refs[a] for a in range(na)]

        def rows(a, pos):
            px, py, pc = pos
            return out_refs[a].at[4 * px + 2 * py + pc]

        def copy(a, k, block, to, src=None):
            return pltpu.make_async_remote_copy(
                src_ref=rows(a, block) if src is None else src, dst_ref=rows(a, block),
                send_sem=send_sems.at[a, k], recv_sem=recv_sems.at[a, k], device_id=to, device_id_type=MESH_ID)

        mine = [pltpu.make_async_copy(srcs[a], rows(a, me), local_sems.at[a]) for a in range(na)]
        own = [copy(a, k, me, to, src=srcs[a]) for a in range(na)
               for k, to in ((0, sibling), (1, (*chip_a, c)), (2, (*chip_b, c)))]
        onward = [copy(a, 3, (*chip_a, c), (*chip_b, c)) for a in range(na)]
        to_sibling = {k: [copy(a, k, (*chip, c), sibling) for a in range(na)]
                      for k, chip in ((4, chip_a), (5, chip_b), (6, diagonal))}

        def start():
            for a in staged:
                stage[a][...] = in_refs[a][...].astype(BF16)
            for cp in own:
                cp.start()
            for cp in mine:
                cp.start(priority=1)

        def forward_neighbours():
            for a in range(na):
                copy(a, 1, (*chip_a, c), me).wait_recv()
                onward[a].start()
                to_sibling[4][a].start()
            for a in range(na):
                copy(a, 2, (*chip_b, c), me).wait_recv()
                to_sibling[5][a].start()

        def forward_diagonal():
            for a in range(na):
                copy(a, 3, (*diagonal, c), me).wait_recv()
                to_sibling[6][a].start()

        def finish():
            for a in range(na):
                for k, chip in ((0, (x, y)), (4, chip_b), (5, chip_a), (6, diagonal)):
                    copy(a, k, (*chip, 1 - c), me).wait_recv()
            for cp in own + onward + to_sibling[4] + to_sibling[5] + to_sibling[6]:
                cp.wait_send()
            for cp in mine:
                cp.wait()

        return start, forward_neighbours, forward_diagonal, finish

    def before(step, nsteps, in_refs, out_refs, scr):
        start, forward_neighbours, forward_diagonal, _ = phases(in_refs, out_refs, scr)
        pl.when(step == 0)(start)
        pl.when(step == nsteps * mid[0][0] // mid[0][1])(forward_neighbours)
        pl.when(step == nsteps * mid[1][0] // mid[1][1])(forward_diagonal)

    def after(step, nsteps, in_refs, out_refs, scr):
        pl.when(step == nsteps - 1)(phases(in_refs, out_refs, scr)[3])

    return _Comm(
        inputs=list(shards), in_specs=[_VMEM_WHOLE] * na,
        out_shape=[jax.ShapeDtypeStruct((N_DEV,) + s.shape, BF16 if cast[a] else s.dtype)
                   for a, s in enumerate(shards)],
        out_specs=[_HBM_WHOLE] * na,
        scratch_shapes=[pltpu.VMEM(shards[a].shape, BF16) for a in staged] + [
            pltpu.SemaphoreType.DMA((na, 7)), pltpu.SemaphoreType.DMA((na, 7)), pltpu.SemaphoreType.DMA((na,))],
        before=before, after=after)


W_IN_WINDOW = 336


def _w_in_block_pieces(g_ref, chip, core):
    j = 2 * chip + core
    rows = PROJ_W // N_DEV
    first = rows * j - jnp.where(j > 4, 2 * LOWRANK, 0)
    start = pl.multiple_of((first >> 3) << 3, 8)
    head = LR_REF - 4 * rows
    split = [(g_ref.at[pl.ds(4 * rows, head)], 0, head), (g_ref.at[pl.ds(LR_COL, 2 * LOWRANK)], head, 2 * LOWRANK),
             (g_ref.at[pl.ds(LR_REF, 64)], head + 2 * LOWRANK, 64)]
    return [(j != 4, [(g_ref.at[pl.ds(start, W_IN_WINDOW)], 0, W_IN_WINDOW)]), (j == 4, split)]


def _reduce_scatter_comm(grads, rows=None, pieces=None):
    if pieces is None:
        _, _, r, c = grads.shape
        pieces = lambda g_ref, chip, core: [(None, [(g_ref.at[chip, core], 0, r)])]
    else:
        r, c = rows, grads.shape[1]
    order = (3, 1, 2, 0)

    def transfer(k, kind, in_refs, scr, act):
        (g_ref,), (sib, own, _, sems) = in_refs, scr
        x, y, core = _mesh_pos()
        chip = (2 * x + y) ^ k
        for cond, parts in pieces(g_ref, chip, 1 - core if kind == "send" else core):
            def run(parts=parts):
                for i, (src, row0, n) in enumerate(parts):
                    if kind == "local":
                        act(pltpu.make_async_copy(src, own.at[k, pl.ds(row0, n)], sems.at[2, 3 * k + i]))
                    else:
                        act(pltpu.make_async_remote_copy(
                            src_ref=src, dst_ref=sib.at[k, pl.ds(row0, n)], send_sem=sems.at[0, 3 * k + i],
                            recv_sem=sems.at[1, 3 * k + i], device_id=(x, y, 1 - core), device_id_type=MESH_ID))

            run() if cond is None else pl.when(cond)(run)

    def chip_copies(out_refs, scr):
        (_, recv), (_, _, part, sems) = out_refs, scr
        x, y, core = _mesh_pos()
        return [pltpu.make_async_remote_copy(
            src_ref=part.at[j], dst_ref=recv.at[j], send_sem=sems.at[3, j], recv_sem=sems.at[4, j],
            device_id=(*chip, core), device_id_type=MESH_ID) for j, chip in enumerate(_other_chips(x, y))]

    def before(step, nsteps, in_refs, out_refs, scr):
        @pl.when(step == 0)
        def _():
            for k in order:
                transfer(k, "send", in_refs, scr, lambda cp: cp.start())
                transfer(k, "local", in_refs, scr, lambda cp: cp.start(priority=1))

    def after(step, nsteps, in_refs, out_refs, scr):
        sib, own, part, _ = scr

        @pl.when(step == (nsteps - 1) // 2)
        def _():
            to_chips = chip_copies(out_refs, scr)
            for k in order:
                transfer(k, "recv", in_refs, scr, lambda cp: cp.wait_recv())
                transfer(k, "local", in_refs, scr, lambda cp: cp.wait())
                if k:
                    part[k - 1] = (own[k] + sib[k]).astype(BF16)
                    to_chips[k - 1].start()
                else:
                    out_refs[0][...] = own[0] + sib[0]
            for k in order:
                transfer(k, "send", in_refs, scr, lambda cp: cp.wait_send())

        @pl.when(step == nsteps - 1)
        def _():
            for cp in chip_copies(out_refs, scr):
                cp.wait()

    return _Comm(inputs=[grads], in_specs=[_HBM_WHOLE],
                 out_shape=[jax.ShapeDtypeStruct((r, c), F32), jax.ShapeDtypeStruct((3, r, c), BF16)],
                 out_specs=[_VMEM_WHOLE, _HBM_WHOLE],
                 scratch_shapes=[pltpu.VMEM((4, r, c), F32), pltpu.VMEM((4, r, c), F32), pltpu.VMEM((3, r, c), BF16),
                                 pltpu.SemaphoreType.DMA((5, 12))],
                 before=before, after=after)


def _exchange_comm(arrays, out_shape, make_copies):
    na = len(arrays)

    def copies(in_refs, out_refs, scr):
        return make_copies(in_refs, out_refs, *scr)

    def before(step, nsteps, in_refs, out_refs, scr):
        @pl.when(step == 0)
        def _():
            for cp in copies(in_refs, out_refs, scr):
                cp.start()

    def after(step, nsteps, in_refs, out_refs, scr):
        @pl.when(step == nsteps - 1)
        def _():
            for cp in copies(in_refs, out_refs, scr):
                cp.wait()

    return _Comm(inputs=list(arrays), in_specs=[_HBM_WHOLE] * na, out_shape=list(out_shape),
                 out_specs=[_HBM_WHOLE] * na,
                 scratch_shapes=[pltpu.SemaphoreType.DMA((na, 3)), pltpu.SemaphoreType.DMA((na, 3))],
                 before=before, after=after)


def _sibling_exchange_comm(grads):
    def make_copies(in_refs, out_refs, send_sems, recv_sems):
        x, y, c = _mesh_pos()
        return [pltpu.make_async_remote_copy(
            src_ref=in_refs[a].at[:, pl.ds(1 - c, 1)], dst_ref=out_refs[a], send_sem=send_sems.at[a, 0],
            recv_sem=recv_sems.at[a, 0], device_id=(x, y, 1 - c), device_id_type=MESH_ID)
            for a in range(len(grads))]

    return _exchange_comm(grads, [jax.ShapeDtypeStruct((4, 1) + g.shape[2:], F32) for g in grads], make_copies)


def _chip_sum(my_pos, mine, from_sibling, tr, name):
    _, _, r, c = mine.shape

    def body(pos_ref, a_ref, b_ref, own_ref, out_ref):
        s = a_ref[0, 0] + b_ref[0, 0]

        @pl.when(pl.program_id(1) == 0)
        def _():
            own_ref[...] = s

        @pl.when(pl.program_id(1) > 0)
        def _():
            out_ref[0] = s.astype(BF16)

    grid_spec = pltpu.PrefetchScalarGridSpec(
        num_scalar_prefetch=1, grid=(r // tr, 4),
        in_specs=[pl.BlockSpec((1, 1, tr, c), lambda i, k, pos: (pos[0] ^ k, pos[1], i, 0)),
                  pl.BlockSpec((1, 1, tr, c), lambda i, k, pos: (pos[0] ^ k, 0, i, 0))],
        out_specs=(pl.BlockSpec((tr, c), lambda i, k, pos: (i, 0)),
                   pl.BlockSpec((1, tr, c), lambda i, k, pos: (jnp.maximum(k - 1, 0), i, 0))))
    return pl.pallas_call(
        body, name=name, grid_spec=grid_spec,
        out_shape=(jax.ShapeDtypeStruct((r, c), F32), jax.ShapeDtypeStruct((3, r, c), BF16)),
        compiler_params=_params(2),
    )(my_pos, mine, from_sibling)


def _chips_exchange_comm(partials):
    def make_copies(in_refs, out_refs, send_sems, recv_sems):
        x, y, c = _mesh_pos()
        return [pltpu.make_async_remote_copy(
            src_ref=in_refs[a].at[j], dst_ref=out_refs[a].at[j], send_sem=send_sems.at[a, j],
            recv_sem=recv_sems.at[a, j], device_id=(*chip, c), device_id_type=MESH_ID)
            for a in range(len(partials)) for j, chip in enumerate(_other_chips(x, y))]

    return _exchange_comm(partials, [jax.ShapeDtypeStruct(g.shape, BF16) for g in partials], make_copies)


def _comm_only(comms, name):
    return _fused_call(lambda: None, comms, name=name, grid=(1,), inputs=(), in_specs=[], out_specs=(),
                       out_shape=())[1]


def _all_reduce_small_comm(parts):
    na = len(parts)

    def copies(in_refs, scr):
        gathered, (send_sems, recv_sems) = scr[:na], scr[na:]
        x, y, c = _mesh_pos()
        my_id = 4 * x + 2 * y + c
        return my_id, [pltpu.make_async_remote_copy(
            src_ref=in_refs[a], dst_ref=gathered[a].at[my_id], send_sem=send_sems.at[a, k - 1],
            recv_sem=recv_sems.at[a, k - 1], device_id=(x ^ (k >> 2), y ^ ((k >> 1) & 1), c ^ (k & 1)),
            device_id_type=MESH_ID) for a in range(na) for k in range(1, N_DEV)]

    def before(step, nsteps, in_refs, out_refs, scr):
        @pl.when(step == 0)
        def _():
            for cp in copies(in_refs, scr)[1]:
                cp.start()

    def after(step, nsteps, in_refs, out_refs, scr):
        @pl.when(step == nsteps - 1)
        def _():
            my_id, cps = copies(in_refs, scr)
            for a in range(na):
                scr[a][my_id] = in_refs[a][...]
            for cp in cps:
                cp.wait()
            for a in range(na):
                acc = scr[a][0]
                for d in range(1, N_DEV):
                    acc = acc + scr[a][d]
                out_refs[a][...] = acc

    return _Comm(inputs=list(parts), in_specs=[_VMEM_WHOLE] * na,
                 out_shape=[jax.ShapeDtypeStruct(p.shape, F32) for p in parts], out_specs=[_VMEM_WHOLE] * na,
                 scratch_shapes=[pltpu.VMEM((N_DEV,) + p.shape, F32) for p in parts] + [
                     pltpu.SemaphoreType.DMA((na, N_DEV - 1)), pltpu.SemaphoreType.DMA((na, N_DEV - 1))],
                 before=before, after=after)


def _unshard_cols(g):
    return jnp.transpose(g, (1, 0, 2)).reshape(g.shape[1], N_DEV * g.shape[2])


def _row_blocks(w):
    return w.reshape(4, 2, w.shape[0] // N_DEV, w.shape[1])


def _stack_rows(parts):
    a = jnp.concatenate(parts, axis=0)
    return jnp.pad(a, ((0, (-a.shape[0]) % 8), (0, 0)))


def _padded_decay_weights(wd_f, wd_b):
    zeros = lambda n: jnp.zeros((n, KEY_W), F32)
    return (jnp.concatenate([wd_f, zeros(LANE - LOWRANK)], axis=0),
            jnp.concatenate([zeros(LOWRANK), wd_b, zeros(LANE - 2 * LOWRANK)], axis=0))


def kernel(x, norm1_g, w_in,w_decay_f, b_decay_f, w_decay_b, b_decay_b, gla_norm_g, gmlp_ln_g, gmlp_ln_b, w_spatial, b_spatial, w_out, norm2_g, w_gate, w_up, w_down, final_norm_g, loss_target, m_norm1_g, m_w_in, m_w_decay_f, m_b_decay_f, m_w_decay_b, m_b_decay_b, m_gla_norm_g, m_gmlp_ln_g, m_gmlp_ln_b, m_w_spatial, m_b_spatial, m_w_out, m_norm2_g, m_w_gate, m_w_up, m_w_down, m_final_norm_g, v_norm1_g, v_w_in, v_w_decay_f, v_b_decay_f, v_w_decay_b, v_b_decay_b, v_gla_norm_g, v_gmlp_ln_g, v_gmlp_ln_b, v_w_spatial, v_b_spatial, v_w_out, v_norm2_g, v_w_gate, v_w_up, v_w_down, v_final_norm_g):
    t = x.shape[1]
    xt = x[0]
    target = loss_target[0]
    pos_x, pos_y, pos_c = _mesh_pos()
    my_pos = jnp.stack([2 * pos_x + pos_y, pos_c]).astype(jnp.int32)
    my_id = 4 * pos_x + 2 * pos_y + pos_c

    tile = lambda n: min(n, t)
    ln_g, ln_b, w_sp = gmlp_ln_g, gmlp_ln_b, w_spatial[0]
    b_sp_col = b_spatial[0][:, :, None]
    shard = {"w_in": w_in[0].T, "w_out": w_out[0], "w_gate": w_gate[0].T, "w_up": w_up[0].T, "w_down": w_down[0]}
    shard_m = {"w_in": m_w_in[0].T, "w_out": m_w_out[0], "w_gate": m_w_gate[0].T, "w_up": m_w_up[0].T,
               "w_down": m_w_down[0]}
    shard_v = {"w_in": v_w_in[0].T, "w_out": v_w_out[0], "w_gate": v_w_gate[0].T, "w_up": v_w_up[0].T,
               "w_down": v_w_down[0]}
    transposed = ("w_in", "w_gate", "w_up")

    decay_shard = jnp.stack([w_decay_f[0], w_decay_b[0]])
    (hb,), ((g_in, g_decay),) = _norm1(xt, norm1_g, tile(TOKEN_TILE["norm1"]),
                                       [_gather_comm([shard["w_in"], decay_shard], [True, False])])
    w_in_t = g_in.reshape(PROJ_W, D_MODEL)
    wd_pad_f, wd_pad_b = _padded_decay_weights(_unshard_cols(g_decay[:, 0]), _unshard_cols(g_decay[:, 1]))
    (p,), ((g_gate, g_out),) = _in_proj(
        hb, w_in_t, tile(TOKEN_TILE["in_proj"]), [_gather_comm([shard["w_gate"], shard["w_out"]], [True, True])])
    (o_f, st_f, o_b, st_b), ((g_up,),) = _gla_fwd(
        p, wd_pad_f, b_decay_f, wd_pad_b, b_decay_b, tile(TOKEN_TILE["gla"]), [_gather_comm([shard["w_up"]], [True])])
    w_out_full = g_out.reshape(D_MODEL, D_MODEL)
    (x1, ycat), ((g_down,),) = _mix_fwd(xt, o_f, o_b, p, gla_norm_g, ln_g, ln_b, w_sp, b_sp_col, w_out_full,
                                        tile(TOKEN_TILE["mix_fwd"]), [_gather_comm([shard["w_down"]], [True])])

    dx1, h2b, dgate, dup, act, dx2, loss_acc, d_gf, d_g2 = _ffn(
        x1, target, norm2_g, final_norm_g[None, :], g_gate.reshape(D_FF, D_MODEL), g_up.reshape(D_FF, D_MODEL),
        g_down.reshape(D_FF, D_MODEL), tile(TOKEN_TILE["ffn"]))
    dw_gate, _ = _matmul_tn(dgate, h2b, D_FF // 2, tile(TOKEN_TILE["dw"]), "grad_w_gate")
    dw_up, _ = _matmul_tn(dup, h2b, D_FF // 2, tile(TOKEN_TILE["dw"]), "grad_w_up")
    dw_down, _ = _matmul_tn(act, dx2, D_FF // 2, tile(TOKEN_TILE["dw"]), "grad_w_down")

    reduced = {}
    staged = {"w_gate": _row_blocks(dw_gate), "w_up": _row_blocks(dw_up)}
    (d_o, dpg, dpu, dpv, dw_out, d_gg, d_lg, d_lb, dw_sp, db_sp), (reduced["w_down"], staged_sib) = _mix_bwd(
        dx1, ycat, o_f, o_b, p, gla_norm_g, ln_g, ln_b, w_sp, b_sp_col, w_out_full,
        tile(TOKEN_TILE["mix_bwd"]),
        [_reduce_scatter_comm(_row_blocks(dw_down)), _sibling_exchange_comm(list(staged.values()))])
    staged_sums = [_chip_sum(my_pos, g, s, g.shape[2], "chip_sum_" + n)
                   for (n, g), s in zip(staged.items(), staged_sib)]
    (dq_f, dk_f, dv_f, dlr_f, dwd_f, dbd_f, dq_b, dk_b, dv_b, dlr_b, dwd_b, dbd_b), (reduced["w_out"], staged_recv) = (
        _gla_bwd(p, wd_pad_f, b_decay_f, wd_pad_b, b_decay_b, st_f, st_b, d_o, tile(TOKEN_TILE["gla"]),
                 [_reduce_scatter_comm(_row_blocks(dw_out)), _chips_exchange_comm([s[1] for s in staged_sums])]))
    for n, s, rc in zip(staged, staged_sums, staged_recv):
        reduced[n] = (s[0], rc)
    (grad_x, dp, d_g1), _ = _in_proj_bwd(
        xt, norm1_g, dx1, dq_f, dq_b, dk_f, dk_b, dv_f, dv_b, dpg, dpu, dpv, dlr_f, dlr_b, w_in_t,
        tile(TOKEN_TILE["in_proj_bwd"]))

    stacks = [_stack_rows([d_g1, d_g2, d_gf]), _stack_rows([d_gg, d_lg, d_lb]),
              _stack_rows([dbd_f, dbd_b, jnp.zeros((DECAY_W_ROW - 2, KEY_W), F32), dwd_f[:LOWRANK],
                           dwd_b[LOWRANK:2 * LOWRANK]]),
              _stack_rows([dw_sp.reshape(GMLP_W, GMLP_CHUNK), db_sp[:, :, 0], loss_acc[:1]])]
    dw_main, (small_sums,) = _matmul_tn(dp, hb, PROJ_PAD // 3, tile(TOKEN_TILE["dw"]), "grad_w_in",
                                        [_all_reduce_small_comm(stacks)])
    ((in_own, in_recv),) = _comm_only(
        [_reduce_scatter_comm(dw_main, W_IN_WINDOW, _w_in_block_pieces)], "grad_w_in_reduce_scatter")

    big_out = {"w_in": [r.T for r in _adamw_window(in_own, in_recv, shard["w_in"], shard_m["w_in"], shard_v["w_in"],
                                                   "adamw_w_in")]}
    for n, (own_sum, recv) in reduced.items():
        rows = shard[n].shape[0]
        half = rows // 2 if rows % 32 == 0 else rows
        res = _adamw_shard(own_sum, recv, shard[n], shard_m[n], shard_v[n], half, "adamw_" + n)
        big_out[n] = [r.T if n in transposed else r for r in res]

    s1024, s512, s256, s128 = small_sums
    loss = s128[GMLP_W + GMLP_GROUPS, 0]
    col0 = my_id * (KEY_W // N_DEV)
    decay_cols = lambda row0: lax.dynamic_slice(s256, (row0, col0), (LOWRANK, KEY_W // N_DEV))
    flat = lambda a: a.reshape(-1, a.shape[-1])
    small = {
        "norm1_g": ((s1024, 0, 1), norm1_g, m_norm1_g, v_norm1_g),
        "w_decay_f": ((decay_cols(DECAY_W_ROW), 0, LOWRANK), w_decay_f, m_w_decay_f, v_w_decay_f),
        "b_decay_f": ((s256, 0, 1), b_decay_f, m_b_decay_f, v_b_decay_f),
        "w_decay_b": ((decay_cols(DECAY_W_ROW + LOWRANK), 0, LOWRANK), w_decay_b, m_w_decay_b, v_w_decay_b),
        "b_decay_b": ((s256, 1, 1), b_decay_b, m_b_decay_b, v_b_decay_b),
        "gla_norm_g": ((s512, 0, 1), gla_norm_g, m_gla_norm_g, v_gla_norm_g),
        "gmlp_ln_g": ((s512, 1, 1), gmlp_ln_g, m_gmlp_ln_g, v_gmlp_ln_g),
        "gmlp_ln_b": ((s512, 2, 1), gmlp_ln_b, m_gmlp_ln_b, v_gmlp_ln_b),
        "w_spatial": ((s128, 0, GMLP_W), w_spatial, m_w_spatial, v_w_spatial),
        "b_spatial": ((s128, GMLP_W, GMLP_GROUPS), b_spatial, m_b_spatial, v_b_spatial),
        "norm2_g": ((s1024, 1, 1), norm2_g, m_norm2_g, v_norm2_g),
        "final_norm_g": ((s1024, 2, 1), final_norm_g, m_final_norm_g, v_final_norm_g),
    }
    small_res = _adamw_small([(g, flat(w), flat(m), flat(v)) for g, w, m, v in small.values()])
    small_out = {n: [r.reshape(small[n][1].shape) for r in res] for n, res in zip(small, small_res)}

    order = ["norm1_g", "w_in", "w_decay_f", "b_decay_f", "w_decay_b", "b_decay_b", "gla_norm_g", "gmlp_ln_g",
             "gmlp_ln_b", "w_spatial", "b_spatial", "w_out", "norm2_g", "w_gate", "w_up", "w_down", "final_norm_g"]
    outs = []
    for kind in range(4):
        for n in order:
            outs.append(big_out[n][kind][None] if n in big_out else small_out[n][kind])
    return (loss, grad_x[None], *outs)
```
